```python
import math
import jax, jax.numpy as jnp
from jax import lax
import numpy as np

D_MODEL = 1024
BATCH = 8
SEQ = 4096
DEPTH = 1

D_PLE = 256
D_MIX = D_MODEL
ATTN_HEADS = 8
HEAD_DIM = 64
D_ATTN = ATTN_HEADS * HEAD_DIM
POOL_WINDOWS = (2, 4, 8, 16)
POOL_GROUPS = len(POOL_WINDOWS)
D_POOL = D_MIX - D_ATTN
POOL_CH = D_POOL // POOL_GROUPS
D_IN = 3 * D_ATTN + ATTN_HEADS + D_POOL
D_FF = int(math.ceil(8 * D_MODEL / 3 / 256) * 256)
Q_BLOCK = 128
RMS_EPS = 1e-6

kernel_name = "hymba_fox_poolformer_block"


def rms_norm(x, g):
    xf = x.astype(jnp.float32)
    y = xf * lax.rsqrt(jnp.mean(xf * xf, axis=-1, keepdims=True) + RMS_EPS)
    return (y * g.astype(jnp.float32)).astype(x.dtype)


def forgetting_attention(q, k, v, log_f):
    b, s, h, dh = q.shape
    scale = 1.0 / math.sqrt(dh)
    c = jnp.cumsum(log_f, axis=1).transpose(0, 2, 1)
    nb = s // Q_BLOCK
    qb = q.reshape(b, nb, Q_BLOCK, h, dh).transpose(1, 0, 2, 3, 4)
    cb = c.reshape(b, h, nb, Q_BLOCK).transpose(2, 0, 1, 3)
    starts = jnp.arange(nb, dtype=jnp.int32) * Q_BLOCK
    kpos = jnp.arange(s, dtype=jnp.int32)

    def one_block(args):
        q_i, c_i, s0 = args
        scores = jnp.einsum('bqhd,bkhd->bhqk', q_i, k).astype(jnp.float32) * scale
        scores = scores + c_i[:, :, :, None] - c[:, :, None, :]
        qpos = s0 + jnp.arange(Q_BLOCK, dtype=jnp.int32)
        causal = kpos[None, :] <= qpos[:, None]
        scores = jnp.where(causal, scores, -jnp.inf)
        probs = jax.nn.softmax(scores, axis=-1).astype(v.dtype)
        return jnp.einsum('bhqk,bkhd->bqhd', probs, v)

    out = lax.map(one_block, (qb, cb, starts))
    return out.transpose(1, 0, 2, 3, 4).reshape(b, s, h * dh)


def multiscale_pool(u, w_pool, pool_scale):
    b, s, _ = u.shape
    uf = u.astype(jnp.float32)
    cs = jnp.cumsum(uf, axis=1)
    pos = jnp.arange(s, dtype=jnp.int32)
    outs = []
    for g, w in enumerate(POOL_WINDOWS):
        lo, hi = g * POOL_CH, (g + 1) * POOL_CH
        cs_g = cs[:, :, lo:hi]
        cs_shift = jnp.pad(cs_g, ((0, 0), (w, 0), (0, 0)))[:, :s]
        count = jnp.minimum(pos + 1, w).astype(jnp.float32)[None, :, None]
        outs.append((cs_g - cs_shift) / count - uf[:, :, lo:hi])
    y = jnp.stack(outs, axis=2).astype(u.dtype)
    y = jnp.einsum('bsgc,gcd->bsgd', y, w_pool).reshape(b, s, D_POOL)
    return y * pool_scale


def _fwd_setup_inputs(seed: int = 0) -> dict:
    key = jax.random.key(seed)
    ks = jax.random.split(key, 24)
    f32 = jnp.float32

    def nrm(k, shape, fan_in):
        return jax.random.normal(k, shape, f32) * (fan_in ** -0.5)

    def gain(k, shape):
        return 1.0 + 0.05 * jax.random.normal(k, shape, f32)

    return {
        "x": jax.random.normal(ks[0], (BATCH, SEQ, D_MODEL), f32),
        "p": jax.random.normal(ks[1], (DEPTH, BATCH, SEQ, D_PLE), f32),
        "g_mix_pre": gain(ks[2], (DEPTH, D_MODEL)),
        "w_in": nrm(ks[3], (DEPTH, D_MODEL, D_IN), D_MODEL),
        "b_forget": jax.random.uniform(ks[4], (DEPTH, ATTN_HEADS), f32, 1.0, 4.0),
        "g_attn_grp": gain(ks[5], (DEPTH, D_ATTN)),
        "g_pool_grp": gain(ks[6], (DEPTH, D_POOL)),
        "w_pool": nrm(ks[7], (DEPTH, POOL_GROUPS, POOL_CH, POOL_CH), POOL_CH),
        "pool_scale": 1.0 + 0.1 * jax.random.normal(ks[8], (DEPTH, D_POOL), f32),
        "w_out": nrm(ks[9], (DEPTH, D_MIX, D_MODEL), D_MIX),
        "g_mix_post": gain(ks[10], (DEPTH, D_MODEL)),
        "g_ffn_pre": gain(ks[11], (DEPTH, D_MODEL)),
        "w_ffn_gate": nrm(ks[12], (DEPTH, D_MODEL, D_FF), D_MODEL),
        "w_ffn_up": nrm(ks[13], (DEPTH, D_MODEL, D_FF), D_MODEL),
        "w_ffn_down": nrm(ks[14], (DEPTH, D_FF, D_MODEL), D_FF),
        "g_ffn_post": gain(ks[15], (DEPTH, D_MODEL)),
        "w_ple_proj": nrm(ks[16], (DEPTH, D_PLE, D_MODEL), D_PLE),
        "g_ple": gain(ks[17], (DEPTH, D_MODEL)),
        "w_ple_gate": nrm(ks[18], (DEPTH, D_MODEL, D_MODEL), D_MODEL),
    }


def _fwd_reference(x, p, g_mix_pre, w_in, b_forget, g_attn_grp, g_pool_grp, w_pool, pool_scale,
              w_out, g_mix_post, g_ffn_pre, w_ffn_gate, w_ffn_up, w_ffn_down, g_ffn_post,
              w_ple_proj, g_ple, w_ple_gate):
    b, s, _ = x.shape
    h = x
    for i in range(DEPTH):
        hn = rms_norm(h, g_mix_pre[i])
        z = hn @ w_in[i]
        o = 0
        q = z[..., o:o + D_ATTN].reshape(b, s, ATTN_HEADS, HEAD_DIM); o += D_ATTN
        k = z[..., o:o + D_ATTN].reshape(b, s, ATTN_HEADS, HEAD_DIM); o += D_ATTN
        v = z[..., o:o + D_ATTN].reshape(b, s, ATTN_HEADS, HEAD_DIM); o += D_ATTN
        f_logit = z[..., o:o + ATTN_HEADS]; o += ATTN_HEADS
        u = z[..., o:o + D_POOL]
        log_f = jax.nn.log_sigmoid(f_logit.astype(jnp.float32) + b_forget[i].astype(jnp.float32))
        a = forgetting_attention(q, k, v, log_f)
        m = multiscale_pool(u, w_pool[i], pool_scale[i])
        mix = jnp.concatenate([rms_norm(a, g_attn_grp[i]), rms_norm(m, g_pool_grp[i])], axis=-1)
        h = h + rms_norm(mix @ w_out[i], g_mix_post[i])
        hn = rms_norm(h, g_ffn_pre[i])
        ff = (jax.nn.silu(hn @ w_ffn_gate[i]) * (hn @ w_ffn_up[i])) @ w_ffn_down[i]
        h = h + rms_norm(ff, g_ffn_post[i])
        e = rms_norm(p[i] @ w_ple_proj[i], g_ple[i])
        h = h + jax.nn.sigmoid(h @ w_ple_gate[i]) * e
    return h


import jax as _jax
import jax.numpy as _jnp

TWIN_FORMAT = 'train_step'
FWD_PARAMS = ['x', 'p', 'g_mix_pre', 'w_in', 'b_forget', 'g_attn_grp', 'g_pool_grp', 'w_pool', 'pool_scale', 'w_out', 'g_mix_post', 'g_ffn_pre', 'w_ffn_gate', 'w_ffn_up', 'w_ffn_down', 'g_ffn_post', 'w_ple_proj', 'g_ple', 'w_ple_gate']
TWIN_WEIGHTS = ['g_mix_pre', 'w_in', 'b_forget', 'g_attn_grp', 'g_pool_grp', 'w_pool', 'pool_scale', 'w_out', 'g_mix_post', 'g_ffn_pre', 'w_ffn_gate', 'w_ffn_up', 'w_ffn_down', 'g_ffn_post', 'w_ple_proj', 'g_ple', 'w_ple_gate']
TWIN_DIFF_INPUT = 'x'
TWIN_INPUTS = ['x', 'p', 'g_mix_pre', 'w_in', 'b_forget', 'g_attn_grp', 'g_pool_grp', 'w_pool', 'pool_scale', 'w_out', 'g_mix_post', 'g_ffn_pre', 'w_ffn_gate', 'w_ffn_up', 'w_ffn_down', 'g_ffn_post', 'w_ple_proj', 'g_ple', 'w_ple_gate', 'loss_target', 'm_g_mix_pre', 'm_w_in', 'm_b_forget', 'm_g_attn_grp', 'm_g_pool_grp', 'm_w_pool', 'm_pool_scale', 'm_w_out', 'm_g_mix_post', 'm_g_ffn_pre', 'm_w_ffn_gate', 'm_w_ffn_up', 'm_w_ffn_down', 'm_g_ffn_post', 'm_w_ple_proj', 'm_g_ple', 'm_w_ple_gate', 'v_g_mix_pre', 'v_w_in', 'v_b_forget', 'v_g_attn_grp', 'v_g_pool_grp', 'v_w_pool', 'v_pool_scale', 'v_w_out', 'v_g_mix_post', 'v_g_ffn_pre', 'v_w_ffn_gate', 'v_w_ffn_up', 'v_w_ffn_down', 'v_g_ffn_post', 'v_w_ple_proj', 'v_g_ple', 'v_w_ple_gate']
TWIN_OUTPUTS = ['loss', 'grad_x', 'grad_g_mix_pre', 'grad_w_in', 'grad_b_forget', 'grad_g_attn_grp', 'grad_g_pool_grp', 'grad_w_pool', 'grad_pool_scale', 'grad_w_out', 'grad_g_mix_post', 'grad_g_ffn_pre', 'grad_w_ffn_gate', 'grad_w_ffn_up', 'grad_w_ffn_down', 'grad_g_ffn_post', 'grad_w_ple_proj', 'grad_g_ple', 'grad_w_ple_gate', 'delta_g_mix_pre', 'delta_w_in', 'delta_b_forget', 'delta_g_attn_grp', 'delta_g_pool_grp', 'delta_w_pool', 'delta_pool_scale', 'delta_w_out', 'delta_g_mix_post', 'delta_g_ffn_pre', 'delta_w_ffn_gate', 'delta_w_ffn_up', 'delta_w_ffn_down', 'delta_g_ffn_post', 'delta_w_ple_proj', 'delta_g_ple', 'delta_w_ple_gate', 'new_m_g_mix_pre', 'new_m_w_in', 'new_m_b_forget', 'new_m_g_attn_grp', 'new_m_g_pool_grp', 'new_m_w_pool', 'new_m_pool_scale', 'new_m_w_out', 'new_m_g_mix_post', 'new_m_g_ffn_pre', 'new_m_w_ffn_gate', 'new_m_w_ffn_up', 'new_m_w_ffn_down', 'new_m_g_ffn_post', 'new_m_w_ple_proj', 'new_m_g_ple', 'new_m_w_ple_gate', 'new_v_g_mix_pre', 'new_v_w_in', 'new_v_b_forget', 'new_v_g_attn_grp', 'new_v_g_pool_grp', 'new_v_w_pool', 'new_v_pool_scale', 'new_v_w_out', 'new_v_g_mix_post', 'new_v_g_ffn_pre', 'new_v_w_ffn_gate', 'new_v_w_ffn_up', 'new_v_w_ffn_down', 'new_v_g_ffn_post', 'new_v_w_ple_proj', 'new_v_g_ple', 'new_v_w_ple_gate']
TWIN_LEAF_KINDS = {'loss': 'loss', 'grad_x': 'grad_x', 'grad_g_mix_pre': 'grad_w', 'grad_w_in': 'grad_w', 'grad_b_forget': 'grad_w', 'grad_g_attn_grp': 'grad_w', 'grad_g_pool_grp': 'grad_w', 'grad_w_pool': 'grad_w', 'grad_pool_scale': 'grad_w', 'grad_w_out': 'grad_w', 'grad_g_mix_post': 'grad_w', 'grad_g_ffn_pre': 'grad_w', 'grad_w_ffn_gate': 'grad_w', 'grad_w_ffn_up': 'grad_w', 'grad_w_ffn_down': 'grad_w', 'grad_g_ffn_post': 'grad_w', 'grad_w_ple_proj': 'grad_w', 'grad_g_ple': 'grad_w', 'grad_w_ple_gate': 'grad_w', 'delta_g_mix_pre': 'delta_w', 'delta_w_in': 'delta_w', 'delta_b_forget': 'delta_w', 'delta_g_attn_grp': 'delta_w', 'delta_g_pool_grp': 'delta_w', 'delta_w_pool': 'delta_w', 'delta_pool_scale': 'delta_w', 'delta_w_out': 'delta_w', 'delta_g_mix_post': 'delta_w', 'delta_g_ffn_pre': 'delta_w', 'delta_w_ffn_gate': 'delta_w', 'delta_w_ffn_up': 'delta_w', 'delta_w_ffn_down': 'delta_w', 'delta_g_ffn_post': 'delta_w', 'delta_w_ple_proj': 'delta_w', 'delta_g_ple': 'delta_w', 'delta_w_ple_gate': 'delta_w', 'new_m_g_mix_pre': 'new_m', 'new_m_w_in': 'new_m', 'new_m_b_forget': 'new_m', 'new_m_g_attn_grp': 'new_m', 'new_m_g_pool_grp': 'new_m', 'new_m_w_pool': 'new_m', 'new_m_pool_scale': 'new_m', 'new_m_w_out': 'new_m', 'new_m_g_mix_post': 'new_m', 'new_m_g_ffn_pre': 'new_m', 'new_m_w_ffn_gate': 'new_m', 'new_m_w_ffn_up': 'new_m', 'new_m_w_ffn_down': 'new_m', 'new_m_g_ffn_post': 'new_m', 'new_m_w_ple_proj': 'new_m', 'new_m_g_ple': 'new_m', 'new_m_w_ple_gate': 'new_m', 'new_v_g_mix_pre': 'new_v', 'new_v_w_in': 'new_v', 'new_v_b_forget': 'new_v', 'new_v_g_attn_grp': 'new_v', 'new_v_g_pool_grp': 'new_v', 'new_v_w_pool': 'new_v', 'new_v_pool_scale': 'new_v', 'new_v_w_out': 'new_v', 'new_v_g_mix_post': 'new_v', 'new_v_g_ffn_pre': 'new_v', 'new_v_w_ffn_gate': 'new_v', 'new_v_w_ffn_up': 'new_v', 'new_v_w_ffn_down': 'new_v', 'new_v_g_ffn_post': 'new_v', 'new_v_w_ple_proj': 'new_v', 'new_v_g_ple': 'new_v', 'new_v_w_ple_gate': 'new_v'}


def _forward(args):
    return _fwd_reference(*[args[k] for k in FWD_PARAMS])


def _output_shape():
    out = _jax.eval_shape(lambda: _forward(_fwd_setup_inputs(0)))
    return out.shape, out.dtype

N_MICROBATCH = 1
ADAM_LR = 0.001
ADAM_B1 = 0.9
ADAM_B2 = 0.999
ADAM_EPS = 1e-08
ADAM_WD = 0.01
ADAM_STEP = 10
PER_EXAMPLE_BATCH_AXIS = {'x': 0, 'p': 1, 'loss_target': 0}
SHARED_INPUTS = []
_WEIGHT_DTYPES = {'g_mix_pre': _jnp.float32, 'w_in': _jnp.float32, 'b_forget': _jnp.float32, 'g_attn_grp': _jnp.float32, 'g_pool_grp': _jnp.float32, 'w_pool': _jnp.float32, 'pool_scale': _jnp.float32, 'w_out': _jnp.float32, 'g_mix_post': _jnp.float32, 'g_ffn_pre': _jnp.float32, 'w_ffn_gate': _jnp.float32, 'w_ffn_up': _jnp.float32, 'w_ffn_down': _jnp.float32, 'g_ffn_post': _jnp.float32, 'w_ple_proj': _jnp.float32, 'g_ple': _jnp.float32, 'w_ple_gate': _jnp.float32}
MOMENT_SCALE = {'g_mix_pre': 7.815371e-01, 'w_in': 5.235322e-01, 'b_forget': 2.813783e+00, 'g_attn_grp': 6.268444e-01, 'g_pool_grp': 8.311727e-01, 'w_pool': 7.895398e-01, 'pool_scale': 8.287884e-01, 'w_out': 7.137836e-01, 'g_mix_post': 3.233373e+01, 'g_ffn_pre': 6.637610e-01, 'w_ffn_gate': 2.125261e-01, 'w_ffn_up': 3.069143e-01, 'w_ffn_down': 5.089246e-01, 'g_ffn_post': 3.239291e+01, 'w_ple_proj': 2.148743e-01, 'g_ple': 1.072778e+01, 'w_ple_gate': 1.363386e-01}


def _to_microbatches(a, axis):
    t = _jnp.moveaxis(a, axis, 0)
    t = t.reshape((N_MICROBATCH, t.shape[0] // N_MICROBATCH) + t.shape[1:])
    return _jnp.moveaxis(t, 1, axis + 1)


def setup_inputs(seed: int = 0) -> dict:
    inp = _fwd_setup_inputs(seed)
    key = _jax.random.fold_in(_jax.random.key(seed), 7919)
    shape, _ = _output_shape()
    out = dict(inp)
    out["loss_target"] = _jax.random.normal(_jax.random.fold_in(key, 0), shape, _jnp.float32)
    for i, name in enumerate(TWIN_WEIGHTS):
        w = inp[name].astype(_jnp.float32)
        if MOMENT_SCALE is None:
            s = _jnp.sqrt(_jnp.mean(_jnp.square(w)) + 1e-30)
        else:
            s = MOMENT_SCALE[name]
        km, kv = _jax.random.split(_jax.random.fold_in(key, i + 1))
        out[name] = w
        out["m_" + name] = s * _jax.random.normal(km, w.shape, _jnp.float32)
        out["v_" + name] = (s * s) * _jax.random.uniform(kv, w.shape, _jnp.float32, 0.5, 1.5)
    if N_MICROBATCH > 1:
        for name, axis in PER_EXAMPLE_BATCH_AXIS.items():
            out[name] = _to_microbatches(out[name], axis)
    return {'x': out['x'], 'p': out['p'], 'g_mix_pre': out['g_mix_pre'], 'w_in': out['w_in'], 'b_forget': out['b_forget'], 'g_attn_grp': out['g_attn_grp'], 'g_pool_grp': out['g_pool_grp'], 'w_pool': out['w_pool'], 'pool_scale': out['pool_scale'], 'w_out': out['w_out'], 'g_mix_post': out['g_mix_post'], 'g_ffn_pre': out['g_ffn_pre'], 'w_ffn_gate': out['w_ffn_gate'], 'w_ffn_up': out['w_ffn_up'], 'w_ffn_down': out['w_ffn_down'], 'g_ffn_post': out['g_ffn_post'], 'w_ple_proj': out['w_ple_proj'], 'g_ple': out['g_ple'], 'w_ple_gate': out['w_ple_gate'], 'loss_target': out['loss_target'], 'm_g_mix_pre': out['m_g_mix_pre'], 'm_w_in': out['m_w_in'], 'm_b_forget': out['m_b_forget'], 'm_g_attn_grp': out['m_g_attn_grp'], 'm_g_pool_grp': out['m_g_pool_grp'], 'm_w_pool': out['m_w_pool'], 'm_pool_scale': out['m_pool_scale'], 'm_w_out': out['m_w_out'], 'm_g_mix_post': out['m_g_mix_post'], 'm_g_ffn_pre': out['m_g_ffn_pre'], 'm_w_ffn_gate': out['m_w_ffn_gate'], 'm_w_ffn_up': out['m_w_ffn_up'], 'm_w_ffn_down': out['m_w_ffn_down'], 'm_g_ffn_post': out['m_g_ffn_post'], 'm_w_ple_proj': out['m_w_ple_proj'], 'm_g_ple': out['m_g_ple'], 'm_w_ple_gate': out['m_w_ple_gate'], 'v_g_mix_pre': out['v_g_mix_pre'], 'v_w_in': out['v_w_in'], 'v_b_forget': out['v_b_forget'], 'v_g_attn_grp': out['v_g_attn_grp'], 'v_g_pool_grp': out['v_g_pool_grp'], 'v_w_pool': out['v_w_pool'], 'v_pool_scale': out['v_pool_scale'], 'v_w_out': out['v_w_out'], 'v_g_mix_post': out['v_g_mix_post'], 'v_g_ffn_pre': out['v_g_ffn_pre'], 'v_w_ffn_gate': out['v_w_ffn_gate'], 'v_w_ffn_up': out['v_w_ffn_up'], 'v_w_ffn_down': out['v_w_ffn_down'], 'v_g_ffn_post': out['v_g_ffn_post'], 'v_w_ple_proj': out['v_w_ple_proj'], 'v_g_ple': out['v_g_ple'], 'v_w_ple_gate': out['v_w_ple_gate']}


def _loss(weights, diff, rest, loss_target):
    with _jax.named_scope("forward"):
        args = {**rest, TWIN_DIFF_INPUT: diff, **{k: w.astype(_WEIGHT_DTYPES[k]) for k, w in weights.items()}}
        y = _forward(args)
    with _jax.named_scope("loss_head"):
        err = _jnp.square(y.astype(_jnp.float32) - loss_target)
        return 0.5 * _jnp.sum(_jnp.mean(err, axis=-1)) if err.ndim else 0.5 * err


def _adamw(w, g, m, v):
    m = ADAM_B1 * m + (1.0 - ADAM_B1) * g
    v = ADAM_B2 * v + (1.0 - ADAM_B2) * _jnp.square(g)
    m_hat = m / (1.0 - ADAM_B1 ** ADAM_STEP)
    v_hat = v / (1.0 - ADAM_B2 ** ADAM_STEP)
    delta = -ADAM_LR * (m_hat / (_jnp.sqrt(v_hat) + ADAM_EPS) + ADAM_WD * w)
    return delta, m, v


def reference(x, p, g_mix_pre, w_in, b_forget, g_attn_grp, g_pool_grp, w_pool, pool_scale, w_out, g_mix_post, g_ffn_pre, w_ffn_gate, w_ffn_up, w_ffn_down, g_ffn_post, w_ple_proj, g_ple, w_ple_gate, loss_target, m_g_mix_pre, m_w_in, m_b_forget, m_g_attn_grp, m_g_pool_grp, m_w_pool, m_pool_scale, m_w_out, m_g_mix_post, m_g_ffn_pre, m_w_ffn_gate, m_w_ffn_up, m_w_ffn_down, m_g_ffn_post, m_w_ple_proj, m_g_ple, m_w_ple_gate, v_g_mix_pre, v_w_in, v_b_forget, v_g_attn_grp, v_g_pool_grp, v_w_pool, v_pool_scale, v_w_out, v_g_mix_post, v_g_ffn_pre, v_w_ffn_gate, v_w_ffn_up, v_w_ffn_down, v_g_ffn_post, v_w_ple_proj, v_g_ple, v_w_ple_gate):
    given = dict(x=x, p=p, g_mix_pre=g_mix_pre, w_in=w_in, b_forget=b_forget, g_attn_grp=g_attn_grp, g_pool_grp=g_pool_grp, w_pool=w_pool, pool_scale=pool_scale, w_out=w_out, g_mix_post=g_mix_post, g_ffn_pre=g_ffn_pre, w_ffn_gate=w_ffn_gate, w_ffn_up=w_ffn_up, w_ffn_down=w_ffn_down, g_ffn_post=g_ffn_post, w_ple_proj=w_ple_proj, g_ple=g_ple, w_ple_gate=w_ple_gate, loss_target=loss_target, m_g_mix_pre=m_g_mix_pre, m_w_in=m_w_in, m_b_forget=m_b_forget, m_g_attn_grp=m_g_attn_grp, m_g_pool_grp=m_g_pool_grp, m_w_pool=m_w_pool, m_pool_scale=m_pool_scale, m_w_out=m_w_out, m_g_mix_post=m_g_mix_post, m_g_ffn_pre=m_g_ffn_pre, m_w_ffn_gate=m_w_ffn_gate, m_w_ffn_up=m_w_ffn_up, m_w_ffn_down=m_w_ffn_down, m_g_ffn_post=m_g_ffn_post, m_w_ple_proj=m_w_ple_proj, m_g_ple=m_g_ple, m_w_ple_gate=m_w_ple_gate, v_g_mix_pre=v_g_mix_pre, v_w_in=v_w_in, v_b_forget=v_b_forget, v_g_attn_grp=v_g_attn_grp, v_g_pool_grp=v_g_pool_grp, v_w_pool=v_w_pool, v_pool_scale=v_pool_scale, v_w_out=v_w_out, v_g_mix_post=v_g_mix_post, v_g_ffn_pre=v_g_ffn_pre, v_w_ffn_gate=v_w_ffn_gate, v_w_ffn_up=v_w_ffn_up, v_w_ffn_down=v_w_ffn_down, v_g_ffn_post=v_g_ffn_post, v_w_ple_proj=v_w_ple_proj, v_g_ple=v_g_ple, v_w_ple_gate=v_w_ple_gate)
    weights = {n: given[n] for n in TWIN_WEIGHTS}
    shared = {n: given[n] for n in SHARED_INPUTS}
    per_example = {n: given[n] for n in ['x', 'p']}
    grad_fn = _jax.value_and_grad(_loss, argnums=(0, 1))

    def one_microbatch(ex, loss_target):
        ex = dict(ex)
        diff = ex.pop(TWIN_DIFF_INPUT)
        return grad_fn(weights, diff, {**shared, **ex}, loss_target)

    if N_MICROBATCH == 1:
        loss, (grad_w, grad_x) = one_microbatch(per_example, given["loss_target"])
    else:
        def body(carry, xs):
            loss_sum, grad_sum = carry
            l_k, (gw_k, gx_k) = one_microbatch(xs[0], xs[1])
            with _jax.named_scope("update"):
                return (loss_sum + l_k, _jax.tree.map(_jnp.add, grad_sum, gw_k)), gx_k

        init = (_jnp.zeros((), _jnp.float32), _jax.tree.map(_jnp.zeros_like, weights))
        (loss, grad_w), grad_x = _jax.lax.scan(body, init, (per_example, given["loss_target"]))
    with _jax.named_scope("update"):
        delta_w, new_m, new_v = {}, {}, {}
        for n in TWIN_WEIGHTS:
            delta_w[n], new_m[n], new_v[n] = _adamw(weights[n], grad_w[n], given["m_" + n], given["v_" + n])
    return (loss, grad_x, *[grad_w[n] for n in TWIN_WEIGHTS], *[delta_w[n] for n in TWIN_WEIGHTS],
            *[new_m[n] for n in TWIN_WEIGHTS], *[new_v[n] for n in TWIN_WEIGHTS])
```

```python
import functools

import jax
import jax.numpy as jnp
from jax import lax
from jax.experimental import pallas as pl
from jax.experimental.pallas import tpu as pltpu

F32 = jnp.float32
BF16 = jnp.bfloat16
HIGHEST = lax.Precision.HIGHEST

D_MODEL = 1024
HEADS = 8
HEAD_DIM = 64
D_ATTN = HEADS * HEAD_DIM
POOL_WINDOWS = (2, 4, 8, 16)
POOL_CH = 128
D_POOL = POOL_CH * len(POOL_WINDOWS)
D_FF = 2816
D_PLE = 256
D_IN = 3 * D_ATTN + HEADS + D_POOL
RMS_EPS = 1e-6
N_DEV = 8

ADAM_LR = 0.001
ADAM_B1 = 0.9
ADAM_B2 = 0.999
ADAM_EPS = 1e-08
ADAM_WD = 0.01
ADAM_STEP = 10

LANES = 128
HALO = 16
TS = 512
TQ = 256
TN_FF = 256
NEG = -1e30
VMEM_LIMIT = 56 * 1024 * 1024

ROWS_IN = 272
OFF_OUT = ROWS_IN
OFF_GATE = OFF_OUT + 128
OFF_UP = OFF_GATE + 352
OFF_DOWN = OFF_UP + 352
OFF_PLE = OFF_DOWN + 352
OFF_PG = OFF_PLE + 32
ROWS_USED = OFF_PG + 128
ROWS_PACK = 1664
TR_PACK = 208

SMALL_ROWS = 80
ROW_G_MIX_PRE, ROW_G_MIX_POST, ROW_G_FFN_PRE, ROW_G_FFN_POST, ROW_G_PLE = 64, 65, 66, 67, 68
ROW_G_ATTN, ROW_G_POOL, ROW_POOL_SCALE, ROW_B_FORGET, ROW_LOSS = 69, 70, 71, 72, 73


def _nn(a, b):
    return jnp.dot(a, b, preferred_element_type=F32)


def _nt(a, b):
    return lax.dot_general(a, b, (((1,), (1,)), ((), ())), preferred_element_type=F32)


def _tn(a, b):
    return lax.dot_general(a, b, (((0,), (0,)), ((), ())), preferred_element_type=F32)


def _rstd(v):
    return lax.rsqrt(jnp.mean(v * v, axis=-1, keepdims=True) + RMS_EPS)


def _rms_bwd(v, g, dy):
    r = _rstd(v)
    vh = v * r
    t = dy * g
    dv = r * (t - vh * jnp.mean(t * vh, axis=-1, keepdims=True))
    return dv, jnp.sum(dy * vh, axis=0, keepdims=True)


def _params(n_grid):
    return pltpu.CompilerParams(dimension_semantics=("arbitrary",) * n_grid, vmem_limit_bytes=VMEM_LIMIT)


def _row(i):
    return (i, 0)


def _fixed(*_):
    return (0, 0)


VMEM_WHOLE = pl.BlockSpec(memory_space=pltpu.VMEM)
SMEM_WHOLE = pl.BlockSpec(memory_space=pltpu.SMEM)
ANY = pl.BlockSpec(memory_space=pl.ANY)


def _pre_attn_fwd(x, g1, wqkv, wf, wu, bpad, wpool):
    s, d = x.shape
    nt = s // TS
    sub = TS // TQ

    def body(x_ref, g_ref, wqkv_ref, wf_ref, wu_ref, b_ref, wp_ref,
             hn_ref, q_ref, k_ref, v_ref, fl_ref, c_ref, ct_ref, y_ref, mp_ref, ubuf, ccar):
        i = pl.program_id(0)

        @pl.when(i == 0)
        def _():
            ubuf[0:HALO, :] = jnp.zeros((HALO, D_POOL), F32)
            ccar[...] = jnp.zeros_like(ccar)

        xv = x_ref[...]
        hn = (xv * _rstd(xv) * g_ref[...]).astype(BF16)
        hn_ref[...] = hn
        zq = _nt(hn, wqkv_ref[...])
        q_ref[...] = (zq[:, 0:D_ATTN] * 0.125).astype(BF16)
        k_ref[...] = zq[:, D_ATTN:2 * D_ATTN].astype(BF16)
        v_ref[...] = zq[:, 2 * D_ATTN:3 * D_ATTN].astype(BF16)

        fl = _nt(hn, wf_ref[...]) + b_ref[...]
        fl_ref[...] = fl
        logf = jax.nn.log_sigmoid(fl)
        rr = lax.broadcasted_iota(jnp.int32, (TS, TS), 0)
        cc = lax.broadcasted_iota(jnp.int32, (TS, TS), 1)
        tril = (cc <= rr).astype(F32)
        c = jnp.dot(tril, logf, precision=HIGHEST, preferred_element_type=F32) + ccar[...]
        c_ref[...] = c
        ccar[...] = c_ref[TS - 1:TS, :]
        ct = c.T
        for a in range(sub):
            ct_ref[a] = ct[0:HEADS, a * TQ:(a + 1) * TQ]

        u = _nt(hn, wu_ref[...])
        ubuf[HALO:HALO + TS, :] = u
        t = i * TS + lax.broadcasted_iota(jnp.int32, (TS, 1), 0)
        for g, w in enumerate(POOL_WINDOWS):
            cols = slice(g * POOL_CH, (g + 1) * POOL_CH)
            sm = ubuf[:, cols]
            step = 1
            while step < w:
                sm = sm + pltpu.roll(sm, step, 0)
                step *= 2
            cnt = jnp.minimum(t + 1, w).astype(F32)
            yg = (sm[HALO:, :] / cnt - u[:, cols]).astype(BF16)
            y_ref[:, cols] = yg
            mp_ref[:, cols] = _nn(yg, wp_ref[g])
        ubuf[0:HALO, :] = u[TS - HALO:, :]

    outs = (
        jax.ShapeDtypeStruct((s, d), BF16),
        jax.ShapeDtypeStruct((s, D_ATTN), BF16), jax.ShapeDtypeStruct((s, D_ATTN), BF16),
        jax.ShapeDtypeStruct((s, D_ATTN), BF16),
        jax.ShapeDtypeStruct((s, LANES), F32), jax.ShapeDtypeStruct((s, LANES), F32),
        jax.ShapeDtypeStruct((s // TQ, HEADS, TQ), F32),
        jax.ShapeDtypeStruct((s, D_POOL), BF16), jax.ShapeDtypeStruct((s, D_POOL), F32),
    )
    return pl.pallas_call(
        body, grid=(nt,), out_shape=outs, name="pre_attn_fwd",
        in_specs=[pl.BlockSpec((TS, d), _row), pl.BlockSpec((1, d), _fixed),
                  pl.BlockSpec(wqkv.shape, _fixed), pl.BlockSpec(wf.shape, _fixed), pl.BlockSpec(wu.shape, _fixed),
                  pl.BlockSpec((1, LANES), _fixed), pl.BlockSpec(wpool.shape, lambda i: (0, 0, 0))],
        out_specs=(pl.BlockSpec((TS, d), _row),
                   pl.BlockSpec((TS, D_ATTN), _row), pl.BlockSpec((TS, D_ATTN), _row), pl.BlockSpec((TS, D_ATTN), _row),
                   pl.BlockSpec((TS, LANES), _row), pl.BlockSpec((TS, LANES), _row),
                   pl.BlockSpec((sub, HEADS, TQ), lambda i: (i, 0, 0)),
                   pl.BlockSpec((TS, D_POOL), _row), pl.BlockSpec((TS, D_POOL), _row)),
        scratch_shapes=[pltpu.VMEM((TS + HALO, D_POOL), F32), pltpu.VMEM((1, LANES), F32)],
        compiler_params=_params(1),
    )(x, g1, wqkv, wf, wu, bpad, wpool)


def _attn_fwd(q, k, v, c0, ct3):
    s = q.shape[0]
    nq = s // TQ

    def body(c0_ref, q_ref, k_ref, v_ref, ct_ref, a_ref, lset_ref):
        i = pl.program_id(0)
        rows = lax.broadcasted_iota(jnp.int32, (TQ, TQ), 0)
        cols = lax.broadcasted_iota(jnp.int32, (TQ, TQ), 1)
        causal = cols <= rows
        lane = lax.broadcasted_iota(jnp.int32, (TQ, LANES), 1)
        lse_all = jnp.zeros((TQ, LANES), F32)
        for h in range(HEADS):
            hs = slice(h * HEAD_DIM, (h + 1) * HEAD_DIM)
            qh = q_ref[:, hs]
            c0h = c0_ref[i, h]

            def logits(j, qh=qh, c0h=c0h, hs=hs, h=h):
                kj = k_ref[pl.ds(j * TQ, TQ), hs]
                return _nt(qh, kj) + (c0h - ct_ref[j, h:h + 1, :])

            sc = jnp.where(causal, logits(i), NEG)
            m = jnp.max(sc, axis=1, keepdims=True)
            p = jnp.exp(sc - m)
            l = jnp.sum(p, axis=1, keepdims=True)
            acc = _nn(p.astype(BF16), v_ref[pl.ds(i * TQ, TQ), hs])

            def step(j, carry, logits=logits, hs=hs):
                m, l, acc = carry
                sc = logits(j)
                mn = jnp.maximum(m, jnp.max(sc, axis=1, keepdims=True))
                al = jnp.exp(m - mn)
                p = jnp.exp(sc - mn)
                l = al * l + jnp.sum(p, axis=1, keepdims=True)
                acc = al * acc + _nn(p.astype(BF16), v_ref[pl.ds(j * TQ, TQ), hs])
                return mn, l, acc

            m, l, acc = lax.fori_loop(0, i, step, (m, l, acc))
            a_ref[:, hs] = acc / l
            lse_all = jnp.where(lane == h, m + jnp.log(l), lse_all)
        lset_ref[0] = lse_all.T[0:HEADS, :]

    return pl.pallas_call(
        body, grid=(nq,), name="attn_fwd",
        out_shape=(jax.ShapeDtypeStruct((s, D_ATTN), F32), jax.ShapeDtypeStruct((nq, HEADS, TQ), F32)),
        in_specs=[SMEM_WHOLE, pl.BlockSpec((TQ, D_ATTN), _row), VMEM_WHOLE, VMEM_WHOLE, VMEM_WHOLE],
        out_specs=(pl.BlockSpec((TQ, D_ATTN), _row), pl.BlockSpec((1, HEADS, TQ), lambda i: (i, 0, 0))),
        compiler_params=_params(1),
    )(c0, q, k, v, ct3)


def _post_attn_fwd(a, mpre, x, g_attn, g_pool, pscale, wout, g_post, g_ffn_pre):
    s, d = x.shape

    def body(a_ref, mp_ref, x_ref, ga_ref, gp_ref, ps_ref, wo_ref, gpost_ref, gpre_ref,
             mix_ref, o_ref, h1_ref, hn2_ref):
        av = a_ref[...]
        mix_ref[:, 0:D_ATTN] = (av * _rstd(av) * ga_ref[...]).astype(BF16)
        mv = mp_ref[...] * ps_ref[...]
        mix_ref[:, D_ATTN:] = (mv * _rstd(mv) * gp_ref[...]).astype(BF16)
        o = _nn(mix_ref[...], wo_ref[...])
        o_ref[...] = o
        h1 = x_ref[...] + o * _rstd(o) * gpost_ref[...]
        h1_ref[...] = h1
        hn2_ref[...] = (h1 * _rstd(h1) * gpre_ref[...]).astype(BF16)

    vec = lambda n: pl.BlockSpec((1, n), _fixed)
    return pl.pallas_call(
        body, grid=(s // TS,), name="post_attn_fwd",
        out_shape=(jax.ShapeDtypeStruct((s, d), BF16), jax.ShapeDtypeStruct((s, d), F32),
                   jax.ShapeDtypeStruct((s, d), F32), jax.ShapeDtypeStruct((s, d), BF16)),
        in_specs=[pl.BlockSpec((TS, D_ATTN), _row), pl.BlockSpec((TS, D_POOL), _row), pl.BlockSpec((TS, d), _row),
                  vec(D_ATTN), vec(D_POOL), vec(D_POOL), pl.BlockSpec(wout.shape, _fixed), vec(d), vec(d)],
        out_specs=(pl.BlockSpec((TS, d), _row),) * 4,
        compiler_params=_params(1),
    )(a, mpre, x, g_attn, g_pool, pscale, wout, g_post, g_ffn_pre)


def _ffn_fwd(hn2, wg, wu, wd, h1, g_post):
    s, d = h1.shape
    nc = D_FF // TN_FF

    def body(hn_ref, wg_ref, wu_ref, wd_ref, h1_ref, g_ref, gate_ref, up_ref, act_ref, ff_ref, h2_ref, acc):
        j = pl.program_id(1)

        @pl.when(j == 0)
        def _():
            acc[...] = jnp.zeros_like(acc)

        hn = hn_ref[...]
        gt = _nt(hn, wg_ref[...])
        up = _nt(hn, wu_ref[...])
        act = (gt * jax.nn.sigmoid(gt) * up).astype(BF16)
        gate_ref[...] = gt.astype(BF16)
        up_ref[...] = up.astype(BF16)
        act_ref[...] = act
        acc[...] += _nn(act, wd_ref[...])

        @pl.when(j == nc - 1)
        def _():
            ff = acc[...]
            ff_ref[...] = ff
            h2_ref[...] = h1_ref[...] + ff * _rstd(ff) * g_ref[...]

    rowblk = pl.BlockSpec((TS, d), lambda i, j: (i, 0))
    wblk = pl.BlockSpec((TN_FF, d), lambda i, j: (j, 0))
    chunk = pl.BlockSpec((TS, TN_FF), lambda i, j: (i, j))
    return pl.pallas_call(
        body, grid=(s // TS, nc), name="ffn_fwd",
        out_shape=(jax.ShapeDtypeStruct((s, D_FF), BF16),) * 3 + (jax.ShapeDtypeStruct((s, d), F32),) * 2,
        in_specs=[rowblk, wblk, wblk, wblk, rowblk, pl.BlockSpec((1, d), lambda i, j: (0, 0))],
        out_specs=(chunk, chunk, chunk, rowblk, rowblk),
        scratch_shapes=[pltpu.VMEM((TS, d), F32)],
        compiler_params=_params(2),
    )(hn2, wg, wu, wd, h1, g_post)


def _tail_fwd_bwd(h2, p, tgt, ff, wple, wpg, g_ple, g_ffn_post):
    s, d = h2.shape

    def body(h2_ref, p_ref, t_ref, ff_ref, wple_ref, wpg_ref, gple_ref, gfp_ref,
             dh2_ref, dff_ref, dgl_ref, dpp_ref, h2b_ref, pb_ref, loss_ref, dgple_ref, dgfp_ref):
        i = pl.program_id(0)

        @pl.when(i == 0)
        def _():
            loss_ref[...] = jnp.zeros_like(loss_ref)
            dgple_ref[...] = jnp.zeros_like(dgple_ref)
            dgfp_ref[...] = jnp.zeros_like(dgfp_ref)

        h2 = h2_ref[...]
        h2b = h2.astype(BF16)
        h2b_ref[...] = h2b
        pb = p_ref[...].astype(BF16)
        pb_ref[...] = pb
        pp = _nt(pb, wple_ref[...])
        gple = gple_ref[...]
        e = pp * _rstd(pp) * gple
        sg = jax.nn.sigmoid(_nn(h2b, wpg_ref[...]))
        diff = h2 + sg * e - t_ref[...]
        sq = jnp.sum(jnp.sum(diff * diff, axis=1, keepdims=True), axis=0, keepdims=True)
        loss_ref[...] += jnp.broadcast_to(sq * (0.5 / d), loss_ref.shape)
        dh3 = diff * (1.0 / d)
        dgl = (dh3 * e * sg * (1.0 - sg)).astype(BF16)
        dgl_ref[...] = dgl
        dh2 = dh3 + _nt(dgl, wpg_ref[...])
        dh2_ref[...] = dh2
        dpp, dg = _rms_bwd(pp, gple, dh3 * sg)
        dpp_ref[...] = dpp.astype(BF16)
        dgple_ref[...] += dg
        dff, dg = _rms_bwd(ff_ref[...], gfp_ref[...], dh2)
        dff_ref[...] = dff.astype(BF16)
        dgfp_ref[...] += dg

    rowblk = pl.BlockSpec((TS, d), _row)
    vec = pl.BlockSpec((1, d), _fixed)
    return pl.pallas_call(
        body, grid=(s // TS,), name="tail_fwd_bwd",
        out_shape=(jax.ShapeDtypeStruct((s, d), F32), jax.ShapeDtypeStruct((s, d), BF16),
                   jax.ShapeDtypeStruct((s, d), BF16), jax.ShapeDtypeStruct((s, d), BF16),
                   jax.ShapeDtypeStruct((s, d), BF16), jax.ShapeDtypeStruct((s, D_PLE), BF16),
                   jax.ShapeDtypeStruct((8, LANES), F32), jax.ShapeDtypeStruct((1, d), F32),
                   jax.ShapeDtypeStruct((1, d), F32)),
        in_specs=[rowblk, pl.BlockSpec((TS, D_PLE), _row), rowblk, rowblk,
                  pl.BlockSpec(wple.shape, _fixed), pl.BlockSpec(wpg.shape, _fixed), vec, vec],
        out_specs=(rowblk, rowblk, rowblk, rowblk, rowblk, pl.BlockSpec((TS, D_PLE), _row),
                   pl.BlockSpec((8, LANES), _fixed), vec, vec),
        compiler_params=_params(1),
    )(h2, p, tgt, ff, wple, wpg, g_ple, g_ffn_post)


def _ffn_bwd(dff, gate, up, wd, wg, wu, h1, dh2, g_pre):
    s, d = h1.shape
    nc = D_FF // TN_FF

    def body(dff_ref, gate_ref, up_ref, wd_ref, wg_ref, wu_ref, h1_ref, dh2_ref, g_ref,
             dgate_ref, dup_ref, dh1_ref, dg_ref, acc):
        i = pl.program_id(0)
        j = pl.program_id(1)

        @pl.when((i == 0) & (j == 0))
        def _():
            dg_ref[...] = jnp.zeros_like(dg_ref)

        @pl.when(j == 0)
        def _():
            acc[...] = jnp.zeros_like(acc)

        dact = _nt(dff_ref[...], wd_ref[...])
        gt = gate_ref[...].astype(F32)
        sg = jax.nn.sigmoid(gt)
        dup = (dact * gt * sg).astype(BF16)
        dgate = (dact * up_ref[...].astype(F32) * (sg * (1.0 + gt * (1.0 - sg)))).astype(BF16)
        dgate_ref[...] = dgate
        dup_ref[...] = dup
        acc[...] += _nn(dgate, wg_ref[...]) + _nn(dup, wu_ref[...])

        @pl.when(j == nc - 1)
        def _():
            dv, dg = _rms_bwd(h1_ref[...], g_ref[...], acc[...])
            dh1_ref[...] = dh2_ref[...] + dv
            dg_ref[...] += dg

    rowblk = pl.BlockSpec((TS, d), lambda i, j: (i, 0))
    wblk = pl.BlockSpec((TN_FF, d), lambda i, j: (j, 0))
    chunk = pl.BlockSpec((TS, TN_FF), lambda i, j: (i, j))
    vec = pl.BlockSpec((1, d), lambda i, j: (0, 0))
    return pl.pallas_call(
        body, grid=(s // TS, nc), name="ffn_bwd",
        out_shape=(jax.ShapeDtypeStruct((s, D_FF), BF16), jax.ShapeDtypeStruct((s, D_FF), BF16),
                   jax.ShapeDtypeStruct((s, d), F32), jax.ShapeDtypeStruct((1, d), F32)),
        in_specs=[rowblk, chunk, chunk, wblk, wblk, wblk, rowblk, rowblk, vec],
        out_specs=(chunk, chunk, rowblk, vec),
        scratch_shapes=[pltpu.VMEM((TS, d), F32)],
        compiler_params=_params(2),
    )(dff, gate, up, wd, wg, wu, h1, dh2, g_pre)


def _post_attn_bwd(dh1, o, a, mpre, wout, wpool, g_post, g_attn, g_pool, pscale):
    s, d = dh1.shape
    sub = TS // TQ

    def body(dh1_ref, o_ref, a_ref, mp_ref, wo_ref, wp_ref, gpost_ref, ga_ref, gp_ref, ps_ref,
             dob_ref, dab_ref, dlt_ref, dmpb_ref, dy_ref, dgpost_ref, dga_ref, dgp_ref, dps_ref):
        i = pl.program_id(0)

        @pl.when(i == 0)
        def _():
            dgpost_ref[...] = jnp.zeros_like(dgpost_ref)
            dga_ref[...] = jnp.zeros_like(dga_ref)
            dgp_ref[...] = jnp.zeros_like(dgp_ref)
            dps_ref[...] = jnp.zeros_like(dps_ref)

        do, dg = _rms_bwd(o_ref[...], gpost_ref[...], dh1_ref[...])
        dgpost_ref[...] += dg
        dob = do.astype(BF16)
        dob_ref[...] = dob
        dmix = _nt(dob, wo_ref[...])

        av = a_ref[...]
        da, dg = _rms_bwd(av, ga_ref[...], dmix[:, 0:D_ATTN])
        dga_ref[...] += dg
        dab_ref[...] = da.astype(BF16)
        hsel = (lax.shift_right_logical(lax.broadcasted_iota(jnp.int32, (HEADS, D_ATTN), 1), 6)
                == lax.broadcasted_iota(jnp.int32, (HEADS, D_ATTN), 0)).astype(F32)
        dlt = lax.dot_general(hsel, da * av, (((1,), (1,)), ((), ())), precision=HIGHEST, preferred_element_type=F32)
        for q in range(sub):
            dlt_ref[q] = dlt[:, q * TQ:(q + 1) * TQ]

        ps = ps_ref[...]
        mp = mp_ref[...]
        dm, dg = _rms_bwd(mp * ps, gp_ref[...], dmix[:, D_ATTN:])
        dgp_ref[...] += dg
        dps_ref[...] += jnp.sum(dm * mp, axis=0, keepdims=True)
        dmpb = (dm * ps).astype(BF16)
        dmpb_ref[...] = dmpb
        for g in range(len(POOL_WINDOWS)):
            cols = slice(g * POOL_CH, (g + 1) * POOL_CH)
            dy_ref[:, cols] = _nt(dmpb[:, cols], wp_ref[g])

    rowblk = pl.BlockSpec((TS, d), _row)
    half = pl.BlockSpec((TS, D_ATTN), _row)
    vec = lambda n: pl.BlockSpec((1, n), _fixed)
    return pl.pallas_call(
        body, grid=(s // TS,), name="post_attn_bwd",
        out_shape=(jax.ShapeDtypeStruct((s, d), BF16), jax.ShapeDtypeStruct((s, D_ATTN), BF16),
                   jax.ShapeDtypeStruct((s // TQ, HEADS, TQ), F32), jax.ShapeDtypeStruct((s, D_POOL), BF16),
                   jax.ShapeDtypeStruct((s, D_POOL), F32), jax.ShapeDtypeStruct((1, d), F32),
                   jax.ShapeDtypeStruct((1, D_ATTN), F32), jax.ShapeDtypeStruct((1, D_POOL), F32),
                   jax.ShapeDtypeStruct((1, D_POOL), F32)),
        in_specs=[rowblk, rowblk, half, half, pl.BlockSpec(wout.shape, _fixed),
                  pl.BlockSpec(wpool.shape, lambda i: (0, 0, 0)), vec(d), vec(D_ATTN), vec(D_POOL), vec(D_POOL)],
        out_specs=(rowblk, half, pl.BlockSpec((sub, HEADS, TQ), lambda i: (i, 0, 0)), half, half,
                   vec(d), vec(D_ATTN), vec(D_POOL), vec(D_POOL)),
        compiler_params=_params(1),
    )(dh1, o, a, mpre, wout, wpool, g_post, g_attn, g_pool, pscale)


def _attn_bwd(q, k, v, do, c, c0, lset3, dlt3):
    s = q.shape[0]
    nq = s // TQ

    def body(c0_ref, k_ref, v_ref, c_ref, q_ref, do_ref, lset_ref, dlt_ref, dq_ref, dk_ref, dv_ref, dcs_ref, drs_ref):
        j = pl.program_id(0)

        @pl.when(j == 0)
        def _():
            dq_ref[...] = jnp.zeros_like(dq_ref)
            drs_ref[...] = jnp.zeros_like(drs_ref)

        krow = lax.broadcasted_iota(jnp.int32, (TQ, TQ), 0)
        qcol = lax.broadcasted_iota(jnp.int32, (TQ, TQ), 1)
        valid = krow <= qcol
        lane = lax.broadcasted_iota(jnp.int32, (TQ, LANES), 1)
        c128 = c_ref[...]
        dcs_all = jnp.zeros((TQ, LANES), F32)
        for h in range(HEADS):
            hs = slice(h * HEAD_DIM, (h + 1) * HEAD_DIM)
            kj = k_ref[:, hs]
            vj = v_ref[:, hs]
            cj = jnp.sum(jnp.where(lane == h, c128, 0.0), axis=1, keepdims=True)

            def tile(i, carry, masked, kj=kj, vj=vj, cj=cj, hs=hs, h=h):
                dk, dv, dca = carry
                qi = q_ref[pl.ds(i * TQ, TQ), hs]
                doi = do_ref[pl.ds(i * TQ, TQ), hs]
                st = _nt(kj, qi) + (c0_ref[i, h] - cj) - lset_ref[i, h:h + 1, :]
                if masked:
                    st = jnp.where(valid, st, NEG)
                pt = jnp.exp(st)
                dv = dv + _nn(pt.astype(BF16), doi)
                dst = pt * (_nt(vj, doi) - dlt_ref[i, h:h + 1, :])
                dsb = dst.astype(BF16)
                dk = dk + _nn(dsb, qi)
                dq_ref[pl.ds(i * TQ, TQ), hs] += _tn(dsb, kj)
                drs_ref[i, h:h + 1, :] += jnp.sum(dst, axis=0, keepdims=True)
                dca = dca + (dst[:, 0:LANES] + dst[:, LANES:2 * LANES])
                return dk, dv, dca

            zero = jnp.zeros((TQ, HEAD_DIM), F32)
            carry = tile(j, (zero, zero, jnp.zeros((TQ, LANES), F32)), True)
            dk, dv, dca = lax.fori_loop(j + 1, nq, functools.partial(tile, masked=False), carry)
            dk_ref[:, hs] = dk
            dv_ref[:, hs] = dv
            dcs_all = jnp.where(lane == h, jnp.sum(dca, axis=1, keepdims=True), dcs_all)
        dcs_ref[...] = dcs_all

    blk = pl.BlockSpec((TQ, D_ATTN), _row)
    return pl.pallas_call(
        body, grid=(nq,), name="attn_bwd",
        out_shape=(jax.ShapeDtypeStruct((s, D_ATTN), F32), jax.ShapeDtypeStruct((s, D_ATTN), F32),
                   jax.ShapeDtypeStruct((s, D_ATTN), F32), jax.ShapeDtypeStruct((s, LANES), F32),
                   jax.ShapeDtypeStruct((nq, HEADS, TQ), F32)),
        in_specs=[SMEM_WHOLE, blk, blk, pl.BlockSpec((TQ, LANES), _row), VMEM_WHOLE, VMEM_WHOLE, VMEM_WHOLE, VMEM_WHOLE],
        out_specs=(pl.BlockSpec((s, D_ATTN), _fixed), blk, blk, pl.BlockSpec((TQ, LANES), _row),
                   pl.BlockSpec((nq, HEADS, TQ), lambda j: (0, 0, 0))),
        compiler_params=_params(1),
    )(c0, k, v, c, q, do, lset3, dlt3)


def _pre_attn_bwd(dq, dk, dv, dcs, drs, fl, dy, x, dh1, g1, wqkv, wf, wu):
    s, d = x.shape
    nt = s // TS
    n = TS + HALO

    def body(dq_ref, dk_ref, dv_ref, dcs_ref, drs_ref, fl_ref, dy_ref, x_ref, dh1_ref, g_ref, wqkv_ref, wf_ref, wu_ref,
             gx_ref, dqkv_ref, dfb_ref, dub_ref, dg_ref, db_ref, ybuf, ccar, dlog):
        i = pl.program_id(0)
        ti = nt - 1 - i

        @pl.when(i == 0)
        def _():
            ybuf[TS:n, :] = jnp.zeros((HALO, D_POOL), F32)
            ccar[...] = jnp.zeros_like(ccar)
            dg_ref[...] = jnp.zeros_like(dg_ref)
            db_ref[...] = jnp.zeros_like(db_ref)

        rr = lax.broadcasted_iota(jnp.int32, (TS, TS), 0)
        cc = lax.broadcasted_iota(jnp.int32, (TS, TS), 1)
        triu = (cc >= rr).astype(F32)
        dlog[...] = ccar[...] + jnp.dot(triu, drs_ref[...] - dcs_ref[...], precision=HIGHEST, preferred_element_type=F32)
        ccar[...] = dlog[0:1, :]
        df = dlog[...] * jax.nn.sigmoid(-fl_ref[...])
        db_ref[...] += jnp.sum(df, axis=0, keepdims=True)
        dfb = df.astype(BF16)
        dfb_ref[...] = dfb

        t = ti * TS + lax.broadcasted_iota(jnp.int32, (TS, 1), 0)
        dy = dy_ref[...]
        for g, w in enumerate(POOL_WINDOWS):
            cols = slice(g * POOL_CH, (g + 1) * POOL_CH)
            ybuf[0:TS, cols] = dy[:, cols] / jnp.minimum(t + 1, w).astype(F32)
        for g, w in enumerate(POOL_WINDOWS):
            cols = slice(g * POOL_CH, (g + 1) * POOL_CH)
            sm = ybuf[:, cols]
            step = 1
            while step < w:
                sm = sm + pltpu.roll(sm, n - step, 0)
                step *= 2
            dub_ref[:, cols] = (sm[0:TS, :] - dy[:, cols]).astype(BF16)
        ybuf[TS:n, :] = ybuf[0:HALO, :]

        dqkv_ref[:, 0:D_ATTN] = (dq_ref[...] * 0.125).astype(BF16)
        dqkv_ref[:, D_ATTN:2 * D_ATTN] = dk_ref[...].astype(BF16)
        dqkv_ref[:, 2 * D_ATTN:] = dv_ref[...].astype(BF16)
        dhn = _nn(dqkv_ref[...], wqkv_ref[...]) + _nn(dfb, wf_ref[...]) + _nn(dub_ref[...], wu_ref[...])
        dx, dg = _rms_bwd(x_ref[...], g_ref[...], dhn)
        gx_ref[...] = dh1_ref[...] + dx
        dg_ref[...] += dg

    rev = lambda i: (nt - 1 - i, 0)
    blk = lambda w: pl.BlockSpec((TS, w), rev)
    return pl.pallas_call(
        body, grid=(nt,), name="pre_attn_bwd",
        out_shape=(jax.ShapeDtypeStruct((s, d), F32), jax.ShapeDtypeStruct((s, 3 * D_ATTN), BF16),
                   jax.ShapeDtypeStruct((s, LANES), BF16), jax.ShapeDtypeStruct((s, D_POOL), BF16),
                   jax.ShapeDtypeStruct((1, d), F32), jax.ShapeDtypeStruct((1, LANES), F32)),
        in_specs=[blk(D_ATTN), blk(D_ATTN), blk(D_ATTN), blk(LANES), blk(LANES), blk(LANES), blk(D_POOL), blk(d), blk(d),
                  pl.BlockSpec((1, d), _fixed), pl.BlockSpec(wqkv.shape, _fixed), pl.BlockSpec(wf.shape, _fixed),
                  pl.BlockSpec(wu.shape, _fixed)],
        out_specs=(blk(d), blk(3 * D_ATTN), blk(LANES), blk(D_POOL),
                   pl.BlockSpec((1, d), _fixed), pl.BlockSpec((1, LANES), _fixed)),
        scratch_shapes=[pltpu.VMEM((n, D_POOL), F32), pltpu.VMEM((1, LANES), F32), pltpu.VMEM((TS, LANES), F32)],
        compiler_params=_params(1),
    )(dq, dk, dv, dcs, drs, fl, dy, x, dh1, g1, wqkv, wf, wu)


def _wgrad(a, b, out_dtype, name):
    s, m = a.shape
    n = b.shape[1]
    tm = next(t for t in (512, 256, 128) if m % t == 0)
    ns = s // TS

    def body(a_ref, b_ref, o_ref, acc):
        i = pl.program_id(1)

        @pl.when(i == 0)
        def _():
            acc[...] = jnp.zeros_like(acc)

        acc[...] += _tn(a_ref[...], b_ref[pl.ds(i * TS, TS), :])

        @pl.when(i == ns - 1)
        def _():
            o_ref[...] = acc[...].astype(out_dtype)

    return pl.pallas_call(
        body, grid=(m // tm, ns), name=name, out_shape=jax.ShapeDtypeStruct((m, n), out_dtype),
        in_specs=[pl.BlockSpec((TS, tm), lambda j, i: (i, j)), VMEM_WHOLE],
        out_specs=pl.BlockSpec((tm, n), lambda j, i: (j, 0)),
        scratch_shapes=[pltpu.VMEM((tm, n), F32)],
        compiler_params=_params(2),
    )(a, b)


def _adamw(w, g, m, v):
    m = ADAM_B1 * m + (1.0 - ADAM_B1) * g
    v = ADAM_B2 * v + (1.0 - ADAM_B2) * (g * g)
    m_hat = m / (1.0 - ADAM_B1 ** ADAM_STEP)
    v_hat = v / (1.0 - ADAM_B2 ** ADAM_STEP)
    delta = -ADAM_LR * (m_hat / (jnp.sqrt(v_hat) + ADAM_EPS) + ADAM_WD * w)
    return delta, m, v


def _pair_sum(mine, theirs):
    nk, r, c = mine.shape

    def body(a_ref, b_ref, o_ref):
        o_ref[...] = (a_ref[...].astype(F32) + b_ref[...].astype(F32)).astype(BF16)

    blk = pl.BlockSpec((1, TR_PACK, c), lambda k, i: (k, i, 0))
    return pl.pallas_call(
        body, grid=(nk, r // TR_PACK), name="rs_pair_sum", out_shape=jax.ShapeDtypeStruct(mine.shape, BF16),
        in_specs=[blk, blk], out_specs=blk, compiler_params=_params(2),
    )(mine, theirs)


def _reduce_update_big(parts, w, m, v):
    nk, r, c = parts.shape

    def body(p_ref, w_ref, m_ref, v_ref, g_ref, d_ref, nm_ref, nv_ref):
        g = p_ref[0].astype(F32)
        for k in range(1, nk):
            g = g + p_ref[k].astype(F32)
        g_ref[...] = g
        d_ref[...], nm_ref[...], nv_ref[...] = _adamw(w_ref[...], g, m_ref[...], v_ref[...])

    blk = pl.BlockSpec((TR_PACK, c), _row)
    out = jax.ShapeDtypeStruct((r, c), F32)
    return pl.pallas_call(
        body, grid=(r // TR_PACK,), name="reduce_update_big", out_shape=(out,) * 4,
        in_specs=[pl.BlockSpec((nk, TR_PACK, c), lambda i: (0, i, 0)), blk, blk, blk],
        out_specs=(blk,) * 4, compiler_params=_params(1),
    )(parts, w, m, v)


def _reduce_update_small(parts, w, m, v):
    nd = parts.shape[0]

    def body(p_ref, w_ref, m_ref, v_ref, g_ref, d_ref, nm_ref, nv_ref):
        g = p_ref[0]
        for k in range(1, nd):
            g = g + p_ref[k]
        g_ref[...] = g
        d_ref[...], nm_ref[...], nv_ref[...] = _adamw(w_ref[...], g, m_ref[...], v_ref[...])

    out = jax.ShapeDtypeStruct(w.shape, F32)
    return pl.pallas_call(body, name="reduce_update_small", out_shape=(out,) * 4,
                          compiler_params=pltpu.CompilerParams(vmem_limit_bytes=VMEM_LIMIT))(parts, w, m, v)


MESH = pl.DeviceIdType.MESH


def _all_gather(xs, name):
    r, cdim = xs.shape

    def body(x_ref, out_ref, send_sems, recv_sems, local_sem):
        x, y, c = lax.axis_index("x"), lax.axis_index("y"), lax.axis_index("c")
        me, sibling = (x, y, c), (x, y, 1 - c)
        chips = [(1 - x, y), (x, 1 - y), (1 - x, 1 - y)]

        def slot(px, py, pc):
            return out_ref.at[4 * px + 2 * py + pc]

        def copy(k, block, to, src=None):
            return pltpu.make_async_remote_copy(
                src_ref=slot(*block) if src is None else src, dst_ref=slot(*block),
                send_sem=send_sems.at[k], recv_sem=recv_sems.at[k], device_id=to, device_id_type=MESH)

        mine = pltpu.make_async_copy(x_ref, slot(*me), local_sem)
        mine.start()
        first = [copy(0, me, sibling, src=x_ref)]
        first += [copy(1 + j, me, (*chip, c), src=x_ref) for j, chip in enumerate(chips)]
        for cp in first:
            cp.start()
        passed = [copy(4 + j, (*chip, c), sibling) for j, chip in enumerate(chips)]
        for j, chip in enumerate(chips):
            copy(1 + j, (*chip, c), me).wait_recv()
            passed[j].start()
        copy(0, sibling, me).wait_recv()
        for j, chip in enumerate(chips):
            copy(4 + j, (*chip, 1 - c), me).wait_recv()
        for cp in first + passed:
            cp.wait_send()
        mine.wait()

    return pl.pallas_call(
        body, name=name, out_shape=jax.ShapeDtypeStruct((N_DEV, r, cdim), xs.dtype),
        in_specs=[ANY], out_specs=ANY,
        scratch_shapes=[pltpu.SemaphoreType.DMA((7,)), pltpu.SemaphoreType.DMA((7,)), pltpu.SemaphoreType.DMA],
    )(xs)


def _rs_pair(t):
    _, r, cdim = t.shape

    def body(t_ref, mine_ref, theirs_ref, send_sems, recv_sems, local_sems):
        x, y, c = lax.axis_index("x"), lax.axis_index("y"), lax.axis_index("c")
        sibling = (x, y, 1 - c)
        local, remote = [], []
        for k in range(4):
            local.append(pltpu.make_async_copy(t_ref.at[2 * k + c], mine_ref.at[k], local_sems.at[k]))
            remote.append(pltpu.make_async_remote_copy(
                src_ref=t_ref.at[2 * k + (1 - c)], dst_ref=theirs_ref.at[k],
                send_sem=send_sems.at[k], recv_sem=recv_sems.at[k], device_id=sibling, device_id_type=MESH))
        for cp in remote + local:
            cp.start()
        for cp in remote + local:
            cp.wait()

    out = jax.ShapeDtypeStruct((4, r, cdim), t.dtype)
    return pl.pallas_call(
        body, name="rs_pair", out_shape=(out, out), in_specs=[ANY], out_specs=(ANY, ANY),
        scratch_shapes=[pltpu.SemaphoreType.DMA((4,)), pltpu.SemaphoreType.DMA((4,)), pltpu.SemaphoreType.DMA((4,))],
    )(t)


def _rs_chips(b):
    _, r, cdim = b.shape

    def body(b_ref, out_ref, send_sems, recv_sems, local_sem):
        x, y, c = lax.axis_index("x"), lax.axis_index("y"), lax.axis_index("c")
        mychip = 2 * x + y
        chips = [(1 - x, y), (x, 1 - y), (1 - x, 1 - y)]
        own = pltpu.make_async_copy(b_ref.at[mychip], out_ref.at[mychip], local_sem)
        own.start()
        sends = [pltpu.make_async_remote_copy(
            src_ref=b_ref.at[2 * px + py], dst_ref=out_ref.at[mychip],
            send_sem=send_sems.at[j], recv_sem=recv_sems.at[j], device_id=(px, py, c), device_id_type=MESH)
            for j, (px, py) in enumerate(chips)]
        for cp in sends:
            cp.start()
        for j, (px, py) in enumerate(chips):
            pltpu.make_async_remote_copy(
                src_ref=b_ref.at[2 * px + py], dst_ref=out_ref.at[2 * px + py],
                send_sem=send_sems.at[j], recv_sem=recv_sems.at[j], device_id=(px, py, c), device_id_type=MESH).wait()
        own.wait()

    return pl.pallas_call(
        body, name="rs_chips", out_shape=jax.ShapeDtypeStruct(b.shape, b.dtype), in_specs=[ANY], out_specs=ANY,
        scratch_shapes=[pltpu.SemaphoreType.DMA((3,)), pltpu.SemaphoreType.DMA((3,)), pltpu.SemaphoreType.DMA],
    )(b)


def _pad_rows(a, rows):
    return jnp.pad(a, ((0, rows - a.shape[0]), (0, 0)))


def _pack_big(w_in, w_out, w_gate, w_up, w_down, w_ple, w_pg):
    parts = [_pad_rows(w_in[0].T, ROWS_IN), w_out[0], w_gate[0].T, w_up[0].T, w_down[0],
             w_ple[0].T.reshape(32, D_MODEL), w_pg[0]]
    return _pad_rows(jnp.concatenate(parts, axis=0), ROWS_PACK)


def _unpack_big(r):
    return (r[0:257].T[None], r[OFF_OUT:OFF_GATE][None], r[OFF_GATE:OFF_UP].T[None], r[OFF_UP:OFF_DOWN].T[None],
            r[OFF_DOWN:OFF_PLE][None], r[OFF_PLE:OFF_PG].reshape(128, D_PLE).T[None], r[OFF_PG:ROWS_USED][None])


def _pack_small(w_pool, g_mix_pre, g_mix_post, g_ffn_pre, g_ffn_post, g_ple, g_attn, g_pool, pool_scale, b_forget,
                loss=None):
    def row(vrow):
        return jnp.pad(vrow.reshape(1, -1), ((0, 0), (0, D_MODEL - vrow.size)))
    rows = [w_pool.reshape(64, D_MODEL), row(g_mix_pre), row(g_mix_post), row(g_ffn_pre), row(g_ffn_post), row(g_ple),
            row(g_attn), row(g_pool), row(pool_scale), row(b_forget),
            row(loss) if loss is not None else jnp.zeros((1, D_MODEL), F32)]
    return _pad_rows(jnp.concatenate(rows, axis=0), SMALL_ROWS)


def _unpack_small(r):
    return dict(
        w_pool=r[0:64].reshape(1, 4, POOL_CH, POOL_CH), g_mix_pre=r[ROW_G_MIX_PRE:ROW_G_MIX_PRE + 1],
        g_mix_post=r[ROW_G_MIX_POST:ROW_G_MIX_POST + 1], g_ffn_pre=r[ROW_G_FFN_PRE:ROW_G_FFN_PRE + 1],
        g_ffn_post=r[ROW_G_FFN_POST:ROW_G_FFN_POST + 1], g_ple=r[ROW_G_PLE:ROW_G_PLE + 1],
        g_attn_grp=r[ROW_G_ATTN:ROW_G_ATTN + 1, 0:D_ATTN], g_pool_grp=r[ROW_G_POOL:ROW_G_POOL + 1, 0:D_POOL],
        pool_scale=r[ROW_POOL_SCALE:ROW_POOL_SCALE + 1, 0:D_POOL], b_forget=r[ROW_B_FORGET:ROW_B_FORGET + 1, 0:HEADS])


def _local_step(x, p, tgt, small, win_t, wout, wg_t, wu_t, wd, wple_t, wpg):
    wqkv = win_t[0:3 * D_ATTN]
    wf = _pad_rows(win_t[3 * D_ATTN:3 * D_ATTN + HEADS], LANES)
    wu = win_t[3 * D_ATTN + HEADS:]
    wpool = small["w_pool"].astype(BF16)
    bpad = jnp.pad(small["b_forget"], ((0, 0), (0, LANES - HEADS)))

    hn, q, k, v, fl, c, ct3, y, mpre = _pre_attn_fwd(x, small["g_mix_pre"], wqkv, wf, wu, bpad, wpool)
    c0 = c[::TQ, 0:HEADS]
    a, lset3 = _attn_fwd(q, k, v, c0, ct3)
    mix, o, h1, hn2 = _post_attn_fwd(a, mpre, x, small["g_attn_grp"], small["g_pool_grp"], small["pool_scale"], wout,
                                     small["g_mix_post"], small["g_ffn_pre"])
    gate, up, act, ff, h2 = _ffn_fwd(hn2, wg_t, wu_t, wd, h1, small["g_ffn_post"])
    dh2, dff, dgl, dpp, h2b, pb, loss8, dg_ple, dg_ffn_post = _tail_fwd_bwd(
        h2, p, tgt, ff, wple_t, wpg, small["g_ple"], small["g_ffn_post"])
    dgate, dup, dh1, dg_ffn_pre = _ffn_bwd(dff, gate, up, wd, wg_t, wu_t, h1, dh2, small["g_ffn_pre"])
    dob, dab, dlt3, dmpb, dy, dg_mix_post, dg_attn, dg_pool, dps = _post_attn_bwd(
        dh1, o, a, mpre, wout, wpool, small["g_mix_post"], small["g_attn_grp"], small["g_pool_grp"], small["pool_scale"])
    dq, dk, dv, dcs, drs3 = _attn_bwd(q, k, v, dab, c, c0, lset3, dlt3)
    drs = jnp.pad(drs3.transpose(0, 2, 1).reshape(-1, HEADS), ((0, 0), (0, LANES - HEADS)))
    gx, dqkv, dfb, dub, dg_mix_pre, db = _pre_attn_bwd(dq, dk, dv, dcs, drs, fl, dy, x, dh1, small["g_mix_pre"], wqkv, wf, wu)

    dwin_t = jnp.concatenate([_wgrad(dqkv, hn, F32, "wgrad_qkv"), _wgrad(dfb, hn, F32, "wgrad_forget")[0:HEADS],
                              _wgrad(dub, hn, F32, "wgrad_pool_in")], axis=0)
    grads = dict(
        win_t=dwin_t, wout=_wgrad(mix, dob, BF16, "wgrad_out"), wg_t=_wgrad(dgate, hn2, BF16, "wgrad_gate"),
        wu_t=_wgrad(dup, hn2, BF16, "wgrad_up"), wd=_wgrad(act, dff, BF16, "wgrad_down"),
        wple_t=_wgrad(dpp, pb, BF16, "wgrad_ple"), wpg=_wgrad(h2b, dgl, BF16, "wgrad_ple_gate"))
    dwp = _wgrad(y, dmpb, F32, "wgrad_pool")
    dw_pool = jnp.stack([dwp[g * POOL_CH:(g + 1) * POOL_CH, g * POOL_CH:(g + 1) * POOL_CH] for g in range(4)])
    small_part = _pack_small(dw_pool, dg_mix_pre, dg_mix_post, dg_ffn_pre, dg_ffn_post, dg_ple, dg_attn, dg_pool, dps,
                             db[:, 0:HEADS], loss8[0:1, 0:1])
    return gx, small_part, grads


def kernel(x, p, g_mix_pre, w_in, b_forget, g_attn_grp, g_pool_grp, w_pool, pool_scale, w_out, g_mix_post, g_ffn_pre, w_ffn_gate, w_ffn_up, w_ffn_down, g_ffn_post, w_ple_proj, g_ple, w_ple_gate, loss_target, m_g_mix_pre, m_w_in, m_b_forget, m_g_attn_grp, m_g_pool_grp, m_w_pool, m_pool_scale, m_w_out, m_g_mix_post, m_g_ffn_pre, m_w_ffn_gate, m_w_ffn_up, m_w_ffn_down, m_g_ffn_post, m_w_ple_proj, m_g_ple, m_w_ple_gate, v_g_mix_pre, v_w_in, v_b_forget, v_g_attn_grp, v_g_pool_grp, v_w_pool, v_pool_scale, v_w_out, v_g_mix_post, v_g_ffn_pre, v_w_ffn_gate, v_w_ffn_up, v_w_ffn_down, v_g_ffn_post, v_w_ple_proj, v_g_ple, v_w_ple_gate):
    big_w = _pack_big(w_in, w_out, w_ffn_gate, w_ffn_up, w_ffn_down, w_ple_proj, w_ple_gate)
    big_m = _pack_big(m_w_in, m_w_out, m_w_ffn_gate, m_w_ffn_up, m_w_ffn_down, m_w_ple_proj, m_w_ple_gate)
    big_v = _pack_big(v_w_in, v_w_out, v_w_ffn_gate, v_w_ffn_up, v_w_ffn_down, v_w_ple_proj, v_w_ple_gate)

    gathered = _all_gather(big_w.astype(BF16), "gather_weights")
    win_t = gathered[:, 0:257].reshape(D_IN, D_MODEL)
    wout = gathered[:, OFF_OUT:OFF_GATE].reshape(D_MODEL, D_MODEL)
    wg_t = gathered[:, OFF_GATE:OFF_UP].reshape(D_FF, D_MODEL)
    wu_t = gathered[:, OFF_UP:OFF_DOWN].reshape(D_FF, D_MODEL)
    wd = gathered[:, OFF_DOWN:OFF_PLE].reshape(D_FF, D_MODEL)
    wple_t = gathered[:, OFF_PLE:OFF_PG].reshape(D_MODEL, D_PLE)
    wpg = gathered[:, OFF_PG:ROWS_USED].reshape(D_MODEL, D_MODEL)

    small = dict(w_pool=w_pool[0], g_mix_pre=g_mix_pre, g_mix_post=g_mix_post, g_ffn_pre=g_ffn_pre,
                 g_ffn_post=g_ffn_post, g_ple=g_ple, g_attn_grp=g_attn_grp, g_pool_grp=g_pool_grp,
                 pool_scale=pool_scale, b_forget=b_forget)
    gx, small_part, grads = _local_step(x[0], p[0, 0], loss_target[0], small, win_t, wout, wg_t, wu_t, wd, wple_t, wpg)

    nd = N_DEV
    send = jnp.concatenate([
        jnp.pad(grads["win_t"].reshape(nd, 257, D_MODEL), ((0, 0), (0, ROWS_IN - 257), (0, 0))).astype(BF16),
        grads["wout"].reshape(nd, 128, D_MODEL), grads["wg_t"].reshape(nd, 352, D_MODEL),
        grads["wu_t"].reshape(nd, 352, D_MODEL), grads["wd"].reshape(nd, 352, D_MODEL),
        grads["wple_t"].reshape(nd, 32, D_MODEL), grads["wpg"].reshape(nd, 128, D_MODEL),
        jnp.zeros((nd, ROWS_PACK - ROWS_USED, D_MODEL), BF16)], axis=1)
    mine, theirs = _rs_pair(send)
    by_chip = _rs_chips(_pair_sum(mine, theirs))
    g_big, d_big, nm_big, nv_big = _reduce_update_big(by_chip, big_w, big_m, big_v)

    small_all = _all_gather(small_part, "gather_small")
    sm_w =_pack_small(w_pool, g_mix_pre, g_mix_post, g_ffn_pre, g_ffn_post, g_ple, g_attn_grp, g_pool_grp, pool_scale, b_forget)
    sm_m = _pack_small(m_w_pool, m_g_mix_pre, m_g_mix_post, m_g_ffn_pre, m_g_ffn_post, m_g_ple, m_g_attn_grp, m_g_pool_grp, m_pool_scale, m_b_forget)
    sm_v = _pack_small(v_w_pool, v_g_mix_pre, v_g_mix_post, v_g_ffn_pre, v_g_ffn_post, v_g_ple, v_g_attn_grp, v_g_pool_grp, v_pool_scale, v_b_forget)
    g_sm, d_sm, nm_sm, nv_sm = _reduce_update_small(small_all, sm_w, sm_m, sm_v)
    loss = g_sm[ROW_LOSS, 0]

    def leaves(big, sm):
        b_in, b_out, b_gate, b_up, b_down, b_ple, b_pg = _unpack_big(big)
        s = _unpack_small(sm)
        return (s["g_mix_pre"], b_in, s["b_forget"], s["g_attn_grp"], s["g_pool_grp"], s["w_pool"], s["pool_scale"], b_out,
                s["g_mix_post"], s["g_ffn_pre"], b_gate, b_up, b_down, s["g_ffn_post"], b_ple, s["g_ple"], b_pg)

    return (loss, gx[None], *leaves(g_big, g_sm), *leaves(d_big, d_sm), *leaves(nm_big, nm_sm), *leaves(nv_big, nv_sm))
```

```python
import functools

import jax
import jax.numpy as jnp
from jax import lax
from jax.experimental import pallas as pl
from jax.experimental.pallas import tpu as pltpu

F32 = jnp.float32
BF16 = jnp.bfloat16
HIGHEST = lax.Precision.HIGHEST

D_MODEL = 1024
HEADS = 8
HEAD_DIM = 64
D_ATTN = HEADS * HEAD_DIM
POOL_WINDOWS = (2, 4, 8, 16)
POOL_CH = 128
D_POOL = POOL_CH * len(POOL_WINDOWS)
D_FF = 2816
D_PLE = 256
D_IN = 3 * D_ATTN + HEADS + D_POOL
RMS_EPS = 1e-6
N_DEV = 8

ADAM_LR = 0.001
ADAM_B1 = 0.9
ADAM_B2 = 0.999
ADAM_EPS = 1e-08
ADAM_WD = 0.01
ADAM_STEP = 10

LANES = 128
HALO = 16
TS = 512
TQ = 256
TN_FF = 256
NEG = -1e30
VMEM_LIMIT = 56 * 1024 * 1024

ROWS_IN = 272
OFF_OUT = ROWS_IN
OFF_GATE = OFF_OUT + 128
OFF_UP = OFF_GATE + 352
OFF_DOWN = OFF_UP + 352
OFF_PLE = OFF_DOWN + 352
OFF_PG = OFF_PLE + 32
ROWS_USED = OFF_PG + 128
ROWS_PACK = 1664
TR_PACK = 208

SMALL_ROWS = 80
ROW_G_MIX_PRE, ROW_G_MIX_POST, ROW_G_FFN_PRE, ROW_G_FFN_POST, ROW_G_PLE = 64, 65, 66, 67, 68
ROW_G_ATTN, ROW_G_POOL, ROW_POOL_SCALE, ROW_B_FORGET, ROW_LOSS = 69, 70, 71, 72, 73


def _nn(a, b):
    return jnp.dot(a, b, preferred_element_type=F32)


def _nt(a, b):
    return lax.dot_general(a, b, (((1,), (1,)), ((), ())), preferred_element_type=F32)


def _tn(a, b):
    return lax.dot_general(a, b, (((0,), (0,)), ((), ())), preferred_element_type=F32)


def _rstd(v):
    return lax.rsqrt(jnp.mean(v * v, axis=-1, keepdims=True) + RMS_EPS)


def _rms_bwd(v, g, dy):
    r = _rstd(v)
    vh = v * r
    t = dy * g
    dv = r * (t - vh * jnp.mean(t * vh, axis=-1, keepdims=True))
    return dv, jnp.sum(dy * vh, axis=0, keepdims=True)


def _params(n_grid):
    return pltpu.CompilerParams(dimension_semantics=("arbitrary",) * n_grid, vmem_limit_bytes=VMEM_LIMIT)


def _row(i):
    return (i, 0)


def _fixed(*_):
    return (0, 0)


VMEM_WHOLE = pl.BlockSpec(memory_space=pltpu.VMEM)
SMEM_WHOLE = pl.BlockSpec(memory_space=pltpu.SMEM)
ANY = pl.BlockSpec(memory_space=pl.ANY)


def _pre_attn_fwd(x, g1, wqkv, wf, wu, bpad, wpool):
    s, d = x.shape
    nt = s // TS
    sub = TS // TQ

    def body(x_ref, g_ref, wqkv_ref, wf_ref, wu_ref, b_ref, wp_ref,
             hn_ref, q_ref, k_ref, v_ref, fl_ref, c_ref, ct_ref, y_ref, mp_ref, ubuf, ccar):
        i = pl.program_id(0)

        @pl.when(i == 0)
        def _():
            ubuf[0:HALO, :] = jnp.zeros((HALO, D_POOL), F32)
            ccar[...] = jnp.zeros_like(ccar)

        xv = x_ref[...]
        hn = (xv * _rstd(xv) * g_ref[...]).astype(BF16)
        hn_ref[...] = hn
        zq = _nt(hn, wqkv_ref[...])
        q_ref[...] = (zq[:, 0:D_ATTN] * 0.125).astype(BF16)
        k_ref[...] = zq[:, D_ATTN:2 * D_ATTN].astype(BF16)
        v_ref[...] = zq[:, 2 * D_ATTN:3 * D_ATTN].astype(BF16)

        fl = _nt(hn, wf_ref[...]) + b_ref[...]
        fl_ref[...] = fl
        logf = jax.nn.log_sigmoid(fl)
        rr = lax.broadcasted_iota(jnp.int32, (TS, TS), 0)
        cc = lax.broadcasted_iota(jnp.int32, (TS, TS), 1)
        tril = (cc <= rr).astype(F32)
        c = jnp.dot(tril, logf, precision=HIGHEST, preferred_element_type=F32) + ccar[...]
        c_ref[...] = c
        ccar[...] = c_ref[TS - 1:TS, :]
        ct = c.T
        for a in range(sub):
            ct_ref[a] = ct[0:HEADS, a * TQ:(a + 1) * TQ]

        u = _nt(hn, wu_ref[...])
        ubuf[HALO:HALO + TS, :] = u
        t = i * TS + lax.broadcasted_iota(jnp.int32, (TS, 1), 0)
        for g, w in enumerate(POOL_WINDOWS):
            cols = slice(g * POOL_CH, (g + 1) * POOL_CH)
            sm = ubuf[:, cols]
            step = 1
            while step < w:
                sm = sm + pltpu.roll(sm, step, 0)
                step *= 2
            cnt = jnp.minimum(t + 1, w).astype(F32)
            yg = (sm[HALO:, :] / cnt - u[:, cols]).astype(BF16)
            y_ref[:, cols] = yg
            mp_ref[:, cols] = _nn(yg, wp_ref[g])
        ubuf[0:HALO, :] = u[TS - HALO:, :]

    outs = (
        jax.ShapeDtypeStruct((s, d), BF16),
        jax.ShapeDtypeStruct((s, D_ATTN), BF16), jax.ShapeDtypeStruct((s, D_ATTN), BF16),
        jax.ShapeDtypeStruct((s, D_ATTN), BF16),
        jax.ShapeDtypeStruct((s, LANES), F32), jax.ShapeDtypeStruct((s, LANES), F32),
        jax.ShapeDtypeStruct((s // TQ, HEADS, TQ), F32),
        jax.ShapeDtypeStruct((s, D_POOL), BF16), jax.ShapeDtypeStruct((s, D_POOL), F32),
    )
    return pl.pallas_call(
        body, grid=(nt,), out_shape=outs, name="pre_attn_fwd",
        in_specs=[pl.BlockSpec((TS, d), _row), pl.BlockSpec((1, d), _fixed),
                  pl.BlockSpec(wqkv.shape, _fixed), pl.BlockSpec(wf.shape, _fixed), pl.BlockSpec(wu.shape, _fixed),
                  pl.BlockSpec((1, LANES), _fixed), pl.BlockSpec(wpool.shape, lambda i: (0, 0, 0))],
        out_specs=(pl.BlockSpec((TS, d), _row),
                   pl.BlockSpec((TS, D_ATTN), _row), pl.BlockSpec((TS, D_ATTN), _row), pl.BlockSpec((TS, D_ATTN), _row),
                   pl.BlockSpec((TS, LANES), _row), pl.BlockSpec((TS, LANES), _row),
                   pl.BlockSpec((sub, HEADS, TQ), lambda i: (i, 0, 0)),
                   pl.BlockSpec((TS, D_POOL), _row), pl.BlockSpec((TS, D_POOL), _row)),
        scratch_shapes=[pltpu.VMEM((TS + HALO, D_POOL), F32), pltpu.VMEM((1, LANES), F32)],
        compiler_params=_params(1),
    )(x, g1, wqkv, wf, wu, bpad, wpool)


def _attn_fwd(q, k, v, c0, ct3):
    s = q.shape[0]
    nq = s // TQ

    def body(c0_ref, q_ref, k_ref, v_ref, ct_ref, a_ref, lset_ref):
        i = pl.program_id(0)
        rows = lax.broadcasted_iota(jnp.int32, (TQ, TQ), 0)
        cols = lax.broadcasted_iota(jnp.int32, (TQ, TQ), 1)
        causal = cols <= rows
        lane = lax.broadcasted_iota(jnp.int32, (TQ, LANES), 1)
        lse_all = jnp.zeros((TQ, LANES), F32)
        for h in range(HEADS):
            hs = slice(h * HEAD_DIM, (h + 1) * HEAD_DIM)
            qh = q_ref[:, hs]
            c0h = c0_ref[i, h]

            def logits(j, qh=qh, c0h=c0h, hs=hs, h=h):
                kj = k_ref[pl.ds(j * TQ, TQ), hs]
                return _nt(qh, kj) + (c0h - ct_ref[j, h:h + 1, :])

            sc = jnp.where(causal, logits(i), NEG)
            m = jnp.max(sc, axis=1, keepdims=True)
            p = jnp.exp(sc - m)
            l = jnp.sum(p, axis=1, keepdims=True)
            acc = _nn(p.astype(BF16), v_ref[pl.ds(i * TQ, TQ), hs])

            def step(j, carry, logits=logits, hs=hs):
                m, l, acc = carry
                sc = logits(j)
                mn = jnp.maximum(m, jnp.max(sc, axis=1, keepdims=True))
                al = jnp.exp(m - mn)
                p = jnp.exp(sc - mn)
                l = al * l + jnp.sum(p, axis=1, keepdims=True)
                acc = al * acc + _nn(p.astype(BF16), v_ref[pl.ds(j * TQ, TQ), hs])
                return mn, l, acc

            m, l, acc = lax.fori_loop(0, i, step, (m, l, acc))
            a_ref[:, hs] = acc / l
            lse_all = jnp.where(lane == h, m + jnp.log(l), lse_all)
        lset_ref[0] = lse_all.T[0:HEADS, :]

    return pl.pallas_call(
        body, grid=(nq,), name="attn_fwd",
        out_shape=(jax.ShapeDtypeStruct((s, D_ATTN), F32), jax.ShapeDtypeStruct((nq, HEADS, TQ), F32)),
        in_specs=[SMEM_WHOLE, pl.BlockSpec((TQ, D_ATTN), _row), VMEM_WHOLE, VMEM_WHOLE, VMEM_WHOLE],
        out_specs=(pl.BlockSpec((TQ, D_ATTN), _row), pl.BlockSpec((1, HEADS, TQ), lambda i: (i, 0, 0))),
        compiler_params=_params(1),
    )(c0, q, k, v, ct3)


def _post_attn_fwd(a, mpre, x, g_attn, g_pool, pscale, wout, g_post, g_ffn_pre):
    s, d = x.shape

    def body(a_ref, mp_ref, x_ref, ga_ref, gp_ref, ps_ref, wo_ref, gpost_ref, gpre_ref,
             mix_ref, o_ref, h1_ref, hn2_ref):
        av = a_ref[...]
        mix_ref[:, 0:D_ATTN] = (av * _rstd(av) * ga_ref[...]).astype(BF16)
        mv = mp_ref[...] * ps_ref[...]
        mix_ref[:, D_ATTN:] = (mv * _rstd(mv) * gp_ref[...]).astype(BF16)
        o = _nn(mix_ref[...], wo_ref[...])
        o_ref[...] = o
        h1 = x_ref[...] + o * _rstd(o) * gpost_ref[...]
        h1_ref[...] = h1
        hn2_ref[...] = (h1 * _rstd(h1) * gpre_ref[...]).astype(BF16)

    vec = lambda n: pl.BlockSpec((1, n), _fixed)
    return pl.pallas_call(
        body, grid=(s // TS,), name="post_attn_fwd",
        out_shape=(jax.ShapeDtypeStruct((s, d), BF16), jax.ShapeDtypeStruct((s, d), F32),
                   jax.ShapeDtypeStruct((s, d), F32), jax.ShapeDtypeStruct((s, d), BF16)),
        in_specs=[pl.BlockSpec((TS, D_ATTN), _row), pl.BlockSpec((TS, D_POOL), _row), pl.BlockSpec((TS, d), _row),
                  vec(D_ATTN), vec(D_POOL), vec(D_POOL), pl.BlockSpec(wout.shape, _fixed), vec(d), vec(d)],
        out_specs=(pl.BlockSpec((TS, d), _row),) * 4,
        compiler_params=_params(1),
    )(a, mpre, x, g_attn, g_pool, pscale, wout, g_post, g_ffn_pre)


def _ffn_fwd(hn2, wg, wu, wd, h1, g_post):
    s, d = h1.shape
    nc = D_FF // TN_FF

    def body(hn_ref, wg_ref, wu_ref, wd_ref, h1_ref, g_ref, gate_ref, up_ref, act_ref, ff_ref, h2_ref, acc):
        j = pl.program_id(1)

        @pl.when(j == 0)
        def _():
            acc[...] = jnp.zeros_like(acc)

        hn = hn_ref[...]
        gt = _nt(hn, wg_ref[...])
        up = _nt(hn, wu_ref[...])
        act = (gt * jax.nn.sigmoid(gt) * up).astype(BF16)
        gate_ref[...] = gt.astype(BF16)
        up_ref[...] = up.astype(BF16)
        act_ref[...] = act
        acc[...] += _nn(act, wd_ref[...])

        @pl.when(j == nc - 1)
        def _():
            ff = acc[...]
            ff_ref[...] = ff
            h2_ref[...] = h1_ref[...] + ff * _rstd(ff) * g_ref[...]

    rowblk = pl.BlockSpec((TS, d), lambda i, j: (i, 0))
    wblk = pl.BlockSpec((TN_FF, d), lambda i, j: (j, 0))
    chunk = pl.BlockSpec((TS, TN_FF), lambda i, j: (i, j))
    return pl.pallas_call(
        body, grid=(s // TS, nc), name="ffn_fwd",
        out_shape=(jax.ShapeDtypeStruct((s, D_FF), BF16),) * 3 + (jax.ShapeDtypeStruct((s, d), F32),) * 2,
        in_specs=[rowblk, wblk, wblk, wblk, rowblk, pl.BlockSpec((1, d), lambda i, j: (0, 0))],
        out_specs=(chunk, chunk, chunk, rowblk, rowblk),
        scratch_shapes=[pltpu.VMEM((TS, d), F32)],
        compiler_params=_params(2),
    )(hn2, wg, wu, wd, h1, g_post)


def _tail_fwd_bwd(h2, p, tgt, ff, wple, wpg, g_ple, g_ffn_post):
    s, d = h2.shape

    def body(h2_ref, p_ref, t_ref, ff_ref, wple_ref, wpg_ref, gple_ref, gfp_ref,
             dh2_ref, dff_ref, dgl_ref, dpp_ref, h2b_ref, pb_ref, loss_ref, dgple_ref, dgfp_ref):
        i = pl.program_id(0)

        @pl.when(i == 0)
        def _():
            loss_ref[...] = jnp.zeros_like(loss_ref)
            dgple_ref[...] = jnp.zeros_like(dgple_ref)
            dgfp_ref[...] = jnp.zeros_like(dgfp_ref)

        h2 = h2_ref[...]
        h2b = h2.astype(BF16)
        h2b_ref[...] = h2b
        pb = p_ref[...].astype(BF16)
        pb_ref[...] = pb
        pp = _nt(pb, wple_ref[...])
        gple = gple_ref[...]
        e = pp * _rstd(pp) * gple
        sg = jax.nn.sigmoid(_nn(h2b, wpg_ref[...]))
        diff = h2 + sg * e - t_ref[...]
        sq = jnp.sum(jnp.sum(diff * diff, axis=1, keepdims=True), axis=0, keepdims=True)
        loss_ref[...] += jnp.broadcast_to(sq * (0.5 / d), loss_ref.shape)
        dh3 = diff * (1.0 / d)
        dgl = (dh3 * e * sg * (1.0 - sg)).astype(BF16)
        dgl_ref[...] = dgl
        dh2 = dh3 + _nt(dgl, wpg_ref[...])
        dh2_ref[...] = dh2
        dpp, dg = _rms_bwd(pp, gple, dh3 * sg)
        dpp_ref[...] = dpp.astype(BF16)
        dgple_ref[...] += dg
        dff, dg = _rms_bwd(ff_ref[...], gfp_ref[...], dh2)
        dff_ref[...] = dff.astype(BF16)
        dgfp_ref[...] += dg

    rowblk = pl.BlockSpec((TS, d), _row)
    vec = pl.BlockSpec((1, d), _fixed)
    return pl.pallas_call(
        body, grid=(s // TS,), name="tail_fwd_bwd",
        out_shape=(jax.ShapeDtypeStruct((s, d), F32), jax.ShapeDtypeStruct((s, d), BF16),
                   jax.ShapeDtypeStruct((s, d), BF16), jax.ShapeDtypeStruct((s, d), BF16),
                   jax.ShapeDtypeStruct((s, d), BF16), jax.ShapeDtypeStruct((s, D_PLE), BF16),
                   jax.ShapeDtypeStruct((8, LANES), F32), jax.ShapeDtypeStruct((1, d), F32),
                   jax.ShapeDtypeStruct((1, d), F32)),
        in_specs=[rowblk, pl.BlockSpec((TS, D_PLE), _row), rowblk, rowblk,
                  pl.BlockSpec(wple.shape, _fixed), pl.BlockSpec(wpg.shape, _fixed), vec, vec],
        out_specs=(rowblk, rowblk, rowblk, rowblk, rowblk, pl.BlockSpec((TS, D_PLE), _row),
                   pl.BlockSpec((8, LANES), _fixed), vec, vec),
        compiler_params=_params(1),
    )(h2, p, tgt, ff, wple, wpg, g_ple, g_ffn_post)


def _ffn_bwd(dff, gate, up, wd, wg, wu, h1, dh2, g_pre):
    s, d = h1.shape
    nc = D_FF // TN_FF

    def body(dff_ref, gate_ref, up_ref, wd_ref, wg_ref, wu_ref, h1_ref, dh2_ref, g_ref,
             dgate_ref, dup_ref, dh1_ref, dg_ref, acc):
        i = pl.program_id(0)
        j = pl.program_id(1)

        @pl.when((i == 0) & (j == 0))
        def _():
            dg_ref[...] = jnp.zeros_like(dg_ref)

        @pl.when(j == 0)
        def _():
            acc[...] = jnp.zeros_like(acc)

        dact = _nt(dff_ref[...], wd_ref[...])
        gt = gate_ref[...].astype(F32)
        sg = jax.nn.sigmoid(gt)
        dup = (dact * gt * sg).astype(BF16)
        dgate = (dact * up_ref[...].astype(F32) * (sg * (1.0 + gt * (1.0 - sg)))).astype(BF16)
        dgate_ref[...] = dgate
        dup_ref[...] = dup
        acc[...] += _nn(dgate, wg_ref[...]) + _nn(dup, wu_ref[...])

        @pl.when(j == nc - 1)
        def _():
            dv, dg = _rms_bwd(h1_ref[...], g_ref[...], acc[...])
            dh1_ref[...] = dh2_ref[...] + dv
            dg_ref[...] += dg

    rowblk = pl.BlockSpec((TS, d), lambda i, j: (i, 0))
    wblk = pl.BlockSpec((TN_FF, d), lambda i, j: (j, 0))
    chunk = pl.BlockSpec((TS, TN_FF), lambda i, j: (i, j))
    vec = pl.BlockSpec((1, d), lambda i, j: (0, 0))
    return pl.pallas_call(
        body, grid=(s // TS, nc), name="ffn_bwd",
        out_shape=(jax.ShapeDtypeStruct((s, D_FF), BF16), jax.ShapeDtypeStruct((s, D_FF), BF16),
                   jax.ShapeDtypeStruct((s, d), F32), jax.ShapeDtypeStruct((1, d), F32)),
        in_specs=[rowblk, chunk, chunk, wblk, wblk, wblk, rowblk, rowblk, vec],
        out_specs=(chunk, chunk, rowblk, vec),
        scratch_shapes=[pltpu.VMEM((TS, d), F32)],
        compiler_params=_params(2),
    )(dff, gate, up, wd, wg, wu, h1, dh2, g_pre)


def _post_attn_bwd(dh1, o, a, mpre, wout, wpool, g_post, g_attn, g_pool, pscale):
    s, d = dh1.shape
    sub = TS // TQ

    def body(dh1_ref, o_ref, a_ref, mp_ref, wo_ref, wp_ref, gpost_ref, ga_ref, gp_ref, ps_ref,
             dob_ref, dab_ref, dlt_ref, dmpb_ref, dy_ref, dgpost_ref, dga_ref, dgp_ref, dps_ref):
        i = pl.program_id(0)

        @pl.when(i == 0)
        def _():
            dgpost_ref[...] = jnp.zeros_like(dgpost_ref)
            dga_ref[...] = jnp.zeros_like(dga_ref)
            dgp_ref[...] = jnp.zeros_like(dgp_ref)
            dps_ref[...] = jnp.zeros_like(dps_ref)

        do, dg = _rms_bwd(o_ref[...], gpost_ref[...], dh1_ref[...])
        dgpost_ref[...] += dg
        dob = do.astype(BF16)
        dob_ref[...] = dob
        dmix = _nt(dob, wo_ref[...])

        av = a_ref[...]
        da, dg = _rms_bwd(av, ga_ref[...], dmix[:, 0:D_ATTN])
        dga_ref[...] += dg
        dab_ref[...] = da.astype(BF16)
        hsel = (lax.shift_right_logical(lax.broadcasted_iota(jnp.int32, (HEADS, D_ATTN), 1), 6)
                == lax.broadcasted_iota(jnp.int32, (HEADS, D_ATTN), 0)).astype(F32)
        dlt = lax.dot_general(hsel, da * av, (((1,), (1,)), ((), ())), precision=HIGHEST, preferred_element_type=F32)
        for q in range(sub):
            dlt_ref[q] = dlt[:, q * TQ:(q + 1) * TQ]

        ps = ps_ref[...]
        mp = mp_ref[...]
        dm, dg = _rms_bwd(mp * ps, gp_ref[...], dmix[:, D_ATTN:])
        dgp_ref[...] += dg
        dps_ref[...] += jnp.sum(dm * mp, axis=0, keepdims=True)
        dmpb = (dm * ps).astype(BF16)
        dmpb_ref[...] = dmpb
        for g in range(len(POOL_WINDOWS)):
            cols = slice(g * POOL_CH, (g + 1) * POOL_CH)
            dy_ref[:, cols] = _nt(dmpb[:, cols], wp_ref[g])

    rowblk = pl.BlockSpec((TS, d), _row)
    half = pl.BlockSpec((TS, D_ATTN), _row)
    vec = lambda n: pl.BlockSpec((1, n), _fixed)
    return pl.pallas_call(
        body, grid=(s // TS,), name="post_attn_bwd",
        out_shape=(jax.ShapeDtypeStruct((s, d), BF16), jax.ShapeDtypeStruct((s, D_ATTN), BF16),
                   jax.ShapeDtypeStruct((s // TQ, HEADS, TQ), F32), jax.ShapeDtypeStruct((s, D_POOL), BF16),
                   jax.ShapeDtypeStruct((s, D_POOL), F32), jax.ShapeDtypeStruct((1, d), F32),
                   jax.ShapeDtypeStruct((1, D_ATTN), F32), jax.ShapeDtypeStruct((1, D_POOL), F32),
                   jax.ShapeDtypeStruct((1, D_POOL), F32)),
        in_specs=[rowblk, rowblk, half, half, pl.BlockSpec(wout.shape, _fixed),
                  pl.BlockSpec(wpool.shape, lambda i: (0, 0, 0)), vec(d), vec(D_ATTN), vec(D_POOL), vec(D_POOL)],
        out_specs=(rowblk, half, pl.BlockSpec((sub, HEADS, TQ), lambda i: (i, 0, 0)), half, half,
                   vec(d), vec(D_ATTN), vec(D_POOL), vec(D_POOL)),
        compiler_params=_params(1),
    )(dh1, o, a, mpre, wout, wpool, g_post, g_attn, g_pool, pscale)


def _attn_bwd(q, k, v, do, c, c0, lset3, dlt3):
    s = q.shape[0]
    nq = s // TQ

    def body(c0_ref, k_ref, v_ref, c_ref, q_ref, do_ref, lset_ref, dlt_ref, dq_ref, dk_ref, dv_ref, dcs_ref, drs_ref):
        j = pl.program_id(0)

        @pl.when(j == 0)
        def _():
            dq_ref[...] = jnp.zeros_like(dq_ref)
            drs_ref[...] = jnp.zeros_like(drs_ref)

        krow = lax.broadcasted_iota(jnp.int32, (TQ, TQ), 0)
        qcol = lax.broadcasted_iota(jnp.int32, (TQ, TQ), 1)
        valid = krow <= qcol
        lane = lax.broadcasted_iota(jnp.int32, (TQ, LANES), 1)
        c128 = c_ref[...]
        dcs_all = jnp.zeros((TQ, LANES), F32)
        for h in range(HEADS):
            hs = slice(h * HEAD_DIM, (h + 1) * HEAD_DIM)
            kj = k_ref[:, hs]
            vj = v_ref[:, hs]
            cj = jnp.sum(jnp.where(lane == h, c128, 0.0), axis=1, keepdims=True)

            def tile(i, carry, masked, kj=kj, vj=vj, cj=cj, hs=hs, h=h):
                dk, dv, dca = carry
                qi = q_ref[pl.ds(i * TQ, TQ), hs]
                doi = do_ref[pl.ds(i * TQ, TQ), hs]
                st = _nt(kj, qi) + (c0_ref[i, h] - cj) - lset_ref[i, h:h + 1, :]
                if masked:
                    st = jnp.where(valid, st, NEG)
                pt = jnp.exp(st)
                dv = dv + _nn(pt.astype(BF16), doi)
                dst = pt * (_nt(vj, doi) - dlt_ref[i, h:h + 1, :])
                dsb = dst.astype(BF16)
                dk = dk + _nn(dsb, qi)
                dq_ref[pl.ds(i * TQ, TQ), hs] += _tn(dsb, kj)
                drs_ref[i, h:h + 1, :] += jnp.sum(dst, axis=0, keepdims=True)
                dca = dca + (dst[:, 0:LANES] + dst[:, LANES:2 * LANES])
                return dk, dv, dca

            zero = jnp.zeros((TQ, HEAD_DIM), F32)
            carry = tile(j, (zero, zero, jnp.zeros((TQ, LANES), F32)), True)
            dk, dv, dca = lax.fori_loop(j + 1, nq, functools.partial(tile, masked=False), carry)
            dk_ref[:, hs] = dk
            dv_ref[:, hs] = dv
            dcs_all = jnp.where(lane == h, jnp.sum(dca, axis=1, keepdims=True), dcs_all)
        dcs_ref[...] = dcs_all

    blk = pl.BlockSpec((TQ, D_ATTN), _row)
    return pl.pallas_call(
        body, grid=(nq,), name="attn_bwd",
        out_shape=(jax.ShapeDtypeStruct((s, D_ATTN), F32), jax.ShapeDtypeStruct((s, D_ATTN), F32),
                   jax.ShapeDtypeStruct((s, D_ATTN), F32), jax.ShapeDtypeStruct((s, LANES), F32),
                   jax.ShapeDtypeStruct((nq, HEADS, TQ), F32)),
        in_specs=[SMEM_WHOLE, blk, blk, pl.BlockSpec((TQ, LANES), _row), VMEM_WHOLE, VMEM_WHOLE, VMEM_WHOLE, VMEM_WHOLE],
        out_specs=(pl.BlockSpec((s, D_ATTN), _fixed), blk, blk, pl.BlockSpec((TQ, LANES), _row),
                   pl.BlockSpec((nq, HEADS, TQ), lambda j: (0, 0, 0))),
        compiler_params=_params(1),
    )(c0, k, v, c, q, do, lset3, dlt3)


def _pre_attn_bwd(dq, dk, dv, dcs, drs, fl, dy, x, dh1, g1, wqkv, wf, wu):
    s, d = x.shape
    nt = s // TS
    n = TS + HALO

    def body(dq_ref, dk_ref, dv_ref, dcs_ref, drs_ref, fl_ref, dy_ref, x_ref, dh1_ref, g_ref, wqkv_ref, wf_ref, wu_ref,
             gx_ref, dqkv_ref, dfb_ref, dub_ref, dg_ref, db_ref, ybuf, ccar, dlog):
        i = pl.program_id(0)
        ti = nt - 1 - i

        @pl.when(i == 0)
        def _():
            ybuf[TS:n, :] = jnp.zeros((HALO, D_POOL), F32)
            ccar[...] = jnp.zeros_like(ccar)
            dg_ref[...] = jnp.zeros_like(dg_ref)
            db_ref[...] = jnp.zeros_like(db_ref)

        rr = lax.broadcasted_iota(jnp.int32, (TS, TS), 0)
        cc = lax.broadcasted_iota(jnp.int32, (TS, TS), 1)
        triu = (cc >= rr).astype(F32)
        dlog[...] = ccar[...] + jnp.dot(triu, drs_ref[...] - dcs_ref[...], precision=HIGHEST, preferred_element_type=F32)
        ccar[...] = dlog[0:1, :]
        df = dlog[...] * jax.nn.sigmoid(-fl_ref[...])
        db_ref[...] += jnp.sum(df, axis=0, keepdims=True)
        dfb = df.astype(BF16)
        dfb_ref[...] = dfb

        t = ti * TS + lax.broadcasted_iota(jnp.int32, (TS, 1), 0)
        dy = dy_ref[...]
        for g, w in enumerate(POOL_WINDOWS):
            cols = slice(g * POOL_CH, (g + 1) * POOL_CH)
            ybuf[0:TS, cols] = dy[:, cols] / jnp.minimum(t + 1, w).astype(F32)
        for g, w in enumerate(POOL_WINDOWS):
            cols = slice(g * POOL_CH, (g + 1) * POOL_CH)
            sm = ybuf[:, cols]
            step = 1
            while step < w:
                sm = sm + pltpu.roll(sm, n - step, 0)
                step *= 2
            dub_ref[:, cols] = (sm[0:TS, :] - dy[:, cols]).astype(BF16)
        ybuf[TS:n, :] = ybuf[0:HALO, :]

        dqkv_ref[:, 0:D_ATTN] = (dq_ref[...] * 0.125).astype(BF16)
        dqkv_ref[:, D_ATTN:2 * D_ATTN] = dk_ref[...].astype(BF16)
        dqkv_ref[:, 2 * D_ATTN:] = dv_ref[...].astype(BF16)
        dhn = _nn(dqkv_ref[...], wqkv_ref[...]) + _nn(dfb, wf_ref[...]) + _nn(dub_ref[...], wu_ref[...])
        dx, dg = _rms_bwd(x_ref[...], g_ref[...], dhn)
        gx_ref[...] = dh1_ref[...] + dx
        dg_ref[...] += dg

    rev = lambda i: (nt - 1 - i, 0)
    blk = lambda w: pl.BlockSpec((TS, w), rev)
    return pl.pallas_call(
        body, grid=(nt,), name="pre_attn_bwd",
        out_shape=(jax.ShapeDtypeStruct((s, d), F32), jax.ShapeDtypeStruct((s, 3 * D_ATTN), BF16),
                   jax.ShapeDtypeStruct((s, LANES), BF16), jax.ShapeDtypeStruct((s, D_POOL), BF16),
                   jax.ShapeDtypeStruct((1, d), F32), jax.ShapeDtypeStruct((1, LANES), F32)),
        in_specs=[blk(D_ATTN), blk(D_ATTN), blk(D_ATTN), blk(LANES), blk(LANES), blk(LANES), blk(D_POOL), blk(d), blk(d),
                  pl.BlockSpec((1, d), _fixed), pl.BlockSpec(wqkv.shape, _fixed), pl.BlockSpec(wf.shape, _fixed),
                  pl.BlockSpec(wu.shape, _fixed)],
        out_specs=(blk(d), blk(3 * D_ATTN), blk(LANES), blk(D_POOL),
                   pl.BlockSpec((1, d), _fixed), pl.BlockSpec((1, LANES), _fixed)),
        scratch_shapes=[pltpu.VMEM((n, D_POOL), F32), pltpu.VMEM((1, LANES), F32), pltpu.VMEM((TS, LANES), F32)],
        compiler_params=_params(1),
    )(dq, dk, dv, dcs, drs, fl, dy, x, dh1, g1, wqkv, wf, wu)


def _wgrad(a, b, out_dtype, name):
    s, m = a.shape
    n = b.shape[1]
    tm = next(t for t in (512, 256, 128) if m % t == 0)
    ns = s // TS

    def body(a_ref, b_ref, o_ref, acc):
        i = pl.program_id(1)

        @pl.when(i == 0)
        def _():
            acc[...] = jnp.zeros_like(acc)

        acc[...] += _tn(a_ref[...], b_ref[pl.ds(i * TS, TS), :])

        @pl.when(i == ns - 1)
        def _():
            o_ref[...] = acc[...].astype(out_dtype)

    return pl.pallas_call(
        body, grid=(m // tm, ns), name=name, out_shape=jax.ShapeDtypeStruct((m, n), out_dtype),
        in_specs=[pl.BlockSpec((TS, tm), lambda j, i: (i, j)), VMEM_WHOLE],
        out_specs=pl.BlockSpec((tm, n), lambda j, i: (j, 0)),
        scratch_shapes=[pltpu.VMEM((tm, n), F32)],
        compiler_params=_params(2),
    )(a, b)


def _adamw(w, g, m, v):
    m = ADAM_B1 * m + (1.0 - ADAM_B1) * g
    v = ADAM_B2 * v + (1.0 - ADAM_B2) * (g * g)
    m_hat = m / (1.0 - ADAM_B1 ** ADAM_STEP)
    v_hat = v / (1.0 - ADAM_B2 ** ADAM_STEP)
    delta = -ADAM_LR * (m_hat / (jnp.sqrt(v_hat) + ADAM_EPS) + ADAM_WD * w)
    return delta, m, v


def _pair_sum(core, t, theirs):
    nk, r, c = theirs.shape

    def body(core_ref, a_ref, b_ref, o_ref):
        o_ref[...] = (a_ref[...].astype(F32) + b_ref[...].astype(F32)).astype(BF16)

    blk = pl.BlockSpec((1, TR_PACK, c), lambda k, i, core_ref: (k, i, 0))
    return pl.pallas_call(
        body, name="rs_pair_sum", out_shape=jax.ShapeDtypeStruct(theirs.shape, BF16),
        grid_spec=pltpu.PrefetchScalarGridSpec(
            num_scalar_prefetch=1, grid=(nk, r // TR_PACK),
            in_specs=[pl.BlockSpec((1, TR_PACK, c), lambda k, i, core_ref: (2 * k + core_ref[0], i, 0)), blk],
            out_specs=blk),
        compiler_params=_params(2),
    )(core, t, theirs)


def _reduce_update_big(parts, w, m, v):
    nk, r, c = parts.shape

    def body(p_ref, w_ref, m_ref, v_ref, g_ref, d_ref, nm_ref, nv_ref):
        g = p_ref[0].astype(F32)
        for k in range(1, nk):
            g = g + p_ref[k].astype(F32)
        g_ref[...] = g
        d_ref[...], nm_ref[...], nv_ref[...] = _adamw(w_ref[...], g, m_ref[...], v_ref[...])

    blk = pl.BlockSpec((TR_PACK, c), _row)
    out = jax.ShapeDtypeStruct((r, c), F32)
    return pl.pallas_call(
        body, grid=(r // TR_PACK,), name="reduce_update_big", out_shape=(out,) * 4,
        in_specs=[pl.BlockSpec((nk, TR_PACK, c), lambda i: (0, i, 0)), blk, blk, blk],
        out_specs=(blk,) * 4, compiler_params=_params(1),
    )(parts, w, m, v)


def _reduce_update_small(parts, w, m, v):
    nd = parts.shape[0]

    def body(p_ref, w_ref, m_ref, v_ref, g_ref, d_ref, nm_ref, nv_ref):
        g = p_ref[0]
        for k in range(1, nd):
            g = g + p_ref[k]
        g_ref[...] = g
        d_ref[...], nm_ref[...], nv_ref[...] = _adamw(w_ref[...], g, m_ref[...], v_ref[...])

    out = jax.ShapeDtypeStruct(w.shape, F32)
    return pl.pallas_call(body, name="reduce_update_small", out_shape=(out,) * 4,
                          compiler_params=pltpu.CompilerParams(vmem_limit_bytes=VMEM_LIMIT))(parts, w, m, v)


MESH = pl.DeviceIdType.MESH


def _copy_through_vmem(src_hbm, dst_hbm, stage, sem):
    load = pltpu.make_async_copy(src_hbm, stage, sem)
    load.start()
    load.wait()
    store = pltpu.make_async_copy(stage, dst_hbm, sem)
    store.start()
    store.wait()


def _all_gather(xs, name):
    r, cdim = xs.shape

    def body(x_ref, out_ref, stage, send_sems, recv_sems, local_sem):
        x, y, c = lax.axis_index("x"), lax.axis_index("y"), lax.axis_index("c")
        me, sibling = (x, y, c), (x, y, 1 - c)
        chips = [(1 - x, y), (x, 1 - y), (1 - x, 1 - y)]

        def slot(px, py, pc):
            return out_ref.at[4 * px + 2 * py + pc]

        def copy(k, block, to, src=None):
            return pltpu.make_async_remote_copy(
                src_ref=slot(*block) if src is None else src, dst_ref=slot(*block),
                send_sem=send_sems.at[k], recv_sem=recv_sems.at[k], device_id=to, device_id_type=MESH)

        first = [copy(0, me, sibling, src=x_ref)]
        first += [copy(1 + j, me, (*chip, c), src=x_ref) for j, chip in enumerate(chips)]
        for cp in first:
            cp.start()
        _copy_through_vmem(x_ref, slot(*me), stage, local_sem)
        passed = [copy(4 + j, (*chip, c), sibling) for j, chip in enumerate(chips)]
        for j, chip in enumerate(chips):
            copy(1 + j, (*chip, c), me).wait_recv()
            passed[j].start()
        copy(0, sibling, me).wait_recv()
        for j, chip in enumerate(chips):
            copy(4 + j, (*chip, 1 - c), me).wait_recv()
        for cp in first + passed:
            cp.wait_send()

    return pl.pallas_call(
        body, name=name, out_shape=jax.ShapeDtypeStruct((N_DEV, r, cdim), xs.dtype),
        in_specs=[ANY], out_specs=ANY,
        scratch_shapes=[pltpu.VMEM((r, cdim), xs.dtype), pltpu.SemaphoreType.DMA((7,)), pltpu.SemaphoreType.DMA((7,)),
                        pltpu.SemaphoreType.DMA],
        compiler_params=pltpu.CompilerParams(vmem_limit_bytes=VMEM_LIMIT),
    )(xs)


def _rs_pair(t):
    _, r, cdim = t.shape

    def body(t_ref, theirs_ref, send_sems, recv_sems):
        x, y, c = lax.axis_index("x"), lax.axis_index("y"), lax.axis_index("c")
        remote = [pltpu.make_async_remote_copy(
            src_ref=t_ref.at[2 * k + (1 - c)], dst_ref=theirs_ref.at[k],
            send_sem=send_sems.at[k], recv_sem=recv_sems.at[k], device_id=(x, y, 1 - c), device_id_type=MESH)
            for k in range(4)]
        for cp in remote:
            cp.start()
        for cp in remote:
            cp.wait()

    return pl.pallas_call(
        body, name="rs_pair", out_shape=jax.ShapeDtypeStruct((4, r, cdim), t.dtype), in_specs=[ANY], out_specs=ANY,
        scratch_shapes=[pltpu.SemaphoreType.DMA((4,)), pltpu.SemaphoreType.DMA((4,))],
    )(t)


def _rs_chips(b):
    _, r, cdim = b.shape

    def body(b_ref, out_ref, stage, send_sems, recv_sems, local_sem):
        x, y, c = lax.axis_index("x"), lax.axis_index("y"), lax.axis_index("c")
        mychip = 2 * x + y
        chips = [(1 - x, y), (x, 1 - y), (1 - x, 1 - y)]
        sends = [pltpu.make_async_remote_copy(
            src_ref=b_ref.at[2 * px + py], dst_ref=out_ref.at[mychip],
            send_sem=send_sems.at[j], recv_sem=recv_sems.at[j], device_id=(px, py, c), device_id_type=MESH)
            for j, (px, py) in enumerate(chips)]
        for cp in sends:
            cp.start()
        _copy_through_vmem(b_ref.at[mychip], out_ref.at[mychip], stage, local_sem)
        for j, (px, py) in enumerate(chips):
            pltpu.make_async_remote_copy(
                src_ref=b_ref.at[2 * px + py], dst_ref=out_ref.at[2 * px + py],
                send_sem=send_sems.at[j], recv_sem=recv_sems.at[j], device_id=(px, py, c), device_id_type=MESH).wait()

    return pl.pallas_call(
        body, name="rs_chips", out_shape=jax.ShapeDtypeStruct(b.shape, b.dtype), in_specs=[ANY], out_specs=ANY,
        scratch_shapes=[pltpu.VMEM((r, cdim), b.dtype), pltpu.SemaphoreType.DMA((3,)), pltpu.SemaphoreType.DMA((3,)),
                        pltpu.SemaphoreType.DMA],
        compiler_params=pltpu.CompilerParams(vmem_limit_bytes=VMEM_LIMIT),
    )(b)


def _pad_rows(a, rows):
    return jnp.pad(a, ((0, rows - a.shape[0]), (0, 0)))


def _pack_big(w_in, w_out, w_gate, w_up, w_down, w_ple, w_pg):
    parts = [_pad_rows(w_in[0].T, ROWS_IN), w_out[0], w_gate[0].T, w_up[0].T, w_down[0],
             w_ple[0].T.reshape(32, D_MODEL), w_pg[0]]
    return _pad_rows(jnp.concatenate(parts, axis=0), ROWS_PACK)


def _unpack_big(r):
    return (r[0:257].T[None], r[OFF_OUT:OFF_GATE][None], r[OFF_GATE:OFF_UP].T[None], r[OFF_UP:OFF_DOWN].T[None],
            r[OFF_DOWN:OFF_PLE][None], r[OFF_PLE:OFF_PG].reshape(128, D_PLE).T[None], r[OFF_PG:ROWS_USED][None])


def _pack_small(w_pool, g_mix_pre, g_mix_post, g_ffn_pre, g_ffn_post, g_ple, g_attn, g_pool, pool_scale, b_forget,
                loss=None):
    def row(vrow):
        return jnp.pad(vrow.reshape(1, -1), ((0, 0), (0, D_MODEL - vrow.size)))
    rows = [w_pool.reshape(64, D_MODEL), row(g_mix_pre), row(g_mix_post), row(g_ffn_pre), row(g_ffn_post), row(g_ple),
            row(g_attn), row(g_pool), row(pool_scale), row(b_forget),
            row(loss) if loss is not None else jnp.zeros((1, D_MODEL), F32)]
    return _pad_rows(jnp.concatenate(rows, axis=0), SMALL_ROWS)


def _unpack_small(r):
    return dict(
        w_pool=r[0:64].reshape(1, 4, POOL_CH, POOL_CH), g_mix_pre=r[ROW_G_MIX_PRE:ROW_G_MIX_PRE + 1],
        g_mix_post=r[ROW_G_MIX_POST:ROW_G_MIX_POST + 1], g_ffn_pre=r[ROW_G_FFN_PRE:ROW_G_FFN_PRE + 1],
        g_ffn_post=r[ROW_G_FFN_POST:ROW_G_FFN_POST + 1], g_ple=r[ROW_G_PLE:ROW_G_PLE + 1],
        g_attn_grp=r[ROW_G_ATTN:ROW_G_ATTN + 1, 0:D_ATTN], g_pool_grp=r[ROW_G_POOL:ROW_G_POOL + 1, 0:D_POOL],
        pool_scale=r[ROW_POOL_SCALE:ROW_POOL_SCALE + 1, 0:D_POOL], b_forget=r[ROW_B_FORGET:ROW_B_FORGET + 1, 0:HEADS])


def _local_step(x, p, tgt, small, win_t, wout, wg_t, wu_t, wd, wple_t, wpg):
    wqkv = win_t[0:3 * D_ATTN]
    wf = _pad_rows(win_t[3 * D_ATTN:3 * D_ATTN + HEADS], LANES)
    wu = win_t[3 * D_ATTN + HEADS:]
    wpool = small["w_pool"].astype(BF16)
    bpad = jnp.pad(small["b_forget"], ((0, 0), (0, LANES - HEADS)))

    hn, q, k, v, fl, c, ct3, y, mpre = _pre_attn_fwd(x, small["g_mix_pre"], wqkv, wf, wu, bpad, wpool)
    c0 = c[::TQ, 0:HEADS]
    a, lset3 = _attn_fwd(q, k, v, c0, ct3)
    mix, o, h1, hn2 = _post_attn_fwd(a, mpre, x, small["g_attn_grp"], small["g_pool_grp"], small["pool_scale"], wout,
                                     small["g_mix_post"], small["g_ffn_pre"])
    gate, up, act, ff, h2 = _ffn_fwd(hn2, wg_t, wu_t, wd, h1, small["g_ffn_post"])
    dh2, dff, dgl, dpp, h2b, pb, loss8, dg_ple, dg_ffn_post = _tail_fwd_bwd(
        h2, p, tgt, ff, wple_t, wpg, small["g_ple"], small["g_ffn_post"])
    dgate, dup, dh1, dg_ffn_pre = _ffn_bwd(dff, gate, up, wd, wg_t, wu_t, h1, dh2, small["g_ffn_pre"])
    dob, dab, dlt3, dmpb, dy, dg_mix_post, dg_attn, dg_pool, dps = _post_attn_bwd(
        dh1, o, a, mpre, wout, wpool, small["g_mix_post"], small["g_attn_grp"], small["g_pool_grp"], small["pool_scale"])
    dq, dk, dv, dcs, drs3 = _attn_bwd(q, k, v, dab, c, c0, lset3, dlt3)
    drs = jnp.pad(drs3.transpose(0, 2, 1).reshape(-1, HEADS), ((0, 0), (0, LANES - HEADS)))
    gx, dqkv, dfb, dub, dg_mix_pre, db = _pre_attn_bwd(dq, dk, dv, dcs, drs, fl, dy, x, dh1, small["g_mix_pre"], wqkv, wf, wu)

    dwin_t = jnp.concatenate([_wgrad(dqkv, hn, F32, "wgrad_qkv"), _wgrad(dfb, hn, F32, "wgrad_forget")[0:HEADS],
                              _wgrad(dub, hn, F32, "wgrad_pool_in")], axis=0)
    grads = dict(
        win_t=dwin_t, wout=_wgrad(mix, dob, BF16, "wgrad_out"), wg_t=_wgrad(dgate, hn2, BF16, "wgrad_gate"),
        wu_t=_wgrad(dup, hn2, BF16, "wgrad_up"), wd=_wgrad(act, dff, BF16, "wgrad_down"),
        wple_t=_wgrad(dpp, pb, BF16, "wgrad_ple"), wpg=_wgrad(h2b, dgl, BF16, "wgrad_ple_gate"))
    dwp = _wgrad(y, dmpb, F32, "wgrad_pool")
    dw_pool = jnp.stack([dwp[g * POOL_CH:(g + 1) * POOL_CH, g * POOL_CH:(g + 1) * POOL_CH] for g in range(4)])
    small_part = _pack_small(dw_pool, dg_mix_pre, dg_mix_post, dg_ffn_pre, dg_ffn_post, dg_ple, dg_attn, dg_pool, dps,
                             db[:, 0:HEADS], loss8[0:1, 0:1])
    return gx, small_part, grads


def kernel(x, p, g_mix_pre, w_in, b_forget, g_attn_grp, g_pool_grp, w_pool, pool_scale, w_out, g_mix_post, g_ffn_pre, w_ffn_gate, w_ffn_up, w_ffn_down, g_ffn_post, w_ple_proj, g_ple, w_ple_gate, loss_target, m_g_mix_pre, m_w_in, m_b_forget, m_g_attn_grp, m_g_pool_grp, m_w_pool, m_pool_scale, m_w_out, m_g_mix_post, m_g_ffn_pre, m_w_ffn_gate, m_w_ffn_up, m_w_ffn_down, m_g_ffn_post, m_w_ple_proj, m_g_ple, m_w_ple_gate, v_g_mix_pre, v_w_in, v_b_forget, v_g_attn_grp, v_g_pool_grp, v_w_pool, v_pool_scale, v_w_out, v_g_mix_post, v_g_ffn_pre, v_w_ffn_gate, v_w_ffn_up, v_w_ffn_down, v_g_ffn_post, v_w_ple_proj, v_g_ple, v_w_ple_gate):
    big_w = _pack_big(w_in, w_out, w_ffn_gate, w_ffn_up, w_ffn_down, w_ple_proj, w_ple_gate)
    big_m = _pack_big(m_w_in, m_w_out, m_w_ffn_gate, m_w_ffn_up, m_w_ffn_down, m_w_ple_proj, m_w_ple_gate)
    big_v = _pack_big(v_w_in, v_w_out, v_w_ffn_gate, v_w_ffn_up, v_w_ffn_down, v_w_ple_proj, v_w_ple_gate)

    gathered = _all_gather(big_w.astype(BF16), "gather_weights")
    win_t = gathered[:, 0:257].reshape(D_IN, D_MODEL)
    wout = gathered[:, OFF_OUT:OFF_GATE].reshape(D_MODEL, D_MODEL)
    wg_t = gathered[:, OFF_GATE:OFF_UP].reshape(D_FF, D_MODEL)
    wu_t = gathered[:, OFF_UP:OFF_DOWN].reshape(D_FF, D_MODEL)
    wd = gathered[:, OFF_DOWN:OFF_PLE].reshape(D_FF, D_MODEL)
    wple_t = gathered[:, OFF_PLE:OFF_PG].reshape(D_MODEL, D_PLE)
    wpg = gathered[:, OFF_PG:ROWS_USED].reshape(D_MODEL, D_MODEL)

    small = dict(w_pool=w_pool[0], g_mix_pre=g_mix_pre, g_mix_post=g_mix_post, g_ffn_pre=g_ffn_pre,
                 g_ffn_post=g_ffn_post, g_ple=g_ple, g_attn_grp=g_attn_grp, g_pool_grp=g_pool_grp,
                 pool_scale=pool_scale, b_forget=b_forget)
    gx, small_part, grads = _local_step(x[0], p[0, 0], loss_target[0], small, win_t, wout, wg_t, wu_t, wd, wple_t, wpg)

    nd = N_DEV
    send = jnp.concatenate([
        jnp.pad(grads["win_t"].reshape(nd, 257, D_MODEL), ((0, 0), (0, ROWS_IN - 257), (0, 0))).astype(BF16),
        grads["wout"].reshape(nd, 128, D_MODEL), grads["wg_t"].reshape(nd, 352, D_MODEL),
        grads["wu_t"].reshape(nd, 352, D_MODEL), grads["wd"].reshape(nd, 352, D_MODEL),
        grads["wple_t"].reshape(nd, 32, D_MODEL), grads["wpg"].reshape(nd, 128, D_MODEL),
        jnp.zeros((nd, ROWS_PACK - ROWS_USED, D_MODEL), BF16)], axis=1)
    core = lax.axis_index("c").astype(jnp.int32).reshape(1)
    by_chip = _rs_chips(_pair_sum(core, send, _rs_pair(send)))
    g_big, d_big, nm_big, nv_big = _reduce_update_big(by_chip, big_w, big_m, big_v)

    small_all = _all_gather(small_part, "gather_small")
    sm_w =_pack_small(w_pool, g_mix_pre, g_mix_post, g_ffn_pre, g_ffn_post, g_ple, g_attn_grp, g_pool_grp, pool_scale, b_forget)
    sm_m = _pack_small(m_w_pool, m_g_mix_pre, m_g_mix_post, m_g_ffn_pre, m_g_ffn_post, m_g_ple, m_g_attn_grp, m_g_pool_grp, m_pool_scale, m_b_forget)
    sm_v = _pack_small(v_w_pool, v_g_mix_pre, v_g_mix_post, v_g_ffn_pre, v_g_ffn_post, v_g_ple, v_g_attn_grp, v_g_pool_grp, v_pool_scale, v_b_forget)
    g_sm, d_sm, nm_sm, nv_sm = _reduce_update_small(small_all, sm_w, sm_m, sm_v)
    loss = g_sm[ROW_LOSS, 0]

    def leaves(big, sm):
        b_in, b_out, b_gate, b_up, b_down, b_ple, b_pg = _unpack_big(big)
        s = _unpack_small(sm)
        return (s["g_mix_pre"], b_in, s["b_forget"], s["g_attn_grp"], s["g_pool_grp"], s["w_pool"], s["pool_scale"], b_out,
                s["g_mix_post"], s["g_ffn_pre"], b_gate, b_up, b_down, s["g_ffn_post"], b_ple, s["g_ple"], b_pg)

    return (loss, gx[None], *leaves(g_big, g_sm), *leaves(d_big, d_sm), *leaves(nm_big, nm_sm), *leaves(nv_big, nv_sm))
```

```python
import functools

import jax
import jax.numpy as jnp
from jax import lax
from jax.experimental import pallas as pl
from jax.experimental.pallas import tpu as pltpu

F32 = jnp.float32
BF16 = jnp.bfloat16
HIGHEST = lax.Precision.HIGHEST

D_MODEL = 1024
HEADS = 8
HEAD_DIM = 64
D_ATTN = HEADS * HEAD_DIM
POOL_WINDOWS = (2, 4, 8, 16)
POOL_CH = 128
D_POOL = POOL_CH * len(POOL_WINDOWS)
D_FF = 2816
D_PLE = 256
D_IN = 3 * D_ATTN + HEADS + D_POOL
RMS_EPS = 1e-6
N_DEV = 8

ADAM_LR = 0.001
ADAM_B1 = 0.9
ADAM_B2 = 0.999
ADAM_EPS = 1e-08
ADAM_WD = 0.01
ADAM_STEP = 10

LANES = 128
HALO = 16
TS = 512
TQ = 256
TN_FF = 256
NEG = -1e30
VMEM_LIMIT = 56 * 1024 * 1024

ROWS_IN = 272
OFF_OUT = ROWS_IN
OFF_GATE = OFF_OUT + 128
OFF_UP = OFF_GATE + 352
OFF_DOWN = OFF_UP + 352
OFF_PLE = OFF_DOWN + 352
OFF_PG = OFF_PLE + 32
ROWS_USED = OFF_PG + 128
ROWS_PACK = 1664
TR_PACK = 208

SMALL_ROWS = 80
ROW_G_MIX_PRE, ROW_G_MIX_POST, ROW_G_FFN_PRE, ROW_G_FFN_POST, ROW_G_PLE = 64, 65, 66, 67, 68
ROW_G_ATTN, ROW_G_POOL, ROW_POOL_SCALE, ROW_B_FORGET, ROW_LOSS = 69, 70, 71, 72, 73


def _nn(a, b):
    return jnp.dot(a, b, preferred_element_type=F32)


def _nt(a, b):
    return lax.dot_general(a, b, (((1,), (1,)), ((), ())), preferred_element_type=F32)


def _tn(a, b):
    return lax.dot_general(a, b, (((0,), (0,)), ((), ())), preferred_element_type=F32)


def _rstd(v):
    return lax.rsqrt(jnp.mean(v * v, axis=-1, keepdims=True) + RMS_EPS)


def _rms_bwd(v, g, dy):
    r = _rstd(v)
    vh = v * r
    t = dy * g
    dv = r * (t - vh * jnp.mean(t * vh, axis=-1, keepdims=True))
    return dv, jnp.sum(dy * vh, axis=0, keepdims=True)


def _params(n_grid):
    return pltpu.CompilerParams(dimension_semantics=("arbitrary",) * n_grid, vmem_limit_bytes=VMEM_LIMIT)


def _row(i):
    return (i, 0)


def _fixed(*_):
    return (0, 0)


VMEM_WHOLE = pl.BlockSpec(memory_space=pltpu.VMEM)
SMEM_WHOLE = pl.BlockSpec(memory_space=pltpu.SMEM)
ANY = pl.BlockSpec(memory_space=pl.ANY)


AUG = 128
BIAS_LANE = HEAD_DIM
ONE_LANE = HEAD_DIM + 3
SPARE_LANE = HEADS


def _attn_layout_constants():
    import numpy as np
    place = np.zeros((D_ATTN, HEADS * AUG), np.float32)
    for r in range(D_ATTN):
        place[r, (r // HEAD_DIM) * AUG + r % HEAD_DIM] = 1.0
    bias_k = np.zeros((3, LANES, HEADS * AUG), np.float32)
    bias_q = np.zeros((3, LANES, HEADS * AUG), np.float32)
    for h in range(HEADS):
        for part in range(3):
            bias_k[part, h, h * AUG + BIAS_LANE + part] = -1.0
            bias_q[part, h, h * AUG + ONE_LANE + part] = 1.0
            bias_k[0, SPARE_LANE, h * AUG + ONE_LANE + part] = 1.0
            bias_q[0, SPARE_LANE, h * AUG + BIAS_LANE + part] = 1.0
    as_bf = lambda a: jnp.asarray(a, BF16)
    return dict(place=as_bf(place), place_t=as_bf(place.T), bias_k=as_bf(bias_k),
                bias_q_t=as_bf(bias_q.transpose(0, 2, 1)), eye=as_bf(np.eye(D_ATTN, dtype=np.float32)))


def _pre_attn_fwd(x, g1, wqkv, wf, wu, bpad, wpool, lay):
    s, d = x.shape
    nt = s // TS
    sub = TS // TQ

    def body(x_ref, g_ref, wqkv_ref, wf_ref, wu_ref, b_ref, wp_ref, place_ref, place_t_ref, bk_ref, bqt_ref, eye_ref,
             hn_ref, q_ref, ka_ref, v_ref, qat_ref, vt_ref, kt_ref, fl_ref, y_ref, mp_ref, ubuf, ccar, cbuf):
        i = pl.program_id(0)

        @pl.when(i == 0)
        def _():
            ubuf[0:HALO, :] = jnp.zeros((HALO, D_POOL), F32)
            ccar[...] = jnp.zeros_like(ccar)

        xv = x_ref[...]
        hn = (xv * _rstd(xv) * g_ref[...]).astype(BF16)
        hn_ref[...] = hn
        zq = _nt(hn, wqkv_ref[...])
        qb = (zq[:, 0:D_ATTN] * 0.125).astype(BF16)
        kb = zq[:, D_ATTN:2 * D_ATTN].astype(BF16)
        vb = zq[:, 2 * D_ATTN:3 * D_ATTN].astype(BF16)
        q_ref[...] = qb
        v_ref[...] = vb

        fl = _nt(hn, wf_ref[...]) + b_ref[...]
        fl_ref[...] = fl
        logf = jax.nn.log_sigmoid(fl)
        rr = lax.broadcasted_iota(jnp.int32, (TS, TS), 0)
        cc = lax.broadcasted_iota(jnp.int32, (TS, TS), 1)
        tril = (cc <= rr).astype(F32)
        c = jnp.dot(tril, logf, precision=HIGHEST, preferred_element_type=F32) + ccar[...]
        cbuf[...] = c
        ccar[...] = cbuf[TS - 1:TS, :]
        hi = c.astype(BF16)
        rest = c - hi.astype(F32)
        mid = rest.astype(BF16)
        lo = (rest - mid.astype(F32)).astype(BF16)
        lane = lax.broadcasted_iota(jnp.int32, (TS, LANES), 1)
        parts = (jnp.where(lane == SPARE_LANE, 1.0, hi).astype(BF16), mid, lo)
        ka = _nn(kb, place_ref[...])
        qat = _nt(place_t_ref[...], qb)
        for part in range(3):
            ka = ka + _nn(parts[part], bk_ref[part])
            qat = qat + _nt(bqt_ref[part], parts[part])
        ka_ref[...] = ka.astype(BF16)
        qat = qat.astype(BF16)
        vt = _nt(eye_ref[...], vb).astype(BF16)
        kt = _nt(eye_ref[...], kb).astype(BF16)
        for a in range(sub):
            qat_ref[a] = qat[:, a * TQ:(a + 1) * TQ]
            vt_ref[a] = vt[:, a * TQ:(a + 1) * TQ]
            kt_ref[a] = kt[:, a * TQ:(a + 1) * TQ]

        u = _nt(hn, wu_ref[...])
        ubuf[HALO:HALO + TS, :] = u
        t = i * TS + lax.broadcasted_iota(jnp.int32, (TS, 1), 0)
        for g, w in enumerate(POOL_WINDOWS):
            cols = slice(g * POOL_CH, (g + 1) * POOL_CH)
            sm = ubuf[:, cols]
            step = 1
            while step < w:
                sm = sm + pltpu.roll(sm, step, 0)
                step *= 2
            cnt = jnp.minimum(t + 1, w).astype(F32)
            yg = (sm[HALO:, :] / cnt - u[:, cols]).astype(BF16)
            y_ref[:, cols] = yg
            mp_ref[:, cols] = _nn(yg, wp_ref[g])
        ubuf[0:HALO, :] = u[TS - HALO:, :]

    nq = s // TQ
    aug = HEADS * AUG
    outs = (
        jax.ShapeDtypeStruct((s, d), BF16), jax.ShapeDtypeStruct((s, D_ATTN), BF16),
        jax.ShapeDtypeStruct((s, aug), BF16), jax.ShapeDtypeStruct((s, D_ATTN), BF16),
        jax.ShapeDtypeStruct((nq, aug, TQ), BF16), jax.ShapeDtypeStruct((nq, D_ATTN, TQ), BF16),
        jax.ShapeDtypeStruct((nq, D_ATTN, TQ), BF16),
        jax.ShapeDtypeStruct((s, LANES), F32),
        jax.ShapeDtypeStruct((s, D_POOL), BF16), jax.ShapeDtypeStruct((s, D_POOL), F32),
    )
    fixed3 = lambda i: (0, 0, 0)
    tiles3 = lambda rows: pl.BlockSpec((sub, rows, TQ), lambda i: (i, 0, 0))
    return pl.pallas_call(
        body, grid=(nt,), out_shape=outs, name="pre_attn_fwd",
        in_specs=[pl.BlockSpec((TS, d), _row), pl.BlockSpec((1, d), _fixed),
                  pl.BlockSpec(wqkv.shape, _fixed), pl.BlockSpec(wf.shape, _fixed), pl.BlockSpec(wu.shape, _fixed),
                  pl.BlockSpec((1, LANES), _fixed), pl.BlockSpec(wpool.shape, fixed3),
                  pl.BlockSpec(lay["place"].shape, _fixed), pl.BlockSpec(lay["place_t"].shape, _fixed),
                  pl.BlockSpec(lay["bias_k"].shape, fixed3), pl.BlockSpec(lay["bias_q_t"].shape, fixed3),
                  pl.BlockSpec(lay["eye"].shape, _fixed)],
        out_specs=(pl.BlockSpec((TS, d), _row), pl.BlockSpec((TS, D_ATTN), _row),
                   pl.BlockSpec((TS, aug), _row), pl.BlockSpec((TS, D_ATTN), _row),
                   tiles3(aug), tiles3(D_ATTN), tiles3(D_ATTN),
                   pl.BlockSpec((TS, LANES), _row),
                   pl.BlockSpec((TS, D_POOL), _row), pl.BlockSpec((TS, D_POOL), _row)),
        scratch_shapes=[pltpu.VMEM((TS + HALO, D_POOL), F32), pltpu.VMEM((1, LANES), F32), pltpu.VMEM((TS, LANES), F32)],
        compiler_params=_params(1),
    )(x, g1, wqkv, wf, wu, bpad, wpool, lay["place"], lay["place_t"], lay["bias_k"], lay["bias_q_t"], lay["eye"])


def _causal_in_tile():
    krow = lax.broadcasted_iota(jnp.int32, (TQ, TQ), 0)
    qcol = lax.broadcasted_iota(jnp.int32, (TQ, TQ), 1)
    return krow <= qcol


def _attn_fwd(ka, qat3, vt3):
    s = ka.shape[0]
    nq = s // TQ

    def body(qa_ref, ka_ref, vt_ref, a_ref, lset_ref, acc, m_scr, l_scr):
        i = pl.program_id(0)
        m_scr[...] = jnp.full(m_scr.shape, NEG, F32)
        l_scr[...] = jnp.zeros_like(l_scr)
        acc[...] = jnp.zeros_like(acc)

        def tile(j, masked):
            for h in range(HEADS):
                rows = slice(h * HEAD_DIM, (h + 1) * HEAD_DIM)
                aug = slice(h * AUG, (h + 1) * AUG)
                st = _nn(ka_ref[pl.ds(j * TQ, TQ), aug], qa_ref[0, aug, :])
                if masked:
                    st = jnp.where(_causal_in_tile(), st, NEG)
                m_old = m_scr[h:h + 1, :]
                m_new = jnp.maximum(m_old, jnp.max(st, axis=0, keepdims=True))
                al = jnp.exp(m_old - m_new)
                pt = jnp.exp(st - m_new)
                m_scr[h:h + 1, :] = m_new
                l_scr[h:h + 1, :] = al * l_scr[h:h + 1, :] + jnp.sum(pt, axis=0, keepdims=True)
                acc[rows, :] = al * acc[rows, :] + _nn(vt_ref[j, rows, :], pt.astype(BF16))

        def step(j, carry):
            tile(j, False)
            return carry

        lax.fori_loop(0, i, step, 0)
        tile(i, True)
        for h in range(HEADS):
            rows = slice(h * HEAD_DIM, (h + 1) * HEAD_DIM)
            acc[rows, :] = acc[rows, :] / l_scr[h:h + 1, :]
        a_ref[...] = acc[...].T
        lset_ref[0] = m_scr[...] + jnp.log(l_scr[...])

    return pl.pallas_call(
        body, grid=(nq,), name="attn_fwd",
        out_shape=(jax.ShapeDtypeStruct((s, D_ATTN), F32), jax.ShapeDtypeStruct((nq, HEADS, TQ), F32)),
        in_specs=[pl.BlockSpec((1, HEADS * AUG, TQ), lambda i: (i, 0, 0)), VMEM_WHOLE, VMEM_WHOLE],
        out_specs=(pl.BlockSpec((TQ, D_ATTN), _row), pl.BlockSpec((1, HEADS, TQ), lambda i: (i, 0, 0))),
        scratch_shapes=[pltpu.VMEM((D_ATTN, TQ), F32), pltpu.VMEM((HEADS, TQ), F32), pltpu.VMEM((HEADS, TQ), F32)],
        compiler_params=_params(1),
    )(qat3, ka, vt3)


def _post_attn_fwd(a, mpre, x, g_attn, g_pool, pscale, wout, g_post, g_ffn_pre):
    s, d = x.shape

    def body(a_ref, mp_ref, x_ref, ga_ref, gp_ref, ps_ref, wo_ref, gpost_ref, gpre_ref,
             mix_ref, o_ref, h1_ref, hn2_ref):
        av = a_ref[...]
        mix_ref[:, 0:D_ATTN] = (av * _rstd(av) * ga_ref[...]).astype(BF16)
        mv = mp_ref[...] * ps_ref[...]
        mix_ref[:, D_ATTN:] = (mv * _rstd(mv) * gp_ref[...]).astype(BF16)
        o = _nn(mix_ref[...], wo_ref[...])
        o_ref[...] = o
        h1 = x_ref[...] + o * _rstd(o) * gpost_ref[...]
        h1_ref[...] = h1
        hn2_ref[...] = (h1 * _rstd(h1) * gpre_ref[...]).astype(BF16)

    vec = lambda n: pl.BlockSpec((1, n), _fixed)
    return pl.pallas_call(
        body, grid=(s // TS,), name="post_attn_fwd",
        out_shape=(jax.ShapeDtypeStruct((s, d), BF16), jax.ShapeDtypeStruct((s, d), F32),
                   jax.ShapeDtypeStruct((s, d), F32), jax.ShapeDtypeStruct((s, d), BF16)),
        in_specs=[pl.BlockSpec((TS, D_ATTN), _row), pl.BlockSpec((TS, D_POOL), _row), pl.BlockSpec((TS, d), _row),
                  vec(D_ATTN), vec(D_POOL), vec(D_POOL), pl.BlockSpec(wout.shape, _fixed), vec(d), vec(d)],
        out_specs=(pl.BlockSpec((TS, d), _row),) * 4,
        compiler_params=_params(1),
    )(a, mpre, x, g_attn, g_pool, pscale, wout, g_post, g_ffn_pre)


def _ffn_fwd(hn2, wg, wu, wd, h1, g_post):
    s, d = h1.shape
    nc = D_FF // TN_FF

    def body(hn_ref, wg_ref, wu_ref, wd_ref, h1_ref, g_ref, gate_ref, up_ref, act_ref, ff_ref, h2_ref, acc):
        j = pl.program_id(1)

        @pl.when(j == 0)
        def _():
            acc[...] = jnp.zeros_like(acc)

        hn = hn_ref[...]
        gt = _nt(hn, wg_ref[...])
        up = _nt(hn, wu_ref[...])
        act = (gt * jax.nn.sigmoid(gt) * up).astype(BF16)
        gate_ref[...] = gt.astype(BF16)
        up_ref[...] = up.astype(BF16)
        act_ref[...] = act
        acc[...] += _nn(act, wd_ref[...])

        @pl.when(j == nc - 1)
        def _():
            ff = acc[...]
            ff_ref[...] = ff
            h2_ref[...] = h1_ref[...] + ff * _rstd(ff) * g_ref[...]

    rowblk = pl.BlockSpec((TS, d), lambda i, j: (i, 0))
    wblk = pl.BlockSpec((TN_FF, d), lambda i, j: (j, 0))
    chunk = pl.BlockSpec((TS, TN_FF), lambda i, j: (i, j))
    return pl.pallas_call(
        body, grid=(s // TS, nc), name="ffn_fwd",
        out_shape=(jax.ShapeDtypeStruct((s, D_FF), BF16),) * 3 + (jax.ShapeDtypeStruct((s, d), F32),) * 2,
        in_specs=[rowblk, wblk, wblk, wblk, rowblk, pl.BlockSpec((1, d), lambda i, j: (0, 0))],
        out_specs=(chunk, chunk, chunk, rowblk, rowblk),
        scratch_shapes=[pltpu.VMEM((TS, d), F32)],
        compiler_params=_params(2),
    )(hn2, wg, wu, wd, h1, g_post)


def _tail_fwd_bwd(h2, p, tgt, ff, wple, wpg, g_ple, g_ffn_post):
    s, d = h2.shape

    def body(h2_ref, p_ref, t_ref, ff_ref, wple_ref, wpg_ref, gple_ref, gfp_ref,
             dh2_ref, dff_ref, dgl_ref, dpp_ref, h2b_ref, pb_ref, loss_ref, dgple_ref, dgfp_ref):
        i = pl.program_id(0)

        @pl.when(i == 0)
        def _():
            loss_ref[...] = jnp.zeros_like(loss_ref)
            dgple_ref[...] = jnp.zeros_like(dgple_ref)
            dgfp_ref[...] = jnp.zeros_like(dgfp_ref)

        h2 = h2_ref[...]
        h2b = h2.astype(BF16)
        h2b_ref[...] = h2b
        pb = p_ref[...].astype(BF16)
        pb_ref[...] = pb
        pp = _nt(pb, wple_ref[...])
        gple = gple_ref[...]
        e = pp * _rstd(pp) * gple
        sg = jax.nn.sigmoid(_nn(h2b, wpg_ref[...]))
        diff = h2 + sg * e - t_ref[...]
        sq = jnp.sum(jnp.sum(diff * diff, axis=1, keepdims=True), axis=0, keepdims=True)
        loss_ref[...] += jnp.broadcast_to(sq * (0.5 / d), loss_ref.shape)
        dh3 = diff * (1.0 / d)
        dgl = (dh3 * e * sg * (1.0 - sg)).astype(BF16)
        dgl_ref[...] = dgl
        dh2 = dh3 + _nt(dgl, wpg_ref[...])
        dh2_ref[...] = dh2
        dpp, dg = _rms_bwd(pp, gple, dh3 * sg)
        dpp_ref[...] = dpp.astype(BF16)
        dgple_ref[...] += dg
        dff, dg = _rms_bwd(ff_ref[...], gfp_ref[...], dh2)
        dff_ref[...] = dff.astype(BF16)
        dgfp_ref[...] += dg

    rowblk = pl.BlockSpec((TS, d), _row)
    vec = pl.BlockSpec((1, d), _fixed)
    return pl.pallas_call(
        body, grid=(s // TS,), name="tail_fwd_bwd",
        out_shape=(jax.ShapeDtypeStruct((s, d), F32), jax.ShapeDtypeStruct((s, d), BF16),
                   jax.ShapeDtypeStruct((s, d), BF16), jax.ShapeDtypeStruct((s, d), BF16),
                   jax.ShapeDtypeStruct((s, d), BF16), jax.ShapeDtypeStruct((s, D_PLE), BF16),
                   jax.ShapeDtypeStruct((8, LANES), F32), jax.ShapeDtypeStruct((1, d), F32),
                   jax.ShapeDtypeStruct((1, d), F32)),
        in_specs=[rowblk, pl.BlockSpec((TS, D_PLE), _row), rowblk, rowblk,
                  pl.BlockSpec(wple.shape, _fixed), pl.BlockSpec(wpg.shape, _fixed), vec, vec],
        out_specs=(rowblk, rowblk, rowblk, rowblk, rowblk, pl.BlockSpec((TS, D_PLE), _row),
                   pl.BlockSpec((8, LANES), _fixed), vec, vec),
        compiler_params=_params(1),
    )(h2, p, tgt, ff, wple, wpg, g_ple, g_ffn_post)


def _ffn_bwd(dff, gate, up, wd, wg, wu, h1, dh2, g_pre):
    s, d = h1.shape
    nc = D_FF // TN_FF

    def body(dff_ref, gate_ref, up_ref, wd_ref, wg_ref, wu_ref, h1_ref, dh2_ref, g_ref,
             dgate_ref, dup_ref, dh1_ref, dg_ref, acc):
        i = pl.program_id(0)
        j = pl.program_id(1)

        @pl.when((i == 0) & (j == 0))
        def _():
            dg_ref[...] = jnp.zeros_like(dg_ref)

        @pl.when(j == 0)
        def _():
            acc[...] = jnp.zeros_like(acc)

        dact = _nt(dff_ref[...], wd_ref[...])
        gt = gate_ref[...].astype(F32)
        sg = jax.nn.sigmoid(gt)
        dup = (dact * gt * sg).astype(BF16)
        dgate = (dact * up_ref[...].astype(F32) * (sg * (1.0 + gt * (1.0 - sg)))).astype(BF16)
        dgate_ref[...] = dgate
        dup_ref[...] = dup
        acc[...] += _nn(dgate, wg_ref[...]) + _nn(dup, wu_ref[...])

        @pl.when(j == nc - 1)
        def _():
            dv, dg = _rms_bwd(h1_ref[...], g_ref[...], acc[...])
            dh1_ref[...] = dh2_ref[...] + dv
            dg_ref[...] += dg

    rowblk = pl.BlockSpec((TS, d), lambda i, j: (i, 0))
    wblk = pl.BlockSpec((TN_FF, d), lambda i, j: (j, 0))
    chunk = pl.BlockSpec((TS, TN_FF), lambda i, j: (i, j))
    vec = pl.BlockSpec((1, d), lambda i, j: (0, 0))
    return pl.pallas_call(
        body, grid=(s // TS, nc), name="ffn_bwd",
        out_shape=(jax.ShapeDtypeStruct((s, D_FF), BF16), jax.ShapeDtypeStruct((s, D_FF), BF16),
                   jax.ShapeDtypeStruct((s, d), F32), jax.ShapeDtypeStruct((1, d), F32)),
        in_specs=[rowblk, chunk, chunk, wblk, wblk, wblk, rowblk, rowblk, vec],
        out_specs=(chunk, chunk, rowblk, vec),
        scratch_shapes=[pltpu.VMEM((TS, d), F32)],
        compiler_params=_params(2),
    )(dff, gate, up, wd, wg, wu, h1, dh2, g_pre)


def _post_attn_bwd(dh1, o, a, mpre, wout, wpool, g_post, g_attn, g_pool, pscale, eye):
    s, d = dh1.shape
    sub = TS // TQ

    def body(dh1_ref, o_ref, a_ref, mp_ref, wo_ref, wp_ref, gpost_ref, ga_ref, gp_ref, ps_ref, eye_ref,
             dob_ref, dab_ref, dat_ref, dlt_ref, dmpb_ref, dy_ref, dgpost_ref, dga_ref, dgp_ref, dps_ref):
        i = pl.program_id(0)

        @pl.when(i == 0)
        def _():
            dgpost_ref[...] = jnp.zeros_like(dgpost_ref)
            dga_ref[...] = jnp.zeros_like(dga_ref)
            dgp_ref[...] = jnp.zeros_like(dgp_ref)
            dps_ref[...] = jnp.zeros_like(dps_ref)

        do, dg = _rms_bwd(o_ref[...], gpost_ref[...], dh1_ref[...])
        dgpost_ref[...] += dg
        dob = do.astype(BF16)
        dob_ref[...] = dob
        dmix = _nt(dob, wo_ref[...])

        av = a_ref[...]
        da, dg = _rms_bwd(av, ga_ref[...], dmix[:, 0:D_ATTN])
        dga_ref[...] += dg
        dab = da.astype(BF16)
        dab_ref[...] = dab
        dat = _nt(eye_ref[...], dab).astype(BF16)
        hsel = (lax.shift_right_logical(lax.broadcasted_iota(jnp.int32, (HEADS, D_ATTN), 1), 6)
                == lax.broadcasted_iota(jnp.int32, (HEADS, D_ATTN), 0)).astype(F32)
        dlt = lax.dot_general(hsel, da * av, (((1,), (1,)), ((), ())), precision=HIGHEST, preferred_element_type=F32)
        for q in range(sub):
            dlt_ref[q] = dlt[:, q * TQ:(q + 1) * TQ]
            dat_ref[q] = dat[:, q * TQ:(q + 1) * TQ]

        ps = ps_ref[...]
        mp = mp_ref[...]
        dm, dg = _rms_bwd(mp * ps, gp_ref[...], dmix[:, D_ATTN:])
        dgp_ref[...] += dg
        dps_ref[...] += jnp.sum(dm * mp, axis=0, keepdims=True)
        dmpb = (dm * ps).astype(BF16)
        dmpb_ref[...] = dmpb
        for g in range(len(POOL_WINDOWS)):
            cols = slice(g * POOL_CH, (g + 1) * POOL_CH)
            dy_ref[:, cols] = _nt(dmpb[:, cols], wp_ref[g])

    rowblk = pl.BlockSpec((TS, d), _row)
    half = pl.BlockSpec((TS, D_ATTN), _row)
    vec = lambda n: pl.BlockSpec((1, n), _fixed)
    return pl.pallas_call(
        body, grid=(s // TS,), name="post_attn_bwd",
        out_shape=(jax.ShapeDtypeStruct((s, d), BF16), jax.ShapeDtypeStruct((s, D_ATTN), BF16),
                   jax.ShapeDtypeStruct((s // TQ, D_ATTN, TQ), BF16),
                   jax.ShapeDtypeStruct((s // TQ, HEADS, TQ), F32), jax.ShapeDtypeStruct((s, D_POOL), BF16),
                   jax.ShapeDtypeStruct((s, D_POOL), F32), jax.ShapeDtypeStruct((1, d), F32),
                   jax.ShapeDtypeStruct((1, D_ATTN), F32), jax.ShapeDtypeStruct((1, D_POOL), F32),
                   jax.ShapeDtypeStruct((1, D_POOL), F32)),
        in_specs=[rowblk, rowblk, half, half, pl.BlockSpec(wout.shape, _fixed),
                  pl.BlockSpec(wpool.shape, lambda i: (0, 0, 0)), vec(d), vec(D_ATTN), vec(D_POOL), vec(D_POOL),
                  pl.BlockSpec(eye.shape, _fixed)],
        out_specs=(rowblk, half, pl.BlockSpec((sub, D_ATTN, TQ), lambda i: (i, 0, 0)),
                   pl.BlockSpec((sub, HEADS, TQ), lambda i: (i, 0, 0)), half, half,
                   vec(d), vec(D_ATTN), vec(D_POOL), vec(D_POOL)),
        compiler_params=_params(1),
    )(dh1, o, a, mpre, wout, wpool, g_post, g_attn, g_pool, pscale, eye)


def _attn_bwd(ka, v, kt3, qat3, q, do, dot3, lset3, dlt3):
    s = q.shape[0]
    nq = s // TQ

    def body(ka_ref, v_ref, kt_ref, qat_ref, q_ref, do_ref, dot_ref, lset_ref, dlt_ref,
             dqt_ref, dk_ref, dv_ref, dcs_ref, drs_ref, dca):
        j = pl.program_id(0)

        @pl.when(j == 0)
        def _():
            dqt_ref[...] = jnp.zeros_like(dqt_ref)
            drs_ref[...] = jnp.zeros_like(drs_ref)

        dk_ref[...] = jnp.zeros_like(dk_ref)
        dv_ref[...] = jnp.zeros_like(dv_ref)
        dca[...] = jnp.zeros_like(dca)

        def tile(i, masked):
            for h in range(HEADS):
                hs = slice(h * HEAD_DIM, (h + 1) * HEAD_DIM)
                aug = slice(h * AUG, (h + 1) * AUG)
                rows = pl.ds(i * TQ, TQ)
                st = _nn(ka_ref[:, aug], qat_ref[i, aug, :]) - lset_ref[i, h:h + 1, :]
                if masked:
                    st = jnp.where(_causal_in_tile(), st, NEG)
                pt = jnp.exp(st)
                dv_ref[:, hs] += _nn(pt.astype(BF16), do_ref[rows, hs])
                dst = pt * (_nn(v_ref[:, hs], dot_ref[i, hs, :]) - dlt_ref[i, h:h + 1, :])
                dsb = dst.astype(BF16)
                dk_ref[:, hs] += _nn(dsb, q_ref[rows, hs])
                dqt_ref[i, hs, :] += _nn(kt_ref[0, hs, :], dsb)
                drs_ref[i, h:h + 1, :] += jnp.sum(dst, axis=0, keepdims=True)
                dca[:, h * LANES:(h + 1) * LANES] += dst[:, 0:LANES] + dst[:, LANES:2 * LANES]

        def step(i, carry):
            tile(i, False)
            return carry

        tile(j, True)
        lax.fori_loop(j + 1, nq, step, 0)
        lane = lax.broadcasted_iota(jnp.int32, (TQ, LANES), 1)
        dcs_all = jnp.zeros((TQ, LANES), F32)
        for h in range(HEADS):
            colsum = jnp.sum(dca[:, h * LANES:(h + 1) * LANES], axis=1, keepdims=True)
            dcs_all = jnp.where(lane == h, colsum, dcs_all)
        dcs_ref[...] = dcs_all

    blk = pl.BlockSpec((TQ, D_ATTN), _row)
    return pl.pallas_call(
        body, grid=(nq,), name="attn_bwd",
        out_shape=(jax.ShapeDtypeStruct((nq, D_ATTN, TQ), F32), jax.ShapeDtypeStruct((s, D_ATTN), F32),
                   jax.ShapeDtypeStruct((s, D_ATTN), F32), jax.ShapeDtypeStruct((s, LANES), F32),
                   jax.ShapeDtypeStruct((nq, HEADS, TQ), F32)),
        in_specs=[pl.BlockSpec((TQ, HEADS * AUG), _row), blk, pl.BlockSpec((1, D_ATTN, TQ), lambda j: (j, 0, 0)),
                  VMEM_WHOLE, VMEM_WHOLE, VMEM_WHOLE, VMEM_WHOLE, VMEM_WHOLE, VMEM_WHOLE],
        out_specs=(pl.BlockSpec((nq, D_ATTN, TQ), lambda j: (0, 0, 0)), blk, blk, pl.BlockSpec((TQ, LANES), _row),
                   pl.BlockSpec((nq, HEADS, TQ), lambda j: (0, 0, 0))),
        scratch_shapes=[pltpu.VMEM((TQ, HEADS * LANES), F32)],
        compiler_params=_params(1),
    )(ka, v, kt3, qat3, q, do, dot3, lset3, dlt3)


def _pre_attn_bwd(dqt3, dk, dv, dcs, drs, fl, dy, x, dh1, g1, wqkv, wf, wu):
    s, d = x.shape
    nt = s // TS
    n = TS + HALO
    sub = TS // TQ

    def body(dqt_ref, dk_ref, dv_ref, dcs_ref, drs_ref, fl_ref, dy_ref, x_ref, dh1_ref, g_ref, wqkv_ref, wf_ref, wu_ref,
             gx_ref, dqkv_ref, dfb_ref, dub_ref, dg_ref, db_ref, ybuf, ccar, dlog):
        i = pl.program_id(0)
        ti = nt - 1 - i

        @pl.when(i == 0)
        def _():
            ybuf[TS:n, :] = jnp.zeros((HALO, D_POOL), F32)
            ccar[...] = jnp.zeros_like(ccar)
            dg_ref[...] = jnp.zeros_like(dg_ref)
            db_ref[...] = jnp.zeros_like(db_ref)

        rr = lax.broadcasted_iota(jnp.int32, (TS, TS), 0)
        cc = lax.broadcasted_iota(jnp.int32, (TS, TS), 1)
        triu = (cc >= rr).astype(F32)
        dlog[...] = ccar[...] + jnp.dot(triu, drs_ref[...] - dcs_ref[...], precision=HIGHEST, preferred_element_type=F32)
        ccar[...] = dlog[0:1, :]
        df = dlog[...] * jax.nn.sigmoid(-fl_ref[...])
        db_ref[...] += jnp.sum(df, axis=0, keepdims=True)
        dfb = df.astype(BF16)
        dfb_ref[...] = dfb

        t = ti * TS + lax.broadcasted_iota(jnp.int32, (TS, 1), 0)
        dy = dy_ref[...]
        for g, w in enumerate(POOL_WINDOWS):
            cols = slice(g * POOL_CH, (g + 1) * POOL_CH)
            ybuf[0:TS, cols] = dy[:, cols] / jnp.minimum(t + 1, w).astype(F32)
        for g, w in enumerate(POOL_WINDOWS):
            cols = slice(g * POOL_CH, (g + 1) * POOL_CH)
            sm = ybuf[:, cols]
            step = 1
            while step < w:
                sm = sm + pltpu.roll(sm, n - step, 0)
                step *= 2
            dub_ref[:, cols] = (sm[0:TS, :] - dy[:, cols]).astype(BF16)
        ybuf[TS:n, :] = ybuf[0:HALO, :]

        for a in range(sub):
            dqkv_ref[a * TQ:(a + 1) * TQ, 0:D_ATTN] = (dqt_ref[a].T * 0.125).astype(BF16)
        dqkv_ref[:, D_ATTN:2 * D_ATTN] = dk_ref[...].astype(BF16)
        dqkv_ref[:, 2 * D_ATTN:] = dv_ref[...].astype(BF16)
        dhn = _nn(dqkv_ref[...], wqkv_ref[...]) + _nn(dfb, wf_ref[...]) + _nn(dub_ref[...], wu_ref[...])
        dx, dg = _rms_bwd(x_ref[...], g_ref[...], dhn)
        gx_ref[...] = dh1_ref[...] + dx
        dg_ref[...] += dg

    rev = lambda i: (nt - 1 - i, 0)
    blk = lambda w: pl.BlockSpec((TS, w), rev)
    return pl.pallas_call(
        body, grid=(nt,), name="pre_attn_bwd",
        out_shape=(jax.ShapeDtypeStruct((s, d), F32), jax.ShapeDtypeStruct((s, 3 * D_ATTN), BF16),
                   jax.ShapeDtypeStruct((s, LANES), BF16), jax.ShapeDtypeStruct((s, D_POOL), BF16),
                   jax.ShapeDtypeStruct((1, d), F32), jax.ShapeDtypeStruct((1, LANES), F32)),
        in_specs=[pl.BlockSpec((sub, D_ATTN, TQ), lambda i: (nt - 1 - i, 0, 0)),
                  blk(D_ATTN), blk(D_ATTN), blk(LANES), blk(LANES), blk(LANES), blk(D_POOL), blk(d), blk(d),
                  pl.BlockSpec((1, d), _fixed), pl.BlockSpec(wqkv.shape, _fixed), pl.BlockSpec(wf.shape, _fixed),
                  pl.BlockSpec(wu.shape, _fixed)],
        out_specs=(blk(d), blk(3 * D_ATTN), blk(LANES), blk(D_POOL),
                   pl.BlockSpec((1, d), _fixed), pl.BlockSpec((1, LANES), _fixed)),
        scratch_shapes=[pltpu.VMEM((n, D_POOL), F32), pltpu.VMEM((1, LANES), F32), pltpu.VMEM((TS, LANES), F32)],
        compiler_params=_params(1),
    )(dqt3, dk, dv, dcs, drs, fl, dy, x, dh1, g1, wqkv, wf, wu)


def _wgrad(a, b, out_dtype, name):
    s, m = a.shape
    n = b.shape[1]
    tm = next(t for t in (512, 256, 128) if m % t == 0)
    ns = s // TS

    def body(a_ref, b_ref, o_ref, acc):
        i = pl.program_id(1)

        @pl.when(i == 0)
        def _():
            acc[...] = jnp.zeros_like(acc)

        acc[...] += _tn(a_ref[...], b_ref[pl.ds(i * TS, TS), :])

        @pl.when(i == ns - 1)
        def _():
            o_ref[...] = acc[...].astype(out_dtype)

    return pl.pallas_call(
        body, grid=(m // tm, ns), name=name, out_shape=jax.ShapeDtypeStruct((m, n), out_dtype),
        in_specs=[pl.BlockSpec((TS, tm), lambda j, i: (i, j)), VMEM_WHOLE],
        out_specs=pl.BlockSpec((tm, n), lambda j, i: (j, 0)),
        scratch_shapes=[pltpu.VMEM((tm, n), F32)],
        compiler_params=_params(2),
    )(a, b)


def _adamw(w, g, m, v):
    m = ADAM_B1 * m + (1.0 - ADAM_B1) * g
    v = ADAM_B2 * v + (1.0 - ADAM_B2) * (g * g)
    m_hat = m / (1.0 - ADAM_B1 ** ADAM_STEP)
    v_hat = v / (1.0 - ADAM_B2 ** ADAM_STEP)
    delta = -ADAM_LR * (m_hat / (jnp.sqrt(v_hat) + ADAM_EPS) + ADAM_WD * w)
    return delta, m, v


def _pair_sum(core, t, theirs):
    nk, r, c = theirs.shape

    def body(core_ref, a_ref, b_ref, o_ref):
        o_ref[...] = (a_ref[...].astype(F32) + b_ref[...].astype(F32)).astype(BF16)

    blk = pl.BlockSpec((1, TR_PACK, c), lambda k, i, core_ref: (k, i, 0))
    return pl.pallas_call(
        body, name="rs_pair_sum", out_shape=jax.ShapeDtypeStruct(theirs.shape, BF16),
        grid_spec=pltpu.PrefetchScalarGridSpec(
            num_scalar_prefetch=1, grid=(nk, r // TR_PACK),
            in_specs=[pl.BlockSpec((1, TR_PACK, c), lambda k, i, core_ref: (2 * k + core_ref[0], i, 0)), blk],
            out_specs=blk),
        compiler_params=_params(2),
    )(core, t, theirs)


def _reduce_update_big(parts, w, m, v):
    nk, r, c = parts.shape

    def body(p_ref, w_ref, m_ref, v_ref, g_ref, d_ref, nm_ref, nv_ref):
        g = p_ref[0].astype(F32)
        for k in range(1, nk):
            g = g + p_ref[k].astype(F32)
        g_ref[...] = g
        d_ref[...], nm_ref[...], nv_ref[...] = _adamw(w_ref[...], g, m_ref[...], v_ref[...])

    blk = pl.BlockSpec((TR_PACK, c), _row)
    out = jax.ShapeDtypeStruct((r, c), F32)
    return pl.pallas_call(
        body, grid=(r // TR_PACK,), name="reduce_update_big", out_shape=(out,) * 4,
        in_specs=[pl.BlockSpec((nk, TR_PACK, c), lambda i: (0, i, 0)), blk, blk, blk],
        out_specs=(blk,) * 4, compiler_params=_params(1),
    )(parts, w, m, v)


def _reduce_update_small(parts, w, m, v):
    nd = parts.shape[0]

    def body(p_ref, w_ref, m_ref, v_ref, g_ref, d_ref, nm_ref, nv_ref):
        g = p_ref[0]
        for k in range(1, nd):
            g = g + p_ref[k]
        g_ref[...] = g
        d_ref[...], nm_ref[...], nv_ref[...] = _adamw(w_ref[...], g, m_ref[...], v_ref[...])

    out = jax.ShapeDtypeStruct(w.shape, F32)
    return pl.pallas_call(body, name="reduce_update_small", out_shape=(out,) * 4,
                          compiler_params=pltpu.CompilerParams(vmem_limit_bytes=VMEM_LIMIT))(parts, w, m, v)


MESH = pl.DeviceIdType.MESH


def _copy_through_vmem(src_hbm, dst_hbm, stage, sem):
    load = pltpu.make_async_copy(src_hbm, stage, sem)
    load.start()
    load.wait()
    store = pltpu.make_async_copy(stage, dst_hbm, sem)
    store.start()
    store.wait()


def _all_gather(xs, name):
    r, cdim = xs.shape

    def body(x_ref, out_ref, stage, send_sems, recv_sems, local_sem):
        x, y, c = lax.axis_index("x"), lax.axis_index("y"), lax.axis_index("c")
        me, sibling = (x, y, c), (x, y, 1 - c)
        chips = [(1 - x, y), (x, 1 - y), (1 - x, 1 - y)]

        def slot(px, py, pc):
            return out_ref.at[4 * px + 2 * py + pc]

        def copy(k, block, to, src=None):
            return pltpu.make_async_remote_copy(
                src_ref=slot(*block) if src is None else src, dst_ref=slot(*block),
                send_sem=send_sems.at[k], recv_sem=recv_sems.at[k], device_id=to, device_id_type=MESH)

        first = [copy(0, me, sibling, src=x_ref)]
        first += [copy(1 + j, me, (*chip, c), src=x_ref) for j, chip in enumerate(chips)]
        for cp in first:
            cp.start()
        _copy_through_vmem(x_ref, slot(*me), stage, local_sem)
        passed = [copy(4 + j, (*chip, c), sibling) for j, chip in enumerate(chips)]
        for j, chip in enumerate(chips):
            copy(1 + j, (*chip, c), me).wait_recv()
            passed[j].start()
        copy(0, sibling, me).wait_recv()
        for j, chip in enumerate(chips):
            copy(4 + j, (*chip, 1 - c), me).wait_recv()
        for cp in first + passed:
            cp.wait_send()

    return pl.pallas_call(
        body, name=name, out_shape=jax.ShapeDtypeStruct((N_DEV, r, cdim), xs.dtype),
        in_specs=[ANY], out_specs=ANY,
        scratch_shapes=[pltpu.VMEM((r, cdim), xs.dtype), pltpu.SemaphoreType.DMA((7,)), pltpu.SemaphoreType.DMA((7,)),
                        pltpu.SemaphoreType.DMA],
        compiler_params=pltpu.CompilerParams(vmem_limit_bytes=VMEM_LIMIT),
    )(xs)


def _rs_pair(t):
    _, r, cdim = t.shape

    def body(t_ref, theirs_ref, send_sems, recv_sems):
        x, y, c = lax.axis_index("x"), lax.axis_index("y"), lax.axis_index("c")
        remote = [pltpu.make_async_remote_copy(
            src_ref=t_ref.at[2 * k + (1 - c)], dst_ref=theirs_ref.at[k],
            send_sem=send_sems.at[k], recv_sem=recv_sems.at[k], device_id=(x, y, 1 - c), device_id_type=MESH)
            for k in range(4)]
        for cp in remote:
            cp.start()
        for cp in remote:
            cp.wait()

    return pl.pallas_call(
        body, name="rs_pair", out_shape=jax.ShapeDtypeStruct((4, r, cdim), t.dtype), in_specs=[ANY], out_specs=ANY,
        scratch_shapes=[pltpu.SemaphoreType.DMA((4,)), pltpu.SemaphoreType.DMA((4,))],
    )(t)


def _rs_chips(b):
    _, r, cdim = b.shape

    def body(b_ref, out_ref, stage, send_sems, recv_sems, local_sem):
        x, y, c = lax.axis_index("x"), lax.axis_index("y"), lax.axis_index("c")
        mychip = 2 * x + y
        chips = [(1 - x, y), (x, 1 - y), (1 - x, 1 - y)]
        sends = [pltpu.make_async_remote_copy(
            src_ref=b_ref.at[2 * px + py], dst_ref=out_ref.at[mychip],
            send_sem=send_sems.at[j], recv_sem=recv_sems.at[j], device_id=(px, py, c), device_id_type=MESH)
            for j, (px, py) in enumerate(chips)]
        for cp in sends:
            cp.start()
        _copy_through_vmem(b_ref.at[mychip], out_ref.at[mychip], stage, local_sem)
        for j, (px, py) in enumerate(chips):
            pltpu.make_async_remote_copy(
                src_ref=b_ref.at[2 * px + py], dst_ref=out_ref.at[2 * px + py],
                send_sem=send_sems.at[j], recv_sem=recv_sems.at[j], device_id=(px, py, c), device_id_type=MESH).wait()

    return pl.pallas_call(
        body, name="rs_chips", out_shape=jax.ShapeDtypeStruct(b.shape, b.dtype), in_specs=[ANY], out_specs=ANY,
        scratch_shapes=[pltpu.VMEM((r, cdim), b.dtype), pltpu.SemaphoreType.DMA((3,)), pltpu.SemaphoreType.DMA((3,)),
                        pltpu.SemaphoreType.DMA],
        compiler_params=pltpu.CompilerParams(vmem_limit_bytes=VMEM_LIMIT),
    )(b)


def _pad_rows(a, rows):
    return jnp.pad(a, ((0, rows - a.shape[0]), (0, 0)))


def _pack_big(w_in, w_out, w_gate, w_up, w_down, w_ple, w_pg):
    parts = [_pad_rows(w_in[0].T, ROWS_IN), w_out[0], w_gate[0].T, w_up[0].T, w_down[0],
             w_ple[0].T.reshape(32, D_MODEL), w_pg[0]]
    return _pad_rows(jnp.concatenate(parts, axis=0), ROWS_PACK)


def _unpack_big(r):
    return (r[0:257].T[None], r[OFF_OUT:OFF_GATE][None], r[OFF_GATE:OFF_UP].T[None], r[OFF_UP:OFF_DOWN].T[None],
            r[OFF_DOWN:OFF_PLE][None], r[OFF_PLE:OFF_PG].reshape(128, D_PLE).T[None], r[OFF_PG:ROWS_USED][None])


def _pack_small(w_pool, g_mix_pre, g_mix_post, g_ffn_pre, g_ffn_post, g_ple, g_attn, g_pool, pool_scale, b_forget,
                loss=None):
    def row(vrow):
        return jnp.pad(vrow.reshape(1, -1), ((0, 0), (0, D_MODEL - vrow.size)))
    rows = [w_pool.reshape(64, D_MODEL), row(g_mix_pre), row(g_mix_post), row(g_ffn_pre), row(g_ffn_post), row(g_ple),
            row(g_attn), row(g_pool), row(pool_scale), row(b_forget),
            row(loss) if loss is not None else jnp.zeros((1, D_MODEL), F32)]
    return _pad_rows(jnp.concatenate(rows, axis=0), SMALL_ROWS)


def _unpack_small(r):
    return dict(
        w_pool=r[0:64].reshape(1, 4, POOL_CH, POOL_CH), g_mix_pre=r[ROW_G_MIX_PRE:ROW_G_MIX_PRE + 1],
        g_mix_post=r[ROW_G_MIX_POST:ROW_G_MIX_POST + 1], g_ffn_pre=r[ROW_G_FFN_PRE:ROW_G_FFN_PRE + 1],
        g_ffn_post=r[ROW_G_FFN_POST:ROW_G_FFN_POST + 1], g_ple=r[ROW_G_PLE:ROW_G_PLE + 1],
        g_attn_grp=r[ROW_G_ATTN:ROW_G_ATTN + 1, 0:D_ATTN], g_pool_grp=r[ROW_G_POOL:ROW_G_POOL + 1, 0:D_POOL],
        pool_scale=r[ROW_POOL_SCALE:ROW_POOL_SCALE + 1, 0:D_POOL], b_forget=r[ROW_B_FORGET:ROW_B_FORGET + 1, 0:HEADS])


def _local_step(x, p, tgt, small, win_t, wout, wg_t, wu_t, wd, wple_t, wpg):
    wqkv = win_t[0:3 * D_ATTN]
    wf = _pad_rows(win_t[3 * D_ATTN:3 * D_ATTN + HEADS], LANES)
    wu = win_t[3 * D_ATTN + HEADS:]
    wpool = small["w_pool"].astype(BF16)
    bpad = jnp.pad(small["b_forget"], ((0, 0), (0, LANES - HEADS)))

    lay = _attn_layout_constants()
    hn, q, ka, v, qat3, vt3, kt3, fl, y, mpre = _pre_attn_fwd(x, small["g_mix_pre"], wqkv, wf, wu, bpad, wpool, lay)
    a, lset3 = _attn_fwd(ka, qat3, vt3)
    mix, o, h1, hn2 = _post_attn_fwd(a, mpre, x, small["g_attn_grp"], small["g_pool_grp"], small["pool_scale"], wout,
                                     small["g_mix_post"], small["g_ffn_pre"])
    gate, up, act, ff, h2 = _ffn_fwd(hn2, wg_t, wu_t, wd, h1, small["g_ffn_post"])
    dh2, dff, dgl, dpp, h2b, pb, loss8, dg_ple, dg_ffn_post = _tail_fwd_bwd(
        h2, p, tgt, ff, wple_t, wpg, small["g_ple"], small["g_ffn_post"])
    dgate, dup, dh1, dg_ffn_pre = _ffn_bwd(dff, gate, up, wd, wg_t, wu_t, h1, dh2, small["g_ffn_pre"])
    dob, dab, dat3, dlt3, dmpb, dy, dg_mix_post, dg_attn, dg_pool, dps = _post_attn_bwd(
        dh1, o, a, mpre, wout, wpool, small["g_mix_post"], small["g_attn_grp"], small["g_pool_grp"], small["pool_scale"],
        lay["eye"])
    dqt3, dk, dv, dcs, drs3 = _attn_bwd(ka, v, kt3, qat3, q, dab, dat3, lset3, dlt3)
    drs = jnp.pad(drs3.transpose(0, 2, 1).reshape(-1, HEADS), ((0, 0), (0, LANES - HEADS)))
    gx, dqkv, dfb, dub, dg_mix_pre, db = _pre_attn_bwd(dqt3, dk, dv, dcs, drs, fl, dy, x, dh1, small["g_mix_pre"], wqkv, wf, wu)

    dwin_t = jnp.concatenate([_wgrad(dqkv, hn, F32, "wgrad_qkv"), _wgrad(dfb, hn, F32, "wgrad_forget")[0:HEADS],
                              _wgrad(dub, hn, F32, "wgrad_pool_in")], axis=0)
    grads = dict(
        win_t=dwin_t, wout=_wgrad(mix, dob, BF16, "wgrad_out"), wg_t=_wgrad(dgate, hn2, BF16, "wgrad_gate"),
        wu_t=_wgrad(dup, hn2, BF16, "wgrad_up"), wd=_wgrad(act, dff, BF16, "wgrad_down"),
        wple_t=_wgrad(dpp, pb, BF16, "wgrad_ple"), wpg=_wgrad(h2b, dgl, BF16, "wgrad_ple_gate"))
    dwp = _wgrad(y, dmpb, F32, "wgrad_pool")
    dw_pool = jnp.stack([dwp[g * POOL_CH:(g + 1) * POOL_CH, g * POOL_CH:(g + 1) * POOL_CH] for g in range(4)])
    small_part = _pack_small(dw_pool, dg_mix_pre, dg_mix_post, dg_ffn_pre, dg_ffn_post, dg_ple, dg_attn, dg_pool, dps,
                             db[:, 0:HEADS], loss8[0:1, 0:1])
    return gx, small_part, grads


def kernel(x, p, g_mix_pre, w_in, b_forget, g_attn_grp, g_pool_grp, w_pool, pool_scale, w_out, g_mix_post, g_ffn_pre, w_ffn_gate, w_ffn_up, w_ffn_down, g_ffn_post, w_ple_proj, g_ple, w_ple_gate, loss_target, m_g_mix_pre, m_w_in, m_b_forget, m_g_attn_grp, m_g_pool_grp, m_w_pool, m_pool_scale, m_w_out, m_g_mix_post, m_g_ffn_pre, m_w_ffn_gate, m_w_ffn_up, m_w_ffn_down, m_g_ffn_post, m_w_ple_proj, m_g_ple, m_w_ple_gate, v_g_mix_pre, v_w_in, v_b_forget, v_g_attn_grp, v_g_pool_grp, v_w_pool, v_pool_scale, v_w_out, v_g_mix_post, v_g_ffn_pre, v_w_ffn_gate, v_w_ffn_up, v_w_ffn_down, v_g_ffn_post, v_w_ple_proj, v_g_ple, v_w_ple_gate):
    big_w = _pack_big(w_in, w_out, w_ffn_gate, w_ffn_up, w_ffn_down, w_ple_proj, w_ple_gate)
    big_m = _pack_big(m_w_in, m_w_out, m_w_ffn_gate, m_w_ffn_up, m_w_ffn_down, m_w_ple_proj, m_w_ple_gate)
    big_v = _pack_big(v_w_in, v_w_out, v_w_ffn_gate, v_w_ffn_up, v_w_ffn_down, v_w_ple_proj, v_w_ple_gate)

    gathered = _all_gather(big_w.astype(BF16), "gather_weights")
    win_t = gathered[:, 0:257].reshape(D_IN, D_MODEL)
    wout = gathered[:, OFF_OUT:OFF_GATE].reshape(D_MODEL, D_MODEL)
    wg_t = gathered[:, OFF_GATE:OFF_UP].reshape(D_FF, D_MODEL)
    wu_t = gathered[:, OFF_UP:OFF_DOWN].reshape(D_FF, D_MODEL)
    wd = gathered[:, OFF_DOWN:OFF_PLE].reshape(D_FF, D_MODEL)
    wple_t = gathered[:, OFF_PLE:OFF_PG].reshape(D_MODEL, D_PLE)
    wpg = gathered[:, OFF_PG:ROWS_USED].reshape(D_MODEL, D_MODEL)

    small = dict(w_pool=w_pool[0], g_mix_pre=g_mix_pre, g_mix_post=g_mix_post, g_ffn_pre=g_ffn_pre,
                 g_ffn_post=g_ffn_post, g_ple=g_ple, g_attn_grp=g_attn_grp, g_pool_grp=g_pool_grp,
                 pool_scale=pool_scale, b_forget=b_forget)
    gx, small_part, grads = _local_step(x[0], p[0, 0], loss_target[0], small, win_t, wout, wg_t, wu_t, wd, wple_t, wpg)

    nd = N_DEV
    send = jnp.concatenate([
        jnp.pad(grads["win_t"].reshape(nd, 257, D_MODEL), ((0, 0), (0, ROWS_IN - 257), (0, 0))).astype(BF16),
        grads["wout"].reshape(nd, 128, D_MODEL), grads["wg_t"].reshape(nd, 352, D_MODEL),
        grads["wu_t"].reshape(nd, 352, D_MODEL), grads["wd"].reshape(nd, 352, D_MODEL),
        grads["wple_t"].reshape(nd, 32, D_MODEL), grads["wpg"].reshape(nd, 128, D_MODEL),
        jnp.zeros((nd, ROWS_PACK - ROWS_USED, D_MODEL), BF16)], axis=1)
    core = lax.axis_index("c").astype(jnp.int32).reshape(1)
    by_chip = _rs_chips(_pair_sum(core, send, _rs_pair(send)))
    g_big, d_big, nm_big, nv_big = _reduce_update_big(by_chip, big_w, big_m, big_v)

    small_all = _all_gather(small_part, "gather_small")
    sm_w =_pack_small(w_pool, g_mix_pre, g_mix_post, g_ffn_pre, g_ffn_post, g_ple, g_attn_grp, g_pool_grp, pool_scale, b_forget)
    sm_m = _pack_small(m_w_pool, m_g_mix_pre, m_g_mix_post, m_g_ffn_pre, m_g_ffn_post, m_g_ple, m_g_attn_grp, m_g_pool_grp, m_pool_scale, m_b_forget)
    sm_v = _pack_small(v_w_pool, v_g_mix_pre, v_g_mix_post, v_g_ffn_pre, v_g_ffn_post, v_g_ple, v_g_attn_grp, v_g_pool_grp, v_pool_scale, v_b_forget)
    g_sm, d_sm, nm_sm, nv_sm = _reduce_update_small(small_all, sm_w, sm_m, sm_v)
    loss = g_sm[ROW_LOSS, 0]

    def leaves(big, sm):
        b_in, b_out, b_gate, b_up, b_down, b_ple, b_pg = _unpack_big(big)
        s = _unpack_small(sm)
        return (s["g_mix_pre"], b_in, s["b_forget"], s["g_attn_grp"], s["g_pool_grp"], s["w_pool"], s["pool_scale"], b_out,
                s["g_mix_post"], s["g_ffn_pre"], b_gate, b_up, b_down, s["g_ffn_post"], b_ple, s["g_ple"], b_pg)

    return (loss, gx[None], *leaves(g_big, g_sm), *leaves(d_big, d_sm), *leaves(nm_big, nm_sm), *leaves(nv_big, nv_sm))
```

```python
import functools

import jax
import jax.numpy as jnp
from jax import lax
from jax.experimental import pallas as pl
from jax.experimental.pallas import tpu as pltpu

F32 = jnp.float32
BF16 = jnp.bfloat16
HIGHEST = lax.Precision.HIGHEST

D_MODEL = 1024
HEADS = 8
HEAD_DIM = 64
D_ATTN = HEADS * HEAD_DIM
POOL_WINDOWS = (2, 4, 8, 16)
POOL_CH = 128
D_POOL = POOL_CH * len(POOL_WINDOWS)
D_FF = 2816
D_PLE = 256
D_IN = 3 * D_ATTN + HEADS + D_POOL
RMS_EPS = 1e-6
N_DEV = 8

ADAM_LR = 0.001
ADAM_B1 = 0.9
ADAM_B2 = 0.999
ADAM_EPS = 1e-08
ADAM_WD = 0.01
ADAM_STEP = 10

LANES = 128
HALO = 16
TS = 512
TQ = 256
TN_FF = 256
NEG = -1e30
VMEM_LIMIT = 56 * 1024 * 1024

SHARD_IN = 257
ROWS_IN = 272
OFF_GATE = 128
OFF_UP = OFF_GATE + 352
OFF_DOWN = OFF_UP + 352
OFF_PLE = OFF_DOWN + 352
OFF_PG = OFF_PLE + 32
ROWS_REST = OFF_PG + 128
TR_REST = 192

SMALL_ROWS = 80
ROW_G_MIX_PRE, ROW_G_MIX_POST, ROW_G_FFN_PRE, ROW_G_FFN_POST, ROW_G_PLE = 64, 65, 66, 67, 68
ROW_G_ATTN, ROW_G_POOL, ROW_POOL_SCALE, ROW_B_FORGET, ROW_LOSS = 69, 70, 71, 72, 73


def _nn(a, b):
    return jnp.dot(a, b, preferred_element_type=F32)


def _nt(a, b):
    return lax.dot_general(a, b, (((1,), (1,)), ((), ())), preferred_element_type=F32)


def _tn(a, b):
    return lax.dot_general(a, b, (((0,), (0,)), ((), ())), preferred_element_type=F32)


def _rstd(v):
    return lax.rsqrt(jnp.mean(v * v, axis=-1, keepdims=True) + RMS_EPS)


def _rms_bwd(v, g, dy):
    r = _rstd(v)
    vh = v * r
    t = dy * g
    dv = r * (t - vh * jnp.mean(t * vh, axis=-1, keepdims=True))
    return dv, jnp.sum(dy * vh, axis=0, keepdims=True)


def _params(n_grid):
    return pltpu.CompilerParams(dimension_semantics=("arbitrary",) * n_grid, vmem_limit_bytes=VMEM_LIMIT)


def _row(i):
    return (i, 0)


def _fixed(*_):
    return (0, 0)


VMEM_WHOLE = pl.BlockSpec(memory_space=pltpu.VMEM)
SMEM_WHOLE = pl.BlockSpec(memory_space=pltpu.SMEM)
ANY = pl.BlockSpec(memory_space=pl.ANY)


AUG = 128
BIAS_LANE = HEAD_DIM
ONE_LANE = HEAD_DIM + 3
SPARE_LANE = HEADS


def _attn_layout_constants():
    import numpy as np
    place = np.zeros((D_ATTN, HEADS * AUG), np.float32)
    for r in range(D_ATTN):
        place[r, (r // HEAD_DIM) * AUG + r % HEAD_DIM] = 1.0
    bias_k = np.zeros((3, LANES, HEADS * AUG), np.float32)
    bias_q = np.zeros((3, LANES, HEADS * AUG), np.float32)
    for h in range(HEADS):
        for part in range(3):
            bias_k[part, h, h * AUG + BIAS_LANE + part] = -1.0
            bias_q[part, h, h * AUG + ONE_LANE + part] = 1.0
            bias_k[0, SPARE_LANE, h * AUG + ONE_LANE + part] = 1.0
            bias_q[0, SPARE_LANE, h * AUG + BIAS_LANE + part] = 1.0
    as_bf = lambda a: jnp.asarray(a, BF16)
    return dict(place=as_bf(place), place_t=as_bf(place.T), bias_k=as_bf(bias_k),
                bias_q_t=as_bf(bias_q.transpose(0, 2, 1)), eye=as_bf(np.eye(D_ATTN, dtype=np.float32)))


def _pre_attn_fwd(x, g1, wqkv, wf, wu, bpad, wpool, lay):
    s, d = x.shape
    nt = s // TS
    sub = TS // TQ

    def body(x_ref, g_ref, wqkv_ref, wf_ref, wu_ref, b_ref, wp_ref, place_ref, place_t_ref, bk_ref, bqt_ref, eye_ref,
             hn_ref, q_ref, ka_ref, v_ref, qat_ref, vt_ref, kt_ref, fl_ref, y_ref, mp_ref, ubuf, ccar, cbuf):
        i = pl.program_id(0)

        @pl.when(i == 0)
        def _():
            ubuf[0:HALO, :] = jnp.zeros((HALO, D_POOL), F32)
            ccar[...] = jnp.zeros_like(ccar)

        xv = x_ref[...]
        hn = (xv * _rstd(xv) * g_ref[...]).astype(BF16)
        hn_ref[...] = hn
        zq = _nt(hn, wqkv_ref[...])
        qb = (zq[:, 0:D_ATTN] * 0.125).astype(BF16)
        kb = zq[:, D_ATTN:2 * D_ATTN].astype(BF16)
        vb = zq[:, 2 * D_ATTN:3 * D_ATTN].astype(BF16)
        q_ref[...] = qb
        v_ref[...] = vb

        fl = _nt(hn, wf_ref[...]) + b_ref[...]
        fl_ref[...] = fl
        logf = jax.nn.log_sigmoid(fl)
        rr = lax.broadcasted_iota(jnp.int32, (TS, TS), 0)
        cc = lax.broadcasted_iota(jnp.int32, (TS, TS), 1)
        tril = (cc <= rr).astype(F32)
        c = jnp.dot(tril, logf, precision=HIGHEST, preferred_element_type=F32) + ccar[...]
        cbuf[...] = c
        ccar[...] = cbuf[TS - 1:TS, :]
        hi = c.astype(BF16)
        rest = c - hi.astype(F32)
        mid = rest.astype(BF16)
        lo = (rest - mid.astype(F32)).astype(BF16)
        lane = lax.broadcasted_iota(jnp.int32, (TS, LANES), 1)
        parts = (jnp.where(lane == SPARE_LANE, 1.0, hi).astype(BF16), mid, lo)
        ka = _nn(kb, place_ref[...])
        qat = _nt(place_t_ref[...], qb)
        for part in range(3):
            ka = ka + _nn(parts[part], bk_ref[part])
            qat = qat + _nt(bqt_ref[part], parts[part])
        ka_ref[...] = ka.astype(BF16)
        qat = qat.astype(BF16)
        vt = _nt(eye_ref[...], vb).astype(BF16)
        kt = _nt(eye_ref[...], kb).astype(BF16)
        for a in range(sub):
            qat_ref[a] = qat[:, a * TQ:(a + 1) * TQ]
            vt_ref[a] = vt[:, a * TQ:(a + 1) * TQ]
            kt_ref[a] = kt[:, a * TQ:(a + 1) * TQ]

        u = _nt(hn, wu_ref[...])
        ubuf[HALO:HALO + TS, :] = u
        t = i * TS + lax.broadcasted_iota(jnp.int32, (TS, 1), 0)
        for g, w in enumerate(POOL_WINDOWS):
            cols = slice(g * POOL_CH, (g + 1) * POOL_CH)
            sm = ubuf[:, cols]
            step = 1
            while step < w:
                sm = sm + pltpu.roll(sm, step, 0)
                step *= 2
            cnt = jnp.minimum(t + 1, w).astype(F32)
            yg = (sm[HALO:, :] / cnt - u[:, cols]).astype(BF16)
            y_ref[:, cols] = yg
            mp_ref[:, cols] = _nn(yg, wp_ref[g])
        ubuf[0:HALO, :] = u[TS - HALO:, :]

    nq = s // TQ
    aug = HEADS * AUG
    outs = (
        jax.ShapeDtypeStruct((s, d), BF16), jax.ShapeDtypeStruct((s, D_ATTN), BF16),
        jax.ShapeDtypeStruct((s, aug), BF16), jax.ShapeDtypeStruct((s, D_ATTN), BF16),
        jax.ShapeDtypeStruct((nq, aug, TQ), BF16), jax.ShapeDtypeStruct((nq, D_ATTN, TQ), BF16),
        jax.ShapeDtypeStruct((nq, D_ATTN, TQ), BF16),
        jax.ShapeDtypeStruct((s, LANES), F32),
        jax.ShapeDtypeStruct((s, D_POOL), BF16), jax.ShapeDtypeStruct((s, D_POOL), F32),
    )
    fixed3 = lambda i: (0, 0, 0)
    tiles3 = lambda rows: pl.BlockSpec((sub, rows, TQ), lambda i: (i, 0, 0))
    return pl.pallas_call(
        body, grid=(nt,), out_shape=outs, name="pre_attn_fwd",
        in_specs=[pl.BlockSpec((TS, d), _row), pl.BlockSpec((1, d), _fixed),
                  pl.BlockSpec(wqkv.shape, _fixed), pl.BlockSpec(wf.shape, _fixed), pl.BlockSpec(wu.shape, _fixed),
                  pl.BlockSpec((1, LANES), _fixed), pl.BlockSpec(wpool.shape, fixed3),
                  pl.BlockSpec(lay["place"].shape, _fixed), pl.BlockSpec(lay["place_t"].shape, _fixed),
                  pl.BlockSpec(lay["bias_k"].shape, fixed3), pl.BlockSpec(lay["bias_q_t"].shape, fixed3),
                  pl.BlockSpec(lay["eye"].shape, _fixed)],
        out_specs=(pl.BlockSpec((TS, d), _row), pl.BlockSpec((TS, D_ATTN), _row),
                   pl.BlockSpec((TS, aug), _row), pl.BlockSpec((TS, D_ATTN), _row),
                   tiles3(aug), tiles3(D_ATTN), tiles3(D_ATTN),
                   pl.BlockSpec((TS, LANES), _row),
                   pl.BlockSpec((TS, D_POOL), _row), pl.BlockSpec((TS, D_POOL), _row)),
        scratch_shapes=[pltpu.VMEM((TS + HALO, D_POOL), F32), pltpu.VMEM((1, LANES), F32), pltpu.VMEM((TS, LANES), F32)],
        compiler_params=_params(1),
    )(x, g1, wqkv, wf, wu, bpad, wpool, lay["place"], lay["place_t"], lay["bias_k"], lay["bias_q_t"], lay["eye"])


def _causal_in_tile():
    krow = lax.broadcasted_iota(jnp.int32, (TQ, TQ), 0)
    qcol = lax.broadcasted_iota(jnp.int32, (TQ, TQ), 1)
    return krow <= qcol


def _attn_fwd(ka, qat3, vt3, own_block):
    s = ka.shape[0]
    nq = s // TQ
    pass_on_step = (2 * nq) // 3

    def body(qa_ref, ka_ref, vt_ref, own_ref, a_ref, lset_ref, all_ref, acc, stage, send_sems, recv_sems, local_sem):
        i = pl.program_id(0)

        @pl.when(i == 0)
        def _():
            _gather_start(own_ref, all_ref, stage, send_sems, recv_sems, local_sem)

        @pl.when(i == pass_on_step)
        def _():
            _gather_pass_on(all_ref, send_sems, recv_sems)

        acc[...] = jnp.zeros_like(acc)

        def tile(j, stats, masked):
            new = []
            for h in range(HEADS):
                rows = slice(h * HEAD_DIM, (h + 1) * HEAD_DIM)
                aug = slice(h * AUG, (h + 1) * AUG)
                m_old, l_old = stats[h]
                st = _nn(ka_ref[pl.ds(j * TQ, TQ), aug], qa_ref[0, aug, :])
                if masked:
                    st = jnp.where(_causal_in_tile(), st, NEG)
                m_new = jnp.maximum(m_old, jnp.max(st, axis=0, keepdims=True))
                al = jnp.exp(m_old - m_new)
                pt = jnp.exp(st - m_new)
                new.append((m_new, al * l_old + jnp.sum(pt, axis=0, keepdims=True)))
                acc[rows, :] = al * acc[rows, :] + _nn(vt_ref[j, rows, :], pt.astype(BF16))
            return tuple(new)

        init = tuple((jnp.full((1, TQ), NEG, F32), jnp.zeros((1, TQ), F32)) for _ in range(HEADS))
        stats = lax.fori_loop(0, i, functools.partial(tile, masked=False), init)
        stats = tile(i, stats, True)
        for h in range(HEADS):
            rows = slice(h * HEAD_DIM, (h + 1) * HEAD_DIM)
            acc[rows, :] = acc[rows, :] / stats[h][1]
            lset_ref[0, h:h + 1, :] = stats[h][0] + jnp.log(stats[h][1])
        a_ref[...] = acc[...].T

        @pl.when(i == nq - 1)
        def _():
            _gather_finish(own_ref, all_ref, send_sems, recv_sems)

    r, cdim = own_block.shape
    return pl.pallas_call(
        body, grid=(nq,), name="attn_fwd",
        out_shape=(jax.ShapeDtypeStruct((s, D_ATTN), F32), jax.ShapeDtypeStruct((nq, HEADS, TQ), F32),
                   jax.ShapeDtypeStruct((N_DEV, r, cdim), own_block.dtype)),
        in_specs=[pl.BlockSpec((1, HEADS * AUG, TQ), lambda i: (i, 0, 0)), VMEM_WHOLE, VMEM_WHOLE, ANY],
        out_specs=(pl.BlockSpec((TQ, D_ATTN), _row), pl.BlockSpec((1, HEADS, TQ), lambda i: (i, 0, 0)), ANY),
        scratch_shapes=[pltpu.VMEM((D_ATTN, TQ), F32), pltpu.VMEM((r, cdim), own_block.dtype),
                        pltpu.SemaphoreType.DMA((7,)), pltpu.SemaphoreType.DMA((7,)), pltpu.SemaphoreType.DMA],
        compiler_params=_params(1),
    )(qat3, ka, vt3, own_block)


def _post_attn_fwd(a, mpre, x, g_attn, g_pool, pscale, wout, g_post, g_ffn_pre):
    s, d = x.shape

    def body(a_ref, mp_ref, x_ref, ga_ref, gp_ref, ps_ref, wo_ref, gpost_ref, gpre_ref,
             mix_ref, o_ref, h1_ref, hn2_ref):
        av = a_ref[...]
        mix_ref[:, 0:D_ATTN] = (av * _rstd(av) * ga_ref[...]).astype(BF16)
        mv = mp_ref[...] * ps_ref[...]
        mix_ref[:, D_ATTN:] = (mv * _rstd(mv) * gp_ref[...]).astype(BF16)
        o = _nn(mix_ref[...], wo_ref[...])
        o_ref[...] = o
        h1 = x_ref[...] + o * _rstd(o) * gpost_ref[...]
        h1_ref[...] = h1
        hn2_ref[...] = (h1 * _rstd(h1) * gpre_ref[...]).astype(BF16)

    vec = lambda n: pl.BlockSpec((1, n), _fixed)
    return pl.pallas_call(
        body, grid=(s // TS,), name="post_attn_fwd",
        out_shape=(jax.ShapeDtypeStruct((s, d), BF16), jax.ShapeDtypeStruct((s, d), F32),
                   jax.ShapeDtypeStruct((s, d), F32), jax.ShapeDtypeStruct((s, d), BF16)),
        in_specs=[pl.BlockSpec((TS, D_ATTN), _row), pl.BlockSpec((TS, D_POOL), _row), pl.BlockSpec((TS, d), _row),
                  vec(D_ATTN), vec(D_POOL), vec(D_POOL), pl.BlockSpec(wout.shape, _fixed), vec(d), vec(d)],
        out_specs=(pl.BlockSpec((TS, d), _row),) * 4,
        compiler_params=_params(1),
    )(a, mpre, x, g_attn, g_pool, pscale, wout, g_post, g_ffn_pre)


def _ffn_fwd(hn2, wg, wu, wd, h1, g_post):
    s, d = h1.shape
    nc = D_FF // TN_FF

    def body(hn_ref, wg_ref, wu_ref, wd_ref, h1_ref, g_ref, gate_ref, up_ref, act_ref, ff_ref, h2_ref, acc):
        j = pl.program_id(1)

        @pl.when(j == 0)
        def _():
            acc[...] = jnp.zeros_like(acc)

        hn = hn_ref[...]
        gt = _nt(hn, wg_ref[...])
        up = _nt(hn, wu_ref[...])
        act = (gt * jax.nn.sigmoid(gt) * up).astype(BF16)
        gate_ref[...] = gt.astype(BF16)
        up_ref[...] = up.astype(BF16)
        act_ref[...] = act
        acc[...] += _nn(act, wd_ref[...])

        @pl.when(j == nc - 1)
        def _():
            ff = acc[...]
            ff_ref[...] = ff
            h2_ref[...] = h1_ref[...] + ff * _rstd(ff) * g_ref[...]

    rowblk = pl.BlockSpec((TS, d), lambda i, j: (i, 0))
    wblk = pl.BlockSpec((TN_FF, d), lambda i, j: (j, 0))
    chunk = pl.BlockSpec((TS, TN_FF), lambda i, j: (i, j))
    return pl.pallas_call(
        body, grid=(s // TS, nc), name="ffn_fwd",
        out_shape=(jax.ShapeDtypeStruct((s, D_FF), BF16),) * 3 + (jax.ShapeDtypeStruct((s, d), F32),) * 2,
        in_specs=[rowblk, wblk, wblk, wblk, rowblk, pl.BlockSpec((1, d), lambda i, j: (0, 0))],
        out_specs=(chunk, chunk, chunk, rowblk, rowblk),
        scratch_shapes=[pltpu.VMEM((TS, d), F32)],
        compiler_params=_params(2),
    )(hn2, wg, wu, wd, h1, g_post)


def _tail_fwd_bwd(h2, p, tgt, ff, wple, wpg, g_ple, g_ffn_post):
    s, d = h2.shape

    def body(h2_ref, p_ref, t_ref, ff_ref, wple_ref, wpg_ref, gple_ref, gfp_ref,
             dh2_ref, dff_ref, dgl_ref, dpp_ref, h2b_ref, pb_ref, loss_ref, dgple_ref, dgfp_ref):
        i = pl.program_id(0)

        @pl.when(i == 0)
        def _():
            loss_ref[...] = jnp.zeros_like(loss_ref)
            dgple_ref[...] = jnp.zeros_like(dgple_ref)
            dgfp_ref[...] = jnp.zeros_like(dgfp_ref)

        h2 = h2_ref[...]
        h2b = h2.astype(BF16)
        h2b_ref[...] = h2b
        pb = p_ref[...].astype(BF16)
        pb_ref[...] = pb
        pp = _nt(pb, wple_ref[...])
        gple = gple_ref[...]
        e = pp * _rstd(pp) * gple
        sg = jax.nn.sigmoid(_nn(h2b, wpg_ref[...]))
        diff = h2 + sg * e - t_ref[...]
        sq = jnp.sum(jnp.sum(diff * diff, axis=1, keepdims=True), axis=0, keepdims=True)
        loss_ref[...] += jnp.broadcast_to(sq * (0.5 / d), loss_ref.shape)
        dh3 = diff * (1.0 / d)
        dgl = (dh3 * e * sg * (1.0 - sg)).astype(BF16)
        dgl_ref[...] = dgl
        dh2 = dh3 + _nt(dgl, wpg_ref[...])
        dh2_ref[...] = dh2
        dpp, dg = _rms_bwd(pp, gple, dh3 * sg)
        dpp_ref[...] = dpp.astype(BF16)
        dgple_ref[...] += dg
        dff, dg = _rms_bwd(ff_ref[...], gfp_ref[...], dh2)
        dff_ref[...] = dff.astype(BF16)
        dgfp_ref[...] += dg

    rowblk = pl.BlockSpec((TS, d), _row)
    vec = pl.BlockSpec((1, d), _fixed)
    return pl.pallas_call(
        body, grid=(s // TS,), name="tail_fwd_bwd",
        out_shape=(jax.ShapeDtypeStruct((s, d), F32), jax.ShapeDtypeStruct((s, d), BF16),
                   jax.ShapeDtypeStruct((s, d), BF16), jax.ShapeDtypeStruct((s, d), BF16),
                   jax.ShapeDtypeStruct((s, d), BF16), jax.ShapeDtypeStruct((s, D_PLE), BF16),
                   jax.ShapeDtypeStruct((8, LANES), F32), jax.ShapeDtypeStruct((1, d), F32),
                   jax.ShapeDtypeStruct((1, d), F32)),
        in_specs=[rowblk, pl.BlockSpec((TS, D_PLE), _row), rowblk, rowblk,
                  pl.BlockSpec(wple.shape, _fixed), pl.BlockSpec(wpg.shape, _fixed), vec, vec],
        out_specs=(rowblk, rowblk, rowblk, rowblk, rowblk, pl.BlockSpec((TS, D_PLE), _row),
                   pl.BlockSpec((8, LANES), _fixed), vec, vec),
        compiler_params=_params(1),
    )(h2, p, tgt, ff, wple, wpg, g_ple, g_ffn_post)


def _ffn_bwd(dff, gate, up, wd, wg, wu, h1, dh2, g_pre):
    s, d = h1.shape
    nc = D_FF // TN_FF

    def body(dff_ref, gate_ref, up_ref, wd_ref, wg_ref, wu_ref, h1_ref, dh2_ref, g_ref,
             dgate_ref, dup_ref, dh1_ref, dg_ref, acc):
        i = pl.program_id(0)
        j = pl.program_id(1)

        @pl.when((i == 0) & (j == 0))
        def _():
            dg_ref[...] = jnp.zeros_like(dg_ref)

        @pl.when(j == 0)
        def _():
            acc[...] = jnp.zeros_like(acc)

        dact = _nt(dff_ref[...], wd_ref[...])
        gt = gate_ref[...].astype(F32)
        sg = jax.nn.sigmoid(gt)
        dup = (dact * gt * sg).astype(BF16)
        dgate = (dact * up_ref[...].astype(F32) * (sg * (1.0 + gt * (1.0 - sg)))).astype(BF16)
        dgate_ref[...] = dgate
        dup_ref[...] = dup
        acc[...] += _nn(dgate, wg_ref[...]) + _nn(dup, wu_ref[...])

        @pl.when(j == nc - 1)
        def _():
            dv, dg = _rms_bwd(h1_ref[...], g_ref[...], acc[...])
            dh1_ref[...] = dh2_ref[...] + dv
            dg_ref[...] += dg

    rowblk = pl.BlockSpec((TS, d), lambda i, j: (i, 0))
    wblk = pl.BlockSpec((TN_FF, d), lambda i, j: (j, 0))
    chunk = pl.BlockSpec((TS, TN_FF), lambda i, j: (i, j))
    vec = pl.BlockSpec((1, d), lambda i, j: (0, 0))
    return pl.pallas_call(
        body, grid=(s // TS, nc), name="ffn_bwd",
        out_shape=(jax.ShapeDtypeStruct((s, D_FF), BF16), jax.ShapeDtypeStruct((s, D_FF), BF16),
                   jax.ShapeDtypeStruct((s, d), F32), jax.ShapeDtypeStruct((1, d), F32)),
        in_specs=[rowblk, chunk, chunk, wblk, wblk, wblk, rowblk, rowblk, vec],
        out_specs=(chunk, chunk, rowblk, vec),
        scratch_shapes=[pltpu.VMEM((TS, d), F32)],
        compiler_params=_params(2),
    )(dff, gate, up, wd, wg, wu, h1, dh2, g_pre)


def _post_attn_bwd(dh1, o, a, mpre, wout, wpool, g_post, g_attn, g_pool, pscale, eye):
    s, d = dh1.shape
    sub = TS // TQ

    def body(dh1_ref, o_ref, a_ref, mp_ref, wo_ref, wp_ref, gpost_ref, ga_ref, gp_ref, ps_ref, eye_ref,
             dob_ref, dab_ref, dat_ref, dlt_ref, dmpb_ref, dy_ref, dgpost_ref, dga_ref, dgp_ref, dps_ref):
        i = pl.program_id(0)

        @pl.when(i == 0)
        def _():
            dgpost_ref[...] = jnp.zeros_like(dgpost_ref)
            dga_ref[...] = jnp.zeros_like(dga_ref)
            dgp_ref[...] = jnp.zeros_like(dgp_ref)
            dps_ref[...] = jnp.zeros_like(dps_ref)

        do, dg = _rms_bwd(o_ref[...], gpost_ref[...], dh1_ref[...])
        dgpost_ref[...] += dg
        dob = do.astype(BF16)
        dob_ref[...] = dob
        dmix = _nt(dob, wo_ref[...])

        av = a_ref[...]
        da, dg = _rms_bwd(av, ga_ref[...], dmix[:, 0:D_ATTN])
        dga_ref[...] += dg
        dab = da.astype(BF16)
        dab_ref[...] = dab
        dat = _nt(eye_ref[...], dab).astype(BF16)
        hsel = (lax.shift_right_logical(lax.broadcasted_iota(jnp.int32, (HEADS, D_ATTN), 1), 6)
                == lax.broadcasted_iota(jnp.int32, (HEADS, D_ATTN), 0)).astype(F32)
        dlt = lax.dot_general(hsel, da * av, (((1,), (1,)), ((), ())), precision=HIGHEST, preferred_element_type=F32)
        for q in range(sub):
            dlt_ref[q] = dlt[:, q * TQ:(q + 1) * TQ]
            dat_ref[q] = dat[:, q * TQ:(q + 1) * TQ]

        ps = ps_ref[...]
        mp = mp_ref[...]
        dm, dg = _rms_bwd(mp * ps, gp_ref[...], dmix[:, D_ATTN:])
        dgp_ref[...] += dg
        dps_ref[...] += jnp.sum(dm * mp, axis=0, keepdims=True)
        dmpb = (dm * ps).astype(BF16)
        dmpb_ref[...] = dmpb
        for g in range(len(POOL_WINDOWS)):
            cols = slice(g * POOL_CH, (g + 1) * POOL_CH)
            dy_ref[:, cols] = _nt(dmpb[:, cols], wp_ref[g])

    rowblk = pl.BlockSpec((TS, d), _row)
    half = pl.BlockSpec((TS, D_ATTN), _row)
    vec = lambda n: pl.BlockSpec((1, n), _fixed)
    return pl.pallas_call(
        body, grid=(s // TS,), name="post_attn_bwd",
        out_shape=(jax.ShapeDtypeStruct((s, d), BF16), jax.ShapeDtypeStruct((s, D_ATTN), BF16),
                   jax.ShapeDtypeStruct((s // TQ, D_ATTN, TQ), BF16),
                   jax.ShapeDtypeStruct((s // TQ, HEADS, TQ), F32), jax.ShapeDtypeStruct((s, D_POOL), BF16),
                   jax.ShapeDtypeStruct((s, D_POOL), F32), jax.ShapeDtypeStruct((1, d), F32),
                   jax.ShapeDtypeStruct((1, D_ATTN), F32), jax.ShapeDtypeStruct((1, D_POOL), F32),
                   jax.ShapeDtypeStruct((1, D_POOL), F32)),
        in_specs=[rowblk, rowblk, half, half, pl.BlockSpec(wout.shape, _fixed),
                  pl.BlockSpec(wpool.shape, lambda i: (0, 0, 0)), vec(d), vec(D_ATTN), vec(D_POOL), vec(D_POOL),
                  pl.BlockSpec(eye.shape, _fixed)],
        out_specs=(rowblk, half, pl.BlockSpec((sub, D_ATTN, TQ), lambda i: (i, 0, 0)),
                   pl.BlockSpec((sub, HEADS, TQ), lambda i: (i, 0, 0)), half, half,
                   vec(d), vec(D_ATTN), vec(D_POOL), vec(D_POOL)),
        compiler_params=_params(1),
    )(dh1, o, a, mpre, wout, wpool, g_post, g_attn, g_pool, pscale, eye)


def _attn_bwd(ka, v, kt3, qat3, q, do, dot3, lset3, dlt3, chip_blocks):
    s = q.shape[0]
    nq = s // TQ
    wide = HEADS * LANES

    def body(ka_ref, v_ref, kt_ref, qat_ref, q_ref, do_ref, dot_ref, lset_ref, dlt_ref, b_ref,
             dqt_ref, dk_ref, dv_ref, dcs_ref, drs_ref, got_ref, dca, dkw, dvw, stage, send_sems, recv_sems, local_sem):
        j = pl.program_id(0)

        @pl.when(j == 0)
        def _():
            _chips_start(b_ref, got_ref, stage, send_sems, recv_sems, local_sem)
            dqt_ref[...] = jnp.zeros_like(dqt_ref)
            drs_ref[...] = jnp.zeros_like(drs_ref)

        dkw[...] = jnp.zeros_like(dkw)
        dvw[...] = jnp.zeros_like(dvw)
        dca[...] = jnp.zeros_like(dca)

        def tile(i, masked):
            for h in range(HEADS):
                hs = slice(h * HEAD_DIM, (h + 1) * HEAD_DIM)
                aug = slice(h * AUG, (h + 1) * AUG)
                half = slice(h * LANES, h * LANES + HEAD_DIM)
                rows = pl.ds(i * TQ, TQ)
                st = _nn(ka_ref[:, aug], qat_ref[i, aug, :]) - lset_ref[i, h:h + 1, :]
                if masked:
                    st = jnp.where(_causal_in_tile(), st, NEG)
                pt = jnp.exp(st)
                dvw[:, half] += _nn(pt.astype(BF16), do_ref[rows, hs])
                dst = pt * (_nn(v_ref[:, hs], dot_ref[i, hs, :]) - dlt_ref[i, h:h + 1, :])
                dsb = dst.astype(BF16)
                dkw[:, half] += _nn(dsb, q_ref[rows, hs])
                dqt_ref[i, hs, :] += _nn(kt_ref[0, hs, :], dsb)
                drs_ref[i, h, 0:1, :] += jnp.sum(dst, axis=0, keepdims=True)
                dca[:, h * LANES:(h + 1) * LANES] += dst[:, 0:LANES] + dst[:, LANES:2 * LANES]

        def step(i, carry):
            tile(i, False)
            return carry

        tile(j, True)
        lax.fori_loop(j + 1, nq, step, 0)
        lane = lax.broadcasted_iota(jnp.int32, (TQ, LANES), 1)
        dcs_all = jnp.zeros((TQ, LANES), F32)
        for h in range(HEADS):
            hs = slice(h * HEAD_DIM, (h + 1) * HEAD_DIM)
            half = slice(h * LANES, h * LANES + HEAD_DIM)
            dk_ref[:, hs] = dkw[:, half]
            dv_ref[:, hs] = dvw[:, half]
            colsum = jnp.sum(dca[:, h * LANES:(h + 1) * LANES], axis=1, keepdims=True)
            dcs_all = jnp.where(lane == h, colsum, dcs_all)
        dcs_ref[...] = dcs_all

        @pl.when(j == nq - 1)
        def _():
            _chips_finish(b_ref, got_ref, send_sems, recv_sems)

    blk = pl.BlockSpec((TQ, D_ATTN), _row)
    _, r, cdim = chip_blocks.shape
    return pl.pallas_call(
        body, grid=(nq,), name="attn_bwd",
        out_shape=(jax.ShapeDtypeStruct((nq, D_ATTN, TQ), F32), jax.ShapeDtypeStruct((s, D_ATTN), F32),
                   jax.ShapeDtypeStruct((s, D_ATTN), F32), jax.ShapeDtypeStruct((s, LANES), F32),
                   jax.ShapeDtypeStruct((nq, HEADS, 8, TQ), F32),
                   jax.ShapeDtypeStruct(chip_blocks.shape, chip_blocks.dtype)),
        in_specs=[pl.BlockSpec((TQ, HEADS * AUG), _row), blk, pl.BlockSpec((1, D_ATTN, TQ), lambda j: (j, 0, 0)),
                  VMEM_WHOLE, VMEM_WHOLE, VMEM_WHOLE, VMEM_WHOLE, VMEM_WHOLE, VMEM_WHOLE, ANY],
        out_specs=(pl.BlockSpec((nq, D_ATTN, TQ), lambda j: (0, 0, 0)), blk, blk, pl.BlockSpec((TQ, LANES), _row),
                   pl.BlockSpec((nq, HEADS, 8, TQ), lambda j: (0, 0, 0, 0)), ANY),
        scratch_shapes=[pltpu.VMEM((TQ, wide), F32), pltpu.VMEM((TQ, wide), F32), pltpu.VMEM((TQ, wide), F32),
                        pltpu.VMEM((r, cdim), chip_blocks.dtype),
                        pltpu.SemaphoreType.DMA((3,)), pltpu.SemaphoreType.DMA((3,)), pltpu.SemaphoreType.DMA],
        compiler_params=_params(1),
    )(ka, v, kt3, qat3, q, do, dot3, lset3, dlt3, chip_blocks)


def _pre_attn_bwd(dqt3, dk, dv, dcs, drs, fl, dy, x, dh1, g1, wqkv, wf, wu):
    s, d = x.shape
    nt = s // TS
    n = TS + HALO
    sub = TS // TQ

    def body(dqt_ref, dk_ref, dv_ref, dcs_ref, drs_ref, fl_ref, dy_ref, x_ref, dh1_ref, g_ref, wqkv_ref, wf_ref, wu_ref,
             gx_ref, dqkv_ref, dfb_ref, dub_ref, dg_ref, db_ref, ybuf, ccar, dlog):
        i = pl.program_id(0)
        ti = nt - 1 - i

        @pl.when(i == 0)
        def _():
            ybuf[TS:n, :] = jnp.zeros((HALO, D_POOL), F32)
            ccar[...] = jnp.zeros_like(ccar)
            dg_ref[...] = jnp.zeros_like(dg_ref)
            db_ref[...] = jnp.zeros_like(db_ref)

        rr = lax.broadcasted_iota(jnp.int32, (TS, TS), 0)
        cc = lax.broadcasted_iota(jnp.int32, (TS, TS), 1)
        triu = (cc >= rr).astype(F32)
        dlog[...] = ccar[...] + jnp.dot(triu, drs_ref[...] - dcs_ref[...], precision=HIGHEST, preferred_element_type=F32)
        ccar[...] = dlog[0:1, :]
        df = dlog[...] * jax.nn.sigmoid(-fl_ref[...])
        db_ref[...] += jnp.sum(df, axis=0, keepdims=True)
        dfb = df.astype(BF16)
        dfb_ref[...] = dfb

        t = ti * TS + lax.broadcasted_iota(jnp.int32, (TS, 1), 0)
        dy = dy_ref[...]
        for g, w in enumerate(POOL_WINDOWS):
            cols = slice(g * POOL_CH, (g + 1) * POOL_CH)
            ybuf[0:TS, cols] = dy[:, cols] / jnp.minimum(t + 1, w).astype(F32)
        for g, w in enumerate(POOL_WINDOWS):
            cols = slice(g * POOL_CH, (g + 1) * POOL_CH)
            sm = ybuf[:, cols]
            step = 1
            while step < w:
                sm = sm + pltpu.roll(sm, n - step, 0)
                step *= 2
            dub_ref[:, cols] = (sm[0:TS, :] - dy[:, cols]).astype(BF16)
        ybuf[TS:n, :] = ybuf[0:HALO, :]

        for a in range(sub):
            dqkv_ref[a * TQ:(a + 1) * TQ, 0:D_ATTN] = (dqt_ref[a].T * 0.125).astype(BF16)
        dqkv_ref[:, D_ATTN:2 * D_ATTN] = dk_ref[...].astype(BF16)
        dqkv_ref[:, 2 * D_ATTN:] = dv_ref[...].astype(BF16)
        dhn = _nn(dqkv_ref[...], wqkv_ref[...]) + _nn(dfb, wf_ref[...]) + _nn(dub_ref[...], wu_ref[...])
        dx, dg = _rms_bwd(x_ref[...], g_ref[...], dhn)
        gx_ref[...] = dh1_ref[...] + dx
        dg_ref[...] += dg

    rev = lambda i: (nt - 1 - i, 0)
    blk = lambda w: pl.BlockSpec((TS, w), rev)
    return pl.pallas_call(
        body, grid=(nt,), name="pre_attn_bwd",
        out_shape=(jax.ShapeDtypeStruct((s, d), F32), jax.ShapeDtypeStruct((s, 3 * D_ATTN), BF16),
                   jax.ShapeDtypeStruct((s, LANES), BF16), jax.ShapeDtypeStruct((s, D_POOL), BF16),
                   jax.ShapeDtypeStruct((1, d), F32), jax.ShapeDtypeStruct((1, LANES), F32)),
        in_specs=[pl.BlockSpec((sub, D_ATTN, TQ), lambda i: (nt - 1 - i, 0, 0)),
                  blk(D_ATTN), blk(D_ATTN), blk(LANES), blk(LANES), blk(LANES), blk(D_POOL), blk(d), blk(d),
                  pl.BlockSpec((1, d), _fixed), pl.BlockSpec(wqkv.shape, _fixed), pl.BlockSpec(wf.shape, _fixed),
                  pl.BlockSpec(wu.shape, _fixed)],
        out_specs=(blk(d), blk(3 * D_ATTN), blk(LANES), blk(D_POOL),
                   pl.BlockSpec((1, d), _fixed), pl.BlockSpec((1, LANES), _fixed)),
        scratch_shapes=[pltpu.VMEM((n, D_POOL), F32), pltpu.VMEM((1, LANES), F32), pltpu.VMEM((TS, LANES), F32)],
        compiler_params=_params(1),
    )(dqt3, dk, dv, dcs, drs, fl, dy, x, dh1, g1, wqkv, wf, wu)


def _wgrad(a, b, out_dtype, name):
    s, m = a.shape
    n = b.shape[1]
    tm = next(t for t in (512, 256, 128) if m % t == 0)
    ns = s // TS

    def body(a_ref, b_ref, o_ref, acc):
        i = pl.program_id(1)

        @pl.when(i == 0)
        def _():
            acc[...] = jnp.zeros_like(acc)

        acc[...] += _tn(a_ref[...], b_ref[pl.ds(i * TS, TS), :])

        @pl.when(i == ns - 1)
        def _():
            o_ref[...] = acc[...].astype(out_dtype)

    return pl.pallas_call(
        body, grid=(m // tm, ns), name=name, out_shape=jax.ShapeDtypeStruct((m, n), out_dtype),
        in_specs=[pl.BlockSpec((TS, tm), lambda j, i: (i, j)), VMEM_WHOLE],
        out_specs=pl.BlockSpec((tm, n), lambda j, i: (j, 0)),
        scratch_shapes=[pltpu.VMEM((tm, n), F32)],
        compiler_params=_params(2),
    )(a, b)


def _adamw(w, g, m, v):
    m = ADAM_B1 * m + (1.0 - ADAM_B1) * g
    v = ADAM_B2 * v + (1.0 - ADAM_B2) * (g * g)
    m_hat = m / (1.0 - ADAM_B1 ** ADAM_STEP)
    v_hat = v / (1.0 - ADAM_B2 ** ADAM_STEP)
    delta = -ADAM_LR * (m_hat / (jnp.sqrt(v_hat) + ADAM_EPS) + ADAM_WD * w)
    return delta, m, v


def _pair_sum(core, t, theirs, tr, name):
    nk, r, c = theirs.shape

    def body(core_ref, a_ref, b_ref, o_ref):
        o_ref[...] = (a_ref[...].astype(F32) + b_ref[...].astype(F32)).astype(BF16)

    blk = pl.BlockSpec((1, tr, c), lambda k, i, core_ref: (k, i, 0))
    return pl.pallas_call(
        body, name=name, out_shape=jax.ShapeDtypeStruct(theirs.shape, BF16),
        grid_spec=pltpu.PrefetchScalarGridSpec(
            num_scalar_prefetch=1, grid=(nk, r // tr),
            in_specs=[pl.BlockSpec((1, tr, c), lambda k, i, core_ref: (2 * k + core_ref[0], i, 0)), blk],
            out_specs=blk),
        compiler_params=_params(2),
    )(core, t, theirs)


def _reduce_update_big(parts, w, m, v, tr, name):
    nk, r, c = parts.shape

    def body(p_ref, w_ref, m_ref, v_ref, g_ref, d_ref, nm_ref, nv_ref):
        g = p_ref[0].astype(F32)
        for k in range(1, nk):
            g = g + p_ref[k].astype(F32)
        g_ref[...] = g
        d_ref[...], nm_ref[...], nv_ref[...] = _adamw(w_ref[...], g, m_ref[...], v_ref[...])

    blk = pl.BlockSpec((tr, c), _row)
    out = jax.ShapeDtypeStruct((r, c), F32)
    return pl.pallas_call(
        body, grid=(r // tr,), name=name, out_shape=(out,) * 4,
        in_specs=[pl.BlockSpec((nk, tr, c), lambda i: (0, i, 0)), blk, blk, blk],
        out_specs=(blk,) * 4, compiler_params=_params(1),
    )(parts, w, m, v)


def _reduce_update_small(parts, w, m, v):
    nd = parts.shape[0]

    def body(p_ref, w_ref, m_ref, v_ref, g_ref, d_ref, nm_ref, nv_ref):
        g = p_ref[0]
        for k in range(1, nd):
            g = g + p_ref[k]
        g_ref[...] = g
        d_ref[...], nm_ref[...], nv_ref[...] = _adamw(w_ref[...], g, m_ref[...], v_ref[...])

    out = jax.ShapeDtypeStruct(w.shape, F32)
    return pl.pallas_call(body, name="reduce_update_small", out_shape=(out,) * 4,
                          compiler_params=pltpu.CompilerParams(vmem_limit_bytes=VMEM_LIMIT))(parts, w, m, v)


MESH = pl.DeviceIdType.MESH


def _copy_through_vmem(src_hbm, dst_hbm, stage, sem):
    load = pltpu.make_async_copy(src_hbm, stage, sem)
    load.start()
    load.wait()
    store = pltpu.make_async_copy(stage, dst_hbm, sem)
    store.start()
    store.wait()


class _GatherPlan:
    def __init__(self, x_ref, out_ref, send_sems, recv_sems):
        x, y, c = lax.axis_index("x"), lax.axis_index("y"), lax.axis_index("c")
        self.me, self.sibling, self.c = (x, y, c), (x, y, 1 - c), c
        self.chips = [(1 - x, y), (x, 1 - y), (1 - x, 1 - y)]
        self.x_ref, self.out_ref, self.send_sems, self.recv_sems = x_ref, out_ref, send_sems, recv_sems

    def slot(self, px, py, pc):
        return self.out_ref.at[4 * px + 2 * py + pc]

    def copy(self, k, block, to, src=None):
        return pltpu.make_async_remote_copy(
            src_ref=self.slot(*block) if src is None else src, dst_ref=self.slot(*block),
            send_sem=self.send_sems.at[k], recv_sem=self.recv_sems.at[k], device_id=to, device_id_type=MESH)

    def first(self):
        return [self.copy(0, self.me, self.sibling, src=self.x_ref)] + [
            self.copy(1 + j, self.me, (*chip, self.c), src=self.x_ref) for j, chip in enumerate(self.chips)]

    def passed(self):
        return [self.copy(4 + j, (*chip, self.c), self.sibling) for j, chip in enumerate(self.chips)]


def _gather_start(x_ref, out_ref, stage, send_sems, recv_sems, local_sem):
    plan = _GatherPlan(x_ref, out_ref, send_sems, recv_sems)
    for cp in plan.first():
        cp.start()
    _copy_through_vmem(x_ref, plan.slot(*plan.me), stage, local_sem)


def _gather_pass_on(out_ref, send_sems, recv_sems):
    plan = _GatherPlan(None, out_ref, send_sems, recv_sems)
    passed = plan.passed()
    for j, chip in enumerate(plan.chips):
        plan.copy(1 + j, (*chip, plan.c), plan.me).wait_recv()
        passed[j].start()


def _gather_finish(x_ref, out_ref, send_sems, recv_sems):
    plan = _GatherPlan(x_ref, out_ref, send_sems, recv_sems)
    plan.copy(0, plan.sibling, plan.me).wait_recv()
    for j, chip in enumerate(plan.chips):
        plan.copy(4 + j, (*chip, 1 - plan.c), plan.me).wait_recv()
    for cp in plan.first() + plan.passed():
        cp.wait_send()


def _all_gather(xs, name):
    r, cdim = xs.shape

    def body(x_ref, out_ref, stage, send_sems, recv_sems, local_sem):
        _gather_start(x_ref, out_ref, stage, send_sems, recv_sems, local_sem)
        _gather_pass_on(out_ref, send_sems, recv_sems)
        _gather_finish(x_ref, out_ref, send_sems, recv_sems)

    return pl.pallas_call(
        body, name=name, out_shape=jax.ShapeDtypeStruct((N_DEV, r, cdim), xs.dtype),
        in_specs=[ANY], out_specs=ANY,
        scratch_shapes=[pltpu.VMEM((r, cdim), xs.dtype), pltpu.SemaphoreType.DMA((7,)), pltpu.SemaphoreType.DMA((7,)),
                        pltpu.SemaphoreType.DMA],
        compiler_params=pltpu.CompilerParams(vmem_limit_bytes=VMEM_LIMIT),
    )(xs)


def _rs_pair(t, name):
    _, r, cdim = t.shape

    def body(t_ref, theirs_ref, send_sems, recv_sems):
        x, y, c = lax.axis_index("x"), lax.axis_index("y"), lax.axis_index("c")
        remote = [pltpu.make_async_remote_copy(
            src_ref=t_ref.at[2 * k + (1 - c)], dst_ref=theirs_ref.at[k],
            send_sem=send_sems.at[k], recv_sem=recv_sems.at[k], device_id=(x, y, 1 - c), device_id_type=MESH)
            for k in range(4)]
        for cp in remote:
            cp.start()
        for cp in remote:
            cp.wait()

    return pl.pallas_call(
        body, name=name, out_shape=jax.ShapeDtypeStruct((4, r, cdim), t.dtype), in_specs=[ANY], out_specs=ANY,
        scratch_shapes=[pltpu.SemaphoreType.DMA((4,)), pltpu.SemaphoreType.DMA((4,))],
    )(t)


def _chips_start(b_ref, out_ref, stage, send_sems, recv_sems, local_sem):
    x, y, c = lax.axis_index("x"), lax.axis_index("y"), lax.axis_index("c")
    mychip = 2 * x + y
    for j, (px, py) in enumerate([(1 - x, y), (x, 1 - y), (1 - x, 1 - y)]):
        pltpu.make_async_remote_copy(
            src_ref=b_ref.at[2 * px + py], dst_ref=out_ref.at[mychip],
            send_sem=send_sems.at[j], recv_sem=recv_sems.at[j], device_id=(px, py, c), device_id_type=MESH).start()
    _copy_through_vmem(b_ref.at[mychip], out_ref.at[mychip], stage, local_sem)


def _chips_finish(b_ref, out_ref, send_sems, recv_sems):
    x, y, c = lax.axis_index("x"), lax.axis_index("y"), lax.axis_index("c")
    for j, (px, py) in enumerate([(1 - x, y), (x, 1 - y), (1 - x, 1 - y)]):
        pltpu.make_async_remote_copy(
            src_ref=b_ref.at[2 * px + py], dst_ref=out_ref.at[2 * px + py],
            send_sem=send_sems.at[j], recv_sem=recv_sems.at[j], device_id=(px, py, c), device_id_type=MESH).wait()


def _rs_chips(b, name):
    _, r, cdim = b.shape

    def body(b_ref, out_ref, stage, send_sems, recv_sems, local_sem):
        _chips_start(b_ref, out_ref, stage, send_sems, recv_sems, local_sem)
        _chips_finish(b_ref, out_ref, send_sems, recv_sems)

    return pl.pallas_call(
        body, name=name, out_shape=jax.ShapeDtypeStruct(b.shape, b.dtype), in_specs=[ANY], out_specs=ANY,
        scratch_shapes=[pltpu.VMEM((r, cdim), b.dtype), pltpu.SemaphoreType.DMA((3,)), pltpu.SemaphoreType.DMA((3,)),
                        pltpu.SemaphoreType.DMA],
        compiler_params=pltpu.CompilerParams(vmem_limit_bytes=VMEM_LIMIT),
    )(b)


def _pad_rows(a, rows):
    return jnp.pad(a, ((0, rows - a.shape[0]), (0, 0)))


def _pack_in(w_in):
    return _pad_rows(w_in[0].T, ROWS_IN)


def _unpack_in(r):
    return r[0:SHARD_IN].T[None]


def _pack_rest(w_out, w_gate, w_up, w_down, w_ple, w_pg):
    return jnp.concatenate([w_out[0], w_gate[0].T, w_up[0].T, w_down[0], w_ple[0].T.reshape(32, D_MODEL), w_pg[0]],
                           axis=0)


def _unpack_rest(r):
    return (r[0:OFF_GATE][None], r[OFF_GATE:OFF_UP].T[None], r[OFF_UP:OFF_DOWN].T[None],
            r[OFF_DOWN:OFF_PLE][None], r[OFF_PLE:OFF_PG].reshape(128, D_PLE).T[None], r[OFF_PG:ROWS_REST][None])


def _full_rest(g):
    return (g[:, 0:OFF_GATE].reshape(D_MODEL, D_MODEL), g[:, OFF_GATE:OFF_UP].reshape(D_FF, D_MODEL),
            g[:, OFF_UP:OFF_DOWN].reshape(D_FF, D_MODEL), g[:, OFF_DOWN:OFF_PLE].reshape(D_FF, D_MODEL),
            g[:, OFF_PLE:OFF_PG].reshape(D_MODEL, D_PLE), g[:, OFF_PG:ROWS_REST].reshape(D_MODEL, D_MODEL))


def _pack_small(w_pool, g_mix_pre, g_mix_post, g_ffn_pre, g_ffn_post, g_ple, g_attn, g_pool, pool_scale, b_forget,
                loss=None):
    def row(vrow):
        return jnp.pad(vrow.reshape(1, -1), ((0, 0), (0, D_MODEL - vrow.size)))
    rows = [w_pool.reshape(64, D_MODEL), row(g_mix_pre), row(g_mix_post), row(g_ffn_pre), row(g_ffn_post), row(g_ple),
            row(g_attn), row(g_pool), row(pool_scale), row(b_forget),
            row(loss) if loss is not None else jnp.zeros((1, D_MODEL), F32)]
    return _pad_rows(jnp.concatenate(rows, axis=0), SMALL_ROWS)


def _unpack_small(r):
    return dict(
        w_pool=r[0:64].reshape(1, 4, POOL_CH, POOL_CH), g_mix_pre=r[ROW_G_MIX_PRE:ROW_G_MIX_PRE + 1],
        g_mix_post=r[ROW_G_MIX_POST:ROW_G_MIX_POST + 1], g_ffn_pre=r[ROW_G_FFN_PRE:ROW_G_FFN_PRE + 1],
        g_ffn_post=r[ROW_G_FFN_POST:ROW_G_FFN_POST + 1], g_ple=r[ROW_G_PLE:ROW_G_PLE + 1],
        g_attn_grp=r[ROW_G_ATTN:ROW_G_ATTN + 1, 0:D_ATTN], g_pool_grp=r[ROW_G_POOL:ROW_G_POOL + 1, 0:D_POOL],
        pool_scale=r[ROW_POOL_SCALE:ROW_POOL_SCALE + 1, 0:D_POOL], b_forget=r[ROW_B_FORGET:ROW_B_FORGET + 1, 0:HEADS])


def _step(x, p, tgt, small, in_w, in_m, in_v, rest_w, rest_m, rest_v):
    core = lax.axis_index("c").astype(jnp.int32).reshape(1)
    win_t = _all_gather(in_w.astype(BF16), "gather_w_in")[:, 0:SHARD_IN].reshape(D_IN, D_MODEL)
    wqkv = win_t[0:3 * D_ATTN]
    wf = _pad_rows(win_t[3 * D_ATTN:3 * D_ATTN + HEADS], LANES)
    wu = win_t[3 * D_ATTN + HEADS:]
    wpool = small["w_pool"].astype(BF16)
    bpad = jnp.pad(small["b_forget"], ((0, 0), (0, LANES - HEADS)))

    lay = _attn_layout_constants()
    hn, q, ka, v, qat3, vt3, kt3, fl, y, mpre = _pre_attn_fwd(x, small["g_mix_pre"], wqkv, wf, wu, bpad, wpool, lay)
    a, lset3, gathered = _attn_fwd(ka, qat3, vt3, rest_w.astype(BF16))
    wout, wg_t, wu_t, wd, wple_t, wpg = _full_rest(gathered)
    mix, o, h1, hn2 = _post_attn_fwd(a, mpre, x, small["g_attn_grp"], small["g_pool_grp"], small["pool_scale"], wout,
                                     small["g_mix_post"], small["g_ffn_pre"])
    gate, up, act, ff, h2 = _ffn_fwd(hn2, wg_t, wu_t, wd, h1, small["g_ffn_post"])
    dh2, dff, dgl, dpp, h2b, pb, loss8, dg_ple, dg_ffn_post = _tail_fwd_bwd(
        h2, p, tgt, ff, wple_t, wpg, small["g_ple"], small["g_ffn_post"])
    dgate, dup, dh1, dg_ffn_pre = _ffn_bwd(dff, gate, up, wd, wg_t, wu_t, h1, dh2, small["g_ffn_pre"])
    dob, dab, dat3, dlt3, dmpb, dy, dg_mix_post, dg_attn, dg_pool, dps = _post_attn_bwd(
        dh1, o, a, mpre, wout, wpool, small["g_mix_post"], small["g_attn_grp"], small["g_pool_grp"], small["pool_scale"],
        lay["eye"])

    nd = N_DEV
    send_rest = jnp.concatenate([
        _wgrad(mix, dob, BF16, "wgrad_out").reshape(nd, 128, D_MODEL),
        _wgrad(dgate, hn2, BF16, "wgrad_gate").reshape(nd, 352, D_MODEL),
        _wgrad(dup, hn2, BF16, "wgrad_up").reshape(nd, 352, D_MODEL),
        _wgrad(act, dff, BF16, "wgrad_down").reshape(nd, 352, D_MODEL),
        _wgrad(dpp, pb, BF16, "wgrad_ple").reshape(nd, 32, D_MODEL),
        _wgrad(h2b, dgl, BF16, "wgrad_ple_gate").reshape(nd, 128, D_MODEL)], axis=1)
    pair_rest = _pair_sum(core, send_rest, _rs_pair(send_rest, "rs_pair_rest"), TR_REST, "rs_pair_sum_rest")

    dqt3, dk, dv, dcs, drs4, chips_rest = _attn_bwd(ka, v, kt3, qat3, q, dab, dat3, lset3, dlt3, pair_rest)
    upd_rest = _reduce_update_big(chips_rest, rest_w, rest_m, rest_v, TR_REST, "reduce_update_rest")
    drs = jnp.pad(drs4[:, :, 0, :].transpose(0, 2, 1).reshape(-1, HEADS), ((0, 0), (0, LANES - HEADS)))
    gx, dqkv, dfb, dub, dg_mix_pre, db = _pre_attn_bwd(dqt3, dk, dv, dcs, drs, fl, dy, x, dh1, small["g_mix_pre"], wqkv, wf, wu)

    dwin_t = jnp.concatenate([_wgrad(dqkv, hn, F32, "wgrad_qkv"), _wgrad(dfb, hn, F32, "wgrad_forget")[0:HEADS],
                              _wgrad(dub, hn, F32, "wgrad_pool_in")], axis=0)
    send_in = jnp.pad(dwin_t.reshape(nd, SHARD_IN, D_MODEL), ((0, 0), (0, ROWS_IN - SHARD_IN), (0, 0))).astype(BF16)
    pair_in = _pair_sum(core, send_in, _rs_pair(send_in, "rs_pair_in"), ROWS_IN, "rs_pair_sum_in")
    upd_in = _reduce_update_big(_rs_chips(pair_in, "rs_chips_in"), in_w, in_m, in_v, ROWS_IN, "reduce_update_in")

    dwp = _wgrad(y, dmpb, F32, "wgrad_pool")
    dw_pool = jnp.stack([dwp[g * POOL_CH:(g + 1) * POOL_CH, g * POOL_CH:(g + 1) * POOL_CH] for g in range(4)])
    small_part = _pack_small(dw_pool, dg_mix_pre, dg_mix_post, dg_ffn_pre, dg_ffn_post, dg_ple, dg_attn, dg_pool, dps,
                             db[:, 0:HEADS], loss8[0:1, 0:1])
    return gx, small_part, upd_in, upd_rest


def kernel(x, p, g_mix_pre, w_in, b_forget, g_attn_grp, g_pool_grp, w_pool, pool_scale, w_out, g_mix_post, g_ffn_pre, w_ffn_gate, w_ffn_up, w_ffn_down, g_ffn_post, w_ple_proj, g_ple, w_ple_gate, loss_target, m_g_mix_pre, m_w_in, m_b_forget, m_g_attn_grp, m_g_pool_grp, m_w_pool, m_pool_scale, m_w_out, m_g_mix_post, m_g_ffn_pre, m_w_ffn_gate, m_w_ffn_up, m_w_ffn_down, m_g_ffn_post, m_w_ple_proj, m_g_ple, m_w_ple_gate, v_g_mix_pre, v_w_in, v_b_forget, v_g_attn_grp, v_g_pool_grp, v_w_pool, v_pool_scale, v_w_out, v_g_mix_post, v_g_ffn_pre, v_w_ffn_gate, v_w_ffn_up, v_w_ffn_down, v_g_ffn_post, v_w_ple_proj, v_g_ple, v_w_ple_gate):
    small = dict(w_pool=w_pool[0], g_mix_pre=g_mix_pre, g_mix_post=g_mix_post, g_ffn_pre=g_ffn_pre,
                 g_ffn_post=g_ffn_post, g_ple=g_ple, g_attn_grp=g_attn_grp, g_pool_grp=g_pool_grp,
                 pool_scale=pool_scale, b_forget=b_forget)
    gx, small_part, upd_in, upd_rest = _step(
        x[0], p[0, 0], loss_target[0], small, _pack_in(w_in), _pack_in(m_w_in), _pack_in(v_w_in),
        _pack_rest(w_out, w_ffn_gate, w_ffn_up, w_ffn_down, w_ple_proj, w_ple_gate),
        _pack_rest(m_w_out, m_w_ffn_gate, m_w_ffn_up, m_w_ffn_down, m_w_ple_proj, m_w_ple_gate),
        _pack_rest(v_w_out, v_w_ffn_gate, v_w_ffn_up, v_w_ffn_down, v_w_ple_proj, v_w_ple_gate))

    small_all = _all_gather(small_part, "gather_small")
    sm_w = _pack_small(w_pool, g_mix_pre, g_mix_post, g_ffn_pre, g_ffn_post, g_ple, g_attn_grp, g_pool_grp, pool_scale, b_forget)
    sm_m = _pack_small(m_w_pool, m_g_mix_pre, m_g_mix_post, m_g_ffn_pre, m_g_ffn_post, m_g_ple, m_g_attn_grp, m_g_pool_grp, m_pool_scale, m_b_forget)
    sm_v = _pack_small(v_w_pool, v_g_mix_pre, v_g_mix_post, v_g_ffn_pre, v_g_ffn_post, v_g_ple, v_g_attn_grp, v_g_pool_grp, v_pool_scale, v_b_forget)
    upd_small = _reduce_update_small(small_all, sm_w, sm_m, sm_v)
    loss = upd_small[0][ROW_LOSS, 0]

    def leaves(k):
        b_out, b_gate, b_up, b_down, b_ple, b_pg = _unpack_rest(upd_rest[k])
        s = _unpack_small(upd_small[k])
        return (s["g_mix_pre"], _unpack_in(upd_in[k]), s["b_forget"], s["g_attn_grp"], s["g_pool_grp"], s["w_pool"],
                s["pool_scale"], b_out, s["g_mix_post"], s["g_ffn_pre"], b_gate, b_up, b_down, s["g_ffn_post"], b_ple,
                s["g_ple"], b_pg)

    return (loss, gx[None], *leaves(0), *leaves(1), *leaves(2), *leaves(3))
```

```python
import functools

import jax
import jax.numpy as jnp
from jax import lax
from jax.experimental import pallas as pl
from jax.experimental.pallas import tpu as pltpu

F32 = jnp.float32
BF16 = jnp.bfloat16
HIGHEST = lax.Precision.HIGHEST

D_MODEL = 1024
HEADS = 8
HEAD_DIM = 64
D_ATTN = HEADS * HEAD_DIM
POOL_WINDOWS = (2, 4, 8, 16)
POOL_CH = 128
D_POOL = POOL_CH * len(POOL_WINDOWS)
D_FF = 2816
D_PLE = 256
D_IN = 3 * D_ATTN + HEADS + D_POOL
RMS_EPS = 1e-6
N_DEV = 8

ADAM_LR = 0.001
ADAM_B1 = 0.9
ADAM_B2 = 0.999
ADAM_EPS = 1e-08
ADAM_WD = 0.01
ADAM_STEP = 10

LANES = 128
HALO = 16
TS = 512
TQ = 256
TN_FF = 256
NEG = -1e30
VMEM_LIMIT = 56 * 1024 * 1024

SHARD_IN = 257
ROWS_IN = 272
OFF_GATE = 128
OFF_UP = OFF_GATE + 352
OFF_DOWN = OFF_UP + 352
OFF_PLE = OFF_DOWN + 352
OFF_PG = OFF_PLE + 32
ROWS_REST = OFF_PG + 128
TR_REST = 192

SMALL_ROWS = 80
ROW_G_MIX_PRE, ROW_G_MIX_POST, ROW_G_FFN_PRE, ROW_G_FFN_POST, ROW_G_PLE = 64, 65, 66, 67, 68
ROW_G_ATTN, ROW_G_POOL, ROW_POOL_SCALE, ROW_B_FORGET, ROW_LOSS = 69, 70, 71, 72, 73


def _nn(a, b):
    return jnp.dot(a, b, preferred_element_type=F32)


def _nt(a, b):
    return lax.dot_general(a, b, (((1,), (1,)), ((), ())), preferred_element_type=F32)


def _tn(a, b):
    return lax.dot_general(a, b, (((0,), (0,)), ((), ())), preferred_element_type=F32)


def _rstd(v):
    return lax.rsqrt(jnp.mean(v * v, axis=-1, keepdims=True) + RMS_EPS)


def _rms_bwd(v, g, dy):
    r = _rstd(v)
    vh = v * r
    t = dy * g
    dv = r * (t - vh * jnp.mean(t * vh, axis=-1, keepdims=True))
    return dv, jnp.sum(dy * vh, axis=0, keepdims=True)


def _params(n_grid):
    return pltpu.CompilerParams(dimension_semantics=("arbitrary",) * n_grid, vmem_limit_bytes=VMEM_LIMIT)


def _row(i):
    return (i, 0)


def _fixed(*_):
    return (0, 0)


VMEM_WHOLE = pl.BlockSpec(memory_space=pltpu.VMEM)
SMEM_WHOLE = pl.BlockSpec(memory_space=pltpu.SMEM)
ANY = pl.BlockSpec(memory_space=pl.ANY)


AUG = 128
BIAS_LANE = HEAD_DIM
ONE_LANE = HEAD_DIM + 3
SPARE_LANE = HEADS


def _attn_layout_constants():
    import numpy as np
    place = np.zeros((D_ATTN, HEADS * AUG), np.float32)
    for r in range(D_ATTN):
        place[r, (r // HEAD_DIM) * AUG + r % HEAD_DIM] = 1.0
    bias_k = np.zeros((3, LANES, HEADS * AUG), np.float32)
    bias_q = np.zeros((3, LANES, HEADS * AUG), np.float32)
    for h in range(HEADS):
        for part in range(3):
            bias_k[part, h, h * AUG + BIAS_LANE + part] = -1.0
            bias_q[part, h, h * AUG + ONE_LANE + part] = 1.0
            bias_k[0, SPARE_LANE, h * AUG + ONE_LANE + part] = 1.0
            bias_q[0, SPARE_LANE, h * AUG + BIAS_LANE + part] = 1.0
    as_bf = lambda a: jnp.asarray(a, BF16)
    return dict(place=as_bf(place), place_t=as_bf(place.T), bias_k=as_bf(bias_k),
                bias_q_t=as_bf(bias_q.transpose(0, 2, 1)), eye=as_bf(np.eye(D_ATTN, dtype=np.float32)))


def _pre_attn_fwd(x, g1, wqkv, wf, wu, bpad, wpool, lay):
    s, d = x.shape
    nt = s // TS
    sub = TS // TQ

    def body(x_ref, g_ref, wqkv_ref, wf_ref, wu_ref, b_ref, wp_ref, place_ref, place_t_ref, bk_ref, bqt_ref, eye_ref,
             hn_ref, q_ref, ka_ref, v_ref, qat_ref, vt_ref, kt_ref, fl_ref, y_ref, mp_ref, ubuf, ccar, cbuf):
        i = pl.program_id(0)

        @pl.when(i == 0)
        def _():
            ubuf[0:HALO, :] = jnp.zeros((HALO, D_POOL), F32)
            ccar[...] = jnp.zeros_like(ccar)

        xv = x_ref[...]
        hn = (xv * _rstd(xv) * g_ref[...]).astype(BF16)
        hn_ref[...] = hn
        zq = _nt(hn, wqkv_ref[...])
        qb = (zq[:, 0:D_ATTN] * 0.125).astype(BF16)
        kb = zq[:, D_ATTN:2 * D_ATTN].astype(BF16)
        vb = zq[:, 2 * D_ATTN:3 * D_ATTN].astype(BF16)
        q_ref[...] = qb
        v_ref[...] = vb

        fl = _nt(hn, wf_ref[...]) + b_ref[...]
        fl_ref[...] = fl
        logf = jax.nn.log_sigmoid(fl)
        rr = lax.broadcasted_iota(jnp.int32, (TS, TS), 0)
        cc = lax.broadcasted_iota(jnp.int32, (TS, TS), 1)
        tril = (cc <= rr).astype(F32)
        c = jnp.dot(tril, logf, precision=HIGHEST, preferred_element_type=F32) + ccar[...]
        cbuf[...] = c
        ccar[...] = cbuf[TS - 1:TS, :]
        hi = c.astype(BF16)
        rest = c - hi.astype(F32)
        mid = rest.astype(BF16)
        lo = (rest - mid.astype(F32)).astype(BF16)
        lane = lax.broadcasted_iota(jnp.int32, (TS, LANES), 1)
        parts = (jnp.where(lane == SPARE_LANE, 1.0, hi).astype(BF16), mid, lo)
        ka = _nn(kb, place_ref[...])
        qat = _nt(place_t_ref[...], qb)
        for part in range(3):
            ka = ka + _nn(parts[part], bk_ref[part])
            qat = qat + _nt(bqt_ref[part], parts[part])
        ka_ref[...] = ka.astype(BF16)
        qat = qat.astype(BF16)
        vt = _nt(eye_ref[...], vb).astype(BF16)
        kt = _nt(eye_ref[...], kb).astype(BF16)
        for a in range(sub):
            qat_ref[a] = qat[:, a * TQ:(a + 1) * TQ]
            vt_ref[a] = vt[:, a * TQ:(a + 1) * TQ]
            kt_ref[a] = kt[:, a * TQ:(a + 1) * TQ]

        u = _nt(hn, wu_ref[...])
        ubuf[HALO:HALO + TS, :] = u
        t = i * TS + lax.broadcasted_iota(jnp.int32, (TS, 1), 0)
        for g, w in enumerate(POOL_WINDOWS):
            cols = slice(g * POOL_CH, (g + 1) * POOL_CH)
            sm = ubuf[:, cols]
            step = 1
            while step < w:
                sm = sm + pltpu.roll(sm, step, 0)
                step *= 2
            cnt = jnp.minimum(t + 1, w).astype(F32)
            yg = (sm[HALO:, :] / cnt - u[:, cols]).astype(BF16)
            y_ref[:, cols] = yg
            mp_ref[:, cols] = _nn(yg, wp_ref[g])
        ubuf[0:HALO, :] = u[TS - HALO:, :]

    nq = s // TQ
    aug = HEADS * AUG
    outs = (
        jax.ShapeDtypeStruct((s, d), BF16), jax.ShapeDtypeStruct((s, D_ATTN), BF16),
        jax.ShapeDtypeStruct((s, aug), BF16), jax.ShapeDtypeStruct((s, D_ATTN), BF16),
        jax.ShapeDtypeStruct((nq, aug, TQ), BF16), jax.ShapeDtypeStruct((nq, D_ATTN, TQ), BF16),
        jax.ShapeDtypeStruct((nq, D_ATTN, TQ), BF16),
        jax.ShapeDtypeStruct((s, LANES), F32),
        jax.ShapeDtypeStruct((s, D_POOL), BF16), jax.ShapeDtypeStruct((s, D_POOL), F32),
    )
    fixed3 = lambda i: (0, 0, 0)
    tiles3 = lambda rows: pl.BlockSpec((sub, rows, TQ), lambda i: (i, 0, 0))
    return pl.pallas_call(
        body, grid=(nt,), out_shape=outs, name="pre_attn_fwd",
        in_specs=[pl.BlockSpec((TS, d), _row), pl.BlockSpec((1, d), _fixed),
                  pl.BlockSpec(wqkv.shape, _fixed), pl.BlockSpec(wf.shape, _fixed), pl.BlockSpec(wu.shape, _fixed),
                  pl.BlockSpec((1, LANES), _fixed), pl.BlockSpec(wpool.shape, fixed3),
                  pl.BlockSpec(lay["place"].shape, _fixed), pl.BlockSpec(lay["place_t"].shape, _fixed),
                  pl.BlockSpec(lay["bias_k"].shape, fixed3), pl.BlockSpec(lay["bias_q_t"].shape, fixed3),
                  pl.BlockSpec(lay["eye"].shape, _fixed)],
        out_specs=(pl.BlockSpec((TS, d), _row), pl.BlockSpec((TS, D_ATTN), _row),
                   pl.BlockSpec((TS, aug), _row), pl.BlockSpec((TS, D_ATTN), _row),
                   tiles3(aug), tiles3(D_ATTN), tiles3(D_ATTN),
                   pl.BlockSpec((TS, LANES), _row),
                   pl.BlockSpec((TS, D_POOL), _row), pl.BlockSpec((TS, D_POOL), _row)),
        scratch_shapes=[pltpu.VMEM((TS + HALO, D_POOL), F32), pltpu.VMEM((1, LANES), F32), pltpu.VMEM((TS, LANES), F32)],
        compiler_params=_params(1),
    )(x, g1, wqkv, wf, wu, bpad, wpool, lay["place"], lay["place_t"], lay["bias_k"], lay["bias_q_t"], lay["eye"])


def _causal_in_tile():
    krow = lax.broadcasted_iota(jnp.int32, (TQ, TQ), 0)
    qcol = lax.broadcasted_iota(jnp.int32, (TQ, TQ), 1)
    return krow <= qcol


def _attn_fwd(ka, qat3, vt3, own_block):
    s = ka.shape[0]
    nq = s // TQ
    pass_on_step = (2 * nq) // 3

    def body(qa_ref, ka_ref, vt_ref, own_ref, a_ref, lset_ref, all_ref, acc, st_scr, pt_scr,
             stage, send_sems, recv_sems, local_sem):
        i = pl.program_id(0)

        @pl.when(i == 0)
        def _():
            _gather_start(own_ref, all_ref, stage, send_sems, recv_sems, local_sem)

        @pl.when(i == pass_on_step)
        def _():
            _gather_pass_on(all_ref, send_sems, recv_sems)

        acc[...] = jnp.zeros_like(acc)

        def tile(j, stats, masked):
            tile_max = []
            for h in range(HEADS):
                aug = slice(h * AUG, (h + 1) * AUG)
                st = _nn(ka_ref[pl.ds(j * TQ, TQ), aug], qa_ref[0, aug, :])
                if masked:
                    st = jnp.where(_causal_in_tile(), st, NEG)
                st_scr[h] = st
                tile_max.append(jnp.max(st, axis=0, keepdims=True))
            new, scale = [], []
            for h in range(HEADS):
                m_old, l_old = stats[h]
                m_new = jnp.maximum(m_old, tile_max[h])
                al = jnp.exp(m_old - m_new)
                pt = jnp.exp(st_scr[h] - m_new)
                pt_scr[h] = pt.astype(BF16)
                new.append((m_new, al * l_old + jnp.sum(pt, axis=0, keepdims=True)))
                scale.append(al)
            for h in range(HEADS):
                rows = slice(h * HEAD_DIM, (h + 1) * HEAD_DIM)
                acc[rows, :] = scale[h] * acc[rows, :] + _nn(vt_ref[j, rows, :], pt_scr[h])
            return tuple(new)

        init = tuple((jnp.full((1, TQ), NEG, F32), jnp.zeros((1, TQ), F32)) for _ in range(HEADS))
        stats = lax.fori_loop(0, i, functools.partial(tile, masked=False), init)
        stats = tile(i, stats, True)
        for h in range(HEADS):
            rows = slice(h * HEAD_DIM, (h + 1) * HEAD_DIM)
            acc[rows, :] = acc[rows, :] / stats[h][1]
            lset_ref[0, h:h + 1, :] = stats[h][0] + jnp.log(stats[h][1])
        a_ref[...] = acc[...].T

        @pl.when(i == nq - 1)
        def _():
            _gather_finish(own_ref, all_ref, send_sems, recv_sems)

    r, cdim = own_block.shape
    return pl.pallas_call(
        body, grid=(nq,), name="attn_fwd",
        out_shape=(jax.ShapeDtypeStruct((s, D_ATTN), F32), jax.ShapeDtypeStruct((nq, HEADS, TQ), F32),
                   jax.ShapeDtypeStruct((N_DEV, r, cdim), own_block.dtype)),
        in_specs=[pl.BlockSpec((1, HEADS * AUG, TQ), lambda i: (i, 0, 0)), VMEM_WHOLE, VMEM_WHOLE, ANY],
        out_specs=(pl.BlockSpec((TQ, D_ATTN), _row), pl.BlockSpec((1, HEADS, TQ), lambda i: (i, 0, 0)), ANY),
        scratch_shapes=[pltpu.VMEM((D_ATTN, TQ), F32), pltpu.VMEM((HEADS, TQ, TQ), F32), pltpu.VMEM((HEADS, TQ, TQ), BF16),
                        pltpu.VMEM((r, cdim), own_block.dtype),
                        pltpu.SemaphoreType.DMA((7,)), pltpu.SemaphoreType.DMA((7,)), pltpu.SemaphoreType.DMA],
        compiler_params=_params(1),
    )(qat3, ka, vt3, own_block)


def _post_attn_fwd(a, mpre, x, g_attn, g_pool, pscale, wout, g_post, g_ffn_pre):
    s, d = x.shape

    def body(a_ref, mp_ref, x_ref, ga_ref, gp_ref, ps_ref, wo_ref, gpost_ref, gpre_ref,
             mix_ref, o_ref, h1_ref, hn2_ref):
        av = a_ref[...]
        mix_ref[:, 0:D_ATTN] = (av * _rstd(av) * ga_ref[...]).astype(BF16)
        mv = mp_ref[...] * ps_ref[...]
        mix_ref[:, D_ATTN:] = (mv * _rstd(mv) * gp_ref[...]).astype(BF16)
        o = _nn(mix_ref[...], wo_ref[...])
        o_ref[...] = o
        h1 = x_ref[...] + o * _rstd(o) * gpost_ref[...]
        h1_ref[...] = h1
        hn2_ref[...] = (h1 * _rstd(h1) * gpre_ref[...]).astype(BF16)

    vec = lambda n: pl.BlockSpec((1, n), _fixed)
    return pl.pallas_call(
        body, grid=(s // TS,), name="post_attn_fwd",
        out_shape=(jax.ShapeDtypeStruct((s, d), BF16), jax.ShapeDtypeStruct((s, d), F32),
                   jax.ShapeDtypeStruct((s, d), F32), jax.ShapeDtypeStruct((s, d), BF16)),
        in_specs=[pl.BlockSpec((TS, D_ATTN), _row), pl.BlockSpec((TS, D_POOL), _row), pl.BlockSpec((TS, d), _row),
                  vec(D_ATTN), vec(D_POOL), vec(D_POOL), pl.BlockSpec(wout.shape, _fixed), vec(d), vec(d)],
        out_specs=(pl.BlockSpec((TS, d), _row),) * 4,
        compiler_params=_params(1),
    )(a, mpre, x, g_attn, g_pool, pscale, wout, g_post, g_ffn_pre)


def _ffn_fwd(hn2, wg, wu, wd, h1, g_post):
    s, d = h1.shape
    nc = D_FF // TN_FF

    def body(hn_ref, wg_ref, wu_ref, wd_ref, h1_ref, g_ref, gate_ref, up_ref, act_ref, ff_ref, h2_ref, acc):
        j = pl.program_id(1)

        @pl.when(j == 0)
        def _():
            acc[...] = jnp.zeros_like(acc)

        hn = hn_ref[...]
        gt = _nt(hn, wg_ref[...])
        up = _nt(hn, wu_ref[...])
        act = (gt * jax.nn.sigmoid(gt) * up).astype(BF16)
        gate_ref[...] = gt.astype(BF16)
        up_ref[...] = up.astype(BF16)
        act_ref[...] = act
        acc[...] += _nn(act, wd_ref[...])

        @pl.when(j == nc - 1)
        def _():
            ff = acc[...]
            ff_ref[...] = ff
            h2_ref[...] = h1_ref[...] + ff * _rstd(ff) * g_ref[...]

    rowblk = pl.BlockSpec((TS, d), lambda i, j: (i, 0))
    wblk = pl.BlockSpec((TN_FF, d), lambda i, j: (j, 0))
    chunk = pl.BlockSpec((TS, TN_FF), lambda i, j: (i, j))
    return pl.pallas_call(
        body, grid=(s // TS, nc), name="ffn_fwd",
        out_shape=(jax.ShapeDtypeStruct((s, D_FF), BF16),) * 3 + (jax.ShapeDtypeStruct((s, d), F32),) * 2,
        in_specs=[rowblk, wblk, wblk, wblk, rowblk, pl.BlockSpec((1, d), lambda i, j: (0, 0))],
        out_specs=(chunk, chunk, chunk, rowblk, rowblk),
        scratch_shapes=[pltpu.VMEM((TS, d), F32)],
        compiler_params=_params(2),
    )(hn2, wg, wu, wd, h1, g_post)


def _tail_fwd_bwd(h2, p, tgt, ff, wple, wpg, g_ple, g_ffn_post):
    s, d = h2.shape

    def body(h2_ref, p_ref, t_ref, ff_ref, wple_ref, wpg_ref, gple_ref, gfp_ref,
             dh2_ref, dff_ref, dgl_ref, dpp_ref, h2b_ref, pb_ref, loss_ref, dgple_ref, dgfp_ref):
        i = pl.program_id(0)

        @pl.when(i == 0)
        def _():
            loss_ref[...] = jnp.zeros_like(loss_ref)
            dgple_ref[...] = jnp.zeros_like(dgple_ref)
            dgfp_ref[...] = jnp.zeros_like(dgfp_ref)

        h2 = h2_ref[...]
        h2b = h2.astype(BF16)
        h2b_ref[...] = h2b
        pb = p_ref[...].astype(BF16)
        pb_ref[...] = pb
        pp = _nt(pb, wple_ref[...])
        gple = gple_ref[...]
        e = pp * _rstd(pp) * gple
        sg = jax.nn.sigmoid(_nn(h2b, wpg_ref[...]))
        diff = h2 + sg * e - t_ref[...]
        sq = jnp.sum(jnp.sum(diff * diff, axis=1, keepdims=True), axis=0, keepdims=True)
        loss_ref[...] += jnp.broadcast_to(sq * (0.5 / d), loss_ref.shape)
        dh3 = diff * (1.0 / d)
        dgl = (dh3 * e * sg * (1.0 - sg)).astype(BF16)
        dgl_ref[...] = dgl
        dh2 = dh3 + _nt(dgl, wpg_ref[...])
        dh2_ref[...] = dh2
        dpp, dg = _rms_bwd(pp, gple, dh3 * sg)
        dpp_ref[...] = dpp.astype(BF16)
        dgple_ref[...] += dg
        dff, dg = _rms_bwd(ff_ref[...], gfp_ref[...], dh2)
        dff_ref[...] = dff.astype(BF16)
        dgfp_ref[...] += dg

    rowblk = pl.BlockSpec((TS, d), _row)
    vec = pl.BlockSpec((1, d), _fixed)
    return pl.pallas_call(
        body, grid=(s // TS,), name="tail_fwd_bwd",
        out_shape=(jax.ShapeDtypeStruct((s, d), F32), jax.ShapeDtypeStruct((s, d), BF16),
                   jax.ShapeDtypeStruct((s, d), BF16), jax.ShapeDtypeStruct((s, d), BF16),
                   jax.ShapeDtypeStruct((s, d), BF16), jax.ShapeDtypeStruct((s, D_PLE), BF16),
                   jax.ShapeDtypeStruct((8, LANES), F32), jax.ShapeDtypeStruct((1, d), F32),
                   jax.ShapeDtypeStruct((1, d), F32)),
        in_specs=[rowblk, pl.BlockSpec((TS, D_PLE), _row), rowblk, rowblk,
                  pl.BlockSpec(wple.shape, _fixed), pl.BlockSpec(wpg.shape, _fixed), vec, vec],
        out_specs=(rowblk, rowblk, rowblk, rowblk, rowblk, pl.BlockSpec((TS, D_PLE), _row),
                   pl.BlockSpec((8, LANES), _fixed), vec, vec),
        compiler_params=_params(1),
    )(h2, p, tgt, ff, wple, wpg, g_ple, g_ffn_post)


def _ffn_bwd(dff, gate, up, wd, wg, wu, h1, dh2, g_pre):
    s, d = h1.shape
    nc = D_FF // TN_FF

    def body(dff_ref, gate_ref, up_ref, wd_ref, wg_ref, wu_ref, h1_ref, dh2_ref, g_ref,
             dgate_ref, dup_ref, dh1_ref, dg_ref, acc):
        i = pl.program_id(0)
        j = pl.program_id(1)

        @pl.when((i == 0) & (j == 0))
        def _():
            dg_ref[...] = jnp.zeros_like(dg_ref)

        @pl.when(j == 0)
        def _():
            acc[...] = jnp.zeros_like(acc)

        dact = _nt(dff_ref[...], wd_ref[...])
        gt = gate_ref[...].astype(F32)
        sg = jax.nn.sigmoid(gt)
        dup = (dact * gt * sg).astype(BF16)
        dgate = (dact * up_ref[...].astype(F32) * (sg * (1.0 + gt * (1.0 - sg)))).astype(BF16)
        dgate_ref[...] = dgate
        dup_ref[...] = dup
        acc[...] += _nn(dgate, wg_ref[...]) + _nn(dup, wu_ref[...])

        @pl.when(j == nc - 1)
        def _():
            dv, dg = _rms_bwd(h1_ref[...], g_ref[...], acc[...])
            dh1_ref[...] = dh2_ref[...] + dv
            dg_ref[...] += dg

    rowblk = pl.BlockSpec((TS, d), lambda i, j: (i, 0))
    wblk = pl.BlockSpec((TN_FF, d), lambda i, j: (j, 0))
    chunk = pl.BlockSpec((TS, TN_FF), lambda i, j: (i, j))
    vec = pl.BlockSpec((1, d), lambda i, j: (0, 0))
    return pl.pallas_call(
        body, grid=(s // TS, nc), name="ffn_bwd",
        out_shape=(jax.ShapeDtypeStruct((s, D_FF), BF16), jax.ShapeDtypeStruct((s, D_FF), BF16),
                   jax.ShapeDtypeStruct((s, d), F32), jax.ShapeDtypeStruct((1, d), F32)),
        in_specs=[rowblk, chunk, chunk, wblk, wblk, wblk, rowblk, rowblk, vec],
        out_specs=(chunk, chunk, rowblk, vec),
        scratch_shapes=[pltpu.VMEM((TS, d), F32)],
        compiler_params=_params(2),
    )(dff, gate, up, wd, wg, wu, h1, dh2, g_pre)


def _post_attn_bwd(dh1, o, a, mpre, wout, wpool, g_post, g_attn, g_pool, pscale, eye):
    s, d = dh1.shape
    sub = TS // TQ

    def body(dh1_ref, o_ref, a_ref, mp_ref, wo_ref, wp_ref, gpost_ref, ga_ref, gp_ref, ps_ref, eye_ref,
             dob_ref, dab_ref, dat_ref, dlt_ref, dmpb_ref, dy_ref, dgpost_ref, dga_ref, dgp_ref, dps_ref):
        i = pl.program_id(0)

        @pl.when(i == 0)
        def _():
            dgpost_ref[...] = jnp.zeros_like(dgpost_ref)
            dga_ref[...] = jnp.zeros_like(dga_ref)
            dgp_ref[...] = jnp.zeros_like(dgp_ref)
            dps_ref[...] = jnp.zeros_like(dps_ref)

        do, dg = _rms_bwd(o_ref[...], gpost_ref[...], dh1_ref[...])
        dgpost_ref[...] += dg
        dob = do.astype(BF16)
        dob_ref[...] = dob
        dmix = _nt(dob, wo_ref[...])

        av = a_ref[...]
        da, dg = _rms_bwd(av, ga_ref[...], dmix[:, 0:D_ATTN])
        dga_ref[...] += dg
        dab = da.astype(BF16)
        dab_ref[...] = dab
        dat = _nt(eye_ref[...], dab).astype(BF16)
        hsel = (lax.shift_right_logical(lax.broadcasted_iota(jnp.int32, (HEADS, D_ATTN), 1), 6)
                == lax.broadcasted_iota(jnp.int32, (HEADS, D_ATTN), 0)).astype(F32)
        dlt = lax.dot_general(hsel, da * av, (((1,), (1,)), ((), ())), precision=HIGHEST, preferred_element_type=F32)
        for q in range(sub):
            dlt_ref[q] = dlt[:, q * TQ:(q + 1) * TQ]
            dat_ref[q] = dat[:, q * TQ:(q + 1) * TQ]

        ps = ps_ref[...]
        mp = mp_ref[...]
        dm, dg = _rms_bwd(mp * ps, gp_ref[...], dmix[:, D_ATTN:])
        dgp_ref[...] += dg
        dps_ref[...] += jnp.sum(dm * mp, axis=0, keepdims=True)
        dmpb = (dm * ps).astype(BF16)
        dmpb_ref[...] = dmpb
        for g in range(len(POOL_WINDOWS)):
            cols = slice(g * POOL_CH, (g + 1) * POOL_CH)
            dy_ref[:, cols] = _nt(dmpb[:, cols], wp_ref[g])

    rowblk = pl.BlockSpec((TS, d), _row)
    half = pl.BlockSpec((TS, D_ATTN), _row)
    vec = lambda n: pl.BlockSpec((1, n), _fixed)
    return pl.pallas_call(
        body, grid=(s // TS,), name="post_attn_bwd",
        out_shape=(jax.ShapeDtypeStruct((s, d), BF16), jax.ShapeDtypeStruct((s, D_ATTN), BF16),
                   jax.ShapeDtypeStruct((s // TQ, D_ATTN, TQ), BF16),
                   jax.ShapeDtypeStruct((s // TQ, HEADS, TQ), F32), jax.ShapeDtypeStruct((s, D_POOL), BF16),
                   jax.ShapeDtypeStruct((s, D_POOL), F32), jax.ShapeDtypeStruct((1, d), F32),
                   jax.ShapeDtypeStruct((1, D_ATTN), F32), jax.ShapeDtypeStruct((1, D_POOL), F32),
                   jax.ShapeDtypeStruct((1, D_POOL), F32)),
        in_specs=[rowblk, rowblk, half, half, pl.BlockSpec(wout.shape, _fixed),
                  pl.BlockSpec(wpool.shape, lambda i: (0, 0, 0)), vec(d), vec(D_ATTN), vec(D_POOL), vec(D_POOL),
                  pl.BlockSpec(eye.shape, _fixed)],
        out_specs=(rowblk, half, pl.BlockSpec((sub, D_ATTN, TQ), lambda i: (i, 0, 0)),
                   pl.BlockSpec((sub, HEADS, TQ), lambda i: (i, 0, 0)), half, half,
                   vec(d), vec(D_ATTN), vec(D_POOL), vec(D_POOL)),
        compiler_params=_params(1),
    )(dh1, o, a, mpre, wout, wpool, g_post, g_attn, g_pool, pscale, eye)


def _attn_bwd(ka, v, kt3, qat3, q, do, dot3, lset3, dlt3, chip_blocks):
    s = q.shape[0]
    nq = s // TQ
    wide = HEADS * LANES

    def body(ka_ref, v_ref, kt_ref, qat_ref, q_ref, do_ref, dot_ref, lset_ref, dlt_ref, b_ref,
             dqt_ref, dk_ref, dv_ref, dcs_ref, drs_ref, got_ref, dca, dkw, dvw, pt_scr, ptb_scr, dsb_scr,
             stage, send_sems, recv_sems, local_sem):
        j = pl.program_id(0)

        @pl.when(j == 0)
        def _():
            _chips_start(b_ref, got_ref, stage, send_sems, recv_sems, local_sem)
            dqt_ref[...] = jnp.zeros_like(dqt_ref)
            drs_ref[...] = jnp.zeros_like(drs_ref)

        dkw[...] = jnp.zeros_like(dkw)
        dvw[...] = jnp.zeros_like(dvw)
        dca[...] = jnp.zeros_like(dca)

        def tile(i, masked):
            rows = pl.ds(i * TQ, TQ)
            for h in range(HEADS):
                aug = slice(h * AUG, (h + 1) * AUG)
                st = _nn(ka_ref[:, aug], qat_ref[i, aug, :]) - lset_ref[i, h:h + 1, :]
                if masked:
                    st = jnp.where(_causal_in_tile(), st, NEG)
                pt = jnp.exp(st)
                pt_scr[h] = pt
                ptb_scr[h] = pt.astype(BF16)
            for h in range(HEADS):
                hs = slice(h * HEAD_DIM, (h + 1) * HEAD_DIM)
                half = slice(h * LANES, h * LANES + HEAD_DIM)
                dvw[:, half] += _nn(ptb_scr[h], do_ref[rows, hs])
                dst = pt_scr[h] * (_nn(v_ref[:, hs], dot_ref[i, hs, :]) - dlt_ref[i, h:h + 1, :])
                dsb_scr[h] = dst.astype(BF16)
                drs_ref[i, h, 0:1, :] += jnp.sum(dst, axis=0, keepdims=True)
                dca[:, h * LANES:(h + 1) * LANES] += dst[:, 0:LANES] + dst[:, LANES:2 * LANES]
            for h in range(HEADS):
                hs = slice(h * HEAD_DIM, (h + 1) * HEAD_DIM)
                half = slice(h * LANES, h * LANES + HEAD_DIM)
                dkw[:, half] += _nn(dsb_scr[h], q_ref[rows, hs])
                dqt_ref[i, hs, :] += _nn(kt_ref[0, hs, :], dsb_scr[h])

        def step(i, carry):
            tile(i, False)
            return carry

        tile(j, True)
        lax.fori_loop(j + 1, nq, step, 0)
        lane = lax.broadcasted_iota(jnp.int32, (TQ, LANES), 1)
        dcs_all = jnp.zeros((TQ, LANES), F32)
        for h in range(HEADS):
            hs = slice(h * HEAD_DIM, (h + 1) * HEAD_DIM)
            half = slice(h * LANES, h * LANES + HEAD_DIM)
            dk_ref[:, hs] = dkw[:, half]
            dv_ref[:, hs] = dvw[:, half]
            colsum = jnp.sum(dca[:, h * LANES:(h + 1) * LANES], axis=1, keepdims=True)
            dcs_all = jnp.where(lane == h, colsum, dcs_all)
        dcs_ref[...] = dcs_all

        @pl.when(j == nq - 1)
        def _():
            _chips_finish(b_ref, got_ref, send_sems, recv_sems)

    blk = pl.BlockSpec((TQ, D_ATTN), _row)
    _, r, cdim = chip_blocks.shape
    return pl.pallas_call(
        body, grid=(nq,), name="attn_bwd",
        out_shape=(jax.ShapeDtypeStruct((nq, D_ATTN, TQ), F32), jax.ShapeDtypeStruct((s, D_ATTN), F32),
                   jax.ShapeDtypeStruct((s, D_ATTN), F32), jax.ShapeDtypeStruct((s, LANES), F32),
                   jax.ShapeDtypeStruct((nq, HEADS, 8, TQ), F32),
                   jax.ShapeDtypeStruct(chip_blocks.shape, chip_blocks.dtype)),
        in_specs=[pl.BlockSpec((TQ, HEADS * AUG), _row), blk, pl.BlockSpec((1, D_ATTN, TQ), lambda j: (j, 0, 0)),
                  VMEM_WHOLE, VMEM_WHOLE, VMEM_WHOLE, VMEM_WHOLE, VMEM_WHOLE, VMEM_WHOLE, ANY],
        out_specs=(pl.BlockSpec((nq, D_ATTN, TQ), lambda j: (0, 0, 0)), blk, blk, pl.BlockSpec((TQ, LANES), _row),
                   pl.BlockSpec((nq, HEADS, 8, TQ), lambda j: (0, 0, 0, 0)), ANY),
        scratch_shapes=[pltpu.VMEM((TQ, wide), F32), pltpu.VMEM((TQ, wide), F32), pltpu.VMEM((TQ, wide), F32),
                        pltpu.VMEM((HEADS, TQ, TQ), F32), pltpu.VMEM((HEADS, TQ, TQ), BF16),
                        pltpu.VMEM((HEADS, TQ, TQ), BF16), pltpu.VMEM((r, cdim), chip_blocks.dtype),
                        pltpu.SemaphoreType.DMA((3,)), pltpu.SemaphoreType.DMA((3,)), pltpu.SemaphoreType.DMA],
        compiler_params=_params(1),
    )(ka, v, kt3, qat3, q, do, dot3, lset3, dlt3, chip_blocks)


def _pre_attn_bwd(dqt3, dk, dv, dcs, drs, fl, dy, x, dh1, g1, wqkv, wf, wu):
    s, d = x.shape
    nt = s // TS
    n = TS + HALO
    sub = TS // TQ

    def body(dqt_ref, dk_ref, dv_ref, dcs_ref, drs_ref, fl_ref, dy_ref, x_ref, dh1_ref, g_ref, wqkv_ref, wf_ref, wu_ref,
             gx_ref, dqkv_ref, dfb_ref, dub_ref, dg_ref, db_ref, ybuf, ccar, dlog):
        i = pl.program_id(0)
        ti = nt - 1 - i

        @pl.when(i == 0)
        def _():
            ybuf[TS:n, :] = jnp.zeros((HALO, D_POOL), F32)
            ccar[...] = jnp.zeros_like(ccar)
            dg_ref[...] = jnp.zeros_like(dg_ref)
            db_ref[...] = jnp.zeros_like(db_ref)

        rr = lax.broadcasted_iota(jnp.int32, (TS, TS), 0)
        cc = lax.broadcasted_iota(jnp.int32, (TS, TS), 1)
        triu = (cc >= rr).astype(F32)
        dlog[...] = ccar[...] + jnp.dot(triu, drs_ref[...] - dcs_ref[...], precision=HIGHEST, preferred_element_type=F32)
        ccar[...] = dlog[0:1, :]
        df = dlog[...] * jax.nn.sigmoid(-fl_ref[...])
        db_ref[...] += jnp.sum(df, axis=0, keepdims=True)
        dfb = df.astype(BF16)
        dfb_ref[...] = dfb

        t = ti * TS + lax.broadcasted_iota(jnp.int32, (TS, 1), 0)
        dy = dy_ref[...]
        for g, w in enumerate(POOL_WINDOWS):
            cols = slice(g * POOL_CH, (g + 1) * POOL_CH)
            ybuf[0:TS, cols] = dy[:, cols] / jnp.minimum(t + 1, w).astype(F32)
        for g, w in enumerate(POOL_WINDOWS):
            cols = slice(g * POOL_CH, (g + 1) * POOL_CH)
            sm = ybuf[:, cols]
            step = 1
            while step < w:
                sm = sm + pltpu.roll(sm, n - step, 0)
                step *= 2
            dub_ref[:, cols] = (sm[0:TS, :] - dy[:, cols]).astype(BF16)
        ybuf[TS:n, :] = ybuf[0:HALO, :]

        for a in range(sub):
            dqkv_ref[a * TQ:(a + 1) * TQ, 0:D_ATTN] = (dqt_ref[a].T * 0.125).astype(BF16)
        dqkv_ref[:, D_ATTN:2 * D_ATTN] = dk_ref[...].astype(BF16)
        dqkv_ref[:, 2 * D_ATTN:] = dv_ref[...].astype(BF16)
        dhn = _nn(dqkv_ref[...], wqkv_ref[...]) + _nn(dfb, wf_ref[...]) + _nn(dub_ref[...], wu_ref[...])
        dx, dg = _rms_bwd(x_ref[...], g_ref[...], dhn)
        gx_ref[...] = dh1_ref[...] + dx
        dg_ref[...] += dg

    rev = lambda i: (nt - 1 - i, 0)
    blk = lambda w: pl.BlockSpec((TS, w), rev)
    return pl.pallas_call(
        body, grid=(nt,), name="pre_attn_bwd",
        out_shape=(jax.ShapeDtypeStruct((s, d), F32), jax.ShapeDtypeStruct((s, 3 * D_ATTN), BF16),
                   jax.ShapeDtypeStruct((s, LANES), BF16), jax.ShapeDtypeStruct((s, D_POOL), BF16),
                   jax.ShapeDtypeStruct((1, d), F32), jax.ShapeDtypeStruct((1, LANES), F32)),
        in_specs=[pl.BlockSpec((sub, D_ATTN, TQ), lambda i: (nt - 1 - i, 0, 0)),
                  blk(D_ATTN), blk(D_ATTN), blk(LANES), blk(LANES), blk(LANES), blk(D_POOL), blk(d), blk(d),
                  pl.BlockSpec((1, d), _fixed), pl.BlockSpec(wqkv.shape, _fixed), pl.BlockSpec(wf.shape, _fixed),
                  pl.BlockSpec(wu.shape, _fixed)],
        out_specs=(blk(d), blk(3 * D_ATTN), blk(LANES), blk(D_POOL),
                   pl.BlockSpec((1, d), _fixed), pl.BlockSpec((1, LANES), _fixed)),
        scratch_shapes=[pltpu.VMEM((n, D_POOL), F32), pltpu.VMEM((1, LANES), F32), pltpu.VMEM((TS, LANES), F32)],
        compiler_params=_params(1),
    )(dqt3, dk, dv, dcs, drs, fl, dy, x, dh1, g1, wqkv, wf, wu)


def _wgrad(a, b, out_dtype, name):
    s, m = a.shape
    n = b.shape[1]
    tm = next(t for t in (512, 256, 128) if m % t == 0)
    ns = s // TS

    def body(a_ref, b_ref, o_ref, acc):
        i = pl.program_id(1)

        @pl.when(i == 0)
        def _():
            acc[...] = jnp.zeros_like(acc)

        acc[...] += _tn(a_ref[...], b_ref[pl.ds(i * TS, TS), :])

        @pl.when(i == ns - 1)
        def _():
            o_ref[...] = acc[...].astype(out_dtype)

    return pl.pallas_call(
        body, grid=(m // tm, ns), name=name, out_shape=jax.ShapeDtypeStruct((m, n), out_dtype),
        in_specs=[pl.BlockSpec((TS, tm), lambda j, i: (i, j)), VMEM_WHOLE],
        out_specs=pl.BlockSpec((tm, n), lambda j, i: (j, 0)),
        scratch_shapes=[pltpu.VMEM((tm, n), F32)],
        compiler_params=_params(2),
    )(a, b)


def _adamw(w, g, m, v):
    m = ADAM_B1 * m + (1.0 - ADAM_B1) * g
    v = ADAM_B2 * v + (1.0 - ADAM_B2) * (g * g)
    m_hat = m / (1.0 - ADAM_B1 ** ADAM_STEP)
    v_hat = v / (1.0 - ADAM_B2 ** ADAM_STEP)
    delta = -ADAM_LR * (m_hat / (jnp.sqrt(v_hat) + ADAM_EPS) + ADAM_WD * w)
    return delta, m, v


def _pair_sum(core, t, theirs, tr, name):
    nk, r, c = theirs.shape

    def body(core_ref, a_ref, b_ref, o_ref):
        o_ref[...] = (a_ref[...].astype(F32) + b_ref[...].astype(F32)).astype(BF16)

    blk = pl.BlockSpec((1, tr, c), lambda k, i, core_ref: (k, i, 0))
    return pl.pallas_call(
        body, name=name, out_shape=jax.ShapeDtypeStruct(theirs.shape, BF16),
        grid_spec=pltpu.PrefetchScalarGridSpec(
            num_scalar_prefetch=1, grid=(nk, r // tr),
            in_specs=[pl.BlockSpec((1, tr, c), lambda k, i, core_ref: (2 * k + core_ref[0], i, 0)), blk],
            out_specs=blk),
        compiler_params=_params(2),
    )(core, t, theirs)


def _reduce_update_big(parts, w, m, v, tr, name):
    nk, r, c = parts.shape

    def body(p_ref, w_ref, m_ref, v_ref, g_ref, d_ref, nm_ref, nv_ref):
        g = p_ref[0].astype(F32)
        for k in range(1, nk):
            g = g + p_ref[k].astype(F32)
        g_ref[...] = g
        d_ref[...], nm_ref[...], nv_ref[...] = _adamw(w_ref[...], g, m_ref[...], v_ref[...])

    blk = pl.BlockSpec((tr, c), _row)
    out = jax.ShapeDtypeStruct((r, c), F32)
    return pl.pallas_call(
        body, grid=(r // tr,), name=name, out_shape=(out,) * 4,
        in_specs=[pl.BlockSpec((nk, tr, c), lambda i: (0, i, 0)), blk, blk, blk],
        out_specs=(blk,) * 4, compiler_params=_params(1),
    )(parts, w, m, v)


def _reduce_update_small(parts, w, m, v):
    nd = parts.shape[0]

    def body(p_ref, w_ref, m_ref, v_ref, g_ref, d_ref, nm_ref, nv_ref):
        g = p_ref[0]
        for k in range(1, nd):
            g = g + p_ref[k]
        g_ref[...] = g
        d_ref[...], nm_ref[...], nv_ref[...] = _adamw(w_ref[...], g, m_ref[...], v_ref[...])

    out = jax.ShapeDtypeStruct(w.shape, F32)
    return pl.pallas_call(body, name="reduce_update_small", out_shape=(out,) * 4,
                          compiler_params=pltpu.CompilerParams(vmem_limit_bytes=VMEM_LIMIT))(parts, w, m, v)


MESH = pl.DeviceIdType.MESH


def _copy_through_vmem(src_hbm, dst_hbm, stage, sem):
    load = pltpu.make_async_copy(src_hbm, stage, sem)
    load.start()
    load.wait()
    store = pltpu.make_async_copy(stage, dst_hbm, sem)
    store.start()
    store.wait()


class _GatherPlan:
    def __init__(self, x_ref, out_ref, send_sems, recv_sems):
        x, y, c = lax.axis_index("x"), lax.axis_index("y"), lax.axis_index("c")
        self.me, self.sibling, self.c = (x, y, c), (x, y, 1 - c), c
        self.chips = [(1 - x, y), (x, 1 - y), (1 - x, 1 - y)]
        self.x_ref, self.out_ref, self.send_sems, self.recv_sems = x_ref, out_ref, send_sems, recv_sems

    def slot(self, px, py, pc):
        return self.out_ref.at[4 * px + 2 * py + pc]

    def copy(self, k, block, to, src=None):
        return pltpu.make_async_remote_copy(
            src_ref=self.slot(*block) if src is None else src, dst_ref=self.slot(*block),
            send_sem=self.send_sems.at[k], recv_sem=self.recv_sems.at[k], device_id=to, device_id_type=MESH)

    def first(self):
        return [self.copy(0, self.me, self.sibling, src=self.x_ref)] + [
            self.copy(1 + j, self.me, (*chip, self.c), src=self.x_ref) for j, chip in enumerate(self.chips)]

    def passed(self):
        return [self.copy(4 + j, (*chip, self.c), self.sibling) for j, chip in enumerate(self.chips)]


def _gather_start(x_ref, out_ref, stage, send_sems, recv_sems, local_sem):
    plan = _GatherPlan(x_ref, out_ref, send_sems, recv_sems)
    for cp in plan.first():
        cp.start()
    _copy_through_vmem(x_ref, plan.slot(*plan.me), stage, local_sem)


def _gather_pass_on(out_ref, send_sems, recv_sems):
    plan = _GatherPlan(None, out_ref, send_sems, recv_sems)
    passed = plan.passed()
    for j, chip in enumerate(plan.chips):
        plan.copy(1 + j, (*chip, plan.c), plan.me).wait_recv()
        passed[j].start()


def _gather_finish(x_ref, out_ref, send_sems, recv_sems):
    plan = _GatherPlan(x_ref, out_ref, send_sems, recv_sems)
    plan.copy(0, plan.sibling, plan.me).wait_recv()
    for j, chip in enumerate(plan.chips):
        plan.copy(4 + j, (*chip, 1 - plan.c), plan.me).wait_recv()
    for cp in plan.first() + plan.passed():
        cp.wait_send()


def _all_gather(xs, name):
    r, cdim = xs.shape

    def body(x_ref, out_ref, stage, send_sems, recv_sems, local_sem):
        _gather_start(x_ref, out_ref, stage, send_sems, recv_sems, local_sem)
        _gather_pass_on(out_ref, send_sems, recv_sems)
        _gather_finish(x_ref, out_ref, send_sems, recv_sems)

    return pl.pallas_call(
        body, name=name, out_shape=jax.ShapeDtypeStruct((N_DEV, r, cdim), xs.dtype),
        in_specs=[ANY], out_specs=ANY,
        scratch_shapes=[pltpu.VMEM((r, cdim), xs.dtype), pltpu.SemaphoreType.DMA((7,)), pltpu.SemaphoreType.DMA((7,)),
                        pltpu.SemaphoreType.DMA],
        compiler_params=pltpu.CompilerParams(vmem_limit_bytes=VMEM_LIMIT),
    )(xs)


def _rs_pair(t, name):
    _, r, cdim = t.shape

    def body(t_ref, theirs_ref, send_sems, recv_sems):
        x, y, c = lax.axis_index("x"), lax.axis_index("y"), lax.axis_index("c")
        remote = [pltpu.make_async_remote_copy(
            src_ref=t_ref.at[2 * k + (1 - c)], dst_ref=theirs_ref.at[k],
            send_sem=send_sems.at[k], recv_sem=recv_sems.at[k], device_id=(x, y, 1 - c), device_id_type=MESH)
            for k in range(4)]
        for cp in remote:
            cp.start()
        for cp in remote:
            cp.wait()

    return pl.pallas_call(
        body, name=name, out_shape=jax.ShapeDtypeStruct((4, r, cdim), t.dtype), in_specs=[ANY], out_specs=ANY,
        scratch_shapes=[pltpu.SemaphoreType.DMA((4,)), pltpu.SemaphoreType.DMA((4,))],
    )(t)


def _chips_start(b_ref, out_ref, stage, send_sems, recv_sems, local_sem):
    x, y, c = lax.axis_index("x"), lax.axis_index("y"), lax.axis_index("c")
    mychip = 2 * x + y
    for j, (px, py) in enumerate([(1 - x, y), (x, 1 - y), (1 - x, 1 - y)]):
        pltpu.make_async_remote_copy(
            src_ref=b_ref.at[2 * px + py], dst_ref=out_ref.at[mychip],
            send_sem=send_sems.at[j], recv_sem=recv_sems.at[j], device_id=(px, py, c), device_id_type=MESH).start()
    _copy_through_vmem(b_ref.at[mychip], out_ref.at[mychip], stage, local_sem)


def _chips_finish(b_ref, out_ref, send_sems, recv_sems):
    x, y, c = lax.axis_index("x"), lax.axis_index("y"), lax.axis_index("c")
    for j, (px, py) in enumerate([(1 - x, y), (x, 1 - y), (1 - x, 1 - y)]):
        pltpu.make_async_remote_copy(
            src_ref=b_ref.at[2 * px + py], dst_ref=out_ref.at[2 * px + py],
            send_sem=send_sems.at[j], recv_sem=recv_sems.at[j], device_id=(px, py, c), device_id_type=MESH).wait()


def _rs_chips(b, name):
    _, r, cdim = b.shape

    def body(b_ref, out_ref, stage, send_sems, recv_sems, local_sem):
        _chips_start(b_ref, out_ref, stage, send_sems, recv_sems, local_sem)
        _chips_finish(b_ref, out_ref, send_sems, recv_sems)

    return pl.pallas_call(
        body, name=name, out_shape=jax.ShapeDtypeStruct(b.shape, b.dtype), in_specs=[ANY], out_specs=ANY,
        scratch_shapes=[pltpu.VMEM((r, cdim), b.dtype), pltpu.SemaphoreType.DMA((3,)), pltpu.SemaphoreType.DMA((3,)),
                        pltpu.SemaphoreType.DMA],
        compiler_params=pltpu.CompilerParams(vmem_limit_bytes=VMEM_LIMIT),
    )(b)


def _pad_rows(a, rows):
    return jnp.pad(a, ((0, rows - a.shape[0]), (0, 0)))


def _pack_in(w_in):
    return _pad_rows(w_in[0].T, ROWS_IN)


def _unpack_in(r):
    return r[0:SHARD_IN].T[None]


def _pack_rest(w_out, w_gate, w_up, w_down, w_ple, w_pg):
    return jnp.concatenate([w_out[0], w_gate[0].T, w_up[0].T, w_down[0], w_ple[0].T.reshape(32, D_MODEL), w_pg[0]],
                           axis=0)


def _unpack_rest(r):
    return (r[0:OFF_GATE][None], r[OFF_GATE:OFF_UP].T[None], r[OFF_UP:OFF_DOWN].T[None],
            r[OFF_DOWN:OFF_PLE][None], r[OFF_PLE:OFF_PG].reshape(128, D_PLE).T[None], r[OFF_PG:ROWS_REST][None])


def _full_rest(g):
    return (g[:, 0:OFF_GATE].reshape(D_MODEL, D_MODEL), g[:, OFF_GATE:OFF_UP].reshape(D_FF, D_MODEL),
            g[:, OFF_UP:OFF_DOWN].reshape(D_FF, D_MODEL), g[:, OFF_DOWN:OFF_PLE].reshape(D_FF, D_MODEL),
            g[:, OFF_PLE:OFF_PG].reshape(D_MODEL, D_PLE), g[:, OFF_PG:ROWS_REST].reshape(D_MODEL, D_MODEL))


def _pack_small(w_pool, g_mix_pre, g_mix_post, g_ffn_pre, g_ffn_post, g_ple, g_attn, g_pool, pool_scale, b_forget,
                loss=None):
    def row(vrow):
        return jnp.pad(vrow.reshape(1, -1), ((0, 0), (0, D_MODEL - vrow.size)))
    rows = [w_pool.reshape(64, D_MODEL), row(g_mix_pre), row(g_mix_post), row(g_ffn_pre), row(g_ffn_post), row(g_ple),
            row(g_attn), row(g_pool), row(pool_scale), row(b_forget),
            row(loss) if loss is not None else jnp.zeros((1, D_MODEL), F32)]
    return _pad_rows(jnp.concatenate(rows, axis=0), SMALL_ROWS)


def _unpack_small(r):
    return dict(
        w_pool=r[0:64].reshape(1, 4, POOL_CH, POOL_CH), g_mix_pre=r[ROW_G_MIX_PRE:ROW_G_MIX_PRE + 1],
        g_mix_post=r[ROW_G_MIX_POST:ROW_G_MIX_POST + 1], g_ffn_pre=r[ROW_G_FFN_PRE:ROW_G_FFN_PRE + 1],
        g_ffn_post=r[ROW_G_FFN_POST:ROW_G_FFN_POST + 1], g_ple=r[ROW_G_PLE:ROW_G_PLE + 1],
        g_attn_grp=r[ROW_G_ATTN:ROW_G_ATTN + 1, 0:D_ATTN], g_pool_grp=r[ROW_G_POOL:ROW_G_POOL + 1, 0:D_POOL],
        pool_scale=r[ROW_POOL_SCALE:ROW_POOL_SCALE + 1, 0:D_POOL], b_forget=r[ROW_B_FORGET:ROW_B_FORGET + 1, 0:HEADS])


def _step(x, p, tgt, small, in_w, in_m, in_v, rest_w, rest_m, rest_v):
    core = lax.axis_index("c").astype(jnp.int32).reshape(1)
    win_t = _all_gather(in_w.astype(BF16), "gather_w_in")[:, 0:SHARD_IN].reshape(D_IN, D_MODEL)
    wqkv = win_t[0:3 * D_ATTN]
    wf = _pad_rows(win_t[3 * D_ATTN:3 * D_ATTN + HEADS], LANES)
    wu = win_t[3 * D_ATTN + HEADS:]
    wpool = small["w_pool"].astype(BF16)
    bpad = jnp.pad(small["b_forget"], ((0, 0), (0, LANES - HEADS)))

    lay = _attn_layout_constants()
    hn, q, ka, v, qat3, vt3, kt3, fl, y, mpre = _pre_attn_fwd(x, small["g_mix_pre"], wqkv, wf, wu, bpad, wpool, lay)
    a, lset3, gathered = _attn_fwd(ka, qat3, vt3, rest_w.astype(BF16))
    wout, wg_t, wu_t, wd, wple_t, wpg = _full_rest(gathered)
    mix, o, h1, hn2 = _post_attn_fwd(a, mpre, x, small["g_attn_grp"], small["g_pool_grp"], small["pool_scale"], wout,
                                     small["g_mix_post"], small["g_ffn_pre"])
    gate, up, act, ff, h2 = _ffn_fwd(hn2, wg_t, wu_t, wd, h1, small["g_ffn_post"])
    dh2, dff, dgl, dpp, h2b, pb, loss8, dg_ple, dg_ffn_post = _tail_fwd_bwd(
        h2, p, tgt, ff, wple_t, wpg, small["g_ple"], small["g_ffn_post"])
    dgate, dup, dh1, dg_ffn_pre = _ffn_bwd(dff, gate, up, wd, wg_t, wu_t, h1, dh2, small["g_ffn_pre"])
    dob, dab, dat3, dlt3, dmpb, dy, dg_mix_post, dg_attn, dg_pool, dps = _post_attn_bwd(
        dh1, o, a, mpre, wout, wpool, small["g_mix_post"], small["g_attn_grp"], small["g_pool_grp"], small["pool_scale"],
        lay["eye"])

    nd = N_DEV
    send_rest = jnp.concatenate([
        _wgrad(mix, dob, BF16, "wgrad_out").reshape(nd, 128, D_MODEL),
        _wgrad(dgate, hn2, BF16, "wgrad_gate").reshape(nd, 352, D_MODEL),
        _wgrad(dup, hn2, BF16, "wgrad_up").reshape(nd, 352, D_MODEL),
        _wgrad(act, dff, BF16, "wgrad_down").reshape(nd, 352, D_MODEL),
        _wgrad(dpp, pb, BF16, "wgrad_ple").reshape(nd, 32, D_MODEL),
        _wgrad(h2b, dgl, BF16, "wgrad_ple_gate").reshape(nd, 128, D_MODEL)], axis=1)
    pair_rest = _pair_sum(core, send_rest, _rs_pair(send_rest, "rs_pair_rest"), TR_REST, "rs_pair_sum_rest")

    dqt3, dk, dv, dcs, drs4, chips_rest = _attn_bwd(ka, v, kt3, qat3, q, dab, dat3, lset3, dlt3, pair_rest)
    upd_rest = _reduce_update_big(chips_rest, rest_w, rest_m, rest_v, TR_REST, "reduce_update_rest")
    drs = jnp.pad(drs4[:, :, 0, :].transpose(0, 2, 1).reshape(-1, HEADS), ((0, 0), (0, LANES - HEADS)))
    gx, dqkv, dfb, dub, dg_mix_pre, db = _pre_attn_bwd(dqt3, dk, dv, dcs, drs, fl, dy, x, dh1, small["g_mix_pre"], wqkv, wf, wu)

    dwin_t = jnp.concatenate([_wgrad(dqkv, hn, F32, "wgrad_qkv"), _wgrad(dfb, hn, F32, "wgrad_forget")[0:HEADS],
                              _wgrad(dub, hn, F32, "wgrad_pool_in")], axis=0)
    send_in = jnp.pad(dwin_t.reshape(nd, SHARD_IN, D_MODEL), ((0, 0), (0, ROWS_IN - SHARD_IN), (0, 0))).astype(BF16)
    pair_in = _pair_sum(core, send_in, _rs_pair(send_in, "rs_pair_in"), ROWS_IN, "rs_pair_sum_in")
    upd_in = _reduce_update_big(_rs_chips(pair_in, "rs_chips_in"), in_w, in_m, in_v, ROWS_IN, "reduce_update_in")

    dwp = _wgrad(y, dmpb, F32, "wgrad_pool")
    dw_pool = jnp.stack([dwp[g * POOL_CH:(g + 1) * POOL_CH, g * POOL_CH:(g + 1) * POOL_CH] for g in range(4)])
    small_part = _pack_small(dw_pool, dg_mix_pre, dg_mix_post, dg_ffn_pre, dg_ffn_post, dg_ple, dg_attn, dg_pool, dps,
                             db[:, 0:HEADS], loss8[0:1, 0:1])
    return gx, small_part, upd_in, upd_rest


def kernel(x, p, g_mix_pre, w_in, b_forget, g_attn_grp, g_pool_grp, w_pool, pool_scale, w_out, g_mix_post, g_ffn_pre, w_ffn_gate, w_ffn_up, w_ffn_down, g_ffn_post, w_ple_proj, g_ple, w_ple_gate, loss_target, m_g_mix_pre, m_w_in, m_b_forget, m_g_attn_grp, m_g_pool_grp, m_w_pool, m_pool_scale, m_w_out, m_g_mix_post, m_g_ffn_pre, m_w_ffn_gate, m_w_ffn_up, m_w_ffn_down, m_g_ffn_post, m_w_ple_proj, m_g_ple, m_w_ple_gate, v_g_mix_pre, v_w_in, v_b_forget, v_g_attn_grp, v_g_pool_grp, v_w_pool, v_pool_scale, v_w_out, v_g_mix_post, v_g_ffn_pre, v_w_ffn_gate, v_w_ffn_up, v_w_ffn_down, v_g_ffn_post, v_w_ple_proj, v_g_ple, v_w_ple_gate):
    small = dict(w_pool=w_pool[0], g_mix_pre=g_mix_pre, g_mix_post=g_mix_post, g_ffn_pre=g_ffn_pre,
                 g_ffn_post=g_ffn_post, g_ple=g_ple, g_attn_grp=g_attn_grp, g_pool_grp=g_pool_grp,
                 pool_scale=pool_scale, b_forget=b_forget)
    gx, small_part, upd_in, upd_rest = _step(
        x[0], p[0, 0], loss_target[0], small, _pack_in(w_in), _pack_in(m_w_in), _pack_in(v_w_in),
        _pack_rest(w_out, w_ffn_gate, w_ffn_up, w_ffn_down, w_ple_proj, w_ple_gate),
        _pack_rest(m_w_out, m_w_ffn_gate, m_w_ffn_up, m_w_ffn_down, m_w_ple_proj, m_w_ple_gate),
        _pack_rest(v_w_out, v_w_ffn_gate, v_w_ffn_up, v_w_ffn_down, v_w_ple_proj, v_w_ple_gate))

    small_all = _all_gather(small_part, "gather_small")
    sm_w = _pack_small(w_pool, g_mix_pre, g_mix_post, g_ffn_pre, g_ffn_post, g_ple, g_attn_grp, g_pool_grp, pool_scale, b_forget)
    sm_m = _pack_small(m_w_pool, m_g_mix_pre, m_g_mix_post, m_g_ffn_pre, m_g_ffn_post, m_g_ple, m_g_attn_grp, m_g_pool_grp, m_pool_scale, m_b_forget)
    sm_v = _pack_small(v_w_pool, v_g_mix_pre, v_g_mix_post, v_g_ffn_pre, v_g_ffn_post, v_g_ple, v_g_attn_grp, v_g_pool_grp, v_pool_scale, v_b_forget)
    upd_small = _reduce_update_small(small_all, sm_w, sm_m, sm_v)
    loss = upd_small[0][ROW_LOSS, 0]

    def leaves(k):
        b_out, b_gate, b_up, b_down, b_ple, b_pg = _unpack_rest(upd_rest[k])
        s = _unpack_small(upd_small[k])
        return (s["g_mix_pre"], _unpack_in(upd_in[k]), s["b_forget"], s["g_attn_grp"], s["g_pool_grp"], s["w_pool"],
                s["pool_scale"], b_out, s["g_mix_post"], s["g_ffn_pre"], b_gate, b_up, b_down, s["g_ffn_post"], b_ple,
                s["g_ple"], b_pg)

    return (loss, gx[None], *leaves(0), *leaves(1), *leaves(2), *leaves(3))
```

```python
import functools

import jax
import jax.numpy as jnp
from jax import lax
from jax.experimental import pallas as pl
from jax.experimental.pallas import tpu as pltpu

F32 = jnp.float32
BF16 = jnp.bfloat16
HIGHEST = lax.Precision.HIGHEST

D_MODEL = 1024
HEADS = 8
HEAD_DIM = 64
D_ATTN = HEADS * HEAD_DIM
POOL_WINDOWS = (2, 4, 8, 16)
POOL_CH = 128
D_POOL = POOL_CH * len(POOL_WINDOWS)
D_FF = 2816
D_PLE = 256
D_IN = 3 * D_ATTN + HEADS + D_POOL
RMS_EPS = 1e-6
N_DEV = 8

ADAM_LR = 0.001
ADAM_B1 = 0.9
ADAM_B2 = 0.999
ADAM_EPS = 1e-08
ADAM_WD = 0.01
ADAM_STEP = 10

LANES = 128
HALO = 16
TS = 512
TS_FF = 1024
TM_WGRAD = 1536
TQ = 256
TN_FF = 256
NEG = -1e30
VMEM_LIMIT = 56 * 1024 * 1024

SHARD_IN = 257
ROWS_IN = 272
OFF_GATE = 128
OFF_UP = OFF_GATE + 352
OFF_DOWN = OFF_UP + 352
OFF_PLE = OFF_DOWN + 352
OFF_PG = OFF_PLE + 32
ROWS_REST = OFF_PG + 128
TR_REST = 192

SMALL_ROWS = 80
ROW_G_MIX_PRE, ROW_G_MIX_POST, ROW_G_FFN_PRE, ROW_G_FFN_POST, ROW_G_PLE = 64, 65, 66, 67, 68
ROW_G_ATTN, ROW_G_POOL, ROW_POOL_SCALE, ROW_B_FORGET, ROW_LOSS = 69, 70, 71, 72, 73


def _nn(a, b):
    return jnp.dot(a, b, preferred_element_type=F32)


def _nt(a, b):
    return lax.dot_general(a, b, (((1,), (1,)), ((), ())), preferred_element_type=F32)


def _tn(a, b):
    return lax.dot_general(a, b, (((0,), (0,)), ((), ())), preferred_element_type=F32)


def _rstd(v):
    return lax.rsqrt(jnp.mean(v * v, axis=-1, keepdims=True) + RMS_EPS)


def _rms_bwd(v, g, dy):
    r = _rstd(v)
    vh = v * r
    t = dy * g
    dv = r * (t - vh * jnp.mean(t * vh, axis=-1, keepdims=True))
    return dv, jnp.sum(dy * vh, axis=0, keepdims=True)


def _params(n_grid):
    return pltpu.CompilerParams(dimension_semantics=("arbitrary",) * n_grid, vmem_limit_bytes=VMEM_LIMIT)


def _row(i):
    return (i, 0)


def _fixed(*_):
    return (0, 0)


VMEM_WHOLE = pl.BlockSpec(memory_space=pltpu.VMEM)
SMEM_WHOLE = pl.BlockSpec(memory_space=pltpu.SMEM)
ANY = pl.BlockSpec(memory_space=pl.ANY)


AUG = 128
BIAS_LANE = HEAD_DIM
ONE_LANE = HEAD_DIM + 3
SPARE_LANE = HEADS


def _attn_layout_constants():
    import numpy as np
    place = np.zeros((D_ATTN, HEADS * AUG), np.float32)
    for r in range(D_ATTN):
        place[r, (r // HEAD_DIM) * AUG + r % HEAD_DIM] = 1.0
    bias_k = np.zeros((3, LANES, HEADS * AUG), np.float32)
    bias_q = np.zeros((3, LANES, HEADS * AUG), np.float32)
    for h in range(HEADS):
        for part in range(3):
            bias_k[part, h, h * AUG + BIAS_LANE + part] = -1.0
            bias_q[part, h, h * AUG + ONE_LANE + part] = 1.0
            bias_k[0, SPARE_LANE, h * AUG + ONE_LANE + part] = 1.0
            bias_q[0, SPARE_LANE, h * AUG + BIAS_LANE + part] = 1.0
    as_bf = lambda a: jnp.asarray(a, BF16)
    return dict(place=as_bf(place), place_t=as_bf(place.T), bias_k=as_bf(bias_k),
                bias_q_t=as_bf(bias_q.transpose(0, 2, 1)), eye=as_bf(np.eye(D_ATTN, dtype=np.float32)))


def _pre_attn_fwd(x, g1, wqkv, wf, wu, bpad, wpool, lay):
    s, d = x.shape
    nt = s // TS
    sub = TS // TQ

    def body(x_ref, g_ref, wqkv_ref, wf_ref, wu_ref, b_ref, wp_ref, place_ref, place_t_ref, bk_ref, bqt_ref, eye_ref,
             hn_ref, q_ref, ka_ref, v_ref, qat_ref, vt_ref, kt_ref, fl_ref, y_ref, mp_ref, ubuf, ccar, cbuf):
        i = pl.program_id(0)

        @pl.when(i == 0)
        def _():
            ubuf[0:HALO, :] = jnp.zeros((HALO, D_POOL), F32)
            ccar[...] = jnp.zeros_like(ccar)

        xv = x_ref[...]
        hn = (xv * _rstd(xv) * g_ref[...]).astype(BF16)
        hn_ref[...] = hn
        zq = _nt(hn, wqkv_ref[...])
        qb = (zq[:, 0:D_ATTN] * 0.125).astype(BF16)
        kb = zq[:, D_ATTN:2 * D_ATTN].astype(BF16)
        vb = zq[:, 2 * D_ATTN:3 * D_ATTN].astype(BF16)
        q_ref[...] = qb
        v_ref[...] = vb

        fl = _nt(hn, wf_ref[...]) + b_ref[...]
        fl_ref[...] = fl
        logf = jax.nn.log_sigmoid(fl)
        rr = lax.broadcasted_iota(jnp.int32, (TS, TS), 0)
        cc = lax.broadcasted_iota(jnp.int32, (TS, TS), 1)
        tril = (cc <= rr).astype(F32)
        c = jnp.dot(tril, logf, precision=HIGHEST, preferred_element_type=F32) + ccar[...]
        cbuf[...] = c
        ccar[...] = cbuf[TS - 1:TS, :]
        hi = c.astype(BF16)
        rest = c - hi.astype(F32)
        mid = rest.astype(BF16)
        lo = (rest - mid.astype(F32)).astype(BF16)
        lane = lax.broadcasted_iota(jnp.int32, (TS, LANES), 1)
        parts = (jnp.where(lane == SPARE_LANE, 1.0, hi).astype(BF16), mid, lo)
        ka = _nn(kb, place_ref[...])
        qat = _nt(place_t_ref[...], qb)
        for part in range(3):
            ka = ka + _nn(parts[part], bk_ref[part])
            qat = qat + _nt(bqt_ref[part], parts[part])
        ka_ref[...] = ka.astype(BF16)
        qat = qat.astype(BF16)
        vt = _nt(eye_ref[...], vb).astype(BF16)
        kt = _nt(eye_ref[...], kb).astype(BF16)
        for a in range(sub):
            qat_ref[a] = qat[:, a * TQ:(a + 1) * TQ]
            vt_ref[a] = vt[:, a * TQ:(a + 1) * TQ]
            kt_ref[a] = kt[:, a * TQ:(a + 1) * TQ]

        u = _nt(hn, wu_ref[...])
        ubuf[HALO:HALO + TS, :] = u
        t = i * TS + lax.broadcasted_iota(jnp.int32, (TS, 1), 0)
        for g, w in enumerate(POOL_WINDOWS):
            cols = slice(g * POOL_CH, (g + 1) * POOL_CH)
            sm = ubuf[:, cols]
            step = 1
            while step < w:
                sm = sm + pltpu.roll(sm, step, 0)
                step *= 2
            cnt = jnp.minimum(t + 1, w).astype(F32)
            yg = (sm[HALO:, :] / cnt - u[:, cols]).astype(BF16)
            y_ref[:, cols] = yg
            mp_ref[:, cols] = _nn(yg, wp_ref[g])
        ubuf[0:HALO, :] = u[TS - HALO:, :]

    nq = s // TQ
    aug = HEADS * AUG
    outs = (
        jax.ShapeDtypeStruct((s, d), BF16), jax.ShapeDtypeStruct((s, D_ATTN), BF16),
        jax.ShapeDtypeStruct((s, aug), BF16), jax.ShapeDtypeStruct((s, D_ATTN), BF16),
        jax.ShapeDtypeStruct((nq, aug, TQ), BF16), jax.ShapeDtypeStruct((nq, D_ATTN, TQ), BF16),
        jax.ShapeDtypeStruct((nq, D_ATTN, TQ), BF16),
        jax.ShapeDtypeStruct((s, LANES), F32),
        jax.ShapeDtypeStruct((s, D_POOL), BF16), jax.ShapeDtypeStruct((s, D_POOL), F32),
    )
    fixed3 = lambda i: (0, 0, 0)
    tiles3 = lambda rows: pl.BlockSpec((sub, rows, TQ), lambda i: (i, 0, 0))
    return pl.pallas_call(
        body, grid=(nt,), out_shape=outs, name="pre_attn_fwd",
        in_specs=[pl.BlockSpec((TS, d), _row), pl.BlockSpec((1, d), _fixed),
                  pl.BlockSpec(wqkv.shape, _fixed), pl.BlockSpec(wf.shape, _fixed), pl.BlockSpec(wu.shape, _fixed),
                  pl.BlockSpec((1, LANES), _fixed), pl.BlockSpec(wpool.shape, fixed3),
                  pl.BlockSpec(lay["place"].shape, _fixed), pl.BlockSpec(lay["place_t"].shape, _fixed),
                  pl.BlockSpec(lay["bias_k"].shape, fixed3), pl.BlockSpec(lay["bias_q_t"].shape, fixed3),
                  pl.BlockSpec(lay["eye"].shape, _fixed)],
        out_specs=(pl.BlockSpec((TS, d), _row), pl.BlockSpec((TS, D_ATTN), _row),
                   pl.BlockSpec((TS, aug), _row), pl.BlockSpec((TS, D_ATTN), _row),
                   tiles3(aug), tiles3(D_ATTN), tiles3(D_ATTN),
                   pl.BlockSpec((TS, LANES), _row),
                   pl.BlockSpec((TS, D_POOL), _row), pl.BlockSpec((TS, D_POOL), _row)),
        scratch_shapes=[pltpu.VMEM((TS + HALO, D_POOL), F32), pltpu.VMEM((1, LANES), F32), pltpu.VMEM((TS, LANES), F32)],
        compiler_params=_params(1),
    )(x, g1, wqkv, wf, wu, bpad, wpool, lay["place"], lay["place_t"], lay["bias_k"], lay["bias_q_t"], lay["eye"])


def _causal_in_tile():
    krow = lax.broadcasted_iota(jnp.int32, (TQ, TQ), 0)
    qcol = lax.broadcasted_iota(jnp.int32, (TQ, TQ), 1)
    return krow <= qcol


def _attn_fwd(ka, qat3, vt3, own_block):
    s = ka.shape[0]
    nq = s // TQ
    pass_on_step = max(nq - 2, 0)

    def body(qa_ref, ka_ref, vt_ref, own_ref, a_ref, lset_ref, all_ref, acc, st_scr, pt_scr,
             stage, send_sems, recv_sems, local_sem):
        i = pl.program_id(0)

        @pl.when(i == 0)
        def _():
            _gather_start(own_ref, all_ref, stage, send_sems, recv_sems, local_sem)

        @pl.when(i == pass_on_step)
        def _():
            _gather_pass_on(all_ref, send_sems, recv_sems)

        acc[...] = jnp.zeros_like(acc)

        def tile(j, stats, masked):
            tile_max = []
            for h in range(HEADS):
                aug = slice(h * AUG, (h + 1) * AUG)
                st = _nn(ka_ref[pl.ds(j * TQ, TQ), aug], qa_ref[0, aug, :])
                if masked:
                    st = jnp.where(_causal_in_tile(), st, NEG)
                st_scr[h] = st
                tile_max.append(jnp.max(st, axis=0, keepdims=True))
            new, scale = [], []
            for h in range(HEADS):
                m_old, l_old = stats[h]
                m_new = jnp.maximum(m_old, tile_max[h])
                al = jnp.exp(m_old - m_new)
                pt = jnp.exp(st_scr[h] - m_new)
                pt_scr[h] = pt.astype(BF16)
                new.append((m_new, al * l_old + jnp.sum(pt, axis=0, keepdims=True)))
                scale.append(al)
            for h in range(HEADS):
                rows = slice(h * HEAD_DIM, (h + 1) * HEAD_DIM)
                acc[rows, :] = scale[h] * acc[rows, :] + _nn(vt_ref[j, rows, :], pt_scr[h])
            return tuple(new)

        init = tuple((jnp.full((1, TQ), NEG, F32), jnp.zeros((1, TQ), F32)) for _ in range(HEADS))
        stats = lax.fori_loop(0, i, functools.partial(tile, masked=False), init)
        stats = tile(i, stats, True)
        for h in range(HEADS):
            rows = slice(h * HEAD_DIM, (h + 1) * HEAD_DIM)
            acc[rows, :] = acc[rows, :] / stats[h][1]
            lset_ref[0, h:h + 1, :] = stats[h][0] + jnp.log(stats[h][1])
        a_ref[...] = acc[...].T

        @pl.when(i == nq - 1)
        def _():
            _gather_finish(own_ref, all_ref, send_sems, recv_sems)

    r, cdim = own_block.shape
    return pl.pallas_call(
        body, grid=(nq,), name="attn_fwd",
        out_shape=(jax.ShapeDtypeStruct((s, D_ATTN), F32), jax.ShapeDtypeStruct((nq, HEADS, TQ), F32),
                   jax.ShapeDtypeStruct((N_DEV, r, cdim), own_block.dtype)),
        in_specs=[pl.BlockSpec((1, HEADS * AUG, TQ), lambda i: (i, 0, 0)), VMEM_WHOLE, VMEM_WHOLE, ANY],
        out_specs=(pl.BlockSpec((TQ, D_ATTN), _row), pl.BlockSpec((1, HEADS, TQ), lambda i: (i, 0, 0)), ANY),
        scratch_shapes=[pltpu.VMEM((D_ATTN, TQ), F32), pltpu.VMEM((HEADS, TQ, TQ), F32), pltpu.VMEM((HEADS, TQ, TQ), BF16),
                        pltpu.VMEM((r, cdim), own_block.dtype),
                        pltpu.SemaphoreType.DMA((7,)), pltpu.SemaphoreType.DMA((7,)), pltpu.SemaphoreType.DMA],
        compiler_params=_params(1),
    )(qat3, ka, vt3, own_block)


def _post_attn_fwd(a, mpre, x, g_attn, g_pool, pscale, wout, g_post, g_ffn_pre):
    s, d = x.shape

    def body(a_ref, mp_ref, x_ref, ga_ref, gp_ref, ps_ref, wo_ref, gpost_ref, gpre_ref,
             mix_ref, o_ref, h1_ref, hn2_ref):
        av = a_ref[...]
        mix_ref[:, 0:D_ATTN] = (av * _rstd(av) * ga_ref[...]).astype(BF16)
        mv = mp_ref[...] * ps_ref[...]
        mix_ref[:, D_ATTN:] = (mv * _rstd(mv) * gp_ref[...]).astype(BF16)
        o = _nn(mix_ref[...], wo_ref[...])
        o_ref[...] = o
        h1 = x_ref[...] + o * _rstd(o) * gpost_ref[...]
        h1_ref[...] = h1
        hn2_ref[...] = (h1 * _rstd(h1) * gpre_ref[...]).astype(BF16)

    vec = lambda n: pl.BlockSpec((1, n), _fixed)
    return pl.pallas_call(
        body, grid=(s // TS,), name="post_attn_fwd",
        out_shape=(jax.ShapeDtypeStruct((s, d), BF16), jax.ShapeDtypeStruct((s, d), F32),
                   jax.ShapeDtypeStruct((s, d), F32), jax.ShapeDtypeStruct((s, d), BF16)),
        in_specs=[pl.BlockSpec((TS, D_ATTN), _row), pl.BlockSpec((TS, D_POOL), _row), pl.BlockSpec((TS, d), _row),
                  vec(D_ATTN), vec(D_POOL), vec(D_POOL), pl.BlockSpec(wout.shape, _fixed), vec(d), vec(d)],
        out_specs=(pl.BlockSpec((TS, d), _row),) * 4,
        compiler_params=_params(1),
    )(a, mpre, x, g_attn, g_pool, pscale, wout, g_post, g_ffn_pre)


def _ffn_fwd(hn2, wg, wu, wd, h1, g_post):
    s, d = h1.shape
    nc = D_FF // TN_FF
    ts = min(TS_FF, s)

    def body(hn_ref, wg_ref, wu_ref, wd_ref, h1_ref, g_ref, gate_ref, up_ref, act_ref, ff_ref, h2_ref, acc):
        j = pl.program_id(1)

        @pl.when(j == 0)
        def _():
            acc[...] = jnp.zeros_like(acc)

        hn = hn_ref[...]
        gt = _nt(hn, wg_ref[...])
        up = _nt(hn, wu_ref[...])
        act = (gt * jax.nn.sigmoid(gt) * up).astype(BF16)
        gate_ref[...] = gt.astype(BF16)
        up_ref[...] = up.astype(BF16)
        act_ref[...] = act
        acc[...] += _nn(act, wd_ref[...])

        @pl.when(j == nc - 1)
        def _():
            ff = acc[...]
            ff_ref[...] = ff
            h2_ref[...] = h1_ref[...] + ff * _rstd(ff) * g_ref[...]

    rowblk = pl.BlockSpec((ts, d), lambda i, j: (i, 0))
    wblk = pl.BlockSpec((TN_FF, d), lambda i, j: (j, 0))
    chunk = pl.BlockSpec((ts, TN_FF), lambda i, j: (i, j))
    return pl.pallas_call(
        body, grid=(s // ts, nc), name="ffn_fwd",
        out_shape=(jax.ShapeDtypeStruct((s, D_FF), BF16),) * 3 + (jax.ShapeDtypeStruct((s, d), F32),) * 2,
        in_specs=[rowblk, wblk, wblk, wblk, rowblk, pl.BlockSpec((1, d), lambda i, j: (0, 0))],
        out_specs=(chunk, chunk, chunk, rowblk, rowblk),
        scratch_shapes=[pltpu.VMEM((ts, d), F32)],
        compiler_params=_params(2),
    )(hn2, wg, wu, wd, h1, g_post)


def _tail_fwd_bwd(h2, p, tgt, ff, wple, wpg, g_ple, g_ffn_post):
    s, d = h2.shape

    def body(h2_ref, p_ref, t_ref, ff_ref, wple_ref, wpg_ref, gple_ref, gfp_ref,
             dh2_ref, dff_ref, dgl_ref, dpp_ref, h2b_ref, pb_ref, loss_ref, dgple_ref, dgfp_ref):
        i = pl.program_id(0)

        @pl.when(i == 0)
        def _():
            loss_ref[...] = jnp.zeros_like(loss_ref)
            dgple_ref[...] = jnp.zeros_like(dgple_ref)
            dgfp_ref[...] = jnp.zeros_like(dgfp_ref)

        h2 = h2_ref[...]
        h2b = h2.astype(BF16)
        h2b_ref[...] = h2b
        pb = p_ref[...].astype(BF16)
        pb_ref[...] = pb
        pp = _nt(pb, wple_ref[...])
        gple = gple_ref[...]
        e = pp * _rstd(pp) * gple
        sg = jax.nn.sigmoid(_nn(h2b, wpg_ref[...]))
        diff = h2 + sg * e - t_ref[...]
        sq = jnp.sum(jnp.sum(diff * diff, axis=1, keepdims=True), axis=0, keepdims=True)
        loss_ref[...] += jnp.broadcast_to(sq * (0.5 / d), loss_ref.shape)
        dh3 = diff * (1.0 / d)
        dgl = (dh3 * e * sg * (1.0 - sg)).astype(BF16)
        dgl_ref[...] = dgl
        dh2 = dh3 + _nt(dgl, wpg_ref[...])
        dh2_ref[...] = dh2
        dpp, dg = _rms_bwd(pp, gple, dh3 * sg)
        dpp_ref[...] = dpp.astype(BF16)
        dgple_ref[...] += dg
        dff, dg = _rms_bwd(ff_ref[...], gfp_ref[...], dh2)
        dff_ref[...] = dff.astype(BF16)
        dgfp_ref[...] += dg

    rowblk = pl.BlockSpec((TS, d), _row)
    vec = pl.BlockSpec((1, d), _fixed)
    return pl.pallas_call(
        body, grid=(s // TS,), name="tail_fwd_bwd",
        out_shape=(jax.ShapeDtypeStruct((s, d), F32), jax.ShapeDtypeStruct((s, d), BF16),
                   jax.ShapeDtypeStruct((s, d), BF16), jax.ShapeDtypeStruct((s, d), BF16),
                   jax.ShapeDtypeStruct((s, d), BF16), jax.ShapeDtypeStruct((s, D_PLE), BF16),
                   jax.ShapeDtypeStruct((8, LANES), F32), jax.ShapeDtypeStruct((1, d), F32),
                   jax.ShapeDtypeStruct((1, d), F32)),
        in_specs=[rowblk, pl.BlockSpec((TS, D_PLE), _row), rowblk, rowblk,
                  pl.BlockSpec(wple.shape, _fixed), pl.BlockSpec(wpg.shape, _fixed), vec, vec],
        out_specs=(rowblk, rowblk, rowblk, rowblk, rowblk, pl.BlockSpec((TS, D_PLE), _row),
                   pl.BlockSpec((8, LANES), _fixed), vec, vec),
        compiler_params=_params(1),
    )(h2, p, tgt, ff, wple, wpg, g_ple, g_ffn_post)


def _ffn_bwd(dff, gate, up, wd, wg, wu, h1, dh2, g_pre):
    s, d = h1.shape
    nc = D_FF // TN_FF
    ts = min(TS_FF, s)

    def body(dff_ref, gate_ref, up_ref, wd_ref, wg_ref, wu_ref, h1_ref, dh2_ref, g_ref,
             dgate_ref, dup_ref, dh1_ref, dg_ref, acc):
        i = pl.program_id(0)
        j = pl.program_id(1)

        @pl.when((i == 0) & (j == 0))
        def _():
            dg_ref[...] = jnp.zeros_like(dg_ref)

        @pl.when(j == 0)
        def _():
            acc[...] = jnp.zeros_like(acc)

        dact = _nt(dff_ref[...], wd_ref[...])
        gt = gate_ref[...].astype(F32)
        sg = jax.nn.sigmoid(gt)
        dup = (dact * gt * sg).astype(BF16)
        dgate = (dact * up_ref[...].astype(F32) * (sg * (1.0 + gt * (1.0 - sg)))).astype(BF16)
        dgate_ref[...] = dgate
        dup_ref[...] = dup
        acc[...] += _nn(dgate, wg_ref[...]) + _nn(dup, wu_ref[...])

        @pl.when(j == nc - 1)
        def _():
            dv, dg = _rms_bwd(h1_ref[...], g_ref[...], acc[...])
            dh1_ref[...] = dh2_ref[...] + dv
            dg_ref[...] += dg

    rowblk = pl.BlockSpec((ts, d), lambda i, j: (i, 0))
    wblk = pl.BlockSpec((TN_FF, d), lambda i, j: (j, 0))
    chunk = pl.BlockSpec((ts, TN_FF), lambda i, j: (i, j))
    vec = pl.BlockSpec((1, d), lambda i, j: (0, 0))
    return pl.pallas_call(
        body, grid=(s // ts, nc), name="ffn_bwd",
        out_shape=(jax.ShapeDtypeStruct((s, D_FF), BF16), jax.ShapeDtypeStruct((s, D_FF), BF16),
                   jax.ShapeDtypeStruct((s, d), F32), jax.ShapeDtypeStruct((1, d), F32)),
        in_specs=[rowblk, chunk, chunk, wblk, wblk, wblk, rowblk, rowblk, vec],
        out_specs=(chunk, chunk, rowblk, vec),
        scratch_shapes=[pltpu.VMEM((ts, d), F32)],
        compiler_params=_params(2),
    )(dff, gate, up, wd, wg, wu, h1, dh2, g_pre)


def _post_attn_bwd(dh1, o, a, mpre, wout, wpool, g_post, g_attn, g_pool, pscale, eye):
    s, d = dh1.shape
    sub = TS // TQ

    def body(dh1_ref, o_ref, a_ref, mp_ref, wo_ref, wp_ref, gpost_ref, ga_ref, gp_ref, ps_ref, eye_ref,
             dob_ref, dab_ref, dat_ref, dlt_ref, dmpb_ref, dy_ref, dgpost_ref, dga_ref, dgp_ref, dps_ref):
        i = pl.program_id(0)

        @pl.when(i == 0)
        def _():
            dgpost_ref[...] = jnp.zeros_like(dgpost_ref)
            dga_ref[...] = jnp.zeros_like(dga_ref)
            dgp_ref[...] = jnp.zeros_like(dgp_ref)
            dps_ref[...] = jnp.zeros_like(dps_ref)

        do, dg = _rms_bwd(o_ref[...], gpost_ref[...], dh1_ref[...])
        dgpost_ref[...] += dg
        dob = do.astype(BF16)
        dob_ref[...] = dob
        dmix = _nt(dob, wo_ref[...])

        av = a_ref[...]
        da, dg = _rms_bwd(av, ga_ref[...], dmix[:, 0:D_ATTN])
        dga_ref[...] += dg
        dab = da.astype(BF16)
        dab_ref[...] = dab
        dat = _nt(eye_ref[...], dab).astype(BF16)
        hsel = (lax.shift_right_logical(lax.broadcasted_iota(jnp.int32, (HEADS, D_ATTN), 1), 6)
                == lax.broadcasted_iota(jnp.int32, (HEADS, D_ATTN), 0)).astype(F32)
        dlt = lax.dot_general(hsel, da * av, (((1,), (1,)), ((), ())), precision=HIGHEST, preferred_element_type=F32)
        for q in range(sub):
            dlt_ref[q] = dlt[:, q * TQ:(q + 1) * TQ]
            dat_ref[q] = dat[:, q * TQ:(q + 1) * TQ]

        ps = ps_ref[...]
        mp = mp_ref[...]
        dm, dg = _rms_bwd(mp * ps, gp_ref[...], dmix[:, D_ATTN:])
        dgp_ref[...] += dg
        dps_ref[...] += jnp.sum(dm * mp, axis=0, keepdims=True)
        dmpb = (dm * ps).astype(BF16)
        dmpb_ref[...] = dmpb
        for g in range(len(POOL_WINDOWS)):
            cols = slice(g * POOL_CH, (g + 1) * POOL_CH)
            dy_ref[:, cols] = _nt(dmpb[:, cols], wp_ref[g])

    rowblk = pl.BlockSpec((TS, d), _row)
    half = pl.BlockSpec((TS, D_ATTN), _row)
    vec = lambda n: pl.BlockSpec((1, n), _fixed)
    return pl.pallas_call(
        body, grid=(s // TS,), name="post_attn_bwd",
        out_shape=(jax.ShapeDtypeStruct((s, d), BF16), jax.ShapeDtypeStruct((s, D_ATTN), BF16),
                   jax.ShapeDtypeStruct((s // TQ, D_ATTN, TQ), BF16),
                   jax.ShapeDtypeStruct((s // TQ, HEADS, TQ), F32), jax.ShapeDtypeStruct((s, D_POOL), BF16),
                   jax.ShapeDtypeStruct((s, D_POOL), F32), jax.ShapeDtypeStruct((1, d), F32),
                   jax.ShapeDtypeStruct((1, D_ATTN), F32), jax.ShapeDtypeStruct((1, D_POOL), F32),
                   jax.ShapeDtypeStruct((1, D_POOL), F32)),
        in_specs=[rowblk, rowblk, half, half, pl.BlockSpec(wout.shape, _fixed),
                  pl.BlockSpec(wpool.shape, lambda i: (0, 0, 0)), vec(d), vec(D_ATTN), vec(D_POOL), vec(D_POOL),
                  pl.BlockSpec(eye.shape, _fixed)],
        out_specs=(rowblk, half, pl.BlockSpec((sub, D_ATTN, TQ), lambda i: (i, 0, 0)),
                   pl.BlockSpec((sub, HEADS, TQ), lambda i: (i, 0, 0)), half, half,
                   vec(d), vec(D_ATTN), vec(D_POOL), vec(D_POOL)),
        compiler_params=_params(1),
    )(dh1, o, a, mpre, wout, wpool, g_post, g_attn, g_pool, pscale, eye)


def _attn_bwd(ka, v, kt3, qat3, q, do, dot3, lset3, dlt3, chip_blocks):
    s = q.shape[0]
    nq = s // TQ
    wide = HEADS * LANES

    def body(ka_ref, v_ref, kt_ref, qat_ref, q_ref, do_ref, dot_ref, lset_ref, dlt_ref, b_ref,
             dqt_ref, dk_ref, dv_ref, dcs_ref, drs_ref, got_ref, dca, dkw, dvw, pt_scr, ptb_scr, dsb_scr,
             stage, send_sems, recv_sems, local_sem):
        j = pl.program_id(0)

        @pl.when(j == 0)
        def _():
            _chips_start(b_ref, got_ref, stage, send_sems, recv_sems, local_sem)
            dqt_ref[...] = jnp.zeros_like(dqt_ref)
            drs_ref[...] = jnp.zeros_like(drs_ref)

        dkw[...] = jnp.zeros_like(dkw)
        dvw[...] = jnp.zeros_like(dvw)
        dca[...] = jnp.zeros_like(dca)

        def tile(i, masked):
            rows = pl.ds(i * TQ, TQ)
            for h in range(HEADS):
                aug = slice(h * AUG, (h + 1) * AUG)
                st = _nn(ka_ref[:, aug], qat_ref[i, aug, :]) - lset_ref[i, h:h + 1, :]
                if masked:
                    st = jnp.where(_causal_in_tile(), st, NEG)
                pt = jnp.exp(st)
                pt_scr[h] = pt
                ptb_scr[h] = pt.astype(BF16)
            for h in range(HEADS):
                hs = slice(h * HEAD_DIM, (h + 1) * HEAD_DIM)
                half = slice(h * LANES, h * LANES + HEAD_DIM)
                dvw[:, half] += _nn(ptb_scr[h], do_ref[rows, hs])
                dst = pt_scr[h] * (_nn(v_ref[:, hs], dot_ref[i, hs, :]) - dlt_ref[i, h:h + 1, :])
                dsb_scr[h] = dst.astype(BF16)
                drs_ref[i, h, 0:1, :] += jnp.sum(dst, axis=0, keepdims=True)
                dca[:, h * LANES:(h + 1) * LANES] += dst[:, 0:LANES] + dst[:, LANES:2 * LANES]
            for h in range(HEADS):
                hs = slice(h * HEAD_DIM, (h + 1) * HEAD_DIM)
                half = slice(h * LANES, h * LANES + HEAD_DIM)
                dkw[:, half] += _nn(dsb_scr[h], q_ref[rows, hs])
                dqt_ref[i, hs, :] += _nn(kt_ref[0, hs, :], dsb_scr[h])

        def step(i, carry):
            tile(i, False)
            return carry

        tile(j, True)
        lax.fori_loop(j + 1, nq, step, 0)
        lane = lax.broadcasted_iota(jnp.int32, (TQ, LANES), 1)
        dcs_all = jnp.zeros((TQ, LANES), F32)
        for h in range(HEADS):
            hs = slice(h * HEAD_DIM, (h + 1) * HEAD_DIM)
            half = slice(h * LANES, h * LANES + HEAD_DIM)
            dk_ref[:, hs] = dkw[:, half]
            dv_ref[:, hs] = dvw[:, half]
            colsum = jnp.sum(dca[:, h * LANES:(h + 1) * LANES], axis=1, keepdims=True)
            dcs_all = jnp.where(lane == h, colsum, dcs_all)
        dcs_ref[...] = dcs_all

        @pl.when(j == nq - 1)
        def _():
            _chips_finish(b_ref, got_ref, send_sems, recv_sems)

    blk = pl.BlockSpec((TQ, D_ATTN), _row)
    _, r, cdim = chip_blocks.shape
    return pl.pallas_call(
        body, grid=(nq,), name="attn_bwd",
        out_shape=(jax.ShapeDtypeStruct((nq, D_ATTN, TQ), F32), jax.ShapeDtypeStruct((s, D_ATTN), F32),
                   jax.ShapeDtypeStruct((s, D_ATTN), F32), jax.ShapeDtypeStruct((s, LANES), F32),
                   jax.ShapeDtypeStruct((nq, HEADS, 8, TQ), F32),
                   jax.ShapeDtypeStruct(chip_blocks.shape, chip_blocks.dtype)),
        in_specs=[pl.BlockSpec((TQ, HEADS * AUG), _row), blk, pl.BlockSpec((1, D_ATTN, TQ), lambda j: (j, 0, 0)),
                  VMEM_WHOLE, VMEM_WHOLE, VMEM_WHOLE, VMEM_WHOLE, VMEM_WHOLE, VMEM_WHOLE, ANY],
        out_specs=(pl.BlockSpec((nq, D_ATTN, TQ), lambda j: (0, 0, 0)), blk, blk, pl.BlockSpec((TQ, LANES), _row),
                   pl.BlockSpec((nq, HEADS, 8, TQ), lambda j: (0, 0, 0, 0)), ANY),
        scratch_shapes=[pltpu.VMEM((TQ, wide), F32), pltpu.VMEM((TQ, wide), F32), pltpu.VMEM((TQ, wide), F32),
                        pltpu.VMEM((HEADS, TQ, TQ), F32), pltpu.VMEM((HEADS, TQ, TQ), BF16),
                        pltpu.VMEM((HEADS, TQ, TQ), BF16), pltpu.VMEM((r, cdim), chip_blocks.dtype),
                        pltpu.SemaphoreType.DMA((3,)), pltpu.SemaphoreType.DMA((3,)), pltpu.SemaphoreType.DMA],
        compiler_params=_params(1),
    )(ka, v, kt3, qat3, q, do, dot3, lset3, dlt3, chip_blocks)


def _pre_attn_bwd(dqt3, dk, dv, dcs, drs, fl, dy, x, dh1, g1, wqkv, wf, wu):
    s, d = x.shape
    nt = s // TS
    n = TS + HALO
    sub = TS // TQ

    def body(dqt_ref, dk_ref, dv_ref, dcs_ref, drs_ref, fl_ref, dy_ref, x_ref, dh1_ref, g_ref, wqkv_ref, wf_ref, wu_ref,
             gx_ref, dqkv_ref, dfb_ref, dub_ref, dg_ref, db_ref, ybuf, ccar, dlog):
        i = pl.program_id(0)
        ti = nt - 1 - i

        @pl.when(i == 0)
        def _():
            ybuf[TS:n, :] = jnp.zeros((HALO, D_POOL), F32)
            ccar[...] = jnp.zeros_like(ccar)
            dg_ref[...] = jnp.zeros_like(dg_ref)
            db_ref[...] = jnp.zeros_like(db_ref)

        rr = lax.broadcasted_iota(jnp.int32, (TS, TS), 0)
        cc = lax.broadcasted_iota(jnp.int32, (TS, TS), 1)
        triu = (cc >= rr).astype(F32)
        dlog[...] = ccar[...] + jnp.dot(triu, drs_ref[...] - dcs_ref[...], precision=HIGHEST, preferred_element_type=F32)
        ccar[...] = dlog[0:1, :]
        df = dlog[...] * jax.nn.sigmoid(-fl_ref[...])
        db_ref[...] += jnp.sum(df, axis=0, keepdims=True)
        dfb = df.astype(BF16)
        dfb_ref[...] = dfb

        t = ti * TS + lax.broadcasted_iota(jnp.int32, (TS, 1), 0)
        dy = dy_ref[...]
        for g, w in enumerate(POOL_WINDOWS):
            cols = slice(g * POOL_CH, (g + 1) * POOL_CH)
            ybuf[0:TS, cols] = dy[:, cols] / jnp.minimum(t + 1, w).astype(F32)
        for g, w in enumerate(POOL_WINDOWS):
            cols = slice(g * POOL_CH, (g + 1) * POOL_CH)
            sm = ybuf[:, cols]
            step = 1
            while step < w:
                sm = sm + pltpu.roll(sm, n - step, 0)
                step *= 2
            dub_ref[:, cols] = (sm[0:TS, :] - dy[:, cols]).astype(BF16)
        ybuf[TS:n, :] = ybuf[0:HALO, :]

        for a in range(sub):
            dqkv_ref[a * TQ:(a + 1) * TQ, 0:D_ATTN] = (dqt_ref[a].T * 0.125).astype(BF16)
        dqkv_ref[:, D_ATTN:2 * D_ATTN] = dk_ref[...].astype(BF16)
        dqkv_ref[:, 2 * D_ATTN:] = dv_ref[...].astype(BF16)
        dhn = _nn(dqkv_ref[...], wqkv_ref[...]) + _nn(dfb, wf_ref[...]) + _nn(dub_ref[...], wu_ref[...])
        dx, dg = _rms_bwd(x_ref[...], g_ref[...], dhn)
        gx_ref[...] = dh1_ref[...] + dx
        dg_ref[...] += dg

    rev = lambda i: (nt - 1 - i, 0)
    blk = lambda w: pl.BlockSpec((TS, w), rev)
    return pl.pallas_call(
        body, grid=(nt,), name="pre_attn_bwd",
        out_shape=(jax.ShapeDtypeStruct((s, d), F32), jax.ShapeDtypeStruct((s, 3 * D_ATTN), BF16),
                   jax.ShapeDtypeStruct((s, LANES), BF16), jax.ShapeDtypeStruct((s, D_POOL), BF16),
                   jax.ShapeDtypeStruct((1, d), F32), jax.ShapeDtypeStruct((1, LANES), F32)),
        in_specs=[pl.BlockSpec((sub, D_ATTN, TQ), lambda i: (nt - 1 - i, 0, 0)),
                  blk(D_ATTN), blk(D_ATTN), blk(LANES), blk(LANES), blk(LANES), blk(D_POOL), blk(d), blk(d),
                  pl.BlockSpec((1, d), _fixed), pl.BlockSpec(wqkv.shape, _fixed), pl.BlockSpec(wf.shape, _fixed),
                  pl.BlockSpec(wu.shape, _fixed)],
        out_specs=(blk(d), blk(3 * D_ATTN), blk(LANES), blk(D_POOL),
                   pl.BlockSpec((1, d), _fixed), pl.BlockSpec((1, LANES), _fixed)),
        scratch_shapes=[pltpu.VMEM((n, D_POOL), F32), pltpu.VMEM((1, LANES), F32), pltpu.VMEM((TS, LANES), F32)],
        compiler_params=_params(1),
    )(dqt3, dk, dv, dcs, drs, fl, dy, x, dh1, g1, wqkv, wf, wu)


def _wgrad(a, b, out_dtype, name):
    s, m = a.shape
    n = b.shape[1]
    tm = max(t for t in range(LANES, min(m, TM_WGRAD) + 1, LANES) if m % t == 0)
    ns = s // TS

    def body(a_ref, b_ref, o_ref, acc):
        i = pl.program_id(1)

        @pl.when(i == 0)
        def _():
            acc[...] = jnp.zeros_like(acc)

        acc[...] += _tn(a_ref[...], b_ref[pl.ds(i * TS, TS), :])

        @pl.when(i == ns - 1)
        def _():
            o_ref[...] = acc[...].astype(out_dtype)

    return pl.pallas_call(
        body, grid=(m // tm, ns), name=name, out_shape=jax.ShapeDtypeStruct((m, n), out_dtype),
        in_specs=[pl.BlockSpec((TS, tm), lambda j, i: (i, j)), VMEM_WHOLE],
        out_specs=pl.BlockSpec((tm, n), lambda j, i: (j, 0)),
        scratch_shapes=[pltpu.VMEM((tm, n), F32)],
        compiler_params=_params(2),
    )(a, b)


def _adamw(w, g, m, v):
    m = ADAM_B1 * m + (1.0 - ADAM_B1) * g
    v = ADAM_B2 * v + (1.0 - ADAM_B2) * (g * g)
    m_hat = m / (1.0 - ADAM_B1 ** ADAM_STEP)
    v_hat = v / (1.0 - ADAM_B2 ** ADAM_STEP)
    delta = -ADAM_LR * (m_hat / (jnp.sqrt(v_hat) + ADAM_EPS) + ADAM_WD * w)
    return delta, m, v


def _pair_sum(core, t, theirs, tr, name):
    nk, r, c = theirs.shape

    def body(core_ref, a_ref, b_ref, o_ref):
        o_ref[...] = (a_ref[...].astype(F32) + b_ref[...].astype(F32)).astype(BF16)

    blk = pl.BlockSpec((1, tr, c), lambda k, i, core_ref: (k, i, 0))
    return pl.pallas_call(
        body, name=name, out_shape=jax.ShapeDtypeStruct(theirs.shape, BF16),
        grid_spec=pltpu.PrefetchScalarGridSpec(
            num_scalar_prefetch=1, grid=(nk, r // tr),
            in_specs=[pl.BlockSpec((1, tr, c), lambda k, i, core_ref: (2 * k + core_ref[0], i, 0)), blk],
            out_specs=blk),
        compiler_params=_params(2),
    )(core, t, theirs)


def _reduce_update_big(parts, w, m, v, tr, name):
    nk, r, c = parts.shape

    def body(p_ref, w_ref, m_ref, v_ref, g_ref, d_ref, nm_ref, nv_ref):
        g = p_ref[0].astype(F32)
        for k in range(1, nk):
            g = g + p_ref[k].astype(F32)
        g_ref[...] = g
        d_ref[...], nm_ref[...], nv_ref[...] = _adamw(w_ref[...], g, m_ref[...], v_ref[...])

    blk = pl.BlockSpec((tr, c), _row)
    out = jax.ShapeDtypeStruct((r, c), F32)
    return pl.pallas_call(
        body, grid=(r // tr,), name=name, out_shape=(out,) * 4,
        in_specs=[pl.BlockSpec((nk, tr, c), lambda i: (0, i, 0)), blk, blk, blk],
        out_specs=(blk,) * 4, compiler_params=_params(1),
    )(parts, w, m, v)


def _reduce_update_small(parts, w, m, v):
    nd = parts.shape[0]

    def body(p_ref, w_ref, m_ref, v_ref, g_ref, d_ref, nm_ref, nv_ref):
        g = p_ref[0]
        for k in range(1, nd):
            g = g + p_ref[k]
        g_ref[...] = g
        d_ref[...], nm_ref[...], nv_ref[...] = _adamw(w_ref[...], g, m_ref[...], v_ref[...])

    out = jax.ShapeDtypeStruct(w.shape, F32)
    return pl.pallas_call(body, name="reduce_update_small", out_shape=(out,) * 4,
                          compiler_params=pltpu.CompilerParams(vmem_limit_bytes=VMEM_LIMIT))(parts, w, m, v)


MESH = pl.DeviceIdType.MESH


def _copy_through_vmem(src_hbm, dst_hbm, stage, sem):
    load = pltpu.make_async_copy(src_hbm, stage, sem)
    load.start()
    load.wait()
    store = pltpu.make_async_copy(stage, dst_hbm, sem)
    store.start()
    store.wait()


class _GatherPlan:
    def __init__(self, x_ref, out_ref, send_sems, recv_sems):
        x, y, c = lax.axis_index("x"), lax.axis_index("y"), lax.axis_index("c")
        self.me, self.sibling, self.c = (x, y, c), (x, y, 1 - c), c
        self.chips = [(1 - x, y), (x, 1 - y), (1 - x, 1 - y)]
        self.x_ref, self.out_ref, self.send_sems, self.recv_sems = x_ref, out_ref, send_sems, recv_sems

    def slot(self, px, py, pc):
        return self.out_ref.at[4 * px + 2 * py + pc]

    def copy(self, k, block, to, src=None):
        return pltpu.make_async_remote_copy(
            src_ref=self.slot(*block) if src is None else src, dst_ref=self.slot(*block),
            send_sem=self.send_sems.at[k], recv_sem=self.recv_sems.at[k], device_id=to, device_id_type=MESH)

    def first(self):
        return [self.copy(0, self.me, self.sibling, src=self.x_ref)] + [
            self.copy(1 + j, self.me, (*chip, self.c), src=self.x_ref) for j, chip in enumerate(self.chips)]

    def passed(self):
        return [self.copy(4 + j, (*chip, self.c), self.sibling) for j, chip in enumerate(self.chips)]


def _gather_start(x_ref, out_ref, stage, send_sems, recv_sems, local_sem):
    plan = _GatherPlan(x_ref, out_ref, send_sems, recv_sems)
    for cp in plan.first():
        cp.start()
    _copy_through_vmem(x_ref, plan.slot(*plan.me), stage, local_sem)


def _gather_pass_on(out_ref, send_sems, recv_sems):
    plan = _GatherPlan(None, out_ref, send_sems, recv_sems)
    passed = plan.passed()
    for j, chip in enumerate(plan.chips):
        plan.copy(1 + j, (*chip, plan.c), plan.me).wait_recv()
        passed[j].start()


def _gather_finish(x_ref, out_ref, send_sems, recv_sems):
    plan = _GatherPlan(x_ref, out_ref, send_sems, recv_sems)
    plan.copy(0, plan.sibling, plan.me).wait_recv()
    for j, chip in enumerate(plan.chips):
        plan.copy(4 + j, (*chip, 1 - plan.c), plan.me).wait_recv()
    for cp in plan.first() + plan.passed():
        cp.wait_send()


def _all_gather(xs, name):
    r, cdim = xs.shape

    def body(x_ref, out_ref, stage, send_sems, recv_sems, local_sem):
        _gather_start(x_ref, out_ref, stage, send_sems, recv_sems, local_sem)
        _gather_pass_on(out_ref, send_sems, recv_sems)
        _gather_finish(x_ref, out_ref, send_sems, recv_sems)

    return pl.pallas_call(
        body, name=name, out_shape=jax.ShapeDtypeStruct((N_DEV, r, cdim), xs.dtype),
        in_specs=[ANY], out_specs=ANY,
        scratch_shapes=[pltpu.VMEM((r, cdim), xs.dtype), pltpu.SemaphoreType.DMA((7,)), pltpu.SemaphoreType.DMA((7,)),
                        pltpu.SemaphoreType.DMA],
        compiler_params=pltpu.CompilerParams(vmem_limit_bytes=VMEM_LIMIT),
    )(xs)


def _rs_pair(t, name):
    _, r, cdim = t.shape

    def body(t_ref, theirs_ref, send_sems, recv_sems):
        x, y, c = lax.axis_index("x"), lax.axis_index("y"), lax.axis_index("c")
        remote = [pltpu.make_async_remote_copy(
            src_ref=t_ref.at[2 * k + (1 - c)], dst_ref=theirs_ref.at[k],
            send_sem=send_sems.at[k], recv_sem=recv_sems.at[k], device_id=(x, y, 1 - c), device_id_type=MESH)
            for k in range(4)]
        for cp in remote:
            cp.start()
        for cp in remote:
            cp.wait()

    return pl.pallas_call(
        body, name=name, out_shape=jax.ShapeDtypeStruct((4, r, cdim), t.dtype), in_specs=[ANY], out_specs=ANY,
        scratch_shapes=[pltpu.SemaphoreType.DMA((4,)), pltpu.SemaphoreType.DMA((4,))],
    )(t)


def _chips_start(b_ref, out_ref, stage, send_sems, recv_sems, local_sem):
    x, y, c = lax.axis_index("x"), lax.axis_index("y"), lax.axis_index("c")
    mychip = 2 * x + y
    for j, (px, py) in enumerate([(1 - x, y), (x, 1 - y), (1 - x, 1 - y)]):
        pltpu.make_async_remote_copy(
            src_ref=b_ref.at[2 * px + py], dst_ref=out_ref.at[mychip],
            send_sem=send_sems.at[j], recv_sem=recv_sems.at[j], device_id=(px, py, c), device_id_type=MESH).start()
    _copy_through_vmem(b_ref.at[mychip], out_ref.at[mychip], stage, local_sem)


def _chips_finish(b_ref, out_ref, send_sems, recv_sems):
    x, y, c = lax.axis_index("x"), lax.axis_index("y"), lax.axis_index("c")
    for j, (px, py) in enumerate([(1 - x, y), (x, 1 - y), (1 - x, 1 - y)]):
        pltpu.make_async_remote_copy(
            src_ref=b_ref.at[2 * px + py], dst_ref=out_ref.at[2 * px + py],
            send_sem=send_sems.at[j], recv_sem=recv_sems.at[j], device_id=(px, py, c), device_id_type=MESH).wait()


def _rs_chips(b, name):
    _, r, cdim = b.shape

    def body(b_ref, out_ref, stage, send_sems, recv_sems, local_sem):
        _chips_start(b_ref, out_ref, stage, send_sems, recv_sems, local_sem)
        _chips_finish(b_ref, out_ref, send_sems, recv_sems)

    return pl.pallas_call(
        body, name=name, out_shape=jax.ShapeDtypeStruct(b.shape, b.dtype), in_specs=[ANY], out_specs=ANY,
        scratch_shapes=[pltpu.VMEM((r, cdim), b.dtype), pltpu.SemaphoreType.DMA((3,)), pltpu.SemaphoreType.DMA((3,)),
                        pltpu.SemaphoreType.DMA],
        compiler_params=pltpu.CompilerParams(vmem_limit_bytes=VMEM_LIMIT),
    )(b)


def _pad_rows(a, rows):
    return jnp.pad(a, ((0, rows - a.shape[0]), (0, 0)))


def _pack_in(w_in):
    return _pad_rows(w_in[0].T, ROWS_IN)


def _unpack_in(r):
    return r[0:SHARD_IN].T[None]


def _pack_rest(w_out, w_gate, w_up, w_down, w_ple, w_pg):
    return jnp.concatenate([w_out[0], w_gate[0].T, w_up[0].T, w_down[0], w_ple[0].T.reshape(32, D_MODEL), w_pg[0]],
                           axis=0)


def _unpack_rest(r):
    return (r[0:OFF_GATE][None], r[OFF_GATE:OFF_UP].T[None], r[OFF_UP:OFF_DOWN].T[None],
            r[OFF_DOWN:OFF_PLE][None], r[OFF_PLE:OFF_PG].reshape(128, D_PLE).T[None], r[OFF_PG:ROWS_REST][None])


def _full_rest(g):
    return (g[:, 0:OFF_GATE].reshape(D_MODEL, D_MODEL), g[:, OFF_GATE:OFF_UP].reshape(D_FF, D_MODEL),
            g[:, OFF_UP:OFF_DOWN].reshape(D_FF, D_MODEL), g[:, OFF_DOWN:OFF_PLE].reshape(D_FF, D_MODEL),
            g[:, OFF_PLE:OFF_PG].reshape(D_MODEL, D_PLE), g[:, OFF_PG:ROWS_REST].reshape(D_MODEL, D_MODEL))


def _pack_small(w_pool, g_mix_pre, g_mix_post, g_ffn_pre, g_ffn_post, g_ple, g_attn, g_pool, pool_scale, b_forget,
                loss=None):
    def row(vrow):
        return jnp.pad(vrow.reshape(1, -1), ((0, 0), (0, D_MODEL - vrow.size)))
    rows = [w_pool.reshape(64, D_MODEL), row(g_mix_pre), row(g_mix_post), row(g_ffn_pre), row(g_ffn_post), row(g_ple),
            row(g_attn), row(g_pool), row(pool_scale), row(b_forget),
            row(loss) if loss is not None else jnp.zeros((1, D_MODEL), F32)]
    return _pad_rows(jnp.concatenate(rows, axis=0), SMALL_ROWS)


def _unpack_small(r):
    return dict(
        w_pool=r[0:64].reshape(1, 4, POOL_CH, POOL_CH), g_mix_pre=r[ROW_G_MIX_PRE:ROW_G_MIX_PRE + 1],
        g_mix_post=r[ROW_G_MIX_POST:ROW_G_MIX_POST + 1], g_ffn_pre=r[ROW_G_FFN_PRE:ROW_G_FFN_PRE + 1],
        g_ffn_post=r[ROW_G_FFN_POST:ROW_G_FFN_POST + 1], g_ple=r[ROW_G_PLE:ROW_G_PLE + 1],
        g_attn_grp=r[ROW_G_ATTN:ROW_G_ATTN + 1, 0:D_ATTN], g_pool_grp=r[ROW_G_POOL:ROW_G_POOL + 1, 0:D_POOL],
        pool_scale=r[ROW_POOL_SCALE:ROW_POOL_SCALE + 1, 0:D_POOL], b_forget=r[ROW_B_FORGET:ROW_B_FORGET + 1, 0:HEADS])


def _step(x, p, tgt, small, in_w, in_m, in_v, rest_w, rest_m, rest_v):
    core = lax.axis_index("c").astype(jnp.int32).reshape(1)
    win_t = _all_gather(in_w.astype(BF16), "gather_w_in")[:, 0:SHARD_IN].reshape(D_IN, D_MODEL)
    wqkv = win_t[0:3 * D_ATTN]
    wf = _pad_rows(win_t[3 * D_ATTN:3 * D_ATTN + HEADS], LANES)
    wu = win_t[3 * D_ATTN + HEADS:]
    wpool = small["w_pool"].astype(BF16)
    bpad = jnp.pad(small["b_forget"], ((0, 0), (0, LANES - HEADS)))

    lay = _attn_layout_constants()
    hn, q, ka, v, qat3, vt3, kt3, fl, y, mpre = _pre_attn_fwd(x, small["g_mix_pre"], wqkv, wf, wu, bpad, wpool, lay)
    a, lset3, gathered = _attn_fwd(ka, qat3, vt3, rest_w.astype(BF16))
    wout, wg_t, wu_t, wd, wple_t, wpg = _full_rest(gathered)
    mix, o, h1, hn2 = _post_attn_fwd(a, mpre, x, small["g_attn_grp"], small["g_pool_grp"], small["pool_scale"], wout,
                                     small["g_mix_post"], small["g_ffn_pre"])
    gate, up, act, ff, h2 = _ffn_fwd(hn2, wg_t, wu_t, wd, h1, small["g_ffn_post"])
    dh2, dff, dgl, dpp, h2b, pb, loss8, dg_ple, dg_ffn_post = _tail_fwd_bwd(
        h2, p, tgt, ff, wple_t, wpg, small["g_ple"], small["g_ffn_post"])
    dgate, dup, dh1, dg_ffn_pre = _ffn_bwd(dff, gate, up, wd, wg_t, wu_t, h1, dh2, small["g_ffn_pre"])
    dob, dab, dat3, dlt3, dmpb, dy, dg_mix_post, dg_attn, dg_pool, dps = _post_attn_bwd(
        dh1, o, a, mpre, wout, wpool, small["g_mix_post"], small["g_attn_grp"], small["g_pool_grp"], small["pool_scale"],
        lay["eye"])

    nd = N_DEV
    send_rest = jnp.concatenate([
        _wgrad(mix, dob, BF16, "wgrad_out").reshape(nd, 128, D_MODEL),
        _wgrad(dgate, hn2, BF16, "wgrad_gate").reshape(nd, 352, D_MODEL),
        _wgrad(dup, hn2, BF16, "wgrad_up").reshape(nd, 352, D_MODEL),
        _wgrad(act, dff, BF16, "wgrad_down").reshape(nd, 352, D_MODEL),
        _wgrad(dpp, pb, BF16, "wgrad_ple").reshape(nd, 32, D_MODEL),
        _wgrad(h2b, dgl, BF16, "wgrad_ple_gate").reshape(nd, 128, D_MODEL)], axis=1)
    pair_rest = _pair_sum(core, send_rest, _rs_pair(send_rest, "rs_pair_rest"), TR_REST, "rs_pair_sum_rest")

    dqt3, dk, dv, dcs, drs4, chips_rest = _attn_bwd(ka, v, kt3, qat3, q, dab, dat3, lset3, dlt3, pair_rest)
    upd_rest = _reduce_update_big(chips_rest, rest_w, rest_m, rest_v, TR_REST, "reduce_update_rest")
    drs = jnp.pad(drs4[:, :, 0, :].transpose(0, 2, 1).reshape(-1, HEADS), ((0, 0), (0, LANES - HEADS)))
    gx, dqkv, dfb, dub, dg_mix_pre, db = _pre_attn_bwd(dqt3, dk, dv, dcs, drs, fl, dy, x, dh1, small["g_mix_pre"], wqkv, wf, wu)

    dwin_t = jnp.concatenate([_wgrad(dqkv, hn, F32, "wgrad_qkv"), _wgrad(dfb, hn, F32, "wgrad_forget")[0:HEADS],
                              _wgrad(dub, hn, F32, "wgrad_pool_in")], axis=0)
    send_in = jnp.pad(dwin_t.reshape(nd, SHARD_IN, D_MODEL), ((0, 0), (0, ROWS_IN - SHARD_IN), (0, 0))).astype(BF16)
    pair_in = _pair_sum(core, send_in, _rs_pair(send_in, "rs_pair_in"), ROWS_IN, "rs_pair_sum_in")
    upd_in = _reduce_update_big(_rs_chips(pair_in, "rs_chips_in"), in_w, in_m, in_v, ROWS_IN, "reduce_update_in")

    dwp = _wgrad(y, dmpb, F32, "wgrad_pool")
    dw_pool = jnp.stack([dwp[g * POOL_CH:(g + 1) * POOL_CH, g * POOL_CH:(g + 1) * POOL_CH] for g in range(4)])
    small_part = _pack_small(dw_pool, dg_mix_pre, dg_mix_post, dg_ffn_pre, dg_ffn_post, dg_ple, dg_attn, dg_pool, dps,
                             db[:, 0:HEADS], loss8[0:1, 0:1])
    return gx, small_part, upd_in, upd_rest


def kernel(x, p, g_mix_pre, w_in, b_forget, g_attn_grp, g_pool_grp, w_pool, pool_scale, w_out, g_mix_post, g_ffn_pre, w_ffn_gate, w_ffn_up, w_ffn_down, g_ffn_post, w_ple_proj, g_ple, w_ple_gate, loss_target, m_g_mix_pre, m_w_in, m_b_forget, m_g_attn_grp, m_g_pool_grp, m_w_pool, m_pool_scale, m_w_out, m_g_mix_post, m_g_ffn_pre, m_w_ffn_gate, m_w_ffn_up, m_w_ffn_down, m_g_ffn_post, m_w_ple_proj, m_g_ple, m_w_ple_gate, v_g_mix_pre, v_w_in, v_b_forget, v_g_attn_grp, v_g_pool_grp, v_w_pool, v_pool_scale, v_w_out, v_g_mix_post, v_g_ffn_pre, v_w_ffn_gate, v_w_ffn_up, v_w_ffn_down, v_g_ffn_post, v_w_ple_proj, v_g_ple, v_w_ple_gate):
    small = dict(w_pool=w_pool[0], g_mix_pre=g_mix_pre, g_mix_post=g_mix_post, g_ffn_pre=g_ffn_pre,
                 g_ffn_post=g_ffn_post, g_ple=g_ple, g_attn_grp=g_attn_grp, g_pool_grp=g_pool_grp,
                 pool_scale=pool_scale, b_forget=b_forget)
    gx, small_part, upd_in, upd_rest = _step(
        x[0], p[0, 0], loss_target[0], small, _pack_in(w_in), _pack_in(m_w_in), _pack_in(v_w_in),
        _pack_rest(w_out, w_ffn_gate, w_ffn_up, w_ffn_down, w_ple_proj, w_ple_gate),
        _pack_rest(m_w_out, m_w_ffn_gate, m_w_ffn_up, m_w_ffn_down, m_w_ple_proj, m_w_ple_gate),
        _pack_rest(v_w_out, v_w_ffn_gate, v_w_ffn_up, v_w_ffn_down, v_w_ple_proj, v_w_ple_gate))

    small_all = _all_gather(small_part, "gather_small")
    sm_w = _pack_small(w_pool, g_mix_pre, g_mix_post, g_ffn_pre, g_ffn_post, g_ple, g_attn_grp, g_pool_grp, pool_scale, b_forget)
    sm_m = _pack_small(m_w_pool, m_g_mix_pre, m_g_mix_post, m_g_ffn_pre, m_g_ffn_post, m_g_ple, m_g_attn_grp, m_g_pool_grp, m_pool_scale, m_b_forget)
    sm_v = _pack_small(v_w_pool, v_g_mix_pre, v_g_mix_post, v_g_ffn_pre, v_g_ffn_post, v_g_ple, v_g_attn_grp, v_g_pool_grp, v_pool_scale, v_b_forget)
    upd_small = _reduce_update_small(small_all, sm_w, sm_m, sm_v)
    loss = upd_small[0][ROW_LOSS, 0]

    def leaves(k):
        b_out, b_gate, b_up, b_down, b_ple, b_pg = _unpack_rest(upd_rest[k])
        s = _unpack_small(upd_small[k])
        return (s["g_mix_pre"], _unpack_in(upd_in[k]), s["b_forget"], s["g_attn_grp"], s["g_pool_grp"], s["w_pool"],
                s["pool_scale"], b_out, s["g_mix_post"], s["g_ffn_pre"], b_gate, b_up, b_down, s["g_ffn_post"], b_ple,
                s["g_ple"], b_pg)

    return (loss, gx[None], *leaves(0), *leaves(1), *leaves(2), *leaves(3))
```

```python
import functools

import jax
import jax.numpy as jnp
from jax import lax
from jax.experimental import pallas as pl
from jax.experimental.pallas import tpu as pltpu

F32 = jnp.float32
BF16 = jnp.bfloat16
HIGHEST = lax.Precision.HIGHEST

D_MODEL = 1024
HEADS = 8
HEAD_DIM = 64
D_ATTN = HEADS * HEAD_DIM
POOL_WINDOWS = (2, 4, 8, 16)
POOL_CH = 128
D_POOL = POOL_CH * len(POOL_WINDOWS)
D_FF = 2816
D_PLE = 256
D_IN = 3 * D_ATTN + HEADS + D_POOL
RMS_EPS = 1e-6
N_DEV = 8

ADAM_LR = 0.001
ADAM_B1 = 0.9
ADAM_B2 = 0.999
ADAM_EPS = 1e-08
ADAM_WD = 0.01
ADAM_STEP = 10

LANES = 128
HALO = 16
TS = 512
TS_FF = 1024
TM_WGRAD = 1536
TQ = 256
TN_FF = 256
NEG = -1e30
VMEM_LIMIT = 56 * 1024 * 1024

SHARD_IN = 257
ROWS_IN = 272
OFF_GATE = 128
OFF_UP = OFF_GATE + 352
OFF_DOWN = OFF_UP + 352
OFF_PLE = OFF_DOWN + 352
OFF_PG = OFF_PLE + 32
ROWS_REST = OFF_PG + 128
TR_REST = 192

SMALL_ROWS = 80
ROW_G_MIX_PRE, ROW_G_MIX_POST, ROW_G_FFN_PRE, ROW_G_FFN_POST, ROW_G_PLE = 64, 65, 66, 67, 68
ROW_G_ATTN, ROW_G_POOL, ROW_POOL_SCALE, ROW_B_FORGET, ROW_LOSS = 69, 70, 71, 72, 73


def _nn(a, b):
    return jnp.dot(a, b, preferred_element_type=F32)


def _nt(a, b):
    return lax.dot_general(a, b, (((1,), (1,)), ((), ())), preferred_element_type=F32)


def _tn(a, b):
    return lax.dot_general(a, b, (((0,), (0,)), ((), ())), preferred_element_type=F32)


def _rstd(v):
    return lax.rsqrt(jnp.mean(v * v, axis=-1, keepdims=True) + RMS_EPS)


def _rms_bwd(v, g, dy):
    r = _rstd(v)
    vh = v * r
    t = dy * g
    dv = r * (t - vh * jnp.mean(t * vh, axis=-1, keepdims=True))
    return dv, jnp.sum(dy * vh, axis=0, keepdims=True)


def _params(n_grid):
    return pltpu.CompilerParams(dimension_semantics=("arbitrary",) * n_grid, vmem_limit_bytes=VMEM_LIMIT)


def _row(i):
    return (i, 0)


def _fixed(*_):
    return (0, 0)


VMEM_WHOLE = pl.BlockSpec(memory_space=pltpu.VMEM)
SMEM_WHOLE = pl.BlockSpec(memory_space=pltpu.SMEM)
ANY = pl.BlockSpec(memory_space=pl.ANY)


AUG = 128
BIAS_LANE = HEAD_DIM
ONE_LANE = HEAD_DIM + 3
SPARE_LANE = HEADS


def _attn_layout_constants():
    import numpy as np
    place = np.zeros((D_ATTN, HEADS * AUG), np.float32)
    for r in range(D_ATTN):
        place[r, (r // HEAD_DIM) * AUG + r % HEAD_DIM] = 1.0
    bias_k = np.zeros((3, LANES, HEADS * AUG), np.float32)
    bias_q = np.zeros((3, LANES, HEADS * AUG), np.float32)
    for h in range(HEADS):
        for part in range(3):
            bias_k[part, h, h * AUG + BIAS_LANE + part] = -1.0
            bias_q[part, h, h * AUG + ONE_LANE + part] = 1.0
            bias_k[0, SPARE_LANE, h * AUG + ONE_LANE + part] = 1.0
            bias_q[0, SPARE_LANE, h * AUG + BIAS_LANE + part] = 1.0
    as_bf = lambda a: jnp.asarray(a, BF16)
    return dict(place=as_bf(place), place_t=as_bf(place.T), bias_k=as_bf(bias_k),
                bias_q_t=as_bf(bias_q.transpose(0, 2, 1)), eye=as_bf(np.eye(D_ATTN, dtype=np.float32)))


def _pre_attn_fwd(x, g1, wqkv, wf, wu, bpad, wpool, lay):
    s, d = x.shape
    nt = s // TS
    sub = TS // TQ

    def body(x_ref, g_ref, wqkv_ref, wf_ref, wu_ref, b_ref, wp_ref, place_ref, place_t_ref, bk_ref, bqt_ref, eye_ref,
             hn_ref, q_ref, ka_ref, v_ref, qat_ref, vt_ref, kt_ref, fl_ref, y_ref, mp_ref, ubuf, ccar, cbuf):
        i = pl.program_id(0)

        @pl.when(i == 0)
        def _():
            ubuf[0:HALO, :] = jnp.zeros((HALO, D_POOL), F32)
            ccar[...] = jnp.zeros_like(ccar)

        xv = x_ref[...]
        hn = (xv * _rstd(xv) * g_ref[...]).astype(BF16)
        hn_ref[...] = hn
        zq = _nt(hn, wqkv_ref[...])
        qb = (zq[:, 0:D_ATTN] * 0.125).astype(BF16)
        kb = zq[:, D_ATTN:2 * D_ATTN].astype(BF16)
        vb = zq[:, 2 * D_ATTN:3 * D_ATTN].astype(BF16)
        q_ref[...] = qb
        v_ref[...] = vb

        fl = _nt(hn, wf_ref[...]) + b_ref[...]
        fl_ref[...] = fl
        logf = jax.nn.log_sigmoid(fl)
        rr = lax.broadcasted_iota(jnp.int32, (TS, TS), 0)
        cc = lax.broadcasted_iota(jnp.int32, (TS, TS), 1)
        tril = (cc <= rr).astype(F32)
        c = jnp.dot(tril, logf, precision=HIGHEST, preferred_element_type=F32) + ccar[...]
        cbuf[...] = c
        ccar[...] = cbuf[TS - 1:TS, :]
        hi = c.astype(BF16)
        rest = c - hi.astype(F32)
        mid = rest.astype(BF16)
        lo = (rest - mid.astype(F32)).astype(BF16)
        lane = lax.broadcasted_iota(jnp.int32, (TS, LANES), 1)
        parts = (jnp.where(lane == SPARE_LANE, 1.0, hi).astype(BF16), mid, lo)
        ka = _nn(kb, place_ref[...])
        qat = _nt(place_t_ref[...], qb)
        for part in range(3):
            ka = ka + _nn(parts[part], bk_ref[part])
            qat = qat + _nt(bqt_ref[part], parts[part])
        ka_ref[...] = ka.astype(BF16)
        qat = qat.astype(BF16)
        vt = _nt(eye_ref[...], vb).astype(BF16)
        kt = _nt(eye_ref[...], kb).astype(BF16)
        for a in range(sub):
            qat_ref[a] = qat[:, a * TQ:(a + 1) * TQ]
            vt_ref[a] = vt[:, a * TQ:(a + 1) * TQ]
            kt_ref[a] = kt[:, a * TQ:(a + 1) * TQ]

        u = _nt(hn, wu_ref[...])
        ubuf[HALO:HALO + TS, :] = u
        t = i * TS + lax.broadcasted_iota(jnp.int32, (TS, 1), 0)
        for g, w in enumerate(POOL_WINDOWS):
            cols = slice(g * POOL_CH, (g + 1) * POOL_CH)
            sm = ubuf[:, cols]
            step = 1
            while step < w:
                sm = sm + pltpu.roll(sm, step, 0)
                step *= 2
            cnt = jnp.minimum(t + 1, w).astype(F32)
            yg = (sm[HALO:, :] / cnt - u[:, cols]).astype(BF16)
            y_ref[:, cols] = yg
            mp_ref[:, cols] = _nn(yg, wp_ref[g])
        ubuf[0:HALO, :] = u[TS - HALO:, :]

    nq = s // TQ
    aug = HEADS * AUG
    outs = (
        jax.ShapeDtypeStruct((s, d), BF16), jax.ShapeDtypeStruct((s, D_ATTN), BF16),
        jax.ShapeDtypeStruct((s, aug), BF16), jax.ShapeDtypeStruct((s, D_ATTN), BF16),
        jax.ShapeDtypeStruct((nq, aug, TQ), BF16), jax.ShapeDtypeStruct((nq, D_ATTN, TQ), BF16),
        jax.ShapeDtypeStruct((nq, D_ATTN, TQ), BF16),
        jax.ShapeDtypeStruct((s, LANES), F32),
        jax.ShapeDtypeStruct((s, D_POOL), BF16), jax.ShapeDtypeStruct((s, D_POOL), F32),
    )
    fixed3 = lambda i: (0, 0, 0)
    tiles3 = lambda rows: pl.BlockSpec((sub, rows, TQ), lambda i: (i, 0, 0))
    return pl.pallas_call(
        body, grid=(nt,), out_shape=outs, name="pre_attn_fwd",
        in_specs=[pl.BlockSpec((TS, d), _row), pl.BlockSpec((1, d), _fixed),
                  pl.BlockSpec(wqkv.shape, _fixed), pl.BlockSpec(wf.shape, _fixed), pl.BlockSpec(wu.shape, _fixed),
                  pl.BlockSpec((1, LANES), _fixed), pl.BlockSpec(wpool.shape, fixed3),
                  pl.BlockSpec(lay["place"].shape, _fixed), pl.BlockSpec(lay["place_t"].shape, _fixed),
                  pl.BlockSpec(lay["bias_k"].shape, fixed3), pl.BlockSpec(lay["bias_q_t"].shape, fixed3),
                  pl.BlockSpec(lay["eye"].shape, _fixed)],
        out_specs=(pl.BlockSpec((TS, d), _row), pl.BlockSpec((TS, D_ATTN), _row),
                   pl.BlockSpec((TS, aug), _row), pl.BlockSpec((TS, D_ATTN), _row),
                   tiles3(aug), tiles3(D_ATTN), tiles3(D_ATTN),
                   pl.BlockSpec((TS, LANES), _row),
                   pl.BlockSpec((TS, D_POOL), _row), pl.BlockSpec((TS, D_POOL), _row)),
        scratch_shapes=[pltpu.VMEM((TS + HALO, D_POOL), F32), pltpu.VMEM((1, LANES), F32), pltpu.VMEM((TS, LANES), F32)],
        compiler_params=_params(1),
    )(x, g1, wqkv, wf, wu, bpad, wpool, lay["place"], lay["place_t"], lay["bias_k"], lay["bias_q_t"], lay["eye"])


def _causal_in_tile():
    krow = lax.broadcasted_iota(jnp.int32, (TQ, TQ), 0)
    qcol = lax.broadcasted_iota(jnp.int32, (TQ, TQ), 1)
    return krow <= qcol


def _attn_fwd(ka, qat3, vt3, own_block):
    s = ka.shape[0]
    nq = s // TQ
    pass_on_step = max(nq - 2, 0)

    def body(qa_ref, ka_ref, vt_ref, own_ref, a_ref, lset_ref, all_ref, acc, st_scr, pt_scr,
             stage, send_sems, recv_sems, local_sem):
        i = pl.program_id(0)

        @pl.when(i == 0)
        def _():
            _gather_start(own_ref, all_ref, stage, send_sems, recv_sems, local_sem)

        @pl.when(i == pass_on_step)
        def _():
            _gather_pass_on(all_ref, send_sems, recv_sems)

        acc[...] = jnp.zeros_like(acc)

        def tile(j, stats, masked):
            tile_max = []
            for h in range(HEADS):
                aug = slice(h * AUG, (h + 1) * AUG)
                st = _nn(ka_ref[pl.ds(j * TQ, TQ), aug], qa_ref[0, aug, :])
                if masked:
                    st = jnp.where(_causal_in_tile(), st, NEG)
                st_scr[h] = st
                tile_max.append(jnp.max(st, axis=0, keepdims=True))
            new, scale = [], []
            for h in range(HEADS):
                m_old, l_old = stats[h]
                m_new = jnp.maximum(m_old, tile_max[h])
                al = jnp.exp(m_old - m_new)
                pt = jnp.exp(st_scr[h] - m_new)
                pt_scr[h] = pt.astype(BF16)
                new.append((m_new, al * l_old + jnp.sum(pt, axis=0, keepdims=True)))
                scale.append(al)
            for h in range(HEADS):
                rows = slice(h * HEAD_DIM, (h + 1) * HEAD_DIM)
                acc[rows, :] = scale[h] * acc[rows, :] + _nn(vt_ref[j, rows, :], pt_scr[h])
            return tuple(new)

        init = tuple((jnp.full((1, TQ), NEG, F32), jnp.zeros((1, TQ), F32)) for _ in range(HEADS))
        stats = lax.fori_loop(0, i, functools.partial(tile, masked=False), init)
        stats = tile(i, stats, True)
        for h in range(HEADS):
            rows = slice(h * HEAD_DIM, (h + 1) * HEAD_DIM)
            acc[rows, :] = acc[rows, :] / stats[h][1]
            lset_ref[0, h:h + 1, :] = stats[h][0] + jnp.log(stats[h][1])
        a_ref[...] = acc[...].T

        @pl.when(i == nq - 1)
        def _():
            _gather_finish(own_ref, all_ref, send_sems, recv_sems)

    r, cdim = own_block.shape
    return pl.pallas_call(
        body, grid=(nq,), name="attn_fwd",
        out_shape=(jax.ShapeDtypeStruct((s, D_ATTN), F32), jax.ShapeDtypeStruct((nq, HEADS, TQ), F32),
                   jax.ShapeDtypeStruct((N_DEV, r, cdim), own_block.dtype)),
        in_specs=[pl.BlockSpec((1, HEADS * AUG, TQ), lambda i: (i, 0, 0)), VMEM_WHOLE, VMEM_WHOLE, ANY],
        out_specs=(pl.BlockSpec((TQ, D_ATTN), _row), pl.BlockSpec((1, HEADS, TQ), lambda i: (i, 0, 0)), ANY),
        scratch_shapes=[pltpu.VMEM((D_ATTN, TQ), F32), pltpu.VMEM((HEADS, TQ, TQ), F32), pltpu.VMEM((HEADS, TQ, TQ), BF16),
                        pltpu.VMEM((r, cdim), own_block.dtype),
                        pltpu.SemaphoreType.DMA((7,)), pltpu.SemaphoreType.DMA((7,)), pltpu.SemaphoreType.DMA],
        compiler_params=_params(1),
    )(qat3, ka, vt3, own_block)


def _post_attn_fwd(a, mpre, x, g_attn, g_pool, pscale, wout, g_post, g_ffn_pre):
    s, d = x.shape

    def body(a_ref, mp_ref, x_ref, ga_ref, gp_ref, ps_ref, wo_ref, gpost_ref, gpre_ref,
             mix_ref, o_ref, h1_ref, hn2_ref):
        av = a_ref[...]
        mix_ref[:, 0:D_ATTN] = (av * _rstd(av) * ga_ref[...]).astype(BF16)
        mv = mp_ref[...] * ps_ref[...]
        mix_ref[:, D_ATTN:] = (mv * _rstd(mv) * gp_ref[...]).astype(BF16)
        o = _nn(mix_ref[...], wo_ref[...])
        o_ref[...] = o
        h1 = x_ref[...] + o * _rstd(o) * gpost_ref[...]
        h1_ref[...] = h1
        hn2_ref[...] = (h1 * _rstd(h1) * gpre_ref[...]).astype(BF16)

    vec = lambda n: pl.BlockSpec((1, n), _fixed)
    return pl.pallas_call(
        body, grid=(s // TS,), name="post_attn_fwd",
        out_shape=(jax.ShapeDtypeStruct((s, d), BF16), jax.ShapeDtypeStruct((s, d), F32),
                   jax.ShapeDtypeStruct((s, d), F32), jax.ShapeDtypeStruct((s, d), BF16)),
        in_specs=[pl.BlockSpec((TS, D_ATTN), _row), pl.BlockSpec((TS, D_POOL), _row), pl.BlockSpec((TS, d), _row),
                  vec(D_ATTN), vec(D_POOL), vec(D_POOL), pl.BlockSpec(wout.shape, _fixed), vec(d), vec(d)],
        out_specs=(pl.BlockSpec((TS, d), _row),) * 4,
        compiler_params=_params(1),
    )(a, mpre, x, g_attn, g_pool, pscale, wout, g_post, g_ffn_pre)


def _ffn_fwd(hn2, wg, wu, wd, h1, g_post):
    s, d = h1.shape
    nc = D_FF // TN_FF
    ts = min(TS_FF, s)

    def body(hn_ref, wg_ref, wu_ref, wd_ref, h1_ref, g_ref, gate_ref, up_ref, act_ref, ff_ref, h2_ref, acc):
        j = pl.program_id(1)

        @pl.when(j == 0)
        def _():
            acc[...] = jnp.zeros_like(acc)

        for r in range(2):
            rows = slice(r * (ts // 2), (r + 1) * (ts // 2))
            hn = hn_ref[rows, :]
            gt = _nt(hn, wg_ref[...])
            up = _nt(hn, wu_ref[...])
            gate_ref[rows, :] = gt.astype(BF16)
            up_ref[rows, :] = up.astype(BF16)
            act_ref[rows, :] = (gt * jax.nn.sigmoid(gt) * up).astype(BF16)
            acc[rows, :] += _nn(act_ref[rows, :], wd_ref[...])

        @pl.when(j == nc - 1)
        def _():
            ff = acc[...]
            ff_ref[...] = ff
            h2_ref[...] = h1_ref[...] + ff * _rstd(ff) * g_ref[...]

    rowblk = pl.BlockSpec((ts, d), lambda i, j: (i, 0))
    wblk = pl.BlockSpec((TN_FF, d), lambda i, j: (j, 0))
    chunk = pl.BlockSpec((ts, TN_FF), lambda i, j: (i, j))
    return pl.pallas_call(
        body, grid=(s // ts, nc), name="ffn_fwd",
        out_shape=(jax.ShapeDtypeStruct((s, D_FF), BF16),) * 3 + (jax.ShapeDtypeStruct((s, d), F32),) * 2,
        in_specs=[rowblk, wblk, wblk, wblk, rowblk, pl.BlockSpec((1, d), lambda i, j: (0, 0))],
        out_specs=(chunk, chunk, chunk, rowblk, rowblk),
        scratch_shapes=[pltpu.VMEM((ts, d), F32)],
        compiler_params=_params(2),
    )(hn2, wg, wu, wd, h1, g_post)


def _tail_fwd_bwd(h2, p, tgt, ff, wple, wpg, g_ple, g_ffn_post):
    s, d = h2.shape

    def body(h2_ref, p_ref, t_ref, ff_ref, wple_ref, wpg_ref, gple_ref, gfp_ref,
             dh2_ref, dff_ref, dgl_ref, dpp_ref, h2b_ref, pb_ref, loss_ref, dgple_ref, dgfp_ref):
        i = pl.program_id(0)

        @pl.when(i == 0)
        def _():
            loss_ref[...] = jnp.zeros_like(loss_ref)
            dgple_ref[...] = jnp.zeros_like(dgple_ref)
            dgfp_ref[...] = jnp.zeros_like(dgfp_ref)

        h2 = h2_ref[...]
        h2b = h2.astype(BF16)
        h2b_ref[...] = h2b
        pb = p_ref[...].astype(BF16)
        pb_ref[...] = pb
        pp = _nt(pb, wple_ref[...])
        gple = gple_ref[...]
        e = pp * _rstd(pp) * gple
        sg = jax.nn.sigmoid(_nn(h2b, wpg_ref[...]))
        diff = h2 + sg * e - t_ref[...]
        sq = jnp.sum(jnp.sum(diff * diff, axis=1, keepdims=True), axis=0, keepdims=True)
        loss_ref[...] += jnp.broadcast_to(sq * (0.5 / d), loss_ref.shape)
        dh3 = diff * (1.0 / d)
        dgl = (dh3 * e * sg * (1.0 - sg)).astype(BF16)
        dgl_ref[...] = dgl
        dh2 = dh3 + _nt(dgl, wpg_ref[...])
        dh2_ref[...] = dh2
        dpp, dg = _rms_bwd(pp, gple, dh3 * sg)
        dpp_ref[...] = dpp.astype(BF16)
        dgple_ref[...] += dg
        dff, dg = _rms_bwd(ff_ref[...], gfp_ref[...], dh2)
        dff_ref[...] = dff.astype(BF16)
        dgfp_ref[...] += dg

    rowblk = pl.BlockSpec((TS, d), _row)
    vec = pl.BlockSpec((1, d), _fixed)
    return pl.pallas_call(
        body, grid=(s // TS,), name="tail_fwd_bwd",
        out_shape=(jax.ShapeDtypeStruct((s, d), F32), jax.ShapeDtypeStruct((s, d), BF16),
                   jax.ShapeDtypeStruct((s, d), BF16), jax.ShapeDtypeStruct((s, d), BF16),
                   jax.ShapeDtypeStruct((s, d), BF16), jax.ShapeDtypeStruct((s, D_PLE), BF16),
                   jax.ShapeDtypeStruct((8, LANES), F32), jax.ShapeDtypeStruct((1, d), F32),
                   jax.ShapeDtypeStruct((1, d), F32)),
        in_specs=[rowblk, pl.BlockSpec((TS, D_PLE), _row), rowblk, rowblk,
                  pl.BlockSpec(wple.shape, _fixed), pl.BlockSpec(wpg.shape, _fixed), vec, vec],
        out_specs=(rowblk, rowblk, rowblk, rowblk, rowblk, pl.BlockSpec((TS, D_PLE), _row),
                   pl.BlockSpec((8, LANES), _fixed), vec, vec),
        compiler_params=_params(1),
    )(h2, p, tgt, ff, wple, wpg, g_ple, g_ffn_post)


def _ffn_bwd(dff, gate, up, wd, wg, wu, h1, dh2, g_pre):
    s, d = h1.shape
    nc = D_FF // TN_FF
    ts = min(TS_FF, s)

    def body(dff_ref, gate_ref, up_ref, wd_ref, wg_ref, wu_ref, h1_ref, dh2_ref, g_ref,
             dgate_ref, dup_ref, dh1_ref, dg_ref, acc):
        i = pl.program_id(0)
        j = pl.program_id(1)

        @pl.when((i == 0) & (j == 0))
        def _():
            dg_ref[...] = jnp.zeros_like(dg_ref)

        @pl.when(j == 0)
        def _():
            acc[...] = jnp.zeros_like(acc)

        for r in range(2):
            rows = slice(r * (ts // 2), (r + 1) * (ts // 2))
            dact = _nt(dff_ref[rows, :], wd_ref[...])
            gt = gate_ref[rows, :].astype(F32)
            sg = jax.nn.sigmoid(gt)
            dup_ref[rows, :] = (dact * gt * sg).astype(BF16)
            dgate_ref[rows, :] = (dact * up_ref[rows, :].astype(F32) * (sg * (1.0 + gt * (1.0 - sg)))).astype(BF16)
            acc[rows, :] += _nn(dgate_ref[rows, :], wg_ref[...]) + _nn(dup_ref[rows, :], wu_ref[...])

        @pl.when(j == nc - 1)
        def _():
            dv, dg = _rms_bwd(h1_ref[...], g_ref[...], acc[...])
            dh1_ref[...] = dh2_ref[...] + dv
            dg_ref[...] += dg

    rowblk = pl.BlockSpec((ts, d), lambda i, j: (i, 0))
    wblk = pl.BlockSpec((TN_FF, d), lambda i, j: (j, 0))
    chunk = pl.BlockSpec((ts, TN_FF), lambda i, j: (i, j))
    vec = pl.BlockSpec((1, d), lambda i, j: (0, 0))
    return pl.pallas_call(
        body, grid=(s // ts, nc), name="ffn_bwd",
        out_shape=(jax.ShapeDtypeStruct((s, D_FF), BF16), jax.ShapeDtypeStruct((s, D_FF), BF16),
                   jax.ShapeDtypeStruct((s, d), F32), jax.ShapeDtypeStruct((1, d), F32)),
        in_specs=[rowblk, chunk, chunk, wblk, wblk, wblk, rowblk, rowblk, vec],
        out_specs=(chunk, chunk, rowblk, vec),
        scratch_shapes=[pltpu.VMEM((ts, d), F32)],
        compiler_params=_params(2),
    )(dff, gate, up, wd, wg, wu, h1, dh2, g_pre)


def _post_attn_bwd(dh1, o, a, mpre, wout, wpool, g_post, g_attn, g_pool, pscale, eye):
    s, d = dh1.shape
    sub = TS // TQ

    def body(dh1_ref, o_ref, a_ref, mp_ref, wo_ref, wp_ref, gpost_ref, ga_ref, gp_ref, ps_ref, eye_ref,
             dob_ref, dab_ref, dat_ref, dlt_ref, dmpb_ref, dy_ref, dgpost_ref, dga_ref, dgp_ref, dps_ref):
        i = pl.program_id(0)

        @pl.when(i == 0)
        def _():
            dgpost_ref[...] = jnp.zeros_like(dgpost_ref)
            dga_ref[...] = jnp.zeros_like(dga_ref)
            dgp_ref[...] = jnp.zeros_like(dgp_ref)
            dps_ref[...] = jnp.zeros_like(dps_ref)

        do, dg = _rms_bwd(o_ref[...], gpost_ref[...], dh1_ref[...])
        dgpost_ref[...] += dg
        dob = do.astype(BF16)
        dob_ref[...] = dob
        dmix = _nt(dob, wo_ref[...])

        av = a_ref[...]
        da, dg = _rms_bwd(av, ga_ref[...], dmix[:, 0:D_ATTN])
        dga_ref[...] += dg
        dab = da.astype(BF16)
        dab_ref[...] = dab
        dat = _nt(eye_ref[...], dab).astype(BF16)
        hsel = (lax.shift_right_logical(lax.broadcasted_iota(jnp.int32, (HEADS, D_ATTN), 1), 6)
                == lax.broadcasted_iota(jnp.int32, (HEADS, D_ATTN), 0)).astype(F32)
        dlt = lax.dot_general(hsel, da * av, (((1,), (1,)), ((), ())), precision=HIGHEST, preferred_element_type=F32)
        for q in range(sub):
            dlt_ref[q] = dlt[:, q * TQ:(q + 1) * TQ]
            dat_ref[q] = dat[:, q * TQ:(q + 1) * TQ]

        ps = ps_ref[...]
        mp = mp_ref[...]
        dm, dg = _rms_bwd(mp * ps, gp_ref[...], dmix[:, D_ATTN:])
        dgp_ref[...] += dg
        dps_ref[...] += jnp.sum(dm * mp, axis=0, keepdims=True)
        dmpb = (dm * ps).astype(BF16)
        dmpb_ref[...] = dmpb
        for g in range(len(POOL_WINDOWS)):
            cols = slice(g * POOL_CH, (g + 1) * POOL_CH)
            dy_ref[:, cols] = _nt(dmpb[:, cols], wp_ref[g])

    rowblk = pl.BlockSpec((TS, d), _row)
    half = pl.BlockSpec((TS, D_ATTN), _row)
    vec = lambda n: pl.BlockSpec((1, n), _fixed)
    return pl.pallas_call(
        body, grid=(s // TS,), name="post_attn_bwd",
        out_shape=(jax.ShapeDtypeStruct((s, d), BF16), jax.ShapeDtypeStruct((s, D_ATTN), BF16),
                   jax.ShapeDtypeStruct((s // TQ, D_ATTN, TQ), BF16),
                   jax.ShapeDtypeStruct((s // TQ, HEADS, TQ), F32), jax.ShapeDtypeStruct((s, D_POOL), BF16),
                   jax.ShapeDtypeStruct((s, D_POOL), F32), jax.ShapeDtypeStruct((1, d), F32),
                   jax.ShapeDtypeStruct((1, D_ATTN), F32), jax.ShapeDtypeStruct((1, D_POOL), F32),
                   jax.ShapeDtypeStruct((1, D_POOL), F32)),
        in_specs=[rowblk, rowblk, half, half, pl.BlockSpec(wout.shape, _fixed),
                  pl.BlockSpec(wpool.shape, lambda i: (0, 0, 0)), vec(d), vec(D_ATTN), vec(D_POOL), vec(D_POOL),
                  pl.BlockSpec(eye.shape, _fixed)],
        out_specs=(rowblk, half, pl.BlockSpec((sub, D_ATTN, TQ), lambda i: (i, 0, 0)),
                   pl.BlockSpec((sub, HEADS, TQ), lambda i: (i, 0, 0)), half, half,
                   vec(d), vec(D_ATTN), vec(D_POOL), vec(D_POOL)),
        compiler_params=_params(1),
    )(dh1, o, a, mpre, wout, wpool, g_post, g_attn, g_pool, pscale, eye)


def _attn_bwd(ka, v, kt3, qat3, q, do, dot3, lset3, dlt3, chip_blocks):
    s = q.shape[0]
    nq = s // TQ
    wide = HEADS * LANES

    def body(ka_ref, v_ref, kt_ref, qat_ref, q_ref, do_ref, dot_ref, lset_ref, dlt_ref, b_ref,
             dqt_ref, dk_ref, dv_ref, dcs_ref, drs_ref, got_ref, dca, dkw, dvw, pt_scr, ptb_scr, dsb_scr,
             stage, send_sems, recv_sems, local_sem):
        j = pl.program_id(0)

        @pl.when(j == 0)
        def _():
            _chips_start(b_ref, got_ref, stage, send_sems, recv_sems, local_sem)
            dqt_ref[...] = jnp.zeros_like(dqt_ref)
            drs_ref[...] = jnp.zeros_like(drs_ref)

        dkw[...] = jnp.zeros_like(dkw)
        dvw[...] = jnp.zeros_like(dvw)
        dca[...] = jnp.zeros_like(dca)

        def tile(i, masked):
            rows = pl.ds(i * TQ, TQ)
            for h in range(HEADS):
                aug = slice(h * AUG, (h + 1) * AUG)
                st = _nn(ka_ref[:, aug], qat_ref[i, aug, :]) - lset_ref[i, h:h + 1, :]
                if masked:
                    st = jnp.where(_causal_in_tile(), st, NEG)
                pt = jnp.exp(st)
                pt_scr[h] = pt
                ptb_scr[h] = pt.astype(BF16)
            for h in range(HEADS):
                hs = slice(h * HEAD_DIM, (h + 1) * HEAD_DIM)
                half = slice(h * LANES, h * LANES + HEAD_DIM)
                dvw[:, half] += _nn(ptb_scr[h], do_ref[rows, hs])
                dst = pt_scr[h] * (_nn(v_ref[:, hs], dot_ref[i, hs, :]) - dlt_ref[i, h:h + 1, :])
                dsb_scr[h] = dst.astype(BF16)
                drs_ref[i, h, 0:1, :] += jnp.sum(dst, axis=0, keepdims=True)
                dca[:, h * LANES:(h + 1) * LANES] += dst[:, 0:LANES] + dst[:, LANES:2 * LANES]
            for h in range(HEADS):
                hs = slice(h * HEAD_DIM, (h + 1) * HEAD_DIM)
                half = slice(h * LANES, h * LANES + HEAD_DIM)
                dkw[:, half] += _nn(dsb_scr[h], q_ref[rows, hs])
                dqt_ref[i, hs, :] += _nn(kt_ref[0, hs, :], dsb_scr[h])

        def step(i, carry):
            tile(i, False)
            return carry

        tile(j, True)
        lax.fori_loop(j + 1, nq, step, 0)
        lane = lax.broadcasted_iota(jnp.int32, (TQ, LANES), 1)
        dcs_all = jnp.zeros((TQ, LANES), F32)
        for h in range(HEADS):
            hs = slice(h * HEAD_DIM, (h + 1) * HEAD_DIM)
            half = slice(h * LANES, h * LANES + HEAD_DIM)
            dk_ref[:, hs] = dkw[:, half]
            dv_ref[:, hs] = dvw[:, half]
            colsum = jnp.sum(dca[:, h * LANES:(h + 1) * LANES], axis=1, keepdims=True)
            dcs_all = jnp.where(lane == h, colsum, dcs_all)
        dcs_ref[...] = dcs_all

        @pl.when(j == nq - 1)
        def _():
            _chips_finish(b_ref, got_ref, send_sems, recv_sems)

    blk = pl.BlockSpec((TQ, D_ATTN), _row)
    _, r, cdim = chip_blocks.shape
    return pl.pallas_call(
        body, grid=(nq,), name="attn_bwd",
        out_shape=(jax.ShapeDtypeStruct((nq, D_ATTN, TQ), F32), jax.ShapeDtypeStruct((s, D_ATTN), F32),
                   jax.ShapeDtypeStruct((s, D_ATTN), F32), jax.ShapeDtypeStruct((s, LANES), F32),
                   jax.ShapeDtypeStruct((nq, HEADS, 8, TQ), F32),
                   jax.ShapeDtypeStruct(chip_blocks.shape, chip_blocks.dtype)),
        in_specs=[pl.BlockSpec((TQ, HEADS * AUG), _row), blk, pl.BlockSpec((1, D_ATTN, TQ), lambda j: (j, 0, 0)),
                  VMEM_WHOLE, VMEM_WHOLE, VMEM_WHOLE, VMEM_WHOLE, VMEM_WHOLE, VMEM_WHOLE, ANY],
        out_specs=(pl.BlockSpec((nq, D_ATTN, TQ), lambda j: (0, 0, 0)), blk, blk, pl.BlockSpec((TQ, LANES), _row),
                   pl.BlockSpec((nq, HEADS, 8, TQ), lambda j: (0, 0, 0, 0)), ANY),
        scratch_shapes=[pltpu.VMEM((TQ, wide), F32), pltpu.VMEM((TQ, wide), F32), pltpu.VMEM((TQ, wide), F32),
                        pltpu.VMEM((HEADS, TQ, TQ), F32), pltpu.VMEM((HEADS, TQ, TQ), BF16),
                        pltpu.VMEM((HEADS, TQ, TQ), BF16), pltpu.VMEM((r, cdim), chip_blocks.dtype),
                        pltpu.SemaphoreType.DMA((3,)), pltpu.SemaphoreType.DMA((3,)), pltpu.SemaphoreType.DMA],
        compiler_params=_params(1),
    )(ka, v, kt3, qat3, q, do, dot3, lset3, dlt3, chip_blocks)


def _pre_attn_bwd(dqt3, dk, dv, dcs, drs, fl, dy, x, dh1, g1, wqkv, wf, wu):
    s, d = x.shape
    nt = s // TS
    n = TS + HALO
    sub = TS // TQ

    def body(dqt_ref, dk_ref, dv_ref, dcs_ref, drs_ref, fl_ref, dy_ref, x_ref, dh1_ref, g_ref, wqkv_ref, wf_ref, wu_ref,
             gx_ref, dqkv_ref, dfb_ref, dub_ref, dg_ref, db_ref, ybuf, ccar, dlog):
        i = pl.program_id(0)
        ti = nt - 1 - i

        @pl.when(i == 0)
        def _():
            ybuf[TS:n, :] = jnp.zeros((HALO, D_POOL), F32)
            ccar[...] = jnp.zeros_like(ccar)
            dg_ref[...] = jnp.zeros_like(dg_ref)
            db_ref[...] = jnp.zeros_like(db_ref)

        rr = lax.broadcasted_iota(jnp.int32, (TS, TS), 0)
        cc = lax.broadcasted_iota(jnp.int32, (TS, TS), 1)
        triu = (cc >= rr).astype(F32)
        dlog[...] = ccar[...] + jnp.dot(triu, drs_ref[...] - dcs_ref[...], precision=HIGHEST, preferred_element_type=F32)
        ccar[...] = dlog[0:1, :]
        df = dlog[...] * jax.nn.sigmoid(-fl_ref[...])
        db_ref[...] += jnp.sum(df, axis=0, keepdims=True)
        dfb = df.astype(BF16)
        dfb_ref[...] = dfb

        t = ti * TS + lax.broadcasted_iota(jnp.int32, (TS, 1), 0)
        dy = dy_ref[...]
        for g, w in enumerate(POOL_WINDOWS):
            cols = slice(g * POOL_CH, (g + 1) * POOL_CH)
            ybuf[0:TS, cols] = dy[:, cols] / jnp.minimum(t + 1, w).astype(F32)
        for g, w in enumerate(POOL_WINDOWS):
            cols = slice(g * POOL_CH, (g + 1) * POOL_CH)
            sm = ybuf[:, cols]
            step = 1
            while step < w:
                sm = sm + pltpu.roll(sm, n - step, 0)
                step *= 2
            dub_ref[:, cols] = (sm[0:TS, :] - dy[:, cols]).astype(BF16)
        ybuf[TS:n, :] = ybuf[0:HALO, :]

        for a in range(sub):
            dqkv_ref[a * TQ:(a + 1) * TQ, 0:D_ATTN] = (dqt_ref[a].T * 0.125).astype(BF16)
        dqkv_ref[:, D_ATTN:2 * D_ATTN] = dk_ref[...].astype(BF16)
        dqkv_ref[:, 2 * D_ATTN:] = dv_ref[...].astype(BF16)
        dhn = _nn(dqkv_ref[...], wqkv_ref[...]) + _nn(dfb, wf_ref[...]) + _nn(dub_ref[...], wu_ref[...])
        dx, dg = _rms_bwd(x_ref[...], g_ref[...], dhn)
        gx_ref[...] = dh1_ref[...] + dx
        dg_ref[...] += dg

    rev = lambda i: (nt - 1 - i, 0)
    blk = lambda w: pl.BlockSpec((TS, w), rev)
    return pl.pallas_call(
        body, grid=(nt,), name="pre_attn_bwd",
        out_shape=(jax.ShapeDtypeStruct((s, d), F32), jax.ShapeDtypeStruct((s, 3 * D_ATTN), BF16),
                   jax.ShapeDtypeStruct((s, LANES), BF16), jax.ShapeDtypeStruct((s, D_POOL), BF16),
                   jax.ShapeDtypeStruct((1, d), F32), jax.ShapeDtypeStruct((1, LANES), F32)),
        in_specs=[pl.BlockSpec((sub, D_ATTN, TQ), lambda i: (nt - 1 - i, 0, 0)),
                  blk(D_ATTN), blk(D_ATTN), blk(LANES), blk(LANES), blk(LANES), blk(D_POOL), blk(d), blk(d),
                  pl.BlockSpec((1, d), _fixed), pl.BlockSpec(wqkv.shape, _fixed), pl.BlockSpec(wf.shape, _fixed),
                  pl.BlockSpec(wu.shape, _fixed)],
        out_specs=(blk(d), blk(3 * D_ATTN), blk(LANES), blk(D_POOL),
                   pl.BlockSpec((1, d), _fixed), pl.BlockSpec((1, LANES), _fixed)),
        scratch_shapes=[pltpu.VMEM((n, D_POOL), F32), pltpu.VMEM((1, LANES), F32), pltpu.VMEM((TS, LANES), F32)],
        compiler_params=_params(1),
    )(dqt3, dk, dv, dcs, drs, fl, dy, x, dh1, g1, wqkv, wf, wu)


def _wgrad(a, b, out_dtype, name):
    s, m = a.shape
    n = b.shape[1]
    tm = max(t for t in range(LANES, min(m, TM_WGRAD) + 1, LANES) if m % t == 0)
    ns = s // TS

    def body(a_ref, b_ref, o_ref, acc):
        i = pl.program_id(1)

        @pl.when(i == 0)
        def _():
            acc[...] = jnp.zeros_like(acc)

        acc[...] += _tn(a_ref[...], b_ref[pl.ds(i * TS, TS), :])

        @pl.when(i == ns - 1)
        def _():
            o_ref[...] = acc[...].astype(out_dtype)

    return pl.pallas_call(
        body, grid=(m // tm, ns), name=name, out_shape=jax.ShapeDtypeStruct((m, n), out_dtype),
        in_specs=[pl.BlockSpec((TS, tm), lambda j, i: (i, j)), VMEM_WHOLE],
        out_specs=pl.BlockSpec((tm, n), lambda j, i: (j, 0)),
        scratch_shapes=[pltpu.VMEM((tm, n), F32)],
        compiler_params=_params(2),
    )(a, b)


def _adamw(w, g, m, v):
    m = ADAM_B1 * m + (1.0 - ADAM_B1) * g
    v = ADAM_B2 * v + (1.0 - ADAM_B2) * (g * g)
    m_hat = m / (1.0 - ADAM_B1 ** ADAM_STEP)
    v_hat = v / (1.0 - ADAM_B2 ** ADAM_STEP)
    delta = -ADAM_LR * (m_hat / (jnp.sqrt(v_hat) + ADAM_EPS) + ADAM_WD * w)
    return delta, m, v


def _pair_sum(core, t, theirs, tr, name):
    nk, r, c = theirs.shape

    def body(core_ref, a_ref, b_ref, o_ref):
        o_ref[...] = (a_ref[...].astype(F32) + b_ref[...].astype(F32)).astype(BF16)

    blk = pl.BlockSpec((1, tr, c), lambda k, i, core_ref: (k, i, 0))
    return pl.pallas_call(
        body, name=name, out_shape=jax.ShapeDtypeStruct(theirs.shape, BF16),
        grid_spec=pltpu.PrefetchScalarGridSpec(
            num_scalar_prefetch=1, grid=(nk, r // tr),
            in_specs=[pl.BlockSpec((1, tr, c), lambda k, i, core_ref: (2 * k + core_ref[0], i, 0)), blk],
            out_specs=blk),
        compiler_params=_params(2),
    )(core, t, theirs)


def _reduce_update_big(parts, w, m, v, tr, name):
    nk, r, c = parts.shape

    def body(p_ref, w_ref, m_ref, v_ref, g_ref, d_ref, nm_ref, nv_ref):
        g = p_ref[0].astype(F32)
        for k in range(1, nk):
            g = g + p_ref[k].astype(F32)
        g_ref[...] = g
        d_ref[...], nm_ref[...], nv_ref[...] = _adamw(w_ref[...], g, m_ref[...], v_ref[...])

    blk = pl.BlockSpec((tr, c), _row)
    out = jax.ShapeDtypeStruct((r, c), F32)
    return pl.pallas_call(
        body, grid=(r // tr,), name=name, out_shape=(out,) * 4,
        in_specs=[pl.BlockSpec((nk, tr, c), lambda i: (0, i, 0)), blk, blk, blk],
        out_specs=(blk,) * 4, compiler_params=_params(1),
    )(parts, w, m, v)


def _reduce_update_small(parts, w, m, v):
    nd = parts.shape[0]

    def body(p_ref, w_ref, m_ref, v_ref, g_ref, d_ref, nm_ref, nv_ref):
        g = p_ref[0]
        for k in range(1, nd):
            g = g + p_ref[k]
        g_ref[...] = g
        d_ref[...], nm_ref[...], nv_ref[...] = _adamw(w_ref[...], g, m_ref[...], v_ref[...])

    out = jax.ShapeDtypeStruct(w.shape, F32)
    return pl.pallas_call(body, name="reduce_update_small", out_shape=(out,) * 4,
                          compiler_params=pltpu.CompilerParams(vmem_limit_bytes=VMEM_LIMIT))(parts, w, m, v)


MESH = pl.DeviceIdType.MESH


def _copy_through_vmem(src_hbm, dst_hbm, stage, sem):
    load = pltpu.make_async_copy(src_hbm, stage, sem)
    load.start()
    load.wait()
    store = pltpu.make_async_copy(stage, dst_hbm, sem)
    store.start()
    store.wait()


class _GatherPlan:
    def __init__(self, x_ref, out_ref, send_sems, recv_sems):
        x, y, c = lax.axis_index("x"), lax.axis_index("y"), lax.axis_index("c")
        self.me, self.sibling, self.c = (x, y, c), (x, y, 1 - c), c
        self.chips = [(1 - x, y), (x, 1 - y), (1 - x, 1 - y)]
        self.x_ref, self.out_ref, self.send_sems, self.recv_sems = x_ref, out_ref, send_sems, recv_sems

    def slot(self, px, py, pc):
        return self.out_ref.at[4 * px + 2 * py + pc]

    def copy(self, k, block, to, src=None):
        return pltpu.make_async_remote_copy(
            src_ref=self.slot(*block) if src is None else src, dst_ref=self.slot(*block),
            send_sem=self.send_sems.at[k], recv_sem=self.recv_sems.at[k], device_id=to, device_id_type=MESH)

    def first(self):
        return [self.copy(0, self.me, self.sibling, src=self.x_ref)] + [
            self.copy(1 + j, self.me, (*chip, self.c), src=self.x_ref) for j, chip in enumerate(self.chips)]

    def passed(self):
        return [self.copy(4 + j, (*chip, self.c), self.sibling) for j, chip in enumerate(self.chips)]


def _gather_start(x_ref, out_ref, stage, send_sems, recv_sems, local_sem):
    plan = _GatherPlan(x_ref, out_ref, send_sems, recv_sems)
    for cp in plan.first():
        cp.start()
    _copy_through_vmem(x_ref, plan.slot(*plan.me), stage, local_sem)


def _gather_pass_on(out_ref, send_sems, recv_sems):
    plan = _GatherPlan(None, out_ref, send_sems, recv_sems)
    passed = plan.passed()
    for j, chip in enumerate(plan.chips):
        plan.copy(1 + j, (*chip, plan.c), plan.me).wait_recv()
        passed[j].start()


def _gather_finish(x_ref, out_ref, send_sems, recv_sems):
    plan = _GatherPlan(x_ref, out_ref, send_sems, recv_sems)
    plan.copy(0, plan.sibling, plan.me).wait_recv()
    for j, chip in enumerate(plan.chips):
        plan.copy(4 + j, (*chip, 1 - plan.c), plan.me).wait_recv()
    for cp in plan.first() + plan.passed():
        cp.wait_send()


def _all_gather(xs, name):
    r, cdim = xs.shape

    def body(x_ref, out_ref, stage, send_sems, recv_sems, local_sem):
        _gather_start(x_ref, out_ref, stage, send_sems, recv_sems, local_sem)
        _gather_pass_on(out_ref, send_sems, recv_sems)
        _gather_finish(x_ref, out_ref, send_sems, recv_sems)

    return pl.pallas_call(
        body, name=name, out_shape=jax.ShapeDtypeStruct((N_DEV, r, cdim), xs.dtype),
        in_specs=[ANY], out_specs=ANY,
        scratch_shapes=[pltpu.VMEM((r, cdim), xs.dtype), pltpu.SemaphoreType.DMA((7,)), pltpu.SemaphoreType.DMA((7,)),
                        pltpu.SemaphoreType.DMA],
        compiler_params=pltpu.CompilerParams(vmem_limit_bytes=VMEM_LIMIT),
    )(xs)


def _rs_pair(t, name):
    _, r, cdim = t.shape

    def body(t_ref, theirs_ref, send_sems, recv_sems):
        x, y, c = lax.axis_index("x"), lax.axis_index("y"), lax.axis_index("c")
        remote = [pltpu.make_async_remote_copy(
            src_ref=t_ref.at[2 * k + (1 - c)], dst_ref=theirs_ref.at[k],
            send_sem=send_sems.at[k], recv_sem=recv_sems.at[k], device_id=(x, y, 1 - c), device_id_type=MESH)
            for k in range(4)]
        for cp in remote:
            cp.start()
        for cp in remote:
            cp.wait()

    return pl.pallas_call(
        body, name=name, out_shape=jax.ShapeDtypeStruct((4, r, cdim), t.dtype), in_specs=[ANY], out_specs=ANY,
        scratch_shapes=[pltpu.SemaphoreType.DMA((4,)), pltpu.SemaphoreType.DMA((4,))],
    )(t)


def _chips_start(b_ref, out_ref, stage, send_sems, recv_sems, local_sem):
    x, y, c = lax.axis_index("x"), lax.axis_index("y"), lax.axis_index("c")
    mychip = 2 * x + y
    for j, (px, py) in enumerate([(1 - x, y), (x, 1 - y), (1 - x, 1 - y)]):
        pltpu.make_async_remote_copy(
            src_ref=b_ref.at[2 * px + py], dst_ref=out_ref.at[mychip],
            send_sem=send_sems.at[j], recv_sem=recv_sems.at[j], device_id=(px, py, c), device_id_type=MESH).start()
    _copy_through_vmem(b_ref.at[mychip], out_ref.at[mychip], stage, local_sem)


def _chips_finish(b_ref, out_ref, send_sems, recv_sems):
    x, y, c = lax.axis_index("x"), lax.axis_index("y"), lax.axis_index("c")
    for j, (px, py) in enumerate([(1 - x, y), (x, 1 - y), (1 - x, 1 - y)]):
        pltpu.make_async_remote_copy(
            src_ref=b_ref.at[2 * px + py], dst_ref=out_ref.at[2 * px + py],
            send_sem=send_sems.at[j], recv_sem=recv_sems.at[j], device_id=(px, py, c), device_id_type=MESH).wait()


def _rs_chips(b, name):
    _, r, cdim = b.shape

    def body(b_ref, out_ref, stage, send_sems, recv_sems, local_sem):
        _chips_start(b_ref, out_ref, stage, send_sems, recv_sems, local_sem)
        _chips_finish(b_ref, out_ref, send_sems, recv_sems)

    return pl.pallas_call(
        body, name=name, out_shape=jax.ShapeDtypeStruct(b.shape, b.dtype), in_specs=[ANY], out_specs=ANY,
        scratch_shapes=[pltpu.VMEM((r, cdim), b.dtype), pltpu.SemaphoreType.DMA((3,)), pltpu.SemaphoreType.DMA((3,)),
                        pltpu.SemaphoreType.DMA],
        compiler_params=pltpu.CompilerParams(vmem_limit_bytes=VMEM_LIMIT),
    )(b)


def _pad_rows(a, rows):
    return jnp.pad(a, ((0, rows - a.shape[0]), (0, 0)))


def _pack_in(w_in):
    return _pad_rows(w_in[0].T, ROWS_IN)


def _unpack_in(r):
    return r[0:SHARD_IN].T[None]


def _pack_rest(w_out, w_gate, w_up, w_down, w_ple, w_pg):
    return jnp.concatenate([w_out[0], w_gate[0].T, w_up[0].T, w_down[0], w_ple[0].T.reshape(32, D_MODEL), w_pg[0]],
                           axis=0)


def _unpack_rest(r):
    return (r[0:OFF_GATE][None], r[OFF_GATE:OFF_UP].T[None], r[OFF_UP:OFF_DOWN].T[None],
            r[OFF_DOWN:OFF_PLE][None], r[OFF_PLE:OFF_PG].reshape(128, D_PLE).T[None], r[OFF_PG:ROWS_REST][None])


def _full_rest(g):
    return (g[:, 0:OFF_GATE].reshape(D_MODEL, D_MODEL), g[:, OFF_GATE:OFF_UP].reshape(D_FF, D_MODEL),
            g[:, OFF_UP:OFF_DOWN].reshape(D_FF, D_MODEL), g[:, OFF_DOWN:OFF_PLE].reshape(D_FF, D_MODEL),
            g[:, OFF_PLE:OFF_PG].reshape(D_MODEL, D_PLE), g[:, OFF_PG:ROWS_REST].reshape(D_MODEL, D_MODEL))


def _pack_small(w_pool, g_mix_pre, g_mix_post, g_ffn_pre, g_ffn_post, g_ple, g_attn, g_pool, pool_scale, b_forget,
                loss=None):
    def row(vrow):
        return jnp.pad(vrow.reshape(1, -1), ((0, 0), (0, D_MODEL - vrow.size)))
    rows = [w_pool.reshape(64, D_MODEL), row(g_mix_pre), row(g_mix_post), row(g_ffn_pre), row(g_ffn_post), row(g_ple),
            row(g_attn), row(g_pool), row(pool_scale), row(b_forget),
            row(loss) if loss is not None else jnp.zeros((1, D_MODEL), F32)]
    return _pad_rows(jnp.concatenate(rows, axis=0), SMALL_ROWS)


def _unpack_small(r):
    return dict(
        w_pool=r[0:64].reshape(1, 4, POOL_CH, POOL_CH), g_mix_pre=r[ROW_G_MIX_PRE:ROW_G_MIX_PRE + 1],
        g_mix_post=r[ROW_G_MIX_POST:ROW_G_MIX_POST + 1], g_ffn_pre=r[ROW_G_FFN_PRE:ROW_G_FFN_PRE + 1],
        g_ffn_post=r[ROW_G_FFN_POST:ROW_G_FFN_POST + 1], g_ple=r[ROW_G_PLE:ROW_G_PLE + 1],
        g_attn_grp=r[ROW_G_ATTN:ROW_G_ATTN + 1, 0:D_ATTN], g_pool_grp=r[ROW_G_POOL:ROW_G_POOL + 1, 0:D_POOL],
        pool_scale=r[ROW_POOL_SCALE:ROW_POOL_SCALE + 1, 0:D_POOL], b_forget=r[ROW_B_FORGET:ROW_B_FORGET + 1, 0:HEADS])


def _step(x, p, tgt, small, in_w, in_m, in_v, rest_w, rest_m, rest_v):
    core = lax.axis_index("c").astype(jnp.int32).reshape(1)
    win_t = _all_gather(in_w.astype(BF16), "gather_w_in")[:, 0:SHARD_IN].reshape(D_IN, D_MODEL)
    wqkv = win_t[0:3 * D_ATTN]
    wf = _pad_rows(win_t[3 * D_ATTN:3 * D_ATTN + HEADS], LANES)
    wu = win_t[3 * D_ATTN + HEADS:]
    wpool = small["w_pool"].astype(BF16)
    bpad = jnp.pad(small["b_forget"], ((0, 0), (0, LANES - HEADS)))

    lay = _attn_layout_constants()
    hn, q, ka, v, qat3, vt3, kt3, fl, y, mpre = _pre_attn_fwd(x, small["g_mix_pre"], wqkv, wf, wu, bpad, wpool, lay)
    a, lset3, gathered = _attn_fwd(ka, qat3, vt3, rest_w.astype(BF16))
    wout, wg_t, wu_t, wd, wple_t, wpg = _full_rest(gathered)
    mix, o, h1, hn2 = _post_attn_fwd(a, mpre, x, small["g_attn_grp"], small["g_pool_grp"], small["pool_scale"], wout,
                                     small["g_mix_post"], small["g_ffn_pre"])
    gate, up, act, ff, h2 = _ffn_fwd(hn2, wg_t, wu_t, wd, h1, small["g_ffn_post"])
    dh2, dff, dgl, dpp, h2b, pb, loss8, dg_ple, dg_ffn_post = _tail_fwd_bwd(
        h2, p, tgt, ff, wple_t, wpg, small["g_ple"], small["g_ffn_post"])
    dgate, dup, dh1, dg_ffn_pre = _ffn_bwd(dff, gate, up, wd, wg_t, wu_t, h1, dh2, small["g_ffn_pre"])
    dob, dab, dat3, dlt3, dmpb, dy, dg_mix_post, dg_attn, dg_pool, dps = _post_attn_bwd(
        dh1, o, a, mpre, wout, wpool, small["g_mix_post"], small["g_attn_grp"], small["g_pool_grp"], small["pool_scale"],
        lay["eye"])

    nd = N_DEV
    send_rest = jnp.concatenate([
        _wgrad(mix, dob, BF16, "wgrad_out").reshape(nd, 128, D_MODEL),
        _wgrad(dgate, hn2, BF16, "wgrad_gate").reshape(nd, 352, D_MODEL),
        _wgrad(dup, hn2, BF16, "wgrad_up").reshape(nd, 352, D_MODEL),
        _wgrad(act, dff, BF16, "wgrad_down").reshape(nd, 352, D_MODEL),
        _wgrad(dpp, pb, BF16, "wgrad_ple").reshape(nd, 32, D_MODEL),
        _wgrad(h2b, dgl, BF16, "wgrad_ple_gate").reshape(nd, 128, D_MODEL)], axis=1)
    pair_rest = _pair_sum(core, send_rest, _rs_pair(send_rest, "rs_pair_rest"), TR_REST, "rs_pair_sum_rest")

    dqt3, dk, dv, dcs, drs4, chips_rest = _attn_bwd(ka, v, kt3, qat3, q, dab, dat3, lset3, dlt3, pair_rest)
    upd_rest = _reduce_update_big(chips_rest, rest_w, rest_m, rest_v, TR_REST, "reduce_update_rest")
    drs = jnp.pad(drs4[:, :, 0, :].transpose(0, 2, 1).reshape(-1, HEADS), ((0, 0), (0, LANES - HEADS)))
    gx, dqkv, dfb, dub, dg_mix_pre, db = _pre_attn_bwd(dqt3, dk, dv, dcs, drs, fl, dy, x, dh1, small["g_mix_pre"], wqkv, wf, wu)

    dwin_t = jnp.concatenate([_wgrad(dqkv, hn, F32, "wgrad_qkv"), _wgrad(dfb, hn, F32, "wgrad_forget")[0:HEADS],
                              _wgrad(dub, hn, F32, "wgrad_pool_in")], axis=0)
    send_in = jnp.pad(dwin_t.reshape(nd, SHARD_IN, D_MODEL), ((0, 0), (0, ROWS_IN - SHARD_IN), (0, 0))).astype(BF16)
    pair_in = _pair_sum(core, send_in, _rs_pair(send_in, "rs_pair_in"), ROWS_IN, "rs_pair_sum_in")
    upd_in = _reduce_update_big(_rs_chips(pair_in, "rs_chips_in"), in_w, in_m, in_v, ROWS_IN, "reduce_update_in")

    dwp = _wgrad(y, dmpb, F32, "wgrad_pool")
    dw_pool = jnp.stack([dwp[g * POOL_CH:(g + 1) * POOL_CH, g * POOL_CH:(g + 1) * POOL_CH] for g in range(4)])
    small_part = _pack_small(dw_pool, dg_mix_pre, dg_mix_post, dg_ffn_pre, dg_ffn_post, dg_ple, dg_attn, dg_pool, dps,
                             db[:, 0:HEADS], loss8[0:1, 0:1])
    return gx, small_part, upd_in, upd_rest


def kernel(x, p, g_mix_pre, w_in, b_forget, g_attn_grp, g_pool_grp, w_pool, pool_scale, w_out, g_mix_post, g_ffn_pre, w_ffn_gate, w_ffn_up, w_ffn_down, g_ffn_post, w_ple_proj, g_ple, w_ple_gate, loss_target, m_g_mix_pre, m_w_in, m_b_forget, m_g_attn_grp, m_g_pool_grp, m_w_pool, m_pool_scale, m_w_out, m_g_mix_post, m_g_ffn_pre, m_w_ffn_gate, m_w_ffn_up, m_w_ffn_down, m_g_ffn_post, m_w_ple_proj, m_g_ple, m_w_ple_gate, v_g_mix_pre, v_w_in, v_b_forget, v_g_attn_grp, v_g_pool_grp, v_w_pool, v_pool_scale, v_w_out, v_g_mix_post, v_g_ffn_pre, v_w_ffn_gate, v_w_ffn_up, v_w_ffn_down, v_g_ffn_post, v_w_ple_proj, v_g_ple, v_w_ple_gate):
    small = dict(w_pool=w_pool[0], g_mix_pre=g_mix_pre, g_mix_post=g_mix_post, g_ffn_pre=g_ffn_pre,
                 g_ffn_post=g_ffn_post, g_ple=g_ple, g_attn_grp=g_attn_grp, g_pool_grp=g_pool_grp,
                 pool_scale=pool_scale, b_forget=b_forget)
    gx, small_part, upd_in, upd_rest = _step(
        x[0], p[0, 0], loss_target[0], small, _pack_in(w_in), _pack_in(m_w_in), _pack_in(v_w_in),
        _pack_rest(w_out, w_ffn_gate, w_ffn_up, w_ffn_down, w_ple_proj, w_ple_gate),
        _pack_rest(m_w_out, m_w_ffn_gate, m_w_ffn_up, m_w_ffn_down, m_w_ple_proj, m_w_ple_gate),
        _pack_rest(v_w_out, v_w_ffn_gate, v_w_ffn_up, v_w_ffn_down, v_w_ple_proj, v_w_ple_gate))

    small_all = _all_gather(small_part, "gather_small")
    sm_w = _pack_small(w_pool, g_mix_pre, g_mix_post, g_ffn_pre, g_ffn_post, g_ple, g_attn_grp, g_pool_grp, pool_scale, b_forget)
    sm_m = _pack_small(m_w_pool, m_g_mix_pre, m_g_mix_post, m_g_ffn_pre, m_g_ffn_post, m_g_ple, m_g_attn_grp, m_g_pool_grp, m_pool_scale, m_b_forget)
    sm_v = _pack_small(v_w_pool, v_g_mix_pre, v_g_mix_post, v_g_ffn_pre, v_g_ffn_post, v_g_ple, v_g_attn_grp, v_g_pool_grp, v_pool_scale, v_b_forget)
    upd_small = _reduce_update_small(small_all, sm_w, sm_m, sm_v)
    loss = upd_small[0][ROW_LOSS, 0]

    def leaves(k):
        b_out, b_gate, b_up, b_down, b_ple, b_pg = _unpack_rest(upd_rest[k])
        s = _unpack_small(upd_small[k])
        return (s["g_mix_pre"], _unpack_in(upd_in[k]), s["b_forget"], s["g_attn_grp"], s["g_pool_grp"], s["w_pool"],
                s["pool_scale"], b_out, s["g_mix_post"], s["g_ffn_pre"], b_gate, b_up, b_down, s["g_ffn_post"], b_ple,
                s["g_ple"], b_pg)

    return (loss, gx[None], *leaves(0), *leaves(1), *leaves(2), *leaves(3))
```

```python
import functools

import jax
import jax.numpy as jnp
from jax import lax
from jax.experimental import pallas as pl
from jax.experimental.pallas import tpu as pltpu

F32 = jnp.float32
BF16 = jnp.bfloat16
HIGHEST = lax.Precision.HIGHEST

D_MODEL = 1024
HEADS = 8
HEAD_DIM = 64
D_ATTN = HEADS * HEAD_DIM
POOL_WINDOWS = (2, 4, 8, 16)
POOL_CH = 128
D_POOL = POOL_CH * len(POOL_WINDOWS)
D_FF = 2816
D_PLE = 256
D_IN = 3 * D_ATTN + HEADS + D_POOL
RMS_EPS = 1e-6
N_DEV = 8

ADAM_LR = 0.001
ADAM_B1 = 0.9
ADAM_B2 = 0.999
ADAM_EPS = 1e-08
ADAM_WD = 0.01
ADAM_STEP = 10

LANES = 128
HALO = 16
TS = 512
TS_FF = 1024
TM_WGRAD = 2176
TQ = 256
TN_FF = 256
NEG = -1e30
VMEM_LIMIT = 56 * 1024 * 1024

SHARD_IN = 257
ROWS_IN = 272
OFF_GATE = 128
OFF_UP = OFF_GATE + 352
OFF_DOWN = OFF_UP + 352
OFF_PLE = OFF_DOWN + 352
OFF_PG = OFF_PLE + 32
ROWS_REST = OFF_PG + 128
TR_REST = 192

SMALL_ROWS = 80
ROW_G_MIX_PRE, ROW_G_MIX_POST, ROW_G_FFN_PRE, ROW_G_FFN_POST, ROW_G_PLE = 64, 65, 66, 67, 68
ROW_G_ATTN, ROW_G_POOL, ROW_POOL_SCALE, ROW_B_FORGET, ROW_LOSS = 69, 70, 71, 72, 73


def _nn(a, b):
    return jnp.dot(a, b, preferred_element_type=F32)


def _nt(a, b):
    return lax.dot_general(a, b, (((1,), (1,)), ((), ())), preferred_element_type=F32)


def _tn(a, b):
    return lax.dot_general(a, b, (((0,), (0,)), ((), ())), preferred_element_type=F32)


def _rstd(v):
    return lax.rsqrt(jnp.mean(v * v, axis=-1, keepdims=True) + RMS_EPS)


def _rms_bwd(v, g, dy):
    r = _rstd(v)
    vh = v * r
    t = dy * g
    dv = r * (t - vh * jnp.mean(t * vh, axis=-1, keepdims=True))
    return dv, jnp.sum(dy * vh, axis=0, keepdims=True)


def _params(n_grid):
    return pltpu.CompilerParams(dimension_semantics=("arbitrary",) * n_grid, vmem_limit_bytes=VMEM_LIMIT)


def _row(i):
    return (i, 0)


def _fixed(*_):
    return (0, 0)


VMEM_WHOLE = pl.BlockSpec(memory_space=pltpu.VMEM)
SMEM_WHOLE = pl.BlockSpec(memory_space=pltpu.SMEM)
ANY = pl.BlockSpec(memory_space=pl.ANY)


AUG = 128
BIAS_LANE = HEAD_DIM
ONE_LANE = HEAD_DIM + 3
SPARE_LANE = HEADS


def _attn_layout_constants():
    import numpy as np
    place = np.zeros((D_ATTN, HEADS * AUG), np.float32)
    for r in range(D_ATTN):
        place[r, (r // HEAD_DIM) * AUG + r % HEAD_DIM] = 1.0
    bias_k = np.zeros((3, LANES, HEADS * AUG), np.float32)
    bias_q = np.zeros((3, LANES, HEADS * AUG), np.float32)
    for h in range(HEADS):
        for part in range(3):
            bias_k[part, h, h * AUG + BIAS_LANE + part] = -1.0
            bias_q[part, h, h * AUG + ONE_LANE + part] = 1.0
            bias_k[0, SPARE_LANE, h * AUG + ONE_LANE + part] = 1.0
            bias_q[0, SPARE_LANE, h * AUG + BIAS_LANE + part] = 1.0
    as_bf = lambda a: jnp.asarray(a, BF16)
    return dict(place=as_bf(place), place_t=as_bf(place.T), bias_k=as_bf(bias_k),
                bias_q_t=as_bf(bias_q.transpose(0, 2, 1)))


def _pre_attn_fwd(x, g1, wqkv, wf, wu, bpad, wpool, lay):
    s, d = x.shape
    nt = s // TS
    sub = TS // TQ

    def body(x_ref, g_ref, wqkv_ref, wf_ref, wu_ref, b_ref, wp_ref, place_ref, place_t_ref, bk_ref, bqt_ref,
             hn_ref, q_ref, ka_ref, v_ref, qat_ref, vt_ref, kt_ref, fl_ref, y_ref, mp_ref, ubuf, ccar, cbuf):
        i = pl.program_id(0)

        @pl.when(i == 0)
        def _():
            ubuf[0:HALO, :] = jnp.zeros((HALO, D_POOL), F32)
            ccar[...] = jnp.zeros_like(ccar)

        xv = x_ref[...]
        hn = (xv * _rstd(xv) * g_ref[...]).astype(BF16)
        hn_ref[...] = hn
        zq = _nt(hn, wqkv_ref[...])
        qb = (zq[:, 0:D_ATTN] * 0.125).astype(BF16)
        kb = zq[:, D_ATTN:2 * D_ATTN].astype(BF16)
        vb = zq[:, 2 * D_ATTN:3 * D_ATTN].astype(BF16)
        q_ref[...] = qb
        v_ref[...] = vb

        fl = _nt(hn, wf_ref[...]) + b_ref[...]
        fl_ref[...] = fl
        logf = jax.nn.log_sigmoid(fl)
        rr = lax.broadcasted_iota(jnp.int32, (TS, TS), 0)
        cc = lax.broadcasted_iota(jnp.int32, (TS, TS), 1)
        tril = (cc <= rr).astype(F32)
        c = jnp.dot(tril, logf, precision=HIGHEST, preferred_element_type=F32) + ccar[...]
        cbuf[...] = c
        ccar[...] = cbuf[TS - 1:TS, :]
        hi = c.astype(BF16)
        rest = c - hi.astype(F32)
        mid = rest.astype(BF16)
        lo = (rest - mid.astype(F32)).astype(BF16)
        lane = lax.broadcasted_iota(jnp.int32, (TS, LANES), 1)
        parts = (jnp.where(lane == SPARE_LANE, 1.0, hi).astype(BF16), mid, lo)
        ka = _nn(kb, place_ref[...])
        qat = _nt(place_t_ref[...], qb)
        for part in range(3):
            ka = ka + _nn(parts[part], bk_ref[part])
            qat = qat + _nt(bqt_ref[part], parts[part])
        ka_ref[...] = ka.astype(BF16)
        qat = qat.astype(BF16)
        vt = vb.T
        kt = kb.T
        for a in range(sub):
            qat_ref[a] = qat[:, a * TQ:(a + 1) * TQ]
            vt_ref[a] = vt[:, a * TQ:(a + 1) * TQ]
            kt_ref[a] = kt[:, a * TQ:(a + 1) * TQ]

        u = _nt(hn, wu_ref[...])
        ubuf[HALO:HALO + TS, :] = u
        t = i * TS + lax.broadcasted_iota(jnp.int32, (TS, 1), 0)
        for g, w in enumerate(POOL_WINDOWS):
            cols = slice(g * POOL_CH, (g + 1) * POOL_CH)
            sm = ubuf[:, cols]
            step = 1
            while step < w:
                sm = sm + pltpu.roll(sm, step, 0)
                step *= 2
            cnt = jnp.minimum(t + 1, w).astype(F32)
            yg = (sm[HALO:, :] / cnt - u[:, cols]).astype(BF16)
            y_ref[:, cols] = yg
            mp_ref[:, cols] = _nn(yg, wp_ref[g])
        ubuf[0:HALO, :] = u[TS - HALO:, :]

    nq = s // TQ
    aug = HEADS * AUG
    outs = (
        jax.ShapeDtypeStruct((s, d), BF16), jax.ShapeDtypeStruct((s, D_ATTN), BF16),
        jax.ShapeDtypeStruct((s, aug), BF16), jax.ShapeDtypeStruct((s, D_ATTN), BF16),
        jax.ShapeDtypeStruct((nq, aug, TQ), BF16), jax.ShapeDtypeStruct((nq, D_ATTN, TQ), BF16),
        jax.ShapeDtypeStruct((nq, D_ATTN, TQ), BF16),
        jax.ShapeDtypeStruct((s, LANES), F32),
        jax.ShapeDtypeStruct((s, D_POOL), BF16), jax.ShapeDtypeStruct((s, D_POOL), F32),
    )
    fixed3 = lambda i: (0, 0, 0)
    tiles3 = lambda rows: pl.BlockSpec((sub, rows, TQ), lambda i: (i, 0, 0))
    return pl.pallas_call(
        body, grid=(nt,), out_shape=outs, name="pre_attn_fwd",
        in_specs=[pl.BlockSpec((TS, d), _row), pl.BlockSpec((1, d), _fixed),
                  pl.BlockSpec(wqkv.shape, _fixed), pl.BlockSpec(wf.shape, _fixed), pl.BlockSpec(wu.shape, _fixed),
                  pl.BlockSpec((1, LANES), _fixed), pl.BlockSpec(wpool.shape, fixed3),
                  pl.BlockSpec(lay["place"].shape, _fixed), pl.BlockSpec(lay["place_t"].shape, _fixed),
                  pl.BlockSpec(lay["bias_k"].shape, fixed3), pl.BlockSpec(lay["bias_q_t"].shape, fixed3)],
        out_specs=(pl.BlockSpec((TS, d), _row), pl.BlockSpec((TS, D_ATTN), _row),
                   pl.BlockSpec((TS, aug), _row), pl.BlockSpec((TS, D_ATTN), _row),
                   tiles3(aug), tiles3(D_ATTN), tiles3(D_ATTN),
                   pl.BlockSpec((TS, LANES), _row),
                   pl.BlockSpec((TS, D_POOL), _row), pl.BlockSpec((TS, D_POOL), _row)),
        scratch_shapes=[pltpu.VMEM((TS + HALO, D_POOL), F32), pltpu.VMEM((1, LANES), F32), pltpu.VMEM((TS, LANES), F32)],
        compiler_params=_params(1),
    )(x, g1, wqkv, wf, wu, bpad, wpool, lay["place"], lay["place_t"], lay["bias_k"], lay["bias_q_t"])


def _causal_in_tile():
    krow = lax.broadcasted_iota(jnp.int32, (TQ, TQ), 0)
    qcol = lax.broadcasted_iota(jnp.int32, (TQ, TQ), 1)
    return krow <= qcol


def _attn_fwd(ka, qat3, vt3, own_block):
    s = ka.shape[0]
    nq = s // TQ
    pass_on_step = max(nq - 2, 0)

    def body(qa_ref, ka_ref, vt_ref, own_ref, a_ref, lset_ref, all_ref, acc, st_scr, pt_scr,
             stage, send_sems, recv_sems, local_sem):
        i = pl.program_id(0)

        @pl.when(i == 0)
        def _():
            _gather_start(own_ref, all_ref, stage, send_sems, recv_sems, local_sem)

        @pl.when(i == pass_on_step)
        def _():
            _gather_pass_on(all_ref, send_sems, recv_sems)

        acc[...] = jnp.zeros_like(acc)

        def tile(j, stats, masked):
            tile_max = []
            for h in range(HEADS):
                aug = slice(h * AUG, (h + 1) * AUG)
                st = _nn(ka_ref[pl.ds(j * TQ, TQ), aug], qa_ref[0, aug, :])
                if masked:
                    st = jnp.where(_causal_in_tile(), st, NEG)
                st_scr[h] = st
                tile_max.append(jnp.max(st, axis=0, keepdims=True))
            new, scale = [], []
            for h in range(HEADS):
                m_old, l_old = stats[h]
                m_new = jnp.maximum(m_old, tile_max[h])
                al = jnp.exp(m_old - m_new)
                pt = jnp.exp(st_scr[h] - m_new)
                pt_scr[h] = pt.astype(BF16)
                new.append((m_new, al * l_old + jnp.sum(pt, axis=0, keepdims=True)))
                scale.append(al)
            for h in range(HEADS):
                rows = slice(h * HEAD_DIM, (h + 1) * HEAD_DIM)
                acc[rows, :] = scale[h] * acc[rows, :] + _nn(vt_ref[j, rows, :], pt_scr[h])
            return tuple(new)

        init = tuple((jnp.full((1, TQ), NEG, F32), jnp.zeros((1, TQ), F32)) for _ in range(HEADS))
        stats = lax.fori_loop(0, i, functools.partial(tile, masked=False), init)
        stats = tile(i, stats, True)
        for h in range(HEADS):
            rows = slice(h * HEAD_DIM, (h + 1) * HEAD_DIM)
            acc[rows, :] = acc[rows, :] / stats[h][1]
            lset_ref[0, h:h + 1, :] = stats[h][0] + jnp.log(stats[h][1])
        a_ref[...] = acc[...].T

        @pl.when(i == nq - 1)
        def _():
            _gather_finish(own_ref, all_ref, send_sems, recv_sems)

    r, cdim = own_block.shape
    return pl.pallas_call(
        body, grid=(nq,), name="attn_fwd",
        out_shape=(jax.ShapeDtypeStruct((s, D_ATTN), F32), jax.ShapeDtypeStruct((nq, HEADS, TQ), F32),
                   jax.ShapeDtypeStruct((N_DEV, r, cdim), own_block.dtype)),
        in_specs=[pl.BlockSpec((1, HEADS * AUG, TQ), lambda i: (i, 0, 0)), VMEM_WHOLE, VMEM_WHOLE, ANY],
        out_specs=(pl.BlockSpec((TQ, D_ATTN), _row), pl.BlockSpec((1, HEADS, TQ), lambda i: (i, 0, 0)), ANY),
        scratch_shapes=[pltpu.VMEM((D_ATTN, TQ), F32), pltpu.VMEM((HEADS, TQ, TQ), F32), pltpu.VMEM((HEADS, TQ, TQ), BF16),
                        pltpu.VMEM((r, cdim), own_block.dtype),
                        pltpu.SemaphoreType.DMA((7,)), pltpu.SemaphoreType.DMA((7,)), pltpu.SemaphoreType.DMA],
        compiler_params=_params(1),
    )(qat3, ka, vt3, own_block)


def _post_attn_fwd(a, mpre, x, g_attn, g_pool, pscale, wout, g_post, g_ffn_pre):
    s, d = x.shape

    def body(a_ref, mp_ref, x_ref, ga_ref, gp_ref, ps_ref, wo_ref, gpost_ref, gpre_ref,
             mix_ref, o_ref, h1_ref, hn2_ref):
        av = a_ref[...]
        mix_ref[:, 0:D_ATTN] = (av * _rstd(av) * ga_ref[...]).astype(BF16)
        mv = mp_ref[...] * ps_ref[...]
        mix_ref[:, D_ATTN:] = (mv * _rstd(mv) * gp_ref[...]).astype(BF16)
        o = _nn(mix_ref[...], wo_ref[...])
        o_ref[...] = o
        h1 = x_ref[...] + o * _rstd(o) * gpost_ref[...]
        h1_ref[...] = h1
        hn2_ref[...] = (h1 * _rstd(h1) * gpre_ref[...]).astype(BF16)

    vec = lambda n: pl.BlockSpec((1, n), _fixed)
    return pl.pallas_call(
        body, grid=(s // TS,), name="post_attn_fwd",
        out_shape=(jax.ShapeDtypeStruct((s, d), BF16), jax.ShapeDtypeStruct((s, d), F32),
                   jax.ShapeDtypeStruct((s, d), F32), jax.ShapeDtypeStruct((s, d), BF16)),
        in_specs=[pl.BlockSpec((TS, D_ATTN), _row), pl.BlockSpec((TS, D_POOL), _row), pl.BlockSpec((TS, d), _row),
                  vec(D_ATTN), vec(D_POOL), vec(D_POOL), pl.BlockSpec(wout.shape, _fixed), vec(d), vec(d)],
        out_specs=(pl.BlockSpec((TS, d), _row),) * 4,
        compiler_params=_params(1),
    )(a, mpre, x, g_attn, g_pool, pscale, wout, g_post, g_ffn_pre)


def _ffn_fwd(hn2, wg, wu, wd, h1, g_post):
    s, d = h1.shape
    nc = D_FF // TN_FF
    ts = min(TS_FF, s)

    def body(hn_ref, wg_ref, wu_ref, wd_ref, h1_ref, g_ref, gate_ref, up_ref, act_ref, ff_ref, h2_ref, acc):
        j = pl.program_id(1)

        @pl.when(j == 0)
        def _():
            acc[...] = jnp.zeros_like(acc)

        for r in range(2):
            rows = slice(r * (ts // 2), (r + 1) * (ts // 2))
            hn = hn_ref[rows, :]
            gt = _nt(hn, wg_ref[...])
            up = _nt(hn, wu_ref[...])
            gate_ref[rows, :] = gt.astype(BF16)
            up_ref[rows, :] = up.astype(BF16)
            act_ref[rows, :] = (gt * jax.nn.sigmoid(gt) * up).astype(BF16)
            acc[rows, :] += _nn(act_ref[rows, :], wd_ref[...])

        @pl.when(j == nc - 1)
        def _():
            ff = acc[...]
            ff_ref[...] = ff
            h2_ref[...] = h1_ref[...] + ff * _rstd(ff) * g_ref[...]

    rowblk = pl.BlockSpec((ts, d), lambda i, j: (i, 0))
    wblk = pl.BlockSpec((TN_FF, d), lambda i, j: (j, 0))
    chunk = pl.BlockSpec((ts, TN_FF), lambda i, j: (i, j))
    return pl.pallas_call(
        body, grid=(s // ts, nc), name="ffn_fwd",
        out_shape=(jax.ShapeDtypeStruct((s, D_FF), BF16),) * 3 + (jax.ShapeDtypeStruct((s, d), F32),) * 2,
        in_specs=[rowblk, wblk, wblk, wblk, rowblk, pl.BlockSpec((1, d), lambda i, j: (0, 0))],
        out_specs=(chunk, chunk, chunk, rowblk, rowblk),
        scratch_shapes=[pltpu.VMEM((ts, d), F32)],
        compiler_params=_params(2),
    )(hn2, wg, wu, wd, h1, g_post)


def _tail_fwd_bwd(h2, p, tgt, ff, wple, wpg, g_ple, g_ffn_post):
    s, d = h2.shape

    def body(h2_ref, p_ref, t_ref, ff_ref, wple_ref, wpg_ref, gple_ref, gfp_ref,
             dh2_ref, dff_ref, dgl_ref, dpp_ref, h2b_ref, pb_ref, loss_ref, dgple_ref, dgfp_ref):
        i = pl.program_id(0)

        @pl.when(i == 0)
        def _():
            loss_ref[...] = jnp.zeros_like(loss_ref)
            dgple_ref[...] = jnp.zeros_like(dgple_ref)
            dgfp_ref[...] = jnp.zeros_like(dgfp_ref)

        h2 = h2_ref[...]
        h2b = h2.astype(BF16)
        h2b_ref[...] = h2b
        pb = p_ref[...].astype(BF16)
        pb_ref[...] = pb
        pp = _nt(pb, wple_ref[...])
        gple = gple_ref[...]
        e = pp * _rstd(pp) * gple
        sg = jax.nn.sigmoid(_nn(h2b, wpg_ref[...]))
        diff = h2 + sg * e - t_ref[...]
        sq = jnp.sum(jnp.sum(diff * diff, axis=1, keepdims=True), axis=0, keepdims=True)
        loss_ref[...] += jnp.broadcast_to(sq * (0.5 / d), loss_ref.shape)
        dh3 = diff * (1.0 / d)
        dgl = (dh3 * e * sg * (1.0 - sg)).astype(BF16)
        dgl_ref[...] = dgl
        dh2 = dh3 + _nt(dgl, wpg_ref[...])
        dh2_ref[...] = dh2
        dpp, dg = _rms_bwd(pp, gple, dh3 * sg)
        dpp_ref[...] = dpp.astype(BF16)
        dgple_ref[...] += dg
        dff, dg = _rms_bwd(ff_ref[...], gfp_ref[...], dh2)
        dff_ref[...] = dff.astype(BF16)
        dgfp_ref[...] += dg

    rowblk = pl.BlockSpec((TS, d), _row)
    vec = pl.BlockSpec((1, d), _fixed)
    return pl.pallas_call(
        body, grid=(s // TS,), name="tail_fwd_bwd",
        out_shape=(jax.ShapeDtypeStruct((s, d), F32), jax.ShapeDtypeStruct((s, d), BF16),
                   jax.ShapeDtypeStruct((s, d), BF16), jax.ShapeDtypeStruct((s, d), BF16),
                   jax.ShapeDtypeStruct((s, d), BF16), jax.ShapeDtypeStruct((s, D_PLE), BF16),
                   jax.ShapeDtypeStruct((8, LANES), F32), jax.ShapeDtypeStruct((1, d), F32),
                   jax.ShapeDtypeStruct((1, d), F32)),
        in_specs=[rowblk, pl.BlockSpec((TS, D_PLE), _row), rowblk, rowblk,
                  pl.BlockSpec(wple.shape, _fixed), pl.BlockSpec(wpg.shape, _fixed), vec, vec],
        out_specs=(rowblk, rowblk, rowblk, rowblk, rowblk, pl.BlockSpec((TS, D_PLE), _row),
                   pl.BlockSpec((8, LANES), _fixed), vec, vec),
        compiler_params=_params(1),
    )(h2, p, tgt, ff, wple, wpg, g_ple, g_ffn_post)


def _ffn_bwd(dff, gate, up, wd, wg, wu, h1, dh2, g_pre):
    s, d = h1.shape
    nc = D_FF // TN_FF
    ts = min(TS_FF, s)

    def body(dff_ref, gate_ref, up_ref, wd_ref, wg_ref, wu_ref, h1_ref, dh2_ref, g_ref,
             dgate_ref, dup_ref, dh1_ref, dg_ref, acc):
        i = pl.program_id(0)
        j = pl.program_id(1)

        @pl.when((i == 0) & (j == 0))
        def _():
            dg_ref[...] = jnp.zeros_like(dg_ref)

        @pl.when(j == 0)
        def _():
            acc[...] = jnp.zeros_like(acc)

        for r in range(2):
            rows = slice(r * (ts // 2), (r + 1) * (ts // 2))
            dact = _nt(dff_ref[rows, :], wd_ref[...])
            gt = gate_ref[rows, :].astype(F32)
            sg = jax.nn.sigmoid(gt)
            dup_ref[rows, :] = (dact * gt * sg).astype(BF16)
            dgate_ref[rows, :] = (dact * up_ref[rows, :].astype(F32) * (sg * (1.0 + gt * (1.0 - sg)))).astype(BF16)
            acc[rows, :] += _nn(dgate_ref[rows, :], wg_ref[...]) + _nn(dup_ref[rows, :], wu_ref[...])

        @pl.when(j == nc - 1)
        def _():
            dv, dg = _rms_bwd(h1_ref[...], g_ref[...], acc[...])
            dh1_ref[...] = dh2_ref[...] + dv
            dg_ref[...] += dg

    rowblk = pl.BlockSpec((ts, d), lambda i, j: (i, 0))
    wblk = pl.BlockSpec((TN_FF, d), lambda i, j: (j, 0))
    chunk = pl.BlockSpec((ts, TN_FF), lambda i, j: (i, j))
    vec = pl.BlockSpec((1, d), lambda i, j: (0, 0))
    return pl.pallas_call(
        body, grid=(s // ts, nc), name="ffn_bwd",
        out_shape=(jax.ShapeDtypeStruct((s, D_FF), BF16), jax.ShapeDtypeStruct((s, D_FF), BF16),
                   jax.ShapeDtypeStruct((s, d), F32), jax.ShapeDtypeStruct((1, d), F32)),
        in_specs=[rowblk, chunk, chunk, wblk, wblk, wblk, rowblk, rowblk, vec],
        out_specs=(chunk, chunk, rowblk, vec),
        scratch_shapes=[pltpu.VMEM((ts, d), F32)],
        compiler_params=_params(2),
    )(dff, gate, up, wd, wg, wu, h1, dh2, g_pre)


def _post_attn_bwd(dh1, o, a, mpre, wout, wpool, g_post, g_attn, g_pool, pscale):
    s, d = dh1.shape
    sub = TS // TQ

    def body(dh1_ref, o_ref, a_ref, mp_ref, wo_ref, wp_ref, gpost_ref, ga_ref, gp_ref, ps_ref,
             dob_ref, dab_ref, dat_ref, dlt_ref, dmpb_ref, dy_ref, dgpost_ref, dga_ref, dgp_ref, dps_ref):
        i = pl.program_id(0)

        @pl.when(i == 0)
        def _():
            dgpost_ref[...] = jnp.zeros_like(dgpost_ref)
            dga_ref[...] = jnp.zeros_like(dga_ref)
            dgp_ref[...] = jnp.zeros_like(dgp_ref)
            dps_ref[...] = jnp.zeros_like(dps_ref)

        do, dg = _rms_bwd(o_ref[...], gpost_ref[...], dh1_ref[...])
        dgpost_ref[...] += dg
        dob = do.astype(BF16)
        dob_ref[...] = dob
        dmix = _nt(dob, wo_ref[...])

        av = a_ref[...]
        da, dg = _rms_bwd(av, ga_ref[...], dmix[:, 0:D_ATTN])
        dga_ref[...] += dg
        dab = da.astype(BF16)
        dab_ref[...] = dab
        dat = dab.T
        hsel = (lax.shift_right_logical(lax.broadcasted_iota(jnp.int32, (HEADS, D_ATTN), 1), 6)
                == lax.broadcasted_iota(jnp.int32, (HEADS, D_ATTN), 0)).astype(F32)
        dlt = lax.dot_general(hsel, da * av, (((1,), (1,)), ((), ())), precision=HIGHEST, preferred_element_type=F32)
        for q in range(sub):
            dlt_ref[q] = dlt[:, q * TQ:(q + 1) * TQ]
            dat_ref[q] = dat[:, q * TQ:(q + 1) * TQ]

        ps = ps_ref[...]
        mp = mp_ref[...]
        dm, dg = _rms_bwd(mp * ps, gp_ref[...], dmix[:, D_ATTN:])
        dgp_ref[...] += dg
        dps_ref[...] += jnp.sum(dm * mp, axis=0, keepdims=True)
        dmpb = (dm * ps).astype(BF16)
        dmpb_ref[...] = dmpb
        for g in range(len(POOL_WINDOWS)):
            cols = slice(g * POOL_CH, (g + 1) * POOL_CH)
            dy_ref[:, cols] = _nt(dmpb[:, cols], wp_ref[g])

    rowblk = pl.BlockSpec((TS, d), _row)
    half = pl.BlockSpec((TS, D_ATTN), _row)
    vec = lambda n: pl.BlockSpec((1, n), _fixed)
    return pl.pallas_call(
        body, grid=(s // TS,), name="post_attn_bwd",
        out_shape=(jax.ShapeDtypeStruct((s, d), BF16), jax.ShapeDtypeStruct((s, D_ATTN), BF16),
                   jax.ShapeDtypeStruct((s // TQ, D_ATTN, TQ), BF16),
                   jax.ShapeDtypeStruct((s // TQ, HEADS, TQ), F32), jax.ShapeDtypeStruct((s, D_POOL), BF16),
                   jax.ShapeDtypeStruct((s, D_POOL), F32), jax.ShapeDtypeStruct((1, d), F32),
                   jax.ShapeDtypeStruct((1, D_ATTN), F32), jax.ShapeDtypeStruct((1, D_POOL), F32),
                   jax.ShapeDtypeStruct((1, D_POOL), F32)),
        in_specs=[rowblk, rowblk, half, half, pl.BlockSpec(wout.shape, _fixed),
                  pl.BlockSpec(wpool.shape, lambda i: (0, 0, 0)), vec(d), vec(D_ATTN), vec(D_POOL), vec(D_POOL)],
        out_specs=(rowblk, half, pl.BlockSpec((sub, D_ATTN, TQ), lambda i: (i, 0, 0)),
                   pl.BlockSpec((sub, HEADS, TQ), lambda i: (i, 0, 0)), half, half,
                   vec(d), vec(D_ATTN), vec(D_POOL), vec(D_POOL)),
        compiler_params=_params(1),
    )(dh1, o, a, mpre, wout, wpool, g_post, g_attn, g_pool, pscale)


def _attn_bwd(ka, v, kt3, qat3, q, do, dot3, lset3, dlt3, chip_blocks):
    s = q.shape[0]
    nq = s // TQ
    wide = HEADS * LANES

    def body(ka_ref, v_ref, kt_ref, qat_ref, q_ref, do_ref, dot_ref, lset_ref, dlt_ref, b_ref,
             dqt_ref, dk_ref, dv_ref, dcs_ref, drs_ref, got_ref, dca, dkw, dvw, pt_scr, ptb_scr, dsb_scr,
             stage, send_sems, recv_sems, local_sem):
        j = pl.program_id(0)

        @pl.when(j == 0)
        def _():
            _chips_start(b_ref, got_ref, stage, send_sems, recv_sems, local_sem)
            dqt_ref[...] = jnp.zeros_like(dqt_ref)
            drs_ref[...] = jnp.zeros_like(drs_ref)

        dkw[...] = jnp.zeros_like(dkw)
        dvw[...] = jnp.zeros_like(dvw)
        dca[...] = jnp.zeros_like(dca)

        def tile(i, masked):
            rows = pl.ds(i * TQ, TQ)
            for h in range(HEADS):
                aug = slice(h * AUG, (h + 1) * AUG)
                st = _nn(ka_ref[:, aug], qat_ref[i, aug, :]) - lset_ref[i, h:h + 1, :]
                if masked:
                    st = jnp.where(_causal_in_tile(), st, NEG)
                pt = jnp.exp(st)
                pt_scr[h] = pt
                ptb_scr[h] = pt.astype(BF16)
            for h in range(HEADS):
                hs = slice(h * HEAD_DIM, (h + 1) * HEAD_DIM)
                half = slice(h * LANES, h * LANES + HEAD_DIM)
                dvw[:, half] += _nn(ptb_scr[h], do_ref[rows, hs])
                dst = pt_scr[h] * (_nn(v_ref[:, hs], dot_ref[i, hs, :]) - dlt_ref[i, h:h + 1, :])
                dsb_scr[h] = dst.astype(BF16)
                drs_ref[i, h, 0:1, :] += jnp.sum(dst, axis=0, keepdims=True)
                dca[:, h * LANES:(h + 1) * LANES] += dst[:, 0:LANES] + dst[:, LANES:2 * LANES]
            for h in range(HEADS):
                hs = slice(h * HEAD_DIM, (h + 1) * HEAD_DIM)
                half = slice(h * LANES, h * LANES + HEAD_DIM)
                dkw[:, half] += _nn(dsb_scr[h], q_ref[rows, hs])
                dqt_ref[i, hs, :] += _nn(kt_ref[0, hs, :], dsb_scr[h])

        def step(i, carry):
            tile(i, False)
            return carry

        tile(j, True)
        lax.fori_loop(j + 1, nq, step, 0)
        lane = lax.broadcasted_iota(jnp.int32, (TQ, LANES), 1)
        dcs_all = jnp.zeros((TQ, LANES), F32)
        for h in range(HEADS):
            hs = slice(h * HEAD_DIM, (h + 1) * HEAD_DIM)
            half = slice(h * LANES, h * LANES + HEAD_DIM)
            dk_ref[:, hs] = dkw[:, half]
            dv_ref[:, hs] = dvw[:, half]
            colsum = jnp.sum(dca[:, h * LANES:(h + 1) * LANES], axis=1, keepdims=True)
            dcs_all = jnp.where(lane == h, colsum, dcs_all)
        dcs_ref[...] = dcs_all

        @pl.when(j == nq - 1)
        def _():
            _chips_finish(b_ref, got_ref, send_sems, recv_sems)

    blk = pl.BlockSpec((TQ, D_ATTN), _row)
    _, r, cdim = chip_blocks.shape
    return pl.pallas_call(
        body, grid=(nq,), name="attn_bwd",
        out_shape=(jax.ShapeDtypeStruct((nq, D_ATTN, TQ), F32), jax.ShapeDtypeStruct((s, D_ATTN), F32),
                   jax.ShapeDtypeStruct((s, D_ATTN), F32), jax.ShapeDtypeStruct((s, LANES), F32),
                   jax.ShapeDtypeStruct((nq, HEADS, 8, TQ), F32),
                   jax.ShapeDtypeStruct(chip_blocks.shape, chip_blocks.dtype)),
        in_specs=[pl.BlockSpec((TQ, HEADS * AUG), _row), blk, pl.BlockSpec((1, D_ATTN, TQ), lambda j: (j, 0, 0)),
                  VMEM_WHOLE, VMEM_WHOLE, VMEM_WHOLE, VMEM_WHOLE, VMEM_WHOLE, VMEM_WHOLE, ANY],
        out_specs=(pl.BlockSpec((nq, D_ATTN, TQ), lambda j: (0, 0, 0)), blk, blk, pl.BlockSpec((TQ, LANES), _row),
                   pl.BlockSpec((nq, HEADS, 8, TQ), lambda j: (0, 0, 0, 0)), ANY),
        scratch_shapes=[pltpu.VMEM((TQ, wide), F32), pltpu.VMEM((TQ, wide), F32), pltpu.VMEM((TQ, wide), F32),
                        pltpu.VMEM((HEADS, TQ, TQ), F32), pltpu.VMEM((HEADS, TQ, TQ), BF16),
                        pltpu.VMEM((HEADS, TQ, TQ), BF16), pltpu.VMEM((r, cdim), chip_blocks.dtype),
                        pltpu.SemaphoreType.DMA((3,)), pltpu.SemaphoreType.DMA((3,)), pltpu.SemaphoreType.DMA],
        compiler_params=_params(1),
    )(ka, v, kt3, qat3, q, do, dot3, lset3, dlt3, chip_blocks)


def _pre_attn_bwd(dqt3, dk, dv, dcs, drs, fl, dy, x, dh1, g1, wqkv, wf, wu):
    s, d = x.shape
    nt = s // TS
    n = TS + HALO
    sub = TS // TQ
    qkv, fcols = 3 * D_ATTN, 3 * D_ATTN + LANES

    def body(dqt_ref, dk_ref, dv_ref, dcs_ref, drs_ref, fl_ref, dy_ref, x_ref, dh1_ref, g_ref, wqkv_ref, wf_ref, wu_ref,
             gx_ref, dz_ref, dg_ref, db_ref, ybuf, ccar, dlog):
        dqkv_ref = dz_ref.at[:, 0:qkv]
        dfb_ref = dz_ref.at[:, qkv:fcols]
        dub_ref = dz_ref.at[:, fcols:]
        i = pl.program_id(0)
        ti = nt - 1 - i

        @pl.when(i == 0)
        def _():
            ybuf[TS:n, :] = jnp.zeros((HALO, D_POOL), F32)
            ccar[...] = jnp.zeros_like(ccar)
            dg_ref[...] = jnp.zeros_like(dg_ref)
            db_ref[...] = jnp.zeros_like(db_ref)

        rr = lax.broadcasted_iota(jnp.int32, (TS, TS), 0)
        cc = lax.broadcasted_iota(jnp.int32, (TS, TS), 1)
        triu = (cc >= rr).astype(F32)
        dlog[...] = ccar[...] + jnp.dot(triu, drs_ref[...] - dcs_ref[...], precision=HIGHEST, preferred_element_type=F32)
        ccar[...] = dlog[0:1, :]
        df = dlog[...] * jax.nn.sigmoid(-fl_ref[...])
        db_ref[...] += jnp.sum(df, axis=0, keepdims=True)
        dfb = df.astype(BF16)
        dfb_ref[...] = dfb

        t = ti * TS + lax.broadcasted_iota(jnp.int32, (TS, 1), 0)
        dy = dy_ref[...]
        for g, w in enumerate(POOL_WINDOWS):
            cols = slice(g * POOL_CH, (g + 1) * POOL_CH)
            ybuf[0:TS, cols] = dy[:, cols] / jnp.minimum(t + 1, w).astype(F32)
        for g, w in enumerate(POOL_WINDOWS):
            cols = slice(g * POOL_CH, (g + 1) * POOL_CH)
            sm = ybuf[:, cols]
            step = 1
            while step < w:
                sm = sm + pltpu.roll(sm, n - step, 0)
                step *= 2
            dub_ref[:, cols] = (sm[0:TS, :] - dy[:, cols]).astype(BF16)
        ybuf[TS:n, :] = ybuf[0:HALO, :]

        for a in range(sub):
            dqkv_ref[a * TQ:(a + 1) * TQ, 0:D_ATTN] = (dqt_ref[a].T * 0.125).astype(BF16)
        dqkv_ref[:, D_ATTN:2 * D_ATTN] = dk_ref[...].astype(BF16)
        dqkv_ref[:, 2 * D_ATTN:] = dv_ref[...].astype(BF16)
        dhn = _nn(dqkv_ref[...], wqkv_ref[...]) + _nn(dfb, wf_ref[...]) + _nn(dub_ref[...], wu_ref[...])
        dx, dg = _rms_bwd(x_ref[...], g_ref[...], dhn)
        gx_ref[...] = dh1_ref[...] + dx
        dg_ref[...] += dg

    rev = lambda i: (nt - 1 - i, 0)
    blk = lambda w: pl.BlockSpec((TS, w), rev)
    return pl.pallas_call(
        body, grid=(nt,), name="pre_attn_bwd",
        out_shape=(jax.ShapeDtypeStruct((s, d), F32), jax.ShapeDtypeStruct((s, fcols + D_POOL), BF16),
                   jax.ShapeDtypeStruct((1, d), F32), jax.ShapeDtypeStruct((1, LANES), F32)),
        in_specs=[pl.BlockSpec((sub, D_ATTN, TQ), lambda i: (nt - 1 - i, 0, 0)),
                  blk(D_ATTN), blk(D_ATTN), blk(LANES), blk(LANES), blk(LANES), blk(D_POOL), blk(d), blk(d),
                  pl.BlockSpec((1, d), _fixed), pl.BlockSpec(wqkv.shape, _fixed), pl.BlockSpec(wf.shape, _fixed),
                  pl.BlockSpec(wu.shape, _fixed)],
        out_specs=(blk(d), blk(fcols + D_POOL), pl.BlockSpec((1, d), _fixed), pl.BlockSpec((1, LANES), _fixed)),
        scratch_shapes=[pltpu.VMEM((n, D_POOL), F32), pltpu.VMEM((1, LANES), F32), pltpu.VMEM((TS, LANES), F32)],
        compiler_params=_params(1),
    )(dqt3, dk, dv, dcs, drs, fl, dy, x, dh1, g1, wqkv, wf, wu)


def _wgrad(a, b, out_dtype, name):
    s, m = a.shape
    n = b.shape[1]
    tm = max(t for t in range(LANES, min(m, TM_WGRAD) + 1, LANES) if m % t == 0)
    ns = s // TS

    def body(a_ref, b_ref, o_ref, acc):
        i = pl.program_id(1)

        @pl.when(i == 0)
        def _():
            acc[...] = jnp.zeros_like(acc)

        acc[...] += _tn(a_ref[...], b_ref[pl.ds(i * TS, TS), :])

        @pl.when(i == ns - 1)
        def _():
            o_ref[...] = acc[...].astype(out_dtype)

    return pl.pallas_call(
        body, grid=(m // tm, ns), name=name, out_shape=jax.ShapeDtypeStruct((m, n), out_dtype),
        in_specs=[pl.BlockSpec((TS, tm), lambda j, i: (i, j)), VMEM_WHOLE],
        out_specs=pl.BlockSpec((tm, n), lambda j, i: (j, 0)),
        scratch_shapes=[pltpu.VMEM((tm, n), F32)],
        compiler_params=_params(2),
    )(a, b)


def _adamw(w, g, m, v):
    m = ADAM_B1 * m + (1.0 - ADAM_B1) * g
    v = ADAM_B2 * v + (1.0 - ADAM_B2) * (g * g)
    m_hat = m / (1.0 - ADAM_B1 ** ADAM_STEP)
    v_hat = v / (1.0 - ADAM_B2 ** ADAM_STEP)
    delta = -ADAM_LR * (m_hat / (jnp.sqrt(v_hat) + ADAM_EPS) + ADAM_WD * w)
    return delta, m, v


def _pair_sum(core, t, theirs, tr, name):
    nk, r, c = theirs.shape

    def body(core_ref, a_ref, b_ref, o_ref):
        o_ref[...] = (a_ref[...].astype(F32) + b_ref[...].astype(F32)).astype(BF16)

    blk = pl.BlockSpec((1, tr, c), lambda k, i, core_ref: (k, i, 0))
    return pl.pallas_call(
        body, name=name, out_shape=jax.ShapeDtypeStruct(theirs.shape, BF16),
        grid_spec=pltpu.PrefetchScalarGridSpec(
            num_scalar_prefetch=1, grid=(nk, r // tr),
            in_specs=[pl.BlockSpec((1, tr, c), lambda k, i, core_ref: (2 * k + core_ref[0], i, 0)), blk],
            out_specs=blk),
        compiler_params=_params(2),
    )(core, t, theirs)


def _sum_update(p_ref, w_ref, m_ref, v_ref, g_ref, d_ref, nm_ref, nv_ref):
    g = p_ref[0].astype(F32)
    for k in range(1, p_ref.shape[0]):
        g = g + p_ref[k].astype(F32)
    g_ref[...] = g
    d_ref[...], nm_ref[...], nv_ref[...] = _adamw(w_ref[...], g, m_ref[...], v_ref[...])


def _reduce_update_rest(parts, w, m, v, chip_blocks, small_block):
    nk, r, c = parts.shape
    ns = r // TR_REST

    def body(p_ref, w_ref, m_ref, v_ref, b_ref, sm_ref, g_ref, d_ref, nm_ref, nv_ref, got_ref, all_ref,
             stage_b, stage_s, send_b, recv_b, local_b, send_s, recv_s, local_s):
        i = pl.program_id(0)

        @pl.when(i == 0)
        def _():
            _chips_start(b_ref, got_ref, stage_b, send_b, recv_b, local_b)
            _gather_start(sm_ref, all_ref, stage_s, send_s, recv_s, local_s)

        _sum_update(p_ref, w_ref, m_ref, v_ref, g_ref, d_ref, nm_ref, nv_ref)

        @pl.when(i == ns - 1)
        def _():
            _gather_pass_on(all_ref, send_s, recv_s)
            _chips_finish(b_ref, got_ref, send_b, recv_b)
            _gather_finish(sm_ref, all_ref, send_s, recv_s)

    blk = pl.BlockSpec((TR_REST, c), _row)
    out = jax.ShapeDtypeStruct((r, c), F32)
    dma = pltpu.SemaphoreType.DMA
    return pl.pallas_call(
        body, grid=(ns,), name="reduce_update_rest",
        out_shape=(out,) * 4 + (jax.ShapeDtypeStruct(chip_blocks.shape, chip_blocks.dtype),
                                jax.ShapeDtypeStruct((N_DEV,) + small_block.shape, small_block.dtype)),
        in_specs=[pl.BlockSpec((nk, TR_REST, c), lambda i: (0, i, 0)), blk, blk, blk, ANY, ANY],
        out_specs=(blk,) * 4 + (ANY, ANY),
        scratch_shapes=[pltpu.VMEM(chip_blocks.shape[1:], chip_blocks.dtype), pltpu.VMEM(small_block.shape, small_block.dtype),
                        dma((3,)), dma((3,)), dma, dma((7,)), dma((7,)), dma],
        compiler_params=_params(1),
    )(parts, w, m, v, chip_blocks, small_block)


def _reduce_update_big(parts, w, m, v, tr, name):
    nk, r, c = parts.shape

    def body(p_ref, w_ref, m_ref, v_ref, g_ref, d_ref, nm_ref, nv_ref):
        _sum_update(p_ref, w_ref, m_ref, v_ref, g_ref, d_ref, nm_ref, nv_ref)

    blk = pl.BlockSpec((tr, c), _row)
    out = jax.ShapeDtypeStruct((r, c), F32)
    return pl.pallas_call(
        body, grid=(r // tr,), name=name, out_shape=(out,) * 4,
        in_specs=[pl.BlockSpec((nk, tr, c), lambda i: (0, i, 0)), blk, blk, blk],
        out_specs=(blk,) * 4, compiler_params=_params(1),
    )(parts, w, m, v)


def _reduce_update_small(parts, w, m, v):
    nd = parts.shape[0]

    def body(p_ref, w_ref, m_ref, v_ref, g_ref, d_ref, nm_ref, nv_ref):
        g = p_ref[0]
        for k in range(1, nd):
            g = g + p_ref[k]
        g_ref[...] = g
        d_ref[...], nm_ref[...], nv_ref[...] = _adamw(w_ref[...], g, m_ref[...], v_ref[...])

    out = jax.ShapeDtypeStruct(w.shape, F32)
    return pl.pallas_call(body, name="reduce_update_small", out_shape=(out,) * 4,
                          compiler_params=pltpu.CompilerParams(vmem_limit_bytes=VMEM_LIMIT))(parts, w, m, v)


MESH = pl.DeviceIdType.MESH


def _copy_through_vmem(src_hbm, dst_hbm, stage, sem):
    load = pltpu.make_async_copy(src_hbm, stage, sem)
    load.start()
    load.wait()
    store = pltpu.make_async_copy(stage, dst_hbm, sem)
    store.start()
    store.wait()


class _GatherPlan:
    def __init__(self, x_ref, out_ref, send_sems, recv_sems):
        x, y, c = lax.axis_index("x"), lax.axis_index("y"), lax.axis_index("c")
        self.me, self.sibling, self.c = (x, y, c), (x, y, 1 - c), c
        self.chips = [(1 - x, y), (x, 1 - y), (1 - x, 1 - y)]
        self.x_ref, self.out_ref, self.send_sems, self.recv_sems = x_ref, out_ref, send_sems, recv_sems

    def slot(self, px, py, pc):
        return self.out_ref.at[4 * px + 2 * py + pc]

    def copy(self, k, block, to, src=None):
        return pltpu.make_async_remote_copy(
            src_ref=self.slot(*block) if src is None else src, dst_ref=self.slot(*block),
            send_sem=self.send_sems.at[k], recv_sem=self.recv_sems.at[k], device_id=to, device_id_type=MESH)

    def first(self):
        return [self.copy(0, self.me, self.sibling, src=self.x_ref)] + [
            self.copy(1 + j, self.me, (*chip, self.c), src=self.x_ref) for j, chip in enumerate(self.chips)]

    def passed(self):
        return [self.copy(4 + j, (*chip, self.c), self.sibling) for j, chip in enumerate(self.chips)]


def _gather_start(x_ref, out_ref, stage, send_sems, recv_sems, local_sem):
    plan = _GatherPlan(x_ref, out_ref, send_sems, recv_sems)
    for cp in plan.first():
        cp.start()
    _copy_through_vmem(x_ref, plan.slot(*plan.me), stage, local_sem)


def _gather_pass_on(out_ref, send_sems, recv_sems):
    plan = _GatherPlan(None, out_ref, send_sems, recv_sems)
    passed = plan.passed()
    for j, chip in enumerate(plan.chips):
        plan.copy(1 + j, (*chip, plan.c), plan.me).wait_recv()
        passed[j].start()


def _gather_finish(x_ref, out_ref, send_sems, recv_sems):
    plan = _GatherPlan(x_ref, out_ref, send_sems, recv_sems)
    plan.copy(0, plan.sibling, plan.me).wait_recv()
    for j, chip in enumerate(plan.chips):
        plan.copy(4 + j, (*chip, 1 - plan.c), plan.me).wait_recv()
    for cp in plan.first() + plan.passed():
        cp.wait_send()


def _all_gather(xs, name):
    r, cdim = xs.shape

    def body(x_ref, out_ref, stage, send_sems, recv_sems, local_sem):
        _gather_start(x_ref, out_ref, stage, send_sems, recv_sems, local_sem)
        _gather_pass_on(out_ref, send_sems, recv_sems)
        _gather_finish(x_ref, out_ref, send_sems, recv_sems)

    return pl.pallas_call(
        body, name=name, out_shape=jax.ShapeDtypeStruct((N_DEV, r, cdim), xs.dtype),
        in_specs=[ANY], out_specs=ANY,
        scratch_shapes=[pltpu.VMEM((r, cdim), xs.dtype), pltpu.SemaphoreType.DMA((7,)), pltpu.SemaphoreType.DMA((7,)),
                        pltpu.SemaphoreType.DMA],
        compiler_params=pltpu.CompilerParams(vmem_limit_bytes=VMEM_LIMIT),
    )(xs)


def _rs_pair(t, name):
    _, r, cdim = t.shape

    def body(t_ref, theirs_ref, send_sems, recv_sems):
        x, y, c = lax.axis_index("x"), lax.axis_index("y"), lax.axis_index("c")
        remote = [pltpu.make_async_remote_copy(
            src_ref=t_ref.at[2 * k + (1 - c)], dst_ref=theirs_ref.at[k],
            send_sem=send_sems.at[k], recv_sem=recv_sems.at[k], device_id=(x, y, 1 - c), device_id_type=MESH)
            for k in range(4)]
        for cp in remote:
            cp.start()
        for cp in remote:
            cp.wait()

    return pl.pallas_call(
        body, name=name, out_shape=jax.ShapeDtypeStruct((4, r, cdim), t.dtype), in_specs=[ANY], out_specs=ANY,
        scratch_shapes=[pltpu.SemaphoreType.DMA((4,)), pltpu.SemaphoreType.DMA((4,))],
    )(t)


def _chips_start(b_ref, out_ref, stage, send_sems, recv_sems, local_sem):
    x, y, c = lax.axis_index("x"), lax.axis_index("y"), lax.axis_index("c")
    mychip = 2 * x + y
    for j, (px, py) in enumerate([(1 - x, y), (x, 1 - y), (1 - x, 1 - y)]):
        pltpu.make_async_remote_copy(
            src_ref=b_ref.at[2 * px + py], dst_ref=out_ref.at[mychip],
            send_sem=send_sems.at[j], recv_sem=recv_sems.at[j], device_id=(px, py, c), device_id_type=MESH).start()
    _copy_through_vmem(b_ref.at[mychip], out_ref.at[mychip], stage, local_sem)


def _chips_finish(b_ref, out_ref, send_sems, recv_sems):
    x, y, c = lax.axis_index("x"), lax.axis_index("y"), lax.axis_index("c")
    for j, (px, py) in enumerate([(1 - x, y), (x, 1 - y), (1 - x, 1 - y)]):
        pltpu.make_async_remote_copy(
            src_ref=b_ref.at[2 * px + py], dst_ref=out_ref.at[2 * px + py],
            send_sem=send_sems.at[j], recv_sem=recv_sems.at[j], device_id=(px, py, c), device_id_type=MESH).wait()


def _pad_rows(a, rows):
    return jnp.pad(a, ((0, rows - a.shape[0]), (0, 0)))


def _pack_in(w_in):
    return _pad_rows(w_in[0].T, ROWS_IN)


def _unpack_in(r):
    return r[0:SHARD_IN].T[None]


def _pack_rest(w_out, w_gate, w_up, w_down, w_ple, w_pg):
    return jnp.concatenate([w_out[0], w_gate[0].T, w_up[0].T, w_down[0], w_ple[0].T.reshape(32, D_MODEL), w_pg[0]],
                           axis=0)


def _unpack_rest(r):
    return (r[0:OFF_GATE][None], r[OFF_GATE:OFF_UP].T[None], r[OFF_UP:OFF_DOWN].T[None],
            r[OFF_DOWN:OFF_PLE][None], r[OFF_PLE:OFF_PG].reshape(128, D_PLE).T[None], r[OFF_PG:ROWS_REST][None])


def _full_rest(g):
    return (g[:, 0:OFF_GATE].reshape(D_MODEL, D_MODEL), g[:, OFF_GATE:OFF_UP].reshape(D_FF, D_MODEL),
            g[:, OFF_UP:OFF_DOWN].reshape(D_FF, D_MODEL), g[:, OFF_DOWN:OFF_PLE].reshape(D_FF, D_MODEL),
            g[:, OFF_PLE:OFF_PG].reshape(D_MODEL, D_PLE), g[:, OFF_PG:ROWS_REST].reshape(D_MODEL, D_MODEL))


def _pack_small(w_pool, g_mix_pre, g_mix_post, g_ffn_pre, g_ffn_post, g_ple, g_attn, g_pool, pool_scale, b_forget,
                loss=None):
    def row(vrow):
        return jnp.pad(vrow.reshape(1, -1), ((0, 0), (0, D_MODEL - vrow.size)))
    rows = [w_pool.reshape(64, D_MODEL), row(g_mix_pre), row(g_mix_post), row(g_ffn_pre), row(g_ffn_post), row(g_ple),
            row(g_attn), row(g_pool), row(pool_scale), row(b_forget),
            row(loss) if loss is not None else jnp.zeros((1, D_MODEL), F32)]
    return _pad_rows(jnp.concatenate(rows, axis=0), SMALL_ROWS)


def _unpack_small(r):
    return dict(
        w_pool=r[0:64].reshape(1, 4, POOL_CH, POOL_CH), g_mix_pre=r[ROW_G_MIX_PRE:ROW_G_MIX_PRE + 1],
        g_mix_post=r[ROW_G_MIX_POST:ROW_G_MIX_POST + 1], g_ffn_pre=r[ROW_G_FFN_PRE:ROW_G_FFN_PRE + 1],
        g_ffn_post=r[ROW_G_FFN_POST:ROW_G_FFN_POST + 1], g_ple=r[ROW_G_PLE:ROW_G_PLE + 1],
        g_attn_grp=r[ROW_G_ATTN:ROW_G_ATTN + 1, 0:D_ATTN], g_pool_grp=r[ROW_G_POOL:ROW_G_POOL + 1, 0:D_POOL],
        pool_scale=r[ROW_POOL_SCALE:ROW_POOL_SCALE + 1, 0:D_POOL], b_forget=r[ROW_B_FORGET:ROW_B_FORGET + 1, 0:HEADS])


def _step(x, p, tgt, small, in_w, in_m, in_v, rest_w, rest_m, rest_v):
    core = lax.axis_index("c").astype(jnp.int32).reshape(1)
    win_t = _all_gather(in_w.astype(BF16), "gather_w_in")[:, 0:SHARD_IN].reshape(D_IN, D_MODEL)
    wqkv = win_t[0:3 * D_ATTN]
    wf = _pad_rows(win_t[3 * D_ATTN:3 * D_ATTN + HEADS], LANES)
    wu = win_t[3 * D_ATTN + HEADS:]
    wpool = small["w_pool"].astype(BF16)
    bpad = jnp.pad(small["b_forget"], ((0, 0), (0, LANES - HEADS)))

    lay = _attn_layout_constants()
    hn, q, ka, v, qat3, vt3, kt3, fl, y, mpre = _pre_attn_fwd(x, small["g_mix_pre"], wqkv, wf, wu, bpad, wpool, lay)
    a, lset3, gathered = _attn_fwd(ka, qat3, vt3, rest_w.astype(BF16))
    wout, wg_t, wu_t, wd, wple_t, wpg = _full_rest(gathered)
    mix, o, h1, hn2 = _post_attn_fwd(a, mpre, x, small["g_attn_grp"], small["g_pool_grp"], small["pool_scale"], wout,
                                     small["g_mix_post"], small["g_ffn_pre"])
    gate, up, act, ff, h2 = _ffn_fwd(hn2, wg_t, wu_t, wd, h1, small["g_ffn_post"])
    dh2, dff, dgl, dpp, h2b, pb, loss8, dg_ple, dg_ffn_post = _tail_fwd_bwd(
        h2, p, tgt, ff, wple_t, wpg, small["g_ple"], small["g_ffn_post"])
    dgate, dup, dh1, dg_ffn_pre = _ffn_bwd(dff, gate, up, wd, wg_t, wu_t, h1, dh2, small["g_ffn_pre"])
    dob, dab, dat3, dlt3, dmpb, dy, dg_mix_post, dg_attn, dg_pool, dps = _post_attn_bwd(
        dh1, o, a, mpre, wout, wpool, small["g_mix_post"], small["g_attn_grp"], small["g_pool_grp"], small["pool_scale"])

    nd = N_DEV
    send_rest = jnp.concatenate([
        _wgrad(mix, dob, BF16, "wgrad_out").reshape(nd, 128, D_MODEL),
        _wgrad(dgate, hn2, BF16, "wgrad_gate").reshape(nd, 352, D_MODEL),
        _wgrad(dup, hn2, BF16, "wgrad_up").reshape(nd, 352, D_MODEL),
        _wgrad(act, dff, BF16, "wgrad_down").reshape(nd, 352, D_MODEL),
        _wgrad(dpp, pb, BF16, "wgrad_ple").reshape(nd, 32, D_MODEL),
        _wgrad(h2b, dgl, BF16, "wgrad_ple_gate").reshape(nd, 128, D_MODEL)], axis=1)
    pair_rest = _pair_sum(core, send_rest, _rs_pair(send_rest, "rs_pair_rest"), TR_REST, "rs_pair_sum_rest")

    dqt3, dk, dv, dcs, drs4, chips_rest = _attn_bwd(ka, v, kt3, qat3, q, dab, dat3, lset3, dlt3, pair_rest)
    drs = jnp.pad(drs4[:, :, 0, :].transpose(0, 2, 1).reshape(-1, HEADS), ((0, 0), (0, LANES - HEADS)))
    gx, dz, dg_mix_pre, db = _pre_attn_bwd(dqt3, dk, dv, dcs, drs, fl, dy, x, dh1, small["g_mix_pre"], wqkv, wf, wu)

    dwz = _wgrad(dz, hn, F32, "wgrad_in")
    dwin_t = jnp.concatenate([dwz[0:3 * D_ATTN], dwz[3 * D_ATTN:3 * D_ATTN + HEADS], dwz[3 * D_ATTN + LANES:]], axis=0)
    send_in = jnp.pad(dwin_t.reshape(nd, SHARD_IN, D_MODEL), ((0, 0), (0, ROWS_IN - SHARD_IN), (0, 0))).astype(BF16)
    pair_in = _pair_sum(core, send_in, _rs_pair(send_in, "rs_pair_in"), ROWS_IN, "rs_pair_sum_in")

    dwp = _wgrad(y, dmpb, F32, "wgrad_pool")
    dw_pool = jnp.stack([dwp[g * POOL_CH:(g + 1) * POOL_CH, g * POOL_CH:(g + 1) * POOL_CH] for g in range(4)])
    small_part = _pack_small(dw_pool, dg_mix_pre, dg_mix_post, dg_ffn_pre, dg_ffn_post, dg_ple, dg_attn, dg_pool, dps,
                             db[:, 0:HEADS], loss8[0:1, 0:1])

    *upd_rest, chips_in, small_all = _reduce_update_rest(chips_rest, rest_w, rest_m, rest_v, pair_in, small_part)
    upd_in = _reduce_update_big(chips_in, in_w, in_m, in_v, ROWS_IN, "reduce_update_in")
    return gx, small_all, upd_in, upd_rest


def kernel(x, p, g_mix_pre, w_in, b_forget, g_attn_grp, g_pool_grp, w_pool, pool_scale, w_out, g_mix_post, g_ffn_pre, w_ffn_gate, w_ffn_up, w_ffn_down, g_ffn_post, w_ple_proj, g_ple, w_ple_gate, loss_target, m_g_mix_pre, m_w_in, m_b_forget, m_g_attn_grp, m_g_pool_grp, m_w_pool, m_pool_scale, m_w_out, m_g_mix_post, m_g_ffn_pre, m_w_ffn_gate, m_w_ffn_up, m_w_ffn_down, m_g_ffn_post, m_w_ple_proj, m_g_ple, m_w_ple_gate, v_g_mix_pre, v_w_in, v_b_forget, v_g_attn_grp, v_g_pool_grp, v_w_pool, v_pool_scale, v_w_out, v_g_mix_post, v_g_ffn_pre, v_w_ffn_gate, v_w_ffn_up, v_w_ffn_down, v_g_ffn_post, v_w_ple_proj, v_g_ple, v_w_ple_gate):
    small = dict(w_pool=w_pool[0], g_mix_pre=g_mix_pre, g_mix_post=g_mix_post, g_ffn_pre=g_ffn_pre,
                 g_ffn_post=g_ffn_post, g_ple=g_ple, g_attn_grp=g_attn_grp, g_pool_grp=g_pool_grp,
                 pool_scale=pool_scale, b_forget=b_forget)
    gx, small_all, upd_in, upd_rest = _step(
        x[0], p[0, 0], loss_target[0], small, _pack_in(w_in), _pack_in(m_w_in), _pack_in(v_w_in),
        _pack_rest(w_out, w_ffn_gate, w_ffn_up, w_ffn_down, w_ple_proj, w_ple_gate),
        _pack_rest(m_w_out, m_w_ffn_gate, m_w_ffn_up, m_w_ffn_down, m_w_ple_proj, m_w_ple_gate),
        _pack_rest(v_w_out, v_w_ffn_gate, v_w_ffn_up, v_w_ffn_down, v_w_ple_proj, v_w_ple_gate))

    sm_w = _pack_small(w_pool, g_mix_pre, g_mix_post, g_ffn_pre, g_ffn_post, g_ple, g_attn_grp, g_pool_grp, pool_scale, b_forget)
    sm_m = _pack_small(m_w_pool, m_g_mix_pre, m_g_mix_post, m_g_ffn_pre, m_g_ffn_post, m_g_ple, m_g_attn_grp, m_g_pool_grp, m_pool_scale, m_b_forget)
    sm_v = _pack_small(v_w_pool, v_g_mix_pre, v_g_mix_post, v_g_ffn_pre, v_g_ffn_post, v_g_ple, v_g_attn_grp, v_g_pool_grp, v_pool_scale, v_b_forget)
    upd_small = _reduce_update_small(small_all, sm_w, sm_m, sm_v)
    loss = upd_small[0][ROW_LOSS, 0]

    def leaves(k):
        b_out, b_gate, b_up, b_down, b_ple, b_pg = _unpack_rest(upd_rest[k])
        s = _unpack_small(upd_small[k])
        return (s["g_mix_pre"], _unpack_in(upd_in[k]), s["b_forget"], s["g_attn_grp"], s["g_pool_grp"], s["w_pool"],
                s["pool_scale"], b_out, s["g_mix_post"], s["g_ffn_pre"], b_gate, b_up, b_down, s["g_ffn_post"], b_ple,
                s["g_ple"], b_pg)

    return (loss, gx[None], *leaves(0), *leaves(1), *leaves(2), *leaves(3))
```

```python
import functools

import jax
import jax.numpy as jnp
from jax import lax
from jax.experimental import pallas as pl
from jax.experimental.pallas import tpu as pltpu

F32 = jnp.float32
BF16 = jnp.bfloat16
HIGHEST = lax.Precision.HIGHEST

D_MODEL = 1024
HEADS = 8
HEAD_DIM = 64
D_ATTN = HEADS * HEAD_DIM
POOL_WINDOWS = (2, 4, 8, 16)
POOL_CH = 128
D_POOL = POOL_CH * len(POOL_WINDOWS)
D_FF = 2816
D_PLE = 256
D_IN = 3 * D_ATTN + HEADS + D_POOL
RMS_EPS = 1e-6
N_DEV = 8

ADAM_LR = 0.001
ADAM_B1 = 0.9
ADAM_B2 = 0.999
ADAM_EPS = 1e-08
ADAM_WD = 0.01
ADAM_STEP = 10

LANES = 128
HALO = 16
TS = 512
TS_FF = 1024
TS_WGRAD = 1024
TM_WGRAD = 2176
TQ = 256
TN_FF = 256
NEG = -1e30
VMEM_LIMIT = 56 * 1024 * 1024

SHARD_IN = 257
ROWS_IN = 272
OFF_GATE = 128
OFF_UP = OFF_GATE + 352
OFF_DOWN = OFF_UP + 352
OFF_PLE = OFF_DOWN + 352
OFF_PG = OFF_PLE + 32
ROWS_REST = OFF_PG + 128
TR_REST = 192

SMALL_ROWS = 80
ROW_G_MIX_PRE, ROW_G_MIX_POST, ROW_G_FFN_PRE, ROW_G_FFN_POST, ROW_G_PLE = 64, 65, 66, 67, 68
ROW_G_ATTN, ROW_G_POOL, ROW_POOL_SCALE, ROW_B_FORGET, ROW_LOSS = 69, 70, 71, 72, 73


def _nn(a, b):
    return jnp.dot(a, b, preferred_element_type=F32)


def _nt(a, b):
    return lax.dot_general(a, b, (((1,), (1,)), ((), ())), preferred_element_type=F32)


def _tn(a, b):
    return lax.dot_general(a, b, (((0,), (0,)), ((), ())), preferred_element_type=F32)


def _rstd(v):
    return lax.rsqrt(jnp.mean(v * v, axis=-1, keepdims=True) + RMS_EPS)


def _rms_bwd(v, g, dy):
    r = _rstd(v)
    vh = v * r
    t = dy * g
    dv = r * (t - vh * jnp.mean(t * vh, axis=-1, keepdims=True))
    return dv, jnp.sum(dy * vh, axis=0, keepdims=True)


def _params(n_grid):
    return pltpu.CompilerParams(dimension_semantics=("arbitrary",) * n_grid, vmem_limit_bytes=VMEM_LIMIT)


def _row(i):
    return (i, 0)


def _fixed(*_):
    return (0, 0)


assert TS == 2 * TQ
_HALVES = (slice(0, TQ), slice(TQ, TS))

VMEM_WHOLE = pl.BlockSpec(memory_space=pltpu.VMEM)
SMEM_WHOLE = pl.BlockSpec(memory_space=pltpu.SMEM)
ANY = pl.BlockSpec(memory_space=pl.ANY)


AUG = 128
BIAS_LANE = HEAD_DIM
ONE_LANE = HEAD_DIM + 3
SPARE_LANE = HEADS


def _attn_layout_constants():
    import numpy as np
    place = np.zeros((D_ATTN, HEADS * AUG), np.float32)
    for r in range(D_ATTN):
        place[r, (r // HEAD_DIM) * AUG + r % HEAD_DIM] = 1.0
    bias_k = np.zeros((3, LANES, HEADS * AUG), np.float32)
    bias_q = np.zeros((3, LANES, HEADS * AUG), np.float32)
    for h in range(HEADS):
        for part in range(3):
            bias_k[part, h, h * AUG + BIAS_LANE + part] = -1.0
            bias_q[part, h, h * AUG + ONE_LANE + part] = 1.0
            bias_k[0, SPARE_LANE, h * AUG + ONE_LANE + part] = 1.0
            bias_q[0, SPARE_LANE, h * AUG + BIAS_LANE + part] = 1.0
    as_bf = lambda a: jnp.asarray(a, BF16)
    return dict(place=as_bf(place), place_t=as_bf(place.T), bias_k=as_bf(bias_k),
                bias_q_t=as_bf(bias_q.transpose(0, 2, 1)))


def _pre_attn_fwd(x, g1, wqkv, wf, wu, bpad, wpool, lay):
    s, d = x.shape
    nt = s // TS
    sub = TS // TQ

    def body(x_ref, g_ref, wqkv_ref, wf_ref, wu_ref, b_ref, wp_ref, place_ref, place_t_ref, bk_ref, bqt_ref,
             hn_ref, q_ref, ka_ref, v_ref, qat_ref, vt_ref, kt_ref, fl_ref, y_ref, mp_ref, ubuf, ccar, cbuf):
        i = pl.program_id(0)

        @pl.when(i == 0)
        def _():
            ubuf[0:HALO, :] = jnp.zeros((HALO, D_POOL), F32)
            ccar[...] = jnp.zeros_like(ccar)

        xv = x_ref[...]
        hn = (xv * _rstd(xv) * g_ref[...]).astype(BF16)
        hn_ref[...] = hn
        zq = _nt(hn, wqkv_ref[...])
        qb = (zq[:, 0:D_ATTN] * 0.125).astype(BF16)
        kb = zq[:, D_ATTN:2 * D_ATTN].astype(BF16)
        vb = zq[:, 2 * D_ATTN:3 * D_ATTN].astype(BF16)
        q_ref[...] = qb
        v_ref[...] = vb

        fl = _nt(hn, wf_ref[...]) + b_ref[...]
        fl_ref[...] = fl
        logf = jax.nn.log_sigmoid(fl)
        rr = lax.broadcasted_iota(jnp.int32, (TS, TS), 0)
        cc = lax.broadcasted_iota(jnp.int32, (TS, TS), 1)
        tril = (cc <= rr).astype(F32)
        c = jnp.dot(tril, logf, precision=HIGHEST, preferred_element_type=F32) + ccar[...]
        cbuf[...] = c
        ccar[...] = cbuf[TS - 1:TS, :]
        hi = c.astype(BF16)
        rest = c - hi.astype(F32)
        mid = rest.astype(BF16)
        lo = (rest - mid.astype(F32)).astype(BF16)
        lane = lax.broadcasted_iota(jnp.int32, (TS, LANES), 1)
        parts = (jnp.where(lane == SPARE_LANE, 1.0, hi).astype(BF16), mid, lo)
        ka = _nn(kb, place_ref[...])
        qat = _nt(place_t_ref[...], qb)
        for part in range(3):
            ka = ka + _nn(parts[part], bk_ref[part])
            qat = qat + _nt(bqt_ref[part], parts[part])
        ka_ref[...] = ka.astype(BF16)
        qat = qat.astype(BF16)
        vt = vb.T
        kt = kb.T
        for a in range(sub):
            qat_ref[a] = qat[:, a * TQ:(a + 1) * TQ]
            vt_ref[a] = vt[:, a * TQ:(a + 1) * TQ]
            kt_ref[a] = kt[:, a * TQ:(a + 1) * TQ]

        u = _nt(hn, wu_ref[...])
        ubuf[HALO:HALO + TS, :] = u
        t = i * TS + lax.broadcasted_iota(jnp.int32, (TS, 1), 0)
        for g, w in enumerate(POOL_WINDOWS):
            cols = slice(g * POOL_CH, (g + 1) * POOL_CH)
            sm = ubuf[:, cols]
            step = 1
            while step < w:
                sm = sm + pltpu.roll(sm, step, 0)
                step *= 2
            cnt = jnp.minimum(t + 1, w).astype(F32)
            yg = (sm[HALO:, :] / cnt - u[:, cols]).astype(BF16)
            y_ref[:, cols] = yg
            mp_ref[:, cols] = _nn(yg, wp_ref[g])
        ubuf[0:HALO, :] = u[TS - HALO:, :]

    nq = s // TQ
    aug = HEADS * AUG
    outs = (
        jax.ShapeDtypeStruct((s, d), BF16), jax.ShapeDtypeStruct((s, D_ATTN), BF16),
        jax.ShapeDtypeStruct((s, aug), BF16), jax.ShapeDtypeStruct((s, D_ATTN), BF16),
        jax.ShapeDtypeStruct((nq, aug, TQ), BF16), jax.ShapeDtypeStruct((nq, D_ATTN, TQ), BF16),
        jax.ShapeDtypeStruct((nq, D_ATTN, TQ), BF16),
        jax.ShapeDtypeStruct((s, LANES), F32),
        jax.ShapeDtypeStruct((s, D_POOL), BF16), jax.ShapeDtypeStruct((s, D_POOL), F32),
    )
    fixed3 = lambda i: (0, 0, 0)
    tiles3 = lambda rows: pl.BlockSpec((sub, rows, TQ), lambda i: (i, 0, 0))
    return pl.pallas_call(
        body, grid=(nt,), out_shape=outs, name="pre_attn_fwd",
        in_specs=[pl.BlockSpec((TS, d), _row), pl.BlockSpec((1, d), _fixed),
                  pl.BlockSpec(wqkv.shape, _fixed), pl.BlockSpec(wf.shape, _fixed), pl.BlockSpec(wu.shape, _fixed),
                  pl.BlockSpec((1, LANES), _fixed), pl.BlockSpec(wpool.shape, fixed3),
                  pl.BlockSpec(lay["place"].shape, _fixed), pl.BlockSpec(lay["place_t"].shape, _fixed),
                  pl.BlockSpec(lay["bias_k"].shape, fixed3), pl.BlockSpec(lay["bias_q_t"].shape, fixed3)],
        out_specs=(pl.BlockSpec((TS, d), _row), pl.BlockSpec((TS, D_ATTN), _row),
                   pl.BlockSpec((TS, aug), _row), pl.BlockSpec((TS, D_ATTN), _row),
                   tiles3(aug), tiles3(D_ATTN), tiles3(D_ATTN),
                   pl.BlockSpec((TS, LANES), _row),
                   pl.BlockSpec((TS, D_POOL), _row), pl.BlockSpec((TS, D_POOL), _row)),
        scratch_shapes=[pltpu.VMEM((TS + HALO, D_POOL), F32), pltpu.VMEM((1, LANES), F32), pltpu.VMEM((TS, LANES), F32)],
        compiler_params=_params(1),
    )(x, g1, wqkv, wf, wu, bpad, wpool, lay["place"], lay["place_t"], lay["bias_k"], lay["bias_q_t"])


def _causal_in_tile():
    krow = lax.broadcasted_iota(jnp.int32, (TQ, TQ), 0)
    qcol = lax.broadcasted_iota(jnp.int32, (TQ, TQ), 1)
    return krow <= qcol


def _attn_fwd(ka, qat3, vt3, own_block):
    s = ka.shape[0]
    nq = s // TQ
    pass_on_step = max(nq - 2, 0)

    def body(qa_ref, ka_ref, vt_ref, own_ref, a_ref, lset_ref, all_ref, acc, st_scr, pt_scr,
             stage, send_sems, recv_sems, local_sem):
        i = pl.program_id(0)

        @pl.when(i == 0)
        def _():
            _gather_start(own_ref, all_ref, stage, send_sems, recv_sems, local_sem)

        @pl.when(i == pass_on_step)
        def _():
            _gather_pass_on(all_ref, send_sems, recv_sems)

        acc[...] = jnp.zeros_like(acc)

        def tile(j, stats, masked):
            tile_max = []
            for h in range(HEADS):
                aug = slice(h * AUG, (h + 1) * AUG)
                st = _nn(ka_ref[pl.ds(j * TQ, TQ), aug], qa_ref[0, aug, :])
                if masked:
                    st = jnp.where(_causal_in_tile(), st, NEG)
                st_scr[h] = st
                tile_max.append(jnp.max(st, axis=0, keepdims=True))
            new, scale = [], []
            for h in range(HEADS):
                m_old, l_old = stats[h]
                m_new = jnp.maximum(m_old, tile_max[h])
                al = jnp.exp(m_old - m_new)
                pt = jnp.exp(st_scr[h] - m_new)
                pt_scr[h] = pt.astype(BF16)
                new.append((m_new, al * l_old + jnp.sum(pt, axis=0, keepdims=True)))
                scale.append(al)
            for h in range(HEADS):
                rows = slice(h * HEAD_DIM, (h + 1) * HEAD_DIM)
                acc[rows, :] = scale[h] * acc[rows, :] + _nn(vt_ref[j, rows, :], pt_scr[h])
            return tuple(new)

        init = tuple((jnp.full((1, TQ), NEG, F32), jnp.zeros((1, TQ), F32)) for _ in range(HEADS))
        stats = lax.fori_loop(0, i, functools.partial(tile, masked=False), init)
        stats = tile(i, stats, True)
        for h in range(HEADS):
            rows = slice(h * HEAD_DIM, (h + 1) * HEAD_DIM)
            acc[rows, :] = acc[rows, :] / stats[h][1]
            lset_ref[0, h:h + 1, :] = stats[h][0] + jnp.log(stats[h][1])
        a_ref[...] = acc[...].T

        @pl.when(i == nq - 1)
        def _():
            _gather_finish(own_ref, all_ref, send_sems, recv_sems)

    r, cdim = own_block.shape
    return pl.pallas_call(
        body, grid=(nq,), name="attn_fwd",
        out_shape=(jax.ShapeDtypeStruct((s, D_ATTN), F32), jax.ShapeDtypeStruct((nq, HEADS, TQ), F32),
                   jax.ShapeDtypeStruct((N_DEV, r, cdim), own_block.dtype)),
        in_specs=[pl.BlockSpec((1, HEADS * AUG, TQ), lambda i: (i, 0, 0)), VMEM_WHOLE, VMEM_WHOLE, ANY],
        out_specs=(pl.BlockSpec((TQ, D_ATTN), _row), pl.BlockSpec((1, HEADS, TQ), lambda i: (i, 0, 0)), ANY),
        scratch_shapes=[pltpu.VMEM((D_ATTN, TQ), F32), pltpu.VMEM((HEADS, TQ, TQ), F32), pltpu.VMEM((HEADS, TQ, TQ), BF16),
                        pltpu.VMEM((r, cdim), own_block.dtype),
                        pltpu.SemaphoreType.DMA((7,)), pltpu.SemaphoreType.DMA((7,)), pltpu.SemaphoreType.DMA],
        compiler_params=_params(1),
    )(qat3, ka, vt3, own_block)


def _post_attn_fwd(a, mpre, x, g_attn, g_pool, pscale, wout, g_post, g_ffn_pre):
    s, d = x.shape

    def body(a_ref, mp_ref, x_ref, ga_ref, gp_ref, ps_ref, wo_ref, gpost_ref, gpre_ref,
             mix_ref, o_ref, h1_ref, hn2_ref):
        for rows in _HALVES:
            av = a_ref[rows, :]
            mix_ref[rows, 0:D_ATTN] = (av * _rstd(av) * ga_ref[...]).astype(BF16)
            mv = mp_ref[rows, :] * ps_ref[...]
            mix_ref[rows, D_ATTN:] = (mv * _rstd(mv) * gp_ref[...]).astype(BF16)
            o = _nn(mix_ref[rows, :], wo_ref[...])
            o_ref[rows, :] = o
            h1 = x_ref[rows, :] + o * _rstd(o) * gpost_ref[...]
            h1_ref[rows, :] = h1
            hn2_ref[rows, :] = (h1 * _rstd(h1) * gpre_ref[...]).astype(BF16)

    vec = lambda n: pl.BlockSpec((1, n), _fixed)
    return pl.pallas_call(
        body, grid=(s // TS,), name="post_attn_fwd",
        out_shape=(jax.ShapeDtypeStruct((s, d), BF16), jax.ShapeDtypeStruct((s, d), F32),
                   jax.ShapeDtypeStruct((s, d), F32), jax.ShapeDtypeStruct((s, d), BF16)),
        in_specs=[pl.BlockSpec((TS, D_ATTN), _row), pl.BlockSpec((TS, D_POOL), _row), pl.BlockSpec((TS, d), _row),
                  vec(D_ATTN), vec(D_POOL), vec(D_POOL), pl.BlockSpec(wout.shape, _fixed), vec(d), vec(d)],
        out_specs=(pl.BlockSpec((TS, d), _row),) * 4,
        compiler_params=_params(1),
    )(a, mpre, x, g_attn, g_pool, pscale, wout, g_post, g_ffn_pre)


def _ffn_fwd(hn2, wg, wu, wd, h1, g_post):
    s, d = h1.shape
    nc = D_FF // TN_FF
    ts = min(TS_FF, s)

    def body(hn_ref, wg_ref, wu_ref, wd_ref, h1_ref, g_ref, gate_ref, up_ref, act_ref, ff_ref, h2_ref, acc):
        j = pl.program_id(1)

        @pl.when(j == 0)
        def _():
            acc[...] = jnp.zeros_like(acc)

        for r in range(2):
            rows = slice(r * (ts // 2), (r + 1) * (ts // 2))
            hn = hn_ref[rows, :]
            gt = _nt(hn, wg_ref[...])
            up = _nt(hn, wu_ref[...])
            gate_ref[rows, :] = gt.astype(BF16)
            up_ref[rows, :] = up.astype(BF16)
            act_ref[rows, :] = (gt * jax.nn.sigmoid(gt) * up).astype(BF16)
            acc[rows, :] += _nn(act_ref[rows, :], wd_ref[...])

        @pl.when(j == nc - 1)
        def _():
            ff = acc[...]
            ff_ref[...] = ff
            h2_ref[...] = h1_ref[...] + ff * _rstd(ff) * g_ref[...]

    rowblk = pl.BlockSpec((ts, d), lambda i, j: (i, 0))
    wblk = pl.BlockSpec((TN_FF, d), lambda i, j: (j, 0))
    chunk = pl.BlockSpec((ts, TN_FF), lambda i, j: (i, j))
    return pl.pallas_call(
        body, grid=(s // ts, nc), name="ffn_fwd",
        out_shape=(jax.ShapeDtypeStruct((s, D_FF), BF16),) * 3 + (jax.ShapeDtypeStruct((s, d), F32),) * 2,
        in_specs=[rowblk, wblk, wblk, wblk, rowblk, pl.BlockSpec((1, d), lambda i, j: (0, 0))],
        out_specs=(chunk, chunk, chunk, rowblk, rowblk),
        scratch_shapes=[pltpu.VMEM((ts, d), F32)],
        compiler_params=_params(2),
    )(hn2, wg, wu, wd, h1, g_post)


def _tail_fwd_bwd(h2, p, tgt, ff, wple, wpg, g_ple, g_ffn_post):
    s, d = h2.shape

    def body(h2_ref, p_ref, t_ref, ff_ref, wple_ref, wpg_ref, gple_ref, gfp_ref,
             dh2_ref, dff_ref, dgl_ref, dpp_ref, h2b_ref, pb_ref, loss_ref, dgple_ref, dgfp_ref):
        i = pl.program_id(0)

        @pl.when(i == 0)
        def _():
            loss_ref[...] = jnp.zeros_like(loss_ref)
            dgple_ref[...] = jnp.zeros_like(dgple_ref)
            dgfp_ref[...] = jnp.zeros_like(dgfp_ref)

        h2 = h2_ref[...]
        h2b = h2.astype(BF16)
        h2b_ref[...] = h2b
        pb = p_ref[...].astype(BF16)
        pb_ref[...] = pb
        pp = _nt(pb, wple_ref[...])
        gple = gple_ref[...]
        e = pp * _rstd(pp) * gple
        sg = jax.nn.sigmoid(_nn(h2b, wpg_ref[...]))
        diff = h2 + sg * e - t_ref[...]
        sq = jnp.sum(jnp.sum(diff * diff, axis=1, keepdims=True), axis=0, keepdims=True)
        loss_ref[...] += jnp.broadcast_to(sq * (0.5 / d), loss_ref.shape)
        dh3 = diff * (1.0 / d)
        dgl = (dh3 * e * sg * (1.0 - sg)).astype(BF16)
        dgl_ref[...] = dgl
        dh2 = dh3 + _nt(dgl, wpg_ref[...])
        dh2_ref[...] = dh2
        dpp, dg = _rms_bwd(pp, gple, dh3 * sg)
        dpp_ref[...] = dpp.astype(BF16)
        dgple_ref[...] += dg
        dff, dg = _rms_bwd(ff_ref[...], gfp_ref[...], dh2)
        dff_ref[...] = dff.astype(BF16)
        dgfp_ref[...] += dg

    rowblk = pl.BlockSpec((TS, d), _row)
    vec = pl.BlockSpec((1, d), _fixed)
    return pl.pallas_call(
        body, grid=(s // TS,), name="tail_fwd_bwd",
        out_shape=(jax.ShapeDtypeStruct((s, d), F32), jax.ShapeDtypeStruct((s, d), BF16),
                   jax.ShapeDtypeStruct((s, d), BF16), jax.ShapeDtypeStruct((s, d), BF16),
                   jax.ShapeDtypeStruct((s, d), BF16), jax.ShapeDtypeStruct((s, D_PLE), BF16),
                   jax.ShapeDtypeStruct((8, LANES), F32), jax.ShapeDtypeStruct((1, d), F32),
                   jax.ShapeDtypeStruct((1, d), F32)),
        in_specs=[rowblk, pl.BlockSpec((TS, D_PLE), _row), rowblk, rowblk,
                  pl.BlockSpec(wple.shape, _fixed), pl.BlockSpec(wpg.shape, _fixed), vec, vec],
        out_specs=(rowblk, rowblk, rowblk, rowblk, rowblk, pl.BlockSpec((TS, D_PLE), _row),
                   pl.BlockSpec((8, LANES), _fixed), vec, vec),
        compiler_params=_params(1),
    )(h2, p, tgt, ff, wple, wpg, g_ple, g_ffn_post)


def _ffn_bwd(dff, gate, up, wd, wg, wu, h1, dh2, g_pre):
    s, d = h1.shape
    nc = D_FF // TN_FF
    ts = min(TS_FF, s)

    def body(dff_ref, gate_ref, up_ref, wd_ref, wg_ref, wu_ref, h1_ref, dh2_ref, g_ref,
             dgate_ref, dup_ref, dh1_ref, dg_ref, acc):
        i = pl.program_id(0)
        j = pl.program_id(1)

        @pl.when((i == 0) & (j == 0))
        def _():
            dg_ref[...] = jnp.zeros_like(dg_ref)

        @pl.when(j == 0)
        def _():
            acc[...] = jnp.zeros_like(acc)

        for r in range(2):
            rows = slice(r * (ts // 2), (r + 1) * (ts // 2))
            dact = _nt(dff_ref[rows, :], wd_ref[...])
            gt = gate_ref[rows, :].astype(F32)
            sg = jax.nn.sigmoid(gt)
            dup_ref[rows, :] = (dact * gt * sg).astype(BF16)
            dgate_ref[rows, :] = (dact * up_ref[rows, :].astype(F32) * (sg * (1.0 + gt * (1.0 - sg)))).astype(BF16)
            acc[rows, :] += _nn(dgate_ref[rows, :], wg_ref[...]) + _nn(dup_ref[rows, :], wu_ref[...])

        @pl.when(j == nc - 1)
        def _():
            dv, dg = _rms_bwd(h1_ref[...], g_ref[...], acc[...])
            dh1_ref[...] = dh2_ref[...] + dv
            dg_ref[...] += dg

    rowblk = pl.BlockSpec((ts, d), lambda i, j: (i, 0))
    wblk = pl.BlockSpec((TN_FF, d), lambda i, j: (j, 0))
    chunk = pl.BlockSpec((ts, TN_FF), lambda i, j: (i, j))
    vec = pl.BlockSpec((1, d), lambda i, j: (0, 0))
    return pl.pallas_call(
        body, grid=(s // ts, nc), name="ffn_bwd",
        out_shape=(jax.ShapeDtypeStruct((s, D_FF), BF16), jax.ShapeDtypeStruct((s, D_FF), BF16),
                   jax.ShapeDtypeStruct((s, d), F32), jax.ShapeDtypeStruct((1, d), F32)),
        in_specs=[rowblk, chunk, chunk, wblk, wblk, wblk, rowblk, rowblk, vec],
        out_specs=(chunk, chunk, rowblk, vec),
        scratch_shapes=[pltpu.VMEM((ts, d), F32)],
        compiler_params=_params(2),
    )(dff, gate, up, wd, wg, wu, h1, dh2, g_pre)


def _post_attn_bwd(dh1, o, a, mpre, wout, wpool, g_post, g_attn, g_pool, pscale):
    s, d = dh1.shape
    sub = TS // TQ

    def body(dh1_ref, o_ref, a_ref, mp_ref, wo_ref, wp_ref, gpost_ref, ga_ref, gp_ref, ps_ref,
             dob_ref, dab_ref, dat_ref, dlt_ref, dmpb_ref, dy_ref, dgpost_ref, dga_ref, dgp_ref, dps_ref):
        i = pl.program_id(0)

        @pl.when(i == 0)
        def _():
            dgpost_ref[...] = jnp.zeros_like(dgpost_ref)
            dga_ref[...] = jnp.zeros_like(dga_ref)
            dgp_ref[...] = jnp.zeros_like(dgp_ref)
            dps_ref[...] = jnp.zeros_like(dps_ref)

        do, dg = _rms_bwd(o_ref[...], gpost_ref[...], dh1_ref[...])
        dgpost_ref[...] += dg
        dob = do.astype(BF16)
        dob_ref[...] = dob
        dmix = _nt(dob, wo_ref[...])

        av = a_ref[...]
        da, dg = _rms_bwd(av, ga_ref[...], dmix[:, 0:D_ATTN])
        dga_ref[...] += dg
        dab = da.astype(BF16)
        dab_ref[...] = dab
        dat = dab.T
        hsel = (lax.shift_right_logical(lax.broadcasted_iota(jnp.int32, (HEADS, D_ATTN), 1), 6)
                == lax.broadcasted_iota(jnp.int32, (HEADS, D_ATTN), 0)).astype(F32)
        dlt = lax.dot_general(hsel, da * av, (((1,), (1,)), ((), ())), precision=HIGHEST, preferred_element_type=F32)
        for q in range(sub):
            dlt_ref[q] = dlt[:, q * TQ:(q + 1) * TQ]
            dat_ref[q] = dat[:, q * TQ:(q + 1) * TQ]

        ps = ps_ref[...]
        mp = mp_ref[...]
        dm, dg = _rms_bwd(mp * ps, gp_ref[...], dmix[:, D_ATTN:])
        dgp_ref[...] += dg
        dps_ref[...] += jnp.sum(dm * mp, axis=0, keepdims=True)
        dmpb = (dm * ps).astype(BF16)
        dmpb_ref[...] = dmpb
        for g in range(len(POOL_WINDOWS)):
            cols = slice(g * POOL_CH, (g + 1) * POOL_CH)
            dy_ref[:, cols] = _nt(dmpb[:, cols], wp_ref[g])

    rowblk = pl.BlockSpec((TS, d), _row)
    half = pl.BlockSpec((TS, D_ATTN), _row)
    vec = lambda n: pl.BlockSpec((1, n), _fixed)
    return pl.pallas_call(
        body, grid=(s // TS,), name="post_attn_bwd",
        out_shape=(jax.ShapeDtypeStruct((s, d), BF16), jax.ShapeDtypeStruct((s, D_ATTN), BF16),
                   jax.ShapeDtypeStruct((s // TQ, D_ATTN, TQ), BF16),
                   jax.ShapeDtypeStruct((s // TQ, HEADS, TQ), F32), jax.ShapeDtypeStruct((s, D_POOL), BF16),
                   jax.ShapeDtypeStruct((s, D_POOL), F32), jax.ShapeDtypeStruct((1, d), F32),
                   jax.ShapeDtypeStruct((1, D_ATTN), F32), jax.ShapeDtypeStruct((1, D_POOL), F32),
                   jax.ShapeDtypeStruct((1, D_POOL), F32)),
        in_specs=[rowblk, rowblk, half, half, pl.BlockSpec(wout.shape, _fixed),
                  pl.BlockSpec(wpool.shape, lambda i: (0, 0, 0)), vec(d), vec(D_ATTN), vec(D_POOL), vec(D_POOL)],
        out_specs=(rowblk, half, pl.BlockSpec((sub, D_ATTN, TQ), lambda i: (i, 0, 0)),
                   pl.BlockSpec((sub, HEADS, TQ), lambda i: (i, 0, 0)), half, half,
                   vec(d), vec(D_ATTN), vec(D_POOL), vec(D_POOL)),
        compiler_params=_params(1),
    )(dh1, o, a, mpre, wout, wpool, g_post, g_attn, g_pool, pscale)


def _attn_bwd(ka, v, kt3, qat3, q, do, dot3, lset3, dlt3, chip_blocks):
    s = q.shape[0]
    nq = s // TQ
    wide = HEADS * LANES

    def body(ka_ref, v_ref, kt_ref, qat_ref, q_ref, do_ref, dot_ref, lset_ref, dlt_ref, b_ref,
             dqt_ref, dk_ref, dv_ref, dcs_ref, drs_ref, got_ref, dca, dkw, dvw, pt_scr, ptb_scr, dsb_scr,
             stage, send_sems, recv_sems, local_sem):
        j = pl.program_id(0)

        @pl.when(j == 0)
        def _():
            _chips_start(b_ref, got_ref, stage, send_sems, recv_sems, local_sem)
            dqt_ref[...] = jnp.zeros_like(dqt_ref)
            drs_ref[...] = jnp.zeros_like(drs_ref)

        dkw[...] = jnp.zeros_like(dkw)
        dvw[...] = jnp.zeros_like(dvw)
        dca[...] = jnp.zeros_like(dca)

        def tile(i, masked):
            rows = pl.ds(i * TQ, TQ)
            for h in range(HEADS):
                aug = slice(h * AUG, (h + 1) * AUG)
                st = _nn(ka_ref[:, aug], qat_ref[i, aug, :]) - lset_ref[i, h:h + 1, :]
                if masked:
                    st = jnp.where(_causal_in_tile(), st, NEG)
                pt = jnp.exp(st)
                pt_scr[h] = pt
                ptb_scr[h] = pt.astype(BF16)
            for h in range(HEADS):
                hs = slice(h * HEAD_DIM, (h + 1) * HEAD_DIM)
                half = slice(h * LANES, h * LANES + HEAD_DIM)
                dvw[:, half] += _nn(ptb_scr[h], do_ref[rows, hs])
                dst = pt_scr[h] * (_nn(v_ref[:, hs], dot_ref[i, hs, :]) - dlt_ref[i, h:h + 1, :])
                dsb_scr[h] = dst.astype(BF16)
                drs_ref[i, h, 0:1, :] += jnp.sum(dst, axis=0, keepdims=True)
                dca[:, h * LANES:(h + 1) * LANES] += dst[:, 0:LANES] + dst[:, LANES:2 * LANES]
            for h in range(HEADS):
                hs = slice(h * HEAD_DIM, (h + 1) * HEAD_DIM)
                half = slice(h * LANES, h * LANES + HEAD_DIM)
                dkw[:, half] += _nn(dsb_scr[h], q_ref[rows, hs])
                dqt_ref[i, hs, :] += _nn(kt_ref[0, hs, :], dsb_scr[h])

        def step(i, carry):
            tile(i, False)
            return carry

        tile(j, True)
        lax.fori_loop(j + 1, nq, step, 0)
        lane = lax.broadcasted_iota(jnp.int32, (TQ, LANES), 1)
        dcs_all = jnp.zeros((TQ, LANES), F32)
        for h in range(HEADS):
            hs = slice(h * HEAD_DIM, (h + 1) * HEAD_DIM)
            half = slice(h * LANES, h * LANES + HEAD_DIM)
            dk_ref[:, hs] = dkw[:, half]
            dv_ref[:, hs] = dvw[:, half]
            colsum = jnp.sum(dca[:, h * LANES:(h + 1) * LANES], axis=1, keepdims=True)
            dcs_all = jnp.where(lane == h, colsum, dcs_all)
        dcs_ref[...] = dcs_all

        @pl.when(j == nq - 1)
        def _():
            _chips_finish(b_ref, got_ref, send_sems, recv_sems)

    blk = pl.BlockSpec((TQ, D_ATTN), _row)
    _, r, cdim = chip_blocks.shape
    return pl.pallas_call(
        body, grid=(nq,), name="attn_bwd",
        out_shape=(jax.ShapeDtypeStruct((nq, D_ATTN, TQ), F32), jax.ShapeDtypeStruct((s, D_ATTN), F32),
                   jax.ShapeDtypeStruct((s, D_ATTN), F32), jax.ShapeDtypeStruct((s, LANES), F32),
                   jax.ShapeDtypeStruct((nq, HEADS, 8, TQ), F32),
                   jax.ShapeDtypeStruct(chip_blocks.shape, chip_blocks.dtype)),
        in_specs=[pl.BlockSpec((TQ, HEADS * AUG), _row), blk, pl.BlockSpec((1, D_ATTN, TQ), lambda j: (j, 0, 0)),
                  VMEM_WHOLE, VMEM_WHOLE, VMEM_WHOLE, VMEM_WHOLE, VMEM_WHOLE, VMEM_WHOLE, ANY],
        out_specs=(pl.BlockSpec((nq, D_ATTN, TQ), lambda j: (0, 0, 0)), blk, blk, pl.BlockSpec((TQ, LANES), _row),
                   pl.BlockSpec((nq, HEADS, 8, TQ), lambda j: (0, 0, 0, 0)), ANY),
        scratch_shapes=[pltpu.VMEM((TQ, wide), F32), pltpu.VMEM((TQ, wide), F32), pltpu.VMEM((TQ, wide), F32),
                        pltpu.VMEM((HEADS, TQ, TQ), F32), pltpu.VMEM((HEADS, TQ, TQ), BF16),
                        pltpu.VMEM((HEADS, TQ, TQ), BF16), pltpu.VMEM((r, cdim), chip_blocks.dtype),
                        pltpu.SemaphoreType.DMA((3,)), pltpu.SemaphoreType.DMA((3,)), pltpu.SemaphoreType.DMA],
        compiler_params=_params(1),
    )(ka, v, kt3, qat3, q, do, dot3, lset3, dlt3, chip_blocks)


def _pre_attn_bwd(dqt3, dk, dv, dcs, drs, fl, dy, x, dh1, g1, wqkv, wf, wu):
    s, d = x.shape
    nt = s // TS
    n = TS + HALO
    sub = TS // TQ
    qkv, fcols = 3 * D_ATTN, 3 * D_ATTN + LANES

    def body(dqt_ref, dk_ref, dv_ref, dcs_ref, drs_ref, fl_ref, dy_ref, x_ref, dh1_ref, g_ref, wqkv_ref, wf_ref, wu_ref,
             gx_ref, dz_ref, dg_ref, db_ref, ybuf, ccar, dlog):
        dqkv_ref = dz_ref.at[:, 0:qkv]
        dfb_ref = dz_ref.at[:, qkv:fcols]
        dub_ref = dz_ref.at[:, fcols:]
        i = pl.program_id(0)
        ti = nt - 1 - i

        @pl.when(i == 0)
        def _():
            ybuf[TS:n, :] = jnp.zeros((HALO, D_POOL), F32)
            ccar[...] = jnp.zeros_like(ccar)
            dg_ref[...] = jnp.zeros_like(dg_ref)
            db_ref[...] = jnp.zeros_like(db_ref)

        rr = lax.broadcasted_iota(jnp.int32, (TS, TS), 0)
        cc = lax.broadcasted_iota(jnp.int32, (TS, TS), 1)
        triu = (cc >= rr).astype(F32)
        dlog[...] = ccar[...] + jnp.dot(triu, drs_ref[...] - dcs_ref[...], precision=HIGHEST, preferred_element_type=F32)
        ccar[...] = dlog[0:1, :]
        df = dlog[...] * jax.nn.sigmoid(-fl_ref[...])
        db_ref[...] += jnp.sum(df, axis=0, keepdims=True)
        dfb = df.astype(BF16)
        dfb_ref[...] = dfb

        t = ti * TS + lax.broadcasted_iota(jnp.int32, (TS, 1), 0)
        dy = dy_ref[...]
        for g, w in enumerate(POOL_WINDOWS):
            cols = slice(g * POOL_CH, (g + 1) * POOL_CH)
            ybuf[0:TS, cols] = dy[:, cols] / jnp.minimum(t + 1, w).astype(F32)
        for g, w in enumerate(POOL_WINDOWS):
            cols = slice(g * POOL_CH, (g + 1) * POOL_CH)
            sm = ybuf[:, cols]
            step = 1
            while step < w:
                sm = sm + pltpu.roll(sm, n - step, 0)
                step *= 2
            dub_ref[:, cols] = (sm[0:TS, :] - dy[:, cols]).astype(BF16)
        ybuf[TS:n, :] = ybuf[0:HALO, :]

        for a in range(sub):
            dqkv_ref[a * TQ:(a + 1) * TQ, 0:D_ATTN] = (dqt_ref[a].T * 0.125).astype(BF16)
        dqkv_ref[:, D_ATTN:2 * D_ATTN] = dk_ref[...].astype(BF16)
        dqkv_ref[:, 2 * D_ATTN:] = dv_ref[...].astype(BF16)
        dhn = _nn(dqkv_ref[...], wqkv_ref[...]) + _nn(dfb, wf_ref[...]) + _nn(dub_ref[...], wu_ref[...])
        dx, dg = _rms_bwd(x_ref[...], g_ref[...], dhn)
        gx_ref[...] = dh1_ref[...] + dx
        dg_ref[...] += dg

    rev = lambda i: (nt - 1 - i, 0)
    blk = lambda w: pl.BlockSpec((TS, w), rev)
    return pl.pallas_call(
        body, grid=(nt,), name="pre_attn_bwd",
        out_shape=(jax.ShapeDtypeStruct((s, d), F32), jax.ShapeDtypeStruct((s, fcols + D_POOL), BF16),
                   jax.ShapeDtypeStruct((1, d), F32), jax.ShapeDtypeStruct((1, LANES), F32)),
        in_specs=[pl.BlockSpec((sub, D_ATTN, TQ), lambda i: (nt - 1 - i, 0, 0)),
                  blk(D_ATTN), blk(D_ATTN), blk(LANES), blk(LANES), blk(LANES), blk(D_POOL), blk(d), blk(d),
                  pl.BlockSpec((1, d), _fixed), pl.BlockSpec(wqkv.shape, _fixed), pl.BlockSpec(wf.shape, _fixed),
                  pl.BlockSpec(wu.shape, _fixed)],
        out_specs=(blk(d), blk(fcols + D_POOL), pl.BlockSpec((1, d), _fixed), pl.BlockSpec((1, LANES), _fixed)),
        scratch_shapes=[pltpu.VMEM((n, D_POOL), F32), pltpu.VMEM((1, LANES), F32), pltpu.VMEM((TS, LANES), F32)],
        compiler_params=_params(1),
    )(dqt3, dk, dv, dcs, drs, fl, dy, x, dh1, g1, wqkv, wf, wu)


def _wgrad(a, b, out_dtype, name):
    s, m = a.shape
    n = b.shape[1]
    tm = max(t for t in range(LANES, min(m, TM_WGRAD) + 1, LANES) if m % t == 0)
    ts = min(TS_WGRAD, s)
    ns = s // ts

    def body(a_ref, b_ref, o_ref, acc):
        i = pl.program_id(1)

        @pl.when(i == 0)
        def _():
            acc[...] = jnp.zeros_like(acc)

        acc[...] += _tn(a_ref[...], b_ref[pl.ds(i * ts, ts), :])

        @pl.when(i == ns - 1)
        def _():
            o_ref[...] = acc[...].astype(out_dtype)

    return pl.pallas_call(
        body, grid=(m // tm, ns), name=name, out_shape=jax.ShapeDtypeStruct((m, n), out_dtype),
        in_specs=[pl.BlockSpec((ts, tm), lambda j, i: (i, j)), VMEM_WHOLE],
        out_specs=pl.BlockSpec((tm, n), lambda j, i: (j, 0)),
        scratch_shapes=[pltpu.VMEM((tm, n), F32)],
        compiler_params=_params(2),
    )(a, b)


def _adamw(w, g, m, v):
    m = ADAM_B1 * m + (1.0 - ADAM_B1) * g
    v = ADAM_B2 * v + (1.0 - ADAM_B2) * (g * g)
    m_hat = m / (1.0 - ADAM_B1 ** ADAM_STEP)
    v_hat = v / (1.0 - ADAM_B2 ** ADAM_STEP)
    delta = -ADAM_LR * (m_hat / (jnp.sqrt(v_hat) + ADAM_EPS) + ADAM_WD * w)
    return delta, m, v


def _pair_sum(core, t, theirs, tr, name):
    nk, r, c = theirs.shape

    def body(core_ref, a_ref, b_ref, o_ref):
        o_ref[...] = (a_ref[...].astype(F32) + b_ref[...].astype(F32)).astype(BF16)

    blk = pl.BlockSpec((1, tr, c), lambda k, i, core_ref: (k, i, 0))
    return pl.pallas_call(
        body, name=name, out_shape=jax.ShapeDtypeStruct(theirs.shape, BF16),
        grid_spec=pltpu.PrefetchScalarGridSpec(
            num_scalar_prefetch=1, grid=(nk, r // tr),
            in_specs=[pl.BlockSpec((1, tr, c), lambda k, i, core_ref: (2 * k + core_ref[0], i, 0)), blk],
            out_specs=blk),
        compiler_params=_params(2),
    )(core, t, theirs)


def _sum_update(p_ref, w_ref, m_ref, v_ref, g_ref, d_ref, nm_ref, nv_ref):
    g = p_ref[0].astype(F32)
    for k in range(1, p_ref.shape[0]):
        g = g + p_ref[k].astype(F32)
    g_ref[...] = g
    d_ref[...], nm_ref[...], nv_ref[...] = _adamw(w_ref[...], g, m_ref[...], v_ref[...])


def _reduce_update_rest(parts, w, m, v, chip_blocks, small_block):
    nk, r, c = parts.shape
    ns = r // TR_REST

    def body(p_ref, w_ref, m_ref, v_ref, b_ref, sm_ref, g_ref, d_ref, nm_ref, nv_ref, got_ref, all_ref,
             stage_b, stage_s, send_b, recv_b, local_b, send_s, recv_s, local_s):
        i = pl.program_id(0)

        @pl.when(i == 0)
        def _():
            _chips_start(b_ref, got_ref, stage_b, send_b, recv_b, local_b)
            _gather_start(sm_ref, all_ref, stage_s, send_s, recv_s, local_s)

        _sum_update(p_ref, w_ref, m_ref, v_ref, g_ref, d_ref, nm_ref, nv_ref)

        @pl.when(i == ns - 1)
        def _():
            _gather_pass_on(all_ref, send_s, recv_s)
            _chips_finish(b_ref, got_ref, send_b, recv_b)
            _gather_finish(sm_ref, all_ref, send_s, recv_s)

    blk = pl.BlockSpec((TR_REST, c), _row)
    out = jax.ShapeDtypeStruct((r, c), F32)
    dma = pltpu.SemaphoreType.DMA
    return pl.pallas_call(
        body, grid=(ns,), name="reduce_update_rest",
        out_shape=(out,) * 4 + (jax.ShapeDtypeStruct(chip_blocks.shape, chip_blocks.dtype),
                                jax.ShapeDtypeStruct((N_DEV,) + small_block.shape, small_block.dtype)),
        in_specs=[pl.BlockSpec((nk, TR_REST, c), lambda i: (0, i, 0)), blk, blk, blk, ANY, ANY],
        out_specs=(blk,) * 4 + (ANY, ANY),
        scratch_shapes=[pltpu.VMEM(chip_blocks.shape[1:], chip_blocks.dtype), pltpu.VMEM(small_block.shape, small_block.dtype),
                        dma((3,)), dma((3,)), dma, dma((7,)), dma((7,)), dma],
        compiler_params=_params(1),
    )(parts, w, m, v, chip_blocks, small_block)


def _reduce_update_big(parts, w, m, v, tr, name):
    nk, r, c = parts.shape

    def body(p_ref, w_ref, m_ref, v_ref, g_ref, d_ref, nm_ref, nv_ref):
        _sum_update(p_ref, w_ref, m_ref, v_ref, g_ref, d_ref, nm_ref, nv_ref)

    blk = pl.BlockSpec((tr, c), _row)
    out = jax.ShapeDtypeStruct((r, c), F32)
    return pl.pallas_call(
        body, grid=(r // tr,), name=name, out_shape=(out,) * 4,
        in_specs=[pl.BlockSpec((nk, tr, c), lambda i: (0, i, 0)), blk, blk, blk],
        out_specs=(blk,) * 4, compiler_params=_params(1),
    )(parts, w, m, v)


def _reduce_update_small(parts, w, m, v):
    nd = parts.shape[0]

    def body(p_ref, w_ref, m_ref, v_ref, g_ref, d_ref, nm_ref, nv_ref):
        g = p_ref[0]
        for k in range(1, nd):
            g = g + p_ref[k]
        g_ref[...] = g
        d_ref[...], nm_ref[...], nv_ref[...] = _adamw(w_ref[...], g, m_ref[...], v_ref[...])

    out = jax.ShapeDtypeStruct(w.shape, F32)
    return pl.pallas_call(body, name="reduce_update_small", out_shape=(out,) * 4,
                          compiler_params=pltpu.CompilerParams(vmem_limit_bytes=VMEM_LIMIT))(parts, w, m, v)


MESH = pl.DeviceIdType.MESH


def _copy_through_vmem(src_hbm, dst_hbm, stage, sem):
    load = pltpu.make_async_copy(src_hbm, stage, sem)
    load.start()
    load.wait()
    store = pltpu.make_async_copy(stage, dst_hbm, sem)
    store.start()
    store.wait()


class _GatherPlan:
    def __init__(self, x_ref, out_ref, send_sems, recv_sems):
        x, y, c = lax.axis_index("x"), lax.axis_index("y"), lax.axis_index("c")
        self.me, self.sibling, self.c = (x, y, c), (x, y, 1 - c), c
        self.chips = [(1 - x, y), (x, 1 - y), (1 - x, 1 - y)]
        self.x_ref, self.out_ref, self.send_sems, self.recv_sems = x_ref, out_ref, send_sems, recv_sems

    def slot(self, px, py, pc):
        return self.out_ref.at[4 * px + 2 * py + pc]

    def copy(self, k, block, to, src=None):
        return pltpu.make_async_remote_copy(
            src_ref=self.slot(*block) if src is None else src, dst_ref=self.slot(*block),
            send_sem=self.send_sems.at[k], recv_sem=self.recv_sems.at[k], device_id=to, device_id_type=MESH)

    def first(self):
        return [self.copy(0, self.me, self.sibling, src=self.x_ref)] + [
            self.copy(1 + j, self.me, (*chip, self.c), src=self.x_ref) for j, chip in enumerate(self.chips)]

    def passed(self):
        return [self.copy(4 + j, (*chip, self.c), self.sibling) for j, chip in enumerate(self.chips)]


def _gather_start(x_ref, out_ref, stage, send_sems, recv_sems, local_sem):
    plan = _GatherPlan(x_ref, out_ref, send_sems, recv_sems)
    for cp in plan.first():
        cp.start()
    _copy_through_vmem(x_ref, plan.slot(*plan.me), stage, local_sem)


def _gather_pass_on(out_ref, send_sems, recv_sems):
    plan = _GatherPlan(None, out_ref, send_sems, recv_sems)
    passed = plan.passed()
    for j, chip in enumerate(plan.chips):
        plan.copy(1 + j, (*chip, plan.c), plan.me).wait_recv()
        passed[j].start()


def _gather_finish(x_ref, out_ref, send_sems, recv_sems):
    plan = _GatherPlan(x_ref, out_ref, send_sems, recv_sems)
    plan.copy(0, plan.sibling, plan.me).wait_recv()
    for j, chip in enumerate(plan.chips):
        plan.copy(4 + j, (*chip, 1 - plan.c), plan.me).wait_recv()
    for cp in plan.first() + plan.passed():
        cp.wait_send()


def _all_gather(xs, name):
    r, cdim = xs.shape

    def body(x_ref, out_ref, stage, send_sems, recv_sems, local_sem):
        _gather_start(x_ref, out_ref, stage, send_sems, recv_sems, local_sem)
        _gather_pass_on(out_ref, send_sems, recv_sems)
        _gather_finish(x_ref, out_ref, send_sems, recv_sems)

    return pl.pallas_call(
        body, name=name, out_shape=jax.ShapeDtypeStruct((N_DEV, r, cdim), xs.dtype),
        in_specs=[ANY], out_specs=ANY,
        scratch_shapes=[pltpu.VMEM((r, cdim), xs.dtype), pltpu.SemaphoreType.DMA((7,)), pltpu.SemaphoreType.DMA((7,)),
                        pltpu.SemaphoreType.DMA],
        compiler_params=pltpu.CompilerParams(vmem_limit_bytes=VMEM_LIMIT),
    )(xs)


def _rs_pair(t, name):
    _, r, cdim = t.shape

    def body(t_ref, theirs_ref, send_sems, recv_sems):
        x, y, c = lax.axis_index("x"), lax.axis_index("y"), lax.axis_index("c")
        remote = [pltpu.make_async_remote_copy(
            src_ref=t_ref.at[2 * k + (1 - c)], dst_ref=theirs_ref.at[k],
            send_sem=send_sems.at[k], recv_sem=recv_sems.at[k], device_id=(x, y, 1 - c), device_id_type=MESH)
            for k in range(4)]
        for cp in remote:
            cp.start()
        for cp in remote:
            cp.wait()

    return pl.pallas_call(
        body, name=name, out_shape=jax.ShapeDtypeStruct((4, r, cdim), t.dtype), in_specs=[ANY], out_specs=ANY,
        scratch_shapes=[pltpu.SemaphoreType.DMA((4,)), pltpu.SemaphoreType.DMA((4,))],
    )(t)


def _chips_start(b_ref, out_ref, stage, send_sems, recv_sems, local_sem):
    x, y, c = lax.axis_index("x"), lax.axis_index("y"), lax.axis_index("c")
    mychip = 2 * x + y
    for j, (px, py) in enumerate([(1 - x, y), (x, 1 - y), (1 - x, 1 - y)]):
        pltpu.make_async_remote_copy(
            src_ref=b_ref.at[2 * px + py], dst_ref=out_ref.at[mychip],
            send_sem=send_sems.at[j], recv_sem=recv_sems.at[j], device_id=(px, py, c), device_id_type=MESH).start()
    _copy_through_vmem(b_ref.at[mychip], out_ref.at[mychip], stage, local_sem)


def _chips_finish(b_ref, out_ref, send_sems, recv_sems):
    x, y, c = lax.axis_index("x"), lax.axis_index("y"), lax.axis_index("c")
    for j, (px, py) in enumerate([(1 - x, y), (x, 1 - y), (1 - x, 1 - y)]):
        pltpu.make_async_remote_copy(
            src_ref=b_ref.at[2 * px + py], dst_ref=out_ref.at[2 * px + py],
            send_sem=send_sems.at[j], recv_sem=recv_sems.at[j], device_id=(px, py, c), device_id_type=MESH).wait()


def _pad_rows(a, rows):
    return jnp.pad(a, ((0, rows - a.shape[0]), (0, 0)))


def _pack_in(w_in):
    return _pad_rows(w_in[0].T, ROWS_IN)


def _unpack_in(r):
    return r[0:SHARD_IN].T[None]


def _pack_rest(w_out, w_gate, w_up, w_down, w_ple, w_pg):
    return jnp.concatenate([w_out[0], w_gate[0].T, w_up[0].T, w_down[0], w_ple[0].T.reshape(32, D_MODEL), w_pg[0]],
                           axis=0)


def _unpack_rest(r):
    return (r[0:OFF_GATE][None], r[OFF_GATE:OFF_UP].T[None], r[OFF_UP:OFF_DOWN].T[None],
            r[OFF_DOWN:OFF_PLE][None], r[OFF_PLE:OFF_PG].reshape(128, D_PLE).T[None], r[OFF_PG:ROWS_REST][None])


def _full_rest(g):
    return (g[:, 0:OFF_GATE].reshape(D_MODEL, D_MODEL), g[:, OFF_GATE:OFF_UP].reshape(D_FF, D_MODEL),
            g[:, OFF_UP:OFF_DOWN].reshape(D_FF, D_MODEL), g[:, OFF_DOWN:OFF_PLE].reshape(D_FF, D_MODEL),
            g[:, OFF_PLE:OFF_PG].reshape(D_MODEL, D_PLE), g[:, OFF_PG:ROWS_REST].reshape(D_MODEL, D_MODEL))


def _pack_small(w_pool, g_mix_pre, g_mix_post, g_ffn_pre, g_ffn_post, g_ple, g_attn, g_pool, pool_scale, b_forget,
                loss=None):
    def row(vrow):
        return jnp.pad(vrow.reshape(1, -1), ((0, 0), (0, D_MODEL - vrow.size)))
    rows = [w_pool.reshape(64, D_MODEL), row(g_mix_pre), row(g_mix_post), row(g_ffn_pre), row(g_ffn_post), row(g_ple),
            row(g_attn), row(g_pool), row(pool_scale), row(b_forget),
            row(loss) if loss is not None else jnp.zeros((1, D_MODEL), F32)]
    return _pad_rows(jnp.concatenate(rows, axis=0), SMALL_ROWS)


def _unpack_small(r):
    return dict(
        w_pool=r[0:64].reshape(1, 4, POOL_CH, POOL_CH), g_mix_pre=r[ROW_G_MIX_PRE:ROW_G_MIX_PRE + 1],
        g_mix_post=r[ROW_G_MIX_POST:ROW_G_MIX_POST + 1], g_ffn_pre=r[ROW_G_FFN_PRE:ROW_G_FFN_PRE + 1],
        g_ffn_post=r[ROW_G_FFN_POST:ROW_G_FFN_POST + 1], g_ple=r[ROW_G_PLE:ROW_G_PLE + 1],
        g_attn_grp=r[ROW_G_ATTN:ROW_G_ATTN + 1, 0:D_ATTN], g_pool_grp=r[ROW_G_POOL:ROW_G_POOL + 1, 0:D_POOL],
        pool_scale=r[ROW_POOL_SCALE:ROW_POOL_SCALE + 1, 0:D_POOL], b_forget=r[ROW_B_FORGET:ROW_B_FORGET + 1, 0:HEADS])


def _step(x, p, tgt, small, in_w, in_m, in_v, rest_w, rest_m, rest_v):
    core = lax.axis_index("c").astype(jnp.int32).reshape(1)
    win_t = _all_gather(in_w.astype(BF16), "gather_w_in")[:, 0:SHARD_IN].reshape(D_IN, D_MODEL)
    wqkv = win_t[0:3 * D_ATTN]
    wf = _pad_rows(win_t[3 * D_ATTN:3 * D_ATTN + HEADS], LANES)
    wu = win_t[3 * D_ATTN + HEADS:]
    wpool = small["w_pool"].astype(BF16)
    bpad = jnp.pad(small["b_forget"], ((0, 0), (0, LANES - HEADS)))

    lay = _attn_layout_constants()
    hn, q, ka, v, qat3, vt3, kt3, fl, y, mpre = _pre_attn_fwd(x, small["g_mix_pre"], wqkv, wf, wu, bpad, wpool, lay)
    a, lset3, gathered = _attn_fwd(ka, qat3, vt3, rest_w.astype(BF16))
    wout, wg_t, wu_t, wd, wple_t, wpg = _full_rest(gathered)
    mix, o, h1, hn2 = _post_attn_fwd(a, mpre, x, small["g_attn_grp"], small["g_pool_grp"], small["pool_scale"], wout,
                                     small["g_mix_post"], small["g_ffn_pre"])
    gate, up, act, ff, h2 = _ffn_fwd(hn2, wg_t, wu_t, wd, h1, small["g_ffn_post"])
    dh2, dff, dgl, dpp, h2b, pb, loss8, dg_ple, dg_ffn_post = _tail_fwd_bwd(
        h2, p, tgt, ff, wple_t, wpg, small["g_ple"], small["g_ffn_post"])
    dgate, dup, dh1, dg_ffn_pre = _ffn_bwd(dff, gate, up, wd, wg_t, wu_t, h1, dh2, small["g_ffn_pre"])
    dob, dab, dat3, dlt3, dmpb, dy, dg_mix_post, dg_attn, dg_pool, dps = _post_attn_bwd(
        dh1, o, a, mpre, wout, wpool, small["g_mix_post"], small["g_attn_grp"], small["g_pool_grp"], small["pool_scale"])

    nd = N_DEV
    send_rest = jnp.concatenate([
        _wgrad(mix, dob, BF16, "wgrad_out").reshape(nd, 128, D_MODEL),
        _wgrad(dgate, hn2, BF16, "wgrad_gate").reshape(nd, 352, D_MODEL),
        _wgrad(dup, hn2, BF16, "wgrad_up").reshape(nd, 352, D_MODEL),
        _wgrad(act, dff, BF16, "wgrad_down").reshape(nd, 352, D_MODEL),
        _wgrad(dpp, pb, BF16, "wgrad_ple").reshape(nd, 32, D_MODEL),
        _wgrad(h2b, dgl, BF16, "wgrad_ple_gate").reshape(nd, 128, D_MODEL)], axis=1)
    pair_rest = _pair_sum(core, send_rest, _rs_pair(send_rest, "rs_pair_rest"), TR_REST, "rs_pair_sum_rest")

    dqt3, dk, dv, dcs, drs4, chips_rest = _attn_bwd(ka, v, kt3, qat3, q, dab, dat3, lset3, dlt3, pair_rest)
    drs = jnp.pad(drs4[:, :, 0, :].transpose(0, 2, 1).reshape(-1, HEADS), ((0, 0), (0, LANES - HEADS)))
    gx, dz, dg_mix_pre, db = _pre_attn_bwd(dqt3, dk, dv, dcs, drs, fl, dy, x, dh1, small["g_mix_pre"], wqkv, wf, wu)

    dwz = _wgrad(dz, hn, F32, "wgrad_in")
    dwin_t = jnp.concatenate([dwz[0:3 * D_ATTN], dwz[3 * D_ATTN:3 * D_ATTN + HEADS], dwz[3 * D_ATTN + LANES:]], axis=0)
    send_in = jnp.pad(dwin_t.reshape(nd, SHARD_IN, D_MODEL), ((0, 0), (0, ROWS_IN - SHARD_IN), (0, 0))).astype(BF16)
    pair_in = _pair_sum(core, send_in, _rs_pair(send_in, "rs_pair_in"), ROWS_IN, "rs_pair_sum_in")

    dwp = _wgrad(y, dmpb, F32, "wgrad_pool")
    dw_pool = jnp.stack([dwp[g * POOL_CH:(g + 1) * POOL_CH, g * POOL_CH:(g + 1) * POOL_CH] for g in range(4)])
    small_part = _pack_small(dw_pool, dg_mix_pre, dg_mix_post, dg_ffn_pre, dg_ffn_post, dg_ple, dg_attn, dg_pool, dps,
                             db[:, 0:HEADS], loss8[0:1, 0:1])

    *upd_rest, chips_in, small_all = _reduce_update_rest(chips_rest, rest_w, rest_m, rest_v, pair_in, small_part)
    upd_in = _reduce_update_big(chips_in, in_w, in_m, in_v, ROWS_IN, "reduce_update_in")
    return gx, small_all, upd_in, upd_rest


def kernel(x, p, g_mix_pre, w_in, b_forget, g_attn_grp, g_pool_grp, w_pool, pool_scale, w_out, g_mix_post, g_ffn_pre, w_ffn_gate, w_ffn_up, w_ffn_down, g_ffn_post, w_ple_proj, g_ple, w_ple_gate, loss_target, m_g_mix_pre, m_w_in, m_b_forget, m_g_attn_grp, m_g_pool_grp, m_w_pool, m_pool_scale, m_w_out, m_g_mix_post, m_g_ffn_pre, m_w_ffn_gate, m_w_ffn_up, m_w_ffn_down, m_g_ffn_post, m_w_ple_proj, m_g_ple, m_w_ple_gate, v_g_mix_pre, v_w_in, v_b_forget, v_g_attn_grp, v_g_pool_grp, v_w_pool, v_pool_scale, v_w_out, v_g_mix_post, v_g_ffn_pre, v_w_ffn_gate, v_w_ffn_up, v_w_ffn_down, v_g_ffn_post, v_w_ple_proj, v_g_ple, v_w_ple_gate):
    small = dict(w_pool=w_pool[0], g_mix_pre=g_mix_pre, g_mix_post=g_mix_post, g_ffn_pre=g_ffn_pre,
                 g_ffn_post=g_ffn_post, g_ple=g_ple, g_attn_grp=g_attn_grp, g_pool_grp=g_pool_grp,
                 pool_scale=pool_scale, b_forget=b_forget)
    gx, small_all, upd_in, upd_rest = _step(
        x[0], p[0, 0], loss_target[0], small, _pack_in(w_in), _pack_in(m_w_in), _pack_in(v_w_in),
        _pack_rest(w_out, w_ffn_gate, w_ffn_up, w_ffn_down, w_ple_proj, w_ple_gate),
        _pack_rest(m_w_out, m_w_ffn_gate, m_w_ffn_up, m_w_ffn_down, m_w_ple_proj, m_w_ple_gate),
        _pack_rest(v_w_out, v_w_ffn_gate, v_w_ffn_up, v_w_ffn_down, v_w_ple_proj, v_w_ple_gate))

    sm_w = _pack_small(w_pool, g_mix_pre, g_mix_post, g_ffn_pre, g_ffn_post, g_ple, g_attn_grp, g_pool_grp, pool_scale, b_forget)
    sm_m = _pack_small(m_w_pool, m_g_mix_pre, m_g_mix_post, m_g_ffn_pre, m_g_ffn_post, m_g_ple, m_g_attn_grp, m_g_pool_grp, m_pool_scale, m_b_forget)
    sm_v = _pack_small(v_w_pool, v_g_mix_pre, v_g_mix_post, v_g_ffn_pre, v_g_ffn_post, v_g_ple, v_g_attn_grp, v_g_pool_grp, v_pool_scale, v_b_forget)
    upd_small = _reduce_update_small(small_all, sm_w, sm_m, sm_v)
    loss = upd_small[0][ROW_LOSS, 0]

    def leaves(k):
        b_out, b_gate, b_up, b_down, b_ple, b_pg = _unpack_rest(upd_rest[k])
        s = _unpack_small(upd_small[k])
        return (s["g_mix_pre"], _unpack_in(upd_in[k]), s["b_forget"], s["g_attn_grp"], s["g_pool_grp"], s["w_pool"],
                s["pool_scale"], b_out, s["g_mix_post"], s["g_ffn_pre"], b_gate, b_up, b_down, s["g_ffn_post"], b_ple,
                s["g_ple"], b_pg)

    return (loss, gx[None], *leaves(0), *leaves(1), *leaves(2), *leaves(3))
```

```python
import functools

import jax
import jax.numpy as jnp
from jax import lax
from jax.experimental import pallas as pl
from jax.experimental.pallas import tpu as pltpu

F32 = jnp.float32
BF16 = jnp.bfloat16
HIGHEST = lax.Precision.HIGHEST

D_MODEL = 1024
HEADS = 8
HEAD_DIM = 64
D_ATTN = HEADS * HEAD_DIM
POOL_WINDOWS = (2, 4, 8, 16)
POOL_CH = 128
D_POOL = POOL_CH * len(POOL_WINDOWS)
D_FF = 2816
D_PLE = 256
D_IN = 3 * D_ATTN + HEADS + D_POOL
RMS_EPS = 1e-6
N_DEV = 8

ADAM_LR = 0.001
ADAM_B1 = 0.9
ADAM_B2 = 0.999
ADAM_EPS = 1e-08
ADAM_WD = 0.01
ADAM_STEP = 10

LANES = 128
HALO = 16
TS = 512
TS_FF = 512
TS_WGRAD = 1024
TM_WGRAD = 2176
TQ = 256
TN_FF = 1408
NEG = -1e30
VMEM_LIMIT = 56 * 1024 * 1024

SHARD_IN = 257
ROWS_IN = 272
OFF_GATE = 128
OFF_UP = OFF_GATE + 352
OFF_DOWN = OFF_UP + 352
OFF_PLE = OFF_DOWN + 352
OFF_PG = OFF_PLE + 32
ROWS_REST = OFF_PG + 128
TR_REST = 192

SMALL_ROWS = 80
ROW_G_MIX_PRE, ROW_G_MIX_POST, ROW_G_FFN_PRE, ROW_G_FFN_POST, ROW_G_PLE = 64, 65, 66, 67, 68
ROW_G_ATTN, ROW_G_POOL, ROW_POOL_SCALE, ROW_B_FORGET, ROW_LOSS = 69, 70, 71, 72, 73


def _nn(a, b):
    return jnp.dot(a, b, preferred_element_type=F32)


def _nt(a, b):
    return lax.dot_general(a, b, (((1,), (1,)), ((), ())), preferred_element_type=F32)


def _tn(a, b):
    return lax.dot_general(a, b, (((0,), (0,)), ((), ())), preferred_element_type=F32)


def _rstd(v):
    return lax.rsqrt(jnp.mean(v * v, axis=-1, keepdims=True) + RMS_EPS)


def _rms_bwd(v, g, dy):
    r = _rstd(v)
    vh = v * r
    t = dy * g
    dv = r * (t - vh * jnp.mean(t * vh, axis=-1, keepdims=True))
    return dv, jnp.sum(dy * vh, axis=0, keepdims=True)


def _params(n_grid):
    return pltpu.CompilerParams(dimension_semantics=("arbitrary",) * n_grid, vmem_limit_bytes=VMEM_LIMIT)


def _row(i):
    return (i, 0)


def _fixed(*_):
    return (0, 0)


assert TS == 2 * TQ
_HALVES = (slice(0, TQ), slice(TQ, TS))

VMEM_WHOLE = pl.BlockSpec(memory_space=pltpu.VMEM)
SMEM_WHOLE = pl.BlockSpec(memory_space=pltpu.SMEM)
ANY = pl.BlockSpec(memory_space=pl.ANY)


AUG = 128
BIAS_LANE = HEAD_DIM
ONE_LANE = HEAD_DIM + 3
SPARE_LANE = HEADS


def _attn_layout_constants():
    import numpy as np
    place = np.zeros((D_ATTN, HEADS * AUG), np.float32)
    for r in range(D_ATTN):
        place[r, (r // HEAD_DIM) * AUG + r % HEAD_DIM] = 1.0
    bias_k = np.zeros((3, LANES, HEADS * AUG), np.float32)
    bias_q = np.zeros((3, LANES, HEADS * AUG), np.float32)
    for h in range(HEADS):
        for part in range(3):
            bias_k[part, h, h * AUG + BIAS_LANE + part] = -1.0
            bias_q[part, h, h * AUG + ONE_LANE + part] = 1.0
            bias_k[0, SPARE_LANE, h * AUG + ONE_LANE + part] = 1.0
            bias_q[0, SPARE_LANE, h * AUG + BIAS_LANE + part] = 1.0
    as_bf = lambda a: jnp.asarray(a, BF16)
    return dict(place=as_bf(place), place_t=as_bf(place.T), bias_k=as_bf(bias_k),
                bias_q_t=as_bf(bias_q.transpose(0, 2, 1)))


def _pre_attn_fwd(x, g1, wqkv, wf, wu, bpad, wpool, lay):
    s, d = x.shape
    nt = s // TS
    sub = TS // TQ

    def body(x_ref, g_ref, wqkv_ref, wf_ref, wu_ref, b_ref, wp_ref, place_ref, place_t_ref, bk_ref, bqt_ref,
             hn_ref, q_ref, ka_ref, v_ref, qat_ref, vt_ref, kt_ref, fl_ref, y_ref, mp_ref, ubuf, ccar, cbuf):
        i = pl.program_id(0)

        @pl.when(i == 0)
        def _():
            ubuf[0:HALO, :] = jnp.zeros((HALO, D_POOL), F32)
            ccar[...] = jnp.zeros_like(ccar)

        xv = x_ref[...]
        hn = (xv * _rstd(xv) * g_ref[...]).astype(BF16)
        hn_ref[...] = hn
        zq = _nt(hn, wqkv_ref[...])
        qb = (zq[:, 0:D_ATTN] * 0.125).astype(BF16)
        kb = zq[:, D_ATTN:2 * D_ATTN].astype(BF16)
        vb = zq[:, 2 * D_ATTN:3 * D_ATTN].astype(BF16)
        q_ref[...] = qb
        v_ref[...] = vb

        fl = _nt(hn, wf_ref[...]) + b_ref[...]
        fl_ref[...] = fl
        logf = jax.nn.log_sigmoid(fl)
        rr = lax.broadcasted_iota(jnp.int32, (TS, TS), 0)
        cc = lax.broadcasted_iota(jnp.int32, (TS, TS), 1)
        tril = (cc <= rr).astype(F32)
        c = jnp.dot(tril, logf, precision=HIGHEST, preferred_element_type=F32) + ccar[...]
        cbuf[...] = c
        ccar[...] = cbuf[TS - 1:TS, :]
        hi = c.astype(BF16)
        rest = c - hi.astype(F32)
        mid = rest.astype(BF16)
        lo = (rest - mid.astype(F32)).astype(BF16)
        lane = lax.broadcasted_iota(jnp.int32, (TS, LANES), 1)
        parts = (jnp.where(lane == SPARE_LANE, 1.0, hi).astype(BF16), mid, lo)
        ka = _nn(kb, place_ref[...])
        qat = _nt(place_t_ref[...], qb)
        for part in range(3):
            ka = ka + _nn(parts[part], bk_ref[part])
            qat = qat + _nt(bqt_ref[part], parts[part])
        ka_ref[...] = ka.astype(BF16)
        qat = qat.astype(BF16)
        vt = vb.T
        kt = kb.T
        for a in range(sub):
            qat_ref[a] = qat[:, a * TQ:(a + 1) * TQ]
            vt_ref[a] = vt[:, a * TQ:(a + 1) * TQ]
            kt_ref[a] = kt[:, a * TQ:(a + 1) * TQ]

        u = _nt(hn, wu_ref[...])
        ubuf[HALO:HALO + TS, :] = u
        t = i * TS + lax.broadcasted_iota(jnp.int32, (TS, 1), 0)
        for g, w in enumerate(POOL_WINDOWS):
            cols = slice(g * POOL_CH, (g + 1) * POOL_CH)
            sm = ubuf[:, cols]
            step = 1
            while step < w:
                sm = sm + pltpu.roll(sm, step, 0)
                step *= 2
            cnt = jnp.minimum(t + 1, w).astype(F32)
            yg = (sm[HALO:, :] / cnt - u[:, cols]).astype(BF16)
            y_ref[:, cols] = yg
            mp_ref[:, cols] = _nn(yg, wp_ref[g])
        ubuf[0:HALO, :] = u[TS - HALO:, :]

    nq = s // TQ
    aug = HEADS * AUG
    outs = (
        jax.ShapeDtypeStruct((s, d), BF16), jax.ShapeDtypeStruct((s, D_ATTN), BF16),
        jax.ShapeDtypeStruct((s, aug), BF16), jax.ShapeDtypeStruct((s, D_ATTN), BF16),
        jax.ShapeDtypeStruct((nq, aug, TQ), BF16), jax.ShapeDtypeStruct((nq, D_ATTN, TQ), BF16),
        jax.ShapeDtypeStruct((nq, D_ATTN, TQ), BF16),
        jax.ShapeDtypeStruct((s, LANES), F32),
        jax.ShapeDtypeStruct((s, D_POOL), BF16), jax.ShapeDtypeStruct((s, D_POOL), F32),
    )
    fixed3 = lambda i: (0, 0, 0)
    tiles3 = lambda rows: pl.BlockSpec((sub, rows, TQ), lambda i: (i, 0, 0))
    return pl.pallas_call(
        body, grid=(nt,), out_shape=outs, name="pre_attn_fwd",
        in_specs=[pl.BlockSpec((TS, d), _row), pl.BlockSpec((1, d), _fixed),
                  pl.BlockSpec(wqkv.shape, _fixed), pl.BlockSpec(wf.shape, _fixed), pl.BlockSpec(wu.shape, _fixed),
                  pl.BlockSpec((1, LANES), _fixed), pl.BlockSpec(wpool.shape, fixed3),
                  pl.BlockSpec(lay["place"].shape, _fixed), pl.BlockSpec(lay["place_t"].shape, _fixed),
                  pl.BlockSpec(lay["bias_k"].shape, fixed3), pl.BlockSpec(lay["bias_q_t"].shape, fixed3)],
        out_specs=(pl.BlockSpec((TS, d), _row), pl.BlockSpec((TS, D_ATTN), _row),
                   pl.BlockSpec((TS, aug), _row), pl.BlockSpec((TS, D_ATTN), _row),
                   tiles3(aug), tiles3(D_ATTN), tiles3(D_ATTN),
                   pl.BlockSpec((TS, LANES), _row),
                   pl.BlockSpec((TS, D_POOL), _row), pl.BlockSpec((TS, D_POOL), _row)),
        scratch_shapes=[pltpu.VMEM((TS + HALO, D_POOL), F32), pltpu.VMEM((1, LANES), F32), pltpu.VMEM((TS, LANES), F32)],
        compiler_params=_params(1),
    )(x, g1, wqkv, wf, wu, bpad, wpool, lay["place"], lay["place_t"], lay["bias_k"], lay["bias_q_t"])


def _causal_in_tile():
    krow = lax.broadcasted_iota(jnp.int32, (TQ, TQ), 0)
    qcol = lax.broadcasted_iota(jnp.int32, (TQ, TQ), 1)
    return krow <= qcol


def _attn_fwd(ka, qat3, vt3, own_block):
    s = ka.shape[0]
    nq = s // TQ
    pass_on_step = max(nq - 2, 0)

    def body(qa_ref, ka_ref, vt_ref, own_ref, a_ref, lset_ref, all_ref, acc, st_scr, pt_scr,
             stage, send_sems, recv_sems, local_sem):
        i = pl.program_id(0)

        @pl.when(i == 0)
        def _():
            _gather_start(own_ref, all_ref, stage, send_sems, recv_sems, local_sem)

        @pl.when(i == pass_on_step)
        def _():
            _gather_pass_on(all_ref, send_sems, recv_sems)

        acc[...] = jnp.zeros_like(acc)

        def tile(j, stats, masked):
            tile_max = []
            for h in range(HEADS):
                aug = slice(h * AUG, (h + 1) * AUG)
                st = _nn(ka_ref[pl.ds(j * TQ, TQ), aug], qa_ref[0, aug, :])
                if masked:
                    st = jnp.where(_causal_in_tile(), st, NEG)
                st_scr[h] = st
                tile_max.append(jnp.max(st, axis=0, keepdims=True))
            new, scale = [], []
            for h in range(HEADS):
                m_old, l_old = stats[h]
                m_new = jnp.maximum(m_old, tile_max[h])
                al = jnp.exp(m_old - m_new)
                pt = jnp.exp(st_scr[h] - m_new)
                pt_scr[h] = pt.astype(BF16)
                new.append((m_new, al * l_old + jnp.sum(pt, axis=0, keepdims=True)))
                scale.append(al)
            for h in range(HEADS):
                rows = slice(h * HEAD_DIM, (h + 1) * HEAD_DIM)
                acc[rows, :] = scale[h] * acc[rows, :] + _nn(vt_ref[j, rows, :], pt_scr[h])
            return tuple(new)

        init = tuple((jnp.full((1, TQ), NEG, F32), jnp.zeros((1, TQ), F32)) for _ in range(HEADS))
        stats = lax.fori_loop(0, i, functools.partial(tile, masked=False), init)
        stats = tile(i, stats, True)
        for h in range(HEADS):
            rows = slice(h * HEAD_DIM, (h + 1) * HEAD_DIM)
            acc[rows, :] = acc[rows, :] / stats[h][1]
            lset_ref[0, h:h + 1, :] = stats[h][0] + jnp.log(stats[h][1])
        a_ref[...] = acc[...].T

        @pl.when(i == nq - 1)
        def _():
            _gather_finish(own_ref, all_ref, send_sems, recv_sems)

    r, cdim = own_block.shape
    return pl.pallas_call(
        body, grid=(nq,), name="attn_fwd",
        out_shape=(jax.ShapeDtypeStruct((s, D_ATTN), F32), jax.ShapeDtypeStruct((nq, HEADS, TQ), F32),
                   jax.ShapeDtypeStruct((N_DEV, r, cdim), own_block.dtype)),
        in_specs=[pl.BlockSpec((1, HEADS * AUG, TQ), lambda i: (i, 0, 0)), VMEM_WHOLE, VMEM_WHOLE, ANY],
        out_specs=(pl.BlockSpec((TQ, D_ATTN), _row), pl.BlockSpec((1, HEADS, TQ), lambda i: (i, 0, 0)), ANY),
        scratch_shapes=[pltpu.VMEM((D_ATTN, TQ), F32), pltpu.VMEM((HEADS, TQ, TQ), F32), pltpu.VMEM((HEADS, TQ, TQ), BF16),
                        pltpu.VMEM((r, cdim), own_block.dtype),
                        pltpu.SemaphoreType.DMA((7,)), pltpu.SemaphoreType.DMA((7,)), pltpu.SemaphoreType.DMA],
        compiler_params=_params(1),
    )(qat3, ka, vt3, own_block)


def _post_attn_fwd(a, mpre, x, g_attn, g_pool, pscale, wout, g_post, g_ffn_pre):
    s, d = x.shape

    def body(a_ref, mp_ref, x_ref, ga_ref, gp_ref, ps_ref, wo_ref, gpost_ref, gpre_ref,
             mix_ref, o_ref, h1_ref, hn2_ref):
        for rows in _HALVES:
            av = a_ref[rows, :]
            mix_ref[rows, 0:D_ATTN] = (av * _rstd(av) * ga_ref[...]).astype(BF16)
            mv = mp_ref[rows, :] * ps_ref[...]
            mix_ref[rows, D_ATTN:] = (mv * _rstd(mv) * gp_ref[...]).astype(BF16)
            o = _nn(mix_ref[rows, :], wo_ref[...])
            o_ref[rows, :] = o
            h1 = x_ref[rows, :] + o * _rstd(o) * gpost_ref[...]
            h1_ref[rows, :] = h1
            hn2_ref[rows, :] = (h1 * _rstd(h1) * gpre_ref[...]).astype(BF16)

    vec = lambda n: pl.BlockSpec((1, n), _fixed)
    return pl.pallas_call(
        body, grid=(s // TS,), name="post_attn_fwd",
        out_shape=(jax.ShapeDtypeStruct((s, d), BF16), jax.ShapeDtypeStruct((s, d), F32),
                   jax.ShapeDtypeStruct((s, d), F32), jax.ShapeDtypeStruct((s, d), BF16)),
        in_specs=[pl.BlockSpec((TS, D_ATTN), _row), pl.BlockSpec((TS, D_POOL), _row), pl.BlockSpec((TS, d), _row),
                  vec(D_ATTN), vec(D_POOL), vec(D_POOL), pl.BlockSpec(wout.shape, _fixed), vec(d), vec(d)],
        out_specs=(pl.BlockSpec((TS, d), _row),) * 4,
        compiler_params=_params(1),
    )(a, mpre, x, g_attn, g_pool, pscale, wout, g_post, g_ffn_pre)


def _ffn_fwd(hn2, wg, wu, wd, h1, g_post):
    s, d = h1.shape
    nc = D_FF // TN_FF
    ts = min(TS_FF, s)

    def body(hn_ref, wg_ref, wu_ref, wd_ref, h1_ref, g_ref, gate_ref, up_ref, act_ref, ff_ref, h2_ref, acc):
        j = pl.program_id(1)

        @pl.when(j == 0)
        def _():
            acc[...] = jnp.zeros_like(acc)

        for r in range(2):
            rows = slice(r * (ts // 2), (r + 1) * (ts // 2))
            hn = hn_ref[rows, :]
            gt = _nt(hn, wg_ref[...])
            up = _nt(hn, wu_ref[...])
            gate_ref[rows, :] = gt.astype(BF16)
            up_ref[rows, :] = up.astype(BF16)
            act_ref[rows, :] = (gt * jax.nn.sigmoid(gt) * up).astype(BF16)
            acc[rows, :] += _nn(act_ref[rows, :], wd_ref[...])

        @pl.when(j == nc - 1)
        def _():
            ff = acc[...]
            ff_ref[...] = ff
            h2_ref[...] = h1_ref[...] + ff * _rstd(ff) * g_ref[...]

    rowblk = pl.BlockSpec((ts, d), lambda i, j: (i, 0))
    wblk = pl.BlockSpec((TN_FF, d), lambda i, j: (j, 0))
    chunk = pl.BlockSpec((ts, TN_FF), lambda i, j: (i, j))
    return pl.pallas_call(
        body, grid=(s // ts, nc), name="ffn_fwd",
        out_shape=(jax.ShapeDtypeStruct((s, D_FF), BF16),) * 3 + (jax.ShapeDtypeStruct((s, d), F32),) * 2,
        in_specs=[rowblk, wblk, wblk, wblk, rowblk, pl.BlockSpec((1, d), lambda i, j: (0, 0))],
        out_specs=(chunk, chunk, chunk, rowblk, rowblk),
        scratch_shapes=[pltpu.VMEM((ts, d), F32)],
        compiler_params=_params(2),
    )(hn2, wg, wu, wd, h1, g_post)


def _tail_fwd_bwd(h2, p, tgt, ff, wple, wpg, g_ple, g_ffn_post):
    s, d = h2.shape

    def body(h2_ref, p_ref, t_ref, ff_ref, wple_ref, wpg_ref, gple_ref, gfp_ref,
             dh2_ref, dff_ref, dgl_ref, dpp_ref, h2b_ref, pb_ref, loss_ref, dgple_ref, dgfp_ref):
        i = pl.program_id(0)

        @pl.when(i == 0)
        def _():
            loss_ref[...] = jnp.zeros_like(loss_ref)
            dgple_ref[...] = jnp.zeros_like(dgple_ref)
            dgfp_ref[...] = jnp.zeros_like(dgfp_ref)

        h2 = h2_ref[...]
        h2b = h2.astype(BF16)
        h2b_ref[...] = h2b
        pb = p_ref[...].astype(BF16)
        pb_ref[...] = pb
        pp = _nt(pb, wple_ref[...])
        gple = gple_ref[...]
        e = pp * _rstd(pp) * gple
        sg = jax.nn.sigmoid(_nn(h2b, wpg_ref[...]))
        diff = h2 + sg * e - t_ref[...]
        sq = jnp.sum(jnp.sum(diff * diff, axis=1, keepdims=True), axis=0, keepdims=True)
        loss_ref[...] += jnp.broadcast_to(sq * (0.5 / d), loss_ref.shape)
        dh3 = diff * (1.0 / d)
        dgl = (dh3 * e * sg * (1.0 - sg)).astype(BF16)
        dgl_ref[...] = dgl
        dh2 = dh3 + _nt(dgl, wpg_ref[...])
        dh2_ref[...] = dh2
        dpp, dg = _rms_bwd(pp, gple, dh3 * sg)
        dpp_ref[...] = dpp.astype(BF16)
        dgple_ref[...] += dg
        dff, dg = _rms_bwd(ff_ref[...], gfp_ref[...], dh2)
        dff_ref[...] = dff.astype(BF16)
        dgfp_ref[...] += dg

    rowblk = pl.BlockSpec((TS, d), _row)
    vec = pl.BlockSpec((1, d), _fixed)
    return pl.pallas_call(
        body, grid=(s // TS,), name="tail_fwd_bwd",
        out_shape=(jax.ShapeDtypeStruct((s, d), F32), jax.ShapeDtypeStruct((s, d), BF16),
                   jax.ShapeDtypeStruct((s, d), BF16), jax.ShapeDtypeStruct((s, d), BF16),
                   jax.ShapeDtypeStruct((s, d), BF16), jax.ShapeDtypeStruct((s, D_PLE), BF16),
                   jax.ShapeDtypeStruct((8, LANES), F32), jax.ShapeDtypeStruct((1, d), F32),
                   jax.ShapeDtypeStruct((1, d), F32)),
        in_specs=[rowblk, pl.BlockSpec((TS, D_PLE), _row), rowblk, rowblk,
                  pl.BlockSpec(wple.shape, _fixed), pl.BlockSpec(wpg.shape, _fixed), vec, vec],
        out_specs=(rowblk, rowblk, rowblk, rowblk, rowblk, pl.BlockSpec((TS, D_PLE), _row),
                   pl.BlockSpec((8, LANES), _fixed), vec, vec),
        compiler_params=_params(1),
    )(h2, p, tgt, ff, wple, wpg, g_ple, g_ffn_post)


def _ffn_bwd(dff, gate, up, wd, wg, wu, h1, dh2, g_pre):
    s, d = h1.shape
    nc = D_FF // TN_FF
    ts = min(TS_FF, s)

    def body(dff_ref, gate_ref, up_ref, wd_ref, wg_ref, wu_ref, h1_ref, dh2_ref, g_ref,
             dgate_ref, dup_ref, dh1_ref, dg_ref, acc):
        i = pl.program_id(0)
        j = pl.program_id(1)

        @pl.when((i == 0) & (j == 0))
        def _():
            dg_ref[...] = jnp.zeros_like(dg_ref)

        @pl.when(j == 0)
        def _():
            acc[...] = jnp.zeros_like(acc)

        for r in range(2):
            rows = slice(r * (ts // 2), (r + 1) * (ts // 2))
            dact = _nt(dff_ref[rows, :], wd_ref[...])
            gt = gate_ref[rows, :].astype(F32)
            sg = jax.nn.sigmoid(gt)
            dup_ref[rows, :] = (dact * gt * sg).astype(BF16)
            dgate_ref[rows, :] = (dact * up_ref[rows, :].astype(F32) * (sg * (1.0 + gt * (1.0 - sg)))).astype(BF16)
            acc[rows, :] += _nn(dgate_ref[rows, :], wg_ref[...]) + _nn(dup_ref[rows, :], wu_ref[...])

        @pl.when(j == nc - 1)
        def _():
            dv, dg = _rms_bwd(h1_ref[...], g_ref[...], acc[...])
            dh1_ref[...] = dh2_ref[...] + dv
            dg_ref[...] += dg

    rowblk = pl.BlockSpec((ts, d), lambda i, j: (i, 0))
    wblk = pl.BlockSpec((TN_FF, d), lambda i, j: (j, 0))
    chunk = pl.BlockSpec((ts, TN_FF), lambda i, j: (i, j))
    vec = pl.BlockSpec((1, d), lambda i, j: (0, 0))
    return pl.pallas_call(
        body, grid=(s // ts, nc), name="ffn_bwd",
        out_shape=(jax.ShapeDtypeStruct((s, D_FF), BF16), jax.ShapeDtypeStruct((s, D_FF), BF16),
                   jax.ShapeDtypeStruct((s, d), F32), jax.ShapeDtypeStruct((1, d), F32)),
        in_specs=[rowblk, chunk, chunk, wblk, wblk, wblk, rowblk, rowblk, vec],
        out_specs=(chunk, chunk, rowblk, vec),
        scratch_shapes=[pltpu.VMEM((ts, d), F32)],
        compiler_params=_params(2),
    )(dff, gate, up, wd, wg, wu, h1, dh2, g_pre)


def _post_attn_bwd(dh1, o, a, mpre, wout, wpool, g_post, g_attn, g_pool, pscale):
    s, d = dh1.shape
    sub = TS // TQ

    def body(dh1_ref, o_ref, a_ref, mp_ref, wo_ref, wp_ref, gpost_ref, ga_ref, gp_ref, ps_ref,
             dob_ref, dab_ref, dat_ref, dlt_ref, dmpb_ref, dy_ref, dgpost_ref, dga_ref, dgp_ref, dps_ref):
        i = pl.program_id(0)

        @pl.when(i == 0)
        def _():
            dgpost_ref[...] = jnp.zeros_like(dgpost_ref)
            dga_ref[...] = jnp.zeros_like(dga_ref)
            dgp_ref[...] = jnp.zeros_like(dgp_ref)
            dps_ref[...] = jnp.zeros_like(dps_ref)

        do, dg = _rms_bwd(o_ref[...], gpost_ref[...], dh1_ref[...])
        dgpost_ref[...] += dg
        dob = do.astype(BF16)
        dob_ref[...] = dob
        dmix = _nt(dob, wo_ref[...])

        av = a_ref[...]
        da, dg = _rms_bwd(av, ga_ref[...], dmix[:, 0:D_ATTN])
        dga_ref[...] += dg
        dab = da.astype(BF16)
        dab_ref[...] = dab
        dat = dab.T
        hsel = (lax.shift_right_logical(lax.broadcasted_iota(jnp.int32, (HEADS, D_ATTN), 1), 6)
                == lax.broadcasted_iota(jnp.int32, (HEADS, D_ATTN), 0)).astype(F32)
        dlt = lax.dot_general(hsel, da * av, (((1,), (1,)), ((), ())), precision=HIGHEST, preferred_element_type=F32)
        for q in range(sub):
            dlt_ref[q] = dlt[:, q * TQ:(q + 1) * TQ]
            dat_ref[q] = dat[:, q * TQ:(q + 1) * TQ]

        ps = ps_ref[...]
        mp = mp_ref[...]
        dm, dg = _rms_bwd(mp * ps, gp_ref[...], dmix[:, D_ATTN:])
        dgp_ref[...] += dg
        dps_ref[...] += jnp.sum(dm * mp, axis=0, keepdims=True)
        dmpb = (dm * ps).astype(BF16)
        dmpb_ref[...] = dmpb
        for g in range(len(POOL_WINDOWS)):
            cols = slice(g * POOL_CH, (g + 1) * POOL_CH)
            dy_ref[:, cols] = _nt(dmpb[:, cols], wp_ref[g])

    rowblk = pl.BlockSpec((TS, d), _row)
    half = pl.BlockSpec((TS, D_ATTN), _row)
    vec = lambda n: pl.BlockSpec((1, n), _fixed)
    return pl.pallas_call(
        body, grid=(s // TS,), name="post_attn_bwd",
        out_shape=(jax.ShapeDtypeStruct((s, d), BF16), jax.ShapeDtypeStruct((s, D_ATTN), BF16),
                   jax.ShapeDtypeStruct((s // TQ, D_ATTN, TQ), BF16),
                   jax.ShapeDtypeStruct((s // TQ, HEADS, TQ), F32), jax.ShapeDtypeStruct((s, D_POOL), BF16),
                   jax.ShapeDtypeStruct((s, D_POOL), F32), jax.ShapeDtypeStruct((1, d), F32),
                   jax.ShapeDtypeStruct((1, D_ATTN), F32), jax.ShapeDtypeStruct((1, D_POOL), F32),
                   jax.ShapeDtypeStruct((1, D_POOL), F32)),
        in_specs=[rowblk, rowblk, half, half, pl.BlockSpec(wout.shape, _fixed),
                  pl.BlockSpec(wpool.shape, lambda i: (0, 0, 0)), vec(d), vec(D_ATTN), vec(D_POOL), vec(D_POOL)],
        out_specs=(rowblk, half, pl.BlockSpec((sub, D_ATTN, TQ), lambda i: (i, 0, 0)),
                   pl.BlockSpec((sub, HEADS, TQ), lambda i: (i, 0, 0)), half, half,
                   vec(d), vec(D_ATTN), vec(D_POOL), vec(D_POOL)),
        compiler_params=_params(1),
    )(dh1, o, a, mpre, wout, wpool, g_post, g_attn, g_pool, pscale)


def _attn_bwd(ka, v, kt3, qat3, q, do, dot3, lset3, dlt3, chip_blocks):
    s = q.shape[0]
    nq = s // TQ
    wide = HEADS * LANES

    def body(ka_ref, v_ref, kt_ref, qat_ref, q_ref, do_ref, dot_ref, lset_ref, dlt_ref, b_ref,
             dqt_ref, dk_ref, dv_ref, dcs_ref, drs_ref, got_ref, dca, dkw, dvw, pt_scr, ptb_scr, dsb_scr,
             stage, send_sems, recv_sems, local_sem):
        j = pl.program_id(0)

        @pl.when(j == 0)
        def _():
            _chips_start(b_ref, got_ref, stage, send_sems, recv_sems, local_sem)
            dqt_ref[...] = jnp.zeros_like(dqt_ref)
            drs_ref[...] = jnp.zeros_like(drs_ref)

        dkw[...] = jnp.zeros_like(dkw)
        dvw[...] = jnp.zeros_like(dvw)
        dca[...] = jnp.zeros_like(dca)

        def tile(i, masked):
            rows = pl.ds(i * TQ, TQ)
            for h in range(HEADS):
                aug = slice(h * AUG, (h + 1) * AUG)
                st = _nn(ka_ref[:, aug], qat_ref[i, aug, :]) - lset_ref[i, h:h + 1, :]
                if masked:
                    st = jnp.where(_causal_in_tile(), st, NEG)
                pt = jnp.exp(st)
                pt_scr[h] = pt
                ptb_scr[h] = pt.astype(BF16)
            for h in range(HEADS):
                hs = slice(h * HEAD_DIM, (h + 1) * HEAD_DIM)
                half = slice(h * LANES, h * LANES + HEAD_DIM)
                dvw[:, half] += _nn(ptb_scr[h], do_ref[rows, hs])
                dst = pt_scr[h] * (_nn(v_ref[:, hs], dot_ref[i, hs, :]) - dlt_ref[i, h:h + 1, :])
                dsb_scr[h] = dst.astype(BF16)
                drs_ref[i, h, 0:1, :] += jnp.sum(dst, axis=0, keepdims=True)
                dca[:, h * LANES:(h + 1) * LANES] += dst[:, 0:LANES] + dst[:, LANES:2 * LANES]
            for h in range(HEADS):
                hs = slice(h * HEAD_DIM, (h + 1) * HEAD_DIM)
                half = slice(h * LANES, h * LANES + HEAD_DIM)
                dkw[:, half] += _nn(dsb_scr[h], q_ref[rows, hs])
                dqt_ref[i, hs, :] += _nn(kt_ref[0, hs, :], dsb_scr[h])

        def step(i, carry):
            tile(i, False)
            return carry

        tile(j, True)
        lax.fori_loop(j + 1, nq, step, 0)
        lane = lax.broadcasted_iota(jnp.int32, (TQ, LANES), 1)
        dcs_all = jnp.zeros((TQ, LANES), F32)
        for h in range(HEADS):
            hs = slice(h * HEAD_DIM, (h + 1) * HEAD_DIM)
            half = slice(h * LANES, h * LANES + HEAD_DIM)
            dk_ref[:, hs] = dkw[:, half]
            dv_ref[:, hs] = dvw[:, half]
            colsum = jnp.sum(dca[:, h * LANES:(h + 1) * LANES], axis=1, keepdims=True)
            dcs_all = jnp.where(lane == h, colsum, dcs_all)
        dcs_ref[...] = dcs_all

        @pl.when(j == nq - 1)
        def _():
            _chips_finish(b_ref, got_ref, send_sems, recv_sems)

    blk = pl.BlockSpec((TQ, D_ATTN), _row)
    _, r, cdim = chip_blocks.shape
    return pl.pallas_call(
        body, grid=(nq,), name="attn_bwd",
        out_shape=(jax.ShapeDtypeStruct((nq, D_ATTN, TQ), F32), jax.ShapeDtypeStruct((s, D_ATTN), F32),
                   jax.ShapeDtypeStruct((s, D_ATTN), F32), jax.ShapeDtypeStruct((s, LANES), F32),
                   jax.ShapeDtypeStruct((nq, HEADS, 8, TQ), F32),
                   jax.ShapeDtypeStruct(chip_blocks.shape, chip_blocks.dtype)),
        in_specs=[pl.BlockSpec((TQ, HEADS * AUG), _row), blk, pl.BlockSpec((1, D_ATTN, TQ), lambda j: (j, 0, 0)),
                  VMEM_WHOLE, VMEM_WHOLE, VMEM_WHOLE, VMEM_WHOLE, VMEM_WHOLE, VMEM_WHOLE, ANY],
        out_specs=(pl.BlockSpec((nq, D_ATTN, TQ), lambda j: (0, 0, 0)), blk, blk, pl.BlockSpec((TQ, LANES), _row),
                   pl.BlockSpec((nq, HEADS, 8, TQ), lambda j: (0, 0, 0, 0)), ANY),
        scratch_shapes=[pltpu.VMEM((TQ, wide), F32), pltpu.VMEM((TQ, wide), F32), pltpu.VMEM((TQ, wide), F32),
                        pltpu.VMEM((HEADS, TQ, TQ), F32), pltpu.VMEM((HEADS, TQ, TQ), BF16),
                        pltpu.VMEM((HEADS, TQ, TQ), BF16), pltpu.VMEM((r, cdim), chip_blocks.dtype),
                        pltpu.SemaphoreType.DMA((3,)), pltpu.SemaphoreType.DMA((3,)), pltpu.SemaphoreType.DMA],
        compiler_params=_params(1),
    )(ka, v, kt3, qat3, q, do, dot3, lset3, dlt3, chip_blocks)


def _pre_attn_bwd(dqt3, dk, dv, dcs, drs, fl, dy, x, dh1, g1, wqkv, wf, wu):
    s, d = x.shape
    nt = s // TS
    n = TS + HALO
    sub = TS // TQ
    qkv, fcols = 3 * D_ATTN, 3 * D_ATTN + LANES

    def body(dqt_ref, dk_ref, dv_ref, dcs_ref, drs_ref, fl_ref, dy_ref, x_ref, dh1_ref, g_ref, wqkv_ref, wf_ref, wu_ref,
             gx_ref, dz_ref, dg_ref, db_ref, ybuf, ccar, dlog):
        dqkv_ref = dz_ref.at[:, 0:qkv]
        dfb_ref = dz_ref.at[:, qkv:fcols]
        dub_ref = dz_ref.at[:, fcols:]
        i = pl.program_id(0)
        ti = nt - 1 - i

        @pl.when(i == 0)
        def _():
            ybuf[TS:n, :] = jnp.zeros((HALO, D_POOL), F32)
            ccar[...] = jnp.zeros_like(ccar)
            dg_ref[...] = jnp.zeros_like(dg_ref)
            db_ref[...] = jnp.zeros_like(db_ref)

        rr = lax.broadcasted_iota(jnp.int32, (TS, TS), 0)
        cc = lax.broadcasted_iota(jnp.int32, (TS, TS), 1)
        triu = (cc >= rr).astype(F32)
        dlog[...] = ccar[...] + jnp.dot(triu, drs_ref[...] - dcs_ref[...], precision=HIGHEST, preferred_element_type=F32)
        ccar[...] = dlog[0:1, :]
        df = dlog[...] * jax.nn.sigmoid(-fl_ref[...])
        db_ref[...] += jnp.sum(df, axis=0, keepdims=True)
        dfb = df.astype(BF16)
        dfb_ref[...] = dfb

        t = ti * TS + lax.broadcasted_iota(jnp.int32, (TS, 1), 0)
        dy = dy_ref[...]
        for g, w in enumerate(POOL_WINDOWS):
            cols = slice(g * POOL_CH, (g + 1) * POOL_CH)
            ybuf[0:TS, cols] = dy[:, cols] / jnp.minimum(t + 1, w).astype(F32)
        for g, w in enumerate(POOL_WINDOWS):
            cols = slice(g * POOL_CH, (g + 1) * POOL_CH)
            sm = ybuf[:, cols]
            step = 1
            while step < w:
                sm = sm + pltpu.roll(sm, n - step, 0)
                step *= 2
            dub_ref[:, cols] = (sm[0:TS, :] - dy[:, cols]).astype(BF16)
        ybuf[TS:n, :] = ybuf[0:HALO, :]

        for a in range(sub):
            dqkv_ref[a * TQ:(a + 1) * TQ, 0:D_ATTN] = (dqt_ref[a].T * 0.125).astype(BF16)
        dqkv_ref[:, D_ATTN:2 * D_ATTN] = dk_ref[...].astype(BF16)
        dqkv_ref[:, 2 * D_ATTN:] = dv_ref[...].astype(BF16)
        dhn = _nn(dqkv_ref[...], wqkv_ref[...]) + _nn(dfb, wf_ref[...]) + _nn(dub_ref[...], wu_ref[...])
        dx, dg = _rms_bwd(x_ref[...], g_ref[...], dhn)
        gx_ref[...] = dh1_ref[...] + dx
        dg_ref[...] += dg

    rev = lambda i: (nt - 1 - i, 0)
    blk = lambda w: pl.BlockSpec((TS, w), rev)
    return pl.pallas_call(
        body, grid=(nt,), name="pre_attn_bwd",
        out_shape=(jax.ShapeDtypeStruct((s, d), F32), jax.ShapeDtypeStruct((s, fcols + D_POOL), BF16),
                   jax.ShapeDtypeStruct((1, d), F32), jax.ShapeDtypeStruct((1, LANES), F32)),
        in_specs=[pl.BlockSpec((sub, D_ATTN, TQ), lambda i: (nt - 1 - i, 0, 0)),
                  blk(D_ATTN), blk(D_ATTN), blk(LANES), blk(LANES), blk(LANES), blk(D_POOL), blk(d), blk(d),
                  pl.BlockSpec((1, d), _fixed), pl.BlockSpec(wqkv.shape, _fixed), pl.BlockSpec(wf.shape, _fixed),
                  pl.BlockSpec(wu.shape, _fixed)],
        out_specs=(blk(d), blk(fcols + D_POOL), pl.BlockSpec((1, d), _fixed), pl.BlockSpec((1, LANES), _fixed)),
        scratch_shapes=[pltpu.VMEM((n, D_POOL), F32), pltpu.VMEM((1, LANES), F32), pltpu.VMEM((TS, LANES), F32)],
        compiler_params=_params(1),
    )(dqt3, dk, dv, dcs, drs, fl, dy, x, dh1, g1, wqkv, wf, wu)


def _wgrad(a, b, out_dtype, name):
    s, m = a.shape
    n = b.shape[1]
    tm = max(t for t in range(LANES, min(m, TM_WGRAD) + 1, LANES) if m % t == 0)
    ts = min(TS_WGRAD, s)
    ns = s // ts

    def body(a_ref, b_ref, o_ref, acc):
        i = pl.program_id(1)

        @pl.when(i == 0)
        def _():
            acc[...] = jnp.zeros_like(acc)

        acc[...] += _tn(a_ref[...], b_ref[pl.ds(i * ts, ts), :])

        @pl.when(i == ns - 1)
        def _():
            o_ref[...] = acc[...].astype(out_dtype)

    return pl.pallas_call(
        body, grid=(m // tm, ns), name=name, out_shape=jax.ShapeDtypeStruct((m, n), out_dtype),
        in_specs=[pl.BlockSpec((ts, tm), lambda j, i: (i, j)), VMEM_WHOLE],
        out_specs=pl.BlockSpec((tm, n), lambda j, i: (j, 0)),
        scratch_shapes=[pltpu.VMEM((tm, n), F32)],
        compiler_params=_params(2),
    )(a, b)


def _adamw(w, g, m, v):
    m = ADAM_B1 * m + (1.0 - ADAM_B1) * g
    v = ADAM_B2 * v + (1.0 - ADAM_B2) * (g * g)
    m_hat = m / (1.0 - ADAM_B1 ** ADAM_STEP)
    v_hat = v / (1.0 - ADAM_B2 ** ADAM_STEP)
    delta = -ADAM_LR * (m_hat / (jnp.sqrt(v_hat) + ADAM_EPS) + ADAM_WD * w)
    return delta, m, v


def _pair_sum(core, t, theirs, tr, name):
    nk, r, c = theirs.shape

    def body(core_ref, a_ref, b_ref, o_ref):
        o_ref[...] = (a_ref[...].astype(F32) + b_ref[...].astype(F32)).astype(BF16)

    blk = pl.BlockSpec((1, tr, c), lambda k, i, core_ref: (k, i, 0))
    return pl.pallas_call(
        body, name=name, out_shape=jax.ShapeDtypeStruct(theirs.shape, BF16),
        grid_spec=pltpu.PrefetchScalarGridSpec(
            num_scalar_prefetch=1, grid=(nk, r // tr),
            in_specs=[pl.BlockSpec((1, tr, c), lambda k, i, core_ref: (2 * k + core_ref[0], i, 0)), blk],
            out_specs=blk),
        compiler_params=_params(2),
    )(core, t, theirs)


def _sum_update(p_ref, w_ref, m_ref, v_ref, g_ref, d_ref, nm_ref, nv_ref):
    g = p_ref[0].astype(F32)
    for k in range(1, p_ref.shape[0]):
        g = g + p_ref[k].astype(F32)
    g_ref[...] = g
    d_ref[...], nm_ref[...], nv_ref[...] = _adamw(w_ref[...], g, m_ref[...], v_ref[...])


def _reduce_update_rest(parts, w, m, v, chip_blocks, small_block):
    nk, r, c = parts.shape
    ns = r // TR_REST

    def body(p_ref, w_ref, m_ref, v_ref, b_ref, sm_ref, g_ref, d_ref, nm_ref, nv_ref, got_ref, all_ref,
             stage_b, stage_s, send_b, recv_b, local_b, send_s, recv_s, local_s):
        i = pl.program_id(0)

        @pl.when(i == 0)
        def _():
            _chips_start(b_ref, got_ref, stage_b, send_b, recv_b, local_b)
            _gather_start(sm_ref, all_ref, stage_s, send_s, recv_s, local_s)

        _sum_update(p_ref, w_ref, m_ref, v_ref, g_ref, d_ref, nm_ref, nv_ref)

        @pl.when(i == ns - 1)
        def _():
            _gather_pass_on(all_ref, send_s, recv_s)
            _chips_finish(b_ref, got_ref, send_b, recv_b)
            _gather_finish(sm_ref, all_ref, send_s, recv_s)

    blk = pl.BlockSpec((TR_REST, c), _row)
    out = jax.ShapeDtypeStruct((r, c), F32)
    dma = pltpu.SemaphoreType.DMA
    return pl.pallas_call(
        body, grid=(ns,), name="reduce_update_rest",
        out_shape=(out,) * 4 + (jax.ShapeDtypeStruct(chip_blocks.shape, chip_blocks.dtype),
                                jax.ShapeDtypeStruct((N_DEV,) + small_block.shape, small_block.dtype)),
        in_specs=[pl.BlockSpec((nk, TR_REST, c), lambda i: (0, i, 0)), blk, blk, blk, ANY, ANY],
        out_specs=(blk,) * 4 + (ANY, ANY),
        scratch_shapes=[pltpu.VMEM(chip_blocks.shape[1:], chip_blocks.dtype), pltpu.VMEM(small_block.shape, small_block.dtype),
                        dma((3,)), dma((3,)), dma, dma((7,)), dma((7,)), dma],
        compiler_params=_params(1),
    )(parts, w, m, v, chip_blocks, small_block)


def _reduce_update_big(parts, w, m, v, tr, name):
    nk, r, c = parts.shape

    def body(p_ref, w_ref, m_ref, v_ref, g_ref, d_ref, nm_ref, nv_ref):
        _sum_update(p_ref, w_ref, m_ref, v_ref, g_ref, d_ref, nm_ref, nv_ref)

    blk = pl.BlockSpec((tr, c), _row)
    out = jax.ShapeDtypeStruct((r, c), F32)
    return pl.pallas_call(
        body, grid=(r // tr,), name=name, out_shape=(out,) * 4,
        in_specs=[pl.BlockSpec((nk, tr, c), lambda i: (0, i, 0)), blk, blk, blk],
        out_specs=(blk,) * 4, compiler_params=_params(1),
    )(parts, w, m, v)


def _reduce_update_small(parts, w, m, v):
    nd = parts.shape[0]

    def body(p_ref, w_ref, m_ref, v_ref, g_ref, d_ref, nm_ref, nv_ref):
        g = p_ref[0]
        for k in range(1, nd):
            g = g + p_ref[k]
        g_ref[...] = g
        d_ref[...], nm_ref[...], nv_ref[...] = _adamw(w_ref[...], g, m_ref[...], v_ref[...])

    out = jax.ShapeDtypeStruct(w.shape, F32)
    return pl.pallas_call(body, name="reduce_update_small", out_shape=(out,) * 4,
                          compiler_params=pltpu.CompilerParams(vmem_limit_bytes=VMEM_LIMIT))(parts, w, m, v)


MESH = pl.DeviceIdType.MESH


def _copy_through_vmem(src_hbm, dst_hbm, stage, sem):
    load = pltpu.make_async_copy(src_hbm, stage, sem)
    load.start()
    load.wait()
    store = pltpu.make_async_copy(stage, dst_hbm, sem)
    store.start()
    store.wait()


class _GatherPlan:
    def __init__(self, x_ref, out_ref, send_sems, recv_sems):
        x, y, c = lax.axis_index("x"), lax.axis_index("y"), lax.axis_index("c")
        self.me, self.sibling, self.c = (x, y, c), (x, y, 1 - c), c
        self.chips = [(1 - x, y), (x, 1 - y), (1 - x, 1 - y)]
        self.x_ref, self.out_ref, self.send_sems, self.recv_sems = x_ref, out_ref, send_sems, recv_sems

    def slot(self, px, py, pc):
        return self.out_ref.at[4 * px + 2 * py + pc]

    def copy(self, k, block, to, src=None):
        return pltpu.make_async_remote_copy(
            src_ref=self.slot(*block) if src is None else src, dst_ref=self.slot(*block),
            send_sem=self.send_sems.at[k], recv_sem=self.recv_sems.at[k], device_id=to, device_id_type=MESH)

    def first(self):
        return [self.copy(0, self.me, self.sibling, src=self.x_ref)] + [
            self.copy(1 + j, self.me, (*chip, self.c), src=self.x_ref) for j, chip in enumerate(self.chips)]

    def passed(self):
        return [self.copy(4 + j, (*chip, self.c), self.sibling) for j, chip in enumerate(self.chips)]


def _gather_start(x_ref, out_ref, stage, send_sems, recv_sems, local_sem):
    plan = _GatherPlan(x_ref, out_ref, send_sems, recv_sems)
    for cp in plan.first():
        cp.start()
    _copy_through_vmem(x_ref, plan.slot(*plan.me), stage, local_sem)


def _gather_pass_on(out_ref, send_sems, recv_sems):
    plan = _GatherPlan(None, out_ref, send_sems, recv_sems)
    passed = plan.passed()
    for j, chip in enumerate(plan.chips):
        plan.copy(1 + j, (*chip, plan.c), plan.me).wait_recv()
        passed[j].start()


def _gather_finish(x_ref, out_ref, send_sems, recv_sems):
    plan = _GatherPlan(x_ref, out_ref, send_sems, recv_sems)
    plan.copy(0, plan.sibling, plan.me).wait_recv()
    for j, chip in enumerate(plan.chips):
        plan.copy(4 + j, (*chip, 1 - plan.c), plan.me).wait_recv()
    for cp in plan.first() + plan.passed():
        cp.wait_send()


def _all_gather(xs, name):
    r, cdim = xs.shape

    def body(x_ref, out_ref, stage, send_sems, recv_sems, local_sem):
        _gather_start(x_ref, out_ref, stage, send_sems, recv_sems, local_sem)
        _gather_pass_on(out_ref, send_sems, recv_sems)
        _gather_finish(x_ref, out_ref, send_sems, recv_sems)

    return pl.pallas_call(
        body, name=name, out_shape=jax.ShapeDtypeStruct((N_DEV, r, cdim), xs.dtype),
        in_specs=[ANY], out_specs=ANY,
        scratch_shapes=[pltpu.VMEM((r, cdim), xs.dtype), pltpu.SemaphoreType.DMA((7,)), pltpu.SemaphoreType.DMA((7,)),
                        pltpu.SemaphoreType.DMA],
        compiler_params=pltpu.CompilerParams(vmem_limit_bytes=VMEM_LIMIT),
    )(xs)


def _rs_pair(t, name):
    _, r, cdim = t.shape

    def body(t_ref, theirs_ref, send_sems, recv_sems):
        x, y, c = lax.axis_index("x"), lax.axis_index("y"), lax.axis_index("c")
        remote = [pltpu.make_async_remote_copy(
            src_ref=t_ref.at[2 * k + (1 - c)], dst_ref=theirs_ref.at[k],
            send_sem=send_sems.at[k], recv_sem=recv_sems.at[k], device_id=(x, y, 1 - c), device_id_type=MESH)
            for k in range(4)]
        for cp in remote:
            cp.start()
        for cp in remote:
            cp.wait()

    return pl.pallas_call(
        body, name=name, out_shape=jax.ShapeDtypeStruct((4, r, cdim), t.dtype), in_specs=[ANY], out_specs=ANY,
        scratch_shapes=[pltpu.SemaphoreType.DMA((4,)), pltpu.SemaphoreType.DMA((4,))],
    )(t)


def _chips_start(b_ref, out_ref, stage, send_sems, recv_sems, local_sem):
    x, y, c = lax.axis_index("x"), lax.axis_index("y"), lax.axis_index("c")
    mychip = 2 * x + y
    for j, (px, py) in enumerate([(1 - x, y), (x, 1 - y), (1 - x, 1 - y)]):
        pltpu.make_async_remote_copy(
            src_ref=b_ref.at[2 * px + py], dst_ref=out_ref.at[mychip],
            send_sem=send_sems.at[j], recv_sem=recv_sems.at[j], device_id=(px, py, c), device_id_type=MESH).start()
    _copy_through_vmem(b_ref.at[mychip], out_ref.at[mychip], stage, local_sem)


def _chips_finish(b_ref, out_ref, send_sems, recv_sems):
    x, y, c = lax.axis_index("x"), lax.axis_index("y"), lax.axis_index("c")
    for j, (px, py) in enumerate([(1 - x, y), (x, 1 - y), (1 - x, 1 - y)]):
        pltpu.make_async_remote_copy(
            src_ref=b_ref.at[2 * px + py], dst_ref=out_ref.at[2 * px + py],
            send_sem=send_sems.at[j], recv_sem=recv_sems.at[j], device_id=(px, py, c), device_id_type=MESH).wait()


def _pad_rows(a, rows):
    return jnp.pad(a, ((0, rows - a.shape[0]), (0, 0)))


def _pack_in(w_in):
    return _pad_rows(w_in[0].T, ROWS_IN)


def _unpack_in(r):
    return r[0:SHARD_IN].T[None]


def _pack_rest(w_out, w_gate, w_up, w_down, w_ple, w_pg):
    return jnp.concatenate([w_out[0], w_gate[0].T, w_up[0].T, w_down[0], w_ple[0].T.reshape(32, D_MODEL), w_pg[0]],
                           axis=0)


def _unpack_rest(r):
    return (r[0:OFF_GATE][None], r[OFF_GATE:OFF_UP].T[None], r[OFF_UP:OFF_DOWN].T[None],
            r[OFF_DOWN:OFF_PLE][None], r[OFF_PLE:OFF_PG].reshape(128, D_PLE).T[None], r[OFF_PG:ROWS_REST][None])


def _full_rest(g):
    return (g[:, 0:OFF_GATE].reshape(D_MODEL, D_MODEL), g[:, OFF_GATE:OFF_UP].reshape(D_FF, D_MODEL),
            g[:, OFF_UP:OFF_DOWN].reshape(D_FF, D_MODEL), g[:, OFF_DOWN:OFF_PLE].reshape(D_FF, D_MODEL),
            g[:, OFF_PLE:OFF_PG].reshape(D_MODEL, D_PLE), g[:, OFF_PG:ROWS_REST].reshape(D_MODEL, D_MODEL))


def _pack_small(w_pool, g_mix_pre, g_mix_post, g_ffn_pre, g_ffn_post, g_ple, g_attn, g_pool, pool_scale, b_forget,
                loss=None):
    def row(vrow):
        return jnp.pad(vrow.reshape(1, -1), ((0, 0), (0, D_MODEL - vrow.size)))
    rows = [w_pool.reshape(64, D_MODEL), row(g_mix_pre), row(g_mix_post), row(g_ffn_pre), row(g_ffn_post), row(g_ple),
            row(g_attn), row(g_pool), row(pool_scale), row(b_forget),
            row(loss) if loss is not None else jnp.zeros((1, D_MODEL), F32)]
    return _pad_rows(jnp.concatenate(rows, axis=0), SMALL_ROWS)


def _unpack_small(r):
    return dict(
        w_pool=r[0:64].reshape(1, 4, POOL_CH, POOL_CH), g_mix_pre=r[ROW_G_MIX_PRE:ROW_G_MIX_PRE + 1],
        g_mix_post=r[ROW_G_MIX_POST:ROW_G_MIX_POST + 1], g_ffn_pre=r[ROW_G_FFN_PRE:ROW_G_FFN_PRE + 1],
        g_ffn_post=r[ROW_G_FFN_POST:ROW_G_FFN_POST + 1], g_ple=r[ROW_G_PLE:ROW_G_PLE + 1],
        g_attn_grp=r[ROW_G_ATTN:ROW_G_ATTN + 1, 0:D_ATTN], g_pool_grp=r[ROW_G_POOL:ROW_G_POOL + 1, 0:D_POOL],
        pool_scale=r[ROW_POOL_SCALE:ROW_POOL_SCALE + 1, 0:D_POOL], b_forget=r[ROW_B_FORGET:ROW_B_FORGET + 1, 0:HEADS])


def _step(x, p, tgt, small, in_w, in_m, in_v, rest_w, rest_m, rest_v):
    core = lax.axis_index("c").astype(jnp.int32).reshape(1)
    win_t = _all_gather(in_w.astype(BF16), "gather_w_in")[:, 0:SHARD_IN].reshape(D_IN, D_MODEL)
    wqkv = win_t[0:3 * D_ATTN]
    wf = _pad_rows(win_t[3 * D_ATTN:3 * D_ATTN + HEADS], LANES)
    wu = win_t[3 * D_ATTN + HEADS:]
    wpool = small["w_pool"].astype(BF16)
    bpad = jnp.pad(small["b_forget"], ((0, 0), (0, LANES - HEADS)))

    lay = _attn_layout_constants()
    hn, q, ka, v, qat3, vt3, kt3, fl, y, mpre = _pre_attn_fwd(x, small["g_mix_pre"], wqkv, wf, wu, bpad, wpool, lay)
    a, lset3, gathered = _attn_fwd(ka, qat3, vt3, rest_w.astype(BF16))
    wout, wg_t, wu_t, wd, wple_t, wpg = _full_rest(gathered)
    mix, o, h1, hn2 = _post_attn_fwd(a, mpre, x, small["g_attn_grp"], small["g_pool_grp"], small["pool_scale"], wout,
                                     small["g_mix_post"], small["g_ffn_pre"])
    gate, up, act, ff, h2 = _ffn_fwd(hn2, wg_t, wu_t, wd, h1, small["g_ffn_post"])
    dh2, dff, dgl, dpp, h2b, pb, loss8, dg_ple, dg_ffn_post = _tail_fwd_bwd(
        h2, p, tgt, ff, wple_t, wpg, small["g_ple"], small["g_ffn_post"])
    dgate, dup, dh1, dg_ffn_pre = _ffn_bwd(dff, gate, up, wd, wg_t, wu_t, h1, dh2, small["g_ffn_pre"])
    dob, dab, dat3, dlt3, dmpb, dy, dg_mix_post, dg_attn, dg_pool, dps = _post_attn_bwd(
        dh1, o, a, mpre, wout, wpool, small["g_mix_post"], small["g_attn_grp"], small["g_pool_grp"], small["pool_scale"])

    nd = N_DEV
    send_rest = jnp.concatenate([
        _wgrad(mix, dob, BF16, "wgrad_out").reshape(nd, 128, D_MODEL),
        _wgrad(dgate, hn2, BF16, "wgrad_gate").reshape(nd, 352, D_MODEL),
        _wgrad(dup, hn2, BF16, "wgrad_up").reshape(nd, 352, D_MODEL),
        _wgrad(act, dff, BF16, "wgrad_down").reshape(nd, 352, D_MODEL),
        _wgrad(dpp, pb, BF16, "wgrad_ple").reshape(nd, 32, D_MODEL),
        _wgrad(h2b, dgl, BF16, "wgrad_ple_gate").reshape(nd, 128, D_MODEL)], axis=1)
    pair_rest = _pair_sum(core, send_rest, _rs_pair(send_rest, "rs_pair_rest"), TR_REST, "rs_pair_sum_rest")

    dqt3, dk, dv, dcs, drs4, chips_rest = _attn_bwd(ka, v, kt3, qat3, q, dab, dat3, lset3, dlt3, pair_rest)
    drs = jnp.pad(drs4[:, :, 0, :].transpose(0, 2, 1).reshape(-1, HEADS), ((0, 0), (0, LANES - HEADS)))
    gx, dz, dg_mix_pre, db = _pre_attn_bwd(dqt3, dk, dv, dcs, drs, fl, dy, x, dh1, small["g_mix_pre"], wqkv, wf, wu)

    dwz = _wgrad(dz, hn, F32, "wgrad_in")
    dwin_t = jnp.concatenate([dwz[0:3 * D_ATTN], dwz[3 * D_ATTN:3 * D_ATTN + HEADS], dwz[3 * D_ATTN + LANES:]], axis=0)
    send_in = jnp.pad(dwin_t.reshape(nd, SHARD_IN, D_MODEL), ((0, 0), (0, ROWS_IN - SHARD_IN), (0, 0))).astype(BF16)
    pair_in = _pair_sum(core, send_in, _rs_pair(send_in, "rs_pair_in"), ROWS_IN, "rs_pair_sum_in")

    dwp = _wgrad(y, dmpb, F32, "wgrad_pool")
    dw_pool = jnp.stack([dwp[g * POOL_CH:(g + 1) * POOL_CH, g * POOL_CH:(g + 1) * POOL_CH] for g in range(4)])
    small_part = _pack_small(dw_pool, dg_mix_pre, dg_mix_post, dg_ffn_pre, dg_ffn_post, dg_ple, dg_attn, dg_pool, dps,
                             db[:, 0:HEADS], loss8[0:1, 0:1])

    *upd_rest, chips_in, small_all = _reduce_update_rest(chips_rest, rest_w, rest_m, rest_v, pair_in, small_part)
    upd_in = _reduce_update_big(chips_in, in_w, in_m, in_v, ROWS_IN, "reduce_update_in")
    return gx, small_all, upd_in, upd_rest


def kernel(x, p, g_mix_pre, w_in, b_forget, g_attn_grp, g_pool_grp, w_pool, pool_scale, w_out, g_mix_post, g_ffn_pre, w_ffn_gate, w_ffn_up, w_ffn_down, g_ffn_post, w_ple_proj, g_ple, w_ple_gate, loss_target, m_g_mix_pre, m_w_in, m_b_forget, m_g_attn_grp, m_g_pool_grp, m_w_pool, m_pool_scale, m_w_out, m_g_mix_post, m_g_ffn_pre, m_w_ffn_gate, m_w_ffn_up, m_w_ffn_down, m_g_ffn_post, m_w_ple_proj, m_g_ple, m_w_ple_gate, v_g_mix_pre, v_w_in, v_b_forget, v_g_attn_grp, v_g_pool_grp, v_w_pool, v_pool_scale, v_w_out, v_g_mix_post, v_g_ffn_pre, v_w_ffn_gate, v_w_ffn_up, v_w_ffn_down, v_g_ffn_post, v_w_ple_proj, v_g_ple, v_w_ple_gate):
    small = dict(w_pool=w_pool[0], g_mix_pre=g_mix_pre, g_mix_post=g_mix_post, g_ffn_pre=g_ffn_pre,
                 g_ffn_post=g_ffn_post, g_ple=g_ple, g_attn_grp=g_attn_grp, g_pool_grp=g_pool_grp,
                 pool_scale=pool_scale, b_forget=b_forget)
    gx, small_all, upd_in, upd_rest = _step(
        x[0], p[0, 0], loss_target[0], small, _pack_in(w_in), _pack_in(m_w_in), _pack_in(v_w_in),
        _pack_rest(w_out, w_ffn_gate, w_ffn_up, w_ffn_down, w_ple_proj, w_ple_gate),
        _pack_rest(m_w_out, m_w_ffn_gate, m_w_ffn_up, m_w_ffn_down, m_w_ple_proj, m_w_ple_gate),
        _pack_rest(v_w_out, v_w_ffn_gate, v_w_ffn_up, v_w_ffn_down, v_w_ple_proj, v_w_ple_gate))

    sm_w = _pack_small(w_pool, g_mix_pre, g_mix_post, g_ffn_pre, g_ffn_post, g_ple, g_attn_grp, g_pool_grp, pool_scale, b_forget)
    sm_m = _pack_small(m_w_pool, m_g_mix_pre, m_g_mix_post, m_g_ffn_pre, m_g_ffn_post, m_g_ple, m_g_attn_grp, m_g_pool_grp, m_pool_scale, m_b_forget)
    sm_v = _pack_small(v_w_pool, v_g_mix_pre, v_g_mix_post, v_g_ffn_pre, v_g_ffn_post, v_g_ple, v_g_attn_grp, v_g_pool_grp, v_pool_scale, v_b_forget)
    upd_small = _reduce_update_small(small_all, sm_w, sm_m, sm_v)
    loss = upd_small[0][ROW_LOSS, 0]

    def leaves(k):
        b_out, b_gate, b_up, b_down, b_ple, b_pg = _unpack_rest(upd_rest[k])
        s = _unpack_small(upd_small[k])
        return (s["g_mix_pre"], _unpack_in(upd_in[k]), s["b_forget"], s["g_attn_grp"], s["g_pool_grp"], s["w_pool"],
                s["pool_scale"], b_out, s["g_mix_post"], s["g_ffn_pre"], b_gate, b_up, b_down, s["g_ffn_post"], b_ple,
                s["g_ple"], b_pg)

    return (loss, gx[None], *leaves(0), *leaves(1), *leaves(2), *leaves(3))
```

```python
import functools

import jax
import jax.numpy as jnp
from jax import lax
from jax.experimental import pallas as pl
from jax.experimental.pallas import tpu as pltpu

F32 = jnp.float32
BF16 = jnp.bfloat16
HIGHEST = lax.Precision.HIGHEST

D_MODEL = 1024
HEADS = 8
HEAD_DIM = 64
D_ATTN = HEADS * HEAD_DIM
POOL_WINDOWS = (2, 4, 8, 16)
POOL_CH = 128
D_POOL = POOL_CH * len(POOL_WINDOWS)
D_FF = 2816
D_PLE = 256
D_IN = 3 * D_ATTN + HEADS + D_POOL
RMS_EPS = 1e-6
N_DEV = 8

ADAM_LR = 0.001
ADAM_B1 = 0.9
ADAM_B2 = 0.999
ADAM_EPS = 1e-08
ADAM_WD = 0.01
ADAM_STEP = 10

LANES = 128
HALO = 16
TS = 512
TS_FF = 512
TS_WGRAD = 1024
TM_WGRAD = 2176
TQ = 256
TN_FF = 1408
NEG = -1e30
VMEM_LIMIT = 56 * 1024 * 1024

SHARD_IN = 257
ROWS_IN = 272
OFF_GATE = 128
OFF_UP = OFF_GATE + 352
OFF_DOWN = OFF_UP + 352
OFF_PLE = OFF_DOWN + 352
OFF_PG = OFF_PLE + 32
ROWS_REST = OFF_PG + 128
TR_REST = 192

SMALL_ROWS = 80
ROW_G_MIX_PRE, ROW_G_MIX_POST, ROW_G_FFN_PRE, ROW_G_FFN_POST, ROW_G_PLE = 64, 65, 66, 67, 68
ROW_G_ATTN, ROW_G_POOL, ROW_POOL_SCALE, ROW_B_FORGET, ROW_LOSS = 69, 70, 71, 72, 73


def _nn(a, b):
    return jnp.dot(a, b, preferred_element_type=F32)


def _nt(a, b):
    return lax.dot_general(a, b, (((1,), (1,)), ((), ())), preferred_element_type=F32)


def _tn(a, b):
    return lax.dot_general(a, b, (((0,), (0,)), ((), ())), preferred_element_type=F32)


def _rstd(v):
    return lax.rsqrt(jnp.mean(v * v, axis=-1, keepdims=True) + RMS_EPS)


def _rms_bwd(v, g, dy):
    r = _rstd(v)
    vh = v * r
    t = dy * g
    dv = r * (t - vh * jnp.mean(t * vh, axis=-1, keepdims=True))
    return dv, jnp.sum(dy * vh, axis=0, keepdims=True)


def _split3(v):
    hi = v.astype(BF16)
    rest = v - hi.astype(F32)
    mid = rest.astype(BF16)
    return hi, mid, (rest - mid.astype(F32)).astype(BF16)


def _mask_matmul(mask, v):
    hi, mid, lo = _split3(v)
    return _nn(mask, lo) + _nn(mask, mid) + _nn(mask, hi)


def _params(n_grid):
    return pltpu.CompilerParams(dimension_semantics=("arbitrary",) * n_grid, vmem_limit_bytes=VMEM_LIMIT)


def _row(i):
    return (i, 0)


def _fixed(*_):
    return (0, 0)


assert TS == 2 * TQ
_HALVES = (slice(0, TQ), slice(TQ, TS))

VMEM_WHOLE = pl.BlockSpec(memory_space=pltpu.VMEM)
SMEM_WHOLE = pl.BlockSpec(memory_space=pltpu.SMEM)
ANY = pl.BlockSpec(memory_space=pl.ANY)


AUG = 128
BIAS_LANE = HEAD_DIM
ONE_LANE = HEAD_DIM + 3
SPARE_LANE = HEADS


def _attn_layout_constants():
    import numpy as np
    place = np.zeros((D_ATTN, HEADS * AUG), np.float32)
    for r in range(D_ATTN):
        place[r, (r // HEAD_DIM) * AUG + r % HEAD_DIM] = 1.0
    bias_k = np.zeros((3, LANES, HEADS * AUG), np.float32)
    bias_q = np.zeros((3, LANES, HEADS * AUG), np.float32)
    for h in range(HEADS):
        for part in range(3):
            bias_k[part, h, h * AUG + BIAS_LANE + part] = -1.0
            bias_q[part, h, h * AUG + ONE_LANE + part] = 1.0
            bias_k[0, SPARE_LANE, h * AUG + ONE_LANE + part] = 1.0
            bias_q[0, SPARE_LANE, h * AUG + BIAS_LANE + part] = 1.0
    as_bf = lambda a: jnp.asarray(a, BF16)
    return dict(place=as_bf(place), place_t=as_bf(place.T), bias_k=as_bf(bias_k),
                bias_q_t=as_bf(bias_q.transpose(0, 2, 1)))


def _pre_attn_fwd(x, g1, wqkv, wf, wu, bpad, wpool, lay):
    s, d = x.shape
    nt = s // TS
    sub = TS // TQ

    def body(x_ref, g_ref, wqkv_ref, wf_ref, wu_ref, b_ref, wp_ref, place_ref, place_t_ref, bk_ref, bqt_ref,
             hn_ref, q_ref, ka_ref, v_ref, qat_ref, vt_ref, kt_ref, fl_ref, y_ref, mp_ref, ubuf, ccar, cbuf):
        i = pl.program_id(0)

        @pl.when(i == 0)
        def _():
            ubuf[0:HALO, :] = jnp.zeros((HALO, D_POOL), F32)
            ccar[...] = jnp.zeros_like(ccar)

        xv = x_ref[...]
        hn = (xv * _rstd(xv) * g_ref[...]).astype(BF16)
        hn_ref[...] = hn
        zq = _nt(hn, wqkv_ref[...])
        qb = (zq[:, 0:D_ATTN] * 0.125).astype(BF16)
        kb = zq[:, D_ATTN:2 * D_ATTN].astype(BF16)
        vb = zq[:, 2 * D_ATTN:3 * D_ATTN].astype(BF16)
        q_ref[...] = qb
        v_ref[...] = vb

        fl = _nt(hn, wf_ref[...]) + b_ref[...]
        fl_ref[...] = fl
        logf = jax.nn.log_sigmoid(fl)
        rr = lax.broadcasted_iota(jnp.int32, (TS, TS), 0)
        cc = lax.broadcasted_iota(jnp.int32, (TS, TS), 1)
        c = _mask_matmul((cc <= rr).astype(BF16), logf) + ccar[...]
        cbuf[...] = c
        ccar[...] = cbuf[TS - 1:TS, :]
        hi, mid, lo = _split3(c)
        lane = lax.broadcasted_iota(jnp.int32, (TS, LANES), 1)
        parts = (jnp.where(lane == SPARE_LANE, 1.0, hi).astype(BF16), mid, lo)
        ka = _nn(kb, place_ref[...])
        qat = _nt(place_t_ref[...], qb)
        for part in range(3):
            ka = ka + _nn(parts[part], bk_ref[part])
            qat = qat + _nt(bqt_ref[part], parts[part])
        ka_ref[...] = ka.astype(BF16)
        qat = qat.astype(BF16)
        vt = vb.T
        kt = kb.T
        for a in range(sub):
            qat_ref[a] = qat[:, a * TQ:(a + 1) * TQ]
            vt_ref[a] = vt[:, a * TQ:(a + 1) * TQ]
            kt_ref[a] = kt[:, a * TQ:(a + 1) * TQ]

        u = _nt(hn, wu_ref[...])
        ubuf[HALO:HALO + TS, :] = u
        t = i * TS + lax.broadcasted_iota(jnp.int32, (TS, 1), 0)
        for g, w in enumerate(POOL_WINDOWS):
            cols = slice(g * POOL_CH, (g + 1) * POOL_CH)
            sm = ubuf[:, cols]
            step = 1
            while step < w:
                sm = sm + pltpu.roll(sm, step, 0)
                step *= 2
            cnt = jnp.minimum(t + 1, w).astype(F32)
            yg = (sm[HALO:, :] / cnt - u[:, cols]).astype(BF16)
            y_ref[:, cols] = yg
            mp_ref[:, cols] = _nn(yg, wp_ref[g])
        ubuf[0:HALO, :] = u[TS - HALO:, :]

    nq = s // TQ
    aug = HEADS * AUG
    outs = (
        jax.ShapeDtypeStruct((s, d), BF16), jax.ShapeDtypeStruct((s, D_ATTN), BF16),
        jax.ShapeDtypeStruct((s, aug), BF16), jax.ShapeDtypeStruct((s, D_ATTN), BF16),
        jax.ShapeDtypeStruct((nq, aug, TQ), BF16), jax.ShapeDtypeStruct((nq, D_ATTN, TQ), BF16),
        jax.ShapeDtypeStruct((nq, D_ATTN, TQ), BF16),
        jax.ShapeDtypeStruct((s, LANES), F32),
        jax.ShapeDtypeStruct((s, D_POOL), BF16), jax.ShapeDtypeStruct((s, D_POOL), F32),
    )
    fixed3 = lambda i: (0, 0, 0)
    tiles3 = lambda rows: pl.BlockSpec((sub, rows, TQ), lambda i: (i, 0, 0))
    return pl.pallas_call(
        body, grid=(nt,), out_shape=outs, name="pre_attn_fwd",
        in_specs=[pl.BlockSpec((TS, d), _row), pl.BlockSpec((1, d), _fixed),
                  pl.BlockSpec(wqkv.shape, _fixed), pl.BlockSpec(wf.shape, _fixed), pl.BlockSpec(wu.shape, _fixed),
                  pl.BlockSpec((1, LANES), _fixed), pl.BlockSpec(wpool.shape, fixed3),
                  pl.BlockSpec(lay["place"].shape, _fixed), pl.BlockSpec(lay["place_t"].shape, _fixed),
                  pl.BlockSpec(lay["bias_k"].shape, fixed3), pl.BlockSpec(lay["bias_q_t"].shape, fixed3)],
        out_specs=(pl.BlockSpec((TS, d), _row), pl.BlockSpec((TS, D_ATTN), _row),
                   pl.BlockSpec((TS, aug), _row), pl.BlockSpec((TS, D_ATTN), _row),
                   tiles3(aug), tiles3(D_ATTN), tiles3(D_ATTN),
                   pl.BlockSpec((TS, LANES), _row),
                   pl.BlockSpec((TS, D_POOL), _row), pl.BlockSpec((TS, D_POOL), _row)),
        scratch_shapes=[pltpu.VMEM((TS + HALO, D_POOL), F32), pltpu.VMEM((1, LANES), F32), pltpu.VMEM((TS, LANES), F32)],
        compiler_params=_params(1),
    )(x, g1, wqkv, wf, wu, bpad, wpool, lay["place"], lay["place_t"], lay["bias_k"], lay["bias_q_t"])


def _causal_in_tile():
    krow = lax.broadcasted_iota(jnp.int32, (TQ, TQ), 0)
    qcol = lax.broadcasted_iota(jnp.int32, (TQ, TQ), 1)
    return krow <= qcol


def _attn_fwd(ka, qat3, vt3, own_block):
    s = ka.shape[0]
    nq = s // TQ
    pass_on_step = max(nq - 2, 0)

    def body(qa_ref, ka_ref, vt_ref, own_ref, a_ref, lset_ref, all_ref, acc, st_scr, pt_scr,
             stage, send_sems, recv_sems, local_sem):
        i = pl.program_id(0)

        @pl.when(i == 0)
        def _():
            _gather_start(own_ref, all_ref, stage, send_sems, recv_sems, local_sem)

        @pl.when(i == pass_on_step)
        def _():
            _gather_pass_on(all_ref, send_sems, recv_sems)

        acc[...] = jnp.zeros_like(acc)

        def tile(j, stats, masked):
            tile_max = []
            for h in range(HEADS):
                aug = slice(h * AUG, (h + 1) * AUG)
                st = _nn(ka_ref[pl.ds(j * TQ, TQ), aug], qa_ref[0, aug, :])
                if masked:
                    st = jnp.where(_causal_in_tile(), st, NEG)
                st_scr[h] = st
                tile_max.append(jnp.max(st, axis=0, keepdims=True))
            new, scale = [], []
            for h in range(HEADS):
                m_old, l_old = stats[h]
                m_new = jnp.maximum(m_old, tile_max[h])
                al = jnp.exp(m_old - m_new)
                pt = jnp.exp(st_scr[h] - m_new)
                pt_scr[h] = pt.astype(BF16)
                new.append((m_new, al * l_old + jnp.sum(pt, axis=0, keepdims=True)))
                scale.append(al)
            for h in range(HEADS):
                rows = slice(h * HEAD_DIM, (h + 1) * HEAD_DIM)
                acc[rows, :] = scale[h] * acc[rows, :] + _nn(vt_ref[j, rows, :], pt_scr[h])
            return tuple(new)

        init = tuple((jnp.full((1, TQ), NEG, F32), jnp.zeros((1, TQ), F32)) for _ in range(HEADS))
        stats = lax.fori_loop(0, i, functools.partial(tile, masked=False), init)
        stats = tile(i, stats, True)
        for h in range(HEADS):
            rows = slice(h * HEAD_DIM, (h + 1) * HEAD_DIM)
            acc[rows, :] = acc[rows, :] / stats[h][1]
            lset_ref[0, h:h + 1, :] = stats[h][0] + jnp.log(stats[h][1])
        a_ref[...] = acc[...].T

        @pl.when(i == nq - 1)
        def _():
            _gather_finish(own_ref, all_ref, send_sems, recv_sems)

    r, cdim = own_block.shape
    return pl.pallas_call(
        body, grid=(nq,), name="attn_fwd",
        out_shape=(jax.ShapeDtypeStruct((s, D_ATTN), F32), jax.ShapeDtypeStruct((nq, HEADS, TQ), F32),
                   jax.ShapeDtypeStruct((N_DEV, r, cdim), own_block.dtype)),
        in_specs=[pl.BlockSpec((1, HEADS * AUG, TQ), lambda i: (i, 0, 0)), VMEM_WHOLE, VMEM_WHOLE, ANY],
        out_specs=(pl.BlockSpec((TQ, D_ATTN), _row), pl.BlockSpec((1, HEADS, TQ), lambda i: (i, 0, 0)), ANY),
        scratch_shapes=[pltpu.VMEM((D_ATTN, TQ), F32), pltpu.VMEM((HEADS, TQ, TQ), F32), pltpu.VMEM((HEADS, TQ, TQ), BF16),
                        pltpu.VMEM((r, cdim), own_block.dtype),
                        pltpu.SemaphoreType.DMA((7,)), pltpu.SemaphoreType.DMA((7,)), pltpu.SemaphoreType.DMA],
        compiler_params=_params(1),
    )(qat3, ka, vt3, own_block)


def _post_attn_fwd(a, mpre, x, g_attn, g_pool, pscale, wout, g_post, g_ffn_pre):
    s, d = x.shape

    def body(a_ref, mp_ref, x_ref, ga_ref, gp_ref, ps_ref, wo_ref, gpost_ref, gpre_ref,
             mix_ref, o_ref, h1_ref, hn2_ref):
        for rows in _HALVES:
            av = a_ref[rows, :]
            mix_ref[rows, 0:D_ATTN] = (av * _rstd(av) * ga_ref[...]).astype(BF16)
            mv = mp_ref[rows, :] * ps_ref[...]
            mix_ref[rows, D_ATTN:] = (mv * _rstd(mv) * gp_ref[...]).astype(BF16)
            o = _nn(mix_ref[rows, :], wo_ref[...])
            o_ref[rows, :] = o
            h1 = x_ref[rows, :] + o * _rstd(o) * gpost_ref[...]
            h1_ref[rows, :] = h1
            hn2_ref[rows, :] = (h1 * _rstd(h1) * gpre_ref[...]).astype(BF16)

    vec = lambda n: pl.BlockSpec((1, n), _fixed)
    return pl.pallas_call(
        body, grid=(s // TS,), name="post_attn_fwd",
        out_shape=(jax.ShapeDtypeStruct((s, d), BF16), jax.ShapeDtypeStruct((s, d), F32),
                   jax.ShapeDtypeStruct((s, d), F32), jax.ShapeDtypeStruct((s, d), BF16)),
        in_specs=[pl.BlockSpec((TS, D_ATTN), _row), pl.BlockSpec((TS, D_POOL), _row), pl.BlockSpec((TS, d), _row),
                  vec(D_ATTN), vec(D_POOL), vec(D_POOL), pl.BlockSpec(wout.shape, _fixed), vec(d), vec(d)],
        out_specs=(pl.BlockSpec((TS, d), _row),) * 4,
        compiler_params=_params(1),
    )(a, mpre, x, g_attn, g_pool, pscale, wout, g_post, g_ffn_pre)


def _ffn_fwd(hn2, wg, wu, wd, h1, g_post):
    s, d = h1.shape
    nc = D_FF // TN_FF
    ts = min(TS_FF, s)

    def body(hn_ref, wg_ref, wu_ref, wd_ref, h1_ref, g_ref, gate_ref, up_ref, act_ref, ff_ref, h2_ref, acc):
        j = pl.program_id(1)

        @pl.when(j == 0)
        def _():
            acc[...] = jnp.zeros_like(acc)

        for r in range(2):
            rows = slice(r * (ts // 2), (r + 1) * (ts // 2))
            hn = hn_ref[rows, :]
            gt = _nt(hn, wg_ref[...])
            up = _nt(hn, wu_ref[...])
            gate_ref[rows, :] = gt.astype(BF16)
            up_ref[rows, :] = up.astype(BF16)
            act_ref[rows, :] = (gt * jax.nn.sigmoid(gt) * up).astype(BF16)
            acc[rows, :] += _nn(act_ref[rows, :], wd_ref[...])

        @pl.when(j == nc - 1)
        def _():
            ff = acc[...]
            ff_ref[...] = ff
            h2_ref[...] = h1_ref[...] + ff * _rstd(ff) * g_ref[...]

    rowblk = pl.BlockSpec((ts, d), lambda i, j: (i, 0))
    wblk = pl.BlockSpec((TN_FF, d), lambda i, j: (j, 0))
    chunk = pl.BlockSpec((ts, TN_FF), lambda i, j: (i, j))
    return pl.pallas_call(
        body, grid=(s // ts, nc), name="ffn_fwd",
        out_shape=(jax.ShapeDtypeStruct((s, D_FF), BF16),) * 3 + (jax.ShapeDtypeStruct((s, d), F32),) * 2,
        in_specs=[rowblk, wblk, wblk, wblk, rowblk, pl.BlockSpec((1, d), lambda i, j: (0, 0))],
        out_specs=(chunk, chunk, chunk, rowblk, rowblk),
        scratch_shapes=[pltpu.VMEM((ts, d), F32)],
        compiler_params=_params(2),
    )(hn2, wg, wu, wd, h1, g_post)


def _tail_fwd_bwd(h2, p, tgt, ff, wple, wpg, g_ple, g_ffn_post):
    s, d = h2.shape

    def body(h2_ref, p_ref, t_ref, ff_ref, wple_ref, wpg_ref, gple_ref, gfp_ref,
             dh2_ref, dff_ref, dgl_ref, dpp_ref, h2b_ref, pb_ref, loss_ref, dgple_ref, dgfp_ref):
        i = pl.program_id(0)

        @pl.when(i == 0)
        def _():
            loss_ref[...] = jnp.zeros_like(loss_ref)
            dgple_ref[...] = jnp.zeros_like(dgple_ref)
            dgfp_ref[...] = jnp.zeros_like(dgfp_ref)

        h2 = h2_ref[...]
        h2b = h2.astype(BF16)
        h2b_ref[...] = h2b
        pb = p_ref[...].astype(BF16)
        pb_ref[...] = pb
        pp = _nt(pb, wple_ref[...])
        gple = gple_ref[...]
        e = pp * _rstd(pp) * gple
        sg = jax.nn.sigmoid(_nn(h2b, wpg_ref[...]))
        diff = h2 + sg * e - t_ref[...]
        sq = jnp.sum(jnp.sum(diff * diff, axis=1, keepdims=True), axis=0, keepdims=True)
        loss_ref[...] += jnp.broadcast_to(sq * (0.5 / d), loss_ref.shape)
        dh3 = diff * (1.0 / d)
        dgl = (dh3 * e * sg * (1.0 - sg)).astype(BF16)
        dgl_ref[...] = dgl
        dh2 = dh3 + _nt(dgl, wpg_ref[...])
        dh2_ref[...] = dh2
        dpp, dg = _rms_bwd(pp, gple, dh3 * sg)
        dpp_ref[...] = dpp.astype(BF16)
        dgple_ref[...] += dg
        dff, dg = _rms_bwd(ff_ref[...], gfp_ref[...], dh2)
        dff_ref[...] = dff.astype(BF16)
        dgfp_ref[...] += dg

    rowblk = pl.BlockSpec((TS, d), _row)
    vec = pl.BlockSpec((1, d), _fixed)
    return pl.pallas_call(
        body, grid=(s // TS,), name="tail_fwd_bwd",
        out_shape=(jax.ShapeDtypeStruct((s, d), F32), jax.ShapeDtypeStruct((s, d), BF16),
                   jax.ShapeDtypeStruct((s, d), BF16), jax.ShapeDtypeStruct((s, d), BF16),
                   jax.ShapeDtypeStruct((s, d), BF16), jax.ShapeDtypeStruct((s, D_PLE), BF16),
                   jax.ShapeDtypeStruct((8, LANES), F32), jax.ShapeDtypeStruct((1, d), F32),
                   jax.ShapeDtypeStruct((1, d), F32)),
        in_specs=[rowblk, pl.BlockSpec((TS, D_PLE), _row), rowblk, rowblk,
                  pl.BlockSpec(wple.shape, _fixed), pl.BlockSpec(wpg.shape, _fixed), vec, vec],
        out_specs=(rowblk, rowblk, rowblk, rowblk, rowblk, pl.BlockSpec((TS, D_PLE), _row),
                   pl.BlockSpec((8, LANES), _fixed), vec, vec),
        compiler_params=_params(1),
    )(h2, p, tgt, ff, wple, wpg, g_ple, g_ffn_post)


def _ffn_bwd(dff, gate, up, wd, wg, wu, h1, dh2, g_pre):
    s, d = h1.shape
    nc = D_FF // TN_FF
    ts = min(TS_FF, s)

    def body(dff_ref, gate_ref, up_ref, wd_ref, wg_ref, wu_ref, h1_ref, dh2_ref, g_ref,
             dgate_ref, dup_ref, dh1_ref, dg_ref, acc):
        i = pl.program_id(0)
        j = pl.program_id(1)

        @pl.when((i == 0) & (j == 0))
        def _():
            dg_ref[...] = jnp.zeros_like(dg_ref)

        @pl.when(j == 0)
        def _():
            acc[...] = jnp.zeros_like(acc)

        for r in range(2):
            rows = slice(r * (ts // 2), (r + 1) * (ts // 2))
            dact = _nt(dff_ref[rows, :], wd_ref[...])
            gt = gate_ref[rows, :].astype(F32)
            sg = jax.nn.sigmoid(gt)
            dup_ref[rows, :] = (dact * gt * sg).astype(BF16)
            dgate_ref[rows, :] = (dact * up_ref[rows, :].astype(F32) * (sg * (1.0 + gt * (1.0 - sg)))).astype(BF16)
            acc[rows, :] += _nn(dgate_ref[rows, :], wg_ref[...]) + _nn(dup_ref[rows, :], wu_ref[...])

        @pl.when(j == nc - 1)
        def _():
            dv, dg = _rms_bwd(h1_ref[...], g_ref[...], acc[...])
            dh1_ref[...] = dh2_ref[...] + dv
            dg_ref[...] += dg

    rowblk = pl.BlockSpec((ts, d), lambda i, j: (i, 0))
    wblk = pl.BlockSpec((TN_FF, d), lambda i, j: (j, 0))
    chunk = pl.BlockSpec((ts, TN_FF), lambda i, j: (i, j))
    vec = pl.BlockSpec((1, d), lambda i, j: (0, 0))
    return pl.pallas_call(
        body, grid=(s // ts, nc), name="ffn_bwd",
        out_shape=(jax.ShapeDtypeStruct((s, D_FF), BF16), jax.ShapeDtypeStruct((s, D_FF), BF16),
                   jax.ShapeDtypeStruct((s, d), F32), jax.ShapeDtypeStruct((1, d), F32)),
        in_specs=[rowblk, chunk, chunk, wblk, wblk, wblk, rowblk, rowblk, vec],
        out_specs=(chunk, chunk, rowblk, vec),
        scratch_shapes=[pltpu.VMEM((ts, d), F32)],
        compiler_params=_params(2),
    )(dff, gate, up, wd, wg, wu, h1, dh2, g_pre)


def _post_attn_bwd(dh1, o, a, mpre, wout, wpool, g_post, g_attn, g_pool, pscale):
    s, d = dh1.shape
    sub = TS // TQ

    def body(dh1_ref, o_ref, a_ref, mp_ref, wo_ref, wp_ref, gpost_ref, ga_ref, gp_ref, ps_ref,
             dob_ref, dab_ref, dat_ref, dlt_ref, dmpb_ref, dy_ref, dgpost_ref, dga_ref, dgp_ref, dps_ref):
        i = pl.program_id(0)

        @pl.when(i == 0)
        def _():
            dgpost_ref[...] = jnp.zeros_like(dgpost_ref)
            dga_ref[...] = jnp.zeros_like(dga_ref)
            dgp_ref[...] = jnp.zeros_like(dgp_ref)
            dps_ref[...] = jnp.zeros_like(dps_ref)

        do, dg = _rms_bwd(o_ref[...], gpost_ref[...], dh1_ref[...])
        dgpost_ref[...] += dg
        dob = do.astype(BF16)
        dob_ref[...] = dob
        dmix = _nt(dob, wo_ref[...])

        av = a_ref[...]
        da, dg = _rms_bwd(av, ga_ref[...], dmix[:, 0:D_ATTN])
        dga_ref[...] += dg
        dab = da.astype(BF16)
        dab_ref[...] = dab
        dat = dab.T
        hsel = (lax.shift_right_logical(lax.broadcasted_iota(jnp.int32, (HEADS, D_ATTN), 1), 6)
                == lax.broadcasted_iota(jnp.int32, (HEADS, D_ATTN), 0)).astype(F32)
        dlt = lax.dot_general(hsel, da * av, (((1,), (1,)), ((), ())), precision=HIGHEST, preferred_element_type=F32)
        for q in range(sub):
            dlt_ref[q] = dlt[:, q * TQ:(q + 1) * TQ]
            dat_ref[q] = dat[:, q * TQ:(q + 1) * TQ]

        ps = ps_ref[...]
        mp = mp_ref[...]
        dm, dg = _rms_bwd(mp * ps, gp_ref[...], dmix[:, D_ATTN:])
        dgp_ref[...] += dg
        dps_ref[...] += jnp.sum(dm * mp, axis=0, keepdims=True)
        dmpb = (dm * ps).astype(BF16)
        dmpb_ref[...] = dmpb
        for g in range(len(POOL_WINDOWS)):
            cols = slice(g * POOL_CH, (g + 1) * POOL_CH)
            dy_ref[:, cols] = _nt(dmpb[:, cols], wp_ref[g])

    rowblk = pl.BlockSpec((TS, d), _row)
    half = pl.BlockSpec((TS, D_ATTN), _row)
    vec = lambda n: pl.BlockSpec((1, n), _fixed)
    return pl.pallas_call(
        body, grid=(s // TS,), name="post_attn_bwd",
        out_shape=(jax.ShapeDtypeStruct((s, d), BF16), jax.ShapeDtypeStruct((s, D_ATTN), BF16),
                   jax.ShapeDtypeStruct((s // TQ, D_ATTN, TQ), BF16),
                   jax.ShapeDtypeStruct((s // TQ, HEADS, TQ), F32), jax.ShapeDtypeStruct((s, D_POOL), BF16),
                   jax.ShapeDtypeStruct((s, D_POOL), F32), jax.ShapeDtypeStruct((1, d), F32),
                   jax.ShapeDtypeStruct((1, D_ATTN), F32), jax.ShapeDtypeStruct((1, D_POOL), F32),
                   jax.ShapeDtypeStruct((1, D_POOL), F32)),
        in_specs=[rowblk, rowblk, half, half, pl.BlockSpec(wout.shape, _fixed),
                  pl.BlockSpec(wpool.shape, lambda i: (0, 0, 0)), vec(d), vec(D_ATTN), vec(D_POOL), vec(D_POOL)],
        out_specs=(rowblk, half, pl.BlockSpec((sub, D_ATTN, TQ), lambda i: (i, 0, 0)),
                   pl.BlockSpec((sub, HEADS, TQ), lambda i: (i, 0, 0)), half, half,
                   vec(d), vec(D_ATTN), vec(D_POOL), vec(D_POOL)),
        compiler_params=_params(1),
    )(dh1, o, a, mpre, wout, wpool, g_post, g_attn, g_pool, pscale)


def _attn_bwd(ka, v, kt3, qat3, q, do, dot3, lset3, dlt3, chip_blocks):
    s = q.shape[0]
    nq = s // TQ
    wide = HEADS * LANES

    def body(ka_ref, v_ref, kt_ref, qat_ref, q_ref, do_ref, dot_ref, lset_ref, dlt_ref, b_ref,
             dqt_ref, dk_ref, dv_ref, dcs_ref, drs_ref, got_ref, dca, dkw, dvw, pt_scr, ptb_scr, dsb_scr,
             stage, send_sems, recv_sems, local_sem):
        j = pl.program_id(0)

        @pl.when(j == 0)
        def _():
            _chips_start(b_ref, got_ref, stage, send_sems, recv_sems, local_sem)
            dqt_ref[...] = jnp.zeros_like(dqt_ref)
            drs_ref[...] = jnp.zeros_like(drs_ref)

        dkw[...] = jnp.zeros_like(dkw)
        dvw[...] = jnp.zeros_like(dvw)
        dca[...] = jnp.zeros_like(dca)

        def tile(i, masked):
            rows = pl.ds(i * TQ, TQ)
            for h in range(HEADS):
                aug = slice(h * AUG, (h + 1) * AUG)
                st = _nn(ka_ref[:, aug], qat_ref[i, aug, :]) - lset_ref[i, h:h + 1, :]
                if masked:
                    st = jnp.where(_causal_in_tile(), st, NEG)
                pt = jnp.exp(st)
                pt_scr[h] = pt
                ptb_scr[h] = pt.astype(BF16)
            for h in range(HEADS):
                hs = slice(h * HEAD_DIM, (h + 1) * HEAD_DIM)
                half = slice(h * LANES, h * LANES + HEAD_DIM)
                dvw[:, half] += _nn(ptb_scr[h], do_ref[rows, hs])
                dst = pt_scr[h] * (_nn(v_ref[:, hs], dot_ref[i, hs, :]) - dlt_ref[i, h:h + 1, :])
                dsb_scr[h] = dst.astype(BF16)
                drs_ref[i, h, 0:1, :] += jnp.sum(dst, axis=0, keepdims=True)
                dca[:, h * LANES:(h + 1) * LANES] += dst[:, 0:LANES] + dst[:, LANES:2 * LANES]
            for h in range(HEADS):
                hs = slice(h * HEAD_DIM, (h + 1) * HEAD_DIM)
                half = slice(h * LANES, h * LANES + HEAD_DIM)
                dkw[:, half] += _nn(dsb_scr[h], q_ref[rows, hs])
                dqt_ref[i, hs, :] += _nn(kt_ref[0, hs, :], dsb_scr[h])

        def step(i, carry):
            tile(i, False)
            return carry

        tile(j, True)
        lax.fori_loop(j + 1, nq, step, 0)
        lane = lax.broadcasted_iota(jnp.int32, (TQ, LANES), 1)
        dcs_all = jnp.zeros((TQ, LANES), F32)
        for h in range(HEADS):
            hs = slice(h * HEAD_DIM, (h + 1) * HEAD_DIM)
            half = slice(h * LANES, h * LANES + HEAD_DIM)
            dk_ref[:, hs] = dkw[:, half]
            dv_ref[:, hs] = dvw[:, half]
            colsum = jnp.sum(dca[:, h * LANES:(h + 1) * LANES], axis=1, keepdims=True)
            dcs_all = jnp.where(lane == h, colsum, dcs_all)
        dcs_ref[...] = dcs_all

        @pl.when(j == nq - 1)
        def _():
            _chips_finish(b_ref, got_ref, send_sems, recv_sems)

    blk = pl.BlockSpec((TQ, D_ATTN), _row)
    _, r, cdim = chip_blocks.shape
    return pl.pallas_call(
        body, grid=(nq,), name="attn_bwd",
        out_shape=(jax.ShapeDtypeStruct((nq, D_ATTN, TQ), F32), jax.ShapeDtypeStruct((s, D_ATTN), F32),
                   jax.ShapeDtypeStruct((s, D_ATTN), F32), jax.ShapeDtypeStruct((s, LANES), F32),
                   jax.ShapeDtypeStruct((nq, HEADS, 8, TQ), F32),
                   jax.ShapeDtypeStruct(chip_blocks.shape, chip_blocks.dtype)),
        in_specs=[pl.BlockSpec((TQ, HEADS * AUG), _row), blk, pl.BlockSpec((1, D_ATTN, TQ), lambda j: (j, 0, 0)),
                  VMEM_WHOLE, VMEM_WHOLE, VMEM_WHOLE, VMEM_WHOLE, VMEM_WHOLE, VMEM_WHOLE, ANY],
        out_specs=(pl.BlockSpec((nq, D_ATTN, TQ), lambda j: (0, 0, 0)), blk, blk, pl.BlockSpec((TQ, LANES), _row),
                   pl.BlockSpec((nq, HEADS, 8, TQ), lambda j: (0, 0, 0, 0)), ANY),
        scratch_shapes=[pltpu.VMEM((TQ, wide), F32), pltpu.VMEM((TQ, wide), F32), pltpu.VMEM((TQ, wide), F32),
                        pltpu.VMEM((HEADS, TQ, TQ), F32), pltpu.VMEM((HEADS, TQ, TQ), BF16),
                        pltpu.VMEM((HEADS, TQ, TQ), BF16), pltpu.VMEM((r, cdim), chip_blocks.dtype),
                        pltpu.SemaphoreType.DMA((3,)), pltpu.SemaphoreType.DMA((3,)), pltpu.SemaphoreType.DMA],
        compiler_params=_params(1),
    )(ka, v, kt3, qat3, q, do, dot3, lset3, dlt3, chip_blocks)


def _pre_attn_bwd(dqt3, dk, dv, dcs, drs, fl, dy, x, dh1, g1, wqkv, wf, wu):
    s, d = x.shape
    nt = s // TS
    n = TS + HALO
    sub = TS // TQ
    qkv, fcols = 3 * D_ATTN, 3 * D_ATTN + LANES

    def body(dqt_ref, dk_ref, dv_ref, dcs_ref, drs_ref, fl_ref, dy_ref, x_ref, dh1_ref, g_ref, wqkv_ref, wf_ref, wu_ref,
             gx_ref, dz_ref, dg_ref, db_ref, ybuf, ccar, dlog):
        dqkv_ref = dz_ref.at[:, 0:qkv]
        dfb_ref = dz_ref.at[:, qkv:fcols]
        dub_ref = dz_ref.at[:, fcols:]
        i = pl.program_id(0)
        ti = nt - 1 - i

        @pl.when(i == 0)
        def _():
            ybuf[TS:n, :] = jnp.zeros((HALO, D_POOL), F32)
            ccar[...] = jnp.zeros_like(ccar)
            dg_ref[...] = jnp.zeros_like(dg_ref)
            db_ref[...] = jnp.zeros_like(db_ref)

        rr = lax.broadcasted_iota(jnp.int32, (TS, TS), 0)
        cc = lax.broadcasted_iota(jnp.int32, (TS, TS), 1)
        dlog[...] = ccar[...] + _mask_matmul((cc >= rr).astype(BF16), drs_ref[...] - dcs_ref[...])
        ccar[...] = dlog[0:1, :]
        df = dlog[...] * jax.nn.sigmoid(-fl_ref[...])
        db_ref[...] += jnp.sum(df, axis=0, keepdims=True)
        dfb = df.astype(BF16)
        dfb_ref[...] = dfb

        t = ti * TS + lax.broadcasted_iota(jnp.int32, (TS, 1), 0)
        dy = dy_ref[...]
        for g, w in enumerate(POOL_WINDOWS):
            cols = slice(g * POOL_CH, (g + 1) * POOL_CH)
            ybuf[0:TS, cols] = dy[:, cols] / jnp.minimum(t + 1, w).astype(F32)
        for g, w in enumerate(POOL_WINDOWS):
            cols = slice(g * POOL_CH, (g + 1) * POOL_CH)
            sm = ybuf[:, cols]
            step = 1
            while step < w:
                sm = sm + pltpu.roll(sm, n - step, 0)
                step *= 2
            dub_ref[:, cols] = (sm[0:TS, :] - dy[:, cols]).astype(BF16)
        ybuf[TS:n, :] = ybuf[0:HALO, :]

        for a in range(sub):
            dqkv_ref[a * TQ:(a + 1) * TQ, 0:D_ATTN] = (dqt_ref[a].T * 0.125).astype(BF16)
        dqkv_ref[:, D_ATTN:2 * D_ATTN] = dk_ref[...].astype(BF16)
        dqkv_ref[:, 2 * D_ATTN:] = dv_ref[...].astype(BF16)
        dhn = _nn(dqkv_ref[...], wqkv_ref[...]) + _nn(dfb, wf_ref[...]) + _nn(dub_ref[...], wu_ref[...])
        dx, dg = _rms_bwd(x_ref[...], g_ref[...], dhn)
        gx_ref[...] = dh1_ref[...] + dx
        dg_ref[...] += dg

    rev = lambda i: (nt - 1 - i, 0)
    blk = lambda w: pl.BlockSpec((TS, w), rev)
    return pl.pallas_call(
        body, grid=(nt,), name="pre_attn_bwd",
        out_shape=(jax.ShapeDtypeStruct((s, d), F32), jax.ShapeDtypeStruct((s, fcols + D_POOL), BF16),
                   jax.ShapeDtypeStruct((1, d), F32), jax.ShapeDtypeStruct((1, LANES), F32)),
        in_specs=[pl.BlockSpec((sub, D_ATTN, TQ), lambda i: (nt - 1 - i, 0, 0)),
                  blk(D_ATTN), blk(D_ATTN), blk(LANES), blk(LANES), blk(LANES), blk(D_POOL), blk(d), blk(d),
                  pl.BlockSpec((1, d), _fixed), pl.BlockSpec(wqkv.shape, _fixed), pl.BlockSpec(wf.shape, _fixed),
                  pl.BlockSpec(wu.shape, _fixed)],
        out_specs=(blk(d), blk(fcols + D_POOL), pl.BlockSpec((1, d), _fixed), pl.BlockSpec((1, LANES), _fixed)),
        scratch_shapes=[pltpu.VMEM((n, D_POOL), F32), pltpu.VMEM((1, LANES), F32), pltpu.VMEM((TS, LANES), F32)],
        compiler_params=_params(1),
    )(dqt3, dk, dv, dcs, drs, fl, dy, x, dh1, g1, wqkv, wf, wu)


def _wgrad(a, b, out_dtype, name):
    s, m = a.shape
    n = b.shape[1]
    tm = max(t for t in range(LANES, min(m, TM_WGRAD) + 1, LANES) if m % t == 0)
    ts = min(TS_WGRAD, s)
    ns = s // ts

    def body(a_ref, b_ref, o_ref, acc):
        i = pl.program_id(1)

        @pl.when(i == 0)
        def _():
            acc[...] = jnp.zeros_like(acc)

        acc[...] += _tn(a_ref[...], b_ref[...])

        @pl.when(i == ns - 1)
        def _():
            o_ref[...] = acc[...].astype(out_dtype)

    return pl.pallas_call(
        body, grid=(m // tm, ns), name=name, out_shape=jax.ShapeDtypeStruct((m, n), out_dtype),
        in_specs=[pl.BlockSpec((ts, tm), lambda j, i: (i, j)), pl.BlockSpec((ts, n), lambda j, i: (i, 0))],
        out_specs=pl.BlockSpec((tm, n), lambda j, i: (j, 0)),
        scratch_shapes=[pltpu.VMEM((tm, n), F32)],
        compiler_params=_params(2),
    )(a, b)


def _adamw(w, g, m, v):
    m = ADAM_B1 * m + (1.0 - ADAM_B1) * g
    v = ADAM_B2 * v + (1.0 - ADAM_B2) * (g * g)
    m_hat = m / (1.0 - ADAM_B1 ** ADAM_STEP)
    v_hat = v / (1.0 - ADAM_B2 ** ADAM_STEP)
    delta = -ADAM_LR * (m_hat / (jnp.sqrt(v_hat) + ADAM_EPS) + ADAM_WD * w)
    return delta, m, v


def _pair_sum(core, t, theirs, tr, name):
    nk, r, c = theirs.shape

    def body(core_ref, a_ref, b_ref, o_ref):
        o_ref[...] = (a_ref[...].astype(F32) + b_ref[...].astype(F32)).astype(BF16)

    blk = pl.BlockSpec((1, tr, c), lambda k, i, core_ref: (k, i, 0))
    return pl.pallas_call(
        body, name=name, out_shape=jax.ShapeDtypeStruct(theirs.shape, BF16),
        grid_spec=pltpu.PrefetchScalarGridSpec(
            num_scalar_prefetch=1, grid=(nk, r // tr),
            in_specs=[pl.BlockSpec((1, tr, c), lambda k, i, core_ref: (2 * k + core_ref[0], i, 0)), blk],
            out_specs=blk),
        compiler_params=_params(2),
    )(core, t, theirs)


def _sum_update(p_ref, w_ref, m_ref, v_ref, g_ref, d_ref, nm_ref, nv_ref):
    g = p_ref[0].astype(F32)
    for k in range(1, p_ref.shape[0]):
        g = g + p_ref[k].astype(F32)
    g_ref[...] = g
    d_ref[...], nm_ref[...], nv_ref[...] = _adamw(w_ref[...], g, m_ref[...], v_ref[...])


def _reduce_update_rest(parts, w, m, v, chip_blocks, small_block):
    nk, r, c = parts.shape
    ns = r // TR_REST

    def body(p_ref, w_ref, m_ref, v_ref, b_ref, sm_ref, g_ref, d_ref, nm_ref, nv_ref, got_ref, all_ref,
             stage_b, stage_s, send_b, recv_b, local_b, send_s, recv_s, local_s):
        i = pl.program_id(0)

        @pl.when(i == 0)
        def _():
            _chips_start(b_ref, got_ref, stage_b, send_b, recv_b, local_b)
            _gather_start(sm_ref, all_ref, stage_s, send_s, recv_s, local_s)

        _sum_update(p_ref, w_ref, m_ref, v_ref, g_ref, d_ref, nm_ref, nv_ref)

        @pl.when(i == ns - 1)
        def _():
            _gather_pass_on(all_ref, send_s, recv_s)
            _chips_finish(b_ref, got_ref, send_b, recv_b)
            _gather_finish(sm_ref, all_ref, send_s, recv_s)

    blk = pl.BlockSpec((TR_REST, c), _row)
    out = jax.ShapeDtypeStruct((r, c), F32)
    dma = pltpu.SemaphoreType.DMA
    return pl.pallas_call(
        body, grid=(ns,), name="reduce_update_rest",
        out_shape=(out,) * 4 + (jax.ShapeDtypeStruct(chip_blocks.shape, chip_blocks.dtype),
                                jax.ShapeDtypeStruct((N_DEV,) + small_block.shape, small_block.dtype)),
        in_specs=[pl.BlockSpec((nk, TR_REST, c), lambda i: (0, i, 0)), blk, blk, blk, ANY, ANY],
        out_specs=(blk,) * 4 + (ANY, ANY),
        scratch_shapes=[pltpu.VMEM(chip_blocks.shape[1:], chip_blocks.dtype), pltpu.VMEM(small_block.shape, small_block.dtype),
                        dma((3,)), dma((3,)), dma, dma((7,)), dma((7,)), dma],
        compiler_params=_params(1),
    )(parts, w, m, v, chip_blocks, small_block)


def _reduce_update_big(parts, w, m, v, tr, name):
    nk, r, c = parts.shape

    def body(p_ref, w_ref, m_ref, v_ref, g_ref, d_ref, nm_ref, nv_ref):
        _sum_update(p_ref, w_ref, m_ref, v_ref, g_ref, d_ref, nm_ref, nv_ref)

    blk = pl.BlockSpec((tr, c), _row)
    out = jax.ShapeDtypeStruct((r, c), F32)
    return pl.pallas_call(
        body, grid=(r // tr,), name=name, out_shape=(out,) * 4,
        in_specs=[pl.BlockSpec((nk, tr, c), lambda i: (0, i, 0)), blk, blk, blk],
        out_specs=(blk,) * 4, compiler_params=_params(1),
    )(parts, w, m, v)


def _reduce_update_small(parts, w, m, v):
    nd = parts.shape[0]

    def body(p_ref, w_ref, m_ref, v_ref, g_ref, d_ref, nm_ref, nv_ref):
        g = p_ref[0]
        for k in range(1, nd):
            g = g + p_ref[k]
        g_ref[...] = g
        d_ref[...], nm_ref[...], nv_ref[...] = _adamw(w_ref[...], g, m_ref[...], v_ref[...])

    out = jax.ShapeDtypeStruct(w.shape, F32)
    return pl.pallas_call(body, name="reduce_update_small", out_shape=(out,) * 4,
                          compiler_params=pltpu.CompilerParams(vmem_limit_bytes=VMEM_LIMIT))(parts, w, m, v)


MESH = pl.DeviceIdType.MESH


def _copy_through_vmem(src_hbm, dst_hbm, stage, sem):
    load = pltpu.make_async_copy(src_hbm, stage, sem)
    load.start()
    load.wait()
    store = pltpu.make_async_copy(stage, dst_hbm, sem)
    store.start()
    store.wait()


class _GatherPlan:
    def __init__(self, x_ref, out_ref, send_sems, recv_sems):
        x, y, c = lax.axis_index("x"), lax.axis_index("y"), lax.axis_index("c")
        self.me, self.sibling, self.c = (x, y, c), (x, y, 1 - c), c
        self.chips = [(1 - x, y), (x, 1 - y), (1 - x, 1 - y)]
        self.x_ref, self.out_ref, self.send_sems, self.recv_sems = x_ref, out_ref, send_sems, recv_sems

    def slot(self, px, py, pc):
        return self.out_ref.at[4 * px + 2 * py + pc]

    def copy(self, k, block, to, src=None):
        return pltpu.make_async_remote_copy(
            src_ref=self.slot(*block) if src is None else src, dst_ref=self.slot(*block),
            send_sem=self.send_sems.at[k], recv_sem=self.recv_sems.at[k], device_id=to, device_id_type=MESH)

    def first(self):
        return [self.copy(0, self.me, self.sibling, src=self.x_ref)] + [
            self.copy(1 + j, self.me, (*chip, self.c), src=self.x_ref) for j, chip in enumerate(self.chips)]

    def passed(self):
        return [self.copy(4 + j, (*chip, self.c), self.sibling) for j, chip in enumerate(self.chips)]


def _gather_start(x_ref, out_ref, stage, send_sems, recv_sems, local_sem):
    plan = _GatherPlan(x_ref, out_ref, send_sems, recv_sems)
    for cp in plan.first():
        cp.start()
    _copy_through_vmem(x_ref, plan.slot(*plan.me), stage, local_sem)


def _gather_pass_on(out_ref, send_sems, recv_sems):
    plan = _GatherPlan(None, out_ref, send_sems, recv_sems)
    passed = plan.passed()
    for j, chip in enumerate(plan.chips):
        plan.copy(1 + j, (*chip, plan.c), plan.me).wait_recv()
        passed[j].start()


def _gather_finish(x_ref, out_ref, send_sems, recv_sems):
    plan = _GatherPlan(x_ref, out_ref, send_sems, recv_sems)
    plan.copy(0, plan.sibling, plan.me).wait_recv()
    for j, chip in enumerate(plan.chips):
        plan.copy(4 + j, (*chip, 1 - plan.c), plan.me).wait_recv()
    for cp in plan.first() + plan.passed():
        cp.wait_send()


def _all_gather(xs, name):
    r, cdim = xs.shape

    def body(x_ref, out_ref, stage, send_sems, recv_sems, local_sem):
        _gather_start(x_ref, out_ref, stage, send_sems, recv_sems, local_sem)
        _gather_pass_on(out_ref, send_sems, recv_sems)
        _gather_finish(x_ref, out_ref, send_sems, recv_sems)

    return pl.pallas_call(
        body, name=name, out_shape=jax.ShapeDtypeStruct((N_DEV, r, cdim), xs.dtype),
        in_specs=[ANY], out_specs=ANY,
        scratch_shapes=[pltpu.VMEM((r, cdim), xs.dtype), pltpu.SemaphoreType.DMA((7,)), pltpu.SemaphoreType.DMA((7,)),
                        pltpu.SemaphoreType.DMA],
        compiler_params=pltpu.CompilerParams(vmem_limit_bytes=VMEM_LIMIT),
    )(xs)


def _rs_pair(t, name):
    _, r, cdim = t.shape

    def body(t_ref, theirs_ref, send_sems, recv_sems):
        x, y, c = lax.axis_index("x"), lax.axis_index("y"), lax.axis_index("c")
        remote = [pltpu.make_async_remote_copy(
            src_ref=t_ref.at[2 * k + (1 - c)], dst_ref=theirs_ref.at[k],
            send_sem=send_sems.at[k], recv_sem=recv_sems.at[k], device_id=(x, y, 1 - c), device_id_type=MESH)
            for k in range(4)]
        for cp in remote:
            cp.start()
        for cp in remote:
            cp.wait()

    return pl.pallas_call(
        body, name=name, out_shape=jax.ShapeDtypeStruct((4, r, cdim), t.dtype), in_specs=[ANY], out_specs=ANY,
        scratch_shapes=[pltpu.SemaphoreType.DMA((4,)), pltpu.SemaphoreType.DMA((4,))],
    )(t)


def _chips_start(b_ref, out_ref, stage, send_sems, recv_sems, local_sem):
    x, y, c = lax.axis_index("x"), lax.axis_index("y"), lax.axis_index("c")
    mychip = 2 * x + y
    for j, (px, py) in enumerate([(1 - x, y), (x, 1 - y), (1 - x, 1 - y)]):
        pltpu.make_async_remote_copy(
            src_ref=b_ref.at[2 * px + py], dst_ref=out_ref.at[mychip],
            send_sem=send_sems.at[j], recv_sem=recv_sems.at[j], device_id=(px, py, c), device_id_type=MESH).start()
    _copy_through_vmem(b_ref.at[mychip], out_ref.at[mychip], stage, local_sem)


def _chips_finish(b_ref, out_ref, send_sems, recv_sems):
    x, y, c = lax.axis_index("x"), lax.axis_index("y"), lax.axis_index("c")
    for j, (px, py) in enumerate([(1 - x, y), (x, 1 - y), (1 - x, 1 - y)]):
        pltpu.make_async_remote_copy(
            src_ref=b_ref.at[2 * px + py], dst_ref=out_ref.at[2 * px + py],
            send_sem=send_sems.at[j], recv_sem=recv_sems.at[j], device_id=(px, py, c), device_id_type=MESH).wait()


def _pad_rows(a, rows):
    return jnp.pad(a, ((0, rows - a.shape[0]), (0, 0)))


def _pack_in(w_in):
    return _pad_rows(w_in[0].T, ROWS_IN)


def _unpack_in(r):
    return r[0:SHARD_IN].T[None]


def _pack_rest(w_out, w_gate, w_up, w_down, w_ple, w_pg):
    return jnp.concatenate([w_out[0], w_gate[0].T, w_up[0].T, w_down[0], w_ple[0].T.reshape(32, D_MODEL), w_pg[0]],
                           axis=0)


def _unpack_rest(r):
    return (r[0:OFF_GATE][None], r[OFF_GATE:OFF_UP].T[None], r[OFF_UP:OFF_DOWN].T[None],
            r[OFF_DOWN:OFF_PLE][None], r[OFF_PLE:OFF_PG].reshape(128, D_PLE).T[None], r[OFF_PG:ROWS_REST][None])


def _full_rest(g):
    return (g[:, 0:OFF_GATE].reshape(D_MODEL, D_MODEL), g[:, OFF_GATE:OFF_UP].reshape(D_FF, D_MODEL),
            g[:, OFF_UP:OFF_DOWN].reshape(D_FF, D_MODEL), g[:, OFF_DOWN:OFF_PLE].reshape(D_FF, D_MODEL),
            g[:, OFF_PLE:OFF_PG].reshape(D_MODEL, D_PLE), g[:, OFF_PG:ROWS_REST].reshape(D_MODEL, D_MODEL))


def _pack_small(w_pool, g_mix_pre, g_mix_post, g_ffn_pre, g_ffn_post, g_ple, g_attn, g_pool, pool_scale, b_forget,
                loss=None):
    def row(vrow):
        return jnp.pad(vrow.reshape(1, -1), ((0, 0), (0, D_MODEL - vrow.size)))
    rows = [w_pool.reshape(64, D_MODEL), row(g_mix_pre), row(g_mix_post), row(g_ffn_pre), row(g_ffn_post), row(g_ple),
            row(g_attn), row(g_pool), row(pool_scale), row(b_forget),
            row(loss) if loss is not None else jnp.zeros((1, D_MODEL), F32)]
    return _pad_rows(jnp.concatenate(rows, axis=0), SMALL_ROWS)


def _unpack_small(r):
    return dict(
        w_pool=r[0:64].reshape(1, 4, POOL_CH, POOL_CH), g_mix_pre=r[ROW_G_MIX_PRE:ROW_G_MIX_PRE + 1],
        g_mix_post=r[ROW_G_MIX_POST:ROW_G_MIX_POST + 1], g_ffn_pre=r[ROW_G_FFN_PRE:ROW_G_FFN_PRE + 1],
        g_ffn_post=r[ROW_G_FFN_POST:ROW_G_FFN_POST + 1], g_ple=r[ROW_G_PLE:ROW_G_PLE + 1],
        g_attn_grp=r[ROW_G_ATTN:ROW_G_ATTN + 1, 0:D_ATTN], g_pool_grp=r[ROW_G_POOL:ROW_G_POOL + 1, 0:D_POOL],
        pool_scale=r[ROW_POOL_SCALE:ROW_POOL_SCALE + 1, 0:D_POOL], b_forget=r[ROW_B_FORGET:ROW_B_FORGET + 1, 0:HEADS])


def _step(x, p, tgt, small, in_w, in_m, in_v, rest_w, rest_m, rest_v):
    core = lax.axis_index("c").astype(jnp.int32).reshape(1)
    win_t = _all_gather(in_w.astype(BF16), "gather_w_in")[:, 0:SHARD_IN].reshape(D_IN, D_MODEL)
    wqkv = win_t[0:3 * D_ATTN]
    wf = _pad_rows(win_t[3 * D_ATTN:3 * D_ATTN + HEADS], LANES)
    wu = win_t[3 * D_ATTN + HEADS:]
    wpool = small["w_pool"].astype(BF16)
    bpad = jnp.pad(small["b_forget"], ((0, 0), (0, LANES - HEADS)))

    lay = _attn_layout_constants()
    hn, q, ka, v, qat3, vt3, kt3, fl, y, mpre = _pre_attn_fwd(x, small["g_mix_pre"], wqkv, wf, wu, bpad, wpool, lay)
    a, lset3, gathered = _attn_fwd(ka, qat3, vt3, rest_w.astype(BF16))
    wout, wg_t, wu_t, wd, wple_t, wpg = _full_rest(gathered)
    mix, o, h1, hn2 = _post_attn_fwd(a, mpre, x, small["g_attn_grp"], small["g_pool_grp"], small["pool_scale"], wout,
                                     small["g_mix_post"], small["g_ffn_pre"])
    gate, up, act, ff, h2 = _ffn_fwd(hn2, wg_t, wu_t, wd, h1, small["g_ffn_post"])
    dh2, dff, dgl, dpp, h2b, pb, loss8, dg_ple, dg_ffn_post = _tail_fwd_bwd(
        h2, p, tgt, ff, wple_t, wpg, small["g_ple"], small["g_ffn_post"])
    dgate, dup, dh1, dg_ffn_pre = _ffn_bwd(dff, gate, up, wd, wg_t, wu_t, h1, dh2, small["g_ffn_pre"])
    dob, dab, dat3, dlt3, dmpb, dy, dg_mix_post, dg_attn, dg_pool, dps = _post_attn_bwd(
        dh1, o, a, mpre, wout, wpool, small["g_mix_post"], small["g_attn_grp"], small["g_pool_grp"], small["pool_scale"])

    nd = N_DEV
    send_rest = jnp.concatenate([
        _wgrad(mix, dob, BF16, "wgrad_out").reshape(nd, 128, D_MODEL),
        _wgrad(dgate, hn2, BF16, "wgrad_gate").reshape(nd, 352, D_MODEL),
        _wgrad(dup, hn2, BF16, "wgrad_up").reshape(nd, 352, D_MODEL),
        _wgrad(act, dff, BF16, "wgrad_down").reshape(nd, 352, D_MODEL),
        _wgrad(dpp, pb, BF16, "wgrad_ple").reshape(nd, 32, D_MODEL),
        _wgrad(h2b, dgl, BF16, "wgrad_ple_gate").reshape(nd, 128, D_MODEL)], axis=1)
    pair_rest = _pair_sum(core, send_rest, _rs_pair(send_rest, "rs_pair_rest"), TR_REST, "rs_pair_sum_rest")

    dqt3, dk, dv, dcs, drs4, chips_rest = _attn_bwd(ka, v, kt3, qat3, q, dab, dat3, lset3, dlt3, pair_rest)
    drs = jnp.pad(drs4[:, :, 0, :].transpose(0, 2, 1).reshape(-1, HEADS), ((0, 0), (0, LANES - HEADS)))
    gx, dz, dg_mix_pre, db = _pre_attn_bwd(dqt3, dk, dv, dcs, drs, fl, dy, x, dh1, small["g_mix_pre"], wqkv, wf, wu)

    dwz = _wgrad(dz, hn, F32, "wgrad_in")
    dwin_t = jnp.concatenate([dwz[0:3 * D_ATTN], dwz[3 * D_ATTN:3 * D_ATTN + HEADS], dwz[3 * D_ATTN + LANES:]], axis=0)
    send_in = jnp.pad(dwin_t.reshape(nd, SHARD_IN, D_MODEL), ((0, 0), (0, ROWS_IN - SHARD_IN), (0, 0))).astype(BF16)
    pair_in = _pair_sum(core, send_in, _rs_pair(send_in, "rs_pair_in"), ROWS_IN, "rs_pair_sum_in")

    dwp = _wgrad(y, dmpb, F32, "wgrad_pool")
    dw_pool = jnp.stack([dwp[g * POOL_CH:(g + 1) * POOL_CH, g * POOL_CH:(g + 1) * POOL_CH] for g in range(4)])
    small_part = _pack_small(dw_pool, dg_mix_pre, dg_mix_post, dg_ffn_pre, dg_ffn_post, dg_ple, dg_attn, dg_pool, dps,
                             db[:, 0:HEADS], loss8[0:1, 0:1])

    *upd_rest, chips_in, small_all = _reduce_update_rest(chips_rest, rest_w, rest_m, rest_v, pair_in, small_part)
    upd_in = _reduce_update_big(chips_in, in_w, in_m, in_v, ROWS_IN, "reduce_update_in")
    return gx, small_all, upd_in, upd_rest


def kernel(x, p, g_mix_pre, w_in, b_forget, g_attn_grp, g_pool_grp, w_pool, pool_scale, w_out, g_mix_post, g_ffn_pre, w_ffn_gate, w_ffn_up, w_ffn_down, g_ffn_post, w_ple_proj, g_ple, w_ple_gate, loss_target, m_g_mix_pre, m_w_in, m_b_forget, m_g_attn_grp, m_g_pool_grp, m_w_pool, m_pool_scale, m_w_out, m_g_mix_post, m_g_ffn_pre, m_w_ffn_gate, m_w_ffn_up, m_w_ffn_down, m_g_ffn_post, m_w_ple_proj, m_g_ple, m_w_ple_gate, v_g_mix_pre, v_w_in, v_b_forget, v_g_attn_grp, v_g_pool_grp, v_w_pool, v_pool_scale, v_w_out, v_g_mix_post, v_g_ffn_pre, v_w_ffn_gate, v_w_ffn_up, v_w_ffn_down, v_g_ffn_post, v_w_ple_proj, v_g_ple, v_w_ple_gate):
    small = dict(w_pool=w_pool[0], g_mix_pre=g_mix_pre, g_mix_post=g_mix_post, g_ffn_pre=g_ffn_pre,
                 g_ffn_post=g_ffn_post, g_ple=g_ple, g_attn_grp=g_attn_grp, g_pool_grp=g_pool_grp,
                 pool_scale=pool_scale, b_forget=b_forget)
    gx, small_all, upd_in, upd_rest = _step(
        x[0], p[0, 0], loss_target[0], small, _pack_in(w_in), _pack_in(m_w_in), _pack_in(v_w_in),
        _pack_rest(w_out, w_ffn_gate, w_ffn_up, w_ffn_down, w_ple_proj, w_ple_gate),
        _pack_rest(m_w_out, m_w_ffn_gate, m_w_ffn_up, m_w_ffn_down, m_w_ple_proj, m_w_ple_gate),
        _pack_rest(v_w_out, v_w_ffn_gate, v_w_ffn_up, v_w_ffn_down, v_w_ple_proj, v_w_ple_gate))

    sm_w = _pack_small(w_pool, g_mix_pre, g_mix_post, g_ffn_pre, g_ffn_post, g_ple, g_attn_grp, g_pool_grp, pool_scale, b_forget)
    sm_m = _pack_small(m_w_pool, m_g_mix_pre, m_g_mix_post, m_g_ffn_pre, m_g_ffn_post, m_g_ple, m_g_attn_grp, m_g_pool_grp, m_pool_scale, m_b_forget)
    sm_v = _pack_small(v_w_pool, v_g_mix_pre, v_g_mix_post, v_g_ffn_pre, v_g_ffn_post, v_g_ple, v_g_attn_grp, v_g_pool_grp, v_pool_scale, v_b_forget)
    upd_small = _reduce_update_small(small_all, sm_w, sm_m, sm_v)
    loss = upd_small[0][ROW_LOSS, 0]

    def leaves(k):
        b_out, b_gate, b_up, b_down, b_ple, b_pg = _unpack_rest(upd_rest[k])
        s = _unpack_small(upd_small[k])
        return (s["g_mix_pre"], _unpack_in(upd_in[k]), s["b_forget"], s["g_attn_grp"], s["g_pool_grp"], s["w_pool"],
                s["pool_scale"], b_out, s["g_mix_post"], s["g_ffn_pre"], b_gate, b_up, b_down, s["g_ffn_post"], b_ple,
                s["g_ple"], b_pg)

    return (loss, gx[None], *leaves(0), *leaves(1), *leaves(2), *leaves(3))
```

```python
import functools

import jax
import jax.numpy as jnp
from jax import lax
from jax.experimental import pallas as pl
from jax.experimental.pallas import tpu as pltpu

F32 = jnp.float32
BF16 = jnp.bfloat16
HIGHEST = lax.Precision.HIGHEST

D_MODEL = 1024
HEADS = 8
HEAD_DIM = 64
D_ATTN = HEADS * HEAD_DIM
POOL_WINDOWS = (2, 4, 8, 16)
POOL_CH = 128
D_POOL = POOL_CH * len(POOL_WINDOWS)
D_FF = 2816
D_PLE = 256
D_IN = 3 * D_ATTN + HEADS + D_POOL
RMS_EPS = 1e-6
N_DEV = 8

ADAM_LR = 0.001
ADAM_B1 = 0.9
ADAM_B2 = 0.999
ADAM_EPS = 1e-08
ADAM_WD = 0.01
ADAM_STEP = 10

LANES = 128
HALO = 16
TS = 512
TS_FF = 512
TS_WGRAD = 1024
TM_WGRAD = 2176
TQ = 256
TN_FF = 1408
NEG = -1e30
VMEM_LIMIT = 56 * 1024 * 1024

SHARD_IN = 257
ROWS_IN = 272
SHARD_FF = 352
OFF_PG = 128
OFF_PLE = 256
OFF_GATE = SHARD_FF
OFF_UP = 2 * SHARD_FF
OFF_DOWN = 3 * SHARD_FF
ROWS_REST = 4 * SHARD_FF
TR_REST = SHARD_FF

SMALL_ROWS = 80
ROW_G_MIX_PRE, ROW_G_MIX_POST, ROW_G_FFN_PRE, ROW_G_FFN_POST, ROW_G_PLE = 64, 65, 66, 67, 68
ROW_G_ATTN, ROW_G_POOL, ROW_POOL_SCALE, ROW_B_FORGET, ROW_LOSS = 69, 70, 71, 72, 73


def _nn(a, b):
    return jnp.dot(a, b, preferred_element_type=F32)


def _nt(a, b):
    return lax.dot_general(a, b, (((1,), (1,)), ((), ())), preferred_element_type=F32)


def _tn(a, b):
    return lax.dot_general(a, b, (((0,), (0,)), ((), ())), preferred_element_type=F32)


def _rstd(v):
    return lax.rsqrt(jnp.mean(v * v, axis=-1, keepdims=True) + RMS_EPS)


def _rms_bwd(v, g, dy):
    r = _rstd(v)
    vh = v * r
    t = dy * g
    dv = r * (t - vh * jnp.mean(t * vh, axis=-1, keepdims=True))
    return dv, jnp.sum(dy * vh, axis=0, keepdims=True)


def _split3(v):
    hi = v.astype(BF16)
    rest = v - hi.astype(F32)
    mid = rest.astype(BF16)
    return hi, mid, (rest - mid.astype(F32)).astype(BF16)


def _mask_matmul(mask, v):
    hi, mid, lo = _split3(v)
    return _nn(mask, lo) + _nn(mask, mid) + _nn(mask, hi)


def _params(n_grid):
    return pltpu.CompilerParams(dimension_semantics=("arbitrary",) * n_grid, vmem_limit_bytes=VMEM_LIMIT)


def _row(i):
    return (i, 0)


def _fixed(*_):
    return (0, 0)


def _spec_square(part):
    return pl.BlockSpec((N_DEV, 128, D_MODEL), lambda *_: (0, part, 0))


def _spec_ff(part):
    return pl.BlockSpec((TN_FF // SHARD_FF, SHARD_FF, D_MODEL), lambda i, j: (j, part, 0))


assert TS == 2 * TQ and TN_FF % SHARD_FF == 0
_HALVES = (slice(0, TQ), slice(TQ, TS))

VMEM_WHOLE = pl.BlockSpec(memory_space=pltpu.VMEM)
SMEM_WHOLE = pl.BlockSpec(memory_space=pltpu.SMEM)
ANY = pl.BlockSpec(memory_space=pl.ANY)


AUG = 128
BIAS_LANE = HEAD_DIM
ONE_LANE = HEAD_DIM + 3
SPARE_LANE = HEADS


def _attn_layout_constants():
    import numpy as np
    place = np.zeros((D_ATTN, HEADS * AUG), np.float32)
    for r in range(D_ATTN):
        place[r, (r // HEAD_DIM) * AUG + r % HEAD_DIM] = 1.0
    bias_k = np.zeros((3, LANES, HEADS * AUG), np.float32)
    bias_q = np.zeros((3, LANES, HEADS * AUG), np.float32)
    for h in range(HEADS):
        for part in range(3):
            bias_k[part, h, h * AUG + BIAS_LANE + part] = -1.0
            bias_q[part, h, h * AUG + ONE_LANE + part] = 1.0
            bias_k[0, SPARE_LANE, h * AUG + ONE_LANE + part] = 1.0
            bias_q[0, SPARE_LANE, h * AUG + BIAS_LANE + part] = 1.0
    as_bf = lambda a: jnp.asarray(a, BF16)
    return dict(place=as_bf(place), place_t=as_bf(place.T), bias_k=as_bf(bias_k),
                bias_q_t=as_bf(bias_q.transpose(0, 2, 1)))


def _pre_attn_fwd(x, g1, wqkv, wf, wu, bpad, wpool, lay):
    s, d = x.shape
    nt = s // TS
    sub = TS // TQ

    def body(x_ref, g_ref, wqkv_ref, wf_ref, wu_ref, b_ref, wp_ref, place_ref, place_t_ref, bk_ref, bqt_ref,
             hn_ref, q_ref, ka_ref, v_ref, qat_ref, vt_ref, kt_ref, fl_ref, y_ref, mp_ref, ubuf, ccar, cbuf):
        i = pl.program_id(0)

        @pl.when(i == 0)
        def _():
            ubuf[0:HALO, :] = jnp.zeros((HALO, D_POOL), F32)
            ccar[...] = jnp.zeros_like(ccar)

        xv = x_ref[...]
        hn = (xv * _rstd(xv) * g_ref[...]).astype(BF16)
        hn_ref[...] = hn
        zq = _nt(hn, wqkv_ref[...])
        qb = (zq[:, 0:D_ATTN] * 0.125).astype(BF16)
        kb = zq[:, D_ATTN:2 * D_ATTN].astype(BF16)
        vb = zq[:, 2 * D_ATTN:3 * D_ATTN].astype(BF16)
        q_ref[...] = qb
        v_ref[...] = vb

        fl = _nt(hn, wf_ref[...]) + b_ref[...]
        fl_ref[...] = fl
        logf = jax.nn.log_sigmoid(fl)
        rr = lax.broadcasted_iota(jnp.int32, (TS, TS), 0)
        cc = lax.broadcasted_iota(jnp.int32, (TS, TS), 1)
        c = _mask_matmul((cc <= rr).astype(BF16), logf) + ccar[...]
        cbuf[...] = c
        ccar[...] = cbuf[TS - 1:TS, :]
        hi, mid, lo = _split3(c)
        lane = lax.broadcasted_iota(jnp.int32, (TS, LANES), 1)
        parts = (jnp.where(lane == SPARE_LANE, 1.0, hi).astype(BF16), mid, lo)
        ka = _nn(kb, place_ref[...])
        qat = _nt(place_t_ref[...], qb)
        for part in range(3):
            ka = ka + _nn(parts[part], bk_ref[part])
            qat = qat + _nt(bqt_ref[part], parts[part])
        ka_ref[...] = ka.astype(BF16)
        qat = qat.astype(BF16)
        vt = vb.T
        kt = kb.T
        for a in range(sub):
            qat_ref[a] = qat[:, a * TQ:(a + 1) * TQ]
            vt_ref[a] = vt[:, a * TQ:(a + 1) * TQ]
            kt_ref[a] = kt[:, a * TQ:(a + 1) * TQ]

        u = _nt(hn, wu_ref[...])
        ubuf[HALO:HALO + TS, :] = u
        t = i * TS + lax.broadcasted_iota(jnp.int32, (TS, 1), 0)
        for g, w in enumerate(POOL_WINDOWS):
            cols = slice(g * POOL_CH, (g + 1) * POOL_CH)
            sm = ubuf[:, cols]
            step = 1
            while step < w:
                sm = sm + pltpu.roll(sm, step, 0)
                step *= 2
            cnt = jnp.minimum(t + 1, w).astype(F32)
            yg = (sm[HALO:, :] / cnt - u[:, cols]).astype(BF16)
            y_ref[:, cols] = yg
            mp_ref[:, cols] = _nn(yg, wp_ref[g])
        ubuf[0:HALO, :] = u[TS - HALO:, :]

    nq = s // TQ
    aug = HEADS * AUG
    outs = (
        jax.ShapeDtypeStruct((s, d), BF16), jax.ShapeDtypeStruct((s, D_ATTN), BF16),
        jax.ShapeDtypeStruct((s, aug), BF16), jax.ShapeDtypeStruct((s, D_ATTN), BF16),
        jax.ShapeDtypeStruct((nq, aug, TQ), BF16), jax.ShapeDtypeStruct((nq, D_ATTN, TQ), BF16),
        jax.ShapeDtypeStruct((nq, D_ATTN, TQ), BF16),
        jax.ShapeDtypeStruct((s, LANES), F32),
        jax.ShapeDtypeStruct((s, D_POOL), BF16), jax.ShapeDtypeStruct((s, D_POOL), F32),
    )
    fixed3 = lambda i: (0, 0, 0)
    tiles3 = lambda rows: pl.BlockSpec((sub, rows, TQ), lambda i: (i, 0, 0))
    return pl.pallas_call(
        body, grid=(nt,), out_shape=outs, name="pre_attn_fwd",
        in_specs=[pl.BlockSpec((TS, d), _row), pl.BlockSpec((1, d), _fixed),
                  pl.BlockSpec(wqkv.shape, _fixed), pl.BlockSpec(wf.shape, _fixed), pl.BlockSpec(wu.shape, _fixed),
                  pl.BlockSpec((1, LANES), _fixed), pl.BlockSpec(wpool.shape, fixed3),
                  pl.BlockSpec(lay["place"].shape, _fixed), pl.BlockSpec(lay["place_t"].shape, _fixed),
                  pl.BlockSpec(lay["bias_k"].shape, fixed3), pl.BlockSpec(lay["bias_q_t"].shape, fixed3)],
        out_specs=(pl.BlockSpec((TS, d), _row), pl.BlockSpec((TS, D_ATTN), _row),
                   pl.BlockSpec((TS, aug), _row), pl.BlockSpec((TS, D_ATTN), _row),
                   tiles3(aug), tiles3(D_ATTN), tiles3(D_ATTN),
                   pl.BlockSpec((TS, LANES), _row),
                   pl.BlockSpec((TS, D_POOL), _row), pl.BlockSpec((TS, D_POOL), _row)),
        scratch_shapes=[pltpu.VMEM((TS + HALO, D_POOL), F32), pltpu.VMEM((1, LANES), F32), pltpu.VMEM((TS, LANES), F32)],
        compiler_params=_params(1),
    )(x, g1, wqkv, wf, wu, bpad, wpool, lay["place"], lay["place_t"], lay["bias_k"], lay["bias_q_t"])


def _causal_in_tile():
    krow = lax.broadcasted_iota(jnp.int32, (TQ, TQ), 0)
    qcol = lax.broadcasted_iota(jnp.int32, (TQ, TQ), 1)
    return krow <= qcol


def _attn_fwd(ka, qat3, vt3, own_block):
    s = ka.shape[0]
    nq = s // TQ
    pass_on_step = max(nq - 2, 0)

    def body(qa_ref, ka_ref, vt_ref, own_ref, a_ref, lset_ref, all_ref, acc, st_scr, pt_scr,
             stage, send_sems, recv_sems, local_sem):
        i = pl.program_id(0)

        @pl.when(i == 0)
        def _():
            _gather_start(own_ref, all_ref, stage, send_sems, recv_sems, local_sem)

        @pl.when(i == pass_on_step)
        def _():
            _gather_pass_on(all_ref, send_sems, recv_sems)

        acc[...] = jnp.zeros_like(acc)

        def tile(j, stats, masked):
            tile_max = []
            for h in range(HEADS):
                aug = slice(h * AUG, (h + 1) * AUG)
                st = _nn(ka_ref[pl.ds(j * TQ, TQ), aug], qa_ref[0, aug, :])
                if masked:
                    st = jnp.where(_causal_in_tile(), st, NEG)
                st_scr[h] = st
                tile_max.append(jnp.max(st, axis=0, keepdims=True))
            new, scale = [], []
            for h in range(HEADS):
                m_old, l_old = stats[h]
                m_new = jnp.maximum(m_old, tile_max[h])
                al = jnp.exp(m_old - m_new)
                pt = jnp.exp(st_scr[h] - m_new)
                pt_scr[h] = pt.astype(BF16)
                new.append((m_new, al * l_old + jnp.sum(pt, axis=0, keepdims=True)))
                scale.append(al)
            for h in range(HEADS):
                rows = slice(h * HEAD_DIM, (h + 1) * HEAD_DIM)
                acc[rows, :] = scale[h] * acc[rows, :] + _nn(vt_ref[j, rows, :], pt_scr[h])
            return tuple(new)

        init = tuple((jnp.full((1, TQ), NEG, F32), jnp.zeros((1, TQ), F32)) for _ in range(HEADS))
        stats = lax.fori_loop(0, i, functools.partial(tile, masked=False), init)
        stats = tile(i, stats, True)
        for h in range(HEADS):
            rows = slice(h * HEAD_DIM, (h + 1) * HEAD_DIM)
            acc[rows, :] = acc[rows, :] / stats[h][1]
            lset_ref[0, h:h + 1, :] = stats[h][0] + jnp.log(stats[h][1])
        a_ref[...] = acc[...].T

        @pl.when(i == nq - 1)
        def _():
            _gather_finish(own_ref, all_ref, send_sems, recv_sems)

    r, cdim = own_block.shape
    return pl.pallas_call(
        body, grid=(nq,), name="attn_fwd",
        out_shape=(jax.ShapeDtypeStruct((s, D_ATTN), F32), jax.ShapeDtypeStruct((nq, HEADS, TQ), F32),
                   jax.ShapeDtypeStruct((N_DEV, r, cdim), own_block.dtype)),
        in_specs=[pl.BlockSpec((1, HEADS * AUG, TQ), lambda i: (i, 0, 0)), VMEM_WHOLE, VMEM_WHOLE, ANY],
        out_specs=(pl.BlockSpec((TQ, D_ATTN), _row), pl.BlockSpec((1, HEADS, TQ), lambda i: (i, 0, 0)), ANY),
        scratch_shapes=[pltpu.VMEM((D_ATTN, TQ), F32), pltpu.VMEM((HEADS, TQ, TQ), F32), pltpu.VMEM((HEADS, TQ, TQ), BF16),
                        pltpu.VMEM((r, cdim), own_block.dtype),
                        pltpu.SemaphoreType.DMA((7,)), pltpu.SemaphoreType.DMA((7,)), pltpu.SemaphoreType.DMA],
        compiler_params=_params(1),
    )(qat3, ka, vt3, own_block)


def _post_attn_fwd(a, mpre, x, g_attn, g_pool, pscale, wout, g_post, g_ffn_pre):
    s, d = x.shape

    def body(a_ref, mp_ref, x_ref, ga_ref, gp_ref, ps_ref, wo_ref, gpost_ref, gpre_ref,
             mix_ref, o_ref, h1_ref, hn2_ref):
        for rows in _HALVES:
            av = a_ref[rows, :]
            mix_ref[rows, 0:D_ATTN] = (av * _rstd(av) * ga_ref[...]).astype(BF16)
            mv = mp_ref[rows, :] * ps_ref[...]
            mix_ref[rows, D_ATTN:] = (mv * _rstd(mv) * gp_ref[...]).astype(BF16)
            o = _nn(mix_ref[rows, :], wo_ref[...].reshape(d, d))
            o_ref[rows, :] = o
            h1 = x_ref[rows, :] + o * _rstd(o) * gpost_ref[...]
            h1_ref[rows, :] = h1
            hn2_ref[rows, :] = (h1 * _rstd(h1) * gpre_ref[...]).astype(BF16)

    vec = lambda n: pl.BlockSpec((1, n), _fixed)
    return pl.pallas_call(
        body, grid=(s // TS,), name="post_attn_fwd",
        out_shape=(jax.ShapeDtypeStruct((s, d), BF16), jax.ShapeDtypeStruct((s, d), F32),
                   jax.ShapeDtypeStruct((s, d), F32), jax.ShapeDtypeStruct((s, d), BF16)),
        in_specs=[pl.BlockSpec((TS, D_ATTN), _row), pl.BlockSpec((TS, D_POOL), _row), pl.BlockSpec((TS, d), _row),
                  vec(D_ATTN), vec(D_POOL), vec(D_POOL), _spec_square(0), vec(d), vec(d)],
        out_specs=(pl.BlockSpec((TS, d), _row),) * 4,
        compiler_params=_params(1),
    )(a, mpre, x, g_attn, g_pool, pscale, wout, g_post, g_ffn_pre)


def _ffn_fwd(hn2, wg, wu, wd, h1, g_post):
    s, d = h1.shape
    nc = D_FF // TN_FF
    ts = min(TS_FF, s)

    def body(hn_ref, wg_ref, wu_ref, wd_ref, h1_ref, g_ref, gate_ref, up_ref, act_ref, ff_ref, h2_ref, acc):
        j = pl.program_id(1)

        @pl.when(j == 0)
        def _():
            acc[...] = jnp.zeros_like(acc)

        for r in range(2):
            rows = slice(r * (ts // 2), (r + 1) * (ts // 2))
            hn = hn_ref[rows, :]
            gt = _nt(hn, wg_ref[...].reshape(TN_FF, d))
            up = _nt(hn, wu_ref[...].reshape(TN_FF, d))
            gate_ref[rows, :] = gt.astype(BF16)
            up_ref[rows, :] = up.astype(BF16)
            act_ref[rows, :] = (gt * jax.nn.sigmoid(gt) * up).astype(BF16)
            acc[rows, :] += _nn(act_ref[rows, :], wd_ref[...].reshape(TN_FF, d))

        @pl.when(j == nc - 1)
        def _():
            ff = acc[...]
            ff_ref[...] = ff
            h2_ref[...] = h1_ref[...] + ff * _rstd(ff) * g_ref[...]

    rowblk = pl.BlockSpec((ts, d), lambda i, j: (i, 0))
    chunk = pl.BlockSpec((ts, TN_FF), lambda i, j: (i, j))
    return pl.pallas_call(
        body, grid=(s // ts, nc), name="ffn_fwd",
        out_shape=(jax.ShapeDtypeStruct((s, D_FF), BF16),) * 3 + (jax.ShapeDtypeStruct((s, d), F32),) * 2,
        in_specs=[rowblk, _spec_ff(1), _spec_ff(2), _spec_ff(3), rowblk, pl.BlockSpec((1, d), lambda i, j: (0, 0))],
        out_specs=(chunk, chunk, chunk, rowblk, rowblk),
        scratch_shapes=[pltpu.VMEM((ts, d), F32)],
        compiler_params=_params(2),
    )(hn2, wg, wu, wd, h1, g_post)


def _tail_fwd_bwd(h2, p, tgt, ff, wple, wpg, g_ple, g_ffn_post):
    s, d = h2.shape

    def body(h2_ref, p_ref, t_ref, ff_ref, wple_ref, wpg_ref, gple_ref, gfp_ref,
             dh2_ref, dff_ref, dgl_ref, dpp_ref, h2b_ref, pb_ref, loss_ref, dgple_ref, dgfp_ref):
        i = pl.program_id(0)

        @pl.when(i == 0)
        def _():
            loss_ref[...] = jnp.zeros_like(loss_ref)
            dgple_ref[...] = jnp.zeros_like(dgple_ref)
            dgfp_ref[...] = jnp.zeros_like(dgfp_ref)

        h2 = h2_ref[...]
        h2b = h2.astype(BF16)
        h2b_ref[...] = h2b
        pb = p_ref[...].astype(BF16)
        pb_ref[...] = pb
        pp = _nt(pb, wple_ref[...])
        gple = gple_ref[...]
        e = pp * _rstd(pp) * gple
        wpg = wpg_ref[...].reshape(d, d)
        sg = jax.nn.sigmoid(_nn(h2b, wpg))
        diff = h2 + sg * e - t_ref[...]
        sq = jnp.sum(jnp.sum(diff * diff, axis=1, keepdims=True), axis=0, keepdims=True)
        loss_ref[...] += jnp.broadcast_to(sq * (0.5 / d), loss_ref.shape)
        dh3 = diff * (1.0 / d)
        dgl = (dh3 * e * sg * (1.0 - sg)).astype(BF16)
        dgl_ref[...] = dgl
        dh2 = dh3 + _nt(dgl, wpg)
        dh2_ref[...] = dh2
        dpp, dg = _rms_bwd(pp, gple, dh3 * sg)
        dpp_ref[...] = dpp.astype(BF16)
        dgple_ref[...] += dg
        dff, dg = _rms_bwd(ff_ref[...], gfp_ref[...], dh2)
        dff_ref[...] = dff.astype(BF16)
        dgfp_ref[...] += dg

    rowblk = pl.BlockSpec((TS, d), _row)
    vec = pl.BlockSpec((1, d), _fixed)
    return pl.pallas_call(
        body, grid=(s // TS,), name="tail_fwd_bwd",
        out_shape=(jax.ShapeDtypeStruct((s, d), F32), jax.ShapeDtypeStruct((s, d), BF16),
                   jax.ShapeDtypeStruct((s, d), BF16), jax.ShapeDtypeStruct((s, d), BF16),
                   jax.ShapeDtypeStruct((s, d), BF16), jax.ShapeDtypeStruct((s, D_PLE), BF16),
                   jax.ShapeDtypeStruct((8, LANES), F32), jax.ShapeDtypeStruct((1, d), F32),
                   jax.ShapeDtypeStruct((1, d), F32)),
        in_specs=[rowblk, pl.BlockSpec((TS, D_PLE), _row), rowblk, rowblk,
                  pl.BlockSpec(wple.shape, _fixed), _spec_square(1), vec, vec],
        out_specs=(rowblk, rowblk, rowblk, rowblk, rowblk, pl.BlockSpec((TS, D_PLE), _row),
                   pl.BlockSpec((8, LANES), _fixed), vec, vec),
        compiler_params=_params(1),
    )(h2, p, tgt, ff, wple, wpg, g_ple, g_ffn_post)


def _ffn_bwd(dff, gate, up, wd, wg, wu, h1, dh2, g_pre):
    s, d = h1.shape
    nc = D_FF // TN_FF
    ts = min(TS_FF, s)

    def body(dff_ref, gate_ref, up_ref, wd_ref, wg_ref, wu_ref, h1_ref, dh2_ref, g_ref,
             dgate_ref, dup_ref, dh1_ref, dg_ref, acc):
        i = pl.program_id(0)
        j = pl.program_id(1)

        @pl.when((i == 0) & (j == 0))
        def _():
            dg_ref[...] = jnp.zeros_like(dg_ref)

        @pl.when(j == 0)
        def _():
            acc[...] = jnp.zeros_like(acc)

        for r in range(2):
            rows = slice(r * (ts // 2), (r + 1) * (ts // 2))
            dact = _nt(dff_ref[rows, :], wd_ref[...].reshape(TN_FF, d))
            gt = gate_ref[rows, :].astype(F32)
            sg = jax.nn.sigmoid(gt)
            dup_ref[rows, :] = (dact * gt * sg).astype(BF16)
            dgate_ref[rows, :] = (dact * up_ref[rows, :].astype(F32) * (sg * (1.0 + gt * (1.0 - sg)))).astype(BF16)
            acc[rows, :] += (_nn(dgate_ref[rows, :], wg_ref[...].reshape(TN_FF, d))
                             + _nn(dup_ref[rows, :], wu_ref[...].reshape(TN_FF, d)))

        @pl.when(j == nc - 1)
        def _():
            dv, dg = _rms_bwd(h1_ref[...], g_ref[...], acc[...])
            dh1_ref[...] = dh2_ref[...] + dv
            dg_ref[...] += dg

    rowblk = pl.BlockSpec((ts, d), lambda i, j: (i, 0))
    chunk = pl.BlockSpec((ts, TN_FF), lambda i, j: (i, j))
    vec = pl.BlockSpec((1, d), lambda i, j: (0, 0))
    return pl.pallas_call(
        body, grid=(s // ts, nc), name="ffn_bwd",
        out_shape=(jax.ShapeDtypeStruct((s, D_FF), BF16), jax.ShapeDtypeStruct((s, D_FF), BF16),
                   jax.ShapeDtypeStruct((s, d), F32), jax.ShapeDtypeStruct((1, d), F32)),
        in_specs=[rowblk, chunk, chunk, _spec_ff(3), _spec_ff(1), _spec_ff(2), rowblk, rowblk, vec],
        out_specs=(chunk, chunk, rowblk, vec),
        scratch_shapes=[pltpu.VMEM((ts, d), F32)],
        compiler_params=_params(2),
    )(dff, gate, up, wd, wg, wu, h1, dh2, g_pre)


def _post_attn_bwd(dh1, o, a, mpre, wout, wpool, g_post, g_attn, g_pool, pscale):
    s, d = dh1.shape
    sub = TS // TQ

    def body(dh1_ref, o_ref, a_ref, mp_ref, wo_ref, wp_ref, gpost_ref, ga_ref, gp_ref, ps_ref,
             dob_ref, dab_ref, dat_ref, dlt_ref, dmpb_ref, dy_ref, dgpost_ref, dga_ref, dgp_ref, dps_ref):
        i = pl.program_id(0)

        @pl.when(i == 0)
        def _():
            dgpost_ref[...] = jnp.zeros_like(dgpost_ref)
            dga_ref[...] = jnp.zeros_like(dga_ref)
            dgp_ref[...] = jnp.zeros_like(dgp_ref)
            dps_ref[...] = jnp.zeros_like(dps_ref)

        do, dg = _rms_bwd(o_ref[...], gpost_ref[...], dh1_ref[...])
        dgpost_ref[...] += dg
        dob = do.astype(BF16)
        dob_ref[...] = dob
        dmix = _nt(dob, wo_ref[...].reshape(d, d))

        av = a_ref[...]
        da, dg = _rms_bwd(av, ga_ref[...], dmix[:, 0:D_ATTN])
        dga_ref[...] += dg
        dab = da.astype(BF16)
        dab_ref[...] = dab
        dat = dab.T
        hsel = (lax.shift_right_logical(lax.broadcasted_iota(jnp.int32, (HEADS, D_ATTN), 1), 6)
                == lax.broadcasted_iota(jnp.int32, (HEADS, D_ATTN), 0)).astype(F32)
        dlt = lax.dot_general(hsel, da * av, (((1,), (1,)), ((), ())), precision=HIGHEST, preferred_element_type=F32)
        for q in range(sub):
            dlt_ref[q] = dlt[:, q * TQ:(q + 1) * TQ]
            dat_ref[q] = dat[:, q * TQ:(q + 1) * TQ]

        ps = ps_ref[...]
        mp = mp_ref[...]
        dm, dg = _rms_bwd(mp * ps, gp_ref[...], dmix[:, D_ATTN:])
        dgp_ref[...] += dg
        dps_ref[...] += jnp.sum(dm * mp, axis=0, keepdims=True)
        dmpb = (dm * ps).astype(BF16)
        dmpb_ref[...] = dmpb
        for g in range(len(POOL_WINDOWS)):
            cols = slice(g * POOL_CH, (g + 1) * POOL_CH)
            dy_ref[:, cols] = _nt(dmpb[:, cols], wp_ref[g])

    rowblk = pl.BlockSpec((TS, d), _row)
    half = pl.BlockSpec((TS, D_ATTN), _row)
    vec = lambda n: pl.BlockSpec((1, n), _fixed)
    return pl.pallas_call(
        body, grid=(s // TS,), name="post_attn_bwd",
        out_shape=(jax.ShapeDtypeStruct((s, d), BF16), jax.ShapeDtypeStruct((s, D_ATTN), BF16),
                   jax.ShapeDtypeStruct((s // TQ, D_ATTN, TQ), BF16),
                   jax.ShapeDtypeStruct((s // TQ, HEADS, TQ), F32), jax.ShapeDtypeStruct((s, D_POOL), BF16),
                   jax.ShapeDtypeStruct((s, D_POOL), F32), jax.ShapeDtypeStruct((1, d), F32),
                   jax.ShapeDtypeStruct((1, D_ATTN), F32), jax.ShapeDtypeStruct((1, D_POOL), F32),
                   jax.ShapeDtypeStruct((1, D_POOL), F32)),
        in_specs=[rowblk, rowblk, half, half, _spec_square(0),
                  pl.BlockSpec(wpool.shape, lambda i: (0, 0, 0)), vec(d), vec(D_ATTN), vec(D_POOL), vec(D_POOL)],
        out_specs=(rowblk, half, pl.BlockSpec((sub, D_ATTN, TQ), lambda i: (i, 0, 0)),
                   pl.BlockSpec((sub, HEADS, TQ), lambda i: (i, 0, 0)), half, half,
                   vec(d), vec(D_ATTN), vec(D_POOL), vec(D_POOL)),
        compiler_params=_params(1),
    )(dh1, o, a, mpre, wout, wpool, g_post, g_attn, g_pool, pscale)


def _attn_bwd(ka, v, kt3, qat3, q, do, dot3, lset3, dlt3, chip_blocks):
    s = q.shape[0]
    nq = s // TQ
    wide = HEADS * LANES

    def body(ka_ref, v_ref, kt_ref, qat_ref, q_ref, do_ref, dot_ref, lset_ref, dlt_ref, b_ref,
             dqt_ref, dk_ref, dv_ref, dcs_ref, drs_ref, got_ref, dca, dkw, dvw, pt_scr, ptb_scr, dsb_scr,
             stage, send_sems, recv_sems, local_sem):
        j = pl.program_id(0)

        @pl.when(j == 0)
        def _():
            _chips_start(b_ref, got_ref, stage, send_sems, recv_sems, local_sem)
            dqt_ref[...] = jnp.zeros_like(dqt_ref)
            drs_ref[...] = jnp.zeros_like(drs_ref)

        dkw[...] = jnp.zeros_like(dkw)
        dvw[...] = jnp.zeros_like(dvw)
        dca[...] = jnp.zeros_like(dca)

        def tile(i, masked):
            rows = pl.ds(i * TQ, TQ)
            for h in range(HEADS):
                aug = slice(h * AUG, (h + 1) * AUG)
                st = _nn(ka_ref[:, aug], qat_ref[i, aug, :]) - lset_ref[i, h:h + 1, :]
                if masked:
                    st = jnp.where(_causal_in_tile(), st, NEG)
                pt = jnp.exp(st)
                pt_scr[h] = pt
                ptb_scr[h] = pt.astype(BF16)
            for h in range(HEADS):
                hs = slice(h * HEAD_DIM, (h + 1) * HEAD_DIM)
                half = slice(h * LANES, h * LANES + HEAD_DIM)
                dvw[:, half] += _nn(ptb_scr[h], do_ref[rows, hs])
                dst = pt_scr[h] * (_nn(v_ref[:, hs], dot_ref[i, hs, :]) - dlt_ref[i, h:h + 1, :])
                dsb_scr[h] = dst.astype(BF16)
                drs_ref[i, h, 0:1, :] += jnp.sum(dst, axis=0, keepdims=True)
                dca[:, h * LANES:(h + 1) * LANES] += dst[:, 0:LANES] + dst[:, LANES:2 * LANES]
            for h in range(HEADS):
                hs = slice(h * HEAD_DIM, (h + 1) * HEAD_DIM)
                half = slice(h * LANES, h * LANES + HEAD_DIM)
                dkw[:, half] += _nn(dsb_scr[h], q_ref[rows, hs])
                dqt_ref[i, hs, :] += _nn(kt_ref[0, hs, :], dsb_scr[h])

        def step(i, carry):
            tile(i, False)
            return carry

        tile(j, True)
        lax.fori_loop(j + 1, nq, step, 0)
        lane = lax.broadcasted_iota(jnp.int32, (TQ, LANES), 1)
        dcs_all = jnp.zeros((TQ, LANES), F32)
        for h in range(HEADS):
            hs = slice(h * HEAD_DIM, (h + 1) * HEAD_DIM)
            half = slice(h * LANES, h * LANES + HEAD_DIM)
            dk_ref[:, hs] = dkw[:, half]
            dv_ref[:, hs] = dvw[:, half]
            colsum = jnp.sum(dca[:, h * LANES:(h + 1) * LANES], axis=1, keepdims=True)
            dcs_all = jnp.where(lane == h, colsum, dcs_all)
        dcs_ref[...] = dcs_all

        @pl.when(j == nq - 1)
        def _():
            _chips_finish(b_ref, got_ref, send_sems, recv_sems)

    blk = pl.BlockSpec((TQ, D_ATTN), _row)
    _, r, cdim = chip_blocks.shape
    return pl.pallas_call(
        body, grid=(nq,), name="attn_bwd",
        out_shape=(jax.ShapeDtypeStruct((nq, D_ATTN, TQ), F32), jax.ShapeDtypeStruct((s, D_ATTN), F32),
                   jax.ShapeDtypeStruct((s, D_ATTN), F32), jax.ShapeDtypeStruct((s, LANES), F32),
                   jax.ShapeDtypeStruct((nq, HEADS, 8, TQ), F32),
                   jax.ShapeDtypeStruct(chip_blocks.shape, chip_blocks.dtype)),
        in_specs=[pl.BlockSpec((TQ, HEADS * AUG), _row), blk, pl.BlockSpec((1, D_ATTN, TQ), lambda j: (j, 0, 0)),
                  VMEM_WHOLE, VMEM_WHOLE, VMEM_WHOLE, VMEM_WHOLE, VMEM_WHOLE, VMEM_WHOLE, ANY],
        out_specs=(pl.BlockSpec((nq, D_ATTN, TQ), lambda j: (0, 0, 0)), blk, blk, pl.BlockSpec((TQ, LANES), _row),
                   pl.BlockSpec((nq, HEADS, 8, TQ), lambda j: (0, 0, 0, 0)), ANY),
        scratch_shapes=[pltpu.VMEM((TQ, wide), F32), pltpu.VMEM((TQ, wide), F32), pltpu.VMEM((TQ, wide), F32),
                        pltpu.VMEM((HEADS, TQ, TQ), F32), pltpu.VMEM((HEADS, TQ, TQ), BF16),
                        pltpu.VMEM((HEADS, TQ, TQ), BF16), pltpu.VMEM((r, cdim), chip_blocks.dtype),
                        pltpu.SemaphoreType.DMA((3,)), pltpu.SemaphoreType.DMA((3,)), pltpu.SemaphoreType.DMA],
        compiler_params=_params(1),
    )(ka, v, kt3, qat3, q, do, dot3, lset3, dlt3, chip_blocks)


def _pre_attn_bwd(dqt3, dk, dv, dcs, drs, fl, dy, x, dh1, g1, wqkv, wf, wu):
    s, d = x.shape
    nt = s // TS
    n = TS + HALO
    sub = TS // TQ
    qkv, fcols = 3 * D_ATTN, 3 * D_ATTN + LANES

    def body(dqt_ref, dk_ref, dv_ref, dcs_ref, drs_ref, fl_ref, dy_ref, x_ref, dh1_ref, g_ref, wqkv_ref, wf_ref, wu_ref,
             gx_ref, dz_ref, dg_ref, db_ref, ybuf, ccar, dlog):
        dqkv_ref = dz_ref.at[:, 0:qkv]
        dfb_ref = dz_ref.at[:, qkv:fcols]
        dub_ref = dz_ref.at[:, fcols:]
        i = pl.program_id(0)
        ti = nt - 1 - i

        @pl.when(i == 0)
        def _():
            ybuf[TS:n, :] = jnp.zeros((HALO, D_POOL), F32)
            ccar[...] = jnp.zeros_like(ccar)
            dg_ref[...] = jnp.zeros_like(dg_ref)
            db_ref[...] = jnp.zeros_like(db_ref)

        rr = lax.broadcasted_iota(jnp.int32, (TS, TS), 0)
        cc = lax.broadcasted_iota(jnp.int32, (TS, TS), 1)
        dlog[...] = ccar[...] + _mask_matmul((cc >= rr).astype(BF16), drs_ref[...] - dcs_ref[...])
        ccar[...] = dlog[0:1, :]
        df = dlog[...] * jax.nn.sigmoid(-fl_ref[...])
        db_ref[...] += jnp.sum(df, axis=0, keepdims=True)
        dfb = df.astype(BF16)
        dfb_ref[...] = dfb

        t = ti * TS + lax.broadcasted_iota(jnp.int32, (TS, 1), 0)
        dy = dy_ref[...]
        for g, w in enumerate(POOL_WINDOWS):
            cols = slice(g * POOL_CH, (g + 1) * POOL_CH)
            ybuf[0:TS, cols] = dy[:, cols] / jnp.minimum(t + 1, w).astype(F32)
        for g, w in enumerate(POOL_WINDOWS):
            cols = slice(g * POOL_CH, (g + 1) * POOL_CH)
            sm = ybuf[:, cols]
            step = 1
            while step < w:
                sm = sm + pltpu.roll(sm, n - step, 0)
                step *= 2
            dub_ref[:, cols] = (sm[0:TS, :] - dy[:, cols]).astype(BF16)
        ybuf[TS:n, :] = ybuf[0:HALO, :]

        for a in range(sub):
            dqkv_ref[a * TQ:(a + 1) * TQ, 0:D_ATTN] = (dqt_ref[a].T * 0.125).astype(BF16)
        dqkv_ref[:, D_ATTN:2 * D_ATTN] = dk_ref[...].astype(BF16)
        dqkv_ref[:, 2 * D_ATTN:] = dv_ref[...].astype(BF16)
        dhn = _nn(dqkv_ref[...], wqkv_ref[...]) + _nn(dfb, wf_ref[...]) + _nn(dub_ref[...], wu_ref[...])
        dx, dg = _rms_bwd(x_ref[...], g_ref[...], dhn)
        gx_ref[...] = dh1_ref[...] + dx
        dg_ref[...] += dg

    rev = lambda i: (nt - 1 - i, 0)
    blk = lambda w: pl.BlockSpec((TS, w), rev)
    return pl.pallas_call(
        body, grid=(nt,), name="pre_attn_bwd",
        out_shape=(jax.ShapeDtypeStruct((s, d), F32), jax.ShapeDtypeStruct((s, fcols + D_POOL), BF16),
                   jax.ShapeDtypeStruct((1, d), F32), jax.ShapeDtypeStruct((1, LANES), F32)),
        in_specs=[pl.BlockSpec((sub, D_ATTN, TQ), lambda i: (nt - 1 - i, 0, 0)),
                  blk(D_ATTN), blk(D_ATTN), blk(LANES), blk(LANES), blk(LANES), blk(D_POOL), blk(d), blk(d),
                  pl.BlockSpec((1, d), _fixed), pl.BlockSpec(wqkv.shape, _fixed), pl.BlockSpec(wf.shape, _fixed),
                  pl.BlockSpec(wu.shape, _fixed)],
        out_specs=(blk(d), blk(fcols + D_POOL), pl.BlockSpec((1, d), _fixed), pl.BlockSpec((1, LANES), _fixed)),
        scratch_shapes=[pltpu.VMEM((n, D_POOL), F32), pltpu.VMEM((1, LANES), F32), pltpu.VMEM((TS, LANES), F32)],
        compiler_params=_params(1),
    )(dqt3, dk, dv, dcs, drs, fl, dy, x, dh1, g1, wqkv, wf, wu)


def _wgrad(a, b, out_dtype, name):
    s, m = a.shape
    n = b.shape[1]
    tm = max(t for t in range(LANES, min(m, TM_WGRAD) + 1, LANES) if m % t == 0)
    ts = min(TS_WGRAD, s)
    ns = s // ts

    def body(a_ref, b_ref, o_ref, acc):
        i = pl.program_id(1)

        @pl.when(i == 0)
        def _():
            acc[...] = jnp.zeros_like(acc)

        acc[...] += _tn(a_ref[...], b_ref[...])

        @pl.when(i == ns - 1)
        def _():
            o_ref[...] = acc[...].astype(out_dtype)

    return pl.pallas_call(
        body, grid=(m // tm, ns), name=name, out_shape=jax.ShapeDtypeStruct((m, n), out_dtype),
        in_specs=[pl.BlockSpec((ts, tm), lambda j, i: (i, j)), pl.BlockSpec((ts, n), lambda j, i: (i, 0))],
        out_specs=pl.BlockSpec((tm, n), lambda j, i: (j, 0)),
        scratch_shapes=[pltpu.VMEM((tm, n), F32)],
        compiler_params=_params(2),
    )(a, b)


def _adamw(w, g, m, v):
    m = ADAM_B1 * m + (1.0 - ADAM_B1) * g
    v = ADAM_B2 * v + (1.0 - ADAM_B2) * (g * g)
    m_hat = m / (1.0 - ADAM_B1 ** ADAM_STEP)
    v_hat = v / (1.0 - ADAM_B2 ** ADAM_STEP)
    delta = -ADAM_LR * (m_hat / (jnp.sqrt(v_hat) + ADAM_EPS) + ADAM_WD * w)
    return delta, m, v


def _pair_sum(core, t, theirs, tr, name):
    nk, r, c = theirs.shape

    def body(core_ref, a_ref, b_ref, o_ref):
        o_ref[...] = (a_ref[...].astype(F32) + b_ref[...].astype(F32)).astype(BF16)

    blk = pl.BlockSpec((1, tr, c), lambda k, i, core_ref: (k, i, 0))
    return pl.pallas_call(
        body, name=name, out_shape=jax.ShapeDtypeStruct(theirs.shape, BF16),
        grid_spec=pltpu.PrefetchScalarGridSpec(
            num_scalar_prefetch=1, grid=(nk, r // tr),
            in_specs=[pl.BlockSpec((1, tr, c), lambda k, i, core_ref: (2 * k + core_ref[0], i, 0)), blk],
            out_specs=blk),
        compiler_params=_params(2),
    )(core, t, theirs)


def _sum_update(p_ref, w_ref, m_ref, v_ref, g_ref, d_ref, nm_ref, nv_ref):
    g = p_ref[0].astype(F32)
    for k in range(1, p_ref.shape[0]):
        g = g + p_ref[k].astype(F32)
    g_ref[...] = g
    d_ref[...], nm_ref[...], nv_ref[...] = _adamw(w_ref[...], g, m_ref[...], v_ref[...])


def _reduce_update_rest(parts, w, m, v, chip_blocks, small_block):
    nk, r, c = parts.shape
    ns = r // TR_REST

    def body(p_ref, w_ref, m_ref, v_ref, b_ref, sm_ref, g_ref, d_ref, nm_ref, nv_ref, got_ref, all_ref,
             stage_b, stage_s, send_b, recv_b, local_b, send_s, recv_s, local_s):
        i = pl.program_id(0)

        @pl.when(i == 0)
        def _():
            _chips_start(b_ref, got_ref, stage_b, send_b, recv_b, local_b)
            _gather_start(sm_ref, all_ref, stage_s, send_s, recv_s, local_s)

        _sum_update(p_ref, w_ref, m_ref, v_ref, g_ref, d_ref, nm_ref, nv_ref)

        @pl.when(i == ns - 1)
        def _():
            _gather_pass_on(all_ref, send_s, recv_s)
            _chips_finish(b_ref, got_ref, send_b, recv_b)
            _gather_finish(sm_ref, all_ref, send_s, recv_s)

    blk = pl.BlockSpec((TR_REST, c), _row)
    out = jax.ShapeDtypeStruct((r, c), F32)
    dma = pltpu.SemaphoreType.DMA
    return pl.pallas_call(
        body, grid=(ns,), name="reduce_update_rest",
        out_shape=(out,) * 4 + (jax.ShapeDtypeStruct(chip_blocks.shape, chip_blocks.dtype),
                                jax.ShapeDtypeStruct((N_DEV,) + small_block.shape, small_block.dtype)),
        in_specs=[pl.BlockSpec((nk, TR_REST, c), lambda i: (0, i, 0)), blk, blk, blk, ANY, ANY],
        out_specs=(blk,) * 4 + (ANY, ANY),
        scratch_shapes=[pltpu.VMEM(chip_blocks.shape[1:], chip_blocks.dtype), pltpu.VMEM(small_block.shape, small_block.dtype),
                        dma((3,)), dma((3,)), dma, dma((7,)), dma((7,)), dma],
        compiler_params=_params(1),
    )(parts, w, m, v, chip_blocks, small_block)


def _reduce_update_big(parts, w, m, v, tr, name):
    nk, r, c = parts.shape

    def body(p_ref, w_ref, m_ref, v_ref, g_ref, d_ref, nm_ref, nv_ref):
        _sum_update(p_ref, w_ref, m_ref, v_ref, g_ref, d_ref, nm_ref, nv_ref)

    blk = pl.BlockSpec((tr, c), _row)
    out = jax.ShapeDtypeStruct((r, c), F32)
    return pl.pallas_call(
        body, grid=(r // tr,), name=name, out_shape=(out,) * 4,
        in_specs=[pl.BlockSpec((nk, tr, c), lambda i: (0, i, 0)), blk, blk, blk],
        out_specs=(blk,) * 4, compiler_params=_params(1),
    )(parts, w, m, v)


def _reduce_update_small(parts, w, m, v):
    nd = parts.shape[0]

    def body(p_ref, w_ref, m_ref, v_ref, g_ref, d_ref, nm_ref, nv_ref):
        g = p_ref[0]
        for k in range(1, nd):
            g = g + p_ref[k]
        g_ref[...] = g
        d_ref[...], nm_ref[...], nv_ref[...] = _adamw(w_ref[...], g, m_ref[...], v_ref[...])

    out = jax.ShapeDtypeStruct(w.shape, F32)
    return pl.pallas_call(body, name="reduce_update_small", out_shape=(out,) * 4,
                          compiler_params=pltpu.CompilerParams(vmem_limit_bytes=VMEM_LIMIT))(parts, w, m, v)


MESH = pl.DeviceIdType.MESH


def _copy_through_vmem(src_hbm, dst_hbm, stage, sem):
    load = pltpu.make_async_copy(src_hbm, stage, sem)
    load.start()
    load.wait()
    store = pltpu.make_async_copy(stage, dst_hbm, sem)
    store.start()
    store.wait()


class _GatherPlan:
    def __init__(self, x_ref, out_ref, send_sems, recv_sems):
        x, y, c = lax.axis_index("x"), lax.axis_index("y"), lax.axis_index("c")
        self.me, self.sibling, self.c = (x, y, c), (x, y, 1 - c), c
        self.chips = [(1 - x, y), (x, 1 - y), (1 - x, 1 - y)]
        self.x_ref, self.out_ref, self.send_sems, self.recv_sems = x_ref, out_ref, send_sems, recv_sems

    def slot(self, px, py, pc):
        return self.out_ref.at[4 * px + 2 * py + pc]

    def copy(self, k, block, to, src=None):
        return pltpu.make_async_remote_copy(
            src_ref=self.slot(*block) if src is None else src, dst_ref=self.slot(*block),
            send_sem=self.send_sems.at[k], recv_sem=self.recv_sems.at[k], device_id=to, device_id_type=MESH)

    def first(self):
        return [self.copy(0, self.me, self.sibling, src=self.x_ref)] + [
            self.copy(1 + j, self.me, (*chip, self.c), src=self.x_ref) for j, chip in enumerate(self.chips)]

    def passed(self):
        return [self.copy(4 + j, (*chip, self.c), self.sibling) for j, chip in enumerate(self.chips)]


def _gather_start(x_ref, out_ref, stage, send_sems, recv_sems, local_sem):
    plan = _GatherPlan(x_ref, out_ref, send_sems, recv_sems)
    for cp in plan.first():
        cp.start()
    _copy_through_vmem(x_ref, plan.slot(*plan.me), stage, local_sem)


def _gather_pass_on(out_ref, send_sems, recv_sems):
    plan = _GatherPlan(None, out_ref, send_sems, recv_sems)
    passed = plan.passed()
    for j, chip in enumerate(plan.chips):
        plan.copy(1 + j, (*chip, plan.c), plan.me).wait_recv()
        passed[j].start()


def _gather_finish(x_ref, out_ref, send_sems, recv_sems):
    plan = _GatherPlan(x_ref, out_ref, send_sems, recv_sems)
    plan.copy(0, plan.sibling, plan.me).wait_recv()
    for j, chip in enumerate(plan.chips):
        plan.copy(4 + j, (*chip, 1 - plan.c), plan.me).wait_recv()
    for cp in plan.first() + plan.passed():
        cp.wait_send()


def _all_gather(xs, name):
    r, cdim = xs.shape

    def body(x_ref, out_ref, stage, send_sems, recv_sems, local_sem):
        _gather_start(x_ref, out_ref, stage, send_sems, recv_sems, local_sem)
        _gather_pass_on(out_ref, send_sems, recv_sems)
        _gather_finish(x_ref, out_ref, send_sems, recv_sems)

    return pl.pallas_call(
        body, name=name, out_shape=jax.ShapeDtypeStruct((N_DEV, r, cdim), xs.dtype),
        in_specs=[ANY], out_specs=ANY,
        scratch_shapes=[pltpu.VMEM((r, cdim), xs.dtype), pltpu.SemaphoreType.DMA((7,)), pltpu.SemaphoreType.DMA((7,)),
                        pltpu.SemaphoreType.DMA],
        compiler_params=pltpu.CompilerParams(vmem_limit_bytes=VMEM_LIMIT),
    )(xs)


def _rs_pair(t, name):
    _, r, cdim = t.shape

    def body(t_ref, theirs_ref, send_sems, recv_sems):
        x, y, c = lax.axis_index("x"), lax.axis_index("y"), lax.axis_index("c")
        remote = [pltpu.make_async_remote_copy(
            src_ref=t_ref.at[2 * k + (1 - c)], dst_ref=theirs_ref.at[k],
            send_sem=send_sems.at[k], recv_sem=recv_sems.at[k], device_id=(x, y, 1 - c), device_id_type=MESH)
            for k in range(4)]
        for cp in remote:
            cp.start()
        for cp in remote:
            cp.wait()

    return pl.pallas_call(
        body, name=name, out_shape=jax.ShapeDtypeStruct((4, r, cdim), t.dtype), in_specs=[ANY], out_specs=ANY,
        scratch_shapes=[pltpu.SemaphoreType.DMA((4,)), pltpu.SemaphoreType.DMA((4,))],
    )(t)


def _chips_start(b_ref, out_ref, stage, send_sems, recv_sems, local_sem):
    x, y, c = lax.axis_index("x"), lax.axis_index("y"), lax.axis_index("c")
    mychip = 2 * x + y
    for j, (px, py) in enumerate([(1 - x, y), (x, 1 - y), (1 - x, 1 - y)]):
        pltpu.make_async_remote_copy(
            src_ref=b_ref.at[2 * px + py], dst_ref=out_ref.at[mychip],
            send_sem=send_sems.at[j], recv_sem=recv_sems.at[j], device_id=(px, py, c), device_id_type=MESH).start()
    _copy_through_vmem(b_ref.at[mychip], out_ref.at[mychip], stage, local_sem)


def _chips_finish(b_ref, out_ref, send_sems, recv_sems):
    x, y, c = lax.axis_index("x"), lax.axis_index("y"), lax.axis_index("c")
    for j, (px, py) in enumerate([(1 - x, y), (x, 1 - y), (1 - x, 1 - y)]):
        pltpu.make_async_remote_copy(
            src_ref=b_ref.at[2 * px + py], dst_ref=out_ref.at[2 * px + py],
            send_sem=send_sems.at[j], recv_sem=recv_sems.at[j], device_id=(px, py, c), device_id_type=MESH).wait()


def _pad_rows(a, rows):
    return jnp.pad(a, ((0, rows - a.shape[0]), (0, 0)))


def _pack_in(w_in):
    return _pad_rows(w_in[0].T, ROWS_IN)


def _unpack_in(r):
    return r[0:SHARD_IN].T[None]


def _pack_rest(w_out, w_gate, w_up, w_down, w_ple, w_pg):
    head = _pad_rows(jnp.concatenate([w_out[0], w_pg[0], w_ple[0].T.reshape(32, D_MODEL)], axis=0), OFF_GATE)
    return jnp.concatenate([head, w_gate[0].T, w_up[0].T, w_down[0]], axis=0)


def _unpack_rest(r):
    return (r[0:OFF_PG][None], r[OFF_GATE:OFF_UP].T[None], r[OFF_UP:OFF_DOWN].T[None], r[OFF_DOWN:ROWS_REST][None],
            r[OFF_PLE:OFF_PLE + 32].reshape(128, D_PLE).T[None], r[OFF_PG:OFF_PLE][None])


def _pack_small(w_pool, g_mix_pre, g_mix_post, g_ffn_pre, g_ffn_post, g_ple, g_attn, g_pool, pool_scale, b_forget,
                loss=None):
    def row(vrow):
        return jnp.pad(vrow.reshape(1, -1), ((0, 0), (0, D_MODEL - vrow.size)))
    rows = [w_pool.reshape(64, D_MODEL), row(g_mix_pre), row(g_mix_post), row(g_ffn_pre), row(g_ffn_post), row(g_ple),
            row(g_attn), row(g_pool), row(pool_scale), row(b_forget),
            row(loss) if loss is not None else jnp.zeros((1, D_MODEL), F32)]
    return _pad_rows(jnp.concatenate(rows, axis=0), SMALL_ROWS)


def _unpack_small(r):
    return dict(
        w_pool=r[0:64].reshape(1, 4, POOL_CH, POOL_CH), g_mix_pre=r[ROW_G_MIX_PRE:ROW_G_MIX_PRE + 1],
        g_mix_post=r[ROW_G_MIX_POST:ROW_G_MIX_POST + 1], g_ffn_pre=r[ROW_G_FFN_PRE:ROW_G_FFN_PRE + 1],
        g_ffn_post=r[ROW_G_FFN_POST:ROW_G_FFN_POST + 1], g_ple=r[ROW_G_PLE:ROW_G_PLE + 1],
        g_attn_grp=r[ROW_G_ATTN:ROW_G_ATTN + 1, 0:D_ATTN], g_pool_grp=r[ROW_G_POOL:ROW_G_POOL + 1, 0:D_POOL],
        pool_scale=r[ROW_POOL_SCALE:ROW_POOL_SCALE + 1, 0:D_POOL], b_forget=r[ROW_B_FORGET:ROW_B_FORGET + 1, 0:HEADS])


def _step(x, p, tgt, small, in_w, in_m, in_v, rest_w, rest_m, rest_v):
    core = lax.axis_index("c").astype(jnp.int32).reshape(1)
    win_t = _all_gather(in_w.astype(BF16), "gather_w_in")[:, 0:SHARD_IN].reshape(D_IN, D_MODEL)
    wqkv = win_t[0:3 * D_ATTN]
    wf = _pad_rows(win_t[3 * D_ATTN:3 * D_ATTN + HEADS], LANES)
    wu = win_t[3 * D_ATTN + HEADS:]
    wpool = small["w_pool"].astype(BF16)
    bpad = jnp.pad(small["b_forget"], ((0, 0), (0, LANES - HEADS)))

    lay = _attn_layout_constants()
    hn, q, ka, v, qat3, vt3, kt3, fl, y, mpre = _pre_attn_fwd(x, small["g_mix_pre"], wqkv, wf, wu, bpad, wpool, lay)
    a, lset3, g = _attn_fwd(ka, qat3, vt3, rest_w.astype(BF16))
    wple_t = g[:, OFF_PLE:OFF_PLE + 32].reshape(D_MODEL, D_PLE)
    mix, o, h1, hn2 = _post_attn_fwd(a, mpre, x, small["g_attn_grp"], small["g_pool_grp"], small["pool_scale"], g,
                                     small["g_mix_post"], small["g_ffn_pre"])
    gate, up, act, ff, h2 = _ffn_fwd(hn2, g, g, g, h1, small["g_ffn_post"])
    dh2, dff, dgl, dpp, h2b, pb, loss8, dg_ple, dg_ffn_post = _tail_fwd_bwd(
        h2, p, tgt, ff, wple_t, g, small["g_ple"], small["g_ffn_post"])
    dgate, dup, dh1, dg_ffn_pre = _ffn_bwd(dff, gate, up, g, g, g, h1, dh2, small["g_ffn_pre"])
    dob, dab, dat3, dlt3, dmpb, dy, dg_mix_post, dg_attn, dg_pool, dps = _post_attn_bwd(
        dh1, o, a, mpre, g, wpool, small["g_mix_post"], small["g_attn_grp"], small["g_pool_grp"], small["pool_scale"])

    nd = N_DEV
    send_rest = jnp.concatenate([
        _wgrad(mix, dob, BF16, "wgrad_out").reshape(nd, 128, D_MODEL),
        _wgrad(h2b, dgl, BF16, "wgrad_ple_gate").reshape(nd, 128, D_MODEL),
        _wgrad(dpp, pb, BF16, "wgrad_ple").reshape(nd, 32, D_MODEL),
        jnp.zeros((nd, OFF_GATE - OFF_PLE - 32, D_MODEL), BF16),
        _wgrad(dgate, hn2, BF16, "wgrad_gate").reshape(nd, SHARD_FF, D_MODEL),
        _wgrad(dup, hn2, BF16, "wgrad_up").reshape(nd, SHARD_FF, D_MODEL),
        _wgrad(act, dff, BF16, "wgrad_down").reshape(nd, SHARD_FF, D_MODEL)], axis=1)
    pair_rest = _pair_sum(core, send_rest, _rs_pair(send_rest, "rs_pair_rest"), TR_REST, "rs_pair_sum_rest")

    dqt3, dk, dv, dcs, drs4, chips_rest = _attn_bwd(ka, v, kt3, qat3, q, dab, dat3, lset3, dlt3, pair_rest)
    drs = jnp.pad(drs4[:, :, 0, :].transpose(0, 2, 1).reshape(-1, HEADS), ((0, 0), (0, LANES - HEADS)))
    gx, dz, dg_mix_pre, db = _pre_attn_bwd(dqt3, dk, dv, dcs, drs, fl, dy, x, dh1, small["g_mix_pre"], wqkv, wf, wu)

    dwz = _wgrad(dz, hn, F32, "wgrad_in")
    dwin_t = jnp.concatenate([dwz[0:3 * D_ATTN], dwz[3 * D_ATTN:3 * D_ATTN + HEADS], dwz[3 * D_ATTN + LANES:]], axis=0)
    send_in = jnp.pad(dwin_t.reshape(nd, SHARD_IN, D_MODEL), ((0, 0), (0, ROWS_IN - SHARD_IN), (0, 0))).astype(BF16)
    pair_in = _pair_sum(core, send_in, _rs_pair(send_in, "rs_pair_in"), ROWS_IN, "rs_pair_sum_in")

    dwp = _wgrad(y, dmpb, F32, "wgrad_pool")
    dw_pool = jnp.stack([dwp[g * POOL_CH:(g + 1) * POOL_CH, g * POOL_CH:(g + 1) * POOL_CH] for g in range(4)])
    small_part = _pack_small(dw_pool, dg_mix_pre, dg_mix_post, dg_ffn_pre, dg_ffn_post, dg_ple, dg_attn, dg_pool, dps,
                             db[:, 0:HEADS], loss8[0:1, 0:1])

    *upd_rest, chips_in, small_all = _reduce_update_rest(chips_rest, rest_w, rest_m, rest_v, pair_in, small_part)
    upd_in = _reduce_update_big(chips_in, in_w, in_m, in_v, ROWS_IN, "reduce_update_in")
    return gx, small_all, upd_in, upd_rest


def kernel(x, p, g_mix_pre, w_in, b_forget, g_attn_grp, g_pool_grp, w_pool, pool_scale, w_out, g_mix_post, g_ffn_pre, w_ffn_gate, w_ffn_up, w_ffn_down, g_ffn_post, w_ple_proj, g_ple, w_ple_gate, loss_target, m_g_mix_pre, m_w_in, m_b_forget, m_g_attn_grp, m_g_pool_grp, m_w_pool, m_pool_scale, m_w_out, m_g_mix_post, m_g_ffn_pre, m_w_ffn_gate, m_w_ffn_up, m_w_ffn_down, m_g_ffn_post, m_w_ple_proj, m_g_ple, m_w_ple_gate, v_g_mix_pre, v_w_in, v_b_forget, v_g_attn_grp, v_g_pool_grp, v_w_pool, v_pool_scale, v_w_out, v_g_mix_post, v_g_ffn_pre, v_w_ffn_gate, v_w_ffn_up, v_w_ffn_down, v_g_ffn_post, v_w_ple_proj, v_g_ple, v_w_ple_gate):
    small = dict(w_pool=w_pool[0], g_mix_pre=g_mix_pre, g_mix_post=g_mix_post, g_ffn_pre=g_ffn_pre,
                 g_ffn_post=g_ffn_post, g_ple=g_ple, g_attn_grp=g_attn_grp, g_pool_grp=g_pool_grp,
                 pool_scale=pool_scale, b_forget=b_forget)
    gx, small_all, upd_in, upd_rest = _step(
        x[0], p[0, 0], loss_target[0], small, _pack_in(w_in), _pack_in(m_w_in), _pack_in(v_w_in),
        _pack_rest(w_out, w_ffn_gate, w_ffn_up, w_ffn_down, w_ple_proj, w_ple_gate),
        _pack_rest(m_w_out, m_w_ffn_gate, m_w_ffn_up, m_w_ffn_down, m_w_ple_proj, m_w_ple_gate),
        _pack_rest(v_w_out, v_w_ffn_gate, v_w_ffn_up, v_w_ffn_down, v_w_ple_proj, v_w_ple_gate))

    sm_w = _pack_small(w_pool, g_mix_pre, g_mix_post, g_ffn_pre, g_ffn_post, g_ple, g_attn_grp, g_pool_grp, pool_scale, b_forget)
    sm_m = _pack_small(m_w_pool, m_g_mix_pre, m_g_mix_post, m_g_ffn_pre, m_g_ffn_post, m_g_ple, m_g_attn_grp, m_g_pool_grp, m_pool_scale, m_b_forget)
    sm_v = _pack_small(v_w_pool, v_g_mix_pre, v_g_mix_post, v_g_ffn_pre, v_g_ffn_post, v_g_ple, v_g_attn_grp, v_g_pool_grp, v_pool_scale, v_b_forget)
    upd_small = _reduce_update_small(small_all, sm_w, sm_m, sm_v)
    loss = upd_small[0][ROW_LOSS, 0]

    def leaves(k):
        b_out, b_gate, b_up, b_down, b_ple, b_pg = _unpack_rest(upd_rest[k])
        s = _unpack_small(upd_small[k])
        return (s["g_mix_pre"], _unpack_in(upd_in[k]), s["b_forget"], s["g_attn_grp"], s["g_pool_grp"], s["w_pool"],
                s["pool_scale"], b_out, s["g_mix_post"], s["g_ffn_pre"], b_gate, b_up, b_down, s["g_ffn_post"], b_ple,
                s["g_ple"], b_pg)

    return (loss, gx[None], *leaves(0), *leaves(1), *leaves(2), *leaves(3))
```

```python
import functools

import jax
import jax.numpy as jnp
from jax import lax
from jax.experimental import pallas as pl
from jax.experimental.pallas import tpu as pltpu

F32 = jnp.float32
BF16 = jnp.bfloat16
HIGHEST = lax.Precision.HIGHEST

D_MODEL = 1024
HEADS = 8
HEAD_DIM = 64
D_ATTN = HEADS * HEAD_DIM
POOL_WINDOWS = (2, 4, 8, 16)
POOL_CH = 128
D_POOL = POOL_CH * len(POOL_WINDOWS)
D_FF = 2816
D_PLE = 256
D_IN = 3 * D_ATTN + HEADS + D_POOL
RMS_EPS = 1e-6
N_DEV = 8

ADAM_LR = 0.001
ADAM_B1 = 0.9
ADAM_B2 = 0.999
ADAM_EPS = 1e-08
ADAM_WD = 0.01
ADAM_STEP = 10

LANES = 128
HALO = 16
TS = 512
TS_FF = 512
TS_WGRAD = 1024
TM_WGRAD = 2176
TQ = 256
TN_FF = 1408
NEG = -1e30
VMEM_LIMIT = 56 * 1024 * 1024

SHARD_IN = 257
ROWS_IN = 272
SHARD_FF = 352
OFF_PG = 128
OFF_PLE = 256
OFF_GATE = SHARD_FF
OFF_UP = 2 * SHARD_FF
OFF_DOWN = 3 * SHARD_FF
ROWS_REST = 4 * SHARD_FF
TR_REST = SHARD_FF

SMALL_ROWS = 80
ROW_G_MIX_PRE, ROW_G_MIX_POST, ROW_G_FFN_PRE, ROW_G_FFN_POST, ROW_G_PLE = 64, 65, 66, 67, 68
ROW_G_ATTN, ROW_G_POOL, ROW_POOL_SCALE, ROW_B_FORGET, ROW_LOSS = 69, 70, 71, 72, 73


def _nn(a, b):
    return jnp.dot(a, b, preferred_element_type=F32)


def _nt(a, b):
    return lax.dot_general(a, b, (((1,), (1,)), ((), ())), preferred_element_type=F32)


def _tn(a, b):
    return lax.dot_general(a, b, (((0,), (0,)), ((), ())), preferred_element_type=F32)


def _rstd(v):
    return lax.rsqrt(jnp.mean(v * v, axis=-1, keepdims=True) + RMS_EPS)


def _rms_bwd(v, g, dy):
    r = _rstd(v)
    vh = v * r
    t = dy * g
    dv = r * (t - vh * jnp.mean(t * vh, axis=-1, keepdims=True))
    return dv, jnp.sum(dy * vh, axis=0, keepdims=True)


def _split3(v):
    hi = v.astype(BF16)
    rest = v - hi.astype(F32)
    mid = rest.astype(BF16)
    return hi, mid, (rest - mid.astype(F32)).astype(BF16)


def _mask_matmul(mask, v):
    hi, mid, lo = _split3(v)
    return _nn(mask, lo) + _nn(mask, mid) + _nn(mask, hi)


def _params(n_grid):
    return pltpu.CompilerParams(dimension_semantics=("arbitrary",) * n_grid, vmem_limit_bytes=VMEM_LIMIT)


def _row(i):
    return (i, 0)


def _fixed(*_):
    return (0, 0)


def _spec_square(part):
    return pl.BlockSpec((N_DEV, 128, D_MODEL), lambda *_: (0, part, 0))


def _spec_ff(part):
    return pl.BlockSpec((TN_FF // SHARD_FF, SHARD_FF, D_MODEL), lambda i, j: (j, part, 0))


assert TS == 2 * TQ and TN_FF % SHARD_FF == 0
_HALVES = (slice(0, TQ), slice(TQ, TS))

VMEM_WHOLE = pl.BlockSpec(memory_space=pltpu.VMEM)
SMEM_WHOLE = pl.BlockSpec(memory_space=pltpu.SMEM)
ANY = pl.BlockSpec(memory_space=pl.ANY)


AUG = 128
BIAS_LANE = HEAD_DIM
ONE_LANE = HEAD_DIM + 3
SPARE_LANE = HEADS


def _attn_layout_constants():
    import numpy as np
    place = np.zeros((D_ATTN, HEADS * AUG), np.float32)
    for r in range(D_ATTN):
        place[r, (r // HEAD_DIM) * AUG + r % HEAD_DIM] = 1.0
    bias_k = np.zeros((3, LANES, HEADS * AUG), np.float32)
    bias_q = np.zeros((3, LANES, HEADS * AUG), np.float32)
    for h in range(HEADS):
        for part in range(3):
            bias_k[part, h, h * AUG + BIAS_LANE + part] = -1.0
            bias_q[part, h, h * AUG + ONE_LANE + part] = 1.0
            bias_k[0, SPARE_LANE, h * AUG + ONE_LANE + part] = 1.0
            bias_q[0, SPARE_LANE, h * AUG + BIAS_LANE + part] = 1.0
    as_bf = lambda a: jnp.asarray(a, BF16)
    return dict(place=as_bf(place), place_t=as_bf(place.T), bias_k=as_bf(bias_k),
                bias_q_t=as_bf(bias_q.transpose(0, 2, 1)))


def _pre_attn_fwd(x, g1, wqkv, wf, wu, bpad, wpool, lay, own_block):
    s, d = x.shape
    nt = s // TS
    sub = TS // TQ

    def body(x_ref, g_ref, wqkv_ref, wf_ref, wu_ref, b_ref, wp_ref, place_ref, place_t_ref, bk_ref, bqt_ref, own_ref,
             hn_ref, q_ref, ka_ref, v_ref, qat_ref, vt_ref, kt_ref, fl_ref, y_ref, mp_ref, all_ref,
             ubuf, ccar, cbuf, stage, send_sems, recv_sems, local_sem):
        i = pl.program_id(0)

        @pl.when(i == 0)
        def _():
            _gather_start(own_ref, all_ref, stage, send_sems, recv_sems, local_sem)
            ubuf[0:HALO, :] = jnp.zeros((HALO, D_POOL), F32)
            ccar[...] = jnp.zeros_like(ccar)

        @pl.when(i == max(nt - 2, 0))
        def _():
            _gather_pass_on(all_ref, send_sems, recv_sems)

        xv = x_ref[...]
        hn = (xv * _rstd(xv) * g_ref[...]).astype(BF16)
        hn_ref[...] = hn
        zq = _nt(hn, wqkv_ref[...])
        qb = (zq[:, 0:D_ATTN] * 0.125).astype(BF16)
        kb = zq[:, D_ATTN:2 * D_ATTN].astype(BF16)
        vb = zq[:, 2 * D_ATTN:3 * D_ATTN].astype(BF16)
        q_ref[...] = qb
        v_ref[...] = vb

        fl = _nt(hn, wf_ref[...]) + b_ref[...]
        fl_ref[...] = fl
        logf = jax.nn.log_sigmoid(fl)
        rr = lax.broadcasted_iota(jnp.int32, (TS, TS), 0)
        cc = lax.broadcasted_iota(jnp.int32, (TS, TS), 1)
        c = _mask_matmul((cc <= rr).astype(BF16), logf) + ccar[...]
        cbuf[...] = c
        ccar[...] = cbuf[TS - 1:TS, :]
        hi, mid, lo = _split3(c)
        lane = lax.broadcasted_iota(jnp.int32, (TS, LANES), 1)
        parts = (jnp.where(lane == SPARE_LANE, 1.0, hi).astype(BF16), mid, lo)
        ka = _nn(kb, place_ref[...])
        qat = _nt(place_t_ref[...], qb)
        for part in range(3):
            ka = ka + _nn(parts[part], bk_ref[part])
            qat = qat + _nt(bqt_ref[part], parts[part])
        ka_ref[...] = ka.astype(BF16)
        qat = qat.astype(BF16)
        vt = vb.T
        kt = kb.T
        for a in range(sub):
            qat_ref[a] = qat[:, a * TQ:(a + 1) * TQ]
            vt_ref[a] = vt[:, a * TQ:(a + 1) * TQ]
            kt_ref[a] = kt[:, a * TQ:(a + 1) * TQ]

        u = _nt(hn, wu_ref[...])
        ubuf[HALO:HALO + TS, :] = u
        t = i * TS + lax.broadcasted_iota(jnp.int32, (TS, 1), 0)
        for g, w in enumerate(POOL_WINDOWS):
            cols = slice(g * POOL_CH, (g + 1) * POOL_CH)
            sm = ubuf[:, cols]
            step = 1
            while step < w:
                sm = sm + pltpu.roll(sm, step, 0)
                step *= 2
            cnt = jnp.minimum(t + 1, w).astype(F32)
            yg = (sm[HALO:, :] / cnt - u[:, cols]).astype(BF16)
            y_ref[:, cols] = yg
            mp_ref[:, cols] = _nn(yg, wp_ref[g])
        ubuf[0:HALO, :] = u[TS - HALO:, :]

        @pl.when(i == nt - 1)
        def _():
            _gather_finish(own_ref, all_ref, send_sems, recv_sems)

    nq = s // TQ
    aug = HEADS * AUG
    outs = (
        jax.ShapeDtypeStruct((s, d), BF16), jax.ShapeDtypeStruct((s, D_ATTN), BF16),
        jax.ShapeDtypeStruct((s, aug), BF16), jax.ShapeDtypeStruct((s, D_ATTN), BF16),
        jax.ShapeDtypeStruct((nq, aug, TQ), BF16), jax.ShapeDtypeStruct((nq, D_ATTN, TQ), BF16),
        jax.ShapeDtypeStruct((nq, D_ATTN, TQ), BF16),
        jax.ShapeDtypeStruct((s, LANES), F32),
        jax.ShapeDtypeStruct((s, D_POOL), BF16), jax.ShapeDtypeStruct((s, D_POOL), F32),
        jax.ShapeDtypeStruct((N_DEV,) + own_block.shape, own_block.dtype),
    )
    fixed3 = lambda i: (0, 0, 0)
    tiles3 = lambda rows: pl.BlockSpec((sub, rows, TQ), lambda i: (i, 0, 0))
    return pl.pallas_call(
        body, grid=(nt,), out_shape=outs, name="pre_attn_fwd",
        in_specs=[pl.BlockSpec((TS, d), _row), pl.BlockSpec((1, d), _fixed),
                  pl.BlockSpec(wqkv.shape, _fixed), pl.BlockSpec(wf.shape, _fixed), pl.BlockSpec(wu.shape, _fixed),
                  pl.BlockSpec((1, LANES), _fixed), pl.BlockSpec(wpool.shape, fixed3),
                  pl.BlockSpec(lay["place"].shape, _fixed), pl.BlockSpec(lay["place_t"].shape, _fixed),
                  pl.BlockSpec(lay["bias_k"].shape, fixed3), pl.BlockSpec(lay["bias_q_t"].shape, fixed3), ANY],
        out_specs=(pl.BlockSpec((TS, d), _row), pl.BlockSpec((TS, D_ATTN), _row),
                   pl.BlockSpec((TS, aug), _row), pl.BlockSpec((TS, D_ATTN), _row),
                   tiles3(aug), tiles3(D_ATTN), tiles3(D_ATTN),
                   pl.BlockSpec((TS, LANES), _row),
                   pl.BlockSpec((TS, D_POOL), _row), pl.BlockSpec((TS, D_POOL), _row), ANY),
        scratch_shapes=[pltpu.VMEM((TS + HALO, D_POOL), F32), pltpu.VMEM((1, LANES), F32), pltpu.VMEM((TS, LANES), F32),
                        pltpu.VMEM(own_block.shape, own_block.dtype),
                        pltpu.SemaphoreType.DMA((7,)), pltpu.SemaphoreType.DMA((7,)), pltpu.SemaphoreType.DMA],
        compiler_params=_params(1),
    )(x, g1, wqkv, wf, wu, bpad, wpool, lay["place"], lay["place_t"], lay["bias_k"], lay["bias_q_t"], own_block)


def _causal_in_tile():
    krow = lax.broadcasted_iota(jnp.int32, (TQ, TQ), 0)
    qcol = lax.broadcasted_iota(jnp.int32, (TQ, TQ), 1)
    return krow <= qcol


def _attn_fwd(ka, qat3, vt3, own_block):
    s = ka.shape[0]
    nq = s // TQ
    pass_on_step = max(nq - 2, 0)

    def body(qa_ref, ka_ref, vt_ref, own_ref, a_ref, lset_ref, all_ref, acc, st_scr, pt_scr,
             stage, send_sems, recv_sems, local_sem):
        i = pl.program_id(0)

        @pl.when(i == 0)
        def _():
            _gather_start(own_ref, all_ref, stage, send_sems, recv_sems, local_sem)

        @pl.when(i == pass_on_step)
        def _():
            _gather_pass_on(all_ref, send_sems, recv_sems)

        acc[...] = jnp.zeros_like(acc)

        def tile(j, stats, masked):
            tile_max = []
            for h in range(HEADS):
                aug = slice(h * AUG, (h + 1) * AUG)
                st = _nn(ka_ref[pl.ds(j * TQ, TQ), aug], qa_ref[0, aug, :])
                if masked:
                    st = jnp.where(_causal_in_tile(), st, NEG)
                st_scr[h] = st
                tile_max.append(jnp.max(st, axis=0, keepdims=True))
            new, scale = [], []
            for h in range(HEADS):
                m_old, l_old = stats[h]
                m_new = jnp.maximum(m_old, tile_max[h])
                al = jnp.exp(m_old - m_new)
                pt = jnp.exp(st_scr[h] - m_new)
                pt_scr[h] = pt.astype(BF16)
                new.append((m_new, al * l_old + jnp.sum(pt, axis=0, keepdims=True)))
                scale.append(al)
            for h in range(HEADS):
                rows = slice(h * HEAD_DIM, (h + 1) * HEAD_DIM)
                acc[rows, :] = scale[h] * acc[rows, :] + _nn(vt_ref[j, rows, :], pt_scr[h])
            return tuple(new)

        init = tuple((jnp.full((1, TQ), NEG, F32), jnp.zeros((1, TQ), F32)) for _ in range(HEADS))
        stats = lax.fori_loop(0, i, functools.partial(tile, masked=False), init)
        stats = tile(i, stats, True)
        for h in range(HEADS):
            rows = slice(h * HEAD_DIM, (h + 1) * HEAD_DIM)
            acc[rows, :] = acc[rows, :] / stats[h][1]
            lset_ref[0, h:h + 1, :] = stats[h][0] + jnp.log(stats[h][1])
        a_ref[...] = acc[...].T

        @pl.when(i == nq - 1)
        def _():
            _gather_finish(own_ref, all_ref, send_sems, recv_sems)

    r, cdim = own_block.shape
    return pl.pallas_call(
        body, grid=(nq,), name="attn_fwd",
        out_shape=(jax.ShapeDtypeStruct((s, D_ATTN), F32), jax.ShapeDtypeStruct((nq, HEADS, TQ), F32),
                   jax.ShapeDtypeStruct((N_DEV, r, cdim), own_block.dtype)),
        in_specs=[pl.BlockSpec((1, HEADS * AUG, TQ), lambda i: (i, 0, 0)), VMEM_WHOLE, VMEM_WHOLE, ANY],
        out_specs=(pl.BlockSpec((TQ, D_ATTN), _row), pl.BlockSpec((1, HEADS, TQ), lambda i: (i, 0, 0)), ANY),
        scratch_shapes=[pltpu.VMEM((D_ATTN, TQ), F32), pltpu.VMEM((HEADS, TQ, TQ), F32), pltpu.VMEM((HEADS, TQ, TQ), BF16),
                        pltpu.VMEM((r, cdim), own_block.dtype),
                        pltpu.SemaphoreType.DMA((7,)), pltpu.SemaphoreType.DMA((7,)), pltpu.SemaphoreType.DMA],
        compiler_params=_params(1),
    )(qat3, ka, vt3, own_block)


def _post_attn_fwd(a, mpre, x, g_attn, g_pool, pscale, wout, g_post, g_ffn_pre):
    s, d = x.shape

    def body(a_ref, mp_ref, x_ref, ga_ref, gp_ref, ps_ref, wo_ref, gpost_ref, gpre_ref,
             mix_ref, o_ref, h1_ref, hn2_ref):
        for rows in _HALVES:
            av = a_ref[rows, :]
            mix_ref[rows, 0:D_ATTN] = (av * _rstd(av) * ga_ref[...]).astype(BF16)
            mv = mp_ref[rows, :] * ps_ref[...]
            mix_ref[rows, D_ATTN:] = (mv * _rstd(mv) * gp_ref[...]).astype(BF16)
            o = _nn(mix_ref[rows, :], wo_ref[...].reshape(d, d))
            o_ref[rows, :] = o
            h1 = x_ref[rows, :] + o * _rstd(o) * gpost_ref[...]
            h1_ref[rows, :] = h1
            hn2_ref[rows, :] = (h1 * _rstd(h1) * gpre_ref[...]).astype(BF16)

    vec = lambda n: pl.BlockSpec((1, n), _fixed)
    return pl.pallas_call(
        body, grid=(s // TS,), name="post_attn_fwd",
        out_shape=(jax.ShapeDtypeStruct((s, d), BF16), jax.ShapeDtypeStruct((s, d), F32),
                   jax.ShapeDtypeStruct((s, d), F32), jax.ShapeDtypeStruct((s, d), BF16)),
        in_specs=[pl.BlockSpec((TS, D_ATTN), _row), pl.BlockSpec((TS, D_POOL), _row), pl.BlockSpec((TS, d), _row),
                  vec(D_ATTN), vec(D_POOL), vec(D_POOL), _spec_square(0), vec(d), vec(d)],
        out_specs=(pl.BlockSpec((TS, d), _row),) * 4,
        compiler_params=_params(1),
    )(a, mpre, x, g_attn, g_pool, pscale, wout, g_post, g_ffn_pre)


def _ffn_fwd(hn2, wg, wu, wd, h1, g_post):
    s, d = h1.shape
    nc = D_FF // TN_FF
    ts = min(TS_FF, s)

    def body(hn_ref, wg_ref, wu_ref, wd_ref, h1_ref, g_ref, gate_ref, up_ref, act_ref, ff_ref, h2_ref, acc):
        j = pl.program_id(1)

        @pl.when(j == 0)
        def _():
            acc[...] = jnp.zeros_like(acc)

        for r in range(2):
            rows = slice(r * (ts // 2), (r + 1) * (ts // 2))
            hn = hn_ref[rows, :]
            gt = _nt(hn, wg_ref[...].reshape(TN_FF, d))
            up = _nt(hn, wu_ref[...].reshape(TN_FF, d))
            gate_ref[rows, :] = gt.astype(BF16)
            up_ref[rows, :] = up.astype(BF16)
            act_ref[rows, :] = (gt * jax.nn.sigmoid(gt) * up).astype(BF16)
            acc[rows, :] += _nn(act_ref[rows, :], wd_ref[...].reshape(TN_FF, d))

        @pl.when(j == nc - 1)
        def _():
            ff = acc[...]
            ff_ref[...] = ff
            h2_ref[...] = h1_ref[...] + ff * _rstd(ff) * g_ref[...]

    rowblk = pl.BlockSpec((ts, d), lambda i, j: (i, 0))
    chunk = pl.BlockSpec((ts, TN_FF), lambda i, j: (i, j))
    return pl.pallas_call(
        body, grid=(s // ts, nc), name="ffn_fwd",
        out_shape=(jax.ShapeDtypeStruct((s, D_FF), BF16),) * 3 + (jax.ShapeDtypeStruct((s, d), F32),) * 2,
        in_specs=[rowblk, _spec_ff(0), _spec_ff(1), _spec_ff(2), rowblk, pl.BlockSpec((1, d), lambda i, j: (0, 0))],
        out_specs=(chunk, chunk, chunk, rowblk, rowblk),
        scratch_shapes=[pltpu.VMEM((ts, d), F32)],
        compiler_params=_params(2),
    )(hn2, wg, wu, wd, h1, g_post)


def _tail_fwd_bwd(h2, p, tgt, ff, wple, wpg, g_ple, g_ffn_post):
    s, d = h2.shape

    def body(h2_ref, p_ref, t_ref, ff_ref, wple_ref, wpg_ref, gple_ref, gfp_ref,
             dh2_ref, dff_ref, dgl_ref, dpp_ref, h2b_ref, pb_ref, loss_ref, dgple_ref, dgfp_ref):
        i = pl.program_id(0)

        @pl.when(i == 0)
        def _():
            loss_ref[...] = jnp.zeros_like(loss_ref)
            dgple_ref[...] = jnp.zeros_like(dgple_ref)
            dgfp_ref[...] = jnp.zeros_like(dgfp_ref)

        h2 = h2_ref[...]
        h2b = h2.astype(BF16)
        h2b_ref[...] = h2b
        pb = p_ref[...].astype(BF16)
        pb_ref[...] = pb
        pp = _nt(pb, wple_ref[...])
        gple = gple_ref[...]
        e = pp * _rstd(pp) * gple
        wpg = wpg_ref[...].reshape(d, d)
        sg = jax.nn.sigmoid(_nn(h2b, wpg))
        diff = h2 + sg * e - t_ref[...]
        sq = jnp.sum(jnp.sum(diff * diff, axis=1, keepdims=True), axis=0, keepdims=True)
        loss_ref[...] += jnp.broadcast_to(sq * (0.5 / d), loss_ref.shape)
        dh3 = diff * (1.0 / d)
        dgl = (dh3 * e * sg * (1.0 - sg)).astype(BF16)
        dgl_ref[...] = dgl
        dh2 = dh3 + _nt(dgl, wpg)
        dh2_ref[...] = dh2
        dpp, dg = _rms_bwd(pp, gple, dh3 * sg)
        dpp_ref[...] = dpp.astype(BF16)
        dgple_ref[...] += dg
        dff, dg = _rms_bwd(ff_ref[...], gfp_ref[...], dh2)
        dff_ref[...] = dff.astype(BF16)
        dgfp_ref[...] += dg

    rowblk = pl.BlockSpec((TS, d), _row)
    vec = pl.BlockSpec((1, d), _fixed)
    return pl.pallas_call(
        body, grid=(s // TS,), name="tail_fwd_bwd",
        out_shape=(jax.ShapeDtypeStruct((s, d), F32), jax.ShapeDtypeStruct((s, d), BF16),
                   jax.ShapeDtypeStruct((s, d), BF16), jax.ShapeDtypeStruct((s, d), BF16),
                   jax.ShapeDtypeStruct((s, d), BF16), jax.ShapeDtypeStruct((s, D_PLE), BF16),
                   jax.ShapeDtypeStruct((8, LANES), F32), jax.ShapeDtypeStruct((1, d), F32),
                   jax.ShapeDtypeStruct((1, d), F32)),
        in_specs=[rowblk, pl.BlockSpec((TS, D_PLE), _row), rowblk, rowblk,
                  pl.BlockSpec(wple.shape, _fixed), _spec_square(1), vec, vec],
        out_specs=(rowblk, rowblk, rowblk, rowblk, rowblk, pl.BlockSpec((TS, D_PLE), _row),
                   pl.BlockSpec((8, LANES), _fixed), vec, vec),
        compiler_params=_params(1),
    )(h2, p, tgt, ff, wple, wpg, g_ple, g_ffn_post)


def _ffn_bwd(dff, gate, up, wd, wg, wu, h1, dh2, g_pre):
    s, d = h1.shape
    nc = D_FF // TN_FF
    ts = min(TS_FF, s)

    def body(dff_ref, gate_ref, up_ref, wd_ref, wg_ref, wu_ref, h1_ref, dh2_ref, g_ref,
             dgate_ref, dup_ref, dh1_ref, dg_ref, acc):
        i = pl.program_id(0)
        j = pl.program_id(1)

        @pl.when((i == 0) & (j == 0))
        def _():
            dg_ref[...] = jnp.zeros_like(dg_ref)

        @pl.when(j == 0)
        def _():
            acc[...] = jnp.zeros_like(acc)

        for r in range(2):
            rows = slice(r * (ts // 2), (r + 1) * (ts // 2))
            dact = _nt(dff_ref[rows, :], wd_ref[...].reshape(TN_FF, d))
            gt = gate_ref[rows, :].astype(F32)
            sg = jax.nn.sigmoid(gt)
            dup_ref[rows, :] = (dact * gt * sg).astype(BF16)
            dgate_ref[rows, :] = (dact * up_ref[rows, :].astype(F32) * (sg * (1.0 + gt * (1.0 - sg)))).astype(BF16)
            acc[rows, :] += (_nn(dgate_ref[rows, :], wg_ref[...].reshape(TN_FF, d))
                             + _nn(dup_ref[rows, :], wu_ref[...].reshape(TN_FF, d)))

        @pl.when(j == nc - 1)
        def _():
            dv, dg = _rms_bwd(h1_ref[...], g_ref[...], acc[...])
            dh1_ref[...] = dh2_ref[...] + dv
            dg_ref[...] += dg

    rowblk = pl.BlockSpec((ts, d), lambda i, j: (i, 0))
    chunk = pl.BlockSpec((ts, TN_FF), lambda i, j: (i, j))
    vec = pl.BlockSpec((1, d), lambda i, j: (0, 0))
    return pl.pallas_call(
        body, grid=(s // ts, nc), name="ffn_bwd",
        out_shape=(jax.ShapeDtypeStruct((s, D_FF), BF16), jax.ShapeDtypeStruct((s, D_FF), BF16),
                   jax.ShapeDtypeStruct((s, d), F32), jax.ShapeDtypeStruct((1, d), F32)),
        in_specs=[rowblk, chunk, chunk, _spec_ff(2), _spec_ff(0), _spec_ff(1), rowblk, rowblk, vec],
        out_specs=(chunk, chunk, rowblk, vec),
        scratch_shapes=[pltpu.VMEM((ts, d), F32)],
        compiler_params=_params(2),
    )(dff, gate, up, wd, wg, wu, h1, dh2, g_pre)


def _post_attn_bwd(dh1, o, a, mpre, wout, wpool, g_post, g_attn, g_pool, pscale):
    s, d = dh1.shape
    sub = TS // TQ

    def body(dh1_ref, o_ref, a_ref, mp_ref, wo_ref, wp_ref, gpost_ref, ga_ref, gp_ref, ps_ref,
             dob_ref, dab_ref, dat_ref, dlt_ref, dmpb_ref, dy_ref, dgpost_ref, dga_ref, dgp_ref, dps_ref):
        i = pl.program_id(0)

        @pl.when(i == 0)
        def _():
            dgpost_ref[...] = jnp.zeros_like(dgpost_ref)
            dga_ref[...] = jnp.zeros_like(dga_ref)
            dgp_ref[...] = jnp.zeros_like(dgp_ref)
            dps_ref[...] = jnp.zeros_like(dps_ref)

        do, dg = _rms_bwd(o_ref[...], gpost_ref[...], dh1_ref[...])
        dgpost_ref[...] += dg
        dob = do.astype(BF16)
        dob_ref[...] = dob
        dmix = _nt(dob, wo_ref[...].reshape(d, d))

        av = a_ref[...]
        da, dg = _rms_bwd(av, ga_ref[...], dmix[:, 0:D_ATTN])
        dga_ref[...] += dg
        dab = da.astype(BF16)
        dab_ref[...] = dab
        dat = dab.T
        hsel = (lax.shift_right_logical(lax.broadcasted_iota(jnp.int32, (HEADS, D_ATTN), 1), 6)
                == lax.broadcasted_iota(jnp.int32, (HEADS, D_ATTN), 0)).astype(F32)
        dlt = lax.dot_general(hsel, da * av, (((1,), (1,)), ((), ())), precision=HIGHEST, preferred_element_type=F32)
        for q in range(sub):
            dlt_ref[q] = dlt[:, q * TQ:(q + 1) * TQ]
            dat_ref[q] = dat[:, q * TQ:(q + 1) * TQ]

        ps = ps_ref[...]
        mp = mp_ref[...]
        dm, dg = _rms_bwd(mp * ps, gp_ref[...], dmix[:, D_ATTN:])
        dgp_ref[...] += dg
        dps_ref[...] += jnp.sum(dm * mp, axis=0, keepdims=True)
        dmpb = (dm * ps).astype(BF16)
        dmpb_ref[...] = dmpb
        for g in range(len(POOL_WINDOWS)):
            cols = slice(g * POOL_CH, (g + 1) * POOL_CH)
            dy_ref[:, cols] = _nt(dmpb[:, cols], wp_ref[g])

    rowblk = pl.BlockSpec((TS, d), _row)
    half = pl.BlockSpec((TS, D_ATTN), _row)
    vec = lambda n: pl.BlockSpec((1, n), _fixed)
    return pl.pallas_call(
        body, grid=(s // TS,), name="post_attn_bwd",
        out_shape=(jax.ShapeDtypeStruct((s, d), BF16), jax.ShapeDtypeStruct((s, D_ATTN), BF16),
                   jax.ShapeDtypeStruct((s // TQ, D_ATTN, TQ), BF16),
                   jax.ShapeDtypeStruct((s // TQ, HEADS, TQ), F32), jax.ShapeDtypeStruct((s, D_POOL), BF16),
                   jax.ShapeDtypeStruct((s, D_POOL), F32), jax.ShapeDtypeStruct((1, d), F32),
                   jax.ShapeDtypeStruct((1, D_ATTN), F32), jax.ShapeDtypeStruct((1, D_POOL), F32),
                   jax.ShapeDtypeStruct((1, D_POOL), F32)),
        in_specs=[rowblk, rowblk, half, half, _spec_square(0),
                  pl.BlockSpec(wpool.shape, lambda i: (0, 0, 0)), vec(d), vec(D_ATTN), vec(D_POOL), vec(D_POOL)],
        out_specs=(rowblk, half, pl.BlockSpec((sub, D_ATTN, TQ), lambda i: (i, 0, 0)),
                   pl.BlockSpec((sub, HEADS, TQ), lambda i: (i, 0, 0)), half, half,
                   vec(d), vec(D_ATTN), vec(D_POOL), vec(D_POOL)),
        compiler_params=_params(1),
    )(dh1, o, a, mpre, wout, wpool, g_post, g_attn, g_pool, pscale)


def _attn_bwd(ka, v, kt3, qat3, q, do, dot3, lset3, dlt3, chip_blocks):
    s = q.shape[0]
    nq = s // TQ
    wide = HEADS * LANES

    def body(ka_ref, v_ref, kt_ref, qat_ref, q_ref, do_ref, dot_ref, lset_ref, dlt_ref, b_ref,
             dqt_ref, dk_ref, dv_ref, dcs_ref, drs_ref, got_ref, dca, dkw, dvw, pt_scr, ptb_scr, dsb_scr,
             stage, send_sems, recv_sems, local_sem):
        j = pl.program_id(0)

        @pl.when(j == 0)
        def _():
            _chips_start(b_ref, got_ref, stage, send_sems, recv_sems, local_sem)
            dqt_ref[...] = jnp.zeros_like(dqt_ref)
            drs_ref[...] = jnp.zeros_like(drs_ref)

        dkw[...] = jnp.zeros_like(dkw)
        dvw[...] = jnp.zeros_like(dvw)
        dca[...] = jnp.zeros_like(dca)

        def tile(i, masked):
            rows = pl.ds(i * TQ, TQ)
            for h in range(HEADS):
                aug = slice(h * AUG, (h + 1) * AUG)
                st = _nn(ka_ref[:, aug], qat_ref[i, aug, :]) - lset_ref[i, h:h + 1, :]
                if masked:
                    st = jnp.where(_causal_in_tile(), st, NEG)
                pt = jnp.exp(st)
                pt_scr[h] = pt
                ptb_scr[h] = pt.astype(BF16)
            for h in range(HEADS):
                hs = slice(h * HEAD_DIM, (h + 1) * HEAD_DIM)
                half = slice(h * LANES, h * LANES + HEAD_DIM)
                dvw[:, half] += _nn(ptb_scr[h], do_ref[rows, hs])
                dst = pt_scr[h] * (_nn(v_ref[:, hs], dot_ref[i, hs, :]) - dlt_ref[i, h:h + 1, :])
                dsb_scr[h] = dst.astype(BF16)
                drs_ref[i, h, 0:1, :] += jnp.sum(dst, axis=0, keepdims=True)
                dca[:, h * LANES:(h + 1) * LANES] += dst[:, 0:LANES] + dst[:, LANES:2 * LANES]
            for h in range(HEADS):
                hs = slice(h * HEAD_DIM, (h + 1) * HEAD_DIM)
                half = slice(h * LANES, h * LANES + HEAD_DIM)
                dkw[:, half] += _nn(dsb_scr[h], q_ref[rows, hs])
                dqt_ref[i, hs, :] += _nn(kt_ref[0, hs, :], dsb_scr[h])

        def step(i, carry):
            tile(i, False)
            return carry

        tile(j, True)
        lax.fori_loop(j + 1, nq, step, 0)
        lane = lax.broadcasted_iota(jnp.int32, (TQ, LANES), 1)
        dcs_all = jnp.zeros((TQ, LANES), F32)
        for h in range(HEADS):
            hs = slice(h * HEAD_DIM, (h + 1) * HEAD_DIM)
            half = slice(h * LANES, h * LANES + HEAD_DIM)
            dk_ref[:, hs] = dkw[:, half]
            dv_ref[:, hs] = dvw[:, half]
            colsum = jnp.sum(dca[:, h * LANES:(h + 1) * LANES], axis=1, keepdims=True)
            dcs_all = jnp.where(lane == h, colsum, dcs_all)
        dcs_ref[...] = dcs_all

        @pl.when(j == nq - 1)
        def _():
            _chips_finish(b_ref, got_ref, send_sems, recv_sems)

    blk = pl.BlockSpec((TQ, D_ATTN), _row)
    _, r, cdim = chip_blocks.shape
    return pl.pallas_call(
        body, grid=(nq,), name="attn_bwd",
        out_shape=(jax.ShapeDtypeStruct((nq, D_ATTN, TQ), F32), jax.ShapeDtypeStruct((s, D_ATTN), F32),
                   jax.ShapeDtypeStruct((s, D_ATTN), F32), jax.ShapeDtypeStruct((s, LANES), F32),
                   jax.ShapeDtypeStruct((nq, HEADS, 8, TQ), F32),
                   jax.ShapeDtypeStruct(chip_blocks.shape, chip_blocks.dtype)),
        in_specs=[pl.BlockSpec((TQ, HEADS * AUG), _row), blk, pl.BlockSpec((1, D_ATTN, TQ), lambda j: (j, 0, 0)),
                  VMEM_WHOLE, VMEM_WHOLE, VMEM_WHOLE, VMEM_WHOLE, VMEM_WHOLE, VMEM_WHOLE, ANY],
        out_specs=(pl.BlockSpec((nq, D_ATTN, TQ), lambda j: (0, 0, 0)), blk, blk, pl.BlockSpec((TQ, LANES), _row),
                   pl.BlockSpec((nq, HEADS, 8, TQ), lambda j: (0, 0, 0, 0)), ANY),
        scratch_shapes=[pltpu.VMEM((TQ, wide), F32), pltpu.VMEM((TQ, wide), F32), pltpu.VMEM((TQ, wide), F32),
                        pltpu.VMEM((HEADS, TQ, TQ), F32), pltpu.VMEM((HEADS, TQ, TQ), BF16),
                        pltpu.VMEM((HEADS, TQ, TQ), BF16), pltpu.VMEM((r, cdim), chip_blocks.dtype),
                        pltpu.SemaphoreType.DMA((3,)), pltpu.SemaphoreType.DMA((3,)), pltpu.SemaphoreType.DMA],
        compiler_params=_params(1),
    )(ka, v, kt3, qat3, q, do, dot3, lset3, dlt3, chip_blocks)


def _pre_attn_bwd(dqt3, dk, dv, dcs, drs, fl, dy, x, dh1, g1, wqkv, wf, wu):
    s, d = x.shape
    nt = s // TS
    n = TS + HALO
    sub = TS // TQ
    qkv, fcols = 3 * D_ATTN, 3 * D_ATTN + LANES

    def body(dqt_ref, dk_ref, dv_ref, dcs_ref, drs_ref, fl_ref, dy_ref, x_ref, dh1_ref, g_ref, wqkv_ref, wf_ref, wu_ref,
             gx_ref, dz_ref, dg_ref, db_ref, ybuf, ccar, dlog):
        dqkv_ref = dz_ref.at[:, 0:qkv]
        dfb_ref = dz_ref.at[:, qkv:fcols]
        dub_ref = dz_ref.at[:, fcols:]
        i = pl.program_id(0)
        ti = nt - 1 - i

        @pl.when(i == 0)
        def _():
            ybuf[TS:n, :] = jnp.zeros((HALO, D_POOL), F32)
            ccar[...] = jnp.zeros_like(ccar)
            dg_ref[...] = jnp.zeros_like(dg_ref)
            db_ref[...] = jnp.zeros_like(db_ref)

        rr = lax.broadcasted_iota(jnp.int32, (TS, TS), 0)
        cc = lax.broadcasted_iota(jnp.int32, (TS, TS), 1)
        dlog[...] = ccar[...] + _mask_matmul((cc >= rr).astype(BF16), drs_ref[...] - dcs_ref[...])
        ccar[...] = dlog[0:1, :]
        df = dlog[...] * jax.nn.sigmoid(-fl_ref[...])
        db_ref[...] += jnp.sum(df, axis=0, keepdims=True)
        dfb = df.astype(BF16)
        dfb_ref[...] = dfb

        t = ti * TS + lax.broadcasted_iota(jnp.int32, (TS, 1), 0)
        dy = dy_ref[...]
        for g, w in enumerate(POOL_WINDOWS):
            cols = slice(g * POOL_CH, (g + 1) * POOL_CH)
            ybuf[0:TS, cols] = dy[:, cols] / jnp.minimum(t + 1, w).astype(F32)
        for g, w in enumerate(POOL_WINDOWS):
            cols = slice(g * POOL_CH, (g + 1) * POOL_CH)
            sm = ybuf[:, cols]
            step = 1
            while step < w:
                sm = sm + pltpu.roll(sm, n - step, 0)
                step *= 2
            dub_ref[:, cols] = (sm[0:TS, :] - dy[:, cols]).astype(BF16)
        ybuf[TS:n, :] = ybuf[0:HALO, :]

        for a in range(sub):
            dqkv_ref[a * TQ:(a + 1) * TQ, 0:D_ATTN] = (dqt_ref[a].T * 0.125).astype(BF16)
        dqkv_ref[:, D_ATTN:2 * D_ATTN] = dk_ref[...].astype(BF16)
        dqkv_ref[:, 2 * D_ATTN:] = dv_ref[...].astype(BF16)
        dhn = _nn(dqkv_ref[...], wqkv_ref[...]) + _nn(dfb, wf_ref[...]) + _nn(dub_ref[...], wu_ref[...])
        dx, dg = _rms_bwd(x_ref[...], g_ref[...], dhn)
        gx_ref[...] = dh1_ref[...] + dx
        dg_ref[...] += dg

    rev = lambda i: (nt - 1 - i, 0)
    blk = lambda w: pl.BlockSpec((TS, w), rev)
    return pl.pallas_call(
        body, grid=(nt,), name="pre_attn_bwd",
        out_shape=(jax.ShapeDtypeStruct((s, d), F32), jax.ShapeDtypeStruct((s, fcols + D_POOL), BF16),
                   jax.ShapeDtypeStruct((1, d), F32), jax.ShapeDtypeStruct((1, LANES), F32)),
        in_specs=[pl.BlockSpec((sub, D_ATTN, TQ), lambda i: (nt - 1 - i, 0, 0)),
                  blk(D_ATTN), blk(D_ATTN), blk(LANES), blk(LANES), blk(LANES), blk(D_POOL), blk(d), blk(d),
                  pl.BlockSpec((1, d), _fixed), pl.BlockSpec(wqkv.shape, _fixed), pl.BlockSpec(wf.shape, _fixed),
                  pl.BlockSpec(wu.shape, _fixed)],
        out_specs=(blk(d), blk(fcols + D_POOL), pl.BlockSpec((1, d), _fixed), pl.BlockSpec((1, LANES), _fixed)),
        scratch_shapes=[pltpu.VMEM((n, D_POOL), F32), pltpu.VMEM((1, LANES), F32), pltpu.VMEM((TS, LANES), F32)],
        compiler_params=_params(1),
    )(dqt3, dk, dv, dcs, drs, fl, dy, x, dh1, g1, wqkv, wf, wu)


def _wgrad(a, b, out_dtype, name):
    s, m = a.shape
    n = b.shape[1]
    tm = max(t for t in range(LANES, min(m, TM_WGRAD) + 1, LANES) if m % t == 0)
    ts = min(TS_WGRAD, s)
    ns = s // ts

    def body(a_ref, b_ref, o_ref, acc):
        i = pl.program_id(1)

        @pl.when(i == 0)
        def _():
            acc[...] = jnp.zeros_like(acc)

        acc[...] += _tn(a_ref[...], b_ref[...])

        @pl.when(i == ns - 1)
        def _():
            o_ref[...] = acc[...].astype(out_dtype)

    return pl.pallas_call(
        body, grid=(m // tm, ns), name=name, out_shape=jax.ShapeDtypeStruct((m, n), out_dtype),
        in_specs=[pl.BlockSpec((ts, tm), lambda j, i: (i, j)), pl.BlockSpec((ts, n), lambda j, i: (i, 0))],
        out_specs=pl.BlockSpec((tm, n), lambda j, i: (j, 0)),
        scratch_shapes=[pltpu.VMEM((tm, n), F32)],
        compiler_params=_params(2),
    )(a, b)


def _adamw(w, g, m, v):
    m = ADAM_B1 * m + (1.0 - ADAM_B1) * g
    v = ADAM_B2 * v + (1.0 - ADAM_B2) * (g * g)
    m_hat = m / (1.0 - ADAM_B1 ** ADAM_STEP)
    v_hat = v / (1.0 - ADAM_B2 ** ADAM_STEP)
    delta = -ADAM_LR * (m_hat / (jnp.sqrt(v_hat) + ADAM_EPS) + ADAM_WD * w)
    return delta, m, v


def _pair_sum(core, t, theirs, tr, name):
    nk, r, c = theirs.shape

    def body(core_ref, a_ref, b_ref, o_ref):
        o_ref[...] = (a_ref[...].astype(F32) + b_ref[...].astype(F32)).astype(BF16)

    blk = pl.BlockSpec((1, tr, c), lambda k, i, core_ref: (k, i, 0))
    return pl.pallas_call(
        body, name=name, out_shape=jax.ShapeDtypeStruct(theirs.shape, BF16),
        grid_spec=pltpu.PrefetchScalarGridSpec(
            num_scalar_prefetch=1, grid=(nk, r // tr),
            in_specs=[pl.BlockSpec((1, tr, c), lambda k, i, core_ref: (2 * k + core_ref[0], i, 0)), blk],
            out_specs=blk),
        compiler_params=_params(2),
    )(core, t, theirs)


def _sum_update(p_ref, w_ref, m_ref, v_ref, g_ref, d_ref, nm_ref, nv_ref):
    g = p_ref[0].astype(F32)
    for k in range(1, p_ref.shape[0]):
        g = g + p_ref[k].astype(F32)
    g_ref[...] = g
    d_ref[...], nm_ref[...], nv_ref[...] = _adamw(w_ref[...], g, m_ref[...], v_ref[...])


def _reduce_update_rest(parts, w, m, v, chip_blocks, small_block):
    nk, r, c = parts.shape
    ns = r // TR_REST

    def body(p_ref, w_ref, m_ref, v_ref, b_ref, sm_ref, g_ref, d_ref, nm_ref, nv_ref, got_ref, all_ref,
             stage_b, stage_s, send_b, recv_b, local_b, send_s, recv_s, local_s):
        i = pl.program_id(0)

        @pl.when(i == 0)
        def _():
            _chips_start(b_ref, got_ref, stage_b, send_b, recv_b, local_b)
            _gather_start(sm_ref, all_ref, stage_s, send_s, recv_s, local_s)

        _sum_update(p_ref, w_ref, m_ref, v_ref, g_ref, d_ref, nm_ref, nv_ref)

        @pl.when(i == ns - 1)
        def _():
            _gather_pass_on(all_ref, send_s, recv_s)
            _chips_finish(b_ref, got_ref, send_b, recv_b)
            _gather_finish(sm_ref, all_ref, send_s, recv_s)

    blk = pl.BlockSpec((TR_REST, c), _row)
    out = jax.ShapeDtypeStruct((r, c), F32)
    dma = pltpu.SemaphoreType.DMA
    return pl.pallas_call(
        body, grid=(ns,), name="reduce_update_rest",
        out_shape=(out,) * 4 + (jax.ShapeDtypeStruct(chip_blocks.shape, chip_blocks.dtype),
                                jax.ShapeDtypeStruct((N_DEV,) + small_block.shape, small_block.dtype)),
        in_specs=[pl.BlockSpec((nk, TR_REST, c), lambda i: (0, i, 0)), blk, blk, blk, ANY, ANY],
        out_specs=(blk,) * 4 + (ANY, ANY),
        scratch_shapes=[pltpu.VMEM(chip_blocks.shape[1:], chip_blocks.dtype), pltpu.VMEM(small_block.shape, small_block.dtype),
                        dma((3,)), dma((3,)), dma, dma((7,)), dma((7,)), dma],
        compiler_params=_params(1),
    )(parts, w, m, v, chip_blocks, small_block)


def _reduce_update_big(parts, w, m, v, tr, name):
    nk, r, c = parts.shape

    def body(p_ref, w_ref, m_ref, v_ref, g_ref, d_ref, nm_ref, nv_ref):
        _sum_update(p_ref, w_ref, m_ref, v_ref, g_ref, d_ref, nm_ref, nv_ref)

    blk = pl.BlockSpec((tr, c), _row)
    out = jax.ShapeDtypeStruct((r, c), F32)
    return pl.pallas_call(
        body, grid=(r // tr,), name=name, out_shape=(out,) * 4,
        in_specs=[pl.BlockSpec((nk, tr, c), lambda i: (0, i, 0)), blk, blk, blk],
        out_specs=(blk,) * 4, compiler_params=_params(1),
    )(parts, w, m, v)


def _reduce_update_small(parts, w, m, v):
    nd = parts.shape[0]

    def body(p_ref, w_ref, m_ref, v_ref, g_ref, d_ref, nm_ref, nv_ref):
        g = p_ref[0]
        for k in range(1, nd):
            g = g + p_ref[k]
        g_ref[...] = g
        d_ref[...], nm_ref[...], nv_ref[...] = _adamw(w_ref[...], g, m_ref[...], v_ref[...])

    out = jax.ShapeDtypeStruct(w.shape, F32)
    return pl.pallas_call(body, name="reduce_update_small", out_shape=(out,) * 4,
                          compiler_params=pltpu.CompilerParams(vmem_limit_bytes=VMEM_LIMIT))(parts, w, m, v)


MESH = pl.DeviceIdType.MESH


def _copy_through_vmem(src_hbm, dst_hbm, stage, sem):
    load = pltpu.make_async_copy(src_hbm, stage, sem)
    load.start()
    load.wait()
    store = pltpu.make_async_copy(stage, dst_hbm, sem)
    store.start()
    store.wait()


class _GatherPlan:
    def __init__(self, x_ref, out_ref, send_sems, recv_sems):
        x, y, c = lax.axis_index("x"), lax.axis_index("y"), lax.axis_index("c")
        self.me, self.sibling, self.c = (x, y, c), (x, y, 1 - c), c
        self.chips = [(1 - x, y), (x, 1 - y), (1 - x, 1 - y)]
        self.x_ref, self.out_ref, self.send_sems, self.recv_sems = x_ref, out_ref, send_sems, recv_sems

    def slot(self, px, py, pc):
        return self.out_ref.at[4 * px + 2 * py + pc]

    def copy(self, k, block, to, src=None):
        return pltpu.make_async_remote_copy(
            src_ref=self.slot(*block) if src is None else src, dst_ref=self.slot(*block),
            send_sem=self.send_sems.at[k], recv_sem=self.recv_sems.at[k], device_id=to, device_id_type=MESH)

    def first(self):
        return [self.copy(0, self.me, self.sibling, src=self.x_ref)] + [
            self.copy(1 + j, self.me, (*chip, self.c), src=self.x_ref) for j, chip in enumerate(self.chips)]

    def passed(self):
        return [self.copy(4 + j, (*chip, self.c), self.sibling) for j, chip in enumerate(self.chips)]


def _gather_start(x_ref, out_ref, stage, send_sems, recv_sems, local_sem):
    plan = _GatherPlan(x_ref, out_ref, send_sems, recv_sems)
    for cp in plan.first():
        cp.start()
    _copy_through_vmem(x_ref, plan.slot(*plan.me), stage, local_sem)


def _gather_pass_on(out_ref, send_sems, recv_sems):
    plan = _GatherPlan(None, out_ref, send_sems, recv_sems)
    passed = plan.passed()
    for j, chip in enumerate(plan.chips):
        plan.copy(1 + j, (*chip, plan.c), plan.me).wait_recv()
        passed[j].start()


def _gather_finish(x_ref, out_ref, send_sems, recv_sems):
    plan = _GatherPlan(x_ref, out_ref, send_sems, recv_sems)
    plan.copy(0, plan.sibling, plan.me).wait_recv()
    for j, chip in enumerate(plan.chips):
        plan.copy(4 + j, (*chip, 1 - plan.c), plan.me).wait_recv()
    for cp in plan.first() + plan.passed():
        cp.wait_send()


def _all_gather(xs, name):
    r, cdim = xs.shape

    def body(x_ref, out_ref, stage, send_sems, recv_sems, local_sem):
        _gather_start(x_ref, out_ref, stage, send_sems, recv_sems, local_sem)
        _gather_pass_on(out_ref, send_sems, recv_sems)
        _gather_finish(x_ref, out_ref, send_sems, recv_sems)

    return pl.pallas_call(
        body, name=name, out_shape=jax.ShapeDtypeStruct((N_DEV, r, cdim), xs.dtype),
        in_specs=[ANY], out_specs=ANY,
        scratch_shapes=[pltpu.VMEM((r, cdim), xs.dtype), pltpu.SemaphoreType.DMA((7,)), pltpu.SemaphoreType.DMA((7,)),
                        pltpu.SemaphoreType.DMA],
        compiler_params=pltpu.CompilerParams(vmem_limit_bytes=VMEM_LIMIT),
    )(xs)


def _rs_pair(t, name):
    _, r, cdim = t.shape

    def body(t_ref, theirs_ref, send_sems, recv_sems):
        x, y, c = lax.axis_index("x"), lax.axis_index("y"), lax.axis_index("c")
        remote = [pltpu.make_async_remote_copy(
            src_ref=t_ref.at[2 * k + (1 - c)], dst_ref=theirs_ref.at[k],
            send_sem=send_sems.at[k], recv_sem=recv_sems.at[k], device_id=(x, y, 1 - c), device_id_type=MESH)
            for k in range(4)]
        for cp in remote:
            cp.start()
        for cp in remote:
            cp.wait()

    return pl.pallas_call(
        body, name=name, out_shape=jax.ShapeDtypeStruct((4, r, cdim), t.dtype), in_specs=[ANY], out_specs=ANY,
        scratch_shapes=[pltpu.SemaphoreType.DMA((4,)), pltpu.SemaphoreType.DMA((4,))],
    )(t)


def _chips_start(b_ref, out_ref, stage, send_sems, recv_sems, local_sem):
    x, y, c = lax.axis_index("x"), lax.axis_index("y"), lax.axis_index("c")
    mychip = 2 * x + y
    for j, (px, py) in enumerate([(1 - x, y), (x, 1 - y), (1 - x, 1 - y)]):
        pltpu.make_async_remote_copy(
            src_ref=b_ref.at[2 * px + py], dst_ref=out_ref.at[mychip],
            send_sem=send_sems.at[j], recv_sem=recv_sems.at[j], device_id=(px, py, c), device_id_type=MESH).start()
    _copy_through_vmem(b_ref.at[mychip], out_ref.at[mychip], stage, local_sem)


def _chips_finish(b_ref, out_ref, send_sems, recv_sems):
    x, y, c = lax.axis_index("x"), lax.axis_index("y"), lax.axis_index("c")
    for j, (px, py) in enumerate([(1 - x, y), (x, 1 - y), (1 - x, 1 - y)]):
        pltpu.make_async_remote_copy(
            src_ref=b_ref.at[2 * px + py], dst_ref=out_ref.at[2 * px + py],
            send_sem=send_sems.at[j], recv_sem=recv_sems.at[j], device_id=(px, py, c), device_id_type=MESH).wait()


def _pad_rows(a, rows):
    return jnp.pad(a, ((0, rows - a.shape[0]), (0, 0)))


def _pack_in(w_in):
    return _pad_rows(w_in[0].T, ROWS_IN)


def _unpack_in(r):
    return r[0:SHARD_IN].T[None]


def _pack_rest(w_out, w_gate, w_up, w_down, w_ple, w_pg):
    head = _pad_rows(jnp.concatenate([w_out[0], w_pg[0], w_ple[0].T.reshape(32, D_MODEL)], axis=0), OFF_GATE)
    return jnp.concatenate([head, w_gate[0].T, w_up[0].T, w_down[0]], axis=0)


def _unpack_rest(r):
    return (r[0:OFF_PG][None], r[OFF_GATE:OFF_UP].T[None], r[OFF_UP:OFF_DOWN].T[None], r[OFF_DOWN:ROWS_REST][None],
            r[OFF_PLE:OFF_PLE + 32].reshape(128, D_PLE).T[None], r[OFF_PG:OFF_PLE][None])


def _pack_small(w_pool, g_mix_pre, g_mix_post, g_ffn_pre, g_ffn_post, g_ple, g_attn, g_pool, pool_scale, b_forget,
                loss=None):
    def row(vrow):
        return jnp.pad(vrow.reshape(1, -1), ((0, 0), (0, D_MODEL - vrow.size)))
    rows = [w_pool.reshape(64, D_MODEL), row(g_mix_pre), row(g_mix_post), row(g_ffn_pre), row(g_ffn_post), row(g_ple),
            row(g_attn), row(g_pool), row(pool_scale), row(b_forget),
            row(loss) if loss is not None else jnp.zeros((1, D_MODEL), F32)]
    return _pad_rows(jnp.concatenate(rows, axis=0), SMALL_ROWS)


def _unpack_small(r):
    return dict(
        w_pool=r[0:64].reshape(1, 4, POOL_CH, POOL_CH), g_mix_pre=r[ROW_G_MIX_PRE:ROW_G_MIX_PRE + 1],
        g_mix_post=r[ROW_G_MIX_POST:ROW_G_MIX_POST + 1], g_ffn_pre=r[ROW_G_FFN_PRE:ROW_G_FFN_PRE + 1],
        g_ffn_post=r[ROW_G_FFN_POST:ROW_G_FFN_POST + 1], g_ple=r[ROW_G_PLE:ROW_G_PLE + 1],
        g_attn_grp=r[ROW_G_ATTN:ROW_G_ATTN + 1, 0:D_ATTN], g_pool_grp=r[ROW_G_POOL:ROW_G_POOL + 1, 0:D_POOL],
        pool_scale=r[ROW_POOL_SCALE:ROW_POOL_SCALE + 1, 0:D_POOL], b_forget=r[ROW_B_FORGET:ROW_B_FORGET + 1, 0:HEADS])


def _step(x, p, tgt, small, in_w, in_m, in_v, rest_w, rest_m, rest_v):
    core = lax.axis_index("c").astype(jnp.int32).reshape(1)
    win_t = _all_gather(in_w.astype(BF16), "gather_w_in")[:, 0:SHARD_IN].reshape(D_IN, D_MODEL)
    wqkv = win_t[0:3 * D_ATTN]
    wf = _pad_rows(win_t[3 * D_ATTN:3 * D_ATTN + HEADS], LANES)
    wu = win_t[3 * D_ATTN + HEADS:]
    wpool = small["w_pool"].astype(BF16)
    bpad = jnp.pad(small["b_forget"], ((0, 0), (0, LANES - HEADS)))

    lay = _attn_layout_constants()
    rest_b = rest_w.astype(BF16)
    hn, q, ka, v, qat3, vt3, kt3, fl, y, mpre, gh = _pre_attn_fwd(x, small["g_mix_pre"], wqkv, wf, wu, bpad, wpool, lay,
                                                                 rest_b[0:OFF_GATE])
    a, lset3, gf = _attn_fwd(ka, qat3, vt3, rest_b[OFF_GATE:])
    wple_t = gh[:, OFF_PLE:OFF_PLE + 32].reshape(D_MODEL, D_PLE)
    mix, o, h1, hn2 = _post_attn_fwd(a, mpre, x, small["g_attn_grp"], small["g_pool_grp"], small["pool_scale"], gh,
                                     small["g_mix_post"], small["g_ffn_pre"])
    gate, up, act, ff, h2 = _ffn_fwd(hn2, gf, gf, gf, h1, small["g_ffn_post"])
    dh2, dff, dgl, dpp, h2b, pb, loss8, dg_ple, dg_ffn_post = _tail_fwd_bwd(
        h2, p, tgt, ff, wple_t, gh, small["g_ple"], small["g_ffn_post"])
    dgate, dup, dh1, dg_ffn_pre = _ffn_bwd(dff, gate, up, gf, gf, gf, h1, dh2, small["g_ffn_pre"])
    dob, dab, dat3, dlt3, dmpb, dy, dg_mix_post, dg_attn, dg_pool, dps = _post_attn_bwd(
        dh1, o, a, mpre, gh, wpool, small["g_mix_post"], small["g_attn_grp"], small["g_pool_grp"], small["pool_scale"])

    nd = N_DEV
    send_rest = jnp.concatenate([
        _wgrad(mix, dob, BF16, "wgrad_out").reshape(nd, 128, D_MODEL),
        _wgrad(h2b, dgl, BF16, "wgrad_ple_gate").reshape(nd, 128, D_MODEL),
        _wgrad(dpp, pb, BF16, "wgrad_ple").reshape(nd, 32, D_MODEL),
        jnp.zeros((nd, OFF_GATE - OFF_PLE - 32, D_MODEL), BF16),
        _wgrad(dgate, hn2, BF16, "wgrad_gate").reshape(nd, SHARD_FF, D_MODEL),
        _wgrad(dup, hn2, BF16, "wgrad_up").reshape(nd, SHARD_FF, D_MODEL),
        _wgrad(act, dff, BF16, "wgrad_down").reshape(nd, SHARD_FF, D_MODEL)], axis=1)
    pair_rest = _pair_sum(core, send_rest, _rs_pair(send_rest, "rs_pair_rest"), TR_REST, "rs_pair_sum_rest")

    dqt3, dk, dv, dcs, drs4, chips_rest = _attn_bwd(ka, v, kt3, qat3, q, dab, dat3, lset3, dlt3, pair_rest)
    drs = jnp.pad(drs4[:, :, 0, :].transpose(0, 2, 1).reshape(-1, HEADS), ((0, 0), (0, LANES - HEADS)))
    gx, dz, dg_mix_pre, db = _pre_attn_bwd(dqt3, dk, dv, dcs, drs, fl, dy, x, dh1, small["g_mix_pre"], wqkv, wf, wu)

    dwz = _wgrad(dz, hn, F32, "wgrad_in")
    dwin_t = jnp.concatenate([dwz[0:3 * D_ATTN], dwz[3 * D_ATTN:3 * D_ATTN + HEADS], dwz[3 * D_ATTN + LANES:]], axis=0)
    send_in = jnp.pad(dwin_t.reshape(nd, SHARD_IN, D_MODEL), ((0, 0), (0, ROWS_IN - SHARD_IN), (0, 0))).astype(BF16)
    pair_in = _pair_sum(core, send_in, _rs_pair(send_in, "rs_pair_in"), ROWS_IN, "rs_pair_sum_in")

    dwp = _wgrad(y, dmpb, F32, "wgrad_pool")
    dw_pool = jnp.stack([dwp[g * POOL_CH:(g + 1) * POOL_CH, g * POOL_CH:(g + 1) * POOL_CH] for g in range(4)])
    small_part = _pack_small(dw_pool, dg_mix_pre, dg_mix_post, dg_ffn_pre, dg_ffn_post, dg_ple, dg_attn, dg_pool, dps,
                             db[:, 0:HEADS], loss8[0:1, 0:1])

    *upd_rest, chips_in, small_all = _reduce_update_rest(chips_rest, rest_w, rest_m, rest_v, pair_in, small_part)
    upd_in = _reduce_update_big(chips_in, in_w, in_m, in_v, ROWS_IN, "reduce_update_in")
    return gx, small_all, upd_in, upd_rest


def kernel(x, p, g_mix_pre, w_in, b_forget, g_attn_grp, g_pool_grp, w_pool, pool_scale, w_out, g_mix_post, g_ffn_pre, w_ffn_gate, w_ffn_up, w_ffn_down, g_ffn_post, w_ple_proj, g_ple, w_ple_gate, loss_target, m_g_mix_pre, m_w_in, m_b_forget, m_g_attn_grp, m_g_pool_grp, m_w_pool, m_pool_scale, m_w_out, m_g_mix_post, m_g_ffn_pre, m_w_ffn_gate, m_w_ffn_up, m_w_ffn_down, m_g_ffn_post, m_w_ple_proj, m_g_ple, m_w_ple_gate, v_g_mix_pre, v_w_in, v_b_forget, v_g_attn_grp, v_g_pool_grp, v_w_pool, v_pool_scale, v_w_out, v_g_mix_post, v_g_ffn_pre, v_w_ffn_gate, v_w_ffn_up, v_w_ffn_down, v_g_ffn_post, v_w_ple_proj, v_g_ple, v_w_ple_gate):
    small = dict(w_pool=w_pool[0], g_mix_pre=g_mix_pre, g_mix_post=g_mix_post, g_ffn_pre=g_ffn_pre,
                 g_ffn_post=g_ffn_post, g_ple=g_ple, g_attn_grp=g_attn_grp, g_pool_grp=g_pool_grp,
                 pool_scale=pool_scale, b_forget=b_forget)
    gx, small_all, upd_in, upd_rest = _step(
        x[0], p[0, 0], loss_target[0], small, _pack_in(w_in), _pack_in(m_w_in), _pack_in(v_w_in),
        _pack_rest(w_out, w_ffn_gate, w_ffn_up, w_ffn_down, w_ple_proj, w_ple_gate),
        _pack_rest(m_w_out, m_w_ffn_gate, m_w_ffn_up, m_w_ffn_down, m_w_ple_proj, m_w_ple_gate),
        _pack_rest(v_w_out, v_w_ffn_gate, v_w_ffn_up, v_w_ffn_down, v_w_ple_proj, v_w_ple_gate))

    sm_w = _pack_small(w_pool, g_mix_pre, g_mix_post, g_ffn_pre, g_ffn_post, g_ple, g_attn_grp, g_pool_grp, pool_scale, b_forget)
    sm_m = _pack_small(m_w_pool, m_g_mix_pre, m_g_mix_post, m_g_ffn_pre, m_g_ffn_post, m_g_ple, m_g_attn_grp, m_g_pool_grp, m_pool_scale, m_b_forget)
    sm_v = _pack_small(v_w_pool, v_g_mix_pre, v_g_mix_post, v_g_ffn_pre, v_g_ffn_post, v_g_ple, v_g_attn_grp, v_g_pool_grp, v_pool_scale, v_b_forget)
    upd_small = _reduce_update_small(small_all, sm_w, sm_m, sm_v)
    loss = upd_small[0][ROW_LOSS, 0]

    def leaves(k):
        b_out, b_gate, b_up, b_down, b_ple, b_pg = _unpack_rest(upd_rest[k])
        s = _unpack_small(upd_small[k])
        return (s["g_mix_pre"], _unpack_in(upd_in[k]), s["b_forget"], s["g_attn_grp"], s["g_pool_grp"], s["w_pool"],
                s["pool_scale"], b_out, s["g_mix_post"], s["g_ffn_pre"], b_gate, b_up, b_down, s["g_ffn_post"], b_ple,
                s["g_ple"], b_pg)

    return (loss, gx[None], *leaves(0), *leaves(1), *leaves(2), *leaves(3))
```

```python
import functools

import jax
import jax.numpy as jnp
from jax import lax
from jax.experimental import pallas as pl
from jax.experimental.pallas import tpu as pltpu

F32 = jnp.float32
BF16 = jnp.bfloat16
HIGHEST = lax.Precision.HIGHEST

D_MODEL = 1024
HEADS = 8
HEAD_DIM = 64
D_ATTN = HEADS * HEAD_DIM
POOL_WINDOWS = (2, 4, 8, 16)
POOL_CH = 128
D_POOL = POOL_CH * len(POOL_WINDOWS)
D_FF = 2816
D_PLE = 256
D_IN = 3 * D_ATTN + HEADS + D_POOL
RMS_EPS = 1e-6
N_DEV = 8

ADAM_LR = 0.001
ADAM_B1 = 0.9
ADAM_B2 = 0.999
ADAM_EPS = 1e-08
ADAM_WD = 0.01
ADAM_STEP = 10

LANES = 128
HALO = 16
TS = 512
TS_FF = 512
TS_WGRAD = 1024
TM_WGRAD = 2176
TQ = 256
TN_FF = 1408
NEG = -1e30
VMEM_LIMIT = 56 * 1024 * 1024

SHARD_IN = 257
ROWS_IN = 272
SHARD_FF = 352
OFF_PG = 128
OFF_PLE = 256
OFF_GATE = SHARD_FF
OFF_UP = 2 * SHARD_FF
OFF_DOWN = 3 * SHARD_FF
ROWS_REST = 4 * SHARD_FF
TR_REST = SHARD_FF

SMALL_ROWS = 80
ROW_G_MIX_PRE, ROW_G_MIX_POST, ROW_G_FFN_PRE, ROW_G_FFN_POST, ROW_G_PLE = 64, 65, 66, 67, 68
ROW_G_ATTN, ROW_G_POOL, ROW_POOL_SCALE, ROW_B_FORGET, ROW_LOSS = 69, 70, 71, 72, 73


def _nn(a, b):
    return jnp.dot(a, b, preferred_element_type=F32)


def _nt(a, b):
    return lax.dot_general(a, b, (((1,), (1,)), ((), ())), preferred_element_type=F32)


def _tn(a, b):
    return lax.dot_general(a, b, (((0,), (0,)), ((), ())), preferred_element_type=F32)


def _rstd(v):
    return lax.rsqrt(jnp.mean(v * v, axis=-1, keepdims=True) + RMS_EPS)


def _rms_bwd(v, g, dy):
    r = _rstd(v)
    vh = v * r
    t = dy * g
    dv = r * (t - vh * jnp.mean(t * vh, axis=-1, keepdims=True))
    return dv, jnp.sum(dy * vh, axis=0, keepdims=True)


def _split3(v):
    hi = v.astype(BF16)
    rest = v - hi.astype(F32)
    mid = rest.astype(BF16)
    return hi, mid, (rest - mid.astype(F32)).astype(BF16)


def _mask_matmul(mask, v):
    hi, mid, lo = _split3(v)
    return _nn(mask, lo) + _nn(mask, mid) + _nn(mask, hi)


def _params(n_grid):
    return pltpu.CompilerParams(dimension_semantics=("arbitrary",) * n_grid, vmem_limit_bytes=VMEM_LIMIT)


def _row(i):
    return (i, 0)


def _fixed(*_):
    return (0, 0)


def _spec_square(part):
    return pl.BlockSpec((N_DEV, 128, D_MODEL), lambda *_: (0, part, 0))


def _spec_ff(part):
    return pl.BlockSpec((TN_FF // SHARD_FF, SHARD_FF, D_MODEL), lambda i, j: (j, part, 0))


assert TS == 2 * TQ and TN_FF % SHARD_FF == 0
_HALVES = (slice(0, TQ), slice(TQ, TS))

VMEM_WHOLE = pl.BlockSpec(memory_space=pltpu.VMEM)
SMEM_WHOLE = pl.BlockSpec(memory_space=pltpu.SMEM)
ANY = pl.BlockSpec(memory_space=pl.ANY)


LOG2E = 1.4426950408889634
VROWS = HEAD_DIM + 16
AUG = 128
BIAS_LANE = HEAD_DIM
ONE_LANE = HEAD_DIM + 3
SPARE_LANE = HEADS


def _attn_layout_constants():
    import numpy as np
    place = np.zeros((D_ATTN, HEADS * AUG), np.float32)
    for r in range(D_ATTN):
        place[r, (r // HEAD_DIM) * AUG + r % HEAD_DIM] = 1.0
    bias_k = np.zeros((3, LANES, HEADS * AUG), np.float32)
    bias_q = np.zeros((3, LANES, HEADS * AUG), np.float32)
    for h in range(HEADS):
        for part in range(3):
            bias_k[part, h, h * AUG + BIAS_LANE + part] = -1.0
            bias_q[part, h, h * AUG + ONE_LANE + part] = 1.0
            bias_k[0, SPARE_LANE, h * AUG + ONE_LANE + part] = 1.0
            bias_q[0, SPARE_LANE, h * AUG + BIAS_LANE + part] = 1.0
    as_bf = lambda a: jnp.asarray(a, BF16)
    return dict(place=as_bf(place), place_t=as_bf(place.T), bias_k=as_bf(bias_k),
                bias_q_t=as_bf(bias_q.transpose(0, 2, 1)))


def _pre_attn_fwd(x, g1, wqkv, wf, wu, bpad, wpool, lay, own_block):
    s, d = x.shape
    nt = s // TS
    sub = TS // TQ

    def body(x_ref, g_ref, wqkv_ref, wf_ref, wu_ref, b_ref, wp_ref, place_ref, place_t_ref, bk_ref, bqt_ref, own_ref,
             hn_ref, q_ref, ka_ref, v_ref, qat_ref, vt_ref, kt_ref, fl_ref, y_ref, mp_ref, all_ref,
             ubuf, ccar, cbuf, stage, send_sems, recv_sems, local_sem):
        i = pl.program_id(0)

        @pl.when(i == 0)
        def _():
            _gather_start(own_ref, all_ref, stage, send_sems, recv_sems, local_sem)
            ubuf[0:HALO, :] = jnp.zeros((HALO, D_POOL), F32)
            ccar[...] = jnp.zeros_like(ccar)

        @pl.when(i == max(nt - 2, 0))
        def _():
            _gather_pass_on(all_ref, send_sems, recv_sems)

        xv = x_ref[...]
        hn = (xv * _rstd(xv) * g_ref[...]).astype(BF16)
        hn_ref[...] = hn
        zq = _nt(hn, wqkv_ref[...])
        q_ref[...] = (zq[:, 0:D_ATTN] * 0.125).astype(BF16)
        qb = (zq[:, 0:D_ATTN] * (0.125 * LOG2E)).astype(BF16)
        kb = zq[:, D_ATTN:2 * D_ATTN].astype(BF16)
        vb = zq[:, 2 * D_ATTN:3 * D_ATTN].astype(BF16)
        v_ref[...] = vb

        fl = _nt(hn, wf_ref[...]) + b_ref[...]
        fl_ref[...] = fl
        logf = jax.nn.log_sigmoid(fl)
        rr = lax.broadcasted_iota(jnp.int32, (TS, TS), 0)
        cc = lax.broadcasted_iota(jnp.int32, (TS, TS), 1)
        c = _mask_matmul((cc <= rr).astype(BF16), logf) + ccar[...]
        cbuf[...] = c
        ccar[...] = cbuf[TS - 1:TS, :]
        hi, mid, lo = _split3(c * LOG2E)
        lane = lax.broadcasted_iota(jnp.int32, (TS, LANES), 1)
        parts = (jnp.where(lane == SPARE_LANE, 1.0, hi).astype(BF16), mid, lo)
        ka = _nn(kb, place_ref[...])
        qat = _nt(place_t_ref[...], qb)
        for part in range(3):
            ka = ka + _nn(parts[part], bk_ref[part])
            qat = qat + _nt(bqt_ref[part], parts[part])
        ka_ref[...] = ka.astype(BF16)
        qat = qat.astype(BF16)
        vt = vb.T
        kt = kb.T
        for a in range(sub):
            cols = slice(a * TQ, (a + 1) * TQ)
            qat_ref[a] = qat[:, cols]
            kt_ref[a] = kt[:, cols]
            for h in range(HEADS):
                vt_ref[a, h * VROWS:h * VROWS + HEAD_DIM, :] = vt[h * HEAD_DIM:(h + 1) * HEAD_DIM, cols]
                vt_ref[a, h * VROWS + HEAD_DIM:(h + 1) * VROWS, :] = jnp.ones((VROWS - HEAD_DIM, TQ), BF16)

        u = _nt(hn, wu_ref[...])
        ubuf[HALO:HALO + TS, :] = u
        t = i * TS + lax.broadcasted_iota(jnp.int32, (TS, 1), 0)
        for g, w in enumerate(POOL_WINDOWS):
            cols = slice(g * POOL_CH, (g + 1) * POOL_CH)
            sm = ubuf[:, cols]
            step = 1
            while step < w:
                sm = sm + pltpu.roll(sm, step, 0)
                step *= 2
            cnt = jnp.minimum(t + 1, w).astype(F32)
            yg = (sm[HALO:, :] / cnt - u[:, cols]).astype(BF16)
            y_ref[:, cols] = yg
            mp_ref[:, cols] = _nn(yg, wp_ref[g])
        ubuf[0:HALO, :] = u[TS - HALO:, :]

        @pl.when(i == nt - 1)
        def _():
            _gather_finish(own_ref, all_ref, send_sems, recv_sems)

    nq = s // TQ
    aug = HEADS * AUG
    outs = (
        jax.ShapeDtypeStruct((s, d), BF16), jax.ShapeDtypeStruct((s, D_ATTN), BF16),
        jax.ShapeDtypeStruct((s, aug), BF16), jax.ShapeDtypeStruct((s, D_ATTN), BF16),
        jax.ShapeDtypeStruct((nq, aug, TQ), BF16), jax.ShapeDtypeStruct((nq, HEADS * VROWS, TQ), BF16),
        jax.ShapeDtypeStruct((nq, D_ATTN, TQ), BF16),
        jax.ShapeDtypeStruct((s, LANES), F32),
        jax.ShapeDtypeStruct((s, D_POOL), BF16), jax.ShapeDtypeStruct((s, D_POOL), F32),
        jax.ShapeDtypeStruct((N_DEV,) + own_block.shape, own_block.dtype),
    )
    fixed3 = lambda i: (0, 0, 0)
    tiles3 = lambda rows: pl.BlockSpec((sub, rows, TQ), lambda i: (i, 0, 0))
    return pl.pallas_call(
        body, grid=(nt,), out_shape=outs, name="pre_attn_fwd",
        in_specs=[pl.BlockSpec((TS, d), _row), pl.BlockSpec((1, d), _fixed),
                  pl.BlockSpec(wqkv.shape, _fixed), pl.BlockSpec(wf.shape, _fixed), pl.BlockSpec(wu.shape, _fixed),
                  pl.BlockSpec((1, LANES), _fixed), pl.BlockSpec(wpool.shape, fixed3),
                  pl.BlockSpec(lay["place"].shape, _fixed), pl.BlockSpec(lay["place_t"].shape, _fixed),
                  pl.BlockSpec(lay["bias_k"].shape, fixed3), pl.BlockSpec(lay["bias_q_t"].shape, fixed3), ANY],
        out_specs=(pl.BlockSpec((TS, d), _row), pl.BlockSpec((TS, D_ATTN), _row),
                   pl.BlockSpec((TS, aug), _row), pl.BlockSpec((TS, D_ATTN), _row),
                   tiles3(aug), tiles3(HEADS * VROWS), tiles3(D_ATTN),
                   pl.BlockSpec((TS, LANES), _row),
                   pl.BlockSpec((TS, D_POOL), _row), pl.BlockSpec((TS, D_POOL), _row), ANY),
        scratch_shapes=[pltpu.VMEM((TS + HALO, D_POOL), F32), pltpu.VMEM((1, LANES), F32), pltpu.VMEM((TS, LANES), F32),
                        pltpu.VMEM(own_block.shape, own_block.dtype),
                        pltpu.SemaphoreType.DMA((7,)), pltpu.SemaphoreType.DMA((7,)), pltpu.SemaphoreType.DMA],
        compiler_params=_params(1),
    )(x, g1, wqkv, wf, wu, bpad, wpool, lay["place"], lay["place_t"], lay["bias_k"], lay["bias_q_t"], own_block)


def _causal_in_tile():
    krow = lax.broadcasted_iota(jnp.int32, (TQ, TQ), 0)
    qcol = lax.broadcasted_iota(jnp.int32, (TQ, TQ), 1)
    return krow <= qcol


def _attn_fwd(ka, qat3, vt3, own_block):
    s = ka.shape[0]
    nq = s // TQ
    pass_on_step = max(nq - 2, 0)

    def body(qa_ref, ka_ref, vt_ref, own_ref, a_ref, lset_ref, all_ref, acc, out_t, st_scr, pt_scr,
             stage, send_sems, recv_sems, local_sem):
        i = pl.program_id(0)

        @pl.when(i == 0)
        def _():
            _gather_start(own_ref, all_ref, stage, send_sems, recv_sems, local_sem)

        @pl.when(i == pass_on_step)
        def _():
            _gather_pass_on(all_ref, send_sems, recv_sems)

        acc[...] = jnp.zeros_like(acc)

        def tile(j, stats, masked):
            tile_max = []
            for h in range(HEADS):
                aug = slice(h * AUG, (h + 1) * AUG)
                st = _nn(ka_ref[pl.ds(j * TQ, TQ), aug], qa_ref[0, aug, :])
                if masked:
                    st = jnp.where(_causal_in_tile(), st, NEG)
                st_scr[h] = st
                tile_max.append(jnp.max(st, axis=0, keepdims=True))
            new, scale = [], []
            for h in range(HEADS):
                m_new = jnp.maximum(stats[h], tile_max[h])
                scale.append(jnp.exp2(stats[h] - m_new))
                pt_scr[h] = jnp.exp2(st_scr[h] - m_new).astype(BF16)
                new.append(m_new)
            for h in range(HEADS):
                rows = slice(h * VROWS, (h + 1) * VROWS)
                acc[rows, :] = scale[h] * acc[rows, :] + _nn(vt_ref[j, rows, :], pt_scr[h])
            return tuple(new)

        init = tuple(jnp.full((1, TQ), NEG, F32) for _ in range(HEADS))
        stats = lax.fori_loop(0, i, functools.partial(tile, masked=False), init)
        stats = tile(i, stats, True)
        for h in range(HEADS):
            denom = acc[h * VROWS + HEAD_DIM:h * VROWS + HEAD_DIM + 1, :]
            out_t[h * HEAD_DIM:(h + 1) * HEAD_DIM, :] = acc[h * VROWS:h * VROWS + HEAD_DIM, :] / denom
            lset_ref[0, h:h + 1, :] = stats[h] + jnp.log2(denom)
        a_ref[...] = out_t[...].T

        @pl.when(i == nq - 1)
        def _():
            _gather_finish(own_ref, all_ref, send_sems, recv_sems)

    r, cdim = own_block.shape
    return pl.pallas_call(
        body, grid=(nq,), name="attn_fwd",
        out_shape=(jax.ShapeDtypeStruct((s, D_ATTN), F32), jax.ShapeDtypeStruct((nq, HEADS, TQ), F32),
                   jax.ShapeDtypeStruct((N_DEV, r, cdim), own_block.dtype)),
        in_specs=[pl.BlockSpec((1, HEADS * AUG, TQ), lambda i: (i, 0, 0)), VMEM_WHOLE, VMEM_WHOLE, ANY],
        out_specs=(pl.BlockSpec((TQ, D_ATTN), _row), pl.BlockSpec((1, HEADS, TQ), lambda i: (i, 0, 0)), ANY),
        scratch_shapes=[pltpu.VMEM((HEADS * VROWS, TQ), F32), pltpu.VMEM((D_ATTN, TQ), F32),
                        pltpu.VMEM((HEADS, TQ, TQ), F32), pltpu.VMEM((HEADS, TQ, TQ), BF16),
                        pltpu.VMEM((r, cdim), own_block.dtype),
                        pltpu.SemaphoreType.DMA((7,)), pltpu.SemaphoreType.DMA((7,)), pltpu.SemaphoreType.DMA],
        compiler_params=_params(1),
    )(qat3, ka, vt3, own_block)


def _post_attn_fwd(a, mpre, x, g_attn, g_pool, pscale, wout, g_post, g_ffn_pre):
    s, d = x.shape

    def body(a_ref, mp_ref, x_ref, ga_ref, gp_ref, ps_ref, wo_ref, gpost_ref, gpre_ref,
             mix_ref, o_ref, h1_ref, hn2_ref):
        for rows in _HALVES:
            av = a_ref[rows, :]
            mix_ref[rows, 0:D_ATTN] = (av * _rstd(av) * ga_ref[...]).astype(BF16)
            mv = mp_ref[rows, :] * ps_ref[...]
            mix_ref[rows, D_ATTN:] = (mv * _rstd(mv) * gp_ref[...]).astype(BF16)
            o = _nn(mix_ref[rows, :], wo_ref[...].reshape(d, d))
            o_ref[rows, :] = o
            h1 = x_ref[rows, :] + o * _rstd(o) * gpost_ref[...]
            h1_ref[rows, :] = h1
            hn2_ref[rows, :] = (h1 * _rstd(h1) * gpre_ref[...]).astype(BF16)

    vec = lambda n: pl.BlockSpec((1, n), _fixed)
    return pl.pallas_call(
        body, grid=(s // TS,), name="post_attn_fwd",
        out_shape=(jax.ShapeDtypeStruct((s, d), BF16), jax.ShapeDtypeStruct((s, d), F32),
                   jax.ShapeDtypeStruct((s, d), F32), jax.ShapeDtypeStruct((s, d), BF16)),
        in_specs=[pl.BlockSpec((TS, D_ATTN), _row), pl.BlockSpec((TS, D_POOL), _row), pl.BlockSpec((TS, d), _row),
                  vec(D_ATTN), vec(D_POOL), vec(D_POOL), _spec_square(0), vec(d), vec(d)],
        out_specs=(pl.BlockSpec((TS, d), _row),) * 4,
        compiler_params=_params(1),
    )(a, mpre, x, g_attn, g_pool, pscale, wout, g_post, g_ffn_pre)


def _ffn_fwd(hn2, wg, wu, wd, h1, g_post):
    s, d = h1.shape
    nc = D_FF // TN_FF
    ts = min(TS_FF, s)

    def body(hn_ref, wg_ref, wu_ref, wd_ref, h1_ref, g_ref, gate_ref, up_ref, act_ref, ff_ref, h2_ref, acc):
        j = pl.program_id(1)

        @pl.when(j == 0)
        def _():
            acc[...] = jnp.zeros_like(acc)

        for r in range(2):
            rows = slice(r * (ts // 2), (r + 1) * (ts // 2))
            hn = hn_ref[rows, :]
            gt = _nt(hn, wg_ref[...].reshape(TN_FF, d))
            up = _nt(hn, wu_ref[...].reshape(TN_FF, d))
            gate_ref[rows, :] = gt.astype(BF16)
            up_ref[rows, :] = up.astype(BF16)
            act_ref[rows, :] = (gt * jax.nn.sigmoid(gt) * up).astype(BF16)
            acc[rows, :] += _nn(act_ref[rows, :], wd_ref[...].reshape(TN_FF, d))

        @pl.when(j == nc - 1)
        def _():
            ff = acc[...]
            ff_ref[...] = ff
            h2_ref[...] = h1_ref[...] + ff * _rstd(ff) * g_ref[...]

    rowblk = pl.BlockSpec((ts, d), lambda i, j: (i, 0))
    chunk = pl.BlockSpec((ts, TN_FF), lambda i, j: (i, j))
    return pl.pallas_call(
        body, grid=(s // ts, nc), name="ffn_fwd",
        out_shape=(jax.ShapeDtypeStruct((s, D_FF), BF16),) * 3 + (jax.ShapeDtypeStruct((s, d), F32),) * 2,
        in_specs=[rowblk, _spec_ff(0), _spec_ff(1), _spec_ff(2), rowblk, pl.BlockSpec((1, d), lambda i, j: (0, 0))],
        out_specs=(chunk, chunk, chunk, rowblk, rowblk),
        scratch_shapes=[pltpu.VMEM((ts, d), F32)],
        compiler_params=_params(2),
    )(hn2, wg, wu, wd, h1, g_post)


def _tail_fwd_bwd(h2, p, tgt, ff, wple, wpg, g_ple, g_ffn_post):
    s, d = h2.shape

    def body(h2_ref, p_ref, t_ref, ff_ref, wple_ref, wpg_ref, gple_ref, gfp_ref,
             dh2_ref, dff_ref, dgl_ref, dpp_ref, h2b_ref, pb_ref, loss_ref, dgple_ref, dgfp_ref):
        i = pl.program_id(0)

        @pl.when(i == 0)
        def _():
            loss_ref[...] = jnp.zeros_like(loss_ref)
            dgple_ref[...] = jnp.zeros_like(dgple_ref)
            dgfp_ref[...] = jnp.zeros_like(dgfp_ref)

        h2 = h2_ref[...]
        h2b = h2.astype(BF16)
        h2b_ref[...] = h2b
        pb = p_ref[...].astype(BF16)
        pb_ref[...] = pb
        pp = _nt(pb, wple_ref[...])
        gple = gple_ref[...]
        e = pp * _rstd(pp) * gple
        wpg = wpg_ref[...].reshape(d, d)
        sg = jax.nn.sigmoid(_nn(h2b, wpg))
        diff = h2 + sg * e - t_ref[...]
        sq = jnp.sum(jnp.sum(diff * diff, axis=1, keepdims=True), axis=0, keepdims=True)
        loss_ref[...] += jnp.broadcast_to(sq * (0.5 / d), loss_ref.shape)
        dh3 = diff * (1.0 / d)
        dgl = (dh3 * e * sg * (1.0 - sg)).astype(BF16)
        dgl_ref[...] = dgl
        dh2 = dh3 + _nt(dgl, wpg)
        dh2_ref[...] = dh2
        dpp, dg = _rms_bwd(pp, gple, dh3 * sg)
        dpp_ref[...] = dpp.astype(BF16)
        dgple_ref[...] += dg
        dff, dg = _rms_bwd(ff_ref[...], gfp_ref[...], dh2)
        dff_ref[...] = dff.astype(BF16)
        dgfp_ref[...] += dg

    rowblk = pl.BlockSpec((TS, d), _row)
    vec = pl.BlockSpec((1, d), _fixed)
    return pl.pallas_call(
        body, grid=(s // TS,), name="tail_fwd_bwd",
        out_shape=(jax.ShapeDtypeStruct((s, d), F32), jax.ShapeDtypeStruct((s, d), BF16),
                   jax.ShapeDtypeStruct((s, d), BF16), jax.ShapeDtypeStruct((s, d), BF16),
                   jax.ShapeDtypeStruct((s, d), BF16), jax.ShapeDtypeStruct((s, D_PLE), BF16),
                   jax.ShapeDtypeStruct((8, LANES), F32), jax.ShapeDtypeStruct((1, d), F32),
                   jax.ShapeDtypeStruct((1, d), F32)),
        in_specs=[rowblk, pl.BlockSpec((TS, D_PLE), _row), rowblk, rowblk,
                  pl.BlockSpec(wple.shape, _fixed), _spec_square(1), vec, vec],
        out_specs=(rowblk, rowblk, rowblk, rowblk, rowblk, pl.BlockSpec((TS, D_PLE), _row),
                   pl.BlockSpec((8, LANES), _fixed), vec, vec),
        compiler_params=_params(1),
    )(h2, p, tgt, ff, wple, wpg, g_ple, g_ffn_post)


def _ffn_bwd(dff, gate, up, wd, wg, wu, h1, dh2, g_pre):
    s, d = h1.shape
    nc = D_FF // TN_FF
    ts = min(TS_FF, s)

    def body(dff_ref, gate_ref, up_ref, wd_ref, wg_ref, wu_ref, h1_ref, dh2_ref, g_ref,
             dgate_ref, dup_ref, dh1_ref, dg_ref, acc):
        i = pl.program_id(0)
        j = pl.program_id(1)

        @pl.when((i == 0) & (j == 0))
        def _():
            dg_ref[...] = jnp.zeros_like(dg_ref)

        @pl.when(j == 0)
        def _():
            acc[...] = jnp.zeros_like(acc)

        for r in range(2):
            rows = slice(r * (ts // 2), (r + 1) * (ts // 2))
            dact = _nt(dff_ref[rows, :], wd_ref[...].reshape(TN_FF, d))
            gt = gate_ref[rows, :].astype(F32)
            sg = jax.nn.sigmoid(gt)
            dup_ref[rows, :] = (dact * gt * sg).astype(BF16)
            dgate_ref[rows, :] = (dact * up_ref[rows, :].astype(F32) * (sg * (1.0 + gt * (1.0 - sg)))).astype(BF16)
            acc[rows, :] += (_nn(dgate_ref[rows, :], wg_ref[...].reshape(TN_FF, d))
                             + _nn(dup_ref[rows, :], wu_ref[...].reshape(TN_FF, d)))

        @pl.when(j == nc - 1)
        def _():
            dv, dg = _rms_bwd(h1_ref[...], g_ref[...], acc[...])
            dh1_ref[...] = dh2_ref[...] + dv
            dg_ref[...] += dg

    rowblk = pl.BlockSpec((ts, d), lambda i, j: (i, 0))
    chunk = pl.BlockSpec((ts, TN_FF), lambda i, j: (i, j))
    vec = pl.BlockSpec((1, d), lambda i, j: (0, 0))
    return pl.pallas_call(
        body, grid=(s // ts, nc), name="ffn_bwd",
        out_shape=(jax.ShapeDtypeStruct((s, D_FF), BF16), jax.ShapeDtypeStruct((s, D_FF), BF16),
                   jax.ShapeDtypeStruct((s, d), F32), jax.ShapeDtypeStruct((1, d), F32)),
        in_specs=[rowblk, chunk, chunk, _spec_ff(2), _spec_ff(0), _spec_ff(1), rowblk, rowblk, vec],
        out_specs=(chunk, chunk, rowblk, vec),
        scratch_shapes=[pltpu.VMEM((ts, d), F32)],
        compiler_params=_params(2),
    )(dff, gate, up, wd, wg, wu, h1, dh2, g_pre)


def _post_attn_bwd(dh1, o, a, mpre, wout, wpool, g_post, g_attn, g_pool, pscale):
    s, d = dh1.shape
    sub = TS // TQ

    def body(dh1_ref, o_ref, a_ref, mp_ref, wo_ref, wp_ref, gpost_ref, ga_ref, gp_ref, ps_ref,
             dob_ref, dab_ref, dat_ref, dlt_ref, dmpb_ref, dy_ref, dgpost_ref, dga_ref, dgp_ref, dps_ref):
        i = pl.program_id(0)

        @pl.when(i == 0)
        def _():
            dgpost_ref[...] = jnp.zeros_like(dgpost_ref)
            dga_ref[...] = jnp.zeros_like(dga_ref)
            dgp_ref[...] = jnp.zeros_like(dgp_ref)
            dps_ref[...] = jnp.zeros_like(dps_ref)

        do, dg = _rms_bwd(o_ref[...], gpost_ref[...], dh1_ref[...])
        dgpost_ref[...] += dg
        dob = do.astype(BF16)
        dob_ref[...] = dob
        dmix = _nt(dob, wo_ref[...].reshape(d, d))

        av = a_ref[...]
        da, dg = _rms_bwd(av, ga_ref[...], dmix[:, 0:D_ATTN])
        dga_ref[...] += dg
        dab = da.astype(BF16)
        dab_ref[...] = dab
        dat = dab.T
        hsel = (lax.shift_right_logical(lax.broadcasted_iota(jnp.int32, (HEADS, D_ATTN), 1), 6)
                == lax.broadcasted_iota(jnp.int32, (HEADS, D_ATTN), 0)).astype(F32)
        dlt = lax.dot_general(hsel, da * av, (((1,), (1,)), ((), ())), precision=HIGHEST, preferred_element_type=F32)
        for q in range(sub):
            dlt_ref[q] = dlt[:, q * TQ:(q + 1) * TQ]
            dat_ref[q] = dat[:, q * TQ:(q + 1) * TQ]

        ps = ps_ref[...]
        mp = mp_ref[...]
        dm, dg = _rms_bwd(mp * ps, gp_ref[...], dmix[:, D_ATTN:])
        dgp_ref[...] += dg
        dps_ref[...] += jnp.sum(dm * mp, axis=0, keepdims=True)
        dmpb = (dm * ps).astype(BF16)
        dmpb_ref[...] = dmpb
        for g in range(len(POOL_WINDOWS)):
            cols = slice(g * POOL_CH, (g + 1) * POOL_CH)
            dy_ref[:, cols] = _nt(dmpb[:, cols], wp_ref[g])

    rowblk = pl.BlockSpec((TS, d), _row)
    half = pl.BlockSpec((TS, D_ATTN), _row)
    vec = lambda n: pl.BlockSpec((1, n), _fixed)
    return pl.pallas_call(
        body, grid=(s // TS,), name="post_attn_bwd",
        out_shape=(jax.ShapeDtypeStruct((s, d), BF16), jax.ShapeDtypeStruct((s, D_ATTN), BF16),
                   jax.ShapeDtypeStruct((s // TQ, D_ATTN, TQ), BF16),
                   jax.ShapeDtypeStruct((s // TQ, HEADS, TQ), F32), jax.ShapeDtypeStruct((s, D_POOL), BF16),
                   jax.ShapeDtypeStruct((s, D_POOL), F32), jax.ShapeDtypeStruct((1, d), F32),
                   jax.ShapeDtypeStruct((1, D_ATTN), F32), jax.ShapeDtypeStruct((1, D_POOL), F32),
                   jax.ShapeDtypeStruct((1, D_POOL), F32)),
        in_specs=[rowblk, rowblk, half, half, _spec_square(0),
                  pl.BlockSpec(wpool.shape, lambda i: (0, 0, 0)), vec(d), vec(D_ATTN), vec(D_POOL), vec(D_POOL)],
        out_specs=(rowblk, half, pl.BlockSpec((sub, D_ATTN, TQ), lambda i: (i, 0, 0)),
                   pl.BlockSpec((sub, HEADS, TQ), lambda i: (i, 0, 0)), half, half,
                   vec(d), vec(D_ATTN), vec(D_POOL), vec(D_POOL)),
        compiler_params=_params(1),
    )(dh1, o, a, mpre, wout, wpool, g_post, g_attn, g_pool, pscale)


def _attn_bwd(ka, v, kt3, qat3, q, do, dot3, lset3, dlt3, chip_blocks):
    s = q.shape[0]
    nq = s // TQ
    wide = HEADS * LANES

    def body(ka_ref, v_ref, kt_ref, qat_ref, q_ref, do_ref, dot_ref, lset_ref, dlt_ref, b_ref,
             dqt_ref, dk_ref, dv_ref, dcs_ref, drs_ref, got_ref, dca, dkw, dvw, pt_scr, ptb_scr, dsb_scr,
             stage, send_sems, recv_sems, local_sem):
        j = pl.program_id(0)

        @pl.when(j == 0)
        def _():
            _chips_start(b_ref, got_ref, stage, send_sems, recv_sems, local_sem)
            dqt_ref[...] = jnp.zeros_like(dqt_ref)
            drs_ref[...] = jnp.zeros_like(drs_ref)

        def tile(i, masked):
            rows = pl.ds(i * TQ, TQ)

            def accumulate(ref, idx, val):
                if masked:
                    ref[idx] = val
                else:
                    ref[idx] += val

            for h in range(HEADS):
                aug = slice(h * AUG, (h + 1) * AUG)
                st = _nn(ka_ref[:, aug], qat_ref[i, aug, :]) - lset_ref[i, h:h + 1, :]
                if masked:
                    st = jnp.where(_causal_in_tile(), st, NEG)
                pt = jnp.exp2(st)
                pt_scr[h] = pt
                ptb_scr[h] = pt.astype(BF16)
            for h in range(HEADS):
                hs = slice(h * HEAD_DIM, (h + 1) * HEAD_DIM)
                half = slice(h * LANES, h * LANES + HEAD_DIM)
                accumulate(dvw, (slice(None), half), _nn(ptb_scr[h], do_ref[rows, hs]))
                dst = pt_scr[h] * (_nn(v_ref[:, hs], dot_ref[i, hs, :]) - dlt_ref[i, h:h + 1, :])
                dsb_scr[h] = dst.astype(BF16)
                drs_ref[i, h, 0:1, :] += jnp.sum(dst, axis=0, keepdims=True)
                accumulate(dca, (slice(None), slice(h * LANES, (h + 1) * LANES)), dst[:, 0:LANES] + dst[:, LANES:2 * LANES])
            for h in range(HEADS):
                hs = slice(h * HEAD_DIM, (h + 1) * HEAD_DIM)
                half = slice(h * LANES, h * LANES + HEAD_DIM)
                accumulate(dkw, (slice(None), half), _nn(dsb_scr[h], q_ref[rows, hs]))
                dqt_ref[i, hs, :] += _nn(kt_ref[0, hs, :], dsb_scr[h])

        def step(i, carry):
            tile(i, False)
            return carry

        tile(j, True)
        lax.fori_loop(j + 1, nq, step, 0)
        lane = lax.broadcasted_iota(jnp.int32, (TQ, LANES), 1)
        dcs_all = jnp.zeros((TQ, LANES), F32)
        for h in range(HEADS):
            hs = slice(h * HEAD_DIM, (h + 1) * HEAD_DIM)
            half = slice(h * LANES, h * LANES + HEAD_DIM)
            dk_ref[:, hs] = dkw[:, half]
            dv_ref[:, hs] = dvw[:, half]
            colsum = jnp.sum(dca[:, h * LANES:(h + 1) * LANES], axis=1, keepdims=True)
            dcs_all = jnp.where(lane == h, colsum, dcs_all)
        dcs_ref[...] = dcs_all

        @pl.when(j == nq - 1)
        def _():
            _chips_finish(b_ref, got_ref, send_sems, recv_sems)

    blk = pl.BlockSpec((TQ, D_ATTN), _row)
    _, r, cdim = chip_blocks.shape
    return pl.pallas_call(
        body, grid=(nq,), name="attn_bwd",
        out_shape=(jax.ShapeDtypeStruct((nq, D_ATTN, TQ), F32), jax.ShapeDtypeStruct((s, D_ATTN), F32),
                   jax.ShapeDtypeStruct((s, D_ATTN), F32), jax.ShapeDtypeStruct((s, LANES), F32),
                   jax.ShapeDtypeStruct((nq, HEADS, 8, TQ), F32),
                   jax.ShapeDtypeStruct(chip_blocks.shape, chip_blocks.dtype)),
        in_specs=[pl.BlockSpec((TQ, HEADS * AUG), _row), blk, pl.BlockSpec((1, D_ATTN, TQ), lambda j: (j, 0, 0)),
                  VMEM_WHOLE, VMEM_WHOLE, VMEM_WHOLE, VMEM_WHOLE, VMEM_WHOLE, VMEM_WHOLE, ANY],
        out_specs=(pl.BlockSpec((nq, D_ATTN, TQ), lambda j: (0, 0, 0)), blk, blk, pl.BlockSpec((TQ, LANES), _row),
                   pl.BlockSpec((nq, HEADS, 8, TQ), lambda j: (0, 0, 0, 0)), ANY),
        scratch_shapes=[pltpu.VMEM((TQ, wide), F32), pltpu.VMEM((TQ, wide), F32), pltpu.VMEM((TQ, wide), F32),
                        pltpu.VMEM((HEADS, TQ, TQ), F32), pltpu.VMEM((HEADS, TQ, TQ), BF16),
                        pltpu.VMEM((HEADS, TQ, TQ), BF16), pltpu.VMEM((r, cdim), chip_blocks.dtype),
                        pltpu.SemaphoreType.DMA((3,)), pltpu.SemaphoreType.DMA((3,)), pltpu.SemaphoreType.DMA],
        compiler_params=_params(1),
    )(ka, v, kt3, qat3, q, do, dot3, lset3, dlt3, chip_blocks)


def _pre_attn_bwd(dqt3, dk, dv, dcs, drs, fl, dy, x, dh1, g1, wqkv, wf, wu):
    s, d = x.shape
    nt = s // TS
    n = TS + HALO
    sub = TS // TQ
    qkv, fcols = 3 * D_ATTN, 3 * D_ATTN + LANES

    def body(dqt_ref, dk_ref, dv_ref, dcs_ref, drs_ref, fl_ref, dy_ref, x_ref, dh1_ref, g_ref, wqkv_ref, wf_ref, wu_ref,
             gx_ref, dz_ref, dg_ref, db_ref, ybuf, ccar, dlog):
        dqkv_ref = dz_ref.at[:, 0:qkv]
        dfb_ref = dz_ref.at[:, qkv:fcols]
        dub_ref = dz_ref.at[:, fcols:]
        i = pl.program_id(0)
        ti = nt - 1 - i

        @pl.when(i == 0)
        def _():
            ybuf[TS:n, :] = jnp.zeros((HALO, D_POOL), F32)
            ccar[...] = jnp.zeros_like(ccar)
            dg_ref[...] = jnp.zeros_like(dg_ref)
            db_ref[...] = jnp.zeros_like(db_ref)

        rr = lax.broadcasted_iota(jnp.int32, (TS, TS), 0)
        cc = lax.broadcasted_iota(jnp.int32, (TS, TS), 1)
        dlog[...] = ccar[...] + _mask_matmul((cc >= rr).astype(BF16), drs_ref[...] - dcs_ref[...])
        ccar[...] = dlog[0:1, :]
        df = dlog[...] * jax.nn.sigmoid(-fl_ref[...])
        db_ref[...] += jnp.sum(df, axis=0, keepdims=True)
        dfb = df.astype(BF16)
        dfb_ref[...] = dfb

        t = ti * TS + lax.broadcasted_iota(jnp.int32, (TS, 1), 0)
        dy = dy_ref[...]
        for g, w in enumerate(POOL_WINDOWS):
            cols = slice(g * POOL_CH, (g + 1) * POOL_CH)
            ybuf[0:TS, cols] = dy[:, cols] / jnp.minimum(t + 1, w).astype(F32)
        for g, w in enumerate(POOL_WINDOWS):
            cols = slice(g * POOL_CH, (g + 1) * POOL_CH)
            sm = ybuf[:, cols]
            step = 1
            while step < w:
                sm = sm + pltpu.roll(sm, n - step, 0)
                step *= 2
            dub_ref[:, cols] = (sm[0:TS, :] - dy[:, cols]).astype(BF16)
        ybuf[TS:n, :] = ybuf[0:HALO, :]

        for a in range(sub):
            dqkv_ref[a * TQ:(a + 1) * TQ, 0:D_ATTN] = (dqt_ref[a].T * 0.125).astype(BF16)
        dqkv_ref[:, D_ATTN:2 * D_ATTN] = dk_ref[...].astype(BF16)
        dqkv_ref[:, 2 * D_ATTN:] = dv_ref[...].astype(BF16)
        dhn = _nn(dqkv_ref[...], wqkv_ref[...]) + _nn(dfb, wf_ref[...]) + _nn(dub_ref[...], wu_ref[...])
        dx, dg = _rms_bwd(x_ref[...], g_ref[...], dhn)
        gx_ref[...] = dh1_ref[...] + dx
        dg_ref[...] += dg

    rev = lambda i: (nt - 1 - i, 0)
    blk = lambda w: pl.BlockSpec((TS, w), rev)
    return pl.pallas_call(
        body, grid=(nt,), name="pre_attn_bwd",
        out_shape=(jax.ShapeDtypeStruct((s, d), F32), jax.ShapeDtypeStruct((s, fcols + D_POOL), BF16),
                   jax.ShapeDtypeStruct((1, d), F32), jax.ShapeDtypeStruct((1, LANES), F32)),
        in_specs=[pl.BlockSpec((sub, D_ATTN, TQ), lambda i: (nt - 1 - i, 0, 0)),
                  blk(D_ATTN), blk(D_ATTN), blk(LANES), blk(LANES), blk(LANES), blk(D_POOL), blk(d), blk(d),
                  pl.BlockSpec((1, d), _fixed), pl.BlockSpec(wqkv.shape, _fixed), pl.BlockSpec(wf.shape, _fixed),
                  pl.BlockSpec(wu.shape, _fixed)],
        out_specs=(blk(d), blk(fcols + D_POOL), pl.BlockSpec((1, d), _fixed), pl.BlockSpec((1, LANES), _fixed)),
        scratch_shapes=[pltpu.VMEM((n, D_POOL), F32), pltpu.VMEM((1, LANES), F32), pltpu.VMEM((TS, LANES), F32)],
        compiler_params=_params(1),
    )(dqt3, dk, dv, dcs, drs, fl, dy, x, dh1, g1, wqkv, wf, wu)


def _wgrad(a, b, out_dtype, name):
    s, m = a.shape
    n = b.shape[1]
    tm = max(t for t in range(LANES, min(m, TM_WGRAD) + 1, LANES) if m % t == 0)
    ts = min(TS_WGRAD, s)
    ns = s // ts

    def body(a_ref, b_ref, o_ref, acc):
        i = pl.program_id(1)

        @pl.when(i == 0)
        def _():
            acc[...] = jnp.zeros_like(acc)

        acc[...] += _tn(a_ref[...], b_ref[...])

        @pl.when(i == ns - 1)
        def _():
            o_ref[...] = acc[...].astype(out_dtype)

    return pl.pallas_call(
        body, grid=(m // tm, ns), name=name, out_shape=jax.ShapeDtypeStruct((m, n), out_dtype),
        in_specs=[pl.BlockSpec((ts, tm), lambda j, i: (i, j)), pl.BlockSpec((ts, n), lambda j, i: (i, 0))],
        out_specs=pl.BlockSpec((tm, n), lambda j, i: (j, 0)),
        scratch_shapes=[pltpu.VMEM((tm, n), F32)],
        compiler_params=_params(2),
    )(a, b)


def _adamw(w, g, m, v):
    m = ADAM_B1 * m + (1.0 - ADAM_B1) * g
    v = ADAM_B2 * v + (1.0 - ADAM_B2) * (g * g)
    m_hat = m / (1.0 - ADAM_B1 ** ADAM_STEP)
    v_hat = v / (1.0 - ADAM_B2 ** ADAM_STEP)
    delta = -ADAM_LR * (m_hat / (jnp.sqrt(v_hat) + ADAM_EPS) + ADAM_WD * w)
    return delta, m, v


def _pair_sum(core, t, theirs, tr, name):
    nk, r, c = theirs.shape

    def body(core_ref, a_ref, b_ref, o_ref):
        o_ref[...] = (a_ref[...].astype(F32) + b_ref[...].astype(F32)).astype(BF16)

    blk = pl.BlockSpec((1, tr, c), lambda k, i, core_ref: (k, i, 0))
    return pl.pallas_call(
        body, name=name, out_shape=jax.ShapeDtypeStruct(theirs.shape, BF16),
        grid_spec=pltpu.PrefetchScalarGridSpec(
            num_scalar_prefetch=1, grid=(nk, r // tr),
            in_specs=[pl.BlockSpec((1, tr, c), lambda k, i, core_ref: (2 * k + core_ref[0], i, 0)), blk],
            out_specs=blk),
        compiler_params=_params(2),
    )(core, t, theirs)


def _sum_update(p_ref, w_ref, m_ref, v_ref, g_ref, d_ref, nm_ref, nv_ref):
    g = p_ref[0].astype(F32)
    for k in range(1, p_ref.shape[0]):
        g = g + p_ref[k].astype(F32)
    g_ref[...] = g
    d_ref[...], nm_ref[...], nv_ref[...] = _adamw(w_ref[...], g, m_ref[...], v_ref[...])


def _reduce_update_rest(parts, w, m, v, chip_blocks, small_block):
    nk, r, c = parts.shape
    ns = r // TR_REST

    def body(p_ref, w_ref, m_ref, v_ref, b_ref, sm_ref, g_ref, d_ref, nm_ref, nv_ref, got_ref, all_ref,
             stage_b, stage_s, send_b, recv_b, local_b, send_s, recv_s, local_s):
        i = pl.program_id(0)

        @pl.when(i == 0)
        def _():
            _chips_start(b_ref, got_ref, stage_b, send_b, recv_b, local_b)
            _gather_start(sm_ref, all_ref, stage_s, send_s, recv_s, local_s)

        _sum_update(p_ref, w_ref, m_ref, v_ref, g_ref, d_ref, nm_ref, nv_ref)

        @pl.when(i == ns - 1)
        def _():
            _gather_pass_on(all_ref, send_s, recv_s)
            _chips_finish(b_ref, got_ref, send_b, recv_b)
            _gather_finish(sm_ref, all_ref, send_s, recv_s)

    blk = pl.BlockSpec((TR_REST, c), _row)
    out = jax.ShapeDtypeStruct((r, c), F32)
    dma = pltpu.SemaphoreType.DMA
    return pl.pallas_call(
        body, grid=(ns,), name="reduce_update_rest",
        out_shape=(out,) * 4 + (jax.ShapeDtypeStruct(chip_blocks.shape, chip_blocks.dtype),
                                jax.ShapeDtypeStruct((N_DEV,) + small_block.shape, small_block.dtype)),
        in_specs=[pl.BlockSpec((nk, TR_REST, c), lambda i: (0, i, 0)), blk, blk, blk, ANY, ANY],
        out_specs=(blk,) * 4 + (ANY, ANY),
        scratch_shapes=[pltpu.VMEM(chip_blocks.shape[1:], chip_blocks.dtype), pltpu.VMEM(small_block.shape, small_block.dtype),
                        dma((3,)), dma((3,)), dma, dma((7,)), dma((7,)), dma],
        compiler_params=_params(1),
    )(parts, w, m, v, chip_blocks, small_block)


def _reduce_update_big(parts, w, m, v, tr, name):
    nk, r, c = parts.shape

    def body(p_ref, w_ref, m_ref, v_ref, g_ref, d_ref, nm_ref, nv_ref):
        _sum_update(p_ref, w_ref, m_ref, v_ref, g_ref, d_ref, nm_ref, nv_ref)

    blk = pl.BlockSpec((tr, c), _row)
    out = jax.ShapeDtypeStruct((r, c), F32)
    return pl.pallas_call(
        body, grid=(r // tr,), name=name, out_shape=(out,) * 4,
        in_specs=[pl.BlockSpec((nk, tr, c), lambda i: (0, i, 0)), blk, blk, blk],
        out_specs=(blk,) * 4, compiler_params=_params(1),
    )(parts, w, m, v)


def _reduce_update_small(parts, w, m, v):
    nd = parts.shape[0]

    def body(p_ref, w_ref, m_ref, v_ref, g_ref, d_ref, nm_ref, nv_ref):
        g = p_ref[0]
        for k in range(1, nd):
            g = g + p_ref[k]
        g_ref[...] = g
        d_ref[...], nm_ref[...], nv_ref[...] = _adamw(w_ref[...], g, m_ref[...], v_ref[...])

    out = jax.ShapeDtypeStruct(w.shape, F32)
    return pl.pallas_call(body, name="reduce_update_small", out_shape=(out,) * 4,
                          compiler_params=pltpu.CompilerParams(vmem_limit_bytes=VMEM_LIMIT))(parts, w, m, v)


MESH = pl.DeviceIdType.MESH


def _copy_through_vmem(src_hbm, dst_hbm, stage, sem):
    load = pltpu.make_async_copy(src_hbm, stage, sem)
    load.start()
    load.wait()
    store = pltpu.make_async_copy(stage, dst_hbm, sem)
    store.start()
    store.wait()


class _GatherPlan:
    def __init__(self, x_ref, out_ref, send_sems, recv_sems):
        x, y, c = lax.axis_index("x"), lax.axis_index("y"), lax.axis_index("c")
        self.me, self.sibling, self.c = (x, y, c), (x, y, 1 - c), c
        self.chips = [(1 - x, y), (x, 1 - y), (1 - x, 1 - y)]
        self.x_ref, self.out_ref, self.send_sems, self.recv_sems = x_ref, out_ref, send_sems, recv_sems

    def slot(self, px, py, pc):
        return self.out_ref.at[4 * px + 2 * py + pc]

    def copy(self, k, block, to, src=None):
        return pltpu.make_async_remote_copy(
            src_ref=self.slot(*block) if src is None else src, dst_ref=self.slot(*block),
            send_sem=self.send_sems.at[k], recv_sem=self.recv_sems.at[k], device_id=to, device_id_type=MESH)

    def first(self):
        return [self.copy(0, self.me, self.sibling, src=self.x_ref)] + [
            self.copy(1 + j, self.me, (*chip, self.c), src=self.x_ref) for j, chip in enumerate(self.chips)]

    def passed(self):
        return [self.copy(4 + j, (*chip, self.c), self.sibling) for j, chip in enumerate(self.chips)]


def _gather_start(x_ref, out_ref, stage, send_sems, recv_sems, local_sem):
    plan = _GatherPlan(x_ref, out_ref, send_sems, recv_sems)
    for cp in plan.first():
        cp.start()
    _copy_through_vmem(x_ref, plan.slot(*plan.me), stage, local_sem)


def _gather_pass_on(out_ref, send_sems, recv_sems):
    plan = _GatherPlan(None, out_ref, send_sems, recv_sems)
    passed = plan.passed()
    for j, chip in enumerate(plan.chips):
        plan.copy(1 + j, (*chip, plan.c), plan.me).wait_recv()
        passed[j].start()


def _gather_finish(x_ref, out_ref, send_sems, recv_sems):
    plan = _GatherPlan(x_ref, out_ref, send_sems, recv_sems)
    plan.copy(0, plan.sibling, plan.me).wait_recv()
    for j, chip in enumerate(plan.chips):
        plan.copy(4 + j, (*chip, 1 - plan.c), plan.me).wait_recv()
    for cp in plan.first() + plan.passed():
        cp.wait_send()


def _all_gather(xs, name):
    r, cdim = xs.shape

    def body(x_ref, out_ref, stage, send_sems, recv_sems, local_sem):
        _gather_start(x_ref, out_ref, stage, send_sems, recv_sems, local_sem)
        _gather_pass_on(out_ref, send_sems, recv_sems)
        _gather_finish(x_ref, out_ref, send_sems, recv_sems)

    return pl.pallas_call(
        body, name=name, out_shape=jax.ShapeDtypeStruct((N_DEV, r, cdim), xs.dtype),
        in_specs=[ANY], out_specs=ANY,
        scratch_shapes=[pltpu.VMEM((r, cdim), xs.dtype), pltpu.SemaphoreType.DMA((7,)), pltpu.SemaphoreType.DMA((7,)),
                        pltpu.SemaphoreType.DMA],
        compiler_params=pltpu.CompilerParams(vmem_limit_bytes=VMEM_LIMIT),
    )(xs)


def _rs_pair(t, name):
    _, r, cdim = t.shape

    def body(t_ref, theirs_ref, send_sems, recv_sems):
        x, y, c = lax.axis_index("x"), lax.axis_index("y"), lax.axis_index("c")
        remote = [pltpu.make_async_remote_copy(
            src_ref=t_ref.at[2 * k + (1 - c)], dst_ref=theirs_ref.at[k],
            send_sem=send_sems.at[k], recv_sem=recv_sems.at[k], device_id=(x, y, 1 - c), device_id_type=MESH)
            for k in range(4)]
        for cp in remote:
            cp.start()
        for cp in remote:
            cp.wait()

    return pl.pallas_call(
        body, name=name, out_shape=jax.ShapeDtypeStruct((4, r, cdim), t.dtype), in_specs=[ANY], out_specs=ANY,
        scratch_shapes=[pltpu.SemaphoreType.DMA((4,)), pltpu.SemaphoreType.DMA((4,))],
    )(t)


def _chips_start(b_ref, out_ref, stage, send_sems, recv_sems, local_sem):
    x, y, c = lax.axis_index("x"), lax.axis_index("y"), lax.axis_index("c")
    mychip = 2 * x + y
    for j, (px, py) in enumerate([(1 - x, y), (x, 1 - y), (1 - x, 1 - y)]):
        pltpu.make_async_remote_copy(
            src_ref=b_ref.at[2 * px + py], dst_ref=out_ref.at[mychip],
            send_sem=send_sems.at[j], recv_sem=recv_sems.at[j], device_id=(px, py, c), device_id_type=MESH).start()
    _copy_through_vmem(b_ref.at[mychip], out_ref.at[mychip], stage, local_sem)


def _chips_finish(b_ref, out_ref, send_sems, recv_sems):
    x, y, c = lax.axis_index("x"), lax.axis_index("y"), lax.axis_index("c")
    for j, (px, py) in enumerate([(1 - x, y), (x, 1 - y), (1 - x, 1 - y)]):
        pltpu.make_async_remote_copy(
            src_ref=b_ref.at[2 * px + py], dst_ref=out_ref.at[2 * px + py],
            send_sem=send_sems.at[j], recv_sem=recv_sems.at[j], device_id=(px, py, c), device_id_type=MESH).wait()


def _pad_rows(a, rows):
    return jnp.pad(a, ((0, rows - a.shape[0]), (0, 0)))


def _pack_in(w_in):
    return _pad_rows(w_in[0].T, ROWS_IN)


def _unpack_in(r):
    return r[0:SHARD_IN].T[None]


def _pack_rest(w_out, w_gate, w_up, w_down, w_ple, w_pg):
    head = _pad_rows(jnp.concatenate([w_out[0], w_pg[0], w_ple[0].T.reshape(32, D_MODEL)], axis=0), OFF_GATE)
    return jnp.concatenate([head, w_gate[0].T, w_up[0].T, w_down[0]], axis=0)


def _unpack_rest(r):
    return (r[0:OFF_PG][None], r[OFF_GATE:OFF_UP].T[None], r[OFF_UP:OFF_DOWN].T[None], r[OFF_DOWN:ROWS_REST][None],
            r[OFF_PLE:OFF_PLE + 32].reshape(128, D_PLE).T[None], r[OFF_PG:OFF_PLE][None])


def _pack_small(w_pool, g_mix_pre, g_mix_post, g_ffn_pre, g_ffn_post, g_ple, g_attn, g_pool, pool_scale, b_forget,
                loss=None):
    def row(vrow):
        return jnp.pad(vrow.reshape(1, -1), ((0, 0), (0, D_MODEL - vrow.size)))
    rows = [w_pool.reshape(64, D_MODEL), row(g_mix_pre), row(g_mix_post), row(g_ffn_pre), row(g_ffn_post), row(g_ple),
            row(g_attn), row(g_pool), row(pool_scale), row(b_forget),
            row(loss) if loss is not None else jnp.zeros((1, D_MODEL), F32)]
    return _pad_rows(jnp.concatenate(rows, axis=0), SMALL_ROWS)


def _unpack_small(r):
    return dict(
        w_pool=r[0:64].reshape(1, 4, POOL_CH, POOL_CH), g_mix_pre=r[ROW_G_MIX_PRE:ROW_G_MIX_PRE + 1],
        g_mix_post=r[ROW_G_MIX_POST:ROW_G_MIX_POST + 1], g_ffn_pre=r[ROW_G_FFN_PRE:ROW_G_FFN_PRE + 1],
        g_ffn_post=r[ROW_G_FFN_POST:ROW_G_FFN_POST + 1], g_ple=r[ROW_G_PLE:ROW_G_PLE + 1],
        g_attn_grp=r[ROW_G_ATTN:ROW_G_ATTN + 1, 0:D_ATTN], g_pool_grp=r[ROW_G_POOL:ROW_G_POOL + 1, 0:D_POOL],
        pool_scale=r[ROW_POOL_SCALE:ROW_POOL_SCALE + 1, 0:D_POOL], b_forget=r[ROW_B_FORGET:ROW_B_FORGET + 1, 0:HEADS])


def _step(x, p, tgt, small, in_w, in_m, in_v, rest_w, rest_m, rest_v):
    core = lax.axis_index("c").astype(jnp.int32).reshape(1)
    win_t = _all_gather(in_w.astype(BF16), "gather_w_in")[:, 0:SHARD_IN].reshape(D_IN, D_MODEL)
    wqkv = win_t[0:3 * D_ATTN]
    wf = _pad_rows(win_t[3 * D_ATTN:3 * D_ATTN + HEADS], LANES)
    wu = win_t[3 * D_ATTN + HEADS:]
    wpool = small["w_pool"].astype(BF16)
    bpad = jnp.pad(small["b_forget"], ((0, 0), (0, LANES - HEADS)))

    lay = _attn_layout_constants()
    rest_b = rest_w.astype(BF16)
    hn, q, ka, v, qat3, vt3, kt3, fl, y, mpre, gh = _pre_attn_fwd(x, small["g_mix_pre"], wqkv, wf, wu, bpad, wpool, lay,
                                                                 rest_b[0:OFF_GATE])
    a, lset3, gf = _attn_fwd(ka, qat3, vt3, rest_b[OFF_GATE:])
    wple_t = gh[:, OFF_PLE:OFF_PLE + 32].reshape(D_MODEL, D_PLE)
    mix, o, h1, hn2 = _post_attn_fwd(a, mpre, x, small["g_attn_grp"], small["g_pool_grp"], small["pool_scale"], gh,
                                     small["g_mix_post"], small["g_ffn_pre"])
    gate, up, act, ff, h2 = _ffn_fwd(hn2, gf, gf, gf, h1, small["g_ffn_post"])
    dh2, dff, dgl, dpp, h2b, pb, loss8, dg_ple, dg_ffn_post = _tail_fwd_bwd(
        h2, p, tgt, ff, wple_t, gh, small["g_ple"], small["g_ffn_post"])
    dgate, dup, dh1, dg_ffn_pre = _ffn_bwd(dff, gate, up, gf, gf, gf, h1, dh2, small["g_ffn_pre"])
    dob, dab, dat3, dlt3, dmpb, dy, dg_mix_post, dg_attn, dg_pool, dps = _post_attn_bwd(
        dh1, o, a, mpre, gh, wpool, small["g_mix_post"], small["g_attn_grp"], small["g_pool_grp"], small["pool_scale"])

    nd = N_DEV
    send_rest = jnp.concatenate([
        _wgrad(mix, dob, BF16, "wgrad_out").reshape(nd, 128, D_MODEL),
        _wgrad(h2b, dgl, BF16, "wgrad_ple_gate").reshape(nd, 128, D_MODEL),
        _wgrad(dpp, pb, BF16, "wgrad_ple").reshape(nd, 32, D_MODEL),
        jnp.zeros((nd, OFF_GATE - OFF_PLE - 32, D_MODEL), BF16),
        _wgrad(dgate, hn2, BF16, "wgrad_gate").reshape(nd, SHARD_FF, D_MODEL),
        _wgrad(dup, hn2, BF16, "wgrad_up").reshape(nd, SHARD_FF, D_MODEL),
        _wgrad(act, dff, BF16, "wgrad_down").reshape(nd, SHARD_FF, D_MODEL)], axis=1)
    pair_rest = _pair_sum(core, send_rest, _rs_pair(send_rest, "rs_pair_rest"), TR_REST, "rs_pair_sum_rest")

    dqt3, dk, dv, dcs, drs4, chips_rest = _attn_bwd(ka, v, kt3, qat3, q, dab, dat3, lset3, dlt3, pair_rest)
    drs = jnp.pad(drs4[:, :, 0, :].transpose(0, 2, 1).reshape(-1, HEADS), ((0, 0), (0, LANES - HEADS)))
    gx, dz, dg_mix_pre, db = _pre_attn_bwd(dqt3, dk, dv, dcs, drs, fl, dy, x, dh1, small["g_mix_pre"], wqkv, wf, wu)

    dwz = _wgrad(dz, hn, F32, "wgrad_in")
    dwin_t = jnp.concatenate([dwz[0:3 * D_ATTN], dwz[3 * D_ATTN:3 * D_ATTN + HEADS], dwz[3 * D_ATTN + LANES:]], axis=0)
    send_in = jnp.pad(dwin_t.reshape(nd, SHARD_IN, D_MODEL), ((0, 0), (0, ROWS_IN - SHARD_IN), (0, 0))).astype(BF16)
    pair_in = _pair_sum(core, send_in, _rs_pair(send_in, "rs_pair_in"), ROWS_IN, "rs_pair_sum_in")

    dwp = _wgrad(y, dmpb, F32, "wgrad_pool")
    dw_pool = jnp.stack([dwp[g * POOL_CH:(g + 1) * POOL_CH, g * POOL_CH:(g + 1) * POOL_CH] for g in range(4)])
    small_part = _pack_small(dw_pool, dg_mix_pre, dg_mix_post, dg_ffn_pre, dg_ffn_post, dg_ple, dg_attn, dg_pool, dps,
                             db[:, 0:HEADS], loss8[0:1, 0:1])

    *upd_rest, chips_in, small_all = _reduce_update_rest(chips_rest, rest_w, rest_m, rest_v, pair_in, small_part)
    upd_in = _reduce_update_big(chips_in, in_w, in_m, in_v, ROWS_IN, "reduce_update_in")
    return gx, small_all, upd_in, upd_rest


def kernel(x, p, g_mix_pre, w_in, b_forget, g_attn_grp, g_pool_grp, w_pool, pool_scale, w_out, g_mix_post, g_ffn_pre, w_ffn_gate, w_ffn_up, w_ffn_down, g_ffn_post, w_ple_proj, g_ple, w_ple_gate, loss_target, m_g_mix_pre, m_w_in, m_b_forget, m_g_attn_grp, m_g_pool_grp, m_w_pool, m_pool_scale, m_w_out, m_g_mix_post, m_g_ffn_pre, m_w_ffn_gate, m_w_ffn_up, m_w_ffn_down, m_g_ffn_post, m_w_ple_proj, m_g_ple, m_w_ple_gate, v_g_mix_pre, v_w_in, v_b_forget, v_g_attn_grp, v_g_pool_grp, v_w_pool, v_pool_scale, v_w_out, v_g_mix_post, v_g_ffn_pre, v_w_ffn_gate, v_w_ffn_up, v_w_ffn_down, v_g_ffn_post, v_w_ple_proj, v_g_ple, v_w_ple_gate):
    small = dict(w_pool=w_pool[0], g_mix_pre=g_mix_pre, g_mix_post=g_mix_post, g_ffn_pre=g_ffn_pre,
                 g_ffn_post=g_ffn_post, g_ple=g_ple, g_attn_grp=g_attn_grp, g_pool_grp=g_pool_grp,
                 pool_scale=pool_scale, b_forget=b_forget)
    gx, small_all, upd_in, upd_rest = _step(
        x[0], p[0, 0], loss_target[0], small, _pack_in(w_in), _pack_in(m_w_in), _pack_in(v_w_in),
        _pack_rest(w_out, w_ffn_gate, w_ffn_up, w_ffn_down, w_ple_proj, w_ple_gate),
        _pack_rest(m_w_out, m_w_ffn_gate, m_w_ffn_up, m_w_ffn_down, m_w_ple_proj, m_w_ple_gate),
        _pack_rest(v_w_out, v_w_ffn_gate, v_w_ffn_up, v_w_ffn_down, v_w_ple_proj, v_w_ple_gate))

    sm_w = _pack_small(w_pool, g_mix_pre, g_mix_post, g_ffn_pre, g_ffn_post, g_ple, g_attn_grp, g_pool_grp, pool_scale, b_forget)
    sm_m = _pack_small(m_w_pool, m_g_mix_pre, m_g_mix_post, m_g_ffn_pre, m_g_ffn_post, m_g_ple, m_g_attn_grp, m_g_pool_grp, m_pool_scale, m_b_forget)
    sm_v = _pack_small(v_w_pool, v_g_mix_pre, v_g_mix_post, v_g_ffn_pre, v_g_ffn_post, v_g_ple, v_g_attn_grp, v_g_pool_grp, v_pool_scale, v_b_forget)
    upd_small = _reduce_update_small(small_all, sm_w, sm_m, sm_v)
    loss = upd_small[0][ROW_LOSS, 0]

    def leaves(k):
        b_out, b_gate, b_up, b_down, b_ple, b_pg = _unpack_rest(upd_rest[k])
        s = _unpack_small(upd_small[k])
        return (s["g_mix_pre"], _unpack_in(upd_in[k]), s["b_forget"], s["g_attn_grp"], s["g_pool_grp"], s["w_pool"],
                s["pool_scale"], b_out, s["g_mix_post"], s["g_ffn_pre"], b_gate, b_up, b_down, s["g_ffn_post"], b_ple,
                s["g_ple"], b_pg)

    return (loss, gx[None], *leaves(0), *leaves(1), *leaves(2), *leaves(3))
```

```python
import functools

import jax
import jax.numpy as jnp
from jax import lax
from jax.experimental import pallas as pl
from jax.experimental.pallas import tpu as pltpu

F32 = jnp.float32
BF16 = jnp.bfloat16
HIGHEST = lax.Precision.HIGHEST

D_MODEL = 1024
HEADS = 8
HEAD_DIM = 64
D_ATTN = HEADS * HEAD_DIM
POOL_WINDOWS = (2, 4, 8, 16)
POOL_CH = 128
D_POOL = POOL_CH * len(POOL_WINDOWS)
D_FF = 2816
D_PLE = 256
D_IN = 3 * D_ATTN + HEADS + D_POOL
RMS_EPS = 1e-6
N_DEV = 8

ADAM_LR = 0.001
ADAM_B1 = 0.9
ADAM_B2 = 0.999
ADAM_EPS = 1e-08
ADAM_WD = 0.01
ADAM_STEP = 10

LANES = 128
HALO = 16
TS = 512
TS_FF = 512
TS_WGRAD = 1024
TM_WGRAD = 2176
TQ = 256
TN_FF = 1408
NEG = -1e30
VMEM_LIMIT = 56 * 1024 * 1024

SHARD_IN = 257
ROWS_IN = 272
SHARD_FF = 352
OFF_PG = 128
OFF_PLE = 256
OFF_GATE = SHARD_FF
OFF_UP = 2 * SHARD_FF
OFF_DOWN = 3 * SHARD_FF
ROWS_REST = 4 * SHARD_FF
TR_REST = SHARD_FF

SMALL_ROWS = 80
ROW_G_MIX_PRE, ROW_G_MIX_POST, ROW_G_FFN_PRE, ROW_G_FFN_POST, ROW_G_PLE = 64, 65, 66, 67, 68
ROW_G_ATTN, ROW_G_POOL, ROW_POOL_SCALE, ROW_B_FORGET, ROW_LOSS = 69, 70, 71, 72, 73


def _nn(a, b):
    return jnp.dot(a, b, preferred_element_type=F32)


def _nt(a, b):
    return lax.dot_general(a, b, (((1,), (1,)), ((), ())), preferred_element_type=F32)


def _tn(a, b):
    return lax.dot_general(a, b, (((0,), (0,)), ((), ())), preferred_element_type=F32)


def _rstd(v):
    return lax.rsqrt(jnp.mean(v * v, axis=-1, keepdims=True) + RMS_EPS)


def _rms_bwd(v, g, dy):
    r = _rstd(v)
    vh = v * r
    t = dy * g
    dv = r * (t - vh * jnp.mean(t * vh, axis=-1, keepdims=True))
    return dv, jnp.sum(dy * vh, axis=0, keepdims=True)


def _split3(v):
    hi = v.astype(BF16)
    rest = v - hi.astype(F32)
    mid = rest.astype(BF16)
    return hi, mid, (rest - mid.astype(F32)).astype(BF16)


def _mask_matmul(mask, v):
    hi, mid, lo = _split3(v)
    return _nn(mask, lo) + _nn(mask, mid) + _nn(mask, hi)


def _params(n_grid):
    return pltpu.CompilerParams(dimension_semantics=("arbitrary",) * n_grid, vmem_limit_bytes=VMEM_LIMIT)


def _row(i):
    return (i, 0)


def _fixed(*_):
    return (0, 0)


def _spec_square(part):
    return pl.BlockSpec((N_DEV, 128, D_MODEL), lambda *_: (0, part, 0))


def _spec_ff(part):
    return pl.BlockSpec((TN_FF // SHARD_FF, SHARD_FF, D_MODEL), lambda i, j: (j, part, 0))


assert TS == 2 * TQ and TN_FF % SHARD_FF == 0
_HALVES = (slice(0, TQ), slice(TQ, TS))

VMEM_WHOLE = pl.BlockSpec(memory_space=pltpu.VMEM)
SMEM_WHOLE = pl.BlockSpec(memory_space=pltpu.SMEM)
ANY = pl.BlockSpec(memory_space=pl.ANY)


LOG2E = 1.4426950408889634
VROWS = HEAD_DIM + 16
AUG = 128
BIAS_LANE = HEAD_DIM
ONE_LANE = HEAD_DIM + 3
SPARE_LANE = HEADS


def _attn_layout_constants():
    import numpy as np
    place = np.zeros((D_ATTN, HEADS * AUG), np.float32)
    for r in range(D_ATTN):
        place[r, (r // HEAD_DIM) * AUG + r % HEAD_DIM] = 1.0
    bias_k = np.zeros((3, LANES, HEADS * AUG), np.float32)
    bias_q = np.zeros((3, LANES, HEADS * AUG), np.float32)
    for h in range(HEADS):
        for part in range(3):
            bias_k[part, h, h * AUG + BIAS_LANE + part] = -1.0
            bias_q[part, h, h * AUG + ONE_LANE + part] = 1.0
            bias_k[0, SPARE_LANE, h * AUG + ONE_LANE + part] = 1.0
            bias_q[0, SPARE_LANE, h * AUG + BIAS_LANE + part] = 1.0
    as_bf = lambda a: jnp.asarray(a, BF16)
    return dict(place=as_bf(place), place_t=as_bf(place.T), bias_k=as_bf(bias_k),
                bias_q_t=as_bf(bias_q.transpose(0, 2, 1)))


def _pre_attn_fwd(x, g1, wqkv, wf, wu, bpad, wpool, lay, own_block):
    s, d = x.shape
    nt = s // TS
    sub = TS // TQ

    def body(x_ref, g_ref, wqkv_ref, wf_ref, wu_ref, b_ref, wp_ref, place_ref, place_t_ref, bk_ref, bqt_ref, own_ref,
             hn_ref, q_ref, ka_ref, v_ref, qat_ref, vt_ref, kt_ref, fl_ref, y_ref, mp_ref, all_ref,
             ubuf, ccar, cbuf, stage, send_sems, recv_sems, local_sem):
        i = pl.program_id(0)

        @pl.when(i == 0)
        def _():
            _gather_start(own_ref, all_ref, stage, send_sems, recv_sems, local_sem)
            ubuf[0:HALO, :] = jnp.zeros((HALO, D_POOL), F32)
            ccar[...] = jnp.zeros_like(ccar)

        @pl.when(i == max(nt - 2, 0))
        def _():
            _gather_pass_on(all_ref, send_sems, recv_sems)

        xv = x_ref[...]
        hn = (xv * _rstd(xv) * g_ref[...]).astype(BF16)
        hn_ref[...] = hn
        zq = _nt(hn, wqkv_ref[...])
        q_ref[...] = (zq[:, 0:D_ATTN] * 0.125).astype(BF16)
        qb = (zq[:, 0:D_ATTN] * (0.125 * LOG2E)).astype(BF16)
        kb = zq[:, D_ATTN:2 * D_ATTN].astype(BF16)
        vb = zq[:, 2 * D_ATTN:3 * D_ATTN].astype(BF16)
        v_ref[...] = vb

        fl = _nt(hn, wf_ref[...]) + b_ref[...]
        fl_ref[...] = fl
        logf = jax.nn.log_sigmoid(fl)
        rr = lax.broadcasted_iota(jnp.int32, (TS, TS), 0)
        cc = lax.broadcasted_iota(jnp.int32, (TS, TS), 1)
        c = _mask_matmul((cc <= rr).astype(BF16), logf) + ccar[...]
        cbuf[...] = c
        ccar[...] = cbuf[TS - 1:TS, :]
        hi, mid, lo = _split3(c * LOG2E)
        lane = lax.broadcasted_iota(jnp.int32, (TS, LANES), 1)
        parts = (jnp.where(lane == SPARE_LANE, 1.0, hi).astype(BF16), mid, lo)
        ka = _nn(kb, place_ref[...])
        qat = _nt(place_t_ref[...], qb)
        for part in range(3):
            ka = ka + _nn(parts[part], bk_ref[part])
            qat = qat + _nt(bqt_ref[part], parts[part])
        ka_ref[...] = ka.astype(BF16)
        qat = qat.astype(BF16)
        vt = vb.T
        kt = kb.T
        for a in range(sub):
            cols = slice(a * TQ, (a + 1) * TQ)
            qat_ref[a] = qat[:, cols]
            kt_ref[a] = kt[:, cols]
            for h in range(HEADS):
                vt_ref[a, h * VROWS:h * VROWS + HEAD_DIM, :] = vt[h * HEAD_DIM:(h + 1) * HEAD_DIM, cols]
                vt_ref[a, h * VROWS + HEAD_DIM:(h + 1) * VROWS, :] = jnp.ones((VROWS - HEAD_DIM, TQ), BF16)

        u = _nt(hn, wu_ref[...])
        ubuf[HALO:HALO + TS, :] = u
        t = i * TS + lax.broadcasted_iota(jnp.int32, (TS, 1), 0)
        for g, w in enumerate(POOL_WINDOWS):
            cols = slice(g * POOL_CH, (g + 1) * POOL_CH)
            sm = ubuf[:, cols]
            step = 1
            while step < w:
                sm = sm + pltpu.roll(sm, step, 0)
                step *= 2
            cnt = jnp.minimum(t + 1, w).astype(F32)
            yg = (sm[HALO:, :] / cnt - u[:, cols]).astype(BF16)
            y_ref[:, cols] = yg
            mp_ref[:, cols] = _nn(yg, wp_ref[g])
        ubuf[0:HALO, :] = u[TS - HALO:, :]

        @pl.when(i == nt - 1)
        def _():
            _gather_finish(own_ref, all_ref, send_sems, recv_sems)

    nq = s // TQ
    aug = HEADS * AUG
    outs = (
        jax.ShapeDtypeStruct((s, d), BF16), jax.ShapeDtypeStruct((s, D_ATTN), BF16),
        jax.ShapeDtypeStruct((s, aug), BF16), jax.ShapeDtypeStruct((s, D_ATTN), BF16),
        jax.ShapeDtypeStruct((nq, aug, TQ), BF16), jax.ShapeDtypeStruct((nq, HEADS * VROWS, TQ), BF16),
        jax.ShapeDtypeStruct((nq, D_ATTN, TQ), BF16),
        jax.ShapeDtypeStruct((s, LANES), F32),
        jax.ShapeDtypeStruct((s, D_POOL), BF16), jax.ShapeDtypeStruct((s, D_POOL), F32),
        jax.ShapeDtypeStruct((N_DEV,) + own_block.shape, own_block.dtype),
    )
    fixed3 = lambda i: (0, 0, 0)
    tiles3 = lambda rows: pl.BlockSpec((sub, rows, TQ), lambda i: (i, 0, 0))
    return pl.pallas_call(
        body, grid=(nt,), out_shape=outs, name="pre_attn_fwd",
        in_specs=[pl.BlockSpec((TS, d), _row), pl.BlockSpec((1, d), _fixed),
                  pl.BlockSpec(wqkv.shape, _fixed), pl.BlockSpec(wf.shape, _fixed), pl.BlockSpec(wu.shape, _fixed),
                  pl.BlockSpec((1, LANES), _fixed), pl.BlockSpec(wpool.shape, fixed3),
                  pl.BlockSpec(lay["place"].shape, _fixed), pl.BlockSpec(lay["place_t"].shape, _fixed),
                  pl.BlockSpec(lay["bias_k"].shape, fixed3), pl.BlockSpec(lay["bias_q_t"].shape, fixed3), ANY],
        out_specs=(pl.BlockSpec((TS, d), _row), pl.BlockSpec((TS, D_ATTN), _row),
                   pl.BlockSpec((TS, aug), _row), pl.BlockSpec((TS, D_ATTN), _row),
                   tiles3(aug), tiles3(HEADS * VROWS), tiles3(D_ATTN),
                   pl.BlockSpec((TS, LANES), _row),
                   pl.BlockSpec((TS, D_POOL), _row), pl.BlockSpec((TS, D_POOL), _row), ANY),
        scratch_shapes=[pltpu.VMEM((TS + HALO, D_POOL), F32), pltpu.VMEM((1, LANES), F32), pltpu.VMEM((TS, LANES), F32),
                        pltpu.VMEM(own_block.shape, own_block.dtype),
                        pltpu.SemaphoreType.DMA((7,)), pltpu.SemaphoreType.DMA((7,)), pltpu.SemaphoreType.DMA],
        compiler_params=_params(1),
    )(x, g1, wqkv, wf, wu, bpad, wpool, lay["place"], lay["place_t"], lay["bias_k"], lay["bias_q_t"], own_block)


def _causal_in_tile():
    krow = lax.broadcasted_iota(jnp.int32, (TQ, TQ), 0)
    qcol = lax.broadcasted_iota(jnp.int32, (TQ, TQ), 1)
    return krow <= qcol


def _attn_fwd(ka, qat3, vt3, own_block):
    s = ka.shape[0]
    nq = s // TQ
    pass_on_step = max(nq - 2, 0)

    def body(qa_ref, ka_ref, vt_ref, own_ref, a_ref, lset_ref, all_ref, acc, out_t, st_scr, pt_scr,
             stage, send_sems, recv_sems, local_sem):
        i = pl.program_id(0)

        @pl.when(i == 0)
        def _():
            _gather_start(own_ref, all_ref, stage, send_sems, recv_sems, local_sem)

        @pl.when(i == pass_on_step)
        def _():
            _gather_pass_on(all_ref, send_sems, recv_sems)

        acc[...] = jnp.zeros_like(acc)

        def tile(j, stats, masked):
            tile_max = []
            for h in range(HEADS):
                aug = slice(h * AUG, (h + 1) * AUG)
                st = _nn(ka_ref[pl.ds(j * TQ, TQ), aug], qa_ref[0, aug, :])
                if masked:
                    st = jnp.where(_causal_in_tile(), st, NEG)
                st_scr[h] = st
                tile_max.append(jnp.max(st, axis=0, keepdims=True))
            new, scale = [], []
            for h in range(HEADS):
                m_new = jnp.maximum(stats[h], tile_max[h])
                scale.append(jnp.exp2(stats[h] - m_new))
                pt_scr[h] = jnp.exp2(st_scr[h] - m_new).astype(BF16)
                new.append(m_new)
            for h in range(HEADS):
                rows = slice(h * VROWS, (h + 1) * VROWS)
                acc[rows, :] = scale[h] * acc[rows, :] + _nn(vt_ref[j, rows, :], pt_scr[h])
            return tuple(new)

        init = tuple(jnp.full((1, TQ), NEG, F32) for _ in range(HEADS))
        stats = lax.fori_loop(0, i, functools.partial(tile, masked=False), init)
        stats = tile(i, stats, True)
        for h in range(HEADS):
            denom = acc[h * VROWS + HEAD_DIM:h * VROWS + HEAD_DIM + 1, :]
            out_t[h * HEAD_DIM:(h + 1) * HEAD_DIM, :] = acc[h * VROWS:h * VROWS + HEAD_DIM, :] / denom
            lset_ref[0, h:h + 1, :] = stats[h] + jnp.log2(denom)
        a_ref[...] = out_t[...].T

        @pl.when(i == nq - 1)
        def _():
            _gather_finish(own_ref, all_ref, send_sems, recv_sems)

    r, cdim = own_block.shape
    return pl.pallas_call(
        body, grid=(nq,), name="attn_fwd",
        out_shape=(jax.ShapeDtypeStruct((s, D_ATTN), F32), jax.ShapeDtypeStruct((nq, HEADS, TQ), F32),
                   jax.ShapeDtypeStruct((N_DEV, r, cdim), own_block.dtype)),
        in_specs=[pl.BlockSpec((1, HEADS * AUG, TQ), lambda i: (i, 0, 0)), VMEM_WHOLE, VMEM_WHOLE, ANY],
        out_specs=(pl.BlockSpec((TQ, D_ATTN), _row), pl.BlockSpec((1, HEADS, TQ), lambda i: (i, 0, 0)), ANY),
        scratch_shapes=[pltpu.VMEM((HEADS * VROWS, TQ), F32), pltpu.VMEM((D_ATTN, TQ), F32),
                        pltpu.VMEM((HEADS, TQ, TQ), F32), pltpu.VMEM((HEADS, TQ, TQ), BF16),
                        pltpu.VMEM((r, cdim), own_block.dtype),
                        pltpu.SemaphoreType.DMA((7,)), pltpu.SemaphoreType.DMA((7,)), pltpu.SemaphoreType.DMA],
        compiler_params=_params(1),
    )(qat3, ka, vt3, own_block)


def _post_attn_fwd(a, mpre, x, g_attn, g_pool, pscale, wout, g_post, g_ffn_pre):
    s, d = x.shape

    def body(a_ref, mp_ref, x_ref, ga_ref, gp_ref, ps_ref, wo_ref, gpost_ref, gpre_ref,
             mix_ref, o_ref, h1_ref, hn2_ref):
        for rows in _HALVES:
            av = a_ref[rows, :]
            mix_ref[rows, 0:D_ATTN] = (av * _rstd(av) * ga_ref[...]).astype(BF16)
            mv = mp_ref[rows, :] * ps_ref[...]
            mix_ref[rows, D_ATTN:] = (mv * _rstd(mv) * gp_ref[...]).astype(BF16)
            o = _nn(mix_ref[rows, :], wo_ref[...].reshape(d, d))
            o_ref[rows, :] = o
            h1 = x_ref[rows, :] + o * _rstd(o) * gpost_ref[...]
            h1_ref[rows, :] = h1
            hn2_ref[rows, :] = (h1 * _rstd(h1) * gpre_ref[...]).astype(BF16)

    vec = lambda n: pl.BlockSpec((1, n), _fixed)
    return pl.pallas_call(
        body, grid=(s // TS,), name="post_attn_fwd",
        out_shape=(jax.ShapeDtypeStruct((s, d), BF16), jax.ShapeDtypeStruct((s, d), F32),
                   jax.ShapeDtypeStruct((s, d), F32), jax.ShapeDtypeStruct((s, d), BF16)),
        in_specs=[pl.BlockSpec((TS, D_ATTN), _row), pl.BlockSpec((TS, D_POOL), _row), pl.BlockSpec((TS, d), _row),
                  vec(D_ATTN), vec(D_POOL), vec(D_POOL), _spec_square(0), vec(d), vec(d)],
        out_specs=(pl.BlockSpec((TS, d), _row),) * 4,
        compiler_params=_params(1),
    )(a, mpre, x, g_attn, g_pool, pscale, wout, g_post, g_ffn_pre)


def _ffn_fwd(hn2, wg, wu, wd, h1, g_post):
    s, d = h1.shape
    nc = D_FF // TN_FF
    ts = min(TS_FF, s)

    def body(hn_ref, wg_ref, wu_ref, wd_ref, h1_ref, g_ref, gate_ref, up_ref, act_ref, ff_ref, h2_ref, acc):
        j = pl.program_id(1)

        @pl.when(j == 0)
        def _():
            acc[...] = jnp.zeros_like(acc)

        for r in range(2):
            rows = slice(r * (ts // 2), (r + 1) * (ts // 2))
            hn = hn_ref[rows, :]
            gt = _nt(hn, wg_ref[...].reshape(TN_FF, d))
            up = _nt(hn, wu_ref[...].reshape(TN_FF, d))
            gate_ref[rows, :] = gt.astype(BF16)
            up_ref[rows, :] = up.astype(BF16)
            act_ref[rows, :] = (gt * jax.nn.sigmoid(gt) * up).astype(BF16)
            acc[rows, :] += _nn(act_ref[rows, :], wd_ref[...].reshape(TN_FF, d))

        @pl.when(j == nc - 1)
        def _():
            ff = acc[...]
            ff_ref[...] = ff
            h2_ref[...] = h1_ref[...] + ff * _rstd(ff) * g_ref[...]

    rowblk = pl.BlockSpec((ts, d), lambda i, j: (i, 0))
    chunk = pl.BlockSpec((ts, TN_FF), lambda i, j: (i, j))
    return pl.pallas_call(
        body, grid=(s // ts, nc), name="ffn_fwd",
        out_shape=(jax.ShapeDtypeStruct((s, D_FF), BF16),) * 3 + (jax.ShapeDtypeStruct((s, d), F32),) * 2,
        in_specs=[rowblk, _spec_ff(0), _spec_ff(1), _spec_ff(2), rowblk, pl.BlockSpec((1, d), lambda i, j: (0, 0))],
        out_specs=(chunk, chunk, chunk, rowblk, rowblk),
        scratch_shapes=[pltpu.VMEM((ts, d), F32)],
        compiler_params=_params(2),
    )(hn2, wg, wu, wd, h1, g_post)


def _tail_fwd_bwd(h2, p, tgt, ff, wple, wpg, g_ple, g_ffn_post):
    s, d = h2.shape

    def body(h2_ref, p_ref, t_ref, ff_ref, wple_ref, wpg_ref, gple_ref, gfp_ref,
             dh2_ref, dff_ref, dgl_ref, dpp_ref, h2b_ref, pb_ref, loss_ref, dgple_ref, dgfp_ref):
        i = pl.program_id(0)

        @pl.when(i == 0)
        def _():
            loss_ref[...] = jnp.zeros_like(loss_ref)
            dgple_ref[...] = jnp.zeros_like(dgple_ref)
            dgfp_ref[...] = jnp.zeros_like(dgfp_ref)

        h2 = h2_ref[...]
        h2b = h2.astype(BF16)
        h2b_ref[...] = h2b
        pb = p_ref[...].astype(BF16)
        pb_ref[...] = pb
        pp = _nt(pb, wple_ref[...])
        gple = gple_ref[...]
        e = pp * _rstd(pp) * gple
        wpg = wpg_ref[...].reshape(d, d)
        sg = jax.nn.sigmoid(_nn(h2b, wpg))
        diff = h2 + sg * e - t_ref[...]
        sq = jnp.sum(jnp.sum(diff * diff, axis=1, keepdims=True), axis=0, keepdims=True)
        loss_ref[...] += jnp.broadcast_to(sq * (0.5 / d), loss_ref.shape)
        dh3 = diff * (1.0 / d)
        dgl = (dh3 * e * sg * (1.0 - sg)).astype(BF16)
        dgl_ref[...] = dgl
        dh2 = dh3 + _nt(dgl, wpg)
        dh2_ref[...] = dh2
        dpp, dg = _rms_bwd(pp, gple, dh3 * sg)
        dpp_ref[...] = dpp.astype(BF16)
        dgple_ref[...] += dg
        dff, dg = _rms_bwd(ff_ref[...], gfp_ref[...], dh2)
        dff_ref[...] = dff.astype(BF16)
        dgfp_ref[...] += dg

    rowblk = pl.BlockSpec((TS, d), _row)
    vec = pl.BlockSpec((1, d), _fixed)
    return pl.pallas_call(
        body, grid=(s // TS,), name="tail_fwd_bwd",
        out_shape=(jax.ShapeDtypeStruct((s, d), F32), jax.ShapeDtypeStruct((s, d), BF16),
                   jax.ShapeDtypeStruct((s, d), BF16), jax.ShapeDtypeStruct((s, d), BF16),
                   jax.ShapeDtypeStruct((s, d), BF16), jax.ShapeDtypeStruct((s, D_PLE), BF16),
                   jax.ShapeDtypeStruct((8, LANES), F32), jax.ShapeDtypeStruct((1, d), F32),
                   jax.ShapeDtypeStruct((1, d), F32)),
        in_specs=[rowblk, pl.BlockSpec((TS, D_PLE), _row), rowblk, rowblk,
                  pl.BlockSpec(wple.shape, _fixed), _spec_square(1), vec, vec],
        out_specs=(rowblk, rowblk, rowblk, rowblk, rowblk, pl.BlockSpec((TS, D_PLE), _row),
                   pl.BlockSpec((8, LANES), _fixed), vec, vec),
        compiler_params=_params(1),
    )(h2, p, tgt, ff, wple, wpg, g_ple, g_ffn_post)


def _ffn_bwd(dff, gate, up, wd, wg, wu, h1, dh2, g_pre):
    s, d = h1.shape
    nc = D_FF // TN_FF
    ts = min(TS_FF, s)

    def body(dff_ref, gate_ref, up_ref, wd_ref, wg_ref, wu_ref, h1_ref, dh2_ref, g_ref,
             dgate_ref, dup_ref, dh1_ref, dg_ref, acc):
        i = pl.program_id(0)
        j = pl.program_id(1)

        @pl.when((i == 0) & (j == 0))
        def _():
            dg_ref[...] = jnp.zeros_like(dg_ref)

        @pl.when(j == 0)
        def _():
            acc[...] = jnp.zeros_like(acc)

        for r in range(2):
            rows = slice(r * (ts // 2), (r + 1) * (ts // 2))
            dact = _nt(dff_ref[rows, :], wd_ref[...].reshape(TN_FF, d))
            gt = gate_ref[rows, :].astype(F32)
            sg = jax.nn.sigmoid(gt)
            dup_ref[rows, :] = (dact * gt * sg).astype(BF16)
            dgate_ref[rows, :] = (dact * up_ref[rows, :].astype(F32) * (sg * (1.0 + gt * (1.0 - sg)))).astype(BF16)
            acc[rows, :] += (_nn(dgate_ref[rows, :], wg_ref[...].reshape(TN_FF, d))
                             + _nn(dup_ref[rows, :], wu_ref[...].reshape(TN_FF, d)))

        @pl.when(j == nc - 1)
        def _():
            dv, dg = _rms_bwd(h1_ref[...], g_ref[...], acc[...])
            dh1_ref[...] = dh2_ref[...] + dv
            dg_ref[...] += dg

    rowblk = pl.BlockSpec((ts, d), lambda i, j: (i, 0))
    chunk = pl.BlockSpec((ts, TN_FF), lambda i, j: (i, j))
    vec = pl.BlockSpec((1, d), lambda i, j: (0, 0))
    return pl.pallas_call(
        body, grid=(s // ts, nc), name="ffn_bwd",
        out_shape=(jax.ShapeDtypeStruct((s, D_FF), BF16), jax.ShapeDtypeStruct((s, D_FF), BF16),
                   jax.ShapeDtypeStruct((s, d), F32), jax.ShapeDtypeStruct((1, d), F32)),
        in_specs=[rowblk, chunk, chunk, _spec_ff(2), _spec_ff(0), _spec_ff(1), rowblk, rowblk, vec],
        out_specs=(chunk, chunk, rowblk, vec),
        scratch_shapes=[pltpu.VMEM((ts, d), F32)],
        compiler_params=_params(2),
    )(dff, gate, up, wd, wg, wu, h1, dh2, g_pre)


def _post_attn_bwd(dh1, o, a, mpre, wout, wpool, g_post, g_attn, g_pool, pscale):
    s, d = dh1.shape
    sub = TS // TQ

    def body(dh1_ref, o_ref, a_ref, mp_ref, wo_ref, wp_ref, gpost_ref, ga_ref, gp_ref, ps_ref,
             dob_ref, dab_ref, dat_ref, dlt_ref, dmpb_ref, dy_ref, dgpost_ref, dga_ref, dgp_ref, dps_ref):
        i = pl.program_id(0)

        @pl.when(i == 0)
        def _():
            dgpost_ref[...] = jnp.zeros_like(dgpost_ref)
            dga_ref[...] = jnp.zeros_like(dga_ref)
            dgp_ref[...] = jnp.zeros_like(dgp_ref)
            dps_ref[...] = jnp.zeros_like(dps_ref)

        do, dg = _rms_bwd(o_ref[...], gpost_ref[...], dh1_ref[...])
        dgpost_ref[...] += dg
        dob = do.astype(BF16)
        dob_ref[...] = dob
        dmix = _nt(dob, wo_ref[...].reshape(d, d))

        av = a_ref[...]
        da, dg = _rms_bwd(av, ga_ref[...], dmix[:, 0:D_ATTN])
        dga_ref[...] += dg
        dab = da.astype(BF16)
        dab_ref[...] = dab
        dat = dab.T
        hsel = (lax.shift_right_logical(lax.broadcasted_iota(jnp.int32, (HEADS, D_ATTN), 1), 6)
                == lax.broadcasted_iota(jnp.int32, (HEADS, D_ATTN), 0)).astype(F32)
        dlt = lax.dot_general(hsel, da * av, (((1,), (1,)), ((), ())), precision=HIGHEST, preferred_element_type=F32)
        for q in range(sub):
            dlt_ref[q] = dlt[:, q * TQ:(q + 1) * TQ]
            dat_ref[q] = dat[:, q * TQ:(q + 1) * TQ]

        ps = ps_ref[...]
        mp = mp_ref[...]
        dm, dg = _rms_bwd(mp * ps, gp_ref[...], dmix[:, D_ATTN:])
        dgp_ref[...] += dg
        dps_ref[...] += jnp.sum(dm * mp, axis=0, keepdims=True)
        dmpb = (dm * ps).astype(BF16)
        dmpb_ref[...] = dmpb
        for g in range(len(POOL_WINDOWS)):
            cols = slice(g * POOL_CH, (g + 1) * POOL_CH)
            dy_ref[:, cols] = _nt(dmpb[:, cols], wp_ref[g])

    rowblk = pl.BlockSpec((TS, d), _row)
    half = pl.BlockSpec((TS, D_ATTN), _row)
    vec = lambda n: pl.BlockSpec((1, n), _fixed)
    return pl.pallas_call(
        body, grid=(s // TS,), name="post_attn_bwd",
        out_shape=(jax.ShapeDtypeStruct((s, d), BF16), jax.ShapeDtypeStruct((s, D_ATTN), BF16),
                   jax.ShapeDtypeStruct((s // TQ, D_ATTN, TQ), BF16),
                   jax.ShapeDtypeStruct((s // TQ, HEADS, TQ), F32), jax.ShapeDtypeStruct((s, D_POOL), BF16),
                   jax.ShapeDtypeStruct((s, D_POOL), F32), jax.ShapeDtypeStruct((1, d), F32),
                   jax.ShapeDtypeStruct((1, D_ATTN), F32), jax.ShapeDtypeStruct((1, D_POOL), F32),
                   jax.ShapeDtypeStruct((1, D_POOL), F32)),
        in_specs=[rowblk, rowblk, half, half, _spec_square(0),
                  pl.BlockSpec(wpool.shape, lambda i: (0, 0, 0)), vec(d), vec(D_ATTN), vec(D_POOL), vec(D_POOL)],
        out_specs=(rowblk, half, pl.BlockSpec((sub, D_ATTN, TQ), lambda i: (i, 0, 0)),
                   pl.BlockSpec((sub, HEADS, TQ), lambda i: (i, 0, 0)), half, half,
                   vec(d), vec(D_ATTN), vec(D_POOL), vec(D_POOL)),
        compiler_params=_params(1),
    )(dh1, o, a, mpre, wout, wpool, g_post, g_attn, g_pool, pscale)


def _attn_bwd(ka, v, kt3, qat3, q, do, dot3, lset3, dlt3, chip_blocks):
    s = q.shape[0]
    nq = s // TQ
    wide = HEADS * LANES

    def body(ka_ref, v_ref, kt_ref, qat_ref, q_ref, do_ref, dot_ref, lset_ref, dlt_ref, b_ref,
             dqt_ref, dk_ref, dv_ref, dcs_ref, drs_ref, got_ref, dca, dkw, dvw, pt_scr, ptb_scr, dsb_scr,
             stage, send_sems, recv_sems, local_sem):
        j = pl.program_id(0)

        @pl.when(j == 0)
        def _():
            _chips_start(b_ref, got_ref, stage, send_sems, recv_sems, local_sem)
            dqt_ref[...] = jnp.zeros_like(dqt_ref)
            drs_ref[...] = jnp.zeros_like(drs_ref)

        def tile(i, masked):
            rows = pl.ds(i * TQ, TQ)

            def accumulate(ref, idx, val):
                if masked:
                    ref[idx] = val
                else:
                    ref[idx] += val

            for h in range(HEADS):
                aug = slice(h * AUG, (h + 1) * AUG)
                st = _nn(ka_ref[:, aug], qat_ref[i, aug, :]) - lset_ref[i, h:h + 1, :]
                if masked:
                    st = jnp.where(_causal_in_tile(), st, NEG)
                pt = jnp.exp2(st)
                pt_scr[h] = pt
                ptb_scr[h] = pt.astype(BF16)
            for h in range(HEADS):
                hs = slice(h * HEAD_DIM, (h + 1) * HEAD_DIM)
                half = slice(h * LANES, h * LANES + HEAD_DIM)
                accumulate(dvw, (slice(None), half), _nn(ptb_scr[h], do_ref[rows, hs]))
                dst = pt_scr[h] * (_nn(v_ref[:, hs], dot_ref[i, hs, :]) - dlt_ref[i, h:h + 1, :])
                dsb_scr[h] = dst.astype(BF16)
                drs_ref[i, h, 0:1, :] += jnp.sum(dst, axis=0, keepdims=True)
                accumulate(dca, (slice(None), slice(h * LANES, (h + 1) * LANES)), dst[:, 0:LANES] + dst[:, LANES:2 * LANES])
            for h in range(HEADS):
                hs = slice(h * HEAD_DIM, (h + 1) * HEAD_DIM)
                half = slice(h * LANES, h * LANES + HEAD_DIM)
                accumulate(dkw, (slice(None), half), _nn(dsb_scr[h], q_ref[rows, hs]))
                dqt_ref[i, hs, :] += _nn(kt_ref[0, hs, :], dsb_scr[h])

        def step(i, carry):
            tile(i, False)
            return carry

        tile(j, True)
        lax.fori_loop(j + 1, nq, step, 0)
        lane = lax.broadcasted_iota(jnp.int32, (TQ, LANES), 1)
        dcs_all = jnp.zeros((TQ, LANES), F32)
        for h in range(HEADS):
            hs = slice(h * HEAD_DIM, (h + 1) * HEAD_DIM)
            half = slice(h * LANES, h * LANES + HEAD_DIM)
            dk_ref[:, hs] = dkw[:, half]
            dv_ref[:, hs] = dvw[:, half]
            colsum = jnp.sum(dca[:, h * LANES:(h + 1) * LANES], axis=1, keepdims=True)
            dcs_all = jnp.where(lane == h, colsum, dcs_all)
        dcs_ref[...] = dcs_all

        @pl.when(j == nq - 1)
        def _():
            _chips_finish(b_ref, got_ref, send_sems, recv_sems)

    blk = pl.BlockSpec((TQ, D_ATTN), _row)
    _, r, cdim = chip_blocks.shape
    return pl.pallas_call(
        body, grid=(nq,), name="attn_bwd",
        out_shape=(jax.ShapeDtypeStruct((nq, D_ATTN, TQ), F32), jax.ShapeDtypeStruct((s, D_ATTN), F32),
                   jax.ShapeDtypeStruct((s, D_ATTN), F32), jax.ShapeDtypeStruct((s, LANES), F32),
                   jax.ShapeDtypeStruct((nq, HEADS, 8, TQ), F32),
                   jax.ShapeDtypeStruct(chip_blocks.shape, chip_blocks.dtype)),
        in_specs=[pl.BlockSpec((TQ, HEADS * AUG), _row), blk, pl.BlockSpec((1, D_ATTN, TQ), lambda j: (j, 0, 0)),
                  VMEM_WHOLE, VMEM_WHOLE, VMEM_WHOLE, VMEM_WHOLE, VMEM_WHOLE, VMEM_WHOLE, ANY],
        out_specs=(pl.BlockSpec((nq, D_ATTN, TQ), lambda j: (0, 0, 0)), blk, blk, pl.BlockSpec((TQ, LANES), _row),
                   pl.BlockSpec((nq, HEADS, 8, TQ), lambda j: (0, 0, 0, 0)), ANY),
        scratch_shapes=[pltpu.VMEM((TQ, wide), F32), pltpu.VMEM((TQ, wide), F32), pltpu.VMEM((TQ, wide), F32),
                        pltpu.VMEM((HEADS, TQ, TQ), F32), pltpu.VMEM((HEADS, TQ, TQ), BF16),
                        pltpu.VMEM((HEADS, TQ, TQ), BF16), pltpu.VMEM((r, cdim), chip_blocks.dtype),
                        pltpu.SemaphoreType.DMA((3,)), pltpu.SemaphoreType.DMA((3,)), pltpu.SemaphoreType.DMA],
        compiler_params=_params(1),
    )(ka, v, kt3, qat3, q, do, dot3, lset3, dlt3, chip_blocks)


def _pre_attn_bwd(dqt3, dk, dv, dcs, drs, fl, dy, x, dh1, g1, wqkv, wf, wu):
    s, d = x.shape
    nt = s // TS
    n = TS + HALO
    sub = TS // TQ
    qkv, fcols = 3 * D_ATTN, 3 * D_ATTN + LANES

    def body(dqt_ref, dk_ref, dv_ref, dcs_ref, drs_ref, fl_ref, dy_ref, x_ref, dh1_ref, g_ref, wqkv_ref, wf_ref, wu_ref,
             gx_ref, dz_ref, dg_ref, db_ref, ybuf, ccar, dlog):
        dqkv_ref = dz_ref.at[:, 0:qkv]
        dfb_ref = dz_ref.at[:, qkv:fcols]
        dub_ref = dz_ref.at[:, fcols:]
        i = pl.program_id(0)
        ti = nt - 1 - i

        @pl.when(i == 0)
        def _():
            ybuf[TS:n, :] = jnp.zeros((HALO, D_POOL), F32)
            ccar[...] = jnp.zeros_like(ccar)
            dg_ref[...] = jnp.zeros_like(dg_ref)
            db_ref[...] = jnp.zeros_like(db_ref)

        rr = lax.broadcasted_iota(jnp.int32, (TS, TS), 0)
        cc = lax.broadcasted_iota(jnp.int32, (TS, TS), 1)
        dlog[...] = ccar[...] + _mask_matmul((cc >= rr).astype(BF16), drs_ref[...] - dcs_ref[...])
        ccar[...] = dlog[0:1, :]
        df = dlog[...] * jax.nn.sigmoid(-fl_ref[...])
        db_ref[...] += jnp.sum(df, axis=0, keepdims=True)
        dfb = df.astype(BF16)
        dfb_ref[...] = dfb

        t = ti * TS + lax.broadcasted_iota(jnp.int32, (TS, 1), 0)
        dy = dy_ref[...]
        for g, w in enumerate(POOL_WINDOWS):
            cols = slice(g * POOL_CH, (g + 1) * POOL_CH)
            ybuf[0:TS, cols] = dy[:, cols] / jnp.minimum(t + 1, w).astype(F32)
        for g, w in enumerate(POOL_WINDOWS):
            cols = slice(g * POOL_CH, (g + 1) * POOL_CH)
            sm = ybuf[:, cols]
            step = 1
            while step < w:
                sm = sm + pltpu.roll(sm, n - step, 0)
                step *= 2
            dub_ref[:, cols] = (sm[0:TS, :] - dy[:, cols]).astype(BF16)
        ybuf[TS:n, :] = ybuf[0:HALO, :]

        for a in range(sub):
            dqkv_ref[a * TQ:(a + 1) * TQ, 0:D_ATTN] = (dqt_ref[a].T * 0.125).astype(BF16)
        dqkv_ref[:, D_ATTN:2 * D_ATTN] = dk_ref[...].astype(BF16)
        dqkv_ref[:, 2 * D_ATTN:] = dv_ref[...].astype(BF16)
        dhn = _nn(dqkv_ref[...], wqkv_ref[...]) + _nn(dfb, wf_ref[...]) + _nn(dub_ref[...], wu_ref[...])
        dx, dg = _rms_bwd(x_ref[...], g_ref[...], dhn)
        gx_ref[...] = dh1_ref[...] + dx
        dg_ref[...] += dg

    rev = lambda i: (nt - 1 - i, 0)
    blk = lambda w: pl.BlockSpec((TS, w), rev)
    return pl.pallas_call(
        body, grid=(nt,), name="pre_attn_bwd",
        out_shape=(jax.ShapeDtypeStruct((s, d), F32), jax.ShapeDtypeStruct((s, fcols + D_POOL), BF16),
                   jax.ShapeDtypeStruct((1, d), F32), jax.ShapeDtypeStruct((1, LANES), F32)),
        in_specs=[pl.BlockSpec((sub, D_ATTN, TQ), lambda i: (nt - 1 - i, 0, 0)),
                  blk(D_ATTN), blk(D_ATTN), blk(LANES), blk(LANES), blk(LANES), blk(D_POOL), blk(d), blk(d),
                  pl.BlockSpec((1, d), _fixed), pl.BlockSpec(wqkv.shape, _fixed), pl.BlockSpec(wf.shape, _fixed),
                  pl.BlockSpec(wu.shape, _fixed)],
        out_specs=(blk(d), blk(fcols + D_POOL), pl.BlockSpec((1, d), _fixed), pl.BlockSpec((1, LANES), _fixed)),
        scratch_shapes=[pltpu.VMEM((n, D_POOL), F32), pltpu.VMEM((1, LANES), F32), pltpu.VMEM((TS, LANES), F32)],
        compiler_params=_params(1),
    )(dqt3, dk, dv, dcs, drs, fl, dy, x, dh1, g1, wqkv, wf, wu)


def _wgrad(a, b, out_dtype, name):
    s, m = a.shape
    n = b.shape[1]
    tm = max(t for t in range(LANES, min(m, TM_WGRAD) + 1, LANES) if m % t == 0)
    ts = min(TS_WGRAD, s)
    ns = s // ts

    def body(a_ref, b_ref, o_ref, acc):
        i = pl.program_id(1)

        @pl.when(i == 0)
        def _():
            acc[...] = jnp.zeros_like(acc)

        acc[...] += _tn(a_ref[...], b_ref[...])

        @pl.when(i == ns - 1)
        def _():
            o_ref[...] = acc[...].astype(out_dtype)

    return pl.pallas_call(
        body, grid=(m // tm, ns), name=name, out_shape=jax.ShapeDtypeStruct((m, n), out_dtype),
        in_specs=[pl.BlockSpec((ts, tm), lambda j, i: (i, j)), pl.BlockSpec((ts, n), lambda j, i: (i, 0))],
        out_specs=pl.BlockSpec((tm, n), lambda j, i: (j, 0)),
        scratch_shapes=[pltpu.VMEM((tm, n), F32)],
        compiler_params=_params(2),
    )(a, b)


def _adamw(w, g, m, v):
    m = ADAM_B1 * m + (1.0 - ADAM_B1) * g
    v = ADAM_B2 * v + (1.0 - ADAM_B2) * (g * g)
    m_hat = m / (1.0 - ADAM_B1 ** ADAM_STEP)
    v_hat = v / (1.0 - ADAM_B2 ** ADAM_STEP)
    delta = -ADAM_LR * (m_hat / (jnp.sqrt(v_hat) + ADAM_EPS) + ADAM_WD * w)
    return delta, m, v


def _sum_update(p_ref, w_ref, m_ref, v_ref, g_ref, d_ref, nm_ref, nv_ref):
    g = p_ref[0].astype(F32)
    for k in range(1, p_ref.shape[0]):
        g = g + p_ref[k].astype(F32)
    g_ref[...] = g
    d_ref[...], nm_ref[...], nv_ref[...] = _adamw(w_ref[...], g, m_ref[...], v_ref[...])


def _reduce_update_rest(parts, w, m, v, chip_blocks, small_block):
    nk, r, c = parts.shape
    ns = r // TR_REST

    def body(p_ref, w_ref, m_ref, v_ref, b_ref, sm_ref, g_ref, d_ref, nm_ref, nv_ref, got_ref, all_ref,
             stage_b, stage_s, send_b, recv_b, local_b, send_s, recv_s, local_s):
        i = pl.program_id(0)

        @pl.when(i == 0)
        def _():
            _chips_start(b_ref, got_ref, stage_b, send_b, recv_b, local_b)
            _gather_start(sm_ref, all_ref, stage_s, send_s, recv_s, local_s)

        _sum_update(p_ref, w_ref, m_ref, v_ref, g_ref, d_ref, nm_ref, nv_ref)

        @pl.when(i == ns - 1)
        def _():
            _gather_pass_on(all_ref, send_s, recv_s)
            _chips_finish(b_ref, got_ref, send_b, recv_b)
            _gather_finish(sm_ref, all_ref, send_s, recv_s)

    blk = pl.BlockSpec((TR_REST, c), _row)
    out = jax.ShapeDtypeStruct((r, c), F32)
    dma = pltpu.SemaphoreType.DMA
    return pl.pallas_call(
        body, grid=(ns,), name="reduce_update_rest",
        out_shape=(out,) * 4 + (jax.ShapeDtypeStruct(chip_blocks.shape, chip_blocks.dtype),
                                jax.ShapeDtypeStruct((N_DEV,) + small_block.shape, small_block.dtype)),
        in_specs=[pl.BlockSpec((nk, TR_REST, c), lambda i: (0, i, 0)), blk, blk, blk, ANY, ANY],
        out_specs=(blk,) * 4 + (ANY, ANY),
        scratch_shapes=[pltpu.VMEM(chip_blocks.shape[1:], chip_blocks.dtype), pltpu.VMEM(small_block.shape, small_block.dtype),
                        dma((3,)), dma((3,)), dma, dma((7,)), dma((7,)), dma],
        compiler_params=_params(1),
    )(parts, w, m, v, chip_blocks, small_block)


def _reduce_update_big(parts, w, m, v, tr, name):
    nk, r, c = parts.shape

    def body(p_ref, w_ref, m_ref, v_ref, g_ref, d_ref, nm_ref, nv_ref):
        _sum_update(p_ref, w_ref, m_ref, v_ref, g_ref, d_ref, nm_ref, nv_ref)

    blk = pl.BlockSpec((tr, c), _row)
    out = jax.ShapeDtypeStruct((r, c), F32)
    return pl.pallas_call(
        body, grid=(r // tr,), name=name, out_shape=(out,) * 4,
        in_specs=[pl.BlockSpec((nk, tr, c), lambda i: (0, i, 0)), blk, blk, blk],
        out_specs=(blk,) * 4, compiler_params=_params(1),
    )(parts, w, m, v)


def _reduce_update_small(parts, w, m, v):
    nd = parts.shape[0]

    def body(p_ref, w_ref, m_ref, v_ref, g_ref, d_ref, nm_ref, nv_ref):
        g = p_ref[0]
        for k in range(1, nd):
            g = g + p_ref[k]
        g_ref[...] = g
        d_ref[...], nm_ref[...], nv_ref[...] = _adamw(w_ref[...], g, m_ref[...], v_ref[...])

    out = jax.ShapeDtypeStruct(w.shape, F32)
    return pl.pallas_call(body, name="reduce_update_small", out_shape=(out,) * 4,
                          compiler_params=pltpu.CompilerParams(vmem_limit_bytes=VMEM_LIMIT))(parts, w, m, v)


MESH = pl.DeviceIdType.MESH


def _copy_through_vmem(src_hbm, dst_hbm, stage, sem):
    load = pltpu.make_async_copy(src_hbm, stage, sem)
    load.start()
    load.wait()
    store = pltpu.make_async_copy(stage, dst_hbm, sem)
    store.start()
    store.wait()


class _GatherPlan:
    def __init__(self, x_ref, out_ref, send_sems, recv_sems):
        x, y, c = lax.axis_index("x"), lax.axis_index("y"), lax.axis_index("c")
        self.me, self.sibling, self.c = (x, y, c), (x, y, 1 - c), c
        self.chips = [(1 - x, y), (x, 1 - y), (1 - x, 1 - y)]
        self.x_ref, self.out_ref, self.send_sems, self.recv_sems = x_ref, out_ref, send_sems, recv_sems

    def slot(self, px, py, pc):
        return self.out_ref.at[4 * px + 2 * py + pc]

    def copy(self, k, block, to, src=None):
        return pltpu.make_async_remote_copy(
            src_ref=self.slot(*block) if src is None else src, dst_ref=self.slot(*block),
            send_sem=self.send_sems.at[k], recv_sem=self.recv_sems.at[k], device_id=to, device_id_type=MESH)

    def first(self):
        return [self.copy(0, self.me, self.sibling, src=self.x_ref)] + [
            self.copy(1 + j, self.me, (*chip, self.c), src=self.x_ref) for j, chip in enumerate(self.chips)]

    def passed(self):
        return [self.copy(4 + j, (*chip, self.c), self.sibling) for j, chip in enumerate(self.chips)]


def _gather_start(x_ref, out_ref, stage, send_sems, recv_sems, local_sem):
    plan = _GatherPlan(x_ref, out_ref, send_sems, recv_sems)
    for cp in plan.first():
        cp.start()
    _copy_through_vmem(x_ref, plan.slot(*plan.me), stage, local_sem)


def _gather_pass_on(out_ref, send_sems, recv_sems):
    plan = _GatherPlan(None, out_ref, send_sems, recv_sems)
    passed = plan.passed()
    for j, chip in enumerate(plan.chips):
        plan.copy(1 + j, (*chip, plan.c), plan.me).wait_recv()
        passed[j].start()


def _gather_finish(x_ref, out_ref, send_sems, recv_sems):
    plan = _GatherPlan(x_ref, out_ref, send_sems, recv_sems)
    plan.copy(0, plan.sibling, plan.me).wait_recv()
    for j, chip in enumerate(plan.chips):
        plan.copy(4 + j, (*chip, 1 - plan.c), plan.me).wait_recv()
    for cp in plan.first() + plan.passed():
        cp.wait_send()


def _all_gather(xs, name):
    r, cdim = xs.shape

    def body(x_ref, out_ref, stage, send_sems, recv_sems, local_sem):
        _gather_start(x_ref, out_ref, stage, send_sems, recv_sems, local_sem)
        _gather_pass_on(out_ref, send_sems, recv_sems)
        _gather_finish(x_ref, out_ref, send_sems, recv_sems)

    return pl.pallas_call(
        body, name=name, out_shape=jax.ShapeDtypeStruct((N_DEV, r, cdim), xs.dtype),
        in_specs=[ANY], out_specs=ANY,
        scratch_shapes=[pltpu.VMEM((r, cdim), xs.dtype), pltpu.SemaphoreType.DMA((7,)), pltpu.SemaphoreType.DMA((7,)),
                        pltpu.SemaphoreType.DMA],
        compiler_params=pltpu.CompilerParams(vmem_limit_bytes=VMEM_LIMIT),
    )(xs)


def _rs_pair_sum(core, t, name):
    _, r, cdim = t.shape
    nk = N_DEV // 2

    def body(core_ref, own_ref, t_ref, o_ref, landing, send_sems, recv_sems):
        k = pl.program_id(0)
        x, y, c = lax.axis_index("x"), lax.axis_index("y"), lax.axis_index("c")

        def copy(kk):
            return pltpu.make_async_remote_copy(
                src_ref=t_ref.at[2 * kk + (1 - c)], dst_ref=landing.at[kk],
                send_sem=send_sems.at[kk], recv_sem=recv_sems.at[kk], device_id=(x, y, 1 - c), device_id_type=MESH)

        @pl.when(k == 0)
        def _():
            for kk in range(nk):
                copy(kk).start()

        copy(k).wait_recv()
        o_ref[0] = (own_ref[0].astype(F32) + landing[k].astype(F32)).astype(BF16)

        @pl.when(k == nk - 1)
        def _():
            for kk in range(nk):
                copy(kk).wait_send()

    blk = pl.BlockSpec((1, r, cdim), lambda k, core_ref: (k, 0, 0))
    return pl.pallas_call(
        body, name=name, out_shape=jax.ShapeDtypeStruct((nk, r, cdim), BF16),
        grid_spec=pltpu.PrefetchScalarGridSpec(
            num_scalar_prefetch=1, grid=(nk,),
            in_specs=[pl.BlockSpec((1, r, cdim), lambda k, core_ref: (2 * k + core_ref[0], 0, 0)), ANY],
            out_specs=blk,
            scratch_shapes=[pltpu.VMEM((nk, r, cdim), BF16), pltpu.SemaphoreType.DMA((nk,)),
                            pltpu.SemaphoreType.DMA((nk,))]),
        compiler_params=_params(1),
    )(core, t, t)


def _chips_start(b_ref, out_ref, stage, send_sems, recv_sems, local_sem):
    x, y, c = lax.axis_index("x"), lax.axis_index("y"), lax.axis_index("c")
    mychip = 2 * x + y
    for j, (px, py) in enumerate([(1 - x, y), (x, 1 - y), (1 - x, 1 - y)]):
        pltpu.make_async_remote_copy(
            src_ref=b_ref.at[2 * px + py], dst_ref=out_ref.at[mychip],
            send_sem=send_sems.at[j], recv_sem=recv_sems.at[j], device_id=(px, py, c), device_id_type=MESH).start()
    _copy_through_vmem(b_ref.at[mychip], out_ref.at[mychip], stage, local_sem)


def _chips_finish(b_ref, out_ref, send_sems, recv_sems):
    x, y, c = lax.axis_index("x"), lax.axis_index("y"), lax.axis_index("c")
    for j, (px, py) in enumerate([(1 - x, y), (x, 1 - y), (1 - x, 1 - y)]):
        pltpu.make_async_remote_copy(
            src_ref=b_ref.at[2 * px + py], dst_ref=out_ref.at[2 * px + py],
            send_sem=send_sems.at[j], recv_sem=recv_sems.at[j], device_id=(px, py, c), device_id_type=MESH).wait()


def _pad_rows(a, rows):
    return jnp.pad(a, ((0, rows - a.shape[0]), (0, 0)))


def _pack_in(w_in):
    return _pad_rows(w_in[0].T, ROWS_IN)


def _unpack_in(r):
    return r[0:SHARD_IN].T[None]


def _pack_rest(w_out, w_gate, w_up, w_down, w_ple, w_pg):
    head = _pad_rows(jnp.concatenate([w_out[0], w_pg[0], w_ple[0].T.reshape(32, D_MODEL)], axis=0), OFF_GATE)
    return jnp.concatenate([head, w_gate[0].T, w_up[0].T, w_down[0]], axis=0)


def _unpack_rest(r):
    return (r[0:OFF_PG][None], r[OFF_GATE:OFF_UP].T[None], r[OFF_UP:OFF_DOWN].T[None], r[OFF_DOWN:ROWS_REST][None],
            r[OFF_PLE:OFF_PLE + 32].reshape(128, D_PLE).T[None], r[OFF_PG:OFF_PLE][None])


def _pack_small(w_pool, g_mix_pre, g_mix_post, g_ffn_pre, g_ffn_post, g_ple, g_attn, g_pool, pool_scale, b_forget,
                loss=None):
    def row(vrow):
        return jnp.pad(vrow.reshape(1, -1), ((0, 0), (0, D_MODEL - vrow.size)))
    rows = [w_pool.reshape(64, D_MODEL), row(g_mix_pre), row(g_mix_post), row(g_ffn_pre), row(g_ffn_post), row(g_ple),
            row(g_attn), row(g_pool), row(pool_scale), row(b_forget),
            row(loss) if loss is not None else jnp.zeros((1, D_MODEL), F32)]
    return _pad_rows(jnp.concatenate(rows, axis=0), SMALL_ROWS)


def _unpack_small(r):
    return dict(
        w_pool=r[0:64].reshape(1, 4, POOL_CH, POOL_CH), g_mix_pre=r[ROW_G_MIX_PRE:ROW_G_MIX_PRE + 1],
        g_mix_post=r[ROW_G_MIX_POST:ROW_G_MIX_POST + 1], g_ffn_pre=r[ROW_G_FFN_PRE:ROW_G_FFN_PRE + 1],
        g_ffn_post=r[ROW_G_FFN_POST:ROW_G_FFN_POST + 1], g_ple=r[ROW_G_PLE:ROW_G_PLE + 1],
        g_attn_grp=r[ROW_G_ATTN:ROW_G_ATTN + 1, 0:D_ATTN], g_pool_grp=r[ROW_G_POOL:ROW_G_POOL + 1, 0:D_POOL],
        pool_scale=r[ROW_POOL_SCALE:ROW_POOL_SCALE + 1, 0:D_POOL], b_forget=r[ROW_B_FORGET:ROW_B_FORGET + 1, 0:HEADS])


def _step(x, p, tgt, small, in_w, in_m, in_v, rest_w, rest_m, rest_v):
    core = lax.axis_index("c").astype(jnp.int32).reshape(1)
    win_t = _all_gather(in_w.astype(BF16), "gather_w_in")[:, 0:SHARD_IN].reshape(D_IN, D_MODEL)
    wqkv = win_t[0:3 * D_ATTN]
    wf = _pad_rows(win_t[3 * D_ATTN:3 * D_ATTN + HEADS], LANES)
    wu = win_t[3 * D_ATTN + HEADS:]
    wpool = small["w_pool"].astype(BF16)
    bpad = jnp.pad(small["b_forget"], ((0, 0), (0, LANES - HEADS)))

    lay = _attn_layout_constants()
    rest_b = rest_w.astype(BF16)
    hn, q, ka, v, qat3, vt3, kt3, fl, y, mpre, gh = _pre_attn_fwd(x, small["g_mix_pre"], wqkv, wf, wu, bpad, wpool, lay,
                                                                 rest_b[0:OFF_GATE])
    a, lset3, gf = _attn_fwd(ka, qat3, vt3, rest_b[OFF_GATE:])
    wple_t = gh[:, OFF_PLE:OFF_PLE + 32].reshape(D_MODEL, D_PLE)
    mix, o, h1, hn2 = _post_attn_fwd(a, mpre, x, small["g_attn_grp"], small["g_pool_grp"], small["pool_scale"], gh,
                                     small["g_mix_post"], small["g_ffn_pre"])
    gate, up, act, ff, h2 = _ffn_fwd(hn2, gf, gf, gf, h1, small["g_ffn_post"])
    dh2, dff, dgl, dpp, h2b, pb, loss8, dg_ple, dg_ffn_post = _tail_fwd_bwd(
        h2, p, tgt, ff, wple_t, gh, small["g_ple"], small["g_ffn_post"])
    dgate, dup, dh1, dg_ffn_pre = _ffn_bwd(dff, gate, up, gf, gf, gf, h1, dh2, small["g_ffn_pre"])
    dob, dab, dat3, dlt3, dmpb, dy, dg_mix_post, dg_attn, dg_pool, dps = _post_attn_bwd(
        dh1, o, a, mpre, gh, wpool, small["g_mix_post"], small["g_attn_grp"], small["g_pool_grp"], small["pool_scale"])

    nd = N_DEV
    send_rest = jnp.concatenate([
        _wgrad(mix, dob, BF16, "wgrad_out").reshape(nd, 128, D_MODEL),
        _wgrad(h2b, dgl, BF16, "wgrad_ple_gate").reshape(nd, 128, D_MODEL),
        _wgrad(dpp, pb, BF16, "wgrad_ple").reshape(nd, 32, D_MODEL),
        jnp.zeros((nd, OFF_GATE - OFF_PLE - 32, D_MODEL), BF16),
        _wgrad(dgate, hn2, BF16, "wgrad_gate").reshape(nd, SHARD_FF, D_MODEL),
        _wgrad(dup, hn2, BF16, "wgrad_up").reshape(nd, SHARD_FF, D_MODEL),
        _wgrad(act, dff, BF16, "wgrad_down").reshape(nd, SHARD_FF, D_MODEL)], axis=1)
    pair_rest = _rs_pair_sum(core, send_rest, "rs_pair_sum_rest")

    dqt3, dk, dv, dcs, drs4, chips_rest = _attn_bwd(ka, v, kt3, qat3, q, dab, dat3, lset3, dlt3, pair_rest)
    drs = jnp.pad(drs4[:, :, 0, :].transpose(0, 2, 1).reshape(-1, HEADS), ((0, 0), (0, LANES - HEADS)))
    gx, dz, dg_mix_pre, db = _pre_attn_bwd(dqt3, dk, dv, dcs, drs, fl, dy, x, dh1, small["g_mix_pre"], wqkv, wf, wu)

    dwz = _wgrad(dz, hn, F32, "wgrad_in")
    dwin_t = jnp.concatenate([dwz[0:3 * D_ATTN], dwz[3 * D_ATTN:3 * D_ATTN + HEADS], dwz[3 * D_ATTN + LANES:]], axis=0)
    send_in = jnp.pad(dwin_t.reshape(nd, SHARD_IN, D_MODEL), ((0, 0), (0, ROWS_IN - SHARD_IN), (0, 0))).astype(BF16)
    pair_in = _rs_pair_sum(core, send_in, "rs_pair_sum_in")

    dwp = _wgrad(y, dmpb, F32, "wgrad_pool")
    dw_pool = jnp.stack([dwp[g * POOL_CH:(g + 1) * POOL_CH, g * POOL_CH:(g + 1) * POOL_CH] for g in range(4)])
    small_part = _pack_small(dw_pool, dg_mix_pre, dg_mix_post, dg_ffn_pre, dg_ffn_post, dg_ple, dg_attn, dg_pool, dps,
                             db[:, 0:HEADS], loss8[0:1, 0:1])

    *upd_rest, chips_in, small_all = _reduce_update_rest(chips_rest, rest_w, rest_m, rest_v, pair_in, small_part)
    upd_in = _reduce_update_big(chips_in, in_w, in_m, in_v, ROWS_IN, "reduce_update_in")
    return gx, small_all, upd_in, upd_rest


def kernel(x, p, g_mix_pre, w_in, b_forget, g_attn_grp, g_pool_grp, w_pool, pool_scale, w_out, g_mix_post, g_ffn_pre, w_ffn_gate, w_ffn_up, w_ffn_down, g_ffn_post, w_ple_proj, g_ple, w_ple_gate, loss_target, m_g_mix_pre, m_w_in, m_b_forget, m_g_attn_grp, m_g_pool_grp, m_w_pool, m_pool_scale, m_w_out, m_g_mix_post, m_g_ffn_pre, m_w_ffn_gate, m_w_ffn_up, m_w_ffn_down, m_g_ffn_post, m_w_ple_proj, m_g_ple, m_w_ple_gate, v_g_mix_pre, v_w_in, v_b_forget, v_g_attn_grp, v_g_pool_grp, v_w_pool, v_pool_scale, v_w_out, v_g_mix_post, v_g_ffn_pre, v_w_ffn_gate, v_w_ffn_up, v_w_ffn_down, v_g_ffn_post, v_w_ple_proj, v_g_ple, v_w_ple_gate):
    small = dict(w_pool=w_pool[0], g_mix_pre=g_mix_pre, g_mix_post=g_mix_post, g_ffn_pre=g_ffn_pre,
                 g_ffn_post=g_ffn_post, g_ple=g_ple, g_attn_grp=g_attn_grp, g_pool_grp=g_pool_grp,
                 pool_scale=pool_scale, b_forget=b_forget)
    gx, small_all, upd_in, upd_rest = _step(
        x[0], p[0, 0], loss_target[0], small, _pack_in(w_in), _pack_in(m_w_in), _pack_in(v_w_in),
        _pack_rest(w_out, w_ffn_gate, w_ffn_up, w_ffn_down, w_ple_proj, w_ple_gate),
        _pack_rest(m_w_out, m_w_ffn_gate, m_w_ffn_up, m_w_ffn_down, m_w_ple_proj, m_w_ple_gate),
        _pack_rest(v_w_out, v_w_ffn_gate, v_w_ffn_up, v_w_ffn_down, v_w_ple_proj, v_w_ple_gate))

    sm_w = _pack_small(w_pool, g_mix_pre, g_mix_post, g_ffn_pre, g_ffn_post, g_ple, g_attn_grp, g_pool_grp, pool_scale, b_forget)
    sm_m = _pack_small(m_w_pool, m_g_mix_pre, m_g_mix_post, m_g_ffn_pre, m_g_ffn_post, m_g_ple, m_g_attn_grp, m_g_pool_grp, m_pool_scale, m_b_forget)
    sm_v = _pack_small(v_w_pool, v_g_mix_pre, v_g_mix_post, v_g_ffn_pre, v_g_ffn_post, v_g_ple, v_g_attn_grp, v_g_pool_grp, v_pool_scale, v_b_forget)
    upd_small = _reduce_update_small(small_all, sm_w, sm_m, sm_v)
    loss = upd_small[0][ROW_LOSS, 0]

    def leaves(k):
        b_out, b_gate, b_up, b_down, b_ple, b_pg = _unpack_rest(upd_rest[k])
        s = _unpack_small(upd_small[k])
        return (s["g_mix_pre"], _unpack_in(upd_in[k]), s["b_forget"], s["g_attn_grp"], s["g_pool_grp"], s["w_pool"],
                s["pool_scale"], b_out, s["g_mix_post"], s["g_ffn_pre"], b_gate, b_up, b_down, s["g_ffn_post"], b_ple,
                s["g_ple"], b_pg)

    return (loss, gx[None], *leaves(0), *leaves(1), *leaves(2), *leaves(3))
```

```python
import functools

import jax
import jax.numpy as jnp
from jax import lax
from jax.experimental import pallas as pl
from jax.experimental.pallas import tpu as pltpu

F32 = jnp.float32
BF16 = jnp.bfloat16
HIGHEST = lax.Precision.HIGHEST

D_MODEL = 1024
HEADS = 8
HEAD_DIM = 64
D_ATTN = HEADS * HEAD_DIM
POOL_WINDOWS = (2, 4, 8, 16)
POOL_CH = 128
D_POOL = POOL_CH * len(POOL_WINDOWS)
D_FF = 2816
D_PLE = 256
D_IN = 3 * D_ATTN + HEADS + D_POOL
RMS_EPS = 1e-6
N_DEV = 8

ADAM_LR = 0.001
ADAM_B1 = 0.9
ADAM_B2 = 0.999
ADAM_EPS = 1e-08
ADAM_WD = 0.01
ADAM_STEP = 10

LANES = 128
HALO = 16
TS = 512
TS_FF = 512
TS_WGRAD = 1024
TM_WGRAD = 2176
TQ = 256
TN_FF = 1408
NEG = -1e30
VMEM_LIMIT = 56 * 1024 * 1024

SHARD_IN = 257
ROWS_IN = 272
SHARD_FF = 352
OFF_PG = 128
OFF_PLE = 256
OFF_GATE = SHARD_FF
OFF_UP = 2 * SHARD_FF
OFF_DOWN = 3 * SHARD_FF
ROWS_REST = 4 * SHARD_FF
TR_REST = SHARD_FF

SMALL_ROWS = 80
ROW_G_MIX_PRE, ROW_G_MIX_POST, ROW_G_FFN_PRE, ROW_G_FFN_POST, ROW_G_PLE = 64, 65, 66, 67, 68
ROW_G_ATTN, ROW_G_POOL, ROW_POOL_SCALE, ROW_B_FORGET, ROW_LOSS = 69, 70, 71, 72, 73


def _nn(a, b):
    return jnp.dot(a, b, preferred_element_type=F32)


def _nt(a, b):
    return lax.dot_general(a, b, (((1,), (1,)), ((), ())), preferred_element_type=F32)


def _tn(a, b):
    return lax.dot_general(a, b, (((0,), (0,)), ((), ())), preferred_element_type=F32)


def _rstd(v):
    return lax.rsqrt(jnp.mean(v * v, axis=-1, keepdims=True) + RMS_EPS)


def _rms_bwd(v, g, dy):
    r = _rstd(v)
    vh = v * r
    t = dy * g
    dv = r * (t - vh * jnp.mean(t * vh, axis=-1, keepdims=True))
    return dv, jnp.sum(dy * vh, axis=0, keepdims=True)


def _split3(v):
    hi = v.astype(BF16)
    rest = v - hi.astype(F32)
    mid = rest.astype(BF16)
    return hi, mid, (rest - mid.astype(F32)).astype(BF16)


def _mask_matmul(mask, v):
    hi, mid, lo = _split3(v)
    return _nn(mask, lo) + _nn(mask, mid) + _nn(mask, hi)


def _params(n_grid):
    return pltpu.CompilerParams(dimension_semantics=("arbitrary",) * n_grid, vmem_limit_bytes=VMEM_LIMIT)


def _row(i):
    return (i, 0)


def _fixed(*_):
    return (0, 0)


def _spec_square(part):
    return pl.BlockSpec((N_DEV, 128, D_MODEL), lambda *_: (0, part, 0))


def _spec_ff(part):
    return pl.BlockSpec((TN_FF // SHARD_FF, SHARD_FF, D_MODEL), lambda i, j: (j, part, 0))


assert TS == 2 * TQ and TN_FF % SHARD_FF == 0
_HALVES = (slice(0, TQ), slice(TQ, TS))

VMEM_WHOLE = pl.BlockSpec(memory_space=pltpu.VMEM)
SMEM_WHOLE = pl.BlockSpec(memory_space=pltpu.SMEM)
ANY = pl.BlockSpec(memory_space=pl.ANY)


LOG2E = 1.4426950408889634
VROWS = HEAD_DIM + 16
AUG = 128
BIAS_LANE = HEAD_DIM
ONE_LANE = HEAD_DIM + 3
SPARE_LANE = HEADS


def _attn_layout_constants():
    import numpy as np
    place = np.zeros((D_ATTN, HEADS * AUG), np.float32)
    for r in range(D_ATTN):
        place[r, (r // HEAD_DIM) * AUG + r % HEAD_DIM] = 1.0
    bias_k = np.zeros((3, LANES, HEADS * AUG), np.float32)
    bias_q = np.zeros((3, LANES, HEADS * AUG), np.float32)
    for h in range(HEADS):
        for part in range(3):
            bias_k[part, h, h * AUG + BIAS_LANE + part] = -1.0
            bias_q[part, h, h * AUG + ONE_LANE + part] = 1.0
            bias_k[0, SPARE_LANE, h * AUG + ONE_LANE + part] = 1.0
            bias_q[0, SPARE_LANE, h * AUG + BIAS_LANE + part] = 1.0
    as_bf = lambda a: jnp.asarray(a, BF16)
    return dict(place=as_bf(place), place_t=as_bf(place.T), bias_k=as_bf(bias_k),
                bias_q_t=as_bf(bias_q.transpose(0, 2, 1)))


def _pre_attn_fwd(x, g1, wqkv, wf, wu, bpad, wpool, lay, own_block):
    s, d = x.shape
    nt = s // TS
    sub = TS // TQ

    def body(x_ref, g_ref, wqkv_ref, wf_ref, wu_ref, b_ref, wp_ref, place_ref, place_t_ref, bk_ref, bqt_ref, own_ref,
             hn_ref, qt_ref, ka_ref, v_ref, qat_ref, vt_ref, kt_ref, fl_ref, y_ref, mp_ref, all_ref,
             ubuf, ccar, cbuf, stage, send_sems, recv_sems, local_sem):
        i = pl.program_id(0)

        @pl.when(i == 0)
        def _():
            _gather_start(own_ref, all_ref, stage, send_sems, recv_sems, local_sem)
            ubuf[0:HALO, :] = jnp.zeros((HALO, D_POOL), F32)
            ccar[...] = jnp.zeros_like(ccar)

        @pl.when(i == max(nt - 2, 0))
        def _():
            _gather_pass_on(all_ref, send_sems, recv_sems)

        xv = x_ref[...]
        hn = (xv * _rstd(xv) * g_ref[...]).astype(BF16)
        hn_ref[...] = hn
        zq = _nt(hn, wqkv_ref[...])
        qt = (zq[:, 0:D_ATTN] * 0.125).astype(BF16).T
        qb = (zq[:, 0:D_ATTN] * (0.125 * LOG2E)).astype(BF16)
        kb = zq[:, D_ATTN:2 * D_ATTN].astype(BF16)
        vb = zq[:, 2 * D_ATTN:3 * D_ATTN].astype(BF16)
        v_ref[...] = vb

        fl = _nt(hn, wf_ref[...]) + b_ref[...]
        fl_ref[...] = fl
        logf = jax.nn.log_sigmoid(fl)
        rr = lax.broadcasted_iota(jnp.int32, (TS, TS), 0)
        cc = lax.broadcasted_iota(jnp.int32, (TS, TS), 1)
        c = _mask_matmul((cc <= rr).astype(BF16), logf) + ccar[...]
        cbuf[...] = c
        ccar[...] = cbuf[TS - 1:TS, :]
        hi, mid, lo = _split3(c * LOG2E)
        lane = lax.broadcasted_iota(jnp.int32, (TS, LANES), 1)
        parts = (jnp.where(lane == SPARE_LANE, 1.0, hi).astype(BF16), mid, lo)
        ka = _nn(kb, place_ref[...])
        qat = _nt(place_t_ref[...], qb)
        for part in range(3):
            ka = ka + _nn(parts[part], bk_ref[part])
            qat = qat + _nt(bqt_ref[part], parts[part])
        ka_ref[...] = ka.astype(BF16)
        qat = qat.astype(BF16)
        vt = vb.T
        kt = kb.T
        for a in range(sub):
            cols = slice(a * TQ, (a + 1) * TQ)
            qat_ref[a] = qat[:, cols]
            qt_ref[a] = qt[:, cols]
            kt_ref[a] = kt[:, cols]
            for h in range(HEADS):
                vt_ref[a, h * VROWS:h * VROWS + HEAD_DIM, :] = vt[h * HEAD_DIM:(h + 1) * HEAD_DIM, cols]
                vt_ref[a, h * VROWS + HEAD_DIM:(h + 1) * VROWS, :] = jnp.ones((VROWS - HEAD_DIM, TQ), BF16)

        u = _nt(hn, wu_ref[...])
        ubuf[HALO:HALO + TS, :] = u
        t = i * TS + lax.broadcasted_iota(jnp.int32, (TS, 1), 0)
        for g, w in enumerate(POOL_WINDOWS):
            cols = slice(g * POOL_CH, (g + 1) * POOL_CH)
            sm = ubuf[:, cols]
            step = 1
            while step < w:
                sm = sm + pltpu.roll(sm, step, 0)
                step *= 2
            cnt = jnp.minimum(t + 1, w).astype(F32)
            yg = (sm[HALO:, :] / cnt - u[:, cols]).astype(BF16)
            y_ref[:, cols] = yg
            mp_ref[:, cols] = _nn(yg, wp_ref[g])
        ubuf[0:HALO, :] = u[TS - HALO:, :]

        @pl.when(i == nt - 1)
        def _():
            _gather_finish(own_ref, all_ref, send_sems, recv_sems)

    nq = s // TQ
    aug = HEADS * AUG
    outs = (
        jax.ShapeDtypeStruct((s, d), BF16), jax.ShapeDtypeStruct((nq, D_ATTN, TQ), BF16),
        jax.ShapeDtypeStruct((s, aug), BF16), jax.ShapeDtypeStruct((s, D_ATTN), BF16),
        jax.ShapeDtypeStruct((nq, aug, TQ), BF16), jax.ShapeDtypeStruct((nq, HEADS * VROWS, TQ), BF16),
        jax.ShapeDtypeStruct((nq, D_ATTN, TQ), BF16),
        jax.ShapeDtypeStruct((s, LANES), F32),
        jax.ShapeDtypeStruct((s, D_POOL), BF16), jax.ShapeDtypeStruct((s, D_POOL), F32),
        jax.ShapeDtypeStruct((N_DEV,) + own_block.shape, own_block.dtype),
    )
    fixed3 = lambda i: (0, 0, 0)
    tiles3 = lambda rows: pl.BlockSpec((sub, rows, TQ), lambda i: (i, 0, 0))
    return pl.pallas_call(
        body, grid=(nt,), out_shape=outs, name="pre_attn_fwd",
        in_specs=[pl.BlockSpec((TS, d), _row), pl.BlockSpec((1, d), _fixed),
                  pl.BlockSpec(wqkv.shape, _fixed), pl.BlockSpec(wf.shape, _fixed), pl.BlockSpec(wu.shape, _fixed),
                  pl.BlockSpec((1, LANES), _fixed), pl.BlockSpec(wpool.shape, fixed3),
                  pl.BlockSpec(lay["place"].shape, _fixed), pl.BlockSpec(lay["place_t"].shape, _fixed),
                  pl.BlockSpec(lay["bias_k"].shape, fixed3), pl.BlockSpec(lay["bias_q_t"].shape, fixed3), ANY],
        out_specs=(pl.BlockSpec((TS, d), _row), tiles3(D_ATTN),
                   pl.BlockSpec((TS, aug), _row), pl.BlockSpec((TS, D_ATTN), _row),
                   tiles3(aug), tiles3(HEADS * VROWS), tiles3(D_ATTN),
                   pl.BlockSpec((TS, LANES), _row),
                   pl.BlockSpec((TS, D_POOL), _row), pl.BlockSpec((TS, D_POOL), _row), ANY),
        scratch_shapes=[pltpu.VMEM((TS + HALO, D_POOL), F32), pltpu.VMEM((1, LANES), F32), pltpu.VMEM((TS, LANES), F32),
                        pltpu.VMEM(own_block.shape, own_block.dtype),
                        pltpu.SemaphoreType.DMA((7,)), pltpu.SemaphoreType.DMA((7,)), pltpu.SemaphoreType.DMA],
        compiler_params=_params(1),
    )(x, g1, wqkv, wf, wu, bpad, wpool, lay["place"], lay["place_t"], lay["bias_k"], lay["bias_q_t"], own_block)


def _causal_in_tile():
    krow = lax.broadcasted_iota(jnp.int32, (TQ, TQ), 0)
    qcol = lax.broadcasted_iota(jnp.int32, (TQ, TQ), 1)
    return krow <= qcol


def _attn_fwd(ka, qat3, vt3, own_block):
    s = ka.shape[0]
    nq = s // TQ
    pass_on_step = max(nq - 2, 0)

    def body(qa_ref, ka_ref, vt_ref, own_ref, a_ref, lset_ref, all_ref, acc, out_t, st_scr, pt_scr,
             stage, send_sems, recv_sems, local_sem):
        i = pl.program_id(0)

        @pl.when(i == 0)
        def _():
            _gather_start(own_ref, all_ref, stage, send_sems, recv_sems, local_sem)

        @pl.when(i == pass_on_step)
        def _():
            _gather_pass_on(all_ref, send_sems, recv_sems)

        acc[...] = jnp.zeros_like(acc)

        def tile(j, stats, masked):
            tile_max = []
            for h in range(HEADS):
                aug = slice(h * AUG, (h + 1) * AUG)
                st = _nn(ka_ref[pl.ds(j * TQ, TQ), aug], qa_ref[0, aug, :])
                if masked:
                    st = jnp.where(_causal_in_tile(), st, NEG)
                st_scr[h] = st
                tile_max.append(jnp.max(st, axis=0, keepdims=True))
            new, scale = [], []
            for h in range(HEADS):
                m_new = jnp.maximum(stats[h], tile_max[h])
                scale.append(jnp.exp2(stats[h] - m_new))
                pt_scr[h] = jnp.exp2(st_scr[h] - m_new).astype(BF16)
                new.append(m_new)
            for h in range(HEADS):
                rows = slice(h * VROWS, (h + 1) * VROWS)
                acc[rows, :] = scale[h] * acc[rows, :] + _nn(vt_ref[j, rows, :], pt_scr[h])
            return tuple(new)

        init = tuple(jnp.full((1, TQ), NEG, F32) for _ in range(HEADS))
        stats = lax.fori_loop(0, i, functools.partial(tile, masked=False), init)
        stats = tile(i, stats, True)
        for h in range(HEADS):
            denom = acc[h * VROWS + HEAD_DIM:h * VROWS + HEAD_DIM + 1, :]
            out_t[h * HEAD_DIM:(h + 1) * HEAD_DIM, :] = acc[h * VROWS:h * VROWS + HEAD_DIM, :] / denom
            lset_ref[0, h:h + 1, :] = stats[h] + jnp.log2(denom)
        a_ref[...] = out_t[...].T

        @pl.when(i == nq - 1)
        def _():
            _gather_finish(own_ref, all_ref, send_sems, recv_sems)

    r, cdim = own_block.shape
    return pl.pallas_call(
        body, grid=(nq,), name="attn_fwd",
        out_shape=(jax.ShapeDtypeStruct((s, D_ATTN), F32), jax.ShapeDtypeStruct((nq, HEADS, TQ), F32),
                   jax.ShapeDtypeStruct((N_DEV, r, cdim), own_block.dtype)),
        in_specs=[pl.BlockSpec((1, HEADS * AUG, TQ), lambda i: (i, 0, 0)), VMEM_WHOLE, VMEM_WHOLE, ANY],
        out_specs=(pl.BlockSpec((TQ, D_ATTN), _row), pl.BlockSpec((1, HEADS, TQ), lambda i: (i, 0, 0)), ANY),
        scratch_shapes=[pltpu.VMEM((HEADS * VROWS, TQ), F32), pltpu.VMEM((D_ATTN, TQ), F32),
                        pltpu.VMEM((HEADS, TQ, TQ), F32), pltpu.VMEM((HEADS, TQ, TQ), BF16),
                        pltpu.VMEM((r, cdim), own_block.dtype),
                        pltpu.SemaphoreType.DMA((7,)), pltpu.SemaphoreType.DMA((7,)), pltpu.SemaphoreType.DMA],
        compiler_params=_params(1),
    )(qat3, ka, vt3, own_block)


def _post_attn_fwd(a, mpre, x, g_attn, g_pool, pscale, wout, g_post, g_ffn_pre):
    s, d = x.shape

    def body(a_ref, mp_ref, x_ref, ga_ref, gp_ref, ps_ref, wo_ref, gpost_ref, gpre_ref,
             mix_ref, o_ref, h1_ref, hn2_ref):
        for rows in _HALVES:
            av = a_ref[rows, :]
            mix_ref[rows, 0:D_ATTN] = (av * _rstd(av) * ga_ref[...]).astype(BF16)
            mv = mp_ref[rows, :] * ps_ref[...]
            mix_ref[rows, D_ATTN:] = (mv * _rstd(mv) * gp_ref[...]).astype(BF16)
            o = _nn(mix_ref[rows, :], wo_ref[...].reshape(d, d))
            o_ref[rows, :] = o
            h1 = x_ref[rows, :] + o * _rstd(o) * gpost_ref[...]
            h1_ref[rows, :] = h1
            hn2_ref[rows, :] = (h1 * _rstd(h1) * gpre_ref[...]).astype(BF16)

    vec = lambda n: pl.BlockSpec((1, n), _fixed)
    return pl.pallas_call(
        body, grid=(s // TS,), name="post_attn_fwd",
        out_shape=(jax.ShapeDtypeStruct((s, d), BF16), jax.ShapeDtypeStruct((s, d), F32),
                   jax.ShapeDtypeStruct((s, d), F32), jax.ShapeDtypeStruct((s, d), BF16)),
        in_specs=[pl.BlockSpec((TS, D_ATTN), _row), pl.BlockSpec((TS, D_POOL), _row), pl.BlockSpec((TS, d), _row),
                  vec(D_ATTN), vec(D_POOL), vec(D_POOL), _spec_square(0), vec(d), vec(d)],
        out_specs=(pl.BlockSpec((TS, d), _row),) * 4,
        compiler_params=_params(1),
    )(a, mpre, x, g_attn, g_pool, pscale, wout, g_post, g_ffn_pre)


def _ffn_fwd(hn2, wg, wu, wd, h1, g_post):
    s, d = h1.shape
    nc = D_FF // TN_FF
    ts = min(TS_FF, s)

    def body(hn_ref, wg_ref, wu_ref, wd_ref, h1_ref, g_ref, gate_ref, up_ref, act_ref, ff_ref, h2_ref, acc):
        j = pl.program_id(1)

        @pl.when(j == 0)
        def _():
            acc[...] = jnp.zeros_like(acc)

        for r in range(2):
            rows = slice(r * (ts // 2), (r + 1) * (ts // 2))
            hn = hn_ref[rows, :]
            gt = _nt(hn, wg_ref[...].reshape(TN_FF, d))
            up = _nt(hn, wu_ref[...].reshape(TN_FF, d))
            gate_ref[rows, :] = gt.astype(BF16)
            up_ref[rows, :] = up.astype(BF16)
            act_ref[rows, :] = (gt * jax.nn.sigmoid(gt) * up).astype(BF16)
            acc[rows, :] += _nn(act_ref[rows, :], wd_ref[...].reshape(TN_FF, d))

        @pl.when(j == nc - 1)
        def _():
            ff = acc[...]
            ff_ref[...] = ff
            h2_ref[...] = h1_ref[...] + ff * _rstd(ff) * g_ref[...]

    rowblk = pl.BlockSpec((ts, d), lambda i, j: (i, 0))
    chunk = pl.BlockSpec((ts, TN_FF), lambda i, j: (i, j))
    return pl.pallas_call(
        body, grid=(s // ts, nc), name="ffn_fwd",
        out_shape=(jax.ShapeDtypeStruct((s, D_FF), BF16),) * 3 + (jax.ShapeDtypeStruct((s, d), F32),) * 2,
        in_specs=[rowblk, _spec_ff(0), _spec_ff(1), _spec_ff(2), rowblk, pl.BlockSpec((1, d), lambda i, j: (0, 0))],
        out_specs=(chunk, chunk, chunk, rowblk, rowblk),
        scratch_shapes=[pltpu.VMEM((ts, d), F32)],
        compiler_params=_params(2),
    )(hn2, wg, wu, wd, h1, g_post)


def _tail_fwd_bwd(h2, p, tgt, ff, wple, wpg, g_ple, g_ffn_post):
    s, d = h2.shape

    def body(h2_ref, p_ref, t_ref, ff_ref, wple_ref, wpg_ref, gple_ref, gfp_ref,
             dh2_ref, dff_ref, dgl_ref, dpp_ref, h2b_ref, pb_ref, loss_ref, dgple_ref, dgfp_ref):
        i = pl.program_id(0)

        @pl.when(i == 0)
        def _():
            loss_ref[...] = jnp.zeros_like(loss_ref)
            dgple_ref[...] = jnp.zeros_like(dgple_ref)
            dgfp_ref[...] = jnp.zeros_like(dgfp_ref)

        h2 = h2_ref[...]
        h2b = h2.astype(BF16)
        h2b_ref[...] = h2b
        pb = p_ref[...].astype(BF16)
        pb_ref[...] = pb
        pp = _nt(pb, wple_ref[...])
        gple = gple_ref[...]
        e = pp * _rstd(pp) * gple
        wpg = wpg_ref[...].reshape(d, d)
        sg = jax.nn.sigmoid(_nn(h2b, wpg))
        diff = h2 + sg * e - t_ref[...]
        sq = jnp.sum(jnp.sum(diff * diff, axis=1, keepdims=True), axis=0, keepdims=True)
        loss_ref[...] += jnp.broadcast_to(sq * (0.5 / d), loss_ref.shape)
        dh3 = diff * (1.0 / d)
        dgl = (dh3 * e * sg * (1.0 - sg)).astype(BF16)
        dgl_ref[...] = dgl
        dh2 = dh3 + _nt(dgl, wpg)
        dh2_ref[...] = dh2
        dpp, dg = _rms_bwd(pp, gple, dh3 * sg)
        dpp_ref[...] = dpp.astype(BF16)
        dgple_ref[...] += dg
        dff, dg = _rms_bwd(ff_ref[...], gfp_ref[...], dh2)
        dff_ref[...] = dff.astype(BF16)
        dgfp_ref[...] += dg

    rowblk = pl.BlockSpec((TS, d), _row)
    vec = pl.BlockSpec((1, d), _fixed)
    return pl.pallas_call(
        body, grid=(s // TS,), name="tail_fwd_bwd",
        out_shape=(jax.ShapeDtypeStruct((s, d), F32), jax.ShapeDtypeStruct((s, d), BF16),
                   jax.ShapeDtypeStruct((s, d), BF16), jax.ShapeDtypeStruct((s, d), BF16),
                   jax.ShapeDtypeStruct((s, d), BF16), jax.ShapeDtypeStruct((s, D_PLE), BF16),
                   jax.ShapeDtypeStruct((8, LANES), F32), jax.ShapeDtypeStruct((1, d), F32),
                   jax.ShapeDtypeStruct((1, d), F32)),
        in_specs=[rowblk, pl.BlockSpec((TS, D_PLE), _row), rowblk, rowblk,
                  pl.BlockSpec(wple.shape, _fixed), _spec_square(1), vec, vec],
        out_specs=(rowblk, rowblk, rowblk, rowblk, rowblk, pl.BlockSpec((TS, D_PLE), _row),
                   pl.BlockSpec((8, LANES), _fixed), vec, vec),
        compiler_params=_params(1),
    )(h2, p, tgt, ff, wple, wpg, g_ple, g_ffn_post)


def _ffn_bwd(dff, gate, up, wd, wg, wu, h1, dh2, g_pre):
    s, d = h1.shape
    nc = D_FF // TN_FF
    ts = min(TS_FF, s)

    def body(dff_ref, gate_ref, up_ref, wd_ref, wg_ref, wu_ref, h1_ref, dh2_ref, g_ref,
             dgate_ref, dup_ref, dh1_ref, dg_ref, acc):
        i = pl.program_id(0)
        j = pl.program_id(1)

        @pl.when((i == 0) & (j == 0))
        def _():
            dg_ref[...] = jnp.zeros_like(dg_ref)

        @pl.when(j == 0)
        def _():
            acc[...] = jnp.zeros_like(acc)

        for r in range(2):
            rows = slice(r * (ts // 2), (r + 1) * (ts // 2))
            dact = _nt(dff_ref[rows, :], wd_ref[...].reshape(TN_FF, d))
            gt = gate_ref[rows, :].astype(F32)
            sg = jax.nn.sigmoid(gt)
            dup_ref[rows, :] = (dact * gt * sg).astype(BF16)
            dgate_ref[rows, :] = (dact * up_ref[rows, :].astype(F32) * (sg * (1.0 + gt * (1.0 - sg)))).astype(BF16)
            acc[rows, :] += (_nn(dgate_ref[rows, :], wg_ref[...].reshape(TN_FF, d))
                             + _nn(dup_ref[rows, :], wu_ref[...].reshape(TN_FF, d)))

        @pl.when(j == nc - 1)
        def _():
            dv, dg = _rms_bwd(h1_ref[...], g_ref[...], acc[...])
            dh1_ref[...] = dh2_ref[...] + dv
            dg_ref[...] += dg

    rowblk = pl.BlockSpec((ts, d), lambda i, j: (i, 0))
    chunk = pl.BlockSpec((ts, TN_FF), lambda i, j: (i, j))
    vec = pl.BlockSpec((1, d), lambda i, j: (0, 0))
    return pl.pallas_call(
        body, grid=(s // ts, nc), name="ffn_bwd",
        out_shape=(jax.ShapeDtypeStruct((s, D_FF), BF16), jax.ShapeDtypeStruct((s, D_FF), BF16),
                   jax.ShapeDtypeStruct((s, d), F32), jax.ShapeDtypeStruct((1, d), F32)),
        in_specs=[rowblk, chunk, chunk, _spec_ff(2), _spec_ff(0), _spec_ff(1), rowblk, rowblk, vec],
        out_specs=(chunk, chunk, rowblk, vec),
        scratch_shapes=[pltpu.VMEM((ts, d), F32)],
        compiler_params=_params(2),
    )(dff, gate, up, wd, wg, wu, h1, dh2, g_pre)


def _post_attn_bwd(dh1, o, a, mpre, wout, wpool, g_post, g_attn, g_pool, pscale):
    s, d = dh1.shape
    sub = TS // TQ

    def body(dh1_ref, o_ref, a_ref, mp_ref, wo_ref, wp_ref, gpost_ref, ga_ref, gp_ref, ps_ref,
             dob_ref, dab_ref, dat_ref, dlt_ref, dmpb_ref, dy_ref, dgpost_ref, dga_ref, dgp_ref, dps_ref):
        i = pl.program_id(0)

        @pl.when(i == 0)
        def _():
            dgpost_ref[...] = jnp.zeros_like(dgpost_ref)
            dga_ref[...] = jnp.zeros_like(dga_ref)
            dgp_ref[...] = jnp.zeros_like(dgp_ref)
            dps_ref[...] = jnp.zeros_like(dps_ref)

        do, dg = _rms_bwd(o_ref[...], gpost_ref[...], dh1_ref[...])
        dgpost_ref[...] += dg
        dob = do.astype(BF16)
        dob_ref[...] = dob
        dmix = _nt(dob, wo_ref[...].reshape(d, d))

        av = a_ref[...]
        da, dg = _rms_bwd(av, ga_ref[...], dmix[:, 0:D_ATTN])
        dga_ref[...] += dg
        dab = da.astype(BF16)
        dab_ref[...] = dab
        dat = dab.T
        hsel = (lax.shift_right_logical(lax.broadcasted_iota(jnp.int32, (HEADS, D_ATTN), 1), 6)
                == lax.broadcasted_iota(jnp.int32, (HEADS, D_ATTN), 0)).astype(F32)
        dlt = lax.dot_general(hsel, da * av, (((1,), (1,)), ((), ())), precision=HIGHEST, preferred_element_type=F32)
        for q in range(sub):
            dlt_ref[q] = dlt[:, q * TQ:(q + 1) * TQ]
            dat_ref[q] = dat[:, q * TQ:(q + 1) * TQ]

        ps = ps_ref[...]
        mp = mp_ref[...]
        dm, dg = _rms_bwd(mp * ps, gp_ref[...], dmix[:, D_ATTN:])
        dgp_ref[...] += dg
        dps_ref[...] += jnp.sum(dm * mp, axis=0, keepdims=True)
        dmpb = (dm * ps).astype(BF16)
        dmpb_ref[...] = dmpb
        for g in range(len(POOL_WINDOWS)):
            cols = slice(g * POOL_CH, (g + 1) * POOL_CH)
            dy_ref[:, cols] = _nt(dmpb[:, cols], wp_ref[g])

    rowblk = pl.BlockSpec((TS, d), _row)
    half = pl.BlockSpec((TS, D_ATTN), _row)
    vec = lambda n: pl.BlockSpec((1, n), _fixed)
    return pl.pallas_call(
        body, grid=(s // TS,), name="post_attn_bwd",
        out_shape=(jax.ShapeDtypeStruct((s, d), BF16), jax.ShapeDtypeStruct((s, D_ATTN), BF16),
                   jax.ShapeDtypeStruct((s // TQ, D_ATTN, TQ), BF16),
                   jax.ShapeDtypeStruct((s // TQ, HEADS, TQ), F32), jax.ShapeDtypeStruct((s, D_POOL), BF16),
                   jax.ShapeDtypeStruct((s, D_POOL), F32), jax.ShapeDtypeStruct((1, d), F32),
                   jax.ShapeDtypeStruct((1, D_ATTN), F32), jax.ShapeDtypeStruct((1, D_POOL), F32),
                   jax.ShapeDtypeStruct((1, D_POOL), F32)),
        in_specs=[rowblk, rowblk, half, half, _spec_square(0),
                  pl.BlockSpec(wpool.shape, lambda i: (0, 0, 0)), vec(d), vec(D_ATTN), vec(D_POOL), vec(D_POOL)],
        out_specs=(rowblk, half, pl.BlockSpec((sub, D_ATTN, TQ), lambda i: (i, 0, 0)),
                   pl.BlockSpec((sub, HEADS, TQ), lambda i: (i, 0, 0)), half, half,
                   vec(d), vec(D_ATTN), vec(D_POOL), vec(D_POOL)),
        compiler_params=_params(1),
    )(dh1, o, a, mpre, wout, wpool, g_post, g_attn, g_pool, pscale)


def _attn_bwd(ka, v, kt3, qat3, qt3, dot3, lset3, dlt3, chip_blocks):
    s = ka.shape[0]
    nq = s // TQ
    wide = HEADS * LANES

    def body(ka_ref, v_ref, kt_ref, qat_ref, qt_ref, dot_ref, lset_ref, dlt_ref, b_ref,
             dqt_ref, dkt_ref, dvt_ref, dcs_ref, drs_ref, got_ref, dca, pt_scr, ptb_scr, dsb_scr,
             stage, send_sems, recv_sems, local_sem):
        j = pl.program_id(0)

        @pl.when(j == 0)
        def _():
            _chips_start(b_ref, got_ref, stage, send_sems, recv_sems, local_sem)
            dqt_ref[...] = jnp.zeros_like(dqt_ref)
            drs_ref[...] = jnp.zeros_like(drs_ref)

        def tile(i, masked):
            def accumulate(ref, idx, val):
                if masked:
                    ref[idx] = val
                else:
                    ref[idx] += val

            for h in range(HEADS):
                aug = slice(h * AUG, (h + 1) * AUG)
                st = _nn(ka_ref[:, aug], qat_ref[i, aug, :]) - lset_ref[i, h:h + 1, :]
                if masked:
                    st = jnp.where(_causal_in_tile(), st, NEG)
                pt = jnp.exp2(st)
                pt_scr[h] = pt
                ptb_scr[h] = pt.astype(BF16)
            heads = [(h, slice(h * HEAD_DIM, (h + 1) * HEAD_DIM)) for h in range(HEADS)]
            for h, hs in heads:
                dst = pt_scr[h] * (_nn(v_ref[:, hs], dot_ref[i, hs, :]) - dlt_ref[i, h:h + 1, :])
                dsb_scr[h] = dst.astype(BF16)
                drs_ref[i, h, 0:1, :] += jnp.sum(dst, axis=0, keepdims=True)
                accumulate(dca, (slice(None), slice(h * LANES, (h + 1) * LANES)), dst[:, 0:LANES] + dst[:, LANES:2 * LANES])
            for h, hs in heads:
                accumulate(dvt_ref, (0, hs, slice(None)), _nt(dot_ref[i, hs, :], ptb_scr[h]))
            for h, hs in heads:
                accumulate(dkt_ref, (0, hs, slice(None)), _nt(qt_ref[i, hs, :], dsb_scr[h]))
            for h, hs in heads:
                dqt_ref[i, hs, :] += _nn(kt_ref[0, hs, :], dsb_scr[h])

        def step(i, carry):
            tile(i, False)
            return carry

        tile(j, True)
        lax.fori_loop(j + 1, nq, step, 0)
        lane = lax.broadcasted_iota(jnp.int32, (TQ, LANES), 1)
        dcs_all = jnp.zeros((TQ, LANES), F32)
        for h in range(HEADS):
            colsum = jnp.sum(dca[:, h * LANES:(h + 1) * LANES], axis=1, keepdims=True)
            dcs_all = jnp.where(lane == h, colsum, dcs_all)
        dcs_ref[...] = dcs_all

        @pl.when(j == nq - 1)
        def _():
            _chips_finish(b_ref, got_ref, send_sems, recv_sems)

    blk = pl.BlockSpec((TQ, D_ATTN), _row)
    tile_t = pl.BlockSpec((1, D_ATTN, TQ), lambda j: (j, 0, 0))
    per_tile = jax.ShapeDtypeStruct((nq, D_ATTN, TQ), F32)
    _, r, cdim = chip_blocks.shape
    return pl.pallas_call(
        body, grid=(nq,), name="attn_bwd",
        out_shape=(per_tile, per_tile, per_tile, jax.ShapeDtypeStruct((s, LANES), F32),
                   jax.ShapeDtypeStruct((nq, HEADS, 8, TQ), F32),
                   jax.ShapeDtypeStruct(chip_blocks.shape, chip_blocks.dtype)),
        in_specs=[pl.BlockSpec((TQ, HEADS * AUG), _row), blk, tile_t,
                  VMEM_WHOLE, VMEM_WHOLE, VMEM_WHOLE, VMEM_WHOLE, VMEM_WHOLE, ANY],
        out_specs=(pl.BlockSpec((nq, D_ATTN, TQ), lambda j: (0, 0, 0)), tile_t, tile_t, pl.BlockSpec((TQ, LANES), _row),
                   pl.BlockSpec((nq, HEADS, 8, TQ), lambda j: (0, 0, 0, 0)), ANY),
        scratch_shapes=[pltpu.VMEM((TQ, wide), F32),
                        pltpu.VMEM((HEADS, TQ, TQ), F32), pltpu.VMEM((HEADS, TQ, TQ), BF16),
                        pltpu.VMEM((HEADS, TQ, TQ), BF16), pltpu.VMEM((r, cdim), chip_blocks.dtype),
                        pltpu.SemaphoreType.DMA((3,)), pltpu.SemaphoreType.DMA((3,)), pltpu.SemaphoreType.DMA],
        compiler_params=_params(1),
    )(ka, v, kt3, qat3, qt3, dot3, lset3, dlt3, chip_blocks)


def _pre_attn_bwd(dqt3, dkt3, dvt3, dcs, drs, fl, dy, x, dh1, g1, wqkv, wf, wu):
    s, d = x.shape
    nt = s // TS
    n = TS + HALO
    sub = TS // TQ
    qkv, fcols = 3 * D_ATTN, 3 * D_ATTN + LANES

    def body(dqt_ref, dkt_ref, dvt_ref, dcs_ref, drs_ref, fl_ref, dy_ref, x_ref, dh1_ref, g_ref, wqkv_ref, wf_ref, wu_ref,
             gx_ref, dz_ref, dg_ref, db_ref, ybuf, ccar, dlog):
        dqkv_ref = dz_ref.at[:, 0:qkv]
        dfb_ref = dz_ref.at[:, qkv:fcols]
        dub_ref = dz_ref.at[:, fcols:]
        i = pl.program_id(0)
        ti = nt - 1 - i

        @pl.when(i == 0)
        def _():
            ybuf[TS:n, :] = jnp.zeros((HALO, D_POOL), F32)
            ccar[...] = jnp.zeros_like(ccar)
            dg_ref[...] = jnp.zeros_like(dg_ref)
            db_ref[...] = jnp.zeros_like(db_ref)

        rr = lax.broadcasted_iota(jnp.int32, (TS, TS), 0)
        cc = lax.broadcasted_iota(jnp.int32, (TS, TS), 1)
        dlog[...] = ccar[...] + _mask_matmul((cc >= rr).astype(BF16), drs_ref[...] - dcs_ref[...])
        ccar[...] = dlog[0:1, :]
        df = dlog[...] * jax.nn.sigmoid(-fl_ref[...])
        db_ref[...] += jnp.sum(df, axis=0, keepdims=True)
        dfb = df.astype(BF16)
        dfb_ref[...] = dfb

        t = ti * TS + lax.broadcasted_iota(jnp.int32, (TS, 1), 0)
        dy = dy_ref[...]
        for g, w in enumerate(POOL_WINDOWS):
            cols = slice(g * POOL_CH, (g + 1) * POOL_CH)
            ybuf[0:TS, cols] = dy[:, cols] / jnp.minimum(t + 1, w).astype(F32)
        for g, w in enumerate(POOL_WINDOWS):
            cols = slice(g * POOL_CH, (g + 1) * POOL_CH)
            sm = ybuf[:, cols]
            step = 1
            while step < w:
                sm = sm + pltpu.roll(sm, n - step, 0)
                step *= 2
            dub_ref[:, cols] = (sm[0:TS, :] - dy[:, cols]).astype(BF16)
        ybuf[TS:n, :] = ybuf[0:HALO, :]

        for a in range(sub):
            rows = slice(a * TQ, (a + 1) * TQ)
            dqkv_ref[rows, 0:D_ATTN] = (dqt_ref[a].T * 0.125).astype(BF16)
            dqkv_ref[rows, D_ATTN:2 * D_ATTN] = dkt_ref[a].T.astype(BF16)
            dqkv_ref[rows, 2 * D_ATTN:] = dvt_ref[a].T.astype(BF16)
        dhn = _nn(dqkv_ref[...], wqkv_ref[...]) + _nn(dfb, wf_ref[...]) + _nn(dub_ref[...], wu_ref[...])
        dx, dg = _rms_bwd(x_ref[...], g_ref[...], dhn)
        gx_ref[...] = dh1_ref[...] + dx
        dg_ref[...] += dg

    rev = lambda i: (nt - 1 - i, 0)
    blk = lambda w: pl.BlockSpec((TS, w), rev)
    return pl.pallas_call(
        body, grid=(nt,), name="pre_attn_bwd",
        out_shape=(jax.ShapeDtypeStruct((s, d), F32), jax.ShapeDtypeStruct((s, fcols + D_POOL), BF16),
                   jax.ShapeDtypeStruct((1, d), F32), jax.ShapeDtypeStruct((1, LANES), F32)),
        in_specs=[pl.BlockSpec((sub, D_ATTN, TQ), lambda i: (nt - 1 - i, 0, 0)),
                  pl.BlockSpec((sub, D_ATTN, TQ), lambda i: (nt - 1 - i, 0, 0)),
                  pl.BlockSpec((sub, D_ATTN, TQ), lambda i: (nt - 1 - i, 0, 0)),
                  blk(LANES), blk(LANES), blk(LANES), blk(D_POOL), blk(d), blk(d),
                  pl.BlockSpec((1, d), _fixed), pl.BlockSpec(wqkv.shape, _fixed), pl.BlockSpec(wf.shape, _fixed),
                  pl.BlockSpec(wu.shape, _fixed)],
        out_specs=(blk(d), blk(fcols + D_POOL), pl.BlockSpec((1, d), _fixed), pl.BlockSpec((1, LANES), _fixed)),
        scratch_shapes=[pltpu.VMEM((n, D_POOL), F32), pltpu.VMEM((1, LANES), F32), pltpu.VMEM((TS, LANES), F32)],
        compiler_params=_params(1),
    )(dqt3, dkt3, dvt3, dcs, drs, fl, dy, x, dh1, g1, wqkv, wf, wu)


def _wgrad(a, b, out_dtype, name):
    s, m = a.shape
    n = b.shape[1]
    tm = max(t for t in range(LANES, min(m, TM_WGRAD) + 1, LANES) if m % t == 0)
    ts = min(TS_WGRAD, s)
    ns = s // ts

    def body(a_ref, b_ref, o_ref, acc):
        i = pl.program_id(1)

        @pl.when(i == 0)
        def _():
            acc[...] = jnp.zeros_like(acc)

        acc[...] += _tn(a_ref[...], b_ref[...])

        @pl.when(i == ns - 1)
        def _():
            o_ref[...] = acc[...].astype(out_dtype)

    return pl.pallas_call(
        body, grid=(m // tm, ns), name=name, out_shape=jax.ShapeDtypeStruct((m, n), out_dtype),
        in_specs=[pl.BlockSpec((ts, tm), lambda j, i: (i, j)), pl.BlockSpec((ts, n), lambda j, i: (i, 0))],
        out_specs=pl.BlockSpec((tm, n), lambda j, i: (j, 0)),
        scratch_shapes=[pltpu.VMEM((tm, n), F32)],
        compiler_params=_params(2),
    )(a, b)


def _adamw(w, g, m, v):
    m = ADAM_B1 * m + (1.0 - ADAM_B1) * g
    v = ADAM_B2 * v + (1.0 - ADAM_B2) * (g * g)
    m_hat = m / (1.0 - ADAM_B1 ** ADAM_STEP)
    v_hat = v / (1.0 - ADAM_B2 ** ADAM_STEP)
    delta = -ADAM_LR * (m_hat / (jnp.sqrt(v_hat) + ADAM_EPS) + ADAM_WD * w)
    return delta, m, v


def _sum_update(p_ref, w_ref, m_ref, v_ref, g_ref, d_ref, nm_ref, nv_ref):
    g = p_ref[0].astype(F32)
    for k in range(1, p_ref.shape[0]):
        g = g + p_ref[k].astype(F32)
    g_ref[...] = g
    d_ref[...], nm_ref[...], nv_ref[...] = _adamw(w_ref[...], g, m_ref[...], v_ref[...])


def _reduce_update_rest(parts, w, m, v, chip_blocks, small_block):
    nk, r, c = parts.shape
    ns = r // TR_REST

    def body(p_ref, w_ref, m_ref, v_ref, b_ref, sm_ref, g_ref, d_ref, nm_ref, nv_ref, got_ref, all_ref,
             stage_b, stage_s, send_b, recv_b, local_b, send_s, recv_s, local_s):
        i = pl.program_id(0)

        @pl.when(i == 0)
        def _():
            _chips_start(b_ref, got_ref, stage_b, send_b, recv_b, local_b)
            _gather_start(sm_ref, all_ref, stage_s, send_s, recv_s, local_s)

        _sum_update(p_ref, w_ref, m_ref, v_ref, g_ref, d_ref, nm_ref, nv_ref)

        @pl.when(i == ns - 1)
        def _():
            _gather_pass_on(all_ref, send_s, recv_s)
            _chips_finish(b_ref, got_ref, send_b, recv_b)
            _gather_finish(sm_ref, all_ref, send_s, recv_s)

    blk = pl.BlockSpec((TR_REST, c), _row)
    out = jax.ShapeDtypeStruct((r, c), F32)
    dma = pltpu.SemaphoreType.DMA
    return pl.pallas_call(
        body, grid=(ns,), name="reduce_update_rest",
        out_shape=(out,) * 4 + (jax.ShapeDtypeStruct(chip_blocks.shape, chip_blocks.dtype),
                                jax.ShapeDtypeStruct((N_DEV,) + small_block.shape, small_block.dtype)),
        in_specs=[pl.BlockSpec((nk, TR_REST, c), lambda i: (0, i, 0)), blk, blk, blk, ANY, ANY],
        out_specs=(blk,) * 4 + (ANY, ANY),
        scratch_shapes=[pltpu.VMEM(chip_blocks.shape[1:], chip_blocks.dtype), pltpu.VMEM(small_block.shape, small_block.dtype),
                        dma((3,)), dma((3,)), dma, dma((7,)), dma((7,)), dma],
        compiler_params=_params(1),
    )(parts, w, m, v, chip_blocks, small_block)


def _reduce_update_big(parts, w, m, v, tr, name):
    nk, r, c = parts.shape

    def body(p_ref, w_ref, m_ref, v_ref, g_ref, d_ref, nm_ref, nv_ref):
        _sum_update(p_ref, w_ref, m_ref, v_ref, g_ref, d_ref, nm_ref, nv_ref)

    blk = pl.BlockSpec((tr, c), _row)
    out = jax.ShapeDtypeStruct((r, c), F32)
    return pl.pallas_call(
        body, grid=(r // tr,), name=name, out_shape=(out,) * 4,
        in_specs=[pl.BlockSpec((nk, tr, c), lambda i: (0, i, 0)), blk, blk, blk],
        out_specs=(blk,) * 4, compiler_params=_params(1),
    )(parts, w, m, v)


def _reduce_update_small(parts, w, m, v):
    nd = parts.shape[0]

    def body(p_ref, w_ref, m_ref, v_ref, g_ref, d_ref, nm_ref, nv_ref):
        g = p_ref[0]
        for k in range(1, nd):
            g = g + p_ref[k]
        g_ref[...] = g
        d_ref[...], nm_ref[...], nv_ref[...] = _adamw(w_ref[...], g, m_ref[...], v_ref[...])

    out = jax.ShapeDtypeStruct(w.shape, F32)
    return pl.pallas_call(body, name="reduce_update_small", out_shape=(out,) * 4,
                          compiler_params=pltpu.CompilerParams(vmem_limit_bytes=VMEM_LIMIT))(parts, w, m, v)


MESH = pl.DeviceIdType.MESH


def _copy_through_vmem(src_hbm, dst_hbm, stage, sem):
    load = pltpu.make_async_copy(src_hbm, stage, sem)
    load.start()
    load.wait()
    store = pltpu.make_async_copy(stage, dst_hbm, sem)
    store.start()
    store.wait()


class _GatherPlan:
    def __init__(self, x_ref, out_ref, send_sems, recv_sems):
        x, y, c = lax.axis_index("x"), lax.axis_index("y"), lax.axis_index("c")
        self.me, self.sibling, self.c = (x, y, c), (x, y, 1 - c), c
        self.chips = [(1 - x, y), (x, 1 - y), (1 - x, 1 - y)]
        self.x_ref, self.out_ref, self.send_sems, self.recv_sems = x_ref, out_ref, send_sems, recv_sems

    def slot(self, px, py, pc):
        return self.out_ref.at[4 * px + 2 * py + pc]

    def copy(self, k, block, to, src=None):
        return pltpu.make_async_remote_copy(
            src_ref=self.slot(*block) if src is None else src, dst_ref=self.slot(*block),
            send_sem=self.send_sems.at[k], recv_sem=self.recv_sems.at[k], device_id=to, device_id_type=MESH)

    def first(self):
        return [self.copy(0, self.me, self.sibling, src=self.x_ref)] + [
            self.copy(1 + j, self.me, (*chip, self.c), src=self.x_ref) for j, chip in enumerate(self.chips)]

    def passed(self):
        return [self.copy(4 + j, (*chip, self.c), self.sibling) for j, chip in enumerate(self.chips)]


def _gather_start(x_ref, out_ref, stage, send_sems, recv_sems, local_sem):
    plan = _GatherPlan(x_ref, out_ref, send_sems, recv_sems)
    for cp in plan.first():
        cp.start()
    _copy_through_vmem(x_ref, plan.slot(*plan.me), stage, local_sem)


def _gather_pass_on(out_ref, send_sems, recv_sems):
    plan = _GatherPlan(None, out_ref, send_sems, recv_sems)
    passed = plan.passed()
    for j, chip in enumerate(plan.chips):
        plan.copy(1 + j, (*chip, plan.c), plan.me).wait_recv()
        passed[j].start()


def _gather_finish(x_ref, out_ref, send_sems, recv_sems):
    plan = _GatherPlan(x_ref, out_ref, send_sems, recv_sems)
    plan.copy(0, plan.sibling, plan.me).wait_recv()
    for j, chip in enumerate(plan.chips):
        plan.copy(4 + j, (*chip, 1 - plan.c), plan.me).wait_recv()
    for cp in plan.first() + plan.passed():
        cp.wait_send()


def _all_gather(xs, name):
    r, cdim = xs.shape

    def body(x_ref, out_ref, stage, send_sems, recv_sems, local_sem):
        _gather_start(x_ref, out_ref, stage, send_sems, recv_sems, local_sem)
        _gather_pass_on(out_ref, send_sems, recv_sems)
        _gather_finish(x_ref, out_ref, send_sems, recv_sems)

    return pl.pallas_call(
        body, name=name, out_shape=jax.ShapeDtypeStruct((N_DEV, r, cdim), xs.dtype),
        in_specs=[ANY], out_specs=ANY,
        scratch_shapes=[pltpu.VMEM((r, cdim), xs.dtype), pltpu.SemaphoreType.DMA((7,)), pltpu.SemaphoreType.DMA((7,)),
                        pltpu.SemaphoreType.DMA],
        compiler_params=pltpu.CompilerParams(vmem_limit_bytes=VMEM_LIMIT),
    )(xs)


def _rs_pair_sum(core, t, name):
    _, r, cdim = t.shape
    nk = N_DEV // 2

    def body(core_ref, own_ref, t_ref, o_ref, landing, send_sems, recv_sems):
        k = pl.program_id(0)
        x, y, c = lax.axis_index("x"), lax.axis_index("y"), lax.axis_index("c")

        def copy(kk):
            return pltpu.make_async_remote_copy(
                src_ref=t_ref.at[2 * kk + (1 - c)], dst_ref=landing.at[kk],
                send_sem=send_sems.at[kk], recv_sem=recv_sems.at[kk], device_id=(x, y, 1 - c), device_id_type=MESH)

        @pl.when(k == 0)
        def _():
            for kk in range(nk):
                copy(kk).start()

        copy(k).wait_recv()
        o_ref[0] = (own_ref[0].astype(F32) + landing[k].astype(F32)).astype(BF16)

        @pl.when(k == nk - 1)
        def _():
            for kk in range(nk):
                copy(kk).wait_send()

    blk = pl.BlockSpec((1, r, cdim), lambda k, core_ref: (k, 0, 0))
    return pl.pallas_call(
        body, name=name, out_shape=jax.ShapeDtypeStruct((nk, r, cdim), BF16),
        grid_spec=pltpu.PrefetchScalarGridSpec(
            num_scalar_prefetch=1, grid=(nk,),
            in_specs=[pl.BlockSpec((1, r, cdim), lambda k, core_ref: (2 * k + core_ref[0], 0, 0)), ANY],
            out_specs=blk,
            scratch_shapes=[pltpu.VMEM((nk, r, cdim), BF16), pltpu.SemaphoreType.DMA((nk,)),
                            pltpu.SemaphoreType.DMA((nk,))]),
        compiler_params=_params(1),
    )(core, t, t)


def _chips_start(b_ref, out_ref, stage, send_sems, recv_sems, local_sem):
    x, y, c = lax.axis_index("x"), lax.axis_index("y"), lax.axis_index("c")
    mychip = 2 * x + y
    for j, (px, py) in enumerate([(1 - x, y), (x, 1 - y), (1 - x, 1 - y)]):
        pltpu.make_async_remote_copy(
            src_ref=b_ref.at[2 * px + py], dst_ref=out_ref.at[mychip],
            send_sem=send_sems.at[j], recv_sem=recv_sems.at[j], device_id=(px, py, c), device_id_type=MESH).start()
    _copy_through_vmem(b_ref.at[mychip], out_ref.at[mychip], stage, local_sem)


def _chips_finish(b_ref, out_ref, send_sems, recv_sems):
    x, y, c = lax.axis_index("x"), lax.axis_index("y"), lax.axis_index("c")
    for j, (px, py) in enumerate([(1 - x, y), (x, 1 - y), (1 - x, 1 - y)]):
        pltpu.make_async_remote_copy(
            src_ref=b_ref.at[2 * px + py], dst_ref=out_ref.at[2 * px + py],
            send_sem=send_sems.at[j], recv_sem=recv_sems.at[j], device_id=(px, py, c), device_id_type=MESH).wait()


def _pad_rows(a, rows):
    return jnp.pad(a, ((0, rows - a.shape[0]), (0, 0)))


def _pack_in(w_in):
    return _pad_rows(w_in[0].T, ROWS_IN)


def _unpack_in(r):
    return r[0:SHARD_IN].T[None]


def _pack_rest(w_out, w_gate, w_up, w_down, w_ple, w_pg):
    head = _pad_rows(jnp.concatenate([w_out[0], w_pg[0], w_ple[0].T.reshape(32, D_MODEL)], axis=0), OFF_GATE)
    return jnp.concatenate([head, w_gate[0].T, w_up[0].T, w_down[0]], axis=0)


def _unpack_rest(r):
    return (r[0:OFF_PG][None], r[OFF_GATE:OFF_UP].T[None], r[OFF_UP:OFF_DOWN].T[None], r[OFF_DOWN:ROWS_REST][None],
            r[OFF_PLE:OFF_PLE + 32].reshape(128, D_PLE).T[None], r[OFF_PG:OFF_PLE][None])


def _pack_small(w_pool, g_mix_pre, g_mix_post, g_ffn_pre, g_ffn_post, g_ple, g_attn, g_pool, pool_scale, b_forget,
                loss=None):
    def row(vrow):
        return jnp.pad(vrow.reshape(1, -1), ((0, 0), (0, D_MODEL - vrow.size)))
    rows = [w_pool.reshape(64, D_MODEL), row(g_mix_pre), row(g_mix_post), row(g_ffn_pre), row(g_ffn_post), row(g_ple),
            row(g_attn), row(g_pool), row(pool_scale), row(b_forget),
            row(loss) if loss is not None else jnp.zeros((1, D_MODEL), F32)]
    return _pad_rows(jnp.concatenate(rows, axis=0), SMALL_ROWS)


def _unpack_small(r):
    return dict(
        w_pool=r[0:64].reshape(1, 4, POOL_CH, POOL_CH), g_mix_pre=r[ROW_G_MIX_PRE:ROW_G_MIX_PRE + 1],
        g_mix_post=r[ROW_G_MIX_POST:ROW_G_MIX_POST + 1], g_ffn_pre=r[ROW_G_FFN_PRE:ROW_G_FFN_PRE + 1],
        g_ffn_post=r[ROW_G_FFN_POST:ROW_G_FFN_POST + 1], g_ple=r[ROW_G_PLE:ROW_G_PLE + 1],
        g_attn_grp=r[ROW_G_ATTN:ROW_G_ATTN + 1, 0:D_ATTN], g_pool_grp=r[ROW_G_POOL:ROW_G_POOL + 1, 0:D_POOL],
        pool_scale=r[ROW_POOL_SCALE:ROW_POOL_SCALE + 1, 0:D_POOL], b_forget=r[ROW_B_FORGET:ROW_B_FORGET + 1, 0:HEADS])


def _step(x, p, tgt, small, in_w, in_m, in_v, rest_w, rest_m, rest_v):
    core = lax.axis_index("c").astype(jnp.int32).reshape(1)
    win_t = _all_gather(in_w.astype(BF16), "gather_w_in")[:, 0:SHARD_IN].reshape(D_IN, D_MODEL)
    wqkv = win_t[0:3 * D_ATTN]
    wf = _pad_rows(win_t[3 * D_ATTN:3 * D_ATTN + HEADS], LANES)
    wu = win_t[3 * D_ATTN + HEADS:]
    wpool = small["w_pool"].astype(BF16)
    bpad = jnp.pad(small["b_forget"], ((0, 0), (0, LANES - HEADS)))

    lay = _attn_layout_constants()
    rest_b = rest_w.astype(BF16)
    hn, qt3, ka, v, qat3, vt3, kt3, fl, y, mpre, gh = _pre_attn_fwd(x, small["g_mix_pre"], wqkv, wf, wu, bpad, wpool, lay,
                                                                 rest_b[0:OFF_GATE])
    a, lset3, gf = _attn_fwd(ka, qat3, vt3, rest_b[OFF_GATE:])
    wple_t = gh[:, OFF_PLE:OFF_PLE + 32].reshape(D_MODEL, D_PLE)
    mix, o, h1, hn2 = _post_attn_fwd(a, mpre, x, small["g_attn_grp"], small["g_pool_grp"], small["pool_scale"], gh,
                                     small["g_mix_post"], small["g_ffn_pre"])
    gate, up, act, ff, h2 = _ffn_fwd(hn2, gf, gf, gf, h1, small["g_ffn_post"])
    dh2, dff, dgl, dpp, h2b, pb, loss8, dg_ple, dg_ffn_post = _tail_fwd_bwd(
        h2, p, tgt, ff, wple_t, gh, small["g_ple"], small["g_ffn_post"])
    dgate, dup, dh1, dg_ffn_pre = _ffn_bwd(dff, gate, up, gf, gf, gf, h1, dh2, small["g_ffn_pre"])
    dob, dab, dat3, dlt3, dmpb, dy, dg_mix_post, dg_attn, dg_pool, dps = _post_attn_bwd(
        dh1, o, a, mpre, gh, wpool, small["g_mix_post"], small["g_attn_grp"], small["g_pool_grp"], small["pool_scale"])

    nd = N_DEV
    send_rest = jnp.concatenate([
        _wgrad(mix, dob, BF16, "wgrad_out").reshape(nd, 128, D_MODEL),
        _wgrad(h2b, dgl, BF16, "wgrad_ple_gate").reshape(nd, 128, D_MODEL),
        _wgrad(dpp, pb, BF16, "wgrad_ple").reshape(nd, 32, D_MODEL),
        jnp.zeros((nd, OFF_GATE - OFF_PLE - 32, D_MODEL), BF16),
        _wgrad(dgate, hn2, BF16, "wgrad_gate").reshape(nd, SHARD_FF, D_MODEL),
        _wgrad(dup, hn2, BF16, "wgrad_up").reshape(nd, SHARD_FF, D_MODEL),
        _wgrad(act, dff, BF16, "wgrad_down").reshape(nd, SHARD_FF, D_MODEL)], axis=1)
    pair_rest = _rs_pair_sum(core, send_rest, "rs_pair_sum_rest")

    dqt3, dkt3, dvt3, dcs, drs4, chips_rest = _attn_bwd(ka, v, kt3, qat3, qt3, dat3, lset3, dlt3, pair_rest)
    drs = jnp.pad(drs4[:, :, 0, :].transpose(0, 2, 1).reshape(-1, HEADS), ((0, 0), (0, LANES - HEADS)))
    gx, dz, dg_mix_pre, db = _pre_attn_bwd(dqt3, dkt3, dvt3, dcs, drs, fl, dy, x, dh1, small["g_mix_pre"], wqkv, wf, wu)

    dwz = _wgrad(dz, hn, F32, "wgrad_in")
    dwin_t = jnp.concatenate([dwz[0:3 * D_ATTN], dwz[3 * D_ATTN:3 * D_ATTN + HEADS], dwz[3 * D_ATTN + LANES:]], axis=0)
    send_in = jnp.pad(dwin_t.reshape(nd, SHARD_IN, D_MODEL), ((0, 0), (0, ROWS_IN - SHARD_IN), (0, 0))).astype(BF16)
    pair_in = _rs_pair_sum(core, send_in, "rs_pair_sum_in")

    dwp = _wgrad(y, dmpb, F32, "wgrad_pool")
    dw_pool = jnp.stack([dwp[g * POOL_CH:(g + 1) * POOL_CH, g * POOL_CH:(g + 1) * POOL_CH] for g in range(4)])
    small_part = _pack_small(dw_pool, dg_mix_pre, dg_mix_post, dg_ffn_pre, dg_ffn_post, dg_ple, dg_attn, dg_pool, dps,
                             db[:, 0:HEADS], loss8[0:1, 0:1])

    *upd_rest, chips_in, small_all = _reduce_update_rest(chips_rest, rest_w, rest_m, rest_v, pair_in, small_part)
    upd_in = _reduce_update_big(chips_in, in_w, in_m, in_v, ROWS_IN, "reduce_update_in")
    return gx, small_all, upd_in, upd_rest


def kernel(x, p, g_mix_pre, w_in, b_forget, g_attn_grp, g_pool_grp, w_pool, pool_scale, w_out, g_mix_post, g_ffn_pre, w_ffn_gate, w_ffn_up, w_ffn_down, g_ffn_post, w_ple_proj, g_ple, w_ple_gate, loss_target, m_g_mix_pre, m_w_in, m_b_forget, m_g_attn_grp, m_g_pool_grp, m_w_pool, m_pool_scale, m_w_out, m_g_mix_post, m_g_ffn_pre, m_w_ffn_gate, m_w_ffn_up, m_w_ffn_down, m_g_ffn_post, m_w_ple_proj, m_g_ple, m_w_ple_gate, v_g_mix_pre, v_w_in, v_b_forget, v_g_attn_grp, v_g_pool_grp, v_w_pool, v_pool_scale, v_w_out, v_g_mix_post, v_g_ffn_pre, v_w_ffn_gate, v_w_ffn_up, v_w_ffn_down, v_g_ffn_post, v_w_ple_proj, v_g_ple, v_w_ple_gate):
    small = dict(w_pool=w_pool[0], g_mix_pre=g_mix_pre, g_mix_post=g_mix_post, g_ffn_pre=g_ffn_pre,
                 g_ffn_post=g_ffn_post, g_ple=g_ple, g_attn_grp=g_attn_grp, g_pool_grp=g_pool_grp,
                 pool_scale=pool_scale, b_forget=b_forget)
    gx, small_all, upd_in, upd_rest = _step(
        x[0], p[0, 0], loss_target[0], small, _pack_in(w_in), _pack_in(m_w_in), _pack_in(v_w_in),
        _pack_rest(w_out, w_ffn_gate, w_ffn_up, w_ffn_down, w_ple_proj, w_ple_gate),
        _pack_rest(m_w_out, m_w_ffn_gate, m_w_ffn_up, m_w_ffn_down, m_w_ple_proj, m_w_ple_gate),
        _pack_rest(v_w_out, v_w_ffn_gate, v_w_ffn_up, v_w_ffn_down, v_w_ple_proj, v_w_ple_gate))

    sm_w = _pack_small(w_pool, g_mix_pre, g_mix_post, g_ffn_pre, g_ffn_post, g_ple, g_attn_grp, g_pool_grp, pool_scale, b_forget)
    sm_m = _pack_small(m_w_pool, m_g_mix_pre, m_g_mix_post, m_g_ffn_pre, m_g_ffn_post, m_g_ple, m_g_attn_grp, m_g_pool_grp, m_pool_scale, m_b_forget)
    sm_v = _pack_small(v_w_pool, v_g_mix_pre, v_g_mix_post, v_g_ffn_pre, v_g_ffn_post, v_g_ple, v_g_attn_grp, v_g_pool_grp, v_pool_scale, v_b_forget)
    upd_small = _reduce_update_small(small_all, sm_w, sm_m, sm_v)
    loss = upd_small[0][ROW_LOSS, 0]

    def leaves(k):
        b_out, b_gate, b_up, b_down, b_ple, b_pg = _unpack_rest(upd_rest[k])
        s = _unpack_small(upd_small[k])
        return (s["g_mix_pre"], _unpack_in(upd_in[k]), s["b_forget"], s["g_attn_grp"], s["g_pool_grp"], s["w_pool"],
                s["pool_scale"], b_out, s["g_mix_post"], s["g_ffn_pre"], b_gate, b_up, b_down, s["g_ffn_post"], b_ple,
                s["g_ple"], b_pg)

    return (loss, gx[None], *leaves(0), *leaves(1), *leaves(2), *leaves(3))
```

```python
import functools

import jax
import jax.numpy as jnp
from jax import lax
from jax.experimental import pallas as pl
from jax.experimental.pallas import tpu as pltpu

F32 = jnp.float32
BF16 = jnp.bfloat16
HIGHEST = lax.Precision.HIGHEST

D_MODEL = 1024
HEADS = 8
HEAD_DIM = 64
D_ATTN = HEADS * HEAD_DIM
POOL_WINDOWS = (2, 4, 8, 16)
POOL_CH = 128
D_POOL = POOL_CH * len(POOL_WINDOWS)
D_FF = 2816
D_PLE = 256
D_IN = 3 * D_ATTN + HEADS + D_POOL
RMS_EPS = 1e-6
N_DEV = 8

ADAM_LR = 0.001
ADAM_B1 = 0.9
ADAM_B2 = 0.999
ADAM_EPS = 1e-08
ADAM_WD = 0.01
ADAM_STEP = 10

LANES = 128
HALO = 16
TS = 512
TS_FF = 512
TS_WGRAD = 1024
TM_WGRAD = 2176
TQ = 256
TN_FF = 1408
NEG = -1e30
VMEM_LIMIT = 56 * 1024 * 1024

SHARD_IN = 257
ROWS_IN = 272
SHARD_FF = 352
OFF_PG = 128
OFF_PLE = 256
OFF_GATE = SHARD_FF
OFF_UP = 2 * SHARD_FF
OFF_DOWN = 3 * SHARD_FF
ROWS_REST = 4 * SHARD_FF
TR_REST = SHARD_FF

SMALL_ROWS = 80
ROW_G_MIX_PRE, ROW_G_MIX_POST, ROW_G_FFN_PRE, ROW_G_FFN_POST, ROW_G_PLE = 64, 65, 66, 67, 68
ROW_G_ATTN, ROW_G_POOL, ROW_POOL_SCALE, ROW_B_FORGET, ROW_LOSS = 69, 70, 71, 72, 73


def _nn(a, b):
    return jnp.dot(a, b, preferred_element_type=F32)


def _nt(a, b):
    return lax.dot_general(a, b, (((1,), (1,)), ((), ())), preferred_element_type=F32)


def _tn(a, b):
    return lax.dot_general(a, b, (((0,), (0,)), ((), ())), preferred_element_type=F32)


def _rstd(v):
    return lax.rsqrt(jnp.mean(v * v, axis=-1, keepdims=True) + RMS_EPS)


def _rms_bwd(v, g, dy):
    r = _rstd(v)
    vh = v * r
    t = dy * g
    dv = r * (t - vh * jnp.mean(t * vh, axis=-1, keepdims=True))
    return dv, jnp.sum(dy * vh, axis=0, keepdims=True)


def _split3(v):
    hi = v.astype(BF16)
    rest = v - hi.astype(F32)
    mid = rest.astype(BF16)
    return hi, mid, (rest - mid.astype(F32)).astype(BF16)


def _mask_matmul(mask, v):
    hi, mid, lo = _split3(v)
    return _nn(mask, lo) + _nn(mask, mid) + _nn(mask, hi)


def _params(n_grid):
    return pltpu.CompilerParams(dimension_semantics=("arbitrary",) * n_grid, vmem_limit_bytes=VMEM_LIMIT)


def _row(i):
    return (i, 0)


def _fixed(*_):
    return (0, 0)


def _spec_square(part):
    return pl.BlockSpec((N_DEV, 128, D_MODEL), lambda *_: (0, part, 0))


def _spec_ff(part):
    return pl.BlockSpec((TN_FF // SHARD_FF, SHARD_FF, D_MODEL), lambda i, j: (j, part, 0))


assert TS == 2 * TQ and TN_FF % SHARD_FF == 0
_HALVES = (slice(0, TQ), slice(TQ, TS))

VMEM_WHOLE = pl.BlockSpec(memory_space=pltpu.VMEM)
SMEM_WHOLE = pl.BlockSpec(memory_space=pltpu.SMEM)
ANY = pl.BlockSpec(memory_space=pl.ANY)


LOG2E = 1.4426950408889634
VROWS = HEAD_DIM + 16
AUG = 128
BIAS_LANE = HEAD_DIM
ONE_LANE = HEAD_DIM + 3
SPARE_LANE = HEADS


def _attn_layout_constants():
    import numpy as np
    place = np.zeros((D_ATTN, HEADS * AUG), np.float32)
    for r in range(D_ATTN):
        place[r, (r // HEAD_DIM) * AUG + r % HEAD_DIM] = 1.0
    bias_k = np.zeros((3, LANES, HEADS * AUG), np.float32)
    bias_q = np.zeros((3, LANES, HEADS * AUG), np.float32)
    for h in range(HEADS):
        for part in range(3):
            bias_k[part, h, h * AUG + BIAS_LANE + part] = -1.0
            bias_q[part, h, h * AUG + ONE_LANE + part] = 1.0
            bias_k[0, SPARE_LANE, h * AUG + ONE_LANE + part] = 1.0
            bias_q[0, SPARE_LANE, h * AUG + BIAS_LANE + part] = 1.0
    as_bf = lambda a: jnp.asarray(a, BF16)
    return dict(place=as_bf(place), place_t=as_bf(place.T), bias_k=as_bf(bias_k),
                bias_q_t=as_bf(bias_q.transpose(0, 2, 1)))


def _pre_attn_fwd(x, g1, wqkv, wf, wu, bpad, wpool, lay, own_block):
    s, d = x.shape
    nt = s // TS
    sub = TS // TQ

    def body(x_ref, g_ref, wqkv_ref, wf_ref, wu_ref, b_ref, wp_ref, place_ref, place_t_ref, bk_ref, bqt_ref, own_ref,
             hn_ref, qt_ref, ka_ref, v_ref, qat_ref, vt_ref, kt_ref, fl_ref, y_ref, mp_ref, all_ref,
             ubuf, ccar, cbuf, stage, send_sems, recv_sems, local_sem):
        i = pl.program_id(0)

        @pl.when(i == 0)
        def _():
            _gather_start(own_ref, all_ref, stage, send_sems, recv_sems, local_sem)
            ubuf[0:HALO, :] = jnp.zeros((HALO, D_POOL), F32)
            ccar[...] = jnp.zeros_like(ccar)

        @pl.when(i == max(nt - 2, 0))
        def _():
            _gather_pass_on(all_ref, send_sems, recv_sems)

        xv = x_ref[...]
        hn = (xv * _rstd(xv) * g_ref[...]).astype(BF16)
        hn_ref[...] = hn
        zq = _nt(hn, wqkv_ref[...])
        qt = (zq[:, 0:D_ATTN] * 0.125).astype(BF16).T
        qb = (zq[:, 0:D_ATTN] * (0.125 * LOG2E)).astype(BF16)
        kb = zq[:, D_ATTN:2 * D_ATTN].astype(BF16)
        vb = zq[:, 2 * D_ATTN:3 * D_ATTN].astype(BF16)
        v_ref[...] = vb

        fl = _nt(hn, wf_ref[...]) + b_ref[...]
        fl_ref[...] = fl
        logf = jax.nn.log_sigmoid(fl)
        rr = lax.broadcasted_iota(jnp.int32, (TS, TS), 0)
        cc = lax.broadcasted_iota(jnp.int32, (TS, TS), 1)
        c = _mask_matmul((cc <= rr).astype(BF16), logf) + ccar[...]
        cbuf[...] = c
        ccar[...] = cbuf[TS - 1:TS, :]
        hi, mid, lo = _split3(c * LOG2E)
        lane = lax.broadcasted_iota(jnp.int32, (TS, LANES), 1)
        parts = (jnp.where(lane == SPARE_LANE, 1.0, hi).astype(BF16), mid, lo)
        ka = _nn(kb, place_ref[...])
        qat = _nt(place_t_ref[...], qb)
        for part in range(3):
            ka = ka + _nn(parts[part], bk_ref[part])
            qat = qat + _nt(bqt_ref[part], parts[part])
        ka_ref[...] = ka.astype(BF16)
        qat = qat.astype(BF16)
        vt = vb.T
        kt = kb.T
        for a in range(sub):
            cols = slice(a * TQ, (a + 1) * TQ)
            qat_ref[a] = qat[:, cols]
            for ref, mat in ((qt_ref, qt), (kt_ref, kt), (vt_ref, vt)):
                for h in range(HEADS):
                    ref[a, h * VROWS:h * VROWS + HEAD_DIM, :] = mat[h * HEAD_DIM:(h + 1) * HEAD_DIM, cols]
                    ref[a, h * VROWS + HEAD_DIM:(h + 1) * VROWS, :] = jnp.ones((VROWS - HEAD_DIM, TQ), BF16)

        u = _nt(hn, wu_ref[...])
        ubuf[HALO:HALO + TS, :] = u
        t = i * TS + lax.broadcasted_iota(jnp.int32, (TS, 1), 0)
        for g, w in enumerate(POOL_WINDOWS):
            cols = slice(g * POOL_CH, (g + 1) * POOL_CH)
            sm = ubuf[:, cols]
            step = 1
            while step < w:
                sm = sm + pltpu.roll(sm, step, 0)
                step *= 2
            cnt = jnp.minimum(t + 1, w).astype(F32)
            yg = (sm[HALO:, :] / cnt - u[:, cols]).astype(BF16)
            y_ref[:, cols] = yg
            mp_ref[:, cols] = _nn(yg, wp_ref[g])
        ubuf[0:HALO, :] = u[TS - HALO:, :]

        @pl.when(i == nt - 1)
        def _():
            _gather_finish(own_ref, all_ref, send_sems, recv_sems)

    nq = s // TQ
    aug = HEADS * AUG
    outs = (
        jax.ShapeDtypeStruct((s, d), BF16), jax.ShapeDtypeStruct((nq, HEADS * VROWS, TQ), BF16),
        jax.ShapeDtypeStruct((s, aug), BF16), jax.ShapeDtypeStruct((s, D_ATTN), BF16),
        jax.ShapeDtypeStruct((nq, aug, TQ), BF16), jax.ShapeDtypeStruct((nq, HEADS * VROWS, TQ), BF16),
        jax.ShapeDtypeStruct((nq, HEADS * VROWS, TQ), BF16),
        jax.ShapeDtypeStruct((s, LANES), F32),
        jax.ShapeDtypeStruct((s, D_POOL), BF16), jax.ShapeDtypeStruct((s, D_POOL), F32),
        jax.ShapeDtypeStruct((N_DEV,) + own_block.shape, own_block.dtype),
    )
    fixed3 = lambda i: (0, 0, 0)
    tiles3 = lambda rows: pl.BlockSpec((sub, rows, TQ), lambda i: (i, 0, 0))
    return pl.pallas_call(
        body, grid=(nt,), out_shape=outs, name="pre_attn_fwd",
        in_specs=[pl.BlockSpec((TS, d), _row), pl.BlockSpec((1, d), _fixed),
                  pl.BlockSpec(wqkv.shape, _fixed), pl.BlockSpec(wf.shape, _fixed), pl.BlockSpec(wu.shape, _fixed),
                  pl.BlockSpec((1, LANES), _fixed), pl.BlockSpec(wpool.shape, fixed3),
                  pl.BlockSpec(lay["place"].shape, _fixed), pl.BlockSpec(lay["place_t"].shape, _fixed),
                  pl.BlockSpec(lay["bias_k"].shape, fixed3), pl.BlockSpec(lay["bias_q_t"].shape, fixed3), ANY],
        out_specs=(pl.BlockSpec((TS, d), _row), tiles3(HEADS * VROWS),
                   pl.BlockSpec((TS, aug), _row), pl.BlockSpec((TS, D_ATTN), _row),
                   tiles3(aug), tiles3(HEADS * VROWS), tiles3(HEADS * VROWS),
                   pl.BlockSpec((TS, LANES), _row),
                   pl.BlockSpec((TS, D_POOL), _row), pl.BlockSpec((TS, D_POOL), _row), ANY),
        scratch_shapes=[pltpu.VMEM((TS + HALO, D_POOL), F32), pltpu.VMEM((1, LANES), F32), pltpu.VMEM((TS, LANES), F32),
                        pltpu.VMEM(own_block.shape, own_block.dtype),
                        pltpu.SemaphoreType.DMA((7,)), pltpu.SemaphoreType.DMA((7,)), pltpu.SemaphoreType.DMA],
        compiler_params=_params(1),
    )(x, g1, wqkv, wf, wu, bpad, wpool, lay["place"], lay["place_t"], lay["bias_k"], lay["bias_q_t"], own_block)


def _causal_in_tile():
    krow = lax.broadcasted_iota(jnp.int32, (TQ, TQ), 0)
    qcol = lax.broadcasted_iota(jnp.int32, (TQ, TQ), 1)
    return krow <= qcol


def _attn_fwd(ka, qat3, vt3, own_block):
    s = ka.shape[0]
    nq = s // TQ
    pass_on_step = max(nq - 2, 0)

    def body(qa_ref, ka_ref, vt_ref, own_ref, a_ref, lset_ref, all_ref, acc, out_t, st_scr, pt_scr,
             stage, send_sems, recv_sems, local_sem):
        i = pl.program_id(0)

        @pl.when(i == 0)
        def _():
            _gather_start(own_ref, all_ref, stage, send_sems, recv_sems, local_sem)

        @pl.when(i == pass_on_step)
        def _():
            _gather_pass_on(all_ref, send_sems, recv_sems)

        acc[...] = jnp.zeros_like(acc)

        def tile(j, stats, masked):
            tile_max = []
            for h in range(HEADS):
                aug = slice(h * AUG, (h + 1) * AUG)
                st = _nn(ka_ref[pl.ds(j * TQ, TQ), aug], qa_ref[0, aug, :])
                if masked:
                    st = jnp.where(_causal_in_tile(), st, NEG)
                st_scr[h] = st
                tile_max.append(jnp.max(st, axis=0, keepdims=True))
            new, scale = [], []
            for h in range(HEADS):
                m_new = jnp.maximum(stats[h], tile_max[h])
                scale.append(jnp.exp2(stats[h] - m_new))
                pt_scr[h] = jnp.exp2(st_scr[h] - m_new).astype(BF16)
                new.append(m_new)
            for h in range(HEADS):
                rows = slice(h * VROWS, (h + 1) * VROWS)
                acc[rows, :] = scale[h] * acc[rows, :] + _nn(vt_ref[j, rows, :], pt_scr[h])
            return tuple(new)

        init = tuple(jnp.full((1, TQ), NEG, F32) for _ in range(HEADS))
        stats = lax.fori_loop(0, i, functools.partial(tile, masked=False), init)
        stats = tile(i, stats, True)
        for h in range(HEADS):
            denom = acc[h * VROWS + HEAD_DIM:h * VROWS + HEAD_DIM + 1, :]
            out_t[h * HEAD_DIM:(h + 1) * HEAD_DIM, :] = acc[h * VROWS:h * VROWS + HEAD_DIM, :] / denom
            lset_ref[0, h:h + 1, :] = stats[h] + jnp.log2(denom)
        a_ref[...] = out_t[...].T

        @pl.when(i == nq - 1)
        def _():
            _gather_finish(own_ref, all_ref, send_sems, recv_sems)

    r, cdim = own_block.shape
    return pl.pallas_call(
        body, grid=(nq,), name="attn_fwd",
        out_shape=(jax.ShapeDtypeStruct((s, D_ATTN), F32), jax.ShapeDtypeStruct((nq, HEADS, TQ), F32),
                   jax.ShapeDtypeStruct((N_DEV, r, cdim), own_block.dtype)),
        in_specs=[pl.BlockSpec((1, HEADS * AUG, TQ), lambda i: (i, 0, 0)), VMEM_WHOLE, VMEM_WHOLE, ANY],
        out_specs=(pl.BlockSpec((TQ, D_ATTN), _row), pl.BlockSpec((1, HEADS, TQ), lambda i: (i, 0, 0)), ANY),
        scratch_shapes=[pltpu.VMEM((HEADS * VROWS, TQ), F32), pltpu.VMEM((D_ATTN, TQ), F32),
                        pltpu.VMEM((HEADS, TQ, TQ), F32), pltpu.VMEM((HEADS, TQ, TQ), BF16),
                        pltpu.VMEM((r, cdim), own_block.dtype),
                        pltpu.SemaphoreType.DMA((7,)), pltpu.SemaphoreType.DMA((7,)), pltpu.SemaphoreType.DMA],
        compiler_params=_params(1),
    )(qat3, ka, vt3, own_block)


def _post_attn_fwd(a, mpre, x, g_attn, g_pool, pscale, wout, g_post, g_ffn_pre):
    s, d = x.shape

    def body(a_ref, mp_ref, x_ref, ga_ref, gp_ref, ps_ref, wo_ref, gpost_ref, gpre_ref,
             mix_ref, o_ref, h1_ref, hn2_ref):
        for rows in _HALVES:
            av = a_ref[rows, :]
            mix_ref[rows, 0:D_ATTN] = (av * _rstd(av) * ga_ref[...]).astype(BF16)
            mv = mp_ref[rows, :] * ps_ref[...]
            mix_ref[rows, D_ATTN:] = (mv * _rstd(mv) * gp_ref[...]).astype(BF16)
            o = _nn(mix_ref[rows, :], wo_ref[...].reshape(d, d))
            o_ref[rows, :] = o
            h1 = x_ref[rows, :] + o * _rstd(o) * gpost_ref[...]
            h1_ref[rows, :] = h1
            hn2_ref[rows, :] = (h1 * _rstd(h1) * gpre_ref[...]).astype(BF16)

    vec = lambda n: pl.BlockSpec((1, n), _fixed)
    return pl.pallas_call(
        body, grid=(s // TS,), name="post_attn_fwd",
        out_shape=(jax.ShapeDtypeStruct((s, d), BF16), jax.ShapeDtypeStruct((s, d), F32),
                   jax.ShapeDtypeStruct((s, d), F32), jax.ShapeDtypeStruct((s, d), BF16)),
        in_specs=[pl.BlockSpec((TS, D_ATTN), _row), pl.BlockSpec((TS, D_POOL), _row), pl.BlockSpec((TS, d), _row),
                  vec(D_ATTN), vec(D_POOL), vec(D_POOL), _spec_square(0), vec(d), vec(d)],
        out_specs=(pl.BlockSpec((TS, d), _row),) * 4,
        compiler_params=_params(1),
    )(a, mpre, x, g_attn, g_pool, pscale, wout, g_post, g_ffn_pre)


def _ffn_fwd(hn2, wg, wu, wd, h1, g_post):
    s, d = h1.shape
    nc = D_FF // TN_FF
    ts = min(TS_FF, s)

    def body(hn_ref, wg_ref, wu_ref, wd_ref, h1_ref, g_ref, gate_ref, up_ref, act_ref, ff_ref, h2_ref, acc):
        j = pl.program_id(1)

        @pl.when(j == 0)
        def _():
            acc[...] = jnp.zeros_like(acc)

        for r in range(2):
            rows = slice(r * (ts // 2), (r + 1) * (ts // 2))
            hn = hn_ref[rows, :]
            gt = _nt(hn, wg_ref[...].reshape(TN_FF, d))
            up = _nt(hn, wu_ref[...].reshape(TN_FF, d))
            gate_ref[rows, :] = gt.astype(BF16)
            up_ref[rows, :] = up.astype(BF16)
            act_ref[rows, :] = (gt * jax.nn.sigmoid(gt) * up).astype(BF16)
            acc[rows, :] += _nn(act_ref[rows, :], wd_ref[...].reshape(TN_FF, d))

        @pl.when(j == nc - 1)
        def _():
            ff = acc[...]
            ff_ref[...] = ff
            h2_ref[...] = h1_ref[...] + ff * _rstd(ff) * g_ref[...]

    rowblk = pl.BlockSpec((ts, d), lambda i, j: (i, 0))
    chunk = pl.BlockSpec((ts, TN_FF), lambda i, j: (i, j))
    return pl.pallas_call(
        body, grid=(s // ts, nc), name="ffn_fwd",
        out_shape=(jax.ShapeDtypeStruct((s, D_FF), BF16),) * 3 + (jax.ShapeDtypeStruct((s, d), F32),) * 2,
        in_specs=[rowblk, _spec_ff(0), _spec_ff(1), _spec_ff(2), rowblk, pl.BlockSpec((1, d), lambda i, j: (0, 0))],
        out_specs=(chunk, chunk, chunk, rowblk, rowblk),
        scratch_shapes=[pltpu.VMEM((ts, d), F32)],
        compiler_params=_params(2),
    )(hn2, wg, wu, wd, h1, g_post)


def _tail_fwd_bwd(h2, p, tgt, ff, wple, wpg, g_ple, g_ffn_post):
    s, d = h2.shape

    def body(h2_ref, p_ref, t_ref, ff_ref, wple_ref, wpg_ref, gple_ref, gfp_ref,
             dh2_ref, dff_ref, dgl_ref, dpp_ref, h2b_ref, pb_ref, loss_ref, dgple_ref, dgfp_ref):
        i = pl.program_id(0)

        @pl.when(i == 0)
        def _():
            loss_ref[...] = jnp.zeros_like(loss_ref)
            dgple_ref[...] = jnp.zeros_like(dgple_ref)
            dgfp_ref[...] = jnp.zeros_like(dgfp_ref)

        h2 = h2_ref[...]
        h2b = h2.astype(BF16)
        h2b_ref[...] = h2b
        pb = p_ref[...].astype(BF16)
        pb_ref[...] = pb
        pp = _nt(pb, wple_ref[...])
        gple = gple_ref[...]
        e = pp * _rstd(pp) * gple
        wpg = wpg_ref[...].reshape(d, d)
        sg = jax.nn.sigmoid(_nn(h2b, wpg))
        diff = h2 + sg * e - t_ref[...]
        sq = jnp.sum(jnp.sum(diff * diff, axis=1, keepdims=True), axis=0, keepdims=True)
        loss_ref[...] += jnp.broadcast_to(sq * (0.5 / d), loss_ref.shape)
        dh3 = diff * (1.0 / d)
        dgl = (dh3 * e * sg * (1.0 - sg)).astype(BF16)
        dgl_ref[...] = dgl
        dh2 = dh3 + _nt(dgl, wpg)
        dh2_ref[...] = dh2
        dpp, dg = _rms_bwd(pp, gple, dh3 * sg)
        dpp_ref[...] = dpp.astype(BF16)
        dgple_ref[...] += dg
        dff, dg = _rms_bwd(ff_ref[...], gfp_ref[...], dh2)
        dff_ref[...] = dff.astype(BF16)
        dgfp_ref[...] += dg

    rowblk = pl.BlockSpec((TS, d), _row)
    vec = pl.BlockSpec((1, d), _fixed)
    return pl.pallas_call(
        body, grid=(s // TS,), name="tail_fwd_bwd",
        out_shape=(jax.ShapeDtypeStruct((s, d), F32), jax.ShapeDtypeStruct((s, d), BF16),
                   jax.ShapeDtypeStruct((s, d), BF16), jax.ShapeDtypeStruct((s, d), BF16),
                   jax.ShapeDtypeStruct((s, d), BF16), jax.ShapeDtypeStruct((s, D_PLE), BF16),
                   jax.ShapeDtypeStruct((8, LANES), F32), jax.ShapeDtypeStruct((1, d), F32),
                   jax.ShapeDtypeStruct((1, d), F32)),
        in_specs=[rowblk, pl.BlockSpec((TS, D_PLE), _row), rowblk, rowblk,
                  pl.BlockSpec(wple.shape, _fixed), _spec_square(1), vec, vec],
        out_specs=(rowblk, rowblk, rowblk, rowblk, rowblk, pl.BlockSpec((TS, D_PLE), _row),
                   pl.BlockSpec((8, LANES), _fixed), vec, vec),
        compiler_params=_params(1),
    )(h2, p, tgt, ff, wple, wpg, g_ple, g_ffn_post)


def _ffn_bwd(dff, gate, up, wd, wg, wu, h1, dh2, g_pre):
    s, d = h1.shape
    nc = D_FF // TN_FF
    ts = min(TS_FF, s)

    def body(dff_ref, gate_ref, up_ref, wd_ref, wg_ref, wu_ref, h1_ref, dh2_ref, g_ref,
             dgate_ref, dup_ref, dh1_ref, dg_ref, acc):
        i = pl.program_id(0)
        j = pl.program_id(1)

        @pl.when((i == 0) & (j == 0))
        def _():
            dg_ref[...] = jnp.zeros_like(dg_ref)

        @pl.when(j == 0)
        def _():
            acc[...] = jnp.zeros_like(acc)

        for r in range(2):
            rows = slice(r * (ts // 2), (r + 1) * (ts // 2))
            dact = _nt(dff_ref[rows, :], wd_ref[...].reshape(TN_FF, d))
            gt = gate_ref[rows, :].astype(F32)
            sg = jax.nn.sigmoid(gt)
            dup_ref[rows, :] = (dact * gt * sg).astype(BF16)
            dgate_ref[rows, :] = (dact * up_ref[rows, :].astype(F32) * (sg * (1.0 + gt * (1.0 - sg)))).astype(BF16)
            acc[rows, :] += (_nn(dgate_ref[rows, :], wg_ref[...].reshape(TN_FF, d))
                             + _nn(dup_ref[rows, :], wu_ref[...].reshape(TN_FF, d)))

        @pl.when(j == nc - 1)
        def _():
            dv, dg = _rms_bwd(h1_ref[...], g_ref[...], acc[...])
            dh1_ref[...] = dh2_ref[...] + dv
            dg_ref[...] += dg

    rowblk = pl.BlockSpec((ts, d), lambda i, j: (i, 0))
    chunk = pl.BlockSpec((ts, TN_FF), lambda i, j: (i, j))
    vec = pl.BlockSpec((1, d), lambda i, j: (0, 0))
    return pl.pallas_call(
        body, grid=(s // ts, nc), name="ffn_bwd",
        out_shape=(jax.ShapeDtypeStruct((s, D_FF), BF16), jax.ShapeDtypeStruct((s, D_FF), BF16),
                   jax.ShapeDtypeStruct((s, d), F32), jax.ShapeDtypeStruct((1, d), F32)),
        in_specs=[rowblk, chunk, chunk, _spec_ff(2), _spec_ff(0), _spec_ff(1), rowblk, rowblk, vec],
        out_specs=(chunk, chunk, rowblk, vec),
        scratch_shapes=[pltpu.VMEM((ts, d), F32)],
        compiler_params=_params(2),
    )(dff, gate, up, wd, wg, wu, h1, dh2, g_pre)


def _post_attn_bwd(dh1, o, a, mpre, wout, wpool, g_post, g_attn, g_pool, pscale):
    s, d = dh1.shape
    sub = TS // TQ

    def body(dh1_ref, o_ref, a_ref, mp_ref, wo_ref, wp_ref, gpost_ref, ga_ref, gp_ref, ps_ref,
             dob_ref, dab_ref, dat_ref, dlt_ref, dmpb_ref, dy_ref, dgpost_ref, dga_ref, dgp_ref, dps_ref):
        i = pl.program_id(0)

        @pl.when(i == 0)
        def _():
            dgpost_ref[...] = jnp.zeros_like(dgpost_ref)
            dga_ref[...] = jnp.zeros_like(dga_ref)
            dgp_ref[...] = jnp.zeros_like(dgp_ref)
            dps_ref[...] = jnp.zeros_like(dps_ref)

        do, dg = _rms_bwd(o_ref[...], gpost_ref[...], dh1_ref[...])
        dgpost_ref[...] += dg
        dob = do.astype(BF16)
        dob_ref[...] = dob
        dmix = _nt(dob, wo_ref[...].reshape(d, d))

        av = a_ref[...]
        da, dg = _rms_bwd(av, ga_ref[...], dmix[:, 0:D_ATTN])
        dga_ref[...] += dg
        dab = da.astype(BF16)
        dab_ref[...] = dab
        dat = dab.T
        hsel = (lax.shift_right_logical(lax.broadcasted_iota(jnp.int32, (HEADS, D_ATTN), 1), 6)
                == lax.broadcasted_iota(jnp.int32, (HEADS, D_ATTN), 0)).astype(F32)
        dlt = lax.dot_general(hsel, da * av, (((1,), (1,)), ((), ())), precision=HIGHEST, preferred_element_type=F32)
        for q in range(sub):
            dlt_ref[q] = dlt[:, q * TQ:(q + 1) * TQ]
            dat_ref[q] = dat[:, q * TQ:(q + 1) * TQ]

        ps = ps_ref[...]
        mp = mp_ref[...]
        dm, dg = _rms_bwd(mp * ps, gp_ref[...], dmix[:, D_ATTN:])
        dgp_ref[...] += dg
        dps_ref[...] += jnp.sum(dm * mp, axis=0, keepdims=True)
        dmpb = (dm * ps).astype(BF16)
        dmpb_ref[...] = dmpb
        for g in range(len(POOL_WINDOWS)):
            cols = slice(g * POOL_CH, (g + 1) * POOL_CH)
            dy_ref[:, cols] = _nt(dmpb[:, cols], wp_ref[g])

    rowblk = pl.BlockSpec((TS, d), _row)
    half = pl.BlockSpec((TS, D_ATTN), _row)
    vec = lambda n: pl.BlockSpec((1, n), _fixed)
    return pl.pallas_call(
        body, grid=(s // TS,), name="post_attn_bwd",
        out_shape=(jax.ShapeDtypeStruct((s, d), BF16), jax.ShapeDtypeStruct((s, D_ATTN), BF16),
                   jax.ShapeDtypeStruct((s // TQ, D_ATTN, TQ), BF16),
                   jax.ShapeDtypeStruct((s // TQ, HEADS, TQ), F32), jax.ShapeDtypeStruct((s, D_POOL), BF16),
                   jax.ShapeDtypeStruct((s, D_POOL), F32), jax.ShapeDtypeStruct((1, d), F32),
                   jax.ShapeDtypeStruct((1, D_ATTN), F32), jax.ShapeDtypeStruct((1, D_POOL), F32),
                   jax.ShapeDtypeStruct((1, D_POOL), F32)),
        in_specs=[rowblk, rowblk, half, half, _spec_square(0),
                  pl.BlockSpec(wpool.shape, lambda i: (0, 0, 0)), vec(d), vec(D_ATTN), vec(D_POOL), vec(D_POOL)],
        out_specs=(rowblk, half, pl.BlockSpec((sub, D_ATTN, TQ), lambda i: (i, 0, 0)),
                   pl.BlockSpec((sub, HEADS, TQ), lambda i: (i, 0, 0)), half, half,
                   vec(d), vec(D_ATTN), vec(D_POOL), vec(D_POOL)),
        compiler_params=_params(1),
    )(dh1, o, a, mpre, wout, wpool, g_post, g_attn, g_pool, pscale)


def _attn_bwd(ka, v, kt3, qat3, qt3, dot3, lset3, dlt3, chip_blocks):
    s = ka.shape[0]
    nq = s // TQ

    def body(ka_ref, v_ref, kt_ref, qat_ref, qt_ref, dot_ref, lset_ref, dlt_ref, b_ref,
             dqt_ref, dkt_ref, dvt_ref, got_ref, pt_scr, ptb_scr, dsb_scr,
             stage, send_sems, recv_sems, local_sem):
        j = pl.program_id(0)

        @pl.when(j == 0)
        def _():
            _chips_start(b_ref, got_ref, stage, send_sems, recv_sems, local_sem)
            dqt_ref[...] = jnp.zeros_like(dqt_ref)

        def tile(i, masked):
            def accumulate(ref, idx, val):
                if masked:
                    ref[idx] = val
                else:
                    ref[idx] += val

            for h in range(HEADS):
                aug = slice(h * AUG, (h + 1) * AUG)
                st = _nn(ka_ref[:, aug], qat_ref[i, aug, :]) - lset_ref[i, h:h + 1, :]
                if masked:
                    st = jnp.where(_causal_in_tile(), st, NEG)
                pt = jnp.exp2(st)
                pt_scr[h] = pt
                ptb_scr[h] = pt.astype(BF16)
            heads = [(h, slice(h * HEAD_DIM, (h + 1) * HEAD_DIM)) for h in range(HEADS)]
            for h, hs in heads:
                dst = pt_scr[h] * (_nn(v_ref[:, hs], dot_ref[i, hs, :]) - dlt_ref[i, h:h + 1, :])
                dsb_scr[h] = dst.astype(BF16)
            for h, hs in heads:
                accumulate(dvt_ref, (0, hs, slice(None)), _nt(dot_ref[i, hs, :], ptb_scr[h]))
            for h, hs in heads:
                rows = slice(h * VROWS, (h + 1) * VROWS)
                accumulate(dkt_ref, (0, rows, slice(None)), _nt(qt_ref[i, rows, :], dsb_scr[h]))
            for h, hs in heads:
                rows = slice(h * VROWS, (h + 1) * VROWS)
                dqt_ref[i, rows, :] += _nn(kt_ref[0, rows, :], dsb_scr[h])

        first = j + 1
        pairs = (nq - first) // 2

        def step(p, carry):
            tile(first + 2 * p, False)
            tile(first + 2 * p + 1, False)
            return carry

        tile(j, True)
        lax.fori_loop(0, pairs, step, 0)

        @pl.when(first + 2 * pairs < nq)
        def _():
            tile(nq - 1, False)

        @pl.when(j == nq - 1)
        def _():
            _chips_finish(b_ref, got_ref, send_sems, recv_sems)

    blk = pl.BlockSpec((TQ, D_ATTN), _row)
    tile_t = lambda rows: pl.BlockSpec((1, rows, TQ), lambda j: (j, 0, 0))
    per_tile = lambda rows: jax.ShapeDtypeStruct((nq, rows, TQ), F32)
    _, r, cdim = chip_blocks.shape
    return pl.pallas_call(
        body, grid=(nq,), name="attn_bwd",
        out_shape=(per_tile(HEADS * VROWS), per_tile(HEADS * VROWS), per_tile(D_ATTN),
                   jax.ShapeDtypeStruct(chip_blocks.shape, chip_blocks.dtype)),
        in_specs=[pl.BlockSpec((TQ, HEADS * AUG), _row), blk, tile_t(HEADS * VROWS),
                  VMEM_WHOLE, VMEM_WHOLE, VMEM_WHOLE, VMEM_WHOLE, VMEM_WHOLE, ANY],
        out_specs=(pl.BlockSpec((nq, HEADS * VROWS, TQ), lambda j: (0, 0, 0)), tile_t(HEADS * VROWS), tile_t(D_ATTN),
                   ANY),
        scratch_shapes=[pltpu.VMEM((HEADS, TQ, TQ), F32), pltpu.VMEM((HEADS, TQ, TQ), BF16),
                        pltpu.VMEM((HEADS, TQ, TQ), BF16), pltpu.VMEM((r, cdim), chip_blocks.dtype),
                        pltpu.SemaphoreType.DMA((3,)), pltpu.SemaphoreType.DMA((3,)), pltpu.SemaphoreType.DMA],
        compiler_params=_params(1),
    )(ka, v, kt3, qat3, qt3, dot3, lset3, dlt3, chip_blocks)


def _pre_attn_bwd(dqt3, dkt3, dvt3, dcs, drs, fl, dy, x, dh1, g1, wqkv, wf, wu):
    s, d = x.shape
    nt = s // TS
    n = TS + HALO
    sub = TS // TQ
    qkv, fcols = 3 * D_ATTN, 3 * D_ATTN + LANES

    def body(dqt_ref, dkt_ref, dvt_ref, dcs_ref, drs_ref, fl_ref, dy_ref, x_ref, dh1_ref, g_ref, wqkv_ref, wf_ref, wu_ref,
             gx_ref, dz_ref, dg_ref, db_ref, ybuf, ccar, dlog):
        dqkv_ref = dz_ref.at[:, 0:qkv]
        dfb_ref = dz_ref.at[:, qkv:fcols]
        dub_ref = dz_ref.at[:, fcols:]
        i = pl.program_id(0)
        ti = nt - 1 - i

        @pl.when(i == 0)
        def _():
            ybuf[TS:n, :] = jnp.zeros((HALO, D_POOL), F32)
            ccar[...] = jnp.zeros_like(ccar)
            dg_ref[...] = jnp.zeros_like(dg_ref)
            db_ref[...] = jnp.zeros_like(db_ref)

        rr = lax.broadcasted_iota(jnp.int32, (TS, TS), 0)
        cc = lax.broadcasted_iota(jnp.int32, (TS, TS), 1)
        dlog[...] = ccar[...] + _mask_matmul((cc >= rr).astype(BF16), drs_ref[...] - dcs_ref[...])
        ccar[...] = dlog[0:1, :]
        df = dlog[...] * jax.nn.sigmoid(-fl_ref[...])
        db_ref[...] += jnp.sum(df, axis=0, keepdims=True)
        dfb = df.astype(BF16)
        dfb_ref[...] = dfb

        t = ti * TS + lax.broadcasted_iota(jnp.int32, (TS, 1), 0)
        dy = dy_ref[...]
        for g, w in enumerate(POOL_WINDOWS):
            cols = slice(g * POOL_CH, (g + 1) * POOL_CH)
            ybuf[0:TS, cols] = dy[:, cols] / jnp.minimum(t + 1, w).astype(F32)
        for g, w in enumerate(POOL_WINDOWS):
            cols = slice(g * POOL_CH, (g + 1) * POOL_CH)
            sm = ybuf[:, cols]
            step = 1
            while step < w:
                sm = sm + pltpu.roll(sm, n - step, 0)
                step *= 2
            dub_ref[:, cols] = (sm[0:TS, :] - dy[:, cols]).astype(BF16)
        ybuf[TS:n, :] = ybuf[0:HALO, :]

        for a in range(sub):
            rows = slice(a * TQ, (a + 1) * TQ)
            for h in range(HEADS):
                src = slice(h * VROWS, h * VROWS + HEAD_DIM)
                dqkv_ref[rows, h * HEAD_DIM:(h + 1) * HEAD_DIM] = (dqt_ref[a, src, :].T * 0.125).astype(BF16)
                dqkv_ref[rows, D_ATTN + h * HEAD_DIM:D_ATTN + (h + 1) * HEAD_DIM] = dkt_ref[a, src, :].T.astype(BF16)
            dqkv_ref[rows, 2 * D_ATTN:] = dvt_ref[a].T.astype(BF16)
        dhn = _nn(dqkv_ref[...], wqkv_ref[...]) + _nn(dfb, wf_ref[...]) + _nn(dub_ref[...], wu_ref[...])
        dx, dg = _rms_bwd(x_ref[...], g_ref[...], dhn)
        gx_ref[...] = dh1_ref[...] + dx
        dg_ref[...] += dg

    rev = lambda i: (nt - 1 - i, 0)
    blk = lambda w: pl.BlockSpec((TS, w), rev)
    return pl.pallas_call(
        body, grid=(nt,), name="pre_attn_bwd",
        out_shape=(jax.ShapeDtypeStruct((s, d), F32), jax.ShapeDtypeStruct((s, fcols + D_POOL), BF16),
                   jax.ShapeDtypeStruct((1, d), F32), jax.ShapeDtypeStruct((1, LANES), F32)),
        in_specs=[pl.BlockSpec((sub, HEADS * VROWS, TQ), lambda i: (nt - 1 - i, 0, 0)),
                  pl.BlockSpec((sub, HEADS * VROWS, TQ), lambda i: (nt - 1 - i, 0, 0)),
                  pl.BlockSpec((sub, D_ATTN, TQ), lambda i: (nt - 1 - i, 0, 0)),
                  blk(LANES), blk(LANES), blk(LANES), blk(D_POOL), blk(d), blk(d),
                  pl.BlockSpec((1, d), _fixed), pl.BlockSpec(wqkv.shape, _fixed), pl.BlockSpec(wf.shape, _fixed),
                  pl.BlockSpec(wu.shape, _fixed)],
        out_specs=(blk(d), blk(fcols + D_POOL), pl.BlockSpec((1, d), _fixed), pl.BlockSpec((1, LANES), _fixed)),
        scratch_shapes=[pltpu.VMEM((n, D_POOL), F32), pltpu.VMEM((1, LANES), F32), pltpu.VMEM((TS, LANES), F32)],
        compiler_params=_params(1),
    )(dqt3, dkt3, dvt3, dcs, drs, fl, dy, x, dh1, g1, wqkv, wf, wu)


def _wgrad(a, b, out_dtype, name):
    s, m = a.shape
    n = b.shape[1]
    tm = max(t for t in range(LANES, min(m, TM_WGRAD) + 1, LANES) if m % t == 0)
    ts = min(TS_WGRAD, s)
    ns = s // ts

    def body(a_ref, b_ref, o_ref, acc):
        i = pl.program_id(1)

        @pl.when(i == 0)
        def _():
            acc[...] = jnp.zeros_like(acc)

        acc[...] += _tn(a_ref[...], b_ref[...])

        @pl.when(i == ns - 1)
        def _():
            o_ref[...] = acc[...].astype(out_dtype)

    return pl.pallas_call(
        body, grid=(m // tm, ns), name=name, out_shape=jax.ShapeDtypeStruct((m, n), out_dtype),
        in_specs=[pl.BlockSpec((ts, tm), lambda j, i: (i, j)), pl.BlockSpec((ts, n), lambda j, i: (i, 0))],
        out_specs=pl.BlockSpec((tm, n), lambda j, i: (j, 0)),
        scratch_shapes=[pltpu.VMEM((tm, n), F32)],
        compiler_params=_params(2),
    )(a, b)


def _adamw(w, g, m, v):
    m = ADAM_B1 * m + (1.0 - ADAM_B1) * g
    v = ADAM_B2 * v + (1.0 - ADAM_B2) * (g * g)
    m_hat = m / (1.0 - ADAM_B1 ** ADAM_STEP)
    v_hat = v / (1.0 - ADAM_B2 ** ADAM_STEP)
    delta = -ADAM_LR * (m_hat / (jnp.sqrt(v_hat) + ADAM_EPS) + ADAM_WD * w)
    return delta, m, v


def _sum_update(p_ref, w_ref, m_ref, v_ref, g_ref, d_ref, nm_ref, nv_ref):
    g = p_ref[0].astype(F32)
    for k in range(1, p_ref.shape[0]):
        g = g + p_ref[k].astype(F32)
    g_ref[...] = g
    d_ref[...], nm_ref[...], nv_ref[...] = _adamw(w_ref[...], g, m_ref[...], v_ref[...])


def _reduce_update_rest(parts, w, m, v, chip_blocks, small_block):
    nk, r, c = parts.shape
    ns = r // TR_REST

    def body(p_ref, w_ref, m_ref, v_ref, b_ref, sm_ref, g_ref, d_ref, nm_ref, nv_ref, got_ref, all_ref,
             stage_b, stage_s, send_b, recv_b, local_b, send_s, recv_s, local_s):
        i = pl.program_id(0)

        @pl.when(i == 0)
        def _():
            _chips_start(b_ref, got_ref, stage_b, send_b, recv_b, local_b)
            _gather_start(sm_ref, all_ref, stage_s, send_s, recv_s, local_s)

        _sum_update(p_ref, w_ref, m_ref, v_ref, g_ref, d_ref, nm_ref, nv_ref)

        @pl.when(i == ns - 1)
        def _():
            _gather_pass_on(all_ref, send_s, recv_s)
            _chips_finish(b_ref, got_ref, send_b, recv_b)
            _gather_finish(sm_ref, all_ref, send_s, recv_s)

    blk = pl.BlockSpec((TR_REST, c), _row)
    out = jax.ShapeDtypeStruct((r, c), F32)
    dma = pltpu.SemaphoreType.DMA
    return pl.pallas_call(
        body, grid=(ns,), name="reduce_update_rest",
        out_shape=(out,) * 4 + (jax.ShapeDtypeStruct(chip_blocks.shape, chip_blocks.dtype),
                                jax.ShapeDtypeStruct((N_DEV,) + small_block.shape, small_block.dtype)),
        in_specs=[pl.BlockSpec((nk, TR_REST, c), lambda i: (0, i, 0)), blk, blk, blk, ANY, ANY],
        out_specs=(blk,) * 4 + (ANY, ANY),
        scratch_shapes=[pltpu.VMEM(chip_blocks.shape[1:], chip_blocks.dtype), pltpu.VMEM(small_block.shape, small_block.dtype),
                        dma((3,)), dma((3,)), dma, dma((7,)), dma((7,)), dma],
        compiler_params=_params(1),
    )(parts, w, m, v, chip_blocks, small_block)


def _reduce_update_big(parts, w, m, v, tr, name):
    nk, r, c = parts.shape

    def body(p_ref, w_ref, m_ref, v_ref, g_ref, d_ref, nm_ref, nv_ref):
        _sum_update(p_ref, w_ref, m_ref, v_ref, g_ref, d_ref, nm_ref, nv_ref)

    blk = pl.BlockSpec((tr, c), _row)
    out = jax.ShapeDtypeStruct((r, c), F32)
    return pl.pallas_call(
        body, grid=(r // tr,), name=name, out_shape=(out,) * 4,
        in_specs=[pl.BlockSpec((nk, tr, c), lambda i: (0, i, 0)), blk, blk, blk],
        out_specs=(blk,) * 4, compiler_params=_params(1),
    )(parts, w, m, v)


def _reduce_update_small(parts, w, m, v):
    nd = parts.shape[0]

    def body(p_ref, w_ref, m_ref, v_ref, g_ref, d_ref, nm_ref, nv_ref):
        g = p_ref[0]
        for k in range(1, nd):
            g = g + p_ref[k]
        g_ref[...] = g
        d_ref[...], nm_ref[...], nv_ref[...] = _adamw(w_ref[...], g, m_ref[...], v_ref[...])

    out = jax.ShapeDtypeStruct(w.shape, F32)
    return pl.pallas_call(body, name="reduce_update_small", out_shape=(out,) * 4,
                          compiler_params=pltpu.CompilerParams(vmem_limit_bytes=VMEM_LIMIT))(parts, w, m, v)


MESH = pl.DeviceIdType.MESH


def _copy_through_vmem(src_hbm, dst_hbm, stage, sem):
    load = pltpu.make_async_copy(src_hbm, stage, sem)
    load.start()
    load.wait()
    store = pltpu.make_async_copy(stage, dst_hbm, sem)
    store.start()
    store.wait()


class _GatherPlan:
    def __init__(self, x_ref, out_ref, send_sems, recv_sems):
        x, y, c = lax.axis_index("x"), lax.axis_index("y"), lax.axis_index("c")
        self.me, self.sibling, self.c = (x, y, c), (x, y, 1 - c), c
        self.chips = [(1 - x, y), (x, 1 - y), (1 - x, 1 - y)]
        self.x_ref, self.out_ref, self.send_sems, self.recv_sems = x_ref, out_ref, send_sems, recv_sems

    def slot(self, px, py, pc):
        return self.out_ref.at[4 * px + 2 * py + pc]

    def copy(self, k, block, to, src=None):
        return pltpu.make_async_remote_copy(
            src_ref=self.slot(*block) if src is None else src, dst_ref=self.slot(*block),
            send_sem=self.send_sems.at[k], recv_sem=self.recv_sems.at[k], device_id=to, device_id_type=MESH)

    def first(self):
        return [self.copy(0, self.me, self.sibling, src=self.x_ref)] + [
            self.copy(1 + j, self.me, (*chip, self.c), src=self.x_ref) for j, chip in enumerate(self.chips)]

    def passed(self):
        return [self.copy(4 + j, (*chip, self.c), self.sibling) for j, chip in enumerate(self.chips)]


def _gather_start(x_ref, out_ref, stage, send_sems, recv_sems, local_sem):
    plan = _GatherPlan(x_ref, out_ref, send_sems, recv_sems)
    for cp in plan.first():
        cp.start()
    _copy_through_vmem(x_ref, plan.slot(*plan.me), stage, local_sem)


def _gather_pass_on(out_ref, send_sems, recv_sems):
    plan = _GatherPlan(None, out_ref, send_sems, recv_sems)
    passed = plan.passed()
    for j, chip in enumerate(plan.chips):
        plan.copy(1 + j, (*chip, plan.c), plan.me).wait_recv()
        passed[j].start()


def _gather_finish(x_ref, out_ref, send_sems, recv_sems):
    plan = _GatherPlan(x_ref, out_ref, send_sems, recv_sems)
    plan.copy(0, plan.sibling, plan.me).wait_recv()
    for j, chip in enumerate(plan.chips):
        plan.copy(4 + j, (*chip, 1 - plan.c), plan.me).wait_recv()
    for cp in plan.first() + plan.passed():
        cp.wait_send()


def _all_gather(xs, name):
    r, cdim = xs.shape

    def body(x_ref, out_ref, stage, send_sems, recv_sems, local_sem):
        _gather_start(x_ref, out_ref, stage, send_sems, recv_sems, local_sem)
        _gather_pass_on(out_ref, send_sems, recv_sems)
        _gather_finish(x_ref, out_ref, send_sems, recv_sems)

    return pl.pallas_call(
        body, name=name, out_shape=jax.ShapeDtypeStruct((N_DEV, r, cdim), xs.dtype),
        in_specs=[ANY], out_specs=ANY,
        scratch_shapes=[pltpu.VMEM((r, cdim), xs.dtype), pltpu.SemaphoreType.DMA((7,)), pltpu.SemaphoreType.DMA((7,)),
                        pltpu.SemaphoreType.DMA],
        compiler_params=pltpu.CompilerParams(vmem_limit_bytes=VMEM_LIMIT),
    )(xs)


def _rs_pair_sum(core, t, name):
    _, r, cdim = t.shape
    nk = N_DEV // 2

    def body(core_ref, own_ref, t_ref, o_ref, landing, send_sems, recv_sems):
        k = pl.program_id(0)
        x, y, c = lax.axis_index("x"), lax.axis_index("y"), lax.axis_index("c")

        def copy(kk):
            return pltpu.make_async_remote_copy(
                src_ref=t_ref.at[2 * kk + (1 - c)], dst_ref=landing.at[kk],
                send_sem=send_sems.at[kk], recv_sem=recv_sems.at[kk], device_id=(x, y, 1 - c), device_id_type=MESH)

        @pl.when(k == 0)
        def _():
            for kk in range(nk):
                copy(kk).start()

        copy(k).wait_recv()
        o_ref[0] = (own_ref[0].astype(F32) + landing[k].astype(F32)).astype(BF16)

        @pl.when(k == nk - 1)
        def _():
            for kk in range(nk):
                copy(kk).wait_send()

    blk = pl.BlockSpec((1, r, cdim), lambda k, core_ref: (k, 0, 0))
    return pl.pallas_call(
        body, name=name, out_shape=jax.ShapeDtypeStruct((nk, r, cdim), BF16),
        grid_spec=pltpu.PrefetchScalarGridSpec(
            num_scalar_prefetch=1, grid=(nk,),
            in_specs=[pl.BlockSpec((1, r, cdim), lambda k, core_ref: (2 * k + core_ref[0], 0, 0)), ANY],
            out_specs=blk,
            scratch_shapes=[pltpu.VMEM((nk, r, cdim), BF16), pltpu.SemaphoreType.DMA((nk,)),
                            pltpu.SemaphoreType.DMA((nk,))]),
        compiler_params=_params(1),
    )(core, t, t)


def _chips_start(b_ref, out_ref, stage, send_sems, recv_sems, local_sem):
    x, y, c = lax.axis_index("x"), lax.axis_index("y"), lax.axis_index("c")
    mychip = 2 * x + y
    for j, (px, py) in enumerate([(1 - x, y), (x, 1 - y), (1 - x, 1 - y)]):
        pltpu.make_async_remote_copy(
            src_ref=b_ref.at[2 * px + py], dst_ref=out_ref.at[mychip],
            send_sem=send_sems.at[j], recv_sem=recv_sems.at[j], device_id=(px, py, c), device_id_type=MESH).start()
    _copy_through_vmem(b_ref.at[mychip], out_ref.at[mychip], stage, local_sem)


def _chips_finish(b_ref, out_ref, send_sems, recv_sems):
    x, y, c = lax.axis_index("x"), lax.axis_index("y"), lax.axis_index("c")
    for j, (px, py) in enumerate([(1 - x, y), (x, 1 - y), (1 - x, 1 - y)]):
        pltpu.make_async_remote_copy(
            src_ref=b_ref.at[2 * px + py], dst_ref=out_ref.at[2 * px + py],
            send_sem=send_sems.at[j], recv_sem=recv_sems.at[j], device_id=(px, py, c), device_id_type=MESH).wait()


def _pad_rows(a, rows):
    return jnp.pad(a, ((0, rows - a.shape[0]), (0, 0)))


def _pack_in(w_in):
    return _pad_rows(w_in[0].T, ROWS_IN)


def _unpack_in(r):
    return r[0:SHARD_IN].T[None]


def _pack_rest(w_out, w_gate, w_up, w_down, w_ple, w_pg):
    head = _pad_rows(jnp.concatenate([w_out[0], w_pg[0], w_ple[0].T.reshape(32, D_MODEL)], axis=0), OFF_GATE)
    return jnp.concatenate([head, w_gate[0].T, w_up[0].T, w_down[0]], axis=0)


def _unpack_rest(r):
    return (r[0:OFF_PG][None], r[OFF_GATE:OFF_UP].T[None], r[OFF_UP:OFF_DOWN].T[None], r[OFF_DOWN:ROWS_REST][None],
            r[OFF_PLE:OFF_PLE + 32].reshape(128, D_PLE).T[None], r[OFF_PG:OFF_PLE][None])


def _pack_small(w_pool, g_mix_pre, g_mix_post, g_ffn_pre, g_ffn_post, g_ple, g_attn, g_pool, pool_scale, b_forget,
                loss=None):
    def row(vrow):
        return jnp.pad(vrow.reshape(1, -1), ((0, 0), (0, D_MODEL - vrow.size)))
    rows = [w_pool.reshape(64, D_MODEL), row(g_mix_pre), row(g_mix_post), row(g_ffn_pre), row(g_ffn_post), row(g_ple),
            row(g_attn), row(g_pool), row(pool_scale), row(b_forget),
            row(loss) if loss is not None else jnp.zeros((1, D_MODEL), F32)]
    return _pad_rows(jnp.concatenate(rows, axis=0), SMALL_ROWS)


def _unpack_small(r):
    return dict(
        w_pool=r[0:64].reshape(1, 4, POOL_CH, POOL_CH), g_mix_pre=r[ROW_G_MIX_PRE:ROW_G_MIX_PRE + 1],
        g_mix_post=r[ROW_G_MIX_POST:ROW_G_MIX_POST + 1], g_ffn_pre=r[ROW_G_FFN_PRE:ROW_G_FFN_PRE + 1],
        g_ffn_post=r[ROW_G_FFN_POST:ROW_G_FFN_POST + 1], g_ple=r[ROW_G_PLE:ROW_G_PLE + 1],
        g_attn_grp=r[ROW_G_ATTN:ROW_G_ATTN + 1, 0:D_ATTN], g_pool_grp=r[ROW_G_POOL:ROW_G_POOL + 1, 0:D_POOL],
        pool_scale=r[ROW_POOL_SCALE:ROW_POOL_SCALE + 1, 0:D_POOL], b_forget=r[ROW_B_FORGET:ROW_B_FORGET + 1, 0:HEADS])


def _step(x, p, tgt, small, in_w, in_m, in_v, rest_w, rest_m, rest_v):
    core = lax.axis_index("c").astype(jnp.int32).reshape(1)
    win_t = _all_gather(in_w.astype(BF16), "gather_w_in")[:, 0:SHARD_IN].reshape(D_IN, D_MODEL)
    wqkv = win_t[0:3 * D_ATTN]
    wf = _pad_rows(win_t[3 * D_ATTN:3 * D_ATTN + HEADS], LANES)
    wu = win_t[3 * D_ATTN + HEADS:]
    wpool = small["w_pool"].astype(BF16)
    bpad = jnp.pad(small["b_forget"], ((0, 0), (0, LANES - HEADS)))

    lay = _attn_layout_constants()
    rest_b = rest_w.astype(BF16)
    hn, qt3, ka, v, qat3, vt3, kt3, fl, y, mpre, gh = _pre_attn_fwd(x, small["g_mix_pre"], wqkv, wf, wu, bpad, wpool, lay,
                                                                 rest_b[0:OFF_GATE])
    a, lset3, gf = _attn_fwd(ka, qat3, vt3, rest_b[OFF_GATE:])
    wple_t = gh[:, OFF_PLE:OFF_PLE + 32].reshape(D_MODEL, D_PLE)
    mix, o, h1, hn2 = _post_attn_fwd(a, mpre, x, small["g_attn_grp"], small["g_pool_grp"], small["pool_scale"], gh,
                                     small["g_mix_post"], small["g_ffn_pre"])
    gate, up, act, ff, h2 = _ffn_fwd(hn2, gf, gf, gf, h1, small["g_ffn_post"])
    dh2, dff, dgl, dpp, h2b, pb, loss8, dg_ple, dg_ffn_post = _tail_fwd_bwd(
        h2, p, tgt, ff, wple_t, gh, small["g_ple"], small["g_ffn_post"])
    dgate, dup, dh1, dg_ffn_pre = _ffn_bwd(dff, gate, up, gf, gf, gf, h1, dh2, small["g_ffn_pre"])
    dob, dab, dat3, dlt3, dmpb, dy, dg_mix_post, dg_attn, dg_pool, dps = _post_attn_bwd(
        dh1, o, a, mpre, gh, wpool, small["g_mix_post"], small["g_attn_grp"], small["g_pool_grp"], small["pool_scale"])

    nd = N_DEV
    send_rest = jnp.concatenate([
        _wgrad(mix, dob, BF16, "wgrad_out").reshape(nd, 128, D_MODEL),
        _wgrad(h2b, dgl, BF16, "wgrad_ple_gate").reshape(nd, 128, D_MODEL),
        _wgrad(dpp, pb, BF16, "wgrad_ple").reshape(nd, 32, D_MODEL),
        jnp.zeros((nd, OFF_GATE - OFF_PLE - 32, D_MODEL), BF16),
        _wgrad(dgate, hn2, BF16, "wgrad_gate").reshape(nd, SHARD_FF, D_MODEL),
        _wgrad(dup, hn2, BF16, "wgrad_up").reshape(nd, SHARD_FF, D_MODEL),
        _wgrad(act, dff, BF16, "wgrad_down").reshape(nd, SHARD_FF, D_MODEL)], axis=1)
    pair_rest = _rs_pair_sum(core, send_rest, "rs_pair_sum_rest")

    dqt3, dkt3, dvt3, chips_rest = _attn_bwd(ka, v, kt3, qat3, qt3, dat3, lset3, dlt3, pair_rest)

    def sums_per_token(t3):
        rows = t3.reshape(-1, HEADS, VROWS, TQ)[:, :, HEAD_DIM, :]
        return jnp.pad(rows.transpose(0, 2, 1).reshape(-1, HEADS), ((0, 0), (0, LANES - HEADS)))

    drs, dcs = sums_per_token(dqt3), sums_per_token(dkt3)
    gx, dz, dg_mix_pre, db = _pre_attn_bwd(dqt3, dkt3, dvt3, dcs, drs, fl, dy, x, dh1, small["g_mix_pre"], wqkv, wf, wu)

    dwz = _wgrad(dz, hn, F32, "wgrad_in")
    dwin_t = jnp.concatenate([dwz[0:3 * D_ATTN], dwz[3 * D_ATTN:3 * D_ATTN + HEADS], dwz[3 * D_ATTN + LANES:]], axis=0)
    send_in = jnp.pad(dwin_t.reshape(nd, SHARD_IN, D_MODEL), ((0, 0), (0, ROWS_IN - SHARD_IN), (0, 0))).astype(BF16)
    pair_in = _rs_pair_sum(core, send_in, "rs_pair_sum_in")

    dwp = _wgrad(y, dmpb, F32, "wgrad_pool")
    dw_pool = jnp.stack([dwp[g * POOL_CH:(g + 1) * POOL_CH, g * POOL_CH:(g + 1) * POOL_CH] for g in range(4)])
    small_part = _pack_small(dw_pool, dg_mix_pre, dg_mix_post, dg_ffn_pre, dg_ffn_post, dg_ple, dg_attn, dg_pool, dps,
                             db[:, 0:HEADS], loss8[0:1, 0:1])

    *upd_rest, chips_in, small_all = _reduce_update_rest(chips_rest, rest_w, rest_m, rest_v, pair_in, small_part)
    upd_in = _reduce_update_big(chips_in, in_w, in_m, in_v, ROWS_IN, "reduce_update_in")
    return gx, small_all, upd_in, upd_rest


def kernel(x, p, g_mix_pre, w_in, b_forget, g_attn_grp, g_pool_grp, w_pool, pool_scale, w_out, g_mix_post, g_ffn_pre, w_ffn_gate, w_ffn_up, w_ffn_down, g_ffn_post, w_ple_proj, g_ple, w_ple_gate, loss_target, m_g_mix_pre, m_w_in, m_b_forget, m_g_attn_grp, m_g_pool_grp, m_w_pool, m_pool_scale, m_w_out, m_g_mix_post, m_g_ffn_pre, m_w_ffn_gate, m_w_ffn_up, m_w_ffn_down, m_g_ffn_post, m_w_ple_proj, m_g_ple, m_w_ple_gate, v_g_mix_pre, v_w_in, v_b_forget, v_g_attn_grp, v_g_pool_grp, v_w_pool, v_pool_scale, v_w_out, v_g_mix_post, v_g_ffn_pre, v_w_ffn_gate, v_w_ffn_up, v_w_ffn_down, v_g_ffn_post, v_w_ple_proj, v_g_ple, v_w_ple_gate):
    small = dict(w_pool=w_pool[0], g_mix_pre=g_mix_pre, g_mix_post=g_mix_post, g_ffn_pre=g_ffn_pre,
                 g_ffn_post=g_ffn_post, g_ple=g_ple, g_attn_grp=g_attn_grp, g_pool_grp=g_pool_grp,
                 pool_scale=pool_scale, b_forget=b_forget)
    gx, small_all, upd_in, upd_rest = _step(
        x[0], p[0, 0], loss_target[0], small, _pack_in(w_in), _pack_in(m_w_in), _pack_in(v_w_in),
        _pack_rest(w_out, w_ffn_gate, w_ffn_up, w_ffn_down, w_ple_proj, w_ple_gate),
        _pack_rest(m_w_out, m_w_ffn_gate, m_w_ffn_up, m_w_ffn_down, m_w_ple_proj, m_w_ple_gate),
        _pack_rest(v_w_out, v_w_ffn_gate, v_w_ffn_up, v_w_ffn_down, v_w_ple_proj, v_w_ple_gate))

    sm_w = _pack_small(w_pool, g_mix_pre, g_mix_post, g_ffn_pre, g_ffn_post, g_ple, g_attn_grp, g_pool_grp, pool_scale, b_forget)
    sm_m = _pack_small(m_w_pool, m_g_mix_pre, m_g_mix_post, m_g_ffn_pre, m_g_ffn_post, m_g_ple, m_g_attn_grp, m_g_pool_grp, m_pool_scale, m_b_forget)
    sm_v = _pack_small(v_w_pool, v_g_mix_pre, v_g_mix_post, v_g_ffn_pre, v_g_ffn_post, v_g_ple, v_g_attn_grp, v_g_pool_grp, v_pool_scale, v_b_forget)
    upd_small = _reduce_update_small(small_all, sm_w, sm_m, sm_v)
    loss = upd_small[0][ROW_LOSS, 0]

    def leaves(k):
        b_out, b_gate, b_up, b_down, b_ple, b_pg = _unpack_rest(upd_rest[k])
        s = _unpack_small(upd_small[k])
        return (s["g_mix_pre"], _unpack_in(upd_in[k]), s["b_forget"], s["g_attn_grp"], s["g_pool_grp"], s["w_pool"],
                s["pool_scale"], b_out, s["g_mix_post"], s["g_ffn_pre"], b_gate, b_up, b_down, s["g_ffn_post"], b_ple,
                s["g_ple"], b_pg)

    return (loss, gx[None], *leaves(0), *leaves(1), *leaves(2), *leaves(3))
```

```python
import functools

import jax
import jax.numpy as jnp
from jax import lax
from jax.experimental import pallas as pl
from jax.experimental.pallas import tpu as pltpu

F32 = jnp.float32
BF16 = jnp.bfloat16
HIGHEST = lax.Precision.HIGHEST

D_MODEL = 1024
HEADS = 8
HEAD_DIM = 64
D_ATTN = HEADS * HEAD_DIM
POOL_WINDOWS = (2, 4, 8, 16)
POOL_CH = 128
D_POOL = POOL_CH * len(POOL_WINDOWS)
D_FF = 2816
D_PLE = 256
D_IN = 3 * D_ATTN + HEADS + D_POOL
RMS_EPS = 1e-6
N_DEV = 8

ADAM_LR = 0.001
ADAM_B1 = 0.9
ADAM_B2 = 0.999
ADAM_EPS = 1e-08
ADAM_WD = 0.01
ADAM_STEP = 10

LANES = 128
HALO = 16
TS = 512
TS_FF = 512
TS_WGRAD = 1024
TM_WGRAD = 2176
TQ = 256
TN_FF = 1408
NEG = -1e30
VMEM_LIMIT = 56 * 1024 * 1024

SHARD_IN = 257
ROWS_IN = 272
SHARD_FF = 352
OFF_PG = 128
OFF_PLE = 256
OFF_GATE = SHARD_FF
OFF_UP = 2 * SHARD_FF
OFF_DOWN = 3 * SHARD_FF
ROWS_REST = 4 * SHARD_FF
TR_REST = SHARD_FF

SMALL_ROWS = 80
ROW_G_MIX_PRE, ROW_G_MIX_POST, ROW_G_FFN_PRE, ROW_G_FFN_POST, ROW_G_PLE = 64, 65, 66, 67, 68
ROW_G_ATTN, ROW_G_POOL, ROW_POOL_SCALE, ROW_B_FORGET, ROW_LOSS = 69, 70, 71, 72, 73


def _nn(a, b):
    return jnp.dot(a, b, preferred_element_type=F32)


def _nt(a, b):
    return lax.dot_general(a, b, (((1,), (1,)), ((), ())), preferred_element_type=F32)


def _tn(a, b):
    return lax.dot_general(a, b, (((0,), (0,)), ((), ())), preferred_element_type=F32)


def _rstd(v):
    return lax.rsqrt(jnp.mean(v * v, axis=-1, keepdims=True) + RMS_EPS)


def _rms_bwd(v, g, dy):
    r = _rstd(v)
    vh = v * r
    t = dy * g
    dv = r * (t - vh * jnp.mean(t * vh, axis=-1, keepdims=True))
    return dv, jnp.sum(dy * vh, axis=0, keepdims=True)


def _split3(v):
    hi = v.astype(BF16)
    rest = v - hi.astype(F32)
    mid = rest.astype(BF16)
    return hi, mid, (rest - mid.astype(F32)).astype(BF16)


def _mask_matmul(mask, v):
    hi, mid, lo = _split3(v)
    return _nn(mask, lo) + _nn(mask, mid) + _nn(mask, hi)


def _params(n_grid):
    return pltpu.CompilerParams(dimension_semantics=("arbitrary",) * n_grid, vmem_limit_bytes=VMEM_LIMIT)


def _row(i):
    return (i, 0)


def _fixed(*_):
    return (0, 0)


def _spec_square(part):
    return pl.BlockSpec((N_DEV, 128, D_MODEL), lambda *_: (0, part, 0))


def _spec_ff(part):
    return pl.BlockSpec((TN_FF // SHARD_FF, SHARD_FF, D_MODEL), lambda i, j: (j, part, 0))


assert TS == 2 * TQ and TN_FF % SHARD_FF == 0
_HALVES = (slice(0, TQ), slice(TQ, TS))

VMEM_WHOLE = pl.BlockSpec(memory_space=pltpu.VMEM)
SMEM_WHOLE = pl.BlockSpec(memory_space=pltpu.SMEM)
ANY = pl.BlockSpec(memory_space=pl.ANY)


LOG2E = 1.4426950408889634
VROWS = HEAD_DIM + 16
AUG = 128
BIAS_LANE = HEAD_DIM
ONE_LANE = HEAD_DIM + 3
SPARE_LANE = HEADS


def _attn_layout_constants():
    import numpy as np
    place = np.zeros((D_ATTN, HEADS * AUG), np.float32)
    for r in range(D_ATTN):
        place[r, (r // HEAD_DIM) * AUG + r % HEAD_DIM] = 1.0
    bias_k = np.zeros((3, LANES, HEADS * AUG), np.float32)
    bias_q = np.zeros((3, LANES, HEADS * AUG), np.float32)
    for h in range(HEADS):
        for part in range(3):
            bias_k[part, h, h * AUG + BIAS_LANE + part] = -1.0
            bias_q[part, h, h * AUG + ONE_LANE + part] = 1.0
            bias_k[0, SPARE_LANE, h * AUG + ONE_LANE + part] = 1.0
            bias_q[0, SPARE_LANE, h * AUG + BIAS_LANE + part] = 1.0
    as_bf = lambda a: jnp.asarray(a, BF16)
    return dict(place=as_bf(place), place_t=as_bf(place.T), bias_k=as_bf(bias_k),
                bias_q_t=as_bf(bias_q.transpose(0, 2, 1)))


def _pre_attn_fwd(x, g1, wqkv, wf, wu, bpad, wpool, lay, own_block):
    s, d = x.shape
    nt = s // TS
    sub = TS // TQ

    def body(x_ref, g_ref, wqkv_ref, wf_ref, wu_ref, b_ref, wp_ref, place_ref, place_t_ref, bk_ref, bqt_ref, own_ref,
             hn_ref, qt_ref, ka_ref, v_ref, qat_ref, vt_ref, kt_ref, fl_ref, y_ref, mp_ref, all_ref,
             ubuf, ccar, cbuf, stage, send_sems, recv_sems, local_sem):
        i = pl.program_id(0)

        @pl.when(i == 0)
        def _():
            _gather_start(own_ref, all_ref, stage, send_sems, recv_sems, local_sem)
            ubuf[0:HALO, :] = jnp.zeros((HALO, D_POOL), F32)
            ccar[...] = jnp.zeros_like(ccar)

        @pl.when(i == max(nt - 2, 0))
        def _():
            _gather_pass_on(all_ref, send_sems, recv_sems)

        xv = x_ref[...]
        hn = (xv * _rstd(xv) * g_ref[...]).astype(BF16)
        hn_ref[...] = hn
        zq = _nt(hn, wqkv_ref[...])
        qt = (zq[:, 0:D_ATTN] * 0.125).astype(BF16).T
        qb = (zq[:, 0:D_ATTN] * (0.125 * LOG2E)).astype(BF16)
        kb = zq[:, D_ATTN:2 * D_ATTN].astype(BF16)
        vb = zq[:, 2 * D_ATTN:3 * D_ATTN].astype(BF16)
        v_ref[...] = vb

        fl = _nt(hn, wf_ref[...]) + b_ref[...]
        fl_ref[...] = fl
        logf = jax.nn.log_sigmoid(fl)
        rr = lax.broadcasted_iota(jnp.int32, (TS, TS), 0)
        cc = lax.broadcasted_iota(jnp.int32, (TS, TS), 1)
        c = _mask_matmul((cc <= rr).astype(BF16), logf) + ccar[...]
        cbuf[...] = c
        ccar[...] = cbuf[TS - 1:TS, :]
        hi, mid, lo = _split3(c * LOG2E)
        lane = lax.broadcasted_iota(jnp.int32, (TS, LANES), 1)
        parts = (jnp.where(lane == SPARE_LANE, 1.0, hi).astype(BF16), mid, lo)
        ka = _nn(kb, place_ref[...])
        qat = _nt(place_t_ref[...], qb)
        for part in range(3):
            ka = ka + _nn(parts[part], bk_ref[part])
            qat = qat + _nt(bqt_ref[part], parts[part])
        ka_ref[...] = ka.astype(BF16)
        qat = qat.astype(BF16)
        vt = vb.T
        kt = kb.T
        for a in range(sub):
            cols = slice(a * TQ, (a + 1) * TQ)
            qat_ref[a] = qat[:, cols]
            for ref, mat in ((qt_ref, qt), (kt_ref, kt), (vt_ref, vt)):
                for h in range(HEADS):
                    ref[a, h * VROWS:h * VROWS + HEAD_DIM, :] = mat[h * HEAD_DIM:(h + 1) * HEAD_DIM, cols]
                    ref[a, h * VROWS + HEAD_DIM:(h + 1) * VROWS, :] = jnp.ones((VROWS - HEAD_DIM, TQ), BF16)

        u = _nt(hn, wu_ref[...])
        ubuf[HALO:HALO + TS, :] = u
        t = i * TS + lax.broadcasted_iota(jnp.int32, (TS, 1), 0)
        for g, w in enumerate(POOL_WINDOWS):
            cols = slice(g * POOL_CH, (g + 1) * POOL_CH)
            sm = ubuf[:, cols]
            step = 1
            while step < w:
                sm = sm + pltpu.roll(sm, step, 0)
                step *= 2
            cnt = jnp.minimum(t + 1, w).astype(F32)
            yg = (sm[HALO:, :] / cnt - u[:, cols]).astype(BF16)
            y_ref[:, cols] = yg
            mp_ref[:, cols] = _nn(yg, wp_ref[g])
        ubuf[0:HALO, :] = u[TS - HALO:, :]

        @pl.when(i == nt - 1)
        def _():
            _gather_finish(own_ref, all_ref, send_sems, recv_sems)

    nq = s // TQ
    aug = HEADS * AUG
    outs = (
        jax.ShapeDtypeStruct((s, d), BF16), jax.ShapeDtypeStruct((nq, HEADS * VROWS, TQ), BF16),
        jax.ShapeDtypeStruct((s, aug), BF16), jax.ShapeDtypeStruct((s, D_ATTN), BF16),
        jax.ShapeDtypeStruct((nq, aug, TQ), BF16), jax.ShapeDtypeStruct((nq, HEADS * VROWS, TQ), BF16),
        jax.ShapeDtypeStruct((nq, HEADS * VROWS, TQ), BF16),
        jax.ShapeDtypeStruct((s, LANES), F32),
        jax.ShapeDtypeStruct((s, D_POOL), BF16), jax.ShapeDtypeStruct((s, D_POOL), F32),
        jax.ShapeDtypeStruct((N_DEV,) + own_block.shape, own_block.dtype),
    )
    fixed3 = lambda i: (0, 0, 0)
    tiles3 = lambda rows: pl.BlockSpec((sub, rows, TQ), lambda i: (i, 0, 0))
    return pl.pallas_call(
        body, grid=(nt,), out_shape=outs, name="pre_attn_fwd",
        in_specs=[pl.BlockSpec((TS, d), _row), pl.BlockSpec((1, d), _fixed),
                  pl.BlockSpec((3 * D_ATTN, d), _fixed), pl.BlockSpec(wf.shape, _fixed), pl.BlockSpec(wu.shape, _fixed),
                  pl.BlockSpec((1, LANES), _fixed), pl.BlockSpec(wpool.shape, fixed3),
                  pl.BlockSpec(lay["place"].shape, _fixed), pl.BlockSpec(lay["place_t"].shape, _fixed),
                  pl.BlockSpec(lay["bias_k"].shape, fixed3), pl.BlockSpec(lay["bias_q_t"].shape, fixed3), ANY],
        out_specs=(pl.BlockSpec((TS, d), _row), tiles3(HEADS * VROWS),
                   pl.BlockSpec((TS, aug), _row), pl.BlockSpec((TS, D_ATTN), _row),
                   tiles3(aug), tiles3(HEADS * VROWS), tiles3(HEADS * VROWS),
                   pl.BlockSpec((TS, LANES), _row),
                   pl.BlockSpec((TS, D_POOL), _row), pl.BlockSpec((TS, D_POOL), _row), ANY),
        scratch_shapes=[pltpu.VMEM((TS + HALO, D_POOL), F32), pltpu.VMEM((1, LANES), F32), pltpu.VMEM((TS, LANES), F32),
                        pltpu.VMEM(own_block.shape, own_block.dtype),
                        pltpu.SemaphoreType.DMA((7,)), pltpu.SemaphoreType.DMA((7,)), pltpu.SemaphoreType.DMA],
        compiler_params=_params(1),
    )(x, g1, wqkv, wf, wu, bpad, wpool, lay["place"], lay["place_t"], lay["bias_k"], lay["bias_q_t"], own_block)


def _causal_in_tile():
    krow = lax.broadcasted_iota(jnp.int32, (TQ, TQ), 0)
    qcol = lax.broadcasted_iota(jnp.int32, (TQ, TQ), 1)
    return krow <= qcol


def _attn_fwd(ka, qat3, vt3, own_block):
    s = ka.shape[0]
    nq = s // TQ
    pass_on_step = max(nq - 2, 0)

    def body(qa_ref, ka_ref, vt_ref, own_ref, a_ref, lset_ref, all_ref, acc, out_t, st_scr, pt_scr,
             stage, send_sems, recv_sems, local_sem):
        i = pl.program_id(0)

        @pl.when(i == 0)
        def _():
            _gather_start(own_ref, all_ref, stage, send_sems, recv_sems, local_sem)

        @pl.when(i == pass_on_step)
        def _():
            _gather_pass_on(all_ref, send_sems, recv_sems)

        acc[...] = jnp.zeros_like(acc)

        def tile(j, stats, masked):
            tile_max = []
            for h in range(HEADS):
                aug = slice(h * AUG, (h + 1) * AUG)
                st = _nn(ka_ref[pl.ds(j * TQ, TQ), aug], qa_ref[0, aug, :])
                if masked:
                    st = jnp.where(_causal_in_tile(), st, NEG)
                st_scr[h] = st
                tile_max.append(jnp.max(st, axis=0, keepdims=True))
            new, scale = [], []
            for h in range(HEADS):
                m_new = jnp.maximum(stats[h], tile_max[h])
                scale.append(jnp.exp2(stats[h] - m_new))
                pt_scr[h] = jnp.exp2(st_scr[h] - m_new).astype(BF16)
                new.append(m_new)
            for h in range(HEADS):
                rows = slice(h * VROWS, (h + 1) * VROWS)
                acc[rows, :] = scale[h] * acc[rows, :] + _nn(vt_ref[j, rows, :], pt_scr[h])
            return tuple(new)

        init = tuple(jnp.full((1, TQ), NEG, F32) for _ in range(HEADS))
        stats = lax.fori_loop(0, i, functools.partial(tile, masked=False), init)
        stats = tile(i, stats, True)
        for h in range(HEADS):
            denom = acc[h * VROWS + HEAD_DIM:h * VROWS + HEAD_DIM + 1, :]
            out_t[h * HEAD_DIM:(h + 1) * HEAD_DIM, :] = acc[h * VROWS:h * VROWS + HEAD_DIM, :] / denom
            lset_ref[0, h:h + 1, :] = stats[h] + jnp.log2(denom)
        a_ref[...] = out_t[...].T

        @pl.when(i == nq - 1)
        def _():
            _gather_finish(own_ref, all_ref, send_sems, recv_sems)

    r, cdim = own_block.shape
    return pl.pallas_call(
        body, grid=(nq,), name="attn_fwd",
        out_shape=(jax.ShapeDtypeStruct((s, D_ATTN), F32), jax.ShapeDtypeStruct((nq, HEADS, TQ), F32),
                   jax.ShapeDtypeStruct((N_DEV, r, cdim), own_block.dtype)),
        in_specs=[pl.BlockSpec((1, HEADS * AUG, TQ), lambda i: (i, 0, 0)), VMEM_WHOLE, VMEM_WHOLE, ANY],
        out_specs=(pl.BlockSpec((TQ, D_ATTN), _row), pl.BlockSpec((1, HEADS, TQ), lambda i: (i, 0, 0)), ANY),
        scratch_shapes=[pltpu.VMEM((HEADS * VROWS, TQ), F32), pltpu.VMEM((D_ATTN, TQ), F32),
                        pltpu.VMEM((HEADS, TQ, TQ), F32), pltpu.VMEM((HEADS, TQ, TQ), BF16),
                        pltpu.VMEM((r, cdim), own_block.dtype),
                        pltpu.SemaphoreType.DMA((7,)), pltpu.SemaphoreType.DMA((7,)), pltpu.SemaphoreType.DMA],
        compiler_params=_params(1),
    )(qat3, ka, vt3, own_block)


def _post_attn_fwd(a, mpre, x, g_attn, g_pool, pscale, wout, g_post, g_ffn_pre):
    s, d = x.shape

    def body(a_ref, mp_ref, x_ref, ga_ref, gp_ref, ps_ref, wo_ref, gpost_ref, gpre_ref,
             mix_ref, o_ref, h1_ref, hn2_ref):
        for rows in _HALVES:
            av = a_ref[rows, :]
            mix_ref[rows, 0:D_ATTN] = (av * _rstd(av) * ga_ref[...]).astype(BF16)
            mv = mp_ref[rows, :] * ps_ref[...]
            mix_ref[rows, D_ATTN:] = (mv * _rstd(mv) * gp_ref[...]).astype(BF16)
            o = _nn(mix_ref[rows, :], wo_ref[...].reshape(d, d))
            o_ref[rows, :] = o
            h1 = x_ref[rows, :] + o * _rstd(o) * gpost_ref[...]
            h1_ref[rows, :] = h1
            hn2_ref[rows, :] = (h1 * _rstd(h1) * gpre_ref[...]).astype(BF16)

    vec = lambda n: pl.BlockSpec((1, n), _fixed)
    return pl.pallas_call(
        body, grid=(s // TS,), name="post_attn_fwd",
        out_shape=(jax.ShapeDtypeStruct((s, d), BF16), jax.ShapeDtypeStruct((s, d), F32),
                   jax.ShapeDtypeStruct((s, d), F32), jax.ShapeDtypeStruct((s, d), BF16)),
        in_specs=[pl.BlockSpec((TS, D_ATTN), _row), pl.BlockSpec((TS, D_POOL), _row), pl.BlockSpec((TS, d), _row),
                  vec(D_ATTN), vec(D_POOL), vec(D_POOL), _spec_square(0), vec(d), vec(d)],
        out_specs=(pl.BlockSpec((TS, d), _row),) * 4,
        compiler_params=_params(1),
    )(a, mpre, x, g_attn, g_pool, pscale, wout, g_post, g_ffn_pre)


def _ffn_fwd(hn2, wg, wu, wd, h1, g_post):
    s, d = h1.shape
    nc = D_FF // TN_FF
    ts = min(TS_FF, s)

    def body(hn_ref, wg_ref, wu_ref, wd_ref, h1_ref, g_ref, gate_ref, up_ref, act_ref, ff_ref, h2_ref, acc):
        j = pl.program_id(1)

        @pl.when(j == 0)
        def _():
            acc[...] = jnp.zeros_like(acc)

        for r in range(2):
            rows = slice(r * (ts // 2), (r + 1) * (ts // 2))
            hn = hn_ref[rows, :]
            gt = _nt(hn, wg_ref[...].reshape(TN_FF, d))
            up = _nt(hn, wu_ref[...].reshape(TN_FF, d))
            gate_ref[rows, :] = gt.astype(BF16)
            up_ref[rows, :] = up.astype(BF16)
            act_ref[rows, :] = (gt * jax.nn.sigmoid(gt) * up).astype(BF16)
            acc[rows, :] += _nn(act_ref[rows, :], wd_ref[...].reshape(TN_FF, d))

        @pl.when(j == nc - 1)
        def _():
            ff = acc[...]
            ff_ref[...] = ff
            h2_ref[...] = h1_ref[...] + ff * _rstd(ff) * g_ref[...]

    rowblk = pl.BlockSpec((ts, d), lambda i, j: (i, 0))
    chunk = pl.BlockSpec((ts, TN_FF), lambda i, j: (i, j))
    return pl.pallas_call(
        body, grid=(s // ts, nc), name="ffn_fwd",
        out_shape=(jax.ShapeDtypeStruct((s, D_FF), BF16),) * 3 + (jax.ShapeDtypeStruct((s, d), F32),) * 2,
        in_specs=[rowblk, _spec_ff(0), _spec_ff(1), _spec_ff(2), rowblk, pl.BlockSpec((1, d), lambda i, j: (0, 0))],
        out_specs=(chunk, chunk, chunk, rowblk, rowblk),
        scratch_shapes=[pltpu.VMEM((ts, d), F32)],
        compiler_params=_params(2),
    )(hn2, wg, wu, wd, h1, g_post)


def _tail_fwd_bwd(h2, p, tgt, ff, wple, wpg, g_ple, g_ffn_post):
    s, d = h2.shape

    def body(h2_ref, p_ref, t_ref, ff_ref, wple_ref, wpg_ref, gple_ref, gfp_ref,
             dh2_ref, dff_ref, dgl_ref, dpp_ref, h2b_ref, pb_ref, loss_ref, dgple_ref, dgfp_ref):
        i = pl.program_id(0)

        @pl.when(i == 0)
        def _():
            loss_ref[...] = jnp.zeros_like(loss_ref)
            dgple_ref[...] = jnp.zeros_like(dgple_ref)
            dgfp_ref[...] = jnp.zeros_like(dgfp_ref)

        h2 = h2_ref[...]
        h2b = h2.astype(BF16)
        h2b_ref[...] = h2b
        pb = p_ref[...].astype(BF16)
        pb_ref[...] = pb
        pp = _nt(pb, wple_ref[...])
        gple = gple_ref[...]
        e = pp * _rstd(pp) * gple
        wpg = wpg_ref[...].reshape(d, d)
        sg = jax.nn.sigmoid(_nn(h2b, wpg))
        diff = h2 + sg * e - t_ref[...]
        sq = jnp.sum(jnp.sum(diff * diff, axis=1, keepdims=True), axis=0, keepdims=True)
        loss_ref[...] += jnp.broadcast_to(sq * (0.5 / d), loss_ref.shape)
        dh3 = diff * (1.0 / d)
        dgl = (dh3 * e * sg * (1.0 - sg)).astype(BF16)
        dgl_ref[...] = dgl
        dh2 = dh3 + _nt(dgl, wpg)
        dh2_ref[...] = dh2
        dpp, dg = _rms_bwd(pp, gple, dh3 * sg)
        dpp_ref[...] = dpp.astype(BF16)
        dgple_ref[...] += dg
        dff, dg = _rms_bwd(ff_ref[...], gfp_ref[...], dh2)
        dff_ref[...] = dff.astype(BF16)
        dgfp_ref[...] += dg

    rowblk = pl.BlockSpec((TS, d), _row)
    vec = pl.BlockSpec((1, d), _fixed)
    return pl.pallas_call(
        body, grid=(s // TS,), name="tail_fwd_bwd",
        out_shape=(jax.ShapeDtypeStruct((s, d), F32), jax.ShapeDtypeStruct((s, d), BF16),
                   jax.ShapeDtypeStruct((s, d), BF16), jax.ShapeDtypeStruct((s, d), BF16),
                   jax.ShapeDtypeStruct((s, d), BF16), jax.ShapeDtypeStruct((s, D_PLE), BF16),
                   jax.ShapeDtypeStruct((8, LANES), F32), jax.ShapeDtypeStruct((1, d), F32),
                   jax.ShapeDtypeStruct((1, d), F32)),
        in_specs=[rowblk, pl.BlockSpec((TS, D_PLE), _row), rowblk, rowblk,
                  pl.BlockSpec(wple.shape, _fixed), _spec_square(1), vec, vec],
        out_specs=(rowblk, rowblk, rowblk, rowblk, rowblk, pl.BlockSpec((TS, D_PLE), _row),
                   pl.BlockSpec((8, LANES), _fixed), vec, vec),
        compiler_params=_params(1),
    )(h2, p, tgt, ff, wple, wpg, g_ple, g_ffn_post)


def _ffn_bwd(dff, gate, up, wd, wg, wu, h1, dh2, g_pre):
    s, d = h1.shape
    nc = D_FF // TN_FF
    ts = min(TS_FF, s)

    def body(dff_ref, gate_ref, up_ref, wd_ref, wg_ref, wu_ref, h1_ref, dh2_ref, g_ref,
             dgate_ref, dup_ref, dh1_ref, dg_ref, acc):
        i = pl.program_id(0)
        j = pl.program_id(1)

        @pl.when((i == 0) & (j == 0))
        def _():
            dg_ref[...] = jnp.zeros_like(dg_ref)

        @pl.when(j == 0)
        def _():
            acc[...] = jnp.zeros_like(acc)

        for r in range(2):
            rows = slice(r * (ts // 2), (r + 1) * (ts // 2))
            dact = _nt(dff_ref[rows, :], wd_ref[...].reshape(TN_FF, d))
            gt = gate_ref[rows, :].astype(F32)
            sg = jax.nn.sigmoid(gt)
            dup_ref[rows, :] = (dact * gt * sg).astype(BF16)
            dgate_ref[rows, :] = (dact * up_ref[rows, :].astype(F32) * (sg * (1.0 + gt * (1.0 - sg)))).astype(BF16)
            acc[rows, :] += (_nn(dgate_ref[rows, :], wg_ref[...].reshape(TN_FF, d))
                             + _nn(dup_ref[rows, :], wu_ref[...].reshape(TN_FF, d)))

        @pl.when(j == nc - 1)
        def _():
            dv, dg = _rms_bwd(h1_ref[...], g_ref[...], acc[...])
            dh1_ref[...] = dh2_ref[...] + dv
            dg_ref[...] += dg

    rowblk = pl.BlockSpec((ts, d), lambda i, j: (i, 0))
    chunk = pl.BlockSpec((ts, TN_FF), lambda i, j: (i, j))
    vec = pl.BlockSpec((1, d), lambda i, j: (0, 0))
    return pl.pallas_call(
        body, grid=(s // ts, nc), name="ffn_bwd",
        out_shape=(jax.ShapeDtypeStruct((s, D_FF), BF16), jax.ShapeDtypeStruct((s, D_FF), BF16),
                   jax.ShapeDtypeStruct((s, d), F32), jax.ShapeDtypeStruct((1, d), F32)),
        in_specs=[rowblk, chunk, chunk, _spec_ff(2), _spec_ff(0), _spec_ff(1), rowblk, rowblk, vec],
        out_specs=(chunk, chunk, rowblk, vec),
        scratch_shapes=[pltpu.VMEM((ts, d), F32)],
        compiler_params=_params(2),
    )(dff, gate, up, wd, wg, wu, h1, dh2, g_pre)


def _post_attn_bwd(dh1, o, a, mpre, wout, wpool, g_post, g_attn, g_pool, pscale):
    s, d = dh1.shape
    sub = TS // TQ

    def body(dh1_ref, o_ref, a_ref, mp_ref, wo_ref, wp_ref, gpost_ref, ga_ref, gp_ref, ps_ref,
             dob_ref, dat_ref, dlt_ref, dmpb_ref, dy_ref, dgpost_ref, dga_ref, dgp_ref, dps_ref):
        i = pl.program_id(0)

        @pl.when(i == 0)
        def _():
            dgpost_ref[...] = jnp.zeros_like(dgpost_ref)
            dga_ref[...] = jnp.zeros_like(dga_ref)
            dgp_ref[...] = jnp.zeros_like(dgp_ref)
            dps_ref[...] = jnp.zeros_like(dps_ref)

        do, dg = _rms_bwd(o_ref[...], gpost_ref[...], dh1_ref[...])
        dgpost_ref[...] += dg
        dob = do.astype(BF16)
        dob_ref[...] = dob
        dmix = _nt(dob, wo_ref[...].reshape(d, d))

        av = a_ref[...]
        da, dg = _rms_bwd(av, ga_ref[...], dmix[:, 0:D_ATTN])
        dga_ref[...] += dg
        dat = da.astype(BF16).T
        hsel = (lax.shift_right_logical(lax.broadcasted_iota(jnp.int32, (HEADS, D_ATTN), 1), 6)
                == lax.broadcasted_iota(jnp.int32, (HEADS, D_ATTN), 0)).astype(F32)
        dlt = lax.dot_general(hsel, da * av, (((1,), (1,)), ((), ())), precision=HIGHEST, preferred_element_type=F32)
        for q in range(sub):
            dlt_ref[q] = dlt[:, q * TQ:(q + 1) * TQ]
            dat_ref[q] = dat[:, q * TQ:(q + 1) * TQ]

        ps = ps_ref[...]
        mp = mp_ref[...]
        dm, dg = _rms_bwd(mp * ps, gp_ref[...], dmix[:, D_ATTN:])
        dgp_ref[...] += dg
        dps_ref[...] += jnp.sum(dm * mp, axis=0, keepdims=True)
        dmpb = (dm * ps).astype(BF16)
        dmpb_ref[...] = dmpb
        for g in range(len(POOL_WINDOWS)):
            cols = slice(g * POOL_CH, (g + 1) * POOL_CH)
            dy_ref[:, cols] = _nt(dmpb[:, cols], wp_ref[g])

    rowblk = pl.BlockSpec((TS, d), _row)
    half = pl.BlockSpec((TS, D_ATTN), _row)
    vec = lambda n: pl.BlockSpec((1, n), _fixed)
    return pl.pallas_call(
        body, grid=(s // TS,), name="post_attn_bwd",
        out_shape=(jax.ShapeDtypeStruct((s, d), BF16), jax.ShapeDtypeStruct((s // TQ, D_ATTN, TQ), BF16),
                   jax.ShapeDtypeStruct((s // TQ, HEADS, TQ), F32), jax.ShapeDtypeStruct((s, D_POOL), BF16),
                   jax.ShapeDtypeStruct((s, D_POOL), F32), jax.ShapeDtypeStruct((1, d), F32),
                   jax.ShapeDtypeStruct((1, D_ATTN), F32), jax.ShapeDtypeStruct((1, D_POOL), F32),
                   jax.ShapeDtypeStruct((1, D_POOL), F32)),
        in_specs=[rowblk, rowblk, half, half, _spec_square(0),
                  pl.BlockSpec(wpool.shape, lambda i: (0, 0, 0)), vec(d), vec(D_ATTN), vec(D_POOL), vec(D_POOL)],
        out_specs=(rowblk, pl.BlockSpec((sub, D_ATTN, TQ), lambda i: (i, 0, 0)),
                   pl.BlockSpec((sub, HEADS, TQ), lambda i: (i, 0, 0)), half, half,
                   vec(d), vec(D_ATTN), vec(D_POOL), vec(D_POOL)),
        compiler_params=_params(1),
    )(dh1, o, a, mpre, wout, wpool, g_post, g_attn, g_pool, pscale)


def _attn_bwd(ka, v, kt3, qat3, qt3, dot3, lset3, dlt3, chip_blocks):
    s = ka.shape[0]
    nq = s // TQ

    def body(ka_ref, v_ref, kt_ref, qat_ref, qt_ref, dot_ref, lset_ref, dlt_ref, b_ref,
             dqt_ref, dkt_ref, dvt_ref, got_ref, pt_scr, ptb_scr, dsb_scr,
             stage, send_sems, recv_sems, local_sem):
        j = pl.program_id(0)

        @pl.when(j == 0)
        def _():
            _chips_start(b_ref, got_ref, stage, send_sems, recv_sems, local_sem)
            dqt_ref[...] = jnp.zeros_like(dqt_ref)

        def tile(i, masked):
            def accumulate(ref, idx, val):
                if masked:
                    ref[idx] = val
                else:
                    ref[idx] += val

            for h in range(HEADS):
                aug = slice(h * AUG, (h + 1) * AUG)
                st = _nn(ka_ref[:, aug], qat_ref[i, aug, :]) - lset_ref[i, h:h + 1, :]
                if masked:
                    st = jnp.where(_causal_in_tile(), st, NEG)
                pt = jnp.exp2(st)
                pt_scr[h] = pt
                ptb_scr[h] = pt.astype(BF16)
            heads = [(h, slice(h * HEAD_DIM, (h + 1) * HEAD_DIM)) for h in range(HEADS)]
            for h, hs in heads:
                dst = pt_scr[h] * (_nn(v_ref[:, hs], dot_ref[i, hs, :]) - dlt_ref[i, h:h + 1, :])
                dsb_scr[h] = dst.astype(BF16)
            for h, hs in heads:
                accumulate(dvt_ref, (0, hs, slice(None)), _nt(dot_ref[i, hs, :], ptb_scr[h]))
            for h, hs in heads:
                rows = slice(h * VROWS, (h + 1) * VROWS)
                accumulate(dkt_ref, (0, rows, slice(None)), _nt(qt_ref[i, rows, :], dsb_scr[h]))
            for h, hs in heads:
                rows = slice(h * VROWS, (h + 1) * VROWS)
                dqt_ref[i, rows, :] += _nn(kt_ref[0, rows, :], dsb_scr[h])

        first = j + 1
        pairs = (nq - first) // 2

        def step(p, carry):
            tile(first + 2 * p, False)
            tile(first + 2 * p + 1, False)
            return carry

        tile(j, True)
        lax.fori_loop(0, pairs, step, 0)

        @pl.when(first + 2 * pairs < nq)
        def _():
            tile(nq - 1, False)

        @pl.when(j == nq - 1)
        def _():
            _chips_finish(b_ref, got_ref, send_sems, recv_sems)

    blk = pl.BlockSpec((TQ, D_ATTN), _row)
    tile_t = lambda rows: pl.BlockSpec((1, rows, TQ), lambda j: (j, 0, 0))
    per_tile = lambda rows: jax.ShapeDtypeStruct((nq, rows, TQ), F32)
    _, r, cdim = chip_blocks.shape
    return pl.pallas_call(
        body, grid=(nq,), name="attn_bwd",
        out_shape=(per_tile(HEADS * VROWS), per_tile(HEADS * VROWS), per_tile(D_ATTN),
                   jax.ShapeDtypeStruct(chip_blocks.shape, chip_blocks.dtype)),
        in_specs=[pl.BlockSpec((TQ, HEADS * AUG), _row), blk, tile_t(HEADS * VROWS),
                  VMEM_WHOLE, VMEM_WHOLE, VMEM_WHOLE, VMEM_WHOLE, VMEM_WHOLE, ANY],
        out_specs=(pl.BlockSpec((nq, HEADS * VROWS, TQ), lambda j: (0, 0, 0)), tile_t(HEADS * VROWS), tile_t(D_ATTN),
                   ANY),
        scratch_shapes=[pltpu.VMEM((HEADS, TQ, TQ), F32), pltpu.VMEM((HEADS, TQ, TQ), BF16),
                        pltpu.VMEM((HEADS, TQ, TQ), BF16), pltpu.VMEM((r, cdim), chip_blocks.dtype),
                        pltpu.SemaphoreType.DMA((3,)), pltpu.SemaphoreType.DMA((3,)), pltpu.SemaphoreType.DMA],
        compiler_params=_params(1),
    )(ka, v, kt3, qat3, qt3, dot3, lset3, dlt3, chip_blocks)


def _pre_attn_bwd(dqt3, dkt3, dvt3, dcs, drs, fl, dy, x, dh1, g1, wqkv, wf, wu):
    s, d = x.shape
    nt = s // TS
    n = TS + HALO
    sub = TS // TQ
    qkv, fcols = 3 * D_ATTN, 3 * D_ATTN + LANES

    def body(dqt_ref, dkt_ref, dvt_ref, dcs_ref, drs_ref, fl_ref, dy_ref, x_ref, dh1_ref, g_ref, wqkv_ref, wf_ref, wu_ref,
             gx_ref, dz_ref, dg_ref, db_ref, ybuf, ccar, dlog):
        dqkv_ref = dz_ref.at[:, 0:qkv]
        dfb_ref = dz_ref.at[:, qkv:fcols]
        dub_ref = dz_ref.at[:, fcols:]
        i = pl.program_id(0)
        ti = nt - 1 - i

        @pl.when(i == 0)
        def _():
            ybuf[TS:n, :] = jnp.zeros((HALO, D_POOL), F32)
            ccar[...] = jnp.zeros_like(ccar)
            dg_ref[...] = jnp.zeros_like(dg_ref)
            db_ref[...] = jnp.zeros_like(db_ref)

        rr = lax.broadcasted_iota(jnp.int32, (TS, TS), 0)
        cc = lax.broadcasted_iota(jnp.int32, (TS, TS), 1)
        dlog[...] = ccar[...] + _mask_matmul((cc >= rr).astype(BF16), drs_ref[...] - dcs_ref[...])
        ccar[...] = dlog[0:1, :]
        df = dlog[...] * jax.nn.sigmoid(-fl_ref[...])
        db_ref[...] += jnp.sum(df, axis=0, keepdims=True)
        dfb = df.astype(BF16)
        dfb_ref[...] = dfb

        t = ti * TS + lax.broadcasted_iota(jnp.int32, (TS, 1), 0)
        dy = dy_ref[...]
        for g, w in enumerate(POOL_WINDOWS):
            cols = slice(g * POOL_CH, (g + 1) * POOL_CH)
            ybuf[0:TS, cols] = dy[:, cols] / jnp.minimum(t + 1, w).astype(F32)
        for g, w in enumerate(POOL_WINDOWS):
            cols = slice(g * POOL_CH, (g + 1) * POOL_CH)
            sm = ybuf[:, cols]
            step = 1
            while step < w:
                sm = sm + pltpu.roll(sm, n - step, 0)
                step *= 2
            dub_ref[:, cols] = (sm[0:TS, :] - dy[:, cols]).astype(BF16)
        ybuf[TS:n, :] = ybuf[0:HALO, :]

        for a in range(sub):
            rows = slice(a * TQ, (a + 1) * TQ)
            for h in range(HEADS):
                src = slice(h * VROWS, h * VROWS + HEAD_DIM)
                dqkv_ref[rows, h * HEAD_DIM:(h + 1) * HEAD_DIM] = (dqt_ref[a, src, :].T * 0.125).astype(BF16)
                dqkv_ref[rows, D_ATTN + h * HEAD_DIM:D_ATTN + (h + 1) * HEAD_DIM] = dkt_ref[a, src, :].T.astype(BF16)
            dqkv_ref[rows, 2 * D_ATTN:] = dvt_ref[a].T.astype(BF16)
        dhn = _nn(dqkv_ref[...], wqkv_ref[...]) + _nn(dfb, wf_ref[...]) + _nn(dub_ref[...], wu_ref[...])
        dx, dg = _rms_bwd(x_ref[...], g_ref[...], dhn)
        gx_ref[...] = dh1_ref[...] + dx
        dg_ref[...] += dg

    rev = lambda i: (nt - 1 - i, 0)
    blk = lambda w: pl.BlockSpec((TS, w), rev)
    return pl.pallas_call(
        body, grid=(nt,), name="pre_attn_bwd",
        out_shape=(jax.ShapeDtypeStruct((s, d), F32), jax.ShapeDtypeStruct((s, fcols + D_POOL), BF16),
                   jax.ShapeDtypeStruct((1, d), F32), jax.ShapeDtypeStruct((1, LANES), F32)),
        in_specs=[pl.BlockSpec((sub, HEADS * VROWS, TQ), lambda i: (nt - 1 - i, 0, 0)),
                  pl.BlockSpec((sub, HEADS * VROWS, TQ), lambda i: (nt - 1 - i, 0, 0)),
                  pl.BlockSpec((sub, D_ATTN, TQ), lambda i: (nt - 1 - i, 0, 0)),
                  blk(LANES), blk(LANES), blk(LANES), blk(D_POOL), blk(d), blk(d),
                  pl.BlockSpec((1, d), _fixed), pl.BlockSpec((qkv, d), _fixed), pl.BlockSpec(wf.shape, _fixed),
                  pl.BlockSpec(wu.shape, _fixed)],
        out_specs=(blk(d), blk(fcols + D_POOL), pl.BlockSpec((1, d), _fixed), pl.BlockSpec((1, LANES), _fixed)),
        scratch_shapes=[pltpu.VMEM((n, D_POOL), F32), pltpu.VMEM((1, LANES), F32), pltpu.VMEM((TS, LANES), F32)],
        compiler_params=_params(1),
    )(dqt3, dkt3, dvt3, dcs, drs, fl, dy, x, dh1, g1, wqkv, wf, wu)


def _wgrad(a, b, out_dtype, name):
    s, m = a.shape
    n = b.shape[1]
    tm = max(t for t in range(LANES, min(m, TM_WGRAD) + 1, LANES) if m % t == 0)
    ts = min(TS_WGRAD, s)
    ns = s // ts

    def body(a_ref, b_ref, o_ref, acc):
        i = pl.program_id(1)

        @pl.when(i == 0)
        def _():
            acc[...] = jnp.zeros_like(acc)

        acc[...] += _tn(a_ref[...], b_ref[...])

        @pl.when(i == ns - 1)
        def _():
            o_ref[...] = acc[...].astype(out_dtype)

    return pl.pallas_call(
        body, grid=(m // tm, ns), name=name, out_shape=jax.ShapeDtypeStruct((m, n), out_dtype),
        in_specs=[pl.BlockSpec((ts, tm), lambda j, i: (i, j)), pl.BlockSpec((ts, n), lambda j, i: (i, 0))],
        out_specs=pl.BlockSpec((tm, n), lambda j, i: (j, 0)),
        scratch_shapes=[pltpu.VMEM((tm, n), F32)],
        compiler_params=_params(2),
    )(a, b)


def _adamw(w, g, m, v):
    m = ADAM_B1 * m + (1.0 - ADAM_B1) * g
    v = ADAM_B2 * v + (1.0 - ADAM_B2) * (g * g)
    m_hat = m / (1.0 - ADAM_B1 ** ADAM_STEP)
    v_hat = v / (1.0 - ADAM_B2 ** ADAM_STEP)
    delta = -ADAM_LR * (m_hat / (jnp.sqrt(v_hat) + ADAM_EPS) + ADAM_WD * w)
    return delta, m, v


def _sum_update(p_ref, w_ref, m_ref, v_ref, g_ref, d_ref, nm_ref, nv_ref):
    g = p_ref[0].astype(F32)
    for k in range(1, p_ref.shape[0]):
        g = g + p_ref[k].astype(F32)
    g_ref[...] = g
    d_ref[...], nm_ref[...], nv_ref[...] = _adamw(w_ref[...], g, m_ref[...], v_ref[...])


def _reduce_update_rest(parts, w, m, v, chip_blocks, small_block):
    nk, r, c = parts.shape
    ns = r // TR_REST

    def body(p_ref, w_ref, m_ref, v_ref, b_ref, sm_ref, g_ref, d_ref, nm_ref, nv_ref, got_ref, all_ref,
             stage_b, stage_s, send_b, recv_b, local_b, send_s, recv_s, local_s):
        i = pl.program_id(0)

        @pl.when(i == 0)
        def _():
            _chips_start(b_ref, got_ref, stage_b, send_b, recv_b, local_b)
            _gather_start(sm_ref, all_ref, stage_s, send_s, recv_s, local_s)

        _sum_update(p_ref, w_ref, m_ref, v_ref, g_ref, d_ref, nm_ref, nv_ref)

        @pl.when(i == ns - 1)
        def _():
            _gather_pass_on(all_ref, send_s, recv_s)
            _chips_finish(b_ref, got_ref, send_b, recv_b)
            _gather_finish(sm_ref, all_ref, send_s, recv_s)

    blk = pl.BlockSpec((TR_REST, c), _row)
    out = jax.ShapeDtypeStruct((r, c), F32)
    dma = pltpu.SemaphoreType.DMA
    return pl.pallas_call(
        body, grid=(ns,), name="reduce_update_rest",
        out_shape=(out,) * 4 + (jax.ShapeDtypeStruct(chip_blocks.shape, chip_blocks.dtype),
                                jax.ShapeDtypeStruct((N_DEV,) + small_block.shape, small_block.dtype)),
        in_specs=[pl.BlockSpec((nk, TR_REST, c), lambda i: (0, i, 0)), blk, blk, blk, ANY, ANY],
        out_specs=(blk,) * 4 + (ANY, ANY),
        scratch_shapes=[pltpu.VMEM(chip_blocks.shape[1:], chip_blocks.dtype), pltpu.VMEM(small_block.shape, small_block.dtype),
                        dma((3,)), dma((3,)), dma, dma((7,)), dma((7,)), dma],
        compiler_params=_params(1),
    )(parts, w, m, v, chip_blocks, small_block)


def _reduce_update_big(parts, w, m, v, tr, name):
    nk, r, c = parts.shape

    def body(p_ref, w_ref, m_ref, v_ref, g_ref, d_ref, nm_ref, nv_ref):
        _sum_update(p_ref, w_ref, m_ref, v_ref, g_ref, d_ref, nm_ref, nv_ref)

    blk = pl.BlockSpec((tr, c), _row)
    out = jax.ShapeDtypeStruct((r, c), F32)
    return pl.pallas_call(
        body, grid=(r // tr,), name=name, out_shape=(out,) * 4,
        in_specs=[pl.BlockSpec((nk, tr, c), lambda i: (0, i, 0)), blk, blk, blk],
        out_specs=(blk,) * 4, compiler_params=_params(1),
    )(parts, w, m, v)


def _reduce_update_small(parts, w, m, v):
    nd = parts.shape[0]

    def body(p_ref, w_ref, m_ref, v_ref, g_ref, d_ref, nm_ref, nv_ref):
        g = p_ref[0]
        for k in range(1, nd):
            g = g + p_ref[k]
        g_ref[...] = g
        d_ref[...], nm_ref[...], nv_ref[...] = _adamw(w_ref[...], g, m_ref[...], v_ref[...])

    out = jax.ShapeDtypeStruct(w.shape, F32)
    return pl.pallas_call(body, name="reduce_update_small", out_shape=(out,) * 4,
                          compiler_params=pltpu.CompilerParams(vmem_limit_bytes=VMEM_LIMIT))(parts, w, m, v)


MESH = pl.DeviceIdType.MESH


def _copy_through_vmem(src_hbm, dst_hbm, stage, sem):
    load = pltpu.make_async_copy(src_hbm, stage, sem)
    load.start()
    load.wait()
    store = pltpu.make_async_copy(stage, dst_hbm, sem)
    store.start()
    store.wait()


class _GatherPlan:
    def __init__(self, x_ref, out_ref, send_sems, recv_sems):
        x, y, c = lax.axis_index("x"), lax.axis_index("y"), lax.axis_index("c")
        self.me, self.sibling, self.c = (x, y, c), (x, y, 1 - c), c
        self.chips = [(1 - x, y), (x, 1 - y), (1 - x, 1 - y)]
        self.x_ref, self.out_ref, self.send_sems, self.recv_sems = x_ref, out_ref, send_sems, recv_sems

    def slot(self, px, py, pc):
        return self.out_ref.at[4 * px + 2 * py + pc]

    def copy(self, k, block, to, src=None):
        return pltpu.make_async_remote_copy(
            src_ref=self.slot(*block) if src is None else src, dst_ref=self.slot(*block),
            send_sem=self.send_sems.at[k], recv_sem=self.recv_sems.at[k], device_id=to, device_id_type=MESH)

    def first(self):
        return [self.copy(0, self.me, self.sibling, src=self.x_ref)] + [
            self.copy(1 + j, self.me, (*chip, self.c), src=self.x_ref) for j, chip in enumerate(self.chips)]

    def passed(self):
        return [self.copy(4 + j, (*chip, self.c), self.sibling) for j, chip in enumerate(self.chips)]


def _gather_start(x_ref, out_ref, stage, send_sems, recv_sems, local_sem):
    plan = _GatherPlan(x_ref, out_ref, send_sems, recv_sems)
    for cp in plan.first():
        cp.start()
    _copy_through_vmem(x_ref, plan.slot(*plan.me), stage, local_sem)


def _gather_pass_on(out_ref, send_sems, recv_sems):
    plan = _GatherPlan(None, out_ref, send_sems, recv_sems)
    passed = plan.passed()
    for j, chip in enumerate(plan.chips):
        plan.copy(1 + j, (*chip, plan.c), plan.me).wait_recv()
        passed[j].start()


def _gather_finish(x_ref, out_ref, send_sems, recv_sems):
    plan = _GatherPlan(x_ref, out_ref, send_sems, recv_sems)
    plan.copy(0, plan.sibling, plan.me).wait_recv()
    for j, chip in enumerate(plan.chips):
        plan.copy(4 + j, (*chip, 1 - plan.c), plan.me).wait_recv()
    for cp in plan.first() + plan.passed():
        cp.wait_send()


def _all_gather(xs, name):
    r, cdim = xs.shape

    def body(x_ref, out_ref, stage, send_sems, recv_sems, local_sem):
        _gather_start(x_ref, out_ref, stage, send_sems, recv_sems, local_sem)
        _gather_pass_on(out_ref, send_sems, recv_sems)
        _gather_finish(x_ref, out_ref, send_sems, recv_sems)

    return pl.pallas_call(
        body, name=name, out_shape=jax.ShapeDtypeStruct((N_DEV, r, cdim), xs.dtype),
        in_specs=[ANY], out_specs=ANY,
        scratch_shapes=[pltpu.VMEM((r, cdim), xs.dtype), pltpu.SemaphoreType.DMA((7,)), pltpu.SemaphoreType.DMA((7,)),
                        pltpu.SemaphoreType.DMA],
        compiler_params=pltpu.CompilerParams(vmem_limit_bytes=VMEM_LIMIT),
    )(xs)


def _rs_pair_sum(core, pieces, offsets, rows, name):
    cdim = pieces[0].shape[2]
    nk = N_DEV // 2
    npc = len(pieces)
    spans = [(o, t.shape[1]) for t, o in zip(pieces, offsets)]
    ends = [o + n for o, n in spans]
    gaps = [(a, b - a) for a, b in zip(ends, [o for o, _ in spans[1:]] + [rows]) if b > a]

    def body(core_ref, *refs):
        own, src, o_ref = refs[:npc], refs[npc:2 * npc], refs[2 * npc]
        landing, send_sems, recv_sems = refs[2 * npc + 1:]
        k = pl.program_id(0)
        x, y, c = lax.axis_index("x"), lax.axis_index("y"), lax.axis_index("c")

        def copies(kk):
            return [pltpu.make_async_remote_copy(
                src_ref=src[p].at[2 * kk + (1 - c)], dst_ref=landing.at[kk, pl.ds(o, n)],
                send_sem=send_sems.at[kk, p], recv_sem=recv_sems.at[kk, p], device_id=(x, y, 1 - c),
                device_id_type=MESH) for p, (o, n) in enumerate(spans)]

        @pl.when(k == 0)
        def _():
            for kk in range(nk):
                for cp in copies(kk):
                    cp.start()

        for cp, piece, (o, n) in zip(copies(k), own, spans):
            cp.wait_recv()
            o_ref[0, o:o + n, :] = (piece[0].astype(F32) + landing[k, o:o + n, :].astype(F32)).astype(BF16)
        for o, n in gaps:
            o_ref[0, o:o + n, :] = jnp.zeros((n, cdim), BF16)

        @pl.when(k == nk - 1)
        def _():
            for kk in range(nk):
                for cp in copies(kk):
                    cp.wait_send()

    own_specs = [pl.BlockSpec((1, n, cdim), lambda k, core_ref: (2 * k + core_ref[0], 0, 0)) for _, n in spans]
    return pl.pallas_call(
        body, name=name, out_shape=jax.ShapeDtypeStruct((nk, rows, cdim), BF16),
        grid_spec=pltpu.PrefetchScalarGridSpec(
            num_scalar_prefetch=1, grid=(nk,),
            in_specs=own_specs + [ANY] * npc,
            out_specs=pl.BlockSpec((1, rows, cdim), lambda k, core_ref: (k, 0, 0)),
            scratch_shapes=[pltpu.VMEM((nk, rows, cdim), BF16), pltpu.SemaphoreType.DMA((nk, npc)),
                            pltpu.SemaphoreType.DMA((nk, npc))]),
        compiler_params=_params(1),
    )(core, *pieces, *pieces)


def _chips_start(b_ref, out_ref, stage, send_sems, recv_sems, local_sem):
    x, y, c = lax.axis_index("x"), lax.axis_index("y"), lax.axis_index("c")
    mychip = 2 * x + y
    for j, (px, py) in enumerate([(1 - x, y), (x, 1 - y), (1 - x, 1 - y)]):
        pltpu.make_async_remote_copy(
            src_ref=b_ref.at[2 * px + py], dst_ref=out_ref.at[mychip],
            send_sem=send_sems.at[j], recv_sem=recv_sems.at[j], device_id=(px, py, c), device_id_type=MESH).start()
    _copy_through_vmem(b_ref.at[mychip], out_ref.at[mychip], stage, local_sem)


def _chips_finish(b_ref, out_ref, send_sems, recv_sems):
    x, y, c = lax.axis_index("x"), lax.axis_index("y"), lax.axis_index("c")
    for j, (px, py) in enumerate([(1 - x, y), (x, 1 - y), (1 - x, 1 - y)]):
        pltpu.make_async_remote_copy(
            src_ref=b_ref.at[2 * px + py], dst_ref=out_ref.at[2 * px + py],
            send_sem=send_sems.at[j], recv_sem=recv_sems.at[j], device_id=(px, py, c), device_id_type=MESH).wait()


def _pad_rows(a, rows):
    return jnp.pad(a, ((0, rows - a.shape[0]), (0, 0)))


def _pack_in(w_in):
    return _pad_rows(w_in[0].T, ROWS_IN)


def _unpack_in(r):
    return r[0:SHARD_IN].T[None]


def _pack_rest(w_out, w_gate, w_up, w_down, w_ple, w_pg):
    head = _pad_rows(jnp.concatenate([w_out[0], w_pg[0], w_ple[0].T.reshape(32, D_MODEL)], axis=0), OFF_GATE)
    return jnp.concatenate([head, w_gate[0].T, w_up[0].T, w_down[0]], axis=0)


def _unpack_rest(r):
    return (r[0:OFF_PG][None], r[OFF_GATE:OFF_UP].T[None], r[OFF_UP:OFF_DOWN].T[None], r[OFF_DOWN:ROWS_REST][None],
            r[OFF_PLE:OFF_PLE + 32].reshape(128, D_PLE).T[None], r[OFF_PG:OFF_PLE][None])


def _pack_small(w_pool, g_mix_pre, g_mix_post, g_ffn_pre, g_ffn_post, g_ple, g_attn, g_pool, pool_scale, b_forget,
                loss=None):
    def row(vrow):
        return jnp.pad(vrow.reshape(1, -1), ((0, 0), (0, D_MODEL - vrow.size)))
    rows = [w_pool.reshape(64, D_MODEL), row(g_mix_pre), row(g_mix_post), row(g_ffn_pre), row(g_ffn_post), row(g_ple),
            row(g_attn), row(g_pool), row(pool_scale), row(b_forget),
            row(loss) if loss is not None else jnp.zeros((1, D_MODEL), F32)]
    return _pad_rows(jnp.concatenate(rows, axis=0), SMALL_ROWS)


def _unpack_small(r):
    return dict(
        w_pool=r[0:64].reshape(1, 4, POOL_CH, POOL_CH), g_mix_pre=r[ROW_G_MIX_PRE:ROW_G_MIX_PRE + 1],
        g_mix_post=r[ROW_G_MIX_POST:ROW_G_MIX_POST + 1], g_ffn_pre=r[ROW_G_FFN_PRE:ROW_G_FFN_PRE + 1],
        g_ffn_post=r[ROW_G_FFN_POST:ROW_G_FFN_POST + 1], g_ple=r[ROW_G_PLE:ROW_G_PLE + 1],
        g_attn_grp=r[ROW_G_ATTN:ROW_G_ATTN + 1, 0:D_ATTN], g_pool_grp=r[ROW_G_POOL:ROW_G_POOL + 1, 0:D_POOL],
        pool_scale=r[ROW_POOL_SCALE:ROW_POOL_SCALE + 1, 0:D_POOL], b_forget=r[ROW_B_FORGET:ROW_B_FORGET + 1, 0:HEADS])


def _step(x, p, tgt, small, in_w, in_m, in_v, rest_w, rest_m, rest_v):
    core = lax.axis_index("c").astype(jnp.int32).reshape(1)
    win_t = _all_gather(in_w.astype(BF16), "gather_w_in")[:, 0:SHARD_IN].reshape(D_IN, D_MODEL)
    wqkv = win_t
    wf = _pad_rows(win_t[3 * D_ATTN:3 * D_ATTN + HEADS], LANES)
    wu = win_t[3 * D_ATTN + HEADS:]
    wpool = small["w_pool"].astype(BF16)
    bpad = jnp.pad(small["b_forget"], ((0, 0), (0, LANES - HEADS)))

    lay = _attn_layout_constants()
    rest_b = rest_w.astype(BF16)
    hn, qt3, ka, v, qat3, vt3, kt3, fl, y, mpre, gh = _pre_attn_fwd(x, small["g_mix_pre"], wqkv, wf, wu, bpad, wpool, lay,
                                                                 rest_b[0:OFF_GATE])
    a, lset3, gf = _attn_fwd(ka, qat3, vt3, rest_b[OFF_GATE:])
    wple_t = gh[:, OFF_PLE:OFF_PLE + 32].reshape(D_MODEL, D_PLE)
    mix, o, h1, hn2 = _post_attn_fwd(a, mpre, x, small["g_attn_grp"], small["g_pool_grp"], small["pool_scale"], gh,
                                     small["g_mix_post"], small["g_ffn_pre"])
    gate, up, act, ff, h2 = _ffn_fwd(hn2, gf, gf, gf, h1, small["g_ffn_post"])
    dh2, dff, dgl, dpp, h2b, pb, loss8, dg_ple, dg_ffn_post = _tail_fwd_bwd(
        h2, p, tgt, ff, wple_t, gh, small["g_ple"], small["g_ffn_post"])
    dgate, dup, dh1, dg_ffn_pre = _ffn_bwd(dff, gate, up, gf, gf, gf, h1, dh2, small["g_ffn_pre"])
    dob, dat3, dlt3, dmpb, dy, dg_mix_post, dg_attn, dg_pool, dps = _post_attn_bwd(
        dh1, o, a, mpre, gh, wpool, small["g_mix_post"], small["g_attn_grp"], small["g_pool_grp"], small["pool_scale"])

    nd = N_DEV
    send_rest = [
        _wgrad(mix, dob, BF16, "wgrad_out").reshape(nd, 128, D_MODEL),
        _wgrad(h2b, dgl, BF16, "wgrad_ple_gate").reshape(nd, 128, D_MODEL),
        _wgrad(dpp, pb, BF16, "wgrad_ple").reshape(nd, 32, D_MODEL),
        _wgrad(dgate, hn2, BF16, "wgrad_gate").reshape(nd, SHARD_FF, D_MODEL),
        _wgrad(dup, hn2, BF16, "wgrad_up").reshape(nd, SHARD_FF, D_MODEL),
        _wgrad(act, dff, BF16, "wgrad_down").reshape(nd, SHARD_FF, D_MODEL)]
    pair_rest = _rs_pair_sum(core, send_rest, [0, OFF_PG, OFF_PLE, OFF_GATE, OFF_UP, OFF_DOWN], ROWS_REST,
                             "rs_pair_sum_rest")

    dqt3, dkt3, dvt3, chips_rest = _attn_bwd(ka, v, kt3, qat3, qt3, dat3, lset3, dlt3, pair_rest)

    def sums_per_token(t3):
        rows = t3.reshape(-1, HEADS, VROWS, TQ)[:, :, HEAD_DIM, :]
        return jnp.pad(rows.transpose(0, 2, 1).reshape(-1, HEADS), ((0, 0), (0, LANES - HEADS)))

    drs, dcs = sums_per_token(dqt3), sums_per_token(dkt3)
    gx, dz, dg_mix_pre, db = _pre_attn_bwd(dqt3, dkt3, dvt3, dcs, drs, fl, dy, x, dh1, small["g_mix_pre"], wqkv, wf, wu)

    dwz = _wgrad(dz, hn, BF16, "wgrad_in")
    dwin_t = jnp.concatenate([dwz[0:3 * D_ATTN], dwz[3 * D_ATTN:3 * D_ATTN + HEADS], dwz[3 * D_ATTN + LANES:]], axis=0)
    send_in = jnp.pad(dwin_t.reshape(nd, SHARD_IN, D_MODEL), ((0, 0), (0, ROWS_IN - SHARD_IN), (0, 0)))
    pair_in = _rs_pair_sum(core, [send_in], [0], ROWS_IN, "rs_pair_sum_in")

    dwp = _wgrad(y, dmpb, F32, "wgrad_pool")
    dw_pool = jnp.stack([dwp[g * POOL_CH:(g + 1) * POOL_CH, g * POOL_CH:(g + 1) * POOL_CH] for g in range(4)])
    small_part = _pack_small(dw_pool, dg_mix_pre, dg_mix_post, dg_ffn_pre, dg_ffn_post, dg_ple, dg_attn, dg_pool, dps,
                             db[:, 0:HEADS], loss8[0:1, 0:1])

    *upd_rest, chips_in, small_all = _reduce_update_rest(chips_rest, rest_w, rest_m, rest_v, pair_in, small_part)
    upd_in = _reduce_update_big(chips_in, in_w, in_m, in_v, ROWS_IN, "reduce_update_in")
    return gx, small_all, upd_in, upd_rest


def kernel(x, p, g_mix_pre, w_in, b_forget, g_attn_grp, g_pool_grp, w_pool, pool_scale, w_out, g_mix_post, g_ffn_pre, w_ffn_gate, w_ffn_up, w_ffn_down, g_ffn_post, w_ple_proj, g_ple, w_ple_gate, loss_target, m_g_mix_pre, m_w_in, m_b_forget, m_g_attn_grp, m_g_pool_grp, m_w_pool, m_pool_scale, m_w_out, m_g_mix_post, m_g_ffn_pre, m_w_ffn_gate, m_w_ffn_up, m_w_ffn_down, m_g_ffn_post, m_w_ple_proj, m_g_ple, m_w_ple_gate, v_g_mix_pre, v_w_in, v_b_forget, v_g_attn_grp, v_g_pool_grp, v_w_pool, v_pool_scale, v_w_out, v_g_mix_post, v_g_ffn_pre, v_w_ffn_gate, v_w_ffn_up, v_w_ffn_down, v_g_ffn_post, v_w_ple_proj, v_g_ple, v_w_ple_gate):
    small = dict(w_pool=w_pool[0], g_mix_pre=g_mix_pre, g_mix_post=g_mix_post, g_ffn_pre=g_ffn_pre,
                 g_ffn_post=g_ffn_post, g_ple=g_ple, g_attn_grp=g_attn_grp, g_pool_grp=g_pool_grp,
                 pool_scale=pool_scale, b_forget=b_forget)
    gx, small_all, upd_in, upd_rest = _step(
        x[0], p[0, 0], loss_target[0], small, _pack_in(w_in), _pack_in(m_w_in), _pack_in(v_w_in),
        _pack_rest(w_out, w_ffn_gate, w_ffn_up, w_ffn_down, w_ple_proj, w_ple_gate),
        _pack_rest(m_w_out, m_w_ffn_gate, m_w_ffn_up, m_w_ffn_down, m_w_ple_proj, m_w_ple_gate),
        _pack_rest(v_w_out, v_w_ffn_gate, v_w_ffn_up, v_w_ffn_down, v_w_ple_proj, v_w_ple_gate))

    sm_w = _pack_small(w_pool, g_mix_pre, g_mix_post, g_ffn_pre, g_ffn_post, g_ple, g_attn_grp, g_pool_grp, pool_scale, b_forget)
    sm_m = _pack_small(m_w_pool, m_g_mix_pre, m_g_mix_post, m_g_ffn_pre, m_g_ffn_post, m_g_ple, m_g_attn_grp, m_g_pool_grp, m_pool_scale, m_b_forget)
    sm_v = _pack_small(v_w_pool, v_g_mix_pre, v_g_mix_post, v_g_ffn_pre, v_g_ffn_post, v_g_ple, v_g_attn_grp, v_g_pool_grp, v_pool_scale, v_b_forget)
    upd_small = _reduce_update_small(small_all, sm_w, sm_m, sm_v)
    loss = upd_small[0][ROW_LOSS, 0]

    def leaves(k):
        b_out, b_gate, b_up, b_down, b_ple, b_pg = _unpack_rest(upd_rest[k])
        s = _unpack_small(upd_small[k])
        return (s["g_mix_pre"], _unpack_in(upd_in[k]), s["b_forget"], s["g_attn_grp"], s["g_pool_grp"], s["w_pool"],
                s["pool_scale"], b_out, s["g_mix_post"], s["g_ffn_pre"], b_gate, b_up, b_down, s["g_ffn_post"], b_ple,
                s["g_ple"], b_pg)

    return (loss, gx[None], *leaves(0), *leaves(1), *leaves(2), *leaves(3))
```

```python
import functools

import jax
import jax.numpy as jnp
from jax import lax
from jax.experimental import pallas as pl
from jax.experimental.pallas import tpu as pltpu

F32 = jnp.float32
BF16 = jnp.bfloat16
HIGHEST = lax.Precision.HIGHEST

D_MODEL = 1024
HEADS = 8
HEAD_DIM = 64
D_ATTN = HEADS * HEAD_DIM
POOL_WINDOWS = (2, 4, 8, 16)
POOL_CH = 128
D_POOL = POOL_CH * len(POOL_WINDOWS)
D_FF = 2816
D_PLE = 256
D_IN = 3 * D_ATTN + HEADS + D_POOL
RMS_EPS = 1e-6
N_DEV = 8

ADAM_LR = 0.001
ADAM_B1 = 0.9
ADAM_B2 = 0.999
ADAM_EPS = 1e-08
ADAM_WD = 0.01
ADAM_STEP = 10

LANES = 128
HALO = 16
TS = 512
TS_FF = 512
TS_WGRAD = 1024
TM_WGRAD = 2176
TQ = 256
TN_FF = 1408
NEG = -1e30
VMEM_LIMIT = 56 * 1024 * 1024

SHARD_IN = 257
ROWS_IN = 272
SHARD_FF = 352
OFF_PG = 128
OFF_PLE = 256
OFF_GATE = SHARD_FF
OFF_UP = 2 * SHARD_FF
OFF_DOWN = 3 * SHARD_FF
ROWS_REST = 4 * SHARD_FF
ROWS_PLE = 32
UPDATE_CHUNKS = 4

SMALL_ROWS = 80
ROW_G_MIX_PRE, ROW_G_MIX_POST, ROW_G_FFN_PRE, ROW_G_FFN_POST, ROW_G_PLE = 64, 65, 66, 67, 68
ROW_G_ATTN, ROW_G_POOL, ROW_POOL_SCALE, ROW_B_FORGET, ROW_LOSS = 69, 70, 71, 72, 73


def _nn(a, b):
    return jnp.dot(a, b, preferred_element_type=F32)


def _nt(a, b):
    return lax.dot_general(a, b, (((1,), (1,)), ((), ())), preferred_element_type=F32)


def _tn(a, b):
    return lax.dot_general(a, b, (((0,), (0,)), ((), ())), preferred_element_type=F32)


def _rstd(v):
    return lax.rsqrt(jnp.mean(v * v, axis=-1, keepdims=True) + RMS_EPS)


def _rms_bwd(v, g, dy):
    r = _rstd(v)
    vh = v * r
    t = dy * g
    dv = r * (t - vh * jnp.mean(t * vh, axis=-1, keepdims=True))
    return dv, jnp.sum(dy * vh, axis=0, keepdims=True)


def _split3(v):
    hi = v.astype(BF16)
    rest = v - hi.astype(F32)
    mid = rest.astype(BF16)
    return hi, mid, (rest - mid.astype(F32)).astype(BF16)


def _mask_matmul(mask, v):
    hi, mid, lo = _split3(v)
    return _nn(mask, lo) + _nn(mask, mid) + _nn(mask, hi)


def _params(n_grid):
    return pltpu.CompilerParams(dimension_semantics=("arbitrary",) * n_grid, vmem_limit_bytes=VMEM_LIMIT)


def _row(i):
    return (i, 0)


def _fixed(*_):
    return (0, 0)


def _spec_square(part):
    return pl.BlockSpec((N_DEV, 128, D_MODEL), lambda *_: (0, part, 0))


def _spec_ff(part):
    return pl.BlockSpec((TN_FF // SHARD_FF, SHARD_FF, D_MODEL), lambda i, j: (j, part, 0))


assert TS == 2 * TQ and TN_FF % SHARD_FF == 0
_HALVES = (slice(0, TQ), slice(TQ, TS))

VMEM_WHOLE = pl.BlockSpec(memory_space=pltpu.VMEM)
SMEM_WHOLE = pl.BlockSpec(memory_space=pltpu.SMEM)
ANY = pl.BlockSpec(memory_space=pl.ANY)


LOG2E = 1.4426950408889634
VROWS = HEAD_DIM + 16
AUG = 128
BIAS_LANE = HEAD_DIM
ONE_LANE = HEAD_DIM + 3
SPARE_LANE = HEADS


def _attn_layout_constants():
    import numpy as np
    place = np.zeros((D_ATTN, HEADS * AUG), np.float32)
    for r in range(D_ATTN):
        place[r, (r // HEAD_DIM) * AUG + r % HEAD_DIM] = 1.0
    bias_k = np.zeros((3, LANES, HEADS * AUG), np.float32)
    bias_q = np.zeros((3, LANES, HEADS * AUG), np.float32)
    for h in range(HEADS):
        for part in range(3):
            bias_k[part, h, h * AUG + BIAS_LANE + part] = -1.0
            bias_q[part, h, h * AUG + ONE_LANE + part] = 1.0
            bias_k[0, SPARE_LANE, h * AUG + ONE_LANE + part] = 1.0
            bias_q[0, SPARE_LANE, h * AUG + BIAS_LANE + part] = 1.0
    as_bf = lambda a: jnp.asarray(a, BF16)
    return dict(place=as_bf(place), place_t=as_bf(place.T), bias_k=as_bf(bias_k),
                bias_q_t=as_bf(bias_q.transpose(0, 2, 1)))


def _pre_attn_fwd(x, g1, wqkv, wf, wu, bpad, wpool, lay, own_block):
    s, d = x.shape
    nt = s // TS
    sub = TS // TQ

    def body(x_ref, g_ref, wqkv_ref, wf_ref, wu_ref, b_ref, wp_ref, place_ref, place_t_ref, bk_ref, bqt_ref, own_ref,
             hn_ref, qt_ref, ka_ref, v_ref, qat_ref, vt_ref, kt_ref, fl_ref, y_ref, mp_ref, all_ref,
             ubuf, ccar, cbuf, stage, send_sems, recv_sems, local_sem):
        i = pl.program_id(0)

        @pl.when(i == 0)
        def _():
            _gather_start(own_ref, all_ref, stage, send_sems, recv_sems, local_sem)
            ubuf[0:HALO, :] = jnp.zeros((HALO, D_POOL), F32)
            ccar[...] = jnp.zeros_like(ccar)

        @pl.when(i == max(nt - 2, 0))
        def _():
            _gather_pass_on(all_ref, send_sems, recv_sems)

        xv = x_ref[...]
        hn = (xv * _rstd(xv) * g_ref[...]).astype(BF16)
        hn_ref[...] = hn
        zq = _nt(hn, wqkv_ref[...])
        qt = (zq[:, 0:D_ATTN] * 0.125).astype(BF16).T
        qb = (zq[:, 0:D_ATTN] * (0.125 * LOG2E)).astype(BF16)
        kb = zq[:, D_ATTN:2 * D_ATTN].astype(BF16)
        vb = zq[:, 2 * D_ATTN:3 * D_ATTN].astype(BF16)
        v_ref[...] = vb

        fl = _nt(hn, wf_ref[...]) + b_ref[...]
        fl_ref[...] = fl
        logf = jax.nn.log_sigmoid(fl)
        rr = lax.broadcasted_iota(jnp.int32, (TS, TS), 0)
        cc = lax.broadcasted_iota(jnp.int32, (TS, TS), 1)
        c = _mask_matmul((cc <= rr).astype(BF16), logf) + ccar[...]
        cbuf[...] = c
        ccar[...] = cbuf[TS - 1:TS, :]
        hi, mid, lo = _split3(c * LOG2E)
        lane = lax.broadcasted_iota(jnp.int32, (TS, LANES), 1)
        parts = (jnp.where(lane == SPARE_LANE, 1.0, hi).astype(BF16), mid, lo)
        ka = _nn(kb, place_ref[...])
        qat = _nt(place_t_ref[...], qb)
        for part in range(3):
            ka = ka + _nn(parts[part], bk_ref[part])
            qat = qat + _nt(bqt_ref[part], parts[part])
        ka_ref[...] = ka.astype(BF16)
        qat = qat.astype(BF16)
        vt = vb.T
        kt = kb.T
        for a in range(sub):
            cols = slice(a * TQ, (a + 1) * TQ)
            qat_ref[a] = qat[:, cols]
            for ref, mat in ((qt_ref, qt), (kt_ref, kt), (vt_ref, vt)):
                for h in range(HEADS):
                    ref[a, h * VROWS:h * VROWS + HEAD_DIM, :] = mat[h * HEAD_DIM:(h + 1) * HEAD_DIM, cols]
                    ref[a, h * VROWS + HEAD_DIM:(h + 1) * VROWS, :] = jnp.ones((VROWS - HEAD_DIM, TQ), BF16)

        u = _nt(hn, wu_ref[...])
        ubuf[HALO:HALO + TS, :] = u
        t = i * TS + lax.broadcasted_iota(jnp.int32, (TS, 1), 0)
        for g, w in enumerate(POOL_WINDOWS):
            cols = slice(g * POOL_CH, (g + 1) * POOL_CH)
            sm = ubuf[:, cols]
            step = 1
            while step < w:
                sm = sm + pltpu.roll(sm, step, 0)
                step *= 2
            cnt = jnp.minimum(t + 1, w).astype(F32)
            yg = (sm[HALO:, :] / cnt - u[:, cols]).astype(BF16)
            y_ref[:, cols] = yg
            mp_ref[:, cols] = _nn(yg, wp_ref[g])
        ubuf[0:HALO, :] = u[TS - HALO:, :]

        @pl.when(i == nt - 1)
        def _():
            _gather_finish(own_ref, all_ref, send_sems, recv_sems)

    nq = s // TQ
    aug = HEADS * AUG
    outs = (
        jax.ShapeDtypeStruct((s, d), BF16), jax.ShapeDtypeStruct((nq, HEADS * VROWS, TQ), BF16),
        jax.ShapeDtypeStruct((s, aug), BF16), jax.ShapeDtypeStruct((s, D_ATTN), BF16),
        jax.ShapeDtypeStruct((nq, aug, TQ), BF16), jax.ShapeDtypeStruct((nq, HEADS * VROWS, TQ), BF16),
        jax.ShapeDtypeStruct((nq, HEADS * VROWS, TQ), BF16),
        jax.ShapeDtypeStruct((s, LANES), F32),
        jax.ShapeDtypeStruct((s, D_POOL), BF16), jax.ShapeDtypeStruct((s, D_POOL), F32),
        jax.ShapeDtypeStruct((N_DEV,) + own_block.shape, own_block.dtype),
    )
    fixed3 = lambda i: (0, 0, 0)
    tiles3 = lambda rows: pl.BlockSpec((sub, rows, TQ), lambda i: (i, 0, 0))
    return pl.pallas_call(
        body, grid=(nt,), out_shape=outs, name="pre_attn_fwd",
        in_specs=[pl.BlockSpec((TS, d), _row), pl.BlockSpec((1, d), _fixed),
                  pl.BlockSpec((3 * D_ATTN, d), _fixed), pl.BlockSpec(wf.shape, _fixed), pl.BlockSpec(wu.shape, _fixed),
                  pl.BlockSpec((1, LANES), _fixed), pl.BlockSpec(wpool.shape, fixed3),
                  pl.BlockSpec(lay["place"].shape, _fixed), pl.BlockSpec(lay["place_t"].shape, _fixed),
                  pl.BlockSpec(lay["bias_k"].shape, fixed3), pl.BlockSpec(lay["bias_q_t"].shape, fixed3), ANY],
        out_specs=(pl.BlockSpec((TS, d), _row), tiles3(HEADS * VROWS),
                   pl.BlockSpec((TS, aug), _row), pl.BlockSpec((TS, D_ATTN), _row),
                   tiles3(aug), tiles3(HEADS * VROWS), tiles3(HEADS * VROWS),
                   pl.BlockSpec((TS, LANES), _row),
                   pl.BlockSpec((TS, D_POOL), _row), pl.BlockSpec((TS, D_POOL), _row), ANY),
        scratch_shapes=[pltpu.VMEM((TS + HALO, D_POOL), F32), pltpu.VMEM((1, LANES), F32), pltpu.VMEM((TS, LANES), F32),
                        pltpu.VMEM(own_block.shape, own_block.dtype),
                        pltpu.SemaphoreType.DMA((7,)), pltpu.SemaphoreType.DMA((7,)), pltpu.SemaphoreType.DMA],
        compiler_params=_params(1),
    )(x, g1, wqkv, wf, wu, bpad, wpool, lay["place"], lay["place_t"], lay["bias_k"], lay["bias_q_t"], own_block)


def _causal_in_tile():
    krow = lax.broadcasted_iota(jnp.int32, (TQ, TQ), 0)
    qcol = lax.broadcasted_iota(jnp.int32, (TQ, TQ), 1)
    return krow <= qcol


def _attn_fwd(ka, qat3, vt3, own_block):
    s = ka.shape[0]
    nq = s // TQ
    pass_on_step = max(nq - 2, 0)

    def body(qa_ref, ka_ref, vt_ref, own_ref, a_ref, lset_ref, all_ref, acc, out_t, st_scr, pt_scr,
             stage, send_sems, recv_sems, local_sem):
        i = pl.program_id(0)

        @pl.when(i == 0)
        def _():
            _gather_start(own_ref, all_ref, stage, send_sems, recv_sems, local_sem)

        @pl.when(i == pass_on_step)
        def _():
            _gather_pass_on(all_ref, send_sems, recv_sems)

        acc[...] = jnp.zeros_like(acc)

        def tile(j, stats, masked):
            tile_max = []
            for h in range(HEADS):
                aug = slice(h * AUG, (h + 1) * AUG)
                st = _nn(ka_ref[pl.ds(j * TQ, TQ), aug], qa_ref[0, aug, :])
                if masked:
                    st = jnp.where(_causal_in_tile(), st, NEG)
                st_scr[h] = st
                tile_max.append(jnp.max(st, axis=0, keepdims=True))
            new, scale = [], []
            for h in range(HEADS):
                m_new = jnp.maximum(stats[h], tile_max[h])
                scale.append(jnp.exp2(stats[h] - m_new))
                pt_scr[h] = jnp.exp2(st_scr[h] - m_new).astype(BF16)
                new.append(m_new)
            for h in range(HEADS):
                rows = slice(h * VROWS, (h + 1) * VROWS)
                acc[rows, :] = scale[h] * acc[rows, :] + _nn(vt_ref[j, rows, :], pt_scr[h])
            return tuple(new)

        init = tuple(jnp.full((1, TQ), NEG, F32) for _ in range(HEADS))
        stats = lax.fori_loop(0, i, functools.partial(tile, masked=False), init)
        stats = tile(i, stats, True)
        for h in range(HEADS):
            denom = acc[h * VROWS + HEAD_DIM:h * VROWS + HEAD_DIM + 1, :]
            out_t[h * HEAD_DIM:(h + 1) * HEAD_DIM, :] = acc[h * VROWS:h * VROWS + HEAD_DIM, :] / denom
            lset_ref[0, h:h + 1, :] = stats[h] + jnp.log2(denom)
        a_ref[...] = out_t[...].T

        @pl.when(i == nq - 1)
        def _():
            _gather_finish(own_ref, all_ref, send_sems, recv_sems)

    r, cdim = own_block.shape
    return pl.pallas_call(
        body, grid=(nq,), name="attn_fwd",
        out_shape=(jax.ShapeDtypeStruct((s, D_ATTN), F32), jax.ShapeDtypeStruct((nq, HEADS, TQ), F32),
                   jax.ShapeDtypeStruct((N_DEV, r, cdim), own_block.dtype)),
        in_specs=[pl.BlockSpec((1, HEADS * AUG, TQ), lambda i: (i, 0, 0)), VMEM_WHOLE, VMEM_WHOLE, ANY],
        out_specs=(pl.BlockSpec((TQ, D_ATTN), _row), pl.BlockSpec((1, HEADS, TQ), lambda i: (i, 0, 0)), ANY),
        scratch_shapes=[pltpu.VMEM((HEADS * VROWS, TQ), F32), pltpu.VMEM((D_ATTN, TQ), F32),
                        pltpu.VMEM((HEADS, TQ, TQ), F32), pltpu.VMEM((HEADS, TQ, TQ), BF16),
                        pltpu.VMEM((r, cdim), own_block.dtype),
                        pltpu.SemaphoreType.DMA((7,)), pltpu.SemaphoreType.DMA((7,)), pltpu.SemaphoreType.DMA],
        compiler_params=_params(1),
    )(qat3, ka, vt3, own_block)


def _post_attn_fwd(a, mpre, x, g_attn, g_pool, pscale, wout, g_post, g_ffn_pre):
    s, d = x.shape

    def body(a_ref, mp_ref, x_ref, ga_ref, gp_ref, ps_ref, wo_ref, gpost_ref, gpre_ref,
             mix_ref, o_ref, h1_ref, hn2_ref):
        for rows in _HALVES:
            av = a_ref[rows, :]
            mix_ref[rows, 0:D_ATTN] = (av * _rstd(av) * ga_ref[...]).astype(BF16)
            mv = mp_ref[rows, :] * ps_ref[...]
            mix_ref[rows, D_ATTN:] = (mv * _rstd(mv) * gp_ref[...]).astype(BF16)
            o = _nn(mix_ref[rows, :], wo_ref[...].reshape(d, d))
            o_ref[rows, :] = o
            h1 = x_ref[rows, :] + o * _rstd(o) * gpost_ref[...]
            h1_ref[rows, :] = h1
            hn2_ref[rows, :] = (h1 * _rstd(h1) * gpre_ref[...]).astype(BF16)

    vec = lambda n: pl.BlockSpec((1, n), _fixed)
    return pl.pallas_call(
        body, grid=(s // TS,), name="post_attn_fwd",
        out_shape=(jax.ShapeDtypeStruct((s, d), BF16), jax.ShapeDtypeStruct((s, d), F32),
                   jax.ShapeDtypeStruct((s, d), F32), jax.ShapeDtypeStruct((s, d), BF16)),
        in_specs=[pl.BlockSpec((TS, D_ATTN), _row), pl.BlockSpec((TS, D_POOL), _row), pl.BlockSpec((TS, d), _row),
                  vec(D_ATTN), vec(D_POOL), vec(D_POOL), _spec_square(0), vec(d), vec(d)],
        out_specs=(pl.BlockSpec((TS, d), _row),) * 4,
        compiler_params=_params(1),
    )(a, mpre, x, g_attn, g_pool, pscale, wout, g_post, g_ffn_pre)


def _ffn_fwd(hn2, wg, wu, wd, h1, g_post):
    s, d = h1.shape
    nc = D_FF // TN_FF
    ts = min(TS_FF, s)

    def body(hn_ref, wg_ref, wu_ref, wd_ref, h1_ref, g_ref, gate_ref, up_ref, act_ref, ff_ref, h2_ref, acc):
        j = pl.program_id(1)

        @pl.when(j == 0)
        def _():
            acc[...] = jnp.zeros_like(acc)

        for r in range(2):
            rows = slice(r * (ts // 2), (r + 1) * (ts // 2))
            hn = hn_ref[rows, :]
            gt = _nt(hn, wg_ref[...].reshape(TN_FF, d))
            up = _nt(hn, wu_ref[...].reshape(TN_FF, d))
            gate_ref[rows, :] = gt.astype(BF16)
            up_ref[rows, :] = up.astype(BF16)
            act_ref[rows, :] = (gt * jax.nn.sigmoid(gt) * up).astype(BF16)
            acc[rows, :] += _nn(act_ref[rows, :], wd_ref[...].reshape(TN_FF, d))

        @pl.when(j == nc - 1)
        def _():
            ff = acc[...]
            ff_ref[...] = ff
            h2_ref[...] = h1_ref[...] + ff * _rstd(ff) * g_ref[...]

    rowblk = pl.BlockSpec((ts, d), lambda i, j: (i, 0))
    chunk = pl.BlockSpec((ts, TN_FF), lambda i, j: (i, j))
    return pl.pallas_call(
        body, grid=(s // ts, nc), name="ffn_fwd",
        out_shape=(jax.ShapeDtypeStruct((s, D_FF), BF16),) * 3 + (jax.ShapeDtypeStruct((s, d), F32),) * 2,
        in_specs=[rowblk, _spec_ff(0), _spec_ff(1), _spec_ff(2), rowblk, pl.BlockSpec((1, d), lambda i, j: (0, 0))],
        out_specs=(chunk, chunk, chunk, rowblk, rowblk),
        scratch_shapes=[pltpu.VMEM((ts, d), F32)],
        compiler_params=_params(2),
    )(hn2, wg, wu, wd, h1, g_post)


def _tail_fwd_bwd(h2, p, tgt, ff, wple, wpg, g_ple, g_ffn_post):
    s, d = h2.shape

    def body(h2_ref, p_ref, t_ref, ff_ref, wple_ref, wpg_ref, gple_ref, gfp_ref,
             dh2_ref, dff_ref, dgl_ref, dpp_ref, h2b_ref, pb_ref, loss_ref, dgple_ref, dgfp_ref):
        i = pl.program_id(0)

        @pl.when(i == 0)
        def _():
            loss_ref[...] = jnp.zeros_like(loss_ref)
            dgple_ref[...] = jnp.zeros_like(dgple_ref)
            dgfp_ref[...] = jnp.zeros_like(dgfp_ref)

        h2 = h2_ref[...]
        h2b = h2.astype(BF16)
        h2b_ref[...] = h2b
        pb = p_ref[...].astype(BF16)
        pb_ref[...] = pb
        pp = _nt(pb, wple_ref[...])
        gple = gple_ref[...]
        e = pp * _rstd(pp) * gple
        wpg = wpg_ref[...].reshape(d, d)
        sg = jax.nn.sigmoid(_nn(h2b, wpg))
        diff = h2 + sg * e - t_ref[...]
        sq = jnp.sum(jnp.sum(diff * diff, axis=1, keepdims=True), axis=0, keepdims=True)
        loss_ref[...] += jnp.broadcast_to(sq * (0.5 / d), loss_ref.shape)
        dh3 = diff * (1.0 / d)
        dgl = (dh3 * e * sg * (1.0 - sg)).astype(BF16)
        dgl_ref[...] = dgl
        dh2 = dh3 + _nt(dgl, wpg)
        dh2_ref[...] = dh2
        dpp, dg = _rms_bwd(pp, gple, dh3 * sg)
        dpp_ref[...] = dpp.astype(BF16)
        dgple_ref[...] += dg
        dff, dg = _rms_bwd(ff_ref[...], gfp_ref[...], dh2)
        dff_ref[...] = dff.astype(BF16)
        dgfp_ref[...] += dg

    rowblk = pl.BlockSpec((TS, d), _row)
    vec = pl.BlockSpec((1, d), _fixed)
    return pl.pallas_call(
        body, grid=(s // TS,), name="tail_fwd_bwd",
        out_shape=(jax.ShapeDtypeStruct((s, d), F32), jax.ShapeDtypeStruct((s, d), BF16),
                   jax.ShapeDtypeStruct((s, d), BF16), jax.ShapeDtypeStruct((s, d), BF16),
                   jax.ShapeDtypeStruct((s, d), BF16), jax.ShapeDtypeStruct((s, D_PLE), BF16),
                   jax.ShapeDtypeStruct((8, LANES), F32), jax.ShapeDtypeStruct((1, d), F32),
                   jax.ShapeDtypeStruct((1, d), F32)),
        in_specs=[rowblk, pl.BlockSpec((TS, D_PLE), _row), rowblk, rowblk,
                  pl.BlockSpec(wple.shape, _fixed), _spec_square(1), vec, vec],
        out_specs=(rowblk, rowblk, rowblk, rowblk, rowblk, pl.BlockSpec((TS, D_PLE), _row),
                   pl.BlockSpec((8, LANES), _fixed), vec, vec),
        compiler_params=_params(1),
    )(h2, p, tgt, ff, wple, wpg, g_ple, g_ffn_post)


def _ffn_bwd(dff, gate, up, wd, wg, wu, h1, dh2, g_pre):
    s, d = h1.shape
    nc = D_FF // TN_FF
    ts = min(TS_FF, s)

    def body(dff_ref, gate_ref, up_ref, wd_ref, wg_ref, wu_ref, h1_ref, dh2_ref, g_ref,
             dgate_ref, dup_ref, dh1_ref, dg_ref, acc):
        i = pl.program_id(0)
        j = pl.program_id(1)

        @pl.when((i == 0) & (j == 0))
        def _():
            dg_ref[...] = jnp.zeros_like(dg_ref)

        @pl.when(j == 0)
        def _():
            acc[...] = jnp.zeros_like(acc)

        for r in range(2):
            rows = slice(r * (ts // 2), (r + 1) * (ts // 2))
            dact = _nt(dff_ref[rows, :], wd_ref[...].reshape(TN_FF, d))
            gt = gate_ref[rows, :].astype(F32)
            sg = jax.nn.sigmoid(gt)
            dup_ref[rows, :] = (dact * gt * sg).astype(BF16)
            dgate_ref[rows, :] = (dact * up_ref[rows, :].astype(F32) * (sg * (1.0 + gt * (1.0 - sg)))).astype(BF16)
            acc[rows, :] += (_nn(dgate_ref[rows, :], wg_ref[...].reshape(TN_FF, d))
                             + _nn(dup_ref[rows, :], wu_ref[...].reshape(TN_FF, d)))

        @pl.when(j == nc - 1)
        def _():
            dv, dg = _rms_bwd(h1_ref[...], g_ref[...], acc[...])
            dh1_ref[...] = dh2_ref[...] + dv
            dg_ref[...] += dg

    rowblk = pl.BlockSpec((ts, d), lambda i, j: (i, 0))
    chunk = pl.BlockSpec((ts, TN_FF), lambda i, j: (i, j))
    vec = pl.BlockSpec((1, d), lambda i, j: (0, 0))
    return pl.pallas_call(
        body, grid=(s // ts, nc), name="ffn_bwd",
        out_shape=(jax.ShapeDtypeStruct((s, D_FF), BF16), jax.ShapeDtypeStruct((s, D_FF), BF16),
                   jax.ShapeDtypeStruct((s, d), F32), jax.ShapeDtypeStruct((1, d), F32)),
        in_specs=[rowblk, chunk, chunk, _spec_ff(2), _spec_ff(0), _spec_ff(1), rowblk, rowblk, vec],
        out_specs=(chunk, chunk, rowblk, vec),
        scratch_shapes=[pltpu.VMEM((ts, d), F32)],
        compiler_params=_params(2),
    )(dff, gate, up, wd, wg, wu, h1, dh2, g_pre)


def _post_attn_bwd(dh1, o, a, mpre, wout, wpool, g_post, g_attn, g_pool, pscale):
    s, d = dh1.shape
    sub = TS // TQ

    def body(dh1_ref, o_ref, a_ref, mp_ref, wo_ref, wp_ref, gpost_ref, ga_ref, gp_ref, ps_ref,
             dob_ref, dat_ref, dlt_ref, dmpb_ref, dy_ref, dgpost_ref, dga_ref, dgp_ref, dps_ref):
        i = pl.program_id(0)

        @pl.when(i == 0)
        def _():
            dgpost_ref[...] = jnp.zeros_like(dgpost_ref)
            dga_ref[...] = jnp.zeros_like(dga_ref)
            dgp_ref[...] = jnp.zeros_like(dgp_ref)
            dps_ref[...] = jnp.zeros_like(dps_ref)

        do, dg = _rms_bwd(o_ref[...], gpost_ref[...], dh1_ref[...])
        dgpost_ref[...] += dg
        dob = do.astype(BF16)
        dob_ref[...] = dob
        dmix = _nt(dob, wo_ref[...].reshape(d, d))

        av = a_ref[...]
        da, dg = _rms_bwd(av, ga_ref[...], dmix[:, 0:D_ATTN])
        dga_ref[...] += dg
        dat = da.astype(BF16).T
        hsel = (lax.shift_right_logical(lax.broadcasted_iota(jnp.int32, (HEADS, D_ATTN), 1), 6)
                == lax.broadcasted_iota(jnp.int32, (HEADS, D_ATTN), 0)).astype(F32)
        dlt = lax.dot_general(hsel, da * av, (((1,), (1,)), ((), ())), precision=HIGHEST, preferred_element_type=F32)
        for q in range(sub):
            dlt_ref[q] = dlt[:, q * TQ:(q + 1) * TQ]
            dat_ref[q] = dat[:, q * TQ:(q + 1) * TQ]

        ps = ps_ref[...]
        mp = mp_ref[...]
        dm, dg = _rms_bwd(mp * ps, gp_ref[...], dmix[:, D_ATTN:])
        dgp_ref[...] += dg
        dps_ref[...] += jnp.sum(dm * mp, axis=0, keepdims=True)
        dmpb = (dm * ps).astype(BF16)
        dmpb_ref[...] = dmpb
        for g in range(len(POOL_WINDOWS)):
            cols = slice(g * POOL_CH, (g + 1) * POOL_CH)
            dy_ref[:, cols] = _nt(dmpb[:, cols], wp_ref[g])

    rowblk = pl.BlockSpec((TS, d), _row)
    half = pl.BlockSpec((TS, D_ATTN), _row)
    vec = lambda n: pl.BlockSpec((1, n), _fixed)
    return pl.pallas_call(
        body, grid=(s // TS,), name="post_attn_bwd",
        out_shape=(jax.ShapeDtypeStruct((s, d), BF16), jax.ShapeDtypeStruct((s // TQ, D_ATTN, TQ), BF16),
                   jax.ShapeDtypeStruct((s // TQ, HEADS, TQ), F32), jax.ShapeDtypeStruct((s, D_POOL), BF16),
                   jax.ShapeDtypeStruct((s, D_POOL), F32), jax.ShapeDtypeStruct((1, d), F32),
                   jax.ShapeDtypeStruct((1, D_ATTN), F32), jax.ShapeDtypeStruct((1, D_POOL), F32),
                   jax.ShapeDtypeStruct((1, D_POOL), F32)),
        in_specs=[rowblk, rowblk, half, half, _spec_square(0),
                  pl.BlockSpec(wpool.shape, lambda i: (0, 0, 0)), vec(d), vec(D_ATTN), vec(D_POOL), vec(D_POOL)],
        out_specs=(rowblk, pl.BlockSpec((sub, D_ATTN, TQ), lambda i: (i, 0, 0)),
                   pl.BlockSpec((sub, HEADS, TQ), lambda i: (i, 0, 0)), half, half,
                   vec(d), vec(D_ATTN), vec(D_POOL), vec(D_POOL)),
        compiler_params=_params(1),
    )(dh1, o, a, mpre, wout, wpool, g_post, g_attn, g_pool, pscale)


def _attn_bwd(ka, v, kt3, qat3, qt3, dot3, lset3, dlt3, chip_blocks):
    s = ka.shape[0]
    nq = s // TQ

    def body(ka_ref, v_ref, kt_ref, qat_ref, qt_ref, dot_ref, lset_ref, dlt_ref, b_ref,
             dqt_ref, dkt_ref, dvt_ref, got_ref, pt_scr, ptb_scr, dsb_scr,
             stage, send_sems, recv_sems, local_sem):
        j = pl.program_id(0)

        @pl.when(j == 0)
        def _():
            _chips_start(b_ref, got_ref, stage, send_sems, recv_sems, local_sem)
            dqt_ref[...] = jnp.zeros_like(dqt_ref)

        def tile(i, masked):
            def accumulate(ref, idx, val):
                if masked:
                    ref[idx] = val
                else:
                    ref[idx] += val

            for h in range(HEADS):
                aug = slice(h * AUG, (h + 1) * AUG)
                st = _nn(ka_ref[:, aug], qat_ref[i, aug, :]) - lset_ref[i, h:h + 1, :]
                if masked:
                    st = jnp.where(_causal_in_tile(), st, NEG)
                pt = jnp.exp2(st)
                pt_scr[h] = pt
                ptb_scr[h] = pt.astype(BF16)
            heads = [(h, slice(h * HEAD_DIM, (h + 1) * HEAD_DIM)) for h in range(HEADS)]
            for h, hs in heads:
                dst = pt_scr[h] * (_nn(v_ref[:, hs], dot_ref[i, hs, :]) - dlt_ref[i, h:h + 1, :])
                dsb_scr[h] = dst.astype(BF16)
            for h, hs in heads:
                accumulate(dvt_ref, (0, hs, slice(None)), _nt(dot_ref[i, hs, :], ptb_scr[h]))
            for h, hs in heads:
                rows = slice(h * VROWS, (h + 1) * VROWS)
                accumulate(dkt_ref, (0, rows, slice(None)), _nt(qt_ref[i, rows, :], dsb_scr[h]))
            for h, hs in heads:
                rows = slice(h * VROWS, (h + 1) * VROWS)
                dqt_ref[i, rows, :] += _nn(kt_ref[0, rows, :], dsb_scr[h])

        first = j + 1
        pairs = (nq - first) // 2

        def step(p, carry):
            tile(first + 2 * p, False)
            tile(first + 2 * p + 1, False)
            return carry

        tile(j, True)
        lax.fori_loop(0, pairs, step, 0)

        @pl.when(first + 2 * pairs < nq)
        def _():
            tile(nq - 1, False)

        @pl.when(j == nq - 1)
        def _():
            _chips_finish(b_ref, got_ref, send_sems, recv_sems)

    blk = pl.BlockSpec((TQ, D_ATTN), _row)
    tile_t = lambda rows: pl.BlockSpec((1, rows, TQ), lambda j: (j, 0, 0))
    per_tile = lambda rows: jax.ShapeDtypeStruct((nq, rows, TQ), F32)
    _, r, cdim = chip_blocks.shape
    return pl.pallas_call(
        body, grid=(nq,), name="attn_bwd",
        out_shape=(per_tile(HEADS * VROWS), per_tile(HEADS * VROWS), per_tile(D_ATTN),
                   jax.ShapeDtypeStruct(chip_blocks.shape, chip_blocks.dtype)),
        in_specs=[pl.BlockSpec((TQ, HEADS * AUG), _row), blk, tile_t(HEADS * VROWS),
                  VMEM_WHOLE, VMEM_WHOLE, VMEM_WHOLE, VMEM_WHOLE, VMEM_WHOLE, ANY],
        out_specs=(pl.BlockSpec((nq, HEADS * VROWS, TQ), lambda j: (0, 0, 0)), tile_t(HEADS * VROWS), tile_t(D_ATTN),
                   ANY),
        scratch_shapes=[pltpu.VMEM((HEADS, TQ, TQ), F32), pltpu.VMEM((HEADS, TQ, TQ), BF16),
                        pltpu.VMEM((HEADS, TQ, TQ), BF16), pltpu.VMEM((r, cdim), chip_blocks.dtype),
                        pltpu.SemaphoreType.DMA((3,)), pltpu.SemaphoreType.DMA((3,)), pltpu.SemaphoreType.DMA],
        compiler_params=_params(1),
    )(ka, v, kt3, qat3, qt3, dot3, lset3, dlt3, chip_blocks)


def _pre_attn_bwd(dqt3, dkt3, dvt3, dcs, drs, fl, dy, x, dh1, g1, wqkv, wf, wu):
    s, d = x.shape
    nt = s // TS
    n = TS + HALO
    sub = TS // TQ
    qkv, fcols = 3 * D_ATTN, 3 * D_ATTN + LANES

    def body(dqt_ref, dkt_ref, dvt_ref, dcs_ref, drs_ref, fl_ref, dy_ref, x_ref, dh1_ref, g_ref, wqkv_ref, wf_ref, wu_ref,
             gx_ref, dz_ref, dg_ref, db_ref, ybuf, ccar, dlog):
        dqkv_ref = dz_ref.at[:, 0:qkv]
        dfb_ref = dz_ref.at[:, qkv:fcols]
        dub_ref = dz_ref.at[:, fcols:]
        i = pl.program_id(0)
        ti = nt - 1 - i

        @pl.when(i == 0)
        def _():
            ybuf[TS:n, :] = jnp.zeros((HALO, D_POOL), F32)
            ccar[...] = jnp.zeros_like(ccar)
            dg_ref[...] = jnp.zeros_like(dg_ref)
            db_ref[...] = jnp.zeros_like(db_ref)

        rr = lax.broadcasted_iota(jnp.int32, (TS, TS), 0)
        cc = lax.broadcasted_iota(jnp.int32, (TS, TS), 1)
        dlog[...] = ccar[...] + _mask_matmul((cc >= rr).astype(BF16), drs_ref[...] - dcs_ref[...])
        ccar[...] = dlog[0:1, :]
        df = dlog[...] * jax.nn.sigmoid(-fl_ref[...])
        db_ref[...] += jnp.sum(df, axis=0, keepdims=True)
        dfb = df.astype(BF16)
        dfb_ref[...] = dfb

        t = ti * TS + lax.broadcasted_iota(jnp.int32, (TS, 1), 0)
        dy = dy_ref[...]
        for g, w in enumerate(POOL_WINDOWS):
            cols = slice(g * POOL_CH, (g + 1) * POOL_CH)
            ybuf[0:TS, cols] = dy[:, cols] / jnp.minimum(t + 1, w).astype(F32)
        for g, w in enumerate(POOL_WINDOWS):
            cols = slice(g * POOL_CH, (g + 1) * POOL_CH)
            sm = ybuf[:, cols]
            step = 1
            while step < w:
                sm = sm + pltpu.roll(sm, n - step, 0)
                step *= 2
            dub_ref[:, cols] = (sm[0:TS, :] - dy[:, cols]).astype(BF16)
        ybuf[TS:n, :] = ybuf[0:HALO, :]

        for a in range(sub):
            rows = slice(a * TQ, (a + 1) * TQ)
            for h in range(HEADS):
                src = slice(h * VROWS, h * VROWS + HEAD_DIM)
                dqkv_ref[rows, h * HEAD_DIM:(h + 1) * HEAD_DIM] = (dqt_ref[a, src, :].T * 0.125).astype(BF16)
                dqkv_ref[rows, D_ATTN + h * HEAD_DIM:D_ATTN + (h + 1) * HEAD_DIM] = dkt_ref[a, src, :].T.astype(BF16)
            dqkv_ref[rows, 2 * D_ATTN:] = dvt_ref[a].T.astype(BF16)
        dhn = _nn(dqkv_ref[...], wqkv_ref[...]) + _nn(dfb, wf_ref[...]) + _nn(dub_ref[...], wu_ref[...])
        dx, dg = _rms_bwd(x_ref[...], g_ref[...], dhn)
        gx_ref[...] = dh1_ref[...] + dx
        dg_ref[...] += dg

    rev = lambda i: (nt - 1 - i, 0)
    blk = lambda w: pl.BlockSpec((TS, w), rev)
    return pl.pallas_call(
        body, grid=(nt,), name="pre_attn_bwd",
        out_shape=(jax.ShapeDtypeStruct((s, d), F32), jax.ShapeDtypeStruct((s, fcols + D_POOL), BF16),
                   jax.ShapeDtypeStruct((1, d), F32), jax.ShapeDtypeStruct((1, LANES), F32)),
        in_specs=[pl.BlockSpec((sub, HEADS * VROWS, TQ), lambda i: (nt - 1 - i, 0, 0)),
                  pl.BlockSpec((sub, HEADS * VROWS, TQ), lambda i: (nt - 1 - i, 0, 0)),
                  pl.BlockSpec((sub, D_ATTN, TQ), lambda i: (nt - 1 - i, 0, 0)),
                  blk(LANES), blk(LANES), blk(LANES), blk(D_POOL), blk(d), blk(d),
                  pl.BlockSpec((1, d), _fixed), pl.BlockSpec((qkv, d), _fixed), pl.BlockSpec(wf.shape, _fixed),
                  pl.BlockSpec(wu.shape, _fixed)],
        out_specs=(blk(d), blk(fcols + D_POOL), pl.BlockSpec((1, d), _fixed), pl.BlockSpec((1, LANES), _fixed)),
        scratch_shapes=[pltpu.VMEM((n, D_POOL), F32), pltpu.VMEM((1, LANES), F32), pltpu.VMEM((TS, LANES), F32)],
        compiler_params=_params(1),
    )(dqt3, dkt3, dvt3, dcs, drs, fl, dy, x, dh1, g1, wqkv, wf, wu)


def _wgrad(a, b, out_dtype, name):
    s, m = a.shape
    n = b.shape[1]
    tm = max(t for t in range(LANES, min(m, TM_WGRAD) + 1, LANES) if m % t == 0)
    ts = min(TS_WGRAD, s)
    ns = s // ts

    def body(a_ref, b_ref, o_ref, acc):
        i = pl.program_id(1)

        @pl.when(i == 0)
        def _():
            acc[...] = jnp.zeros_like(acc)

        acc[...] += _tn(a_ref[...], b_ref[...])

        @pl.when(i == ns - 1)
        def _():
            o_ref[...] = acc[...].astype(out_dtype)

    return pl.pallas_call(
        body, grid=(m // tm, ns), name=name, out_shape=jax.ShapeDtypeStruct((m, n), out_dtype),
        in_specs=[pl.BlockSpec((ts, tm), lambda j, i: (i, j)), pl.BlockSpec((ts, n), lambda j, i: (i, 0))],
        out_specs=pl.BlockSpec((tm, n), lambda j, i: (j, 0)),
        scratch_shapes=[pltpu.VMEM((tm, n), F32)],
        compiler_params=_params(2),
    )(a, b)


def _adamw(w, g, m, v):
    m = ADAM_B1 * m + (1.0 - ADAM_B1) * g
    v = ADAM_B2 * v + (1.0 - ADAM_B2) * (g * g)
    m_hat = m / (1.0 - ADAM_B1 ** ADAM_STEP)
    v_hat = v / (1.0 - ADAM_B2 ** ADAM_STEP)
    delta = -ADAM_LR * (m_hat / (jnp.sqrt(v_hat) + ADAM_EPS) + ADAM_WD * w)
    return delta, m, v


def _sum_parts(p_ref):
    g = p_ref[0].astype(F32)
    for k in range(1, p_ref.shape[0]):
        g = g + p_ref[k].astype(F32)
    return g


def _adamw_store(g, w_ref, m_ref, v_ref, g_ref, d_ref, nm_ref, nv_ref):
    drop = lambda ref: ref.at[0] if len(ref.shape) == 3 else ref
    g_ref, d_ref, nm_ref, nv_ref = drop(g_ref), drop(d_ref), drop(nm_ref), drop(nv_ref)
    g_ref[...] = g
    d_ref[...], nm_ref[...], nv_ref[...] = _adamw(drop(w_ref)[...], g, drop(m_ref)[...], drop(v_ref)[...])


def _reduce_update_rest(parts, states, chip_blocks, small_block):
    nk, r, c = parts.shape
    nch = UPDATE_CHUNKS
    ck = c // nch
    ns = (r // SHARD_FF) * nch
    group_of = [0, 0, 1, 2, 3, 0]
    n_in, n_out = 3 * len(states), 4 * len(states)

    def body(p_ref, *refs):
        ins, (b_ref, sm_ref) = refs[:n_in], refs[n_in:n_in + 2]
        outs, (got_ref, all_ref) = refs[n_in + 2:n_in + 2 + n_out], refs[n_in + 2 + n_out:n_in + 4 + n_out]
        stage_b, stage_s, send_b, recv_b, local_b, send_s, recv_s, local_s = refs[n_in + 4 + n_out:]
        i = pl.program_id(0)

        @pl.when(i == 0)
        def _():
            _chips_start(b_ref, got_ref, stage_b, send_b, recv_b, local_b)
            _gather_start(sm_ref, all_ref, stage_s, send_s, recv_s, local_s)

        g = _sum_parts(p_ref)
        update = lambda k, gk: _adamw_store(gk, *ins[3 * k:3 * k + 3], *outs[4 * k:4 * k + 4])

        @pl.when(i // nch == 0)
        def _():
            update(0, g[0:OFF_PG])
            update(1, g[OFF_PG:OFF_PLE])
            update(5, g[OFF_PLE:OFF_PLE + ROWS_PLE])

        for k in (2, 3):
            @pl.when(i // nch == group_of[k])
            def _():
                update(k, g.T)

        @pl.when(i // nch == 3)
        def _():
            update(4, g)

        @pl.when(i == ns - 1)
        def _():
            _gather_pass_on(all_ref, send_s, recv_s)
            _chips_finish(b_ref, got_ref, send_b, recv_b)
            _gather_finish(sm_ref, all_ref, send_s, recv_s)

    def spec(k, a):
        chunk = lambda i: jnp.clip(i - group_of[k] * nch, 0, nch - 1)
        if a.ndim == 2:
            return pl.BlockSpec((a.shape[0], ck), lambda i: (0, chunk(i)))
        if a.shape[2] == c:
            return pl.BlockSpec((1, a.shape[1], ck), lambda i: (0, 0, chunk(i)))
        return pl.BlockSpec((1, ck, a.shape[2]), lambda i: (0, chunk(i), 0))

    specs = [spec(k, st[0]) for k, st in enumerate(states)]
    dma = pltpu.SemaphoreType.DMA
    res = pl.pallas_call(
        body, grid=(ns,), name="reduce_update_rest",
        out_shape=tuple(jax.ShapeDtypeStruct(st[0].shape, F32) for st in states for _ in range(4))
        + (jax.ShapeDtypeStruct(chip_blocks.shape, chip_blocks.dtype),
           jax.ShapeDtypeStruct((N_DEV,) + small_block.shape, small_block.dtype)),
        in_specs=[pl.BlockSpec((nk, SHARD_FF, ck), lambda i: (0, i // nch, i % nch))]
        + [sp for sp in specs for _ in range(3)] + [ANY, ANY],
        out_specs=tuple(sp for sp in specs for _ in range(4)) + (ANY, ANY),
        scratch_shapes=[pltpu.VMEM(chip_blocks.shape[1:], chip_blocks.dtype), pltpu.VMEM(small_block.shape, small_block.dtype),
                        dma((3,)), dma((3,)), dma, dma((7,)), dma((7,)), dma],
        compiler_params=_params(1),
    )(parts, *[a for st in states for a in st], chip_blocks, small_block)
    return [list(res[4 * k:4 * k + 4]) for k in range(len(states))], res[n_out], res[n_out + 1]


def _reduce_update_in(parts, w, m, v):
    nk, r, c = parts.shape
    ck = c // UPDATE_CHUNKS

    def body(p_ref, w_ref, m_ref, v_ref, g_ref, d_ref, nm_ref, nv_ref):
        g = _sum_parts(p_ref).T[:, 0:SHARD_IN]
        _adamw_store(g, w_ref, m_ref, v_ref, g_ref, d_ref, nm_ref, nv_ref)

    blk = pl.BlockSpec((1, ck, SHARD_IN), lambda i: (0, i, 0))
    return pl.pallas_call(
        body, grid=(UPDATE_CHUNKS,), name="reduce_update_in", out_shape=(jax.ShapeDtypeStruct(w.shape, F32),) * 4,
        in_specs=[pl.BlockSpec((nk, r, ck), lambda i: (0, 0, i)), blk, blk, blk],
        out_specs=(blk,) * 4, compiler_params=_params(1),
    )(parts, w, m, v)


def _reduce_update_small(parts, w, m, v):
    nd = parts.shape[0]

    def body(p_ref, w_ref, m_ref, v_ref, g_ref, d_ref, nm_ref, nv_ref):
        g = p_ref[0]
        for k in range(1, nd):
            g = g + p_ref[k]
        g_ref[...] = g
        d_ref[...], nm_ref[...], nv_ref[...] = _adamw(w_ref[...], g, m_ref[...], v_ref[...])

    out = jax.ShapeDtypeStruct(w.shape, F32)
    return pl.pallas_call(body, name="reduce_update_small", out_shape=(out,) * 4,
                          compiler_params=pltpu.CompilerParams(vmem_limit_bytes=VMEM_LIMIT))(parts, w, m, v)


MESH = pl.DeviceIdType.MESH


def _copy_through_vmem(src_hbm, dst_hbm, stage, sem):
    load = pltpu.make_async_copy(src_hbm, stage, sem)
    load.start()
    load.wait()
    store = pltpu.make_async_copy(stage, dst_hbm, sem)
    store.start()
    store.wait()


class _GatherPlan:
    def __init__(self, x_ref, out_ref, send_sems, recv_sems):
        x, y, c = lax.axis_index("x"), lax.axis_index("y"), lax.axis_index("c")
        self.me, self.sibling, self.c = (x, y, c), (x, y, 1 - c), c
        self.chips = [(1 - x, y), (x, 1 - y), (1 - x, 1 - y)]
        self.x_ref, self.out_ref, self.send_sems, self.recv_sems = x_ref, out_ref, send_sems, recv_sems

    def slot(self, px, py, pc):
        return self.out_ref.at[4 * px + 2 * py + pc]

    def copy(self, k, block, to, src=None):
        return pltpu.make_async_remote_copy(
            src_ref=self.slot(*block) if src is None else src, dst_ref=self.slot(*block),
            send_sem=self.send_sems.at[k], recv_sem=self.recv_sems.at[k], device_id=to, device_id_type=MESH)

    def first(self):
        return [self.copy(0, self.me, self.sibling, src=self.x_ref)] + [
            self.copy(1 + j, self.me, (*chip, self.c), src=self.x_ref) for j, chip in enumerate(self.chips)]

    def passed(self):
        return [self.copy(4 + j, (*chip, self.c), self.sibling) for j, chip in enumerate(self.chips)]


def _gather_start(x_ref, out_ref, stage, send_sems, recv_sems, local_sem):
    plan = _GatherPlan(x_ref, out_ref, send_sems, recv_sems)
    for cp in plan.first():
        cp.start()
    _copy_through_vmem(x_ref, plan.slot(*plan.me), stage, local_sem)


def _gather_pass_on(out_ref, send_sems, recv_sems):
    plan = _GatherPlan(None, out_ref, send_sems, recv_sems)
    passed = plan.passed()
    for j, chip in enumerate(plan.chips):
        plan.copy(1 + j, (*chip, plan.c), plan.me).wait_recv()
        passed[j].start()


def _gather_finish(x_ref, out_ref, send_sems, recv_sems):
    plan = _GatherPlan(x_ref, out_ref, send_sems, recv_sems)
    plan.copy(0, plan.sibling, plan.me).wait_recv()
    for j, chip in enumerate(plan.chips):
        plan.copy(4 + j, (*chip, 1 - plan.c), plan.me).wait_recv()
    for cp in plan.first() + plan.passed():
        cp.wait_send()


def _all_gather(xs, name):
    r, cdim = xs.shape

    def body(x_ref, out_ref, stage, send_sems, recv_sems, local_sem):
        _gather_start(x_ref, out_ref, stage, send_sems, recv_sems, local_sem)
        _gather_pass_on(out_ref, send_sems, recv_sems)
        _gather_finish(x_ref, out_ref, send_sems, recv_sems)

    return pl.pallas_call(
        body, name=name, out_shape=jax.ShapeDtypeStruct((N_DEV, r, cdim), xs.dtype),
        in_specs=[ANY], out_specs=ANY,
        scratch_shapes=[pltpu.VMEM((r, cdim), xs.dtype), pltpu.SemaphoreType.DMA((7,)), pltpu.SemaphoreType.DMA((7,)),
                        pltpu.SemaphoreType.DMA],
        compiler_params=pltpu.CompilerParams(vmem_limit_bytes=VMEM_LIMIT),
    )(xs)


def _rs_pair_sum(core, pieces, offsets, rows, name):
    cdim = pieces[0].shape[2]
    nk = N_DEV // 2
    npc = len(pieces)
    spans = [(o, t.shape[1]) for t, o in zip(pieces, offsets)]
    ends = [o + n for o, n in spans]
    gaps = [(a, b - a) for a, b in zip(ends, [o for o, _ in spans[1:]] + [rows]) if b > a]

    def body(core_ref, *refs):
        own, src, o_ref = refs[:npc], refs[npc:2 * npc], refs[2 * npc]
        landing, send_sems, recv_sems = refs[2 * npc + 1:]
        k = pl.program_id(0)
        x, y, c = lax.axis_index("x"), lax.axis_index("y"), lax.axis_index("c")

        def copies(kk):
            return [pltpu.make_async_remote_copy(
                src_ref=src[p].at[2 * kk + (1 - c)], dst_ref=landing.at[kk, pl.ds(o, n)],
                send_sem=send_sems.at[kk, p], recv_sem=recv_sems.at[kk, p], device_id=(x, y, 1 - c),
                device_id_type=MESH) for p, (o, n) in enumerate(spans)]

        @pl.when(k == 0)
        def _():
            for kk in range(nk):
                for cp in copies(kk):
                    cp.start()

        for cp, piece, (o, n) in zip(copies(k), own, spans):
            cp.wait_recv()
            o_ref[0, o:o + n, :] = (piece[0].astype(F32) + landing[k, o:o + n, :].astype(F32)).astype(BF16)
        for o, n in gaps:
            o_ref[0, o:o + n, :] = jnp.zeros((n, cdim), BF16)

        @pl.when(k == nk - 1)
        def _():
            for kk in range(nk):
                for cp in copies(kk):
                    cp.wait_send()

    own_specs = [pl.BlockSpec((1, n, cdim), lambda k, core_ref: (2 * k + core_ref[0], 0, 0)) for _, n in spans]
    return pl.pallas_call(
        body, name=name, out_shape=jax.ShapeDtypeStruct((nk, rows, cdim), BF16),
        grid_spec=pltpu.PrefetchScalarGridSpec(
            num_scalar_prefetch=1, grid=(nk,),
            in_specs=own_specs + [ANY] * npc,
            out_specs=pl.BlockSpec((1, rows, cdim), lambda k, core_ref: (k, 0, 0)),
            scratch_shapes=[pltpu.VMEM((nk, rows, cdim), BF16), pltpu.SemaphoreType.DMA((nk, npc)),
                            pltpu.SemaphoreType.DMA((nk, npc))]),
        compiler_params=_params(1),
    )(core, *pieces, *pieces)


def _chips_start(b_ref, out_ref, stage, send_sems, recv_sems, local_sem):
    x, y, c = lax.axis_index("x"), lax.axis_index("y"), lax.axis_index("c")
    mychip = 2 * x + y
    for j, (px, py) in enumerate([(1 - x, y), (x, 1 - y), (1 - x, 1 - y)]):
        pltpu.make_async_remote_copy(
            src_ref=b_ref.at[2 * px + py], dst_ref=out_ref.at[mychip],
            send_sem=send_sems.at[j], recv_sem=recv_sems.at[j], device_id=(px, py, c), device_id_type=MESH).start()
    _copy_through_vmem(b_ref.at[mychip], out_ref.at[mychip], stage, local_sem)


def _chips_finish(b_ref, out_ref, send_sems, recv_sems):
    x, y, c = lax.axis_index("x"), lax.axis_index("y"), lax.axis_index("c")
    for j, (px, py) in enumerate([(1 - x, y), (x, 1 - y), (1 - x, 1 - y)]):
        pltpu.make_async_remote_copy(
            src_ref=b_ref.at[2 * px + py], dst_ref=out_ref.at[2 * px + py],
            send_sem=send_sems.at[j], recv_sem=recv_sems.at[j], device_id=(px, py, c), device_id_type=MESH).wait()


def _pad_rows(a, rows):
    return jnp.pad(a, ((0, rows - a.shape[0]), (0, 0)))


def _pack_in(w_in):
    return _pad_rows(w_in[0].T, ROWS_IN)


def _pack_ple(w_ple):
    return w_ple[0].T.reshape(ROWS_PLE, D_MODEL)


def _unpack_ple(r):
    return r.reshape(D_MODEL // N_DEV, D_PLE).T[None]


def _pack_rest(w_out, w_pg, w_gate, w_up, w_down, w_ple):
    head = _pad_rows(jnp.concatenate([w_out[0], w_pg[0], _pack_ple(w_ple)], axis=0), OFF_GATE)
    return head, jnp.concatenate([w_gate[0].T, w_up[0].T, w_down[0]], axis=0)


def _pack_small(w_pool, g_mix_pre, g_mix_post, g_ffn_pre, g_ffn_post, g_ple, g_attn, g_pool, pool_scale, b_forget,
                loss=None):
    def row(vrow):
        return jnp.pad(vrow.reshape(1, -1), ((0, 0), (0, D_MODEL - vrow.size)))
    rows = [w_pool.reshape(64, D_MODEL), row(g_mix_pre), row(g_mix_post), row(g_ffn_pre), row(g_ffn_post), row(g_ple),
            row(g_attn), row(g_pool), row(pool_scale), row(b_forget),
            row(loss) if loss is not None else jnp.zeros((1, D_MODEL), F32)]
    return _pad_rows(jnp.concatenate(rows, axis=0), SMALL_ROWS)


def _unpack_small(r):
    return dict(
        w_pool=r[0:64].reshape(1, 4, POOL_CH, POOL_CH), g_mix_pre=r[ROW_G_MIX_PRE:ROW_G_MIX_PRE + 1],
        g_mix_post=r[ROW_G_MIX_POST:ROW_G_MIX_POST + 1], g_ffn_pre=r[ROW_G_FFN_PRE:ROW_G_FFN_PRE + 1],
        g_ffn_post=r[ROW_G_FFN_POST:ROW_G_FFN_POST + 1], g_ple=r[ROW_G_PLE:ROW_G_PLE + 1],
        g_attn_grp=r[ROW_G_ATTN:ROW_G_ATTN + 1, 0:D_ATTN], g_pool_grp=r[ROW_G_POOL:ROW_G_POOL + 1, 0:D_POOL],
        pool_scale=r[ROW_POOL_SCALE:ROW_POOL_SCALE + 1, 0:D_POOL], b_forget=r[ROW_B_FORGET:ROW_B_FORGET + 1, 0:HEADS])


def _step(x, p, tgt, small, state_in, states_rest):
    core = lax.axis_index("c").astype(jnp.int32).reshape(1)
    as_bf = lambda a: a.astype(BF16)
    win_t = _all_gather(_pack_in(as_bf(state_in[0])), "gather_w_in")[:, 0:SHARD_IN].reshape(D_IN, D_MODEL)
    wqkv = win_t
    wf = _pad_rows(win_t[3 * D_ATTN:3 * D_ATTN + HEADS], LANES)
    wu = win_t[3 * D_ATTN + HEADS:]
    wpool = small["w_pool"].astype(BF16)
    bpad = jnp.pad(small["b_forget"], ((0, 0), (0, LANES - HEADS)))

    lay = _attn_layout_constants()
    head_b, ffn_b = _pack_rest(*[as_bf(st[0]) for st in states_rest])
    hn, qt3, ka, v, qat3, vt3, kt3, fl, y, mpre, gh = _pre_attn_fwd(x, small["g_mix_pre"], wqkv, wf, wu, bpad, wpool, lay,
                                                                 head_b)
    a, lset3, gf = _attn_fwd(ka, qat3, vt3, ffn_b)
    wple_t = gh[:, OFF_PLE:OFF_PLE + ROWS_PLE].reshape(D_MODEL, D_PLE)
    mix, o, h1, hn2 = _post_attn_fwd(a, mpre, x, small["g_attn_grp"], small["g_pool_grp"], small["pool_scale"], gh,
                                     small["g_mix_post"], small["g_ffn_pre"])
    gate, up, act, ff, h2 = _ffn_fwd(hn2, gf, gf, gf, h1, small["g_ffn_post"])
    dh2, dff, dgl, dpp, h2b, pb, loss8, dg_ple, dg_ffn_post = _tail_fwd_bwd(
        h2, p, tgt, ff, wple_t, gh, small["g_ple"], small["g_ffn_post"])
    dgate, dup, dh1, dg_ffn_pre = _ffn_bwd(dff, gate, up, gf, gf, gf, h1, dh2, small["g_ffn_pre"])
    dob, dat3, dlt3, dmpb, dy, dg_mix_post, dg_attn, dg_pool, dps = _post_attn_bwd(
        dh1, o, a, mpre, gh, wpool, small["g_mix_post"], small["g_attn_grp"], small["g_pool_grp"], small["pool_scale"])

    nd = N_DEV
    send_rest = [
        _wgrad(mix, dob, BF16, "wgrad_out").reshape(nd, 128, D_MODEL),
        _wgrad(h2b, dgl, BF16, "wgrad_ple_gate").reshape(nd, 128, D_MODEL),
        _wgrad(dpp, pb, BF16, "wgrad_ple").reshape(nd, 32, D_MODEL),
        _wgrad(dgate, hn2, BF16, "wgrad_gate").reshape(nd, SHARD_FF, D_MODEL),
        _wgrad(dup, hn2, BF16, "wgrad_up").reshape(nd, SHARD_FF, D_MODEL),
        _wgrad(act, dff, BF16, "wgrad_down").reshape(nd, SHARD_FF, D_MODEL)]
    pair_rest = _rs_pair_sum(core, send_rest, [0, OFF_PG, OFF_PLE, OFF_GATE, OFF_UP, OFF_DOWN], ROWS_REST,
                             "rs_pair_sum_rest")

    dqt3, dkt3, dvt3, chips_rest = _attn_bwd(ka, v, kt3, qat3, qt3, dat3, lset3, dlt3, pair_rest)

    def sums_per_token(t3):
        rows = t3.reshape(-1, HEADS, VROWS, TQ)[:, :, HEAD_DIM, :]
        return jnp.pad(rows.transpose(0, 2, 1).reshape(-1, HEADS), ((0, 0), (0, LANES - HEADS)))

    drs, dcs = sums_per_token(dqt3), sums_per_token(dkt3)
    gx, dz, dg_mix_pre, db = _pre_attn_bwd(dqt3, dkt3, dvt3, dcs, drs, fl, dy, x, dh1, small["g_mix_pre"], wqkv, wf, wu)

    dwz = _wgrad(dz, hn, BF16, "wgrad_in")
    dwin_t = jnp.concatenate([dwz[0:3 * D_ATTN], dwz[3 * D_ATTN:3 * D_ATTN + HEADS], dwz[3 * D_ATTN + LANES:]], axis=0)
    send_in = jnp.pad(dwin_t.reshape(nd, SHARD_IN, D_MODEL), ((0, 0), (0, ROWS_IN - SHARD_IN), (0, 0)))
    pair_in = _rs_pair_sum(core, [send_in], [0], ROWS_IN, "rs_pair_sum_in")

    dwp = _wgrad(y, dmpb, F32, "wgrad_pool")
    dw_pool = jnp.stack([dwp[g * POOL_CH:(g + 1) * POOL_CH, g * POOL_CH:(g + 1) * POOL_CH] for g in range(4)])
    small_part = _pack_small(dw_pool, dg_mix_pre, dg_mix_post, dg_ffn_pre, dg_ffn_post, dg_ple, dg_attn, dg_pool, dps,
                             db[:, 0:HEADS], loss8[0:1, 0:1])

    packed = states_rest[:5] + [tuple(_pack_ple(a) for a in states_rest[5])]
    upd_rest, chips_in, small_all = _reduce_update_rest(chips_rest, packed, pair_in, small_part)
    upd_rest[5] = [_unpack_ple(a) for a in upd_rest[5]]
    upd_in = _reduce_update_in(chips_in, *state_in)
    return gx, small_all, upd_in, upd_rest


def kernel(x, p, g_mix_pre, w_in, b_forget, g_attn_grp, g_pool_grp, w_pool, pool_scale, w_out, g_mix_post, g_ffn_pre, w_ffn_gate, w_ffn_up, w_ffn_down, g_ffn_post, w_ple_proj, g_ple, w_ple_gate, loss_target, m_g_mix_pre, m_w_in, m_b_forget, m_g_attn_grp, m_g_pool_grp, m_w_pool, m_pool_scale, m_w_out, m_g_mix_post, m_g_ffn_pre, m_w_ffn_gate, m_w_ffn_up, m_w_ffn_down, m_g_ffn_post, m_w_ple_proj, m_g_ple, m_w_ple_gate, v_g_mix_pre, v_w_in, v_b_forget, v_g_attn_grp, v_g_pool_grp, v_w_pool, v_pool_scale, v_w_out, v_g_mix_post, v_g_ffn_pre, v_w_ffn_gate, v_w_ffn_up, v_w_ffn_down, v_g_ffn_post, v_w_ple_proj, v_g_ple, v_w_ple_gate):
    small = dict(w_pool=w_pool[0], g_mix_pre=g_mix_pre, g_mix_post=g_mix_post, g_ffn_pre=g_ffn_pre,
                 g_ffn_post=g_ffn_post, g_ple=g_ple, g_attn_grp=g_attn_grp, g_pool_grp=g_pool_grp,
                 pool_scale=pool_scale, b_forget=b_forget)
    gx, small_all, upd_in, upd_rest = _step(
        x[0], p[0, 0], loss_target[0], small, (w_in, m_w_in, v_w_in),
        [(w_out, m_w_out, v_w_out), (w_ple_gate, m_w_ple_gate, v_w_ple_gate),
         (w_ffn_gate, m_w_ffn_gate, v_w_ffn_gate), (w_ffn_up, m_w_ffn_up, v_w_ffn_up),
         (w_ffn_down, m_w_ffn_down, v_w_ffn_down), (w_ple_proj, m_w_ple_proj, v_w_ple_proj)])

    sm_w = _pack_small(w_pool, g_mix_pre, g_mix_post, g_ffn_pre, g_ffn_post, g_ple, g_attn_grp, g_pool_grp, pool_scale, b_forget)
    sm_m = _pack_small(m_w_pool, m_g_mix_pre, m_g_mix_post, m_g_ffn_pre, m_g_ffn_post, m_g_ple, m_g_attn_grp, m_g_pool_grp, m_pool_scale, m_b_forget)
    sm_v = _pack_small(v_w_pool, v_g_mix_pre, v_g_mix_post, v_g_ffn_pre, v_g_ffn_post, v_g_ple, v_g_attn_grp, v_g_pool_grp, v_pool_scale, v_b_forget)
    upd_small = _reduce_update_small(small_all, sm_w, sm_m, sm_v)
    loss = upd_small[0][ROW_LOSS, 0]

    def leaves(k):
        b_out, b_pg, b_gate, b_up, b_down, b_ple = (upd[k] for upd in upd_rest)
        s = _unpack_small(upd_small[k])
        return (s["g_mix_pre"], upd_in[k], s["b_forget"], s["g_attn_grp"], s["g_pool_grp"], s["w_pool"],
                s["pool_scale"], b_out, s["g_mix_post"], s["g_ffn_pre"], b_gate, b_up, b_down, s["g_ffn_post"], b_ple,
                s["g_ple"], b_pg)

    return (loss, gx[None], *leaves(0), *leaves(1), *leaves(2), *leaves(3))
```

```python
import functools

import jax
import jax.numpy as jnp
from jax import lax
from jax.experimental import pallas as pl
from jax.experimental.pallas import tpu as pltpu

F32 = jnp.float32
BF16 = jnp.bfloat16
HIGHEST = lax.Precision.HIGHEST

D_MODEL = 1024
HEADS = 8
HEAD_DIM = 64
D_ATTN = HEADS * HEAD_DIM
POOL_WINDOWS = (2, 4, 8, 16)
POOL_CH = 128
D_POOL = POOL_CH * len(POOL_WINDOWS)
D_FF = 2816
D_PLE = 256
D_IN = 3 * D_ATTN + HEADS + D_POOL
RMS_EPS = 1e-6
N_DEV = 8

ADAM_LR = 0.001
ADAM_B1 = 0.9
ADAM_B2 = 0.999
ADAM_EPS = 1e-08
ADAM_WD = 0.01
ADAM_STEP = 10

LANES = 128
HALO = 16
TS = 512
TS_FF = 512
TS_WGRAD = 1024
TM_WGRAD = 2176
TQ = 256
TN_FF = 1408
NEG = -1e30
VMEM_LIMIT = 56 * 1024 * 1024

SHARD_IN = 257
ROWS_IN = 272
SHARD_FF = 352
OFF_PG = 128
OFF_PLE = 256
OFF_GATE = SHARD_FF
OFF_UP = 2 * SHARD_FF
OFF_DOWN = 3 * SHARD_FF
ROWS_REST = 4 * SHARD_FF
ROWS_PLE = 32
UPDATE_CHUNKS = 4

SMALL_ROWS = 80
ROW_G_MIX_PRE, ROW_G_MIX_POST, ROW_G_FFN_PRE, ROW_G_FFN_POST, ROW_G_PLE = 64, 65, 66, 67, 68
ROW_G_ATTN, ROW_G_POOL, ROW_POOL_SCALE, ROW_B_FORGET, ROW_LOSS = 69, 70, 71, 72, 73


def _nn(a, b):
    return jnp.dot(a, b, preferred_element_type=F32)


def _nt(a, b):
    return lax.dot_general(a, b, (((1,), (1,)), ((), ())), preferred_element_type=F32)


def _tn(a, b):
    return lax.dot_general(a, b, (((0,), (0,)), ((), ())), preferred_element_type=F32)


def _rstd(v):
    return lax.rsqrt(jnp.mean(v * v, axis=-1, keepdims=True) + RMS_EPS)


def _rms_bwd(v, g, dy):
    r = _rstd(v)
    vh = v * r
    t = dy * g
    dv = r * (t - vh * jnp.mean(t * vh, axis=-1, keepdims=True))
    return dv, jnp.sum(dy * vh, axis=0, keepdims=True)


def _split3(v):
    hi = v.astype(BF16)
    rest = v - hi.astype(F32)
    mid = rest.astype(BF16)
    return hi, mid, (rest - mid.astype(F32)).astype(BF16)


def _mask_matmul(mask, v):
    hi, mid, lo = _split3(v)
    return _nn(mask, lo) + _nn(mask, mid) + _nn(mask, hi)


def _params(n_grid):
    return pltpu.CompilerParams(dimension_semantics=("arbitrary",) * n_grid, vmem_limit_bytes=VMEM_LIMIT)


def _row(i):
    return (i, 0)


def _fixed(*_):
    return (0, 0)


def _spec_square(part):
    return pl.BlockSpec((N_DEV, 128, D_MODEL), lambda *_: (0, part, 0))


def _spec_ff(part):
    return pl.BlockSpec((TN_FF // SHARD_FF, SHARD_FF, D_MODEL), lambda i, j: (j, part, 0))


assert TS == 2 * TQ and TN_FF % SHARD_FF == 0
_HALVES = (slice(0, TQ), slice(TQ, TS))

VMEM_WHOLE = pl.BlockSpec(memory_space=pltpu.VMEM)
SMEM_WHOLE = pl.BlockSpec(memory_space=pltpu.SMEM)
ANY = pl.BlockSpec(memory_space=pl.ANY)


LOG2E = 1.4426950408889634
VROWS = HEAD_DIM + 16
AUG = 128
BIAS_LANE = HEAD_DIM
ONE_LANE = HEAD_DIM + 3
SPARE_LANE = HEADS


def _attn_layout_constants():
    import numpy as np
    place = np.zeros((D_ATTN, HEADS * AUG), np.float32)
    for r in range(D_ATTN):
        place[r, (r // HEAD_DIM) * AUG + r % HEAD_DIM] = 1.0
    bias_k = np.zeros((3, LANES, HEADS * AUG), np.float32)
    bias_q = np.zeros((3, LANES, HEADS * AUG), np.float32)
    for h in range(HEADS):
        for part in range(3):
            bias_k[part, h, h * AUG + BIAS_LANE + part] = -1.0
            bias_q[part, h, h * AUG + ONE_LANE + part] = 1.0
            bias_k[0, SPARE_LANE, h * AUG + ONE_LANE + part] = 1.0
            bias_q[0, SPARE_LANE, h * AUG + BIAS_LANE + part] = 1.0
    as_bf = lambda a: jnp.asarray(a, BF16)
    return dict(place=as_bf(place), place_t=as_bf(place.T), bias_k=as_bf(bias_k),
                bias_q_t=as_bf(bias_q.transpose(0, 2, 1)))


def _pre_attn_fwd(x, g1, wqkv, wf, wu, bpad, wpool, lay, own_block):
    s, d = x.shape
    nt = s // TS
    sub = TS // TQ

    def body(x_ref, g_ref, wqkv_ref, wf_ref, wu_ref, b_ref, wp_ref, place_ref, place_t_ref, bk_ref, bqt_ref, own_ref,
             hn_ref, qt_ref, ka_ref, v_ref, qat_ref, vt_ref, kt_ref, fl_ref, y_ref, mp_ref, all_ref,
             ubuf, ccar, cbuf, stage, send_sems, recv_sems, local_sem):
        i = pl.program_id(0)

        @pl.when(i == 0)
        def _():
            _gather_start(own_ref, all_ref, stage, send_sems, recv_sems, local_sem)
            ubuf[0:HALO, :] = jnp.zeros((HALO, D_POOL), F32)
            ccar[...] = jnp.zeros_like(ccar)

        @pl.when(i == max(nt - 2, 0))
        def _():
            _gather_pass_on(all_ref, send_sems, recv_sems)

        xv = x_ref[...]
        hn = (xv * _rstd(xv) * g_ref[...]).astype(BF16)
        hn_ref[...] = hn
        zq = _nt(hn, wqkv_ref[...])
        qt = (zq[:, 0:D_ATTN] * 0.125).astype(BF16).T
        qb = (zq[:, 0:D_ATTN] * (0.125 * LOG2E)).astype(BF16)
        kb = zq[:, D_ATTN:2 * D_ATTN].astype(BF16)
        vb = zq[:, 2 * D_ATTN:3 * D_ATTN].astype(BF16)
        v_ref[...] = vb

        fl = _nt(hn, wf_ref[...]) + b_ref[...]
        fl_ref[...] = fl
        logf = jax.nn.log_sigmoid(fl)
        rr = lax.broadcasted_iota(jnp.int32, (TS, TS), 0)
        cc = lax.broadcasted_iota(jnp.int32, (TS, TS), 1)
        c = _mask_matmul((cc <= rr).astype(BF16), logf) + ccar[...]
        cbuf[...] = c
        ccar[...] = cbuf[TS - 1:TS, :]
        hi, mid, lo = _split3(c * LOG2E)
        lane = lax.broadcasted_iota(jnp.int32, (TS, LANES), 1)
        parts = (jnp.where(lane == SPARE_LANE, 1.0, hi).astype(BF16), mid, lo)
        ka = _nn(kb, place_ref[...])
        qat = _nt(place_t_ref[...], qb)
        for part in range(3):
            ka = ka + _nn(parts[part], bk_ref[part])
            qat = qat + _nt(bqt_ref[part], parts[part])
        ka_ref[...] = ka.astype(BF16)
        qat = qat.astype(BF16)
        vt = vb.T
        kt = kb.T
        for a in range(sub):
            cols = slice(a * TQ, (a + 1) * TQ)
            qat_ref[a] = qat[:, cols]
            for ref, mat in ((qt_ref, qt), (kt_ref, kt), (vt_ref, vt)):
                for h in range(HEADS):
                    ref[a, h * VROWS:h * VROWS + HEAD_DIM, :] = mat[h * HEAD_DIM:(h + 1) * HEAD_DIM, cols]
                    ref[a, h * VROWS + HEAD_DIM:(h + 1) * VROWS, :] = jnp.ones((VROWS - HEAD_DIM, TQ), BF16)

        u = _nt(hn, wu_ref[...])
        ubuf[HALO:HALO + TS, :] = u
        t = i * TS + lax.broadcasted_iota(jnp.int32, (TS, 1), 0)
        for g, w in enumerate(POOL_WINDOWS):
            cols = slice(g * POOL_CH, (g + 1) * POOL_CH)
            sm = ubuf[:, cols]
            step = 1
            while step < w:
                sm = sm + pltpu.roll(sm, step, 0)
                step *= 2
            cnt = jnp.minimum(t + 1, w).astype(F32)
            yg = (sm[HALO:, :] / cnt - u[:, cols]).astype(BF16)
            y_ref[:, cols] = yg
            mp_ref[:, cols] = _nn(yg, wp_ref[g])
        ubuf[0:HALO, :] = u[TS - HALO:, :]

        @pl.when(i == nt - 1)
        def _():
            _gather_finish(own_ref, all_ref, send_sems, recv_sems)

    nq = s // TQ
    aug = HEADS * AUG
    outs = (
        jax.ShapeDtypeStruct((s, d), BF16), jax.ShapeDtypeStruct((nq, HEADS * VROWS, TQ), BF16),
        jax.ShapeDtypeStruct((s, aug), BF16), jax.ShapeDtypeStruct((s, D_ATTN), BF16),
        jax.ShapeDtypeStruct((nq, aug, TQ), BF16), jax.ShapeDtypeStruct((nq, HEADS * VROWS, TQ), BF16),
        jax.ShapeDtypeStruct((nq, HEADS * VROWS, TQ), BF16),
        jax.ShapeDtypeStruct((s, LANES), F32),
        jax.ShapeDtypeStruct((s, D_POOL), BF16), jax.ShapeDtypeStruct((s, D_POOL), F32),
        jax.ShapeDtypeStruct((N_DEV,) + own_block.shape, own_block.dtype),
    )
    fixed3 = lambda i: (0, 0, 0)
    tiles3 = lambda rows: pl.BlockSpec((sub, rows, TQ), lambda i: (i, 0, 0))
    return pl.pallas_call(
        body, grid=(nt,), out_shape=outs, name="pre_attn_fwd",
        in_specs=[pl.BlockSpec((TS, d), _row), pl.BlockSpec((1, d), _fixed),
                  pl.BlockSpec((3 * D_ATTN, d), _fixed), pl.BlockSpec(wf.shape, _fixed), pl.BlockSpec(wu.shape, _fixed),
                  pl.BlockSpec((1, LANES), _fixed), pl.BlockSpec(wpool.shape, fixed3),
                  pl.BlockSpec(lay["place"].shape, _fixed), pl.BlockSpec(lay["place_t"].shape, _fixed),
                  pl.BlockSpec(lay["bias_k"].shape, fixed3), pl.BlockSpec(lay["bias_q_t"].shape, fixed3), ANY],
        out_specs=(pl.BlockSpec((TS, d), _row), tiles3(HEADS * VROWS),
                   pl.BlockSpec((TS, aug), _row), pl.BlockSpec((TS, D_ATTN), _row),
                   tiles3(aug), tiles3(HEADS * VROWS), tiles3(HEADS * VROWS),
                   pl.BlockSpec((TS, LANES), _row),
                   pl.BlockSpec((TS, D_POOL), _row), pl.BlockSpec((TS, D_POOL), _row), ANY),
        scratch_shapes=[pltpu.VMEM((TS + HALO, D_POOL), F32), pltpu.VMEM((1, LANES), F32), pltpu.VMEM((TS, LANES), F32),
                        pltpu.VMEM(own_block.shape, own_block.dtype),
                        pltpu.SemaphoreType.DMA((7,)), pltpu.SemaphoreType.DMA((7,)), pltpu.SemaphoreType.DMA],
        compiler_params=_params(1),
    )(x, g1, wqkv, wf, wu, bpad, wpool, lay["place"], lay["place_t"], lay["bias_k"], lay["bias_q_t"], own_block)


def _causal_in_tile():
    krow = lax.broadcasted_iota(jnp.int32, (TQ, TQ), 0)
    qcol = lax.broadcasted_iota(jnp.int32, (TQ, TQ), 1)
    return krow <= qcol


def _attn_fwd(ka, qat3, vt3, own_block):
    s = ka.shape[0]
    nq = s // TQ
    pass_on_step = max(nq - 2, 0)

    def body(qa_ref, ka_ref, vt_ref, own_ref, a_ref, lset_ref, all_ref, acc, out_t, st_scr, pt_scr,
             stage, send_sems, recv_sems, local_sem):
        i = pl.program_id(0)

        @pl.when(i == 0)
        def _():
            _gather_start(own_ref, all_ref, stage, send_sems, recv_sems, local_sem)

        @pl.when(i == pass_on_step)
        def _():
            _gather_pass_on(all_ref, send_sems, recv_sems)

        acc[...] = jnp.zeros_like(acc)

        def tile(j, stats, masked):
            tile_max = []
            for h in range(HEADS):
                aug = slice(h * AUG, (h + 1) * AUG)
                st = _nn(ka_ref[pl.ds(j * TQ, TQ), aug], qa_ref[0, aug, :])
                if masked:
                    st = jnp.where(_causal_in_tile(), st, NEG)
                st_scr[h] = st
                tile_max.append(jnp.max(st, axis=0, keepdims=True))
            new, scale = [], []
            for h in range(HEADS):
                m_new = jnp.maximum(stats[h], tile_max[h])
                scale.append(jnp.exp2(stats[h] - m_new))
                pt_scr[h] = jnp.exp2(st_scr[h] - m_new).astype(BF16)
                new.append(m_new)
            for h in range(HEADS):
                rows = slice(h * VROWS, (h + 1) * VROWS)
                acc[rows, :] = scale[h] * acc[rows, :] + _nn(vt_ref[j, rows, :], pt_scr[h])
            return tuple(new)

        init = tuple(jnp.full((1, TQ), NEG, F32) for _ in range(HEADS))
        stats = lax.fori_loop(0, i, functools.partial(tile, masked=False), init)
        stats = tile(i, stats, True)
        for h in range(HEADS):
            denom = acc[h * VROWS + HEAD_DIM:h * VROWS + HEAD_DIM + 1, :]
            out_t[h * HEAD_DIM:(h + 1) * HEAD_DIM, :] = acc[h * VROWS:h * VROWS + HEAD_DIM, :] / denom
            lset_ref[0, h:h + 1, :] = stats[h] + jnp.log2(denom)
        a_ref[...] = out_t[...].T

        @pl.when(i == nq - 1)
        def _():
            _gather_finish(own_ref, all_ref, send_sems, recv_sems)

    r, cdim = own_block.shape
    return pl.pallas_call(
        body, grid=(nq,), name="attn_fwd",
        out_shape=(jax.ShapeDtypeStruct((s, D_ATTN), F32), jax.ShapeDtypeStruct((nq, HEADS, TQ), F32),
                   jax.ShapeDtypeStruct((N_DEV, r, cdim), own_block.dtype)),
        in_specs=[pl.BlockSpec((1, HEADS * AUG, TQ), lambda i: (i, 0, 0)), VMEM_WHOLE, VMEM_WHOLE, ANY],
        out_specs=(pl.BlockSpec((TQ, D_ATTN), _row), pl.BlockSpec((1, HEADS, TQ), lambda i: (i, 0, 0)), ANY),
        scratch_shapes=[pltpu.VMEM((HEADS * VROWS, TQ), F32), pltpu.VMEM((D_ATTN, TQ), F32),
                        pltpu.VMEM((HEADS, TQ, TQ), F32), pltpu.VMEM((HEADS, TQ, TQ), BF16),
                        pltpu.VMEM((r, cdim), own_block.dtype),
                        pltpu.SemaphoreType.DMA((7,)), pltpu.SemaphoreType.DMA((7,)), pltpu.SemaphoreType.DMA],
        compiler_params=_params(1),
    )(qat3, ka, vt3, own_block)


def _post_attn_fwd(a, mpre, x, g_attn, g_pool, pscale, wout, g_post, g_ffn_pre):
    s, d = x.shape

    def body(a_ref, mp_ref, x_ref, ga_ref, gp_ref, ps_ref, wo_ref, gpost_ref, gpre_ref,
             mix_ref, o_ref, h1_ref, hn2_ref):
        for rows in _HALVES:
            av = a_ref[rows, :]
            mix_ref[rows, 0:D_ATTN] = (av * _rstd(av) * ga_ref[...]).astype(BF16)
            mv = mp_ref[rows, :] * ps_ref[...]
            mix_ref[rows, D_ATTN:] = (mv * _rstd(mv) * gp_ref[...]).astype(BF16)
            o = _nn(mix_ref[rows, :], wo_ref[...].reshape(d, d))
            o_ref[rows, :] = o
            h1 = x_ref[rows, :] + o * _rstd(o) * gpost_ref[...]
            h1_ref[rows, :] = h1
            hn2_ref[rows, :] = (h1 * _rstd(h1) * gpre_ref[...]).astype(BF16)

    vec = lambda n: pl.BlockSpec((1, n), _fixed)
    return pl.pallas_call(
        body, grid=(s // TS,), name="post_attn_fwd",
        out_shape=(jax.ShapeDtypeStruct((s, d), BF16), jax.ShapeDtypeStruct((s, d), F32),
                   jax.ShapeDtypeStruct((s, d), F32), jax.ShapeDtypeStruct((s, d), BF16)),
        in_specs=[pl.BlockSpec((TS, D_ATTN), _row), pl.BlockSpec((TS, D_POOL), _row), pl.BlockSpec((TS, d), _row),
                  vec(D_ATTN), vec(D_POOL), vec(D_POOL), _spec_square(0), vec(d), vec(d)],
        out_specs=(pl.BlockSpec((TS, d), _row),) * 4,
        compiler_params=_params(1),
    )(a, mpre, x, g_attn, g_pool, pscale, wout, g_post, g_ffn_pre)


def _ffn_fwd(hn2, wg, wu, wd, h1, g_post):
    s, d = h1.shape
    nc = D_FF // TN_FF
    ts = min(TS_FF, s)

    def body(hn_ref, wg_ref, wu_ref, wd_ref, h1_ref, g_ref, gate_ref, up_ref, act_ref, ff_ref, h2_ref, acc):
        j = pl.program_id(1)

        @pl.when(j == 0)
        def _():
            acc[...] = jnp.zeros_like(acc)

        for r in range(2):
            rows = slice(r * (ts // 2), (r + 1) * (ts // 2))
            hn = hn_ref[rows, :]
            gt = _nt(hn, wg_ref[...].reshape(TN_FF, d))
            up = _nt(hn, wu_ref[...].reshape(TN_FF, d))
            gate_ref[rows, :] = gt.astype(BF16)
            up_ref[rows, :] = up.astype(BF16)
            act_ref[rows, :] = (gt * jax.nn.sigmoid(gt) * up).astype(BF16)
            acc[rows, :] += _nn(act_ref[rows, :], wd_ref[...].reshape(TN_FF, d))

        @pl.when(j == nc - 1)
        def _():
            ff = acc[...]
            ff_ref[...] = ff
            h2_ref[...] = h1_ref[...] + ff * _rstd(ff) * g_ref[...]

    rowblk = pl.BlockSpec((ts, d), lambda i, j: (i, 0))
    chunk = pl.BlockSpec((ts, TN_FF), lambda i, j: (i, j))
    return pl.pallas_call(
        body, grid=(s // ts, nc), name="ffn_fwd",
        out_shape=(jax.ShapeDtypeStruct((s, D_FF), BF16),) * 3 + (jax.ShapeDtypeStruct((s, d), F32),) * 2,
        in_specs=[rowblk, _spec_ff(0), _spec_ff(1), _spec_ff(2), rowblk, pl.BlockSpec((1, d), lambda i, j: (0, 0))],
        out_specs=(chunk, chunk, chunk, rowblk, rowblk),
        scratch_shapes=[pltpu.VMEM((ts, d), F32)],
        compiler_params=_params(2),
    )(hn2, wg, wu, wd, h1, g_post)


def _tail_fwd_bwd(h2, p, tgt, ff, wple, wpg, g_ple, g_ffn_post):
    s, d = h2.shape

    def body(h2_ref, p_ref, t_ref, ff_ref, wple_ref, wpg_ref, gple_ref, gfp_ref,
             dh2_ref, dff_ref, dgl_ref, dpp_ref, h2b_ref, pb_ref, loss_ref, dgple_ref, dgfp_ref):
        i = pl.program_id(0)

        @pl.when(i == 0)
        def _():
            loss_ref[...] = jnp.zeros_like(loss_ref)
            dgple_ref[...] = jnp.zeros_like(dgple_ref)
            dgfp_ref[...] = jnp.zeros_like(dgfp_ref)

        h2 = h2_ref[...]
        h2b = h2.astype(BF16)
        h2b_ref[...] = h2b
        pb = p_ref[...].astype(BF16)
        pb_ref[...] = pb
        pp = _nt(pb, wple_ref[...])
        gple = gple_ref[...]
        e = pp * _rstd(pp) * gple
        wpg = wpg_ref[...].reshape(d, d)
        sg = jax.nn.sigmoid(_nn(h2b, wpg))
        diff = h2 + sg * e - t_ref[...]
        sq = jnp.sum(jnp.sum(diff * diff, axis=1, keepdims=True), axis=0, keepdims=True)
        loss_ref[...] += jnp.broadcast_to(sq * (0.5 / d), loss_ref.shape)
        dh3 = diff * (1.0 / d)
        dgl = (dh3 * e * sg * (1.0 - sg)).astype(BF16)
        dgl_ref[...] = dgl
        dh2 = dh3 + _nt(dgl, wpg)
        dh2_ref[...] = dh2
        dpp, dg = _rms_bwd(pp, gple, dh3 * sg)
        dpp_ref[...] = dpp.astype(BF16)
        dgple_ref[...] += dg
        dff, dg = _rms_bwd(ff_ref[...], gfp_ref[...], dh2)
        dff_ref[...] = dff.astype(BF16)
        dgfp_ref[...] += dg

    rowblk = pl.BlockSpec((TS, d), _row)
    vec = pl.BlockSpec((1, d), _fixed)
    return pl.pallas_call(
        body, grid=(s // TS,), name="tail_fwd_bwd",
        out_shape=(jax.ShapeDtypeStruct((s, d), F32), jax.ShapeDtypeStruct((s, d), BF16),
                   jax.ShapeDtypeStruct((s, d), BF16), jax.ShapeDtypeStruct((s, d), BF16),
                   jax.ShapeDtypeStruct((s, d), BF16), jax.ShapeDtypeStruct((s, D_PLE), BF16),
                   jax.ShapeDtypeStruct((8, LANES), F32), jax.ShapeDtypeStruct((1, d), F32),
                   jax.ShapeDtypeStruct((1, d), F32)),
        in_specs=[rowblk, pl.BlockSpec((TS, D_PLE), _row), rowblk, rowblk,
                  pl.BlockSpec(wple.shape, _fixed), _spec_square(1), vec, vec],
        out_specs=(rowblk, rowblk, rowblk, rowblk, rowblk, pl.BlockSpec((TS, D_PLE), _row),
                   pl.BlockSpec((8, LANES), _fixed), vec, vec),
        compiler_params=_params(1),
    )(h2, p, tgt, ff, wple, wpg, g_ple, g_ffn_post)


def _ffn_bwd(dff, gate, up, wd, wg, wu, h1, dh2, g_pre):
    s, d = h1.shape
    nc = D_FF // TN_FF
    ts = min(TS_FF, s)

    def body(dff_ref, gate_ref, up_ref, wd_ref, wg_ref, wu_ref, h1_ref, dh2_ref, g_ref,
             dgate_ref, dup_ref, dh1_ref, dg_ref, acc):
        i = pl.program_id(0)
        j = pl.program_id(1)

        @pl.when((i == 0) & (j == 0))
        def _():
            dg_ref[...] = jnp.zeros_like(dg_ref)

        @pl.when(j == 0)
        def _():
            acc[...] = jnp.zeros_like(acc)

        for r in range(2):
            rows = slice(r * (ts // 2), (r + 1) * (ts // 2))
            dact = _nt(dff_ref[rows, :], wd_ref[...].reshape(TN_FF, d))
            gt = gate_ref[rows, :].astype(F32)
            sg = jax.nn.sigmoid(gt)
            dup_ref[rows, :] = (dact * gt * sg).astype(BF16)
            dgate_ref[rows, :] = (dact * up_ref[rows, :].astype(F32) * (sg * (1.0 + gt * (1.0 - sg)))).astype(BF16)
            acc[rows, :] += (_nn(dgate_ref[rows, :], wg_ref[...].reshape(TN_FF, d))
                             + _nn(dup_ref[rows, :], wu_ref[...].reshape(TN_FF, d)))

        @pl.when(j == nc - 1)
        def _():
            dv, dg = _rms_bwd(h1_ref[...], g_ref[...], acc[...])
            dh1_ref[...] = dh2_ref[...] + dv
            dg_ref[...] += dg

    rowblk = pl.BlockSpec((ts, d), lambda i, j: (i, 0))
    chunk = pl.BlockSpec((ts, TN_FF), lambda i, j: (i, j))
    vec = pl.BlockSpec((1, d), lambda i, j: (0, 0))
    return pl.pallas_call(
        body, grid=(s // ts, nc), name="ffn_bwd",
        out_shape=(jax.ShapeDtypeStruct((s, D_FF), BF16), jax.ShapeDtypeStruct((s, D_FF), BF16),
                   jax.ShapeDtypeStruct((s, d), F32), jax.ShapeDtypeStruct((1, d), F32)),
        in_specs=[rowblk, chunk, chunk, _spec_ff(2), _spec_ff(0), _spec_ff(1), rowblk, rowblk, vec],
        out_specs=(chunk, chunk, rowblk, vec),
        scratch_shapes=[pltpu.VMEM((ts, d), F32)],
        compiler_params=_params(2),
    )(dff, gate, up, wd, wg, wu, h1, dh2, g_pre)


def _post_attn_bwd(dh1, o, a, mpre, wout, wpool, g_post, g_attn, g_pool, pscale):
    s, d = dh1.shape
    sub = TS // TQ

    def body(dh1_ref, o_ref, a_ref, mp_ref, wo_ref, wp_ref, gpost_ref, ga_ref, gp_ref, ps_ref,
             dob_ref, dat_ref, dlt_ref, dmpb_ref, dy_ref, dgpost_ref, dga_ref, dgp_ref, dps_ref):
        i = pl.program_id(0)

        @pl.when(i == 0)
        def _():
            dgpost_ref[...] = jnp.zeros_like(dgpost_ref)
            dga_ref[...] = jnp.zeros_like(dga_ref)
            dgp_ref[...] = jnp.zeros_like(dgp_ref)
            dps_ref[...] = jnp.zeros_like(dps_ref)

        do, dg = _rms_bwd(o_ref[...], gpost_ref[...], dh1_ref[...])
        dgpost_ref[...] += dg
        dob = do.astype(BF16)
        dob_ref[...] = dob
        dmix = _nt(dob, wo_ref[...].reshape(d, d))

        av = a_ref[...]
        da, dg = _rms_bwd(av, ga_ref[...], dmix[:, 0:D_ATTN])
        dga_ref[...] += dg
        dat = da.astype(BF16).T
        hsel = (lax.shift_right_logical(lax.broadcasted_iota(jnp.int32, (HEADS, D_ATTN), 1), 6)
                == lax.broadcasted_iota(jnp.int32, (HEADS, D_ATTN), 0)).astype(F32)
        dlt = lax.dot_general(hsel, da * av, (((1,), (1,)), ((), ())), precision=HIGHEST, preferred_element_type=F32)
        for q in range(sub):
            dlt_ref[q] = dlt[:, q * TQ:(q + 1) * TQ]
            dat_ref[q] = dat[:, q * TQ:(q + 1) * TQ]

        ps = ps_ref[...]
        mp = mp_ref[...]
        dm, dg = _rms_bwd(mp * ps, gp_ref[...], dmix[:, D_ATTN:])
        dgp_ref[...] += dg
        dps_ref[...] += jnp.sum(dm * mp, axis=0, keepdims=True)
        dmpb = (dm * ps).astype(BF16)
        dmpb_ref[...] = dmpb
        for g in range(len(POOL_WINDOWS)):
            cols = slice(g * POOL_CH, (g + 1) * POOL_CH)
            dy_ref[:, cols] = _nt(dmpb[:, cols], wp_ref[g])

    rowblk = pl.BlockSpec((TS, d), _row)
    half = pl.BlockSpec((TS, D_ATTN), _row)
    vec = lambda n: pl.BlockSpec((1, n), _fixed)
    return pl.pallas_call(
        body, grid=(s // TS,), name="post_attn_bwd",
        out_shape=(jax.ShapeDtypeStruct((s, d), BF16), jax.ShapeDtypeStruct((s // TQ, D_ATTN, TQ), BF16),
                   jax.ShapeDtypeStruct((s // TQ, HEADS, TQ), F32), jax.ShapeDtypeStruct((s, D_POOL), BF16),
                   jax.ShapeDtypeStruct((s, D_POOL), F32), jax.ShapeDtypeStruct((1, d), F32),
                   jax.ShapeDtypeStruct((1, D_ATTN), F32), jax.ShapeDtypeStruct((1, D_POOL), F32),
                   jax.ShapeDtypeStruct((1, D_POOL), F32)),
        in_specs=[rowblk, rowblk, half, half, _spec_square(0),
                  pl.BlockSpec(wpool.shape, lambda i: (0, 0, 0)), vec(d), vec(D_ATTN), vec(D_POOL), vec(D_POOL)],
        out_specs=(rowblk, pl.BlockSpec((sub, D_ATTN, TQ), lambda i: (i, 0, 0)),
                   pl.BlockSpec((sub, HEADS, TQ), lambda i: (i, 0, 0)), half, half,
                   vec(d), vec(D_ATTN), vec(D_POOL), vec(D_POOL)),
        compiler_params=_params(1),
    )(dh1, o, a, mpre, wout, wpool, g_post, g_attn, g_pool, pscale)


def _attn_bwd(ka, v, kt3, qat3, qt3, dot3, lset3, dlt3, chip_blocks):
    s = ka.shape[0]
    nq = s // TQ

    def body(ka_ref, v_ref, kt_ref, qat_ref, qt_ref, dot_ref, lset_ref, dlt_ref, b_ref,
             dqt_ref, dkt_ref, dvt_ref, got_ref, pt_scr, ptb_scr, dsb_scr,
             stage, send_sems, recv_sems, local_sem):
        j = pl.program_id(0)

        @pl.when(j == 0)
        def _():
            _chips_start(b_ref, got_ref, stage, send_sems, recv_sems, local_sem)
            dqt_ref[...] = jnp.zeros_like(dqt_ref)

        def tile(i, masked):
            def accumulate(ref, idx, val):
                if masked:
                    ref[idx] = val
                else:
                    ref[idx] += val

            for h in range(HEADS):
                aug = slice(h * AUG, (h + 1) * AUG)
                st = _nn(ka_ref[:, aug], qat_ref[i, aug, :]) - lset_ref[i, h:h + 1, :]
                if masked:
                    st = jnp.where(_causal_in_tile(), st, NEG)
                pt = jnp.exp2(st)
                pt_scr[h] = pt
                ptb_scr[h] = pt.astype(BF16)
            heads = [(h, slice(h * HEAD_DIM, (h + 1) * HEAD_DIM)) for h in range(HEADS)]
            for h, hs in heads:
                dst = pt_scr[h] * (_nn(v_ref[:, hs], dot_ref[i, hs, :]) - dlt_ref[i, h:h + 1, :])
                dsb_scr[h] = dst.astype(BF16)
            for h, hs in heads:
                accumulate(dvt_ref, (0, hs, slice(None)), _nt(dot_ref[i, hs, :], ptb_scr[h]))
            for h, hs in heads:
                rows = slice(h * VROWS, (h + 1) * VROWS)
                accumulate(dkt_ref, (0, rows, slice(None)), _nt(qt_ref[i, rows, :], dsb_scr[h]))
            for h, hs in heads:
                rows = slice(h * VROWS, (h + 1) * VROWS)
                dqt_ref[i, rows, :] += _nn(kt_ref[0, rows, :], dsb_scr[h])

        first = j + 1
        pairs = (nq - first) // 2

        def step(p, carry):
            tile(first + 2 * p, False)
            tile(first + 2 * p + 1, False)
            return carry

        tile(j, True)
        lax.fori_loop(0, pairs, step, 0)

        @pl.when(first + 2 * pairs < nq)
        def _():
            tile(nq - 1, False)

        @pl.when(j == nq - 1)
        def _():
            _chips_finish(b_ref, got_ref, send_sems, recv_sems)

    blk = pl.BlockSpec((TQ, D_ATTN), _row)
    tile_t = lambda rows: pl.BlockSpec((1, rows, TQ), lambda j: (j, 0, 0))
    per_tile = lambda rows: jax.ShapeDtypeStruct((nq, rows, TQ), F32)
    _, r, cdim = chip_blocks.shape
    return pl.pallas_call(
        body, grid=(nq,), name="attn_bwd",
        out_shape=(per_tile(HEADS * VROWS), per_tile(HEADS * VROWS), per_tile(D_ATTN),
                   jax.ShapeDtypeStruct(chip_blocks.shape, chip_blocks.dtype)),
        in_specs=[pl.BlockSpec((TQ, HEADS * AUG), _row), blk, tile_t(HEADS * VROWS),
                  VMEM_WHOLE, VMEM_WHOLE, VMEM_WHOLE, VMEM_WHOLE, VMEM_WHOLE, ANY],
        out_specs=(pl.BlockSpec((nq, HEADS * VROWS, TQ), lambda j: (0, 0, 0)), tile_t(HEADS * VROWS), tile_t(D_ATTN),
                   ANY),
        scratch_shapes=[pltpu.VMEM((HEADS, TQ, TQ), F32), pltpu.VMEM((HEADS, TQ, TQ), BF16),
                        pltpu.VMEM((HEADS, TQ, TQ), BF16), pltpu.VMEM((r, cdim), chip_blocks.dtype),
                        pltpu.SemaphoreType.DMA((3,)), pltpu.SemaphoreType.DMA((3,)), pltpu.SemaphoreType.DMA],
        compiler_params=_params(1),
    )(ka, v, kt3, qat3, qt3, dot3, lset3, dlt3, chip_blocks)


def _pre_attn_bwd(dqt3, dkt3, dvt3, dcs, drs, fl, dy, x, dh1, g1, wqkv, wf, wu):
    s, d = x.shape
    nt = s // TS
    n = TS + HALO
    sub = TS // TQ
    qkv, fcols = 3 * D_ATTN, 3 * D_ATTN + LANES

    def body(dqt_ref, dkt_ref, dvt_ref, dcs_ref, drs_ref, fl_ref, dy_ref, x_ref, dh1_ref, g_ref, wqkv_ref, wf_ref, wu_ref,
             gx_ref, dz_ref, dg_ref, db_ref, ybuf, ccar, dlog):
        dqkv_ref = dz_ref.at[:, 0:qkv]
        dfb_ref = dz_ref.at[:, qkv:fcols]
        dub_ref = dz_ref.at[:, fcols:]
        i = pl.program_id(0)
        ti = nt - 1 - i

        @pl.when(i == 0)
        def _():
            ybuf[TS:n, :] = jnp.zeros((HALO, D_POOL), F32)
            ccar[...] = jnp.zeros_like(ccar)
            dg_ref[...] = jnp.zeros_like(dg_ref)
            db_ref[...] = jnp.zeros_like(db_ref)

        rr = lax.broadcasted_iota(jnp.int32, (TS, TS), 0)
        cc = lax.broadcasted_iota(jnp.int32, (TS, TS), 1)
        dlog[...] = ccar[...] + _mask_matmul((cc >= rr).astype(BF16), drs_ref[...] - dcs_ref[...])
        ccar[...] = dlog[0:1, :]
        df = dlog[...] * jax.nn.sigmoid(-fl_ref[...])
        db_ref[...] += jnp.sum(df, axis=0, keepdims=True)
        dfb = df.astype(BF16)
        dfb_ref[...] = dfb

        t = ti * TS + lax.broadcasted_iota(jnp.int32, (TS, 1), 0)
        dy = dy_ref[...]
        for g, w in enumerate(POOL_WINDOWS):
            cols = slice(g * POOL_CH, (g + 1) * POOL_CH)
            ybuf[0:TS, cols] = dy[:, cols] / jnp.minimum(t + 1, w).astype(F32)
        for g, w in enumerate(POOL_WINDOWS):
            cols = slice(g * POOL_CH, (g + 1) * POOL_CH)
            sm = ybuf[:, cols]
            step = 1
            while step < w:
                sm = sm + pltpu.roll(sm, n - step, 0)
                step *= 2
            dub_ref[:, cols] = (sm[0:TS, :] - dy[:, cols]).astype(BF16)
        ybuf[TS:n, :] = ybuf[0:HALO, :]

        for a in range(sub):
            rows = slice(a * TQ, (a + 1) * TQ)
            for h in range(HEADS):
                src = slice(h * VROWS, h * VROWS + HEAD_DIM)
                dqkv_ref[rows, h * HEAD_DIM:(h + 1) * HEAD_DIM] = (dqt_ref[a, src, :].T * 0.125).astype(BF16)
                dqkv_ref[rows, D_ATTN + h * HEAD_DIM:D_ATTN + (h + 1) * HEAD_DIM] = dkt_ref[a, src, :].T.astype(BF16)
            dqkv_ref[rows, 2 * D_ATTN:] = dvt_ref[a].T.astype(BF16)
        dhn = _nn(dqkv_ref[...], wqkv_ref[...]) + _nn(dfb, wf_ref[...]) + _nn(dub_ref[...], wu_ref[...])
        dx, dg = _rms_bwd(x_ref[...], g_ref[...], dhn)
        gx_ref[...] = dh1_ref[...] + dx
        dg_ref[...] += dg

    rev = lambda i: (nt - 1 - i, 0)
    blk = lambda w: pl.BlockSpec((TS, w), rev)
    return pl.pallas_call(
        body, grid=(nt,), name="pre_attn_bwd",
        out_shape=(jax.ShapeDtypeStruct((s, d), F32), jax.ShapeDtypeStruct((s, fcols + D_POOL), BF16),
                   jax.ShapeDtypeStruct((1, d), F32), jax.ShapeDtypeStruct((1, LANES), F32)),
        in_specs=[pl.BlockSpec((sub, HEADS * VROWS, TQ), lambda i: (nt - 1 - i, 0, 0)),
                  pl.BlockSpec((sub, HEADS * VROWS, TQ), lambda i: (nt - 1 - i, 0, 0)),
                  pl.BlockSpec((sub, D_ATTN, TQ), lambda i: (nt - 1 - i, 0, 0)),
                  blk(LANES), blk(LANES), blk(LANES), blk(D_POOL), blk(d), blk(d),
                  pl.BlockSpec((1, d), _fixed), pl.BlockSpec((qkv, d), _fixed), pl.BlockSpec(wf.shape, _fixed),
                  pl.BlockSpec(wu.shape, _fixed)],
        out_specs=(blk(d), blk(fcols + D_POOL), pl.BlockSpec((1, d), _fixed), pl.BlockSpec((1, LANES), _fixed)),
        scratch_shapes=[pltpu.VMEM((n, D_POOL), F32), pltpu.VMEM((1, LANES), F32), pltpu.VMEM((TS, LANES), F32)],
        compiler_params=_params(1),
    )(dqt3, dkt3, dvt3, dcs, drs, fl, dy, x, dh1, g1, wqkv, wf, wu)


def _wgrad(a, b, out_dtype, name):
    s, m = a.shape
    n = b.shape[1]
    tm = max(t for t in range(LANES, min(m, TM_WGRAD) + 1, LANES) if m % t == 0)
    ts = min(TS_WGRAD, s)
    ns = s // ts

    def body(a_ref, b_ref, o_ref, acc):
        i = pl.program_id(1)

        @pl.when(i == 0)
        def _():
            acc[...] = jnp.zeros_like(acc)

        acc[...] += _tn(a_ref[...], b_ref[...])

        @pl.when(i == ns - 1)
        def _():
            o_ref[...] = acc[...].astype(out_dtype)

    return pl.pallas_call(
        body, grid=(m // tm, ns), name=name, out_shape=jax.ShapeDtypeStruct((m, n), out_dtype),
        in_specs=[pl.BlockSpec((ts, tm), lambda j, i: (i, j)), pl.BlockSpec((ts, n), lambda j, i: (i, 0))],
        out_specs=pl.BlockSpec((tm, n), lambda j, i: (j, 0)),
        scratch_shapes=[pltpu.VMEM((tm, n), F32)],
        compiler_params=_params(2),
    )(a, b)


def _adamw(w, g, m, v):
    m = ADAM_B1 * m + (1.0 - ADAM_B1) * g
    v = ADAM_B2 * v + (1.0 - ADAM_B2) * (g * g)
    m_hat = m / (1.0 - ADAM_B1 ** ADAM_STEP)
    v_hat = v / (1.0 - ADAM_B2 ** ADAM_STEP)
    delta = -ADAM_LR * (m_hat / (jnp.sqrt(v_hat) + ADAM_EPS) + ADAM_WD * w)
    return delta, m, v


def _sum_parts(p_ref):
    g = p_ref[0].astype(F32)
    for k in range(1, p_ref.shape[0]):
        g = g + p_ref[k].astype(F32)
    return g


def _adamw_store(g, w_ref, m_ref, v_ref, g_ref, d_ref, nm_ref, nv_ref):
    g_ref[...] = g
    d_ref[...], nm_ref[...], nv_ref[...] = _adamw(w_ref[...], g, m_ref[...], v_ref[...])


def _reduce_update_rest(parts, states, chip_blocks, small_block):
    nk, r, c = parts.shape
    nch = UPDATE_CHUNKS
    ck = c // nch
    ns = (r // SHARD_FF) * nch
    group_of = [0, 0, 1, 2, 3, 0]
    n_in, n_out = 3 * len(states), 4 * len(states)

    def body(p_ref, *refs):
        ins, (b_ref, sm_ref) = refs[:n_in], refs[n_in:n_in + 2]
        outs, (got_ref, all_ref) = refs[n_in + 2:n_in + 2 + n_out], refs[n_in + 2 + n_out:n_in + 4 + n_out]
        stage_b, stage_s, send_b, recv_b, local_b, send_s, recv_s, local_s = refs[n_in + 4 + n_out:]
        i = pl.program_id(0)

        @pl.when(i == 0)
        def _():
            _chips_start(b_ref, got_ref, stage_b, send_b, recv_b, local_b)
            _gather_start(sm_ref, all_ref, stage_s, send_s, recv_s, local_s)

        g = _sum_parts(p_ref)
        update = lambda k, gk: _adamw_store(gk, *ins[3 * k:3 * k + 3], *outs[4 * k:4 * k + 4])

        @pl.when(i // nch == 0)
        def _():
            update(0, g[0:OFF_PG])
            update(1, g[OFF_PG:OFF_PLE])
            update(5, g[OFF_PLE:OFF_PLE + ROWS_PLE])

        for k in (2, 3, 4):
            @pl.when(i // nch == group_of[k])
            def _():
                update(k, g)

        @pl.when(i == ns - 1)
        def _():
            _gather_pass_on(all_ref, send_s, recv_s)
            _chips_finish(b_ref, got_ref, send_b, recv_b)
            _gather_finish(sm_ref, all_ref, send_s, recv_s)

    def spec(k, a):
        return pl.BlockSpec((a.shape[0], ck), lambda i: (0, jnp.clip(i - group_of[k] * nch, 0, nch - 1)))

    specs = [spec(k, st[0]) for k, st in enumerate(states)]
    dma = pltpu.SemaphoreType.DMA
    res = pl.pallas_call(
        body, grid=(ns,), name="reduce_update_rest",
        out_shape=tuple(jax.ShapeDtypeStruct(st[0].shape, F32) for st in states for _ in range(4))
        + (jax.ShapeDtypeStruct(chip_blocks.shape, chip_blocks.dtype),
           jax.ShapeDtypeStruct((N_DEV,) + small_block.shape, small_block.dtype)),
        in_specs=[pl.BlockSpec((nk, SHARD_FF, ck), lambda i: (0, i // nch, i % nch))]
        + [sp for sp in specs for _ in range(3)] + [ANY, ANY],
        out_specs=tuple(sp for sp in specs for _ in range(4)) + (ANY, ANY),
        scratch_shapes=[pltpu.VMEM(chip_blocks.shape[1:], chip_blocks.dtype), pltpu.VMEM(small_block.shape, small_block.dtype),
                        dma((3,)), dma((3,)), dma, dma((7,)), dma((7,)), dma],
        compiler_params=_params(1),
    )(parts, *[a for st in states for a in st], chip_blocks, small_block)
    return [list(res[4 * k:4 * k + 4]) for k in range(len(states))], res[n_out], res[n_out + 1]


def _reduce_update_in(parts, w, m, v):
    nk, r, c = parts.shape
    ck = c // UPDATE_CHUNKS

    def body(p_ref, w_ref, m_ref, v_ref, g_ref, d_ref, nm_ref, nv_ref):
        _adamw_store(_sum_parts(p_ref)[0:SHARD_IN], w_ref, m_ref, v_ref, g_ref, d_ref, nm_ref, nv_ref)

    blk = pl.BlockSpec((SHARD_IN, ck), lambda i: (0, i))
    return pl.pallas_call(
        body, grid=(UPDATE_CHUNKS,), name="reduce_update_in", out_shape=(jax.ShapeDtypeStruct(w.shape, F32),) * 4,
        in_specs=[pl.BlockSpec((nk, r, ck), lambda i: (0, 0, i)), blk, blk, blk],
        out_specs=(blk,) * 4, compiler_params=_params(1),
    )(parts, w, m, v)


def _reduce_update_small(parts, w, m, v):
    nd = parts.shape[0]

    def body(p_ref, w_ref, m_ref, v_ref, g_ref, d_ref, nm_ref, nv_ref):
        g = p_ref[0]
        for k in range(1, nd):
            g = g + p_ref[k]
        g_ref[...] = g
        d_ref[...], nm_ref[...], nv_ref[...] = _adamw(w_ref[...], g, m_ref[...], v_ref[...])

    out = jax.ShapeDtypeStruct(w.shape, F32)
    return pl.pallas_call(body, name="reduce_update_small", out_shape=(out,) * 4,
                          compiler_params=pltpu.CompilerParams(vmem_limit_bytes=VMEM_LIMIT))(parts, w, m, v)


MESH = pl.DeviceIdType.MESH


def _copy_through_vmem(src_hbm, dst_hbm, stage, sem):
    load = pltpu.make_async_copy(src_hbm, stage, sem)
    load.start()
    load.wait()
    store = pltpu.make_async_copy(stage, dst_hbm, sem)
    store.start()
    store.wait()


class _GatherPlan:
    def __init__(self, x_ref, out_ref, send_sems, recv_sems):
        x, y, c = lax.axis_index("x"), lax.axis_index("y"), lax.axis_index("c")
        self.me, self.sibling, self.c = (x, y, c), (x, y, 1 - c), c
        self.chips = [(1 - x, y), (x, 1 - y), (1 - x, 1 - y)]
        self.x_ref, self.out_ref, self.send_sems, self.recv_sems = x_ref, out_ref, send_sems, recv_sems

    def slot(self, px, py, pc):
        return self.out_ref.at[4 * px + 2 * py + pc]

    def copy(self, k, block, to, src=None):
        return pltpu.make_async_remote_copy(
            src_ref=self.slot(*block) if src is None else src, dst_ref=self.slot(*block),
            send_sem=self.send_sems.at[k], recv_sem=self.recv_sems.at[k], device_id=to, device_id_type=MESH)

    def first(self):
        return [self.copy(0, self.me, self.sibling, src=self.x_ref)] + [
            self.copy(1 + j, self.me, (*chip, self.c), src=self.x_ref) for j, chip in enumerate(self.chips)]

    def passed(self):
        return [self.copy(4 + j, (*chip, self.c), self.sibling) for j, chip in enumerate(self.chips)]


def _gather_start(x_ref, out_ref, stage, send_sems, recv_sems, local_sem):
    plan = _GatherPlan(x_ref, out_ref, send_sems, recv_sems)
    for cp in plan.first():
        cp.start()
    _copy_through_vmem(x_ref, plan.slot(*plan.me), stage, local_sem)


def _gather_pass_on(out_ref, send_sems, recv_sems):
    plan = _GatherPlan(None, out_ref, send_sems, recv_sems)
    passed = plan.passed()
    for j, chip in enumerate(plan.chips):
        plan.copy(1 + j, (*chip, plan.c), plan.me).wait_recv()
        passed[j].start()


def _gather_finish(x_ref, out_ref, send_sems, recv_sems):
    plan = _GatherPlan(x_ref, out_ref, send_sems, recv_sems)
    plan.copy(0, plan.sibling, plan.me).wait_recv()
    for j, chip in enumerate(plan.chips):
        plan.copy(4 + j, (*chip, 1 - plan.c), plan.me).wait_recv()
    for cp in plan.first() + plan.passed():
        cp.wait_send()


def _all_gather(xs, name):
    r, cdim = xs.shape

    def body(x_ref, out_ref, stage, send_sems, recv_sems, local_sem):
        _gather_start(x_ref, out_ref, stage, send_sems, recv_sems, local_sem)
        _gather_pass_on(out_ref, send_sems, recv_sems)
        _gather_finish(x_ref, out_ref, send_sems, recv_sems)

    return pl.pallas_call(
        body, name=name, out_shape=jax.ShapeDtypeStruct((N_DEV, r, cdim), xs.dtype),
        in_specs=[ANY], out_specs=ANY,
        scratch_shapes=[pltpu.VMEM((r, cdim), xs.dtype), pltpu.SemaphoreType.DMA((7,)), pltpu.SemaphoreType.DMA((7,)),
                        pltpu.SemaphoreType.DMA],
        compiler_params=pltpu.CompilerParams(vmem_limit_bytes=VMEM_LIMIT),
    )(xs)


def _rs_pair_sum(core, pieces, offsets, rows, name):
    cdim = pieces[0].shape[2]
    nk = N_DEV // 2
    npc = len(pieces)
    spans = [(o, t.shape[1]) for t, o in zip(pieces, offsets)]
    ends = [o + n for o, n in spans]
    gaps = [(a, b - a) for a, b in zip(ends, [o for o, _ in spans[1:]] + [rows]) if b > a]

    def body(core_ref, *refs):
        own, src, o_ref = refs[:npc], refs[npc:2 * npc], refs[2 * npc]
        landing, send_sems, recv_sems = refs[2 * npc + 1:]
        k = pl.program_id(0)
        x, y, c = lax.axis_index("x"), lax.axis_index("y"), lax.axis_index("c")

        def copies(kk):
            return [pltpu.make_async_remote_copy(
                src_ref=src[p].at[2 * kk + (1 - c)], dst_ref=landing.at[kk, pl.ds(o, n)],
                send_sem=send_sems.at[kk, p], recv_sem=recv_sems.at[kk, p], device_id=(x, y, 1 - c),
                device_id_type=MESH) for p, (o, n) in enumerate(spans)]

        @pl.when(k == 0)
        def _():
            for kk in range(nk):
                for cp in copies(kk):
                    cp.start()

        for cp, piece, (o, n) in zip(copies(k), own, spans):
            cp.wait_recv()
            o_ref[0, o:o + n, :] = (piece[0].astype(F32) + landing[k, o:o + n, :].astype(F32)).astype(BF16)
        for o, n in gaps:
            o_ref[0, o:o + n, :] = jnp.zeros((n, cdim), BF16)

        @pl.when(k == nk - 1)
        def _():
            for kk in range(nk):
                for cp in copies(kk):
                    cp.wait_send()

    own_specs = [pl.BlockSpec((1, n, cdim), lambda k, core_ref: (2 * k + core_ref[0], 0, 0)) for _, n in spans]
    return pl.pallas_call(
        body, name=name, out_shape=jax.ShapeDtypeStruct((nk, rows, cdim), BF16),
        grid_spec=pltpu.PrefetchScalarGridSpec(
            num_scalar_prefetch=1, grid=(nk,),
            in_specs=own_specs + [ANY] * npc,
            out_specs=pl.BlockSpec((1, rows, cdim), lambda k, core_ref: (k, 0, 0)),
            scratch_shapes=[pltpu.VMEM((nk, rows, cdim), BF16), pltpu.SemaphoreType.DMA((nk, npc)),
                            pltpu.SemaphoreType.DMA((nk, npc))]),
        compiler_params=_params(1),
    )(core, *pieces, *pieces)


def _chips_start(b_ref, out_ref, stage, send_sems, recv_sems, local_sem):
    x, y, c = lax.axis_index("x"), lax.axis_index("y"), lax.axis_index("c")
    mychip = 2 * x + y
    for j, (px, py) in enumerate([(1 - x, y), (x, 1 - y), (1 - x, 1 - y)]):
        pltpu.make_async_remote_copy(
            src_ref=b_ref.at[2 * px + py], dst_ref=out_ref.at[mychip],
            send_sem=send_sems.at[j], recv_sem=recv_sems.at[j], device_id=(px, py, c), device_id_type=MESH).start()
    _copy_through_vmem(b_ref.at[mychip], out_ref.at[mychip], stage, local_sem)


def _chips_finish(b_ref, out_ref, send_sems, recv_sems):
    x, y, c = lax.axis_index("x"), lax.axis_index("y"), lax.axis_index("c")
    for j, (px, py) in enumerate([(1 - x, y), (x, 1 - y), (1 - x, 1 - y)]):
        pltpu.make_async_remote_copy(
            src_ref=b_ref.at[2 * px + py], dst_ref=out_ref.at[2 * px + py],
            send_sem=send_sems.at[j], recv_sem=recv_sems.at[j], device_id=(px, py, c), device_id_type=MESH).wait()


def _pad_rows(a, rows):
    return jnp.pad(a, ((0, rows - a.shape[0]), (0, 0)))


def _pack_in(w_in):
    return _pad_rows(w_in[0].T, ROWS_IN)


def _pack_ple(w_ple):
    return w_ple[0].T.reshape(ROWS_PLE, D_MODEL)


def _unpack_ple(r):
    return r.reshape(D_MODEL // N_DEV, D_PLE).T[None]


def _pack_rest(w_out, w_pg, w_gate, w_up, w_down, w_ple):
    head = _pad_rows(jnp.concatenate([w_out[0], w_pg[0], _pack_ple(w_ple)], axis=0), OFF_GATE)
    return head, jnp.concatenate([w_gate[0].T, w_up[0].T, w_down[0]], axis=0)


def _pack_small(w_pool, g_mix_pre, g_mix_post, g_ffn_pre, g_ffn_post, g_ple, g_attn, g_pool, pool_scale, b_forget,
                loss=None):
    def row(vrow):
        return jnp.pad(vrow.reshape(1, -1), ((0, 0), (0, D_MODEL - vrow.size)))
    rows = [w_pool.reshape(64, D_MODEL), row(g_mix_pre), row(g_mix_post), row(g_ffn_pre), row(g_ffn_post), row(g_ple),
            row(g_attn), row(g_pool), row(pool_scale), row(b_forget),
            row(loss) if loss is not None else jnp.zeros((1, D_MODEL), F32)]
    return _pad_rows(jnp.concatenate(rows, axis=0), SMALL_ROWS)


def _unpack_small(r):
    return dict(
        w_pool=r[0:64].reshape(1, 4, POOL_CH, POOL_CH), g_mix_pre=r[ROW_G_MIX_PRE:ROW_G_MIX_PRE + 1],
        g_mix_post=r[ROW_G_MIX_POST:ROW_G_MIX_POST + 1], g_ffn_pre=r[ROW_G_FFN_PRE:ROW_G_FFN_PRE + 1],
        g_ffn_post=r[ROW_G_FFN_POST:ROW_G_FFN_POST + 1], g_ple=r[ROW_G_PLE:ROW_G_PLE + 1],
        g_attn_grp=r[ROW_G_ATTN:ROW_G_ATTN + 1, 0:D_ATTN], g_pool_grp=r[ROW_G_POOL:ROW_G_POOL + 1, 0:D_POOL],
        pool_scale=r[ROW_POOL_SCALE:ROW_POOL_SCALE + 1, 0:D_POOL], b_forget=r[ROW_B_FORGET:ROW_B_FORGET + 1, 0:HEADS])


def _step(x, p, tgt, small, state_in, states_rest):
    core = lax.axis_index("c").astype(jnp.int32).reshape(1)
    as_bf = lambda a: a.astype(BF16)
    win_t = _all_gather(_pack_in(as_bf(state_in[0])), "gather_w_in")[:, 0:SHARD_IN].reshape(D_IN, D_MODEL)
    wqkv = win_t
    wf = _pad_rows(win_t[3 * D_ATTN:3 * D_ATTN + HEADS], LANES)
    wu = win_t[3 * D_ATTN + HEADS:]
    wpool = small["w_pool"].astype(BF16)
    bpad = jnp.pad(small["b_forget"], ((0, 0), (0, LANES - HEADS)))

    lay = _attn_layout_constants()
    head_b, ffn_b = _pack_rest(*[as_bf(st[0]) for st in states_rest])
    hn, qt3, ka, v, qat3, vt3, kt3, fl, y, mpre, gh = _pre_attn_fwd(x, small["g_mix_pre"], wqkv, wf, wu, bpad, wpool, lay,
                                                                 head_b)
    a, lset3, gf = _attn_fwd(ka, qat3, vt3, ffn_b)
    wple_t = gh[:, OFF_PLE:OFF_PLE + ROWS_PLE].reshape(D_MODEL, D_PLE)
    mix, o, h1, hn2 = _post_attn_fwd(a, mpre, x, small["g_attn_grp"], small["g_pool_grp"], small["pool_scale"], gh,
                                     small["g_mix_post"], small["g_ffn_pre"])
    gate, up, act, ff, h2 = _ffn_fwd(hn2, gf, gf, gf, h1, small["g_ffn_post"])
    dh2, dff, dgl, dpp, h2b, pb, loss8, dg_ple, dg_ffn_post = _tail_fwd_bwd(
        h2, p, tgt, ff, wple_t, gh, small["g_ple"], small["g_ffn_post"])
    dgate, dup, dh1, dg_ffn_pre = _ffn_bwd(dff, gate, up, gf, gf, gf, h1, dh2, small["g_ffn_pre"])
    dob, dat3, dlt3, dmpb, dy, dg_mix_post, dg_attn, dg_pool, dps = _post_attn_bwd(
        dh1, o, a, mpre, gh, wpool, small["g_mix_post"], small["g_attn_grp"], small["g_pool_grp"], small["pool_scale"])

    nd = N_DEV
    send_rest = [
        _wgrad(mix, dob, BF16, "wgrad_out").reshape(nd, 128, D_MODEL),
        _wgrad(h2b, dgl, BF16, "wgrad_ple_gate").reshape(nd, 128, D_MODEL),
        _wgrad(dpp, pb, BF16, "wgrad_ple").reshape(nd, 32, D_MODEL),
        _wgrad(dgate, hn2, BF16, "wgrad_gate").reshape(nd, SHARD_FF, D_MODEL),
        _wgrad(dup, hn2, BF16, "wgrad_up").reshape(nd, SHARD_FF, D_MODEL),
        _wgrad(act, dff, BF16, "wgrad_down").reshape(nd, SHARD_FF, D_MODEL)]
    pair_rest = _rs_pair_sum(core, send_rest, [0, OFF_PG, OFF_PLE, OFF_GATE, OFF_UP, OFF_DOWN], ROWS_REST,
                             "rs_pair_sum_rest")

    dqt3, dkt3, dvt3, chips_rest = _attn_bwd(ka, v, kt3, qat3, qt3, dat3, lset3, dlt3, pair_rest)

    def sums_per_token(t3):
        rows = t3.reshape(-1, HEADS, VROWS, TQ)[:, :, HEAD_DIM, :]
        return jnp.pad(rows.transpose(0, 2, 1).reshape(-1, HEADS), ((0, 0), (0, LANES - HEADS)))

    drs, dcs = sums_per_token(dqt3), sums_per_token(dkt3)
    gx, dz, dg_mix_pre, db = _pre_attn_bwd(dqt3, dkt3, dvt3, dcs, drs, fl, dy, x, dh1, small["g_mix_pre"], wqkv, wf, wu)

    dwz = _wgrad(dz, hn, BF16, "wgrad_in")
    dwin_t = jnp.concatenate([dwz[0:3 * D_ATTN], dwz[3 * D_ATTN:3 * D_ATTN + HEADS], dwz[3 * D_ATTN + LANES:]], axis=0)
    send_in = jnp.pad(dwin_t.reshape(nd, SHARD_IN, D_MODEL), ((0, 0), (0, ROWS_IN - SHARD_IN), (0, 0)))
    pair_in = _rs_pair_sum(core, [send_in], [0], ROWS_IN, "rs_pair_sum_in")

    dwp = _wgrad(y, dmpb, F32, "wgrad_pool")
    dw_pool = jnp.stack([dwp[g * POOL_CH:(g + 1) * POOL_CH, g * POOL_CH:(g + 1) * POOL_CH] for g in range(4)])
    small_part = _pack_small(dw_pool, dg_mix_pre, dg_mix_post, dg_ffn_pre, dg_ffn_post, dg_ple, dg_attn, dg_pool, dps,
                             db[:, 0:HEADS], loss8[0:1, 0:1])

    to_rows = [lambda a: a[0], lambda a: a[0], lambda a: a[0].T, lambda a: a[0].T, lambda a: a[0], _pack_ple]
    from_rows = [lambda r: r[None], lambda r: r[None], lambda r: r.T[None], lambda r: r.T[None], lambda r: r[None],
                 _unpack_ple]
    rows_rest = [tuple(f(a) for a in st) for f, st in zip(to_rows, states_rest)]
    upd_rest, chips_in, small_all = _reduce_update_rest(chips_rest, rows_rest, pair_in, small_part)
    upd_rest = [[f(r) for r in upd] for f, upd in zip(from_rows, upd_rest)]
    upd_in = [r.T[None] for r in _reduce_update_in(chips_in, *[a[0].T for a in state_in])]
    return gx, small_all, upd_in, upd_rest


def kernel(x, p, g_mix_pre, w_in, b_forget, g_attn_grp, g_pool_grp, w_pool, pool_scale, w_out, g_mix_post, g_ffn_pre, w_ffn_gate, w_ffn_up, w_ffn_down, g_ffn_post, w_ple_proj, g_ple, w_ple_gate, loss_target, m_g_mix_pre, m_w_in, m_b_forget, m_g_attn_grp, m_g_pool_grp, m_w_pool, m_pool_scale, m_w_out, m_g_mix_post, m_g_ffn_pre, m_w_ffn_gate, m_w_ffn_up, m_w_ffn_down, m_g_ffn_post, m_w_ple_proj, m_g_ple, m_w_ple_gate, v_g_mix_pre, v_w_in, v_b_forget, v_g_attn_grp, v_g_pool_grp, v_w_pool, v_pool_scale, v_w_out, v_g_mix_post, v_g_ffn_pre, v_w_ffn_gate, v_w_ffn_up, v_w_ffn_down, v_g_ffn_post, v_w_ple_proj, v_g_ple, v_w_ple_gate):
    small = dict(w_pool=w_pool[0], g_mix_pre=g_mix_pre, g_mix_post=g_mix_post, g_ffn_pre=g_ffn_pre,
                 g_ffn_post=g_ffn_post, g_ple=g_ple, g_attn_grp=g_attn_grp, g_pool_grp=g_pool_grp,
                 pool_scale=pool_scale, b_forget=b_forget)
    gx, small_all, upd_in, upd_rest = _step(
        x[0], p[0, 0], loss_target[0], small, (w_in, m_w_in, v_w_in),
        [(w_out, m_w_out, v_w_out), (w_ple_gate, m_w_ple_gate, v_w_ple_gate),
         (w_ffn_gate, m_w_ffn_gate, v_w_ffn_gate), (w_ffn_up, m_w_ffn_up, v_w_ffn_up),
         (w_ffn_down, m_w_ffn_down, v_w_ffn_down), (w_ple_proj, m_w_ple_proj, v_w_ple_proj)])

    sm_w = _pack_small(w_pool, g_mix_pre, g_mix_post, g_ffn_pre, g_ffn_post, g_ple, g_attn_grp, g_pool_grp, pool_scale, b_forget)
    sm_m = _pack_small(m_w_pool, m_g_mix_pre, m_g_mix_post, m_g_ffn_pre, m_g_ffn_post, m_g_ple, m_g_attn_grp, m_g_pool_grp, m_pool_scale, m_b_forget)
    sm_v = _pack_small(v_w_pool, v_g_mix_pre, v_g_mix_post, v_g_ffn_pre, v_g_ffn_post, v_g_ple, v_g_attn_grp, v_g_pool_grp, v_pool_scale, v_b_forget)
    upd_small = _reduce_update_small(small_all, sm_w, sm_m, sm_v)
    loss = upd_small[0][ROW_LOSS, 0]

    def leaves(k):
        b_out, b_pg, b_gate, b_up, b_down, b_ple = (upd[k] for upd in upd_rest)
        s = _unpack_small(upd_small[k])
        return (s["g_mix_pre"], upd_in[k], s["b_forget"], s["g_attn_grp"], s["g_pool_grp"], s["w_pool"],
                s["pool_scale"], b_out, s["g_mix_post"], s["g_ffn_pre"], b_gate, b_up, b_down, s["g_ffn_post"], b_ple,
                s["g_ple"], b_pg)

    return (loss, gx[None], *leaves(0), *leaves(1), *leaves(2), *leaves(3))
```

```python
import functools

import jax
import jax.numpy as jnp
from jax import lax
from jax.experimental import pallas as pl
from jax.experimental.pallas import tpu as pltpu

F32 = jnp.float32
BF16 = jnp.bfloat16
HIGHEST = lax.Precision.HIGHEST

D_MODEL = 1024
HEADS = 8
HEAD_DIM = 64
D_ATTN = HEADS * HEAD_DIM
POOL_WINDOWS = (2, 4, 8, 16)
POOL_CH = 128
D_POOL = POOL_CH * len(POOL_WINDOWS)
D_FF = 2816
D_PLE = 256
D_IN = 3 * D_ATTN + HEADS + D_POOL
RMS_EPS = 1e-6
N_DEV = 8

ADAM_LR = 0.001
ADAM_B1 = 0.9
ADAM_B2 = 0.999
ADAM_EPS = 1e-08
ADAM_WD = 0.01
ADAM_STEP = 10

LANES = 128
HALO = 16
TS = 512
TS_FF = 512
TS_WGRAD = 1024
TM_WGRAD = 2176
TQ = 256
TN_FF = 1408
NEG = -1e30
VMEM_LIMIT = 56 * 1024 * 1024

SHARD_IN = 257
ROWS_IN = 272
SHARD_FF = 352
OFF_PG = 128
OFF_PLE = 256
OFF_GATE = SHARD_FF
OFF_UP = 2 * SHARD_FF
OFF_DOWN = 3 * SHARD_FF
ROWS_REST = 4 * SHARD_FF
TR_REST = SHARD_FF

SMALL_ROWS = 72
ROW_G_MIX_PRE, ROW_G_MIX_POST, ROW_G_FFN_PRE, ROW_G_FFN_POST, ROW_G_PLE = 64, 65, 66, 67, 68
ROW_GROUP_GAINS, ROW_MISC = 69, 70
COL_B_FORGET = D_POOL
COL_LOSS = D_POOL + HEADS


def _nn(a, b):
    return jnp.dot(a, b, preferred_element_type=F32)


def _nt(a, b):
    return lax.dot_general(a, b, (((1,), (1,)), ((), ())), preferred_element_type=F32)


def _tn(a, b):
    return lax.dot_general(a, b, (((0,), (0,)), ((), ())), preferred_element_type=F32)


def _rstd(v):
    return lax.rsqrt(jnp.mean(v * v, axis=-1, keepdims=True) + RMS_EPS)


def _rms_bwd(v, g, dy):
    r = _rstd(v)
    vh = v * r
    t = dy * g
    dv = r * (t - vh * jnp.mean(t * vh, axis=-1, keepdims=True))
    return dv, jnp.sum(dy * vh, axis=0, keepdims=True)


def _split3(v):
    hi = v.astype(BF16)
    rest = v - hi.astype(F32)
    mid = rest.astype(BF16)
    return hi, mid, (rest - mid.astype(F32)).astype(BF16)


def _mask_matmul(mask, v):
    hi, mid, lo = _split3(v)
    return _nn(mask, lo) + _nn(mask, mid) + _nn(mask, hi)


def _params(n_grid):
    return pltpu.CompilerParams(dimension_semantics=("arbitrary",) * n_grid, vmem_limit_bytes=VMEM_LIMIT)


def _row(i):
    return (i, 0)


def _fixed(*_):
    return (0, 0)


def _spec_square(part):
    return pl.BlockSpec((N_DEV, 128, D_MODEL), lambda *_: (0, part, 0))


def _spec_ff(part):
    return pl.BlockSpec((TN_FF // SHARD_FF, SHARD_FF, D_MODEL), lambda i, j: (j, part, 0))


assert TS == 2 * TQ and TN_FF % SHARD_FF == 0
_HALVES = (slice(0, TQ), slice(TQ, TS))

VMEM_WHOLE = pl.BlockSpec(memory_space=pltpu.VMEM)
SMEM_WHOLE = pl.BlockSpec(memory_space=pltpu.SMEM)
ANY = pl.BlockSpec(memory_space=pl.ANY)


LOG2E = 1.4426950408889634
VROWS = HEAD_DIM + 16
AUG = 128
BIAS_LANE = HEAD_DIM
ONE_LANE = HEAD_DIM + 3
SPARE_LANE = HEADS


def _attn_layout_constants():
    import numpy as np
    place = np.zeros((D_ATTN, HEADS * AUG), np.float32)
    for r in range(D_ATTN):
        place[r, (r // HEAD_DIM) * AUG + r % HEAD_DIM] = 1.0
    bias_k = np.zeros((3, LANES, HEADS * AUG), np.float32)
    bias_q = np.zeros((3, LANES, HEADS * AUG), np.float32)
    for h in range(HEADS):
        for part in range(3):
            bias_k[part, h, h * AUG + BIAS_LANE + part] = -1.0
            bias_q[part, h, h * AUG + ONE_LANE + part] = 1.0
            bias_k[0, SPARE_LANE, h * AUG + ONE_LANE + part] = 1.0
            bias_q[0, SPARE_LANE, h * AUG + BIAS_LANE + part] = 1.0
    as_bf = lambda a: jnp.asarray(a, BF16)
    return dict(place=as_bf(place), place_t=as_bf(place.T), bias_k=as_bf(bias_k),
                bias_q_t=as_bf(bias_q.transpose(0, 2, 1)))


def _pre_attn_fwd(x, g1, wqkv, wf, wu, bpad, wpool, lay, own_block):
    s, d = x.shape
    nt = s // TS
    sub = TS // TQ

    def body(x_ref, g_ref, wqkv_ref, wf_ref, wu_ref, b_ref, wp_ref, place_ref, place_t_ref, bk_ref, bqt_ref, own_ref,
             hn_ref, qt_ref, ka_ref, v_ref, qat_ref, vt_ref, kt_ref, fl_ref, y_ref, mp_ref, all_ref,
             ubuf, ccar, cbuf, stage, send_sems, recv_sems, local_sem):
        i = pl.program_id(0)

        @pl.when(i == 0)
        def _():
            _gather_start(own_ref, all_ref, stage, send_sems, recv_sems, local_sem)
            ubuf[0:HALO, :] = jnp.zeros((HALO, D_POOL), F32)
            ccar[...] = jnp.zeros_like(ccar)

        @pl.when(i == max(nt - 2, 0))
        def _():
            _gather_pass_on(all_ref, send_sems, recv_sems)

        xv = x_ref[...]
        hn = (xv * _rstd(xv) * g_ref[...]).astype(BF16)
        hn_ref[...] = hn
        zq = _nt(hn, wqkv_ref[...])
        qt = (zq[:, 0:D_ATTN] * 0.125).astype(BF16).T
        qb = (zq[:, 0:D_ATTN] * (0.125 * LOG2E)).astype(BF16)
        kb = zq[:, D_ATTN:2 * D_ATTN].astype(BF16)
        vb = zq[:, 2 * D_ATTN:3 * D_ATTN].astype(BF16)
        v_ref[...] = vb

        fl = _nt(hn, wf_ref[...]) + b_ref[...]
        fl_ref[...] = fl
        logf = jax.nn.log_sigmoid(fl)
        rr = lax.broadcasted_iota(jnp.int32, (TS, TS), 0)
        cc = lax.broadcasted_iota(jnp.int32, (TS, TS), 1)
        c = _mask_matmul((cc <= rr).astype(BF16), logf) + ccar[...]
        cbuf[...] = c
        ccar[...] = cbuf[TS - 1:TS, :]
        hi, mid, lo = _split3(c * LOG2E)
        lane = lax.broadcasted_iota(jnp.int32, (TS, LANES), 1)
        parts = (jnp.where(lane == SPARE_LANE, 1.0, hi).astype(BF16), mid, lo)
        ka = _nn(kb, place_ref[...])
        qat = _nt(place_t_ref[...], qb)
        for part in range(3):
            ka = ka + _nn(parts[part], bk_ref[part])
            qat = qat + _nt(bqt_ref[part], parts[part])
        ka_ref[...] = ka.astype(BF16)
        qat = qat.astype(BF16)
        vt = vb.T
        kt = kb.T
        for a in range(sub):
            cols = slice(a * TQ, (a + 1) * TQ)
            qat_ref[a] = qat[:, cols]
            for ref, mat in ((qt_ref, qt), (kt_ref, kt), (vt_ref, vt)):
                for h in range(HEADS):
                    ref[a, h * VROWS:h * VROWS + HEAD_DIM, :] = mat[h * HEAD_DIM:(h + 1) * HEAD_DIM, cols]
                    ref[a, h * VROWS + HEAD_DIM:(h + 1) * VROWS, :] = jnp.ones((VROWS - HEAD_DIM, TQ), BF16)

        u = _nt(hn, wu_ref[...])
        ubuf[HALO:HALO + TS, :] = u
        t = i * TS + lax.broadcasted_iota(jnp.int32, (TS, 1), 0)
        for g, w in enumerate(POOL_WINDOWS):
            cols = slice(g * POOL_CH, (g + 1) * POOL_CH)
            sm = ubuf[:, cols]
            step = 1
            while step < w:
                sm = sm + pltpu.roll(sm, step, 0)
                step *= 2
            cnt = jnp.minimum(t + 1, w).astype(F32)
            yg = (sm[HALO:, :] / cnt - u[:, cols]).astype(BF16)
            y_ref[:, cols] = yg
            mp_ref[:, cols] = _nn(yg, wp_ref[g])
        ubuf[0:HALO, :] = u[TS - HALO:, :]

        @pl.when(i == nt - 1)
        def _():
            _gather_finish(own_ref, all_ref, send_sems, recv_sems)

    nq = s // TQ
    aug = HEADS * AUG
    outs = (
        jax.ShapeDtypeStruct((s, d), BF16), jax.ShapeDtypeStruct((nq, HEADS * VROWS, TQ), BF16),
        jax.ShapeDtypeStruct((s, aug), BF16), jax.ShapeDtypeStruct((s, D_ATTN), BF16),
        jax.ShapeDtypeStruct((nq, aug, TQ), BF16), jax.ShapeDtypeStruct((nq, HEADS * VROWS, TQ), BF16),
        jax.ShapeDtypeStruct((nq, HEADS * VROWS, TQ), BF16),
        jax.ShapeDtypeStruct((s, LANES), F32),
        jax.ShapeDtypeStruct((s, D_POOL), BF16), jax.ShapeDtypeStruct((s, D_POOL), F32),
        jax.ShapeDtypeStruct((N_DEV,) + own_block.shape, own_block.dtype),
    )
    fixed3 = lambda i: (0, 0, 0)
    tiles3 = lambda rows: pl.BlockSpec((sub, rows, TQ), lambda i: (i, 0, 0))
    return pl.pallas_call(
        body, grid=(nt,), out_shape=outs, name="pre_attn_fwd",
        in_specs=[pl.BlockSpec((TS, d), _row), pl.BlockSpec((1, d), _fixed),
                  pl.BlockSpec((3 * D_ATTN, d), _fixed), pl.BlockSpec(wf.shape, _fixed), pl.BlockSpec(wu.shape, _fixed),
                  pl.BlockSpec((1, LANES), _fixed), pl.BlockSpec(wpool.shape, fixed3),
                  pl.BlockSpec(lay["place"].shape, _fixed), pl.BlockSpec(lay["place_t"].shape, _fixed),
                  pl.BlockSpec(lay["bias_k"].shape, fixed3), pl.BlockSpec(lay["bias_q_t"].shape, fixed3), ANY],
        out_specs=(pl.BlockSpec((TS, d), _row), tiles3(HEADS * VROWS),
                   pl.BlockSpec((TS, aug), _row), pl.BlockSpec((TS, D_ATTN), _row),
                   tiles3(aug), tiles3(HEADS * VROWS), tiles3(HEADS * VROWS),
                   pl.BlockSpec((TS, LANES), _row),
                   pl.BlockSpec((TS, D_POOL), _row), pl.BlockSpec((TS, D_POOL), _row), ANY),
        scratch_shapes=[pltpu.VMEM((TS + HALO, D_POOL), F32), pltpu.VMEM((1, LANES), F32), pltpu.VMEM((TS, LANES), F32),
                        pltpu.VMEM(own_block.shape, own_block.dtype),
                        pltpu.SemaphoreType.DMA((7,)), pltpu.SemaphoreType.DMA((7,)), pltpu.SemaphoreType.DMA],
        compiler_params=_params(1),
    )(x, g1, wqkv, wf, wu, bpad, wpool, lay["place"], lay["place_t"], lay["bias_k"], lay["bias_q_t"], own_block)


def _causal_in_tile():
    krow = lax.broadcasted_iota(jnp.int32, (TQ, TQ), 0)
    qcol = lax.broadcasted_iota(jnp.int32, (TQ, TQ), 1)
    return krow <= qcol


def _attn_fwd(ka, qat3, vt3, own_block):
    s = ka.shape[0]
    nq = s // TQ
    pass_on_step = max(nq - 2, 0)

    def body(qa_ref, ka_ref, vt_ref, own_ref, a_ref, lset_ref, all_ref, acc, out_t, st_scr, pt_scr,
             stage, send_sems, recv_sems, local_sem):
        i = pl.program_id(0)

        @pl.when(i == 0)
        def _():
            _gather_start(own_ref, all_ref, stage, send_sems, recv_sems, local_sem)

        @pl.when(i == pass_on_step)
        def _():
            _gather_pass_on(all_ref, send_sems, recv_sems)

        acc[...] = jnp.zeros_like(acc)

        def tile(j, stats, masked):
            tile_max = []
            for h in range(HEADS):
                aug = slice(h * AUG, (h + 1) * AUG)
                st = _nn(ka_ref[pl.ds(j * TQ, TQ), aug], qa_ref[0, aug, :])
                if masked:
                    st = jnp.where(_causal_in_tile(), st, NEG)
                st_scr[h] = st
                tile_max.append(jnp.max(st, axis=0, keepdims=True))
            new, scale = [], []
            for h in range(HEADS):
                m_new = jnp.maximum(stats[h], tile_max[h])
                scale.append(jnp.exp2(stats[h] - m_new))
                pt_scr[h] = jnp.exp2(st_scr[h] - m_new).astype(BF16)
                new.append(m_new)
            for h in range(HEADS):
                rows = slice(h * VROWS, (h + 1) * VROWS)
                acc[rows, :] = scale[h] * acc[rows, :] + _nn(vt_ref[j, rows, :], pt_scr[h])
            return tuple(new)

        init = tuple(jnp.full((1, TQ), NEG, F32) for _ in range(HEADS))
        stats = lax.fori_loop(0, i, functools.partial(tile, masked=False), init)
        stats = tile(i, stats, True)
        for h in range(HEADS):
            denom = acc[h * VROWS + HEAD_DIM:h * VROWS + HEAD_DIM + 1, :]
            out_t[h * HEAD_DIM:(h + 1) * HEAD_DIM, :] = acc[h * VROWS:h * VROWS + HEAD_DIM, :] / denom
            lset_ref[0, h:h + 1, :] = stats[h] + jnp.log2(denom)
        a_ref[...] = out_t[...].T

        @pl.when(i == nq - 1)
        def _():
            _gather_finish(own_ref, all_ref, send_sems, recv_sems)

    r, cdim = own_block.shape
    return pl.pallas_call(
        body, grid=(nq,), name="attn_fwd",
        out_shape=(jax.ShapeDtypeStruct((s, D_ATTN), F32), jax.ShapeDtypeStruct((nq, HEADS, TQ), F32),
                   jax.ShapeDtypeStruct((N_DEV, r, cdim), own_block.dtype)),
        in_specs=[pl.BlockSpec((1, HEADS * AUG, TQ), lambda i: (i, 0, 0)), VMEM_WHOLE, VMEM_WHOLE, ANY],
        out_specs=(pl.BlockSpec((TQ, D_ATTN), _row), pl.BlockSpec((1, HEADS, TQ), lambda i: (i, 0, 0)), ANY),
        scratch_shapes=[pltpu.VMEM((HEADS * VROWS, TQ), F32), pltpu.VMEM((D_ATTN, TQ), F32),
                        pltpu.VMEM((HEADS, TQ, TQ), F32), pltpu.VMEM((HEADS, TQ, TQ), BF16),
                        pltpu.VMEM((r, cdim), own_block.dtype),
                        pltpu.SemaphoreType.DMA((7,)), pltpu.SemaphoreType.DMA((7,)), pltpu.SemaphoreType.DMA],
        compiler_params=_params(1),
    )(qat3, ka, vt3, own_block)


def _post_attn_fwd(a, mpre, x, g_attn, g_pool, pscale, wout, g_post, g_ffn_pre):
    s, d = x.shape

    def body(a_ref, mp_ref, x_ref, ga_ref, gp_ref, ps_ref, wo_ref, gpost_ref, gpre_ref,
             mix_ref, o_ref, h1_ref, hn2_ref):
        for rows in _HALVES:
            av = a_ref[rows, :]
            mix_ref[rows, 0:D_ATTN] = (av * _rstd(av) * ga_ref[...]).astype(BF16)
            mv = mp_ref[rows, :] * ps_ref[...]
            mix_ref[rows, D_ATTN:] = (mv * _rstd(mv) * gp_ref[...]).astype(BF16)
            o = _nn(mix_ref[rows, :], wo_ref[...].reshape(d, d))
            o_ref[rows, :] = o
            h1 = x_ref[rows, :] + o * _rstd(o) * gpost_ref[...]
            h1_ref[rows, :] = h1
            hn2_ref[rows, :] = (h1 * _rstd(h1) * gpre_ref[...]).astype(BF16)

    vec = lambda n: pl.BlockSpec((1, n), _fixed)
    return pl.pallas_call(
        body, grid=(s // TS,), name="post_attn_fwd",
        out_shape=(jax.ShapeDtypeStruct((s, d), BF16), jax.ShapeDtypeStruct((s, d), F32),
                   jax.ShapeDtypeStruct((s, d), F32), jax.ShapeDtypeStruct((s, d), BF16)),
        in_specs=[pl.BlockSpec((TS, D_ATTN), _row), pl.BlockSpec((TS, D_POOL), _row), pl.BlockSpec((TS, d), _row),
                  vec(D_ATTN), vec(D_POOL), vec(D_POOL), _spec_square(0), vec(d), vec(d)],
        out_specs=(pl.BlockSpec((TS, d), _row),) * 4,
        compiler_params=_params(1),
    )(a, mpre, x, g_attn, g_pool, pscale, wout, g_post, g_ffn_pre)


def _ffn_fwd(hn2, wg, wu, wd, h1, g_post):
    s, d = h1.shape
    nc = D_FF // TN_FF
    ts = min(TS_FF, s)

    def body(hn_ref, wg_ref, wu_ref, wd_ref, h1_ref, g_ref, gate_ref, up_ref, act_ref, ff_ref, h2_ref, acc):
        j = pl.program_id(1)

        @pl.when(j == 0)
        def _():
            acc[...] = jnp.zeros_like(acc)

        for r in range(2):
            rows = slice(r * (ts // 2), (r + 1) * (ts // 2))
            hn = hn_ref[rows, :]
            gt = _nt(hn, wg_ref[...].reshape(TN_FF, d))
            up = _nt(hn, wu_ref[...].reshape(TN_FF, d))
            gate_ref[rows, :] = gt.astype(BF16)
            up_ref[rows, :] = up.astype(BF16)
            act_ref[rows, :] = (gt * jax.nn.sigmoid(gt) * up).astype(BF16)
            acc[rows, :] += _nn(act_ref[rows, :], wd_ref[...].reshape(TN_FF, d))

        @pl.when(j == nc - 1)
        def _():
            ff = acc[...]
            ff_ref[...] = ff
            h2_ref[...] = h1_ref[...] + ff * _rstd(ff) * g_ref[...]

    rowblk = pl.BlockSpec((ts, d), lambda i, j: (i, 0))
    chunk = pl.BlockSpec((ts, TN_FF), lambda i, j: (i, j))
    return pl.pallas_call(
        body, grid=(s // ts, nc), name="ffn_fwd",
        out_shape=(jax.ShapeDtypeStruct((s, D_FF), BF16),) * 3 + (jax.ShapeDtypeStruct((s, d), F32),) * 2,
        in_specs=[rowblk, _spec_ff(0), _spec_ff(1), _spec_ff(2), rowblk, pl.BlockSpec((1, d), lambda i, j: (0, 0))],
        out_specs=(chunk, chunk, chunk, rowblk, rowblk),
        scratch_shapes=[pltpu.VMEM((ts, d), F32)],
        compiler_params=_params(2),
    )(hn2, wg, wu, wd, h1, g_post)


def _tail_fwd_bwd(h2, p, tgt, ff, wple, wpg, g_ple, g_ffn_post):
    s, d = h2.shape

    def body(h2_ref, p_ref, t_ref, ff_ref, wple_ref, wpg_ref, gple_ref, gfp_ref,
             dh2_ref, dff_ref, dgl_ref, dpp_ref, h2b_ref, pb_ref, loss_ref, dgple_ref, dgfp_ref):
        i = pl.program_id(0)

        @pl.when(i == 0)
        def _():
            loss_ref[...] = jnp.zeros_like(loss_ref)
            dgple_ref[...] = jnp.zeros_like(dgple_ref)
            dgfp_ref[...] = jnp.zeros_like(dgfp_ref)

        h2 = h2_ref[...]
        h2b = h2.astype(BF16)
        h2b_ref[...] = h2b
        pb = p_ref[...].astype(BF16)
        pb_ref[...] = pb
        pp = _nt(pb, wple_ref[...])
        gple = gple_ref[...]
        e = pp * _rstd(pp) * gple
        wpg = wpg_ref[...].reshape(d, d)
        sg = jax.nn.sigmoid(_nn(h2b, wpg))
        diff = h2 + sg * e - t_ref[...]
        sq = jnp.sum(jnp.sum(diff * diff, axis=1, keepdims=True), axis=0, keepdims=True)
        loss_ref[...] += jnp.broadcast_to(sq * (0.5 / d), loss_ref.shape)
        dh3 = diff * (1.0 / d)
        dgl = (dh3 * e * sg * (1.0 - sg)).astype(BF16)
        dgl_ref[...] = dgl
        dh2 = dh3 + _nt(dgl, wpg)
        dh2_ref[...] = dh2
        dpp, dg = _rms_bwd(pp, gple, dh3 * sg)
        dpp_ref[...] = dpp.astype(BF16)
        dgple_ref[...] += dg
        dff, dg = _rms_bwd(ff_ref[...], gfp_ref[...], dh2)
        dff_ref[...] = dff.astype(BF16)
        dgfp_ref[...] += dg

    rowblk = pl.BlockSpec((TS, d), _row)
    vec = pl.BlockSpec((1, d), _fixed)
    return pl.pallas_call(
        body, grid=(s // TS,), name="tail_fwd_bwd",
        out_shape=(jax.ShapeDtypeStruct((s, d), F32), jax.ShapeDtypeStruct((s, d), BF16),
                   jax.ShapeDtypeStruct((s, d), BF16), jax.ShapeDtypeStruct((s, d), BF16),
                   jax.ShapeDtypeStruct((s, d), BF16), jax.ShapeDtypeStruct((s, D_PLE), BF16),
                   jax.ShapeDtypeStruct((8, LANES), F32), jax.ShapeDtypeStruct((1, d), F32),
                   jax.ShapeDtypeStruct((1, d), F32)),
        in_specs=[rowblk, pl.BlockSpec((TS, D_PLE), _row), rowblk, rowblk,
                  pl.BlockSpec(wple.shape, _fixed), _spec_square(1), vec, vec],
        out_specs=(rowblk, rowblk, rowblk, rowblk, rowblk, pl.BlockSpec((TS, D_PLE), _row),
                   pl.BlockSpec((8, LANES), _fixed), vec, vec),
        compiler_params=_params(1),
    )(h2, p, tgt, ff, wple, wpg, g_ple, g_ffn_post)


def _ffn_bwd(dff, gate, up, wd, wg, wu, h1, dh2, g_pre):
    s, d = h1.shape
    nc = D_FF // TN_FF
    ts = min(TS_FF, s)

    def body(dff_ref, gate_ref, up_ref, wd_ref, wg_ref, wu_ref, h1_ref, dh2_ref, g_ref,
             dgate_ref, dup_ref, dh1_ref, dg_ref, acc):
        i = pl.program_id(0)
        j = pl.program_id(1)

        @pl.when((i == 0) & (j == 0))
        def _():
            dg_ref[...] = jnp.zeros_like(dg_ref)

        @pl.when(j == 0)
        def _():
            acc[...] = jnp.zeros_like(acc)

        for r in range(2):
            rows = slice(r * (ts // 2), (r + 1) * (ts // 2))
            dact = _nt(dff_ref[rows, :], wd_ref[...].reshape(TN_FF, d))
            gt = gate_ref[rows, :].astype(F32)
            sg = jax.nn.sigmoid(gt)
            dup_ref[rows, :] = (dact * gt * sg).astype(BF16)
            dgate_ref[rows, :] = (dact * up_ref[rows, :].astype(F32) * (sg * (1.0 + gt * (1.0 - sg)))).astype(BF16)
            acc[rows, :] += (_nn(dgate_ref[rows, :], wg_ref[...].reshape(TN_FF, d))
                             + _nn(dup_ref[rows, :], wu_ref[...].reshape(TN_FF, d)))

        @pl.when(j == nc - 1)
        def _():
            dv, dg = _rms_bwd(h1_ref[...], g_ref[...], acc[...])
            dh1_ref[...] = dh2_ref[...] + dv
            dg_ref[...] += dg

    rowblk = pl.BlockSpec((ts, d), lambda i, j: (i, 0))
    chunk = pl.BlockSpec((ts, TN_FF), lambda i, j: (i, j))
    vec = pl.BlockSpec((1, d), lambda i, j: (0, 0))
    return pl.pallas_call(
        body, grid=(s // ts, nc), name="ffn_bwd",
        out_shape=(jax.ShapeDtypeStruct((s, D_FF), BF16), jax.ShapeDtypeStruct((s, D_FF), BF16),
                   jax.ShapeDtypeStruct((s, d), F32), jax.ShapeDtypeStruct((1, d), F32)),
        in_specs=[rowblk, chunk, chunk, _spec_ff(2), _spec_ff(0), _spec_ff(1), rowblk, rowblk, vec],
        out_specs=(chunk, chunk, rowblk, vec),
        scratch_shapes=[pltpu.VMEM((ts, d), F32)],
        compiler_params=_params(2),
    )(dff, gate, up, wd, wg, wu, h1, dh2, g_pre)


def _post_attn_bwd(dh1, o, a, mpre, wout, wpool, g_post, g_attn, g_pool, pscale):
    s, d = dh1.shape
    sub = TS // TQ

    def body(dh1_ref, o_ref, a_ref, mp_ref, wo_ref, wp_ref, gpost_ref, ga_ref, gp_ref, ps_ref,
             dob_ref, dat_ref, dlt_ref, dmpb_ref, dy_ref, dgpost_ref, dga_ref, dgp_ref, dps_ref):
        i = pl.program_id(0)

        @pl.when(i == 0)
        def _():
            dgpost_ref[...] = jnp.zeros_like(dgpost_ref)
            dga_ref[...] = jnp.zeros_like(dga_ref)
            dgp_ref[...] = jnp.zeros_like(dgp_ref)
            dps_ref[...] = jnp.zeros_like(dps_ref)

        do, dg = _rms_bwd(o_ref[...], gpost_ref[...], dh1_ref[...])
        dgpost_ref[...] += dg
        dob = do.astype(BF16)
        dob_ref[...] = dob
        dmix = _nt(dob, wo_ref[...].reshape(d, d))

        av = a_ref[...]
        da, dg = _rms_bwd(av, ga_ref[...], dmix[:, 0:D_ATTN])
        dga_ref[...] += dg
        dat = da.astype(BF16).T
        hsel = (lax.shift_right_logical(lax.broadcasted_iota(jnp.int32, (HEADS, D_ATTN), 1), 6)
                == lax.broadcasted_iota(jnp.int32, (HEADS, D_ATTN), 0)).astype(F32)
        dlt = lax.dot_general(hsel, da * av, (((1,), (1,)), ((), ())), precision=HIGHEST, preferred_element_type=F32)
        for q in range(sub):
            dlt_ref[q] = dlt[:, q * TQ:(q + 1) * TQ]
            dat_ref[q] = dat[:, q * TQ:(q + 1) * TQ]

        ps = ps_ref[...]
        mp = mp_ref[...]
        dm, dg = _rms_bwd(mp * ps, gp_ref[...], dmix[:, D_ATTN:])
        dgp_ref[...] += dg
        dps_ref[...] += jnp.sum(dm * mp, axis=0, keepdims=True)
        dmpb = (dm * ps).astype(BF16)
        dmpb_ref[...] = dmpb
        for g in range(len(POOL_WINDOWS)):
            cols = slice(g * POOL_CH, (g + 1) * POOL_CH)
            dy_ref[:, cols] = _nt(dmpb[:, cols], wp_ref[g])

    rowblk = pl.BlockSpec((TS, d), _row)
    half = pl.BlockSpec((TS, D_ATTN), _row)
    vec = lambda n: pl.BlockSpec((1, n), _fixed)
    return pl.pallas_call(
        body, grid=(s // TS,), name="post_attn_bwd",
        out_shape=(jax.ShapeDtypeStruct((s, d), BF16), jax.ShapeDtypeStruct((s // TQ, D_ATTN, TQ), BF16),
                   jax.ShapeDtypeStruct((s // TQ, HEADS, TQ), F32), jax.ShapeDtypeStruct((s, D_POOL), BF16),
                   jax.ShapeDtypeStruct((s, D_POOL), F32), jax.ShapeDtypeStruct((1, d), F32),
                   jax.ShapeDtypeStruct((1, D_ATTN), F32), jax.ShapeDtypeStruct((1, D_POOL), F32),
                   jax.ShapeDtypeStruct((1, D_POOL), F32)),
        in_specs=[rowblk, rowblk, half, half, _spec_square(0),
                  pl.BlockSpec(wpool.shape, lambda i: (0, 0, 0)), vec(d), vec(D_ATTN), vec(D_POOL), vec(D_POOL)],
        out_specs=(rowblk, pl.BlockSpec((sub, D_ATTN, TQ), lambda i: (i, 0, 0)),
                   pl.BlockSpec((sub, HEADS, TQ), lambda i: (i, 0, 0)), half, half,
                   vec(d), vec(D_ATTN), vec(D_POOL), vec(D_POOL)),
        compiler_params=_params(1),
    )(dh1, o, a, mpre, wout, wpool, g_post, g_attn, g_pool, pscale)


def _attn_bwd(ka, v, kt3, qat3, qt3, dot3, lset3, dlt3, chip_blocks):
    s = ka.shape[0]
    nq = s // TQ

    def body(ka_ref, v_ref, kt_ref, qat_ref, qt_ref, dot_ref, lset_ref, dlt_ref, b_ref,
             dqt_ref, dkt_ref, dvt_ref, got_ref, pt_scr, ptb_scr, dsb_scr,
             stage, send_sems, recv_sems, local_sem):
        j = pl.program_id(0)

        @pl.when(j == 0)
        def _():
            _chips_start(b_ref, got_ref, stage, send_sems, recv_sems, local_sem)
            dqt_ref[...] = jnp.zeros_like(dqt_ref)

        def tile(i, masked):
            def accumulate(ref, idx, val):
                if masked:
                    ref[idx] = val
                else:
                    ref[idx] += val

            for h in range(HEADS):
                aug = slice(h * AUG, (h + 1) * AUG)
                st = _nn(ka_ref[:, aug], qat_ref[i, aug, :]) - lset_ref[i, h:h + 1, :]
                if masked:
                    st = jnp.where(_causal_in_tile(), st, NEG)
                pt = jnp.exp2(st)
                pt_scr[h] = pt
                ptb_scr[h] = pt.astype(BF16)
            heads = [(h, slice(h * HEAD_DIM, (h + 1) * HEAD_DIM)) for h in range(HEADS)]
            for h, hs in heads:
                dst = pt_scr[h] * (_nn(v_ref[:, hs], dot_ref[i, hs, :]) - dlt_ref[i, h:h + 1, :])
                dsb_scr[h] = dst.astype(BF16)
            for h, hs in heads:
                accumulate(dvt_ref, (0, hs, slice(None)), _nt(dot_ref[i, hs, :], ptb_scr[h]))
            for h, hs in heads:
                rows = slice(h * VROWS, (h + 1) * VROWS)
                accumulate(dkt_ref, (0, rows, slice(None)), _nt(qt_ref[i, rows, :], dsb_scr[h]))
            for h, hs in heads:
                rows = slice(h * VROWS, (h + 1) * VROWS)
                dqt_ref[i, rows, :] += _nn(kt_ref[0, rows, :], dsb_scr[h])

        first = j + 1
        pairs = (nq - first) // 2

        def step(p, carry):
            tile(first + 2 * p, False)
            tile(first + 2 * p + 1, False)
            return carry

        tile(j, True)
        lax.fori_loop(0, pairs, step, 0)

        @pl.when(first + 2 * pairs < nq)
        def _():
            tile(nq - 1, False)

        @pl.when(j == nq - 1)
        def _():
            _chips_finish(b_ref, got_ref, send_sems, recv_sems)

    blk = pl.BlockSpec((TQ, D_ATTN), _row)
    tile_t = lambda rows: pl.BlockSpec((1, rows, TQ), lambda j: (j, 0, 0))
    per_tile = lambda rows: jax.ShapeDtypeStruct((nq, rows, TQ), F32)
    _, r, cdim = chip_blocks.shape
    return pl.pallas_call(
        body, grid=(nq,), name="attn_bwd",
        out_shape=(per_tile(HEADS * VROWS), per_tile(HEADS * VROWS), per_tile(D_ATTN),
                   jax.ShapeDtypeStruct(chip_blocks.shape, chip_blocks.dtype)),
        in_specs=[pl.BlockSpec((TQ, HEADS * AUG), _row), blk, tile_t(HEADS * VROWS),
                  VMEM_WHOLE, VMEM_WHOLE, VMEM_WHOLE, VMEM_WHOLE, VMEM_WHOLE, ANY],
        out_specs=(pl.BlockSpec((nq, HEADS * VROWS, TQ), lambda j: (0, 0, 0)), tile_t(HEADS * VROWS), tile_t(D_ATTN),
                   ANY),
        scratch_shapes=[pltpu.VMEM((HEADS, TQ, TQ), F32), pltpu.VMEM((HEADS, TQ, TQ), BF16),
                        pltpu.VMEM((HEADS, TQ, TQ), BF16), pltpu.VMEM((r, cdim), chip_blocks.dtype),
                        pltpu.SemaphoreType.DMA((3,)), pltpu.SemaphoreType.DMA((3,)), pltpu.SemaphoreType.DMA],
        compiler_params=_params(1),
    )(ka, v, kt3, qat3, qt3, dot3, lset3, dlt3, chip_blocks)


def _pre_attn_bwd(dqt3, dkt3, dvt3, fl, dy, x, dh1, g1, wqkv, wf, wu):
    s, d = x.shape
    nt = s // TS
    n = TS + HALO
    sub = TS // TQ
    qkv, fcols = 3 * D_ATTN, 3 * D_ATTN + LANES

    def body(dqt_ref, dkt_ref, dvt_ref, fl_ref, dy_ref, x_ref, dh1_ref, g_ref, wqkv_ref, wf_ref, wu_ref,
             gx_ref, dz_ref, dg_ref, db_ref, ybuf, ccar, dlog, dsum):
        dqkv_ref = dz_ref.at[:, 0:qkv]
        dfb_ref = dz_ref.at[:, qkv:fcols]
        dub_ref = dz_ref.at[:, fcols:]
        i = pl.program_id(0)
        ti = nt - 1 - i

        @pl.when(i == 0)
        def _():
            ybuf[TS:n, :] = jnp.zeros((HALO, D_POOL), F32)
            ccar[...] = jnp.zeros_like(ccar)
            dg_ref[...] = jnp.zeros_like(dg_ref)
            db_ref[...] = jnp.zeros_like(db_ref)
            dsum[...] = jnp.zeros_like(dsum)

        for a in range(sub):
            for h in range(HEADS):
                r = h * VROWS + HEAD_DIM
                dsum[h:h + 1, a * TQ:(a + 1) * TQ] = dqt_ref[a, r:r + 1, :] - dkt_ref[a, r:r + 1, :]
        rr = lax.broadcasted_iota(jnp.int32, (TS, TS), 0)
        cc = lax.broadcasted_iota(jnp.int32, (TS, TS), 1)
        dlog[...] = ccar[...] + _mask_matmul((cc >= rr).astype(BF16), dsum[...].T)
        ccar[...] = dlog[0:1, :]
        df = dlog[...] * jax.nn.sigmoid(-fl_ref[...])
        db_ref[...] += jnp.sum(df, axis=0, keepdims=True)
        dfb = df.astype(BF16)
        dfb_ref[...] = dfb

        t = ti * TS + lax.broadcasted_iota(jnp.int32, (TS, 1), 0)
        dy = dy_ref[...]
        for g, w in enumerate(POOL_WINDOWS):
            cols = slice(g * POOL_CH, (g + 1) * POOL_CH)
            ybuf[0:TS, cols] = dy[:, cols] / jnp.minimum(t + 1, w).astype(F32)
        for g, w in enumerate(POOL_WINDOWS):
            cols = slice(g * POOL_CH, (g + 1) * POOL_CH)
            sm = ybuf[:, cols]
            step = 1
            while step < w:
                sm = sm + pltpu.roll(sm, n - step, 0)
                step *= 2
            dub_ref[:, cols] = (sm[0:TS, :] - dy[:, cols]).astype(BF16)
        ybuf[TS:n, :] = ybuf[0:HALO, :]

        for a in range(sub):
            rows = slice(a * TQ, (a + 1) * TQ)
            for h in range(HEADS):
                src = slice(h * VROWS, h * VROWS + HEAD_DIM)
                dqkv_ref[rows, h * HEAD_DIM:(h + 1) * HEAD_DIM] = (dqt_ref[a, src, :].T * 0.125).astype(BF16)
                dqkv_ref[rows, D_ATTN + h * HEAD_DIM:D_ATTN + (h + 1) * HEAD_DIM] = dkt_ref[a, src, :].T.astype(BF16)
            dqkv_ref[rows, 2 * D_ATTN:] = dvt_ref[a].T.astype(BF16)
        dhn = _nn(dqkv_ref[...], wqkv_ref[...]) + _nn(dfb, wf_ref[...]) + _nn(dub_ref[...], wu_ref[...])
        dx, dg = _rms_bwd(x_ref[...], g_ref[...], dhn)
        gx_ref[...] = dh1_ref[...] + dx
        dg_ref[...] += dg

    rev = lambda i: (nt - 1 - i, 0)
    blk = lambda w: pl.BlockSpec((TS, w), rev)
    return pl.pallas_call(
        body, grid=(nt,), name="pre_attn_bwd",
        out_shape=(jax.ShapeDtypeStruct((s, d), F32), jax.ShapeDtypeStruct((s, fcols + D_POOL), BF16),
                   jax.ShapeDtypeStruct((1, d), F32), jax.ShapeDtypeStruct((1, LANES), F32)),
        in_specs=[pl.BlockSpec((sub, HEADS * VROWS, TQ), lambda i: (nt - 1 - i, 0, 0)),
                  pl.BlockSpec((sub, HEADS * VROWS, TQ), lambda i: (nt - 1 - i, 0, 0)),
                  pl.BlockSpec((sub, D_ATTN, TQ), lambda i: (nt - 1 - i, 0, 0)),
                  blk(LANES), blk(D_POOL), blk(d), blk(d),
                  pl.BlockSpec((1, d), _fixed), pl.BlockSpec((qkv, d), _fixed), pl.BlockSpec(wf.shape, _fixed),
                  pl.BlockSpec(wu.shape, _fixed)],
        out_specs=(blk(d), blk(fcols + D_POOL), pl.BlockSpec((1, d), _fixed), pl.BlockSpec((1, LANES), _fixed)),
        scratch_shapes=[pltpu.VMEM((n, D_POOL), F32), pltpu.VMEM((1, LANES), F32), pltpu.VMEM((TS, LANES), F32),
                        pltpu.VMEM((LANES, TS), F32)],
        compiler_params=_params(1),
    )(dqt3, dkt3, dvt3, fl, dy, x, dh1, g1, wqkv, wf, wu)


def _wgrad(a, b, out_dtype, name):
    s, m = a.shape
    n = b.shape[1]
    tm = max(t for t in range(LANES, min(m, TM_WGRAD) + 1, LANES) if m % t == 0)
    ts = min(TS_WGRAD, s)
    ns = s // ts

    def body(a_ref, b_ref, o_ref, acc):
        i = pl.program_id(1)

        @pl.when(i == 0)
        def _():
            acc[...] = jnp.zeros_like(acc)

        acc[...] += _tn(a_ref[...], b_ref[...])

        @pl.when(i == ns - 1)
        def _():
            o_ref[...] = acc[...].astype(out_dtype)

    return pl.pallas_call(
        body, grid=(m // tm, ns), name=name, out_shape=jax.ShapeDtypeStruct((m, n), out_dtype),
        in_specs=[pl.BlockSpec((ts, tm), lambda j, i: (i, j)), pl.BlockSpec((ts, n), lambda j, i: (i, 0))],
        out_specs=pl.BlockSpec((tm, n), lambda j, i: (j, 0)),
        scratch_shapes=[pltpu.VMEM((tm, n), F32)],
        compiler_params=_params(2),
    )(a, b)


def _wgrad_in(dz, hn):
    s, m = dz.shape
    n = hn.shape[1]
    ts = min(TS_WGRAD, s)
    ns = s // ts
    pad_at, pad = 3 * D_ATTN + HEADS, LANES - HEADS
    assert m == D_IN + pad and N_DEV * SHARD_IN == D_IN

    def pieces(d):
        lo, hi = d * SHARD_IN, (d + 1) * SHARD_IN
        spans = [(lo, min(hi, pad_at), 0), (max(lo, pad_at), hi, pad)]
        return [(a + shift, b - a, a - lo) for a, b, shift in spans if b > a]

    def body(a_ref, b_ref, o_ref, acc, stage):
        i = pl.program_id(0)

        @pl.when(i == 0)
        def _():
            acc[...] = jnp.zeros_like(acc)

        acc[...] += _tn(a_ref[...], b_ref[...])

        @pl.when(i == ns - 1)
        def _():
            stage[SHARD_IN:ROWS_IN, :] = jnp.zeros((ROWS_IN - SHARD_IN, n), F32)
            for d in range(N_DEV):
                for src, rows, dst in pieces(d):
                    stage[dst:dst + rows, :] = acc[src:src + rows, :]
                o_ref[d] = stage[...].astype(BF16)

    return pl.pallas_call(
        body, grid=(ns,), name="wgrad_in", out_shape=jax.ShapeDtypeStruct((N_DEV, ROWS_IN, n), BF16),
        in_specs=[pl.BlockSpec((ts, m), _row), pl.BlockSpec((ts, n), _row)],
        out_specs=pl.BlockSpec((N_DEV, ROWS_IN, n), lambda i: (0, 0, 0)),
        scratch_shapes=[pltpu.VMEM((m, n), F32), pltpu.VMEM((ROWS_IN, n), F32)],
        compiler_params=_params(1),
    )(dz, hn)


def _adamw(w, g, m, v):
    m = ADAM_B1 * m + (1.0 - ADAM_B1) * g
    v = ADAM_B2 * v + (1.0 - ADAM_B2) * (g * g)
    m_hat = m / (1.0 - ADAM_B1 ** ADAM_STEP)
    v_hat = v / (1.0 - ADAM_B2 ** ADAM_STEP)
    delta = -ADAM_LR * (m_hat / (jnp.sqrt(v_hat) + ADAM_EPS) + ADAM_WD * w)
    return delta, m, v


def _sum_update(p_ref, w_ref, m_ref, v_ref, g_ref, d_ref, nm_ref, nv_ref):
    g = p_ref[0].astype(F32)
    for k in range(1, p_ref.shape[0]):
        g = g + p_ref[k].astype(F32)
    g_ref[...] = g
    d_ref[...], nm_ref[...], nv_ref[...] = _adamw(w_ref[...], g, m_ref[...], v_ref[...])


def _reduce_update_rest(parts, w, m, v, chip_blocks, small_block):
    nk, r, c = parts.shape
    ns = r // TR_REST

    def body(p_ref, w_ref, m_ref, v_ref, b_ref, sm_ref, g_ref, d_ref, nm_ref, nv_ref, got_ref, all_ref,
             stage_b, stage_s, send_b, recv_b, local_b, send_s, recv_s, local_s):
        i = pl.program_id(0)

        @pl.when(i == 0)
        def _():
            _chips_start(b_ref, got_ref, stage_b, send_b, recv_b, local_b)
            _gather_start(sm_ref, all_ref, stage_s, send_s, recv_s, local_s)

        _sum_update(p_ref, w_ref, m_ref, v_ref, g_ref, d_ref, nm_ref, nv_ref)

        @pl.when(i == ns - 1)
        def _():
            _gather_pass_on(all_ref, send_s, recv_s)
            _chips_finish(b_ref, got_ref, send_b, recv_b)
            _gather_finish(sm_ref, all_ref, send_s, recv_s)

    blk = pl.BlockSpec((TR_REST, c), _row)
    out = jax.ShapeDtypeStruct((r, c), F32)
    dma = pltpu.SemaphoreType.DMA
    return pl.pallas_call(
        body, grid=(ns,), name="reduce_update_rest",
        out_shape=(out,) * 4 + (jax.ShapeDtypeStruct(chip_blocks.shape, chip_blocks.dtype),
                                jax.ShapeDtypeStruct((N_DEV,) + small_block.shape, small_block.dtype)),
        in_specs=[pl.BlockSpec((nk, TR_REST, c), lambda i: (0, i, 0)), blk, blk, blk, ANY, ANY],
        out_specs=(blk,) * 4 + (ANY, ANY),
        scratch_shapes=[pltpu.VMEM(chip_blocks.shape[1:], chip_blocks.dtype), pltpu.VMEM(small_block.shape, small_block.dtype),
                        dma((3,)), dma((3,)), dma, dma((7,)), dma((7,)), dma],
        compiler_params=_params(1),
    )(parts, w, m, v, chip_blocks, small_block)


def _reduce_update_big(parts, w, m, v, tr, name):
    nk, r, c = parts.shape

    def body(p_ref, w_ref, m_ref, v_ref, g_ref, d_ref, nm_ref, nv_ref):
        _sum_update(p_ref, w_ref, m_ref, v_ref, g_ref, d_ref, nm_ref, nv_ref)

    blk = pl.BlockSpec((tr, c), _row)
    out = jax.ShapeDtypeStruct((r, c), F32)
    return pl.pallas_call(
        body, grid=(r // tr,), name=name, out_shape=(out,) * 4,
        in_specs=[pl.BlockSpec((nk, tr, c), lambda i: (0, i, 0)), blk, blk, blk],
        out_specs=(blk,) * 4, compiler_params=_params(1),
    )(parts, w, m, v)


def _reduce_update_small(parts, w, m, v):
    nd = parts.shape[0]

    def body(p_ref, w_ref, m_ref, v_ref, g_ref, d_ref, nm_ref, nv_ref):
        g = p_ref[0]
        for k in range(1, nd):
            g = g + p_ref[k]
        g_ref[...] = g
        d_ref[...], nm_ref[...], nv_ref[...] = _adamw(w_ref[...], g, m_ref[...], v_ref[...])

    out = jax.ShapeDtypeStruct(w.shape, F32)
    return pl.pallas_call(body, name="reduce_update_small", out_shape=(out,) * 4,
                          compiler_params=pltpu.CompilerParams(vmem_limit_bytes=VMEM_LIMIT))(parts, w, m, v)


MESH = pl.DeviceIdType.MESH


def _copy_through_vmem(src_hbm, dst_hbm, stage, sem):
    load = pltpu.make_async_copy(src_hbm, stage, sem)
    load.start()
    load.wait()
    store = pltpu.make_async_copy(stage, dst_hbm, sem)
    store.start()
    store.wait()


class _GatherPlan:
    def __init__(self, x_ref, out_ref, send_sems, recv_sems):
        x, y, c = lax.axis_index("x"), lax.axis_index("y"), lax.axis_index("c")
        self.me, self.sibling, self.c = (x, y, c), (x, y, 1 - c), c
        self.chips = [(1 - x, y), (x, 1 - y), (1 - x, 1 - y)]
        self.x_ref, self.out_ref, self.send_sems, self.recv_sems = x_ref, out_ref, send_sems, recv_sems

    def slot(self, px, py, pc):
        return self.out_ref.at[4 * px + 2 * py + pc]

    def copy(self, k, block, to, src=None):
        return pltpu.make_async_remote_copy(
            src_ref=self.slot(*block) if src is None else src, dst_ref=self.slot(*block),
            send_sem=self.send_sems.at[k], recv_sem=self.recv_sems.at[k], device_id=to, device_id_type=MESH)

    def first(self):
        return [self.copy(0, self.me, self.sibling, src=self.x_ref)] + [
            self.copy(1 + j, self.me, (*chip, self.c), src=self.x_ref) for j, chip in enumerate(self.chips)]

    def passed(self):
        return [self.copy(4 + j, (*chip, self.c), self.sibling) for j, chip in enumerate(self.chips)]


def _gather_start(x_ref, out_ref, stage, send_sems, recv_sems, local_sem):
    plan = _GatherPlan(x_ref, out_ref, send_sems, recv_sems)
    for cp in plan.first():
        cp.start()
    _copy_through_vmem(x_ref, plan.slot(*plan.me), stage, local_sem)


def _gather_pass_on(out_ref, send_sems, recv_sems):
    plan = _GatherPlan(None, out_ref, send_sems, recv_sems)
    passed = plan.passed()
    for j, chip in enumerate(plan.chips):
        plan.copy(1 + j, (*chip, plan.c), plan.me).wait_recv()
        passed[j].start()


def _gather_finish(x_ref, out_ref, send_sems, recv_sems):
    plan = _GatherPlan(x_ref, out_ref, send_sems, recv_sems)
    plan.copy(0, plan.sibling, plan.me).wait_recv()
    for j, chip in enumerate(plan.chips):
        plan.copy(4 + j, (*chip, 1 - plan.c), plan.me).wait_recv()
    for cp in plan.first() + plan.passed():
        cp.wait_send()


def _all_gather(xs, name):
    r, cdim = xs.shape

    def body(x_ref, out_ref, stage, send_sems, recv_sems, local_sem):
        _gather_start(x_ref, out_ref, stage, send_sems, recv_sems, local_sem)
        _gather_pass_on(out_ref, send_sems, recv_sems)
        _gather_finish(x_ref, out_ref, send_sems, recv_sems)

    return pl.pallas_call(
        body, name=name, out_shape=jax.ShapeDtypeStruct((N_DEV, r, cdim), xs.dtype),
        in_specs=[ANY], out_specs=ANY,
        scratch_shapes=[pltpu.VMEM((r, cdim), xs.dtype), pltpu.SemaphoreType.DMA((7,)), pltpu.SemaphoreType.DMA((7,)),
                        pltpu.SemaphoreType.DMA],
        compiler_params=pltpu.CompilerParams(vmem_limit_bytes=VMEM_LIMIT),
    )(xs)


def _rs_pair_sum(core, pieces, offsets, rows, name):
    cdim = pieces[0].shape[2]
    nk = N_DEV // 2
    npc = len(pieces)
    spans = [(o, t.shape[1]) for t, o in zip(pieces, offsets)]
    ends = [o + n for o, n in spans]
    gaps = [(a, b - a) for a, b in zip(ends, [o for o, _ in spans[1:]] + [rows]) if b > a]

    def body(core_ref, *refs):
        own, src, o_ref = refs[:npc], refs[npc:2 * npc], refs[2 * npc]
        landing, send_sems, recv_sems = refs[2 * npc + 1:]
        k = pl.program_id(0)
        x, y, c = lax.axis_index("x"), lax.axis_index("y"), lax.axis_index("c")

        def copies(kk):
            return [pltpu.make_async_remote_copy(
                src_ref=src[p].at[2 * kk + (1 - c)], dst_ref=landing.at[kk, pl.ds(o, n)],
                send_sem=send_sems.at[kk, p], recv_sem=recv_sems.at[kk, p], device_id=(x, y, 1 - c),
                device_id_type=MESH) for p, (o, n) in enumerate(spans)]

        @pl.when(k == 0)
        def _():
            for kk in range(nk):
                for cp in copies(kk):
                    cp.start()

        for cp, piece, (o, n) in zip(copies(k), own, spans):
            cp.wait_recv()
            o_ref[0, o:o + n, :] = (piece[0].astype(F32) + landing[k, o:o + n, :].astype(F32)).astype(BF16)
        for o, n in gaps:
            o_ref[0, o:o + n, :] = jnp.zeros((n, cdim), BF16)

        @pl.when(k == nk - 1)
        def _():
            for kk in range(nk):
                for cp in copies(kk):
                    cp.wait_send()

    own_specs = [pl.BlockSpec((1, n, cdim), lambda k, core_ref: (2 * k + core_ref[0], 0, 0)) for _, n in spans]
    return pl.pallas_call(
        body, name=name, out_shape=jax.ShapeDtypeStruct((nk, rows, cdim), BF16),
        grid_spec=pltpu.PrefetchScalarGridSpec(
            num_scalar_prefetch=1, grid=(nk,),
            in_specs=own_specs + [ANY] * npc,
            out_specs=pl.BlockSpec((1, rows, cdim), lambda k, core_ref: (k, 0, 0)),
            scratch_shapes=[pltpu.VMEM((nk, rows, cdim), BF16), pltpu.SemaphoreType.DMA((nk, npc)),
                            pltpu.SemaphoreType.DMA((nk, npc))]),
        compiler_params=_params(1),
    )(core, *pieces, *pieces)


def _chips_start(b_ref, out_ref, stage, send_sems, recv_sems, local_sem):
    x, y, c = lax.axis_index("x"), lax.axis_index("y"), lax.axis_index("c")
    mychip = 2 * x + y
    for j, (px, py) in enumerate([(1 - x, y), (x, 1 - y), (1 - x, 1 - y)]):
        pltpu.make_async_remote_copy(
            src_ref=b_ref.at[2 * px + py], dst_ref=out_ref.at[mychip],
            send_sem=send_sems.at[j], recv_sem=recv_sems.at[j], device_id=(px, py, c), device_id_type=MESH).start()
    _copy_through_vmem(b_ref.at[mychip], out_ref.at[mychip], stage, local_sem)


def _chips_finish(b_ref, out_ref, send_sems, recv_sems):
    x, y, c = lax.axis_index("x"), lax.axis_index("y"), lax.axis_index("c")
    for j, (px, py) in enumerate([(1 - x, y), (x, 1 - y), (1 - x, 1 - y)]):
        pltpu.make_async_remote_copy(
            src_ref=b_ref.at[2 * px + py], dst_ref=out_ref.at[2 * px + py],
            send_sem=send_sems.at[j], recv_sem=recv_sems.at[j], device_id=(px, py, c), device_id_type=MESH).wait()


def _pad_rows(a, rows):
    return jnp.pad(a, ((0, rows - a.shape[0]), (0, 0)))


def _pack_in(w_in):
    return _pad_rows(w_in[0].T, ROWS_IN)


def _unpack_in(r):
    return r[0:SHARD_IN].T[None]


def _pack_rest(w_out, w_gate, w_up, w_down, w_ple, w_pg):
    head = _pad_rows(jnp.concatenate([w_out[0], w_pg[0], w_ple[0].T.reshape(32, D_MODEL)], axis=0), OFF_GATE)
    return jnp.concatenate([head, w_gate[0].T, w_up[0].T, w_down[0]], axis=0)


def _unpack_rest(r):
    return (r[0:OFF_PG][None], r[OFF_GATE:OFF_UP].T[None], r[OFF_UP:OFF_DOWN].T[None], r[OFF_DOWN:ROWS_REST][None],
            r[OFF_PLE:OFF_PLE + 32].reshape(128, D_PLE).T[None], r[OFF_PG:OFF_PLE][None])


def _pack_small(w_pool, g_mix_pre, g_mix_post, g_ffn_pre, g_ffn_post, g_ple, g_attn, g_pool, pool_scale, b_forget,
                loss=None):
    row = lambda vrow: vrow.reshape(1, -1)
    misc = [row(pool_scale), row(b_forget), row(loss) if loss is not None else jnp.zeros((1, 1), F32),
            jnp.zeros((1, D_MODEL - COL_LOSS - 1), F32)]
    rows = [w_pool.reshape(64, D_MODEL), row(g_mix_pre), row(g_mix_post), row(g_ffn_pre), row(g_ffn_post), row(g_ple),
            jnp.concatenate([row(g_attn), row(g_pool)], axis=1), jnp.concatenate(misc, axis=1),
            jnp.zeros((SMALL_ROWS - ROW_MISC - 1, D_MODEL), F32)]
    return jnp.concatenate(rows, axis=0)


def _unpack_small(r):
    gains, misc = r[ROW_GROUP_GAINS:ROW_GROUP_GAINS + 1], r[ROW_MISC:ROW_MISC + 1]
    return dict(
        w_pool=r[0:64].reshape(1, 4, POOL_CH, POOL_CH), g_mix_pre=r[ROW_G_MIX_PRE:ROW_G_MIX_PRE + 1],
        g_mix_post=r[ROW_G_MIX_POST:ROW_G_MIX_POST + 1], g_ffn_pre=r[ROW_G_FFN_PRE:ROW_G_FFN_PRE + 1],
        g_ffn_post=r[ROW_G_FFN_POST:ROW_G_FFN_POST + 1], g_ple=r[ROW_G_PLE:ROW_G_PLE + 1],
        g_attn_grp=gains[:, 0:D_ATTN], g_pool_grp=gains[:, D_ATTN:D_ATTN + D_POOL],
        pool_scale=misc[:, 0:D_POOL], b_forget=misc[:, COL_B_FORGET:COL_B_FORGET + HEADS])


def _step(x, p, tgt, small, in_w, in_m, in_v, rest_w, rest_m, rest_v):
    core = lax.axis_index("c").astype(jnp.int32).reshape(1)
    win_t = _all_gather(in_w.astype(BF16), "gather_w_in")[:, 0:SHARD_IN].reshape(D_IN, D_MODEL)
    wqkv = win_t
    wf = _pad_rows(win_t[3 * D_ATTN:3 * D_ATTN + HEADS], LANES)
    wu = win_t[3 * D_ATTN + HEADS:]
    wpool = small["w_pool"].astype(BF16)
    bpad = jnp.pad(small["b_forget"], ((0, 0), (0, LANES - HEADS)))

    lay = _attn_layout_constants()
    rest_b = rest_w.astype(BF16)
    hn, qt3, ka, v, qat3, vt3, kt3, fl, y, mpre, gh = _pre_attn_fwd(x, small["g_mix_pre"], wqkv, wf, wu, bpad, wpool, lay,
                                                                 rest_b[0:OFF_GATE])
    a, lset3, gf = _attn_fwd(ka, qat3, vt3, rest_b[OFF_GATE:])
    wple_t = gh[:, OFF_PLE:OFF_PLE + 32].reshape(D_MODEL, D_PLE)
    mix, o, h1, hn2 = _post_attn_fwd(a, mpre, x, small["g_attn_grp"], small["g_pool_grp"], small["pool_scale"], gh,
                                     small["g_mix_post"], small["g_ffn_pre"])
    gate, up, act, ff, h2 = _ffn_fwd(hn2, gf, gf, gf, h1, small["g_ffn_post"])
    dh2, dff, dgl, dpp, h2b, pb, loss8, dg_ple, dg_ffn_post = _tail_fwd_bwd(
        h2, p, tgt, ff, wple_t, gh, small["g_ple"], small["g_ffn_post"])
    dgate, dup, dh1, dg_ffn_pre = _ffn_bwd(dff, gate, up, gf, gf, gf, h1, dh2, small["g_ffn_pre"])
    dob, dat3, dlt3, dmpb, dy, dg_mix_post, dg_attn, dg_pool, dps = _post_attn_bwd(
        dh1, o, a, mpre, gh, wpool, small["g_mix_post"], small["g_attn_grp"], small["g_pool_grp"], small["pool_scale"])

    nd = N_DEV
    send_rest = [
        _wgrad(mix, dob, BF16, "wgrad_out").reshape(nd, 128, D_MODEL),
        _wgrad(h2b, dgl, BF16, "wgrad_ple_gate").reshape(nd, 128, D_MODEL),
        _wgrad(dpp, pb, BF16, "wgrad_ple").reshape(nd, 32, D_MODEL),
        _wgrad(dgate, hn2, BF16, "wgrad_gate").reshape(nd, SHARD_FF, D_MODEL),
        _wgrad(dup, hn2, BF16, "wgrad_up").reshape(nd, SHARD_FF, D_MODEL),
        _wgrad(act, dff, BF16, "wgrad_down").reshape(nd, SHARD_FF, D_MODEL)]
    pair_rest = _rs_pair_sum(core, send_rest, [0, OFF_PG, OFF_PLE, OFF_GATE, OFF_UP, OFF_DOWN], ROWS_REST,
                             "rs_pair_sum_rest")

    dqt3, dkt3, dvt3, chips_rest = _attn_bwd(ka, v, kt3, qat3, qt3, dat3, lset3, dlt3, pair_rest)

    gx, dz, dg_mix_pre, db = _pre_attn_bwd(dqt3, dkt3, dvt3, fl, dy, x, dh1, small["g_mix_pre"], wqkv, wf, wu)

    pair_in = _rs_pair_sum(core, [_wgrad_in(dz, hn)], [0], ROWS_IN, "rs_pair_sum_in")

    dwp = _wgrad(y, dmpb, F32, "wgrad_pool")
    dw_pool = jnp.stack([dwp[g * POOL_CH:(g + 1) * POOL_CH, g * POOL_CH:(g + 1) * POOL_CH] for g in range(4)])
    small_part = _pack_small(dw_pool, dg_mix_pre, dg_mix_post, dg_ffn_pre, dg_ffn_post, dg_ple, dg_attn, dg_pool, dps,
                             db[:, 0:HEADS], loss8[0:1, 0:1])

    *upd_rest, chips_in, small_all = _reduce_update_rest(chips_rest, rest_w, rest_m, rest_v, pair_in, small_part)
    upd_in = _reduce_update_big(chips_in, in_w, in_m, in_v, ROWS_IN, "reduce_update_in")
    return gx, small_all, upd_in, upd_rest


def kernel(x, p, g_mix_pre, w_in, b_forget, g_attn_grp, g_pool_grp, w_pool, pool_scale, w_out, g_mix_post, g_ffn_pre, w_ffn_gate, w_ffn_up, w_ffn_down, g_ffn_post, w_ple_proj, g_ple, w_ple_gate, loss_target, m_g_mix_pre, m_w_in, m_b_forget, m_g_attn_grp, m_g_pool_grp, m_w_pool, m_pool_scale, m_w_out, m_g_mix_post, m_g_ffn_pre, m_w_ffn_gate, m_w_ffn_up, m_w_ffn_down, m_g_ffn_post, m_w_ple_proj, m_g_ple, m_w_ple_gate, v_g_mix_pre, v_w_in, v_b_forget, v_g_attn_grp, v_g_pool_grp, v_w_pool, v_pool_scale, v_w_out, v_g_mix_post, v_g_ffn_pre, v_w_ffn_gate, v_w_ffn_up, v_w_ffn_down, v_g_ffn_post, v_w_ple_proj, v_g_ple, v_w_ple_gate):
    small = dict(w_pool=w_pool[0], g_mix_pre=g_mix_pre, g_mix_post=g_mix_post, g_ffn_pre=g_ffn_pre,
                 g_ffn_post=g_ffn_post, g_ple=g_ple, g_attn_grp=g_attn_grp, g_pool_grp=g_pool_grp,
                 pool_scale=pool_scale, b_forget=b_forget)
    gx, small_all, upd_in, upd_rest = _step(
        x[0], p[0, 0], loss_target[0], small, _pack_in(w_in), _pack_in(m_w_in), _pack_in(v_w_in),
        _pack_rest(w_out, w_ffn_gate, w_ffn_up, w_ffn_down, w_ple_proj, w_ple_gate),
        _pack_rest(m_w_out, m_w_ffn_gate, m_w_ffn_up, m_w_ffn_down, m_w_ple_proj, m_w_ple_gate),
        _pack_rest(v_w_out, v_w_ffn_gate, v_w_ffn_up, v_w_ffn_down, v_w_ple_proj, v_w_ple_gate))

    sm_w = _pack_small(w_pool, g_mix_pre, g_mix_post, g_ffn_pre, g_ffn_post, g_ple, g_attn_grp, g_pool_grp, pool_scale, b_forget)
    sm_m = _pack_small(m_w_pool, m_g_mix_pre, m_g_mix_post, m_g_ffn_pre, m_g_ffn_post, m_g_ple, m_g_attn_grp, m_g_pool_grp, m_pool_scale, m_b_forget)
    sm_v = _pack_small(v_w_pool, v_g_mix_pre, v_g_mix_post, v_g_ffn_pre, v_g_ffn_post, v_g_ple, v_g_attn_grp, v_g_pool_grp, v_pool_scale, v_b_forget)
    upd_small = _reduce_update_small(small_all, sm_w, sm_m, sm_v)
    loss = upd_small[0][ROW_MISC, COL_LOSS]

    def leaves(k):
        b_out, b_gate, b_up, b_down, b_ple, b_pg = _unpack_rest(upd_rest[k])
        s = _unpack_small(upd_small[k])
        return (s["g_mix_pre"], _unpack_in(upd_in[k]), s["b_forget"], s["g_attn_grp"], s["g_pool_grp"], s["w_pool"],
                s["pool_scale"], b_out, s["g_mix_post"], s["g_ffn_pre"], b_gate, b_up, b_down, s["g_ffn_post"], b_ple,
                s["g_ple"], b_pg)

    return (loss, gx[None], *leaves(0), *leaves(1), *leaves(2), *leaves(3))
```

```python
import functools

import jax
import jax.numpy as jnp
from jax import lax
from jax.experimental import pallas as pl
from jax.experimental.pallas import tpu as pltpu

F32 = jnp.float32
BF16 = jnp.bfloat16
HIGHEST = lax.Precision.HIGHEST

D_MODEL = 1024
HEADS = 8
HEAD_DIM = 64
D_ATTN = HEADS * HEAD_DIM
POOL_WINDOWS = (2, 4, 8, 16)
POOL_CH = 128
D_POOL = POOL_CH * len(POOL_WINDOWS)
D_FF = 2816
D_PLE = 256
D_IN = 3 * D_ATTN + HEADS + D_POOL
RMS_EPS = 1e-6
N_DEV = 8

ADAM_LR = 0.001
ADAM_B1 = 0.9
ADAM_B2 = 0.999
ADAM_EPS = 1e-08
ADAM_WD = 0.01
ADAM_STEP = 10

LANES = 128
HALO = 16
TS = 512
TS_FF = 512
TS_WGRAD = 1024
TM_WGRAD = 2176
TQ = 256
TN_FF = 1408
NEG = -1e30
VMEM_LIMIT = 56 * 1024 * 1024

SHARD_IN = 257
ROWS_IN = 272
SHARD_FF = 352
OFF_PG = 128
OFF_PLE = 256
OFF_GATE = SHARD_FF
OFF_UP = 2 * SHARD_FF
OFF_DOWN = 3 * SHARD_FF
ROWS_REST = 4 * SHARD_FF
TR_REST = SHARD_FF

SMALL_ROWS = 72
ROW_G_MIX_PRE, ROW_G_MIX_POST, ROW_G_FFN_PRE, ROW_G_FFN_POST, ROW_G_PLE = 64, 65, 66, 67, 68
ROW_GROUP_GAINS, ROW_MISC = 69, 70
COL_B_FORGET = D_POOL
COL_LOSS = D_POOL + HEADS


def _nn(a, b):
    return jnp.dot(a, b, preferred_element_type=F32)


def _nt(a, b):
    return lax.dot_general(a, b, (((1,), (1,)), ((), ())), preferred_element_type=F32)


def _tn(a, b):
    return lax.dot_general(a, b, (((0,), (0,)), ((), ())), preferred_element_type=F32)


def _rstd(v):
    return lax.rsqrt(jnp.mean(v * v, axis=-1, keepdims=True) + RMS_EPS)


def _rms_bwd(v, g, dy):
    r = _rstd(v)
    vh = v * r
    t = dy * g
    dv = r * (t - vh * jnp.mean(t * vh, axis=-1, keepdims=True))
    return dv, jnp.sum(dy * vh, axis=0, keepdims=True)


def _split3(v):
    hi = v.astype(BF16)
    rest = v - hi.astype(F32)
    mid = rest.astype(BF16)
    return hi, mid, (rest - mid.astype(F32)).astype(BF16)


def _mask_matmul(mask, v):
    hi, mid, lo = _split3(v)
    return _nn(mask, lo) + _nn(mask, mid) + _nn(mask, hi)


def _params(n_grid):
    return pltpu.CompilerParams(dimension_semantics=("arbitrary",) * n_grid, vmem_limit_bytes=VMEM_LIMIT)


def _row(i):
    return (i, 0)


def _fixed(*_):
    return (0, 0)


def _spec_square(part):
    return pl.BlockSpec((N_DEV, 128, D_MODEL), lambda *_: (0, part, 0))


def _spec_ff(part):
    return pl.BlockSpec((TN_FF // SHARD_FF, SHARD_FF, D_MODEL), lambda i, j: (j, part, 0))


assert TS == 2 * TQ and TN_FF % SHARD_FF == 0
_HALVES = (slice(0, TQ), slice(TQ, TS))

VMEM_WHOLE = pl.BlockSpec(memory_space=pltpu.VMEM)
SMEM_WHOLE = pl.BlockSpec(memory_space=pltpu.SMEM)
ANY = pl.BlockSpec(memory_space=pl.ANY)


LOG2E = 1.4426950408889634
VROWS = HEAD_DIM + 16
AUG = 128
BIAS_LANE = HEAD_DIM
ONE_LANE = HEAD_DIM + 3
SPARE_LANE = HEADS


def _attn_layout_constants():
    import numpy as np
    place = np.zeros((D_ATTN, HEADS * AUG), np.float32)
    for r in range(D_ATTN):
        place[r, (r // HEAD_DIM) * AUG + r % HEAD_DIM] = 1.0
    bias_k = np.zeros((3, LANES, HEADS * AUG), np.float32)
    bias_q = np.zeros((3, LANES, HEADS * AUG), np.float32)
    for h in range(HEADS):
        for part in range(3):
            bias_k[part, h, h * AUG + BIAS_LANE + part] = -1.0
            bias_q[part, h, h * AUG + ONE_LANE + part] = 1.0
            bias_k[0, SPARE_LANE, h * AUG + ONE_LANE + part] = 1.0
            bias_q[0, SPARE_LANE, h * AUG + BIAS_LANE + part] = 1.0
    as_bf = lambda a: jnp.asarray(a, BF16)
    return dict(place=as_bf(place), place_t=as_bf(place.T), bias_k=as_bf(bias_k),
                bias_q_t=as_bf(bias_q.transpose(0, 2, 1)))


def _pre_attn_fwd(x, g1, wqkv, wf, wu, bpad, wpool, lay, own_block):
    s, d = x.shape
    nt = s // TS
    sub = TS // TQ

    def body(x_ref, g_ref, wqkv_ref, wf_ref, wu_ref, b_ref, wp_ref, place_ref, place_t_ref, bk_ref, bqt_ref, own_ref,
             hn_ref, qt_ref, ka_ref, v_ref, qat_ref, vt_ref, kt_ref, fl_ref, y_ref, mp_ref, all_ref,
             ubuf, ccar, cbuf, stage, send_sems, recv_sems, local_sem):
        i = pl.program_id(0)

        @pl.when(i == 0)
        def _():
            _gather_start(own_ref, all_ref, stage, send_sems, recv_sems, local_sem)
            ubuf[0:HALO, :] = jnp.zeros((HALO, D_POOL), F32)
            ccar[...] = jnp.zeros_like(ccar)

        @pl.when(i == max(nt - 2, 0))
        def _():
            _gather_pass_on(all_ref, send_sems, recv_sems)

        xv = x_ref[...]
        hn = (xv * _rstd(xv) * g_ref[...]).astype(BF16)
        hn_ref[...] = hn
        zq = _nt(hn, wqkv_ref[...])
        qt = (zq[:, 0:D_ATTN] * 0.125).astype(BF16).T
        qb = (zq[:, 0:D_ATTN] * (0.125 * LOG2E)).astype(BF16)
        kb = zq[:, D_ATTN:2 * D_ATTN].astype(BF16)
        vb = zq[:, 2 * D_ATTN:3 * D_ATTN].astype(BF16)
        v_ref[...] = vb

        fl = _nt(hn, wf_ref[...]) + b_ref[...]
        fl_ref[...] = fl
        logf = jax.nn.log_sigmoid(fl)
        rr = lax.broadcasted_iota(jnp.int32, (TS, TS), 0)
        cc = lax.broadcasted_iota(jnp.int32, (TS, TS), 1)
        c = _mask_matmul((cc <= rr).astype(BF16), logf) + ccar[...]
        cbuf[...] = c
        ccar[...] = cbuf[TS - 1:TS, :]
        hi, mid, lo = _split3(c * LOG2E)
        lane = lax.broadcasted_iota(jnp.int32, (TS, LANES), 1)
        parts = (jnp.where(lane == SPARE_LANE, 1.0, hi).astype(BF16), mid, lo)
        ka = _nn(kb, place_ref[...])
        qat = _nt(place_t_ref[...], qb)
        for part in range(3):
            ka = ka + _nn(parts[part], bk_ref[part])
            qat = qat + _nt(bqt_ref[part], parts[part])
        ka_ref[...] = ka.astype(BF16)
        qat = qat.astype(BF16)
        vt = vb.T
        kt = kb.T
        for a in range(sub):
            cols = slice(a * TQ, (a + 1) * TQ)
            qat_ref[a] = qat[:, cols]
            for ref, mat in ((qt_ref, qt), (kt_ref, kt), (vt_ref, vt)):
                for h in range(HEADS):
                    ref[a, h * VROWS:h * VROWS + HEAD_DIM, :] = mat[h * HEAD_DIM:(h + 1) * HEAD_DIM, cols]
                    ref[a, h * VROWS + HEAD_DIM:(h + 1) * VROWS, :] = jnp.ones((VROWS - HEAD_DIM, TQ), BF16)

        u = _nt(hn, wu_ref[...])
        ubuf[HALO:HALO + TS, :] = u
        t = i * TS + lax.broadcasted_iota(jnp.int32, (TS, 1), 0)
        for g, w in enumerate(POOL_WINDOWS):
            cols = slice(g * POOL_CH, (g + 1) * POOL_CH)
            sm = ubuf[:, cols]
            step = 1
            while step < w:
                sm = sm + pltpu.roll(sm, step, 0)
                step *= 2
            cnt = jnp.minimum(t + 1, w).astype(F32)
            yg = (sm[HALO:, :] / cnt - u[:, cols]).astype(BF16)
            y_ref[:, cols] = yg
            mp_ref[:, cols] = _nn(yg, wp_ref[g])
        ubuf[0:HALO, :] = u[TS - HALO:, :]

        @pl.when(i == nt - 1)
        def _():
            _gather_finish(own_ref, all_ref, send_sems, recv_sems)

    nq = s // TQ
    aug = HEADS * AUG
    outs = (
        jax.ShapeDtypeStruct((s, d), BF16), jax.ShapeDtypeStruct((nq, HEADS * VROWS, TQ), BF16),
        jax.ShapeDtypeStruct((s, aug), BF16), jax.ShapeDtypeStruct((s, D_ATTN), BF16),
        jax.ShapeDtypeStruct((nq, aug, TQ), BF16), jax.ShapeDtypeStruct((nq, HEADS * VROWS, TQ), BF16),
        jax.ShapeDtypeStruct((nq, HEADS * VROWS, TQ), BF16),
        jax.ShapeDtypeStruct((s, LANES), F32),
        jax.ShapeDtypeStruct((s, D_POOL), BF16), jax.ShapeDtypeStruct((s, D_POOL), F32),
        jax.ShapeDtypeStruct((N_DEV,) + own_block.shape, own_block.dtype),
    )
    fixed3 = lambda i: (0, 0, 0)
    tiles3 = lambda rows: pl.BlockSpec((sub, rows, TQ), lambda i: (i, 0, 0))
    return pl.pallas_call(
        body, grid=(nt,), out_shape=outs, name="pre_attn_fwd",
        in_specs=[pl.BlockSpec((TS, d), _row), pl.BlockSpec((1, d), _fixed),
                  pl.BlockSpec((3 * D_ATTN, d), _fixed), pl.BlockSpec(wf.shape, _fixed), pl.BlockSpec(wu.shape, _fixed),
                  pl.BlockSpec((1, LANES), _fixed), pl.BlockSpec(wpool.shape, fixed3),
                  pl.BlockSpec(lay["place"].shape, _fixed), pl.BlockSpec(lay["place_t"].shape, _fixed),
                  pl.BlockSpec(lay["bias_k"].shape, fixed3), pl.BlockSpec(lay["bias_q_t"].shape, fixed3), ANY],
        out_specs=(pl.BlockSpec((TS, d), _row), tiles3(HEADS * VROWS),
                   pl.BlockSpec((TS, aug), _row), pl.BlockSpec((TS, D_ATTN), _row),
                   tiles3(aug), tiles3(HEADS * VROWS), tiles3(HEADS * VROWS),
                   pl.BlockSpec((TS, LANES), _row),
                   pl.BlockSpec((TS, D_POOL), _row), pl.BlockSpec((TS, D_POOL), _row), ANY),
        scratch_shapes=[pltpu.VMEM((TS + HALO, D_POOL), F32), pltpu.VMEM((1, LANES), F32), pltpu.VMEM((TS, LANES), F32),
                        pltpu.VMEM(own_block.shape, own_block.dtype),
                        pltpu.SemaphoreType.DMA((7,)), pltpu.SemaphoreType.DMA((7,)), pltpu.SemaphoreType.DMA],
        compiler_params=_params(1),
    )(x, g1, wqkv, wf, wu, bpad, wpool, lay["place"], lay["place_t"], lay["bias_k"], lay["bias_q_t"], own_block)


def _causal_in_tile():
    krow = lax.broadcasted_iota(jnp.int32, (TQ, TQ), 0)
    qcol = lax.broadcasted_iota(jnp.int32, (TQ, TQ), 1)
    return krow <= qcol


def _attn_fwd(ka, qat3, vt3, own_block):
    s = ka.shape[0]
    nq = s // TQ
    pass_on_step = max(nq - 2, 0)

    def body(qa_ref, ka_ref, vt_ref, own_ref, a_ref, lset_ref, all_ref, acc, out_t, st_scr, pt_scr,
             stage, send_sems, recv_sems, local_sem):
        i = pl.program_id(0)

        @pl.when(i == 0)
        def _():
            _gather_start(own_ref, all_ref, stage, send_sems, recv_sems, local_sem)

        @pl.when(i == pass_on_step)
        def _():
            _gather_pass_on(all_ref, send_sems, recv_sems)

        acc[...] = jnp.zeros_like(acc)

        def tile(j, stats, masked):
            tile_max = []
            for h in range(HEADS):
                aug = slice(h * AUG, (h + 1) * AUG)
                st = _nn(ka_ref[pl.ds(j * TQ, TQ), aug], qa_ref[0, aug, :])
                if masked:
                    st = jnp.where(_causal_in_tile(), st, NEG)
                st_scr[h] = st
                tile_max.append(jnp.max(st, axis=0, keepdims=True))
            new, scale = [], []
            for h in range(HEADS):
                m_new = jnp.maximum(stats[h], tile_max[h])
                scale.append(jnp.exp2(stats[h] - m_new))
                pt_scr[h] = jnp.exp2(st_scr[h] - m_new).astype(BF16)
                new.append(m_new)
            for h in range(HEADS):
                rows = slice(h * VROWS, (h + 1) * VROWS)
                acc[rows, :] = scale[h] * acc[rows, :] + _nn(vt_ref[j, rows, :], pt_scr[h])
            return tuple(new)

        init = tuple(jnp.full((1, TQ), NEG, F32) for _ in range(HEADS))
        stats = lax.fori_loop(0, i, functools.partial(tile, masked=False), init)
        stats = tile(i, stats, True)
        for h in range(HEADS):
            denom = acc[h * VROWS + HEAD_DIM:h * VROWS + HEAD_DIM + 1, :]
            out_t[h * HEAD_DIM:(h + 1) * HEAD_DIM, :] = acc[h * VROWS:h * VROWS + HEAD_DIM, :] / denom
            lset_ref[0, h:h + 1, :] = stats[h] + jnp.log2(denom)
        a_ref[...] = out_t[...].T

        @pl.when(i == nq - 1)
        def _():
            _gather_finish(own_ref, all_ref, send_sems, recv_sems)

    r, cdim = own_block.shape
    return pl.pallas_call(
        body, grid=(nq,), name="attn_fwd",
        out_shape=(jax.ShapeDtypeStruct((s, D_ATTN), F32), jax.ShapeDtypeStruct((nq, HEADS, TQ), F32),
                   jax.ShapeDtypeStruct((N_DEV, r, cdim), own_block.dtype)),
        in_specs=[pl.BlockSpec((1, HEADS * AUG, TQ), lambda i: (i, 0, 0)), VMEM_WHOLE, VMEM_WHOLE, ANY],
        out_specs=(pl.BlockSpec((TQ, D_ATTN), _row), pl.BlockSpec((1, HEADS, TQ), lambda i: (i, 0, 0)), ANY),
        scratch_shapes=[pltpu.VMEM((HEADS * VROWS, TQ), F32), pltpu.VMEM((D_ATTN, TQ), F32),
                        pltpu.VMEM((HEADS, TQ, TQ), F32), pltpu.VMEM((HEADS, TQ, TQ), BF16),
                        pltpu.VMEM((r, cdim), own_block.dtype),
                        pltpu.SemaphoreType.DMA((7,)), pltpu.SemaphoreType.DMA((7,)), pltpu.SemaphoreType.DMA],
        compiler_params=_params(1),
    )(qat3, ka, vt3, own_block)


def _post_attn_fwd(a, mpre, x, g_attn, g_pool, pscale, wout, g_post, g_ffn_pre):
    s, d = x.shape

    def body(a_ref, mp_ref, x_ref, ga_ref, gp_ref, ps_ref, wo_ref, gpost_ref, gpre_ref,
             mix_ref, o_ref, h1_ref, hn2_ref):
        for rows in _HALVES:
            av = a_ref[rows, :]
            mix_ref[rows, 0:D_ATTN] = (av * _rstd(av) * ga_ref[...]).astype(BF16)
            mv = mp_ref[rows, :] * ps_ref[...]
            mix_ref[rows, D_ATTN:] = (mv * _rstd(mv) * gp_ref[...]).astype(BF16)
            o = _nn(mix_ref[rows, :], wo_ref[...].reshape(d, d))
            o_ref[rows, :] = o
            h1 = x_ref[rows, :] + o * _rstd(o) * gpost_ref[...]
            h1_ref[rows, :] = h1
            hn2_ref[rows, :] = (h1 * _rstd(h1) * gpre_ref[...]).astype(BF16)

    vec = lambda n: pl.BlockSpec((1, n), _fixed)
    return pl.pallas_call(
        body, grid=(s // TS,), name="post_attn_fwd",
        out_shape=(jax.ShapeDtypeStruct((s, d), BF16), jax.ShapeDtypeStruct((s, d), F32),
                   jax.ShapeDtypeStruct((s, d), F32), jax.ShapeDtypeStruct((s, d), BF16)),
        in_specs=[pl.BlockSpec((TS, D_ATTN), _row), pl.BlockSpec((TS, D_POOL), _row), pl.BlockSpec((TS, d), _row),
                  vec(D_ATTN), vec(D_POOL), vec(D_POOL), _spec_square(0), vec(d), vec(d)],
        out_specs=(pl.BlockSpec((TS, d), _row),) * 4,
        compiler_params=_params(1),
    )(a, mpre, x, g_attn, g_pool, pscale, wout, g_post, g_ffn_pre)


def _ffn_fwd(hn2, wg, wu, wd, h1, g_post):
    s, d = h1.shape
    nc = D_FF // TN_FF
    ts = min(TS_FF, s)

    def body(hn_ref, wg_ref, wu_ref, wd_ref, h1_ref, g_ref, gate_ref, up_ref, act_ref, ff_ref, h2_ref, acc):
        j = pl.program_id(1)

        @pl.when(j == 0)
        def _():
            acc[...] = jnp.zeros_like(acc)

        for r in range(2):
            rows = slice(r * (ts // 2), (r + 1) * (ts // 2))
            hn = hn_ref[rows, :]
            gt = _nt(hn, wg_ref[...].reshape(TN_FF, d))
            up = _nt(hn, wu_ref[...].reshape(TN_FF, d))
            gate_ref[rows, :] = gt.astype(BF16)
            up_ref[rows, :] = up.astype(BF16)
            act_ref[rows, :] = (gt * jax.nn.sigmoid(gt) * up).astype(BF16)
            acc[rows, :] += _nn(act_ref[rows, :], wd_ref[...].reshape(TN_FF, d))

        @pl.when(j == nc - 1)
        def _():
            ff = acc[...]
            ff_ref[...] = ff
            h2_ref[...] = h1_ref[...] + ff * _rstd(ff) * g_ref[...]

    rowblk = pl.BlockSpec((ts, d), lambda i, j: (i, 0))
    chunk = pl.BlockSpec((ts, TN_FF), lambda i, j: (i, j))
    return pl.pallas_call(
        body, grid=(s // ts, nc), name="ffn_fwd",
        out_shape=(jax.ShapeDtypeStruct((s, D_FF), BF16),) * 3 + (jax.ShapeDtypeStruct((s, d), F32),) * 2,
        in_specs=[rowblk, _spec_ff(0), _spec_ff(1), _spec_ff(2), rowblk, pl.BlockSpec((1, d), lambda i, j: (0, 0))],
        out_specs=(chunk, chunk, chunk, rowblk, rowblk),
        scratch_shapes=[pltpu.VMEM((ts, d), F32)],
        compiler_params=_params(2),
    )(hn2, wg, wu, wd, h1, g_post)


def _tail_fwd_bwd(h2, p, tgt, ff, wple, wpg, g_ple, g_ffn_post):
    s, d = h2.shape

    def body(h2_ref, p_ref, t_ref, ff_ref, wple_ref, wpg_ref, gple_ref, gfp_ref,
             dh2_ref, dff_ref, dgl_ref, dpp_ref, h2b_ref, pb_ref, loss_ref, dgple_ref, dgfp_ref):
        i = pl.program_id(0)

        @pl.when(i == 0)
        def _():
            loss_ref[...] = jnp.zeros_like(loss_ref)
            dgple_ref[...] = jnp.zeros_like(dgple_ref)
            dgfp_ref[...] = jnp.zeros_like(dgfp_ref)

        h2 = h2_ref[...]
        h2b = h2.astype(BF16)
        h2b_ref[...] = h2b
        pb = p_ref[...].astype(BF16)
        pb_ref[...] = pb
        pp = _nt(pb, wple_ref[...])
        gple = gple_ref[...]
        e = pp * _rstd(pp) * gple
        wpg = wpg_ref[...].reshape(d, d)
        sg = jax.nn.sigmoid(_nn(h2b, wpg))
        diff = h2 + sg * e - t_ref[...]
        sq = jnp.sum(jnp.sum(diff * diff, axis=1, keepdims=True), axis=0, keepdims=True)
        loss_ref[...] += jnp.broadcast_to(sq * (0.5 / d), loss_ref.shape)
        dh3 = diff * (1.0 / d)
        dgl = (dh3 * e * sg * (1.0 - sg)).astype(BF16)
        dgl_ref[...] = dgl
        dh2 = dh3 + _nt(dgl, wpg)
        dh2_ref[...] = dh2
        dpp, dg = _rms_bwd(pp, gple, dh3 * sg)
        dpp_ref[...] = dpp.astype(BF16)
        dgple_ref[...] += dg
        dff, dg = _rms_bwd(ff_ref[...], gfp_ref[...], dh2)
        dff_ref[...] = dff.astype(BF16)
        dgfp_ref[...] += dg

    rowblk = pl.BlockSpec((TS, d), _row)
    vec = pl.BlockSpec((1, d), _fixed)
    return pl.pallas_call(
        body, grid=(s // TS,), name="tail_fwd_bwd",
        out_shape=(jax.ShapeDtypeStruct((s, d), F32), jax.ShapeDtypeStruct((s, d), BF16),
                   jax.ShapeDtypeStruct((s, d), BF16), jax.ShapeDtypeStruct((s, d), BF16),
                   jax.ShapeDtypeStruct((s, d), BF16), jax.ShapeDtypeStruct((s, D_PLE), BF16),
                   jax.ShapeDtypeStruct((8, LANES), F32), jax.ShapeDtypeStruct((1, d), F32),
                   jax.ShapeDtypeStruct((1, d), F32)),
        in_specs=[rowblk, pl.BlockSpec((TS, D_PLE), _row), rowblk, rowblk,
                  pl.BlockSpec(wple.shape, _fixed), _spec_square(1), vec, vec],
        out_specs=(rowblk, rowblk, rowblk, rowblk, rowblk, pl.BlockSpec((TS, D_PLE), _row),
                   pl.BlockSpec((8, LANES), _fixed), vec, vec),
        compiler_params=_params(1),
    )(h2, p, tgt, ff, wple, wpg, g_ple, g_ffn_post)


def _ffn_bwd(dff, gate, up, wd, wg, wu, h1, dh2, g_pre):
    s, d = h1.shape
    nc = D_FF // TN_FF
    ts = min(TS_FF, s)

    def body(dff_ref, gate_ref, up_ref, wd_ref, wg_ref, wu_ref, h1_ref, dh2_ref, g_ref,
             dgate_ref, dup_ref, dh1_ref, dg_ref, acc):
        i = pl.program_id(0)
        j = pl.program_id(1)

        @pl.when((i == 0) & (j == 0))
        def _():
            dg_ref[...] = jnp.zeros_like(dg_ref)

        @pl.when(j == 0)
        def _():
            acc[...] = jnp.zeros_like(acc)

        for r in range(2):
            rows = slice(r * (ts // 2), (r + 1) * (ts // 2))
            dact = _nt(dff_ref[rows, :], wd_ref[...].reshape(TN_FF, d))
            gt = gate_ref[rows, :].astype(F32)
            sg = jax.nn.sigmoid(gt)
            dup_ref[rows, :] = (dact * gt * sg).astype(BF16)
            dgate_ref[rows, :] = (dact * up_ref[rows, :].astype(F32) * (sg * (1.0 + gt * (1.0 - sg)))).astype(BF16)
            acc[rows, :] += (_nn(dgate_ref[rows, :], wg_ref[...].reshape(TN_FF, d))
                             + _nn(dup_ref[rows, :], wu_ref[...].reshape(TN_FF, d)))

        @pl.when(j == nc - 1)
        def _():
            dv, dg = _rms_bwd(h1_ref[...], g_ref[...], acc[...])
            dh1_ref[...] = dh2_ref[...] + dv
            dg_ref[...] += dg

    rowblk = pl.BlockSpec((ts, d), lambda i, j: (i, 0))
    chunk = pl.BlockSpec((ts, TN_FF), lambda i, j: (i, j))
    vec = pl.BlockSpec((1, d), lambda i, j: (0, 0))
    return pl.pallas_call(
        body, grid=(s // ts, nc), name="ffn_bwd",
        out_shape=(jax.ShapeDtypeStruct((s, D_FF), BF16), jax.ShapeDtypeStruct((s, D_FF), BF16),
                   jax.ShapeDtypeStruct((s, d), F32), jax.ShapeDtypeStruct((1, d), F32)),
        in_specs=[rowblk, chunk, chunk, _spec_ff(2), _spec_ff(0), _spec_ff(1), rowblk, rowblk, vec],
        out_specs=(chunk, chunk, rowblk, vec),
        scratch_shapes=[pltpu.VMEM((ts, d), F32)],
        compiler_params=_params(2),
    )(dff, gate, up, wd, wg, wu, h1, dh2, g_pre)


def _post_attn_bwd(dh1, o, a, mpre, wout, wpool, g_post, g_attn, g_pool, pscale):
    s, d = dh1.shape
    sub = TS // TQ

    def body(dh1_ref, o_ref, a_ref, mp_ref, wo_ref, wp_ref, gpost_ref, ga_ref, gp_ref, ps_ref,
             dob_ref, dat_ref, dlt_ref, dmpb_ref, dy_ref, dgpost_ref, dga_ref, dgp_ref, dps_ref):
        i = pl.program_id(0)

        @pl.when(i == 0)
        def _():
            dgpost_ref[...] = jnp.zeros_like(dgpost_ref)
            dga_ref[...] = jnp.zeros_like(dga_ref)
            dgp_ref[...] = jnp.zeros_like(dgp_ref)
            dps_ref[...] = jnp.zeros_like(dps_ref)

        do, dg = _rms_bwd(o_ref[...], gpost_ref[...], dh1_ref[...])
        dgpost_ref[...] += dg
        dob = do.astype(BF16)
        dob_ref[...] = dob
        dmix = _nt(dob, wo_ref[...].reshape(d, d))

        av = a_ref[...]
        da, dg = _rms_bwd(av, ga_ref[...], dmix[:, 0:D_ATTN])
        dga_ref[...] += dg
        dat = da.astype(BF16).T
        hsel = (lax.shift_right_logical(lax.broadcasted_iota(jnp.int32, (HEADS, D_ATTN), 1), 6)
                == lax.broadcasted_iota(jnp.int32, (HEADS, D_ATTN), 0)).astype(F32)
        dlt = lax.dot_general(hsel, da * av, (((1,), (1,)), ((), ())), precision=HIGHEST, preferred_element_type=F32)
        for q in range(sub):
            dlt_ref[q] = dlt[:, q * TQ:(q + 1) * TQ]
            dat_ref[q] = dat[:, q * TQ:(q + 1) * TQ]

        ps = ps_ref[...]
        mp = mp_ref[...]
        dm, dg = _rms_bwd(mp * ps, gp_ref[...], dmix[:, D_ATTN:])
        dgp_ref[...] += dg
        dps_ref[...] += jnp.sum(dm * mp, axis=0, keepdims=True)
        dmpb = (dm * ps).astype(BF16)
        dmpb_ref[...] = dmpb
        for g in range(len(POOL_WINDOWS)):
            cols = slice(g * POOL_CH, (g + 1) * POOL_CH)
            dy_ref[:, cols] = _nt(dmpb[:, cols], wp_ref[g])

    rowblk = pl.BlockSpec((TS, d), _row)
    half = pl.BlockSpec((TS, D_ATTN), _row)
    vec = lambda n: pl.BlockSpec((1, n), _fixed)
    return pl.pallas_call(
        body, grid=(s // TS,), name="post_attn_bwd",
        out_shape=(jax.ShapeDtypeStruct((s, d), BF16), jax.ShapeDtypeStruct((s // TQ, D_ATTN, TQ), BF16),
                   jax.ShapeDtypeStruct((s // TQ, HEADS, TQ), F32), jax.ShapeDtypeStruct((s, D_POOL), BF16),
                   jax.ShapeDtypeStruct((s, D_POOL), F32), jax.ShapeDtypeStruct((1, d), F32),
                   jax.ShapeDtypeStruct((1, D_ATTN), F32), jax.ShapeDtypeStruct((1, D_POOL), F32),
                   jax.ShapeDtypeStruct((1, D_POOL), F32)),
        in_specs=[rowblk, rowblk, half, half, _spec_square(0),
                  pl.BlockSpec(wpool.shape, lambda i: (0, 0, 0)), vec(d), vec(D_ATTN), vec(D_POOL), vec(D_POOL)],
        out_specs=(rowblk, pl.BlockSpec((sub, D_ATTN, TQ), lambda i: (i, 0, 0)),
                   pl.BlockSpec((sub, HEADS, TQ), lambda i: (i, 0, 0)), half, half,
                   vec(d), vec(D_ATTN), vec(D_POOL), vec(D_POOL)),
        compiler_params=_params(1),
    )(dh1, o, a, mpre, wout, wpool, g_post, g_attn, g_pool, pscale)


def _attn_bwd(ka, v, kt3, qat3, qt3, dot3, lset3, dlt3, chip_blocks, small_block):
    s = ka.shape[0]
    nq = s // TQ

    def body(ka_ref, v_ref, kt_ref, qat_ref, qt_ref, dot_ref, lset_ref, dlt_ref, b_ref, sm_ref,
             dqt_ref, dkt_ref, dvt_ref, got_ref, all_ref, pt_scr, ptb_scr, dsb_scr,
             stage, send_sems, recv_sems, local_sem, stage_s, send_s, recv_s, local_s):
        j = pl.program_id(0)

        @pl.when(j == 0)
        def _():
            _chips_start(b_ref, got_ref, stage, send_sems, recv_sems, local_sem)
            _gather_start(sm_ref, all_ref, stage_s, send_s, recv_s, local_s)
            dqt_ref[...] = jnp.zeros_like(dqt_ref)

        @pl.when(j == max(nq - 2, 0))
        def _():
            _gather_pass_on(all_ref, send_s, recv_s)

        def tile(i, masked):
            def accumulate(ref, idx, val):
                if masked:
                    ref[idx] = val
                else:
                    ref[idx] += val

            for h in range(HEADS):
                aug = slice(h * AUG, (h + 1) * AUG)
                st = _nn(ka_ref[:, aug], qat_ref[i, aug, :]) - lset_ref[i, h:h + 1, :]
                if masked:
                    st = jnp.where(_causal_in_tile(), st, NEG)
                pt = jnp.exp2(st)
                pt_scr[h] = pt
                ptb_scr[h] = pt.astype(BF16)
            heads = [(h, slice(h * HEAD_DIM, (h + 1) * HEAD_DIM)) for h in range(HEADS)]
            for h, hs in heads:
                dst = pt_scr[h] * (_nn(v_ref[:, hs], dot_ref[i, hs, :]) - dlt_ref[i, h:h + 1, :])
                dsb_scr[h] = dst.astype(BF16)
            for h, hs in heads:
                accumulate(dvt_ref, (0, hs, slice(None)), _nt(dot_ref[i, hs, :], ptb_scr[h]))
            for h, hs in heads:
                rows = slice(h * VROWS, (h + 1) * VROWS)
                accumulate(dkt_ref, (0, rows, slice(None)), _nt(qt_ref[i, rows, :], dsb_scr[h]))
            for h, hs in heads:
                rows = slice(h * VROWS, (h + 1) * VROWS)
                dqt_ref[i, rows, :] += _nn(kt_ref[0, rows, :], dsb_scr[h])

        first = j + 1
        pairs = (nq - first) // 2

        def step(p, carry):
            tile(first + 2 * p, False)
            tile(first + 2 * p + 1, False)
            return carry

        tile(j, True)
        lax.fori_loop(0, pairs, step, 0)

        @pl.when(first + 2 * pairs < nq)
        def _():
            tile(nq - 1, False)

        @pl.when(j == nq - 1)
        def _():
            _chips_finish(b_ref, got_ref, send_sems, recv_sems)
            _gather_finish(sm_ref, all_ref, send_s, recv_s)

    blk = pl.BlockSpec((TQ, D_ATTN), _row)
    tile_t = lambda rows: pl.BlockSpec((1, rows, TQ), lambda j: (j, 0, 0))
    per_tile = lambda rows: jax.ShapeDtypeStruct((nq, rows, TQ), F32)
    _, r, cdim = chip_blocks.shape
    dma = pltpu.SemaphoreType.DMA
    return pl.pallas_call(
        body, grid=(nq,), name="attn_bwd",
        out_shape=(per_tile(HEADS * VROWS), per_tile(HEADS * VROWS), per_tile(D_ATTN),
                   jax.ShapeDtypeStruct(chip_blocks.shape, chip_blocks.dtype),
                   jax.ShapeDtypeStruct((N_DEV,) + small_block.shape, small_block.dtype)),
        in_specs=[pl.BlockSpec((TQ, HEADS * AUG), _row), blk, tile_t(HEADS * VROWS),
                  VMEM_WHOLE, VMEM_WHOLE, VMEM_WHOLE, VMEM_WHOLE, VMEM_WHOLE, ANY, ANY],
        out_specs=(pl.BlockSpec((nq, HEADS * VROWS, TQ), lambda j: (0, 0, 0)), tile_t(HEADS * VROWS), tile_t(D_ATTN),
                   ANY, ANY),
        scratch_shapes=[pltpu.VMEM((HEADS, TQ, TQ), F32), pltpu.VMEM((HEADS, TQ, TQ), BF16),
                        pltpu.VMEM((HEADS, TQ, TQ), BF16), pltpu.VMEM((r, cdim), chip_blocks.dtype),
                        dma((3,)), dma((3,)), dma,
                        pltpu.VMEM(small_block.shape, small_block.dtype), dma((7,)), dma((7,)), dma],
        compiler_params=_params(1),
    )(ka, v, kt3, qat3, qt3, dot3, lset3, dlt3, chip_blocks, small_block)


def _pre_attn_bwd(dqt3, dkt3, dvt3, fl, dy, x, dh1, g1, wqkv, wf, wu):
    s, d = x.shape
    nt = s // TS
    n = TS + HALO
    sub = TS // TQ
    qkv, fcols = 3 * D_ATTN, 3 * D_ATTN + LANES

    def body(dqt_ref, dkt_ref, dvt_ref, fl_ref, dy_ref, x_ref, dh1_ref, g_ref, wqkv_ref, wf_ref, wu_ref,
             gx_ref, dz_ref, dg_ref, db_ref, ybuf, ccar, dlog, dsum):
        dqkv_ref = dz_ref.at[:, 0:qkv]
        dfb_ref = dz_ref.at[:, qkv:fcols]
        dub_ref = dz_ref.at[:, fcols:]
        i = pl.program_id(0)
        ti = nt - 1 - i

        @pl.when(i == 0)
        def _():
            ybuf[TS:n, :] = jnp.zeros((HALO, D_POOL), F32)
            ccar[...] = jnp.zeros_like(ccar)
            dg_ref[...] = jnp.zeros_like(dg_ref)
            db_ref[...] = jnp.zeros_like(db_ref)
            dsum[...] = jnp.zeros_like(dsum)

        for a in range(sub):
            for h in range(HEADS):
                r = h * VROWS + HEAD_DIM
                dsum[h:h + 1, a * TQ:(a + 1) * TQ] = dqt_ref[a, r:r + 1, :] - dkt_ref[a, r:r + 1, :]
        rr = lax.broadcasted_iota(jnp.int32, (TS, TS), 0)
        cc = lax.broadcasted_iota(jnp.int32, (TS, TS), 1)
        dlog[...] = ccar[...] + _mask_matmul((cc >= rr).astype(BF16), dsum[...].T)
        ccar[...] = dlog[0:1, :]
        df = dlog[...] * jax.nn.sigmoid(-fl_ref[...])
        db_ref[...] += jnp.sum(df, axis=0, keepdims=True)
        dfb = df.astype(BF16)
        dfb_ref[...] = dfb

        t = ti * TS + lax.broadcasted_iota(jnp.int32, (TS, 1), 0)
        dy = dy_ref[...]
        for g, w in enumerate(POOL_WINDOWS):
            cols = slice(g * POOL_CH, (g + 1) * POOL_CH)
            ybuf[0:TS, cols] = dy[:, cols] / jnp.minimum(t + 1, w).astype(F32)
        for g, w in enumerate(POOL_WINDOWS):
            cols = slice(g * POOL_CH, (g + 1) * POOL_CH)
            sm = ybuf[:, cols]
            step = 1
            while step < w:
                sm = sm + pltpu.roll(sm, n - step, 0)
                step *= 2
            dub_ref[:, cols] = (sm[0:TS, :] - dy[:, cols]).astype(BF16)
        ybuf[TS:n, :] = ybuf[0:HALO, :]

        for a in range(sub):
            rows = slice(a * TQ, (a + 1) * TQ)
            for h in range(HEADS):
                src = slice(h * VROWS, h * VROWS + HEAD_DIM)
                dqkv_ref[rows, h * HEAD_DIM:(h + 1) * HEAD_DIM] = (dqt_ref[a, src, :].T * 0.125).astype(BF16)
                dqkv_ref[rows, D_ATTN + h * HEAD_DIM:D_ATTN + (h + 1) * HEAD_DIM] = dkt_ref[a, src, :].T.astype(BF16)
            dqkv_ref[rows, 2 * D_ATTN:] = dvt_ref[a].T.astype(BF16)
        dhn = _nn(dqkv_ref[...], wqkv_ref[...]) + _nn(dfb, wf_ref[...]) + _nn(dub_ref[...], wu_ref[...])
        dx, dg = _rms_bwd(x_ref[...], g_ref[...], dhn)
        gx_ref[...] = dh1_ref[...] + dx
        dg_ref[...] += dg

    rev = lambda i: (nt - 1 - i, 0)
    blk = lambda w: pl.BlockSpec((TS, w), rev)
    return pl.pallas_call(
        body, grid=(nt,), name="pre_attn_bwd",
        out_shape=(jax.ShapeDtypeStruct((s, d), F32), jax.ShapeDtypeStruct((s, fcols + D_POOL), BF16),
                   jax.ShapeDtypeStruct((1, d), F32), jax.ShapeDtypeStruct((1, LANES), F32)),
        in_specs=[pl.BlockSpec((sub, HEADS * VROWS, TQ), lambda i: (nt - 1 - i, 0, 0)),
                  pl.BlockSpec((sub, HEADS * VROWS, TQ), lambda i: (nt - 1 - i, 0, 0)),
                  pl.BlockSpec((sub, D_ATTN, TQ), lambda i: (nt - 1 - i, 0, 0)),
                  blk(LANES), blk(D_POOL), blk(d), blk(d),
                  pl.BlockSpec((1, d), _fixed), pl.BlockSpec((qkv, d), _fixed), pl.BlockSpec(wf.shape, _fixed),
                  pl.BlockSpec(wu.shape, _fixed)],
        out_specs=(blk(d), blk(fcols + D_POOL), pl.BlockSpec((1, d), _fixed), pl.BlockSpec((1, LANES), _fixed)),
        scratch_shapes=[pltpu.VMEM((n, D_POOL), F32), pltpu.VMEM((1, LANES), F32), pltpu.VMEM((TS, LANES), F32),
                        pltpu.VMEM((LANES, TS), F32)],
        compiler_params=_params(1),
    )(dqt3, dkt3, dvt3, fl, dy, x, dh1, g1, wqkv, wf, wu)


def _wgrad(a, b, out_dtype, name):
    s, m = a.shape
    n = b.shape[1]
    tm = max(t for t in range(LANES, min(m, TM_WGRAD) + 1, LANES) if m % t == 0)
    ts = min(TS_WGRAD, s)
    ns = s // ts

    def body(a_ref, b_ref, o_ref, acc):
        i = pl.program_id(1)

        @pl.when(i == 0)
        def _():
            acc[...] = jnp.zeros_like(acc)

        acc[...] += _tn(a_ref[...], b_ref[...])

        @pl.when(i == ns - 1)
        def _():
            o_ref[...] = acc[...].astype(out_dtype)

    return pl.pallas_call(
        body, grid=(m // tm, ns), name=name, out_shape=jax.ShapeDtypeStruct((m, n), out_dtype),
        in_specs=[pl.BlockSpec((ts, tm), lambda j, i: (i, j)), pl.BlockSpec((ts, n), lambda j, i: (i, 0))],
        out_specs=pl.BlockSpec((tm, n), lambda j, i: (j, 0)),
        scratch_shapes=[pltpu.VMEM((tm, n), F32)],
        compiler_params=_params(2),
    )(a, b)


def _wgrad_in(dz, hn):
    s, m = dz.shape
    n = hn.shape[1]
    ts = min(TS_WGRAD, s)
    ns = s // ts
    pad_at, pad = 3 * D_ATTN + HEADS, LANES - HEADS
    assert m == D_IN + pad and N_DEV * SHARD_IN == D_IN

    def pieces(d):
        lo, hi = d * SHARD_IN, (d + 1) * SHARD_IN
        spans = [(lo, min(hi, pad_at), 0), (max(lo, pad_at), hi, pad)]
        return [(a + shift, b - a, a - lo) for a, b, shift in spans if b > a]

    def body(a_ref, b_ref, o_ref, acc, stage):
        i = pl.program_id(0)

        @pl.when(i == 0)
        def _():
            acc[...] = jnp.zeros_like(acc)

        acc[...] += _tn(a_ref[...], b_ref[...])

        @pl.when(i == ns - 1)
        def _():
            stage[SHARD_IN:ROWS_IN, :] = jnp.zeros((ROWS_IN - SHARD_IN, n), F32)
            for d in range(N_DEV):
                for src, rows, dst in pieces(d):
                    stage[dst:dst + rows, :] = acc[src:src + rows, :]
                o_ref[d] = stage[...].astype(BF16)

    return pl.pallas_call(
        body, grid=(ns,), name="wgrad_in", out_shape=jax.ShapeDtypeStruct((N_DEV, ROWS_IN, n), BF16),
        in_specs=[pl.BlockSpec((ts, m), _row), pl.BlockSpec((ts, n), _row)],
        out_specs=pl.BlockSpec((N_DEV, ROWS_IN, n), lambda i: (0, 0, 0)),
        scratch_shapes=[pltpu.VMEM((m, n), F32), pltpu.VMEM((ROWS_IN, n), F32)],
        compiler_params=_params(1),
    )(dz, hn)


def _adamw(w, g, m, v):
    m = ADAM_B1 * m + (1.0 - ADAM_B1) * g
    v = ADAM_B2 * v + (1.0 - ADAM_B2) * (g * g)
    m_hat = m / (1.0 - ADAM_B1 ** ADAM_STEP)
    v_hat = v / (1.0 - ADAM_B2 ** ADAM_STEP)
    delta = -ADAM_LR * (m_hat / (jnp.sqrt(v_hat) + ADAM_EPS) + ADAM_WD * w)
    return delta, m, v


def _sum_update(p_ref, w_ref, m_ref, v_ref, g_ref, d_ref, nm_ref, nv_ref):
    g = p_ref[0].astype(F32)
    for k in range(1, p_ref.shape[0]):
        g = g + p_ref[k].astype(F32)
    g_ref[...] = g
    d_ref[...], nm_ref[...], nv_ref[...] = _adamw(w_ref[...], g, m_ref[...], v_ref[...])


def _reduce_update_rest(parts, w, m, v, chip_blocks, small_block):
    nk, r, c = parts.shape
    ns = r // TR_REST

    def body(p_ref, w_ref, m_ref, v_ref, b_ref, sm_ref, g_ref, d_ref, nm_ref, nv_ref, got_ref, all_ref,
             stage_b, stage_s, send_b, recv_b, local_b, send_s, recv_s, local_s):
        i = pl.program_id(0)

        @pl.when(i == 0)
        def _():
            _chips_start(b_ref, got_ref, stage_b, send_b, recv_b, local_b)
            _gather_start(sm_ref, all_ref, stage_s, send_s, recv_s, local_s)

        _sum_update(p_ref, w_ref, m_ref, v_ref, g_ref, d_ref, nm_ref, nv_ref)

        @pl.when(i == ns - 1)
        def _():
            _gather_pass_on(all_ref, send_s, recv_s)
            _chips_finish(b_ref, got_ref, send_b, recv_b)
            _gather_finish(sm_ref, all_ref, send_s, recv_s)

    blk = pl.BlockSpec((TR_REST, c), _row)
    out = jax.ShapeDtypeStruct((r, c), F32)
    dma = pltpu.SemaphoreType.DMA
    return pl.pallas_call(
        body, grid=(ns,), name="reduce_update_rest",
        out_shape=(out,) * 4 + (jax.ShapeDtypeStruct(chip_blocks.shape, chip_blocks.dtype),
                                jax.ShapeDtypeStruct((N_DEV,) + small_block.shape, small_block.dtype)),
        in_specs=[pl.BlockSpec((nk, TR_REST, c), lambda i: (0, i, 0)), blk, blk, blk, ANY, ANY],
        out_specs=(blk,) * 4 + (ANY, ANY),
        scratch_shapes=[pltpu.VMEM(chip_blocks.shape[1:], chip_blocks.dtype), pltpu.VMEM(small_block.shape, small_block.dtype),
                        dma((3,)), dma((3,)), dma, dma((7,)), dma((7,)), dma],
        compiler_params=_params(1),
    )(parts, w, m, v, chip_blocks, small_block)


def _reduce_update_big(parts, w, m, v, tr, name):
    nk, r, c = parts.shape

    def body(p_ref, w_ref, m_ref, v_ref, g_ref, d_ref, nm_ref, nv_ref):
        _sum_update(p_ref, w_ref, m_ref, v_ref, g_ref, d_ref, nm_ref, nv_ref)

    blk = pl.BlockSpec((tr, c), _row)
    out = jax.ShapeDtypeStruct((r, c), F32)
    return pl.pallas_call(
        body, grid=(r // tr,), name=name, out_shape=(out,) * 4,
        in_specs=[pl.BlockSpec((nk, tr, c), lambda i: (0, i, 0)), blk, blk, blk],
        out_specs=(blk,) * 4, compiler_params=_params(1),
    )(parts, w, m, v)


def _reduce_update_small(parts, late, w, m, v):
    nd = parts.shape[0]
    first = parts.shape[1] - late.shape[1]

    def body(p_ref, q_ref, w_ref, m_ref, v_ref, g_ref, d_ref, nm_ref, nv_ref):
        g, t = p_ref[0], q_ref[0]
        for k in range(1, nd):
            g, t = g + p_ref[k], t + q_ref[k]
        g_ref[...] = g
        g_ref[first:, :] = g[first:, :] + t
        d_ref[...], nm_ref[...], nv_ref[...] = _adamw(w_ref[...], g_ref[...], m_ref[...], v_ref[...])

    out = jax.ShapeDtypeStruct(w.shape, F32)
    return pl.pallas_call(body, name="reduce_update_small", out_shape=(out,) * 4,
                          compiler_params=pltpu.CompilerParams(vmem_limit_bytes=VMEM_LIMIT))(parts, late, w, m, v)


MESH = pl.DeviceIdType.MESH


def _copy_through_vmem(src_hbm, dst_hbm, stage, sem):
    load = pltpu.make_async_copy(src_hbm, stage, sem)
    load.start()
    load.wait()
    store = pltpu.make_async_copy(stage, dst_hbm, sem)
    store.start()
    store.wait()


class _GatherPlan:
    def __init__(self, x_ref, out_ref, send_sems, recv_sems):
        x, y, c = lax.axis_index("x"), lax.axis_index("y"), lax.axis_index("c")
        self.me, self.sibling, self.c = (x, y, c), (x, y, 1 - c), c
        self.chips = [(1 - x, y), (x, 1 - y), (1 - x, 1 - y)]
        self.x_ref, self.out_ref, self.send_sems, self.recv_sems = x_ref, out_ref, send_sems, recv_sems

    def slot(self, px, py, pc):
        return self.out_ref.at[4 * px + 2 * py + pc]

    def copy(self, k, block, to, src=None):
        return pltpu.make_async_remote_copy(
            src_ref=self.slot(*block) if src is None else src, dst_ref=self.slot(*block),
            send_sem=self.send_sems.at[k], recv_sem=self.recv_sems.at[k], device_id=to, device_id_type=MESH)

    def first(self):
        return [self.copy(0, self.me, self.sibling, src=self.x_ref)] + [
            self.copy(1 + j, self.me, (*chip, self.c), src=self.x_ref) for j, chip in enumerate(self.chips)]

    def passed(self):
        return [self.copy(4 + j, (*chip, self.c), self.sibling) for j, chip in enumerate(self.chips)]


def _gather_start(x_ref, out_ref, stage, send_sems, recv_sems, local_sem):
    plan = _GatherPlan(x_ref, out_ref, send_sems, recv_sems)
    for cp in plan.first():
        cp.start()
    _copy_through_vmem(x_ref, plan.slot(*plan.me), stage, local_sem)


def _gather_pass_on(out_ref, send_sems, recv_sems):
    plan = _GatherPlan(None, out_ref, send_sems, recv_sems)
    passed = plan.passed()
    for j, chip in enumerate(plan.chips):
        plan.copy(1 + j, (*chip, plan.c), plan.me).wait_recv()
        passed[j].start()


def _gather_finish(x_ref, out_ref, send_sems, recv_sems):
    plan = _GatherPlan(x_ref, out_ref, send_sems, recv_sems)
    plan.copy(0, plan.sibling, plan.me).wait_recv()
    for j, chip in enumerate(plan.chips):
        plan.copy(4 + j, (*chip, 1 - plan.c), plan.me).wait_recv()
    for cp in plan.first() + plan.passed():
        cp.wait_send()


def _all_gather(xs, name):
    r, cdim = xs.shape

    def body(x_ref, out_ref, stage, send_sems, recv_sems, local_sem):
        _gather_start(x_ref, out_ref, stage, send_sems, recv_sems, local_sem)
        _gather_pass_on(out_ref, send_sems, recv_sems)
        _gather_finish(x_ref, out_ref, send_sems, recv_sems)

    return pl.pallas_call(
        body, name=name, out_shape=jax.ShapeDtypeStruct((N_DEV, r, cdim), xs.dtype),
        in_specs=[ANY], out_specs=ANY,
        scratch_shapes=[pltpu.VMEM((r, cdim), xs.dtype), pltpu.SemaphoreType.DMA((7,)), pltpu.SemaphoreType.DMA((7,)),
                        pltpu.SemaphoreType.DMA],
        compiler_params=pltpu.CompilerParams(vmem_limit_bytes=VMEM_LIMIT),
    )(xs)


def _rs_pair_sum(core, pieces, offsets, rows, name):
    cdim = pieces[0].shape[2]
    nk = N_DEV // 2
    npc = len(pieces)
    spans = [(o, t.shape[1]) for t, o in zip(pieces, offsets)]
    ends = [o + n for o, n in spans]
    gaps = [(a, b - a) for a, b in zip(ends, [o for o, _ in spans[1:]] + [rows]) if b > a]

    def body(core_ref, *refs):
        own, src, o_ref = refs[:npc], refs[npc:2 * npc], refs[2 * npc]
        landing, send_sems, recv_sems = refs[2 * npc + 1:]
        k = pl.program_id(0)
        x, y, c = lax.axis_index("x"), lax.axis_index("y"), lax.axis_index("c")

        def copies(kk):
            return [pltpu.make_async_remote_copy(
                src_ref=src[p].at[2 * kk + (1 - c)], dst_ref=landing.at[kk, pl.ds(o, n)],
                send_sem=send_sems.at[kk, p], recv_sem=recv_sems.at[kk, p], device_id=(x, y, 1 - c),
                device_id_type=MESH) for p, (o, n) in enumerate(spans)]

        @pl.when(k == 0)
        def _():
            for kk in range(nk):
                for cp in copies(kk):
                    cp.start()

        for cp, piece, (o, n) in zip(copies(k), own, spans):
            cp.wait_recv()
            o_ref[0, o:o + n, :] = (piece[0].astype(F32) + landing[k, o:o + n, :].astype(F32)).astype(BF16)
        for o, n in gaps:
            o_ref[0, o:o + n, :] = jnp.zeros((n, cdim), BF16)

        @pl.when(k == nk - 1)
        def _():
            for kk in range(nk):
                for cp in copies(kk):
                    cp.wait_send()

    own_specs = [pl.BlockSpec((1, n, cdim), lambda k, core_ref: (2 * k + core_ref[0], 0, 0)) for _, n in spans]
    return pl.pallas_call(
        body, name=name, out_shape=jax.ShapeDtypeStruct((nk, rows, cdim), BF16),
        grid_spec=pltpu.PrefetchScalarGridSpec(
            num_scalar_prefetch=1, grid=(nk,),
            in_specs=own_specs + [ANY] * npc,
            out_specs=pl.BlockSpec((1, rows, cdim), lambda k, core_ref: (k, 0, 0)),
            scratch_shapes=[pltpu.VMEM((nk, rows, cdim), BF16), pltpu.SemaphoreType.DMA((nk, npc)),
                            pltpu.SemaphoreType.DMA((nk, npc))]),
        compiler_params=_params(1),
    )(core, *pieces, *pieces)


def _chips_start(b_ref, out_ref, stage, send_sems, recv_sems, local_sem):
    x, y, c = lax.axis_index("x"), lax.axis_index("y"), lax.axis_index("c")
    mychip = 2 * x + y
    for j, (px, py) in enumerate([(1 - x, y), (x, 1 - y), (1 - x, 1 - y)]):
        pltpu.make_async_remote_copy(
            src_ref=b_ref.at[2 * px + py], dst_ref=out_ref.at[mychip],
            send_sem=send_sems.at[j], recv_sem=recv_sems.at[j], device_id=(px, py, c), device_id_type=MESH).start()
    _copy_through_vmem(b_ref.at[mychip], out_ref.at[mychip], stage, local_sem)


def _chips_finish(b_ref, out_ref, send_sems, recv_sems):
    x, y, c = lax.axis_index("x"), lax.axis_index("y"), lax.axis_index("c")
    for j, (px, py) in enumerate([(1 - x, y), (x, 1 - y), (1 - x, 1 - y)]):
        pltpu.make_async_remote_copy(
            src_ref=b_ref.at[2 * px + py], dst_ref=out_ref.at[2 * px + py],
            send_sem=send_sems.at[j], recv_sem=recv_sems.at[j], device_id=(px, py, c), device_id_type=MESH).wait()


def _pad_rows(a, rows):
    return jnp.pad(a, ((0, rows - a.shape[0]), (0, 0)))


def _pack_in(w_in):
    return _pad_rows(w_in[0].T, ROWS_IN)


def _unpack_in(r):
    return r[0:SHARD_IN].T[None]


def _pack_rest(w_out, w_gate, w_up, w_down, w_ple, w_pg):
    head = _pad_rows(jnp.concatenate([w_out[0], w_pg[0], w_ple[0].T.reshape(32, D_MODEL)], axis=0), OFF_GATE)
    return jnp.concatenate([head, w_gate[0].T, w_up[0].T, w_down[0]], axis=0)


def _unpack_rest(r):
    return (r[0:OFF_PG][None], r[OFF_GATE:OFF_UP].T[None], r[OFF_UP:OFF_DOWN].T[None], r[OFF_DOWN:ROWS_REST][None],
            r[OFF_PLE:OFF_PLE + 32].reshape(128, D_PLE).T[None], r[OFF_PG:OFF_PLE][None])


def _pack_small(w_pool, g_mix_pre, g_mix_post, g_ffn_pre, g_ffn_post, g_ple, g_attn, g_pool, pool_scale, b_forget,
                loss=None):
    row = lambda vrow: vrow.reshape(1, -1)
    misc = [row(pool_scale), row(b_forget), row(loss) if loss is not None else jnp.zeros((1, 1), F32),
            jnp.zeros((1, D_MODEL - COL_LOSS - 1), F32)]
    rows = [w_pool.reshape(64, D_MODEL), row(g_mix_pre), row(g_mix_post), row(g_ffn_pre), row(g_ffn_post), row(g_ple),
            jnp.concatenate([row(g_attn), row(g_pool)], axis=1), jnp.concatenate(misc, axis=1),
            jnp.zeros((SMALL_ROWS - ROW_MISC - 1, D_MODEL), F32)]
    return jnp.concatenate(rows, axis=0)


def _pack_small_late(g_mix_pre, b_forget):
    misc = [jnp.zeros((1, COL_B_FORGET), F32), b_forget.reshape(1, -1), jnp.zeros((1, D_MODEL - COL_LOSS), F32)]
    return jnp.concatenate([g_mix_pre.reshape(1, -1), jnp.zeros((ROW_MISC - ROW_G_MIX_PRE - 1, D_MODEL), F32),
                            jnp.concatenate(misc, axis=1), jnp.zeros((SMALL_ROWS - ROW_MISC - 1, D_MODEL), F32)], axis=0)


def _unpack_small(r):
    gains, misc = r[ROW_GROUP_GAINS:ROW_GROUP_GAINS + 1], r[ROW_MISC:ROW_MISC + 1]
    return dict(
        w_pool=r[0:64].reshape(1, 4, POOL_CH, POOL_CH), g_mix_pre=r[ROW_G_MIX_PRE:ROW_G_MIX_PRE + 1],
        g_mix_post=r[ROW_G_MIX_POST:ROW_G_MIX_POST + 1], g_ffn_pre=r[ROW_G_FFN_PRE:ROW_G_FFN_PRE + 1],
        g_ffn_post=r[ROW_G_FFN_POST:ROW_G_FFN_POST + 1], g_ple=r[ROW_G_PLE:ROW_G_PLE + 1],
        g_attn_grp=gains[:, 0:D_ATTN], g_pool_grp=gains[:, D_ATTN:D_ATTN + D_POOL],
        pool_scale=misc[:, 0:D_POOL], b_forget=misc[:, COL_B_FORGET:COL_B_FORGET + HEADS])


def _step(x, p, tgt, small, in_w, in_m, in_v, rest_w, rest_m, rest_v):
    core = lax.axis_index("c").astype(jnp.int32).reshape(1)
    win_t = _all_gather(in_w.astype(BF16), "gather_w_in")[:, 0:SHARD_IN].reshape(D_IN, D_MODEL)
    wqkv = win_t
    wf = _pad_rows(win_t[3 * D_ATTN:3 * D_ATTN + HEADS], LANES)
    wu = win_t[3 * D_ATTN + HEADS:]
    wpool = small["w_pool"].astype(BF16)
    bpad = jnp.pad(small["b_forget"], ((0, 0), (0, LANES - HEADS)))

    lay = _attn_layout_constants()
    rest_b = rest_w.astype(BF16)
    hn, qt3, ka, v, qat3, vt3, kt3, fl, y, mpre, gh = _pre_attn_fwd(x, small["g_mix_pre"], wqkv, wf, wu, bpad, wpool, lay,
                                                                 rest_b[0:OFF_GATE])
    a, lset3, gf = _attn_fwd(ka, qat3, vt3, rest_b[OFF_GATE:])
    wple_t = gh[:, OFF_PLE:OFF_PLE + 32].reshape(D_MODEL, D_PLE)
    mix, o, h1, hn2 = _post_attn_fwd(a, mpre, x, small["g_attn_grp"], small["g_pool_grp"], small["pool_scale"], gh,
                                     small["g_mix_post"], small["g_ffn_pre"])
    gate, up, act, ff, h2 = _ffn_fwd(hn2, gf, gf, gf, h1, small["g_ffn_post"])
    dh2, dff, dgl, dpp, h2b, pb, loss8, dg_ple, dg_ffn_post = _tail_fwd_bwd(
        h2, p, tgt, ff, wple_t, gh, small["g_ple"], small["g_ffn_post"])
    dgate, dup, dh1, dg_ffn_pre = _ffn_bwd(dff, gate, up, gf, gf, gf, h1, dh2, small["g_ffn_pre"])
    dob, dat3, dlt3, dmpb, dy, dg_mix_post, dg_attn, dg_pool, dps = _post_attn_bwd(
        dh1, o, a, mpre, gh, wpool, small["g_mix_post"], small["g_attn_grp"], small["g_pool_grp"], small["pool_scale"])

    nd = N_DEV
    send_rest = [
        _wgrad(mix, dob, BF16, "wgrad_out").reshape(nd, 128, D_MODEL),
        _wgrad(h2b, dgl, BF16, "wgrad_ple_gate").reshape(nd, 128, D_MODEL),
        _wgrad(dpp, pb, BF16, "wgrad_ple").reshape(nd, 32, D_MODEL),
        _wgrad(dgate, hn2, BF16, "wgrad_gate").reshape(nd, SHARD_FF, D_MODEL),
        _wgrad(dup, hn2, BF16, "wgrad_up").reshape(nd, SHARD_FF, D_MODEL),
        _wgrad(act, dff, BF16, "wgrad_down").reshape(nd, SHARD_FF, D_MODEL)]
    pair_rest = _rs_pair_sum(core, send_rest, [0, OFF_PG, OFF_PLE, OFF_GATE, OFF_UP, OFF_DOWN], ROWS_REST,
                             "rs_pair_sum_rest")

    dwp = _wgrad(y, dmpb, F32, "wgrad_pool")
    dw_pool = jnp.stack([dwp[g * POOL_CH:(g + 1) * POOL_CH, g * POOL_CH:(g + 1) * POOL_CH] for g in range(4)])
    small_part = _pack_small(dw_pool, jnp.zeros((1, D_MODEL), F32), dg_mix_post, dg_ffn_pre, dg_ffn_post, dg_ple,
                             dg_attn, dg_pool, dps, jnp.zeros((1, HEADS), F32), loss8[0:1, 0:1])
    dqt3, dkt3, dvt3, chips_rest, small_all = _attn_bwd(ka, v, kt3, qat3, qt3, dat3, lset3, dlt3, pair_rest, small_part)

    gx, dz, dg_mix_pre, db = _pre_attn_bwd(dqt3, dkt3, dvt3, fl, dy, x, dh1, small["g_mix_pre"], wqkv, wf, wu)

    pair_in = _rs_pair_sum(core, [_wgrad_in(dz, hn)], [0], ROWS_IN, "rs_pair_sum_in")

    small_late = _pack_small_late(dg_mix_pre, db[:, 0:HEADS])
    *upd_rest, chips_in, late_all = _reduce_update_rest(chips_rest, rest_w, rest_m, rest_v, pair_in, small_late)
    upd_in = _reduce_update_big(chips_in, in_w, in_m, in_v, ROWS_IN, "reduce_update_in")
    return gx, (small_all, late_all), upd_in, upd_rest


def kernel(x, p, g_mix_pre, w_in, b_forget, g_attn_grp, g_pool_grp, w_pool, pool_scale, w_out, g_mix_post, g_ffn_pre, w_ffn_gate, w_ffn_up, w_ffn_down, g_ffn_post, w_ple_proj, g_ple, w_ple_gate, loss_target, m_g_mix_pre, m_w_in, m_b_forget, m_g_attn_grp, m_g_pool_grp, m_w_pool, m_pool_scale, m_w_out, m_g_mix_post, m_g_ffn_pre, m_w_ffn_gate, m_w_ffn_up, m_w_ffn_down, m_g_ffn_post, m_w_ple_proj, m_g_ple, m_w_ple_gate, v_g_mix_pre, v_w_in, v_b_forget, v_g_attn_grp, v_g_pool_grp, v_w_pool, v_pool_scale, v_w_out, v_g_mix_post, v_g_ffn_pre, v_w_ffn_gate, v_w_ffn_up, v_w_ffn_down, v_g_ffn_post, v_w_ple_proj, v_g_ple, v_w_ple_gate):
    small = dict(w_pool=w_pool[0], g_mix_pre=g_mix_pre, g_mix_post=g_mix_post, g_ffn_pre=g_ffn_pre,
                 g_ffn_post=g_ffn_post, g_ple=g_ple, g_attn_grp=g_attn_grp, g_pool_grp=g_pool_grp,
                 pool_scale=pool_scale, b_forget=b_forget)
    gx, small_all, upd_in, upd_rest = _step(
        x[0], p[0, 0], loss_target[0], small, _pack_in(w_in), _pack_in(m_w_in), _pack_in(v_w_in),
        _pack_rest(w_out, w_ffn_gate, w_ffn_up, w_ffn_down, w_ple_proj, w_ple_gate),
        _pack_rest(m_w_out, m_w_ffn_gate, m_w_ffn_up, m_w_ffn_down, m_w_ple_proj, m_w_ple_gate),
        _pack_rest(v_w_out, v_w_ffn_gate, v_w_ffn_up, v_w_ffn_down, v_w_ple_proj, v_w_ple_gate))

    sm_w = _pack_small(w_pool, g_mix_pre, g_mix_post, g_ffn_pre, g_ffn_post, g_ple, g_attn_grp, g_pool_grp, pool_scale, b_forget)
    sm_m = _pack_small(m_w_pool, m_g_mix_pre, m_g_mix_post, m_g_ffn_pre, m_g_ffn_post, m_g_ple, m_g_attn_grp, m_g_pool_grp, m_pool_scale, m_b_forget)
    sm_v = _pack_small(v_w_pool, v_g_mix_pre, v_g_mix_post, v_g_ffn_pre, v_g_ffn_post, v_g_ple, v_g_attn_grp, v_g_pool_grp, v_pool_scale, v_b_forget)
    upd_small = _reduce_update_small(*small_all, sm_w, sm_m, sm_v)
    loss = upd_small[0][ROW_MISC, COL_LOSS]

    def leaves(k):
        b_out, b_gate, b_up, b_down, b_ple, b_pg = _unpack_rest(upd_rest[k])
        s = _unpack_small(upd_small[k])
        return (s["g_mix_pre"], _unpack_in(upd_in[k]), s["b_forget"], s["g_attn_grp"], s["g_pool_grp"], s["w_pool"],
                s["pool_scale"], b_out, s["g_mix_post"], s["g_ffn_pre"], b_gate, b_up, b_down, s["g_ffn_post"], b_ple,
                s["g_ple"], b_pg)

    return (loss, gx[None], *leaves(0), *leaves(1), *leaves(2), *leaves(3))
```

```python
import functools

import jax
import jax.numpy as jnp
from jax import lax
from jax.experimental import pallas as pl
from jax.experimental.pallas import tpu as pltpu

F32 = jnp.float32
BF16 = jnp.bfloat16
HIGHEST = lax.Precision.HIGHEST

D_MODEL = 1024
HEADS = 8
HEAD_DIM = 64
D_ATTN = HEADS * HEAD_DIM
POOL_WINDOWS = (2, 4, 8, 16)
POOL_CH = 128
D_POOL = POOL_CH * len(POOL_WINDOWS)
D_FF = 2816
D_PLE = 256
D_IN = 3 * D_ATTN + HEADS + D_POOL
RMS_EPS = 1e-6
N_DEV = 8

ADAM_LR = 0.001
ADAM_B1 = 0.9
ADAM_B2 = 0.999
ADAM_EPS = 1e-08
ADAM_WD = 0.01
ADAM_STEP = 10

LANES = 128
HALO = 16
TS = 512
TS_FF = 512
TS_WGRAD = 1024
TM_WGRAD = 2176
TQ = 256
TN_FF = 1408
NEG = -1e30
VMEM_LIMIT = 56 * 1024 * 1024

SHARD_IN = 257
ROWS_IN = 272
SHARD_FF = 352
OFF_PG = 128
OFF_PLE = 256
OFF_GATE = SHARD_FF
OFF_UP = 2 * SHARD_FF
OFF_DOWN = 3 * SHARD_FF
ROWS_REST = 4 * SHARD_FF
TR_REST = SHARD_FF

SMALL_ROWS = 72
ROW_G_MIX_PRE, ROW_G_MIX_POST, ROW_G_FFN_PRE, ROW_G_FFN_POST, ROW_G_PLE = 64, 65, 66, 67, 68
ROW_GROUP_GAINS, ROW_MISC = 69, 70
COL_B_FORGET = D_POOL
COL_LOSS = D_POOL + HEADS


def _nn(a, b):
    return jnp.dot(a, b, preferred_element_type=F32)


def _nt(a, b):
    return lax.dot_general(a, b, (((1,), (1,)), ((), ())), preferred_element_type=F32)


def _tn(a, b):
    return lax.dot_general(a, b, (((0,), (0,)), ((), ())), preferred_element_type=F32)


def _rstd(v):
    return lax.rsqrt(jnp.mean(v * v, axis=-1, keepdims=True) + RMS_EPS)


def _rms_bwd(v, g, dy):
    r = _rstd(v)
    vh = v * r
    t = dy * g
    dv = r * (t - vh * jnp.mean(t * vh, axis=-1, keepdims=True))
    return dv, jnp.sum(dy * vh, axis=0, keepdims=True)


def _split3(v):
    hi = v.astype(BF16)
    rest = v - hi.astype(F32)
    mid = rest.astype(BF16)
    return hi, mid, (rest - mid.astype(F32)).astype(BF16)


def _mask_matmul(mask, v):
    hi, mid, lo = _split3(v)
    return _nn(mask, lo) + _nn(mask, mid) + _nn(mask, hi)


def _params(n_grid):
    return pltpu.CompilerParams(dimension_semantics=("arbitrary",) * n_grid, vmem_limit_bytes=VMEM_LIMIT)


def _row(i):
    return (i, 0)


def _fixed(*_):
    return (0, 0)


def _spec_square(part):
    return pl.BlockSpec((N_DEV, 128, D_MODEL), lambda *_: (0, part, 0))


def _spec_ff(part):
    return pl.BlockSpec((TN_FF // SHARD_FF, SHARD_FF, D_MODEL), lambda i, j: (j, part, 0))


assert TS == 2 * TQ and TN_FF % SHARD_FF == 0
_HALVES = (slice(0, TQ), slice(TQ, TS))

VMEM_WHOLE = pl.BlockSpec(memory_space=pltpu.VMEM)
SMEM_WHOLE = pl.BlockSpec(memory_space=pltpu.SMEM)
ANY = pl.BlockSpec(memory_space=pl.ANY)


LOG2E = 1.4426950408889634
VROWS = HEAD_DIM + 16
AUG = 128
BIAS_LANE = HEAD_DIM
ONE_LANE = HEAD_DIM + 3
SPARE_LANE = HEADS


def _attn_layout_constants():
    import numpy as np
    place = np.zeros((D_ATTN, HEADS * AUG), np.float32)
    for r in range(D_ATTN):
        place[r, (r // HEAD_DIM) * AUG + r % HEAD_DIM] = 1.0
    bias_k = np.zeros((3, LANES, HEADS * AUG), np.float32)
    bias_q = np.zeros((3, LANES, HEADS * AUG), np.float32)
    for h in range(HEADS):
        for part in range(3):
            bias_k[part, h, h * AUG + BIAS_LANE + part] = -1.0
            bias_q[part, h, h * AUG + ONE_LANE + part] = 1.0
            bias_k[0, SPARE_LANE, h * AUG + ONE_LANE + part] = 1.0
            bias_q[0, SPARE_LANE, h * AUG + BIAS_LANE + part] = 1.0
    as_bf = lambda a: jnp.asarray(a, BF16)
    return dict(place=as_bf(place), place_t=as_bf(place.T), bias_k=as_bf(bias_k),
                bias_q_t=as_bf(bias_q.transpose(0, 2, 1)))


def _pre_attn_fwd(x, g1, wqkv, wf, wu, bpad, wpool, lay, own_block):
    s, d = x.shape
    nt = s // TS
    sub = TS // TQ

    def body(x_ref, g_ref, wqkv_ref, wf_ref, wu_ref, b_ref, wp_ref, place_ref, place_t_ref, bk_ref, bqt_ref, own_ref,
             hn_ref, qt_ref, ka_ref, v_ref, qat_ref, vt_ref, kt_ref, fl_ref, y_ref, mp_ref, all_ref,
             ubuf, ccar, cbuf, stage, send_sems, recv_sems, local_sem):
        i = pl.program_id(0)

        @pl.when(i == 0)
        def _():
            _gather_start(own_ref, all_ref, stage, send_sems, recv_sems, local_sem)
            ubuf[0:HALO, :] = jnp.zeros((HALO, D_POOL), F32)
            ccar[...] = jnp.zeros_like(ccar)

        @pl.when(i == max(nt - 2, 0))
        def _():
            _gather_pass_on(all_ref, send_sems, recv_sems)

        xv = x_ref[...]
        hn = (xv * _rstd(xv) * g_ref[...]).astype(BF16)
        hn_ref[...] = hn
        zq = _nt(hn, wqkv_ref[...])
        qt = (zq[:, 0:D_ATTN] * 0.125).astype(BF16).T
        qb = (zq[:, 0:D_ATTN] * (0.125 * LOG2E)).astype(BF16)
        kb = zq[:, D_ATTN:2 * D_ATTN].astype(BF16)
        vb = zq[:, 2 * D_ATTN:3 * D_ATTN].astype(BF16)
        v_ref[...] = vb

        fl = _nt(hn, wf_ref[...]) + b_ref[...]
        fl_ref[...] = fl
        logf = jax.nn.log_sigmoid(fl)
        rr = lax.broadcasted_iota(jnp.int32, (TS, TS), 0)
        cc = lax.broadcasted_iota(jnp.int32, (TS, TS), 1)
        c = _mask_matmul((cc <= rr).astype(BF16), logf) + ccar[...]
        cbuf[...] = c
        ccar[...] = cbuf[TS - 1:TS, :]
        hi, mid, lo = _split3(c * LOG2E)
        lane = lax.broadcasted_iota(jnp.int32, (TS, LANES), 1)
        parts = (jnp.where(lane == SPARE_LANE, 1.0, hi).astype(BF16), mid, lo)
        ka = _nn(kb, place_ref[...])
        qat = _nt(place_t_ref[...], qb)
        for part in range(3):
            ka = ka + _nn(parts[part], bk_ref[part])
            qat = qat + _nt(bqt_ref[part], parts[part])
        ka_ref[...] = ka.astype(BF16)
        qat = qat.astype(BF16)
        vt = vb.T
        kt = kb.T
        for a in range(sub):
            cols = slice(a * TQ, (a + 1) * TQ)
            qat_ref[a] = qat[:, cols]
            for ref, mat in ((qt_ref, qt), (kt_ref, kt), (vt_ref, vt)):
                for h in range(HEADS):
                    ref[a, h * VROWS:h * VROWS + HEAD_DIM, :] = mat[h * HEAD_DIM:(h + 1) * HEAD_DIM, cols]
                    ref[a, h * VROWS + HEAD_DIM:(h + 1) * VROWS, :] = jnp.ones((VROWS - HEAD_DIM, TQ), BF16)

        u = _nt(hn, wu_ref[...])
        ubuf[HALO:HALO + TS, :] = u
        t = i * TS + lax.broadcasted_iota(jnp.int32, (TS, 1), 0)
        for g, w in enumerate(POOL_WINDOWS):
            cols = slice(g * POOL_CH, (g + 1) * POOL_CH)
            sm = ubuf[:, cols]
            step = 1
            while step < w:
                sm = sm + pltpu.roll(sm, step, 0)
                step *= 2
            cnt = jnp.minimum(t + 1, w).astype(F32)
            yg = (sm[HALO:, :] / cnt - u[:, cols]).astype(BF16)
            y_ref[:, cols] = yg
            mp_ref[:, cols] = _nn(yg, wp_ref[g])
        ubuf[0:HALO, :] = u[TS - HALO:, :]

        @pl.when(i == nt - 1)
        def _():
            _gather_finish(own_ref, all_ref, send_sems, recv_sems)

    nq = s // TQ
    aug = HEADS * AUG
    outs = (
        jax.ShapeDtypeStruct((s, d), BF16), jax.ShapeDtypeStruct((nq, HEADS * VROWS, TQ), BF16),
        jax.ShapeDtypeStruct((s, aug), BF16), jax.ShapeDtypeStruct((s, D_ATTN), BF16),
        jax.ShapeDtypeStruct((nq, aug, TQ), BF16), jax.ShapeDtypeStruct((nq, HEADS * VROWS, TQ), BF16),
        jax.ShapeDtypeStruct((nq, HEADS * VROWS, TQ), BF16),
        jax.ShapeDtypeStruct((s, LANES), F32),
        jax.ShapeDtypeStruct((s, D_POOL), BF16), jax.ShapeDtypeStruct((s, D_POOL), F32),
        jax.ShapeDtypeStruct((N_DEV,) + own_block.shape, own_block.dtype),
    )
    fixed3 = lambda i: (0, 0, 0)
    tiles3 = lambda rows: pl.BlockSpec((sub, rows, TQ), lambda i: (i, 0, 0))
    return pl.pallas_call(
        body, grid=(nt,), out_shape=outs, name="pre_attn_fwd",
        in_specs=[pl.BlockSpec((TS, d), _row), pl.BlockSpec((1, d), _fixed),
                  pl.BlockSpec((3 * D_ATTN, d), _fixed), pl.BlockSpec(wf.shape, _fixed), pl.BlockSpec(wu.shape, _fixed),
                  pl.BlockSpec((1, LANES), _fixed), pl.BlockSpec(wpool.shape, fixed3),
                  pl.BlockSpec(lay["place"].shape, _fixed), pl.BlockSpec(lay["place_t"].shape, _fixed),
                  pl.BlockSpec(lay["bias_k"].shape, fixed3), pl.BlockSpec(lay["bias_q_t"].shape, fixed3), ANY],
        out_specs=(pl.BlockSpec((TS, d), _row), tiles3(HEADS * VROWS),
                   pl.BlockSpec((TS, aug), _row), pl.BlockSpec((TS, D_ATTN), _row),
                   tiles3(aug), tiles3(HEADS * VROWS), tiles3(HEADS * VROWS),
                   pl.BlockSpec((TS, LANES), _row),
                   pl.BlockSpec((TS, D_POOL), _row), pl.BlockSpec((TS, D_POOL), _row), ANY),
        scratch_shapes=[pltpu.VMEM((TS + HALO, D_POOL), F32), pltpu.VMEM((1, LANES), F32), pltpu.VMEM((TS, LANES), F32),
                        pltpu.VMEM(own_block.shape, own_block.dtype),
                        pltpu.SemaphoreType.DMA((7,)), pltpu.SemaphoreType.DMA((7,)), pltpu.SemaphoreType.DMA],
        compiler_params=_params(1),
    )(x, g1, wqkv, wf, wu, bpad, wpool, lay["place"], lay["place_t"], lay["bias_k"], lay["bias_q_t"], own_block)


def _causal_in_tile():
    krow = lax.broadcasted_iota(jnp.int32, (TQ, TQ), 0)
    qcol = lax.broadcasted_iota(jnp.int32, (TQ, TQ), 1)
    return krow <= qcol


def _attn_fwd(ka, qat3, vt3, own_block):
    s = ka.shape[0]
    nq = s // TQ
    pass_on_step = max(nq - 2, 0)

    def body(qa_ref, ka_ref, vt_ref, own_ref, a_ref, lset_ref, all_ref, acc, out_t, st_scr, pt_scr,
             stage, send_sems, recv_sems, local_sem):
        i = pl.program_id(0)

        @pl.when(i == 0)
        def _():
            _gather_start(own_ref, all_ref, stage, send_sems, recv_sems, local_sem)

        @pl.when(i == pass_on_step)
        def _():
            _gather_pass_on(all_ref, send_sems, recv_sems)

        acc[...] = jnp.zeros_like(acc)

        def tile(j, stats, masked):
            tile_max = []
            for h in range(HEADS):
                aug = slice(h * AUG, (h + 1) * AUG)
                st = _nn(ka_ref[pl.ds(j * TQ, TQ), aug], qa_ref[0, aug, :])
                if masked:
                    st = jnp.where(_causal_in_tile(), st, NEG)
                st_scr[h] = st
                tile_max.append(jnp.max(st, axis=0, keepdims=True))
            new, scale = [], []
            for h in range(HEADS):
                m_new = jnp.maximum(stats[h], tile_max[h])
                scale.append(jnp.exp2(stats[h] - m_new))
                pt_scr[h] = jnp.exp2(st_scr[h] - m_new).astype(BF16)
                new.append(m_new)
            for h in range(HEADS):
                rows = slice(h * VROWS, (h + 1) * VROWS)
                acc[rows, :] = scale[h] * acc[rows, :] + _nn(vt_ref[j, rows, :], pt_scr[h])
            return tuple(new)

        init = tuple(jnp.full((1, TQ), NEG, F32) for _ in range(HEADS))
        stats = lax.fori_loop(0, i, functools.partial(tile, masked=False), init)
        stats = tile(i, stats, True)
        for h in range(HEADS):
            denom = acc[h * VROWS + HEAD_DIM:h * VROWS + HEAD_DIM + 1, :]
            out_t[h * HEAD_DIM:(h + 1) * HEAD_DIM, :] = acc[h * VROWS:h * VROWS + HEAD_DIM, :] / denom
            lset_ref[0, h:h + 1, :] = stats[h] + jnp.log2(denom)
        a_ref[...] = out_t[...].T

        @pl.when(i == nq - 1)
        def _():
            _gather_finish(own_ref, all_ref, send_sems, recv_sems)

    r, cdim = own_block.shape
    return pl.pallas_call(
        body, grid=(nq,), name="attn_fwd",
        out_shape=(jax.ShapeDtypeStruct((s, D_ATTN), F32), jax.ShapeDtypeStruct((nq, HEADS, TQ), F32),
                   jax.ShapeDtypeStruct((N_DEV, r, cdim), own_block.dtype)),
        in_specs=[pl.BlockSpec((1, HEADS * AUG, TQ), lambda i: (i, 0, 0)), VMEM_WHOLE, VMEM_WHOLE, ANY],
        out_specs=(pl.BlockSpec((TQ, D_ATTN), _row), pl.BlockSpec((1, HEADS, TQ), lambda i: (i, 0, 0)), ANY),
        scratch_shapes=[pltpu.VMEM((HEADS * VROWS, TQ), F32), pltpu.VMEM((D_ATTN, TQ), F32),
                        pltpu.VMEM((HEADS, TQ, TQ), F32), pltpu.VMEM((HEADS, TQ, TQ), BF16),
                        pltpu.VMEM((r, cdim), own_block.dtype),
                        pltpu.SemaphoreType.DMA((7,)), pltpu.SemaphoreType.DMA((7,)), pltpu.SemaphoreType.DMA],
        compiler_params=_params(1),
    )(qat3, ka, vt3, own_block)


def _post_attn_fwd(a, mpre, x, g_attn, g_pool, pscale, wout, g_post, g_ffn_pre):
    s, d = x.shape

    def body(a_ref, mp_ref, x_ref, ga_ref, gp_ref, ps_ref, wo_ref, gpost_ref, gpre_ref,
             mix_ref, o_ref, h1_ref, hn2_ref):
        for rows in _HALVES:
            av = a_ref[rows, :]
            mix_ref[rows, 0:D_ATTN] = (av * _rstd(av) * ga_ref[...]).astype(BF16)
            mv = mp_ref[rows, :] * ps_ref[...]
            mix_ref[rows, D_ATTN:] = (mv * _rstd(mv) * gp_ref[...]).astype(BF16)
            o = _nn(mix_ref[rows, :], wo_ref[...].reshape(d, d))
            o_ref[rows, :] = o
            h1 = x_ref[rows, :] + o * _rstd(o) * gpost_ref[...]
            h1_ref[rows, :] = h1
            hn2_ref[rows, :] = (h1 * _rstd(h1) * gpre_ref[...]).astype(BF16)

    vec = lambda n: pl.BlockSpec((1, n), _fixed)
    return pl.pallas_call(
        body, grid=(s // TS,), name="post_attn_fwd",
        out_shape=(jax.ShapeDtypeStruct((s, d), BF16), jax.ShapeDtypeStruct((s, d), F32),
                   jax.ShapeDtypeStruct((s, d), F32), jax.ShapeDtypeStruct((s, d), BF16)),
        in_specs=[pl.BlockSpec((TS, D_ATTN), _row), pl.BlockSpec((TS, D_POOL), _row), pl.BlockSpec((TS, d), _row),
                  vec(D_ATTN), vec(D_POOL), vec(D_POOL), _spec_square(0), vec(d), vec(d)],
        out_specs=(pl.BlockSpec((TS, d), _row),) * 4,
        compiler_params=_params(1),
    )(a, mpre, x, g_attn, g_pool, pscale, wout, g_post, g_ffn_pre)


def _ffn_fwd(hn2, wg, wu, wd, h1, g_post):
    s, d = h1.shape
    nc = D_FF // TN_FF
    ts = min(TS_FF, s)

    def body(hn_ref, wg_ref, wu_ref, wd_ref, h1_ref, g_ref, gate_ref, up_ref, act_ref, ff_ref, h2_ref, acc):
        j = pl.program_id(1)

        @pl.when(j == 0)
        def _():
            acc[...] = jnp.zeros_like(acc)

        for r in range(2):
            rows = slice(r * (ts // 2), (r + 1) * (ts // 2))
            hn = hn_ref[rows, :]
            gt = _nt(hn, wg_ref[...].reshape(TN_FF, d))
            up = _nt(hn, wu_ref[...].reshape(TN_FF, d))
            gate_ref[rows, :] = gt.astype(BF16)
            up_ref[rows, :] = up.astype(BF16)
            act_ref[rows, :] = (gt * jax.nn.sigmoid(gt) * up).astype(BF16)
            acc[rows, :] += _nn(act_ref[rows, :], wd_ref[...].reshape(TN_FF, d))

        @pl.when(j == nc - 1)
        def _():
            ff = acc[...]
            ff_ref[...] = ff
            h2_ref[...] = h1_ref[...] + ff * _rstd(ff) * g_ref[...]

    rowblk = pl.BlockSpec((ts, d), lambda i, j: (i, 0))
    chunk = pl.BlockSpec((ts, TN_FF), lambda i, j: (i, j))
    return pl.pallas_call(
        body, grid=(s // ts, nc), name="ffn_fwd",
        out_shape=(jax.ShapeDtypeStruct((s, D_FF), BF16),) * 3 + (jax.ShapeDtypeStruct((s, d), F32),) * 2,
        in_specs=[rowblk, _spec_ff(0), _spec_ff(1), _spec_ff(2), rowblk, pl.BlockSpec((1, d), lambda i, j: (0, 0))],
        out_specs=(chunk, chunk, chunk, rowblk, rowblk),
        scratch_shapes=[pltpu.VMEM((ts, d), F32)],
        compiler_params=_params(2),
    )(hn2, wg, wu, wd, h1, g_post)


def _tail_fwd_bwd(h2, p, tgt, ff, wple, wpg, g_ple, g_ffn_post):
    s, d = h2.shape

    def body(h2_ref, p_ref, t_ref, ff_ref, wple_ref, wpg_ref, gple_ref, gfp_ref,
             dh2_ref, dff_ref, dgl_ref, dpp_ref, h2b_ref, pb_ref, loss_ref, dgple_ref, dgfp_ref):
        i = pl.program_id(0)

        @pl.when(i == 0)
        def _():
            loss_ref[...] = jnp.zeros_like(loss_ref)
            dgple_ref[...] = jnp.zeros_like(dgple_ref)
            dgfp_ref[...] = jnp.zeros_like(dgfp_ref)

        h2 = h2_ref[...]
        h2b = h2.astype(BF16)
        h2b_ref[...] = h2b
        pb = p_ref[...].astype(BF16)
        pb_ref[...] = pb
        pp = _nt(pb, wple_ref[...])
        gple = gple_ref[...]
        e = pp * _rstd(pp) * gple
        wpg = wpg_ref[...].reshape(d, d)
        sg = jax.nn.sigmoid(_nn(h2b, wpg))
        diff = h2 + sg * e - t_ref[...]
        sq = jnp.sum(jnp.sum(diff * diff, axis=1, keepdims=True), axis=0, keepdims=True)
        loss_ref[...] += jnp.broadcast_to(sq * (0.5 / d), loss_ref.shape)
        dh3 = diff * (1.0 / d)
        dgl = (dh3 * e * sg * (1.0 - sg)).astype(BF16)
        dgl_ref[...] = dgl
        dh2 = dh3 + _nt(dgl, wpg)
        dh2_ref[...] = dh2
        dpp, dg = _rms_bwd(pp, gple, dh3 * sg)
        dpp_ref[...] = dpp.astype(BF16)
        dgple_ref[...] += dg
        dff, dg = _rms_bwd(ff_ref[...], gfp_ref[...], dh2)
        dff_ref[...] = dff.astype(BF16)
        dgfp_ref[...] += dg

    rowblk = pl.BlockSpec((TS, d), _row)
    vec = pl.BlockSpec((1, d), _fixed)
    return pl.pallas_call(
        body, grid=(s // TS,), name="tail_fwd_bwd",
        out_shape=(jax.ShapeDtypeStruct((s, d), F32), jax.ShapeDtypeStruct((s, d), BF16),
                   jax.ShapeDtypeStruct((s, d), BF16), jax.ShapeDtypeStruct((s, d), BF16),
                   jax.ShapeDtypeStruct((s, d), BF16), jax.ShapeDtypeStruct((s, D_PLE), BF16),
                   jax.ShapeDtypeStruct((8, LANES), F32), jax.ShapeDtypeStruct((1, d), F32),
                   jax.ShapeDtypeStruct((1, d), F32)),
        in_specs=[rowblk, pl.BlockSpec((TS, D_PLE), _row), rowblk, rowblk,
                  pl.BlockSpec(wple.shape, _fixed), _spec_square(1), vec, vec],
        out_specs=(rowblk, rowblk, rowblk, rowblk, rowblk, pl.BlockSpec((TS, D_PLE), _row),
                   pl.BlockSpec((8, LANES), _fixed), vec, vec),
        compiler_params=_params(1),
    )(h2, p, tgt, ff, wple, wpg, g_ple, g_ffn_post)


def _ffn_bwd(dff, gate, up, wd, wg, wu, h1, dh2, g_pre):
    s, d = h1.shape
    nc = D_FF // TN_FF
    ts = min(TS_FF, s)

    def body(dff_ref, gate_ref, up_ref, wd_ref, wg_ref, wu_ref, h1_ref, dh2_ref, g_ref,
             dgate_ref, dup_ref, dh1_ref, dg_ref, acc):
        i = pl.program_id(0)
        j = pl.program_id(1)

        @pl.when((i == 0) & (j == 0))
        def _():
            dg_ref[...] = jnp.zeros_like(dg_ref)

        @pl.when(j == 0)
        def _():
            acc[...] = jnp.zeros_like(acc)

        for r in range(2):
            rows = slice(r * (ts // 2), (r + 1) * (ts // 2))
            dact = _nt(dff_ref[rows, :], wd_ref[...].reshape(TN_FF, d))
            gt = gate_ref[rows, :].astype(F32)
            sg = jax.nn.sigmoid(gt)
            dup_ref[rows, :] = (dact * gt * sg).astype(BF16)
            dgate_ref[rows, :] = (dact * up_ref[rows, :].astype(F32) * (sg * (1.0 + gt * (1.0 - sg)))).astype(BF16)
            acc[rows, :] += (_nn(dgate_ref[rows, :], wg_ref[...].reshape(TN_FF, d))
                             + _nn(dup_ref[rows, :], wu_ref[...].reshape(TN_FF, d)))

        @pl.when(j == nc - 1)
        def _():
            dv, dg = _rms_bwd(h1_ref[...], g_ref[...], acc[...])
            dh1_ref[...] = dh2_ref[...] + dv
            dg_ref[...] += dg

    rowblk = pl.BlockSpec((ts, d), lambda i, j: (i, 0))
    chunk = pl.BlockSpec((ts, TN_FF), lambda i, j: (i, j))
    vec = pl.BlockSpec((1, d), lambda i, j: (0, 0))
    return pl.pallas_call(
        body, grid=(s // ts, nc), name="ffn_bwd",
        out_shape=(jax.ShapeDtypeStruct((s, D_FF), BF16), jax.ShapeDtypeStruct((s, D_FF), BF16),
                   jax.ShapeDtypeStruct((s, d), F32), jax.ShapeDtypeStruct((1, d), F32)),
        in_specs=[rowblk, chunk, chunk, _spec_ff(2), _spec_ff(0), _spec_ff(1), rowblk, rowblk, vec],
        out_specs=(chunk, chunk, rowblk, vec),
        scratch_shapes=[pltpu.VMEM((ts, d), F32)],
        compiler_params=_params(2),
    )(dff, gate, up, wd, wg, wu, h1, dh2, g_pre)


def _post_attn_bwd(dh1, o, a, mpre, wout, wpool, g_post, g_attn, g_pool, pscale, send):
    s, d = dh1.shape
    sub = TS // TQ
    npc = len(send)

    def body(dh1_ref, o_ref, a_ref, mp_ref, wo_ref, wp_ref, gpost_ref, ga_ref, gp_ref, ps_ref, *refs):
        send_refs, refs = refs[:npc], refs[npc:]
        dob_ref, dat_ref, dlt_ref, dmpb_ref, dy_ref, dgpost_ref, dga_ref, dgp_ref, dps_ref = refs[:9]
        got_refs, (send_sems, recv_sems) = refs[9:9 + npc], refs[9 + npc:]
        i = pl.program_id(0)

        @pl.when(i == 0)
        def _():
            for cp in _pair_copies(send_refs, got_refs, send_sems, recv_sems):
                cp.start()
            dgpost_ref[...] = jnp.zeros_like(dgpost_ref)
            dga_ref[...] = jnp.zeros_like(dga_ref)
            dgp_ref[...] = jnp.zeros_like(dgp_ref)
            dps_ref[...] = jnp.zeros_like(dps_ref)

        do, dg = _rms_bwd(o_ref[...], gpost_ref[...], dh1_ref[...])
        dgpost_ref[...] += dg
        dob = do.astype(BF16)
        dob_ref[...] = dob
        dmix = _nt(dob, wo_ref[...].reshape(d, d))

        av = a_ref[...]
        da, dg = _rms_bwd(av, ga_ref[...], dmix[:, 0:D_ATTN])
        dga_ref[...] += dg
        dat = da.astype(BF16).T
        hsel = (lax.shift_right_logical(lax.broadcasted_iota(jnp.int32, (HEADS, D_ATTN), 1), 6)
                == lax.broadcasted_iota(jnp.int32, (HEADS, D_ATTN), 0)).astype(F32)
        dlt = lax.dot_general(hsel, da * av, (((1,), (1,)), ((), ())), precision=HIGHEST, preferred_element_type=F32)
        for q in range(sub):
            dlt_ref[q] = dlt[:, q * TQ:(q + 1) * TQ]
            dat_ref[q] = dat[:, q * TQ:(q + 1) * TQ]

        ps = ps_ref[...]
        mp = mp_ref[...]
        dm, dg = _rms_bwd(mp * ps, gp_ref[...], dmix[:, D_ATTN:])
        dgp_ref[...] += dg
        dps_ref[...] += jnp.sum(dm * mp, axis=0, keepdims=True)
        dmpb = (dm * ps).astype(BF16)
        dmpb_ref[...] = dmpb
        for g in range(len(POOL_WINDOWS)):
            cols = slice(g * POOL_CH, (g + 1) * POOL_CH)
            dy_ref[:, cols] = _nt(dmpb[:, cols], wp_ref[g])

        @pl.when(i == s // TS - 1)
        def _():
            for cp in _pair_copies(send_refs, got_refs, send_sems, recv_sems):
                cp.wait()

    rowblk = pl.BlockSpec((TS, d), _row)
    half = pl.BlockSpec((TS, D_ATTN), _row)
    vec = lambda n: pl.BlockSpec((1, n), _fixed)
    nk = N_DEV // 2
    res = pl.pallas_call(
        body, grid=(s // TS,), name="post_attn_bwd",
        out_shape=(jax.ShapeDtypeStruct((s, d), BF16), jax.ShapeDtypeStruct((s // TQ, D_ATTN, TQ), BF16),
                   jax.ShapeDtypeStruct((s // TQ, HEADS, TQ), F32), jax.ShapeDtypeStruct((s, D_POOL), BF16),
                   jax.ShapeDtypeStruct((s, D_POOL), F32), jax.ShapeDtypeStruct((1, d), F32),
                   jax.ShapeDtypeStruct((1, D_ATTN), F32), jax.ShapeDtypeStruct((1, D_POOL), F32),
                   jax.ShapeDtypeStruct((1, D_POOL), F32))
        + tuple(jax.ShapeDtypeStruct((nk,) + t.shape[1:], t.dtype) for t in send),
        in_specs=[rowblk, rowblk, half, half, _spec_square(0),
                  pl.BlockSpec(wpool.shape, lambda i: (0, 0, 0)), vec(d), vec(D_ATTN), vec(D_POOL), vec(D_POOL)]
        + [ANY] * npc,
        out_specs=(rowblk, pl.BlockSpec((sub, D_ATTN, TQ), lambda i: (i, 0, 0)),
                   pl.BlockSpec((sub, HEADS, TQ), lambda i: (i, 0, 0)), half, half,
                   vec(d), vec(D_ATTN), vec(D_POOL), vec(D_POOL)) + (ANY,) * npc,
        scratch_shapes=[pltpu.SemaphoreType.DMA((nk, npc)), pltpu.SemaphoreType.DMA((nk, npc))],
        compiler_params=_params(1),
    )(dh1, o, a, mpre, wout, wpool, g_post, g_attn, g_pool, pscale, *send)
    return res[:9], list(res[9:])


def _attn_bwd(ka, v, kt3, qat3, qt3, dot3, lset3, dlt3, chip_blocks, small_block):
    s = ka.shape[0]
    nq = s // TQ

    def body(ka_ref, v_ref, kt_ref, qat_ref, qt_ref, dot_ref, lset_ref, dlt_ref, b_ref, sm_ref,
             dqt_ref, dkt_ref, dvt_ref, got_ref, all_ref, pt_scr, ptb_scr, dsb_scr,
             stage, send_sems, recv_sems, local_sem, stage_s, send_s, recv_s, local_s):
        j = pl.program_id(0)

        @pl.when(j == 0)
        def _():
            _chips_start(b_ref, got_ref, stage, send_sems, recv_sems, local_sem)
            _gather_start(sm_ref, all_ref, stage_s, send_s, recv_s, local_s)
            dqt_ref[...] = jnp.zeros_like(dqt_ref)

        @pl.when(j == max(nq - 2, 0))
        def _():
            _gather_pass_on(all_ref, send_s, recv_s)

        def tile(i, masked):
            def accumulate(ref, idx, val):
                if masked:
                    ref[idx] = val
                else:
                    ref[idx] += val

            for h in range(HEADS):
                aug = slice(h * AUG, (h + 1) * AUG)
                st = _nn(ka_ref[:, aug], qat_ref[i, aug, :]) - lset_ref[i, h:h + 1, :]
                if masked:
                    st = jnp.where(_causal_in_tile(), st, NEG)
                pt = jnp.exp2(st)
                pt_scr[h] = pt
                ptb_scr[h] = pt.astype(BF16)
            heads = [(h, slice(h * HEAD_DIM, (h + 1) * HEAD_DIM)) for h in range(HEADS)]
            for h, hs in heads:
                dst = pt_scr[h] * (_nn(v_ref[:, hs], dot_ref[i, hs, :]) - dlt_ref[i, h:h + 1, :])
                dsb_scr[h] = dst.astype(BF16)
            for h, hs in heads:
                accumulate(dvt_ref, (0, hs, slice(None)), _nt(dot_ref[i, hs, :], ptb_scr[h]))
            for h, hs in heads:
                rows = slice(h * VROWS, (h + 1) * VROWS)
                accumulate(dkt_ref, (0, rows, slice(None)), _nt(qt_ref[i, rows, :], dsb_scr[h]))
            for h, hs in heads:
                rows = slice(h * VROWS, (h + 1) * VROWS)
                dqt_ref[i, rows, :] += _nn(kt_ref[0, rows, :], dsb_scr[h])

        first = j + 1
        pairs = (nq - first) // 2

        def step(p, carry):
            tile(first + 2 * p, False)
            tile(first + 2 * p + 1, False)
            return carry

        tile(j, True)
        lax.fori_loop(0, pairs, step, 0)

        @pl.when(first + 2 * pairs < nq)
        def _():
            tile(nq - 1, False)

        @pl.when(j == nq - 1)
        def _():
            _chips_finish(b_ref, got_ref, send_sems, recv_sems)
            _gather_finish(sm_ref, all_ref, send_s, recv_s)

    blk = pl.BlockSpec((TQ, D_ATTN), _row)
    tile_t = lambda rows: pl.BlockSpec((1, rows, TQ), lambda j: (j, 0, 0))
    per_tile = lambda rows: jax.ShapeDtypeStruct((nq, rows, TQ), F32)
    _, r, cdim = chip_blocks.shape
    dma = pltpu.SemaphoreType.DMA
    return pl.pallas_call(
        body, grid=(nq,), name="attn_bwd",
        out_shape=(per_tile(HEADS * VROWS), per_tile(HEADS * VROWS), per_tile(D_ATTN),
                   jax.ShapeDtypeStruct(chip_blocks.shape, chip_blocks.dtype),
                   jax.ShapeDtypeStruct((N_DEV,) + small_block.shape, small_block.dtype)),
        in_specs=[pl.BlockSpec((TQ, HEADS * AUG), _row), blk, tile_t(HEADS * VROWS),
                  VMEM_WHOLE, VMEM_WHOLE, VMEM_WHOLE, VMEM_WHOLE, VMEM_WHOLE, ANY, ANY],
        out_specs=(pl.BlockSpec((nq, HEADS * VROWS, TQ), lambda j: (0, 0, 0)), tile_t(HEADS * VROWS), tile_t(D_ATTN),
                   ANY, ANY),
        scratch_shapes=[pltpu.VMEM((HEADS, TQ, TQ), F32), pltpu.VMEM((HEADS, TQ, TQ), BF16),
                        pltpu.VMEM((HEADS, TQ, TQ), BF16), pltpu.VMEM((r, cdim), chip_blocks.dtype),
                        dma((3,)), dma((3,)), dma,
                        pltpu.VMEM(small_block.shape, small_block.dtype), dma((7,)), dma((7,)), dma],
        compiler_params=_params(1),
    )(ka, v, kt3, qat3, qt3, dot3, lset3, dlt3, chip_blocks, small_block)


def _pre_attn_bwd(dqt3, dkt3, dvt3, fl, dy, x, dh1, g1, wqkv, wf, wu):
    s, d = x.shape
    nt = s // TS
    n = TS + HALO
    sub = TS // TQ
    qkv, fcols = 3 * D_ATTN, 3 * D_ATTN + LANES

    def body(dqt_ref, dkt_ref, dvt_ref, fl_ref, dy_ref, x_ref, dh1_ref, g_ref, wqkv_ref, wf_ref, wu_ref,
             gx_ref, dz_ref, dg_ref, db_ref, ybuf, ccar, dlog, dsum):
        dqkv_ref = dz_ref.at[:, 0:qkv]
        dfb_ref = dz_ref.at[:, qkv:fcols]
        dub_ref = dz_ref.at[:, fcols:]
        i = pl.program_id(0)
        ti = nt - 1 - i

        @pl.when(i == 0)
        def _():
            ybuf[TS:n, :] = jnp.zeros((HALO, D_POOL), F32)
            ccar[...] = jnp.zeros_like(ccar)
            dg_ref[...] = jnp.zeros_like(dg_ref)
            db_ref[...] = jnp.zeros_like(db_ref)
            dsum[...] = jnp.zeros_like(dsum)

        for a in range(sub):
            for h in range(HEADS):
                r = h * VROWS + HEAD_DIM
                dsum[h:h + 1, a * TQ:(a + 1) * TQ] = dqt_ref[a, r:r + 1, :] - dkt_ref[a, r:r + 1, :]
        rr = lax.broadcasted_iota(jnp.int32, (TS, TS), 0)
        cc = lax.broadcasted_iota(jnp.int32, (TS, TS), 1)
        dlog[...] = ccar[...] + _mask_matmul((cc >= rr).astype(BF16), dsum[...].T)
        ccar[...] = dlog[0:1, :]
        df = dlog[...] * jax.nn.sigmoid(-fl_ref[...])
        db_ref[...] += jnp.sum(df, axis=0, keepdims=True)
        dfb = df.astype(BF16)
        dfb_ref[...] = dfb

        t = ti * TS + lax.broadcasted_iota(jnp.int32, (TS, 1), 0)
        dy = dy_ref[...]
        for g, w in enumerate(POOL_WINDOWS):
            cols = slice(g * POOL_CH, (g + 1) * POOL_CH)
            ybuf[0:TS, cols] = dy[:, cols] / jnp.minimum(t + 1, w).astype(F32)
        for g, w in enumerate(POOL_WINDOWS):
            cols = slice(g * POOL_CH, (g + 1) * POOL_CH)
            sm = ybuf[:, cols]
            step = 1
            while step < w:
                sm = sm + pltpu.roll(sm, n - step, 0)
                step *= 2
            dub_ref[:, cols] = (sm[0:TS, :] - dy[:, cols]).astype(BF16)
        ybuf[TS:n, :] = ybuf[0:HALO, :]

        for a in range(sub):
            rows = slice(a * TQ, (a + 1) * TQ)
            for h in range(HEADS):
                src = slice(h * VROWS, h * VROWS + HEAD_DIM)
                dqkv_ref[rows, h * HEAD_DIM:(h + 1) * HEAD_DIM] = (dqt_ref[a, src, :].T * 0.125).astype(BF16)
                dqkv_ref[rows, D_ATTN + h * HEAD_DIM:D_ATTN + (h + 1) * HEAD_DIM] = dkt_ref[a, src, :].T.astype(BF16)
            dqkv_ref[rows, 2 * D_ATTN:] = dvt_ref[a].T.astype(BF16)
        dhn = _nn(dqkv_ref[...], wqkv_ref[...]) + _nn(dfb, wf_ref[...]) + _nn(dub_ref[...], wu_ref[...])
        dx, dg = _rms_bwd(x_ref[...], g_ref[...], dhn)
        gx_ref[...] = dh1_ref[...] + dx
        dg_ref[...] += dg

    rev = lambda i: (nt - 1 - i, 0)
    blk = lambda w: pl.BlockSpec((TS, w), rev)
    return pl.pallas_call(
        body, grid=(nt,), name="pre_attn_bwd",
        out_shape=(jax.ShapeDtypeStruct((s, d), F32), jax.ShapeDtypeStruct((s, fcols + D_POOL), BF16),
                   jax.ShapeDtypeStruct((1, d), F32), jax.ShapeDtypeStruct((1, LANES), F32)),
        in_specs=[pl.BlockSpec((sub, HEADS * VROWS, TQ), lambda i: (nt - 1 - i, 0, 0)),
                  pl.BlockSpec((sub, HEADS * VROWS, TQ), lambda i: (nt - 1 - i, 0, 0)),
                  pl.BlockSpec((sub, D_ATTN, TQ), lambda i: (nt - 1 - i, 0, 0)),
                  blk(LANES), blk(D_POOL), blk(d), blk(d),
                  pl.BlockSpec((1, d), _fixed), pl.BlockSpec((qkv, d), _fixed), pl.BlockSpec(wf.shape, _fixed),
                  pl.BlockSpec(wu.shape, _fixed)],
        out_specs=(blk(d), blk(fcols + D_POOL), pl.BlockSpec((1, d), _fixed), pl.BlockSpec((1, LANES), _fixed)),
        scratch_shapes=[pltpu.VMEM((n, D_POOL), F32), pltpu.VMEM((1, LANES), F32), pltpu.VMEM((TS, LANES), F32),
                        pltpu.VMEM((LANES, TS), F32)],
        compiler_params=_params(1),
    )(dqt3, dkt3, dvt3, fl, dy, x, dh1, g1, wqkv, wf, wu)


def _wgrad(a, b, out_dtype, name):
    s, m = a.shape
    n = b.shape[1]
    tm = max(t for t in range(LANES, min(m, TM_WGRAD) + 1, LANES) if m % t == 0)
    ts = min(TS_WGRAD, s)
    ns = s // ts

    def body(a_ref, b_ref, o_ref, acc):
        i = pl.program_id(1)

        @pl.when(i == 0)
        def _():
            acc[...] = jnp.zeros_like(acc)

        acc[...] += _tn(a_ref[...], b_ref[...])

        @pl.when(i == ns - 1)
        def _():
            o_ref[...] = acc[...].astype(out_dtype)

    return pl.pallas_call(
        body, grid=(m // tm, ns), name=name, out_shape=jax.ShapeDtypeStruct((m, n), out_dtype),
        in_specs=[pl.BlockSpec((ts, tm), lambda j, i: (i, j)), pl.BlockSpec((ts, n), lambda j, i: (i, 0))],
        out_specs=pl.BlockSpec((tm, n), lambda j, i: (j, 0)),
        scratch_shapes=[pltpu.VMEM((tm, n), F32)],
        compiler_params=_params(2),
    )(a, b)


def _wgrad_in(dz, hn):
    s, m = dz.shape
    n = hn.shape[1]
    ts = min(TS_WGRAD, s)
    ns = s // ts
    pad_at, pad = 3 * D_ATTN + HEADS, LANES - HEADS
    assert m == D_IN + pad and N_DEV * SHARD_IN == D_IN

    def pieces(d):
        lo, hi = d * SHARD_IN, (d + 1) * SHARD_IN
        spans = [(lo, min(hi, pad_at), 0), (max(lo, pad_at), hi, pad)]
        return [(a + shift, b - a, a - lo) for a, b, shift in spans if b > a]

    def body(a_ref, b_ref, o_ref, acc, stage):
        i = pl.program_id(0)

        @pl.when(i == 0)
        def _():
            acc[...] = jnp.zeros_like(acc)

        acc[...] += _tn(a_ref[...], b_ref[...])

        @pl.when(i == ns - 1)
        def _():
            stage[SHARD_IN:ROWS_IN, :] = jnp.zeros((ROWS_IN - SHARD_IN, n), F32)
            for d in range(N_DEV):
                for src, rows, dst in pieces(d):
                    stage[dst:dst + rows, :] = acc[src:src + rows, :]
                o_ref[d] = stage[...].astype(BF16)

    return pl.pallas_call(
        body, grid=(ns,), name="wgrad_in", out_shape=jax.ShapeDtypeStruct((N_DEV, ROWS_IN, n), BF16),
        in_specs=[pl.BlockSpec((ts, m), _row), pl.BlockSpec((ts, n), _row)],
        out_specs=pl.BlockSpec((N_DEV, ROWS_IN, n), lambda i: (0, 0, 0)),
        scratch_shapes=[pltpu.VMEM((m, n), F32), pltpu.VMEM((ROWS_IN, n), F32)],
        compiler_params=_params(1),
    )(dz, hn)


def _adamw(w, g, m, v):
    m = ADAM_B1 * m + (1.0 - ADAM_B1) * g
    v = ADAM_B2 * v + (1.0 - ADAM_B2) * (g * g)
    m_hat = m / (1.0 - ADAM_B1 ** ADAM_STEP)
    v_hat = v / (1.0 - ADAM_B2 ** ADAM_STEP)
    delta = -ADAM_LR * (m_hat / (jnp.sqrt(v_hat) + ADAM_EPS) + ADAM_WD * w)
    return delta, m, v


def _sum_update(p_ref, w_ref, m_ref, v_ref, g_ref, d_ref, nm_ref, nv_ref):
    g = p_ref[0].astype(F32)
    for k in range(1, p_ref.shape[0]):
        g = g + p_ref[k].astype(F32)
    g_ref[...] = g
    d_ref[...], nm_ref[...], nv_ref[...] = _adamw(w_ref[...], g, m_ref[...], v_ref[...])


def _reduce_update_rest(parts, w, m, v, chip_blocks, small_block):
    nk, r, c = parts.shape
    ns = r // TR_REST

    def body(p_ref, w_ref, m_ref, v_ref, b_ref, sm_ref, g_ref, d_ref, nm_ref, nv_ref, got_ref, all_ref,
             stage_b, stage_s, send_b, recv_b, local_b, send_s, recv_s, local_s):
        i = pl.program_id(0)

        @pl.when(i == 0)
        def _():
            _chips_start(b_ref, got_ref, stage_b, send_b, recv_b, local_b)
            _gather_start(sm_ref, all_ref, stage_s, send_s, recv_s, local_s)

        _sum_update(p_ref, w_ref, m_ref, v_ref, g_ref, d_ref, nm_ref, nv_ref)

        @pl.when(i == ns - 1)
        def _():
            _gather_pass_on(all_ref, send_s, recv_s)
            _chips_finish(b_ref, got_ref, send_b, recv_b)
            _gather_finish(sm_ref, all_ref, send_s, recv_s)

    blk = pl.BlockSpec((TR_REST, c), _row)
    out = jax.ShapeDtypeStruct((r, c), F32)
    dma = pltpu.SemaphoreType.DMA
    return pl.pallas_call(
        body, grid=(ns,), name="reduce_update_rest",
        out_shape=(out,) * 4 + (jax.ShapeDtypeStruct(chip_blocks.shape, chip_blocks.dtype),
                                jax.ShapeDtypeStruct((N_DEV,) + small_block.shape, small_block.dtype)),
        in_specs=[pl.BlockSpec((nk, TR_REST, c), lambda i: (0, i, 0)), blk, blk, blk, ANY, ANY],
        out_specs=(blk,) * 4 + (ANY, ANY),
        scratch_shapes=[pltpu.VMEM(chip_blocks.shape[1:], chip_blocks.dtype), pltpu.VMEM(small_block.shape, small_block.dtype),
                        dma((3,)), dma((3,)), dma, dma((7,)), dma((7,)), dma],
        compiler_params=_params(1),
    )(parts, w, m, v, chip_blocks, small_block)


def _reduce_update_big(parts, w, m, v, tr, name):
    nk, r, c = parts.shape

    def body(p_ref, w_ref, m_ref, v_ref, g_ref, d_ref, nm_ref, nv_ref):
        _sum_update(p_ref, w_ref, m_ref, v_ref, g_ref, d_ref, nm_ref, nv_ref)

    blk = pl.BlockSpec((tr, c), _row)
    out = jax.ShapeDtypeStruct((r, c), F32)
    return pl.pallas_call(
        body, grid=(r // tr,), name=name, out_shape=(out,) * 4,
        in_specs=[pl.BlockSpec((nk, tr, c), lambda i: (0, i, 0)), blk, blk, blk],
        out_specs=(blk,) * 4, compiler_params=_params(1),
    )(parts, w, m, v)


def _reduce_update_small(parts, late, w, m, v):
    nd = parts.shape[0]
    first = parts.shape[1] - late.shape[1]

    def body(p_ref, q_ref, w_ref, m_ref, v_ref, g_ref, d_ref, nm_ref, nv_ref):
        g, t = p_ref[0], q_ref[0]
        for k in range(1, nd):
            g, t = g + p_ref[k], t + q_ref[k]
        g_ref[...] = g
        g_ref[first:, :] = g[first:, :] + t
        d_ref[...], nm_ref[...], nv_ref[...] = _adamw(w_ref[...], g_ref[...], m_ref[...], v_ref[...])

    out = jax.ShapeDtypeStruct(w.shape, F32)
    return pl.pallas_call(body, name="reduce_update_small", out_shape=(out,) * 4,
                          compiler_params=pltpu.CompilerParams(vmem_limit_bytes=VMEM_LIMIT))(parts, late, w, m, v)


MESH = pl.DeviceIdType.MESH


def _copy_through_vmem(src_hbm, dst_hbm, stage, sem):
    load = pltpu.make_async_copy(src_hbm, stage, sem)
    load.start()
    load.wait()
    store = pltpu.make_async_copy(stage, dst_hbm, sem)
    store.start()
    store.wait()


class _GatherPlan:
    def __init__(self, x_ref, out_ref, send_sems, recv_sems):
        x, y, c = lax.axis_index("x"), lax.axis_index("y"), lax.axis_index("c")
        self.me, self.sibling, self.c = (x, y, c), (x, y, 1 - c), c
        self.chips = [(1 - x, y), (x, 1 - y), (1 - x, 1 - y)]
        self.x_ref, self.out_ref, self.send_sems, self.recv_sems = x_ref, out_ref, send_sems, recv_sems

    def slot(self, px, py, pc):
        return self.out_ref.at[4 * px + 2 * py + pc]

    def copy(self, k, block, to, src=None):
        return pltpu.make_async_remote_copy(
            src_ref=self.slot(*block) if src is None else src, dst_ref=self.slot(*block),
            send_sem=self.send_sems.at[k], recv_sem=self.recv_sems.at[k], device_id=to, device_id_type=MESH)

    def first(self):
        return [self.copy(0, self.me, self.sibling, src=self.x_ref)] + [
            self.copy(1 + j, self.me, (*chip, self.c), src=self.x_ref) for j, chip in enumerate(self.chips)]

    def passed(self):
        return [self.copy(4 + j, (*chip, self.c), self.sibling) for j, chip in enumerate(self.chips)]


def _gather_start(x_ref, out_ref, stage, send_sems, recv_sems, local_sem):
    plan = _GatherPlan(x_ref, out_ref, send_sems, recv_sems)
    for cp in plan.first():
        cp.start()
    _copy_through_vmem(x_ref, plan.slot(*plan.me), stage, local_sem)


def _gather_pass_on(out_ref, send_sems, recv_sems):
    plan = _GatherPlan(None, out_ref, send_sems, recv_sems)
    passed = plan.passed()
    for j, chip in enumerate(plan.chips):
        plan.copy(1 + j, (*chip, plan.c), plan.me).wait_recv()
        passed[j].start()


def _gather_finish(x_ref, out_ref, send_sems, recv_sems):
    plan = _GatherPlan(x_ref, out_ref, send_sems, recv_sems)
    plan.copy(0, plan.sibling, plan.me).wait_recv()
    for j, chip in enumerate(plan.chips):
        plan.copy(4 + j, (*chip, 1 - plan.c), plan.me).wait_recv()
    for cp in plan.first() + plan.passed():
        cp.wait_send()


def _all_gather(xs, name):
    r, cdim = xs.shape

    def body(x_ref, out_ref, stage, send_sems, recv_sems, local_sem):
        _gather_start(x_ref, out_ref, stage, send_sems, recv_sems, local_sem)
        _gather_pass_on(out_ref, send_sems, recv_sems)
        _gather_finish(x_ref, out_ref, send_sems, recv_sems)

    return pl.pallas_call(
        body, name=name, out_shape=jax.ShapeDtypeStruct((N_DEV, r, cdim), xs.dtype),
        in_specs=[ANY], out_specs=ANY,
        scratch_shapes=[pltpu.VMEM((r, cdim), xs.dtype), pltpu.SemaphoreType.DMA((7,)), pltpu.SemaphoreType.DMA((7,)),
                        pltpu.SemaphoreType.DMA],
        compiler_params=pltpu.CompilerParams(vmem_limit_bytes=VMEM_LIMIT),
    )(xs)


def _pair_copies(src_refs, dst_refs, send_sems, recv_sems):
    x, y, c = lax.axis_index("x"), lax.axis_index("y"), lax.axis_index("c")
    return [pltpu.make_async_remote_copy(
        src_ref=src.at[2 * k + (1 - c)], dst_ref=dst.at[k], send_sem=send_sems.at[k, p], recv_sem=recv_sems.at[k, p],
        device_id=(x, y, 1 - c), device_id_type=MESH)
        for k in range(N_DEV // 2) for p, (src, dst) in enumerate(zip(src_refs, dst_refs))]


def _rs_pair_sum(core, pieces, offsets, rows, name, landed=()):
    cdim = pieces[0].shape[2]
    nk = N_DEV // 2
    npc = len(pieces)
    nrem = npc - len(landed)
    spans = [(o, t.shape[1]) for t, o in zip(pieces, offsets)]
    ends = [o + n for o, n in spans]
    gaps = [(a, b - a) for a, b in zip(ends, [o for o, _ in spans[1:]] + [rows]) if b > a]

    def body(core_ref, *refs):
        own, src, got, o_ref = refs[:npc], refs[npc:npc + nrem], refs[npc + nrem:2 * npc], refs[2 * npc]
        landing, send_sems, recv_sems = refs[2 * npc + 1:]
        k = pl.program_id(0)
        x, y, c = lax.axis_index("x"), lax.axis_index("y"), lax.axis_index("c")

        def copies(kk):
            return [pltpu.make_async_remote_copy(
                src_ref=src[p].at[2 * kk + (1 - c)], dst_ref=landing.at[kk, pl.ds(o, n)],
                send_sem=send_sems.at[kk, p], recv_sem=recv_sems.at[kk, p], device_id=(x, y, 1 - c),
                device_id_type=MESH) for p, (o, n) in enumerate(spans[:nrem])]

        @pl.when(k == 0)
        def _():
            for kk in range(nk):
                for cp in copies(kk):
                    cp.start()

        for cp, piece, (o, n) in zip(copies(k), own, spans):
            cp.wait_recv()
            o_ref[0, o:o + n, :] = (piece[0].astype(F32) + landing[k, o:o + n, :].astype(F32)).astype(BF16)
        for theirs, piece, (o, n) in zip(got, own[nrem:], spans[nrem:]):
            o_ref[0, o:o + n, :] = (piece[0].astype(F32) + theirs[0].astype(F32)).astype(BF16)
        for o, n in gaps:
            o_ref[0, o:o + n, :] = jnp.zeros((n, cdim), BF16)

        @pl.when(k == nk - 1)
        def _():
            for kk in range(nk):
                for cp in copies(kk):
                    cp.wait_send()

    own_specs = [pl.BlockSpec((1, n, cdim), lambda k, core_ref: (2 * k + core_ref[0], 0, 0)) for _, n in spans]
    got_specs = [pl.BlockSpec((1, n, cdim), lambda k, core_ref: (k, 0, 0)) for _, n in spans[nrem:]]
    land_rows = max(o + n for o, n in spans[:nrem])
    return pl.pallas_call(
        body, name=name, out_shape=jax.ShapeDtypeStruct((nk, rows, cdim), BF16),
        grid_spec=pltpu.PrefetchScalarGridSpec(
            num_scalar_prefetch=1, grid=(nk,),
            in_specs=own_specs + [ANY] * nrem + got_specs,
            out_specs=pl.BlockSpec((1, rows, cdim), lambda k, core_ref: (k, 0, 0)),
            scratch_shapes=[pltpu.VMEM((nk, land_rows, cdim), BF16), pltpu.SemaphoreType.DMA((nk, nrem)),
                            pltpu.SemaphoreType.DMA((nk, nrem))]),
        compiler_params=_params(1),
    )(core, *pieces, *pieces[:nrem], *landed)


def _chips_start(b_ref, out_ref, stage, send_sems, recv_sems, local_sem):
    x, y, c = lax.axis_index("x"), lax.axis_index("y"), lax.axis_index("c")
    mychip = 2 * x + y
    for j, (px, py) in enumerate([(1 - x, y), (x, 1 - y), (1 - x, 1 - y)]):
        pltpu.make_async_remote_copy(
            src_ref=b_ref.at[2 * px + py], dst_ref=out_ref.at[mychip],
            send_sem=send_sems.at[j], recv_sem=recv_sems.at[j], device_id=(px, py, c), device_id_type=MESH).start()
    _copy_through_vmem(b_ref.at[mychip], out_ref.at[mychip], stage, local_sem)


def _chips_finish(b_ref, out_ref, send_sems, recv_sems):
    x, y, c = lax.axis_index("x"), lax.axis_index("y"), lax.axis_index("c")
    for j, (px, py) in enumerate([(1 - x, y), (x, 1 - y), (1 - x, 1 - y)]):
        pltpu.make_async_remote_copy(
            src_ref=b_ref.at[2 * px + py], dst_ref=out_ref.at[2 * px + py],
            send_sem=send_sems.at[j], recv_sem=recv_sems.at[j], device_id=(px, py, c), device_id_type=MESH).wait()


def _pad_rows(a, rows):
    return jnp.pad(a, ((0, rows - a.shape[0]), (0, 0)))


def _pack_in(w_in):
    return _pad_rows(w_in[0].T, ROWS_IN)


def _unpack_in(r):
    return r[0:SHARD_IN].T[None]


def _pack_rest(w_out, w_gate, w_up, w_down, w_ple, w_pg):
    head = _pad_rows(jnp.concatenate([w_out[0], w_pg[0], w_ple[0].T.reshape(32, D_MODEL)], axis=0), OFF_GATE)
    return jnp.concatenate([head, w_gate[0].T, w_up[0].T, w_down[0]], axis=0)


def _unpack_rest(r):
    return (r[0:OFF_PG][None], r[OFF_GATE:OFF_UP].T[None], r[OFF_UP:OFF_DOWN].T[None], r[OFF_DOWN:ROWS_REST][None],
            r[OFF_PLE:OFF_PLE + 32].reshape(128, D_PLE).T[None], r[OFF_PG:OFF_PLE][None])


def _pack_small(w_pool, g_mix_pre, g_mix_post, g_ffn_pre, g_ffn_post, g_ple, g_attn, g_pool, pool_scale, b_forget,
                loss=None):
    row = lambda vrow: vrow.reshape(1, -1)
    misc = [row(pool_scale), row(b_forget), row(loss) if loss is not None else jnp.zeros((1, 1), F32),
            jnp.zeros((1, D_MODEL - COL_LOSS - 1), F32)]
    rows = [w_pool.reshape(64, D_MODEL), row(g_mix_pre), row(g_mix_post), row(g_ffn_pre), row(g_ffn_post), row(g_ple),
            jnp.concatenate([row(g_attn), row(g_pool)], axis=1), jnp.concatenate(misc, axis=1),
            jnp.zeros((SMALL_ROWS - ROW_MISC - 1, D_MODEL), F32)]
    return jnp.concatenate(rows, axis=0)


def _pack_small_late(g_mix_pre, b_forget):
    misc = [jnp.zeros((1, COL_B_FORGET), F32), b_forget.reshape(1, -1), jnp.zeros((1, D_MODEL - COL_LOSS), F32)]
    return jnp.concatenate([g_mix_pre.reshape(1, -1), jnp.zeros((ROW_MISC - ROW_G_MIX_PRE - 1, D_MODEL), F32),
                            jnp.concatenate(misc, axis=1), jnp.zeros((SMALL_ROWS - ROW_MISC - 1, D_MODEL), F32)], axis=0)


def _unpack_small(r):
    gains, misc = r[ROW_GROUP_GAINS:ROW_GROUP_GAINS + 1], r[ROW_MISC:ROW_MISC + 1]
    return dict(
        w_pool=r[0:64].reshape(1, 4, POOL_CH, POOL_CH), g_mix_pre=r[ROW_G_MIX_PRE:ROW_G_MIX_PRE + 1],
        g_mix_post=r[ROW_G_MIX_POST:ROW_G_MIX_POST + 1], g_ffn_pre=r[ROW_G_FFN_PRE:ROW_G_FFN_PRE + 1],
        g_ffn_post=r[ROW_G_FFN_POST:ROW_G_FFN_POST + 1], g_ple=r[ROW_G_PLE:ROW_G_PLE + 1],
        g_attn_grp=gains[:, 0:D_ATTN], g_pool_grp=gains[:, D_ATTN:D_ATTN + D_POOL],
        pool_scale=misc[:, 0:D_POOL], b_forget=misc[:, COL_B_FORGET:COL_B_FORGET + HEADS])


def _step(x, p, tgt, small, in_w, in_m, in_v, rest_w, rest_m, rest_v):
    core = lax.axis_index("c").astype(jnp.int32).reshape(1)
    win_t = _all_gather(in_w.astype(BF16), "gather_w_in")[:, 0:SHARD_IN].reshape(D_IN, D_MODEL)
    wqkv = win_t
    wf = _pad_rows(win_t[3 * D_ATTN:3 * D_ATTN + HEADS], LANES)
    wu = win_t[3 * D_ATTN + HEADS:]
    wpool = small["w_pool"].astype(BF16)
    bpad = jnp.pad(small["b_forget"], ((0, 0), (0, LANES - HEADS)))

    lay = _attn_layout_constants()
    rest_b = rest_w.astype(BF16)
    hn, qt3, ka, v, qat3, vt3, kt3, fl, y, mpre, gh = _pre_attn_fwd(x, small["g_mix_pre"], wqkv, wf, wu, bpad, wpool, lay,
                                                                 rest_b[0:OFF_GATE])
    a, lset3, gf = _attn_fwd(ka, qat3, vt3, rest_b[OFF_GATE:])
    wple_t = gh[:, OFF_PLE:OFF_PLE + 32].reshape(D_MODEL, D_PLE)
    mix, o, h1, hn2 = _post_attn_fwd(a, mpre, x, small["g_attn_grp"], small["g_pool_grp"], small["pool_scale"], gh,
                                     small["g_mix_post"], small["g_ffn_pre"])
    gate, up, act, ff, h2 = _ffn_fwd(hn2, gf, gf, gf, h1, small["g_ffn_post"])
    dh2, dff, dgl, dpp, h2b, pb, loss8, dg_ple, dg_ffn_post = _tail_fwd_bwd(
        h2, p, tgt, ff, wple_t, gh, small["g_ple"], small["g_ffn_post"])
    dgate, dup, dh1, dg_ffn_pre = _ffn_bwd(dff, gate, up, gf, gf, gf, h1, dh2, small["g_ffn_pre"])
    nd = N_DEV
    send_rest = [
        _wgrad(h2b, dgl, BF16, "wgrad_ple_gate").reshape(nd, 128, D_MODEL),
        _wgrad(dpp, pb, BF16, "wgrad_ple").reshape(nd, 32, D_MODEL),
        _wgrad(dgate, hn2, BF16, "wgrad_gate").reshape(nd, SHARD_FF, D_MODEL),
        _wgrad(dup, hn2, BF16, "wgrad_up").reshape(nd, SHARD_FF, D_MODEL),
        _wgrad(act, dff, BF16, "wgrad_down").reshape(nd, SHARD_FF, D_MODEL)]
    (dob, dat3, dlt3, dmpb, dy, dg_mix_post, dg_attn, dg_pool, dps), landed = _post_attn_bwd(
        dh1, o, a, mpre, gh, wpool, small["g_mix_post"], small["g_attn_grp"], small["g_pool_grp"], small["pool_scale"],
        send_rest)
    send_rest = [_wgrad(mix, dob, BF16, "wgrad_out").reshape(nd, 128, D_MODEL)] + send_rest
    pair_rest = _rs_pair_sum(core, send_rest, [0, OFF_PG, OFF_PLE, OFF_GATE, OFF_UP, OFF_DOWN], ROWS_REST,
                             "rs_pair_sum_rest", landed)

    dwp = _wgrad(y, dmpb, F32, "wgrad_pool")
    dw_pool = jnp.stack([dwp[g * POOL_CH:(g + 1) * POOL_CH, g * POOL_CH:(g + 1) * POOL_CH] for g in range(4)])
    small_part = _pack_small(dw_pool, jnp.zeros((1, D_MODEL), F32), dg_mix_post, dg_ffn_pre, dg_ffn_post, dg_ple,
                             dg_attn, dg_pool, dps, jnp.zeros((1, HEADS), F32), loss8[0:1, 0:1])
    dqt3, dkt3, dvt3, chips_rest, small_all = _attn_bwd(ka, v, kt3, qat3, qt3, dat3, lset3, dlt3, pair_rest, small_part)

    gx, dz, dg_mix_pre, db = _pre_attn_bwd(dqt3, dkt3, dvt3, fl, dy, x, dh1, small["g_mix_pre"], wqkv, wf, wu)

    pair_in = _rs_pair_sum(core, [_wgrad_in(dz, hn)], [0], ROWS_IN, "rs_pair_sum_in")

    small_late = _pack_small_late(dg_mix_pre, db[:, 0:HEADS])
    *upd_rest, chips_in, late_all = _reduce_update_rest(chips_rest, rest_w, rest_m, rest_v, pair_in, small_late)
    upd_in = _reduce_update_big(chips_in, in_w, in_m, in_v, ROWS_IN, "reduce_update_in")
    return gx, (small_all, late_all), upd_in, upd_rest


def kernel(x, p, g_mix_pre, w_in, b_forget, g_attn_grp, g_pool_grp, w_pool, pool_scale, w_out, g_mix_post, g_ffn_pre, w_ffn_gate, w_ffn_up, w_ffn_down, g_ffn_post, w_ple_proj, g_ple, w_ple_gate, loss_target, m_g_mix_pre, m_w_in, m_b_forget, m_g_attn_grp, m_g_pool_grp, m_w_pool, m_pool_scale, m_w_out, m_g_mix_post, m_g_ffn_pre, m_w_ffn_gate, m_w_ffn_up, m_w_ffn_down, m_g_ffn_post, m_w_ple_proj, m_g_ple, m_w_ple_gate, v_g_mix_pre, v_w_in, v_b_forget, v_g_attn_grp, v_g_pool_grp, v_w_pool, v_pool_scale, v_w_out, v_g_mix_post, v_g_ffn_pre, v_w_ffn_gate, v_w_ffn_up, v_w_ffn_down, v_g_ffn_post, v_w_ple_proj, v_g_ple, v_w_ple_gate):
    small = dict(w_pool=w_pool[0], g_mix_pre=g_mix_pre, g_mix_post=g_mix_post, g_ffn_pre=g_ffn_pre,
                 g_ffn_post=g_ffn_post, g_ple=g_ple, g_attn_grp=g_attn_grp, g_pool_grp=g_pool_grp,
                 pool_scale=pool_scale, b_forget=b_forget)
    gx, small_all, upd_in, upd_rest = _step(
        x[0], p[0, 0], loss_target[0], small, _pack_in(w_in), _pack_in(m_w_in), _pack_in(v_w_in),
        _pack_rest(w_out, w_ffn_gate, w_ffn_up, w_ffn_down, w_ple_proj, w_ple_gate),
        _pack_rest(m_w_out, m_w_ffn_gate, m_w_ffn_up, m_w_ffn_down, m_w_ple_proj, m_w_ple_gate),
        _pack_rest(v_w_out, v_w_ffn_gate, v_w_ffn_up, v_w_ffn_down, v_w_ple_proj, v_w_ple_gate))

    sm_w = _pack_small(w_pool, g_mix_pre, g_mix_post, g_ffn_pre, g_ffn_post, g_ple, g_attn_grp, g_pool_grp, pool_scale, b_forget)
    sm_m = _pack_small(m_w_pool, m_g_mix_pre, m_g_mix_post, m_g_ffn_pre, m_g_ffn_post, m_g_ple, m_g_attn_grp, m_g_pool_grp, m_pool_scale, m_b_forget)
    sm_v = _pack_small(v_w_pool, v_g_mix_pre, v_g_mix_post, v_g_ffn_pre, v_g_ffn_post, v_g_ple, v_g_attn_grp, v_g_pool_grp, v_pool_scale, v_b_forget)
    upd_small = _reduce_update_small(*small_all, sm_w, sm_m, sm_v)
    loss = upd_small[0][ROW_MISC, COL_LOSS]

    def leaves(k):
        b_out, b_gate, b_up, b_down, b_ple, b_pg = _unpack_rest(upd_rest[k])
        s = _unpack_small(upd_small[k])
        return (s["g_mix_pre"], _unpack_in(upd_in[k]), s["b_forget"], s["g_attn_grp"], s["g_pool_grp"], s["w_pool"],
                s["pool_scale"], b_out, s["g_mix_post"], s["g_ffn_pre"], b_gate, b_up, b_down, s["g_ffn_post"], b_ple,
                s["g_ple"], b_pg)

    return (loss, gx[None], *leaves(0), *leaves(1), *leaves(2), *leaves(3))
```

```python
import functools

import jax
import jax.numpy as jnp
from jax import lax
from jax.experimental import pallas as pl
from jax.experimental.pallas import tpu as pltpu

F32 = jnp.float32
BF16 = jnp.bfloat16
HIGHEST = lax.Precision.HIGHEST

D_MODEL = 1024
HEADS = 8
HEAD_DIM = 64
D_ATTN = HEADS * HEAD_DIM
POOL_WINDOWS = (2, 4, 8, 16)
POOL_CH = 128
D_POOL = POOL_CH * len(POOL_WINDOWS)
D_FF = 2816
D_PLE = 256
D_IN = 3 * D_ATTN + HEADS + D_POOL
RMS_EPS = 1e-6
N_DEV = 8

ADAM_LR = 0.001
ADAM_B1 = 0.9
ADAM_B2 = 0.999
ADAM_EPS = 1e-08
ADAM_WD = 0.01
ADAM_STEP = 10

LANES = 128
HALO = 16
TS = 512
TS_FF = 512
TS_WGRAD = 1024
TM_WGRAD = 2176
TQ = 256
TN_FF = 1408
NEG = -1e30
VMEM_LIMIT = 56 * 1024 * 1024

SHARD_IN = 257
ROWS_IN = 272
SHARD_FF = 352
OFF_PG = 128
OFF_PLE = 256
OFF_GATE = SHARD_FF
OFF_UP = 2 * SHARD_FF
OFF_DOWN = 3 * SHARD_FF
ROWS_REST = 4 * SHARD_FF
TR_REST = SHARD_FF

SMALL_ROWS = 72
ROW_G_MIX_PRE, ROW_G_MIX_POST, ROW_G_FFN_PRE, ROW_G_FFN_POST, ROW_G_PLE = 64, 65, 66, 67, 68
ROW_GROUP_GAINS, ROW_MISC = 69, 70
COL_B_FORGET = D_POOL
COL_LOSS = D_POOL + HEADS


def _nn(a, b):
    return jnp.dot(a, b, preferred_element_type=F32)


def _nt(a, b):
    return lax.dot_general(a, b, (((1,), (1,)), ((), ())), preferred_element_type=F32)


def _tn(a, b):
    return lax.dot_general(a, b, (((0,), (0,)), ((), ())), preferred_element_type=F32)


def _rstd(v):
    return lax.rsqrt(jnp.mean(v * v, axis=-1, keepdims=True) + RMS_EPS)


def _rms_bwd(v, g, dy):
    r = _rstd(v)
    vh = v * r
    t = dy * g
    dv = r * (t - vh * jnp.mean(t * vh, axis=-1, keepdims=True))
    return dv, jnp.sum(dy * vh, axis=0, keepdims=True)


def _split3(v):
    hi = v.astype(BF16)
    rest = v - hi.astype(F32)
    mid = rest.astype(BF16)
    return hi, mid, (rest - mid.astype(F32)).astype(BF16)


def _mask_matmul(mask, v):
    hi, mid, lo = _split3(v)
    return _nn(mask, lo) + _nn(mask, mid) + _nn(mask, hi)


def _params(n_grid):
    return pltpu.CompilerParams(dimension_semantics=("arbitrary",) * n_grid, vmem_limit_bytes=VMEM_LIMIT)


def _row(i):
    return (i, 0)


def _fixed(*_):
    return (0, 0)


def _spec_square(part):
    return pl.BlockSpec((N_DEV, 128, D_MODEL), lambda *_: (0, part, 0))


def _spec_ff(part):
    return pl.BlockSpec((TN_FF // SHARD_FF, SHARD_FF, D_MODEL), lambda i, j: (j, part, 0))


assert TS == 2 * TQ and TN_FF % SHARD_FF == 0
_HALVES = (slice(0, TQ), slice(TQ, TS))

VMEM_WHOLE = pl.BlockSpec(memory_space=pltpu.VMEM)
SMEM_WHOLE = pl.BlockSpec(memory_space=pltpu.SMEM)
ANY = pl.BlockSpec(memory_space=pl.ANY)


LOG2E = 1.4426950408889634
VROWS = HEAD_DIM + 16
AUG = 128
BIAS_LANE = HEAD_DIM
ONE_LANE = HEAD_DIM + 3
SPARE_LANE = HEADS


def _attn_layout_constants():
    import numpy as np
    place = np.zeros((D_ATTN, HEADS * AUG), np.float32)
    for r in range(D_ATTN):
        place[r, (r // HEAD_DIM) * AUG + r % HEAD_DIM] = 1.0
    bias_k = np.zeros((3, LANES, HEADS * AUG), np.float32)
    bias_q = np.zeros((3, LANES, HEADS * AUG), np.float32)
    for h in range(HEADS):
        for part in range(3):
            bias_k[part, h, h * AUG + BIAS_LANE + part] = -1.0
            bias_q[part, h, h * AUG + ONE_LANE + part] = 1.0
            bias_k[0, SPARE_LANE, h * AUG + ONE_LANE + part] = 1.0
            bias_q[0, SPARE_LANE, h * AUG + BIAS_LANE + part] = 1.0
    as_bf = lambda a: jnp.asarray(a, BF16)
    return dict(place=as_bf(place), place_t=as_bf(place.T), bias_k=as_bf(bias_k),
                bias_q_t=as_bf(bias_q.transpose(0, 2, 1)))


def _pre_attn_fwd(x, g1, wqkv, wf, wu, bpad, wpool, lay, own_block):
    s, d = x.shape
    nt = s // TS
    sub = TS // TQ

    def body(x_ref, g_ref, wqkv_ref, wf_ref, wu_ref, b_ref, wp_ref, place_ref, place_t_ref, bk_ref, bqt_ref, own_ref,
             hn_ref, qt_ref, ka_ref, v_ref, qat_ref, vt_ref, kt_ref, fl_ref, y_ref, mp_ref, all_ref,
             ubuf, ccar, cbuf, stage, send_sems, recv_sems, local_sem):
        i = pl.program_id(0)

        @pl.when(i == 0)
        def _():
            _gather_start(own_ref, all_ref, stage, send_sems, recv_sems, local_sem)
            ubuf[0:HALO, :] = jnp.zeros((HALO, D_POOL), F32)
            ccar[...] = jnp.zeros_like(ccar)

        @pl.when(i == max(nt - 2, 0))
        def _():
            _gather_pass_on(all_ref, send_sems, recv_sems)

        xv = x_ref[...]
        hn = (xv * _rstd(xv) * g_ref[...]).astype(BF16)
        hn_ref[...] = hn
        zq = _nt(hn, wqkv_ref[...])
        qt = (zq[:, 0:D_ATTN] * 0.125).astype(BF16).T
        qb = (zq[:, 0:D_ATTN] * (0.125 * LOG2E)).astype(BF16)
        kb = zq[:, D_ATTN:2 * D_ATTN].astype(BF16)
        vb = zq[:, 2 * D_ATTN:3 * D_ATTN].astype(BF16)
        v_ref[...] = vb

        fl = _nt(hn, wf_ref[...]) + b_ref[...]
        fl_ref[...] = fl
        logf = jax.nn.log_sigmoid(fl)
        rr = lax.broadcasted_iota(jnp.int32, (TS, TS), 0)
        cc = lax.broadcasted_iota(jnp.int32, (TS, TS), 1)
        c = _mask_matmul((cc <= rr).astype(BF16), logf) + ccar[...]
        cbuf[...] = c
        ccar[...] = cbuf[TS - 1:TS, :]
        hi, mid, lo = _split3(c * LOG2E)
        lane = lax.broadcasted_iota(jnp.int32, (TS, LANES), 1)
        parts = (jnp.where(lane == SPARE_LANE, 1.0, hi).astype(BF16), mid, lo)
        ka = _nn(kb, place_ref[...])
        qat = _nt(place_t_ref[...], qb)
        for part in range(3):
            ka = ka + _nn(parts[part], bk_ref[part])
            qat = qat + _nt(bqt_ref[part], parts[part])
        ka_ref[...] = ka.astype(BF16)
        qat = qat.astype(BF16)
        vt = vb.T
        kt = kb.T
        for a in range(sub):
            cols = slice(a * TQ, (a + 1) * TQ)
            qat_ref[a] = qat[:, cols]
            for ref, mat in ((qt_ref, qt), (kt_ref, kt), (vt_ref, vt)):
                for h in range(HEADS):
                    ref[a, h * VROWS:h * VROWS + HEAD_DIM, :] = mat[h * HEAD_DIM:(h + 1) * HEAD_DIM, cols]
                    ref[a, h * VROWS + HEAD_DIM:(h + 1) * VROWS, :] = jnp.ones((VROWS - HEAD_DIM, TQ), BF16)

        u = _nt(hn, wu_ref[...])
        ubuf[HALO:HALO + TS, :] = u
        t = i * TS + lax.broadcasted_iota(jnp.int32, (TS, 1), 0)
        for g, w in enumerate(POOL_WINDOWS):
            cols = slice(g * POOL_CH, (g + 1) * POOL_CH)
            sm = ubuf[:, cols]
            step = 1
            while step < w:
                sm = sm + pltpu.roll(sm, step, 0)
                step *= 2
            cnt = jnp.minimum(t + 1, w).astype(F32)
            yg = (sm[HALO:, :] / cnt - u[:, cols]).astype(BF16)
            y_ref[:, cols] = yg
            mp_ref[:, cols] = _nn(yg, wp_ref[g])
        ubuf[0:HALO, :] = u[TS - HALO:, :]

        @pl.when(i == nt - 1)
        def _():
            _gather_finish(own_ref, all_ref, send_sems, recv_sems)

    nq = s // TQ
    aug = HEADS * AUG
    outs = (
        jax.ShapeDtypeStruct((s, d), BF16), jax.ShapeDtypeStruct((nq, HEADS * VROWS, TQ), BF16),
        jax.ShapeDtypeStruct((s, aug), BF16), jax.ShapeDtypeStruct((s, D_ATTN), BF16),
        jax.ShapeDtypeStruct((nq, aug, TQ), BF16), jax.ShapeDtypeStruct((nq, HEADS * VROWS, TQ), BF16),
        jax.ShapeDtypeStruct((nq, HEADS * VROWS, TQ), BF16),
        jax.ShapeDtypeStruct((s, LANES), F32),
        jax.ShapeDtypeStruct((s, D_POOL), BF16), jax.ShapeDtypeStruct((s, D_POOL), F32),
        jax.ShapeDtypeStruct((N_DEV,) + own_block.shape, own_block.dtype),
    )
    fixed3 = lambda i: (0, 0, 0)
    tiles3 = lambda rows: pl.BlockSpec((sub, rows, TQ), lambda i: (i, 0, 0))
    return pl.pallas_call(
        body, grid=(nt,), out_shape=outs, name="pre_attn_fwd",
        in_specs=[pl.BlockSpec((TS, d), _row), pl.BlockSpec((1, d), _fixed),
                  pl.BlockSpec((3 * D_ATTN, d), _fixed), pl.BlockSpec(wf.shape, _fixed), pl.BlockSpec(wu.shape, _fixed),
                  pl.BlockSpec((1, LANES), _fixed), pl.BlockSpec(wpool.shape, fixed3),
                  pl.BlockSpec(lay["place"].shape, _fixed), pl.BlockSpec(lay["place_t"].shape, _fixed),
                  pl.BlockSpec(lay["bias_k"].shape, fixed3), pl.BlockSpec(lay["bias_q_t"].shape, fixed3), ANY],
        out_specs=(pl.BlockSpec((TS, d), _row), tiles3(HEADS * VROWS),
                   pl.BlockSpec((TS, aug), _row), pl.BlockSpec((TS, D_ATTN), _row),
                   tiles3(aug), tiles3(HEADS * VROWS), tiles3(HEADS * VROWS),
                   pl.BlockSpec((TS, LANES), _row),
                   pl.BlockSpec((TS, D_POOL), _row), pl.BlockSpec((TS, D_POOL), _row), ANY),
        scratch_shapes=[pltpu.VMEM((TS + HALO, D_POOL), F32), pltpu.VMEM((1, LANES), F32), pltpu.VMEM((TS, LANES), F32),
                        pltpu.VMEM(own_block.shape, own_block.dtype),
                        pltpu.SemaphoreType.DMA((7,)), pltpu.SemaphoreType.DMA((7,)), pltpu.SemaphoreType.DMA],
        compiler_params=_params(1),
    )(x, g1, wqkv, wf, wu, bpad, wpool, lay["place"], lay["place_t"], lay["bias_k"], lay["bias_q_t"], own_block)


def _causal_in_tile():
    krow = lax.broadcasted_iota(jnp.int32, (TQ, TQ), 0)
    qcol = lax.broadcasted_iota(jnp.int32, (TQ, TQ), 1)
    return krow <= qcol


def _attn_fwd(ka, qat3, vt3, own_block):
    s = ka.shape[0]
    nq = s // TQ
    pass_on_step = max(nq - 2, 0)

    def body(qa_ref, ka_ref, vt_ref, own_ref, a_ref, lset_ref, all_ref, acc, out_t, st_scr, pt_scr,
             stage, send_sems, recv_sems, local_sem):
        i = pl.program_id(0)

        @pl.when(i == 0)
        def _():
            _gather_start(own_ref, all_ref, stage, send_sems, recv_sems, local_sem)

        @pl.when(i == pass_on_step)
        def _():
            _gather_pass_on(all_ref, send_sems, recv_sems)

        acc[...] = jnp.zeros_like(acc)

        def tile(j, stats, masked):
            tile_max = []
            for h in range(HEADS):
                aug = slice(h * AUG, (h + 1) * AUG)
                st = _nn(ka_ref[pl.ds(j * TQ, TQ), aug], qa_ref[0, aug, :])
                if masked:
                    st = jnp.where(_causal_in_tile(), st, NEG)
                st_scr[h] = st
                tile_max.append(jnp.max(st, axis=0, keepdims=True))
            new, scale = [], []
            for h in range(HEADS):
                m_new = jnp.maximum(stats[h], tile_max[h])
                scale.append(jnp.exp2(stats[h] - m_new))
                pt_scr[h] = jnp.exp2(st_scr[h] - m_new).astype(BF16)
                new.append(m_new)
            for h in range(HEADS):
                rows = slice(h * VROWS, (h + 1) * VROWS)
                acc[rows, :] = scale[h] * acc[rows, :] + _nn(vt_ref[j, rows, :], pt_scr[h])
            return tuple(new)

        init = tuple(jnp.full((1, TQ), NEG, F32) for _ in range(HEADS))
        stats = lax.fori_loop(0, i, functools.partial(tile, masked=False), init)
        stats = tile(i, stats, True)
        for h in range(HEADS):
            denom = acc[h * VROWS + HEAD_DIM:h * VROWS + HEAD_DIM + 1, :]
            out_t[h * HEAD_DIM:(h + 1) * HEAD_DIM, :] = acc[h * VROWS:h * VROWS + HEAD_DIM, :] / denom
            lset_ref[0, h:h + 1, :] = stats[h] + jnp.log2(denom)
        a_ref[...] = out_t[...].T

        @pl.when(i == nq - 1)
        def _():
            _gather_finish(own_ref, all_ref, send_sems, recv_sems)

    r, cdim = own_block.shape
    return pl.pallas_call(
        body, grid=(nq,), name="attn_fwd",
        out_shape=(jax.ShapeDtypeStruct((s, D_ATTN), F32), jax.ShapeDtypeStruct((nq, HEADS, TQ), F32),
                   jax.ShapeDtypeStruct((N_DEV, r, cdim), own_block.dtype)),
        in_specs=[pl.BlockSpec((1, HEADS * AUG, TQ), lambda i: (i, 0, 0)), VMEM_WHOLE, VMEM_WHOLE, ANY],
        out_specs=(pl.BlockSpec((TQ, D_ATTN), _row), pl.BlockSpec((1, HEADS, TQ), lambda i: (i, 0, 0)), ANY),
        scratch_shapes=[pltpu.VMEM((HEADS * VROWS, TQ), F32), pltpu.VMEM((D_ATTN, TQ), F32),
                        pltpu.VMEM((HEADS, TQ, TQ), F32), pltpu.VMEM((HEADS, TQ, TQ), BF16),
                        pltpu.VMEM((r, cdim), own_block.dtype),
                        pltpu.SemaphoreType.DMA((7,)), pltpu.SemaphoreType.DMA((7,)), pltpu.SemaphoreType.DMA],
        compiler_params=_params(1),
    )(qat3, ka, vt3, own_block)


def _post_attn_fwd(a, mpre, x, g_attn, g_pool, pscale, wout, g_post, g_ffn_pre):
    s, d = x.shape

    def body(a_ref, mp_ref, x_ref, ga_ref, gp_ref, ps_ref, wo_ref, gpost_ref, gpre_ref,
             mix_ref, o_ref, h1_ref, hn2_ref):
        for rows in _HALVES:
            av = a_ref[rows, :]
            mix_ref[rows, 0:D_ATTN] = (av * _rstd(av) * ga_ref[...]).astype(BF16)
            mv = mp_ref[rows, :] * ps_ref[...]
            mix_ref[rows, D_ATTN:] = (mv * _rstd(mv) * gp_ref[...]).astype(BF16)
            o = _nn(mix_ref[rows, :], wo_ref[...].reshape(d, d))
            o_ref[rows, :] = o
            h1 = x_ref[rows, :] + o * _rstd(o) * gpost_ref[...]
            h1_ref[rows, :] = h1
            hn2_ref[rows, :] = (h1 * _rstd(h1) * gpre_ref[...]).astype(BF16)

    vec = lambda n: pl.BlockSpec((1, n), _fixed)
    return pl.pallas_call(
        body, grid=(s // TS,), name="post_attn_fwd",
        out_shape=(jax.ShapeDtypeStruct((s, d), BF16), jax.ShapeDtypeStruct((s, d), F32),
                   jax.ShapeDtypeStruct((s, d), F32), jax.ShapeDtypeStruct((s, d), BF16)),
        in_specs=[pl.BlockSpec((TS, D_ATTN), _row), pl.BlockSpec((TS, D_POOL), _row), pl.BlockSpec((TS, d), _row),
                  vec(D_ATTN), vec(D_POOL), vec(D_POOL), _spec_square(0), vec(d), vec(d)],
        out_specs=(pl.BlockSpec((TS, d), _row),) * 4,
        compiler_params=_params(1),
    )(a, mpre, x, g_attn, g_pool, pscale, wout, g_post, g_ffn_pre)


def _ffn_fwd(hn2, wg, wu, wd, h1, g_post):
    s, d = h1.shape
    nc = D_FF // TN_FF
    ts = min(TS_FF, s)

    def body(hn_ref, wg_ref, wu_ref, wd_ref, h1_ref, g_ref, gate_ref, up_ref, act_ref, ff_ref, h2_ref, acc):
        j = pl.program_id(1)

        @pl.when(j == 0)
        def _():
            acc[...] = jnp.zeros_like(acc)

        for r in range(2):
            rows = slice(r * (ts // 2), (r + 1) * (ts // 2))
            hn = hn_ref[rows, :]
            gt = _nt(hn, wg_ref[...].reshape(TN_FF, d))
            up = _nt(hn, wu_ref[...].reshape(TN_FF, d))
            gate_ref[rows, :] = gt.astype(BF16)
            up_ref[rows, :] = up.astype(BF16)
            act_ref[rows, :] = (gt * jax.nn.sigmoid(gt) * up).astype(BF16)
            acc[rows, :] += _nn(act_ref[rows, :], wd_ref[...].reshape(TN_FF, d))

        @pl.when(j == nc - 1)
        def _():
            ff = acc[...]
            ff_ref[...] = ff
            h2_ref[...] = h1_ref[...] + ff * _rstd(ff) * g_ref[...]

    rowblk = pl.BlockSpec((ts, d), lambda i, j: (i, 0))
    chunk = pl.BlockSpec((ts, TN_FF), lambda i, j: (i, j))
    return pl.pallas_call(
        body, grid=(s // ts, nc), name="ffn_fwd",
        out_shape=(jax.ShapeDtypeStruct((s, D_FF), BF16),) * 3 + (jax.ShapeDtypeStruct((s, d), F32),) * 2,
        in_specs=[rowblk, _spec_ff(0), _spec_ff(1), _spec_ff(2), rowblk, pl.BlockSpec((1, d), lambda i, j: (0, 0))],
        out_specs=(chunk, chunk, chunk, rowblk, rowblk),
        scratch_shapes=[pltpu.VMEM((ts, d), F32)],
        compiler_params=_params(2),
    )(hn2, wg, wu, wd, h1, g_post)


def _tail_fwd_bwd(h2, p, tgt, ff, wple, wpg, g_ple, g_ffn_post):
    s, d = h2.shape

    def body(h2_ref, p_ref, t_ref, ff_ref, wple_ref, wpg_ref, gple_ref, gfp_ref,
             dh2_ref, dff_ref, dgl_ref, dpp_ref, h2b_ref, pb_ref, loss_ref, dgple_ref, dgfp_ref):
        i = pl.program_id(0)

        @pl.when(i == 0)
        def _():
            loss_ref[...] = jnp.zeros_like(loss_ref)
            dgple_ref[...] = jnp.zeros_like(dgple_ref)
            dgfp_ref[...] = jnp.zeros_like(dgfp_ref)

        h2 = h2_ref[...]
        h2b = h2.astype(BF16)
        h2b_ref[...] = h2b
        pb = p_ref[...].astype(BF16)
        pb_ref[...] = pb
        pp = _nt(pb, wple_ref[...])
        gple = gple_ref[...]
        e = pp * _rstd(pp) * gple
        wpg = wpg_ref[...].reshape(d, d)
        sg = jax.nn.sigmoid(_nn(h2b, wpg))
        diff = h2 + sg * e - t_ref[...]
        sq = jnp.sum(jnp.sum(diff * diff, axis=1, keepdims=True), axis=0, keepdims=True)
        loss_ref[...] += jnp.broadcast_to(sq * (0.5 / d), loss_ref.shape)
        dh3 = diff * (1.0 / d)
        dgl = (dh3 * e * sg * (1.0 - sg)).astype(BF16)
        dgl_ref[...] = dgl
        dh2 = dh3 + _nt(dgl, wpg)
        dh2_ref[...] = dh2
        dpp, dg = _rms_bwd(pp, gple, dh3 * sg)
        dpp_ref[...] = dpp.astype(BF16)
        dgple_ref[...] += dg
        dff, dg = _rms_bwd(ff_ref[...], gfp_ref[...], dh2)
        dff_ref[...] = dff.astype(BF16)
        dgfp_ref[...] += dg

    rowblk = pl.BlockSpec((TS, d), _row)
    vec = pl.BlockSpec((1, d), _fixed)
    return pl.pallas_call(
        body, grid=(s // TS,), name="tail_fwd_bwd",
        out_shape=(jax.ShapeDtypeStruct((s, d), F32), jax.ShapeDtypeStruct((s, d), BF16),
                   jax.ShapeDtypeStruct((s, d), BF16), jax.ShapeDtypeStruct((s, d), BF16),
                   jax.ShapeDtypeStruct((s, d), BF16), jax.ShapeDtypeStruct((s, D_PLE), BF16),
                   jax.ShapeDtypeStruct((8, LANES), F32), jax.ShapeDtypeStruct((1, d), F32),
                   jax.ShapeDtypeStruct((1, d), F32)),
        in_specs=[rowblk, pl.BlockSpec((TS, D_PLE), _row), rowblk, rowblk,
                  pl.BlockSpec(wple.shape, _fixed), _spec_square(1), vec, vec],
        out_specs=(rowblk, rowblk, rowblk, rowblk, rowblk, pl.BlockSpec((TS, D_PLE), _row),
                   pl.BlockSpec((8, LANES), _fixed), vec, vec),
        compiler_params=_params(1),
    )(h2, p, tgt, ff, wple, wpg, g_ple, g_ffn_post)


def _ffn_bwd(dff, gate, up, wd, wg, wu, h1, dh2, g_pre, send):
    s, d = h1.shape
    nc = D_FF // TN_FF
    ts = min(TS_FF, s)
    npc = len(send)

    def body(dff_ref, gate_ref, up_ref, wd_ref, wg_ref, wu_ref, h1_ref, dh2_ref, g_ref, *refs):
        send_refs, (dgate_ref, dup_ref, dh1_ref, dg_ref) = refs[:npc], refs[npc:npc + 4]
        got_refs, acc, sems = refs[npc + 4:2 * npc + 4], refs[2 * npc + 4], refs[2 * npc + 5:]
        i = pl.program_id(0)
        j = pl.program_id(1)
        _pair_exchange_start((i == 0) & (j == 0), send_refs, got_refs, sems)

        @pl.when((i == 0) & (j == 0))
        def _():
            dg_ref[...] = jnp.zeros_like(dg_ref)

        @pl.when(j == 0)
        def _():
            acc[...] = jnp.zeros_like(acc)

        for r in range(2):
            rows = slice(r * (ts // 2), (r + 1) * (ts // 2))
            dact = _nt(dff_ref[rows, :], wd_ref[...].reshape(TN_FF, d))
            gt = gate_ref[rows, :].astype(F32)
            sg = jax.nn.sigmoid(gt)
            dup_ref[rows, :] = (dact * gt * sg).astype(BF16)
            dgate_ref[rows, :] = (dact * up_ref[rows, :].astype(F32) * (sg * (1.0 + gt * (1.0 - sg)))).astype(BF16)
            acc[rows, :] += (_nn(dgate_ref[rows, :], wg_ref[...].reshape(TN_FF, d))
                             + _nn(dup_ref[rows, :], wu_ref[...].reshape(TN_FF, d)))

        @pl.when(j == nc - 1)
        def _():
            dv, dg = _rms_bwd(h1_ref[...], g_ref[...], acc[...])
            dh1_ref[...] = dh2_ref[...] + dv
            dg_ref[...] += dg

        _pair_exchange_wait((i == s // ts - 1) & (j == nc - 1), send_refs, got_refs, sems)

    rowblk = pl.BlockSpec((ts, d), lambda i, j: (i, 0))
    chunk = pl.BlockSpec((ts, TN_FF), lambda i, j: (i, j))
    vec = pl.BlockSpec((1, d), lambda i, j: (0, 0))
    res = pl.pallas_call(
        body, grid=(s // ts, nc), name="ffn_bwd",
        out_shape=(jax.ShapeDtypeStruct((s, D_FF), BF16), jax.ShapeDtypeStruct((s, D_FF), BF16),
                   jax.ShapeDtypeStruct((s, d), F32), jax.ShapeDtypeStruct((1, d), F32)) + _pair_shapes(send),
        in_specs=[rowblk, chunk, chunk, _spec_ff(2), _spec_ff(0), _spec_ff(1), rowblk, rowblk, vec] + [ANY] * npc,
        out_specs=(chunk, chunk, rowblk, vec) + (ANY,) * npc,
        scratch_shapes=[pltpu.VMEM((ts, d), F32)] + _pair_semaphores(send),
        compiler_params=_params(2),
    )(dff, gate, up, wd, wg, wu, h1, dh2, g_pre, *send)
    return res[:4], list(res[4:])


def _post_attn_bwd(dh1, o, a, mpre, wout, wpool, g_post, g_attn, g_pool, pscale):
    s, d = dh1.shape
    sub = TS // TQ

    def body(dh1_ref, o_ref, a_ref, mp_ref, wo_ref, wp_ref, gpost_ref, ga_ref, gp_ref, ps_ref,
             dob_ref, dat_ref, dlt_ref, dmpb_ref, dy_ref, dgpost_ref, dga_ref, dgp_ref, dps_ref):
        i = pl.program_id(0)

        @pl.when(i == 0)
        def _():
            dgpost_ref[...] = jnp.zeros_like(dgpost_ref)
            dga_ref[...] = jnp.zeros_like(dga_ref)
            dgp_ref[...] = jnp.zeros_like(dgp_ref)
            dps_ref[...] = jnp.zeros_like(dps_ref)

        do, dg = _rms_bwd(o_ref[...], gpost_ref[...], dh1_ref[...])
        dgpost_ref[...] += dg
        dob = do.astype(BF16)
        dob_ref[...] = dob
        dmix = _nt(dob, wo_ref[...].reshape(d, d))

        av = a_ref[...]
        da, dg = _rms_bwd(av, ga_ref[...], dmix[:, 0:D_ATTN])
        dga_ref[...] += dg
        dat = da.astype(BF16).T
        hsel = (lax.shift_right_logical(lax.broadcasted_iota(jnp.int32, (HEADS, D_ATTN), 1), 6)
                == lax.broadcasted_iota(jnp.int32, (HEADS, D_ATTN), 0)).astype(F32)
        dlt = lax.dot_general(hsel, da * av, (((1,), (1,)), ((), ())), precision=HIGHEST, preferred_element_type=F32)
        for q in range(sub):
            dlt_ref[q] = dlt[:, q * TQ:(q + 1) * TQ]
            dat_ref[q] = dat[:, q * TQ:(q + 1) * TQ]

        ps = ps_ref[...]
        mp = mp_ref[...]
        dm, dg = _rms_bwd(mp * ps, gp_ref[...], dmix[:, D_ATTN:])
        dgp_ref[...] += dg
        dps_ref[...] += jnp.sum(dm * mp, axis=0, keepdims=True)
        dmpb = (dm * ps).astype(BF16)
        dmpb_ref[...] = dmpb
        for g in range(len(POOL_WINDOWS)):
            cols = slice(g * POOL_CH, (g + 1) * POOL_CH)
            dy_ref[:, cols] = _nt(dmpb[:, cols], wp_ref[g])

    rowblk = pl.BlockSpec((TS, d), _row)
    half = pl.BlockSpec((TS, D_ATTN), _row)
    vec = lambda n: pl.BlockSpec((1, n), _fixed)
    return pl.pallas_call(
        body, grid=(s // TS,), name="post_attn_bwd",
        out_shape=(jax.ShapeDtypeStruct((s, d), BF16), jax.ShapeDtypeStruct((s // TQ, D_ATTN, TQ), BF16),
                   jax.ShapeDtypeStruct((s // TQ, HEADS, TQ), F32), jax.ShapeDtypeStruct((s, D_POOL), BF16),
                   jax.ShapeDtypeStruct((s, D_POOL), F32), jax.ShapeDtypeStruct((1, d), F32),
                   jax.ShapeDtypeStruct((1, D_ATTN), F32), jax.ShapeDtypeStruct((1, D_POOL), F32),
                   jax.ShapeDtypeStruct((1, D_POOL), F32)),
        in_specs=[rowblk, rowblk, half, half, _spec_square(0),
                  pl.BlockSpec(wpool.shape, lambda i: (0, 0, 0)), vec(d), vec(D_ATTN), vec(D_POOL), vec(D_POOL)],
        out_specs=(rowblk, pl.BlockSpec((sub, D_ATTN, TQ), lambda i: (i, 0, 0)),
                   pl.BlockSpec((sub, HEADS, TQ), lambda i: (i, 0, 0)), half, half,
                   vec(d), vec(D_ATTN), vec(D_POOL), vec(D_POOL)),
        compiler_params=_params(1),
    )(dh1, o, a, mpre, wout, wpool, g_post, g_attn, g_pool, pscale)


def _attn_bwd(ka, v, kt3, qat3, qt3, dot3, lset3, dlt3, chip_blocks, small_block):
    s = ka.shape[0]
    nq = s // TQ

    def body(ka_ref, v_ref, kt_ref, qat_ref, qt_ref, dot_ref, lset_ref, dlt_ref, b_ref, sm_ref,
             dqt_ref, dkt_ref, dvt_ref, got_ref, all_ref, pt_scr, ptb_scr, dsb_scr,
             stage, send_sems, recv_sems, local_sem, stage_s, send_s, recv_s, local_s):
        j = pl.program_id(0)

        @pl.when(j == 0)
        def _():
            _chips_start(b_ref, got_ref, stage, send_sems, recv_sems, local_sem)
            _gather_start(sm_ref, all_ref, stage_s, send_s, recv_s, local_s)
            dqt_ref[...] = jnp.zeros_like(dqt_ref)

        @pl.when(j == max(nq - 2, 0))
        def _():
            _gather_pass_on(all_ref, send_s, recv_s)

        def tile(i, masked):
            def accumulate(ref, idx, val):
                if masked:
                    ref[idx] = val
                else:
                    ref[idx] += val

            for h in range(HEADS):
                aug = slice(h * AUG, (h + 1) * AUG)
                st = _nn(ka_ref[:, aug], qat_ref[i, aug, :]) - lset_ref[i, h:h + 1, :]
                if masked:
                    st = jnp.where(_causal_in_tile(), st, NEG)
                pt = jnp.exp2(st)
                pt_scr[h] = pt
                ptb_scr[h] = pt.astype(BF16)
            heads = [(h, slice(h * HEAD_DIM, (h + 1) * HEAD_DIM)) for h in range(HEADS)]
            for h, hs in heads:
                dst = pt_scr[h] * (_nn(v_ref[:, hs], dot_ref[i, hs, :]) - dlt_ref[i, h:h + 1, :])
                dsb_scr[h] = dst.astype(BF16)
            for h, hs in heads:
                accumulate(dvt_ref, (0, hs, slice(None)), _nt(dot_ref[i, hs, :], ptb_scr[h]))
            for h, hs in heads:
                rows = slice(h * VROWS, (h + 1) * VROWS)
                accumulate(dkt_ref, (0, rows, slice(None)), _nt(qt_ref[i, rows, :], dsb_scr[h]))
            for h, hs in heads:
                rows = slice(h * VROWS, (h + 1) * VROWS)
                dqt_ref[i, rows, :] += _nn(kt_ref[0, rows, :], dsb_scr[h])

        first = j + 1
        pairs = (nq - first) // 2

        def step(p, carry):
            tile(first + 2 * p, False)
            tile(first + 2 * p + 1, False)
            return carry

        tile(j, True)
        lax.fori_loop(0, pairs, step, 0)

        @pl.when(first + 2 * pairs < nq)
        def _():
            tile(nq - 1, False)

        @pl.when(j == nq - 1)
        def _():
            _chips_finish(b_ref, got_ref, send_sems, recv_sems)
            _gather_finish(sm_ref, all_ref, send_s, recv_s)

    blk = pl.BlockSpec((TQ, D_ATTN), _row)
    tile_t = lambda rows: pl.BlockSpec((1, rows, TQ), lambda j: (j, 0, 0))
    per_tile = lambda rows: jax.ShapeDtypeStruct((nq, rows, TQ), F32)
    _, r, cdim = chip_blocks.shape
    dma = pltpu.SemaphoreType.DMA
    return pl.pallas_call(
        body, grid=(nq,), name="attn_bwd",
        out_shape=(per_tile(HEADS * VROWS), per_tile(HEADS * VROWS), per_tile(D_ATTN),
                   jax.ShapeDtypeStruct(chip_blocks.shape, chip_blocks.dtype),
                   jax.ShapeDtypeStruct((N_DEV,) + small_block.shape, small_block.dtype)),
        in_specs=[pl.BlockSpec((TQ, HEADS * AUG), _row), blk, tile_t(HEADS * VROWS),
                  VMEM_WHOLE, VMEM_WHOLE, VMEM_WHOLE, VMEM_WHOLE, VMEM_WHOLE, ANY, ANY],
        out_specs=(pl.BlockSpec((nq, HEADS * VROWS, TQ), lambda j: (0, 0, 0)), tile_t(HEADS * VROWS), tile_t(D_ATTN),
                   ANY, ANY),
        scratch_shapes=[pltpu.VMEM((HEADS, TQ, TQ), F32), pltpu.VMEM((HEADS, TQ, TQ), BF16),
                        pltpu.VMEM((HEADS, TQ, TQ), BF16), pltpu.VMEM((r, cdim), chip_blocks.dtype),
                        dma((3,)), dma((3,)), dma,
                        pltpu.VMEM(small_block.shape, small_block.dtype), dma((7,)), dma((7,)), dma],
        compiler_params=_params(1),
    )(ka, v, kt3, qat3, qt3, dot3, lset3, dlt3, chip_blocks, small_block)


def _pre_attn_bwd(dqt3, dkt3, dvt3, fl, dy, x, dh1, g1, wqkv, wf, wu):
    s, d = x.shape
    nt = s // TS
    n = TS + HALO
    sub = TS // TQ
    qkv, fcols = 3 * D_ATTN, 3 * D_ATTN + LANES

    def body(dqt_ref, dkt_ref, dvt_ref, fl_ref, dy_ref, x_ref, dh1_ref, g_ref, wqkv_ref, wf_ref, wu_ref,
             gx_ref, dz_ref, dg_ref, db_ref, ybuf, ccar, dlog, dsum):
        dqkv_ref = dz_ref.at[:, 0:qkv]
        dfb_ref = dz_ref.at[:, qkv:fcols]
        dub_ref = dz_ref.at[:, fcols:]
        i = pl.program_id(0)
        ti = nt - 1 - i

        @pl.when(i == 0)
        def _():
            ybuf[TS:n, :] = jnp.zeros((HALO, D_POOL), F32)
            ccar[...] = jnp.zeros_like(ccar)
            dg_ref[...] = jnp.zeros_like(dg_ref)
            db_ref[...] = jnp.zeros_like(db_ref)
            dsum[...] = jnp.zeros_like(dsum)

        for a in range(sub):
            for h in range(HEADS):
                r = h * VROWS + HEAD_DIM
                dsum[h:h + 1, a * TQ:(a + 1) * TQ] = dqt_ref[a, r:r + 1, :] - dkt_ref[a, r:r + 1, :]
        rr = lax.broadcasted_iota(jnp.int32, (TS, TS), 0)
        cc = lax.broadcasted_iota(jnp.int32, (TS, TS), 1)
        dlog[...] = ccar[...] + _mask_matmul((cc >= rr).astype(BF16), dsum[...].T)
        ccar[...] = dlog[0:1, :]
        df = dlog[...] * jax.nn.sigmoid(-fl_ref[...])
        db_ref[...] += jnp.sum(df, axis=0, keepdims=True)
        dfb = df.astype(BF16)
        dfb_ref[...] = dfb

        t = ti * TS + lax.broadcasted_iota(jnp.int32, (TS, 1), 0)
        dy = dy_ref[...]
        for g, w in enumerate(POOL_WINDOWS):
            cols = slice(g * POOL_CH, (g + 1) * POOL_CH)
            ybuf[0:TS, cols] = dy[:, cols] / jnp.minimum(t + 1, w).astype(F32)
        for g, w in enumerate(POOL_WINDOWS):
            cols = slice(g * POOL_CH, (g + 1) * POOL_CH)
            sm = ybuf[:, cols]
            step = 1
            while step < w:
                sm = sm + pltpu.roll(sm, n - step, 0)
                step *= 2
            dub_ref[:, cols] = (sm[0:TS, :] - dy[:, cols]).astype(BF16)
        ybuf[TS:n, :] = ybuf[0:HALO, :]

        for a in range(sub):
            rows = slice(a * TQ, (a + 1) * TQ)
            for h in range(HEADS):
                src = slice(h * VROWS, h * VROWS + HEAD_DIM)
                dqkv_ref[rows, h * HEAD_DIM:(h + 1) * HEAD_DIM] = (dqt_ref[a, src, :].T * 0.125).astype(BF16)
                dqkv_ref[rows, D_ATTN + h * HEAD_DIM:D_ATTN + (h + 1) * HEAD_DIM] = dkt_ref[a, src, :].T.astype(BF16)
            dqkv_ref[rows, 2 * D_ATTN:] = dvt_ref[a].T.astype(BF16)
        dhn = _nn(dqkv_ref[...], wqkv_ref[...]) + _nn(dfb, wf_ref[...]) + _nn(dub_ref[...], wu_ref[...])
        dx, dg = _rms_bwd(x_ref[...], g_ref[...], dhn)
        gx_ref[...] = dh1_ref[...] + dx
        dg_ref[...] += dg

    rev = lambda i: (nt - 1 - i, 0)
    blk = lambda w: pl.BlockSpec((TS, w), rev)
    return pl.pallas_call(
        body, grid=(nt,), name="pre_attn_bwd",
        out_shape=(jax.ShapeDtypeStruct((s, d), F32), jax.ShapeDtypeStruct((s, fcols + D_POOL), BF16),
                   jax.ShapeDtypeStruct((1, d), F32), jax.ShapeDtypeStruct((1, LANES), F32)),
        in_specs=[pl.BlockSpec((sub, HEADS * VROWS, TQ), lambda i: (nt - 1 - i, 0, 0)),
                  pl.BlockSpec((sub, HEADS * VROWS, TQ), lambda i: (nt - 1 - i, 0, 0)),
                  pl.BlockSpec((sub, D_ATTN, TQ), lambda i: (nt - 1 - i, 0, 0)),
                  blk(LANES), blk(D_POOL), blk(d), blk(d),
                  pl.BlockSpec((1, d), _fixed), pl.BlockSpec((qkv, d), _fixed), pl.BlockSpec(wf.shape, _fixed),
                  pl.BlockSpec(wu.shape, _fixed)],
        out_specs=(blk(d), blk(fcols + D_POOL), pl.BlockSpec((1, d), _fixed), pl.BlockSpec((1, LANES), _fixed)),
        scratch_shapes=[pltpu.VMEM((n, D_POOL), F32), pltpu.VMEM((1, LANES), F32), pltpu.VMEM((TS, LANES), F32),
                        pltpu.VMEM((LANES, TS), F32)],
        compiler_params=_params(1),
    )(dqt3, dkt3, dvt3, fl, dy, x, dh1, g1, wqkv, wf, wu)


def _wgrad(a, b, out_dtype, name, send=()):
    s, m = a.shape
    n = b.shape[1]
    tm = max(t for t in range(LANES, min(m, TM_WGRAD) + 1, LANES) if m % t == 0)
    ts = min(TS_WGRAD, s)
    ns = s // ts
    npc = len(send)

    def body(a_ref, b_ref, *refs):
        send_refs, o_ref, got_refs = refs[:npc], refs[npc], refs[npc + 1:2 * npc + 1]
        acc, sems = refs[2 * npc + 1], refs[2 * npc + 2:]
        j, i = pl.program_id(0), pl.program_id(1)
        _pair_exchange_start((j == 0) & (i == 0), send_refs, got_refs, sems)

        @pl.when(i == 0)
        def _():
            acc[...] = jnp.zeros_like(acc)

        acc[...] += _tn(a_ref[...], b_ref[...])

        @pl.when(i == ns - 1)
        def _():
            o_ref[...] = acc[...].astype(out_dtype)

        _pair_exchange_wait((j == m // tm - 1) & (i == ns - 1), send_refs, got_refs, sems)

    res = pl.pallas_call(
        body, grid=(m // tm, ns), name=name, out_shape=(jax.ShapeDtypeStruct((m, n), out_dtype),) + _pair_shapes(send),
        in_specs=[pl.BlockSpec((ts, tm), lambda j, i: (i, j)), pl.BlockSpec((ts, n), lambda j, i: (i, 0))] + [ANY] * npc,
        out_specs=(pl.BlockSpec((tm, n), lambda j, i: (j, 0)),) + (ANY,) * npc,
        scratch_shapes=[pltpu.VMEM((tm, n), F32)] + _pair_semaphores(send),
        compiler_params=_params(2),
    )(a, b, *send)
    return (res[0], list(res[1:])) if send else res[0]


def _wgrad_in(dz, hn):
    s, m = dz.shape
    n = hn.shape[1]
    ts = min(TS_WGRAD, s)
    ns = s // ts
    pad_at, pad = 3 * D_ATTN + HEADS, LANES - HEADS
    assert m == D_IN + pad and N_DEV * SHARD_IN == D_IN

    def pieces(d):
        lo, hi = d * SHARD_IN, (d + 1) * SHARD_IN
        spans = [(lo, min(hi, pad_at), 0), (max(lo, pad_at), hi, pad)]
        return [(a + shift, b - a, a - lo) for a, b, shift in spans if b > a]

    def body(a_ref, b_ref, o_ref, acc, stage):
        i = pl.program_id(0)

        @pl.when(i == 0)
        def _():
            acc[...] = jnp.zeros_like(acc)

        acc[...] += _tn(a_ref[...], b_ref[...])

        @pl.when(i == ns - 1)
        def _():
            stage[SHARD_IN:ROWS_IN, :] = jnp.zeros((ROWS_IN - SHARD_IN, n), F32)
            for d in range(N_DEV):
                for src, rows, dst in pieces(d):
                    stage[dst:dst + rows, :] = acc[src:src + rows, :]
                o_ref[d] = stage[...].astype(BF16)

    return pl.pallas_call(
        body, grid=(ns,), name="wgrad_in", out_shape=jax.ShapeDtypeStruct((N_DEV, ROWS_IN, n), BF16),
        in_specs=[pl.BlockSpec((ts, m), _row), pl.BlockSpec((ts, n), _row)],
        out_specs=pl.BlockSpec((N_DEV, ROWS_IN, n), lambda i: (0, 0, 0)),
        scratch_shapes=[pltpu.VMEM((m, n), F32), pltpu.VMEM((ROWS_IN, n), F32)],
        compiler_params=_params(1),
    )(dz, hn)


def _adamw(w, g, m, v):
    m = ADAM_B1 * m + (1.0 - ADAM_B1) * g
    v = ADAM_B2 * v + (1.0 - ADAM_B2) * (g * g)
    m_hat = m / (1.0 - ADAM_B1 ** ADAM_STEP)
    v_hat = v / (1.0 - ADAM_B2 ** ADAM_STEP)
    delta = -ADAM_LR * (m_hat / (jnp.sqrt(v_hat) + ADAM_EPS) + ADAM_WD * w)
    return delta, m, v


def _sum_update(p_ref, w_ref, m_ref, v_ref, g_ref, d_ref, nm_ref, nv_ref):
    g = p_ref[0].astype(F32)
    for k in range(1, p_ref.shape[0]):
        g = g + p_ref[k].astype(F32)
    g_ref[...] = g
    d_ref[...], nm_ref[...], nv_ref[...] = _adamw(w_ref[...], g, m_ref[...], v_ref[...])


def _reduce_update_rest(parts, w, m, v, chip_blocks, small_block):
    nk, r, c = parts.shape
    ns = r // TR_REST

    def body(p_ref, w_ref, m_ref, v_ref, b_ref, sm_ref, g_ref, d_ref, nm_ref, nv_ref, got_ref, all_ref,
             stage_b, stage_s, send_b, recv_b, local_b, send_s, recv_s, local_s):
        i = pl.program_id(0)

        @pl.when(i == 0)
        def _():
            _chips_start(b_ref, got_ref, stage_b, send_b, recv_b, local_b)
            _gather_start(sm_ref, all_ref, stage_s, send_s, recv_s, local_s)

        _sum_update(p_ref, w_ref, m_ref, v_ref, g_ref, d_ref, nm_ref, nv_ref)

        @pl.when(i == ns - 1)
        def _():
            _gather_pass_on(all_ref, send_s, recv_s)
            _chips_finish(b_ref, got_ref, send_b, recv_b)
            _gather_finish(sm_ref, all_ref, send_s, recv_s)

    blk = pl.BlockSpec((TR_REST, c), _row)
    out = jax.ShapeDtypeStruct((r, c), F32)
    dma = pltpu.SemaphoreType.DMA
    return pl.pallas_call(
        body, grid=(ns,), name="reduce_update_rest",
        out_shape=(out,) * 4 + (jax.ShapeDtypeStruct(chip_blocks.shape, chip_blocks.dtype),
                                jax.ShapeDtypeStruct((N_DEV,) + small_block.shape, small_block.dtype)),
        in_specs=[pl.BlockSpec((nk, TR_REST, c), lambda i: (0, i, 0)), blk, blk, blk, ANY, ANY],
        out_specs=(blk,) * 4 + (ANY, ANY),
        scratch_shapes=[pltpu.VMEM(chip_blocks.shape[1:], chip_blocks.dtype), pltpu.VMEM(small_block.shape, small_block.dtype),
                        dma((3,)), dma((3,)), dma, dma((7,)), dma((7,)), dma],
        compiler_params=_params(1),
    )(parts, w, m, v, chip_blocks, small_block)


def _reduce_update_big(parts, w, m, v, tr, name):
    nk, r, c = parts.shape

    def body(p_ref, w_ref, m_ref, v_ref, g_ref, d_ref, nm_ref, nv_ref):
        _sum_update(p_ref, w_ref, m_ref, v_ref, g_ref, d_ref, nm_ref, nv_ref)

    blk = pl.BlockSpec((tr, c), _row)
    out = jax.ShapeDtypeStruct((r, c), F32)
    return pl.pallas_call(
        body, grid=(r // tr,), name=name, out_shape=(out,) * 4,
        in_specs=[pl.BlockSpec((nk, tr, c), lambda i: (0, i, 0)), blk, blk, blk],
        out_specs=(blk,) * 4, compiler_params=_params(1),
    )(parts, w, m, v)


def _reduce_update_small(parts, late, w, m, v):
    nd = parts.shape[0]
    first = parts.shape[1] - late.shape[1]

    def body(p_ref, q_ref, w_ref, m_ref, v_ref, g_ref, d_ref, nm_ref, nv_ref):
        g, t = p_ref[0], q_ref[0]
        for k in range(1, nd):
            g, t = g + p_ref[k], t + q_ref[k]
        g_ref[...] = g
        g_ref[first:, :] = g[first:, :] + t
        d_ref[...], nm_ref[...], nv_ref[...] = _adamw(w_ref[...], g_ref[...], m_ref[...], v_ref[...])

    out = jax.ShapeDtypeStruct(w.shape, F32)
    return pl.pallas_call(body, name="reduce_update_small", out_shape=(out,) * 4,
                          compiler_params=pltpu.CompilerParams(vmem_limit_bytes=VMEM_LIMIT))(parts, late, w, m, v)


MESH = pl.DeviceIdType.MESH


def _copy_through_vmem(src_hbm, dst_hbm, stage, sem):
    load = pltpu.make_async_copy(src_hbm, stage, sem)
    load.start()
    load.wait()
    store = pltpu.make_async_copy(stage, dst_hbm, sem)
    store.start()
    store.wait()


class _GatherPlan:
    def __init__(self, x_ref, out_ref, send_sems, recv_sems):
        x, y, c = lax.axis_index("x"), lax.axis_index("y"), lax.axis_index("c")
        self.me, self.sibling, self.c = (x, y, c), (x, y, 1 - c), c
        self.chips = [(1 - x, y), (x, 1 - y), (1 - x, 1 - y)]
        self.x_ref, self.out_ref, self.send_sems, self.recv_sems = x_ref, out_ref, send_sems, recv_sems

    def slot(self, px, py, pc):
        return self.out_ref.at[4 * px + 2 * py + pc]

    def copy(self, k, block, to, src=None):
        return pltpu.make_async_remote_copy(
            src_ref=self.slot(*block) if src is None else src, dst_ref=self.slot(*block),
            send_sem=self.send_sems.at[k], recv_sem=self.recv_sems.at[k], device_id=to, device_id_type=MESH)

    def first(self):
        return [self.copy(0, self.me, self.sibling, src=self.x_ref)] + [
            self.copy(1 + j, self.me, (*chip, self.c), src=self.x_ref) for j, chip in enumerate(self.chips)]

    def passed(self):
        return [self.copy(4 + j, (*chip, self.c), self.sibling) for j, chip in enumerate(self.chips)]


def _gather_start(x_ref, out_ref, stage, send_sems, recv_sems, local_sem):
    plan = _GatherPlan(x_ref, out_ref, send_sems, recv_sems)
    for cp in plan.first():
        cp.start()
    _copy_through_vmem(x_ref, plan.slot(*plan.me), stage, local_sem)


def _gather_pass_on(out_ref, send_sems, recv_sems):
    plan = _GatherPlan(None, out_ref, send_sems, recv_sems)
    passed = plan.passed()
    for j, chip in enumerate(plan.chips):
        plan.copy(1 + j, (*chip, plan.c), plan.me).wait_recv()
        passed[j].start()


def _gather_finish(x_ref, out_ref, send_sems, recv_sems):
    plan = _GatherPlan(x_ref, out_ref, send_sems, recv_sems)
    plan.copy(0, plan.sibling, plan.me).wait_recv()
    for j, chip in enumerate(plan.chips):
        plan.copy(4 + j, (*chip, 1 - plan.c), plan.me).wait_recv()
    for cp in plan.first() + plan.passed():
        cp.wait_send()


def _all_gather(xs, name):
    r, cdim = xs.shape

    def body(x_ref, out_ref, stage, send_sems, recv_sems, local_sem):
        _gather_start(x_ref, out_ref, stage, send_sems, recv_sems, local_sem)
        _gather_pass_on(out_ref, send_sems, recv_sems)
        _gather_finish(x_ref, out_ref, send_sems, recv_sems)

    return pl.pallas_call(
        body, name=name, out_shape=jax.ShapeDtypeStruct((N_DEV, r, cdim), xs.dtype),
        in_specs=[ANY], out_specs=ANY,
        scratch_shapes=[pltpu.VMEM((r, cdim), xs.dtype), pltpu.SemaphoreType.DMA((7,)), pltpu.SemaphoreType.DMA((7,)),
                        pltpu.SemaphoreType.DMA],
        compiler_params=pltpu.CompilerParams(vmem_limit_bytes=VMEM_LIMIT),
    )(xs)


def _pair_copies(src_refs, dst_refs, send_sems, recv_sems):
    x, y, c = lax.axis_index("x"), lax.axis_index("y"), lax.axis_index("c")
    return [pltpu.make_async_remote_copy(
        src_ref=src.at[2 * k + (1 - c)], dst_ref=dst.at[k], send_sem=send_sems.at[k, p], recv_sem=recv_sems.at[k, p],
        device_id=(x, y, 1 - c), device_id_type=MESH)
        for k in range(N_DEV // 2) for p, (src, dst) in enumerate(zip(src_refs, dst_refs))]


def _pair_exchange_start(when, send_refs, got_refs, sems):
    if send_refs:
        @pl.when(when)
        def _():
            for cp in _pair_copies(send_refs, got_refs, *sems):
                cp.start()


def _pair_exchange_wait(when, send_refs, got_refs, sems):
    if send_refs:
        @pl.when(when)
        def _():
            for cp in _pair_copies(send_refs, got_refs, *sems):
                cp.wait()


def _pair_shapes(send):
    return tuple(jax.ShapeDtypeStruct((N_DEV // 2,) + t.shape[1:], t.dtype) for t in send)


def _pair_semaphores(send):
    shape = (N_DEV // 2, len(send))
    return [pltpu.SemaphoreType.DMA(shape), pltpu.SemaphoreType.DMA(shape)] if send else []


def _rs_pair_sum(core, pieces, offsets, rows, name, landed=()):
    cdim = pieces[0].shape[2]
    nk = N_DEV // 2
    npc = len(pieces)
    nrem = npc - len(landed)
    spans = [(o, t.shape[1]) for t, o in zip(pieces, offsets)]
    ends = [o + n for o, n in spans]
    gaps = [(a, b - a) for a, b in zip(ends, [o for o, _ in spans[1:]] + [rows]) if b > a]

    def body(core_ref, *refs):
        own, src, got, o_ref = refs[:npc], refs[npc:npc + nrem], refs[npc + nrem:2 * npc], refs[2 * npc]
        landing, send_sems, recv_sems = refs[2 * npc + 1:]
        k = pl.program_id(0)
        x, y, c = lax.axis_index("x"), lax.axis_index("y"), lax.axis_index("c")

        def copies(kk):
            return [pltpu.make_async_remote_copy(
                src_ref=src[p].at[2 * kk + (1 - c)], dst_ref=landing.at[kk, pl.ds(o, n)],
                send_sem=send_sems.at[kk, p], recv_sem=recv_sems.at[kk, p], device_id=(x, y, 1 - c),
                device_id_type=MESH) for p, (o, n) in enumerate(spans[:nrem])]

        @pl.when(k == 0)
        def _():
            for kk in range(nk):
                for cp in copies(kk):
                    cp.start()

        for cp, piece, (o, n) in zip(copies(k), own, spans):
            cp.wait_recv()
            o_ref[0, o:o + n, :] = (piece[0].astype(F32) + landing[k, o:o + n, :].astype(F32)).astype(BF16)
        for theirs, piece, (o, n) in zip(got, own[nrem:], spans[nrem:]):
            o_ref[0, o:o + n, :] = (piece[0].astype(F32) + theirs[0].astype(F32)).astype(BF16)
        for o, n in gaps:
            o_ref[0, o:o + n, :] = jnp.zeros((n, cdim), BF16)

        @pl.when(k == nk - 1)
        def _():
            for kk in range(nk):
                for cp in copies(kk):
                    cp.wait_send()

    own_specs = [pl.BlockSpec((1, n, cdim), lambda k, core_ref: (2 * k + core_ref[0], 0, 0)) for _, n in spans]
    got_specs = [pl.BlockSpec((1, n, cdim), lambda k, core_ref: (k, 0, 0)) for _, n in spans[nrem:]]
    land_rows = max(o + n for o, n in spans[:nrem])
    return pl.pallas_call(
        body, name=name, out_shape=jax.ShapeDtypeStruct((nk, rows, cdim), BF16),
        grid_spec=pltpu.PrefetchScalarGridSpec(
            num_scalar_prefetch=1, grid=(nk,),
            in_specs=own_specs + [ANY] * nrem + got_specs,
            out_specs=pl.BlockSpec((1, rows, cdim), lambda k, core_ref: (k, 0, 0)),
            scratch_shapes=[pltpu.VMEM((nk, land_rows, cdim), BF16), pltpu.SemaphoreType.DMA((nk, nrem)),
                            pltpu.SemaphoreType.DMA((nk, nrem))]),
        compiler_params=_params(1),
    )(core, *pieces, *pieces[:nrem], *landed)


def _chips_start(b_ref, out_ref, stage, send_sems, recv_sems, local_sem):
    x, y, c = lax.axis_index("x"), lax.axis_index("y"), lax.axis_index("c")
    mychip = 2 * x + y
    for j, (px, py) in enumerate([(1 - x, y), (x, 1 - y), (1 - x, 1 - y)]):
        pltpu.make_async_remote_copy(
            src_ref=b_ref.at[2 * px + py], dst_ref=out_ref.at[mychip],
            send_sem=send_sems.at[j], recv_sem=recv_sems.at[j], device_id=(px, py, c), device_id_type=MESH).start()
    _copy_through_vmem(b_ref.at[mychip], out_ref.at[mychip], stage, local_sem)


def _chips_finish(b_ref, out_ref, send_sems, recv_sems):
    x, y, c = lax.axis_index("x"), lax.axis_index("y"), lax.axis_index("c")
    for j, (px, py) in enumerate([(1 - x, y), (x, 1 - y), (1 - x, 1 - y)]):
        pltpu.make_async_remote_copy(
            src_ref=b_ref.at[2 * px + py], dst_ref=out_ref.at[2 * px + py],
            send_sem=send_sems.at[j], recv_sem=recv_sems.at[j], device_id=(px, py, c), device_id_type=MESH).wait()


def _pad_rows(a, rows):
    return jnp.pad(a, ((0, rows - a.shape[0]), (0, 0)))


def _pack_in(w_in):
    return _pad_rows(w_in[0].T, ROWS_IN)


def _unpack_in(r):
    return r[0:SHARD_IN].T[None]


def _pack_rest(w_out, w_gate, w_up, w_down, w_ple, w_pg):
    head = _pad_rows(jnp.concatenate([w_out[0], w_pg[0], w_ple[0].T.reshape(32, D_MODEL)], axis=0), OFF_GATE)
    return jnp.concatenate([head, w_gate[0].T, w_up[0].T, w_down[0]], axis=0)


def _unpack_rest(r):
    return (r[0:OFF_PG][None], r[OFF_GATE:OFF_UP].T[None], r[OFF_UP:OFF_DOWN].T[None], r[OFF_DOWN:ROWS_REST][None],
            r[OFF_PLE:OFF_PLE + 32].reshape(128, D_PLE).T[None], r[OFF_PG:OFF_PLE][None])


def _pack_small(w_pool, g_mix_pre, g_mix_post, g_ffn_pre, g_ffn_post, g_ple, g_attn, g_pool, pool_scale, b_forget,
                loss=None):
    row = lambda vrow: vrow.reshape(1, -1)
    misc = [row(pool_scale), row(b_forget), row(loss) if loss is not None else jnp.zeros((1, 1), F32),
            jnp.zeros((1, D_MODEL - COL_LOSS - 1), F32)]
    rows = [w_pool.reshape(64, D_MODEL), row(g_mix_pre), row(g_mix_post), row(g_ffn_pre), row(g_ffn_post), row(g_ple),
            jnp.concatenate([row(g_attn), row(g_pool)], axis=1), jnp.concatenate(misc, axis=1),
            jnp.zeros((SMALL_ROWS - ROW_MISC - 1, D_MODEL), F32)]
    return jnp.concatenate(rows, axis=0)


def _pack_small_late(g_mix_pre, b_forget):
    misc = [jnp.zeros((1, COL_B_FORGET), F32), b_forget.reshape(1, -1), jnp.zeros((1, D_MODEL - COL_LOSS), F32)]
    return jnp.concatenate([g_mix_pre.reshape(1, -1), jnp.zeros((ROW_MISC - ROW_G_MIX_PRE - 1, D_MODEL), F32),
                            jnp.concatenate(misc, axis=1), jnp.zeros((SMALL_ROWS - ROW_MISC - 1, D_MODEL), F32)], axis=0)


def _unpack_small(r):
    gains, misc = r[ROW_GROUP_GAINS:ROW_GROUP_GAINS + 1], r[ROW_MISC:ROW_MISC + 1]
    return dict(
        w_pool=r[0:64].reshape(1, 4, POOL_CH, POOL_CH), g_mix_pre=r[ROW_G_MIX_PRE:ROW_G_MIX_PRE + 1],
        g_mix_post=r[ROW_G_MIX_POST:ROW_G_MIX_POST + 1], g_ffn_pre=r[ROW_G_FFN_PRE:ROW_G_FFN_PRE + 1],
        g_ffn_post=r[ROW_G_FFN_POST:ROW_G_FFN_POST + 1], g_ple=r[ROW_G_PLE:ROW_G_PLE + 1],
        g_attn_grp=gains[:, 0:D_ATTN], g_pool_grp=gains[:, D_ATTN:D_ATTN + D_POOL],
        pool_scale=misc[:, 0:D_POOL], b_forget=misc[:, COL_B_FORGET:COL_B_FORGET + HEADS])


def _step(x, p, tgt, small, in_w, in_m, in_v, rest_w, rest_m, rest_v):
    core = lax.axis_index("c").astype(jnp.int32).reshape(1)
    win_t = _all_gather(in_w.astype(BF16), "gather_w_in")[:, 0:SHARD_IN].reshape(D_IN, D_MODEL)
    wqkv = win_t
    wf = _pad_rows(win_t[3 * D_ATTN:3 * D_ATTN + HEADS], LANES)
    wu = win_t[3 * D_ATTN + HEADS:]
    wpool = small["w_pool"].astype(BF16)
    bpad = jnp.pad(small["b_forget"], ((0, 0), (0, LANES - HEADS)))

    lay = _attn_layout_constants()
    rest_b = rest_w.astype(BF16)
    hn, qt3, ka, v, qat3, vt3, kt3, fl, y, mpre, gh = _pre_attn_fwd(x, small["g_mix_pre"], wqkv, wf, wu, bpad, wpool, lay,
                                                                 rest_b[0:OFF_GATE])
    a, lset3, gf = _attn_fwd(ka, qat3, vt3, rest_b[OFF_GATE:])
    wple_t = gh[:, OFF_PLE:OFF_PLE + 32].reshape(D_MODEL, D_PLE)
    mix, o, h1, hn2 = _post_attn_fwd(a, mpre, x, small["g_attn_grp"], small["g_pool_grp"], small["pool_scale"], gh,
                                     small["g_mix_post"], small["g_ffn_pre"])
    gate, up, act, ff, h2 = _ffn_fwd(hn2, gf, gf, gf, h1, small["g_ffn_post"])
    dh2, dff, dgl, dpp, h2b, pb, loss8, dg_ple, dg_ffn_post = _tail_fwd_bwd(
        h2, p, tgt, ff, wple_t, gh, small["g_ple"], small["g_ffn_post"])
    nd = N_DEV
    piece = lambda g, rows: g.reshape(nd, rows, D_MODEL)
    s_pg = piece(_wgrad(h2b, dgl, BF16, "wgrad_ple_gate"), 128)
    s_ple = piece(_wgrad(dpp, pb, BF16, "wgrad_ple"), 32)
    s_down = piece(_wgrad(act, dff, BF16, "wgrad_down"), SHARD_FF)
    (dgate, dup, dh1, dg_ffn_pre), (l_pg, l_ple, l_down) = _ffn_bwd(
        dff, gate, up, gf, gf, gf, h1, dh2, small["g_ffn_pre"], [s_pg, s_ple, s_down])
    s_up = piece(_wgrad(dup, hn2, BF16, "wgrad_up"), SHARD_FF)
    s_gate, (l_up,) = _wgrad(dgate, hn2, BF16, "wgrad_gate", [s_up])
    s_gate = piece(s_gate, SHARD_FF)
    dob, dat3, dlt3, dmpb, dy, dg_mix_post, dg_attn, dg_pool, dps = _post_attn_bwd(
        dh1, o, a, mpre, gh, wpool, small["g_mix_post"], small["g_attn_grp"], small["g_pool_grp"], small["pool_scale"])
    s_out, (l_gate,) = _wgrad(mix, dob, BF16, "wgrad_out", [s_gate])
    pair_rest = _rs_pair_sum(core, [piece(s_out, 128), s_pg, s_ple, s_gate, s_up, s_down],
                             [0, OFF_PG, OFF_PLE, OFF_GATE, OFF_UP, OFF_DOWN], ROWS_REST, "rs_pair_sum_rest",
                             [l_pg, l_ple, l_gate, l_up, l_down])

    dwp = _wgrad(y, dmpb, F32, "wgrad_pool")
    dw_pool = jnp.stack([dwp[g * POOL_CH:(g + 1) * POOL_CH, g * POOL_CH:(g + 1) * POOL_CH] for g in range(4)])
    small_part = _pack_small(dw_pool, jnp.zeros((1, D_MODEL), F32), dg_mix_post, dg_ffn_pre, dg_ffn_post, dg_ple,
                             dg_attn, dg_pool, dps, jnp.zeros((1, HEADS), F32), loss8[0:1, 0:1])
    dqt3, dkt3, dvt3, chips_rest, small_all = _attn_bwd(ka, v, kt3, qat3, qt3, dat3, lset3, dlt3, pair_rest, small_part)

    gx, dz, dg_mix_pre, db = _pre_attn_bwd(dqt3, dkt3, dvt3, fl, dy, x, dh1, small["g_mix_pre"], wqkv, wf, wu)

    pair_in = _rs_pair_sum(core, [_wgrad_in(dz, hn)], [0], ROWS_IN, "rs_pair_sum_in")

    small_late = _pack_small_late(dg_mix_pre, db[:, 0:HEADS])
    *upd_rest, chips_in, late_all = _reduce_update_rest(chips_rest, rest_w, rest_m, rest_v, pair_in, small_late)
    upd_in = _reduce_update_big(chips_in, in_w, in_m, in_v, ROWS_IN, "reduce_update_in")
    return gx, (small_all, late_all), upd_in, upd_rest


def kernel(x, p, g_mix_pre, w_in, b_forget, g_attn_grp, g_pool_grp, w_pool, pool_scale, w_out, g_mix_post, g_ffn_pre, w_ffn_gate, w_ffn_up, w_ffn_down, g_ffn_post, w_ple_proj, g_ple, w_ple_gate, loss_target, m_g_mix_pre, m_w_in, m_b_forget, m_g_attn_grp, m_g_pool_grp, m_w_pool, m_pool_scale, m_w_out, m_g_mix_post, m_g_ffn_pre, m_w_ffn_gate, m_w_ffn_up, m_w_ffn_down, m_g_ffn_post, m_w_ple_proj, m_g_ple, m_w_ple_gate, v_g_mix_pre, v_w_in, v_b_forget, v_g_attn_grp, v_g_pool_grp, v_w_pool, v_pool_scale, v_w_out, v_g_mix_post, v_g_ffn_pre, v_w_ffn_gate, v_w_ffn_up, v_w_ffn_down, v_g_ffn_post, v_w_ple_proj, v_g_ple, v_w_ple_gate):
    small = dict(w_pool=w_pool[0], g_mix_pre=g_mix_pre, g_mix_post=g_mix_post, g_ffn_pre=g_ffn_pre,
                 g_ffn_post=g_ffn_post, g_ple=g_ple, g_attn_grp=g_attn_grp, g_pool_grp=g_pool_grp,
                 pool_scale=pool_scale, b_forget=b_forget)
    gx, small_all, upd_in, upd_rest = _step(
        x[0], p[0, 0], loss_target[0], small, _pack_in(w_in), _pack_in(m_w_in), _pack_in(v_w_in),
        _pack_rest(w_out, w_ffn_gate, w_ffn_up, w_ffn_down, w_ple_proj, w_ple_gate),
        _pack_rest(m_w_out, m_w_ffn_gate, m_w_ffn_up, m_w_ffn_down, m_w_ple_proj, m_w_ple_gate),
        _pack_rest(v_w_out, v_w_ffn_gate, v_w_ffn_up, v_w_ffn_down, v_w_ple_proj, v_w_ple_gate))

    sm_w = _pack_small(w_pool, g_mix_pre, g_mix_post, g_ffn_pre, g_ffn_post, g_ple, g_attn_grp, g_pool_grp, pool_scale, b_forget)
    sm_m = _pack_small(m_w_pool, m_g_mix_pre, m_g_mix_post, m_g_ffn_pre, m_g_ffn_post, m_g_ple, m_g_attn_grp, m_g_pool_grp, m_pool_scale, m_b_forget)
    sm_v = _pack_small(v_w_pool, v_g_mix_pre, v_g_mix_post, v_g_ffn_pre, v_g_ffn_post, v_g_ple, v_g_attn_grp, v_g_pool_grp, v_pool_scale, v_b_forget)
    upd_small = _reduce_update_small(*small_all, sm_w, sm_m, sm_v)
    loss = upd_small[0][ROW_MISC, COL_LOSS]

    def leaves(k):
        b_out, b_gate, b_up, b_down, b_ple, b_pg = _unpack_rest(upd_rest[k])
        s = _unpack_small(upd_small[k])
        return (s["g_mix_pre"], _unpack_in(upd_in[k]), s["b_forget"], s["g_attn_grp"], s["g_pool_grp"], s["w_pool"],
                s["pool_scale"], b_out, s["g_mix_post"], s["g_ffn_pre"], b_gate, b_up, b_down, s["g_ffn_post"], b_ple,
                s["g_ple"], b_pg)

    return (loss, gx[None], *leaves(0), *leaves(1), *leaves(2), *leaves(3))
```

```python
import functools

import jax
import jax.numpy as jnp
from jax import lax
from jax.experimental import pallas as pl
from jax.experimental.pallas import tpu as pltpu

F32 = jnp.float32
BF16 = jnp.bfloat16
HIGHEST = lax.Precision.HIGHEST

D_MODEL = 1024
HEADS = 8
HEAD_DIM = 64
D_ATTN = HEADS * HEAD_DIM
POOL_WINDOWS = (2, 4, 8, 16)
POOL_CH = 128
D_POOL = POOL_CH * len(POOL_WINDOWS)
D_FF = 2816
D_PLE = 256
D_IN = 3 * D_ATTN + HEADS + D_POOL
RMS_EPS = 1e-6
N_DEV = 8

ADAM_LR = 0.001
ADAM_B1 = 0.9
ADAM_B2 = 0.999
ADAM_EPS = 1e-08
ADAM_WD = 0.01
ADAM_STEP = 10

LANES = 128
HALO = 16
TS = 512
TS_FF = 512
TS_WGRAD = 1024
TM_WGRAD = 2176
TQ = 256
TN_FF = 1408
NEG = -1e30
VMEM_LIMIT = 56 * 1024 * 1024

SHARD_IN = 257
ROWS_IN = 272
SHARD_FF = 352
OFF_PG = 128
OFF_PLE = 256
OFF_GATE = SHARD_FF
OFF_UP = 2 * SHARD_FF
OFF_DOWN = 3 * SHARD_FF
ROWS_REST = 4 * SHARD_FF
TR_REST = SHARD_FF

SMALL_ROWS = 72
ROW_G_MIX_PRE, ROW_G_MIX_POST, ROW_G_FFN_PRE, ROW_G_FFN_POST, ROW_G_PLE = 64, 65, 66, 67, 68
ROW_GROUP_GAINS, ROW_MISC = 69, 70
COL_B_FORGET = D_POOL
COL_LOSS = D_POOL + HEADS


def _nn(a, b):
    return jnp.dot(a, b, preferred_element_type=F32)


def _nt(a, b):
    return lax.dot_general(a, b, (((1,), (1,)), ((), ())), preferred_element_type=F32)


def _tn(a, b):
    return lax.dot_general(a, b, (((0,), (0,)), ((), ())), preferred_element_type=F32)


def _rstd(v):
    return lax.rsqrt(jnp.mean(v * v, axis=-1, keepdims=True) + RMS_EPS)


def _rms_bwd(v, g, dy):
    r = _rstd(v)
    vh = v * r
    t = dy * g
    dv = r * (t - vh * jnp.mean(t * vh, axis=-1, keepdims=True))
    return dv, jnp.sum(dy * vh, axis=0, keepdims=True)


def _split3(v):
    hi = v.astype(BF16)
    rest = v - hi.astype(F32)
    mid = rest.astype(BF16)
    return hi, mid, (rest - mid.astype(F32)).astype(BF16)


def _mask_matmul(mask, v):
    hi, mid, lo = _split3(v)
    return _nn(mask, lo) + _nn(mask, mid) + _nn(mask, hi)


def _params(n_grid):
    return pltpu.CompilerParams(dimension_semantics=("arbitrary",) * n_grid, vmem_limit_bytes=VMEM_LIMIT)


def _row(i):
    return (i, 0)


def _fixed(*_):
    return (0, 0)


def _spec_square(part):
    return pl.BlockSpec((N_DEV, 128, D_MODEL), lambda *_: (0, part, 0))


def _spec_ff(part):
    return pl.BlockSpec((TN_FF // SHARD_FF, SHARD_FF, D_MODEL), lambda i, j: (j, part, 0))


assert TS == 2 * TQ and TN_FF % SHARD_FF == 0
_HALVES = (slice(0, TQ), slice(TQ, TS))

VMEM_WHOLE = pl.BlockSpec(memory_space=pltpu.VMEM)
SMEM_WHOLE = pl.BlockSpec(memory_space=pltpu.SMEM)
ANY = pl.BlockSpec(memory_space=pl.ANY)


LOG2E = 1.4426950408889634
VROWS = HEAD_DIM + 16
AUG = 128
BIAS_LANE = HEAD_DIM
ONE_LANE = HEAD_DIM + 3
SPARE_LANE = HEADS


def _attn_layout_constants():
    import numpy as np
    place = np.zeros((D_ATTN, HEADS * AUG), np.float32)
    for r in range(D_ATTN):
        place[r, (r // HEAD_DIM) * AUG + r % HEAD_DIM] = 1.0
    bias_k = np.zeros((3, LANES, HEADS * AUG), np.float32)
    bias_q = np.zeros((3, LANES, HEADS * AUG), np.float32)
    for h in range(HEADS):
        for part in range(3):
            bias_k[part, h, h * AUG + BIAS_LANE + part] = -1.0
            bias_q[part, h, h * AUG + ONE_LANE + part] = 1.0
            bias_k[0, SPARE_LANE, h * AUG + ONE_LANE + part] = 1.0
            bias_q[0, SPARE_LANE, h * AUG + BIAS_LANE + part] = 1.0
    as_bf = lambda a: jnp.asarray(a, BF16)
    return dict(place=as_bf(place), place_t=as_bf(place.T), bias_k=as_bf(bias_k),
                bias_q_t=as_bf(bias_q.transpose(0, 2, 1)))


def _pre_attn_fwd(x, g1, wqkv, wf, wu, bpad, wpool, lay, own_block):
    s, d = x.shape
    nt = s // TS
    sub = TS // TQ

    def body(x_ref, g_ref, wqkv_ref, wf_ref, wu_ref, b_ref, wp_ref, place_ref, place_t_ref, bk_ref, bqt_ref, own_ref,
             hn_ref, qt_ref, ka_ref, v_ref, qat_ref, vt_ref, kt_ref, fl_ref, y_ref, mp_ref, all_ref,
             ubuf, ccar, cbuf, stage, send_sems, recv_sems, local_sem):
        i = pl.program_id(0)

        @pl.when(i == 0)
        def _():
            _gather_start(own_ref, all_ref, stage, send_sems, recv_sems, local_sem)
            ubuf[0:HALO, :] = jnp.zeros((HALO, D_POOL), F32)
            ccar[...] = jnp.zeros_like(ccar)

        @pl.when(i == max(nt - 2, 0))
        def _():
            _gather_pass_on(all_ref, send_sems, recv_sems)

        xv = x_ref[...]
        hn = (xv * _rstd(xv) * g_ref[...]).astype(BF16)
        hn_ref[...] = hn
        zq = _nt(hn, wqkv_ref[...])
        qt = (zq[:, 0:D_ATTN] * 0.125).astype(BF16).T
        qb = (zq[:, 0:D_ATTN] * (0.125 * LOG2E)).astype(BF16)
        kb = zq[:, D_ATTN:2 * D_ATTN].astype(BF16)
        vb = zq[:, 2 * D_ATTN:3 * D_ATTN].astype(BF16)
        v_ref[...] = vb

        fl = _nt(hn, wf_ref[...]) + b_ref[...]
        fl_ref[...] = fl
        logf = jax.nn.log_sigmoid(fl)
        rr = lax.broadcasted_iota(jnp.int32, (TS, TS), 0)
        cc = lax.broadcasted_iota(jnp.int32, (TS, TS), 1)
        c = _mask_matmul((cc <= rr).astype(BF16), logf) + ccar[...]
        cbuf[...] = c
        ccar[...] = cbuf[TS - 1:TS, :]
        hi, mid, lo = _split3(c * LOG2E)
        lane = lax.broadcasted_iota(jnp.int32, (TS, LANES), 1)
        parts = (jnp.where(lane == SPARE_LANE, 1.0, hi).astype(BF16), mid, lo)
        ka = _nn(kb, place_ref[...])
        qat = _nt(place_t_ref[...], qb)
        for part in range(3):
            ka = ka + _nn(parts[part], bk_ref[part])
            qat = qat + _nt(bqt_ref[part], parts[part])
        ka_ref[...] = ka.astype(BF16)
        qat = qat.astype(BF16)
        vt = vb.T
        kt = kb.T
        for a in range(sub):
            cols = slice(a * TQ, (a + 1) * TQ)
            qat_ref[a] = qat[:, cols]
            for ref, mat in ((qt_ref, qt), (kt_ref, kt), (vt_ref, vt)):
                for h in range(HEADS):
                    ref[a, h * VROWS:h * VROWS + HEAD_DIM, :] = mat[h * HEAD_DIM:(h + 1) * HEAD_DIM, cols]
                    ref[a, h * VROWS + HEAD_DIM:(h + 1) * VROWS, :] = jnp.ones((VROWS - HEAD_DIM, TQ), BF16)

        u = _nt(hn, wu_ref[...])
        ubuf[HALO:HALO + TS, :] = u
        t = i * TS + lax.broadcasted_iota(jnp.int32, (TS, 1), 0)
        for g, w in enumerate(POOL_WINDOWS):
            cols = slice(g * POOL_CH, (g + 1) * POOL_CH)
            sm = ubuf[:, cols]
            step = 1
            while step < w:
                sm = sm + pltpu.roll(sm, step, 0)
                step *= 2
            cnt = jnp.minimum(t + 1, w).astype(F32)
            yg = (sm[HALO:, :] / cnt - u[:, cols]).astype(BF16)
            y_ref[:, cols] = yg
            mp_ref[:, cols] = _nn(yg, wp_ref[g])
        ubuf[0:HALO, :] = u[TS - HALO:, :]

        @pl.when(i == nt - 1)
        def _():
            _gather_finish(own_ref, all_ref, send_sems, recv_sems)

    nq = s // TQ
    aug = HEADS * AUG
    outs = (
        jax.ShapeDtypeStruct((s, d), BF16), jax.ShapeDtypeStruct((nq, HEADS * VROWS, TQ), BF16),
        jax.ShapeDtypeStruct((s, aug), BF16), jax.ShapeDtypeStruct((s, D_ATTN), BF16),
        jax.ShapeDtypeStruct((nq, aug, TQ), BF16), jax.ShapeDtypeStruct((nq, HEADS * VROWS, TQ), BF16),
        jax.ShapeDtypeStruct((nq, HEADS * VROWS, TQ), BF16),
        jax.ShapeDtypeStruct((s, LANES), F32),
        jax.ShapeDtypeStruct((s, D_POOL), BF16), jax.ShapeDtypeStruct((s, D_POOL), F32),
        jax.ShapeDtypeStruct((N_DEV,) + own_block.shape, own_block.dtype),
    )
    fixed3 = lambda i: (0, 0, 0)
    tiles3 = lambda rows: pl.BlockSpec((sub, rows, TQ), lambda i: (i, 0, 0))
    return pl.pallas_call(
        body, grid=(nt,), out_shape=outs, name="pre_attn_fwd",
        in_specs=[pl.BlockSpec((TS, d), _row), pl.BlockSpec((1, d), _fixed),
                  pl.BlockSpec((3 * D_ATTN, d), _fixed), pl.BlockSpec(wf.shape, _fixed), pl.BlockSpec(wu.shape, _fixed),
                  pl.BlockSpec((1, LANES), _fixed), pl.BlockSpec(wpool.shape, fixed3),
                  pl.BlockSpec(lay["place"].shape, _fixed), pl.BlockSpec(lay["place_t"].shape, _fixed),
                  pl.BlockSpec(lay["bias_k"].shape, fixed3), pl.BlockSpec(lay["bias_q_t"].shape, fixed3), ANY],
        out_specs=(pl.BlockSpec((TS, d), _row), tiles3(HEADS * VROWS),
                   pl.BlockSpec((TS, aug), _row), pl.BlockSpec((TS, D_ATTN), _row),
                   tiles3(aug), tiles3(HEADS * VROWS), tiles3(HEADS * VROWS),
                   pl.BlockSpec((TS, LANES), _row),
                   pl.BlockSpec((TS, D_POOL), _row), pl.BlockSpec((TS, D_POOL), _row), ANY),
        scratch_shapes=[pltpu.VMEM((TS + HALO, D_POOL), F32), pltpu.VMEM((1, LANES), F32), pltpu.VMEM((TS, LANES), F32),
                        pltpu.VMEM(own_block.shape, own_block.dtype),
                        pltpu.SemaphoreType.DMA((7,)), pltpu.SemaphoreType.DMA((7,)), pltpu.SemaphoreType.DMA],
        compiler_params=_params(1),
    )(x, g1, wqkv, wf, wu, bpad, wpool, lay["place"], lay["place_t"], lay["bias_k"], lay["bias_q_t"], own_block)


def _causal_in_tile():
    krow = lax.broadcasted_iota(jnp.int32, (TQ, TQ), 0)
    qcol = lax.broadcasted_iota(jnp.int32, (TQ, TQ), 1)
    return krow <= qcol


def _attn_fwd(ka, qat3, vt3, own_block):
    s = ka.shape[0]
    nq = s // TQ
    pass_on_step = max(nq - 2, 0)

    def body(qa_ref, ka_ref, vt_ref, own_ref, a_ref, lset_ref, all_ref, acc, out_t, st_scr, pt_scr, m_scr,
             stage, send_sems, recv_sems, local_sem):
        i = pl.program_id(0)

        @pl.when(i == 0)
        def _():
            _gather_start(own_ref, all_ref, stage, send_sems, recv_sems, local_sem)

        @pl.when(i == pass_on_step)
        def _():
            _gather_pass_on(all_ref, send_sems, recv_sems)

        acc[...] = jnp.zeros_like(acc)
        m_scr[...] = jnp.full(m_scr.shape, NEG, F32)

        def tiles(js, masked):
            tile_max = []
            for h in range(HEADS):
                aug = slice(h * AUG, (h + 1) * AUG)
                for t, j in enumerate(js):
                    st = _nn(ka_ref[pl.ds(j * TQ, TQ), aug], qa_ref[0, aug, :])
                    if masked and t == len(js) - 1:
                        st = jnp.where(_causal_in_tile(), st, NEG)
                    st_scr[t * HEADS + h] = st
                    mx = jnp.max(st, axis=0, keepdims=True)
                    top = mx if t == 0 else jnp.maximum(top, mx)
                tile_max.append(top)
            scale = []
            for h in range(HEADS):
                m_old = m_scr[h:h + 1, :]
                m_new = jnp.maximum(m_old, tile_max[h])
                m_scr[h:h + 1, :] = m_new
                scale.append(jnp.exp2(m_old - m_new))
                for t in range(len(js)):
                    pt_scr[t * HEADS + h] = jnp.exp2(st_scr[t * HEADS + h] - m_new).astype(BF16)
            for h in range(HEADS):
                rows = slice(h * VROWS, (h + 1) * VROWS)
                upd = _nn(vt_ref[js[0], rows, :], pt_scr[h])
                for t in range(1, len(js)):
                    upd = upd + _nn(vt_ref[js[t], rows, :], pt_scr[t * HEADS + h])
                acc[rows, :] = scale[h] * acc[rows, :] + upd

        def pair(p, carry):
            tiles([2 * p, 2 * p + 1], False)
            return carry

        lax.fori_loop(0, i // 2, pair, 0)

        @pl.when(i % 2 == 1)
        def _():
            tiles([i - 1, i], True)

        @pl.when(i % 2 == 0)
        def _():
            tiles([i], True)

        for h in range(HEADS):
            denom = acc[h * VROWS + HEAD_DIM:h * VROWS + HEAD_DIM + 1, :]
            out_t[h * HEAD_DIM:(h + 1) * HEAD_DIM, :] = acc[h * VROWS:h * VROWS + HEAD_DIM, :] / denom
            lset_ref[0, h:h + 1, :] = m_scr[h:h + 1, :] + jnp.log2(denom)
        a_ref[...] = out_t[...].T

        @pl.when(i == nq - 1)
        def _():
            _gather_finish(own_ref, all_ref, send_sems, recv_sems)

    r, cdim = own_block.shape
    return pl.pallas_call(
        body, grid=(nq,), name="attn_fwd",
        out_shape=(jax.ShapeDtypeStruct((s, D_ATTN), F32), jax.ShapeDtypeStruct((nq, HEADS, TQ), F32),
                   jax.ShapeDtypeStruct((N_DEV, r, cdim), own_block.dtype)),
        in_specs=[pl.BlockSpec((1, HEADS * AUG, TQ), lambda i: (i, 0, 0)), VMEM_WHOLE, VMEM_WHOLE, ANY],
        out_specs=(pl.BlockSpec((TQ, D_ATTN), _row), pl.BlockSpec((1, HEADS, TQ), lambda i: (i, 0, 0)), ANY),
        scratch_shapes=[pltpu.VMEM((HEADS * VROWS, TQ), F32), pltpu.VMEM((D_ATTN, TQ), F32),
                        pltpu.VMEM((2 * HEADS, TQ, TQ), F32), pltpu.VMEM((2 * HEADS, TQ, TQ), BF16),
                        pltpu.VMEM((HEADS, TQ), F32), pltpu.VMEM((r, cdim), own_block.dtype),
                        pltpu.SemaphoreType.DMA((7,)), pltpu.SemaphoreType.DMA((7,)), pltpu.SemaphoreType.DMA],
        compiler_params=_params(1),
    )(qat3, ka, vt3, own_block)


def _post_attn_fwd(a, mpre, x, g_attn, g_pool, pscale, wout, g_post, g_ffn_pre):
    s, d = x.shape

    def body(a_ref, mp_ref, x_ref, ga_ref, gp_ref, ps_ref, wo_ref, gpost_ref, gpre_ref,
             mix_ref, o_ref, h1_ref, hn2_ref):
        for rows in _HALVES:
            av = a_ref[rows, :]
            mix_ref[rows, 0:D_ATTN] = (av * _rstd(av) * ga_ref[...]).astype(BF16)
            mv = mp_ref[rows, :] * ps_ref[...]
            mix_ref[rows, D_ATTN:] = (mv * _rstd(mv) * gp_ref[...]).astype(BF16)
            o = _nn(mix_ref[rows, :], wo_ref[...].reshape(d, d))
            o_ref[rows, :] = o
            h1 = x_ref[rows, :] + o * _rstd(o) * gpost_ref[...]
            h1_ref[rows, :] = h1
            hn2_ref[rows, :] = (h1 * _rstd(h1) * gpre_ref[...]).astype(BF16)

    vec = lambda n: pl.BlockSpec((1, n), _fixed)
    return pl.pallas_call(
        body, grid=(s // TS,), name="post_attn_fwd",
        out_shape=(jax.ShapeDtypeStruct((s, d), BF16), jax.ShapeDtypeStruct((s, d), F32),
                   jax.ShapeDtypeStruct((s, d), F32), jax.ShapeDtypeStruct((s, d), BF16)),
        in_specs=[pl.BlockSpec((TS, D_ATTN), _row), pl.BlockSpec((TS, D_POOL), _row), pl.BlockSpec((TS, d), _row),
                  vec(D_ATTN), vec(D_POOL), vec(D_POOL), _spec_square(0), vec(d), vec(d)],
        out_specs=(pl.BlockSpec((TS, d), _row),) * 4,
        compiler_params=_params(1),
    )(a, mpre, x, g_attn, g_pool, pscale, wout, g_post, g_ffn_pre)


def _ffn_fwd(hn2, wg, wu, wd, h1, g_post):
    s, d = h1.shape
    nc = D_FF // TN_FF
    ts = min(TS_FF, s)

    def body(hn_ref, wg_ref, wu_ref, wd_ref, h1_ref, g_ref, gate_ref, up_ref, act_ref, ff_ref, h2_ref, acc):
        j = pl.program_id(1)

        @pl.when(j == 0)
        def _():
            acc[...] = jnp.zeros_like(acc)

        for r in range(2):
            rows = slice(r * (ts // 2), (r + 1) * (ts // 2))
            hn = hn_ref[rows, :]
            gt = _nt(hn, wg_ref[...].reshape(TN_FF, d))
            up = _nt(hn, wu_ref[...].reshape(TN_FF, d))
            gate_ref[rows, :] = gt.astype(BF16)
            up_ref[rows, :] = up.astype(BF16)
            act_ref[rows, :] = (gt * jax.nn.sigmoid(gt) * up).astype(BF16)
            acc[rows, :] += _nn(act_ref[rows, :], wd_ref[...].reshape(TN_FF, d))

        @pl.when(j == nc - 1)
        def _():
            ff = acc[...]
            ff_ref[...] = ff
            h2_ref[...] = h1_ref[...] + ff * _rstd(ff) * g_ref[...]

    rowblk = pl.BlockSpec((ts, d), lambda i, j: (i, 0))
    chunk = pl.BlockSpec((ts, TN_FF), lambda i, j: (i, j))
    return pl.pallas_call(
        body, grid=(s // ts, nc), name="ffn_fwd",
        out_shape=(jax.ShapeDtypeStruct((s, D_FF), BF16),) * 3 + (jax.ShapeDtypeStruct((s, d), F32),) * 2,
        in_specs=[rowblk, _spec_ff(0), _spec_ff(1), _spec_ff(2), rowblk, pl.BlockSpec((1, d), lambda i, j: (0, 0))],
        out_specs=(chunk, chunk, chunk, rowblk, rowblk),
        scratch_shapes=[pltpu.VMEM((ts, d), F32)],
        compiler_params=_params(2),
    )(hn2, wg, wu, wd, h1, g_post)


def _tail_fwd_bwd(h2, p, tgt, ff, wple, wpg, g_ple, g_ffn_post):
    s, d = h2.shape

    def body(h2_ref, p_ref, t_ref, ff_ref, wple_ref, wpg_ref, gple_ref, gfp_ref,
             dh2_ref, dff_ref, dgl_ref, dpp_ref, h2b_ref, pb_ref, loss_ref, dgple_ref, dgfp_ref):
        i = pl.program_id(0)

        @pl.when(i == 0)
        def _():
            loss_ref[...] = jnp.zeros_like(loss_ref)
            dgple_ref[...] = jnp.zeros_like(dgple_ref)
            dgfp_ref[...] = jnp.zeros_like(dgfp_ref)

        h2 = h2_ref[...]
        h2b = h2.astype(BF16)
        h2b_ref[...] = h2b
        pb = p_ref[...].astype(BF16)
        pb_ref[...] = pb
        pp = _nt(pb, wple_ref[...])
        gple = gple_ref[...]
        e = pp * _rstd(pp) * gple
        wpg = wpg_ref[...].reshape(d, d)
        sg = jax.nn.sigmoid(_nn(h2b, wpg))
        diff = h2 + sg * e - t_ref[...]
        sq = jnp.sum(jnp.sum(diff * diff, axis=1, keepdims=True), axis=0, keepdims=True)
        loss_ref[...] += jnp.broadcast_to(sq * (0.5 / d), loss_ref.shape)
        dh3 = diff * (1.0 / d)
        dgl = (dh3 * e * sg * (1.0 - sg)).astype(BF16)
        dgl_ref[...] = dgl
        dh2 = dh3 + _nt(dgl, wpg)
        dh2_ref[...] = dh2
        dpp, dg = _rms_bwd(pp, gple, dh3 * sg)
        dpp_ref[...] = dpp.astype(BF16)
        dgple_ref[...] += dg
        dff, dg = _rms_bwd(ff_ref[...], gfp_ref[...], dh2)
        dff_ref[...] = dff.astype(BF16)
        dgfp_ref[...] += dg

    rowblk = pl.BlockSpec((TS, d), _row)
    vec = pl.BlockSpec((1, d), _fixed)
    return pl.pallas_call(
        body, grid=(s // TS,), name="tail_fwd_bwd",
        out_shape=(jax.ShapeDtypeStruct((s, d), F32), jax.ShapeDtypeStruct((s, d), BF16),
                   jax.ShapeDtypeStruct((s, d), BF16), jax.ShapeDtypeStruct((s, d), BF16),
                   jax.ShapeDtypeStruct((s, d), BF16), jax.ShapeDtypeStruct((s, D_PLE), BF16),
                   jax.ShapeDtypeStruct((8, LANES), F32), jax.ShapeDtypeStruct((1, d), F32),
                   jax.ShapeDtypeStruct((1, d), F32)),
        in_specs=[rowblk, pl.BlockSpec((TS, D_PLE), _row), rowblk, rowblk,
                  pl.BlockSpec(wple.shape, _fixed), _spec_square(1), vec, vec],
        out_specs=(rowblk, rowblk, rowblk, rowblk, rowblk, pl.BlockSpec((TS, D_PLE), _row),
                   pl.BlockSpec((8, LANES), _fixed), vec, vec),
        compiler_params=_params(1),
    )(h2, p, tgt, ff, wple, wpg, g_ple, g_ffn_post)


def _ffn_bwd(dff, gate, up, wd, wg, wu, h1, dh2, g_pre):
    s, d = h1.shape
    nc = D_FF // TN_FF
    ts = min(TS_FF, s)

    def body(dff_ref, gate_ref, up_ref, wd_ref, wg_ref, wu_ref, h1_ref, dh2_ref, g_ref,
             dgate_ref, dup_ref, dh1_ref, dg_ref, acc):
        i = pl.program_id(0)
        j = pl.program_id(1)

        @pl.when((i == 0) & (j == 0))
        def _():
            dg_ref[...] = jnp.zeros_like(dg_ref)

        @pl.when(j == 0)
        def _():
            acc[...] = jnp.zeros_like(acc)

        for r in range(2):
            rows = slice(r * (ts // 2), (r + 1) * (ts // 2))
            dact = _nt(dff_ref[rows, :], wd_ref[...].reshape(TN_FF, d))
            gt = gate_ref[rows, :].astype(F32)
            sg = jax.nn.sigmoid(gt)
            dup_ref[rows, :] = (dact * gt * sg).astype(BF16)
            dgate_ref[rows, :] = (dact * up_ref[rows, :].astype(F32) * (sg * (1.0 + gt * (1.0 - sg)))).astype(BF16)
            acc[rows, :] += (_nn(dgate_ref[rows, :], wg_ref[...].reshape(TN_FF, d))
                             + _nn(dup_ref[rows, :], wu_ref[...].reshape(TN_FF, d)))

        @pl.when(j == nc - 1)
        def _():
            dv, dg = _rms_bwd(h1_ref[...], g_ref[...], acc[...])
            dh1_ref[...] = dh2_ref[...] + dv
            dg_ref[...] += dg

    rowblk = pl.BlockSpec((ts, d), lambda i, j: (i, 0))
    chunk = pl.BlockSpec((ts, TN_FF), lambda i, j: (i, j))
    vec = pl.BlockSpec((1, d), lambda i, j: (0, 0))
    return pl.pallas_call(
        body, grid=(s // ts, nc), name="ffn_bwd",
        out_shape=(jax.ShapeDtypeStruct((s, D_FF), BF16), jax.ShapeDtypeStruct((s, D_FF), BF16),
                   jax.ShapeDtypeStruct((s, d), F32), jax.ShapeDtypeStruct((1, d), F32)),
        in_specs=[rowblk, chunk, chunk, _spec_ff(2), _spec_ff(0), _spec_ff(1), rowblk, rowblk, vec],
        out_specs=(chunk, chunk, rowblk, vec),
        scratch_shapes=[pltpu.VMEM((ts, d), F32)],
        compiler_params=_params(2),
    )(dff, gate, up, wd, wg, wu, h1, dh2, g_pre)


def _post_attn_bwd(dh1, o, a, mpre, wout, wpool, g_post, g_attn, g_pool, pscale, send):
    s, d = dh1.shape
    sub = TS // TQ
    npc = len(send)

    def body(dh1_ref, o_ref, a_ref, mp_ref, wo_ref, wp_ref, gpost_ref, ga_ref, gp_ref, ps_ref, *refs):
        send_refs, refs = refs[:npc], refs[npc:]
        dob_ref, dat_ref, dlt_ref, dmpb_ref, dy_ref, dgpost_ref, dga_ref, dgp_ref, dps_ref = refs[:9]
        got_refs, (send_sems, recv_sems) = refs[9:9 + npc], refs[9 + npc:]
        i = pl.program_id(0)

        @pl.when(i == 0)
        def _():
            for cp in _pair_copies(send_refs, got_refs, send_sems, recv_sems):
                cp.start()
            dgpost_ref[...] = jnp.zeros_like(dgpost_ref)
            dga_ref[...] = jnp.zeros_like(dga_ref)
            dgp_ref[...] = jnp.zeros_like(dgp_ref)
            dps_ref[...] = jnp.zeros_like(dps_ref)

        do, dg = _rms_bwd(o_ref[...], gpost_ref[...], dh1_ref[...])
        dgpost_ref[...] += dg
        dob = do.astype(BF16)
        dob_ref[...] = dob
        dmix = _nt(dob, wo_ref[...].reshape(d, d))

        av = a_ref[...]
        da, dg = _rms_bwd(av, ga_ref[...], dmix[:, 0:D_ATTN])
        dga_ref[...] += dg
        dat = da.astype(BF16).T
        hsel = (lax.shift_right_logical(lax.broadcasted_iota(jnp.int32, (HEADS, D_ATTN), 1), 6)
                == lax.broadcasted_iota(jnp.int32, (HEADS, D_ATTN), 0)).astype(F32)
        dlt = lax.dot_general(hsel, da * av, (((1,), (1,)), ((), ())), precision=HIGHEST, preferred_element_type=F32)
        for q in range(sub):
            dlt_ref[q] = dlt[:, q * TQ:(q + 1) * TQ]
            dat_ref[q] = dat[:, q * TQ:(q + 1) * TQ]

        ps = ps_ref[...]
        mp = mp_ref[...]
        dm, dg = _rms_bwd(mp * ps, gp_ref[...], dmix[:, D_ATTN:])
        dgp_ref[...] += dg
        dps_ref[...] += jnp.sum(dm * mp, axis=0, keepdims=True)
        dmpb = (dm * ps).astype(BF16)
        dmpb_ref[...] = dmpb
        for g in range(len(POOL_WINDOWS)):
            cols = slice(g * POOL_CH, (g + 1) * POOL_CH)
            dy_ref[:, cols] = _nt(dmpb[:, cols], wp_ref[g])

        @pl.when(i == s // TS - 1)
        def _():
            for cp in _pair_copies(send_refs, got_refs, send_sems, recv_sems):
                cp.wait()

    rowblk = pl.BlockSpec((TS, d), _row)
    half = pl.BlockSpec((TS, D_ATTN), _row)
    vec = lambda n: pl.BlockSpec((1, n), _fixed)
    nk = N_DEV // 2
    res = pl.pallas_call(
        body, grid=(s // TS,), name="post_attn_bwd",
        out_shape=(jax.ShapeDtypeStruct((s, d), BF16), jax.ShapeDtypeStruct((s // TQ, D_ATTN, TQ), BF16),
                   jax.ShapeDtypeStruct((s // TQ, HEADS, TQ), F32), jax.ShapeDtypeStruct((s, D_POOL), BF16),
                   jax.ShapeDtypeStruct((s, D_POOL), F32), jax.ShapeDtypeStruct((1, d), F32),
                   jax.ShapeDtypeStruct((1, D_ATTN), F32), jax.ShapeDtypeStruct((1, D_POOL), F32),
                   jax.ShapeDtypeStruct((1, D_POOL), F32))
        + tuple(jax.ShapeDtypeStruct((nk,) + t.shape[1:], t.dtype) for t in send),
        in_specs=[rowblk, rowblk, half, half, _spec_square(0),
                  pl.BlockSpec(wpool.shape, lambda i: (0, 0, 0)), vec(d), vec(D_ATTN), vec(D_POOL), vec(D_POOL)]
        + [ANY] * npc,
        out_specs=(rowblk, pl.BlockSpec((sub, D_ATTN, TQ), lambda i: (i, 0, 0)),
                   pl.BlockSpec((sub, HEADS, TQ), lambda i: (i, 0, 0)), half, half,
                   vec(d), vec(D_ATTN), vec(D_POOL), vec(D_POOL)) + (ANY,) * npc,
        scratch_shapes=[pltpu.SemaphoreType.DMA((nk, npc)), pltpu.SemaphoreType.DMA((nk, npc))],
        compiler_params=_params(1),
    )(dh1, o, a, mpre, wout, wpool, g_post, g_attn, g_pool, pscale, *send)
    return res[:9], list(res[9:])


def _attn_bwd(ka, v, kt3, qat3, qt3, dot3, lset3, dlt3, chip_blocks, small_block):
    s = ka.shape[0]
    nq = s // TQ

    def body(ka_ref, v_ref, kt_ref, qat_ref, qt_ref, dot_ref, lset_ref, dlt_ref, b_ref, sm_ref,
             dqt_ref, dkt_ref, dvt_ref, got_ref, all_ref, pt_scr, ptb_scr, dsb_scr,
             stage, send_sems, recv_sems, local_sem, stage_s, send_s, recv_s, local_s):
        j = pl.program_id(0)

        @pl.when(j == 0)
        def _():
            _chips_start(b_ref, got_ref, stage, send_sems, recv_sems, local_sem)
            _gather_start(sm_ref, all_ref, stage_s, send_s, recv_s, local_s)
            dqt_ref[...] = jnp.zeros_like(dqt_ref)

        @pl.when(j == max(nq - 2, 0))
        def _():
            _gather_pass_on(all_ref, send_s, recv_s)

        def tile(i, masked):
            def accumulate(ref, idx, val):
                if masked:
                    ref[idx] = val
                else:
                    ref[idx] += val

            for h in range(HEADS):
                aug = slice(h * AUG, (h + 1) * AUG)
                st = _nn(ka_ref[:, aug], qat_ref[i, aug, :]) - lset_ref[i, h:h + 1, :]
                if masked:
                    st = jnp.where(_causal_in_tile(), st, NEG)
                pt = jnp.exp2(st)
                pt_scr[h] = pt
                ptb_scr[h] = pt.astype(BF16)
            heads = [(h, slice(h * HEAD_DIM, (h + 1) * HEAD_DIM)) for h in range(HEADS)]
            for h, hs in heads:
                dst = pt_scr[h] * (_nn(v_ref[:, hs], dot_ref[i, hs, :]) - dlt_ref[i, h:h + 1, :])
                dsb_scr[h] = dst.astype(BF16)
            for h, hs in heads:
                accumulate(dvt_ref, (0, hs, slice(None)), _nt(dot_ref[i, hs, :], ptb_scr[h]))
            for h, hs in heads:
                rows = slice(h * VROWS, (h + 1) * VROWS)
                accumulate(dkt_ref, (0, rows, slice(None)), _nt(qt_ref[i, rows, :], dsb_scr[h]))
            for h, hs in heads:
                rows = slice(h * VROWS, (h + 1) * VROWS)
                dqt_ref[i, rows, :] += _nn(kt_ref[0, rows, :], dsb_scr[h])

        first = j + 1
        pairs = (nq - first) // 2

        def step(p, carry):
            tile(first + 2 * p, False)
            tile(first + 2 * p + 1, False)
            return carry

        tile(j, True)
        lax.fori_loop(0, pairs, step, 0)

        @pl.when(first + 2 * pairs < nq)
        def _():
            tile(nq - 1, False)

        @pl.when(j == nq - 1)
        def _():
            _chips_finish(b_ref, got_ref, send_sems, recv_sems)
            _gather_finish(sm_ref, all_ref, send_s, recv_s)

    blk = pl.BlockSpec((TQ, D_ATTN), _row)
    tile_t = lambda rows: pl.BlockSpec((1, rows, TQ), lambda j: (j, 0, 0))
    per_tile = lambda rows: jax.ShapeDtypeStruct((nq, rows, TQ), F32)
    _, r, cdim = chip_blocks.shape
    dma = pltpu.SemaphoreType.DMA
    return pl.pallas_call(
        body, grid=(nq,), name="attn_bwd",
        out_shape=(per_tile(HEADS * VROWS), per_tile(HEADS * VROWS), per_tile(D_ATTN),
                   jax.ShapeDtypeStruct(chip_blocks.shape, chip_blocks.dtype),
                   jax.ShapeDtypeStruct((N_DEV,) + small_block.shape, small_block.dtype)),
        in_specs=[pl.BlockSpec((TQ, HEADS * AUG), _row), blk, tile_t(HEADS * VROWS),
                  VMEM_WHOLE, VMEM_WHOLE, VMEM_WHOLE, VMEM_WHOLE, VMEM_WHOLE, ANY, ANY],
        out_specs=(pl.BlockSpec((nq, HEADS * VROWS, TQ), lambda j: (0, 0, 0)), tile_t(HEADS * VROWS), tile_t(D_ATTN),
                   ANY, ANY),
        scratch_shapes=[pltpu.VMEM((HEADS, TQ, TQ), F32), pltpu.VMEM((HEADS, TQ, TQ), BF16),
                        pltpu.VMEM((HEADS, TQ, TQ), BF16), pltpu.VMEM((r, cdim), chip_blocks.dtype),
                        dma((3,)), dma((3,)), dma,
                        pltpu.VMEM(small_block.shape, small_block.dtype), dma((7,)), dma((7,)), dma],
        compiler_params=_params(1),
    )(ka, v, kt3, qat3, qt3, dot3, lset3, dlt3, chip_blocks, small_block)


def _pre_attn_bwd(dqt3, dkt3, dvt3, fl, dy, x, dh1, g1, wqkv, wf, wu):
    s, d = x.shape
    nt = s // TS
    n = TS + HALO
    sub = TS // TQ
    qkv, fcols = 3 * D_ATTN, 3 * D_ATTN + LANES

    def body(dqt_ref, dkt_ref, dvt_ref, fl_ref, dy_ref, x_ref, dh1_ref, g_ref, wqkv_ref, wf_ref, wu_ref,
             gx_ref, dz_ref, dg_ref, db_ref, ybuf, ccar, dlog, dsum):
        dqkv_ref = dz_ref.at[:, 0:qkv]
        dfb_ref = dz_ref.at[:, qkv:fcols]
        dub_ref = dz_ref.at[:, fcols:]
        i = pl.program_id(0)
        ti = nt - 1 - i

        @pl.when(i == 0)
        def _():
            ybuf[TS:n, :] = jnp.zeros((HALO, D_POOL), F32)
            ccar[...] = jnp.zeros_like(ccar)
            dg_ref[...] = jnp.zeros_like(dg_ref)
            db_ref[...] = jnp.zeros_like(db_ref)
            dsum[...] = jnp.zeros_like(dsum)

        for a in range(sub):
            for h in range(HEADS):
                r = h * VROWS + HEAD_DIM
                dsum[h:h + 1, a * TQ:(a + 1) * TQ] = dqt_ref[a, r:r + 1, :] - dkt_ref[a, r:r + 1, :]
        rr = lax.broadcasted_iota(jnp.int32, (TS, TS), 0)
        cc = lax.broadcasted_iota(jnp.int32, (TS, TS), 1)
        dlog[...] = ccar[...] + _mask_matmul((cc >= rr).astype(BF16), dsum[...].T)
        ccar[...] = dlog[0:1, :]
        df = dlog[...] * jax.nn.sigmoid(-fl_ref[...])
        db_ref[...] += jnp.sum(df, axis=0, keepdims=True)
        dfb = df.astype(BF16)
        dfb_ref[...] = dfb

        t = ti * TS + lax.broadcasted_iota(jnp.int32, (TS, 1), 0)
        dy = dy_ref[...]
        for g, w in enumerate(POOL_WINDOWS):
            cols = slice(g * POOL_CH, (g + 1) * POOL_CH)
            ybuf[0:TS, cols] = dy[:, cols] / jnp.minimum(t + 1, w).astype(F32)
        for g, w in enumerate(POOL_WINDOWS):
            cols = slice(g * POOL_CH, (g + 1) * POOL_CH)
            sm = ybuf[:, cols]
            step = 1
            while step < w:
                sm = sm + pltpu.roll(sm, n - step, 0)
                step *= 2
            dub_ref[:, cols] = (sm[0:TS, :] - dy[:, cols]).astype(BF16)
        ybuf[TS:n, :] = ybuf[0:HALO, :]

        for a in range(sub):
            rows = slice(a * TQ, (a + 1) * TQ)
            for h in range(HEADS):
                src = slice(h * VROWS, h * VROWS + HEAD_DIM)
                dqkv_ref[rows, h * HEAD_DIM:(h + 1) * HEAD_DIM] = (dqt_ref[a, src, :].T * 0.125).astype(BF16)
                dqkv_ref[rows, D_ATTN + h * HEAD_DIM:D_ATTN + (h + 1) * HEAD_DIM] = dkt_ref[a, src, :].T.astype(BF16)
            dqkv_ref[rows, 2 * D_ATTN:] = dvt_ref[a].T.astype(BF16)
        dhn = _nn(dqkv_ref[...], wqkv_ref[...]) + _nn(dfb, wf_ref[...]) + _nn(dub_ref[...], wu_ref[...])
        dx, dg = _rms_bwd(x_ref[...], g_ref[...], dhn)
        gx_ref[...] = dh1_ref[...] + dx
        dg_ref[...] += dg

    rev = lambda i: (nt - 1 - i, 0)
    blk = lambda w: pl.BlockSpec((TS, w), rev)
    return pl.pallas_call(
        body, grid=(nt,), name="pre_attn_bwd",
        out_shape=(jax.ShapeDtypeStruct((s, d), F32), jax.ShapeDtypeStruct((s, fcols + D_POOL), BF16),
                   jax.ShapeDtypeStruct((1, d), F32), jax.ShapeDtypeStruct((1, LANES), F32)),
        in_specs=[pl.BlockSpec((sub, HEADS * VROWS, TQ), lambda i: (nt - 1 - i, 0, 0)),
                  pl.BlockSpec((sub, HEADS * VROWS, TQ), lambda i: (nt - 1 - i, 0, 0)),
                  pl.BlockSpec((sub, D_ATTN, TQ), lambda i: (nt - 1 - i, 0, 0)),
                  blk(LANES), blk(D_POOL), blk(d), blk(d),
                  pl.BlockSpec((1, d), _fixed), pl.BlockSpec((qkv, d), _fixed), pl.BlockSpec(wf.shape, _fixed),
                  pl.BlockSpec(wu.shape, _fixed)],
        out_specs=(blk(d), blk(fcols + D_POOL), pl.BlockSpec((1, d), _fixed), pl.BlockSpec((1, LANES), _fixed)),
        scratch_shapes=[pltpu.VMEM((n, D_POOL), F32), pltpu.VMEM((1, LANES), F32), pltpu.VMEM((TS, LANES), F32),
                        pltpu.VMEM((LANES, TS), F32)],
        compiler_params=_params(1),
    )(dqt3, dkt3, dvt3, fl, dy, x, dh1, g1, wqkv, wf, wu)


def _wgrad(a, b, out_dtype, name):
    s, m = a.shape
    n = b.shape[1]
    tm = max(t for t in range(LANES, min(m, TM_WGRAD) + 1, LANES) if m % t == 0)
    ts = min(TS_WGRAD, s)
    ns = s // ts

    def body(a_ref, b_ref, o_ref, acc):
        i = pl.program_id(1)

        @pl.when(i == 0)
        def _():
            acc[...] = jnp.zeros_like(acc)

        acc[...] += _tn(a_ref[...], b_ref[...])

        @pl.when(i == ns - 1)
        def _():
            o_ref[...] = acc[...].astype(out_dtype)

    return pl.pallas_call(
        body, grid=(m // tm, ns), name=name, out_shape=jax.ShapeDtypeStruct((m, n), out_dtype),
        in_specs=[pl.BlockSpec((ts, tm), lambda j, i: (i, j)), pl.BlockSpec((ts, n), lambda j, i: (i, 0))],
        out_specs=pl.BlockSpec((tm, n), lambda j, i: (j, 0)),
        scratch_shapes=[pltpu.VMEM((tm, n), F32)],
        compiler_params=_params(2),
    )(a, b)


def _wgrad_in(dz, hn):
    s, m = dz.shape
    n = hn.shape[1]
    ts = min(TS_WGRAD, s)
    ns = s // ts
    pad_at, pad = 3 * D_ATTN + HEADS, LANES - HEADS
    assert m == D_IN + pad and N_DEV * SHARD_IN == D_IN

    def pieces(d):
        lo, hi = d * SHARD_IN, (d + 1) * SHARD_IN
        spans = [(lo, min(hi, pad_at), 0), (max(lo, pad_at), hi, pad)]
        return [(a + shift, b - a, a - lo) for a, b, shift in spans if b > a]

    def body(a_ref, b_ref, o_ref, acc, stage):
        i = pl.program_id(0)

        @pl.when(i == 0)
        def _():
            acc[...] = jnp.zeros_like(acc)

        acc[...] += _tn(a_ref[...], b_ref[...])

        @pl.when(i == ns - 1)
        def _():
            stage[SHARD_IN:ROWS_IN, :] = jnp.zeros((ROWS_IN - SHARD_IN, n), F32)
            for d in range(N_DEV):
                for src, rows, dst in pieces(d):
                    stage[dst:dst + rows, :] = acc[src:src + rows, :]
                o_ref[d] = stage[...].astype(BF16)

    return pl.pallas_call(
        body, grid=(ns,), name="wgrad_in", out_shape=jax.ShapeDtypeStruct((N_DEV, ROWS_IN, n), BF16),
        in_specs=[pl.BlockSpec((ts, m), _row), pl.BlockSpec((ts, n), _row)],
        out_specs=pl.BlockSpec((N_DEV, ROWS_IN, n), lambda i: (0, 0, 0)),
        scratch_shapes=[pltpu.VMEM((m, n), F32), pltpu.VMEM((ROWS_IN, n), F32)],
        compiler_params=_params(1),
    )(dz, hn)


def _adamw(w, g, m, v):
    m = ADAM_B1 * m + (1.0 - ADAM_B1) * g
    v = ADAM_B2 * v + (1.0 - ADAM_B2) * (g * g)
    m_hat = m / (1.0 - ADAM_B1 ** ADAM_STEP)
    v_hat = v / (1.0 - ADAM_B2 ** ADAM_STEP)
    delta = -ADAM_LR * (m_hat / (jnp.sqrt(v_hat) + ADAM_EPS) + ADAM_WD * w)
    return delta, m, v


def _sum_update(p_ref, w_ref, m_ref, v_ref, g_ref, d_ref, nm_ref, nv_ref):
    g = p_ref[0].astype(F32)
    for k in range(1, p_ref.shape[0]):
        g = g + p_ref[k].astype(F32)
    g_ref[...] = g
    d_ref[...], nm_ref[...], nv_ref[...] = _adamw(w_ref[...], g, m_ref[...], v_ref[...])


def _reduce_update_rest(parts, w, m, v, chip_blocks, small_block):
    nk, r, c = parts.shape
    ns = r // TR_REST

    def body(p_ref, w_ref, m_ref, v_ref, b_ref, sm_ref, g_ref, d_ref, nm_ref, nv_ref, got_ref, all_ref,
             stage_b, stage_s, send_b, recv_b, local_b, send_s, recv_s, local_s):
        i = pl.program_id(0)

        @pl.when(i == 0)
        def _():
            _chips_start(b_ref, got_ref, stage_b, send_b, recv_b, local_b)
            _gather_start(sm_ref, all_ref, stage_s, send_s, recv_s, local_s)

        _sum_update(p_ref, w_ref, m_ref, v_ref, g_ref, d_ref, nm_ref, nv_ref)

        @pl.when(i == ns - 1)
        def _():
            _gather_pass_on(all_ref, send_s, recv_s)
            _chips_finish(b_ref, got_ref, send_b, recv_b)
            _gather_finish(sm_ref, all_ref, send_s, recv_s)

    blk = pl.BlockSpec((TR_REST, c), _row)
    out = jax.ShapeDtypeStruct((r, c), F32)
    dma = pltpu.SemaphoreType.DMA
    return pl.pallas_call(
        body, grid=(ns,), name="reduce_update_rest",
        out_shape=(out,) * 4 + (jax.ShapeDtypeStruct(chip_blocks.shape, chip_blocks.dtype),
                                jax.ShapeDtypeStruct((N_DEV,) + small_block.shape, small_block.dtype)),
        in_specs=[pl.BlockSpec((nk, TR_REST, c), lambda i: (0, i, 0)), blk, blk, blk, ANY, ANY],
        out_specs=(blk,) * 4 + (ANY, ANY),
        scratch_shapes=[pltpu.VMEM(chip_blocks.shape[1:], chip_blocks.dtype), pltpu.VMEM(small_block.shape, small_block.dtype),
                        dma((3,)), dma((3,)), dma, dma((7,)), dma((7,)), dma],
        compiler_params=_params(1),
    )(parts, w, m, v, chip_blocks, small_block)


def _reduce_update_big(parts, w, m, v, tr, name):
    nk, r, c = parts.shape

    def body(p_ref, w_ref, m_ref, v_ref, g_ref, d_ref, nm_ref, nv_ref):
        _sum_update(p_ref, w_ref, m_ref, v_ref, g_ref, d_ref, nm_ref, nv_ref)

    blk = pl.BlockSpec((tr, c), _row)
    out = jax.ShapeDtypeStruct((r, c), F32)
    return pl.pallas_call(
        body, grid=(r // tr,), name=name, out_shape=(out,) * 4,
        in_specs=[pl.BlockSpec((nk, tr, c), lambda i: (0, i, 0)), blk, blk, blk],
        out_specs=(blk,) * 4, compiler_params=_params(1),
    )(parts, w, m, v)


def _reduce_update_small(parts, late, w, m, v):
    nd = parts.shape[0]
    first = parts.shape[1] - late.shape[1]

    def body(p_ref, q_ref, w_ref, m_ref, v_ref, g_ref, d_ref, nm_ref, nv_ref):
        g, t = p_ref[0], q_ref[0]
        for k in range(1, nd):
            g, t = g + p_ref[k], t + q_ref[k]
        g_ref[...] = g
        g_ref[first:, :] = g[first:, :] + t
        d_ref[...], nm_ref[...], nv_ref[...] = _adamw(w_ref[...], g_ref[...], m_ref[...], v_ref[...])

    out = jax.ShapeDtypeStruct(w.shape, F32)
    return pl.pallas_call(body, name="reduce_update_small", out_shape=(out,) * 4,
                          compiler_params=pltpu.CompilerParams(vmem_limit_bytes=VMEM_LIMIT))(parts, late, w, m, v)


MESH = pl.DeviceIdType.MESH


def _copy_through_vmem(src_hbm, dst_hbm, stage, sem):
    load = pltpu.make_async_copy(src_hbm, stage, sem)
    load.start()
    load.wait()
    store = pltpu.make_async_copy(stage, dst_hbm, sem)
    store.start()
    store.wait()


class _GatherPlan:
    def __init__(self, x_ref, out_ref, send_sems, recv_sems):
        x, y, c = lax.axis_index("x"), lax.axis_index("y"), lax.axis_index("c")
        self.me, self.sibling, self.c = (x, y, c), (x, y, 1 - c), c
        self.chips = [(1 - x, y), (x, 1 - y), (1 - x, 1 - y)]
        self.x_ref, self.out_ref, self.send_sems, self.recv_sems = x_ref, out_ref, send_sems, recv_sems

    def slot(self, px, py, pc):
        return self.out_ref.at[4 * px + 2 * py + pc]

    def copy(self, k, block, to, src=None):
        return pltpu.make_async_remote_copy(
            src_ref=self.slot(*block) if src is None else src, dst_ref=self.slot(*block),
            send_sem=self.send_sems.at[k], recv_sem=self.recv_sems.at[k], device_id=to, device_id_type=MESH)

    def first(self):
        return [self.copy(0, self.me, self.sibling, src=self.x_ref)] + [
            self.copy(1 + j, self.me, (*chip, self.c), src=self.x_ref) for j, chip in enumerate(self.chips)]

    def passed(self):
        return [self.copy(4 + j, (*chip, self.c), self.sibling) for j, chip in enumerate(self.chips)]


def _gather_start(x_ref, out_ref, stage, send_sems, recv_sems, local_sem):
    plan = _GatherPlan(x_ref, out_ref, send_sems, recv_sems)
    for cp in plan.first():
        cp.start()
    _copy_through_vmem(x_ref, plan.slot(*plan.me), stage, local_sem)


def _gather_pass_on(out_ref, send_sems, recv_sems):
    plan = _GatherPlan(None, out_ref, send_sems, recv_sems)
    passed = plan.passed()
    for j, chip in enumerate(plan.chips):
        plan.copy(1 + j, (*chip, plan.c), plan.me).wait_recv()
        passed[j].start()


def _gather_finish(x_ref, out_ref, send_sems, recv_sems):
    plan = _GatherPlan(x_ref, out_ref, send_sems, recv_sems)
    plan.copy(0, plan.sibling, plan.me).wait_recv()
    for j, chip in enumerate(plan.chips):
        plan.copy(4 + j, (*chip, 1 - plan.c), plan.me).wait_recv()
    for cp in plan.first() + plan.passed():
        cp.wait_send()


def _all_gather(xs, name):
    r, cdim = xs.shape

    def body(x_ref, out_ref, stage, send_sems, recv_sems, local_sem):
        _gather_start(x_ref, out_ref, stage, send_sems, recv_sems, local_sem)
        _gather_pass_on(out_ref, send_sems, recv_sems)
        _gather_finish(x_ref, out_ref, send_sems, recv_sems)

    return pl.pallas_call(
        body, name=name, out_shape=jax.ShapeDtypeStruct((N_DEV, r, cdim), xs.dtype),
        in_specs=[ANY], out_specs=ANY,
        scratch_shapes=[pltpu.VMEM((r, cdim), xs.dtype), pltpu.SemaphoreType.DMA((7,)), pltpu.SemaphoreType.DMA((7,)),
                        pltpu.SemaphoreType.DMA],
        compiler_params=pltpu.CompilerParams(vmem_limit_bytes=VMEM_LIMIT),
    )(xs)


def _pair_copies(src_refs, dst_refs, send_sems, recv_sems):
    x, y, c = lax.axis_index("x"), lax.axis_index("y"), lax.axis_index("c")
    return [pltpu.make_async_remote_copy(
        src_ref=src.at[2 * k + (1 - c)], dst_ref=dst.at[k], send_sem=send_sems.at[k, p], recv_sem=recv_sems.at[k, p],
        device_id=(x, y, 1 - c), device_id_type=MESH)
        for k in range(N_DEV // 2) for p, (src, dst) in enumerate(zip(src_refs, dst_refs))]


def _rs_pair_sum(core, pieces, offsets, rows, name, landed=()):
    cdim = pieces[0].shape[2]
    nk = N_DEV // 2
    npc = len(pieces)
    nrem = npc - len(landed)
    spans = [(o, t.shape[1]) for t, o in zip(pieces, offsets)]
    ends = [o + n for o, n in spans]
    gaps = [(a, b - a) for a, b in zip(ends, [o for o, _ in spans[1:]] + [rows]) if b > a]

    def body(core_ref, *refs):
        own, src, got, o_ref = refs[:npc], refs[npc:npc + nrem], refs[npc + nrem:2 * npc], refs[2 * npc]
        landing, send_sems, recv_sems = refs[2 * npc + 1:]
        k = pl.program_id(0)
        x, y, c = lax.axis_index("x"), lax.axis_index("y"), lax.axis_index("c")

        def copies(kk):
            return [pltpu.make_async_remote_copy(
                src_ref=src[p].at[2 * kk + (1 - c)], dst_ref=landing.at[kk, pl.ds(o, n)],
                send_sem=send_sems.at[kk, p], recv_sem=recv_sems.at[kk, p], device_id=(x, y, 1 - c),
                device_id_type=MESH) for p, (o, n) in enumerate(spans[:nrem])]

        @pl.when(k == 0)
        def _():
            for kk in range(nk):
                for cp in copies(kk):
                    cp.start()

        for cp, piece, (o, n) in zip(copies(k), own, spans):
            cp.wait_recv()
            o_ref[0, o:o + n, :] = (piece[0].astype(F32) + landing[k, o:o + n, :].astype(F32)).astype(BF16)
        for theirs, piece, (o, n) in zip(got, own[nrem:], spans[nrem:]):
            o_ref[0, o:o + n, :] = (piece[0].astype(F32) + theirs[0].astype(F32)).astype(BF16)
        for o, n in gaps:
            o_ref[0, o:o + n, :] = jnp.zeros((n, cdim), BF16)

        @pl.when(k == nk - 1)
        def _():
            for kk in range(nk):
                for cp in copies(kk):
                    cp.wait_send()

    own_specs = [pl.BlockSpec((1, n, cdim), lambda k, core_ref: (2 * k + core_ref[0], 0, 0)) for _, n in spans]
    got_specs = [pl.BlockSpec((1, n, cdim), lambda k, core_ref: (k, 0, 0)) for _, n in spans[nrem:]]
    land_rows = max(o + n for o, n in spans[:nrem])
    return pl.pallas_call(
        body, name=name, out_shape=jax.ShapeDtypeStruct((nk, rows, cdim), BF16),
        grid_spec=pltpu.PrefetchScalarGridSpec(
            num_scalar_prefetch=1, grid=(nk,),
            in_specs=own_specs + [ANY] * nrem + got_specs,
            out_specs=pl.BlockSpec((1, rows, cdim), lambda k, core_ref: (k, 0, 0)),
            scratch_shapes=[pltpu.VMEM((nk, land_rows, cdim), BF16), pltpu.SemaphoreType.DMA((nk, nrem)),
                            pltpu.SemaphoreType.DMA((nk, nrem))]),
        compiler_params=_params(1),
    )(core, *pieces, *pieces[:nrem], *landed)


def _chips_start(b_ref, out_ref, stage, send_sems, recv_sems, local_sem):
    x, y, c = lax.axis_index("x"), lax.axis_index("y"), lax.axis_index("c")
    mychip = 2 * x + y
    for j, (px, py) in enumerate([(1 - x, y), (x, 1 - y), (1 - x, 1 - y)]):
        pltpu.make_async_remote_copy(
            src_ref=b_ref.at[2 * px + py], dst_ref=out_ref.at[mychip],
            send_sem=send_sems.at[j], recv_sem=recv_sems.at[j], device_id=(px, py, c), device_id_type=MESH).start()
    _copy_through_vmem(b_ref.at[mychip], out_ref.at[mychip], stage, local_sem)


def _chips_finish(b_ref, out_ref, send_sems, recv_sems):
    x, y, c = lax.axis_index("x"), lax.axis_index("y"), lax.axis_index("c")
    for j, (px, py) in enumerate([(1 - x, y), (x, 1 - y), (1 - x, 1 - y)]):
        pltpu.make_async_remote_copy(
            src_ref=b_ref.at[2 * px + py], dst_ref=out_ref.at[2 * px + py],
            send_sem=send_sems.at[j], recv_sem=recv_sems.at[j], device_id=(px, py, c), device_id_type=MESH).wait()


def _pad_rows(a, rows):
    return jnp.pad(a, ((0, rows - a.shape[0]), (0, 0)))


def _pack_in(w_in):
    return _pad_rows(w_in[0].T, ROWS_IN)


def _unpack_in(r):
    return r[0:SHARD_IN].T[None]


def _pack_rest(w_out, w_gate, w_up, w_down, w_ple, w_pg):
    head = _pad_rows(jnp.concatenate([w_out[0], w_pg[0], w_ple[0].T.reshape(32, D_MODEL)], axis=0), OFF_GATE)
    return jnp.concatenate([head, w_gate[0].T, w_up[0].T, w_down[0]], axis=0)


def _unpack_rest(r):
    return (r[0:OFF_PG][None], r[OFF_GATE:OFF_UP].T[None], r[OFF_UP:OFF_DOWN].T[None], r[OFF_DOWN:ROWS_REST][None],
            r[OFF_PLE:OFF_PLE + 32].reshape(128, D_PLE).T[None], r[OFF_PG:OFF_PLE][None])


def _pack_small(w_pool, g_mix_pre, g_mix_post, g_ffn_pre, g_ffn_post, g_ple, g_attn, g_pool, pool_scale, b_forget,
                loss=None):
    row = lambda vrow: vrow.reshape(1, -1)
    misc = [row(pool_scale), row(b_forget), row(loss) if loss is not None else jnp.zeros((1, 1), F32),
            jnp.zeros((1, D_MODEL - COL_LOSS - 1), F32)]
    rows = [w_pool.reshape(64, D_MODEL), row(g_mix_pre), row(g_mix_post), row(g_ffn_pre), row(g_ffn_post), row(g_ple),
            jnp.concatenate([row(g_attn), row(g_pool)], axis=1), jnp.concatenate(misc, axis=1),
            jnp.zeros((SMALL_ROWS - ROW_MISC - 1, D_MODEL), F32)]
    return jnp.concatenate(rows, axis=0)


def _pack_small_late(g_mix_pre, b_forget):
    misc = [jnp.zeros((1, COL_B_FORGET), F32), b_forget.reshape(1, -1), jnp.zeros((1, D_MODEL - COL_LOSS), F32)]
    return jnp.concatenate([g_mix_pre.reshape(1, -1), jnp.zeros((ROW_MISC - ROW_G_MIX_PRE - 1, D_MODEL), F32),
                            jnp.concatenate(misc, axis=1), jnp.zeros((SMALL_ROWS - ROW_MISC - 1, D_MODEL), F32)], axis=0)


def _unpack_small(r):
    gains, misc = r[ROW_GROUP_GAINS:ROW_GROUP_GAINS + 1], r[ROW_MISC:ROW_MISC + 1]
    return dict(
        w_pool=r[0:64].reshape(1, 4, POOL_CH, POOL_CH), g_mix_pre=r[ROW_G_MIX_PRE:ROW_G_MIX_PRE + 1],
        g_mix_post=r[ROW_G_MIX_POST:ROW_G_MIX_POST + 1], g_ffn_pre=r[ROW_G_FFN_PRE:ROW_G_FFN_PRE + 1],
        g_ffn_post=r[ROW_G_FFN_POST:ROW_G_FFN_POST + 1], g_ple=r[ROW_G_PLE:ROW_G_PLE + 1],
        g_attn_grp=gains[:, 0:D_ATTN], g_pool_grp=gains[:, D_ATTN:D_ATTN + D_POOL],
        pool_scale=misc[:, 0:D_POOL], b_forget=misc[:, COL_B_FORGET:COL_B_FORGET + HEADS])


def _step(x, p, tgt, small, in_w, in_m, in_v, rest_w, rest_m, rest_v):
    core = lax.axis_index("c").astype(jnp.int32).reshape(1)
    win_t = _all_gather(in_w.astype(BF16), "gather_w_in")[:, 0:SHARD_IN].reshape(D_IN, D_MODEL)
    wqkv = win_t
    wf = _pad_rows(win_t[3 * D_ATTN:3 * D_ATTN + HEADS], LANES)
    wu = win_t[3 * D_ATTN + HEADS:]
    wpool = small["w_pool"].astype(BF16)
    bpad = jnp.pad(small["b_forget"], ((0, 0), (0, LANES - HEADS)))

    lay = _attn_layout_constants()
    rest_b = rest_w.astype(BF16)
    hn, qt3, ka, v, qat3, vt3, kt3, fl, y, mpre, gh = _pre_attn_fwd(x, small["g_mix_pre"], wqkv, wf, wu, bpad, wpool, lay,
                                                                 rest_b[0:OFF_GATE])
    a, lset3, gf = _attn_fwd(ka, qat3, vt3, rest_b[OFF_GATE:])
    wple_t = gh[:, OFF_PLE:OFF_PLE + 32].reshape(D_MODEL, D_PLE)
    mix, o, h1, hn2 = _post_attn_fwd(a, mpre, x, small["g_attn_grp"], small["g_pool_grp"], small["pool_scale"], gh,
                                     small["g_mix_post"], small["g_ffn_pre"])
    gate, up, act, ff, h2 = _ffn_fwd(hn2, gf, gf, gf, h1, small["g_ffn_post"])
    dh2, dff, dgl, dpp, h2b, pb, loss8, dg_ple, dg_ffn_post = _tail_fwd_bwd(
        h2, p, tgt, ff, wple_t, gh, small["g_ple"], small["g_ffn_post"])
    dgate, dup, dh1, dg_ffn_pre = _ffn_bwd(dff, gate, up, gf, gf, gf, h1, dh2, small["g_ffn_pre"])
    nd = N_DEV
    send_rest = [
        _wgrad(h2b, dgl, BF16, "wgrad_ple_gate").reshape(nd, 128, D_MODEL),
        _wgrad(dpp, pb, BF16, "wgrad_ple").reshape(nd, 32, D_MODEL),
        _wgrad(dgate, hn2, BF16, "wgrad_gate").reshape(nd, SHARD_FF, D_MODEL),
        _wgrad(dup, hn2, BF16, "wgrad_up").reshape(nd, SHARD_FF, D_MODEL),
        _wgrad(act, dff, BF16, "wgrad_down").reshape(nd, SHARD_FF, D_MODEL)]
    (dob, dat3, dlt3, dmpb, dy, dg_mix_post, dg_attn, dg_pool, dps), landed = _post_attn_bwd(
        dh1, o, a, mpre, gh, wpool, small["g_mix_post"], small["g_attn_grp"], small["g_pool_grp"], small["pool_scale"],
        send_rest)
    send_rest = [_wgrad(mix, dob, BF16, "wgrad_out").reshape(nd, 128, D_MODEL)] + send_rest
    pair_rest = _rs_pair_sum(core, send_rest, [0, OFF_PG, OFF_PLE, OFF_GATE, OFF_UP, OFF_DOWN], ROWS_REST,
                             "rs_pair_sum_rest", landed)

    dwp = _wgrad(y, dmpb, F32, "wgrad_pool")
    dw_pool = jnp.stack([dwp[g * POOL_CH:(g + 1) * POOL_CH, g * POOL_CH:(g + 1) * POOL_CH] for g in range(4)])
    small_part = _pack_small(dw_pool, jnp.zeros((1, D_MODEL), F32), dg_mix_post, dg_ffn_pre, dg_ffn_post, dg_ple,
                             dg_attn, dg_pool, dps, jnp.zeros((1, HEADS), F32), loss8[0:1, 0:1])
    dqt3, dkt3, dvt3, chips_rest, small_all = _attn_bwd(ka, v, kt3, qat3, qt3, dat3, lset3, dlt3, pair_rest, small_part)

    gx, dz, dg_mix_pre, db = _pre_attn_bwd(dqt3, dkt3, dvt3, fl, dy, x, dh1, small["g_mix_pre"], wqkv, wf, wu)

    pair_in = _rs_pair_sum(core, [_wgrad_in(dz, hn)], [0], ROWS_IN, "rs_pair_sum_in")

    small_late = _pack_small_late(dg_mix_pre, db[:, 0:HEADS])
    *upd_rest, chips_in, late_all = _reduce_update_rest(chips_rest, rest_w, rest_m, rest_v, pair_in, small_late)
    upd_in = _reduce_update_big(chips_in, in_w, in_m, in_v, ROWS_IN, "reduce_update_in")
    return gx, (small_all, late_all), upd_in, upd_rest


def kernel(x, p, g_mix_pre, w_in, b_forget, g_attn_grp, g_pool_grp, w_pool, pool_scale, w_out, g_mix_post, g_ffn_pre, w_ffn_gate, w_ffn_up, w_ffn_down, g_ffn_post, w_ple_proj, g_ple, w_ple_gate, loss_target, m_g_mix_pre, m_w_in, m_b_forget, m_g_attn_grp, m_g_pool_grp, m_w_pool, m_pool_scale, m_w_out, m_g_mix_post, m_g_ffn_pre, m_w_ffn_gate, m_w_ffn_up, m_w_ffn_down, m_g_ffn_post, m_w_ple_proj, m_g_ple, m_w_ple_gate, v_g_mix_pre, v_w_in, v_b_forget, v_g_attn_grp, v_g_pool_grp, v_w_pool, v_pool_scale, v_w_out, v_g_mix_post, v_g_ffn_pre, v_w_ffn_gate, v_w_ffn_up, v_w_ffn_down, v_g_ffn_post, v_w_ple_proj, v_g_ple, v_w_ple_gate):
    small = dict(w_pool=w_pool[0], g_mix_pre=g_mix_pre, g_mix_post=g_mix_post, g_ffn_pre=g_ffn_pre,
                 g_ffn_post=g_ffn_post, g_ple=g_ple, g_attn_grp=g_attn_grp, g_pool_grp=g_pool_grp,
                 pool_scale=pool_scale, b_forget=b_forget)
    gx, small_all, upd_in, upd_rest = _step(
        x[0], p[0, 0], loss_target[0], small, _pack_in(w_in), _pack_in(m_w_in), _pack_in(v_w_in),
        _pack_rest(w_out, w_ffn_gate, w_ffn_up, w_ffn_down, w_ple_proj, w_ple_gate),
        _pack_rest(m_w_out, m_w_ffn_gate, m_w_ffn_up, m_w_ffn_down, m_w_ple_proj, m_w_ple_gate),
        _pack_rest(v_w_out, v_w_ffn_gate, v_w_ffn_up, v_w_ffn_down, v_w_ple_proj, v_w_ple_gate))

    sm_w = _pack_small(w_pool, g_mix_pre, g_mix_post, g_ffn_pre, g_ffn_post, g_ple, g_attn_grp, g_pool_grp, pool_scale, b_forget)
    sm_m = _pack_small(m_w_pool, m_g_mix_pre, m_g_mix_post, m_g_ffn_pre, m_g_ffn_post, m_g_ple, m_g_attn_grp, m_g_pool_grp, m_pool_scale, m_b_forget)
    sm_v = _pack_small(v_w_pool, v_g_mix_pre, v_g_mix_post, v_g_ffn_pre, v_g_ffn_post, v_g_ple, v_g_attn_grp, v_g_pool_grp, v_pool_scale, v_b_forget)
    upd_small = _reduce_update_small(*small_all, sm_w, sm_m, sm_v)
    loss = upd_small[0][ROW_MISC, COL_LOSS]

    def leaves(k):
        b_out, b_gate, b_up, b_down, b_ple, b_pg = _unpack_rest(upd_rest[k])
        s = _unpack_small(upd_small[k])
        return (s["g_mix_pre"], _unpack_in(upd_in[k]), s["b_forget"], s["g_attn_grp"], s["g_pool_grp"], s["w_pool"],
                s["pool_scale"], b_out, s["g_mix_post"], s["g_ffn_pre"], b_gate, b_up, b_down, s["g_ffn_post"], b_ple,
                s["g_ple"], b_pg)

    return (loss, gx[None], *leaves(0), *leaves(1), *leaves(2), *leaves(3))
```

```python
import functools

import jax
import jax.numpy as jnp
from jax import lax
from jax.experimental import pallas as pl
from jax.experimental.pallas import tpu as pltpu

F32 = jnp.float32
BF16 = jnp.bfloat16
HIGHEST = lax.Precision.HIGHEST

D_MODEL = 1024
HEADS = 8
HEAD_DIM = 64
D_ATTN = HEADS * HEAD_DIM
POOL_WINDOWS = (2, 4, 8, 16)
POOL_CH = 128
D_POOL = POOL_CH * len(POOL_WINDOWS)
D_FF = 2816
D_PLE = 256
D_IN = 3 * D_ATTN + HEADS + D_POOL
RMS_EPS = 1e-6
N_DEV = 8

ADAM_LR = 0.001
ADAM_B1 = 0.9
ADAM_B2 = 0.999
ADAM_EPS = 1e-08
ADAM_WD = 0.01
ADAM_STEP = 10

LANES = 128
HALO = 16
TS = 512
TS_FF = 512
TS_WGRAD = 1024
TM_WGRAD = 2176
TQ = 256
TN_FF = 1408
NEG = -1e30
VMEM_LIMIT = 56 * 1024 * 1024

SHARD_IN = 257
ROWS_IN = 272
SHARD_FF = 352
OFF_PG = 128
OFF_PLE = 256
OFF_GATE = SHARD_FF
OFF_UP = 2 * SHARD_FF
OFF_DOWN = 3 * SHARD_FF
ROWS_REST = 4 * SHARD_FF
TR_REST = SHARD_FF

SMALL_ROWS = 72
ROW_G_MIX_PRE, ROW_G_MIX_POST, ROW_G_FFN_PRE, ROW_G_FFN_POST, ROW_G_PLE = 64, 65, 66, 67, 68
ROW_GROUP_GAINS, ROW_MISC = 69, 70
COL_B_FORGET = D_POOL
COL_LOSS = D_POOL + HEADS


def _nn(a, b):
    return jnp.dot(a, b, preferred_element_type=F32)


def _nt(a, b):
    return lax.dot_general(a, b, (((1,), (1,)), ((), ())), preferred_element_type=F32)


def _tn(a, b):
    return lax.dot_general(a, b, (((0,), (0,)), ((), ())), preferred_element_type=F32)


def _rstd(v):
    return lax.rsqrt(jnp.mean(v * v, axis=-1, keepdims=True) + RMS_EPS)


def _rms_bwd(v, g, dy):
    r = _rstd(v)
    vh = v * r
    t = dy * g
    dv = r * (t - vh * jnp.mean(t * vh, axis=-1, keepdims=True))
    return dv, jnp.sum(dy * vh, axis=0, keepdims=True)


def _split3(v):
    hi = v.astype(BF16)
    rest = v - hi.astype(F32)
    mid = rest.astype(BF16)
    return hi, mid, (rest - mid.astype(F32)).astype(BF16)


def _mask_matmul(mask, v):
    hi, mid, lo = _split3(v)
    return _nn(mask, lo) + _nn(mask, mid) + _nn(mask, hi)


def _params(n_grid):
    return pltpu.CompilerParams(dimension_semantics=("arbitrary",) * n_grid, vmem_limit_bytes=VMEM_LIMIT)


def _row(i):
    return (i, 0)


def _fixed(*_):
    return (0, 0)


def _spec_square(part):
    return pl.BlockSpec((N_DEV, 128, D_MODEL), lambda *_: (0, part, 0))


FF_GATE, FF_UP, FF_DOWN = 1, 0, 1


def _spec_ff(half):
    return pl.BlockSpec((TN_FF // SHARD_FF, SHARD_FF, D_MODEL), lambda i, j: (j, half, 0))


assert TS == 2 * TQ and TN_FF % SHARD_FF == 0
_HALVES = (slice(0, TQ), slice(TQ, TS))

VMEM_WHOLE = pl.BlockSpec(memory_space=pltpu.VMEM)
SMEM_WHOLE = pl.BlockSpec(memory_space=pltpu.SMEM)
ANY = pl.BlockSpec(memory_space=pl.ANY)


LOG2E = 1.4426950408889634
VROWS = HEAD_DIM + 16
AUG = 128
BIAS_LANE = HEAD_DIM
ONE_LANE = HEAD_DIM + 3
SPARE_LANE = HEADS


def _attn_layout_constants():
    import numpy as np
    place = np.zeros((D_ATTN, HEADS * AUG), np.float32)
    for r in range(D_ATTN):
        place[r, (r // HEAD_DIM) * AUG + r % HEAD_DIM] = 1.0
    bias_k = np.zeros((3, LANES, HEADS * AUG), np.float32)
    bias_q = np.zeros((3, LANES, HEADS * AUG), np.float32)
    for h in range(HEADS):
        for part in range(3):
            bias_k[part, h, h * AUG + BIAS_LANE + part] = -1.0
            bias_q[part, h, h * AUG + ONE_LANE + part] = 1.0
            bias_k[0, SPARE_LANE, h * AUG + ONE_LANE + part] = 1.0
            bias_q[0, SPARE_LANE, h * AUG + BIAS_LANE + part] = 1.0
    as_bf = lambda a: jnp.asarray(a, BF16)
    return dict(place=as_bf(place), place_t=as_bf(place.T), bias_k=as_bf(bias_k),
                bias_q_t=as_bf(bias_q.transpose(0, 2, 1)))


def _pre_attn_fwd(x, g1, wqkv, wf, wu, bpad, wpool, lay, own_block):
    s, d = x.shape
    nt = s // TS
    sub = TS // TQ

    def body(x_ref, g_ref, wqkv_ref, wf_ref, wu_ref, b_ref, wp_ref, place_ref, place_t_ref, bk_ref, bqt_ref, own_ref,
             hn_ref, qt_ref, ka_ref, v_ref, qat_ref, vt_ref, kt_ref, fl_ref, y_ref, mp_ref, all_ref,
             ubuf, ccar, cbuf, stage, send_sems, recv_sems, local_sem):
        i = pl.program_id(0)

        @pl.when(i == 0)
        def _():
            _gather_start(own_ref, all_ref, stage, send_sems, recv_sems, local_sem)
            ubuf[0:HALO, :] = jnp.zeros((HALO, D_POOL), F32)
            ccar[...] = jnp.zeros_like(ccar)

        @pl.when(i == max(nt - 2, 0))
        def _():
            _gather_pass_on(all_ref, send_sems, recv_sems)

        xv = x_ref[...]
        hn = (xv * _rstd(xv) * g_ref[...]).astype(BF16)
        hn_ref[...] = hn
        zq = _nt(hn, wqkv_ref[...])
        qt = (zq[:, 0:D_ATTN] * 0.125).astype(BF16).T
        qb = (zq[:, 0:D_ATTN] * (0.125 * LOG2E)).astype(BF16)
        kb = zq[:, D_ATTN:2 * D_ATTN].astype(BF16)
        vb = zq[:, 2 * D_ATTN:3 * D_ATTN].astype(BF16)
        v_ref[...] = vb

        fl = _nt(hn, wf_ref[...]) + b_ref[...]
        fl_ref[...] = fl
        logf = jax.nn.log_sigmoid(fl)
        rr = lax.broadcasted_iota(jnp.int32, (TS, TS), 0)
        cc = lax.broadcasted_iota(jnp.int32, (TS, TS), 1)
        c = _mask_matmul((cc <= rr).astype(BF16), logf) + ccar[...]
        cbuf[...] = c
        ccar[...] = cbuf[TS - 1:TS, :]
        hi, mid, lo = _split3(c * LOG2E)
        lane = lax.broadcasted_iota(jnp.int32, (TS, LANES), 1)
        parts = (jnp.where(lane == SPARE_LANE, 1.0, hi).astype(BF16), mid, lo)
        ka = _nn(kb, place_ref[...])
        qat = _nt(place_t_ref[...], qb)
        for part in range(3):
            ka = ka + _nn(parts[part], bk_ref[part])
            qat = qat + _nt(bqt_ref[part], parts[part])
        ka_ref[...] = ka.astype(BF16)
        qat = qat.astype(BF16)
        vt = vb.T
        kt = kb.T
        for a in range(sub):
            cols = slice(a * TQ, (a + 1) * TQ)
            qat_ref[a] = qat[:, cols]
            for ref, mat in ((qt_ref, qt), (kt_ref, kt), (vt_ref, vt)):
                for h in range(HEADS):
                    ref[a, h * VROWS:h * VROWS + HEAD_DIM, :] = mat[h * HEAD_DIM:(h + 1) * HEAD_DIM, cols]
                    ref[a, h * VROWS + HEAD_DIM:(h + 1) * VROWS, :] = jnp.ones((VROWS - HEAD_DIM, TQ), BF16)

        u = _nt(hn, wu_ref[...])
        ubuf[HALO:HALO + TS, :] = u
        t = i * TS + lax.broadcasted_iota(jnp.int32, (TS, 1), 0)
        for g, w in enumerate(POOL_WINDOWS):
            cols = slice(g * POOL_CH, (g + 1) * POOL_CH)
            sm = ubuf[:, cols]
            step = 1
            while step < w:
                sm = sm + pltpu.roll(sm, step, 0)
                step *= 2
            cnt = jnp.minimum(t + 1, w).astype(F32)
            yg = (sm[HALO:, :] / cnt - u[:, cols]).astype(BF16)
            y_ref[:, cols] = yg
            mp_ref[:, cols] = _nn(yg, wp_ref[g])
        ubuf[0:HALO, :] = u[TS - HALO:, :]

        @pl.when(i == nt - 1)
        def _():
            _gather_finish(own_ref, all_ref, send_sems, recv_sems)

    nq = s // TQ
    aug = HEADS * AUG
    outs = (
        jax.ShapeDtypeStruct((s, d), BF16), jax.ShapeDtypeStruct((nq, HEADS * VROWS, TQ), BF16),
        jax.ShapeDtypeStruct((s, aug), BF16), jax.ShapeDtypeStruct((s, D_ATTN), BF16),
        jax.ShapeDtypeStruct((nq, aug, TQ), BF16), jax.ShapeDtypeStruct((nq, HEADS * VROWS, TQ), BF16),
        jax.ShapeDtypeStruct((nq, HEADS * VROWS, TQ), BF16),
        jax.ShapeDtypeStruct((s, LANES), F32),
        jax.ShapeDtypeStruct((s, D_POOL), BF16), jax.ShapeDtypeStruct((s, D_POOL), F32),
        jax.ShapeDtypeStruct((N_DEV,) + own_block.shape, own_block.dtype),
    )
    fixed3 = lambda i: (0, 0, 0)
    tiles3 = lambda rows: pl.BlockSpec((sub, rows, TQ), lambda i: (i, 0, 0))
    return pl.pallas_call(
        body, grid=(nt,), out_shape=outs, name="pre_attn_fwd",
        in_specs=[pl.BlockSpec((TS, d), _row), pl.BlockSpec((1, d), _fixed),
                  pl.BlockSpec((3 * D_ATTN, d), _fixed), pl.BlockSpec(wf.shape, _fixed), pl.BlockSpec(wu.shape, _fixed),
                  pl.BlockSpec((1, LANES), _fixed), pl.BlockSpec(wpool.shape, fixed3),
                  pl.BlockSpec(lay["place"].shape, _fixed), pl.BlockSpec(lay["place_t"].shape, _fixed),
                  pl.BlockSpec(lay["bias_k"].shape, fixed3), pl.BlockSpec(lay["bias_q_t"].shape, fixed3), ANY],
        out_specs=(pl.BlockSpec((TS, d), _row), tiles3(HEADS * VROWS),
                   pl.BlockSpec((TS, aug), _row), pl.BlockSpec((TS, D_ATTN), _row),
                   tiles3(aug), tiles3(HEADS * VROWS), tiles3(HEADS * VROWS),
                   pl.BlockSpec((TS, LANES), _row),
                   pl.BlockSpec((TS, D_POOL), _row), pl.BlockSpec((TS, D_POOL), _row), ANY),
        scratch_shapes=[pltpu.VMEM((TS + HALO, D_POOL), F32), pltpu.VMEM((1, LANES), F32), pltpu.VMEM((TS, LANES), F32),
                        pltpu.VMEM(own_block.shape, own_block.dtype),
                        pltpu.SemaphoreType.DMA((7,)), pltpu.SemaphoreType.DMA((7,)), pltpu.SemaphoreType.DMA],
        compiler_params=_params(1),
    )(x, g1, wqkv, wf, wu, bpad, wpool, lay["place"], lay["place_t"], lay["bias_k"], lay["bias_q_t"], own_block)


def _causal_in_tile():
    krow = lax.broadcasted_iota(jnp.int32, (TQ, TQ), 0)
    qcol = lax.broadcasted_iota(jnp.int32, (TQ, TQ), 1)
    return krow <= qcol


def _attn_fwd(ka, qat3, vt3, own_block):
    s = ka.shape[0]
    nq = s // TQ
    pass_on_step = max(nq - 2, 0)

    def body(qa_ref, ka_ref, vt_ref, own_ref, a_ref, lset_ref, all_ref, acc, out_t, st_scr, pt_scr, m_scr,
             stage, send_sems, recv_sems, local_sem):
        i = pl.program_id(0)

        @pl.when(i == 0)
        def _():
            _gather_start(own_ref, all_ref, stage, send_sems, recv_sems, local_sem)

        @pl.when(i == pass_on_step)
        def _():
            _gather_pass_on(all_ref, send_sems, recv_sems)

        acc[...] = jnp.zeros_like(acc)
        m_scr[...] = jnp.full(m_scr.shape, NEG, F32)

        def tiles(js, masked):
            tile_max = []
            for h in range(HEADS):
                aug = slice(h * AUG, (h + 1) * AUG)
                for t, j in enumerate(js):
                    st = _nn(ka_ref[pl.ds(j * TQ, TQ), aug], qa_ref[0, aug, :])
                    if masked and t == len(js) - 1:
                        st = jnp.where(_causal_in_tile(), st, NEG)
                    st_scr[t * HEADS + h] = st
                    mx = jnp.max(st, axis=0, keepdims=True)
                    top = mx if t == 0 else jnp.maximum(top, mx)
                tile_max.append(top)
            scale = []
            for h in range(HEADS):
                m_old = m_scr[h:h + 1, :]
                m_new = jnp.maximum(m_old, tile_max[h])
                m_scr[h:h + 1, :] = m_new
                scale.append(jnp.exp2(m_old - m_new))
                for t in range(len(js)):
                    pt_scr[t * HEADS + h] = jnp.exp2(st_scr[t * HEADS + h] - m_new).astype(BF16)
            for h in range(HEADS):
                rows = slice(h * VROWS, (h + 1) * VROWS)
                upd = _nn(vt_ref[js[0], rows, :], pt_scr[h])
                for t in range(1, len(js)):
                    upd = upd + _nn(vt_ref[js[t], rows, :], pt_scr[t * HEADS + h])
                acc[rows, :] = scale[h] * acc[rows, :] + upd

        def pair(p, carry):
            tiles([2 * p, 2 * p + 1], False)
            return carry

        lax.fori_loop(0, i // 2, pair, 0)

        @pl.when(i % 2 == 1)
        def _():
            tiles([i - 1, i], True)

        @pl.when(i % 2 == 0)
        def _():
            tiles([i], True)

        for h in range(HEADS):
            denom = acc[h * VROWS + HEAD_DIM:h * VROWS + HEAD_DIM + 1, :]
            out_t[h * HEAD_DIM:(h + 1) * HEAD_DIM, :] = acc[h * VROWS:h * VROWS + HEAD_DIM, :] / denom
            lset_ref[0, h:h + 1, :] = m_scr[h:h + 1, :] + jnp.log2(denom)
        a_ref[...] = out_t[...].T

        @pl.when(i == nq - 1)
        def _():
            _gather_finish(own_ref, all_ref, send_sems, recv_sems)

    r, cdim = own_block.shape
    return pl.pallas_call(
        body, grid=(nq,), name="attn_fwd",
        out_shape=(jax.ShapeDtypeStruct((s, D_ATTN), F32), jax.ShapeDtypeStruct((nq, HEADS, TQ), F32),
                   jax.ShapeDtypeStruct((N_DEV, r, cdim), own_block.dtype)),
        in_specs=[pl.BlockSpec((1, HEADS * AUG, TQ), lambda i: (i, 0, 0)), VMEM_WHOLE, VMEM_WHOLE, ANY],
        out_specs=(pl.BlockSpec((TQ, D_ATTN), _row), pl.BlockSpec((1, HEADS, TQ), lambda i: (i, 0, 0)), ANY),
        scratch_shapes=[pltpu.VMEM((HEADS * VROWS, TQ), F32), pltpu.VMEM((D_ATTN, TQ), F32),
                        pltpu.VMEM((2 * HEADS, TQ, TQ), F32), pltpu.VMEM((2 * HEADS, TQ, TQ), BF16),
                        pltpu.VMEM((HEADS, TQ), F32), pltpu.VMEM((r, cdim), own_block.dtype),
                        pltpu.SemaphoreType.DMA((7,)), pltpu.SemaphoreType.DMA((7,)), pltpu.SemaphoreType.DMA],
        compiler_params=_params(1),
    )(qat3, ka, vt3, own_block)


def _post_attn_fwd(a, mpre, x, g_attn, g_pool, pscale, wout, g_post, g_ffn_pre):
    s, d = x.shape

    def body(a_ref, mp_ref, x_ref, ga_ref, gp_ref, ps_ref, wo_ref, gpost_ref, gpre_ref,
             mix_ref, o_ref, h1_ref, hn2_ref):
        for rows in _HALVES:
            av = a_ref[rows, :]
            mix_ref[rows, 0:D_ATTN] = (av * _rstd(av) * ga_ref[...]).astype(BF16)
            mv = mp_ref[rows, :] * ps_ref[...]
            mix_ref[rows, D_ATTN:] = (mv * _rstd(mv) * gp_ref[...]).astype(BF16)
            o = _nn(mix_ref[rows, :], wo_ref[...].reshape(d, d))
            o_ref[rows, :] = o
            h1 = x_ref[rows, :] + o * _rstd(o) * gpost_ref[...]
            h1_ref[rows, :] = h1
            hn2_ref[rows, :] = (h1 * _rstd(h1) * gpre_ref[...]).astype(BF16)

    vec = lambda n: pl.BlockSpec((1, n), _fixed)
    return pl.pallas_call(
        body, grid=(s // TS,), name="post_attn_fwd",
        out_shape=(jax.ShapeDtypeStruct((s, d), BF16), jax.ShapeDtypeStruct((s, d), F32),
                   jax.ShapeDtypeStruct((s, d), F32), jax.ShapeDtypeStruct((s, d), BF16)),
        in_specs=[pl.BlockSpec((TS, D_ATTN), _row), pl.BlockSpec((TS, D_POOL), _row), pl.BlockSpec((TS, d), _row),
                  vec(D_ATTN), vec(D_POOL), vec(D_POOL), _spec_square(0), vec(d), vec(d)],
        out_specs=(pl.BlockSpec((TS, d), _row),) * 4,
        compiler_params=_params(1),
    )(a, mpre, x, g_attn, g_pool, pscale, wout, g_post, g_ffn_pre)


def _ffn_fwd(hn2, wg, wu, wd, h1, g_post):
    s, d = h1.shape
    nc = D_FF // TN_FF
    ts = min(TS_FF, s)

    def body(hn_ref, wg_ref, wu_ref, wd_ref, h1_ref, g_ref, gate_ref, up_ref, act_ref, ff_ref, h2_ref, acc):
        j = pl.program_id(1)

        @pl.when(j == 0)
        def _():
            acc[...] = jnp.zeros_like(acc)

        for r in range(2):
            rows = slice(r * (ts // 2), (r + 1) * (ts // 2))
            hn = hn_ref[rows, :]
            gt = _nt(hn, wg_ref[...].reshape(TN_FF, d))
            up = _nt(hn, wu_ref[...].reshape(TN_FF, d))
            gate_ref[rows, :] = gt.astype(BF16)
            up_ref[rows, :] = up.astype(BF16)
            act_ref[rows, :] = (gt * jax.nn.sigmoid(gt) * up).astype(BF16)
            acc[rows, :] += _nn(act_ref[rows, :], wd_ref[...].reshape(TN_FF, d))

        @pl.when(j == nc - 1)
        def _():
            ff = acc[...]
            ff_ref[...] = ff
            h2_ref[...] = h1_ref[...] + ff * _rstd(ff) * g_ref[...]

    rowblk = pl.BlockSpec((ts, d), lambda i, j: (i, 0))
    chunk = pl.BlockSpec((ts, TN_FF), lambda i, j: (i, j))
    return pl.pallas_call(
        body, grid=(s // ts, nc), name="ffn_fwd",
        out_shape=(jax.ShapeDtypeStruct((s, D_FF), BF16),) * 3 + (jax.ShapeDtypeStruct((s, d), F32),) * 2,
        in_specs=[rowblk, _spec_ff(FF_GATE), _spec_ff(FF_UP), _spec_ff(FF_DOWN), rowblk,
                  pl.BlockSpec((1, d), lambda i, j: (0, 0))],
        out_specs=(chunk, chunk, chunk, rowblk, rowblk),
        scratch_shapes=[pltpu.VMEM((ts, d), F32)],
        compiler_params=_params(2),
    )(hn2, wg, wu, wd, h1, g_post)


def _tail_fwd_bwd(h2, p, tgt, ff, wple, wpg, g_ple, g_ffn_post):
    s, d = h2.shape

    def body(h2_ref, p_ref, t_ref, ff_ref, wple_ref, wpg_ref, gple_ref, gfp_ref,
             dh2_ref, dff_ref, dgl_ref, dpp_ref, h2b_ref, pb_ref, loss_ref, dgple_ref, dgfp_ref):
        i = pl.program_id(0)

        @pl.when(i == 0)
        def _():
            loss_ref[...] = jnp.zeros_like(loss_ref)
            dgple_ref[...] = jnp.zeros_like(dgple_ref)
            dgfp_ref[...] = jnp.zeros_like(dgfp_ref)

        h2 = h2_ref[...]
        h2b = h2.astype(BF16)
        h2b_ref[...] = h2b
        pb = p_ref[...].astype(BF16)
        pb_ref[...] = pb
        pp = _nt(pb, wple_ref[...])
        gple = gple_ref[...]
        e = pp * _rstd(pp) * gple
        wpg = wpg_ref[...].reshape(d, d)
        sg = jax.nn.sigmoid(_nn(h2b, wpg))
        diff = h2 + sg * e - t_ref[...]
        sq = jnp.sum(jnp.sum(diff * diff, axis=1, keepdims=True), axis=0, keepdims=True)
        loss_ref[...] += jnp.broadcast_to(sq * (0.5 / d), loss_ref.shape)
        dh3 = diff * (1.0 / d)
        dgl = (dh3 * e * sg * (1.0 - sg)).astype(BF16)
        dgl_ref[...] = dgl
        dh2 = dh3 + _nt(dgl, wpg)
        dh2_ref[...] = dh2
        dpp, dg = _rms_bwd(pp, gple, dh3 * sg)
        dpp_ref[...] = dpp.astype(BF16)
        dgple_ref[...] += dg
        dff, dg = _rms_bwd(ff_ref[...], gfp_ref[...], dh2)
        dff_ref[...] = dff.astype(BF16)
        dgfp_ref[...] += dg

    rowblk = pl.BlockSpec((TS, d), _row)
    vec = pl.BlockSpec((1, d), _fixed)
    return pl.pallas_call(
        body, grid=(s // TS,), name="tail_fwd_bwd",
        out_shape=(jax.ShapeDtypeStruct((s, d), F32), jax.ShapeDtypeStruct((s, d), BF16),
                   jax.ShapeDtypeStruct((s, d), BF16), jax.ShapeDtypeStruct((s, d), BF16),
                   jax.ShapeDtypeStruct((s, d), BF16), jax.ShapeDtypeStruct((s, D_PLE), BF16),
                   jax.ShapeDtypeStruct((8, LANES), F32), jax.ShapeDtypeStruct((1, d), F32),
                   jax.ShapeDtypeStruct((1, d), F32)),
        in_specs=[rowblk, pl.BlockSpec((TS, D_PLE), _row), rowblk, rowblk,
                  pl.BlockSpec(wple.shape, _fixed), _spec_square(1), vec, vec],
        out_specs=(rowblk, rowblk, rowblk, rowblk, rowblk, pl.BlockSpec((TS, D_PLE), _row),
                   pl.BlockSpec((8, LANES), _fixed), vec, vec),
        compiler_params=_params(1),
    )(h2, p, tgt, ff, wple, wpg, g_ple, g_ffn_post)


def _ffn_bwd(dff, gate, up, wd, wg, wu, h1, dh2, g_pre):
    s, d = h1.shape
    nc = D_FF // TN_FF
    ts = min(TS_FF, s)

    def body(dff_ref, gate_ref, up_ref, wd_ref, wg_ref, wu_ref, h1_ref, dh2_ref, g_ref,
             dgate_ref, dup_ref, dh1_ref, dg_ref, acc):
        i = pl.program_id(0)
        j = pl.program_id(1)

        @pl.when((i == 0) & (j == 0))
        def _():
            dg_ref[...] = jnp.zeros_like(dg_ref)

        @pl.when(j == 0)
        def _():
            acc[...] = jnp.zeros_like(acc)

        for r in range(2):
            rows = slice(r * (ts // 2), (r + 1) * (ts // 2))
            dact = _nt(dff_ref[rows, :], wd_ref[...].reshape(TN_FF, d))
            gt = gate_ref[rows, :].astype(F32)
            sg = jax.nn.sigmoid(gt)
            dup_ref[rows, :] = (dact * gt * sg).astype(BF16)
            dgate_ref[rows, :] = (dact * up_ref[rows, :].astype(F32) * (sg * (1.0 + gt * (1.0 - sg)))).astype(BF16)
            acc[rows, :] += (_nn(dgate_ref[rows, :], wg_ref[...].reshape(TN_FF, d))
                             + _nn(dup_ref[rows, :], wu_ref[...].reshape(TN_FF, d)))

        @pl.when(j == nc - 1)
        def _():
            dv, dg = _rms_bwd(h1_ref[...], g_ref[...], acc[...])
            dh1_ref[...] = dh2_ref[...] + dv
            dg_ref[...] += dg

    rowblk = pl.BlockSpec((ts, d), lambda i, j: (i, 0))
    chunk = pl.BlockSpec((ts, TN_FF), lambda i, j: (i, j))
    vec = pl.BlockSpec((1, d), lambda i, j: (0, 0))
    return pl.pallas_call(
        body, grid=(s // ts, nc), name="ffn_bwd",
        out_shape=(jax.ShapeDtypeStruct((s, D_FF), BF16), jax.ShapeDtypeStruct((s, D_FF), BF16),
                   jax.ShapeDtypeStruct((s, d), F32), jax.ShapeDtypeStruct((1, d), F32)),
        in_specs=[rowblk, chunk, chunk, _spec_ff(FF_DOWN), _spec_ff(FF_GATE), _spec_ff(FF_UP), rowblk, rowblk, vec],
        out_specs=(chunk, chunk, rowblk, vec),
        scratch_shapes=[pltpu.VMEM((ts, d), F32)],
        compiler_params=_params(2),
    )(dff, gate, up, wd, wg, wu, h1, dh2, g_pre)


def _post_attn_bwd(dh1, o, a, mpre, wout, wpool, g_post, g_attn, g_pool, pscale, send):
    s, d = dh1.shape
    sub = TS // TQ
    npc = len(send)

    def body(dh1_ref, o_ref, a_ref, mp_ref, wo_ref, wp_ref, gpost_ref, ga_ref, gp_ref, ps_ref, *refs):
        send_refs, refs = refs[:npc], refs[npc:]
        dob_ref, dat_ref, dlt_ref, dmpb_ref, dy_ref, dgpost_ref, dga_ref, dgp_ref, dps_ref = refs[:9]
        got_refs, (send_sems, recv_sems) = refs[9:9 + npc], refs[9 + npc:]
        i = pl.program_id(0)

        @pl.when(i == 0)
        def _():
            for cp in _pair_copies(send_refs, got_refs, send_sems, recv_sems):
                cp.start()
            dgpost_ref[...] = jnp.zeros_like(dgpost_ref)
            dga_ref[...] = jnp.zeros_like(dga_ref)
            dgp_ref[...] = jnp.zeros_like(dgp_ref)
            dps_ref[...] = jnp.zeros_like(dps_ref)

        do, dg = _rms_bwd(o_ref[...], gpost_ref[...], dh1_ref[...])
        dgpost_ref[...] += dg
        dob = do.astype(BF16)
        dob_ref[...] = dob
        dmix = _nt(dob, wo_ref[...].reshape(d, d))

        av = a_ref[...]
        da, dg = _rms_bwd(av, ga_ref[...], dmix[:, 0:D_ATTN])
        dga_ref[...] += dg
        dat = da.astype(BF16).T
        hsel = (lax.shift_right_logical(lax.broadcasted_iota(jnp.int32, (HEADS, D_ATTN), 1), 6)
                == lax.broadcasted_iota(jnp.int32, (HEADS, D_ATTN), 0)).astype(F32)
        dlt = lax.dot_general(hsel, da * av, (((1,), (1,)), ((), ())), precision=HIGHEST, preferred_element_type=F32)
        for q in range(sub):
            dlt_ref[q] = dlt[:, q * TQ:(q + 1) * TQ]
            dat_ref[q] = dat[:, q * TQ:(q + 1) * TQ]

        ps = ps_ref[...]
        mp = mp_ref[...]
        dm, dg = _rms_bwd(mp * ps, gp_ref[...], dmix[:, D_ATTN:])
        dgp_ref[...] += dg
        dps_ref[...] += jnp.sum(dm * mp, axis=0, keepdims=True)
        dmpb = (dm * ps).astype(BF16)
        dmpb_ref[...] = dmpb
        for g in range(len(POOL_WINDOWS)):
            cols = slice(g * POOL_CH, (g + 1) * POOL_CH)
            dy_ref[:, cols] = _nt(dmpb[:, cols], wp_ref[g])

        @pl.when(i == s // TS - 1)
        def _():
            for cp in _pair_copies(send_refs, got_refs, send_sems, recv_sems):
                cp.wait()

    rowblk = pl.BlockSpec((TS, d), _row)
    half = pl.BlockSpec((TS, D_ATTN), _row)
    vec = lambda n: pl.BlockSpec((1, n), _fixed)
    nk = N_DEV // 2
    res = pl.pallas_call(
        body, grid=(s // TS,), name="post_attn_bwd",
        out_shape=(jax.ShapeDtypeStruct((s, d), BF16), jax.ShapeDtypeStruct((s // TQ, D_ATTN, TQ), BF16),
                   jax.ShapeDtypeStruct((s // TQ, HEADS, TQ), F32), jax.ShapeDtypeStruct((s, D_POOL), BF16),
                   jax.ShapeDtypeStruct((s, D_POOL), F32), jax.ShapeDtypeStruct((1, d), F32),
                   jax.ShapeDtypeStruct((1, D_ATTN), F32), jax.ShapeDtypeStruct((1, D_POOL), F32),
                   jax.ShapeDtypeStruct((1, D_POOL), F32))
        + tuple(jax.ShapeDtypeStruct((nk,) + t.shape[1:], t.dtype) for t in send),
        in_specs=[rowblk, rowblk, half, half, _spec_square(0),
                  pl.BlockSpec(wpool.shape, lambda i: (0, 0, 0)), vec(d), vec(D_ATTN), vec(D_POOL), vec(D_POOL)]
        + [ANY] * npc,
        out_specs=(rowblk, pl.BlockSpec((sub, D_ATTN, TQ), lambda i: (i, 0, 0)),
                   pl.BlockSpec((sub, HEADS, TQ), lambda i: (i, 0, 0)), half, half,
                   vec(d), vec(D_ATTN), vec(D_POOL), vec(D_POOL)) + (ANY,) * npc,
        scratch_shapes=[pltpu.SemaphoreType.DMA((nk, npc)), pltpu.SemaphoreType.DMA((nk, npc))],
        compiler_params=_params(1),
    )(dh1, o, a, mpre, wout, wpool, g_post, g_attn, g_pool, pscale, *send)
    return res[:9], list(res[9:])


def _attn_bwd(ka, v, kt3, qat3, qt3, dot3, lset3, dlt3, chip_blocks, small_block):
    s = ka.shape[0]
    nq = s // TQ

    def body(ka_ref, v_ref, kt_ref, qat_ref, qt_ref, dot_ref, lset_ref, dlt_ref, b_ref, sm_ref,
             dqt_ref, dkt_ref, dvt_ref, got_ref, all_ref, pt_scr, ptb_scr, dsb_scr,
             stage, send_sems, recv_sems, local_sem, stage_s, send_s, recv_s, local_s):
        j = pl.program_id(0)

        @pl.when(j == 0)
        def _():
            _chips_start(b_ref, got_ref, stage, send_sems, recv_sems, local_sem)
            _gather_start(sm_ref, all_ref, stage_s, send_s, recv_s, local_s)
            dqt_ref[...] = jnp.zeros_like(dqt_ref)

        @pl.when(j == max(nq - 2, 0))
        def _():
            _gather_pass_on(all_ref, send_s, recv_s)

        def tile(i, masked):
            def accumulate(ref, idx, val):
                if masked:
                    ref[idx] = val
                else:
                    ref[idx] += val

            for h in range(HEADS):
                aug = slice(h * AUG, (h + 1) * AUG)
                st = _nn(ka_ref[:, aug], qat_ref[i, aug, :]) - lset_ref[i, h:h + 1, :]
                if masked:
                    st = jnp.where(_causal_in_tile(), st, NEG)
                pt = jnp.exp2(st)
                pt_scr[h] = pt
                ptb_scr[h] = pt.astype(BF16)
            heads = [(h, slice(h * HEAD_DIM, (h + 1) * HEAD_DIM)) for h in range(HEADS)]
            for h, hs in heads:
                dst = pt_scr[h] * (_nn(v_ref[:, hs], dot_ref[i, hs, :]) - dlt_ref[i, h:h + 1, :])
                dsb_scr[h] = dst.astype(BF16)
            for h, hs in heads:
                accumulate(dvt_ref, (0, hs, slice(None)), _nt(dot_ref[i, hs, :], ptb_scr[h]))
            for h, hs in heads:
                rows = slice(h * VROWS, (h + 1) * VROWS)
                accumulate(dkt_ref, (0, rows, slice(None)), _nt(qt_ref[i, rows, :], dsb_scr[h]))
            for h, hs in heads:
                rows = slice(h * VROWS, (h + 1) * VROWS)
                dqt_ref[i, rows, :] += _nn(kt_ref[0, rows, :], dsb_scr[h])

        first = j + 1
        pairs = (nq - first) // 2

        def step(p, carry):
            tile(first + 2 * p, False)
            tile(first + 2 * p + 1, False)
            return carry

        tile(j, True)
        lax.fori_loop(0, pairs, step, 0)

        @pl.when(first + 2 * pairs < nq)
        def _():
            tile(nq - 1, False)

        @pl.when(j == nq - 1)
        def _():
            _chips_finish(b_ref, got_ref, send_sems, recv_sems)
            _gather_finish(sm_ref, all_ref, send_s, recv_s)

    blk = pl.BlockSpec((TQ, D_ATTN), _row)
    tile_t = lambda rows: pl.BlockSpec((1, rows, TQ), lambda j: (j, 0, 0))
    per_tile = lambda rows: jax.ShapeDtypeStruct((nq, rows, TQ), F32)
    _, r, cdim = chip_blocks.shape
    dma = pltpu.SemaphoreType.DMA
    return pl.pallas_call(
        body, grid=(nq,), name="attn_bwd",
        out_shape=(per_tile(HEADS * VROWS), per_tile(HEADS * VROWS), per_tile(D_ATTN),
                   jax.ShapeDtypeStruct(chip_blocks.shape, chip_blocks.dtype),
                   jax.ShapeDtypeStruct((N_DEV,) + small_block.shape, small_block.dtype)),
        in_specs=[pl.BlockSpec((TQ, HEADS * AUG), _row), blk, tile_t(HEADS * VROWS),
                  VMEM_WHOLE, VMEM_WHOLE, VMEM_WHOLE, VMEM_WHOLE, VMEM_WHOLE, ANY, ANY],
        out_specs=(pl.BlockSpec((nq, HEADS * VROWS, TQ), lambda j: (0, 0, 0)), tile_t(HEADS * VROWS), tile_t(D_ATTN),
                   ANY, ANY),
        scratch_shapes=[pltpu.VMEM((HEADS, TQ, TQ), F32), pltpu.VMEM((HEADS, TQ, TQ), BF16),
                        pltpu.VMEM((HEADS, TQ, TQ), BF16), pltpu.VMEM((r, cdim), chip_blocks.dtype),
                        dma((3,)), dma((3,)), dma,
                        pltpu.VMEM(small_block.shape, small_block.dtype), dma((7,)), dma((7,)), dma],
        compiler_params=_params(1),
    )(ka, v, kt3, qat3, qt3, dot3, lset3, dlt3, chip_blocks, small_block)


def _pre_attn_bwd(dqt3, dkt3, dvt3, fl, dy, x, dh1, g1, wqkv, wf, wu):
    s, d = x.shape
    nt = s // TS
    n = TS + HALO
    sub = TS // TQ
    qkv, fcols = 3 * D_ATTN, 3 * D_ATTN + LANES

    def body(dqt_ref, dkt_ref, dvt_ref, fl_ref, dy_ref, x_ref, dh1_ref, g_ref, wqkv_ref, wf_ref, wu_ref,
             gx_ref, dz_ref, dg_ref, db_ref, ybuf, ccar, dlog, dsum):
        dqkv_ref = dz_ref.at[:, 0:qkv]
        dfb_ref = dz_ref.at[:, qkv:fcols]
        dub_ref = dz_ref.at[:, fcols:]
        i = pl.program_id(0)
        ti = nt - 1 - i

        @pl.when(i == 0)
        def _():
            ybuf[TS:n, :] = jnp.zeros((HALO, D_POOL), F32)
            ccar[...] = jnp.zeros_like(ccar)
            dg_ref[...] = jnp.zeros_like(dg_ref)
            db_ref[...] = jnp.zeros_like(db_ref)
            dsum[...] = jnp.zeros_like(dsum)

        for a in range(sub):
            for h in range(HEADS):
                r = h * VROWS + HEAD_DIM
                dsum[h:h + 1, a * TQ:(a + 1) * TQ] = dqt_ref[a, r:r + 1, :] - dkt_ref[a, r:r + 1, :]
        rr = lax.broadcasted_iota(jnp.int32, (TS, TS), 0)
        cc = lax.broadcasted_iota(jnp.int32, (TS, TS), 1)
        dlog[...] = ccar[...] + _mask_matmul((cc >= rr).astype(BF16), dsum[...].T)
        ccar[...] = dlog[0:1, :]
        df = dlog[...] * jax.nn.sigmoid(-fl_ref[...])
        db_ref[...] += jnp.sum(df, axis=0, keepdims=True)
        dfb = df.astype(BF16)
        dfb_ref[...] = dfb

        t = ti * TS + lax.broadcasted_iota(jnp.int32, (TS, 1), 0)
        dy = dy_ref[...]
        for g, w in enumerate(POOL_WINDOWS):
            cols = slice(g * POOL_CH, (g + 1) * POOL_CH)
            ybuf[0:TS, cols] = dy[:, cols] / jnp.minimum(t + 1, w).astype(F32)
        for g, w in enumerate(POOL_WINDOWS):
            cols = slice(g * POOL_CH, (g + 1) * POOL_CH)
            sm = ybuf[:, cols]
            step = 1
            while step < w:
                sm = sm + pltpu.roll(sm, n - step, 0)
                step *= 2
            dub_ref[:, cols] = (sm[0:TS, :] - dy[:, cols]).astype(BF16)
        ybuf[TS:n, :] = ybuf[0:HALO, :]

        for a in range(sub):
            rows = slice(a * TQ, (a + 1) * TQ)
            for h in range(HEADS):
                src = slice(h * VROWS, h * VROWS + HEAD_DIM)
                dqkv_ref[rows, h * HEAD_DIM:(h + 1) * HEAD_DIM] = (dqt_ref[a, src, :].T * 0.125).astype(BF16)
                dqkv_ref[rows, D_ATTN + h * HEAD_DIM:D_ATTN + (h + 1) * HEAD_DIM] = dkt_ref[a, src, :].T.astype(BF16)
            dqkv_ref[rows, 2 * D_ATTN:] = dvt_ref[a].T.astype(BF16)
        dhn = _nn(dqkv_ref[...], wqkv_ref[...]) + _nn(dfb, wf_ref[...]) + _nn(dub_ref[...], wu_ref[...])
        dx, dg = _rms_bwd(x_ref[...], g_ref[...], dhn)
        gx_ref[...] = dh1_ref[...] + dx
        dg_ref[...] += dg

    rev = lambda i: (nt - 1 - i, 0)
    blk = lambda w: pl.BlockSpec((TS, w), rev)
    return pl.pallas_call(
        body, grid=(nt,), name="pre_attn_bwd",
        out_shape=(jax.ShapeDtypeStruct((s, d), F32), jax.ShapeDtypeStruct((s, fcols + D_POOL), BF16),
                   jax.ShapeDtypeStruct((1, d), F32), jax.ShapeDtypeStruct((1, LANES), F32)),
        in_specs=[pl.BlockSpec((sub, HEADS * VROWS, TQ), lambda i: (nt - 1 - i, 0, 0)),
                  pl.BlockSpec((sub, HEADS * VROWS, TQ), lambda i: (nt - 1 - i, 0, 0)),
                  pl.BlockSpec((sub, D_ATTN, TQ), lambda i: (nt - 1 - i, 0, 0)),
                  blk(LANES), blk(D_POOL), blk(d), blk(d),
                  pl.BlockSpec((1, d), _fixed), pl.BlockSpec((qkv, d), _fixed), pl.BlockSpec(wf.shape, _fixed),
                  pl.BlockSpec(wu.shape, _fixed)],
        out_specs=(blk(d), blk(fcols + D_POOL), pl.BlockSpec((1, d), _fixed), pl.BlockSpec((1, LANES), _fixed)),
        scratch_shapes=[pltpu.VMEM((n, D_POOL), F32), pltpu.VMEM((1, LANES), F32), pltpu.VMEM((TS, LANES), F32),
                        pltpu.VMEM((LANES, TS), F32)],
        compiler_params=_params(1),
    )(dqt3, dkt3, dvt3, fl, dy, x, dh1, g1, wqkv, wf, wu)


def _wgrad(a, b, out_dtype, name):
    s, m = a.shape
    n = b.shape[1]
    tm = max(t for t in range(LANES, min(m, TM_WGRAD) + 1, LANES) if m % t == 0)
    ts = min(TS_WGRAD, s)
    ns = s // ts

    def body(a_ref, b_ref, o_ref, acc):
        i = pl.program_id(1)

        @pl.when(i == 0)
        def _():
            acc[...] = jnp.zeros_like(acc)

        acc[...] += _tn(a_ref[...], b_ref[...])

        @pl.when(i == ns - 1)
        def _():
            o_ref[...] = acc[...].astype(out_dtype)

    return pl.pallas_call(
        body, grid=(m // tm, ns), name=name, out_shape=jax.ShapeDtypeStruct((m, n), out_dtype),
        in_specs=[pl.BlockSpec((ts, tm), lambda j, i: (i, j)), pl.BlockSpec((ts, n), lambda j, i: (i, 0))],
        out_specs=pl.BlockSpec((tm, n), lambda j, i: (j, 0)),
        scratch_shapes=[pltpu.VMEM((tm, n), F32)],
        compiler_params=_params(2),
    )(a, b)


def _wgrad_in(dz, hn):
    s, m = dz.shape
    n = hn.shape[1]
    ts = min(TS_WGRAD, s)
    ns = s // ts
    pad_at, pad = 3 * D_ATTN + HEADS, LANES - HEADS
    assert m == D_IN + pad and N_DEV * SHARD_IN == D_IN

    def pieces(d):
        lo, hi = d * SHARD_IN, (d + 1) * SHARD_IN
        spans = [(lo, min(hi, pad_at), 0), (max(lo, pad_at), hi, pad)]
        return [(a + shift, b - a, a - lo) for a, b, shift in spans if b > a]

    def body(a_ref, b_ref, o_ref, acc, stage):
        i = pl.program_id(0)

        @pl.when(i == 0)
        def _():
            acc[...] = jnp.zeros_like(acc)

        acc[...] += _tn(a_ref[...], b_ref[...])

        @pl.when(i == ns - 1)
        def _():
            stage[SHARD_IN:ROWS_IN, :] = jnp.zeros((ROWS_IN - SHARD_IN, n), F32)
            for d in range(N_DEV):
                for src, rows, dst in pieces(d):
                    stage[dst:dst + rows, :] = acc[src:src + rows, :]
                o_ref[d] = stage[...].astype(BF16)

    return pl.pallas_call(
        body, grid=(ns,), name="wgrad_in", out_shape=jax.ShapeDtypeStruct((N_DEV, ROWS_IN, n), BF16),
        in_specs=[pl.BlockSpec((ts, m), _row), pl.BlockSpec((ts, n), _row)],
        out_specs=pl.BlockSpec((N_DEV, ROWS_IN, n), lambda i: (0, 0, 0)),
        scratch_shapes=[pltpu.VMEM((m, n), F32), pltpu.VMEM((ROWS_IN, n), F32)],
        compiler_params=_params(1),
    )(dz, hn)


def _adamw(w, g, m, v):
    m = ADAM_B1 * m + (1.0 - ADAM_B1) * g
    v = ADAM_B2 * v + (1.0 - ADAM_B2) * (g * g)
    m_hat = m / (1.0 - ADAM_B1 ** ADAM_STEP)
    v_hat = v / (1.0 - ADAM_B2 ** ADAM_STEP)
    delta = -ADAM_LR * (m_hat / (jnp.sqrt(v_hat) + ADAM_EPS) + ADAM_WD * w)
    return delta, m, v


def _sum_update(p_ref, w_ref, m_ref, v_ref, g_ref, d_ref, nm_ref, nv_ref):
    g = p_ref[0].astype(F32)
    for k in range(1, p_ref.shape[0]):
        g = g + p_ref[k].astype(F32)
    g_ref[...] = g
    d_ref[...], nm_ref[...], nv_ref[...] = _adamw(w_ref[...], g, m_ref[...], v_ref[...])


def _reduce_update_rest(parts, w, m, v, chip_blocks, small_block):
    nk, r, c = parts.shape
    ns = r // TR_REST

    def body(p_ref, w_ref, m_ref, v_ref, b_ref, sm_ref, g_ref, d_ref, nm_ref, nv_ref, got_ref, all_ref,
             stage_b, stage_s, send_b, recv_b, local_b, send_s, recv_s, local_s):
        i = pl.program_id(0)

        @pl.when(i == 0)
        def _():
            _chips_start(b_ref, got_ref, stage_b, send_b, recv_b, local_b)
            _gather_start(sm_ref, all_ref, stage_s, send_s, recv_s, local_s)

        _sum_update(p_ref, w_ref, m_ref, v_ref, g_ref, d_ref, nm_ref, nv_ref)

        @pl.when(i == ns - 1)
        def _():
            _gather_pass_on(all_ref, send_s, recv_s)
            _chips_finish(b_ref, got_ref, send_b, recv_b)
            _gather_finish(sm_ref, all_ref, send_s, recv_s)

    blk = pl.BlockSpec((TR_REST, c), _row)
    out = jax.ShapeDtypeStruct((r, c), F32)
    dma = pltpu.SemaphoreType.DMA
    return pl.pallas_call(
        body, grid=(ns,), name="reduce_update_rest",
        out_shape=(out,) * 4 + (jax.ShapeDtypeStruct(chip_blocks.shape, chip_blocks.dtype),
                                jax.ShapeDtypeStruct((N_DEV,) + small_block.shape, small_block.dtype)),
        in_specs=[pl.BlockSpec((nk, TR_REST, c), lambda i: (0, i, 0)), blk, blk, blk, ANY, ANY],
        out_specs=(blk,) * 4 + (ANY, ANY),
        scratch_shapes=[pltpu.VMEM(chip_blocks.shape[1:], chip_blocks.dtype), pltpu.VMEM(small_block.shape, small_block.dtype),
                        dma((3,)), dma((3,)), dma, dma((7,)), dma((7,)), dma],
        compiler_params=_params(1),
    )(parts, w, m, v, chip_blocks, small_block)


def _reduce_update_big(parts, w, m, v, tr, name):
    nk, r, c = parts.shape

    def body(p_ref, w_ref, m_ref, v_ref, g_ref, d_ref, nm_ref, nv_ref):
        _sum_update(p_ref, w_ref, m_ref, v_ref, g_ref, d_ref, nm_ref, nv_ref)

    blk = pl.BlockSpec((tr, c), _row)
    out = jax.ShapeDtypeStruct((r, c), F32)
    return pl.pallas_call(
        body, grid=(r // tr,), name=name, out_shape=(out,) * 4,
        in_specs=[pl.BlockSpec((nk, tr, c), lambda i: (0, i, 0)), blk, blk, blk],
        out_specs=(blk,) * 4, compiler_params=_params(1),
    )(parts, w, m, v)


def _reduce_update_small(parts, late, w, m, v):
    nd = parts.shape[0]
    first = parts.shape[1] - late.shape[1]

    def body(p_ref, q_ref, w_ref, m_ref, v_ref, g_ref, d_ref, nm_ref, nv_ref):
        g, t = p_ref[0], q_ref[0]
        for k in range(1, nd):
            g, t = g + p_ref[k], t + q_ref[k]
        g_ref[...] = g
        g_ref[first:, :] = g[first:, :] + t
        d_ref[...], nm_ref[...], nv_ref[...] = _adamw(w_ref[...], g_ref[...], m_ref[...], v_ref[...])

    out = jax.ShapeDtypeStruct(w.shape, F32)
    return pl.pallas_call(body, name="reduce_update_small", out_shape=(out,) * 4,
                          compiler_params=pltpu.CompilerParams(vmem_limit_bytes=VMEM_LIMIT))(parts, late, w, m, v)


MESH = pl.DeviceIdType.MESH


def _copy_through_vmem(src_hbm, dst_hbm, stage, sem):
    load = pltpu.make_async_copy(src_hbm, stage, sem)
    load.start()
    load.wait()
    store = pltpu.make_async_copy(stage, dst_hbm, sem)
    store.start()
    store.wait()


class _GatherPlan:
    def __init__(self, x_ref, out_ref, send_sems, recv_sems):
        x, y, c = lax.axis_index("x"), lax.axis_index("y"), lax.axis_index("c")
        self.me, self.sibling, self.c = (x, y, c), (x, y, 1 - c), c
        self.chips = [(1 - x, y), (x, 1 - y), (1 - x, 1 - y)]
        self.x_ref, self.out_ref, self.send_sems, self.recv_sems = x_ref, out_ref, send_sems, recv_sems

    def slot(self, px, py, pc):
        return self.out_ref.at[4 * px + 2 * py + pc]

    def copy(self, k, block, to, src=None):
        return pltpu.make_async_remote_copy(
            src_ref=self.slot(*block) if src is None else src, dst_ref=self.slot(*block),
            send_sem=self.send_sems.at[k], recv_sem=self.recv_sems.at[k], device_id=to, device_id_type=MESH)

    def first(self):
        return [self.copy(0, self.me, self.sibling, src=self.x_ref)] + [
            self.copy(1 + j, self.me, (*chip, self.c), src=self.x_ref) for j, chip in enumerate(self.chips)]

    def passed(self):
        return [self.copy(4 + j, (*chip, self.c), self.sibling) for j, chip in enumerate(self.chips)]


def _gather_start(x_ref, out_ref, stage, send_sems, recv_sems, local_sem):
    plan = _GatherPlan(x_ref, out_ref, send_sems, recv_sems)
    for cp in plan.first():
        cp.start()
    _copy_through_vmem(x_ref, plan.slot(*plan.me), stage, local_sem)


def _gather_pass_on(out_ref, send_sems, recv_sems):
    plan = _GatherPlan(None, out_ref, send_sems, recv_sems)
    passed = plan.passed()
    for j, chip in enumerate(plan.chips):
        plan.copy(1 + j, (*chip, plan.c), plan.me).wait_recv()
        passed[j].start()


def _gather_finish(x_ref, out_ref, send_sems, recv_sems):
    plan = _GatherPlan(x_ref, out_ref, send_sems, recv_sems)
    plan.copy(0, plan.sibling, plan.me).wait_recv()
    for j, chip in enumerate(plan.chips):
        plan.copy(4 + j, (*chip, 1 - plan.c), plan.me).wait_recv()
    for cp in plan.first() + plan.passed():
        cp.wait_send()


def _all_gather(xs, name):
    r, cdim = xs.shape

    def body(x_ref, out_ref, stage, send_sems, recv_sems, local_sem):
        _gather_start(x_ref, out_ref, stage, send_sems, recv_sems, local_sem)
        _gather_pass_on(out_ref, send_sems, recv_sems)
        _gather_finish(x_ref, out_ref, send_sems, recv_sems)

    return pl.pallas_call(
        body, name=name, out_shape=jax.ShapeDtypeStruct((N_DEV, r, cdim), xs.dtype),
        in_specs=[ANY], out_specs=ANY,
        scratch_shapes=[pltpu.VMEM((r, cdim), xs.dtype), pltpu.SemaphoreType.DMA((7,)), pltpu.SemaphoreType.DMA((7,)),
                        pltpu.SemaphoreType.DMA],
        compiler_params=pltpu.CompilerParams(vmem_limit_bytes=VMEM_LIMIT),
    )(xs)


def _pair_copies(src_refs, dst_refs, send_sems, recv_sems):
    x, y, c = lax.axis_index("x"), lax.axis_index("y"), lax.axis_index("c")
    return [pltpu.make_async_remote_copy(
        src_ref=src.at[2 * k + (1 - c)], dst_ref=dst.at[k], send_sem=send_sems.at[k, p], recv_sem=recv_sems.at[k, p],
        device_id=(x, y, 1 - c), device_id_type=MESH)
        for k in range(N_DEV // 2) for p, (src, dst) in enumerate(zip(src_refs, dst_refs))]


def _rs_pair_sum(core, pieces, offsets, rows, name, landed=()):
    cdim = pieces[0].shape[2]
    nk = N_DEV // 2
    npc = len(pieces)
    nrem = npc - len(landed)
    spans = [(o, t.shape[1]) for t, o in zip(pieces, offsets)]
    ends = [o + n for o, n in spans]
    gaps = [(a, b - a) for a, b in zip(ends, [o for o, _ in spans[1:]] + [rows]) if b > a]

    def body(core_ref, *refs):
        own, src, got, o_ref = refs[:npc], refs[npc:npc + nrem], refs[npc + nrem:2 * npc], refs[2 * npc]
        landing, send_sems, recv_sems = refs[2 * npc + 1:]
        k = pl.program_id(0)
        x, y, c = lax.axis_index("x"), lax.axis_index("y"), lax.axis_index("c")

        def copies(kk):
            return [pltpu.make_async_remote_copy(
                src_ref=src[p].at[2 * kk + (1 - c)], dst_ref=landing.at[kk, pl.ds(o, n)],
                send_sem=send_sems.at[kk, p], recv_sem=recv_sems.at[kk, p], device_id=(x, y, 1 - c),
                device_id_type=MESH) for p, (o, n) in enumerate(spans[:nrem])]

        @pl.when(k == 0)
        def _():
            for kk in range(nk):
                for cp in copies(kk):
                    cp.start()

        for cp, piece, (o, n) in zip(copies(k), own, spans):
            cp.wait_recv()
            o_ref[0, o:o + n, :] = (piece[0].astype(F32) + landing[k, o:o + n, :].astype(F32)).astype(BF16)
        for theirs, piece, (o, n) in zip(got, own[nrem:], spans[nrem:]):
            o_ref[0, o:o + n, :] = (piece[0].astype(F32) + theirs[0].astype(F32)).astype(BF16)
        for o, n in gaps:
            o_ref[0, o:o + n, :] = jnp.zeros((n, cdim), BF16)

        @pl.when(k == nk - 1)
        def _():
            for kk in range(nk):
                for cp in copies(kk):
                    cp.wait_send()

    own_specs = [pl.BlockSpec((1, n, cdim), lambda k, core_ref: (2 * k + core_ref[0], 0, 0)) for _, n in spans]
    got_specs = [pl.BlockSpec((1, n, cdim), lambda k, core_ref: (k, 0, 0)) for _, n in spans[nrem:]]
    land_rows = max(o + n for o, n in spans[:nrem])
    return pl.pallas_call(
        body, name=name, out_shape=jax.ShapeDtypeStruct((nk, rows, cdim), BF16),
        grid_spec=pltpu.PrefetchScalarGridSpec(
            num_scalar_prefetch=1, grid=(nk,),
            in_specs=own_specs + [ANY] * nrem + got_specs,
            out_specs=pl.BlockSpec((1, rows, cdim), lambda k, core_ref: (k, 0, 0)),
            scratch_shapes=[pltpu.VMEM((nk, land_rows, cdim), BF16), pltpu.SemaphoreType.DMA((nk, nrem)),
                            pltpu.SemaphoreType.DMA((nk, nrem))]),
        compiler_params=_params(1),
    )(core, *pieces, *pieces[:nrem], *landed)


def _chips_start(b_ref, out_ref, stage, send_sems, recv_sems, local_sem):
    x, y, c = lax.axis_index("x"), lax.axis_index("y"), lax.axis_index("c")
    mychip = 2 * x + y
    for j, (px, py) in enumerate([(1 - x, y), (x, 1 - y), (1 - x, 1 - y)]):
        pltpu.make_async_remote_copy(
            src_ref=b_ref.at[2 * px + py], dst_ref=out_ref.at[mychip],
            send_sem=send_sems.at[j], recv_sem=recv_sems.at[j], device_id=(px, py, c), device_id_type=MESH).start()
    _copy_through_vmem(b_ref.at[mychip], out_ref.at[mychip], stage, local_sem)


def _chips_finish(b_ref, out_ref, send_sems, recv_sems):
    x, y, c = lax.axis_index("x"), lax.axis_index("y"), lax.axis_index("c")
    for j, (px, py) in enumerate([(1 - x, y), (x, 1 - y), (1 - x, 1 - y)]):
        pltpu.make_async_remote_copy(
            src_ref=b_ref.at[2 * px + py], dst_ref=out_ref.at[2 * px + py],
            send_sem=send_sems.at[j], recv_sem=recv_sems.at[j], device_id=(px, py, c), device_id_type=MESH).wait()


def _pad_rows(a, rows):
    return jnp.pad(a, ((0, rows - a.shape[0]), (0, 0)))


def _pack_in(w_in):
    return _pad_rows(w_in[0].T, ROWS_IN)


def _unpack_in(r):
    return r[0:SHARD_IN].T[None]


def _pack_rest(w_out, w_gate, w_up, w_down, w_ple, w_pg):
    head = _pad_rows(jnp.concatenate([w_out[0], w_pg[0], w_ple[0].T.reshape(32, D_MODEL)], axis=0), OFF_GATE)
    return jnp.concatenate([head, w_gate[0].T, w_up[0].T, w_down[0]], axis=0)


def _unpack_rest(r):
    return (r[0:OFF_PG][None], r[OFF_GATE:OFF_UP].T[None], r[OFF_UP:OFF_DOWN].T[None], r[OFF_DOWN:ROWS_REST][None],
            r[OFF_PLE:OFF_PLE + 32].reshape(128, D_PLE).T[None], r[OFF_PG:OFF_PLE][None])


def _pack_small(w_pool, g_mix_pre, g_mix_post, g_ffn_pre, g_ffn_post, g_ple, g_attn, g_pool, pool_scale, b_forget,
                loss=None):
    row = lambda vrow: vrow.reshape(1, -1)
    misc = [row(pool_scale), row(b_forget), row(loss) if loss is not None else jnp.zeros((1, 1), F32),
            jnp.zeros((1, D_MODEL - COL_LOSS - 1), F32)]
    rows = [w_pool.reshape(64, D_MODEL), row(g_mix_pre), row(g_mix_post), row(g_ffn_pre), row(g_ffn_post), row(g_ple),
            jnp.concatenate([row(g_attn), row(g_pool)], axis=1), jnp.concatenate(misc, axis=1),
            jnp.zeros((SMALL_ROWS - ROW_MISC - 1, D_MODEL), F32)]
    return jnp.concatenate(rows, axis=0)


def _pack_small_late(g_mix_pre, b_forget):
    misc = [jnp.zeros((1, COL_B_FORGET), F32), b_forget.reshape(1, -1), jnp.zeros((1, D_MODEL - COL_LOSS), F32)]
    return jnp.concatenate([g_mix_pre.reshape(1, -1), jnp.zeros((ROW_MISC - ROW_G_MIX_PRE - 1, D_MODEL), F32),
                            jnp.concatenate(misc, axis=1), jnp.zeros((SMALL_ROWS - ROW_MISC - 1, D_MODEL), F32)], axis=0)


def _unpack_small(r):
    gains, misc = r[ROW_GROUP_GAINS:ROW_GROUP_GAINS + 1], r[ROW_MISC:ROW_MISC + 1]
    return dict(
        w_pool=r[0:64].reshape(1, 4, POOL_CH, POOL_CH), g_mix_pre=r[ROW_G_MIX_PRE:ROW_G_MIX_PRE + 1],
        g_mix_post=r[ROW_G_MIX_POST:ROW_G_MIX_POST + 1], g_ffn_pre=r[ROW_G_FFN_PRE:ROW_G_FFN_PRE + 1],
        g_ffn_post=r[ROW_G_FFN_POST:ROW_G_FFN_POST + 1], g_ple=r[ROW_G_PLE:ROW_G_PLE + 1],
        g_attn_grp=gains[:, 0:D_ATTN], g_pool_grp=gains[:, D_ATTN:D_ATTN + D_POOL],
        pool_scale=misc[:, 0:D_POOL], b_forget=misc[:, COL_B_FORGET:COL_B_FORGET + HEADS])


def _step(x, p, tgt, small, in_w, in_m, in_v, rest_w, rest_m, rest_v):
    core = lax.axis_index("c").astype(jnp.int32).reshape(1)
    win_t = _all_gather(in_w.astype(BF16), "gather_w_in")[:, 0:SHARD_IN].reshape(D_IN, D_MODEL)
    wqkv = win_t
    wf = _pad_rows(win_t[3 * D_ATTN:3 * D_ATTN + HEADS], LANES)
    wu = win_t[3 * D_ATTN + HEADS:]
    wpool = small["w_pool"].astype(BF16)
    bpad = jnp.pad(small["b_forget"], ((0, 0), (0, LANES - HEADS)))

    lay = _attn_layout_constants()
    rest_b = rest_w.astype(BF16)
    hn, qt3, ka, v, qat3, vt3, kt3, fl, y, mpre, gh = _pre_attn_fwd(x, small["g_mix_pre"], wqkv, wf, wu, bpad, wpool, lay,
                                                                 rest_b[0:OFF_UP])
    a, lset3, gf = _attn_fwd(ka, qat3, vt3, rest_b[OFF_UP:])
    wple_t = gh[:, OFF_PLE:OFF_PLE + 32].reshape(D_MODEL, D_PLE)
    mix, o, h1, hn2 = _post_attn_fwd(a, mpre, x, small["g_attn_grp"], small["g_pool_grp"], small["pool_scale"], gh,
                                     small["g_mix_post"], small["g_ffn_pre"])
    gate, up, act, ff, h2 = _ffn_fwd(hn2, gh, gf, gf, h1, small["g_ffn_post"])
    dh2, dff, dgl, dpp, h2b, pb, loss8, dg_ple, dg_ffn_post = _tail_fwd_bwd(
        h2, p, tgt, ff, wple_t, gh, small["g_ple"], small["g_ffn_post"])
    dgate, dup, dh1, dg_ffn_pre = _ffn_bwd(dff, gate, up, gf, gh, gf, h1, dh2, small["g_ffn_pre"])
    nd = N_DEV
    send_rest = [
        _wgrad(h2b, dgl, BF16, "wgrad_ple_gate").reshape(nd, 128, D_MODEL),
        _wgrad(dpp, pb, BF16, "wgrad_ple").reshape(nd, 32, D_MODEL),
        _wgrad(dgate, hn2, BF16, "wgrad_gate").reshape(nd, SHARD_FF, D_MODEL),
        _wgrad(dup, hn2, BF16, "wgrad_up").reshape(nd, SHARD_FF, D_MODEL),
        _wgrad(act, dff, BF16, "wgrad_down").reshape(nd, SHARD_FF, D_MODEL)]
    (dob, dat3, dlt3, dmpb, dy, dg_mix_post, dg_attn, dg_pool, dps), landed = _post_attn_bwd(
        dh1, o, a, mpre, gh, wpool, small["g_mix_post"], small["g_attn_grp"], small["g_pool_grp"], small["pool_scale"],
        send_rest)
    send_rest = [_wgrad(mix, dob, BF16, "wgrad_out").reshape(nd, 128, D_MODEL)] + send_rest
    pair_rest = _rs_pair_sum(core, send_rest, [0, OFF_PG, OFF_PLE, OFF_GATE, OFF_UP, OFF_DOWN], ROWS_REST,
                             "rs_pair_sum_rest", landed)

    dwp = _wgrad(y, dmpb, F32, "wgrad_pool")
    dw_pool = jnp.stack([dwp[g * POOL_CH:(g + 1) * POOL_CH, g * POOL_CH:(g + 1) * POOL_CH] for g in range(4)])
    small_part = _pack_small(dw_pool, jnp.zeros((1, D_MODEL), F32), dg_mix_post, dg_ffn_pre, dg_ffn_post, dg_ple,
                             dg_attn, dg_pool, dps, jnp.zeros((1, HEADS), F32), loss8[0:1, 0:1])
    dqt3, dkt3, dvt3, chips_rest, small_all = _attn_bwd(ka, v, kt3, qat3, qt3, dat3, lset3, dlt3, pair_rest, small_part)

    gx, dz, dg_mix_pre, db = _pre_attn_bwd(dqt3, dkt3, dvt3, fl, dy, x, dh1, small["g_mix_pre"], wqkv, wf, wu)

    pair_in = _rs_pair_sum(core, [_wgrad_in(dz, hn)], [0], ROWS_IN, "rs_pair_sum_in")

    small_late = _pack_small_late(dg_mix_pre, db[:, 0:HEADS])
    *upd_rest, chips_in, late_all = _reduce_update_rest(chips_rest, rest_w, rest_m, rest_v, pair_in, small_late)
    upd_in = _reduce_update_big(chips_in, in_w, in_m, in_v, ROWS_IN, "reduce_update_in")
    return gx, (small_all, late_all), upd_in, upd_rest


def kernel(x, p, g_mix_pre, w_in, b_forget, g_attn_grp, g_pool_grp, w_pool, pool_scale, w_out, g_mix_post, g_ffn_pre, w_ffn_gate, w_ffn_up, w_ffn_down, g_ffn_post, w_ple_proj, g_ple, w_ple_gate, loss_target, m_g_mix_pre, m_w_in, m_b_forget, m_g_attn_grp, m_g_pool_grp, m_w_pool, m_pool_scale, m_w_out, m_g_mix_post, m_g_ffn_pre, m_w_ffn_gate, m_w_ffn_up, m_w_ffn_down, m_g_ffn_post, m_w_ple_proj, m_g_ple, m_w_ple_gate, v_g_mix_pre, v_w_in, v_b_forget, v_g_attn_grp, v_g_pool_grp, v_w_pool, v_pool_scale, v_w_out, v_g_mix_post, v_g_ffn_pre, v_w_ffn_gate, v_w_ffn_up, v_w_ffn_down, v_g_ffn_post, v_w_ple_proj, v_g_ple, v_w_ple_gate):
    small = dict(w_pool=w_pool[0], g_mix_pre=g_mix_pre, g_mix_post=g_mix_post, g_ffn_pre=g_ffn_pre,
                 g_ffn_post=g_ffn_post, g_ple=g_ple, g_attn_grp=g_attn_grp, g_pool_grp=g_pool_grp,
                 pool_scale=pool_scale, b_forget=b_forget)
    gx, small_all, upd_in, upd_rest = _step(
        x[0], p[0, 0], loss_target[0], small, _pack_in(w_in), _pack_in(m_w_in), _pack_in(v_w_in),
        _pack_rest(w_out, w_ffn_gate, w_ffn_up, w_ffn_down, w_ple_proj, w_ple_gate),
        _pack_rest(m_w_out, m_w_ffn_gate, m_w_ffn_up, m_w_ffn_down, m_w_ple_proj, m_w_ple_gate),
        _pack_rest(v_w_out, v_w_ffn_gate, v_w_ffn_up, v_w_ffn_down, v_w_ple_proj, v_w_ple_gate))

    sm_w = _pack_small(w_pool, g_mix_pre, g_mix_post, g_ffn_pre, g_ffn_post, g_ple, g_attn_grp, g_pool_grp, pool_scale, b_forget)
    sm_m = _pack_small(m_w_pool, m_g_mix_pre, m_g_mix_post, m_g_ffn_pre, m_g_ffn_post, m_g_ple, m_g_attn_grp, m_g_pool_grp, m_pool_scale, m_b_forget)
    sm_v = _pack_small(v_w_pool, v_g_mix_pre, v_g_mix_post, v_g_ffn_pre, v_g_ffn_post, v_g_ple, v_g_attn_grp, v_g_pool_grp, v_pool_scale, v_b_forget)
    upd_small = _reduce_update_small(*small_all, sm_w, sm_m, sm_v)
    loss = upd_small[0][ROW_MISC, COL_LOSS]

    def leaves(k):
        b_out, b_gate, b_up, b_down, b_ple, b_pg = _unpack_rest(upd_rest[k])
        s = _unpack_small(upd_small[k])
        return (s["g_mix_pre"], _unpack_in(upd_in[k]), s["b_forget"], s["g_attn_grp"], s["g_pool_grp"], s["w_pool"],
                s["pool_scale"], b_out, s["g_mix_post"], s["g_ffn_pre"], b_gate, b_up, b_down, s["g_ffn_post"], b_ple,
                s["g_ple"], b_pg)

    return (loss, gx[None], *leaves(0), *leaves(1), *leaves(2), *leaves(3))
```

```python
import functools

import jax
import jax.numpy as jnp
from jax import lax
from jax.experimental import pallas as pl
from jax.experimental.pallas import tpu as pltpu

F32 = jnp.float32
BF16 = jnp.bfloat16
HIGHEST = lax.Precision.HIGHEST

D_MODEL = 1024
HEADS = 8
HEAD_DIM = 64
D_ATTN = HEADS * HEAD_DIM
POOL_WINDOWS = (2, 4, 8, 16)
POOL_CH = 128
D_POOL = POOL_CH * len(POOL_WINDOWS)
D_FF = 2816
D_PLE = 256
D_IN = 3 * D_ATTN + HEADS + D_POOL
RMS_EPS = 1e-6
N_DEV = 8

ADAM_LR = 0.001
ADAM_B1 = 0.9
ADAM_B2 = 0.999
ADAM_EPS = 1e-08
ADAM_WD = 0.01
ADAM_STEP = 10

LANES = 128
HALO = 16
TS = 512
TS_FF = 512
TS_WGRAD = 1024
TM_WGRAD = 2176
TQ = 256
TN_FF = 1408
NEG = -1e30
VMEM_LIMIT = 56 * 1024 * 1024

SHARD_IN = 257
ROWS_IN = 272
SHARD_FF = 352
OFF_PG = 128
OFF_PLE = 256
OFF_GATE = SHARD_FF
OFF_UP = 2 * SHARD_FF
OFF_DOWN = 3 * SHARD_FF
ROWS_REST = 4 * SHARD_FF
TR_REST = SHARD_FF

SMALL_ROWS = 72
ROW_G_MIX_PRE, ROW_G_MIX_POST, ROW_G_FFN_PRE, ROW_G_FFN_POST, ROW_G_PLE = 64, 65, 66, 67, 68
ROW_GROUP_GAINS, ROW_MISC = 69, 70
COL_B_FORGET = D_POOL
COL_LOSS = D_POOL + HEADS


def _nn(a, b):
    return jnp.dot(a, b, preferred_element_type=F32)


def _nt(a, b):
    return lax.dot_general(a, b, (((1,), (1,)), ((), ())), preferred_element_type=F32)


def _tn(a, b):
    return lax.dot_general(a, b, (((0,), (0,)), ((), ())), preferred_element_type=F32)


def _rstd(v):
    return lax.rsqrt(jnp.mean(v * v, axis=-1, keepdims=True) + RMS_EPS)


def _rms_bwd(v, g, dy):
    r = _rstd(v)
    vh = v * r
    t = dy * g
    dv = r * (t - vh * jnp.mean(t * vh, axis=-1, keepdims=True))
    return dv, jnp.sum(dy * vh, axis=0, keepdims=True)


def _split3(v):
    hi = v.astype(BF16)
    rest = v - hi.astype(F32)
    mid = rest.astype(BF16)
    return hi, mid, (rest - mid.astype(F32)).astype(BF16)


def _mask_matmul(mask, v):
    hi, mid, lo = _split3(v)
    return _nn(mask, lo) + _nn(mask, mid) + _nn(mask, hi)


def _params(n_grid):
    return pltpu.CompilerParams(dimension_semantics=("arbitrary",) * n_grid, vmem_limit_bytes=VMEM_LIMIT)


def _row(i):
    return (i, 0)


def _fixed(*_):
    return (0, 0)


def _spec_square(part):
    return pl.BlockSpec((N_DEV, 128, D_MODEL), lambda *_: (0, part, 0))


def _spec_ff(part):
    return pl.BlockSpec((TN_FF // SHARD_FF, SHARD_FF, D_MODEL), lambda i, j: (j, part, 0))


assert TS == 2 * TQ and TN_FF % SHARD_FF == 0
_HALVES = (slice(0, TQ), slice(TQ, TS))

VMEM_WHOLE = pl.BlockSpec(memory_space=pltpu.VMEM)
SMEM_WHOLE = pl.BlockSpec(memory_space=pltpu.SMEM)
ANY = pl.BlockSpec(memory_space=pl.ANY)


LOG2E = 1.4426950408889634
VROWS = HEAD_DIM + 16
AUG = 128
BIAS_LANE = HEAD_DIM
ONE_LANE = HEAD_DIM + 3
SPARE_LANE = HEADS
PART_LANES = 16
assert SPARE_LANE < PART_LANES and 3 * PART_LANES <= LANES


def _attn_layout_constants():
    import numpy as np
    place = np.zeros((D_ATTN, HEADS * AUG), np.float32)
    for r in range(D_ATTN):
        place[r, (r // HEAD_DIM) * AUG + r % HEAD_DIM] = 1.0
    bias_k = np.zeros((LANES, HEADS * AUG), np.float32)
    bias_q = np.zeros((LANES, HEADS * AUG), np.float32)
    for h in range(HEADS):
        for part in range(3):
            bias_k[part * PART_LANES + h, h * AUG + BIAS_LANE + part] = -1.0
            bias_q[part * PART_LANES + h, h * AUG + ONE_LANE + part] = 1.0
            bias_k[SPARE_LANE, h * AUG + ONE_LANE + part] = 1.0
            bias_q[SPARE_LANE, h * AUG + BIAS_LANE + part] = 1.0
    as_bf = lambda a: jnp.asarray(a, BF16)
    return dict(place=as_bf(place), place_t=as_bf(place.T), bias_k=as_bf(bias_k), bias_q_t=as_bf(bias_q.T))


def _pre_attn_fwd(x, g1, wqkv, wf, wu, bpad, wpool, lay, own_block):
    s, d = x.shape
    nt = s // TS
    sub = TS // TQ

    def body(x_ref, g_ref, wqkv_ref, wf_ref, wu_ref, b_ref, wp_ref, place_ref, place_t_ref, bk_ref, bqt_ref, own_ref,
             hn_ref, qt_ref, ka_ref, v_ref, qat_ref, vt_ref, kt_ref, fl_ref, y_ref, mp_ref, all_ref,
             ubuf, ccar, cbuf, stage, send_sems, recv_sems, local_sem):
        i = pl.program_id(0)

        @pl.when(i == 0)
        def _():
            _gather_start(own_ref, all_ref, stage, send_sems, recv_sems, local_sem)
            ubuf[0:HALO, :] = jnp.zeros((HALO, D_POOL), F32)
            ccar[...] = jnp.zeros_like(ccar)

        @pl.when(i == max(nt - 2, 0))
        def _():
            _gather_pass_on(all_ref, send_sems, recv_sems)

        xv = x_ref[...]
        hn = (xv * _rstd(xv) * g_ref[...]).astype(BF16)
        hn_ref[...] = hn
        zq = _nt(hn, wqkv_ref[...])
        qt = (zq[:, 0:D_ATTN] * 0.125).astype(BF16).T
        qb = (zq[:, 0:D_ATTN] * (0.125 * LOG2E)).astype(BF16)
        kb = zq[:, D_ATTN:2 * D_ATTN].astype(BF16)
        vb = zq[:, 2 * D_ATTN:3 * D_ATTN].astype(BF16)
        v_ref[...] = vb

        fl = _nt(hn, wf_ref[...]) + b_ref[...]
        fl_ref[...] = fl
        logf = jax.nn.log_sigmoid(fl)
        rr = lax.broadcasted_iota(jnp.int32, (TS, TS), 0)
        cc = lax.broadcasted_iota(jnp.int32, (TS, TS), 1)
        c = _mask_matmul((cc <= rr).astype(BF16), logf) + ccar[...]
        cbuf[...] = c
        ccar[...] = cbuf[TS - 1:TS, :]
        hi, mid, lo = (part.astype(F32) for part in _split3(c * LOG2E))
        lane = lax.broadcasted_iota(jnp.int32, (TS, LANES), 1)
        later = jnp.where(lane < 2 * PART_LANES, pltpu.roll(mid, PART_LANES, 1), pltpu.roll(lo, 2 * PART_LANES, 1))
        parts = jnp.where(lane < PART_LANES, jnp.where(lane == SPARE_LANE, 1.0, hi), later).astype(BF16)
        ka_ref[...] = (_nn(kb, place_ref[...]) + _nn(parts, bk_ref[...])).astype(BF16)
        qat = (_nt(place_t_ref[...], qb) + _nt(bqt_ref[...], parts)).astype(BF16)
        vt = vb.T
        kt = kb.T
        for a in range(sub):
            cols = slice(a * TQ, (a + 1) * TQ)
            qat_ref[a] = qat[:, cols]
            for ref, mat in ((qt_ref, qt), (kt_ref, kt), (vt_ref, vt)):
                for h in range(HEADS):
                    ref[a, h * VROWS:h * VROWS + HEAD_DIM, :] = mat[h * HEAD_DIM:(h + 1) * HEAD_DIM, cols]
                    ref[a, h * VROWS + HEAD_DIM:(h + 1) * VROWS, :] = jnp.ones((VROWS - HEAD_DIM, TQ), BF16)

        u = _nt(hn, wu_ref[...])
        ubuf[HALO:HALO + TS, :] = u
        t = i * TS + lax.broadcasted_iota(jnp.int32, (TS, 1), 0)
        for g, w in enumerate(POOL_WINDOWS):
            cols = slice(g * POOL_CH, (g + 1) * POOL_CH)
            sm = ubuf[:, cols]
            step = 1
            while step < w:
                sm = sm + pltpu.roll(sm, step, 0)
                step *= 2
            cnt = jnp.minimum(t + 1, w).astype(F32)
            yg = (sm[HALO:, :] / cnt - u[:, cols]).astype(BF16)
            y_ref[:, cols] = yg
            mp_ref[:, cols] = _nn(yg, wp_ref[g])
        ubuf[0:HALO, :] = u[TS - HALO:, :]

        @pl.when(i == nt - 1)
        def _():
            _gather_finish(own_ref, all_ref, send_sems, recv_sems)

    nq = s // TQ
    aug = HEADS * AUG
    outs = (
        jax.ShapeDtypeStruct((s, d), BF16), jax.ShapeDtypeStruct((nq, HEADS * VROWS, TQ), BF16),
        jax.ShapeDtypeStruct((s, aug), BF16), jax.ShapeDtypeStruct((s, D_ATTN), BF16),
        jax.ShapeDtypeStruct((nq, aug, TQ), BF16), jax.ShapeDtypeStruct((nq, HEADS * VROWS, TQ), BF16),
        jax.ShapeDtypeStruct((nq, HEADS * VROWS, TQ), BF16),
        jax.ShapeDtypeStruct((s, LANES), F32),
        jax.ShapeDtypeStruct((s, D_POOL), BF16), jax.ShapeDtypeStruct((s, D_POOL), F32),
        jax.ShapeDtypeStruct((N_DEV,) + own_block.shape, own_block.dtype),
    )
    fixed3 = lambda i: (0, 0, 0)
    tiles3 = lambda rows: pl.BlockSpec((sub, rows, TQ), lambda i: (i, 0, 0))
    return pl.pallas_call(
        body, grid=(nt,), out_shape=outs, name="pre_attn_fwd",
        in_specs=[pl.BlockSpec((TS, d), _row), pl.BlockSpec((1, d), _fixed),
                  pl.BlockSpec((3 * D_ATTN, d), _fixed), pl.BlockSpec(wf.shape, _fixed), pl.BlockSpec(wu.shape, _fixed),
                  pl.BlockSpec((1, LANES), _fixed), pl.BlockSpec(wpool.shape, fixed3),
                  pl.BlockSpec(lay["place"].shape, _fixed), pl.BlockSpec(lay["place_t"].shape, _fixed),
                  pl.BlockSpec(lay["bias_k"].shape, _fixed), pl.BlockSpec(lay["bias_q_t"].shape, _fixed), ANY],
        out_specs=(pl.BlockSpec((TS, d), _row), tiles3(HEADS * VROWS),
                   pl.BlockSpec((TS, aug), _row), pl.BlockSpec((TS, D_ATTN), _row),
                   tiles3(aug), tiles3(HEADS * VROWS), tiles3(HEADS * VROWS),
                   pl.BlockSpec((TS, LANES), _row),
                   pl.BlockSpec((TS, D_POOL), _row), pl.BlockSpec((TS, D_POOL), _row), ANY),
        scratch_shapes=[pltpu.VMEM((TS + HALO, D_POOL), F32), pltpu.VMEM((1, LANES), F32), pltpu.VMEM((TS, LANES), F32),
                        pltpu.VMEM(own_block.shape, own_block.dtype),
                        pltpu.SemaphoreType.DMA((7,)), pltpu.SemaphoreType.DMA((7,)), pltpu.SemaphoreType.DMA],
        compiler_params=_params(1),
    )(x, g1, wqkv, wf, wu, bpad, wpool, lay["place"], lay["place_t"], lay["bias_k"], lay["bias_q_t"], own_block)


def _causal_in_tile():
    krow = lax.broadcasted_iota(jnp.int32, (TQ, TQ), 0)
    qcol = lax.broadcasted_iota(jnp.int32, (TQ, TQ), 1)
    return krow <= qcol


def _attn_fwd(ka, qat3, vt3, own_block):
    s = ka.shape[0]
    nq = s // TQ
    pass_on_step = max(nq - 2, 0)

    def body(qa_ref, ka_ref, vt_ref, own_ref, a_ref, lset_ref, all_ref, acc, out_t, st_scr, pt_scr,
             stage, send_sems, recv_sems, local_sem):
        i = pl.program_id(0)

        @pl.when(i == 0)
        def _():
            _gather_start(own_ref, all_ref, stage, send_sems, recv_sems, local_sem)

        @pl.when(i == pass_on_step)
        def _():
            _gather_pass_on(all_ref, send_sems, recv_sems)

        acc[...] = jnp.zeros_like(acc)

        def tile(j, stats, masked):
            tile_max = []
            for h in range(HEADS):
                aug = slice(h * AUG, (h + 1) * AUG)
                st = _nn(ka_ref[pl.ds(j * TQ, TQ), aug], qa_ref[0, aug, :])
                if masked:
                    st = jnp.where(_causal_in_tile(), st, NEG)
                st_scr[h] = st
                tile_max.append(jnp.max(st, axis=0, keepdims=True))
            new, scale = [], []
            for h in range(HEADS):
                m_new = jnp.maximum(stats[h], tile_max[h])
                scale.append(jnp.exp2(stats[h] - m_new))
                pt_scr[h] = jnp.exp2(st_scr[h] - m_new).astype(BF16)
                new.append(m_new)
            for h in range(HEADS):
                rows = slice(h * VROWS, (h + 1) * VROWS)
                acc[rows, :] = scale[h] * acc[rows, :] + _nn(vt_ref[j, rows, :], pt_scr[h])
            return tuple(new)

        init = tuple(jnp.full((1, TQ), NEG, F32) for _ in range(HEADS))
        stats = lax.fori_loop(0, i, functools.partial(tile, masked=False), init)
        stats = tile(i, stats, True)
        for h in range(HEADS):
            denom = acc[h * VROWS + HEAD_DIM:h * VROWS + HEAD_DIM + 1, :]
            out_t[h * HEAD_DIM:(h + 1) * HEAD_DIM, :] = acc[h * VROWS:h * VROWS + HEAD_DIM, :] / denom
            lset_ref[0, h:h + 1, :] = stats[h] + jnp.log2(denom)
        a_ref[...] = out_t[...].T

        @pl.when(i == nq - 1)
        def _():
            _gather_finish(own_ref, all_ref, send_sems, recv_sems)

    r, cdim = own_block.shape
    return pl.pallas_call(
        body, grid=(nq,), name="attn_fwd",
        out_shape=(jax.ShapeDtypeStruct((s, D_ATTN), F32), jax.ShapeDtypeStruct((nq, HEADS, TQ), F32),
                   jax.ShapeDtypeStruct((N_DEV, r, cdim), own_block.dtype)),
        in_specs=[pl.BlockSpec((1, HEADS * AUG, TQ), lambda i: (i, 0, 0)), VMEM_WHOLE, VMEM_WHOLE, ANY],
        out_specs=(pl.BlockSpec((TQ, D_ATTN), _row), pl.BlockSpec((1, HEADS, TQ), lambda i: (i, 0, 0)), ANY),
        scratch_shapes=[pltpu.VMEM((HEADS * VROWS, TQ), F32), pltpu.VMEM((D_ATTN, TQ), F32),
                        pltpu.VMEM((HEADS, TQ, TQ), F32), pltpu.VMEM((HEADS, TQ, TQ), BF16),
                        pltpu.VMEM((r, cdim), own_block.dtype),
                        pltpu.SemaphoreType.DMA((7,)), pltpu.SemaphoreType.DMA((7,)), pltpu.SemaphoreType.DMA],
        compiler_params=_params(1),
    )(qat3, ka, vt3, own_block)


def _post_attn_fwd(a, mpre, x, g_attn, g_pool, pscale, wout, g_post, g_ffn_pre):
    s, d = x.shape

    def body(a_ref, mp_ref, x_ref, ga_ref, gp_ref, ps_ref, wo_ref, gpost_ref, gpre_ref,
             mix_ref, o_ref, h1_ref, hn2_ref):
        for rows in _HALVES:
            av = a_ref[rows, :]
            mix_ref[rows, 0:D_ATTN] = (av * _rstd(av) * ga_ref[...]).astype(BF16)
            mv = mp_ref[rows, :] * ps_ref[...]
            mix_ref[rows, D_ATTN:] = (mv * _rstd(mv) * gp_ref[...]).astype(BF16)
            o = _nn(mix_ref[rows, :], wo_ref[...].reshape(d, d))
            o_ref[rows, :] = o
            h1 = x_ref[rows, :] + o * _rstd(o) * gpost_ref[...]
            h1_ref[rows, :] = h1
            hn2_ref[rows, :] = (h1 * _rstd(h1) * gpre_ref[...]).astype(BF16)

    vec = lambda n: pl.BlockSpec((1, n), _fixed)
    return pl.pallas_call(
        body, grid=(s // TS,), name="post_attn_fwd",
        out_shape=(jax.ShapeDtypeStruct((s, d), BF16), jax.ShapeDtypeStruct((s, d), F32),
                   jax.ShapeDtypeStruct((s, d), F32), jax.ShapeDtypeStruct((s, d), BF16)),
        in_specs=[pl.BlockSpec((TS, D_ATTN), _row), pl.BlockSpec((TS, D_POOL), _row), pl.BlockSpec((TS, d), _row),
                  vec(D_ATTN), vec(D_POOL), vec(D_POOL), _spec_square(0), vec(d), vec(d)],
        out_specs=(pl.BlockSpec((TS, d), _row),) * 4,
        compiler_params=_params(1),
    )(a, mpre, x, g_attn, g_pool, pscale, wout, g_post, g_ffn_pre)


def _ffn_fwd(hn2, wg, wu, wd, h1, g_post):
    s, d = h1.shape
    nc = D_FF // TN_FF
    ts = min(TS_FF, s)

    def body(hn_ref, wg_ref, wu_ref, wd_ref, h1_ref, g_ref, gate_ref, up_ref, act_ref, ff_ref, h2_ref, acc):
        j = pl.program_id(1)

        @pl.when(j == 0)
        def _():
            acc[...] = jnp.zeros_like(acc)

        for r in range(2):
            rows = slice(r * (ts // 2), (r + 1) * (ts // 2))
            hn = hn_ref[rows, :]
            gt = _nt(hn, wg_ref[...].reshape(TN_FF, d))
            up = _nt(hn, wu_ref[...].reshape(TN_FF, d))
            gate_ref[rows, :] = gt.astype(BF16)
            up_ref[rows, :] = up.astype(BF16)
            act_ref[rows, :] = (gt * jax.nn.sigmoid(gt) * up).astype(BF16)
            acc[rows, :] += _nn(act_ref[rows, :], wd_ref[...].reshape(TN_FF, d))

        @pl.when(j == nc - 1)
        def _():
            ff = acc[...]
            ff_ref[...] = ff
            h2_ref[...] = h1_ref[...] + ff * _rstd(ff) * g_ref[...]

    rowblk = pl.BlockSpec((ts, d), lambda i, j: (i, 0))
    chunk = pl.BlockSpec((ts, TN_FF), lambda i, j: (i, j))
    return pl.pallas_call(
        body, grid=(s // ts, nc), name="ffn_fwd",
        out_shape=(jax.ShapeDtypeStruct((s, D_FF), BF16),) * 3 + (jax.ShapeDtypeStruct((s, d), F32),) * 2,
        in_specs=[rowblk, _spec_ff(0), _spec_ff(1), _spec_ff(2), rowblk, pl.BlockSpec((1, d), lambda i, j: (0, 0))],
        out_specs=(chunk, chunk, chunk, rowblk, rowblk),
        scratch_shapes=[pltpu.VMEM((ts, d), F32)],
        compiler_params=_params(2),
    )(hn2, wg, wu, wd, h1, g_post)


def _tail_fwd_bwd(h2, p, tgt, ff, wple, wpg, g_ple, g_ffn_post):
    s, d = h2.shape

    def body(h2_ref, p_ref, t_ref, ff_ref, wple_ref, wpg_ref, gple_ref, gfp_ref,
             dh2_ref, dff_ref, dgl_ref, dpp_ref, h2b_ref, pb_ref, loss_ref, dgple_ref, dgfp_ref):
        i = pl.program_id(0)

        @pl.when(i == 0)
        def _():
            loss_ref[...] = jnp.zeros_like(loss_ref)
            dgple_ref[...] = jnp.zeros_like(dgple_ref)
            dgfp_ref[...] = jnp.zeros_like(dgfp_ref)

        h2 = h2_ref[...]
        h2b = h2.astype(BF16)
        h2b_ref[...] = h2b
        pb = p_ref[...].astype(BF16)
        pb_ref[...] = pb
        pp = _nt(pb, wple_ref[...])
        gple = gple_ref[...]
        e = pp * _rstd(pp) * gple
        wpg = wpg_ref[...].reshape(d, d)
        sg = jax.nn.sigmoid(_nn(h2b, wpg))
        diff = h2 + sg * e - t_ref[...]
        sq = jnp.sum(jnp.sum(diff * diff, axis=1, keepdims=True), axis=0, keepdims=True)
        loss_ref[...] += jnp.broadcast_to(sq * (0.5 / d), loss_ref.shape)
        dh3 = diff * (1.0 / d)
        dgl = (dh3 * e * sg * (1.0 - sg)).astype(BF16)
        dgl_ref[...] = dgl
        dh2 = dh3 + _nt(dgl, wpg)
        dh2_ref[...] = dh2
        dpp, dg = _rms_bwd(pp, gple, dh3 * sg)
        dpp_ref[...] = dpp.astype(BF16)
        dgple_ref[...] += dg
        dff, dg = _rms_bwd(ff_ref[...], gfp_ref[...], dh2)
        dff_ref[...] = dff.astype(BF16)
        dgfp_ref[...] += dg

    rowblk = pl.BlockSpec((TS, d), _row)
    vec = pl.BlockSpec((1, d), _fixed)
    return pl.pallas_call(
        body, grid=(s // TS,), name="tail_fwd_bwd",
        out_shape=(jax.ShapeDtypeStruct((s, d), F32), jax.ShapeDtypeStruct((s, d), BF16),
                   jax.ShapeDtypeStruct((s, d), BF16), jax.ShapeDtypeStruct((s, d), BF16),
                   jax.ShapeDtypeStruct((s, d), BF16), jax.ShapeDtypeStruct((s, D_PLE), BF16),
                   jax.ShapeDtypeStruct((8, LANES), F32), jax.ShapeDtypeStruct((1, d), F32),
                   jax.ShapeDtypeStruct((1, d), F32)),
        in_specs=[rowblk, pl.BlockSpec((TS, D_PLE), _row), rowblk, rowblk,
                  pl.BlockSpec(wple.shape, _fixed), _spec_square(1), vec, vec],
        out_specs=(rowblk, rowblk, rowblk, rowblk, rowblk, pl.BlockSpec((TS, D_PLE), _row),
                   pl.BlockSpec((8, LANES), _fixed), vec, vec),
        compiler_params=_params(1),
    )(h2, p, tgt, ff, wple, wpg, g_ple, g_ffn_post)


def _ffn_bwd(dff, gate, up, wd, wg, wu, h1, dh2, g_pre):
    s, d = h1.shape
    nc = D_FF // TN_FF
    ts = min(TS_FF, s)

    def body(dff_ref, gate_ref, up_ref, wd_ref, wg_ref, wu_ref, h1_ref, dh2_ref, g_ref,
             dgate_ref, dup_ref, dh1_ref, dg_ref, acc):
        i = pl.program_id(0)
        j = pl.program_id(1)

        @pl.when((i == 0) & (j == 0))
        def _():
            dg_ref[...] = jnp.zeros_like(dg_ref)

        @pl.when(j == 0)
        def _():
            acc[...] = jnp.zeros_like(acc)

        for r in range(2):
            rows = slice(r * (ts // 2), (r + 1) * (ts // 2))
            dact = _nt(dff_ref[rows, :], wd_ref[...].reshape(TN_FF, d))
            gt = gate_ref[rows, :].astype(F32)
            sg = jax.nn.sigmoid(gt)
            dup_ref[rows, :] = (dact * gt * sg).astype(BF16)
            dgate_ref[rows, :] = (dact * up_ref[rows, :].astype(F32) * (sg * (1.0 + gt * (1.0 - sg)))).astype(BF16)
            acc[rows, :] += (_nn(dgate_ref[rows, :], wg_ref[...].reshape(TN_FF, d))
                             + _nn(dup_ref[rows, :], wu_ref[...].reshape(TN_FF, d)))

        @pl.when(j == nc - 1)
        def _():
            dv, dg = _rms_bwd(h1_ref[...], g_ref[...], acc[...])
            dh1_ref[...] = dh2_ref[...] + dv
            dg_ref[...] += dg

    rowblk = pl.BlockSpec((ts, d), lambda i, j: (i, 0))
    chunk = pl.BlockSpec((ts, TN_FF), lambda i, j: (i, j))
    vec = pl.BlockSpec((1, d), lambda i, j: (0, 0))
    return pl.pallas_call(
        body, grid=(s // ts, nc), name="ffn_bwd",
        out_shape=(jax.ShapeDtypeStruct((s, D_FF), BF16), jax.ShapeDtypeStruct((s, D_FF), BF16),
                   jax.ShapeDtypeStruct((s, d), F32), jax.ShapeDtypeStruct((1, d), F32)),
        in_specs=[rowblk, chunk, chunk, _spec_ff(2), _spec_ff(0), _spec_ff(1), rowblk, rowblk, vec],
        out_specs=(chunk, chunk, rowblk, vec),
        scratch_shapes=[pltpu.VMEM((ts, d), F32)],
        compiler_params=_params(2),
    )(dff, gate, up, wd, wg, wu, h1, dh2, g_pre)


def _post_attn_bwd(dh1, o, a, mpre, wout, wpool, g_post, g_attn, g_pool, pscale, send):
    s, d = dh1.shape
    sub = TS // TQ
    npc = len(send)

    def body(dh1_ref, o_ref, a_ref, mp_ref, wo_ref, wp_ref, gpost_ref, ga_ref, gp_ref, ps_ref, *refs):
        send_refs, refs = refs[:npc], refs[npc:]
        dob_ref, dat_ref, dlt_ref, dmpb_ref, dy_ref, dgpost_ref, dga_ref, dgp_ref, dps_ref = refs[:9]
        got_refs, (send_sems, recv_sems) = refs[9:9 + npc], refs[9 + npc:]
        i = pl.program_id(0)

        @pl.when(i == 0)
        def _():
            for cp in _pair_copies(send_refs, got_refs, send_sems, recv_sems):
                cp.start()
            dgpost_ref[...] = jnp.zeros_like(dgpost_ref)
            dga_ref[...] = jnp.zeros_like(dga_ref)
            dgp_ref[...] = jnp.zeros_like(dgp_ref)
            dps_ref[...] = jnp.zeros_like(dps_ref)

        do, dg = _rms_bwd(o_ref[...], gpost_ref[...], dh1_ref[...])
        dgpost_ref[...] += dg
        dob = do.astype(BF16)
        dob_ref[...] = dob
        dmix = _nt(dob, wo_ref[...].reshape(d, d))

        av = a_ref[...]
        da, dg = _rms_bwd(av, ga_ref[...], dmix[:, 0:D_ATTN])
        dga_ref[...] += dg
        dat = da.astype(BF16).T
        hsel = (lax.shift_right_logical(lax.broadcasted_iota(jnp.int32, (HEADS, D_ATTN), 1), 6)
                == lax.broadcasted_iota(jnp.int32, (HEADS, D_ATTN), 0)).astype(F32)
        dlt = lax.dot_general(hsel, da * av, (((1,), (1,)), ((), ())), precision=HIGHEST, preferred_element_type=F32)
        for q in range(sub):
            dlt_ref[q] = dlt[:, q * TQ:(q + 1) * TQ]
            dat_ref[q] = dat[:, q * TQ:(q + 1) * TQ]

        ps = ps_ref[...]
        mp = mp_ref[...]
        dm, dg = _rms_bwd(mp * ps, gp_ref[...], dmix[:, D_ATTN:])
        dgp_ref[...] += dg
        dps_ref[...] += jnp.sum(dm * mp, axis=0, keepdims=True)
        dmpb = (dm * ps).astype(BF16)
        dmpb_ref[...] = dmpb
        for g in range(len(POOL_WINDOWS)):
            cols = slice(g * POOL_CH, (g + 1) * POOL_CH)
            dy_ref[:, cols] = _nt(dmpb[:, cols], wp_ref[g])

        @pl.when(i == s // TS - 1)
        def _():
            for cp in _pair_copies(send_refs, got_refs, send_sems, recv_sems):
                cp.wait()

    rowblk = pl.BlockSpec((TS, d), _row)
    half = pl.BlockSpec((TS, D_ATTN), _row)
    vec = lambda n: pl.BlockSpec((1, n), _fixed)
    nk = N_DEV // 2
    res = pl.pallas_call(
        body, grid=(s // TS,), name="post_attn_bwd",
        out_shape=(jax.ShapeDtypeStruct((s, d), BF16), jax.ShapeDtypeStruct((s // TQ, D_ATTN, TQ), BF16),
                   jax.ShapeDtypeStruct((s // TQ, HEADS, TQ), F32), jax.ShapeDtypeStruct((s, D_POOL), BF16),
                   jax.ShapeDtypeStruct((s, D_POOL), F32), jax.ShapeDtypeStruct((1, d), F32),
                   jax.ShapeDtypeStruct((1, D_ATTN), F32), jax.ShapeDtypeStruct((1, D_POOL), F32),
                   jax.ShapeDtypeStruct((1, D_POOL), F32))
        + tuple(jax.ShapeDtypeStruct((nk,) + t.shape[1:], t.dtype) for t in send),
        in_specs=[rowblk, rowblk, half, half, _spec_square(0),
                  pl.BlockSpec(wpool.shape, lambda i: (0, 0, 0)), vec(d), vec(D_ATTN), vec(D_POOL), vec(D_POOL)]
        + [ANY] * npc,
        out_specs=(rowblk, pl.BlockSpec((sub, D_ATTN, TQ), lambda i: (i, 0, 0)),
                   pl.BlockSpec((sub, HEADS, TQ), lambda i: (i, 0, 0)), half, half,
                   vec(d), vec(D_ATTN), vec(D_POOL), vec(D_POOL)) + (ANY,) * npc,
        scratch_shapes=[pltpu.SemaphoreType.DMA((nk, npc)), pltpu.SemaphoreType.DMA((nk, npc))],
        compiler_params=_params(1),
    )(dh1, o, a, mpre, wout, wpool, g_post, g_attn, g_pool, pscale, *send)
    return res[:9], list(res[9:])


def _attn_bwd(ka, v, kt3, qat3, qt3, dot3, lset3, dlt3, chip_blocks, small_block):
    s = ka.shape[0]
    nq = s // TQ

    def body(ka_ref, v_ref, kt_ref, qat_ref, qt_ref, dot_ref, lset_ref, dlt_ref, b_ref, sm_ref,
             dqt_ref, dkt_ref, dvt_ref, got_ref, all_ref, pt_scr, ptb_scr, dsb_scr,
             stage, send_sems, recv_sems, local_sem, stage_s, send_s, recv_s, local_s):
        j = pl.program_id(0)

        @pl.when(j == 0)
        def _():
            _chips_start(b_ref, got_ref, stage, send_sems, recv_sems, local_sem)
            _gather_start(sm_ref, all_ref, stage_s, send_s, recv_s, local_s)
            dqt_ref[...] = jnp.zeros_like(dqt_ref)

        @pl.when(j == max(nq - 2, 0))
        def _():
            _gather_pass_on(all_ref, send_s, recv_s)

        def tile(i, masked):
            def accumulate(ref, idx, val):
                if masked:
                    ref[idx] = val
                else:
                    ref[idx] += val

            for h in range(HEADS):
                aug = slice(h * AUG, (h + 1) * AUG)
                st = _nn(ka_ref[:, aug], qat_ref[i, aug, :]) - lset_ref[i, h:h + 1, :]
                if masked:
                    st = jnp.where(_causal_in_tile(), st, NEG)
                pt = jnp.exp2(st)
                pt_scr[h] = pt
                ptb_scr[h] = pt.astype(BF16)
            heads = [(h, slice(h * HEAD_DIM, (h + 1) * HEAD_DIM)) for h in range(HEADS)]
            for h, hs in heads:
                dst = pt_scr[h] * (_nn(v_ref[:, hs], dot_ref[i, hs, :]) - dlt_ref[i, h:h + 1, :])
                dsb_scr[h] = dst.astype(BF16)
            for h, hs in heads:
                accumulate(dvt_ref, (0, hs, slice(None)), _nt(dot_ref[i, hs, :], ptb_scr[h]))
            for h, hs in heads:
                rows = slice(h * VROWS, (h + 1) * VROWS)
                accumulate(dkt_ref, (0, rows, slice(None)), _nt(qt_ref[i, rows, :], dsb_scr[h]))
            for h, hs in heads:
                rows = slice(h * VROWS, (h + 1) * VROWS)
                dqt_ref[i, rows, :] += _nn(kt_ref[0, rows, :], dsb_scr[h])

        first = j + 1
        pairs = (nq - first) // 2

        def step(p, carry):
            tile(first + 2 * p, False)
            tile(first + 2 * p + 1, False)
            return carry

        tile(j, True)
        lax.fori_loop(0, pairs, step, 0)

        @pl.when(first + 2 * pairs < nq)
        def _():
            tile(nq - 1, False)

        @pl.when(j == nq - 1)
        def _():
            _chips_finish(b_ref, got_ref, send_sems, recv_sems)
            _gather_finish(sm_ref, all_ref, send_s, recv_s)

    blk = pl.BlockSpec((TQ, D_ATTN), _row)
    tile_t = lambda rows: pl.BlockSpec((1, rows, TQ), lambda j: (j, 0, 0))
    per_tile = lambda rows: jax.ShapeDtypeStruct((nq, rows, TQ), F32)
    _, r, cdim = chip_blocks.shape
    dma = pltpu.SemaphoreType.DMA
    return pl.pallas_call(
        body, grid=(nq,), name="attn_bwd",
        out_shape=(per_tile(HEADS * VROWS), per_tile(HEADS * VROWS), per_tile(D_ATTN),
                   jax.ShapeDtypeStruct(chip_blocks.shape, chip_blocks.dtype),
                   jax.ShapeDtypeStruct((N_DEV,) + small_block.shape, small_block.dtype)),
        in_specs=[pl.BlockSpec((TQ, HEADS * AUG), _row), blk, tile_t(HEADS * VROWS),
                  VMEM_WHOLE, VMEM_WHOLE, VMEM_WHOLE, VMEM_WHOLE, VMEM_WHOLE, ANY, ANY],
        out_specs=(pl.BlockSpec((nq, HEADS * VROWS, TQ), lambda j: (0, 0, 0)), tile_t(HEADS * VROWS), tile_t(D_ATTN),
                   ANY, ANY),
        scratch_shapes=[pltpu.VMEM((HEADS, TQ, TQ), F32), pltpu.VMEM((HEADS, TQ, TQ), BF16),
                        pltpu.VMEM((HEADS, TQ, TQ), BF16), pltpu.VMEM((r, cdim), chip_blocks.dtype),
                        dma((3,)), dma((3,)), dma,
                        pltpu.VMEM(small_block.shape, small_block.dtype), dma((7,)), dma((7,)), dma],
        compiler_params=_params(1),
    )(ka, v, kt3, qat3, qt3, dot3, lset3, dlt3, chip_blocks, small_block)


def _pre_attn_bwd(dqt3, dkt3, dvt3, fl, dy, x, dh1, g1, wqkv, wf, wu):
    s, d = x.shape
    nt = s // TS
    n = TS + HALO
    sub = TS // TQ
    qkv, fcols = 3 * D_ATTN, 3 * D_ATTN + LANES

    def body(dqt_ref, dkt_ref, dvt_ref, fl_ref, dy_ref, x_ref, dh1_ref, g_ref, wqkv_ref, wf_ref, wu_ref,
             gx_ref, dz_ref, dg_ref, db_ref, ybuf, ccar, dlog, dsum):
        dqkv_ref = dz_ref.at[:, 0:qkv]
        dfb_ref = dz_ref.at[:, qkv:fcols]
        dub_ref = dz_ref.at[:, fcols:]
        i = pl.program_id(0)
        ti = nt - 1 - i

        @pl.when(i == 0)
        def _():
            ybuf[TS:n, :] = jnp.zeros((HALO, D_POOL), F32)
            ccar[...] = jnp.zeros_like(ccar)
            dg_ref[...] = jnp.zeros_like(dg_ref)
            db_ref[...] = jnp.zeros_like(db_ref)
            dsum[...] = jnp.zeros_like(dsum)

        for a in range(sub):
            for h in range(HEADS):
                r = h * VROWS + HEAD_DIM
                dsum[h:h + 1, a * TQ:(a + 1) * TQ] = dqt_ref[a, r:r + 1, :] - dkt_ref[a, r:r + 1, :]
        rr = lax.broadcasted_iota(jnp.int32, (TS, TS), 0)
        cc = lax.broadcasted_iota(jnp.int32, (TS, TS), 1)
        dlog[...] = ccar[...] + _mask_matmul((cc >= rr).astype(BF16), dsum[...].T)
        ccar[...] = dlog[0:1, :]
        df = dlog[...] * jax.nn.sigmoid(-fl_ref[...])
        db_ref[...] += jnp.sum(df, axis=0, keepdims=True)
        dfb = df.astype(BF16)
        dfb_ref[...] = dfb

        t = ti * TS + lax.broadcasted_iota(jnp.int32, (TS, 1), 0)
        dy = dy_ref[...]
        for g, w in enumerate(POOL_WINDOWS):
            cols = slice(g * POOL_CH, (g + 1) * POOL_CH)
            ybuf[0:TS, cols] = dy[:, cols] / jnp.minimum(t + 1, w).astype(F32)
        for g, w in enumerate(POOL_WINDOWS):
            cols = slice(g * POOL_CH, (g + 1) * POOL_CH)
            sm = ybuf[:, cols]
            step = 1
            while step < w:
                sm = sm + pltpu.roll(sm, n - step, 0)
                step *= 2
            dub_ref[:, cols] = (sm[0:TS, :] - dy[:, cols]).astype(BF16)
        ybuf[TS:n, :] = ybuf[0:HALO, :]

        for a in range(sub):
            rows = slice(a * TQ, (a + 1) * TQ)
            for h in range(HEADS):
                src = slice(h * VROWS, h * VROWS + HEAD_DIM)
                dqkv_ref[rows, h * HEAD_DIM:(h + 1) * HEAD_DIM] = (dqt_ref[a, src, :].T * 0.125).astype(BF16)
                dqkv_ref[rows, D_ATTN + h * HEAD_DIM:D_ATTN + (h + 1) * HEAD_DIM] = dkt_ref[a, src, :].T.astype(BF16)
            dqkv_ref[rows, 2 * D_ATTN:] = dvt_ref[a].T.astype(BF16)
        dhn = _nn(dqkv_ref[...], wqkv_ref[...]) + _nn(dfb, wf_ref[...]) + _nn(dub_ref[...], wu_ref[...])
        dx, dg = _rms_bwd(x_ref[...], g_ref[...], dhn)
        gx_ref[...] = dh1_ref[...] + dx
        dg_ref[...] += dg

    rev = lambda i: (nt - 1 - i, 0)
    blk = lambda w: pl.BlockSpec((TS, w), rev)
    return pl.pallas_call(
        body, grid=(nt,), name="pre_attn_bwd",
        out_shape=(jax.ShapeDtypeStruct((s, d), F32), jax.ShapeDtypeStruct((s, fcols + D_POOL), BF16),
                   jax.ShapeDtypeStruct((1, d), F32), jax.ShapeDtypeStruct((1, LANES), F32)),
        in_specs=[pl.BlockSpec((sub, HEADS * VROWS, TQ), lambda i: (nt - 1 - i, 0, 0)),
                  pl.BlockSpec((sub, HEADS * VROWS, TQ), lambda i: (nt - 1 - i, 0, 0)),
                  pl.BlockSpec((sub, D_ATTN, TQ), lambda i: (nt - 1 - i, 0, 0)),
                  blk(LANES), blk(D_POOL), blk(d), blk(d),
                  pl.BlockSpec((1, d), _fixed), pl.BlockSpec((qkv, d), _fixed), pl.BlockSpec(wf.shape, _fixed),
                  pl.BlockSpec(wu.shape, _fixed)],
        out_specs=(blk(d), blk(fcols + D_POOL), pl.BlockSpec((1, d), _fixed), pl.BlockSpec((1, LANES), _fixed)),
        scratch_shapes=[pltpu.VMEM((n, D_POOL), F32), pltpu.VMEM((1, LANES), F32), pltpu.VMEM((TS, LANES), F32),
                        pltpu.VMEM((LANES, TS), F32)],
        compiler_params=_params(1),
    )(dqt3, dkt3, dvt3, fl, dy, x, dh1, g1, wqkv, wf, wu)


def _wgrad(a, b, out_dtype, name):
    s, m = a.shape
    n = b.shape[1]
    tm = max(t for t in range(LANES, min(m, TM_WGRAD) + 1, LANES) if m % t == 0)
    ts = min(TS_WGRAD, s)
    ns = s // ts

    def body(a_ref, b_ref, o_ref, acc):
        i = pl.program_id(1)

        @pl.when(i == 0)
        def _():
            acc[...] = jnp.zeros_like(acc)

        acc[...] += _tn(a_ref[...], b_ref[...])

        @pl.when(i == ns - 1)
        def _():
            o_ref[...] = acc[...].astype(out_dtype)

    return pl.pallas_call(
        body, grid=(m // tm, ns), name=name, out_shape=jax.ShapeDtypeStruct((m, n), out_dtype),
        in_specs=[pl.BlockSpec((ts, tm), lambda j, i: (i, j)), pl.BlockSpec((ts, n), lambda j, i: (i, 0))],
        out_specs=pl.BlockSpec((tm, n), lambda j, i: (j, 0)),
        scratch_shapes=[pltpu.VMEM((tm, n), F32)],
        compiler_params=_params(2),
    )(a, b)


def _wgrad_in(dz, hn):
    s, m = dz.shape
    n = hn.shape[1]
    ts = min(TS_WGRAD, s)
    ns = s // ts
    pad_at, pad = 3 * D_ATTN + HEADS, LANES - HEADS
    assert m == D_IN + pad and N_DEV * SHARD_IN == D_IN

    def pieces(d):
        lo, hi = d * SHARD_IN, (d + 1) * SHARD_IN
        spans = [(lo, min(hi, pad_at), 0), (max(lo, pad_at), hi, pad)]
        return [(a + shift, b - a, a - lo) for a, b, shift in spans if b > a]

    def body(a_ref, b_ref, o_ref, acc, stage):
        i = pl.program_id(0)

        @pl.when(i == 0)
        def _():
            acc[...] = jnp.zeros_like(acc)

        acc[...] += _tn(a_ref[...], b_ref[...])

        @pl.when(i == ns - 1)
        def _():
            stage[SHARD_IN:ROWS_IN, :] = jnp.zeros((ROWS_IN - SHARD_IN, n), F32)
            for d in range(N_DEV):
                for src, rows, dst in pieces(d):
                    stage[dst:dst + rows, :] = acc[src:src + rows, :]
                o_ref[d] = stage[...].astype(BF16)

    return pl.pallas_call(
        body, grid=(ns,), name="wgrad_in", out_shape=jax.ShapeDtypeStruct((N_DEV, ROWS_IN, n), BF16),
        in_specs=[pl.BlockSpec((ts, m), _row), pl.BlockSpec((ts, n), _row)],
        out_specs=pl.BlockSpec((N_DEV, ROWS_IN, n), lambda i: (0, 0, 0)),
        scratch_shapes=[pltpu.VMEM((m, n), F32), pltpu.VMEM((ROWS_IN, n), F32)],
        compiler_params=_params(1),
    )(dz, hn)


def _adamw(w, g, m, v):
    m = ADAM_B1 * m + (1.0 - ADAM_B1) * g
    v = ADAM_B2 * v + (1.0 - ADAM_B2) * (g * g)
    m_hat = m / (1.0 - ADAM_B1 ** ADAM_STEP)
    v_hat = v / (1.0 - ADAM_B2 ** ADAM_STEP)
    delta = -ADAM_LR * (m_hat / (jnp.sqrt(v_hat) + ADAM_EPS) + ADAM_WD * w)
    return delta, m, v


def _sum_update(p_ref, w_ref, m_ref, v_ref, g_ref, d_ref, nm_ref, nv_ref):
    g = p_ref[0].astype(F32)
    for k in range(1, p_ref.shape[0]):
        g = g + p_ref[k].astype(F32)
    g_ref[...] = g
    d_ref[...], nm_ref[...], nv_ref[...] = _adamw(w_ref[...], g, m_ref[...], v_ref[...])


def _reduce_update_rest(parts, w, m, v, chip_blocks, small_block):
    nk, r, c = parts.shape
    ns = r // TR_REST

    def body(p_ref, w_ref, m_ref, v_ref, b_ref, sm_ref, g_ref, d_ref, nm_ref, nv_ref, got_ref, all_ref,
             stage_b, stage_s, send_b, recv_b, local_b, send_s, recv_s, local_s):
        i = pl.program_id(0)

        @pl.when(i == 0)
        def _():
            _chips_start(b_ref, got_ref, stage_b, send_b, recv_b, local_b)
            _gather_start(sm_ref, all_ref, stage_s, send_s, recv_s, local_s)

        _sum_update(p_ref, w_ref, m_ref, v_ref, g_ref, d_ref, nm_ref, nv_ref)

        @pl.when(i == ns - 1)
        def _():
            _gather_pass_on(all_ref, send_s, recv_s)
            _chips_finish(b_ref, got_ref, send_b, recv_b)
            _gather_finish(sm_ref, all_ref, send_s, recv_s)

    blk = pl.BlockSpec((TR_REST, c), _row)
    out = jax.ShapeDtypeStruct((r, c), F32)
    dma = pltpu.SemaphoreType.DMA
    return pl.pallas_call(
        body, grid=(ns,), name="reduce_update_rest",
        out_shape=(out,) * 4 + (jax.ShapeDtypeStruct(chip_blocks.shape, chip_blocks.dtype),
                                jax.ShapeDtypeStruct((N_DEV,) + small_block.shape, small_block.dtype)),
        in_specs=[pl.BlockSpec((nk, TR_REST, c), lambda i: (0, i, 0)), blk, blk, blk, ANY, ANY],
        out_specs=(blk,) * 4 + (ANY, ANY),
        scratch_shapes=[pltpu.VMEM(chip_blocks.shape[1:], chip_blocks.dtype), pltpu.VMEM(small_block.shape, small_block.dtype),
                        dma((3,)), dma((3,)), dma, dma((7,)), dma((7,)), dma],
        compiler_params=_params(1),
    )(parts, w, m, v, chip_blocks, small_block)


def _reduce_update_big(parts, w, m, v, tr, name):
    nk, r, c = parts.shape

    def body(p_ref, w_ref, m_ref, v_ref, g_ref, d_ref, nm_ref, nv_ref):
        _sum_update(p_ref, w_ref, m_ref, v_ref, g_ref, d_ref, nm_ref, nv_ref)

    blk = pl.BlockSpec((tr, c), _row)
    out = jax.ShapeDtypeStruct((r, c), F32)
    return pl.pallas_call(
        body, grid=(r // tr,), name=name, out_shape=(out,) * 4,
        in_specs=[pl.BlockSpec((nk, tr, c), lambda i: (0, i, 0)), blk, blk, blk],
        out_specs=(blk,) * 4, compiler_params=_params(1),
    )(parts, w, m, v)


def _reduce_update_small(parts, late, w, m, v):
    nd = parts.shape[0]
    first = parts.shape[1] - late.shape[1]

    def body(p_ref, q_ref, w_ref, m_ref, v_ref, g_ref, d_ref, nm_ref, nv_ref):
        g, t = p_ref[0], q_ref[0]
        for k in range(1, nd):
            g, t = g + p_ref[k], t + q_ref[k]
        g_ref[...] = g
        g_ref[first:, :] = g[first:, :] + t
        d_ref[...], nm_ref[...], nv_ref[...] = _adamw(w_ref[...], g_ref[...], m_ref[...], v_ref[...])

    out = jax.ShapeDtypeStruct(w.shape, F32)
    return pl.pallas_call(body, name="reduce_update_small", out_shape=(out,) * 4,
                          compiler_params=pltpu.CompilerParams(vmem_limit_bytes=VMEM_LIMIT))(parts, late, w, m, v)


MESH = pl.DeviceIdType.MESH


def _copy_through_vmem(src_hbm, dst_hbm, stage, sem):
    load = pltpu.make_async_copy(src_hbm, stage, sem)
    load.start()
    load.wait()
    store = pltpu.make_async_copy(stage, dst_hbm, sem)
    store.start()
    store.wait()


class _GatherPlan:
    def __init__(self, x_ref, out_ref, send_sems, recv_sems):
        x, y, c = lax.axis_index("x"), lax.axis_index("y"), lax.axis_index("c")
        self.me, self.sibling, self.c = (x, y, c), (x, y, 1 - c), c
        self.chips = [(1 - x, y), (x, 1 - y), (1 - x, 1 - y)]
        self.x_ref, self.out_ref, self.send_sems, self.recv_sems = x_ref, out_ref, send_sems, recv_sems

    def slot(self, px, py, pc):
        return self.out_ref.at[4 * px + 2 * py + pc]

    def copy(self, k, block, to, src=None):
        return pltpu.make_async_remote_copy(
            src_ref=self.slot(*block) if src is None else src, dst_ref=self.slot(*block),
            send_sem=self.send_sems.at[k], recv_sem=self.recv_sems.at[k], device_id=to, device_id_type=MESH)

    def first(self):
        return [self.copy(0, self.me, self.sibling, src=self.x_ref)] + [
            self.copy(1 + j, self.me, (*chip, self.c), src=self.x_ref) for j, chip in enumerate(self.chips)]

    def passed(self):
        return [self.copy(4 + j, (*chip, self.c), self.sibling) for j, chip in enumerate(self.chips)]


def _gather_start(x_ref, out_ref, stage, send_sems, recv_sems, local_sem):
    plan = _GatherPlan(x_ref, out_ref, send_sems, recv_sems)
    for cp in plan.first():
        cp.start()
    _copy_through_vmem(x_ref, plan.slot(*plan.me), stage, local_sem)


def _gather_pass_on(out_ref, send_sems, recv_sems):
    plan = _GatherPlan(None, out_ref, send_sems, recv_sems)
    passed = plan.passed()
    for j, chip in enumerate(plan.chips):
        plan.copy(1 + j, (*chip, plan.c), plan.me).wait_recv()
        passed[j].start()


def _gather_finish(x_ref, out_ref, send_sems, recv_sems):
    plan = _GatherPlan(x_ref, out_ref, send_sems, recv_sems)
    plan.copy(0, plan.sibling, plan.me).wait_recv()
    for j, chip in enumerate(plan.chips):
        plan.copy(4 + j, (*chip, 1 - plan.c), plan.me).wait_recv()
    for cp in plan.first() + plan.passed():
        cp.wait_send()


def _all_gather(xs, name):
    r, cdim = xs.shape

    def body(x_ref, out_ref, stage, send_sems, recv_sems, local_sem):
        _gather_start(x_ref, out_ref, stage, send_sems, recv_sems, local_sem)
        _gather_pass_on(out_ref, send_sems, recv_sems)
        _gather_finish(x_ref, out_ref, send_sems, recv_sems)

    return pl.pallas_call(
        body, name=name, out_shape=jax.ShapeDtypeStruct((N_DEV, r, cdim), xs.dtype),
        in_specs=[ANY], out_specs=ANY,
        scratch_shapes=[pltpu.VMEM((r, cdim), xs.dtype), pltpu.SemaphoreType.DMA((7,)), pltpu.SemaphoreType.DMA((7,)),
                        pltpu.SemaphoreType.DMA],
        compiler_params=pltpu.CompilerParams(vmem_limit_bytes=VMEM_LIMIT),
    )(xs)


def _pair_copies(src_refs, dst_refs, send_sems, recv_sems):
    x, y, c = lax.axis_index("x"), lax.axis_index("y"), lax.axis_index("c")
    return [pltpu.make_async_remote_copy(
        src_ref=src.at[2 * k + (1 - c)], dst_ref=dst.at[k], send_sem=send_sems.at[k, p], recv_sem=recv_sems.at[k, p],
        device_id=(x, y, 1 - c), device_id_type=MESH)
        for k in range(N_DEV // 2) for p, (src, dst) in enumerate(zip(src_refs, dst_refs))]


def _rs_pair_sum(core, pieces, offsets, rows, name, landed=()):
    cdim = pieces[0].shape[2]
    nk = N_DEV // 2
    npc = len(pieces)
    nrem = npc - len(landed)
    spans = [(o, t.shape[1]) for t, o in zip(pieces, offsets)]
    ends = [o + n for o, n in spans]
    gaps = [(a, b - a) for a, b in zip(ends, [o for o, _ in spans[1:]] + [rows]) if b > a]

    def body(core_ref, *refs):
        own, src, got, o_ref = refs[:npc], refs[npc:npc + nrem], refs[npc + nrem:2 * npc], refs[2 * npc]
        landing, send_sems, recv_sems = refs[2 * npc + 1:]
        k = pl.program_id(0)
        x, y, c = lax.axis_index("x"), lax.axis_index("y"), lax.axis_index("c")

        def copies(kk):
            return [pltpu.make_async_remote_copy(
                src_ref=src[p].at[2 * kk + (1 - c)], dst_ref=landing.at[kk, pl.ds(o, n)],
                send_sem=send_sems.at[kk, p], recv_sem=recv_sems.at[kk, p], device_id=(x, y, 1 - c),
                device_id_type=MESH) for p, (o, n) in enumerate(spans[:nrem])]

        @pl.when(k == 0)
        def _():
            for kk in range(nk):
                for cp in copies(kk):
                    cp.start()

        for cp, piece, (o, n) in zip(copies(k), own, spans):
            cp.wait_recv()
            o_ref[0, o:o + n, :] = (piece[0].astype(F32) + landing[k, o:o + n, :].astype(F32)).astype(BF16)
        for theirs, piece, (o, n) in zip(got, own[nrem:], spans[nrem:]):
            o_ref[0, o:o + n, :] = (piece[0].astype(F32) + theirs[0].astype(F32)).astype(BF16)
        for o, n in gaps:
            o_ref[0, o:o + n, :] = jnp.zeros((n, cdim), BF16)

        @pl.when(k == nk - 1)
        def _():
            for kk in range(nk):
                for cp in copies(kk):
                    cp.wait_send()

    own_specs = [pl.BlockSpec((1, n, cdim), lambda k, core_ref: (2 * k + core_ref[0], 0, 0)) for _, n in spans]
    got_specs = [pl.BlockSpec((1, n, cdim), lambda k, core_ref: (k, 0, 0)) for _, n in spans[nrem:]]
    land_rows = max(o + n for o, n in spans[:nrem])
    return pl.pallas_call(
        body, name=name, out_shape=jax.ShapeDtypeStruct((nk, rows, cdim), BF16),
        grid_spec=pltpu.PrefetchScalarGridSpec(
            num_scalar_prefetch=1, grid=(nk,),
            in_specs=own_specs + [ANY] * nrem + got_specs,
            out_specs=pl.BlockSpec((1, rows, cdim), lambda k, core_ref: (k, 0, 0)),
            scratch_shapes=[pltpu.VMEM((nk, land_rows, cdim), BF16), pltpu.SemaphoreType.DMA((nk, nrem)),
                            pltpu.SemaphoreType.DMA((nk, nrem))]),
        compiler_params=_params(1),
    )(core, *pieces, *pieces[:nrem], *landed)


def _chips_start(b_ref, out_ref, stage, send_sems, recv_sems, local_sem):
    x, y, c = lax.axis_index("x"), lax.axis_index("y"), lax.axis_index("c")
    mychip = 2 * x + y
    for j, (px, py) in enumerate([(1 - x, y), (x, 1 - y), (1 - x, 1 - y)]):
        pltpu.make_async_remote_copy(
            src_ref=b_ref.at[2 * px + py], dst_ref=out_ref.at[mychip],
            send_sem=send_sems.at[j], recv_sem=recv_sems.at[j], device_id=(px, py, c), device_id_type=MESH).start()
    _copy_through_vmem(b_ref.at[mychip], out_ref.at[mychip], stage, local_sem)


def _chips_finish(b_ref, out_ref, send_sems, recv_sems):
    x, y, c = lax.axis_index("x"), lax.axis_index("y"), lax.axis_index("c")
    for j, (px, py) in enumerate([(1 - x, y), (x, 1 - y), (1 - x, 1 - y)]):
        pltpu.make_async_remote_copy(
            src_ref=b_ref.at[2 * px + py], dst_ref=out_ref.at[2 * px + py],
            send_sem=send_sems.at[j], recv_sem=recv_sems.at[j], device_id=(px, py, c), device_id_type=MESH).wait()


def _pad_rows(a, rows):
    return jnp.pad(a, ((0, rows - a.shape[0]), (0, 0)))


def _pack_in(w_in):
    return _pad_rows(w_in[0].T, ROWS_IN)


def _unpack_in(r):
    return r[0:SHARD_IN].T[None]


def _pack_rest(w_out, w_gate, w_up, w_down, w_ple, w_pg):
    head = _pad_rows(jnp.concatenate([w_out[0], w_pg[0], w_ple[0].T.reshape(32, D_MODEL)], axis=0), OFF_GATE)
    return jnp.concatenate([head, w_gate[0].T, w_up[0].T, w_down[0]], axis=0)


def _unpack_rest(r):
    return (r[0:OFF_PG][None], r[OFF_GATE:OFF_UP].T[None], r[OFF_UP:OFF_DOWN].T[None], r[OFF_DOWN:ROWS_REST][None],
            r[OFF_PLE:OFF_PLE + 32].reshape(128, D_PLE).T[None], r[OFF_PG:OFF_PLE][None])


def _pack_small(w_pool, g_mix_pre, g_mix_post, g_ffn_pre, g_ffn_post, g_ple, g_attn, g_pool, pool_scale, b_forget,
                loss=None):
    row = lambda vrow: vrow.reshape(1, -1)
    misc = [row(pool_scale), row(b_forget), row(loss) if loss is not None else jnp.zeros((1, 1), F32),
            jnp.zeros((1, D_MODEL - COL_LOSS - 1), F32)]
    rows = [w_pool.reshape(64, D_MODEL), row(g_mix_pre), row(g_mix_post), row(g_ffn_pre), row(g_ffn_post), row(g_ple),
            jnp.concatenate([row(g_attn), row(g_pool)], axis=1), jnp.concatenate(misc, axis=1),
            jnp.zeros((SMALL_ROWS - ROW_MISC - 1, D_MODEL), F32)]
    return jnp.concatenate(rows, axis=0)


def _pack_small_late(g_mix_pre, b_forget):
    misc = [jnp.zeros((1, COL_B_FORGET), F32), b_forget.reshape(1, -1), jnp.zeros((1, D_MODEL - COL_LOSS), F32)]
    return jnp.concatenate([g_mix_pre.reshape(1, -1), jnp.zeros((ROW_MISC - ROW_G_MIX_PRE - 1, D_MODEL), F32),
                            jnp.concatenate(misc, axis=1), jnp.zeros((SMALL_ROWS - ROW_MISC - 1, D_MODEL), F32)], axis=0)


def _unpack_small(r):
    gains, misc = r[ROW_GROUP_GAINS:ROW_GROUP_GAINS + 1], r[ROW_MISC:ROW_MISC + 1]
    return dict(
        w_pool=r[0:64].reshape(1, 4, POOL_CH, POOL_CH), g_mix_pre=r[ROW_G_MIX_PRE:ROW_G_MIX_PRE + 1],
        g_mix_post=r[ROW_G_MIX_POST:ROW_G_MIX_POST + 1], g_ffn_pre=r[ROW_G_FFN_PRE:ROW_G_FFN_PRE + 1],
        g_ffn_post=r[ROW_G_FFN_POST:ROW_G_FFN_POST + 1], g_ple=r[ROW_G_PLE:ROW_G_PLE + 1],
        g_attn_grp=gains[:, 0:D_ATTN], g_pool_grp=gains[:, D_ATTN:D_ATTN + D_POOL],
        pool_scale=misc[:, 0:D_POOL], b_forget=misc[:, COL_B_FORGET:COL_B_FORGET + HEADS])


def _step(x, p, tgt, small, in_w, in_m, in_v, rest_w, rest_m, rest_v):
    core = lax.axis_index("c").astype(jnp.int32).reshape(1)
    win_t = _all_gather(in_w.astype(BF16), "gather_w_in")[:, 0:SHARD_IN].reshape(D_IN, D_MODEL)
    wqkv = win_t
    wf = _pad_rows(win_t[3 * D_ATTN:3 * D_ATTN + HEADS], LANES)
    wu = win_t[3 * D_ATTN + HEADS:]
    wpool = small["w_pool"].astype(BF16)
    bpad = jnp.pad(small["b_forget"], ((0, 0), (0, LANES - HEADS)))

    lay = _attn_layout_constants()
    rest_b = rest_w.astype(BF16)
    hn, qt3, ka, v, qat3, vt3, kt3, fl, y, mpre, gh = _pre_attn_fwd(x, small["g_mix_pre"], wqkv, wf, wu, bpad, wpool, lay,
                                                                 rest_b[0:OFF_GATE])
    a, lset3, gf = _attn_fwd(ka, qat3, vt3, rest_b[OFF_GATE:])
    wple_t = gh[:, OFF_PLE:OFF_PLE + 32].reshape(D_MODEL, D_PLE)
    mix, o, h1, hn2 = _post_attn_fwd(a, mpre, x, small["g_attn_grp"], small["g_pool_grp"], small["pool_scale"], gh,
                                     small["g_mix_post"], small["g_ffn_pre"])
    gate, up, act, ff, h2 = _ffn_fwd(hn2, gf, gf, gf, h1, small["g_ffn_post"])
    dh2, dff, dgl, dpp, h2b, pb, loss8, dg_ple, dg_ffn_post = _tail_fwd_bwd(
        h2, p, tgt, ff, wple_t, gh, small["g_ple"], small["g_ffn_post"])
    dgate, dup, dh1, dg_ffn_pre = _ffn_bwd(dff, gate, up, gf, gf, gf, h1, dh2, small["g_ffn_pre"])
    nd = N_DEV
    send_rest = [
        _wgrad(h2b, dgl, BF16, "wgrad_ple_gate").reshape(nd, 128, D_MODEL),
        _wgrad(dpp, pb, BF16, "wgrad_ple").reshape(nd, 32, D_MODEL),
        _wgrad(dgate, hn2, BF16, "wgrad_gate").reshape(nd, SHARD_FF, D_MODEL),
        _wgrad(dup, hn2, BF16, "wgrad_up").reshape(nd, SHARD_FF, D_MODEL),
        _wgrad(act, dff, BF16, "wgrad_down").reshape(nd, SHARD_FF, D_MODEL)]
    (dob, dat3, dlt3, dmpb, dy, dg_mix_post, dg_attn, dg_pool, dps), landed = _post_attn_bwd(
        dh1, o, a, mpre, gh, wpool, small["g_mix_post"], small["g_attn_grp"], small["g_pool_grp"], small["pool_scale"],
        send_rest)
    send_rest = [_wgrad(mix, dob, BF16, "wgrad_out").reshape(nd, 128, D_MODEL)] + send_rest
    pair_rest = _rs_pair_sum(core, send_rest, [0, OFF_PG, OFF_PLE, OFF_GATE, OFF_UP, OFF_DOWN], ROWS_REST,
                             "rs_pair_sum_rest", landed)

    dwp = _wgrad(y, dmpb, F32, "wgrad_pool")
    dw_pool = jnp.stack([dwp[g * POOL_CH:(g + 1) * POOL_CH, g * POOL_CH:(g + 1) * POOL_CH] for g in range(4)])
    small_part = _pack_small(dw_pool, jnp.zeros((1, D_MODEL), F32), dg_mix_post, dg_ffn_pre, dg_ffn_post, dg_ple,
                             dg_attn, dg_pool, dps, jnp.zeros((1, HEADS), F32), loss8[0:1, 0:1])
    dqt3, dkt3, dvt3, chips_rest, small_all = _attn_bwd(ka, v, kt3, qat3, qt3, dat3, lset3, dlt3, pair_rest, small_part)

    gx, dz, dg_mix_pre, db = _pre_attn_bwd(dqt3, dkt3, dvt3, fl, dy, x, dh1, small["g_mix_pre"], wqkv, wf, wu)

    pair_in = _rs_pair_sum(core, [_wgrad_in(dz, hn)], [0], ROWS_IN, "rs_pair_sum_in")

    small_late = _pack_small_late(dg_mix_pre, db[:, 0:HEADS])
    *upd_rest, chips_in, late_all = _reduce_update_rest(chips_rest, rest_w, rest_m, rest_v, pair_in, small_late)
    upd_in = _reduce_update_big(chips_in, in_w, in_m, in_v, ROWS_IN, "reduce_update_in")
    return gx, (small_all, late_all), upd_in, upd_rest


def kernel(x, p, g_mix_pre, w_in, b_forget, g_attn_grp, g_pool_grp, w_pool, pool_scale, w_out, g_mix_post, g_ffn_pre, w_ffn_gate, w_ffn_up, w_ffn_down, g_ffn_post, w_ple_proj, g_ple, w_ple_gate, loss_target, m_g_mix_pre, m_w_in, m_b_forget, m_g_attn_grp, m_g_pool_grp, m_w_pool, m_pool_scale, m_w_out, m_g_mix_post, m_g_ffn_pre, m_w_ffn_gate, m_w_ffn_up, m_w_ffn_down, m_g_ffn_post, m_w_ple_proj, m_g_ple, m_w_ple_gate, v_g_mix_pre, v_w_in, v_b_forget, v_g_attn_grp, v_g_pool_grp, v_w_pool, v_pool_scale, v_w_out, v_g_mix_post, v_g_ffn_pre, v_w_ffn_gate, v_w_ffn_up, v_w_ffn_down, v_g_ffn_post, v_w_ple_proj, v_g_ple, v_w_ple_gate):
    small = dict(w_pool=w_pool[0], g_mix_pre=g_mix_pre, g_mix_post=g_mix_post, g_ffn_pre=g_ffn_pre,
                 g_ffn_post=g_ffn_post, g_ple=g_ple, g_attn_grp=g_attn_grp, g_pool_grp=g_pool_grp,
                 pool_scale=pool_scale, b_forget=b_forget)
    gx, small_all, upd_in, upd_rest = _step(
        x[0], p[0, 0], loss_target[0], small, _pack_in(w_in), _pack_in(m_w_in), _pack_in(v_w_in),
        _pack_rest(w_out, w_ffn_gate, w_ffn_up, w_ffn_down, w_ple_proj, w_ple_gate),
        _pack_rest(m_w_out, m_w_ffn_gate, m_w_ffn_up, m_w_ffn_down, m_w_ple_proj, m_w_ple_gate),
        _pack_rest(v_w_out, v_w_ffn_gate, v_w_ffn_up, v_w_ffn_down, v_w_ple_proj, v_w_ple_gate))

    sm_w = _pack_small(w_pool, g_mix_pre, g_mix_post, g_ffn_pre, g_ffn_post, g_ple, g_attn_grp, g_pool_grp, pool_scale, b_forget)
    sm_m = _pack_small(m_w_pool, m_g_mix_pre, m_g_mix_post, m_g_ffn_pre, m_g_ffn_post, m_g_ple, m_g_attn_grp, m_g_pool_grp, m_pool_scale, m_b_forget)
    sm_v = _pack_small(v_w_pool, v_g_mix_pre, v_g_mix_post, v_g_ffn_pre, v_g_ffn_post, v_g_ple, v_g_attn_grp, v_g_pool_grp, v_pool_scale, v_b_forget)
    upd_small = _reduce_update_small(*small_all, sm_w, sm_m, sm_v)
    loss = upd_small[0][ROW_MISC, COL_LOSS]

    def leaves(k):
        b_out, b_gate, b_up, b_down, b_ple, b_pg = _unpack_rest(upd_rest[k])
        s = _unpack_small(upd_small[k])
        return (s["g_mix_pre"], _unpack_in(upd_in[k]), s["b_forget"], s["g_attn_grp"], s["g_pool_grp"], s["w_pool"],
                s["pool_scale"], b_out, s["g_mix_post"], s["g_ffn_pre"], b_gate, b_up, b_down, s["g_ffn_post"], b_ple,
                s["g_ple"], b_pg)

    return (loss, gx[None], *leaves(0), *leaves(1), *leaves(2), *leaves(3))
```

```python
import functools

import jax
import jax.numpy as jnp
from jax import lax
from jax.experimental import pallas as pl
from jax.experimental.pallas import tpu as pltpu

F32 = jnp.float32
BF16 = jnp.bfloat16
HIGHEST = lax.Precision.HIGHEST

D_MODEL = 1024
HEADS = 8
HEAD_DIM = 64
D_ATTN = HEADS * HEAD_DIM
POOL_WINDOWS = (2, 4, 8, 16)
POOL_CH = 128
D_POOL = POOL_CH * len(POOL_WINDOWS)
D_FF = 2816
D_PLE = 256
D_IN = 3 * D_ATTN + HEADS + D_POOL
RMS_EPS = 1e-6
N_DEV = 8

ADAM_LR = 0.001
ADAM_B1 = 0.9
ADAM_B2 = 0.999
ADAM_EPS = 1e-08
ADAM_WD = 0.01
ADAM_STEP = 10

LANES = 128
HALO = 16
TS = 512
TS_FF = 512
TS_WGRAD = 1024
TM_WGRAD = 2176
TQ = 256
TN_FF = 1408
NEG = -1e30
VMEM_LIMIT = 56 * 1024 * 1024

SHARD_IN = 257
ROWS_IN = 272
SHARD_FF = 352
OFF_PG = 128
OFF_PLE = 256
OFF_GATE = SHARD_FF
OFF_UP = 2 * SHARD_FF
OFF_DOWN = 3 * SHARD_FF
ROWS_REST = 4 * SHARD_FF
TR_REST = SHARD_FF

SMALL_ROWS = 72
ROW_G_MIX_PRE, ROW_G_MIX_POST, ROW_G_FFN_PRE, ROW_G_FFN_POST, ROW_G_PLE = 64, 65, 66, 67, 68
ROW_GROUP_GAINS, ROW_MISC = 69, 70
COL_B_FORGET = D_POOL
COL_LOSS = D_POOL + HEADS


def _nn(a, b):
    return jnp.dot(a, b, preferred_element_type=F32)


def _nt(a, b):
    return lax.dot_general(a, b, (((1,), (1,)), ((), ())), preferred_element_type=F32)


def _tn(a, b):
    return lax.dot_general(a, b, (((0,), (0,)), ((), ())), preferred_element_type=F32)


def _rstd(v):
    return lax.rsqrt(jnp.mean(v * v, axis=-1, keepdims=True) + RMS_EPS)


def _rms_bwd(v, g, dy):
    r = _rstd(v)
    vh = v * r
    t = dy * g
    dv = r * (t - vh * jnp.mean(t * vh, axis=-1, keepdims=True))
    return dv, jnp.sum(dy * vh, axis=0, keepdims=True)


def _split3(v):
    hi = v.astype(BF16)
    rest = v - hi.astype(F32)
    mid = rest.astype(BF16)
    return hi, mid, (rest - mid.astype(F32)).astype(BF16)


def _mask_matmul(mask, v):
    hi, mid, lo = _split3(v)
    return _nn(mask, lo) + _nn(mask, mid) + _nn(mask, hi)


def _params(n_grid):
    return pltpu.CompilerParams(dimension_semantics=("arbitrary",) * n_grid, vmem_limit_bytes=VMEM_LIMIT)


def _row(i):
    return (i, 0)


def _fixed(*_):
    return (0, 0)


def _spec_square(part):
    return pl.BlockSpec((N_DEV, 128, D_MODEL), lambda *_: (0, part, 0))


def _spec_ff(part):
    return pl.BlockSpec((TN_FF // SHARD_FF, SHARD_FF, D_MODEL), lambda i, j: (j, part, 0))


assert TS == 2 * TQ and TN_FF % SHARD_FF == 0
_HALVES = (slice(0, TQ), slice(TQ, TS))

VMEM_WHOLE = pl.BlockSpec(memory_space=pltpu.VMEM)
SMEM_WHOLE = pl.BlockSpec(memory_space=pltpu.SMEM)
ANY = pl.BlockSpec(memory_space=pl.ANY)


LOG2E = 1.4426950408889634
VROWS = HEAD_DIM + 16
AUG = 128
BIAS_LANE = HEAD_DIM
ONE_LANE = HEAD_DIM + 3
SPARE_LANE = HEADS
PART_LANES = 16
assert SPARE_LANE < PART_LANES and 3 * PART_LANES <= LANES


def _attn_layout_constants():
    import numpy as np
    place = np.zeros((D_ATTN, HEADS * AUG), np.float32)
    for r in range(D_ATTN):
        place[r, (r // HEAD_DIM) * AUG + r % HEAD_DIM] = 1.0
    bias_k = np.zeros((LANES, HEADS * AUG), np.float32)
    bias_q = np.zeros((LANES, HEADS * AUG), np.float32)
    for h in range(HEADS):
        for part in range(3):
            bias_k[part * PART_LANES + h, h * AUG + BIAS_LANE + part] = -1.0
            bias_q[part * PART_LANES + h, h * AUG + ONE_LANE + part] = 1.0
            bias_k[SPARE_LANE, h * AUG + ONE_LANE + part] = 1.0
            bias_q[SPARE_LANE, h * AUG + BIAS_LANE + part] = 1.0
    after = np.concatenate([np.arange(h * AUG + HEAD_DIM, (h + 1) * AUG) for h in range(HEADS)])
    as_bf = lambda a: jnp.asarray(a, BF16)
    return dict(place=as_bf(place), bias_k=as_bf(bias_k), bias_q_t=as_bf(bias_q[:, after].T))


def _pre_attn_fwd(x, g1, wqkv, wf, wu, bpad, wpool, lay, own_block):
    s, d = x.shape
    nt = s // TS
    sub = TS // TQ

    def body(x_ref, g_ref, wqkv_ref, wf_ref, wu_ref, b_ref, wp_ref, place_ref, bk_ref, bqt_ref, own_ref,
             hn_ref, qt_ref, ka_ref, v_ref, qat_ref, vt_ref, kt_ref, fl_ref, y_ref, mp_ref, all_ref,
             ubuf, ccar, cbuf, stage, send_sems, recv_sems, local_sem):
        i = pl.program_id(0)

        @pl.when(i == 0)
        def _():
            _gather_start(own_ref, all_ref, stage, send_sems, recv_sems, local_sem)
            ubuf[0:HALO, :] = jnp.zeros((HALO, D_POOL), F32)
            ccar[...] = jnp.zeros_like(ccar)

        @pl.when(i == max(nt - 2, 0))
        def _():
            _gather_pass_on(all_ref, send_sems, recv_sems)

        xv = x_ref[...]
        hn = (xv * _rstd(xv) * g_ref[...]).astype(BF16)
        hn_ref[...] = hn
        zq = _nt(hn, wqkv_ref[...])
        qt = (zq[:, 0:D_ATTN] * 0.125).astype(BF16).T
        qb = (zq[:, 0:D_ATTN] * (0.125 * LOG2E)).astype(BF16)
        kb = zq[:, D_ATTN:2 * D_ATTN].astype(BF16)
        vb = zq[:, 2 * D_ATTN:3 * D_ATTN].astype(BF16)
        v_ref[...] = vb

        fl = _nt(hn, wf_ref[...]) + b_ref[...]
        fl_ref[...] = fl
        logf = jax.nn.log_sigmoid(fl)
        rr = lax.broadcasted_iota(jnp.int32, (TS, TS), 0)
        cc = lax.broadcasted_iota(jnp.int32, (TS, TS), 1)
        c = _mask_matmul((cc <= rr).astype(BF16), logf) + ccar[...]
        cbuf[...] = c
        ccar[...] = cbuf[TS - 1:TS, :]
        hi, mid, lo = (part.astype(F32) for part in _split3(c * LOG2E))
        lane = lax.broadcasted_iota(jnp.int32, (TS, LANES), 1)
        later = jnp.where(lane < 2 * PART_LANES, pltpu.roll(mid, PART_LANES, 1), pltpu.roll(lo, 2 * PART_LANES, 1))
        parts = jnp.where(lane < PART_LANES, jnp.where(lane == SPARE_LANE, 1.0, hi), later).astype(BF16)
        ka_ref[...] = (_nn(kb, place_ref[...]) + _nn(parts, bk_ref[...])).astype(BF16)
        qbt = qb.T
        qbias = _nt(bqt_ref[...], parts).astype(BF16)
        vt = vb.T
        kt = kb.T
        for a in range(sub):
            cols = slice(a * TQ, (a + 1) * TQ)
            pad = AUG - HEAD_DIM
            for h in range(HEADS):
                qat_ref[a, h * AUG:h * AUG + HEAD_DIM, :] = qbt[h * HEAD_DIM:(h + 1) * HEAD_DIM, cols]
                qat_ref[a, h * AUG + HEAD_DIM:(h + 1) * AUG, :] = qbias[h * pad:(h + 1) * pad, cols]
            for ref, mat in ((qt_ref, qt), (kt_ref, kt), (vt_ref, vt)):
                for h in range(HEADS):
                    ref[a, h * VROWS:h * VROWS + HEAD_DIM, :] = mat[h * HEAD_DIM:(h + 1) * HEAD_DIM, cols]
                    ref[a, h * VROWS + HEAD_DIM:(h + 1) * VROWS, :] = jnp.ones((VROWS - HEAD_DIM, TQ), BF16)

        u = _nt(hn, wu_ref[...])
        ubuf[HALO:HALO + TS, :] = u
        t = i * TS + lax.broadcasted_iota(jnp.int32, (TS, 1), 0)
        for g, w in enumerate(POOL_WINDOWS):
            cols = slice(g * POOL_CH, (g + 1) * POOL_CH)
            sm = ubuf[:, cols]
            step = 1
            while step < w:
                sm = sm + pltpu.roll(sm, step, 0)
                step *= 2
            cnt = jnp.minimum(t + 1, w).astype(F32)
            yg = (sm[HALO:, :] / cnt - u[:, cols]).astype(BF16)
            y_ref[:, cols] = yg
            mp_ref[:, cols] = _nn(yg, wp_ref[g])
        ubuf[0:HALO, :] = u[TS - HALO:, :]

        @pl.when(i == nt - 1)
        def _():
            _gather_finish(own_ref, all_ref, send_sems, recv_sems)

    nq = s // TQ
    aug = HEADS * AUG
    outs = (
        jax.ShapeDtypeStruct((s, d), BF16), jax.ShapeDtypeStruct((nq, HEADS * VROWS, TQ), BF16),
        jax.ShapeDtypeStruct((s, aug), BF16), jax.ShapeDtypeStruct((s, D_ATTN), BF16),
        jax.ShapeDtypeStruct((nq, aug, TQ), BF16), jax.ShapeDtypeStruct((nq, HEADS * VROWS, TQ), BF16),
        jax.ShapeDtypeStruct((nq, HEADS * VROWS, TQ), BF16),
        jax.ShapeDtypeStruct((s, LANES), F32),
        jax.ShapeDtypeStruct((s, D_POOL), BF16), jax.ShapeDtypeStruct((s, D_POOL), F32),
        jax.ShapeDtypeStruct((N_DEV,) + own_block.shape, own_block.dtype),
    )
    fixed3 = lambda i: (0, 0, 0)
    tiles3 = lambda rows: pl.BlockSpec((sub, rows, TQ), lambda i: (i, 0, 0))
    return pl.pallas_call(
        body, grid=(nt,), out_shape=outs, name="pre_attn_fwd",
        in_specs=[pl.BlockSpec((TS, d), _row), pl.BlockSpec((1, d), _fixed),
                  pl.BlockSpec((3 * D_ATTN, d), _fixed), pl.BlockSpec(wf.shape, _fixed), pl.BlockSpec(wu.shape, _fixed),
                  pl.BlockSpec((1, LANES), _fixed), pl.BlockSpec(wpool.shape, fixed3),
                  pl.BlockSpec(lay["place"].shape, _fixed),
                  pl.BlockSpec(lay["bias_k"].shape, _fixed), pl.BlockSpec(lay["bias_q_t"].shape, _fixed), ANY],
        out_specs=(pl.BlockSpec((TS, d), _row), tiles3(HEADS * VROWS),
                   pl.BlockSpec((TS, aug), _row), pl.BlockSpec((TS, D_ATTN), _row),
                   tiles3(aug), tiles3(HEADS * VROWS), tiles3(HEADS * VROWS),
                   pl.BlockSpec((TS, LANES), _row),
                   pl.BlockSpec((TS, D_POOL), _row), pl.BlockSpec((TS, D_POOL), _row), ANY),
        scratch_shapes=[pltpu.VMEM((TS + HALO, D_POOL), F32), pltpu.VMEM((1, LANES), F32), pltpu.VMEM((TS, LANES), F32),
                        pltpu.VMEM(own_block.shape, own_block.dtype),
                        pltpu.SemaphoreType.DMA((7,)), pltpu.SemaphoreType.DMA((7,)), pltpu.SemaphoreType.DMA],
        compiler_params=_params(1),
    )(x, g1, wqkv, wf, wu, bpad, wpool, lay["place"], lay["bias_k"], lay["bias_q_t"], own_block)


def _causal_in_tile():
    krow = lax.broadcasted_iota(jnp.int32, (TQ, TQ), 0)
    qcol = lax.broadcasted_iota(jnp.int32, (TQ, TQ), 1)
    return krow <= qcol


def _attn_fwd(ka, qat3, vt3, own_block):
    s = ka.shape[0]
    nq = s // TQ
    pass_on_step = max(nq - 2, 0)

    def body(qa_ref, ka_ref, vt_ref, own_ref, a_ref, lset_ref, all_ref, acc, out_t, st_scr, pt_scr,
             stage, send_sems, recv_sems, local_sem):
        i = pl.program_id(0)

        @pl.when(i == 0)
        def _():
            _gather_start(own_ref, all_ref, stage, send_sems, recv_sems, local_sem)

        @pl.when(i == pass_on_step)
        def _():
            _gather_pass_on(all_ref, send_sems, recv_sems)

        acc[...] = jnp.zeros_like(acc)

        def tile(j, stats, masked):
            tile_max = []
            for h in range(HEADS):
                aug = slice(h * AUG, (h + 1) * AUG)
                st = _nn(ka_ref[pl.ds(j * TQ, TQ), aug], qa_ref[0, aug, :])
                if masked:
                    st = jnp.where(_causal_in_tile(), st, NEG)
                st_scr[h] = st
                tile_max.append(jnp.max(st, axis=0, keepdims=True))
            new, scale = [], []
            for h in range(HEADS):
                m_new = jnp.maximum(stats[h], tile_max[h])
                scale.append(jnp.exp2(stats[h] - m_new))
                pt_scr[h] = jnp.exp2(st_scr[h] - m_new).astype(BF16)
                new.append(m_new)
            for h in range(HEADS):
                rows = slice(h * VROWS, (h + 1) * VROWS)
                acc[rows, :] = scale[h] * acc[rows, :] + _nn(vt_ref[j, rows, :], pt_scr[h])
            return tuple(new)

        init = tuple(jnp.full((1, TQ), NEG, F32) for _ in range(HEADS))
        stats = lax.fori_loop(0, i, functools.partial(tile, masked=False), init)
        stats = tile(i, stats, True)
        for h in range(HEADS):
            denom = acc[h * VROWS + HEAD_DIM:h * VROWS + HEAD_DIM + 1, :]
            out_t[h * HEAD_DIM:(h + 1) * HEAD_DIM, :] = acc[h * VROWS:h * VROWS + HEAD_DIM, :] / denom
            lset_ref[0, h:h + 1, :] = stats[h] + jnp.log2(denom)
        a_ref[...] = out_t[...].T

        @pl.when(i == nq - 1)
        def _():
            _gather_finish(own_ref, all_ref, send_sems, recv_sems)

    r, cdim = own_block.shape
    return pl.pallas_call(
        body, grid=(nq,), name="attn_fwd",
        out_shape=(jax.ShapeDtypeStruct((s, D_ATTN), F32), jax.ShapeDtypeStruct((nq, HEADS, TQ), F32),
                   jax.ShapeDtypeStruct((N_DEV, r, cdim), own_block.dtype)),
        in_specs=[pl.BlockSpec((1, HEADS * AUG, TQ), lambda i: (i, 0, 0)), VMEM_WHOLE, VMEM_WHOLE, ANY],
        out_specs=(pl.BlockSpec((TQ, D_ATTN), _row), pl.BlockSpec((1, HEADS, TQ), lambda i: (i, 0, 0)), ANY),
        scratch_shapes=[pltpu.VMEM((HEADS * VROWS, TQ), F32), pltpu.VMEM((D_ATTN, TQ), F32),
                        pltpu.VMEM((HEADS, TQ, TQ), F32), pltpu.VMEM((HEADS, TQ, TQ), BF16),
                        pltpu.VMEM((r, cdim), own_block.dtype),
                        pltpu.SemaphoreType.DMA((7,)), pltpu.SemaphoreType.DMA((7,)), pltpu.SemaphoreType.DMA],
        compiler_params=_params(1),
    )(qat3, ka, vt3, own_block)


def _post_attn_fwd(a, mpre, x, g_attn, g_pool, pscale, wout, g_post, g_ffn_pre):
    s, d = x.shape

    def body(a_ref, mp_ref, x_ref, ga_ref, gp_ref, ps_ref, wo_ref, gpost_ref, gpre_ref,
             mix_ref, o_ref, h1_ref, hn2_ref):
        for rows in _HALVES:
            av = a_ref[rows, :]
            mix_ref[rows, 0:D_ATTN] = (av * _rstd(av) * ga_ref[...]).astype(BF16)
            mv = mp_ref[rows, :] * ps_ref[...]
            mix_ref[rows, D_ATTN:] = (mv * _rstd(mv) * gp_ref[...]).astype(BF16)
            o = _nn(mix_ref[rows, :], wo_ref[...].reshape(d, d))
            o_ref[rows, :] = o
            h1 = x_ref[rows, :] + o * _rstd(o) * gpost_ref[...]
            h1_ref[rows, :] = h1
            hn2_ref[rows, :] = (h1 * _rstd(h1) * gpre_ref[...]).astype(BF16)

    vec = lambda n: pl.BlockSpec((1, n), _fixed)
    return pl.pallas_call(
        body, grid=(s // TS,), name="post_attn_fwd",
        out_shape=(jax.ShapeDtypeStruct((s, d), BF16), jax.ShapeDtypeStruct((s, d), F32),
                   jax.ShapeDtypeStruct((s, d), F32), jax.ShapeDtypeStruct((s, d), BF16)),
        in_specs=[pl.BlockSpec((TS, D_ATTN), _row), pl.BlockSpec((TS, D_POOL), _row), pl.BlockSpec((TS, d), _row),
                  vec(D_ATTN), vec(D_POOL), vec(D_POOL), _spec_square(0), vec(d), vec(d)],
        out_specs=(pl.BlockSpec((TS, d), _row),) * 4,
        compiler_params=_params(1),
    )(a, mpre, x, g_attn, g_pool, pscale, wout, g_post, g_ffn_pre)


def _ffn_fwd(hn2, wg, wu, wd, h1, g_post):
    s, d = h1.shape
    nc = D_FF // TN_FF
    ts = min(TS_FF, s)

    def body(hn_ref, wg_ref, wu_ref, wd_ref, h1_ref, g_ref, gate_ref, up_ref, act_ref, ff_ref, h2_ref, acc):
        j = pl.program_id(1)

        @pl.when(j == 0)
        def _():
            acc[...] = jnp.zeros_like(acc)

        for r in range(2):
            rows = slice(r * (ts // 2), (r + 1) * (ts // 2))
            hn = hn_ref[rows, :]
            gt = _nt(hn, wg_ref[...].reshape(TN_FF, d))
            up = _nt(hn, wu_ref[...].reshape(TN_FF, d))
            gate_ref[rows, :] = gt.astype(BF16)
            up_ref[rows, :] = up.astype(BF16)
            act_ref[rows, :] = (gt * jax.nn.sigmoid(gt) * up).astype(BF16)
            acc[rows, :] += _nn(act_ref[rows, :], wd_ref[...].reshape(TN_FF, d))

        @pl.when(j == nc - 1)
        def _():
            ff = acc[...]
            ff_ref[...] = ff
            h2_ref[...] = h1_ref[...] + ff * _rstd(ff) * g_ref[...]

    rowblk = pl.BlockSpec((ts, d), lambda i, j: (i, 0))
    chunk = pl.BlockSpec((ts, TN_FF), lambda i, j: (i, j))
    return pl.pallas_call(
        body, grid=(s // ts, nc), name="ffn_fwd",
        out_shape=(jax.ShapeDtypeStruct((s, D_FF), BF16),) * 3 + (jax.ShapeDtypeStruct((s, d), F32),) * 2,
        in_specs=[rowblk, _spec_ff(0), _spec_ff(1), _spec_ff(2), rowblk, pl.BlockSpec((1, d), lambda i, j: (0, 0))],
        out_specs=(chunk, chunk, chunk, rowblk, rowblk),
        scratch_shapes=[pltpu.VMEM((ts, d), F32)],
        compiler_params=_params(2),
    )(hn2, wg, wu, wd, h1, g_post)


def _tail_fwd_bwd(h2, p, tgt, ff, wple, wpg, g_ple, g_ffn_post):
    s, d = h2.shape

    def body(h2_ref, p_ref, t_ref, ff_ref, wple_ref, wpg_ref, gple_ref, gfp_ref,
             dh2_ref, dff_ref, dgl_ref, dpp_ref, h2b_ref, pb_ref, loss_ref, dgple_ref, dgfp_ref):
        i = pl.program_id(0)

        @pl.when(i == 0)
        def _():
            loss_ref[...] = jnp.zeros_like(loss_ref)
            dgple_ref[...] = jnp.zeros_like(dgple_ref)
            dgfp_ref[...] = jnp.zeros_like(dgfp_ref)

        h2 = h2_ref[...]
        h2b = h2.astype(BF16)
        h2b_ref[...] = h2b
        pb = p_ref[...].astype(BF16)
        pb_ref[...] = pb
        pp = _nt(pb, wple_ref[...])
        gple = gple_ref[...]
        e = pp * _rstd(pp) * gple
        wpg = wpg_ref[...].reshape(d, d)
        sg = jax.nn.sigmoid(_nn(h2b, wpg))
        diff = h2 + sg * e - t_ref[...]
        sq = jnp.sum(jnp.sum(diff * diff, axis=1, keepdims=True), axis=0, keepdims=True)
        loss_ref[...] += jnp.broadcast_to(sq * (0.5 / d), loss_ref.shape)
        dh3 = diff * (1.0 / d)
        dgl = (dh3 * e * sg * (1.0 - sg)).astype(BF16)
        dgl_ref[...] = dgl
        dh2 = dh3 + _nt(dgl, wpg)
        dh2_ref[...] = dh2
        dpp, dg = _rms_bwd(pp, gple, dh3 * sg)
        dpp_ref[...] = dpp.astype(BF16)
        dgple_ref[...] += dg
        dff, dg = _rms_bwd(ff_ref[...], gfp_ref[...], dh2)
        dff_ref[...] = dff.astype(BF16)
        dgfp_ref[...] += dg

    rowblk = pl.BlockSpec((TS, d), _row)
    vec = pl.BlockSpec((1, d), _fixed)
    return pl.pallas_call(
        body, grid=(s // TS,), name="tail_fwd_bwd",
        out_shape=(jax.ShapeDtypeStruct((s, d), F32), jax.ShapeDtypeStruct((s, d), BF16),
                   jax.ShapeDtypeStruct((s, d), BF16), jax.ShapeDtypeStruct((s, d), BF16),
                   jax.ShapeDtypeStruct((s, d), BF16), jax.ShapeDtypeStruct((s, D_PLE), BF16),
                   jax.ShapeDtypeStruct((8, LANES), F32), jax.ShapeDtypeStruct((1, d), F32),
                   jax.ShapeDtypeStruct((1, d), F32)),
        in_specs=[rowblk, pl.BlockSpec((TS, D_PLE), _row), rowblk, rowblk,
                  pl.BlockSpec(wple.shape, _fixed), _spec_square(1), vec, vec],
        out_specs=(rowblk, rowblk, rowblk, rowblk, rowblk, pl.BlockSpec((TS, D_PLE), _row),
                   pl.BlockSpec((8, LANES), _fixed), vec, vec),
        compiler_params=_params(1),
    )(h2, p, tgt, ff, wple, wpg, g_ple, g_ffn_post)


def _ffn_bwd(dff, gate, up, wd, wg, wu, h1, dh2, g_pre):
    s, d = h1.shape
    nc = D_FF // TN_FF
    ts = min(TS_FF, s)

    def body(dff_ref, gate_ref, up_ref, wd_ref, wg_ref, wu_ref, h1_ref, dh2_ref, g_ref,
             dgate_ref, dup_ref, dh1_ref, dg_ref, acc):
        i = pl.program_id(0)
        j = pl.program_id(1)

        @pl.when((i == 0) & (j == 0))
        def _():
            dg_ref[...] = jnp.zeros_like(dg_ref)

        @pl.when(j == 0)
        def _():
            acc[...] = jnp.zeros_like(acc)

        for r in range(2):
            rows = slice(r * (ts // 2), (r + 1) * (ts // 2))
            dact = _nt(dff_ref[rows, :], wd_ref[...].reshape(TN_FF, d))
            gt = gate_ref[rows, :].astype(F32)
            sg = jax.nn.sigmoid(gt)
            dup_ref[rows, :] = (dact * gt * sg).astype(BF16)
            dgate_ref[rows, :] = (dact * up_ref[rows, :].astype(F32) * (sg * (1.0 + gt * (1.0 - sg)))).astype(BF16)
            acc[rows, :] += (_nn(dgate_ref[rows, :], wg_ref[...].reshape(TN_FF, d))
                             + _nn(dup_ref[rows, :], wu_ref[...].reshape(TN_FF, d)))

        @pl.when(j == nc - 1)
        def _():
            dv, dg = _rms_bwd(h1_ref[...], g_ref[...], acc[...])
            dh1_ref[...] = dh2_ref[...] + dv
            dg_ref[...] += dg

    rowblk = pl.BlockSpec((ts, d), lambda i, j: (i, 0))
    chunk = pl.BlockSpec((ts, TN_FF), lambda i, j: (i, j))
    vec = pl.BlockSpec((1, d), lambda i, j: (0, 0))
    return pl.pallas_call(
        body, grid=(s // ts, nc), name="ffn_bwd",
        out_shape=(jax.ShapeDtypeStruct((s, D_FF), BF16), jax.ShapeDtypeStruct((s, D_FF), BF16),
                   jax.ShapeDtypeStruct((s, d), F32), jax.ShapeDtypeStruct((1, d), F32)),
        in_specs=[rowblk, chunk, chunk, _spec_ff(2), _spec_ff(0), _spec_ff(1), rowblk, rowblk, vec],
        out_specs=(chunk, chunk, rowblk, vec),
        scratch_shapes=[pltpu.VMEM((ts, d), F32)],
        compiler_params=_params(2),
    )(dff, gate, up, wd, wg, wu, h1, dh2, g_pre)


def _post_attn_bwd(dh1, o, a, mpre, wout, wpool, g_post, g_attn, g_pool, pscale, send):
    s, d = dh1.shape
    sub = TS // TQ
    npc = len(send)

    def body(dh1_ref, o_ref, a_ref, mp_ref, wo_ref, wp_ref, gpost_ref, ga_ref, gp_ref, ps_ref, *refs):
        send_refs, refs = refs[:npc], refs[npc:]
        dob_ref, dat_ref, dlt_ref, dmpb_ref, dy_ref, dgpost_ref, dga_ref, dgp_ref, dps_ref = refs[:9]
        got_refs, (send_sems, recv_sems) = refs[9:9 + npc], refs[9 + npc:]
        i = pl.program_id(0)

        @pl.when(i == 0)
        def _():
            for cp in _pair_copies(send_refs, got_refs, send_sems, recv_sems):
                cp.start()
            dgpost_ref[...] = jnp.zeros_like(dgpost_ref)
            dga_ref[...] = jnp.zeros_like(dga_ref)
            dgp_ref[...] = jnp.zeros_like(dgp_ref)
            dps_ref[...] = jnp.zeros_like(dps_ref)

        do, dg = _rms_bwd(o_ref[...], gpost_ref[...], dh1_ref[...])
        dgpost_ref[...] += dg
        dob = do.astype(BF16)
        dob_ref[...] = dob
        dmix = _nt(dob, wo_ref[...].reshape(d, d))

        av = a_ref[...]
        da, dg = _rms_bwd(av, ga_ref[...], dmix[:, 0:D_ATTN])
        dga_ref[...] += dg
        dat = da.astype(BF16).T
        hsel = (lax.shift_right_logical(lax.broadcasted_iota(jnp.int32, (HEADS, D_ATTN), 1), 6)
                == lax.broadcasted_iota(jnp.int32, (HEADS, D_ATTN), 0)).astype(F32)
        dlt = lax.dot_general(hsel, da * av, (((1,), (1,)), ((), ())), precision=HIGHEST, preferred_element_type=F32)
        for q in range(sub):
            dlt_ref[q] = dlt[:, q * TQ:(q + 1) * TQ]
            dat_ref[q] = dat[:, q * TQ:(q + 1) * TQ]

        ps = ps_ref[...]
        mp = mp_ref[...]
        dm, dg = _rms_bwd(mp * ps, gp_ref[...], dmix[:, D_ATTN:])
        dgp_ref[...] += dg
        dps_ref[...] += jnp.sum(dm * mp, axis=0, keepdims=True)
        dmpb = (dm * ps).astype(BF16)
        dmpb_ref[...] = dmpb
        for g in range(len(POOL_WINDOWS)):
            cols = slice(g * POOL_CH, (g + 1) * POOL_CH)
            dy_ref[:, cols] = _nt(dmpb[:, cols], wp_ref[g])

        @pl.when(i == s // TS - 1)
        def _():
            for cp in _pair_copies(send_refs, got_refs, send_sems, recv_sems):
                cp.wait()

    rowblk = pl.BlockSpec((TS, d), _row)
    half = pl.BlockSpec((TS, D_ATTN), _row)
    vec = lambda n: pl.BlockSpec((1, n), _fixed)
    nk = N_DEV // 2
    res = pl.pallas_call(
        body, grid=(s // TS,), name="post_attn_bwd",
        out_shape=(jax.ShapeDtypeStruct((s, d), BF16), jax.ShapeDtypeStruct((s // TQ, D_ATTN, TQ), BF16),
                   jax.ShapeDtypeStruct((s // TQ, HEADS, TQ), F32), jax.ShapeDtypeStruct((s, D_POOL), BF16),
                   jax.ShapeDtypeStruct((s, D_POOL), F32), jax.ShapeDtypeStruct((1, d), F32),
                   jax.ShapeDtypeStruct((1, D_ATTN), F32), jax.ShapeDtypeStruct((1, D_POOL), F32),
                   jax.ShapeDtypeStruct((1, D_POOL), F32))
        + tuple(jax.ShapeDtypeStruct((nk,) + t.shape[1:], t.dtype) for t in send),
        in_specs=[rowblk, rowblk, half, half, _spec_square(0),
                  pl.BlockSpec(wpool.shape, lambda i: (0, 0, 0)), vec(d), vec(D_ATTN), vec(D_POOL), vec(D_POOL)]
        + [ANY] * npc,
        out_specs=(rowblk, pl.BlockSpec((sub, D_ATTN, TQ), lambda i: (i, 0, 0)),
                   pl.BlockSpec((sub, HEADS, TQ), lambda i: (i, 0, 0)), half, half,
                   vec(d), vec(D_ATTN), vec(D_POOL), vec(D_POOL)) + (ANY,) * npc,
        scratch_shapes=[pltpu.SemaphoreType.DMA((nk, npc)), pltpu.SemaphoreType.DMA((nk, npc))],
        compiler_params=_params(1),
    )(dh1, o, a, mpre, wout, wpool, g_post, g_attn, g_pool, pscale, *send)
    return res[:9], list(res[9:])


def _attn_bwd(ka, v, kt3, qat3, qt3, dot3, lset3, dlt3, chip_blocks, small_block):
    s = ka.shape[0]
    nq = s // TQ

    def body(ka_ref, v_ref, kt_ref, qat_ref, qt_ref, dot_ref, lset_ref, dlt_ref, b_ref, sm_ref,
             dqt_ref, dkt_ref, dvt_ref, got_ref, all_ref, pt_scr, ptb_scr, dsb_scr,
             stage, send_sems, recv_sems, local_sem, stage_s, send_s, recv_s, local_s):
        j = pl.program_id(0)

        @pl.when(j == 0)
        def _():
            _chips_start(b_ref, got_ref, stage, send_sems, recv_sems, local_sem)
            _gather_start(sm_ref, all_ref, stage_s, send_s, recv_s, local_s)
            dqt_ref[...] = jnp.zeros_like(dqt_ref)

        @pl.when(j == max(nq - 2, 0))
        def _():
            _gather_pass_on(all_ref, send_s, recv_s)

        def tile(i, masked):
            def accumulate(ref, idx, val):
                if masked:
                    ref[idx] = val
                else:
                    ref[idx] += val

            for h in range(HEADS):
                aug = slice(h * AUG, (h + 1) * AUG)
                st = _nn(ka_ref[:, aug], qat_ref[i, aug, :]) - lset_ref[i, h:h + 1, :]
                if masked:
                    st = jnp.where(_causal_in_tile(), st, NEG)
                pt = jnp.exp2(st)
                pt_scr[h] = pt
                ptb_scr[h] = pt.astype(BF16)
            heads = [(h, slice(h * HEAD_DIM, (h + 1) * HEAD_DIM)) for h in range(HEADS)]
            for h, hs in heads:
                dst = pt_scr[h] * (_nn(v_ref[:, hs], dot_ref[i, hs, :]) - dlt_ref[i, h:h + 1, :])
                dsb_scr[h] = dst.astype(BF16)
            for h, hs in heads:
                accumulate(dvt_ref, (0, hs, slice(None)), _nt(dot_ref[i, hs, :], ptb_scr[h]))
            for h, hs in heads:
                rows = slice(h * VROWS, (h + 1) * VROWS)
                accumulate(dkt_ref, (0, rows, slice(None)), _nt(qt_ref[i, rows, :], dsb_scr[h]))
            for h, hs in heads:
                rows = slice(h * VROWS, (h + 1) * VROWS)
                dqt_ref[i, rows, :] += _nn(kt_ref[0, rows, :], dsb_scr[h])

        first = j + 1
        pairs = (nq - first) // 2

        def step(p, carry):
            tile(first + 2 * p, False)
            tile(first + 2 * p + 1, False)
            return carry

        tile(j, True)
        lax.fori_loop(0, pairs, step, 0)

        @pl.when(first + 2 * pairs < nq)
        def _():
            tile(nq - 1, False)

        @pl.when(j == nq - 1)
        def _():
            _chips_finish(b_ref, got_ref, send_sems, recv_sems)
            _gather_finish(sm_ref, all_ref, send_s, recv_s)

    blk = pl.BlockSpec((TQ, D_ATTN), _row)
    tile_t = lambda rows: pl.BlockSpec((1, rows, TQ), lambda j: (j, 0, 0))
    per_tile = lambda rows: jax.ShapeDtypeStruct((nq, rows, TQ), F32)
    _, r, cdim = chip_blocks.shape
    dma = pltpu.SemaphoreType.DMA
    return pl.pallas_call(
        body, grid=(nq,), name="attn_bwd",
        out_shape=(per_tile(HEADS * VROWS), per_tile(HEADS * VROWS), per_tile(D_ATTN),
                   jax.ShapeDtypeStruct(chip_blocks.shape, chip_blocks.dtype),
                   jax.ShapeDtypeStruct((N_DEV,) + small_block.shape, small_block.dtype)),
        in_specs=[pl.BlockSpec((TQ, HEADS * AUG), _row), blk, tile_t(HEADS * VROWS),
                  VMEM_WHOLE, VMEM_WHOLE, VMEM_WHOLE, VMEM_WHOLE, VMEM_WHOLE, ANY, ANY],
        out_specs=(pl.BlockSpec((nq, HEADS * VROWS, TQ), lambda j: (0, 0, 0)), tile_t(HEADS * VROWS), tile_t(D_ATTN),
                   ANY, ANY),
        scratch_shapes=[pltpu.VMEM((HEADS, TQ, TQ), F32), pltpu.VMEM((HEADS, TQ, TQ), BF16),
                        pltpu.VMEM((HEADS, TQ, TQ), BF16), pltpu.VMEM((r, cdim), chip_blocks.dtype),
                        dma((3,)), dma((3,)), dma,
                        pltpu.VMEM(small_block.shape, small_block.dtype), dma((7,)), dma((7,)), dma],
        compiler_params=_params(1),
    )(ka, v, kt3, qat3, qt3, dot3, lset3, dlt3, chip_blocks, small_block)


def _pre_attn_bwd(dqt3, dkt3, dvt3, fl, dy, x, dh1, g1, wqkv, wf, wu):
    s, d = x.shape
    nt = s // TS
    n = TS + HALO
    sub = TS // TQ
    qkv, fcols = 3 * D_ATTN, 3 * D_ATTN + LANES

    def body(dqt_ref, dkt_ref, dvt_ref, fl_ref, dy_ref, x_ref, dh1_ref, g_ref, wqkv_ref, wf_ref, wu_ref,
             gx_ref, dz_ref, dg_ref, db_ref, ybuf, ccar, dlog, dsum):
        dqkv_ref = dz_ref.at[:, 0:qkv]
        dfb_ref = dz_ref.at[:, qkv:fcols]
        dub_ref = dz_ref.at[:, fcols:]
        i = pl.program_id(0)
        ti = nt - 1 - i

        @pl.when(i == 0)
        def _():
            ybuf[TS:n, :] = jnp.zeros((HALO, D_POOL), F32)
            ccar[...] = jnp.zeros_like(ccar)
            dg_ref[...] = jnp.zeros_like(dg_ref)
            db_ref[...] = jnp.zeros_like(db_ref)
            dsum[...] = jnp.zeros_like(dsum)

        for a in range(sub):
            for h in range(HEADS):
                r = h * VROWS + HEAD_DIM
                dsum[h:h + 1, a * TQ:(a + 1) * TQ] = dqt_ref[a, r:r + 1, :] - dkt_ref[a, r:r + 1, :]
        rr = lax.broadcasted_iota(jnp.int32, (TS, TS), 0)
        cc = lax.broadcasted_iota(jnp.int32, (TS, TS), 1)
        dlog[...] = ccar[...] + _mask_matmul((cc >= rr).astype(BF16), dsum[...].T)
        ccar[...] = dlog[0:1, :]
        df = dlog[...] * jax.nn.sigmoid(-fl_ref[...])
        db_ref[...] += jnp.sum(df, axis=0, keepdims=True)
        dfb = df.astype(BF16)
        dfb_ref[...] = dfb

        t = ti * TS + lax.broadcasted_iota(jnp.int32, (TS, 1), 0)
        dy = dy_ref[...]
        for g, w in enumerate(POOL_WINDOWS):
            cols = slice(g * POOL_CH, (g + 1) * POOL_CH)
            ybuf[0:TS, cols] = dy[:, cols] / jnp.minimum(t + 1, w).astype(F32)
        for g, w in enumerate(POOL_WINDOWS):
            cols = slice(g * POOL_CH, (g + 1) * POOL_CH)
            sm = ybuf[:, cols]
            step = 1
            while step < w:
                sm = sm + pltpu.roll(sm, n - step, 0)
                step *= 2
            dub_ref[:, cols] = (sm[0:TS, :] - dy[:, cols]).astype(BF16)
        ybuf[TS:n, :] = ybuf[0:HALO, :]

        for a in range(sub):
            rows = slice(a * TQ, (a + 1) * TQ)
            for h in range(HEADS):
                src = slice(h * VROWS, h * VROWS + HEAD_DIM)
                dqkv_ref[rows, h * HEAD_DIM:(h + 1) * HEAD_DIM] = (dqt_ref[a, src, :].T * 0.125).astype(BF16)
                dqkv_ref[rows, D_ATTN + h * HEAD_DIM:D_ATTN + (h + 1) * HEAD_DIM] = dkt_ref[a, src, :].T.astype(BF16)
            dqkv_ref[rows, 2 * D_ATTN:] = dvt_ref[a].T.astype(BF16)
        dhn = _nn(dqkv_ref[...], wqkv_ref[...]) + _nn(dfb, wf_ref[...]) + _nn(dub_ref[...], wu_ref[...])
        dx, dg = _rms_bwd(x_ref[...], g_ref[...], dhn)
        gx_ref[...] = dh1_ref[...] + dx
        dg_ref[...] += dg

    rev = lambda i: (nt - 1 - i, 0)
    blk = lambda w: pl.BlockSpec((TS, w), rev)
    return pl.pallas_call(
        body, grid=(nt,), name="pre_attn_bwd",
        out_shape=(jax.ShapeDtypeStruct((s, d), F32), jax.ShapeDtypeStruct((s, fcols + D_POOL), BF16),
                   jax.ShapeDtypeStruct((1, d), F32), jax.ShapeDtypeStruct((1, LANES), F32)),
        in_specs=[pl.BlockSpec((sub, HEADS * VROWS, TQ), lambda i: (nt - 1 - i, 0, 0)),
                  pl.BlockSpec((sub, HEADS * VROWS, TQ), lambda i: (nt - 1 - i, 0, 0)),
                  pl.BlockSpec((sub, D_ATTN, TQ), lambda i: (nt - 1 - i, 0, 0)),
                  blk(LANES), blk(D_POOL), blk(d), blk(d),
                  pl.BlockSpec((1, d), _fixed), pl.BlockSpec((qkv, d), _fixed), pl.BlockSpec(wf.shape, _fixed),
                  pl.BlockSpec(wu.shape, _fixed)],
        out_specs=(blk(d), blk(fcols + D_POOL), pl.BlockSpec((1, d), _fixed), pl.BlockSpec((1, LANES), _fixed)),
        scratch_shapes=[pltpu.VMEM((n, D_POOL), F32), pltpu.VMEM((1, LANES), F32), pltpu.VMEM((TS, LANES), F32),
                        pltpu.VMEM((LANES, TS), F32)],
        compiler_params=_params(1),
    )(dqt3, dkt3, dvt3, fl, dy, x, dh1, g1, wqkv, wf, wu)


def _wgrad(a, b, out_dtype, name):
    s, m = a.shape
    n = b.shape[1]
    tm = max(t for t in range(LANES, min(m, TM_WGRAD) + 1, LANES) if m % t == 0)
    ts = min(TS_WGRAD, s)
    ns = s // ts

    def body(a_ref, b_ref, o_ref, acc):
        i = pl.program_id(1)

        @pl.when(i == 0)
        def _():
            acc[...] = jnp.zeros_like(acc)

        acc[...] += _tn(a_ref[...], b_ref[...])

        @pl.when(i == ns - 1)
        def _():
            o_ref[...] = acc[...].astype(out_dtype)

    return pl.pallas_call(
        body, grid=(m // tm, ns), name=name, out_shape=jax.ShapeDtypeStruct((m, n), out_dtype),
        in_specs=[pl.BlockSpec((ts, tm), lambda j, i: (i, j)), pl.BlockSpec((ts, n), lambda j, i: (i, 0))],
        out_specs=pl.BlockSpec((tm, n), lambda j, i: (j, 0)),
        scratch_shapes=[pltpu.VMEM((tm, n), F32)],
        compiler_params=_params(2),
    )(a, b)


def _wgrad_in(dz, hn):
    s, m = dz.shape
    n = hn.shape[1]
    ts = min(TS_WGRAD, s)
    ns = s // ts
    pad_at, pad = 3 * D_ATTN + HEADS, LANES - HEADS
    assert m == D_IN + pad and N_DEV * SHARD_IN == D_IN

    def pieces(d):
        lo, hi = d * SHARD_IN, (d + 1) * SHARD_IN
        spans = [(lo, min(hi, pad_at), 0), (max(lo, pad_at), hi, pad)]
        return [(a + shift, b - a, a - lo) for a, b, shift in spans if b > a]

    def body(a_ref, b_ref, o_ref, acc, stage):
        i = pl.program_id(0)

        @pl.when(i == 0)
        def _():
            acc[...] = jnp.zeros_like(acc)

        acc[...] += _tn(a_ref[...], b_ref[...])

        @pl.when(i == ns - 1)
        def _():
            stage[SHARD_IN:ROWS_IN, :] = jnp.zeros((ROWS_IN - SHARD_IN, n), F32)
            for d in range(N_DEV):
                for src, rows, dst in pieces(d):
                    stage[dst:dst + rows, :] = acc[src:src + rows, :]
                o_ref[d] = stage[...].astype(BF16)

    return pl.pallas_call(
        body, grid=(ns,), name="wgrad_in", out_shape=jax.ShapeDtypeStruct((N_DEV, ROWS_IN, n), BF16),
        in_specs=[pl.BlockSpec((ts, m), _row), pl.BlockSpec((ts, n), _row)],
        out_specs=pl.BlockSpec((N_DEV, ROWS_IN, n), lambda i: (0, 0, 0)),
        scratch_shapes=[pltpu.VMEM((m, n), F32), pltpu.VMEM((ROWS_IN, n), F32)],
        compiler_params=_params(1),
    )(dz, hn)


def _adamw(w, g, m, v):
    m = ADAM_B1 * m + (1.0 - ADAM_B1) * g
    v = ADAM_B2 * v + (1.0 - ADAM_B2) * (g * g)
    m_hat = m / (1.0 - ADAM_B1 ** ADAM_STEP)
    v_hat = v / (1.0 - ADAM_B2 ** ADAM_STEP)
    delta = -ADAM_LR * (m_hat / (jnp.sqrt(v_hat) + ADAM_EPS) + ADAM_WD * w)
    return delta, m, v


def _sum_update(p_ref, w_ref, m_ref, v_ref, g_ref, d_ref, nm_ref, nv_ref):
    g = p_ref[0].astype(F32)
    for k in range(1, p_ref.shape[0]):
        g = g + p_ref[k].astype(F32)
    g_ref[...] = g
    d_ref[...], nm_ref[...], nv_ref[...] = _adamw(w_ref[...], g, m_ref[...], v_ref[...])


def _reduce_update_rest(parts, w, m, v, chip_blocks, small_block):
    nk, r, c = parts.shape
    ns = r // TR_REST

    def body(p_ref, w_ref, m_ref, v_ref, b_ref, sm_ref, g_ref, d_ref, nm_ref, nv_ref, got_ref, all_ref,
             stage_b, stage_s, send_b, recv_b, local_b, send_s, recv_s, local_s):
        i = pl.program_id(0)

        @pl.when(i == 0)
        def _():
            _chips_start(b_ref, got_ref, stage_b, send_b, recv_b, local_b)
            _gather_start(sm_ref, all_ref, stage_s, send_s, recv_s, local_s)

        _sum_update(p_ref, w_ref, m_ref, v_ref, g_ref, d_ref, nm_ref, nv_ref)

        @pl.when(i == ns - 1)
        def _():
            _gather_pass_on(all_ref, send_s, recv_s)
            _chips_finish(b_ref, got_ref, send_b, recv_b)
            _gather_finish(sm_ref, all_ref, send_s, recv_s)

    blk = pl.BlockSpec((TR_REST, c), _row)
    out = jax.ShapeDtypeStruct((r, c), F32)
    dma = pltpu.SemaphoreType.DMA
    return pl.pallas_call(
        body, grid=(ns,), name="reduce_update_rest",
        out_shape=(out,) * 4 + (jax.ShapeDtypeStruct(chip_blocks.shape, chip_blocks.dtype),
                                jax.ShapeDtypeStruct((N_DEV,) + small_block.shape, small_block.dtype)),
        in_specs=[pl.BlockSpec((nk, TR_REST, c), lambda i: (0, i, 0)), blk, blk, blk, ANY, ANY],
        out_specs=(blk,) * 4 + (ANY, ANY),
        scratch_shapes=[pltpu.VMEM(chip_blocks.shape[1:], chip_blocks.dtype), pltpu.VMEM(small_block.shape, small_block.dtype),
                        dma((3,)), dma((3,)), dma, dma((7,)), dma((7,)), dma],
        compiler_params=_params(1),
    )(parts, w, m, v, chip_blocks, small_block)


def _reduce_update_big(parts, w, m, v, tr, name):
    nk, r, c = parts.shape

    def body(p_ref, w_ref, m_ref, v_ref, g_ref, d_ref, nm_ref, nv_ref):
        _sum_update(p_ref, w_ref, m_ref, v_ref, g_ref, d_ref, nm_ref, nv_ref)

    blk = pl.BlockSpec((tr, c), _row)
    out = jax.ShapeDtypeStruct((r, c), F32)
    return pl.pallas_call(
        body, grid=(r // tr,), name=name, out_shape=(out,) * 4,
        in_specs=[pl.BlockSpec((nk, tr, c), lambda i: (0, i, 0)), blk, blk, blk],
        out_specs=(blk,) * 4, compiler_params=_params(1),
    )(parts, w, m, v)


def _reduce_update_small(parts, late, w, m, v):
    nd = parts.shape[0]
    first = parts.shape[1] - late.shape[1]

    def body(p_ref, q_ref, w_ref, m_ref, v_ref, g_ref, d_ref, nm_ref, nv_ref):
        g, t = p_ref[0], q_ref[0]
        for k in range(1, nd):
            g, t = g + p_ref[k], t + q_ref[k]
        g_ref[...] = g
        g_ref[first:, :] = g[first:, :] + t
        d_ref[...], nm_ref[...], nv_ref[...] = _adamw(w_ref[...], g_ref[...], m_ref[...], v_ref[...])

    out = jax.ShapeDtypeStruct(w.shape, F32)
    return pl.pallas_call(body, name="reduce_update_small", out_shape=(out,) * 4,
                          compiler_params=pltpu.CompilerParams(vmem_limit_bytes=VMEM_LIMIT))(parts, late, w, m, v)


MESH = pl.DeviceIdType.MESH


def _copy_through_vmem(src_hbm, dst_hbm, stage, sem):
    load = pltpu.make_async_copy(src_hbm, stage, sem)
    load.start()
    load.wait()
    store = pltpu.make_async_copy(stage, dst_hbm, sem)
    store.start()
    store.wait()


class _GatherPlan:
    def __init__(self, x_ref, out_ref, send_sems, recv_sems):
        x, y, c = lax.axis_index("x"), lax.axis_index("y"), lax.axis_index("c")
        self.me, self.sibling, self.c = (x, y, c), (x, y, 1 - c), c
        self.chips = [(1 - x, y), (x, 1 - y), (1 - x, 1 - y)]
        self.x_ref, self.out_ref, self.send_sems, self.recv_sems = x_ref, out_ref, send_sems, recv_sems

    def slot(self, px, py, pc):
        return self.out_ref.at[4 * px + 2 * py + pc]

    def copy(self, k, block, to, src=None):
        return pltpu.make_async_remote_copy(
            src_ref=self.slot(*block) if src is None else src, dst_ref=self.slot(*block),
            send_sem=self.send_sems.at[k], recv_sem=self.recv_sems.at[k], device_id=to, device_id_type=MESH)

    def first(self):
        return [self.copy(0, self.me, self.sibling, src=self.x_ref)] + [
            self.copy(1 + j, self.me, (*chip, self.c), src=self.x_ref) for j, chip in enumerate(self.chips)]

    def passed(self):
        return [self.copy(4 + j, (*chip, self.c), self.sibling) for j, chip in enumerate(self.chips)]


def _gather_start(x_ref, out_ref, stage, send_sems, recv_sems, local_sem):
    plan = _GatherPlan(x_ref, out_ref, send_sems, recv_sems)
    for cp in plan.first():
        cp.start()
    _copy_through_vmem(x_ref, plan.slot(*plan.me), stage, local_sem)


def _gather_pass_on(out_ref, send_sems, recv_sems):
    plan = _GatherPlan(None, out_ref, send_sems, recv_sems)
    passed = plan.passed()
    for j, chip in enumerate(plan.chips):
        plan.copy(1 + j, (*chip, plan.c), plan.me).wait_recv()
        passed[j].start()


def _gather_finish(x_ref, out_ref, send_sems, recv_sems):
    plan = _GatherPlan(x_ref, out_ref, send_sems, recv_sems)
    plan.copy(0, plan.sibling, plan.me).wait_recv()
    for j, chip in enumerate(plan.chips):
        plan.copy(4 + j, (*chip, 1 - plan.c), plan.me).wait_recv()
    for cp in plan.first() + plan.passed():
        cp.wait_send()


def _all_gather(xs, name):
    r, cdim = xs.shape

    def body(x_ref, out_ref, stage, send_sems, recv_sems, local_sem):
        _gather_start(x_ref, out_ref, stage, send_sems, recv_sems, local_sem)
        _gather_pass_on(out_ref, send_sems, recv_sems)
        _gather_finish(x_ref, out_ref, send_sems, recv_sems)

    return pl.pallas_call(
        body, name=name, out_shape=jax.ShapeDtypeStruct((N_DEV, r, cdim), xs.dtype),
        in_specs=[ANY], out_specs=ANY,
        scratch_shapes=[pltpu.VMEM((r, cdim), xs.dtype), pltpu.SemaphoreType.DMA((7,)), pltpu.SemaphoreType.DMA((7,)),
                        pltpu.SemaphoreType.DMA],
        compiler_params=pltpu.CompilerParams(vmem_limit_bytes=VMEM_LIMIT),
    )(xs)


def _pair_copies(src_refs, dst_refs, send_sems, recv_sems):
    x, y, c = lax.axis_index("x"), lax.axis_index("y"), lax.axis_index("c")
    return [pltpu.make_async_remote_copy(
        src_ref=src.at[2 * k + (1 - c)], dst_ref=dst.at[k], send_sem=send_sems.at[k, p], recv_sem=recv_sems.at[k, p],
        device_id=(x, y, 1 - c), device_id_type=MESH)
        for k in range(N_DEV // 2) for p, (src, dst) in enumerate(zip(src_refs, dst_refs))]


def _rs_pair_sum(core, pieces, offsets, rows, name, landed=()):
    cdim = pieces[0].shape[2]
    nk = N_DEV // 2
    npc = len(pieces)
    nrem = npc - len(landed)
    spans = [(o, t.shape[1]) for t, o in zip(pieces, offsets)]
    ends = [o + n for o, n in spans]
    gaps = [(a, b - a) for a, b in zip(ends, [o for o, _ in spans[1:]] + [rows]) if b > a]

    def body(core_ref, *refs):
        own, src, got, o_ref = refs[:npc], refs[npc:npc + nrem], refs[npc + nrem:2 * npc], refs[2 * npc]
        landing, send_sems, recv_sems = refs[2 * npc + 1:]
        k = pl.program_id(0)
        x, y, c = lax.axis_index("x"), lax.axis_index("y"), lax.axis_index("c")

        def copies(kk):
            return [pltpu.make_async_remote_copy(
                src_ref=src[p].at[2 * kk + (1 - c)], dst_ref=landing.at[kk, pl.ds(o, n)],
                send_sem=send_sems.at[kk, p], recv_sem=recv_sems.at[kk, p], device_id=(x, y, 1 - c),
                device_id_type=MESH) for p, (o, n) in enumerate(spans[:nrem])]

        @pl.when(k == 0)
        def _():
            for kk in range(nk):
                for cp in copies(kk):
                    cp.start()

        for cp, piece, (o, n) in zip(copies(k), own, spans):
            cp.wait_recv()
            o_ref[0, o:o + n, :] = (piece[0].astype(F32) + landing[k, o:o + n, :].astype(F32)).astype(BF16)
        for theirs, piece, (o, n) in zip(got, own[nrem:], spans[nrem:]):
            o_ref[0, o:o + n, :] = (piece[0].astype(F32) + theirs[0].astype(F32)).astype(BF16)
        for o, n in gaps:
            o_ref[0, o:o + n, :] = jnp.zeros((n, cdim), BF16)

        @pl.when(k == nk - 1)
        def _():
            for kk in range(nk):
                for cp in copies(kk):
                    cp.wait_send()

    own_specs = [pl.BlockSpec((1, n, cdim), lambda k, core_ref: (2 * k + core_ref[0], 0, 0)) for _, n in spans]
    got_specs = [pl.BlockSpec((1, n, cdim), lambda k, core_ref: (k, 0, 0)) for _, n in spans[nrem:]]
    land_rows = max(o + n for o, n in spans[:nrem])
    return pl.pallas_call(
        body, name=name, out_shape=jax.ShapeDtypeStruct((nk, rows, cdim), BF16),
        grid_spec=pltpu.PrefetchScalarGridSpec(
            num_scalar_prefetch=1, grid=(nk,),
            in_specs=own_specs + [ANY] * nrem + got_specs,
            out_specs=pl.BlockSpec((1, rows, cdim), lambda k, core_ref: (k, 0, 0)),
            scratch_shapes=[pltpu.VMEM((nk, land_rows, cdim), BF16), pltpu.SemaphoreType.DMA((nk, nrem)),
                            pltpu.SemaphoreType.DMA((nk, nrem))]),
        compiler_params=_params(1),
    )(core, *pieces, *pieces[:nrem], *landed)


def _chips_start(b_ref, out_ref, stage, send_sems, recv_sems, local_sem):
    x, y, c = lax.axis_index("x"), lax.axis_index("y"), lax.axis_index("c")
    mychip = 2 * x + y
    for j, (px, py) in enumerate([(1 - x, y), (x, 1 - y), (1 - x, 1 - y)]):
        pltpu.make_async_remote_copy(
            src_ref=b_ref.at[2 * px + py], dst_ref=out_ref.at[mychip],
            send_sem=send_sems.at[j], recv_sem=recv_sems.at[j], device_id=(px, py, c), device_id_type=MESH).start()
    _copy_through_vmem(b_ref.at[mychip], out_ref.at[mychip], stage, local_sem)


def _chips_finish(b_ref, out_ref, send_sems, recv_sems):
    x, y, c = lax.axis_index("x"), lax.axis_index("y"), lax.axis_index("c")
    for j, (px, py) in enumerate([(1 - x, y), (x, 1 - y), (1 - x, 1 - y)]):
        pltpu.make_async_remote_copy(
            src_ref=b_ref.at[2 * px + py], dst_ref=out_ref.at[2 * px + py],
            send_sem=send_sems.at[j], recv_sem=recv_sems.at[j], device_id=(px, py, c), device_id_type=MESH).wait()


def _pad_rows(a, rows):
    return jnp.pad(a, ((0, rows - a.shape[0]), (0, 0)))


def _pack_in(w_in):
    return _pad_rows(w_in[0].T, ROWS_IN)


def _unpack_in(r):
    return r[0:SHARD_IN].T[None]


def _pack_rest(w_out, w_gate, w_up, w_down, w_ple, w_pg):
    head = _pad_rows(jnp.concatenate([w_out[0], w_pg[0], w_ple[0].T.reshape(32, D_MODEL)], axis=0), OFF_GATE)
    return jnp.concatenate([head, w_gate[0].T, w_up[0].T, w_down[0]], axis=0)


def _unpack_rest(r):
    return (r[0:OFF_PG][None], r[OFF_GATE:OFF_UP].T[None], r[OFF_UP:OFF_DOWN].T[None], r[OFF_DOWN:ROWS_REST][None],
            r[OFF_PLE:OFF_PLE + 32].reshape(128, D_PLE).T[None], r[OFF_PG:OFF_PLE][None])


def _pack_small(w_pool, g_mix_pre, g_mix_post, g_ffn_pre, g_ffn_post, g_ple, g_attn, g_pool, pool_scale, b_forget,
                loss=None):
    row = lambda vrow: vrow.reshape(1, -1)
    misc = [row(pool_scale), row(b_forget), row(loss) if loss is not None else jnp.zeros((1, 1), F32),
            jnp.zeros((1, D_MODEL - COL_LOSS - 1), F32)]
    rows = [w_pool.reshape(64, D_MODEL), row(g_mix_pre), row(g_mix_post), row(g_ffn_pre), row(g_ffn_post), row(g_ple),
            jnp.concatenate([row(g_attn), row(g_pool)], axis=1), jnp.concatenate(misc, axis=1),
            jnp.zeros((SMALL_ROWS - ROW_MISC - 1, D_MODEL), F32)]
    return jnp.concatenate(rows, axis=0)


def _pack_small_late(g_mix_pre, b_forget):
    misc = [jnp.zeros((1, COL_B_FORGET), F32), b_forget.reshape(1, -1), jnp.zeros((1, D_MODEL - COL_LOSS), F32)]
    return jnp.concatenate([g_mix_pre.reshape(1, -1), jnp.zeros((ROW_MISC - ROW_G_MIX_PRE - 1, D_MODEL), F32),
                            jnp.concatenate(misc, axis=1), jnp.zeros((SMALL_ROWS - ROW_MISC - 1, D_MODEL), F32)], axis=0)


def _unpack_small(r):
    gains, misc = r[ROW_GROUP_GAINS:ROW_GROUP_GAINS + 1], r[ROW_MISC:ROW_MISC + 1]
    return dict(
        w_pool=r[0:64].reshape(1, 4, POOL_CH, POOL_CH), g_mix_pre=r[ROW_G_MIX_PRE:ROW_G_MIX_PRE + 1],
        g_mix_post=r[ROW_G_MIX_POST:ROW_G_MIX_POST + 1], g_ffn_pre=r[ROW_G_FFN_PRE:ROW_G_FFN_PRE + 1],
        g_ffn_post=r[ROW_G_FFN_POST:ROW_G_FFN_POST + 1], g_ple=r[ROW_G_PLE:ROW_G_PLE + 1],
        g_attn_grp=gains[:, 0:D_ATTN], g_pool_grp=gains[:, D_ATTN:D_ATTN + D_POOL],
        pool_scale=misc[:, 0:D_POOL], b_forget=misc[:, COL_B_FORGET:COL_B_FORGET + HEADS])


def _step(x, p, tgt, small, in_w, in_m, in_v, rest_w, rest_m, rest_v):
    core = lax.axis_index("c").astype(jnp.int32).reshape(1)
    win_t = _all_gather(in_w.astype(BF16), "gather_w_in")[:, 0:SHARD_IN].reshape(D_IN, D_MODEL)
    wqkv = win_t
    wf = _pad_rows(win_t[3 * D_ATTN:3 * D_ATTN + HEADS], LANES)
    wu = win_t[3 * D_ATTN + HEADS:]
    wpool = small["w_pool"].astype(BF16)
    bpad = jnp.pad(small["b_forget"], ((0, 0), (0, LANES - HEADS)))

    lay = _attn_layout_constants()
    rest_b = rest_w.astype(BF16)
    hn, qt3, ka, v, qat3, vt3, kt3, fl, y, mpre, gh = _pre_attn_fwd(x, small["g_mix_pre"], wqkv, wf, wu, bpad, wpool, lay,
                                                                 rest_b[0:OFF_GATE])
    a, lset3, gf = _attn_fwd(ka, qat3, vt3, rest_b[OFF_GATE:])
    wple_t = gh[:, OFF_PLE:OFF_PLE + 32].reshape(D_MODEL, D_PLE)
    mix, o, h1, hn2 = _post_attn_fwd(a, mpre, x, small["g_attn_grp"], small["g_pool_grp"], small["pool_scale"], gh,
                                     small["g_mix_post"], small["g_ffn_pre"])
    gate, up, act, ff, h2 = _ffn_fwd(hn2, gf, gf, gf, h1, small["g_ffn_post"])
    dh2, dff, dgl, dpp, h2b, pb, loss8, dg_ple, dg_ffn_post = _tail_fwd_bwd(
        h2, p, tgt, ff, wple_t, gh, small["g_ple"], small["g_ffn_post"])
    dgate, dup, dh1, dg_ffn_pre = _ffn_bwd(dff, gate, up, gf, gf, gf, h1, dh2, small["g_ffn_pre"])
    nd = N_DEV
    send_rest = [
        _wgrad(h2b, dgl, BF16, "wgrad_ple_gate").reshape(nd, 128, D_MODEL),
        _wgrad(dpp, pb, BF16, "wgrad_ple").reshape(nd, 32, D_MODEL),
        _wgrad(dgate, hn2, BF16, "wgrad_gate").reshape(nd, SHARD_FF, D_MODEL),
        _wgrad(dup, hn2, BF16, "wgrad_up").reshape(nd, SHARD_FF, D_MODEL),
        _wgrad(act, dff, BF16, "wgrad_down").reshape(nd, SHARD_FF, D_MODEL)]
    (dob, dat3, dlt3, dmpb, dy, dg_mix_post, dg_attn, dg_pool, dps), landed = _post_attn_bwd(
        dh1, o, a, mpre, gh, wpool, small["g_mix_post"], small["g_attn_grp"], small["g_pool_grp"], small["pool_scale"],
        send_rest)
    send_rest = [_wgrad(mix, dob, BF16, "wgrad_out").reshape(nd, 128, D_MODEL)] + send_rest
    pair_rest = _rs_pair_sum(core, send_rest, [0, OFF_PG, OFF_PLE, OFF_GATE, OFF_UP, OFF_DOWN], ROWS_REST,
                             "rs_pair_sum_rest", landed)

    dwp = _wgrad(y, dmpb, F32, "wgrad_pool")
    dw_pool = jnp.stack([dwp[g * POOL_CH:(g + 1) * POOL_CH, g * POOL_CH:(g + 1) * POOL_CH] for g in range(4)])
    small_part = _pack_small(dw_pool, jnp.zeros((1, D_MODEL), F32), dg_mix_post, dg_ffn_pre, dg_ffn_post, dg_ple,
                             dg_attn, dg_pool, dps, jnp.zeros((1, HEADS), F32), loss8[0:1, 0:1])
    dqt3, dkt3, dvt3, chips_rest, small_all = _attn_bwd(ka, v, kt3, qat3, qt3, dat3, lset3, dlt3, pair_rest, small_part)

    gx, dz, dg_mix_pre, db = _pre_attn_bwd(dqt3, dkt3, dvt3, fl, dy, x, dh1, small["g_mix_pre"], wqkv, wf, wu)

    pair_in = _rs_pair_sum(core, [_wgrad_in(dz, hn)], [0], ROWS_IN, "rs_pair_sum_in")

    small_late = _pack_small_late(dg_mix_pre, db[:, 0:HEADS])
    *upd_rest, chips_in, late_all = _reduce_update_rest(chips_rest, rest_w, rest_m, rest_v, pair_in, small_late)
    upd_in = _reduce_update_big(chips_in, in_w, in_m, in_v, ROWS_IN, "reduce_update_in")
    return gx, (small_all, late_all), upd_in, upd_rest


def kernel(x, p, g_mix_pre, w_in, b_forget, g_attn_grp, g_pool_grp, w_pool, pool_scale, w_out, g_mix_post, g_ffn_pre, w_ffn_gate, w_ffn_up, w_ffn_down, g_ffn_post, w_ple_proj, g_ple, w_ple_gate, loss_target, m_g_mix_pre, m_w_in, m_b_forget, m_g_attn_grp, m_g_pool_grp, m_w_pool, m_pool_scale, m_w_out, m_g_mix_post, m_g_ffn_pre, m_w_ffn_gate, m_w_ffn_up, m_w_ffn_down, m_g_ffn_post, m_w_ple_proj, m_g_ple, m_w_ple_gate, v_g_mix_pre, v_w_in, v_b_forget, v_g_attn_grp, v_g_pool_grp, v_w_pool, v_pool_scale, v_w_out, v_g_mix_post, v_g_ffn_pre, v_w_ffn_gate, v_w_ffn_up, v_w_ffn_down, v_g_ffn_post, v_w_ple_proj, v_g_ple, v_w_ple_gate):
    small = dict(w_pool=w_pool[0], g_mix_pre=g_mix_pre, g_mix_post=g_mix_post, g_ffn_pre=g_ffn_pre,
                 g_ffn_post=g_ffn_post, g_ple=g_ple, g_attn_grp=g_attn_grp, g_pool_grp=g_pool_grp,
                 pool_scale=pool_scale, b_forget=b_forget)
    gx, small_all, upd_in, upd_rest = _step(
        x[0], p[0, 0], loss_target[0], small, _pack_in(w_in), _pack_in(m_w_in), _pack_in(v_w_in),
        _pack_rest(w_out, w_ffn_gate, w_ffn_up, w_ffn_down, w_ple_proj, w_ple_gate),
        _pack_rest(m_w_out, m_w_ffn_gate, m_w_ffn_up, m_w_ffn_down, m_w_ple_proj, m_w_ple_gate),
        _pack_rest(v_w_out, v_w_ffn_gate, v_w_ffn_up, v_w_ffn_down, v_w_ple_proj, v_w_ple_gate))

    sm_w = _pack_small(w_pool, g_mix_pre, g_mix_post, g_ffn_pre, g_ffn_post, g_ple, g_attn_grp, g_pool_grp, pool_scale, b_forget)
    sm_m = _pack_small(m_w_pool, m_g_mix_pre, m_g_mix_post, m_g_ffn_pre, m_g_ffn_post, m_g_ple, m_g_attn_grp, m_g_pool_grp, m_pool_scale, m_b_forget)
    sm_v = _pack_small(v_w_pool, v_g_mix_pre, v_g_mix_post, v_g_ffn_pre, v_g_ffn_post, v_g_ple, v_g_attn_grp, v_g_pool_grp, v_pool_scale, v_b_forget)
    upd_small = _reduce_update_small(*small_all, sm_w, sm_m, sm_v)
    loss = upd_small[0][ROW_MISC, COL_LOSS]

    def leaves(k):
        b_out, b_gate, b_up, b_down, b_ple, b_pg = _unpack_rest(upd_rest[k])
        s = _unpack_small(upd_small[k])
        return (s["g_mix_pre"], _unpack_in(upd_in[k]), s["b_forget"], s["g_attn_grp"], s["g_pool_grp"], s["w_pool"],
                s["pool_scale"], b_out, s["g_mix_post"], s["g_ffn_pre"], b_gate, b_up, b_down, s["g_ffn_post"], b_ple,
                s["g_ple"], b_pg)

    return (loss, gx[None], *leaves(0), *leaves(1), *leaves(2), *leaves(3))
```

```python
import functools

import jax
import jax.numpy as jnp
from jax import lax
from jax.experimental import pallas as pl
from jax.experimental.pallas import tpu as pltpu

F32 = jnp.float32
BF16 = jnp.bfloat16
HIGHEST = lax.Precision.HIGHEST

D_MODEL = 1024
HEADS = 8
HEAD_DIM = 64
D_ATTN = HEADS * HEAD_DIM
POOL_WINDOWS = (2, 4, 8, 16)
POOL_CH = 128
D_POOL = POOL_CH * len(POOL_WINDOWS)
D_FF = 2816
D_PLE = 256
D_IN = 3 * D_ATTN + HEADS + D_POOL
RMS_EPS = 1e-6
N_DEV = 8

ADAM_LR = 0.001
ADAM_B1 = 0.9
ADAM_B2 = 0.999
ADAM_EPS = 1e-08
ADAM_WD = 0.01
ADAM_STEP = 10

LANES = 128
HALO = 16
TS = 512
TS_FF = 512
TS_WGRAD = 1024
TM_WGRAD = 2176
TQ = 256
TN_FF = 1408
NEG = -1e30
VMEM_LIMIT = 56 * 1024 * 1024

SHARD_IN = 257
ROWS_IN = 272
SHARD_FF = 352
OFF_PG = 128
OFF_PLE = 256
OFF_GATE = SHARD_FF
OFF_UP = 2 * SHARD_FF
OFF_DOWN = 3 * SHARD_FF
ROWS_REST = 4 * SHARD_FF
TR_REST = SHARD_FF

SMALL_ROWS = 72
ROW_G_MIX_PRE, ROW_G_MIX_POST, ROW_G_FFN_PRE, ROW_G_FFN_POST, ROW_G_PLE = 64, 65, 66, 67, 68
ROW_GROUP_GAINS, ROW_MISC = 69, 70
COL_B_FORGET = D_POOL
COL_LOSS = D_POOL + HEADS


def _nn(a, b):
    return jnp.dot(a, b, preferred_element_type=F32)


def _nt(a, b):
    return lax.dot_general(a, b, (((1,), (1,)), ((), ())), preferred_element_type=F32)


def _tn(a, b):
    return lax.dot_general(a, b, (((0,), (0,)), ((), ())), preferred_element_type=F32)


def _rstd(v):
    return lax.rsqrt(jnp.mean(v * v, axis=-1, keepdims=True) + RMS_EPS)


def _rms_bwd(v, g, dy):
    r = _rstd(v)
    vh = v * r
    t = dy * g
    dv = r * (t - vh * jnp.mean(t * vh, axis=-1, keepdims=True))
    return dv, jnp.sum(dy * vh, axis=0, keepdims=True)


def _split3(v):
    hi = v.astype(BF16)
    rest = v - hi.astype(F32)
    mid = rest.astype(BF16)
    return hi, mid, (rest - mid.astype(F32)).astype(BF16)


def _mask_matmul(mask, v):
    hi, mid, lo = _split3(v)
    return _nn(mask, lo) + _nn(mask, mid) + _nn(mask, hi)


def _params(n_grid):
    return pltpu.CompilerParams(dimension_semantics=("arbitrary",) * n_grid, vmem_limit_bytes=VMEM_LIMIT)


def _row(i):
    return (i, 0)


def _fixed(*_):
    return (0, 0)


def _spec_square(part):
    return pl.BlockSpec((N_DEV, 128, D_MODEL), lambda *_: (0, part, 0))


def _spec_ff(part):
    return pl.BlockSpec((TN_FF // SHARD_FF, SHARD_FF, D_MODEL), lambda i, j: (j, part, 0))


assert TS == 2 * TQ and TN_FF % SHARD_FF == 0
_HALVES = (slice(0, TQ), slice(TQ, TS))

VMEM_WHOLE = pl.BlockSpec(memory_space=pltpu.VMEM)
SMEM_WHOLE = pl.BlockSpec(memory_space=pltpu.SMEM)
ANY = pl.BlockSpec(memory_space=pl.ANY)


LOG2E = 1.4426950408889634
VROWS = HEAD_DIM + 16
AUG = 128
BIAS_LANE = HEAD_DIM
ONE_LANE = HEAD_DIM + 3
SPARE_LANE = HEADS
PART_LANES = 16
assert SPARE_LANE < PART_LANES and 3 * PART_LANES <= LANES


def _attn_layout_constants():
    import numpy as np
    bias_k = np.zeros((LANES, HEADS * AUG), np.float32)
    bias_q = np.zeros((LANES, HEADS * AUG), np.float32)
    for h in range(HEADS):
        for part in range(3):
            bias_k[part * PART_LANES + h, h * AUG + BIAS_LANE + part] = -1.0
            bias_q[part * PART_LANES + h, h * AUG + ONE_LANE + part] = 1.0
            bias_k[SPARE_LANE, h * AUG + ONE_LANE + part] = 1.0
            bias_q[SPARE_LANE, h * AUG + BIAS_LANE + part] = 1.0
    after = np.concatenate([np.arange(h * AUG + HEAD_DIM, (h + 1) * AUG) for h in range(HEADS)])
    as_bf = lambda a: jnp.asarray(a, BF16)
    return dict(bias_k=as_bf(bias_k[:, after]), bias_q_t=as_bf(bias_q[:, after].T))


def _pre_attn_fwd(x, g1, wqkv, wf, wu, bpad, wpool, lay, own_block):
    s, d = x.shape
    nt = s // TS
    sub = TS // TQ

    def body(x_ref, g_ref, wqkv_ref, wf_ref, wu_ref, b_ref, wp_ref, bk_ref, bqt_ref, own_ref,
             hn_ref, qt_ref, ka_ref, v_ref, qat_ref, vt_ref, kt_ref, fl_ref, y_ref, mp_ref, all_ref,
             ubuf, ccar, cbuf, stage, send_sems, recv_sems, local_sem):
        i = pl.program_id(0)

        @pl.when(i == 0)
        def _():
            _gather_start(own_ref, all_ref, stage, send_sems, recv_sems, local_sem)
            ubuf[0:HALO, :] = jnp.zeros((HALO, D_POOL), F32)
            ccar[...] = jnp.zeros_like(ccar)

        @pl.when(i == max(nt - 2, 0))
        def _():
            _gather_pass_on(all_ref, send_sems, recv_sems)

        xv = x_ref[...]
        hn = (xv * _rstd(xv) * g_ref[...]).astype(BF16)
        hn_ref[...] = hn
        zq = _nt(hn, wqkv_ref[...])
        qt = (zq[:, 0:D_ATTN] * 0.125).astype(BF16).T
        qb = (zq[:, 0:D_ATTN] * (0.125 * LOG2E)).astype(BF16)
        kb = zq[:, D_ATTN:2 * D_ATTN].astype(BF16)
        vb = zq[:, 2 * D_ATTN:3 * D_ATTN].astype(BF16)
        v_ref[...] = vb

        fl = _nt(hn, wf_ref[...]) + b_ref[...]
        fl_ref[...] = fl
        logf = jax.nn.log_sigmoid(fl)
        rr = lax.broadcasted_iota(jnp.int32, (TS, TS), 0)
        cc = lax.broadcasted_iota(jnp.int32, (TS, TS), 1)
        c = _mask_matmul((cc <= rr).astype(BF16), logf) + ccar[...]
        cbuf[...] = c
        ccar[...] = cbuf[TS - 1:TS, :]
        hi, mid, lo = (part.astype(F32) for part in _split3(c * LOG2E))
        lane = lax.broadcasted_iota(jnp.int32, (TS, LANES), 1)
        later = jnp.where(lane < 2 * PART_LANES, pltpu.roll(mid, PART_LANES, 1), pltpu.roll(lo, 2 * PART_LANES, 1))
        parts = jnp.where(lane < PART_LANES, jnp.where(lane == SPARE_LANE, 1.0, hi), later).astype(BF16)
        extra = AUG - HEAD_DIM
        kbias = _nn(parts, bk_ref[...]).astype(BF16)
        for h in range(HEADS):
            ka_ref[:, h * AUG:h * AUG + HEAD_DIM] = kb[:, h * HEAD_DIM:(h + 1) * HEAD_DIM]
            ka_ref[:, h * AUG + HEAD_DIM:(h + 1) * AUG] = kbias[:, h * extra:(h + 1) * extra]
        qbt = qb.T
        qbias = _nt(bqt_ref[...], parts).astype(BF16)
        vt = vb.T
        kt = kb.T
        for a in range(sub):
            cols = slice(a * TQ, (a + 1) * TQ)
            for h in range(HEADS):
                qat_ref[a, h * AUG:h * AUG + HEAD_DIM, :] = qbt[h * HEAD_DIM:(h + 1) * HEAD_DIM, cols]
                qat_ref[a, h * AUG + HEAD_DIM:(h + 1) * AUG, :] = qbias[h * extra:(h + 1) * extra, cols]
            for ref, mat in ((qt_ref, qt), (kt_ref, kt), (vt_ref, vt)):
                for h in range(HEADS):
                    ref[a, h * VROWS:h * VROWS + HEAD_DIM, :] = mat[h * HEAD_DIM:(h + 1) * HEAD_DIM, cols]
                    ref[a, h * VROWS + HEAD_DIM:(h + 1) * VROWS, :] = jnp.ones((VROWS - HEAD_DIM, TQ), BF16)

        u = _nt(hn, wu_ref[...])
        ubuf[HALO:HALO + TS, :] = u
        t = i * TS + lax.broadcasted_iota(jnp.int32, (TS, 1), 0)
        for g, w in enumerate(POOL_WINDOWS):
            cols = slice(g * POOL_CH, (g + 1) * POOL_CH)
            sm = ubuf[:, cols]
            step = 1
            while step < w:
                sm = sm + pltpu.roll(sm, step, 0)
                step *= 2
            cnt = jnp.minimum(t + 1, w).astype(F32)
            yg = (sm[HALO:, :] / cnt - u[:, cols]).astype(BF16)
            y_ref[:, cols] = yg
            mp_ref[:, cols] = _nn(yg, wp_ref[g])
        ubuf[0:HALO, :] = u[TS - HALO:, :]

        @pl.when(i == nt - 1)
        def _():
            _gather_finish(own_ref, all_ref, send_sems, recv_sems)

    nq = s // TQ
    aug = HEADS * AUG
    outs = (
        jax.ShapeDtypeStruct((s, d), BF16), jax.ShapeDtypeStruct((nq, HEADS * VROWS, TQ), BF16),
        jax.ShapeDtypeStruct((s, aug), BF16), jax.ShapeDtypeStruct((s, D_ATTN), BF16),
        jax.ShapeDtypeStruct((nq, aug, TQ), BF16), jax.ShapeDtypeStruct((nq, HEADS * VROWS, TQ), BF16),
        jax.ShapeDtypeStruct((nq, HEADS * VROWS, TQ), BF16),
        jax.ShapeDtypeStruct((s, LANES), F32),
        jax.ShapeDtypeStruct((s, D_POOL), BF16), jax.ShapeDtypeStruct((s, D_POOL), F32),
        jax.ShapeDtypeStruct((N_DEV,) + own_block.shape, own_block.dtype),
    )
    fixed3 = lambda i: (0, 0, 0)
    tiles3 = lambda rows: pl.BlockSpec((sub, rows, TQ), lambda i: (i, 0, 0))
    return pl.pallas_call(
        body, grid=(nt,), out_shape=outs, name="pre_attn_fwd",
        in_specs=[pl.BlockSpec((TS, d), _row), pl.BlockSpec((1, d), _fixed),
                  pl.BlockSpec((3 * D_ATTN, d), _fixed), pl.BlockSpec(wf.shape, _fixed), pl.BlockSpec(wu.shape, _fixed),
                  pl.BlockSpec((1, LANES), _fixed), pl.BlockSpec(wpool.shape, fixed3),
                  pl.BlockSpec(lay["bias_k"].shape, _fixed), pl.BlockSpec(lay["bias_q_t"].shape, _fixed), ANY],
        out_specs=(pl.BlockSpec((TS, d), _row), tiles3(HEADS * VROWS),
                   pl.BlockSpec((TS, aug), _row), pl.BlockSpec((TS, D_ATTN), _row),
                   tiles3(aug), tiles3(HEADS * VROWS), tiles3(HEADS * VROWS),
                   pl.BlockSpec((TS, LANES), _row),
                   pl.BlockSpec((TS, D_POOL), _row), pl.BlockSpec((TS, D_POOL), _row), ANY),
        scratch_shapes=[pltpu.VMEM((TS + HALO, D_POOL), F32), pltpu.VMEM((1, LANES), F32), pltpu.VMEM((TS, LANES), F32),
                        pltpu.VMEM(own_block.shape, own_block.dtype),
                        pltpu.SemaphoreType.DMA((7,)), pltpu.SemaphoreType.DMA((7,)), pltpu.SemaphoreType.DMA],
        compiler_params=_params(1),
    )(x, g1, wqkv, wf, wu, bpad, wpool, lay["bias_k"], lay["bias_q_t"], own_block)


def _causal_in_tile():
    krow = lax.broadcasted_iota(jnp.int32, (TQ, TQ), 0)
    qcol = lax.broadcasted_iota(jnp.int32, (TQ, TQ), 1)
    return krow <= qcol


def _attn_fwd(ka, qat3, vt3, own_block):
    s = ka.shape[0]
    nq = s // TQ
    pass_on_step = max(nq - 2, 0)

    def body(qa_ref, ka_ref, vt_ref, own_ref, a_ref, lset_ref, all_ref, acc, out_t, st_scr, pt_scr,
             stage, send_sems, recv_sems, local_sem):
        i = pl.program_id(0)

        @pl.when(i == 0)
        def _():
            _gather_start(own_ref, all_ref, stage, send_sems, recv_sems, local_sem)

        @pl.when(i == pass_on_step)
        def _():
            _gather_pass_on(all_ref, send_sems, recv_sems)

        acc[...] = jnp.zeros_like(acc)

        def tile(j, stats, masked):
            tile_max = []
            for h in range(HEADS):
                aug = slice(h * AUG, (h + 1) * AUG)
                st = _nn(ka_ref[pl.ds(j * TQ, TQ), aug], qa_ref[0, aug, :])
                if masked:
                    st = jnp.where(_causal_in_tile(), st, NEG)
                st_scr[h] = st
                tile_max.append(jnp.max(st, axis=0, keepdims=True))
            new, scale = [], []
            for h in range(HEADS):
                m_new = jnp.maximum(stats[h], tile_max[h])
                scale.append(jnp.exp2(stats[h] - m_new))
                pt_scr[h] = jnp.exp2(st_scr[h] - m_new).astype(BF16)
                new.append(m_new)
            for h in range(HEADS):
                rows = slice(h * VROWS, (h + 1) * VROWS)
                acc[rows, :] = scale[h] * acc[rows, :] + _nn(vt_ref[j, rows, :], pt_scr[h])
            return tuple(new)

        init = tuple(jnp.full((1, TQ), NEG, F32) for _ in range(HEADS))
        stats = lax.fori_loop(0, i, functools.partial(tile, masked=False), init)
        stats = tile(i, stats, True)
        for h in range(HEADS):
            denom = acc[h * VROWS + HEAD_DIM:h * VROWS + HEAD_DIM + 1, :]
            out_t[h * HEAD_DIM:(h + 1) * HEAD_DIM, :] = acc[h * VROWS:h * VROWS + HEAD_DIM, :] / denom
            lset_ref[0, h:h + 1, :] = stats[h] + jnp.log2(denom)
        a_ref[...] = out_t[...].T

        @pl.when(i == nq - 1)
        def _():
            _gather_finish(own_ref, all_ref, send_sems, recv_sems)

    r, cdim = own_block.shape
    return pl.pallas_call(
        body, grid=(nq,), name="attn_fwd",
        out_shape=(jax.ShapeDtypeStruct((s, D_ATTN), F32), jax.ShapeDtypeStruct((nq, HEADS, TQ), F32),
                   jax.ShapeDtypeStruct((N_DEV, r, cdim), own_block.dtype)),
        in_specs=[pl.BlockSpec((1, HEADS * AUG, TQ), lambda i: (i, 0, 0)), VMEM_WHOLE, VMEM_WHOLE, ANY],
        out_specs=(pl.BlockSpec((TQ, D_ATTN), _row), pl.BlockSpec((1, HEADS, TQ), lambda i: (i, 0, 0)), ANY),
        scratch_shapes=[pltpu.VMEM((HEADS * VROWS, TQ), F32), pltpu.VMEM((D_ATTN, TQ), F32),
                        pltpu.VMEM((HEADS, TQ, TQ), F32), pltpu.VMEM((HEADS, TQ, TQ), BF16),
                        pltpu.VMEM((r, cdim), own_block.dtype),
                        pltpu.SemaphoreType.DMA((7,)), pltpu.SemaphoreType.DMA((7,)), pltpu.SemaphoreType.DMA],
        compiler_params=_params(1),
    )(qat3, ka, vt3, own_block)


def _post_attn_fwd(a, mpre, x, g_attn, g_pool, pscale, wout, g_post, g_ffn_pre):
    s, d = x.shape

    def body(a_ref, mp_ref, x_ref, ga_ref, gp_ref, ps_ref, wo_ref, gpost_ref, gpre_ref,
             mix_ref, o_ref, h1_ref, hn2_ref):
        for rows in _HALVES:
            av = a_ref[rows, :]
            mix_ref[rows, 0:D_ATTN] = (av * _rstd(av) * ga_ref[...]).astype(BF16)
            mv = mp_ref[rows, :] * ps_ref[...]
            mix_ref[rows, D_ATTN:] = (mv * _rstd(mv) * gp_ref[...]).astype(BF16)
            o = _nn(mix_ref[rows, :], wo_ref[...].reshape(d, d))
            o_ref[rows, :] = o
            h1 = x_ref[rows, :] + o * _rstd(o) * gpost_ref[...]
            h1_ref[rows, :] = h1
            hn2_ref[rows, :] = (h1 * _rstd(h1) * gpre_ref[...]).astype(BF16)

    vec = lambda n: pl.BlockSpec((1, n), _fixed)
    return pl.pallas_call(
        body, grid=(s // TS,), name="post_attn_fwd",
        out_shape=(jax.ShapeDtypeStruct((s, d), BF16), jax.ShapeDtypeStruct((s, d), F32),
                   jax.ShapeDtypeStruct((s, d), F32), jax.ShapeDtypeStruct((s, d), BF16)),
        in_specs=[pl.BlockSpec((TS, D_ATTN), _row), pl.BlockSpec((TS, D_POOL), _row), pl.BlockSpec((TS, d), _row),
                  vec(D_ATTN), vec(D_POOL), vec(D_POOL), _spec_square(0), vec(d), vec(d)],
        out_specs=(pl.BlockSpec((TS, d), _row),) * 4,
        compiler_params=_params(1),
    )(a, mpre, x, g_attn, g_pool, pscale, wout, g_post, g_ffn_pre)


def _ffn_fwd(hn2, wg, wu, wd, h1, g_post):
    s, d = h1.shape
    nc = D_FF // TN_FF
    ts = min(TS_FF, s)

    def body(hn_ref, wg_ref, wu_ref, wd_ref, h1_ref, g_ref, gate_ref, up_ref, act_ref, ff_ref, h2_ref, acc):
        j = pl.program_id(1)

        @pl.when(j == 0)
        def _():
            acc[...] = jnp.zeros_like(acc)

        for r in range(2):
            rows = slice(r * (ts // 2), (r + 1) * (ts // 2))
            hn = hn_ref[rows, :]
            gt = _nt(hn, wg_ref[...].reshape(TN_FF, d))
            up = _nt(hn, wu_ref[...].reshape(TN_FF, d))
            gate_ref[rows, :] = gt.astype(BF16)
            up_ref[rows, :] = up.astype(BF16)
            act_ref[rows, :] = (gt * jax.nn.sigmoid(gt) * up).astype(BF16)
            acc[rows, :] += _nn(act_ref[rows, :], wd_ref[...].reshape(TN_FF, d))

        @pl.when(j == nc - 1)
        def _():
            ff = acc[...]
            ff_ref[...] = ff
            h2_ref[...] = h1_ref[...] + ff * _rstd(ff) * g_ref[...]

    rowblk = pl.BlockSpec((ts, d), lambda i, j: (i, 0))
    chunk = pl.BlockSpec((ts, TN_FF), lambda i, j: (i, j))
    return pl.pallas_call(
        body, grid=(s // ts, nc), name="ffn_fwd",
        out_shape=(jax.ShapeDtypeStruct((s, D_FF), BF16),) * 3 + (jax.ShapeDtypeStruct((s, d), F32),) * 2,
        in_specs=[rowblk, _spec_ff(0), _spec_ff(1), _spec_ff(2), rowblk, pl.BlockSpec((1, d), lambda i, j: (0, 0))],
        out_specs=(chunk, chunk, chunk, rowblk, rowblk),
        scratch_shapes=[pltpu.VMEM((ts, d), F32)],
        compiler_params=_params(2),
    )(hn2, wg, wu, wd, h1, g_post)


def _tail_fwd_bwd(h2, p, tgt, ff, wple, wpg, g_ple, g_ffn_post):
    s, d = h2.shape

    def body(h2_ref, p_ref, t_ref, ff_ref, wple_ref, wpg_ref, gple_ref, gfp_ref,
             dh2_ref, dff_ref, dgl_ref, dpp_ref, h2b_ref, pb_ref, loss_ref, dgple_ref, dgfp_ref):
        i = pl.program_id(0)

        @pl.when(i == 0)
        def _():
            loss_ref[...] = jnp.zeros_like(loss_ref)
            dgple_ref[...] = jnp.zeros_like(dgple_ref)
            dgfp_ref[...] = jnp.zeros_like(dgfp_ref)

        h2 = h2_ref[...]
        h2b = h2.astype(BF16)
        h2b_ref[...] = h2b
        pb = p_ref[...].astype(BF16)
        pb_ref[...] = pb
        pp = _nt(pb, wple_ref[...])
        gple = gple_ref[...]
        e = pp * _rstd(pp) * gple
        wpg = wpg_ref[...].reshape(d, d)
        sg = jax.nn.sigmoid(_nn(h2b, wpg))
        diff = h2 + sg * e - t_ref[...]
        sq = jnp.sum(jnp.sum(diff * diff, axis=1, keepdims=True), axis=0, keepdims=True)
        loss_ref[...] += jnp.broadcast_to(sq * (0.5 / d), loss_ref.shape)
        dh3 = diff * (1.0 / d)
        dgl = (dh3 * e * sg * (1.0 - sg)).astype(BF16)
        dgl_ref[...] = dgl
        dh2 = dh3 + _nt(dgl, wpg)
        dh2_ref[...] = dh2
        dpp, dg = _rms_bwd(pp, gple, dh3 * sg)
        dpp_ref[...] = dpp.astype(BF16)
        dgple_ref[...] += dg
        dff, dg = _rms_bwd(ff_ref[...], gfp_ref[...], dh2)
        dff_ref[...] = dff.astype(BF16)
        dgfp_ref[...] += dg

    rowblk = pl.BlockSpec((TS, d), _row)
    vec = pl.BlockSpec((1, d), _fixed)
    return pl.pallas_call(
        body, grid=(s // TS,), name="tail_fwd_bwd",
        out_shape=(jax.ShapeDtypeStruct((s, d), F32), jax.ShapeDtypeStruct((s, d), BF16),
                   jax.ShapeDtypeStruct((s, d), BF16), jax.ShapeDtypeStruct((s, d), BF16),
                   jax.ShapeDtypeStruct((s, d), BF16), jax.ShapeDtypeStruct((s, D_PLE), BF16),
                   jax.ShapeDtypeStruct((8, LANES), F32), jax.ShapeDtypeStruct((1, d), F32),
                   jax.ShapeDtypeStruct((1, d), F32)),
        in_specs=[rowblk, pl.BlockSpec((TS, D_PLE), _row), rowblk, rowblk,
                  pl.BlockSpec(wple.shape, _fixed), _spec_square(1), vec, vec],
        out_specs=(rowblk, rowblk, rowblk, rowblk, rowblk, pl.BlockSpec((TS, D_PLE), _row),
                   pl.BlockSpec((8, LANES), _fixed), vec, vec),
        compiler_params=_params(1),
    )(h2, p, tgt, ff, wple, wpg, g_ple, g_ffn_post)


def _ffn_bwd(dff, gate, up, wd, wg, wu, h1, dh2, g_pre):
    s, d = h1.shape
    nc = D_FF // TN_FF
    ts = min(TS_FF, s)

    def body(dff_ref, gate_ref, up_ref, wd_ref, wg_ref, wu_ref, h1_ref, dh2_ref, g_ref,
             dgate_ref, dup_ref, dh1_ref, dg_ref, acc):
        i = pl.program_id(0)
        j = pl.program_id(1)

        @pl.when((i == 0) & (j == 0))
        def _():
            dg_ref[...] = jnp.zeros_like(dg_ref)

        @pl.when(j == 0)
        def _():
            acc[...] = jnp.zeros_like(acc)

        for r in range(2):
            rows = slice(r * (ts // 2), (r + 1) * (ts // 2))
            dact = _nt(dff_ref[rows, :], wd_ref[...].reshape(TN_FF, d))
            gt = gate_ref[rows, :].astype(F32)
            sg = jax.nn.sigmoid(gt)
            dup_ref[rows, :] = (dact * gt * sg).astype(BF16)
            dgate_ref[rows, :] = (dact * up_ref[rows, :].astype(F32) * (sg * (1.0 + gt * (1.0 - sg)))).astype(BF16)
            acc[rows, :] += (_nn(dgate_ref[rows, :], wg_ref[...].reshape(TN_FF, d))
                             + _nn(dup_ref[rows, :], wu_ref[...].reshape(TN_FF, d)))

        @pl.when(j == nc - 1)
        def _():
            dv, dg = _rms_bwd(h1_ref[...], g_ref[...], acc[...])
            dh1_ref[...] = dh2_ref[...] + dv
            dg_ref[...] += dg

    rowblk = pl.BlockSpec((ts, d), lambda i, j: (i, 0))
    chunk = pl.BlockSpec((ts, TN_FF), lambda i, j: (i, j))
    vec = pl.BlockSpec((1, d), lambda i, j: (0, 0))
    return pl.pallas_call(
        body, grid=(s // ts, nc), name="ffn_bwd",
        out_shape=(jax.ShapeDtypeStruct((s, D_FF), BF16), jax.ShapeDtypeStruct((s, D_FF), BF16),
                   jax.ShapeDtypeStruct((s, d), F32), jax.ShapeDtypeStruct((1, d), F32)),
        in_specs=[rowblk, chunk, chunk, _spec_ff(2), _spec_ff(0), _spec_ff(1), rowblk, rowblk, vec],
        out_specs=(chunk, chunk, rowblk, vec),
        scratch_shapes=[pltpu.VMEM((ts, d), F32)],
        compiler_params=_params(2),
    )(dff, gate, up, wd, wg, wu, h1, dh2, g_pre)


def _post_attn_bwd(dh1, o, a, mpre, wout, wpool, g_post, g_attn, g_pool, pscale, send):
    s, d = dh1.shape
    sub = TS // TQ
    npc = len(send)

    def body(dh1_ref, o_ref, a_ref, mp_ref, wo_ref, wp_ref, gpost_ref, ga_ref, gp_ref, ps_ref, *refs):
        send_refs, refs = refs[:npc], refs[npc:]
        dob_ref, dat_ref, dlt_ref, dmpb_ref, dy_ref, dgpost_ref, dga_ref, dgp_ref, dps_ref = refs[:9]
        got_refs, (send_sems, recv_sems) = refs[9:9 + npc], refs[9 + npc:]
        i = pl.program_id(0)

        @pl.when(i == 0)
        def _():
            for cp in _pair_copies(send_refs, got_refs, send_sems, recv_sems):
                cp.start()
            dgpost_ref[...] = jnp.zeros_like(dgpost_ref)
            dga_ref[...] = jnp.zeros_like(dga_ref)
            dgp_ref[...] = jnp.zeros_like(dgp_ref)
            dps_ref[...] = jnp.zeros_like(dps_ref)

        do, dg = _rms_bwd(o_ref[...], gpost_ref[...], dh1_ref[...])
        dgpost_ref[...] += dg
        dob = do.astype(BF16)
        dob_ref[...] = dob
        dmix = _nt(dob, wo_ref[...].reshape(d, d))

        av = a_ref[...]
        da, dg = _rms_bwd(av, ga_ref[...], dmix[:, 0:D_ATTN])
        dga_ref[...] += dg
        dat = da.astype(BF16).T
        hsel = (lax.shift_right_logical(lax.broadcasted_iota(jnp.int32, (HEADS, D_ATTN), 1), 6)
                == lax.broadcasted_iota(jnp.int32, (HEADS, D_ATTN), 0)).astype(F32)
        dlt = lax.dot_general(hsel, da * av, (((1,), (1,)), ((), ())), precision=HIGHEST, preferred_element_type=F32)
        for q in range(sub):
            dlt_ref[q] = dlt[:, q * TQ:(q + 1) * TQ]
            dat_ref[q] = dat[:, q * TQ:(q + 1) * TQ]

        ps = ps_ref[...]
        mp = mp_ref[...]
        dm, dg = _rms_bwd(mp * ps, gp_ref[...], dmix[:, D_ATTN:])
        dgp_ref[...] += dg
        dps_ref[...] += jnp.sum(dm * mp, axis=0, keepdims=True)
        dmpb = (dm * ps).astype(BF16)
        dmpb_ref[...] = dmpb
        for g in range(len(POOL_WINDOWS)):
            cols = slice(g * POOL_CH, (g + 1) * POOL_CH)
            dy_ref[:, cols] = _nt(dmpb[:, cols], wp_ref[g])

        @pl.when(i == s // TS - 1)
        def _():
            for cp in _pair_copies(send_refs, got_refs, send_sems, recv_sems):
                cp.wait()

    rowblk = pl.BlockSpec((TS, d), _row)
    half = pl.BlockSpec((TS, D_ATTN), _row)
    vec = lambda n: pl.BlockSpec((1, n), _fixed)
    nk = N_DEV // 2
    res = pl.pallas_call(
        body, grid=(s // TS,), name="post_attn_bwd",
        out_shape=(jax.ShapeDtypeStruct((s, d), BF16), jax.ShapeDtypeStruct((s // TQ, D_ATTN, TQ), BF16),
                   jax.ShapeDtypeStruct((s // TQ, HEADS, TQ), F32), jax.ShapeDtypeStruct((s, D_POOL), BF16),
                   jax.ShapeDtypeStruct((s, D_POOL), F32), jax.ShapeDtypeStruct((1, d), F32),
                   jax.ShapeDtypeStruct((1, D_ATTN), F32), jax.ShapeDtypeStruct((1, D_POOL), F32),
                   jax.ShapeDtypeStruct((1, D_POOL), F32))
        + tuple(jax.ShapeDtypeStruct((nk,) + t.shape[1:], t.dtype) for t in send),
        in_specs=[rowblk, rowblk, half, half, _spec_square(0),
                  pl.BlockSpec(wpool.shape, lambda i: (0, 0, 0)), vec(d), vec(D_ATTN), vec(D_POOL), vec(D_POOL)]
        + [ANY] * npc,
        out_specs=(rowblk, pl.BlockSpec((sub, D_ATTN, TQ), lambda i: (i, 0, 0)),
                   pl.BlockSpec((sub, HEADS, TQ), lambda i: (i, 0, 0)), half, half,
                   vec(d), vec(D_ATTN), vec(D_POOL), vec(D_POOL)) + (ANY,) * npc,
        scratch_shapes=[pltpu.SemaphoreType.DMA((nk, npc)), pltpu.SemaphoreType.DMA((nk, npc))],
        compiler_params=_params(1),
    )(dh1, o, a, mpre, wout, wpool, g_post, g_attn, g_pool, pscale, *send)
    return res[:9], list(res[9:])


def _attn_bwd(ka, v, kt3, qat3, qt3, dot3, lset3, dlt3, chip_blocks, small_block):
    s = ka.shape[0]
    nq = s // TQ

    def body(ka_ref, v_ref, kt_ref, qat_ref, qt_ref, dot_ref, lset_ref, dlt_ref, b_ref, sm_ref,
             dqt_ref, dkt_ref, dvt_ref, got_ref, all_ref, pt_scr, ptb_scr, dsb_scr,
             stage, send_sems, recv_sems, local_sem, stage_s, send_s, recv_s, local_s):
        j = pl.program_id(0)

        @pl.when(j == 0)
        def _():
            _chips_start(b_ref, got_ref, stage, send_sems, recv_sems, local_sem)
            _gather_start(sm_ref, all_ref, stage_s, send_s, recv_s, local_s)
            dqt_ref[...] = jnp.zeros_like(dqt_ref)

        @pl.when(j == max(nq - 2, 0))
        def _():
            _gather_pass_on(all_ref, send_s, recv_s)

        def tile(i, masked):
            def accumulate(ref, idx, val):
                if masked:
                    ref[idx] = val
                else:
                    ref[idx] += val

            for h in range(HEADS):
                aug = slice(h * AUG, (h + 1) * AUG)
                st = _nn(ka_ref[:, aug], qat_ref[i, aug, :]) - lset_ref[i, h:h + 1, :]
                if masked:
                    st = jnp.where(_causal_in_tile(), st, NEG)
                pt = jnp.exp2(st)
                pt_scr[h] = pt
                ptb_scr[h] = pt.astype(BF16)
            heads = [(h, slice(h * HEAD_DIM, (h + 1) * HEAD_DIM)) for h in range(HEADS)]
            for h, hs in heads:
                dst = pt_scr[h] * (_nn(v_ref[:, hs], dot_ref[i, hs, :]) - dlt_ref[i, h:h + 1, :])
                dsb_scr[h] = dst.astype(BF16)
            for h, hs in heads:
                accumulate(dvt_ref, (0, hs, slice(None)), _nt(dot_ref[i, hs, :], ptb_scr[h]))
            for h, hs in heads:
                rows = slice(h * VROWS, (h + 1) * VROWS)
                accumulate(dkt_ref, (0, rows, slice(None)), _nt(qt_ref[i, rows, :], dsb_scr[h]))
            for h, hs in heads:
                rows = slice(h * VROWS, (h + 1) * VROWS)
                dqt_ref[i, rows, :] += _nn(kt_ref[0, rows, :], dsb_scr[h])

        first = j + 1
        pairs = (nq - first) // 2

        def step(p, carry):
            tile(first + 2 * p, False)
            tile(first + 2 * p + 1, False)
            return carry

        tile(j, True)
        lax.fori_loop(0, pairs, step, 0)

        @pl.when(first + 2 * pairs < nq)
        def _():
            tile(nq - 1, False)

        @pl.when(j == nq - 1)
        def _():
            _chips_finish(b_ref, got_ref, send_sems, recv_sems)
            _gather_finish(sm_ref, all_ref, send_s, recv_s)

    blk = pl.BlockSpec((TQ, D_ATTN), _row)
    tile_t = lambda rows: pl.BlockSpec((1, rows, TQ), lambda j: (j, 0, 0))
    per_tile = lambda rows: jax.ShapeDtypeStruct((nq, rows, TQ), F32)
    _, r, cdim = chip_blocks.shape
    dma = pltpu.SemaphoreType.DMA
    return pl.pallas_call(
        body, grid=(nq,), name="attn_bwd",
        out_shape=(per_tile(HEADS * VROWS), per_tile(HEADS * VROWS), per_tile(D_ATTN),
                   jax.ShapeDtypeStruct(chip_blocks.shape, chip_blocks.dtype),
                   jax.ShapeDtypeStruct((N_DEV,) + small_block.shape, small_block.dtype)),
        in_specs=[pl.BlockSpec((TQ, HEADS * AUG), _row), blk, tile_t(HEADS * VROWS),
                  VMEM_WHOLE, VMEM_WHOLE, VMEM_WHOLE, VMEM_WHOLE, VMEM_WHOLE, ANY, ANY],
        out_specs=(pl.BlockSpec((nq, HEADS * VROWS, TQ), lambda j: (0, 0, 0)), tile_t(HEADS * VROWS), tile_t(D_ATTN),
                   ANY, ANY),
        scratch_shapes=[pltpu.VMEM((HEADS, TQ, TQ), F32), pltpu.VMEM((HEADS, TQ, TQ), BF16),
                        pltpu.VMEM((HEADS, TQ, TQ), BF16), pltpu.VMEM((r, cdim), chip_blocks.dtype),
                        dma((3,)), dma((3,)), dma,
                        pltpu.VMEM(small_block.shape, small_block.dtype), dma((7,)), dma((7,)), dma],
        compiler_params=_params(1),
    )(ka, v, kt3, qat3, qt3, dot3, lset3, dlt3, chip_blocks, small_block)


def _pre_attn_bwd(dqt3, dkt3, dvt3, fl, dy, x, dh1, g1, wqkv, wf, wu):
    s, d = x.shape
    nt = s // TS
    n = TS + HALO
    sub = TS // TQ
    qkv, fcols = 3 * D_ATTN, 3 * D_ATTN + LANES

    def body(dqt_ref, dkt_ref, dvt_ref, fl_ref, dy_ref, x_ref, dh1_ref, g_ref, wqkv_ref, wf_ref, wu_ref,
             gx_ref, dz_ref, dg_ref, db_ref, ybuf, ccar, dlog, dsum):
        dqkv_ref = dz_ref.at[:, 0:qkv]
        dfb_ref = dz_ref.at[:, qkv:fcols]
        dub_ref = dz_ref.at[:, fcols:]
        i = pl.program_id(0)
        ti = nt - 1 - i

        @pl.when(i == 0)
        def _():
            ybuf[TS:n, :] = jnp.zeros((HALO, D_POOL), F32)
            ccar[...] = jnp.zeros_like(ccar)
            dg_ref[...] = jnp.zeros_like(dg_ref)
            db_ref[...] = jnp.zeros_like(db_ref)
            dsum[...] = jnp.zeros_like(dsum)

        for a in range(sub):
            for h in range(HEADS):
                r = h * VROWS + HEAD_DIM
                dsum[h:h + 1, a * TQ:(a + 1) * TQ] = dqt_ref[a, r:r + 1, :] - dkt_ref[a, r:r + 1, :]
        rr = lax.broadcasted_iota(jnp.int32, (TS, TS), 0)
        cc = lax.broadcasted_iota(jnp.int32, (TS, TS), 1)
        dlog[...] = ccar[...] + _mask_matmul((cc >= rr).astype(BF16), dsum[...].T)
        ccar[...] = dlog[0:1, :]
        df = dlog[...] * jax.nn.sigmoid(-fl_ref[...])
        db_ref[...] += jnp.sum(df, axis=0, keepdims=True)
        dfb = df.astype(BF16)
        dfb_ref[...] = dfb

        t = ti * TS + lax.broadcasted_iota(jnp.int32, (TS, 1), 0)
        dy = dy_ref[...]
        for g, w in enumerate(POOL_WINDOWS):
            cols = slice(g * POOL_CH, (g + 1) * POOL_CH)
            ybuf[0:TS, cols] = dy[:, cols] / jnp.minimum(t + 1, w).astype(F32)
        for g, w in enumerate(POOL_WINDOWS):
            cols = slice(g * POOL_CH, (g + 1) * POOL_CH)
            sm = ybuf[:, cols]
            step = 1
            while step < w:
                sm = sm + pltpu.roll(sm, n - step, 0)
                step *= 2
            dub_ref[:, cols] = (sm[0:TS, :] - dy[:, cols]).astype(BF16)
        ybuf[TS:n, :] = ybuf[0:HALO, :]

        for a in range(sub):
            rows = slice(a * TQ, (a + 1) * TQ)
            for h in range(HEADS):
                src = slice(h * VROWS, h * VROWS + HEAD_DIM)
                dqkv_ref[rows, h * HEAD_DIM:(h + 1) * HEAD_DIM] = (dqt_ref[a, src, :].T * 0.125).astype(BF16)
                dqkv_ref[rows, D_ATTN + h * HEAD_DIM:D_ATTN + (h + 1) * HEAD_DIM] = dkt_ref[a, src, :].T.astype(BF16)
            dqkv_ref[rows, 2 * D_ATTN:] = dvt_ref[a].T.astype(BF16)
        dhn = _nn(dqkv_ref[...], wqkv_ref[...]) + _nn(dfb, wf_ref[...]) + _nn(dub_ref[...], wu_ref[...])
        dx, dg = _rms_bwd(x_ref[...], g_ref[...], dhn)
        gx_ref[...] = dh1_ref[...] + dx
        dg_ref[...] += dg

    rev = lambda i: (nt - 1 - i, 0)
    blk = lambda w: pl.BlockSpec((TS, w), rev)
    return pl.pallas_call(
        body, grid=(nt,), name="pre_attn_bwd",
        out_shape=(jax.ShapeDtypeStruct((s, d), F32), jax.ShapeDtypeStruct((s, fcols + D_POOL), BF16),
                   jax.ShapeDtypeStruct((1, d), F32), jax.ShapeDtypeStruct((1, LANES), F32)),
        in_specs=[pl.BlockSpec((sub, HEADS * VROWS, TQ), lambda i: (nt - 1 - i, 0, 0)),
                  pl.BlockSpec((sub, HEADS * VROWS, TQ), lambda i: (nt - 1 - i, 0, 0)),
                  pl.BlockSpec((sub, D_ATTN, TQ), lambda i: (nt - 1 - i, 0, 0)),
                  blk(LANES), blk(D_POOL), blk(d), blk(d),
                  pl.BlockSpec((1, d), _fixed), pl.BlockSpec((qkv, d), _fixed), pl.BlockSpec(wf.shape, _fixed),
                  pl.BlockSpec(wu.shape, _fixed)],
        out_specs=(blk(d), blk(fcols + D_POOL), pl.BlockSpec((1, d), _fixed), pl.BlockSpec((1, LANES), _fixed)),
        scratch_shapes=[pltpu.VMEM((n, D_POOL), F32), pltpu.VMEM((1, LANES), F32), pltpu.VMEM((TS, LANES), F32),
                        pltpu.VMEM((LANES, TS), F32)],
        compiler_params=_params(1),
    )(dqt3, dkt3, dvt3, fl, dy, x, dh1, g1, wqkv, wf, wu)


def _wgrad(a, b, out_dtype, name):
    s, m = a.shape
    n = b.shape[1]
    tm = max(t for t in range(LANES, min(m, TM_WGRAD) + 1, LANES) if m % t == 0)
    ts = min(TS_WGRAD, s)
    ns = s // ts

    def body(a_ref, b_ref, o_ref, acc):
        i = pl.program_id(1)

        @pl.when(i == 0)
        def _():
            acc[...] = jnp.zeros_like(acc)

        acc[...] += _tn(a_ref[...], b_ref[...])

        @pl.when(i == ns - 1)
        def _():
            o_ref[...] = acc[...].astype(out_dtype)

    return pl.pallas_call(
        body, grid=(m // tm, ns), name=name, out_shape=jax.ShapeDtypeStruct((m, n), out_dtype),
        in_specs=[pl.BlockSpec((ts, tm), lambda j, i: (i, j)), pl.BlockSpec((ts, n), lambda j, i: (i, 0))],
        out_specs=pl.BlockSpec((tm, n), lambda j, i: (j, 0)),
        scratch_shapes=[pltpu.VMEM((tm, n), F32)],
        compiler_params=_params(2),
    )(a, b)


def _wgrad_in(dz, hn):
    s, m = dz.shape
    n = hn.shape[1]
    ts = min(TS_WGRAD, s)
    ns = s // ts
    pad_at, pad = 3 * D_ATTN + HEADS, LANES - HEADS
    assert m == D_IN + pad and N_DEV * SHARD_IN == D_IN

    def pieces(d):
        lo, hi = d * SHARD_IN, (d + 1) * SHARD_IN
        spans = [(lo, min(hi, pad_at), 0), (max(lo, pad_at), hi, pad)]
        return [(a + shift, b - a, a - lo) for a, b, shift in spans if b > a]

    def body(a_ref, b_ref, o_ref, acc, stage):
        i = pl.program_id(0)

        @pl.when(i == 0)
        def _():
            acc[...] = jnp.zeros_like(acc)

        acc[...] += _tn(a_ref[...], b_ref[...])

        @pl.when(i == ns - 1)
        def _():
            stage[SHARD_IN:ROWS_IN, :] = jnp.zeros((ROWS_IN - SHARD_IN, n), F32)
            for d in range(N_DEV):
                for src, rows, dst in pieces(d):
                    stage[dst:dst + rows, :] = acc[src:src + rows, :]
                o_ref[d] = stage[...].astype(BF16)

    return pl.pallas_call(
        body, grid=(ns,), name="wgrad_in", out_shape=jax.ShapeDtypeStruct((N_DEV, ROWS_IN, n), BF16),
        in_specs=[pl.BlockSpec((ts, m), _row), pl.BlockSpec((ts, n), _row)],
        out_specs=pl.BlockSpec((N_DEV, ROWS_IN, n), lambda i: (0, 0, 0)),
        scratch_shapes=[pltpu.VMEM((m, n), F32), pltpu.VMEM((ROWS_IN, n), F32)],
        compiler_params=_params(1),
    )(dz, hn)


def _adamw(w, g, m, v):
    m = ADAM_B1 * m + (1.0 - ADAM_B1) * g
    v = ADAM_B2 * v + (1.0 - ADAM_B2) * (g * g)
    m_hat = m / (1.0 - ADAM_B1 ** ADAM_STEP)
    v_hat = v / (1.0 - ADAM_B2 ** ADAM_STEP)
    delta = -ADAM_LR * (m_hat / (jnp.sqrt(v_hat) + ADAM_EPS) + ADAM_WD * w)
    return delta, m, v


def _sum_update(p_ref, w_ref, m_ref, v_ref, g_ref, d_ref, nm_ref, nv_ref):
    g = p_ref[0].astype(F32)
    for k in range(1, p_ref.shape[0]):
        g = g + p_ref[k].astype(F32)
    g_ref[...] = g
    d_ref[...], nm_ref[...], nv_ref[...] = _adamw(w_ref[...], g, m_ref[...], v_ref[...])


def _reduce_update_rest(parts, w, m, v, chip_blocks, small_block):
    nk, r, c = parts.shape
    ns = r // TR_REST

    def body(p_ref, w_ref, m_ref, v_ref, b_ref, sm_ref, g_ref, d_ref, nm_ref, nv_ref, got_ref, all_ref,
             stage_b, stage_s, send_b, recv_b, local_b, send_s, recv_s, local_s):
        i = pl.program_id(0)

        @pl.when(i == 0)
        def _():
            _chips_start(b_ref, got_ref, stage_b, send_b, recv_b, local_b)
            _gather_start(sm_ref, all_ref, stage_s, send_s, recv_s, local_s)

        _sum_update(p_ref, w_ref, m_ref, v_ref, g_ref, d_ref, nm_ref, nv_ref)

        @pl.when(i == ns - 1)
        def _():
            _gather_pass_on(all_ref, send_s, recv_s)
            _chips_finish(b_ref, got_ref, send_b, recv_b)
            _gather_finish(sm_ref, all_ref, send_s, recv_s)

    blk = pl.BlockSpec((TR_REST, c), _row)
    out = jax.ShapeDtypeStruct((r, c), F32)
    dma = pltpu.SemaphoreType.DMA
    return pl.pallas_call(
        body, grid=(ns,), name="reduce_update_rest",
        out_shape=(out,) * 4 + (jax.ShapeDtypeStruct(chip_blocks.shape, chip_blocks.dtype),
                                jax.ShapeDtypeStruct((N_DEV,) + small_block.shape, small_block.dtype)),
        in_specs=[pl.BlockSpec((nk, TR_REST, c), lambda i: (0, i, 0)), blk, blk, blk, ANY, ANY],
        out_specs=(blk,) * 4 + (ANY, ANY),
        scratch_shapes=[pltpu.VMEM(chip_blocks.shape[1:], chip_blocks.dtype), pltpu.VMEM(small_block.shape, small_block.dtype),
                        dma((3,)), dma((3,)), dma, dma((7,)), dma((7,)), dma],
        compiler_params=_params(1),
    )(parts, w, m, v, chip_blocks, small_block)


def _reduce_update_big(parts, w, m, v, tr, name):
    nk, r, c = parts.shape

    def body(p_ref, w_ref, m_ref, v_ref, g_ref, d_ref, nm_ref, nv_ref):
        _sum_update(p_ref, w_ref, m_ref, v_ref, g_ref, d_ref, nm_ref, nv_ref)

    blk = pl.BlockSpec((tr, c), _row)
    out = jax.ShapeDtypeStruct((r, c), F32)
    return pl.pallas_call(
        body, grid=(r // tr,), name=name, out_shape=(out,) * 4,
        in_specs=[pl.BlockSpec((nk, tr, c), lambda i: (0, i, 0)), blk, blk, blk],
        out_specs=(blk,) * 4, compiler_params=_params(1),
    )(parts, w, m, v)


def _reduce_update_small(parts, late, w, m, v):
    nd = parts.shape[0]
    first = parts.shape[1] - late.shape[1]

    def body(p_ref, q_ref, w_ref, m_ref, v_ref, g_ref, d_ref, nm_ref, nv_ref):
        g, t = p_ref[0], q_ref[0]
        for k in range(1, nd):
            g, t = g + p_ref[k], t + q_ref[k]
        g_ref[...] = g
        g_ref[first:, :] = g[first:, :] + t
        d_ref[...], nm_ref[...], nv_ref[...] = _adamw(w_ref[...], g_ref[...], m_ref[...], v_ref[...])

    out = jax.ShapeDtypeStruct(w.shape, F32)
    return pl.pallas_call(body, name="reduce_update_small", out_shape=(out,) * 4,
                          compiler_params=pltpu.CompilerParams(vmem_limit_bytes=VMEM_LIMIT))(parts, late, w, m, v)


MESH = pl.DeviceIdType.MESH


def _copy_through_vmem(src_hbm, dst_hbm, stage, sem):
    load = pltpu.make_async_copy(src_hbm, stage, sem)
    load.start()
    load.wait()
    store = pltpu.make_async_copy(stage, dst_hbm, sem)
    store.start()
    store.wait()


class _GatherPlan:
    def __init__(self, x_ref, out_ref, send_sems, recv_sems):
        x, y, c = lax.axis_index("x"), lax.axis_index("y"), lax.axis_index("c")
        self.me, self.sibling, self.c = (x, y, c), (x, y, 1 - c), c
        self.chips = [(1 - x, y), (x, 1 - y), (1 - x, 1 - y)]
        self.x_ref, self.out_ref, self.send_sems, self.recv_sems = x_ref, out_ref, send_sems, recv_sems

    def slot(self, px, py, pc):
        return self.out_ref.at[4 * px + 2 * py + pc]

    def copy(self, k, block, to, src=None):
        return pltpu.make_async_remote_copy(
            src_ref=self.slot(*block) if src is None else src, dst_ref=self.slot(*block),
            send_sem=self.send_sems.at[k], recv_sem=self.recv_sems.at[k], device_id=to, device_id_type=MESH)

    def first(self):
        return [self.copy(0, self.me, self.sibling, src=self.x_ref)] + [
            self.copy(1 + j, self.me, (*chip, self.c), src=self.x_ref) for j, chip in enumerate(self.chips)]

    def passed(self):
        return [self.copy(4 + j, (*chip, self.c), self.sibling) for j, chip in enumerate(self.chips)]


def _gather_start(x_ref, out_ref, stage, send_sems, recv_sems, local_sem):
    plan = _GatherPlan(x_ref, out_ref, send_sems, recv_sems)
    for cp in plan.first():
        cp.start()
    _copy_through_vmem(x_ref, plan.slot(*plan.me), stage, local_sem)


def _gather_pass_on(out_ref, send_sems, recv_sems):
    plan = _GatherPlan(None, out_ref, send_sems, recv_sems)
    passed = plan.passed()
    for j, chip in enumerate(plan.chips):
        plan.copy(1 + j, (*chip, plan.c), plan.me).wait_recv()
        passed[j].start()


def _gather_finish(x_ref, out_ref, send_sems, recv_sems):
    plan = _GatherPlan(x_ref, out_ref, send_sems, recv_sems)
    plan.copy(0, plan.sibling, plan.me).wait_recv()
    for j, chip in enumerate(plan.chips):
        plan.copy(4 + j, (*chip, 1 - plan.c), plan.me).wait_recv()
    for cp in plan.first() + plan.passed():
        cp.wait_send()


def _all_gather(xs, name):
    r, cdim = xs.shape

    def body(x_ref, out_ref, stage, send_sems, recv_sems, local_sem):
        _gather_start(x_ref, out_ref, stage, send_sems, recv_sems, local_sem)
        _gather_pass_on(out_ref, send_sems, recv_sems)
        _gather_finish(x_ref, out_ref, send_sems, recv_sems)

    return pl.pallas_call(
        body, name=name, out_shape=jax.ShapeDtypeStruct((N_DEV, r, cdim), xs.dtype),
        in_specs=[ANY], out_specs=ANY,
        scratch_shapes=[pltpu.VMEM((r, cdim), xs.dtype), pltpu.SemaphoreType.DMA((7,)), pltpu.SemaphoreType.DMA((7,)),
                        pltpu.SemaphoreType.DMA],
        compiler_params=pltpu.CompilerParams(vmem_limit_bytes=VMEM_LIMIT),
    )(xs)


def _pair_copies(src_refs, dst_refs, send_sems, recv_sems):
    x, y, c = lax.axis_index("x"), lax.axis_index("y"), lax.axis_index("c")
    return [pltpu.make_async_remote_copy(
        src_ref=src.at[2 * k + (1 - c)], dst_ref=dst.at[k], send_sem=send_sems.at[k, p], recv_sem=recv_sems.at[k, p],
        device_id=(x, y, 1 - c), device_id_type=MESH)
        for k in range(N_DEV // 2) for p, (src, dst) in enumerate(zip(src_refs, dst_refs))]


def _rs_pair_sum(core, pieces, offsets, rows, name, landed=()):
    cdim = pieces[0].shape[2]
    nk = N_DEV // 2
    npc = len(pieces)
    nrem = npc - len(landed)
    spans = [(o, t.shape[1]) for t, o in zip(pieces, offsets)]
    ends = [o + n for o, n in spans]
    gaps = [(a, b - a) for a, b in zip(ends, [o for o, _ in spans[1:]] + [rows]) if b > a]

    def body(core_ref, *refs):
        own, src, got, o_ref = refs[:npc], refs[npc:npc + nrem], refs[npc + nrem:2 * npc], refs[2 * npc]
        landing, send_sems, recv_sems = refs[2 * npc + 1:]
        k = pl.program_id(0)
        x, y, c = lax.axis_index("x"), lax.axis_index("y"), lax.axis_index("c")

        def copies(kk):
            return [pltpu.make_async_remote_copy(
                src_ref=src[p].at[2 * kk + (1 - c)], dst_ref=landing.at[kk, pl.ds(o, n)],
                send_sem=send_sems.at[kk, p], recv_sem=recv_sems.at[kk, p], device_id=(x, y, 1 - c),
                device_id_type=MESH) for p, (o, n) in enumerate(spans[:nrem])]

        @pl.when(k == 0)
        def _():
            for kk in range(nk):
                for cp in copies(kk):
                    cp.start()

        for cp, piece, (o, n) in zip(copies(k), own, spans):
            cp.wait_recv()
            o_ref[0, o:o + n, :] = (piece[0].astype(F32) + landing[k, o:o + n, :].astype(F32)).astype(BF16)
        for theirs, piece, (o, n) in zip(got, own[nrem:], spans[nrem:]):
            o_ref[0, o:o + n, :] = (piece[0].astype(F32) + theirs[0].astype(F32)).astype(BF16)
        for o, n in gaps:
            o_ref[0, o:o + n, :] = jnp.zeros((n, cdim), BF16)

        @pl.when(k == nk - 1)
        def _():
            for kk in range(nk):
                for cp in copies(kk):
                    cp.wait_send()

    own_specs = [pl.BlockSpec((1, n, cdim), lambda k, core_ref: (2 * k + core_ref[0], 0, 0)) for _, n in spans]
    got_specs = [pl.BlockSpec((1, n, cdim), lambda k, core_ref: (k, 0, 0)) for _, n in spans[nrem:]]
    land_rows = max(o + n for o, n in spans[:nrem])
    return pl.pallas_call(
        body, name=name, out_shape=jax.ShapeDtypeStruct((nk, rows, cdim), BF16),
        grid_spec=pltpu.PrefetchScalarGridSpec(
            num_scalar_prefetch=1, grid=(nk,),
            in_specs=own_specs + [ANY] * nrem + got_specs,
            out_specs=pl.BlockSpec((1, rows, cdim), lambda k, core_ref: (k, 0, 0)),
            scratch_shapes=[pltpu.VMEM((nk, land_rows, cdim), BF16), pltpu.SemaphoreType.DMA((nk, nrem)),
                            pltpu.SemaphoreType.DMA((nk, nrem))]),
        compiler_params=_params(1),
    )(core, *pieces, *pieces[:nrem], *landed)


def _chips_start(b_ref, out_ref, stage, send_sems, recv_sems, local_sem):
    x, y, c = lax.axis_index("x"), lax.axis_index("y"), lax.axis_index("c")
    mychip = 2 * x + y
    for j, (px, py) in enumerate([(1 - x, y), (x, 1 - y), (1 - x, 1 - y)]):
        pltpu.make_async_remote_copy(
            src_ref=b_ref.at[2 * px + py], dst_ref=out_ref.at[mychip],
            send_sem=send_sems.at[j], recv_sem=recv_sems.at[j], device_id=(px, py, c), device_id_type=MESH).start()
    _copy_through_vmem(b_ref.at[mychip], out_ref.at[mychip], stage, local_sem)


def _chips_finish(b_ref, out_ref, send_sems, recv_sems):
    x, y, c = lax.axis_index("x"), lax.axis_index("y"), lax.axis_index("c")
    for j, (px, py) in enumerate([(1 - x, y), (x, 1 - y), (1 - x, 1 - y)]):
        pltpu.make_async_remote_copy(
            src_ref=b_ref.at[2 * px + py], dst_ref=out_ref.at[2 * px + py],
            send_sem=send_sems.at[j], recv_sem=recv_sems.at[j], device_id=(px, py, c), device_id_type=MESH).wait()


def _pad_rows(a, rows):
    return jnp.pad(a, ((0, rows - a.shape[0]), (0, 0)))


def _pack_in(w_in):
    return _pad_rows(w_in[0].T, ROWS_IN)


def _unpack_in(r):
    return r[0:SHARD_IN].T[None]


def _pack_rest(w_out, w_gate, w_up, w_down, w_ple, w_pg):
    head = _pad_rows(jnp.concatenate([w_out[0], w_pg[0], w_ple[0].T.reshape(32, D_MODEL)], axis=0), OFF_GATE)
    return jnp.concatenate([head, w_gate[0].T, w_up[0].T, w_down[0]], axis=0)


def _unpack_rest(r):
    return (r[0:OFF_PG][None], r[OFF_GATE:OFF_UP].T[None], r[OFF_UP:OFF_DOWN].T[None], r[OFF_DOWN:ROWS_REST][None],
            r[OFF_PLE:OFF_PLE + 32].reshape(128, D_PLE).T[None], r[OFF_PG:OFF_PLE][None])


def _pack_small(w_pool, g_mix_pre, g_mix_post, g_ffn_pre, g_ffn_post, g_ple, g_attn, g_pool, pool_scale, b_forget,
                loss=None):
    row = lambda vrow: vrow.reshape(1, -1)
    misc = [row(pool_scale), row(b_forget), row(loss) if loss is not None else jnp.zeros((1, 1), F32),
            jnp.zeros((1, D_MODEL - COL_LOSS - 1), F32)]
    rows = [w_pool.reshape(64, D_MODEL), row(g_mix_pre), row(g_mix_post), row(g_ffn_pre), row(g_ffn_post), row(g_ple),
            jnp.concatenate([row(g_attn), row(g_pool)], axis=1), jnp.concatenate(misc, axis=1),
            jnp.zeros((SMALL_ROWS - ROW_MISC - 1, D_MODEL), F32)]
    return jnp.concatenate(rows, axis=0)


def _pack_small_late(g_mix_pre, b_forget):
    misc = [jnp.zeros((1, COL_B_FORGET), F32), b_forget.reshape(1, -1), jnp.zeros((1, D_MODEL - COL_LOSS), F32)]
    return jnp.concatenate([g_mix_pre.reshape(1, -1), jnp.zeros((ROW_MISC - ROW_G_MIX_PRE - 1, D_MODEL), F32),
                            jnp.concatenate(misc, axis=1), jnp.zeros((SMALL_ROWS - ROW_MISC - 1, D_MODEL), F32)], axis=0)


def _unpack_small(r):
    gains, misc = r[ROW_GROUP_GAINS:ROW_GROUP_GAINS + 1], r[ROW_MISC:ROW_MISC + 1]
    return dict(
        w_pool=r[0:64].reshape(1, 4, POOL_CH, POOL_CH), g_mix_pre=r[ROW_G_MIX_PRE:ROW_G_MIX_PRE + 1],
        g_mix_post=r[ROW_G_MIX_POST:ROW_G_MIX_POST + 1], g_ffn_pre=r[ROW_G_FFN_PRE:ROW_G_FFN_PRE + 1],
        g_ffn_post=r[ROW_G_FFN_POST:ROW_G_FFN_POST + 1], g_ple=r[ROW_G_PLE:ROW_G_PLE + 1],
        g_attn_grp=gains[:, 0:D_ATTN], g_pool_grp=gains[:, D_ATTN:D_ATTN + D_POOL],
        pool_scale=misc[:, 0:D_POOL], b_forget=misc[:, COL_B_FORGET:COL_B_FORGET + HEADS])


def _step(x, p, tgt, small, in_w, in_m, in_v, rest_w, rest_m, rest_v):
    core = lax.axis_index("c").astype(jnp.int32).reshape(1)
    win_t = _all_gather(in_w.astype(BF16), "gather_w_in")[:, 0:SHARD_IN].reshape(D_IN, D_MODEL)
    wqkv = win_t
    wf = _pad_rows(win_t[3 * D_ATTN:3 * D_ATTN + HEADS], LANES)
    wu = win_t[3 * D_ATTN + HEADS:]
    wpool = small["w_pool"].astype(BF16)
    bpad = jnp.pad(small["b_forget"], ((0, 0), (0, LANES - HEADS)))

    lay = _attn_layout_constants()
    rest_b = rest_w.astype(BF16)
    hn, qt3, ka, v, qat3, vt3, kt3, fl, y, mpre, gh = _pre_attn_fwd(x, small["g_mix_pre"], wqkv, wf, wu, bpad, wpool, lay,
                                                                 rest_b[0:OFF_GATE])
    a, lset3, gf = _attn_fwd(ka, qat3, vt3, rest_b[OFF_GATE:])
    wple_t = gh[:, OFF_PLE:OFF_PLE + 32].reshape(D_MODEL, D_PLE)
    mix, o, h1, hn2 = _post_attn_fwd(a, mpre, x, small["g_attn_grp"], small["g_pool_grp"], small["pool_scale"], gh,
                                     small["g_mix_post"], small["g_ffn_pre"])
    gate, up, act, ff, h2 = _ffn_fwd(hn2, gf, gf, gf, h1, small["g_ffn_post"])
    dh2, dff, dgl, dpp, h2b, pb, loss8, dg_ple, dg_ffn_post = _tail_fwd_bwd(
        h2, p, tgt, ff, wple_t, gh, small["g_ple"], small["g_ffn_post"])
    dgate, dup, dh1, dg_ffn_pre = _ffn_bwd(dff, gate, up, gf, gf, gf, h1, dh2, small["g_ffn_pre"])
    nd = N_DEV
    send_rest = [
        _wgrad(h2b, dgl, BF16, "wgrad_ple_gate").reshape(nd, 128, D_MODEL),
        _wgrad(dpp, pb, BF16, "wgrad_ple").reshape(nd, 32, D_MODEL),
        _wgrad(dgate, hn2, BF16, "wgrad_gate").reshape(nd, SHARD_FF, D_MODEL),
        _wgrad(dup, hn2, BF16, "wgrad_up").reshape(nd, SHARD_FF, D_MODEL),
        _wgrad(act, dff, BF16, "wgrad_down").reshape(nd, SHARD_FF, D_MODEL)]
    (dob, dat3, dlt3, dmpb, dy, dg_mix_post, dg_attn, dg_pool, dps), landed = _post_attn_bwd(
        dh1, o, a, mpre, gh, wpool, small["g_mix_post"], small["g_attn_grp"], small["g_pool_grp"], small["pool_scale"],
        send_rest)
    send_rest = [_wgrad(mix, dob, BF16, "wgrad_out").reshape(nd, 128, D_MODEL)] + send_rest
    pair_rest = _rs_pair_sum(core, send_rest, [0, OFF_PG, OFF_PLE, OFF_GATE, OFF_UP, OFF_DOWN], ROWS_REST,
                             "rs_pair_sum_rest", landed)

    dwp = _wgrad(y, dmpb, F32, "wgrad_pool")
    dw_pool = jnp.stack([dwp[g * POOL_CH:(g + 1) * POOL_CH, g * POOL_CH:(g + 1) * POOL_CH] for g in range(4)])
    small_part = _pack_small(dw_pool, jnp.zeros((1, D_MODEL), F32), dg_mix_post, dg_ffn_pre, dg_ffn_post, dg_ple,
                             dg_attn, dg_pool, dps, jnp.zeros((1, HEADS), F32), loss8[0:1, 0:1])
    dqt3, dkt3, dvt3, chips_rest, small_all = _attn_bwd(ka, v, kt3, qat3, qt3, dat3, lset3, dlt3, pair_rest, small_part)

    gx, dz, dg_mix_pre, db = _pre_attn_bwd(dqt3, dkt3, dvt3, fl, dy, x, dh1, small["g_mix_pre"], wqkv, wf, wu)

    pair_in = _rs_pair_sum(core, [_wgrad_in(dz, hn)], [0], ROWS_IN, "rs_pair_sum_in")

    small_late = _pack_small_late(dg_mix_pre, db[:, 0:HEADS])
    *upd_rest, chips_in, late_all = _reduce_update_rest(chips_rest, rest_w, rest_m, rest_v, pair_in, small_late)
    upd_in = _reduce_update_big(chips_in, in_w, in_m, in_v, ROWS_IN, "reduce_update_in")
    return gx, (small_all, late_all), upd_in, upd_rest


def kernel(x, p, g_mix_pre, w_in, b_forget, g_attn_grp, g_pool_grp, w_pool, pool_scale, w_out, g_mix_post, g_ffn_pre, w_ffn_gate, w_ffn_up, w_ffn_down, g_ffn_post, w_ple_proj, g_ple, w_ple_gate, loss_target, m_g_mix_pre, m_w_in, m_b_forget, m_g_attn_grp, m_g_pool_grp, m_w_pool, m_pool_scale, m_w_out, m_g_mix_post, m_g_ffn_pre, m_w_ffn_gate, m_w_ffn_up, m_w_ffn_down, m_g_ffn_post, m_w_ple_proj, m_g_ple, m_w_ple_gate, v_g_mix_pre, v_w_in, v_b_forget, v_g_attn_grp, v_g_pool_grp, v_w_pool, v_pool_scale, v_w_out, v_g_mix_post, v_g_ffn_pre, v_w_ffn_gate, v_w_ffn_up, v_w_ffn_down, v_g_ffn_post, v_w_ple_proj, v_g_ple, v_w_ple_gate):
    small = dict(w_pool=w_pool[0], g_mix_pre=g_mix_pre, g_mix_post=g_mix_post, g_ffn_pre=g_ffn_pre,
                 g_ffn_post=g_ffn_post, g_ple=g_ple, g_attn_grp=g_attn_grp, g_pool_grp=g_pool_grp,
                 pool_scale=pool_scale, b_forget=b_forget)
    gx, small_all, upd_in, upd_rest = _step(
        x[0], p[0, 0], loss_target[0], small, _pack_in(w_in), _pack_in(m_w_in), _pack_in(v_w_in),
        _pack_rest(w_out, w_ffn_gate, w_ffn_up, w_ffn_down, w_ple_proj, w_ple_gate),
        _pack_rest(m_w_out, m_w_ffn_gate, m_w_ffn_up, m_w_ffn_down, m_w_ple_proj, m_w_ple_gate),
        _pack_rest(v_w_out, v_w_ffn_gate, v_w_ffn_up, v_w_ffn_down, v_w_ple_proj, v_w_ple_gate))

    sm_w = _pack_small(w_pool, g_mix_pre, g_mix_post, g_ffn_pre, g_ffn_post, g_ple, g_attn_grp, g_pool_grp, pool_scale, b_forget)
    sm_m = _pack_small(m_w_pool, m_g_mix_pre, m_g_mix_post, m_g_ffn_pre, m_g_ffn_post, m_g_ple, m_g_attn_grp, m_g_pool_grp, m_pool_scale, m_b_forget)
    sm_v = _pack_small(v_w_pool, v_g_mix_pre, v_g_mix_post, v_g_ffn_pre, v_g_ffn_post, v_g_ple, v_g_attn_grp, v_g_pool_grp, v_pool_scale, v_b_forget)
    upd_small = _reduce_update_small(*small_all, sm_w, sm_m, sm_v)
    loss = upd_small[0][ROW_MISC, COL_LOSS]

    def leaves(k):
        b_out, b_gate, b_up, b_down, b_ple, b_pg = _unpack_rest(upd_rest[k])
        s = _unpack_small(upd_small[k])
        return (s["g_mix_pre"], _unpack_in(upd_in[k]), s["b_forget"], s["g_attn_grp"], s["g_pool_grp"], s["w_pool"],
                s["pool_scale"], b_out, s["g_mix_post"], s["g_ffn_pre"], b_gate, b_up, b_down, s["g_ffn_post"], b_ple,
                s["g_ple"], b_pg)

    return (loss, gx[None], *leaves(0), *leaves(1), *leaves(2), *leaves(3))
```

```python
import functools

import jax
import jax.numpy as jnp
from jax import lax
from jax.experimental import pallas as pl
from jax.experimental.pallas import tpu as pltpu

F32 = jnp.float32
BF16 = jnp.bfloat16
HIGHEST = lax.Precision.HIGHEST

D_MODEL = 1024
HEADS = 8
HEAD_DIM = 64
D_ATTN = HEADS * HEAD_DIM
POOL_WINDOWS = (2, 4, 8, 16)
POOL_CH = 128
D_POOL = POOL_CH * len(POOL_WINDOWS)
D_FF = 2816
D_PLE = 256
D_IN = 3 * D_ATTN + HEADS + D_POOL
RMS_EPS = 1e-6
N_DEV = 8

ADAM_LR = 0.001
ADAM_B1 = 0.9
ADAM_B2 = 0.999
ADAM_EPS = 1e-08
ADAM_WD = 0.01
ADAM_STEP = 10

LANES = 128
HALO = 16
TS = 512
TS_FF = 512
TS_WGRAD = 1024
TM_WGRAD = 2176
TQ = 256
TN_FF = 1408
NEG = -1e30
VMEM_LIMIT = 56 * 1024 * 1024

SHARD_IN = 257
ROWS_IN = 272
SHARD_FF = 352
SHARD_SQ = D_MODEL // N_DEV
ROWS_PLE = D_PLE * SHARD_SQ // D_MODEL
OFF_PG = SHARD_SQ
OFF_PLE = 2 * SHARD_SQ
OFF_GATE = SHARD_FF
OFF_UP = 2 * SHARD_FF
OFF_DOWN = 3 * SHARD_FF
ROWS_REST = 4 * SHARD_FF
TR_REST = SHARD_FF

SMALL_ROWS = 72
ROW_G_MIX_PRE, ROW_G_MIX_POST, ROW_G_FFN_PRE, ROW_G_FFN_POST, ROW_G_PLE = 64, 65, 66, 67, 68
ROW_GROUP_GAINS, ROW_MISC = 69, 70
COL_B_FORGET = D_POOL
COL_LOSS = D_POOL + HEADS


def _nn(a, b):
    return jnp.dot(a, b, preferred_element_type=F32)


def _nt(a, b):
    return lax.dot_general(a, b, (((1,), (1,)), ((), ())), preferred_element_type=F32)


def _tn(a, b):
    return lax.dot_general(a, b, (((0,), (0,)), ((), ())), preferred_element_type=F32)


def _rstd(v):
    return lax.rsqrt(jnp.mean(v * v, axis=-1, keepdims=True) + RMS_EPS)


def _rms_bwd(v, g, dy):
    r = _rstd(v)
    vh = v * r
    t = dy * g
    dv = r * (t - vh * jnp.mean(t * vh, axis=-1, keepdims=True))
    return dv, jnp.sum(dy * vh, axis=0, keepdims=True)


def _split3(v):
    hi = v.astype(BF16)
    rest = v - hi.astype(F32)
    mid = rest.astype(BF16)
    return hi, mid, (rest - mid.astype(F32)).astype(BF16)


def _mask_matmul(mask, v):
    hi, mid, lo = _split3(v)
    return _nn(mask, lo) + _nn(mask, mid) + _nn(mask, hi)


def _params(n_grid):
    return pltpu.CompilerParams(dimension_semantics=("arbitrary",) * n_grid, vmem_limit_bytes=VMEM_LIMIT)


def _row(i):
    return (i, 0)


def _fixed(*_):
    return (0, 0)


def _spec_square(part):
    return pl.BlockSpec((N_DEV, SHARD_SQ, D_MODEL), lambda *_: (0, part, 0))


def _spec_ff(part):
    return pl.BlockSpec((TN_FF // SHARD_FF, SHARD_FF, D_MODEL), lambda i, j: (j, part, 0))


assert TS == 2 * TQ and TN_FF % SHARD_FF == 0
_HALVES = (slice(0, TQ), slice(TQ, TS))

VMEM_WHOLE = pl.BlockSpec(memory_space=pltpu.VMEM)
SMEM_WHOLE = pl.BlockSpec(memory_space=pltpu.SMEM)
ANY = pl.BlockSpec(memory_space=pl.ANY)


LOG2E = 1.4426950408889634
VROWS = HEAD_DIM + 16
AUG = 128
BIAS_LANE = HEAD_DIM
ONE_LANE = HEAD_DIM + 3
SPARE_LANE = HEADS
PART_LANES = 16
assert SPARE_LANE < PART_LANES and 3 * PART_LANES <= LANES


def _attn_layout_constants():
    import numpy as np
    bias_k = np.zeros((LANES, HEADS * AUG), np.float32)
    bias_q = np.zeros((LANES, HEADS * AUG), np.float32)
    for h in range(HEADS):
        for part in range(3):
            bias_k[part * PART_LANES + h, h * AUG + BIAS_LANE + part] = -1.0
            bias_q[part * PART_LANES + h, h * AUG + ONE_LANE + part] = 1.0
            bias_k[SPARE_LANE, h * AUG + ONE_LANE + part] = 1.0
            bias_q[SPARE_LANE, h * AUG + BIAS_LANE + part] = 1.0
    after = np.concatenate([np.arange(h * AUG + HEAD_DIM, (h + 1) * AUG) for h in range(HEADS)])
    as_bf = lambda a: jnp.asarray(a, BF16)
    return dict(bias_k=as_bf(bias_k[:, after]), bias_q_t=as_bf(bias_q[:, after].T))


def _pre_attn_fwd(x, g1, wqkv, wf, wu, bpad, wpool, lay, own_block):
    s, d = x.shape
    nt = s // TS
    sub = TS // TQ

    def body(x_ref, g_ref, wqkv_ref, wf_ref, wu_ref, b_ref, wp_ref, bk_ref, bqt_ref, own_ref,
             hn_ref, qt_ref, ka_ref, v_ref, qat_ref, vt_ref, kt_ref, fl_ref, y_ref, mp_ref, all_ref,
             ubuf, ccar, cbuf, stage, send_sems, recv_sems, local_sem):
        i = pl.program_id(0)

        @pl.when(i == 0)
        def _():
            _gather_start(own_ref, all_ref, stage, send_sems, recv_sems, local_sem)
            ubuf[0:HALO, :] = jnp.zeros((HALO, D_POOL), F32)
            ccar[...] = jnp.zeros_like(ccar)

        @pl.when(i == max(nt - 2, 0))
        def _():
            _gather_pass_on(all_ref, send_sems, recv_sems)

        xv = x_ref[...]
        hn = (xv * _rstd(xv) * g_ref[...]).astype(BF16)
        hn_ref[...] = hn
        zq = _nt(hn, wqkv_ref[...])
        qt = (zq[:, 0:D_ATTN] * 0.125).astype(BF16).T
        qb = (zq[:, 0:D_ATTN] * (0.125 * LOG2E)).astype(BF16)
        kb = zq[:, D_ATTN:2 * D_ATTN].astype(BF16)
        vb = zq[:, 2 * D_ATTN:3 * D_ATTN].astype(BF16)
        v_ref[...] = vb

        fl = _nt(hn, wf_ref[...]) + b_ref[...]
        fl_ref[...] = fl
        logf = jax.nn.log_sigmoid(fl)
        rr = lax.broadcasted_iota(jnp.int32, (TS, TS), 0)
        cc = lax.broadcasted_iota(jnp.int32, (TS, TS), 1)
        c = _mask_matmul((cc <= rr).astype(BF16), logf) + ccar[...]
        cbuf[...] = c
        ccar[...] = cbuf[TS - 1:TS, :]
        hi, mid, lo = (part.astype(F32) for part in _split3(c * LOG2E))
        lane = lax.broadcasted_iota(jnp.int32, (TS, LANES), 1)
        later = jnp.where(lane < 2 * PART_LANES, pltpu.roll(mid, PART_LANES, 1), pltpu.roll(lo, 2 * PART_LANES, 1))
        parts = jnp.where(lane < PART_LANES, jnp.where(lane == SPARE_LANE, 1.0, hi), later).astype(BF16)
        extra = AUG - HEAD_DIM
        kbias = _nn(parts, bk_ref[...]).astype(BF16)
        for h in range(HEADS):
            ka_ref[:, h * AUG:h * AUG + HEAD_DIM] = kb[:, h * HEAD_DIM:(h + 1) * HEAD_DIM]
            ka_ref[:, h * AUG + HEAD_DIM:(h + 1) * AUG] = kbias[:, h * extra:(h + 1) * extra]
        qbt = qb.T
        qbias = _nt(bqt_ref[...], parts).astype(BF16)
        vt = vb.T
        kt = kb.T
        for a in range(sub):
            cols = slice(a * TQ, (a + 1) * TQ)
            for h in range(HEADS):
                qat_ref[a, h * AUG:h * AUG + HEAD_DIM, :] = qbt[h * HEAD_DIM:(h + 1) * HEAD_DIM, cols]
                qat_ref[a, h * AUG + HEAD_DIM:(h + 1) * AUG, :] = qbias[h * extra:(h + 1) * extra, cols]
            for ref, mat in ((qt_ref, qt), (kt_ref, kt), (vt_ref, vt)):
                for h in range(HEADS):
                    ref[a, h * VROWS:h * VROWS + HEAD_DIM, :] = mat[h * HEAD_DIM:(h + 1) * HEAD_DIM, cols]
                    ref[a, h * VROWS + HEAD_DIM:(h + 1) * VROWS, :] = jnp.ones((VROWS - HEAD_DIM, TQ), BF16)

        u = _nt(hn, wu_ref[...])
        ubuf[HALO:HALO + TS, :] = u
        t = i * TS + lax.broadcasted_iota(jnp.int32, (TS, 1), 0)
        for g, w in enumerate(POOL_WINDOWS):
            cols = slice(g * POOL_CH, (g + 1) * POOL_CH)
            sm = ubuf[:, cols]
            step = 1
            while step < w:
                sm = sm + pltpu.roll(sm, step, 0)
                step *= 2
            cnt = jnp.minimum(t + 1, w).astype(F32)
            yg = (sm[HALO:, :] / cnt - u[:, cols]).astype(BF16)
            y_ref[:, cols] = yg
            mp_ref[:, cols] = _nn(yg, wp_ref[g])
        ubuf[0:HALO, :] = u[TS - HALO:, :]

        @pl.when(i == nt - 1)
        def _():
            _gather_finish(own_ref, all_ref, send_sems, recv_sems)

    nq = s // TQ
    aug = HEADS * AUG
    outs = (
        jax.ShapeDtypeStruct((s, d), BF16), jax.ShapeDtypeStruct((nq, HEADS * VROWS, TQ), BF16),
        jax.ShapeDtypeStruct((s, aug), BF16), jax.ShapeDtypeStruct((s, D_ATTN), BF16),
        jax.ShapeDtypeStruct((nq, aug, TQ), BF16), jax.ShapeDtypeStruct((nq, HEADS * VROWS, TQ), BF16),
        jax.ShapeDtypeStruct((nq, HEADS * VROWS, TQ), BF16),
        jax.ShapeDtypeStruct((s, LANES), F32),
        jax.ShapeDtypeStruct((s, D_POOL), BF16), jax.ShapeDtypeStruct((s, D_POOL), F32),
        jax.ShapeDtypeStruct((N_DEV,) + own_block.shape, own_block.dtype),
    )
    fixed3 = lambda i: (0, 0, 0)
    tiles3 = lambda rows: pl.BlockSpec((sub, rows, TQ), lambda i: (i, 0, 0))
    return pl.pallas_call(
        body, grid=(nt,), out_shape=outs, name="pre_attn_fwd",
        in_specs=[pl.BlockSpec((TS, d), _row), pl.BlockSpec((1, d), _fixed),
                  pl.BlockSpec((3 * D_ATTN, d), _fixed), pl.BlockSpec(wf.shape, _fixed), pl.BlockSpec(wu.shape, _fixed),
                  pl.BlockSpec((1, LANES), _fixed), pl.BlockSpec(wpool.shape, fixed3),
                  pl.BlockSpec(lay["bias_k"].shape, _fixed), pl.BlockSpec(lay["bias_q_t"].shape, _fixed), ANY],
        out_specs=(pl.BlockSpec((TS, d), _row), tiles3(HEADS * VROWS),
                   pl.BlockSpec((TS, aug), _row), pl.BlockSpec((TS, D_ATTN), _row),
                   tiles3(aug), tiles3(HEADS * VROWS), tiles3(HEADS * VROWS),
                   pl.BlockSpec((TS, LANES), _row),
                   pl.BlockSpec((TS, D_POOL), _row), pl.BlockSpec((TS, D_POOL), _row), ANY),
        scratch_shapes=[pltpu.VMEM((TS + HALO, D_POOL), F32), pltpu.VMEM((1, LANES), F32), pltpu.VMEM((TS, LANES), F32),
                        pltpu.VMEM(own_block.shape, own_block.dtype),
                        pltpu.SemaphoreType.DMA((7,)), pltpu.SemaphoreType.DMA((7,)), pltpu.SemaphoreType.DMA],
        compiler_params=_params(1),
    )(x, g1, wqkv, wf, wu, bpad, wpool, lay["bias_k"], lay["bias_q_t"], own_block)


def _causal_in_tile():
    krow = lax.broadcasted_iota(jnp.int32, (TQ, TQ), 0)
    qcol = lax.broadcasted_iota(jnp.int32, (TQ, TQ), 1)
    return krow <= qcol


def _attn_fwd(ka, qat3, vt3, own_block):
    s = ka.shape[0]
    nq = s // TQ
    pass_on_step = max(nq - 2, 0)

    def body(qa_ref, ka_ref, vt_ref, own_ref, a_ref, lset_ref, all_ref, acc, out_t, st_scr, pt_scr,
             stage, send_sems, recv_sems, local_sem):
        i = pl.program_id(0)

        @pl.when(i == 0)
        def _():
            _gather_start(own_ref, all_ref, stage, send_sems, recv_sems, local_sem)

        @pl.when(i == pass_on_step)
        def _():
            _gather_pass_on(all_ref, send_sems, recv_sems)

        acc[...] = jnp.zeros_like(acc)

        def tile(j, stats, masked):
            tile_max = []
            for h in range(HEADS):
                aug = slice(h * AUG, (h + 1) * AUG)
                st = _nn(ka_ref[pl.ds(j * TQ, TQ), aug], qa_ref[0, aug, :])
                if masked:
                    st = jnp.where(_causal_in_tile(), st, NEG)
                st_scr[h] = st
                tile_max.append(jnp.max(st, axis=0, keepdims=True))
            new, scale = [], []
            for h in range(HEADS):
                m_new = jnp.maximum(stats[h], tile_max[h])
                scale.append(jnp.exp2(stats[h] - m_new))
                pt_scr[h] = jnp.exp2(st_scr[h] - m_new).astype(BF16)
                new.append(m_new)
            for h in range(HEADS):
                rows = slice(h * VROWS, (h + 1) * VROWS)
                acc[rows, :] = scale[h] * acc[rows, :] + _nn(vt_ref[j, rows, :], pt_scr[h])
            return tuple(new)

        init = tuple(jnp.full((1, TQ), NEG, F32) for _ in range(HEADS))
        stats = lax.fori_loop(0, i, functools.partial(tile, masked=False), init)
        stats = tile(i, stats, True)
        for h in range(HEADS):
            denom = acc[h * VROWS + HEAD_DIM:h * VROWS + HEAD_DIM + 1, :]
            out_t[h * HEAD_DIM:(h + 1) * HEAD_DIM, :] = acc[h * VROWS:h * VROWS + HEAD_DIM, :] / denom
            lset_ref[0, h:h + 1, :] = stats[h] + jnp.log2(denom)
        a_ref[...] = out_t[...].T

        @pl.when(i == nq - 1)
        def _():
            _gather_finish(own_ref, all_ref, send_sems, recv_sems)

    r, cdim = own_block.shape
    return pl.pallas_call(
        body, grid=(nq,), name="attn_fwd",
        out_shape=(jax.ShapeDtypeStruct((s, D_ATTN), F32), jax.ShapeDtypeStruct((nq, HEADS, TQ), F32),
                   jax.ShapeDtypeStruct((N_DEV, r, cdim), own_block.dtype)),
        in_specs=[pl.BlockSpec((1, HEADS * AUG, TQ), lambda i: (i, 0, 0)), VMEM_WHOLE, VMEM_WHOLE, ANY],
        out_specs=(pl.BlockSpec((TQ, D_ATTN), _row), pl.BlockSpec((1, HEADS, TQ), lambda i: (i, 0, 0)), ANY),
        scratch_shapes=[pltpu.VMEM((HEADS * VROWS, TQ), F32), pltpu.VMEM((D_ATTN, TQ), F32),
                        pltpu.VMEM((HEADS, TQ, TQ), F32), pltpu.VMEM((HEADS, TQ, TQ), BF16),
                        pltpu.VMEM((r, cdim), own_block.dtype),
                        pltpu.SemaphoreType.DMA((7,)), pltpu.SemaphoreType.DMA((7,)), pltpu.SemaphoreType.DMA],
        compiler_params=_params(1),
    )(qat3, ka, vt3, own_block)


def _post_attn_fwd(a, mpre, x, g_attn, g_pool, pscale, wout, g_post, g_ffn_pre):
    s, d = x.shape

    def body(a_ref, mp_ref, x_ref, ga_ref, gp_ref, ps_ref, wo_ref, gpost_ref, gpre_ref,
             mix_ref, o_ref, h1_ref, hn2_ref):
        for rows in _HALVES:
            av = a_ref[rows, :]
            mix_ref[rows, 0:D_ATTN] = (av * _rstd(av) * ga_ref[...]).astype(BF16)
            mv = mp_ref[rows, :] * ps_ref[...]
            mix_ref[rows, D_ATTN:] = (mv * _rstd(mv) * gp_ref[...]).astype(BF16)
            o = _nn(mix_ref[rows, :], wo_ref[...].reshape(d, d))
            o_ref[rows, :] = o
            h1 = x_ref[rows, :] + o * _rstd(o) * gpost_ref[...]
            h1_ref[rows, :] = h1
            hn2_ref[rows, :] = (h1 * _rstd(h1) * gpre_ref[...]).astype(BF16)

    vec = lambda n: pl.BlockSpec((1, n), _fixed)
    return pl.pallas_call(
        body, grid=(s // TS,), name="post_attn_fwd",
        out_shape=(jax.ShapeDtypeStruct((s, d), BF16), jax.ShapeDtypeStruct((s, d), F32),
                   jax.ShapeDtypeStruct((s, d), F32), jax.ShapeDtypeStruct((s, d), BF16)),
        in_specs=[pl.BlockSpec((TS, D_ATTN), _row), pl.BlockSpec((TS, D_POOL), _row), pl.BlockSpec((TS, d), _row),
                  vec(D_ATTN), vec(D_POOL), vec(D_POOL), _spec_square(0), vec(d), vec(d)],
        out_specs=(pl.BlockSpec((TS, d), _row),) * 4,
        compiler_params=_params(1),
    )(a, mpre, x, g_attn, g_pool, pscale, wout, g_post, g_ffn_pre)


def _ffn_fwd(hn2, wg, wu, wd, h1, g_post):
    s, d = h1.shape
    nc = D_FF // TN_FF
    ts = min(TS_FF, s)

    def body(hn_ref, wg_ref, wu_ref, wd_ref, h1_ref, g_ref, gate_ref, up_ref, act_ref, ff_ref, h2_ref, acc):
        j = pl.program_id(1)

        @pl.when(j == 0)
        def _():
            acc[...] = jnp.zeros_like(acc)

        for r in range(2):
            rows = slice(r * (ts // 2), (r + 1) * (ts // 2))
            hn = hn_ref[rows, :]
            gt = _nt(hn, wg_ref[...].reshape(TN_FF, d))
            up = _nt(hn, wu_ref[...].reshape(TN_FF, d))
            gate_ref[rows, :] = gt.astype(BF16)
            up_ref[rows, :] = up.astype(BF16)
            act_ref[rows, :] = (gt * jax.nn.sigmoid(gt) * up).astype(BF16)
            acc[rows, :] += _nn(act_ref[rows, :], wd_ref[...].reshape(TN_FF, d))

        @pl.when(j == nc - 1)
        def _():
            ff = acc[...]
            ff_ref[...] = ff
            h2_ref[...] = h1_ref[...] + ff * _rstd(ff) * g_ref[...]

    rowblk = pl.BlockSpec((ts, d), lambda i, j: (i, 0))
    chunk = pl.BlockSpec((ts, TN_FF), lambda i, j: (i, j))
    return pl.pallas_call(
        body, grid=(s // ts, nc), name="ffn_fwd",
        out_shape=(jax.ShapeDtypeStruct((s, D_FF), BF16),) * 3 + (jax.ShapeDtypeStruct((s, d), F32),) * 2,
        in_specs=[rowblk, _spec_ff(0), _spec_ff(1), _spec_ff(2), rowblk, pl.BlockSpec((1, d), lambda i, j: (0, 0))],
        out_specs=(chunk, chunk, chunk, rowblk, rowblk),
        scratch_shapes=[pltpu.VMEM((ts, d), F32)],
        compiler_params=_params(2),
    )(hn2, wg, wu, wd, h1, g_post)


def _tail_fwd_bwd(h2, p, tgt, ff, wple, wpg, g_ple, g_ffn_post):
    s, d = h2.shape

    def body(h2_ref, p_ref, t_ref, ff_ref, wple_ref, wpg_ref, gple_ref, gfp_ref,
             dh2_ref, dff_ref, dgl_ref, dpp_ref, h2b_ref, pb_ref, loss_ref, dgple_ref, dgfp_ref):
        i = pl.program_id(0)

        @pl.when(i == 0)
        def _():
            loss_ref[...] = jnp.zeros_like(loss_ref)
            dgple_ref[...] = jnp.zeros_like(dgple_ref)
            dgfp_ref[...] = jnp.zeros_like(dgfp_ref)

        h2 = h2_ref[...]
        h2b = h2.astype(BF16)
        h2b_ref[...] = h2b
        pb = p_ref[...].astype(BF16)
        pb_ref[...] = pb
        pp = _nt(pb, wple_ref[...])
        gple = gple_ref[...]
        e = pp * _rstd(pp) * gple
        wpg = wpg_ref[...].reshape(d, d)
        sg = jax.nn.sigmoid(_nn(h2b, wpg))
        diff = h2 + sg * e - t_ref[...]
        sq = jnp.sum(jnp.sum(diff * diff, axis=1, keepdims=True), axis=0, keepdims=True)
        loss_ref[...] += jnp.broadcast_to(sq * (0.5 / d), loss_ref.shape)
        dh3 = diff * (1.0 / d)
        dgl = (dh3 * e * sg * (1.0 - sg)).astype(BF16)
        dgl_ref[...] = dgl
        dh2 = dh3 + _nt(dgl, wpg)
        dh2_ref[...] = dh2
        dpp, dg = _rms_bwd(pp, gple, dh3 * sg)
        dpp_ref[...] = dpp.astype(BF16)
        dgple_ref[...] += dg
        dff, dg = _rms_bwd(ff_ref[...], gfp_ref[...], dh2)
        dff_ref[...] = dff.astype(BF16)
        dgfp_ref[...] += dg

    rowblk = pl.BlockSpec((TS, d), _row)
    vec = pl.BlockSpec((1, d), _fixed)
    return pl.pallas_call(
        body, grid=(s // TS,), name="tail_fwd_bwd",
        out_shape=(jax.ShapeDtypeStruct((s, d), F32), jax.ShapeDtypeStruct((s, d), BF16),
                   jax.ShapeDtypeStruct((s, d), BF16), jax.ShapeDtypeStruct((s, d), BF16),
                   jax.ShapeDtypeStruct((s, d), BF16), jax.ShapeDtypeStruct((s, D_PLE), BF16),
                   jax.ShapeDtypeStruct((8, LANES), F32), jax.ShapeDtypeStruct((1, d), F32),
                   jax.ShapeDtypeStruct((1, d), F32)),
        in_specs=[rowblk, pl.BlockSpec((TS, D_PLE), _row), rowblk, rowblk,
                  pl.BlockSpec(wple.shape, _fixed), _spec_square(1), vec, vec],
        out_specs=(rowblk, rowblk, rowblk, rowblk, rowblk, pl.BlockSpec((TS, D_PLE), _row),
                   pl.BlockSpec((8, LANES), _fixed), vec, vec),
        compiler_params=_params(1),
    )(h2, p, tgt, ff, wple, wpg, g_ple, g_ffn_post)


def _ffn_bwd(dff, gate, up, wd, wg, wu, h1, dh2, g_pre):
    s, d = h1.shape
    nc = D_FF // TN_FF
    ts = min(TS_FF, s)

    def body(dff_ref, gate_ref, up_ref, wd_ref, wg_ref, wu_ref, h1_ref, dh2_ref, g_ref,
             dgate_ref, dup_ref, dh1_ref, dg_ref, acc):
        i = pl.program_id(0)
        j = pl.program_id(1)

        @pl.when((i == 0) & (j == 0))
        def _():
            dg_ref[...] = jnp.zeros_like(dg_ref)

        @pl.when(j == 0)
        def _():
            acc[...] = jnp.zeros_like(acc)

        for r in range(2):
            rows = slice(r * (ts // 2), (r + 1) * (ts // 2))
            dact = _nt(dff_ref[rows, :], wd_ref[...].reshape(TN_FF, d))
            gt = gate_ref[rows, :].astype(F32)
            sg = jax.nn.sigmoid(gt)
            dup_ref[rows, :] = (dact * gt * sg).astype(BF16)
            dgate_ref[rows, :] = (dact * up_ref[rows, :].astype(F32) * (sg * (1.0 + gt * (1.0 - sg)))).astype(BF16)
            acc[rows, :] += (_nn(dgate_ref[rows, :], wg_ref[...].reshape(TN_FF, d))
                             + _nn(dup_ref[rows, :], wu_ref[...].reshape(TN_FF, d)))

        @pl.when(j == nc - 1)
        def _():
            dv, dg = _rms_bwd(h1_ref[...], g_ref[...], acc[...])
            dh1_ref[...] = dh2_ref[...] + dv
            dg_ref[...] += dg

    rowblk = pl.BlockSpec((ts, d), lambda i, j: (i, 0))
    chunk = pl.BlockSpec((ts, TN_FF), lambda i, j: (i, j))
    vec = pl.BlockSpec((1, d), lambda i, j: (0, 0))
    return pl.pallas_call(
        body, grid=(s // ts, nc), name="ffn_bwd",
        out_shape=(jax.ShapeDtypeStruct((s, D_FF), BF16), jax.ShapeDtypeStruct((s, D_FF), BF16),
                   jax.ShapeDtypeStruct((s, d), F32), jax.ShapeDtypeStruct((1, d), F32)),
        in_specs=[rowblk, chunk, chunk, _spec_ff(2), _spec_ff(0), _spec_ff(1), rowblk, rowblk, vec],
        out_specs=(chunk, chunk, rowblk, vec),
        scratch_shapes=[pltpu.VMEM((ts, d), F32)],
        compiler_params=_params(2),
    )(dff, gate, up, wd, wg, wu, h1, dh2, g_pre)


def _post_attn_bwd(dh1, o, a, mpre, wout, wpool, g_post, g_attn, g_pool, pscale, send):
    s, d = dh1.shape
    sub = TS // TQ
    npc = len(send)

    def body(dh1_ref, o_ref, a_ref, mp_ref, wo_ref, wp_ref, gpost_ref, ga_ref, gp_ref, ps_ref, *refs):
        send_refs, refs = refs[:npc], refs[npc:]
        dob_ref, dat_ref, dlt_ref, dmpb_ref, dy_ref, dgpost_ref, dga_ref, dgp_ref, dps_ref = refs[:9]
        got_refs, (send_sems, recv_sems) = refs[9:9 + npc], refs[9 + npc:]
        i = pl.program_id(0)

        @pl.when(i == 0)
        def _():
            for cp in _pair_copies(send_refs, got_refs, send_sems, recv_sems):
                cp.start()
            dgpost_ref[...] = jnp.zeros_like(dgpost_ref)
            dga_ref[...] = jnp.zeros_like(dga_ref)
            dgp_ref[...] = jnp.zeros_like(dgp_ref)
            dps_ref[...] = jnp.zeros_like(dps_ref)

        do, dg = _rms_bwd(o_ref[...], gpost_ref[...], dh1_ref[...])
        dgpost_ref[...] += dg
        dob = do.astype(BF16)
        dob_ref[...] = dob
        dmix = _nt(dob, wo_ref[...].reshape(d, d))

        av = a_ref[...]
        da, dg = _rms_bwd(av, ga_ref[...], dmix[:, 0:D_ATTN])
        dga_ref[...] += dg
        dat = da.astype(BF16).T
        hsel = (lax.shift_right_logical(lax.broadcasted_iota(jnp.int32, (HEADS, D_ATTN), 1), 6)
                == lax.broadcasted_iota(jnp.int32, (HEADS, D_ATTN), 0)).astype(F32)
        dlt = lax.dot_general(hsel, da * av, (((1,), (1,)), ((), ())), precision=HIGHEST, preferred_element_type=F32)
        for q in range(sub):
            dlt_ref[q] = dlt[:, q * TQ:(q + 1) * TQ]
            dat_ref[q] = dat[:, q * TQ:(q + 1) * TQ]

        ps = ps_ref[...]
        mp = mp_ref[...]
        dm, dg = _rms_bwd(mp * ps, gp_ref[...], dmix[:, D_ATTN:])
        dgp_ref[...] += dg
        dps_ref[...] += jnp.sum(dm * mp, axis=0, keepdims=True)
        dmpb = (dm * ps).astype(BF16)
        dmpb_ref[...] = dmpb
        for g in range(len(POOL_WINDOWS)):
            cols = slice(g * POOL_CH, (g + 1) * POOL_CH)
            dy_ref[:, cols] = _nt(dmpb[:, cols], wp_ref[g])

        @pl.when(i == s // TS - 1)
        def _():
            for cp in _pair_copies(send_refs, got_refs, send_sems, recv_sems):
                cp.wait()

    rowblk = pl.BlockSpec((TS, d), _row)
    half = pl.BlockSpec((TS, D_ATTN), _row)
    vec = lambda n: pl.BlockSpec((1, n), _fixed)
    nk = N_DEV // 2
    res = pl.pallas_call(
        body, grid=(s // TS,), name="post_attn_bwd",
        out_shape=(jax.ShapeDtypeStruct((s, d), BF16), jax.ShapeDtypeStruct((s // TQ, D_ATTN, TQ), BF16),
                   jax.ShapeDtypeStruct((s // TQ, HEADS, TQ), F32), jax.ShapeDtypeStruct((s, D_POOL), BF16),
                   jax.ShapeDtypeStruct((s, D_POOL), F32), jax.ShapeDtypeStruct((1, d), F32),
                   jax.ShapeDtypeStruct((1, D_ATTN), F32), jax.ShapeDtypeStruct((1, D_POOL), F32),
                   jax.ShapeDtypeStruct((1, D_POOL), F32))
        + tuple(jax.ShapeDtypeStruct((nk,) + t.shape[1:], t.dtype) for t in send),
        in_specs=[rowblk, rowblk, half, half, _spec_square(0),
                  pl.BlockSpec(wpool.shape, lambda i: (0, 0, 0)), vec(d), vec(D_ATTN), vec(D_POOL), vec(D_POOL)]
        + [ANY] * npc,
        out_specs=(rowblk, pl.BlockSpec((sub, D_ATTN, TQ), lambda i: (i, 0, 0)),
                   pl.BlockSpec((sub, HEADS, TQ), lambda i: (i, 0, 0)), half, half,
                   vec(d), vec(D_ATTN), vec(D_POOL), vec(D_POOL)) + (ANY,) * npc,
        scratch_shapes=[pltpu.SemaphoreType.DMA((nk, npc)), pltpu.SemaphoreType.DMA((nk, npc))],
        compiler_params=_params(1),
    )(dh1, o, a, mpre, wout, wpool, g_post, g_attn, g_pool, pscale, *send)
    return res[:9], list(res[9:])


def _attn_bwd(ka, v, kt3, qat3, qt3, dot3, lset3, dlt3, chip_blocks, small_block):
    s = ka.shape[0]
    nq = s // TQ

    def body(ka_ref, v_ref, kt_ref, qat_ref, qt_ref, dot_ref, lset_ref, dlt_ref, b_ref, sm_ref,
             dqt_ref, dkt_ref, dvt_ref, got_ref, all_ref, pt_scr, ptb_scr, dsb_scr,
             stage, send_sems, recv_sems, local_sem, stage_s, send_s, recv_s, local_s):
        j = pl.program_id(0)

        @pl.when(j == 0)
        def _():
            _chips_start(b_ref, got_ref, stage, send_sems, recv_sems, local_sem)
            _gather_start(sm_ref, all_ref, stage_s, send_s, recv_s, local_s)
            dqt_ref[...] = jnp.zeros_like(dqt_ref)

        @pl.when(j == max(nq - 2, 0))
        def _():
            _gather_pass_on(all_ref, send_s, recv_s)

        def tile(i, masked):
            def accumulate(ref, idx, val):
                if masked:
                    ref[idx] = val
                else:
                    ref[idx] += val

            for h in range(HEADS):
                aug = slice(h * AUG, (h + 1) * AUG)
                st = _nn(ka_ref[:, aug], qat_ref[i, aug, :]) - lset_ref[i, h:h + 1, :]
                if masked:
                    st = jnp.where(_causal_in_tile(), st, NEG)
                pt = jnp.exp2(st)
                pt_scr[h] = pt
                ptb_scr[h] = pt.astype(BF16)
            heads = [(h, slice(h * HEAD_DIM, (h + 1) * HEAD_DIM)) for h in range(HEADS)]
            for h, hs in heads:
                dst = pt_scr[h] * (_nn(v_ref[:, hs], dot_ref[i, hs, :]) - dlt_ref[i, h:h + 1, :])
                dsb_scr[h] = dst.astype(BF16)
            for h, hs in heads:
                accumulate(dvt_ref, (0, hs, slice(None)), _nt(dot_ref[i, hs, :], ptb_scr[h]))
            for h, hs in heads:
                rows = slice(h * VROWS, (h + 1) * VROWS)
                accumulate(dkt_ref, (0, rows, slice(None)), _nt(qt_ref[i, rows, :], dsb_scr[h]))
            for h, hs in heads:
                rows = slice(h * VROWS, (h + 1) * VROWS)
                dqt_ref[i, rows, :] += _nn(kt_ref[0, rows, :], dsb_scr[h])

        first = j + 1
        pairs = (nq - first) // 2

        def step(p, carry):
            tile(first + 2 * p, False)
            tile(first + 2 * p + 1, False)
            return carry

        tile(j, True)
        lax.fori_loop(0, pairs, step, 0)

        @pl.when(first + 2 * pairs < nq)
        def _():
            tile(nq - 1, False)

        @pl.when(j == nq - 1)
        def _():
            _chips_finish(b_ref, got_ref, send_sems, recv_sems)
            _gather_finish(sm_ref, all_ref, send_s, recv_s)

    blk = pl.BlockSpec((TQ, D_ATTN), _row)
    tile_t = lambda rows: pl.BlockSpec((1, rows, TQ), lambda j: (j, 0, 0))
    per_tile = lambda rows: jax.ShapeDtypeStruct((nq, rows, TQ), F32)
    _, r, cdim = chip_blocks.shape
    dma = pltpu.SemaphoreType.DMA
    return pl.pallas_call(
        body, grid=(nq,), name="attn_bwd",
        out_shape=(per_tile(HEADS * VROWS), per_tile(HEADS * VROWS), per_tile(D_ATTN),
                   jax.ShapeDtypeStruct(chip_blocks.shape, chip_blocks.dtype),
                   jax.ShapeDtypeStruct((N_DEV,) + small_block.shape, small_block.dtype)),
        in_specs=[pl.BlockSpec((TQ, HEADS * AUG), _row), blk, tile_t(HEADS * VROWS),
                  VMEM_WHOLE, VMEM_WHOLE, VMEM_WHOLE, VMEM_WHOLE, VMEM_WHOLE, ANY, ANY],
        out_specs=(pl.BlockSpec((nq, HEADS * VROWS, TQ), lambda j: (0, 0, 0)), tile_t(HEADS * VROWS), tile_t(D_ATTN),
                   ANY, ANY),
        scratch_shapes=[pltpu.VMEM((HEADS, TQ, TQ), F32), pltpu.VMEM((HEADS, TQ, TQ), BF16),
                        pltpu.VMEM((HEADS, TQ, TQ), BF16), pltpu.VMEM((r, cdim), chip_blocks.dtype),
                        dma((3,)), dma((3,)), dma,
                        pltpu.VMEM(small_block.shape, small_block.dtype), dma((7,)), dma((7,)), dma],
        compiler_params=_params(1),
    )(ka, v, kt3, qat3, qt3, dot3, lset3, dlt3, chip_blocks, small_block)


def _pre_attn_bwd(dqt3, dkt3, dvt3, fl, dy, x, dh1, g1, wqkv, wf, wu):
    s, d = x.shape
    nt = s // TS
    n = TS + HALO
    sub = TS // TQ
    qkv, fcols = 3 * D_ATTN, 3 * D_ATTN + LANES

    def body(dqt_ref, dkt_ref, dvt_ref, fl_ref, dy_ref, x_ref, dh1_ref, g_ref, wqkv_ref, wf_ref, wu_ref,
             gx_ref, dz_ref, dg_ref, db_ref, ybuf, ccar, dlog, dsum):
        dqkv_ref = dz_ref.at[:, 0:qkv]
        dfb_ref = dz_ref.at[:, qkv:fcols]
        dub_ref = dz_ref.at[:, fcols:]
        i = pl.program_id(0)
        ti = nt - 1 - i

        @pl.when(i == 0)
        def _():
            ybuf[TS:n, :] = jnp.zeros((HALO, D_POOL), F32)
            ccar[...] = jnp.zeros_like(ccar)
            dg_ref[...] = jnp.zeros_like(dg_ref)
            db_ref[...] = jnp.zeros_like(db_ref)
            dsum[...] = jnp.zeros_like(dsum)

        for a in range(sub):
            for h in range(HEADS):
                r = h * VROWS + HEAD_DIM
                dsum[h:h + 1, a * TQ:(a + 1) * TQ] = dqt_ref[a, r:r + 1, :] - dkt_ref[a, r:r + 1, :]
        rr = lax.broadcasted_iota(jnp.int32, (TS, TS), 0)
        cc = lax.broadcasted_iota(jnp.int32, (TS, TS), 1)
        dlog[...] = ccar[...] + _mask_matmul((cc >= rr).astype(BF16), dsum[...].T)
        ccar[...] = dlog[0:1, :]
        df = dlog[...] * jax.nn.sigmoid(-fl_ref[...])
        db_ref[...] += jnp.sum(df, axis=0, keepdims=True)
        dfb = df.astype(BF16)
        dfb_ref[...] = dfb

        t = ti * TS + lax.broadcasted_iota(jnp.int32, (TS, 1), 0)
        dy = dy_ref[...]
        for g, w in enumerate(POOL_WINDOWS):
            cols = slice(g * POOL_CH, (g + 1) * POOL_CH)
            ybuf[0:TS, cols] = dy[:, cols] / jnp.minimum(t + 1, w).astype(F32)
        for g, w in enumerate(POOL_WINDOWS):
            cols = slice(g * POOL_CH, (g + 1) * POOL_CH)
            sm = ybuf[:, cols]
            step = 1
            while step < w:
                sm = sm + pltpu.roll(sm, n - step, 0)
                step *= 2
            dub_ref[:, cols] = (sm[0:TS, :] - dy[:, cols]).astype(BF16)
        ybuf[TS:n, :] = ybuf[0:HALO, :]

        for a in range(sub):
            rows = slice(a * TQ, (a + 1) * TQ)
            for h in range(HEADS):
                src = slice(h * VROWS, h * VROWS + HEAD_DIM)
                dqkv_ref[rows, h * HEAD_DIM:(h + 1) * HEAD_DIM] = (dqt_ref[a, src, :].T * 0.125).astype(BF16)
                dqkv_ref[rows, D_ATTN + h * HEAD_DIM:D_ATTN + (h + 1) * HEAD_DIM] = dkt_ref[a, src, :].T.astype(BF16)
            dqkv_ref[rows, 2 * D_ATTN:] = dvt_ref[a].T.astype(BF16)
        dhn = _nn(dqkv_ref[...], wqkv_ref[...]) + _nn(dfb, wf_ref[...]) + _nn(dub_ref[...], wu_ref[...])
        dx, dg = _rms_bwd(x_ref[...], g_ref[...], dhn)
        gx_ref[...] = dh1_ref[...] + dx
        dg_ref[...] += dg

    rev = lambda i: (nt - 1 - i, 0)
    blk = lambda w: pl.BlockSpec((TS, w), rev)
    return pl.pallas_call(
        body, grid=(nt,), name="pre_attn_bwd",
        out_shape=(jax.ShapeDtypeStruct((s, d), F32), jax.ShapeDtypeStruct((s, fcols + D_POOL), BF16),
                   jax.ShapeDtypeStruct((1, d), F32), jax.ShapeDtypeStruct((1, LANES), F32)),
        in_specs=[pl.BlockSpec((sub, HEADS * VROWS, TQ), lambda i: (nt - 1 - i, 0, 0)),
                  pl.BlockSpec((sub, HEADS * VROWS, TQ), lambda i: (nt - 1 - i, 0, 0)),
                  pl.BlockSpec((sub, D_ATTN, TQ), lambda i: (nt - 1 - i, 0, 0)),
                  blk(LANES), blk(D_POOL), blk(d), blk(d),
                  pl.BlockSpec((1, d), _fixed), pl.BlockSpec((qkv, d), _fixed), pl.BlockSpec(wf.shape, _fixed),
                  pl.BlockSpec(wu.shape, _fixed)],
        out_specs=(blk(d), blk(fcols + D_POOL), pl.BlockSpec((1, d), _fixed), pl.BlockSpec((1, LANES), _fixed)),
        scratch_shapes=[pltpu.VMEM((n, D_POOL), F32), pltpu.VMEM((1, LANES), F32), pltpu.VMEM((TS, LANES), F32),
                        pltpu.VMEM((LANES, TS), F32)],
        compiler_params=_params(1),
    )(dqt3, dkt3, dvt3, fl, dy, x, dh1, g1, wqkv, wf, wu)


def _wgrad(a, b, out_dtype, name):
    s, m = a.shape
    n = b.shape[1]
    tm = max(t for t in range(LANES, min(m, TM_WGRAD) + 1, LANES) if m % t == 0)
    ts = min(TS_WGRAD, s)
    ns = s // ts

    def body(a_ref, b_ref, o_ref, acc):
        i = pl.program_id(1)

        @pl.when(i == 0)
        def _():
            acc[...] = jnp.zeros_like(acc)

        acc[...] += _tn(a_ref[...], b_ref[...])

        @pl.when(i == ns - 1)
        def _():
            o_ref[...] = acc[...].astype(out_dtype)

    return pl.pallas_call(
        body, grid=(m // tm, ns), name=name, out_shape=jax.ShapeDtypeStruct((m, n), out_dtype),
        in_specs=[pl.BlockSpec((ts, tm), lambda j, i: (i, j)), pl.BlockSpec((ts, n), lambda j, i: (i, 0))],
        out_specs=pl.BlockSpec((tm, n), lambda j, i: (j, 0)),
        scratch_shapes=[pltpu.VMEM((tm, n), F32)],
        compiler_params=_params(2),
    )(a, b)


def _wgrad_in(dz, hn):
    s, m = dz.shape
    n = hn.shape[1]
    ts = min(TS_WGRAD, s)
    ns = s // ts
    pad_at, pad = 3 * D_ATTN + HEADS, LANES - HEADS
    assert m == D_IN + pad and N_DEV * SHARD_IN == D_IN

    def pieces(d):
        lo, hi = d * SHARD_IN, (d + 1) * SHARD_IN
        spans = [(lo, min(hi, pad_at), 0), (max(lo, pad_at), hi, pad)]
        return [(a + shift, b - a, a - lo) for a, b, shift in spans if b > a]

    def body(a_ref, b_ref, o_ref, acc, stage):
        i = pl.program_id(0)

        @pl.when(i == 0)
        def _():
            acc[...] = jnp.zeros_like(acc)

        acc[...] += _tn(a_ref[...], b_ref[...])

        @pl.when(i == ns - 1)
        def _():
            stage[SHARD_IN:ROWS_IN, :] = jnp.zeros((ROWS_IN - SHARD_IN, n), F32)
            for d in range(N_DEV):
                for src, rows, dst in pieces(d):
                    stage[dst:dst + rows, :] = acc[src:src + rows, :]
                o_ref[d] = stage[...].astype(BF16)

    return pl.pallas_call(
        body, grid=(ns,), name="wgrad_in", out_shape=jax.ShapeDtypeStruct((N_DEV, ROWS_IN, n), BF16),
        in_specs=[pl.BlockSpec((ts, m), _row), pl.BlockSpec((ts, n), _row)],
        out_specs=pl.BlockSpec((N_DEV, ROWS_IN, n), lambda i: (0, 0, 0)),
        scratch_shapes=[pltpu.VMEM((m, n), F32), pltpu.VMEM((ROWS_IN, n), F32)],
        compiler_params=_params(1),
    )(dz, hn)


def _adamw(w, g, m, v):
    m = ADAM_B1 * m + (1.0 - ADAM_B1) * g
    v = ADAM_B2 * v + (1.0 - ADAM_B2) * (g * g)
    m_hat = m / (1.0 - ADAM_B1 ** ADAM_STEP)
    v_hat = v / (1.0 - ADAM_B2 ** ADAM_STEP)
    delta = -ADAM_LR * (m_hat / (jnp.sqrt(v_hat) + ADAM_EPS) + ADAM_WD * w)
    return delta, m, v


def _sum_update(p_ref, w_ref, m_ref, v_ref, g_ref, d_ref, nm_ref, nv_ref):
    g = p_ref[0].astype(F32)
    for k in range(1, p_ref.shape[0]):
        g = g + p_ref[k].astype(F32)
    g_ref[...] = g
    d_ref[...], nm_ref[...], nv_ref[...] = _adamw(w_ref[...], g, m_ref[...], v_ref[...])


def _reduce_update_rest(parts, w, m, v, chip_blocks, small_block):
    nk, r, c = parts.shape
    ns = r // TR_REST

    def body(p_ref, w_ref, m_ref, v_ref, b_ref, sm_ref, g_ref, d_ref, nm_ref, nv_ref, got_ref, all_ref,
             stage_b, stage_s, send_b, recv_b, local_b, send_s, recv_s, local_s):
        i = pl.program_id(0)

        @pl.when(i == 0)
        def _():
            _chips_start(b_ref, got_ref, stage_b, send_b, recv_b, local_b)
            _gather_start(sm_ref, all_ref, stage_s, send_s, recv_s, local_s)

        _sum_update(p_ref, w_ref, m_ref, v_ref, g_ref, d_ref, nm_ref, nv_ref)

        @pl.when(i == ns - 1)
        def _():
            _gather_pass_on(all_ref, send_s, recv_s)
            _chips_finish(b_ref, got_ref, send_b, recv_b)
            _gather_finish(sm_ref, all_ref, send_s, recv_s)

    blk = pl.BlockSpec((TR_REST, c), _row)
    out = jax.ShapeDtypeStruct((r, c), F32)
    dma = pltpu.SemaphoreType.DMA
    return pl.pallas_call(
        body, grid=(ns,), name="reduce_update_rest",
        out_shape=(out,) * 4 + (jax.ShapeDtypeStruct(chip_blocks.shape, chip_blocks.dtype),
                                jax.ShapeDtypeStruct((N_DEV,) + small_block.shape, small_block.dtype)),
        in_specs=[pl.BlockSpec((nk, TR_REST, c), lambda i: (0, i, 0)), blk, blk, blk, ANY, ANY],
        out_specs=(blk,) * 4 + (ANY, ANY),
        scratch_shapes=[pltpu.VMEM(chip_blocks.shape[1:], chip_blocks.dtype), pltpu.VMEM(small_block.shape, small_block.dtype),
                        dma((3,)), dma((3,)), dma, dma((7,)), dma((7,)), dma],
        compiler_params=_params(1),
    )(parts, w, m, v, chip_blocks, small_block)


def _reduce_update_big(parts, w, m, v, tr, name):
    nk, r, c = parts.shape

    def body(p_ref, w_ref, m_ref, v_ref, g_ref, d_ref, nm_ref, nv_ref):
        _sum_update(p_ref, w_ref, m_ref, v_ref, g_ref, d_ref, nm_ref, nv_ref)

    blk = pl.BlockSpec((tr, c), _row)
    out = jax.ShapeDtypeStruct((r, c), F32)
    return pl.pallas_call(
        body, grid=(r // tr,), name=name, out_shape=(out,) * 4,
        in_specs=[pl.BlockSpec((nk, tr, c), lambda i: (0, i, 0)), blk, blk, blk],
        out_specs=(blk,) * 4, compiler_params=_params(1),
    )(parts, w, m, v)


def _reduce_update_small(parts, late, w, m, v):
    nd = parts.shape[0]
    first = parts.shape[1] - late.shape[1]

    def body(p_ref, q_ref, w_ref, m_ref, v_ref, g_ref, d_ref, nm_ref, nv_ref):
        g, t = p_ref[0], q_ref[0]
        for k in range(1, nd):
            g, t = g + p_ref[k], t + q_ref[k]
        g_ref[...] = g
        g_ref[first:, :] = g[first:, :] + t
        d_ref[...], nm_ref[...], nv_ref[...] = _adamw(w_ref[...], g_ref[...], m_ref[...], v_ref[...])

    out = jax.ShapeDtypeStruct(w.shape, F32)
    return pl.pallas_call(body, name="reduce_update_small", out_shape=(out,) * 4,
                          compiler_params=pltpu.CompilerParams(vmem_limit_bytes=VMEM_LIMIT))(parts, late, w, m, v)


MESH = pl.DeviceIdType.MESH


def _copy_through_vmem(src_hbm, dst_hbm, stage, sem):
    load = pltpu.make_async_copy(src_hbm, stage, sem)
    load.start()
    load.wait()
    store = pltpu.make_async_copy(stage, dst_hbm, sem)
    store.start()
    store.wait()


class _GatherPlan:
    def __init__(self, x_ref, out_ref, send_sems, recv_sems):
        x, y, c = lax.axis_index("x"), lax.axis_index("y"), lax.axis_index("c")
        self.me, self.sibling, self.c = (x, y, c), (x, y, 1 - c), c
        self.chips = [(1 - x, y), (x, 1 - y), (1 - x, 1 - y)]
        self.x_ref, self.out_ref, self.send_sems, self.recv_sems = x_ref, out_ref, send_sems, recv_sems

    def slot(self, px, py, pc):
        return self.out_ref.at[4 * px + 2 * py + pc]

    def copy(self, k, block, to, src=None):
        return pltpu.make_async_remote_copy(
            src_ref=self.slot(*block) if src is None else src, dst_ref=self.slot(*block),
            send_sem=self.send_sems.at[k], recv_sem=self.recv_sems.at[k], device_id=to, device_id_type=MESH)

    def first(self):
        return [self.copy(0, self.me, self.sibling, src=self.x_ref)] + [
            self.copy(1 + j, self.me, (*chip, self.c), src=self.x_ref) for j, chip in enumerate(self.chips)]

    def passed(self):
        return [self.copy(4 + j, (*chip, self.c), self.sibling) for j, chip in enumerate(self.chips)]


def _gather_start(x_ref, out_ref, stage, send_sems, recv_sems, local_sem):
    plan = _GatherPlan(x_ref, out_ref, send_sems, recv_sems)
    for cp in plan.first():
        cp.start()
    _copy_through_vmem(x_ref, plan.slot(*plan.me), stage, local_sem)


def _gather_pass_on(out_ref, send_sems, recv_sems):
    plan = _GatherPlan(None, out_ref, send_sems, recv_sems)
    passed = plan.passed()
    for j, chip in enumerate(plan.chips):
        plan.copy(1 + j, (*chip, plan.c), plan.me).wait_recv()
        passed[j].start()


def _gather_finish(x_ref, out_ref, send_sems, recv_sems):
    plan = _GatherPlan(x_ref, out_ref, send_sems, recv_sems)
    plan.copy(0, plan.sibling, plan.me).wait_recv()
    for j, chip in enumerate(plan.chips):
        plan.copy(4 + j, (*chip, 1 - plan.c), plan.me).wait_recv()
    for cp in plan.first() + plan.passed():
        cp.wait_send()


def _gather_w_in(xs):
    r, cdim = xs.shape
    qkv, f_end = 3 * D_ATTN, 3 * D_ATTN + HEADS

    def body(x_ref, all_ref, wqkv_ref, wf_ref, wu_ref, stage, send_sems, recv_sems, local_sem, blocks, flat, load_sem):
        _gather_start(x_ref, all_ref, stage, send_sems, recv_sems, local_sem)
        _gather_pass_on(all_ref, send_sems, recv_sems)
        _gather_finish(x_ref, all_ref, send_sems, recv_sems)
        load = pltpu.make_async_copy(all_ref, blocks, load_sem)
        load.start()
        load.wait()
        for dev in range(N_DEV):
            flat[dev * SHARD_IN:(dev + 1) * SHARD_IN, :] = blocks[dev, 0:SHARD_IN, :].astype(F32)
        wqkv_ref[...] = flat[0:qkv, :].astype(BF16)
        wf_ref[...] = jnp.concatenate([flat[qkv:f_end, :], jnp.zeros((LANES - HEADS, cdim), F32)], axis=0).astype(BF16)
        wu_ref[...] = flat[f_end:D_IN, :].astype(BF16)

    shape = lambda rows: jax.ShapeDtypeStruct((rows, cdim), xs.dtype)
    dma = pltpu.SemaphoreType.DMA
    return pl.pallas_call(
        body, name="gather_w_in",
        out_shape=(jax.ShapeDtypeStruct((N_DEV, r, cdim), xs.dtype), shape(qkv), shape(LANES), shape(D_IN - f_end)),
        in_specs=[ANY], out_specs=(ANY, VMEM_WHOLE, VMEM_WHOLE, VMEM_WHOLE),
        scratch_shapes=[pltpu.VMEM((r, cdim), xs.dtype), dma((7,)), dma((7,)), dma,
                        pltpu.VMEM((N_DEV, r, cdim), xs.dtype), pltpu.VMEM((D_IN, cdim), F32), dma],
        compiler_params=pltpu.CompilerParams(vmem_limit_bytes=VMEM_LIMIT),
    )(xs)[1:]


def _pair_copies(src_refs, dst_refs, send_sems, recv_sems):
    x, y, c = lax.axis_index("x"), lax.axis_index("y"), lax.axis_index("c")
    return [pltpu.make_async_remote_copy(
        src_ref=src.at[2 * k + (1 - c)], dst_ref=dst.at[k], send_sem=send_sems.at[k, p], recv_sem=recv_sems.at[k, p],
        device_id=(x, y, 1 - c), device_id_type=MESH)
        for k in range(N_DEV // 2) for p, (src, dst) in enumerate(zip(src_refs, dst_refs))]


def _rs_pair_sum(core, pieces, offsets, rows, name, landed=()):
    cdim = pieces[0].shape[2]
    nk = N_DEV // 2
    npc = len(pieces)
    nrem = npc - len(landed)
    spans = [(o, t.shape[1]) for t, o in zip(pieces, offsets)]
    ends = [o + n for o, n in spans]
    gaps = [(a, b - a) for a, b in zip(ends, [o for o, _ in spans[1:]] + [rows]) if b > a]

    def body(core_ref, *refs):
        own, src, got, o_ref = refs[:npc], refs[npc:npc + nrem], refs[npc + nrem:2 * npc], refs[2 * npc]
        landing, send_sems, recv_sems = refs[2 * npc + 1:]
        k = pl.program_id(0)
        x, y, c = lax.axis_index("x"), lax.axis_index("y"), lax.axis_index("c")

        def copies(kk):
            return [pltpu.make_async_remote_copy(
                src_ref=src[p].at[2 * kk + (1 - c)], dst_ref=landing.at[kk, pl.ds(o, n)],
                send_sem=send_sems.at[kk, p], recv_sem=recv_sems.at[kk, p], device_id=(x, y, 1 - c),
                device_id_type=MESH) for p, (o, n) in enumerate(spans[:nrem])]

        @pl.when(k == 0)
        def _():
            for kk in range(nk):
                for cp in copies(kk):
                    cp.start()

        for cp, piece, (o, n) in zip(copies(k), own, spans):
            cp.wait_recv()
            o_ref[0, o:o + n, :] = (piece[0].astype(F32) + landing[k, o:o + n, :].astype(F32)).astype(BF16)
        for theirs, piece, (o, n) in zip(got, own[nrem:], spans[nrem:]):
            o_ref[0, o:o + n, :] = (piece[0].astype(F32) + theirs[0].astype(F32)).astype(BF16)
        for o, n in gaps:
            o_ref[0, o:o + n, :] = jnp.zeros((n, cdim), BF16)

        @pl.when(k == nk - 1)
        def _():
            for kk in range(nk):
                for cp in copies(kk):
                    cp.wait_send()

    own_specs = [pl.BlockSpec((1, n, cdim), lambda k, core_ref: (2 * k + core_ref[0], 0, 0)) for _, n in spans]
    got_specs = [pl.BlockSpec((1, n, cdim), lambda k, core_ref: (k, 0, 0)) for _, n in spans[nrem:]]
    land_rows = max(o + n for o, n in spans[:nrem])
    return pl.pallas_call(
        body, name=name, out_shape=jax.ShapeDtypeStruct((nk, rows, cdim), BF16),
        grid_spec=pltpu.PrefetchScalarGridSpec(
            num_scalar_prefetch=1, grid=(nk,),
            in_specs=own_specs + [ANY] * nrem + got_specs,
            out_specs=pl.BlockSpec((1, rows, cdim), lambda k, core_ref: (k, 0, 0)),
            scratch_shapes=[pltpu.VMEM((nk, land_rows, cdim), BF16), pltpu.SemaphoreType.DMA((nk, nrem)),
                            pltpu.SemaphoreType.DMA((nk, nrem))]),
        compiler_params=_params(1),
    )(core, *pieces, *pieces[:nrem], *landed)


def _chips_start(b_ref, out_ref, stage, send_sems, recv_sems, local_sem):
    x, y, c = lax.axis_index("x"), lax.axis_index("y"), lax.axis_index("c")
    mychip = 2 * x + y
    for j, (px, py) in enumerate([(1 - x, y), (x, 1 - y), (1 - x, 1 - y)]):
        pltpu.make_async_remote_copy(
            src_ref=b_ref.at[2 * px + py], dst_ref=out_ref.at[mychip],
            send_sem=send_sems.at[j], recv_sem=recv_sems.at[j], device_id=(px, py, c), device_id_type=MESH).start()
    _copy_through_vmem(b_ref.at[mychip], out_ref.at[mychip], stage, local_sem)


def _chips_finish(b_ref, out_ref, send_sems, recv_sems):
    x, y, c = lax.axis_index("x"), lax.axis_index("y"), lax.axis_index("c")
    for j, (px, py) in enumerate([(1 - x, y), (x, 1 - y), (1 - x, 1 - y)]):
        pltpu.make_async_remote_copy(
            src_ref=b_ref.at[2 * px + py], dst_ref=out_ref.at[2 * px + py],
            send_sem=send_sems.at[j], recv_sem=recv_sems.at[j], device_id=(px, py, c), device_id_type=MESH).wait()


def _pad_rows(a, rows):
    return jnp.pad(a, ((0, rows - a.shape[0]), (0, 0)))


def _pack_in(w_in):
    return _pad_rows(w_in[0].T, ROWS_IN)


def _unpack_in(r):
    return r[0:SHARD_IN].T[None]


def _pack_rest(w_out, w_gate, w_up, w_down, w_ple, w_pg):
    head = _pad_rows(jnp.concatenate([w_out[0], w_pg[0], w_ple[0].T.reshape(ROWS_PLE, D_MODEL)], axis=0), OFF_GATE)
    return jnp.concatenate([head, w_gate[0].T, w_up[0].T, w_down[0]], axis=0)


def _unpack_rest(r):
    return (r[0:OFF_PG][None], r[OFF_GATE:OFF_UP].T[None], r[OFF_UP:OFF_DOWN].T[None], r[OFF_DOWN:ROWS_REST][None],
            r[OFF_PLE:OFF_PLE + ROWS_PLE].reshape(SHARD_SQ, D_PLE).T[None], r[OFF_PG:OFF_PLE][None])


def _pack_small(w_pool, g_mix_pre, g_mix_post, g_ffn_pre, g_ffn_post, g_ple, g_attn, g_pool, pool_scale, b_forget,
                loss=None):
    row = lambda vrow: vrow.reshape(1, -1)
    misc = [row(pool_scale), row(b_forget), row(loss) if loss is not None else jnp.zeros((1, 1), F32),
            jnp.zeros((1, D_MODEL - COL_LOSS - 1), F32)]
    rows = [w_pool.reshape(64, D_MODEL), row(g_mix_pre), row(g_mix_post), row(g_ffn_pre), row(g_ffn_post), row(g_ple),
            jnp.concatenate([row(g_attn), row(g_pool)], axis=1), jnp.concatenate(misc, axis=1),
            jnp.zeros((SMALL_ROWS - ROW_MISC - 1, D_MODEL), F32)]
    return jnp.concatenate(rows, axis=0)


def _pack_small_late(g_mix_pre, b_forget):
    misc = [jnp.zeros((1, COL_B_FORGET), F32), b_forget.reshape(1, -1), jnp.zeros((1, D_MODEL - COL_LOSS), F32)]
    return jnp.concatenate([g_mix_pre.reshape(1, -1), jnp.zeros((ROW_MISC - ROW_G_MIX_PRE - 1, D_MODEL), F32),
                            jnp.concatenate(misc, axis=1), jnp.zeros((SMALL_ROWS - ROW_MISC - 1, D_MODEL), F32)], axis=0)


def _unpack_small(r):
    gains, misc = r[ROW_GROUP_GAINS:ROW_GROUP_GAINS + 1], r[ROW_MISC:ROW_MISC + 1]
    return dict(
        w_pool=r[0:64].reshape(1, 4, POOL_CH, POOL_CH), g_mix_pre=r[ROW_G_MIX_PRE:ROW_G_MIX_PRE + 1],
        g_mix_post=r[ROW_G_MIX_POST:ROW_G_MIX_POST + 1], g_ffn_pre=r[ROW_G_FFN_PRE:ROW_G_FFN_PRE + 1],
        g_ffn_post=r[ROW_G_FFN_POST:ROW_G_FFN_POST + 1], g_ple=r[ROW_G_PLE:ROW_G_PLE + 1],
        g_attn_grp=gains[:, 0:D_ATTN], g_pool_grp=gains[:, D_ATTN:D_ATTN + D_POOL],
        pool_scale=misc[:, 0:D_POOL], b_forget=misc[:, COL_B_FORGET:COL_B_FORGET + HEADS])


def _step(x, p, tgt, small, in_w, in_m, in_v, rest_w, rest_m, rest_v):
    core = lax.axis_index("c").astype(jnp.int32).reshape(1)
    wqkv, wf, wu = _gather_w_in(in_w.astype(BF16))
    wpool = small["w_pool"].astype(BF16)
    bpad = jnp.pad(small["b_forget"], ((0, 0), (0, LANES - HEADS)))

    lay = _attn_layout_constants()
    rest_b = rest_w.astype(BF16)
    hn, qt3, ka, v, qat3, vt3, kt3, fl, y, mpre, gh = _pre_attn_fwd(x, small["g_mix_pre"], wqkv, wf, wu, bpad, wpool, lay,
                                                                 rest_b[0:OFF_GATE])
    a, lset3, gf = _attn_fwd(ka, qat3, vt3, rest_b[OFF_GATE:])
    wple_t = gh[:, OFF_PLE:OFF_PLE + ROWS_PLE].reshape(D_MODEL, D_PLE)
    mix, o, h1, hn2 = _post_attn_fwd(a, mpre, x, small["g_attn_grp"], small["g_pool_grp"], small["pool_scale"], gh,
                                     small["g_mix_post"], small["g_ffn_pre"])
    gate, up, act, ff, h2 = _ffn_fwd(hn2, gf, gf, gf, h1, small["g_ffn_post"])
    dh2, dff, dgl, dpp, h2b, pb, loss8, dg_ple, dg_ffn_post = _tail_fwd_bwd(
        h2, p, tgt, ff, wple_t, gh, small["g_ple"], small["g_ffn_post"])
    dgate, dup, dh1, dg_ffn_pre = _ffn_bwd(dff, gate, up, gf, gf, gf, h1, dh2, small["g_ffn_pre"])
    nd = N_DEV
    send_rest = [
        _wgrad(h2b, dgl, BF16, "wgrad_ple_gate").reshape(nd, SHARD_SQ, D_MODEL),
        _wgrad(dpp, pb, BF16, "wgrad_ple").reshape(nd, ROWS_PLE, D_MODEL),
        _wgrad(dgate, hn2, BF16, "wgrad_gate").reshape(nd, SHARD_FF, D_MODEL),
        _wgrad(dup, hn2, BF16, "wgrad_up").reshape(nd, SHARD_FF, D_MODEL),
        _wgrad(act, dff, BF16, "wgrad_down").reshape(nd, SHARD_FF, D_MODEL)]
    (dob, dat3, dlt3, dmpb, dy, dg_mix_post, dg_attn, dg_pool, dps), landed = _post_attn_bwd(
        dh1, o, a, mpre, gh, wpool, small["g_mix_post"], small["g_attn_grp"], small["g_pool_grp"], small["pool_scale"],
        send_rest)
    send_rest = [_wgrad(mix, dob, BF16, "wgrad_out").reshape(nd, SHARD_SQ, D_MODEL)] + send_rest
    pair_rest = _rs_pair_sum(core, send_rest, [0, OFF_PG, OFF_PLE, OFF_GATE, OFF_UP, OFF_DOWN], ROWS_REST,
                             "rs_pair_sum_rest", landed)

    dwp = _wgrad(y, dmpb, F32, "wgrad_pool")
    dw_pool = jnp.stack([dwp[g * POOL_CH:(g + 1) * POOL_CH, g * POOL_CH:(g + 1) * POOL_CH] for g in range(4)])
    small_part = _pack_small(dw_pool, jnp.zeros((1, D_MODEL), F32), dg_mix_post, dg_ffn_pre, dg_ffn_post, dg_ple,
                             dg_attn, dg_pool, dps, jnp.zeros((1, HEADS), F32), loss8[0:1, 0:1])
    dqt3, dkt3, dvt3, chips_rest, small_all = _attn_bwd(ka, v, kt3, qat3, qt3, dat3, lset3, dlt3, pair_rest, small_part)

    gx, dz, dg_mix_pre, db = _pre_attn_bwd(dqt3, dkt3, dvt3, fl, dy, x, dh1, small["g_mix_pre"], wqkv, wf, wu)

    pair_in = _rs_pair_sum(core, [_wgrad_in(dz, hn)], [0], ROWS_IN, "rs_pair_sum_in")

    small_late = _pack_small_late(dg_mix_pre, db[:, 0:HEADS])
    *upd_rest, chips_in, late_all = _reduce_update_rest(chips_rest, rest_w, rest_m, rest_v, pair_in, small_late)
    upd_in = _reduce_update_big(chips_in, in_w, in_m, in_v, ROWS_IN, "reduce_update_in")
    return gx, (small_all, late_all), upd_in, upd_rest


def kernel(x, p, g_mix_pre, w_in, b_forget, g_attn_grp, g_pool_grp, w_pool, pool_scale, w_out, g_mix_post, g_ffn_pre, w_ffn_gate, w_ffn_up, w_ffn_down, g_ffn_post, w_ple_proj, g_ple, w_ple_gate, loss_target, m_g_mix_pre, m_w_in, m_b_forget, m_g_attn_grp, m_g_pool_grp, m_w_pool, m_pool_scale, m_w_out, m_g_mix_post, m_g_ffn_pre, m_w_ffn_gate, m_w_ffn_up, m_w_ffn_down, m_g_ffn_post, m_w_ple_proj, m_g_ple, m_w_ple_gate, v_g_mix_pre, v_w_in, v_b_forget, v_g_attn_grp, v_g_pool_grp, v_w_pool, v_pool_scale, v_w_out, v_g_mix_post, v_g_ffn_pre, v_w_ffn_gate, v_w_ffn_up, v_w_ffn_down, v_g_ffn_post, v_w_ple_proj, v_g_ple, v_w_ple_gate):
    small = dict(w_pool=w_pool[0], g_mix_pre=g_mix_pre, g_mix_post=g_mix_post, g_ffn_pre=g_ffn_pre,
                 g_ffn_post=g_ffn_post, g_ple=g_ple, g_attn_grp=g_attn_grp, g_pool_grp=g_pool_grp,
                 pool_scale=pool_scale, b_forget=b_forget)
    gx, small_all, upd_in, upd_rest = _step(
        x[0], p[0, 0], loss_target[0], small, _pack_in(w_in), _pack_in(m_w_in), _pack_in(v_w_in),
        _pack_rest(w_out, w_ffn_gate, w_ffn_up, w_ffn_down, w_ple_proj, w_ple_gate),
        _pack_rest(m_w_out, m_w_ffn_gate, m_w_ffn_up, m_w_ffn_down, m_w_ple_proj, m_w_ple_gate),
        _pack_rest(v_w_out, v_w_ffn_gate, v_w_ffn_up, v_w_ffn_down, v_w_ple_proj, v_w_ple_gate))

    sm_w = _pack_small(w_pool, g_mix_pre, g_mix_post, g_ffn_pre, g_ffn_post, g_ple, g_attn_grp, g_pool_grp, pool_scale, b_forget)
    sm_m = _pack_small(m_w_pool, m_g_mix_pre, m_g_mix_post, m_g_ffn_pre, m_g_ffn_post, m_g_ple, m_g_attn_grp, m_g_pool_grp, m_pool_scale, m_b_forget)
    sm_v = _pack_small(v_w_pool, v_g_mix_pre, v_g_mix_post, v_g_ffn_pre, v_g_ffn_post, v_g_ple, v_g_attn_grp, v_g_pool_grp, v_pool_scale, v_b_forget)
    upd_small = _reduce_update_small(*small_all, sm_w, sm_m, sm_v)
    loss = upd_small[0][ROW_MISC, COL_LOSS]

    def leaves(k):
        b_out, b_gate, b_up, b_down, b_ple, b_pg = _unpack_rest(upd_rest[k])
        s = _unpack_small(upd_small[k])
        return (s["g_mix_pre"], _unpack_in(upd_in[k]), s["b_forget"], s["g_attn_grp"], s["g_pool_grp"], s["w_pool"],
                s["pool_scale"], b_out, s["g_mix_post"], s["g_ffn_pre"], b_gate, b_up, b_down, s["g_ffn_post"], b_ple,
                s["g_ple"], b_pg)

    return (loss, gx[None], *leaves(0), *leaves(1), *leaves(2), *leaves(3))
```

```python
import functools

import jax
import jax.numpy as jnp
from jax import lax
from jax.experimental import pallas as pl
from jax.experimental.pallas import tpu as pltpu

F32 = jnp.float32
BF16 = jnp.bfloat16
HIGHEST = lax.Precision.HIGHEST

D_MODEL = 1024
HEADS = 8
HEAD_DIM = 64
D_ATTN = HEADS * HEAD_DIM
POOL_WINDOWS = (2, 4, 8, 16)
POOL_CH = 128
D_POOL = POOL_CH * len(POOL_WINDOWS)
D_FF = 2816
D_PLE = 256
D_IN = 3 * D_ATTN + HEADS + D_POOL
RMS_EPS = 1e-6
N_DEV = 8

ADAM_LR = 0.001
ADAM_B1 = 0.9
ADAM_B2 = 0.999
ADAM_EPS = 1e-08
ADAM_WD = 0.01
ADAM_STEP = 10

LANES = 128
HALO = 16
TS = 512
TS_FF = 512
TS_WGRAD = 1024
TM_WGRAD = 2176
TQ = 256
TN_FF = 1408
NEG = -1e30
VMEM_LIMIT = 56 * 1024 * 1024

SHARD_IN = 257
ROWS_IN = 272
SHARD_FF = 352
SHARD_SQ = D_MODEL // N_DEV
ROWS_PLE = D_PLE * SHARD_SQ // D_MODEL
OFF_PG = SHARD_SQ
OFF_PLE = 2 * SHARD_SQ
OFF_GATE = SHARD_FF
OFF_UP = 2 * SHARD_FF
OFF_DOWN = 3 * SHARD_FF
ROWS_REST = 4 * SHARD_FF
TR_REST = SHARD_FF

SMALL_ROWS = 72
ROW_G_MIX_PRE, ROW_G_MIX_POST, ROW_G_FFN_PRE, ROW_G_FFN_POST, ROW_G_PLE = 64, 65, 66, 67, 68
ROW_GROUP_GAINS, ROW_MISC = 69, 70
COL_B_FORGET = D_POOL
COL_LOSS = D_POOL + HEADS


def _nn(a, b):
    return jnp.dot(a, b, preferred_element_type=F32)


def _nt(a, b):
    return lax.dot_general(a, b, (((1,), (1,)), ((), ())), preferred_element_type=F32)


def _tn(a, b):
    return lax.dot_general(a, b, (((0,), (0,)), ((), ())), preferred_element_type=F32)


def _rstd(v):
    return lax.rsqrt(jnp.mean(v * v, axis=-1, keepdims=True) + RMS_EPS)


def _rms_bwd(v, g, dy):
    r = _rstd(v)
    vh = v * r
    t = dy * g
    dv = r * (t - vh * jnp.mean(t * vh, axis=-1, keepdims=True))
    return dv, jnp.sum(dy * vh, axis=0, keepdims=True)


def _split3(v):
    hi = v.astype(BF16)
    rest = v - hi.astype(F32)
    mid = rest.astype(BF16)
    return hi, mid, (rest - mid.astype(F32)).astype(BF16)


def _mask_matmul(mask, v):
    hi, mid, lo = _split3(v)
    return _nn(mask, lo) + _nn(mask, mid) + _nn(mask, hi)


def _running_sum(v, reverse=False):
    tq = v.shape[0] // 2
    rr = lax.broadcasted_iota(jnp.int32, (tq, tq), 0)
    cc = lax.broadcasted_iota(jnp.int32, (tq, tq), 1)
    mask = ((cc >= rr) if reverse else (cc <= rr)).astype(BF16)
    top, bot = _mask_matmul(mask, v[0:tq]), _mask_matmul(mask, v[tq:])
    if reverse:
        top = top + bot[0:1, :]
    else:
        bot = bot + top[tq - 1:tq, :]
    return jnp.concatenate([top, bot], axis=0)


def _params(n_grid):
    return pltpu.CompilerParams(dimension_semantics=("arbitrary",) * n_grid, vmem_limit_bytes=VMEM_LIMIT)


def _row(i):
    return (i, 0)


def _fixed(*_):
    return (0, 0)


def _spec_square(part):
    return pl.BlockSpec((N_DEV, SHARD_SQ, D_MODEL), lambda *_: (0, part, 0))


def _spec_ff(part):
    return pl.BlockSpec((TN_FF // SHARD_FF, SHARD_FF, D_MODEL), lambda i, j: (j, part, 0))


assert TS == 2 * TQ and TN_FF % SHARD_FF == 0
_HALVES = (slice(0, TQ), slice(TQ, TS))

VMEM_WHOLE = pl.BlockSpec(memory_space=pltpu.VMEM)
SMEM_WHOLE = pl.BlockSpec(memory_space=pltpu.SMEM)
ANY = pl.BlockSpec(memory_space=pl.ANY)


LOG2E = 1.4426950408889634
VROWS = HEAD_DIM + 16
AUG = 128
BIAS_LANE = HEAD_DIM
ONE_LANE = HEAD_DIM + 3
SPARE_LANE = HEADS
PART_LANES = 16
assert SPARE_LANE < PART_LANES and 3 * PART_LANES <= LANES


def _attn_layout_constants():
    import numpy as np
    bias_k = np.zeros((LANES, HEADS * AUG), np.float32)
    bias_q = np.zeros((LANES, HEADS * AUG), np.float32)
    for h in range(HEADS):
        for part in range(3):
            bias_k[part * PART_LANES + h, h * AUG + BIAS_LANE + part] = -1.0
            bias_q[part * PART_LANES + h, h * AUG + ONE_LANE + part] = 1.0
            bias_k[SPARE_LANE, h * AUG + ONE_LANE + part] = 1.0
            bias_q[SPARE_LANE, h * AUG + BIAS_LANE + part] = 1.0
    after = np.concatenate([np.arange(h * AUG + HEAD_DIM, (h + 1) * AUG) for h in range(HEADS)])
    as_bf = lambda a: jnp.asarray(a, BF16)
    return dict(bias_k=as_bf(bias_k[:, after]), bias_q_t=as_bf(bias_q[:, after].T))


def _pre_attn_fwd(x, g1, wqkv, wf, wu, bpad, wpool, lay, own_block):
    s, d = x.shape
    nt = s // TS
    sub = TS // TQ

    def body(x_ref, g_ref, wqkv_ref, wf_ref, wu_ref, b_ref, wp_ref, bk_ref, bqt_ref, own_ref,
             hn_ref, qt_ref, ka_ref, v_ref, qat_ref, vt_ref, kt_ref, fl_ref, y_ref, mp_ref, all_ref,
             ubuf, ccar, cbuf, stage, send_sems, recv_sems, local_sem):
        i = pl.program_id(0)

        @pl.when(i == 0)
        def _():
            _gather_start(own_ref, all_ref, stage, send_sems, recv_sems, local_sem)
            ubuf[0:HALO, :] = jnp.zeros((HALO, D_POOL), F32)
            ccar[...] = jnp.zeros_like(ccar)

        @pl.when(i == max(nt - 2, 0))
        def _():
            _gather_pass_on(all_ref, send_sems, recv_sems)

        xv = x_ref[...]
        hn = (xv * _rstd(xv) * g_ref[...]).astype(BF16)
        hn_ref[...] = hn
        zq = _nt(hn, wqkv_ref[...])
        qt = (zq[:, 0:D_ATTN] * 0.125).astype(BF16).T
        qb = (zq[:, 0:D_ATTN] * (0.125 * LOG2E)).astype(BF16)
        kb = zq[:, D_ATTN:2 * D_ATTN].astype(BF16)
        vb = zq[:, 2 * D_ATTN:3 * D_ATTN].astype(BF16)
        v_ref[...] = vb

        fl = _nt(hn, wf_ref[...]) + b_ref[...]
        fl_ref[...] = fl
        logf = jax.nn.log_sigmoid(fl)
        c = _running_sum(logf) + ccar[...]
        cbuf[...] = c
        ccar[...] = cbuf[TS - 1:TS, :]
        hi, mid, lo = (part.astype(F32) for part in _split3(c * LOG2E))
        lane = lax.broadcasted_iota(jnp.int32, (TS, LANES), 1)
        later = jnp.where(lane < 2 * PART_LANES, pltpu.roll(mid, PART_LANES, 1), pltpu.roll(lo, 2 * PART_LANES, 1))
        parts = jnp.where(lane < PART_LANES, jnp.where(lane == SPARE_LANE, 1.0, hi), later).astype(BF16)
        extra = AUG - HEAD_DIM
        kbias = _nn(parts, bk_ref[...]).astype(BF16)
        for h in range(HEADS):
            ka_ref[:, h * AUG:h * AUG + HEAD_DIM] = kb[:, h * HEAD_DIM:(h + 1) * HEAD_DIM]
            ka_ref[:, h * AUG + HEAD_DIM:(h + 1) * AUG] = kbias[:, h * extra:(h + 1) * extra]
        qbt = qb.T
        qbias = _nt(bqt_ref[...], parts).astype(BF16)
        vt = vb.T
        kt = kb.T
        for a in range(sub):
            cols = slice(a * TQ, (a + 1) * TQ)
            for h in range(HEADS):
                qat_ref[a, h * AUG:h * AUG + HEAD_DIM, :] = qbt[h * HEAD_DIM:(h + 1) * HEAD_DIM, cols]
                qat_ref[a, h * AUG + HEAD_DIM:(h + 1) * AUG, :] = qbias[h * extra:(h + 1) * extra, cols]
            for ref, mat in ((qt_ref, qt), (kt_ref, kt), (vt_ref, vt)):
                for h in range(HEADS):
                    ref[a, h * VROWS:h * VROWS + HEAD_DIM, :] = mat[h * HEAD_DIM:(h + 1) * HEAD_DIM, cols]
                    ref[a, h * VROWS + HEAD_DIM:(h + 1) * VROWS, :] = jnp.ones((VROWS - HEAD_DIM, TQ), BF16)

        u = _nt(hn, wu_ref[...])
        ubuf[HALO:HALO + TS, :] = u
        t = i * TS + lax.broadcasted_iota(jnp.int32, (TS, 1), 0)
        for g, w in enumerate(POOL_WINDOWS):
            cols = slice(g * POOL_CH, (g + 1) * POOL_CH)
            sm = ubuf[:, cols]
            step = 1
            while step < w:
                sm = sm + pltpu.roll(sm, step, 0)
                step *= 2
            cnt = jnp.minimum(t + 1, w).astype(F32)
            yg = (sm[HALO:, :] / cnt - u[:, cols]).astype(BF16)
            y_ref[:, cols] = yg
            mp_ref[:, cols] = _nn(yg, wp_ref[g])
        ubuf[0:HALO, :] = u[TS - HALO:, :]

        @pl.when(i == nt - 1)
        def _():
            _gather_finish(own_ref, all_ref, send_sems, recv_sems)

    nq = s // TQ
    aug = HEADS * AUG
    outs = (
        jax.ShapeDtypeStruct((s, d), BF16), jax.ShapeDtypeStruct((nq, HEADS * VROWS, TQ), BF16),
        jax.ShapeDtypeStruct((s, aug), BF16), jax.ShapeDtypeStruct((s, D_ATTN), BF16),
        jax.ShapeDtypeStruct((nq, aug, TQ), BF16), jax.ShapeDtypeStruct((nq, HEADS * VROWS, TQ), BF16),
        jax.ShapeDtypeStruct((nq, HEADS * VROWS, TQ), BF16),
        jax.ShapeDtypeStruct((s, LANES), F32),
        jax.ShapeDtypeStruct((s, D_POOL), BF16), jax.ShapeDtypeStruct((s, D_POOL), F32),
        jax.ShapeDtypeStruct((N_DEV,) + own_block.shape, own_block.dtype),
    )
    fixed3 = lambda i: (0, 0, 0)
    tiles3 = lambda rows: pl.BlockSpec((sub, rows, TQ), lambda i: (i, 0, 0))
    return pl.pallas_call(
        body, grid=(nt,), out_shape=outs, name="pre_attn_fwd",
        in_specs=[pl.BlockSpec((TS, d), _row), pl.BlockSpec((1, d), _fixed),
                  pl.BlockSpec((3 * D_ATTN, d), _fixed), pl.BlockSpec(wf.shape, _fixed), pl.BlockSpec(wu.shape, _fixed),
                  pl.BlockSpec((1, LANES), _fixed), pl.BlockSpec(wpool.shape, fixed3),
                  pl.BlockSpec(lay["bias_k"].shape, _fixed), pl.BlockSpec(lay["bias_q_t"].shape, _fixed), ANY],
        out_specs=(pl.BlockSpec((TS, d), _row), tiles3(HEADS * VROWS),
                   pl.BlockSpec((TS, aug), _row), pl.BlockSpec((TS, D_ATTN), _row),
                   tiles3(aug), tiles3(HEADS * VROWS), tiles3(HEADS * VROWS),
                   pl.BlockSpec((TS, LANES), _row),
                   pl.BlockSpec((TS, D_POOL), _row), pl.BlockSpec((TS, D_POOL), _row), ANY),
        scratch_shapes=[pltpu.VMEM((TS + HALO, D_POOL), F32), pltpu.VMEM((1, LANES), F32), pltpu.VMEM((TS, LANES), F32),
                        pltpu.VMEM(own_block.shape, own_block.dtype),
                        pltpu.SemaphoreType.DMA((7,)), pltpu.SemaphoreType.DMA((7,)), pltpu.SemaphoreType.DMA],
        compiler_params=_params(1),
    )(x, g1, wqkv, wf, wu, bpad, wpool, lay["bias_k"], lay["bias_q_t"], own_block)


def _causal_in_tile():
    krow = lax.broadcasted_iota(jnp.int32, (TQ, TQ), 0)
    qcol = lax.broadcasted_iota(jnp.int32, (TQ, TQ), 1)
    return krow <= qcol


def _attn_fwd(ka, qat3, vt3, own_block):
    s = ka.shape[0]
    nq = s // TQ
    pass_on_step = max(nq - 2, 0)

    def body(qa_ref, ka_ref, vt_ref, own_ref, a_ref, lset_ref, all_ref, acc, out_t, st_scr, pt_scr,
             stage, send_sems, recv_sems, local_sem):
        i = pl.program_id(0)

        @pl.when(i == 0)
        def _():
            _gather_start(own_ref, all_ref, stage, send_sems, recv_sems, local_sem)

        @pl.when(i == pass_on_step)
        def _():
            _gather_pass_on(all_ref, send_sems, recv_sems)

        acc[...] = jnp.zeros_like(acc)

        def tile(j, stats, masked):
            tile_max = []
            for h in range(HEADS):
                aug = slice(h * AUG, (h + 1) * AUG)
                st = _nn(ka_ref[pl.ds(j * TQ, TQ), aug], qa_ref[0, aug, :])
                if masked:
                    st = jnp.where(_causal_in_tile(), st, NEG)
                st_scr[h] = st
                tile_max.append(jnp.max(st, axis=0, keepdims=True))
            new, scale = [], []
            for h in range(HEADS):
                m_new = jnp.maximum(stats[h], tile_max[h])
                scale.append(jnp.exp2(stats[h] - m_new))
                pt_scr[h] = jnp.exp2(st_scr[h] - m_new).astype(BF16)
                new.append(m_new)
            for h in range(HEADS):
                rows = slice(h * VROWS, (h + 1) * VROWS)
                acc[rows, :] = scale[h] * acc[rows, :] + _nn(vt_ref[j, rows, :], pt_scr[h])
            return tuple(new)

        init = tuple(jnp.full((1, TQ), NEG, F32) for _ in range(HEADS))
        stats = lax.fori_loop(0, i, functools.partial(tile, masked=False), init)
        stats = tile(i, stats, True)
        for h in range(HEADS):
            denom = acc[h * VROWS + HEAD_DIM:h * VROWS + HEAD_DIM + 1, :]
            out_t[h * HEAD_DIM:(h + 1) * HEAD_DIM, :] = acc[h * VROWS:h * VROWS + HEAD_DIM, :] / denom
            lset_ref[0, h:h + 1, :] = stats[h] + jnp.log2(denom)
        a_ref[...] = out_t[...].T

        @pl.when(i == nq - 1)
        def _():
            _gather_finish(own_ref, all_ref, send_sems, recv_sems)

    r, cdim = own_block.shape
    return pl.pallas_call(
        body, grid=(nq,), name="attn_fwd",
        out_shape=(jax.ShapeDtypeStruct((s, D_ATTN), F32), jax.ShapeDtypeStruct((nq, HEADS, TQ), F32),
                   jax.ShapeDtypeStruct((N_DEV, r, cdim), own_block.dtype)),
        in_specs=[pl.BlockSpec((1, HEADS * AUG, TQ), lambda i: (i, 0, 0)), VMEM_WHOLE, VMEM_WHOLE, ANY],
        out_specs=(pl.BlockSpec((TQ, D_ATTN), _row), pl.BlockSpec((1, HEADS, TQ), lambda i: (i, 0, 0)), ANY),
        scratch_shapes=[pltpu.VMEM((HEADS * VROWS, TQ), F32), pltpu.VMEM((D_ATTN, TQ), F32),
                        pltpu.VMEM((HEADS, TQ, TQ), F32), pltpu.VMEM((HEADS, TQ, TQ), BF16),
                        pltpu.VMEM((r, cdim), own_block.dtype),
                        pltpu.SemaphoreType.DMA((7,)), pltpu.SemaphoreType.DMA((7,)), pltpu.SemaphoreType.DMA],
        compiler_params=_params(1),
    )(qat3, ka, vt3, own_block)


def _post_attn_fwd(a, mpre, x, g_attn, g_pool, pscale, wout, g_post, g_ffn_pre):
    s, d = x.shape

    def body(a_ref, mp_ref, x_ref, ga_ref, gp_ref, ps_ref, wo_ref, gpost_ref, gpre_ref,
             mix_ref, o_ref, h1_ref, hn2_ref):
        for rows in _HALVES:
            av = a_ref[rows, :]
            mix_ref[rows, 0:D_ATTN] = (av * _rstd(av) * ga_ref[...]).astype(BF16)
            mv = mp_ref[rows, :] * ps_ref[...]
            mix_ref[rows, D_ATTN:] = (mv * _rstd(mv) * gp_ref[...]).astype(BF16)
            o = _nn(mix_ref[rows, :], wo_ref[...].reshape(d, d))
            o_ref[rows, :] = o
            h1 = x_ref[rows, :] + o * _rstd(o) * gpost_ref[...]
            h1_ref[rows, :] = h1
            hn2_ref[rows, :] = (h1 * _rstd(h1) * gpre_ref[...]).astype(BF16)

    vec = lambda n: pl.BlockSpec((1, n), _fixed)
    return pl.pallas_call(
        body, grid=(s // TS,), name="post_attn_fwd",
        out_shape=(jax.ShapeDtypeStruct((s, d), BF16), jax.ShapeDtypeStruct((s, d), F32),
                   jax.ShapeDtypeStruct((s, d), F32), jax.ShapeDtypeStruct((s, d), BF16)),
        in_specs=[pl.BlockSpec((TS, D_ATTN), _row), pl.BlockSpec((TS, D_POOL), _row), pl.BlockSpec((TS, d), _row),
                  vec(D_ATTN), vec(D_POOL), vec(D_POOL), _spec_square(0), vec(d), vec(d)],
        out_specs=(pl.BlockSpec((TS, d), _row),) * 4,
        compiler_params=_params(1),
    )(a, mpre, x, g_attn, g_pool, pscale, wout, g_post, g_ffn_pre)


def _ffn_fwd(hn2, wg, wu, wd, h1, g_post):
    s, d = h1.shape
    nc = D_FF // TN_FF
    ts = min(TS_FF, s)

    def body(hn_ref, wg_ref, wu_ref, wd_ref, h1_ref, g_ref, gate_ref, up_ref, act_ref, ff_ref, h2_ref, acc):
        j = pl.program_id(1)

        @pl.when(j == 0)
        def _():
            acc[...] = jnp.zeros_like(acc)

        for r in range(2):
            rows = slice(r * (ts // 2), (r + 1) * (ts // 2))
            hn = hn_ref[rows, :]
            gt = _nt(hn, wg_ref[...].reshape(TN_FF, d))
            up = _nt(hn, wu_ref[...].reshape(TN_FF, d))
            gate_ref[rows, :] = gt.astype(BF16)
            up_ref[rows, :] = up.astype(BF16)
            act_ref[rows, :] = (gt * jax.nn.sigmoid(gt) * up).astype(BF16)
            acc[rows, :] += _nn(act_ref[rows, :], wd_ref[...].reshape(TN_FF, d))

        @pl.when(j == nc - 1)
        def _():
            ff = acc[...]
            ff_ref[...] = ff
            h2_ref[...] = h1_ref[...] + ff * _rstd(ff) * g_ref[...]

    rowblk = pl.BlockSpec((ts, d), lambda i, j: (i, 0))
    chunk = pl.BlockSpec((ts, TN_FF), lambda i, j: (i, j))
    return pl.pallas_call(
        body, grid=(s // ts, nc), name="ffn_fwd",
        out_shape=(jax.ShapeDtypeStruct((s, D_FF), BF16),) * 3 + (jax.ShapeDtypeStruct((s, d), F32),) * 2,
        in_specs=[rowblk, _spec_ff(0), _spec_ff(1), _spec_ff(2), rowblk, pl.BlockSpec((1, d), lambda i, j: (0, 0))],
        out_specs=(chunk, chunk, chunk, rowblk, rowblk),
        scratch_shapes=[pltpu.VMEM((ts, d), F32)],
        compiler_params=_params(2),
    )(hn2, wg, wu, wd, h1, g_post)


def _tail_fwd_bwd(h2, p, tgt, ff, wple, wpg, g_ple, g_ffn_post):
    s, d = h2.shape

    def body(h2_ref, p_ref, t_ref, ff_ref, wple_ref, wpg_ref, gple_ref, gfp_ref,
             dh2_ref, dff_ref, dgl_ref, dpp_ref, h2b_ref, pb_ref, loss_ref, dgple_ref, dgfp_ref):
        i = pl.program_id(0)

        @pl.when(i == 0)
        def _():
            loss_ref[...] = jnp.zeros_like(loss_ref)
            dgple_ref[...] = jnp.zeros_like(dgple_ref)
            dgfp_ref[...] = jnp.zeros_like(dgfp_ref)

        h2 = h2_ref[...]
        h2b = h2.astype(BF16)
        h2b_ref[...] = h2b
        pb = p_ref[...].astype(BF16)
        pb_ref[...] = pb
        pp = _nt(pb, wple_ref[...])
        gple = gple_ref[...]
        e = pp * _rstd(pp) * gple
        wpg = wpg_ref[...].reshape(d, d)
        sg = jax.nn.sigmoid(_nn(h2b, wpg))
        diff = h2 + sg * e - t_ref[...]
        sq = jnp.sum(jnp.sum(diff * diff, axis=1, keepdims=True), axis=0, keepdims=True)
        loss_ref[...] += jnp.broadcast_to(sq * (0.5 / d), loss_ref.shape)
        dh3 = diff * (1.0 / d)
        dgl = (dh3 * e * sg * (1.0 - sg)).astype(BF16)
        dgl_ref[...] = dgl
        dh2 = dh3 + _nt(dgl, wpg)
        dh2_ref[...] = dh2
        dpp, dg = _rms_bwd(pp, gple, dh3 * sg)
        dpp_ref[...] = dpp.astype(BF16)
        dgple_ref[...] += dg
        dff, dg = _rms_bwd(ff_ref[...], gfp_ref[...], dh2)
        dff_ref[...] = dff.astype(BF16)
        dgfp_ref[...] += dg

    rowblk = pl.BlockSpec((TS, d), _row)
    vec = pl.BlockSpec((1, d), _fixed)
    return pl.pallas_call(
        body, grid=(s // TS,), name="tail_fwd_bwd",
        out_shape=(jax.ShapeDtypeStruct((s, d), F32), jax.ShapeDtypeStruct((s, d), BF16),
                   jax.ShapeDtypeStruct((s, d), BF16), jax.ShapeDtypeStruct((s, d), BF16),
                   jax.ShapeDtypeStruct((s, d), BF16), jax.ShapeDtypeStruct((s, D_PLE), BF16),
                   jax.ShapeDtypeStruct((8, LANES), F32), jax.ShapeDtypeStruct((1, d), F32),
                   jax.ShapeDtypeStruct((1, d), F32)),
        in_specs=[rowblk, pl.BlockSpec((TS, D_PLE), _row), rowblk, rowblk,
                  pl.BlockSpec(wple.shape, _fixed), _spec_square(1), vec, vec],
        out_specs=(rowblk, rowblk, rowblk, rowblk, rowblk, pl.BlockSpec((TS, D_PLE), _row),
                   pl.BlockSpec((8, LANES), _fixed), vec, vec),
        compiler_params=_params(1),
    )(h2, p, tgt, ff, wple, wpg, g_ple, g_ffn_post)


def _ffn_bwd(dff, gate, up, wd, wg, wu, h1, dh2, g_pre):
    s, d = h1.shape
    nc = D_FF // TN_FF
    ts = min(TS_FF, s)

    def body(dff_ref, gate_ref, up_ref, wd_ref, wg_ref, wu_ref, h1_ref, dh2_ref, g_ref,
             dgate_ref, dup_ref, dh1_ref, dg_ref, acc):
        i = pl.program_id(0)
        j = pl.program_id(1)

        @pl.when((i == 0) & (j == 0))
        def _():
            dg_ref[...] = jnp.zeros_like(dg_ref)

        @pl.when(j == 0)
        def _():
            acc[...] = jnp.zeros_like(acc)

        for r in range(2):
            rows = slice(r * (ts // 2), (r + 1) * (ts // 2))
            dact = _nt(dff_ref[rows, :], wd_ref[...].reshape(TN_FF, d))
            gt = gate_ref[rows, :].astype(F32)
            sg = jax.nn.sigmoid(gt)
            dup_ref[rows, :] = (dact * gt * sg).astype(BF16)
            dgate_ref[rows, :] = (dact * up_ref[rows, :].astype(F32) * (sg * (1.0 + gt * (1.0 - sg)))).astype(BF16)
            acc[rows, :] += (_nn(dgate_ref[rows, :], wg_ref[...].reshape(TN_FF, d))
                             + _nn(dup_ref[rows, :], wu_ref[...].reshape(TN_FF, d)))

        @pl.when(j == nc - 1)
        def _():
            dv, dg = _rms_bwd(h1_ref[...], g_ref[...], acc[...])
            dh1_ref[...] = dh2_ref[...] + dv
            dg_ref[...] += dg

    rowblk = pl.BlockSpec((ts, d), lambda i, j: (i, 0))
    chunk = pl.BlockSpec((ts, TN_FF), lambda i, j: (i, j))
    vec = pl.BlockSpec((1, d), lambda i, j: (0, 0))
    return pl.pallas_call(
        body, grid=(s // ts, nc), name="ffn_bwd",
        out_shape=(jax.ShapeDtypeStruct((s, D_FF), BF16), jax.ShapeDtypeStruct((s, D_FF), BF16),
                   jax.ShapeDtypeStruct((s, d), F32), jax.ShapeDtypeStruct((1, d), F32)),
        in_specs=[rowblk, chunk, chunk, _spec_ff(2), _spec_ff(0), _spec_ff(1), rowblk, rowblk, vec],
        out_specs=(chunk, chunk, rowblk, vec),
        scratch_shapes=[pltpu.VMEM((ts, d), F32)],
        compiler_params=_params(2),
    )(dff, gate, up, wd, wg, wu, h1, dh2, g_pre)


def _post_attn_bwd(dh1, o, a, mpre, wout, wpool, g_post, g_attn, g_pool, pscale, send):
    s, d = dh1.shape
    sub = TS // TQ
    npc = len(send)

    def body(dh1_ref, o_ref, a_ref, mp_ref, wo_ref, wp_ref, gpost_ref, ga_ref, gp_ref, ps_ref, *refs):
        send_refs, refs = refs[:npc], refs[npc:]
        dob_ref, dat_ref, dlt_ref, dmpb_ref, dy_ref, dgpost_ref, dga_ref, dgp_ref, dps_ref = refs[:9]
        got_refs, (send_sems, recv_sems) = refs[9:9 + npc], refs[9 + npc:]
        i = pl.program_id(0)

        @pl.when(i == 0)
        def _():
            for cp in _pair_copies(send_refs, got_refs, send_sems, recv_sems):
                cp.start()
            dgpost_ref[...] = jnp.zeros_like(dgpost_ref)
            dga_ref[...] = jnp.zeros_like(dga_ref)
            dgp_ref[...] = jnp.zeros_like(dgp_ref)
            dps_ref[...] = jnp.zeros_like(dps_ref)

        do, dg = _rms_bwd(o_ref[...], gpost_ref[...], dh1_ref[...])
        dgpost_ref[...] += dg
        dob = do.astype(BF16)
        dob_ref[...] = dob
        dmix = _nt(dob, wo_ref[...].reshape(d, d))

        av = a_ref[...]
        da, dg = _rms_bwd(av, ga_ref[...], dmix[:, 0:D_ATTN])
        dga_ref[...] += dg
        dat = da.astype(BF16).T
        hsel = (lax.shift_right_logical(lax.broadcasted_iota(jnp.int32, (HEADS, D_ATTN), 1), 6)
                == lax.broadcasted_iota(jnp.int32, (HEADS, D_ATTN), 0)).astype(F32)
        dlt = lax.dot_general(hsel, da * av, (((1,), (1,)), ((), ())), precision=HIGHEST, preferred_element_type=F32)
        for q in range(sub):
            dlt_ref[q] = dlt[:, q * TQ:(q + 1) * TQ]
            dat_ref[q] = dat[:, q * TQ:(q + 1) * TQ]

        ps = ps_ref[...]
        mp = mp_ref[...]
        dm, dg = _rms_bwd(mp * ps, gp_ref[...], dmix[:, D_ATTN:])
        dgp_ref[...] += dg
        dps_ref[...] += jnp.sum(dm * mp, axis=0, keepdims=True)
        dmpb = (dm * ps).astype(BF16)
        dmpb_ref[...] = dmpb
        for g in range(len(POOL_WINDOWS)):
            cols = slice(g * POOL_CH, (g + 1) * POOL_CH)
            dy_ref[:, cols] = _nt(dmpb[:, cols], wp_ref[g])

        @pl.when(i == s // TS - 1)
        def _():
            for cp in _pair_copies(send_refs, got_refs, send_sems, recv_sems):
                cp.wait()

    rowblk = pl.BlockSpec((TS, d), _row)
    half = pl.BlockSpec((TS, D_ATTN), _row)
    vec = lambda n: pl.BlockSpec((1, n), _fixed)
    nk = N_DEV // 2
    res = pl.pallas_call(
        body, grid=(s // TS,), name="post_attn_bwd",
        out_shape=(jax.ShapeDtypeStruct((s, d), BF16), jax.ShapeDtypeStruct((s // TQ, D_ATTN, TQ), BF16),
                   jax.ShapeDtypeStruct((s // TQ, HEADS, TQ), F32), jax.ShapeDtypeStruct((s, D_POOL), BF16),
                   jax.ShapeDtypeStruct((s, D_POOL), F32), jax.ShapeDtypeStruct((1, d), F32),
                   jax.ShapeDtypeStruct((1, D_ATTN), F32), jax.ShapeDtypeStruct((1, D_POOL), F32),
                   jax.ShapeDtypeStruct((1, D_POOL), F32))
        + tuple(jax.ShapeDtypeStruct((nk,) + t.shape[1:], t.dtype) for t in send),
        in_specs=[rowblk, rowblk, half, half, _spec_square(0),
                  pl.BlockSpec(wpool.shape, lambda i: (0, 0, 0)), vec(d), vec(D_ATTN), vec(D_POOL), vec(D_POOL)]
        + [ANY] * npc,
        out_specs=(rowblk, pl.BlockSpec((sub, D_ATTN, TQ), lambda i: (i, 0, 0)),
                   pl.BlockSpec((sub, HEADS, TQ), lambda i: (i, 0, 0)), half, half,
                   vec(d), vec(D_ATTN), vec(D_POOL), vec(D_POOL)) + (ANY,) * npc,
        scratch_shapes=[pltpu.SemaphoreType.DMA((nk, npc)), pltpu.SemaphoreType.DMA((nk, npc))],
        compiler_params=_params(1),
    )(dh1, o, a, mpre, wout, wpool, g_post, g_attn, g_pool, pscale, *send)
    return res[:9], list(res[9:])


def _attn_bwd(ka, v, kt3, qat3, qt3, dot3, lset3, dlt3, chip_blocks, small_block):
    s = ka.shape[0]
    nq = s // TQ

    def body(ka_ref, v_ref, kt_ref, qat_ref, qt_ref, dot_ref, lset_ref, dlt_ref, b_ref, sm_ref,
             dqt_ref, dkt_ref, dvt_ref, got_ref, all_ref, pt_scr, ptb_scr, dsb_scr,
             stage, send_sems, recv_sems, local_sem, stage_s, send_s, recv_s, local_s):
        j = pl.program_id(0)

        @pl.when(j == 0)
        def _():
            _chips_start(b_ref, got_ref, stage, send_sems, recv_sems, local_sem)
            _gather_start(sm_ref, all_ref, stage_s, send_s, recv_s, local_s)
            dqt_ref[...] = jnp.zeros_like(dqt_ref)

        @pl.when(j == max(nq - 2, 0))
        def _():
            _gather_pass_on(all_ref, send_s, recv_s)

        def tile(i, masked):
            def accumulate(ref, idx, val):
                if masked:
                    ref[idx] = val
                else:
                    ref[idx] += val

            for h in range(HEADS):
                aug = slice(h * AUG, (h + 1) * AUG)
                st = _nn(ka_ref[:, aug], qat_ref[i, aug, :]) - lset_ref[i, h:h + 1, :]
                if masked:
                    st = jnp.where(_causal_in_tile(), st, NEG)
                pt = jnp.exp2(st)
                pt_scr[h] = pt
                ptb_scr[h] = pt.astype(BF16)
            heads = [(h, slice(h * HEAD_DIM, (h + 1) * HEAD_DIM)) for h in range(HEADS)]
            for h, hs in heads:
                dst = pt_scr[h] * (_nn(v_ref[:, hs], dot_ref[i, hs, :]) - dlt_ref[i, h:h + 1, :])
                dsb_scr[h] = dst.astype(BF16)
            for h, hs in heads:
                accumulate(dvt_ref, (0, hs, slice(None)), _nt(dot_ref[i, hs, :], ptb_scr[h]))
            for h, hs in heads:
                rows = slice(h * VROWS, (h + 1) * VROWS)
                accumulate(dkt_ref, (0, rows, slice(None)), _nt(qt_ref[i, rows, :], dsb_scr[h]))
            for h, hs in heads:
                rows = slice(h * VROWS, (h + 1) * VROWS)
                dqt_ref[i, rows, :] += _nn(kt_ref[0, rows, :], dsb_scr[h])

        first = j + 1
        pairs = (nq - first) // 2

        def step(p, carry):
            tile(first + 2 * p, False)
            tile(first + 2 * p + 1, False)
            return carry

        tile(j, True)
        lax.fori_loop(0, pairs, step, 0)

        @pl.when(first + 2 * pairs < nq)
        def _():
            tile(nq - 1, False)

        @pl.when(j == nq - 1)
        def _():
            _chips_finish(b_ref, got_ref, send_sems, recv_sems)
            _gather_finish(sm_ref, all_ref, send_s, recv_s)

    blk = pl.BlockSpec((TQ, D_ATTN), _row)
    tile_t = lambda rows: pl.BlockSpec((1, rows, TQ), lambda j: (j, 0, 0))
    per_tile = lambda rows: jax.ShapeDtypeStruct((nq, rows, TQ), F32)
    _, r, cdim = chip_blocks.shape
    dma = pltpu.SemaphoreType.DMA
    return pl.pallas_call(
        body, grid=(nq,), name="attn_bwd",
        out_shape=(per_tile(HEADS * VROWS), per_tile(HEADS * VROWS), per_tile(D_ATTN),
                   jax.ShapeDtypeStruct(chip_blocks.shape, chip_blocks.dtype),
                   jax.ShapeDtypeStruct((N_DEV,) + small_block.shape, small_block.dtype)),
        in_specs=[pl.BlockSpec((TQ, HEADS * AUG), _row), blk, tile_t(HEADS * VROWS),
                  VMEM_WHOLE, VMEM_WHOLE, VMEM_WHOLE, VMEM_WHOLE, VMEM_WHOLE, ANY, ANY],
        out_specs=(pl.BlockSpec((nq, HEADS * VROWS, TQ), lambda j: (0, 0, 0)), tile_t(HEADS * VROWS), tile_t(D_ATTN),
                   ANY, ANY),
        scratch_shapes=[pltpu.VMEM((HEADS, TQ, TQ), F32), pltpu.VMEM((HEADS, TQ, TQ), BF16),
                        pltpu.VMEM((HEADS, TQ, TQ), BF16), pltpu.VMEM((r, cdim), chip_blocks.dtype),
                        dma((3,)), dma((3,)), dma,
                        pltpu.VMEM(small_block.shape, small_block.dtype), dma((7,)), dma((7,)), dma],
        compiler_params=_params(1),
    )(ka, v, kt3, qat3, qt3, dot3, lset3, dlt3, chip_blocks, small_block)


def _pre_attn_bwd(dqt3, dkt3, dvt3, fl, dy, x, dh1, g1, wqkv, wf, wu):
    s, d = x.shape
    nt = s // TS
    n = TS + HALO
    sub = TS // TQ
    qkv, fcols = 3 * D_ATTN, 3 * D_ATTN + LANES

    def body(dqt_ref, dkt_ref, dvt_ref, fl_ref, dy_ref, x_ref, dh1_ref, g_ref, wqkv_ref, wf_ref, wu_ref,
             gx_ref, dz_ref, dg_ref, db_ref, ybuf, ccar, dlog, dsum):
        dqkv_ref = dz_ref.at[:, 0:qkv]
        dfb_ref = dz_ref.at[:, qkv:fcols]
        dub_ref = dz_ref.at[:, fcols:]
        i = pl.program_id(0)
        ti = nt - 1 - i

        @pl.when(i == 0)
        def _():
            ybuf[TS:n, :] = jnp.zeros((HALO, D_POOL), F32)
            ccar[...] = jnp.zeros_like(ccar)
            dg_ref[...] = jnp.zeros_like(dg_ref)
            db_ref[...] = jnp.zeros_like(db_ref)
            dsum[...] = jnp.zeros_like(dsum)

        for a in range(sub):
            for h in range(HEADS):
                r = h * VROWS + HEAD_DIM
                dsum[h:h + 1, a * TQ:(a + 1) * TQ] = dqt_ref[a, r:r + 1, :] - dkt_ref[a, r:r + 1, :]
        dlog[...] = ccar[...] + _running_sum(dsum[...].T, reverse=True)
        ccar[...] = dlog[0:1, :]
        df = dlog[...] * jax.nn.sigmoid(-fl_ref[...])
        db_ref[...] += jnp.sum(df, axis=0, keepdims=True)
        dfb = df.astype(BF16)
        dfb_ref[...] = dfb

        t = ti * TS + lax.broadcasted_iota(jnp.int32, (TS, 1), 0)
        dy = dy_ref[...]
        for g, w in enumerate(POOL_WINDOWS):
            cols = slice(g * POOL_CH, (g + 1) * POOL_CH)
            ybuf[0:TS, cols] = dy[:, cols] / jnp.minimum(t + 1, w).astype(F32)
        for g, w in enumerate(POOL_WINDOWS):
            cols = slice(g * POOL_CH, (g + 1) * POOL_CH)
            sm = ybuf[:, cols]
            step = 1
            while step < w:
                sm = sm + pltpu.roll(sm, n - step, 0)
                step *= 2
            dub_ref[:, cols] = (sm[0:TS, :] - dy[:, cols]).astype(BF16)
        ybuf[TS:n, :] = ybuf[0:HALO, :]

        for a in range(sub):
            rows = slice(a * TQ, (a + 1) * TQ)
            for h in range(HEADS):
                src = slice(h * VROWS, h * VROWS + HEAD_DIM)
                dqkv_ref[rows, h * HEAD_DIM:(h + 1) * HEAD_DIM] = (dqt_ref[a, src, :].T * 0.125).astype(BF16)
                dqkv_ref[rows, D_ATTN + h * HEAD_DIM:D_ATTN + (h + 1) * HEAD_DIM] = dkt_ref[a, src, :].T.astype(BF16)
            dqkv_ref[rows, 2 * D_ATTN:] = dvt_ref[a].T.astype(BF16)
        dhn = _nn(dqkv_ref[...], wqkv_ref[...]) + _nn(dfb, wf_ref[...]) + _nn(dub_ref[...], wu_ref[...])
        dx, dg = _rms_bwd(x_ref[...], g_ref[...], dhn)
        gx_ref[...] = dh1_ref[...] + dx
        dg_ref[...] += dg

    rev = lambda i: (nt - 1 - i, 0)
    blk = lambda w: pl.BlockSpec((TS, w), rev)
    return pl.pallas_call(
        body, grid=(nt,), name="pre_attn_bwd",
        out_shape=(jax.ShapeDtypeStruct((s, d), F32), jax.ShapeDtypeStruct((s, fcols + D_POOL), BF16),
                   jax.ShapeDtypeStruct((1, d), F32), jax.ShapeDtypeStruct((1, LANES), F32)),
        in_specs=[pl.BlockSpec((sub, HEADS * VROWS, TQ), lambda i: (nt - 1 - i, 0, 0)),
                  pl.BlockSpec((sub, HEADS * VROWS, TQ), lambda i: (nt - 1 - i, 0, 0)),
                  pl.BlockSpec((sub, D_ATTN, TQ), lambda i: (nt - 1 - i, 0, 0)),
                  blk(LANES), blk(D_POOL), blk(d), blk(d),
                  pl.BlockSpec((1, d), _fixed), pl.BlockSpec((qkv, d), _fixed), pl.BlockSpec(wf.shape, _fixed),
                  pl.BlockSpec(wu.shape, _fixed)],
        out_specs=(blk(d), blk(fcols + D_POOL), pl.BlockSpec((1, d), _fixed), pl.BlockSpec((1, LANES), _fixed)),
        scratch_shapes=[pltpu.VMEM((n, D_POOL), F32), pltpu.VMEM((1, LANES), F32), pltpu.VMEM((TS, LANES), F32),
                        pltpu.VMEM((LANES, TS), F32)],
        compiler_params=_params(1),
    )(dqt3, dkt3, dvt3, fl, dy, x, dh1, g1, wqkv, wf, wu)


def _wgrad(a, b, out_dtype, name):
    s, m = a.shape
    n = b.shape[1]
    tm = max(t for t in range(LANES, min(m, TM_WGRAD) + 1, LANES) if m % t == 0)
    ts = min(TS_WGRAD, s)
    ns = s // ts

    def body(a_ref, b_ref, o_ref, acc):
        i = pl.program_id(1)

        @pl.when(i == 0)
        def _():
            acc[...] = jnp.zeros_like(acc)

        acc[...] += _tn(a_ref[...], b_ref[...])

        @pl.when(i == ns - 1)
        def _():
            o_ref[...] = acc[...].astype(out_dtype)

    return pl.pallas_call(
        body, grid=(m // tm, ns), name=name, out_shape=jax.ShapeDtypeStruct((m, n), out_dtype),
        in_specs=[pl.BlockSpec((ts, tm), lambda j, i: (i, j)), pl.BlockSpec((ts, n), lambda j, i: (i, 0))],
        out_specs=pl.BlockSpec((tm, n), lambda j, i: (j, 0)),
        scratch_shapes=[pltpu.VMEM((tm, n), F32)],
        compiler_params=_params(2),
    )(a, b)


def _wgrad_in(dz, hn):
    s, m = dz.shape
    n = hn.shape[1]
    ts = min(TS_WGRAD, s)
    ns = s // ts
    pad_at, pad = 3 * D_ATTN + HEADS, LANES - HEADS
    assert m == D_IN + pad and N_DEV * SHARD_IN == D_IN

    def pieces(d):
        lo, hi = d * SHARD_IN, (d + 1) * SHARD_IN
        spans = [(lo, min(hi, pad_at), 0), (max(lo, pad_at), hi, pad)]
        return [(a + shift, b - a, a - lo) for a, b, shift in spans if b > a]

    def body(a_ref, b_ref, o_ref, acc, stage):
        i = pl.program_id(0)

        @pl.when(i == 0)
        def _():
            acc[...] = jnp.zeros_like(acc)

        acc[...] += _tn(a_ref[...], b_ref[...])

        @pl.when(i == ns - 1)
        def _():
            stage[SHARD_IN:ROWS_IN, :] = jnp.zeros((ROWS_IN - SHARD_IN, n), F32)
            for d in range(N_DEV):
                for src, rows, dst in pieces(d):
                    stage[dst:dst + rows, :] = acc[src:src + rows, :]
                o_ref[d] = stage[...].astype(BF16)

    return pl.pallas_call(
        body, grid=(ns,), name="wgrad_in", out_shape=jax.ShapeDtypeStruct((N_DEV, ROWS_IN, n), BF16),
        in_specs=[pl.BlockSpec((ts, m), _row), pl.BlockSpec((ts, n), _row)],
        out_specs=pl.BlockSpec((N_DEV, ROWS_IN, n), lambda i: (0, 0, 0)),
        scratch_shapes=[pltpu.VMEM((m, n), F32), pltpu.VMEM((ROWS_IN, n), F32)],
        compiler_params=_params(1),
    )(dz, hn)


def _adamw(w, g, m, v):
    m = ADAM_B1 * m + (1.0 - ADAM_B1) * g
    v = ADAM_B2 * v + (1.0 - ADAM_B2) * (g * g)
    m_hat = m / (1.0 - ADAM_B1 ** ADAM_STEP)
    v_hat = v / (1.0 - ADAM_B2 ** ADAM_STEP)
    delta = -ADAM_LR * (m_hat / (jnp.sqrt(v_hat) + ADAM_EPS) + ADAM_WD * w)
    return delta, m, v


def _sum_update(p_ref, w_ref, m_ref, v_ref, g_ref, d_ref, nm_ref, nv_ref):
    g = p_ref[0].astype(F32)
    for k in range(1, p_ref.shape[0]):
        g = g + p_ref[k].astype(F32)
    g_ref[...] = g
    d_ref[...], nm_ref[...], nv_ref[...] = _adamw(w_ref[...], g, m_ref[...], v_ref[...])


def _reduce_update_rest(parts, w, m, v, chip_blocks, small_block):
    nk, r, c = parts.shape
    ns = r // TR_REST

    def body(p_ref, w_ref, m_ref, v_ref, b_ref, sm_ref, g_ref, d_ref, nm_ref, nv_ref, got_ref, all_ref,
             stage_b, stage_s, send_b, recv_b, local_b, send_s, recv_s, local_s):
        i = pl.program_id(0)

        @pl.when(i == 0)
        def _():
            _chips_start(b_ref, got_ref, stage_b, send_b, recv_b, local_b)
            _gather_start(sm_ref, all_ref, stage_s, send_s, recv_s, local_s)

        _sum_update(p_ref, w_ref, m_ref, v_ref, g_ref, d_ref, nm_ref, nv_ref)

        @pl.when(i == ns - 1)
        def _():
            _gather_pass_on(all_ref, send_s, recv_s)
            _chips_finish(b_ref, got_ref, send_b, recv_b)
            _gather_finish(sm_ref, all_ref, send_s, recv_s)

    blk = pl.BlockSpec((TR_REST, c), _row)
    out = jax.ShapeDtypeStruct((r, c), F32)
    dma = pltpu.SemaphoreType.DMA
    return pl.pallas_call(
        body, grid=(ns,), name="reduce_update_rest",
        out_shape=(out,) * 4 + (jax.ShapeDtypeStruct(chip_blocks.shape, chip_blocks.dtype),
                                jax.ShapeDtypeStruct((N_DEV,) + small_block.shape, small_block.dtype)),
        in_specs=[pl.BlockSpec((nk, TR_REST, c), lambda i: (0, i, 0)), blk, blk, blk, ANY, ANY],
        out_specs=(blk,) * 4 + (ANY, ANY),
        scratch_shapes=[pltpu.VMEM(chip_blocks.shape[1:], chip_blocks.dtype), pltpu.VMEM(small_block.shape, small_block.dtype),
                        dma((3,)), dma((3,)), dma, dma((7,)), dma((7,)), dma],
        compiler_params=_params(1),
    )(parts, w, m, v, chip_blocks, small_block)


def _reduce_update_big(parts, w, m, v, tr, name):
    nk, r, c = parts.shape

    def body(p_ref, w_ref, m_ref, v_ref, g_ref, d_ref, nm_ref, nv_ref):
        _sum_update(p_ref, w_ref, m_ref, v_ref, g_ref, d_ref, nm_ref, nv_ref)

    blk = pl.BlockSpec((tr, c), _row)
    out = jax.ShapeDtypeStruct((r, c), F32)
    return pl.pallas_call(
        body, grid=(r // tr,), name=name, out_shape=(out,) * 4,
        in_specs=[pl.BlockSpec((nk, tr, c), lambda i: (0, i, 0)), blk, blk, blk],
        out_specs=(blk,) * 4, compiler_params=_params(1),
    )(parts, w, m, v)


def _reduce_update_small(parts, late, w, m, v):
    nd = parts.shape[0]
    first = parts.shape[1] - late.shape[1]

    def body(p_ref, q_ref, w_ref, m_ref, v_ref, g_ref, d_ref, nm_ref, nv_ref):
        g, t = p_ref[0], q_ref[0]
        for k in range(1, nd):
            g, t = g + p_ref[k], t + q_ref[k]
        g_ref[...] = g
        g_ref[first:, :] = g[first:, :] + t
        d_ref[...], nm_ref[...], nv_ref[...] = _adamw(w_ref[...], g_ref[...], m_ref[...], v_ref[...])

    out = jax.ShapeDtypeStruct(w.shape, F32)
    return pl.pallas_call(body, name="reduce_update_small", out_shape=(out,) * 4,
                          compiler_params=pltpu.CompilerParams(vmem_limit_bytes=VMEM_LIMIT))(parts, late, w, m, v)


MESH = pl.DeviceIdType.MESH


def _copy_through_vmem(src_hbm, dst_hbm, stage, sem):
    load = pltpu.make_async_copy(src_hbm, stage, sem)
    load.start()
    load.wait()
    store = pltpu.make_async_copy(stage, dst_hbm, sem)
    store.start()
    store.wait()


class _GatherPlan:
    def __init__(self, x_ref, out_ref, send_sems, recv_sems):
        x, y, c = lax.axis_index("x"), lax.axis_index("y"), lax.axis_index("c")
        self.me, self.sibling, self.c = (x, y, c), (x, y, 1 - c), c
        self.chips = [(1 - x, y), (x, 1 - y), (1 - x, 1 - y)]
        self.x_ref, self.out_ref, self.send_sems, self.recv_sems = x_ref, out_ref, send_sems, recv_sems

    def slot(self, px, py, pc):
        return self.out_ref.at[4 * px + 2 * py + pc]

    def copy(self, k, block, to, src=None):
        return pltpu.make_async_remote_copy(
            src_ref=self.slot(*block) if src is None else src, dst_ref=self.slot(*block),
            send_sem=self.send_sems.at[k], recv_sem=self.recv_sems.at[k], device_id=to, device_id_type=MESH)

    def first(self):
        return [self.copy(0, self.me, self.sibling, src=self.x_ref)] + [
            self.copy(1 + j, self.me, (*chip, self.c), src=self.x_ref) for j, chip in enumerate(self.chips)]

    def passed(self):
        return [self.copy(4 + j, (*chip, self.c), self.sibling) for j, chip in enumerate(self.chips)]


def _gather_start(x_ref, out_ref, stage, send_sems, recv_sems, local_sem):
    plan = _GatherPlan(x_ref, out_ref, send_sems, recv_sems)
    for cp in plan.first():
        cp.start()
    _copy_through_vmem(x_ref, plan.slot(*plan.me), stage, local_sem)


def _gather_pass_on(out_ref, send_sems, recv_sems):
    plan = _GatherPlan(None, out_ref, send_sems, recv_sems)
    passed = plan.passed()
    for j, chip in enumerate(plan.chips):
        plan.copy(1 + j, (*chip, plan.c), plan.me).wait_recv()
        passed[j].start()


def _gather_finish(x_ref, out_ref, send_sems, recv_sems):
    plan = _GatherPlan(x_ref, out_ref, send_sems, recv_sems)
    plan.copy(0, plan.sibling, plan.me).wait_recv()
    for j, chip in enumerate(plan.chips):
        plan.copy(4 + j, (*chip, 1 - plan.c), plan.me).wait_recv()
    for cp in plan.first() + plan.passed():
        cp.wait_send()


def _gather_w_in(xs):
    r, cdim = xs.shape
    qkv, f_end = 3 * D_ATTN, 3 * D_ATTN + HEADS

    def body(x_ref, wqkv_ref, wf_ref, wu_ref, send_sems, recv_sems, local_sem, blocks, flat):
        plan = _GatherPlan(x_ref, blocks, send_sems, recv_sems)
        for cp in plan.first():
            cp.start()
        own = pltpu.make_async_copy(x_ref, plan.slot(*plan.me), local_sem)
        own.start()
        own.wait()
        _gather_pass_on(blocks, send_sems, recv_sems)
        _gather_finish(x_ref, blocks, send_sems, recv_sems)
        for dev in range(N_DEV):
            flat[dev * SHARD_IN:(dev + 1) * SHARD_IN, :] = blocks[dev, 0:SHARD_IN, :].astype(F32)
        wqkv_ref[...] = flat[0:qkv, :].astype(BF16)
        wf_ref[...] = jnp.concatenate([flat[qkv:f_end, :], jnp.zeros((LANES - HEADS, cdim), F32)], axis=0).astype(BF16)
        wu_ref[...] = flat[f_end:D_IN, :].astype(BF16)

    shape = lambda rows: jax.ShapeDtypeStruct((rows, cdim), xs.dtype)
    dma = pltpu.SemaphoreType.DMA
    return pl.pallas_call(
        body, name="gather_w_in",
        out_shape=(shape(qkv), shape(LANES), shape(D_IN - f_end)),
        in_specs=[ANY], out_specs=(VMEM_WHOLE, VMEM_WHOLE, VMEM_WHOLE),
        scratch_shapes=[dma((7,)), dma((7,)), dma,
                        pltpu.VMEM((N_DEV, r, cdim), xs.dtype), pltpu.VMEM((D_IN, cdim), F32)],
        compiler_params=pltpu.CompilerParams(vmem_limit_bytes=VMEM_LIMIT),
    )(xs)


def _pair_copies(src_refs, dst_refs, send_sems, recv_sems):
    x, y, c = lax.axis_index("x"), lax.axis_index("y"), lax.axis_index("c")
    return [pltpu.make_async_remote_copy(
        src_ref=src.at[2 * k + (1 - c)], dst_ref=dst.at[k], send_sem=send_sems.at[k, p], recv_sem=recv_sems.at[k, p],
        device_id=(x, y, 1 - c), device_id_type=MESH)
        for k in range(N_DEV // 2) for p, (src, dst) in enumerate(zip(src_refs, dst_refs))]


def _rs_pair_sum(core, pieces, offsets, rows, name, landed=()):
    cdim = pieces[0].shape[2]
    nk = N_DEV // 2
    npc = len(pieces)
    nrem = npc - len(landed)
    spans = [(o, t.shape[1]) for t, o in zip(pieces, offsets)]
    ends = [o + n for o, n in spans]
    gaps = [(a, b - a) for a, b in zip(ends, [o for o, _ in spans[1:]] + [rows]) if b > a]

    def body(core_ref, *refs):
        own, src, got, o_ref = refs[:npc], refs[npc:npc + nrem], refs[npc + nrem:2 * npc], refs[2 * npc]
        landing, send_sems, recv_sems = refs[2 * npc + 1:]
        k = pl.program_id(0)
        x, y, c = lax.axis_index("x"), lax.axis_index("y"), lax.axis_index("c")

        def copies(kk):
            return [pltpu.make_async_remote_copy(
                src_ref=src[p].at[2 * kk + (1 - c)], dst_ref=landing.at[kk, pl.ds(o, n)],
                send_sem=send_sems.at[kk, p], recv_sem=recv_sems.at[kk, p], device_id=(x, y, 1 - c),
                device_id_type=MESH) for p, (o, n) in enumerate(spans[:nrem])]

        @pl.when(k == 0)
        def _():
            for kk in range(nk):
                for cp in copies(kk):
                    cp.start()

        for cp, piece, (o, n) in zip(copies(k), own, spans):
            cp.wait_recv()
            o_ref[0, o:o + n, :] = (piece[0].astype(F32) + landing[k, o:o + n, :].astype(F32)).astype(BF16)
        for theirs, piece, (o, n) in zip(got, own[nrem:], spans[nrem:]):
            o_ref[0, o:o + n, :] = (piece[0].astype(F32) + theirs[0].astype(F32)).astype(BF16)
        for o, n in gaps:
            o_ref[0, o:o + n, :] = jnp.zeros((n, cdim), BF16)

        @pl.when(k == nk - 1)
        def _():
            for kk in range(nk):
                for cp in copies(kk):
                    cp.wait_send()

    own_specs = [pl.BlockSpec((1, n, cdim), lambda k, core_ref: (2 * k + core_ref[0], 0, 0)) for _, n in spans]
    got_specs = [pl.BlockSpec((1, n, cdim), lambda k, core_ref: (k, 0, 0)) for _, n in spans[nrem:]]
    land_rows = max(o + n for o, n in spans[:nrem])
    return pl.pallas_call(
        body, name=name, out_shape=jax.ShapeDtypeStruct((nk, rows, cdim), BF16),
        grid_spec=pltpu.PrefetchScalarGridSpec(
            num_scalar_prefetch=1, grid=(nk,),
            in_specs=own_specs + [ANY] * nrem + got_specs,
            out_specs=pl.BlockSpec((1, rows, cdim), lambda k, core_ref: (k, 0, 0)),
            scratch_shapes=[pltpu.VMEM((nk, land_rows, cdim), BF16), pltpu.SemaphoreType.DMA((nk, nrem)),
                            pltpu.SemaphoreType.DMA((nk, nrem))]),
        compiler_params=_params(1),
    )(core, *pieces, *pieces[:nrem], *landed)


def _chips_start(b_ref, out_ref, stage, send_sems, recv_sems, local_sem):
    x, y, c = lax.axis_index("x"), lax.axis_index("y"), lax.axis_index("c")
    mychip = 2 * x + y
    for j, (px, py) in enumerate([(1 - x, y), (x, 1 - y), (1 - x, 1 - y)]):
        pltpu.make_async_remote_copy(
            src_ref=b_ref.at[2 * px + py], dst_ref=out_ref.at[mychip],
            send_sem=send_sems.at[j], recv_sem=recv_sems.at[j], device_id=(px, py, c), device_id_type=MESH).start()
    _copy_through_vmem(b_ref.at[mychip], out_ref.at[mychip], stage, local_sem)


def _chips_finish(b_ref, out_ref, send_sems, recv_sems):
    x, y, c = lax.axis_index("x"), lax.axis_index("y"), lax.axis_index("c")
    for j, (px, py) in enumerate([(1 - x, y), (x, 1 - y), (1 - x, 1 - y)]):
        pltpu.make_async_remote_copy(
            src_ref=b_ref.at[2 * px + py], dst_ref=out_ref.at[2 * px + py],
            send_sem=send_sems.at[j], recv_sem=recv_sems.at[j], device_id=(px, py, c), device_id_type=MESH).wait()


def _pad_rows(a, rows):
    return jnp.pad(a, ((0, rows - a.shape[0]), (0, 0)))


def _pack_in(w_in):
    return _pad_rows(w_in[0].T, ROWS_IN)


def _unpack_in(r):
    return r[0:SHARD_IN].T[None]


def _pack_rest(w_out, w_gate, w_up, w_down, w_ple, w_pg):
    head = _pad_rows(jnp.concatenate([w_out[0], w_pg[0], w_ple[0].T.reshape(ROWS_PLE, D_MODEL)], axis=0), OFF_GATE)
    return jnp.concatenate([head, w_gate[0].T, w_up[0].T, w_down[0]], axis=0)


def _unpack_rest(r):
    return (r[0:OFF_PG][None], r[OFF_GATE:OFF_UP].T[None], r[OFF_UP:OFF_DOWN].T[None], r[OFF_DOWN:ROWS_REST][None],
            r[OFF_PLE:OFF_PLE + ROWS_PLE].reshape(SHARD_SQ, D_PLE).T[None], r[OFF_PG:OFF_PLE][None])


def _pack_small(w_pool, g_mix_pre, g_mix_post, g_ffn_pre, g_ffn_post, g_ple, g_attn, g_pool, pool_scale, b_forget,
                loss=None):
    row = lambda vrow: vrow.reshape(1, -1)
    misc = [row(pool_scale), row(b_forget), row(loss) if loss is not None else jnp.zeros((1, 1), F32),
            jnp.zeros((1, D_MODEL - COL_LOSS - 1), F32)]
    rows = [w_pool.reshape(64, D_MODEL), row(g_mix_pre), row(g_mix_post), row(g_ffn_pre), row(g_ffn_post), row(g_ple),
            jnp.concatenate([row(g_attn), row(g_pool)], axis=1), jnp.concatenate(misc, axis=1),
            jnp.zeros((SMALL_ROWS - ROW_MISC - 1, D_MODEL), F32)]
    return jnp.concatenate(rows, axis=0)


def _pack_small_late(g_mix_pre, b_forget):
    misc = [jnp.zeros((1, COL_B_FORGET), F32), b_forget.reshape(1, -1), jnp.zeros((1, D_MODEL - COL_LOSS), F32)]
    return jnp.concatenate([g_mix_pre.reshape(1, -1), jnp.zeros((ROW_MISC - ROW_G_MIX_PRE - 1, D_MODEL), F32),
                            jnp.concatenate(misc, axis=1), jnp.zeros((SMALL_ROWS - ROW_MISC - 1, D_MODEL), F32)], axis=0)


def _unpack_small(r):
    gains, misc = r[ROW_GROUP_GAINS:ROW_GROUP_GAINS + 1], r[ROW_MISC:ROW_MISC + 1]
    return dict(
        w_pool=r[0:64].reshape(1, 4, POOL_CH, POOL_CH), g_mix_pre=r[ROW_G_MIX_PRE:ROW_G_MIX_PRE + 1],
        g_mix_post=r[ROW_G_MIX_POST:ROW_G_MIX_POST + 1], g_ffn_pre=r[ROW_G_FFN_PRE:ROW_G_FFN_PRE + 1],
        g_ffn_post=r[ROW_G_FFN_POST:ROW_G_FFN_POST + 1], g_ple=r[ROW_G_PLE:ROW_G_PLE + 1],
        g_attn_grp=gains[:, 0:D_ATTN], g_pool_grp=gains[:, D_ATTN:D_ATTN + D_POOL],
        pool_scale=misc[:, 0:D_POOL], b_forget=misc[:, COL_B_FORGET:COL_B_FORGET + HEADS])


def _step(x, p, tgt, small, in_w, in_m, in_v, rest_w, rest_m, rest_v):
    core = lax.axis_index("c").astype(jnp.int32).reshape(1)
    wqkv, wf, wu = _gather_w_in(in_w.astype(BF16))
    wpool = small["w_pool"].astype(BF16)
    bpad = jnp.pad(small["b_forget"], ((0, 0), (0, LANES - HEADS)))

    lay = _attn_layout_constants()
    rest_b = rest_w.astype(BF16)
    hn, qt3, ka, v, qat3, vt3, kt3, fl, y, mpre, gh = _pre_attn_fwd(x, small["g_mix_pre"], wqkv, wf, wu, bpad, wpool, lay,
                                                                 rest_b[0:OFF_GATE])
    a, lset3, gf = _attn_fwd(ka, qat3, vt3, rest_b[OFF_GATE:])
    wple_t = gh[:, OFF_PLE:OFF_PLE + ROWS_PLE].reshape(D_MODEL, D_PLE)
    mix, o, h1, hn2 = _post_attn_fwd(a, mpre, x, small["g_attn_grp"], small["g_pool_grp"], small["pool_scale"], gh,
                                     small["g_mix_post"], small["g_ffn_pre"])
    gate, up, act, ff, h2 = _ffn_fwd(hn2, gf, gf, gf, h1, small["g_ffn_post"])
    dh2, dff, dgl, dpp, h2b, pb, loss8, dg_ple, dg_ffn_post = _tail_fwd_bwd(
        h2, p, tgt, ff, wple_t, gh, small["g_ple"], small["g_ffn_post"])
    dgate, dup, dh1, dg_ffn_pre = _ffn_bwd(dff, gate, up, gf, gf, gf, h1, dh2, small["g_ffn_pre"])
    nd = N_DEV
    send_rest = [
        _wgrad(h2b, dgl, BF16, "wgrad_ple_gate").reshape(nd, SHARD_SQ, D_MODEL),
        _wgrad(dpp, pb, BF16, "wgrad_ple").reshape(nd, ROWS_PLE, D_MODEL),
        _wgrad(dgate, hn2, BF16, "wgrad_gate").reshape(nd, SHARD_FF, D_MODEL),
        _wgrad(dup, hn2, BF16, "wgrad_up").reshape(nd, SHARD_FF, D_MODEL),
        _wgrad(act, dff, BF16, "wgrad_down").reshape(nd, SHARD_FF, D_MODEL)]
    (dob, dat3, dlt3, dmpb, dy, dg_mix_post, dg_attn, dg_pool, dps), landed = _post_attn_bwd(
        dh1, o, a, mpre, gh, wpool, small["g_mix_post"], small["g_attn_grp"], small["g_pool_grp"], small["pool_scale"],
        send_rest)
    send_rest = [_wgrad(mix, dob, BF16, "wgrad_out").reshape(nd, SHARD_SQ, D_MODEL)] + send_rest
    pair_rest = _rs_pair_sum(core, send_rest, [0, OFF_PG, OFF_PLE, OFF_GATE, OFF_UP, OFF_DOWN], ROWS_REST,
                             "rs_pair_sum_rest", landed)

    dwp = _wgrad(y, dmpb, F32, "wgrad_pool")
    dw_pool = jnp.stack([dwp[g * POOL_CH:(g + 1) * POOL_CH, g * POOL_CH:(g + 1) * POOL_CH] for g in range(4)])
    small_part = _pack_small(dw_pool, jnp.zeros((1, D_MODEL), F32), dg_mix_post, dg_ffn_pre, dg_ffn_post, dg_ple,
                             dg_attn, dg_pool, dps, jnp.zeros((1, HEADS), F32), loss8[0:1, 0:1])
    dqt3, dkt3, dvt3, chips_rest, small_all = _attn_bwd(ka, v, kt3, qat3, qt3, dat3, lset3, dlt3, pair_rest, small_part)

    gx, dz, dg_mix_pre, db = _pre_attn_bwd(dqt3, dkt3, dvt3, fl, dy, x, dh1, small["g_mix_pre"], wqkv, wf, wu)

    pair_in = _rs_pair_sum(core, [_wgrad_in(dz, hn)], [0], ROWS_IN, "rs_pair_sum_in")

    small_late = _pack_small_late(dg_mix_pre, db[:, 0:HEADS])
    *upd_rest, chips_in, late_all = _reduce_update_rest(chips_rest, rest_w, rest_m, rest_v, pair_in, small_late)
    upd_in = _reduce_update_big(chips_in, in_w, in_m, in_v, ROWS_IN, "reduce_update_in")
    return gx, (small_all, late_all), upd_in, upd_rest


def kernel(x, p, g_mix_pre, w_in, b_forget, g_attn_grp, g_pool_grp, w_pool, pool_scale, w_out, g_mix_post, g_ffn_pre, w_ffn_gate, w_ffn_up, w_ffn_down, g_ffn_post, w_ple_proj, g_ple, w_ple_gate, loss_target, m_g_mix_pre, m_w_in, m_b_forget, m_g_attn_grp, m_g_pool_grp, m_w_pool, m_pool_scale, m_w_out, m_g_mix_post, m_g_ffn_pre, m_w_ffn_gate, m_w_ffn_up, m_w_ffn_down, m_g_ffn_post, m_w_ple_proj, m_g_ple, m_w_ple_gate, v_g_mix_pre, v_w_in, v_b_forget, v_g_attn_grp, v_g_pool_grp, v_w_pool, v_pool_scale, v_w_out, v_g_mix_post, v_g_ffn_pre, v_w_ffn_gate, v_w_ffn_up, v_w_ffn_down, v_g_ffn_post, v_w_ple_proj, v_g_ple, v_w_ple_gate):
    small = dict(w_pool=w_pool[0], g_mix_pre=g_mix_pre, g_mix_post=g_mix_post, g_ffn_pre=g_ffn_pre,
                 g_ffn_post=g_ffn_post, g_ple=g_ple, g_attn_grp=g_attn_grp, g_pool_grp=g_pool_grp,
                 pool_scale=pool_scale, b_forget=b_forget)
    gx, small_all, upd_in, upd_rest = _step(
        x[0], p[0, 0], loss_target[0], small, _pack_in(w_in), _pack_in(m_w_in), _pack_in(v_w_in),
        _pack_rest(w_out, w_ffn_gate, w_ffn_up, w_ffn_down, w_ple_proj, w_ple_gate),
        _pack_rest(m_w_out, m_w_ffn_gate, m_w_ffn_up, m_w_ffn_down, m_w_ple_proj, m_w_ple_gate),
        _pack_rest(v_w_out, v_w_ffn_gate, v_w_ffn_up, v_w_ffn_down, v_w_ple_proj, v_w_ple_gate))

    sm_w = _pack_small(w_pool, g_mix_pre, g_mix_post, g_ffn_pre, g_ffn_post, g_ple, g_attn_grp, g_pool_grp, pool_scale, b_forget)
    sm_m = _pack_small(m_w_pool, m_g_mix_pre, m_g_mix_post, m_g_ffn_pre, m_g_ffn_post, m_g_ple, m_g_attn_grp, m_g_pool_grp, m_pool_scale, m_b_forget)
    sm_v = _pack_small(v_w_pool, v_g_mix_pre, v_g_mix_post, v_g_ffn_pre, v_g_ffn_post, v_g_ple, v_g_attn_grp, v_g_pool_grp, v_pool_scale, v_b_forget)
    upd_small = _reduce_update_small(*small_all, sm_w, sm_m, sm_v)
    loss = upd_small[0][ROW_MISC, COL_LOSS]

    def leaves(k):
        b_out, b_gate, b_up, b_down, b_ple, b_pg = _unpack_rest(upd_rest[k])
        s = _unpack_small(upd_small[k])
        return (s["g_mix_pre"], _unpack_in(upd_in[k]), s["b_forget"], s["g_attn_grp"], s["g_pool_grp"], s["w_pool"],
                s["pool_scale"], b_out, s["g_mix_post"], s["g_ffn_pre"], b_gate, b_up, b_down, s["g_ffn_post"], b_ple,
                s["g_ple"], b_pg)

    return (loss, gx[None], *leaves(0), *leaves(1), *leaves(2), *leaves(3))
```

```python
import functools

import jax
import jax.numpy as jnp
from jax import lax
from jax.experimental import pallas as pl
from jax.experimental.pallas import tpu as pltpu

F32 = jnp.float32
BF16 = jnp.bfloat16
HIGHEST = lax.Precision.HIGHEST

D_MODEL = 1024
HEADS = 8
HEAD_DIM = 64
D_ATTN = HEADS * HEAD_DIM
POOL_WINDOWS = (2, 4, 8, 16)
POOL_CH = 128
D_POOL = POOL_CH * len(POOL_WINDOWS)
D_FF = 2816
D_PLE = 256
D_IN = 3 * D_ATTN + HEADS + D_POOL
RMS_EPS = 1e-6
N_DEV = 8

ADAM_LR = 0.001
ADAM_B1 = 0.9
ADAM_B2 = 0.999
ADAM_EPS = 1e-08
ADAM_WD = 0.01
ADAM_STEP = 10

LANES = 128
HALO = 16
TS = 512
TS_FF = 512
TS_WGRAD = 1024
TM_WGRAD = 2176
TQ = 256
TN_FF = 1408
NEG = -1e30
VMEM_LIMIT = 56 * 1024 * 1024

SHARD_IN = 257
ROWS_IN = 272
SHARD_FF = 352
SHARD_SQ = D_MODEL // N_DEV
ROWS_PLE = D_PLE * SHARD_SQ // D_MODEL
OFF_PG = SHARD_SQ
OFF_PLE = 2 * SHARD_SQ
OFF_GATE = SHARD_FF
OFF_UP = 2 * SHARD_FF
OFF_DOWN = 3 * SHARD_FF
ROWS_REST = 4 * SHARD_FF
TR_REST = SHARD_FF

SMALL_ROWS = 72
ROW_G_MIX_PRE, ROW_G_MIX_POST, ROW_G_FFN_PRE, ROW_G_FFN_POST, ROW_G_PLE = 64, 65, 66, 67, 68
ROW_GROUP_GAINS, ROW_MISC = 69, 70
COL_B_FORGET = D_POOL
COL_LOSS = D_POOL + HEADS


def _nn(a, b):
    return jnp.dot(a, b, preferred_element_type=F32)


def _nt(a, b):
    return lax.dot_general(a, b, (((1,), (1,)), ((), ())), preferred_element_type=F32)


def _tn(a, b):
    return lax.dot_general(a, b, (((0,), (0,)), ((), ())), preferred_element_type=F32)


def _rstd(v):
    return lax.rsqrt(jnp.mean(v * v, axis=-1, keepdims=True) + RMS_EPS)


def _rms_bwd(v, g, dy):
    r = _rstd(v)
    vh = v * r
    t = dy * g
    dv = r * (t - vh * jnp.mean(t * vh, axis=-1, keepdims=True))
    return dv, jnp.sum(dy * vh, axis=0, keepdims=True)


def _split3(v):
    hi = v.astype(BF16)
    rest = v - hi.astype(F32)
    mid = rest.astype(BF16)
    return hi, mid, (rest - mid.astype(F32)).astype(BF16)


def _mask_matmul(mask, v):
    hi, mid, lo = _split3(v)
    return _nn(mask, lo) + _nn(mask, mid) + _nn(mask, hi)


def _running_sum(v, reverse=False):
    tq = v.shape[0] // 2
    rr = lax.broadcasted_iota(jnp.int32, (tq, tq), 0)
    cc = lax.broadcasted_iota(jnp.int32, (tq, tq), 1)
    mask = ((cc >= rr) if reverse else (cc <= rr)).astype(BF16)
    top, bot = _mask_matmul(mask, v[0:tq]), _mask_matmul(mask, v[tq:])
    if reverse:
        top = top + bot[0:1, :]
    else:
        bot = bot + top[tq - 1:tq, :]
    return jnp.concatenate([top, bot], axis=0)


def _params(n_grid):
    return pltpu.CompilerParams(dimension_semantics=("arbitrary",) * n_grid, vmem_limit_bytes=VMEM_LIMIT)


def _row(i):
    return (i, 0)


def _fixed(*_):
    return (0, 0)


def _spec_square(part):
    return pl.BlockSpec((N_DEV, SHARD_SQ, D_MODEL), lambda *_: (0, part, 0))


def _spec_ff(part):
    return pl.BlockSpec((TN_FF // SHARD_FF, SHARD_FF, D_MODEL), lambda i, j: (j, part, 0))


assert TS == 2 * TQ and TN_FF % SHARD_FF == 0
_HALVES = (slice(0, TQ), slice(TQ, TS))

VMEM_WHOLE = pl.BlockSpec(memory_space=pltpu.VMEM)
SMEM_WHOLE = pl.BlockSpec(memory_space=pltpu.SMEM)
ANY = pl.BlockSpec(memory_space=pl.ANY)


LOG2E = 1.4426950408889634
VROWS = HEAD_DIM + 16
AUG = 128
BIAS_LANE = HEAD_DIM
ONE_LANE = HEAD_DIM + 3
SPARE_LANE = HEADS
PART_LANES = 16
assert SPARE_LANE < PART_LANES and 3 * PART_LANES <= LANES


def _attn_layout_constants():
    import numpy as np
    bias_k = np.zeros((LANES, HEADS * AUG), np.float32)
    bias_q = np.zeros((LANES, HEADS * AUG), np.float32)
    for h in range(HEADS):
        for part in range(3):
            bias_k[part * PART_LANES + h, h * AUG + BIAS_LANE + part] = -1.0
            bias_q[part * PART_LANES + h, h * AUG + ONE_LANE + part] = 1.0
            bias_k[SPARE_LANE, h * AUG + ONE_LANE + part] = 1.0
            bias_q[SPARE_LANE, h * AUG + BIAS_LANE + part] = 1.0
    after = np.concatenate([np.arange(h * AUG + HEAD_DIM, (h + 1) * AUG) for h in range(HEADS)])
    as_bf = lambda a: jnp.asarray(a, BF16)
    return dict(bias_k=as_bf(bias_k[:, after]), bias_q_t=as_bf(bias_q[:, after].T))


def _pre_attn_fwd(x, g1, wqkv, wf, wu, bpad, wpool, lay, own_block):
    s, d = x.shape
    nt = s // TS
    sub = TS // TQ

    def body(x_ref, g_ref, wqkv_ref, wf_ref, wu_ref, b_ref, wp_ref, bk_ref, bqt_ref, own_ref,
             hn_ref, qt_ref, ka_ref, v_ref, qat_ref, vt_ref, kt_ref, fl_ref, y_ref, mp_ref, all_ref,
             ubuf, ccar, cbuf, stage, send_sems, recv_sems, local_sem):
        i = pl.program_id(0)

        @pl.when(i == 0)
        def _():
            _gather_start(own_ref, all_ref, stage, send_sems, recv_sems, local_sem)
            ubuf[0:HALO, :] = jnp.zeros((HALO, D_POOL), F32)
            ccar[...] = jnp.zeros_like(ccar)

        @pl.when(i == max(nt - 2, 0))
        def _():
            _gather_pass_on(all_ref, send_sems, recv_sems)

        xv = x_ref[...]
        hn = (xv * _rstd(xv) * g_ref[...]).astype(BF16)
        hn_ref[...] = hn
        zq = _nt(hn, wqkv_ref[...])
        qt = (zq[:, 0:D_ATTN] * 0.125).astype(BF16).T
        qb = (zq[:, 0:D_ATTN] * (0.125 * LOG2E)).astype(BF16)
        kb = zq[:, D_ATTN:2 * D_ATTN].astype(BF16)
        vb = zq[:, 2 * D_ATTN:3 * D_ATTN].astype(BF16)
        v_ref[...] = vb

        fl = _nt(hn, wf_ref[...]) + b_ref[...]
        fl_ref[...] = fl
        logf = jax.nn.log_sigmoid(fl)
        c = _running_sum(logf) + ccar[...]
        cbuf[...] = c
        ccar[...] = cbuf[TS - 1:TS, :]
        hi, mid, lo = (part.astype(F32) for part in _split3(c * LOG2E))
        lane = lax.broadcasted_iota(jnp.int32, (TS, LANES), 1)
        later = jnp.where(lane < 2 * PART_LANES, pltpu.roll(mid, PART_LANES, 1), pltpu.roll(lo, 2 * PART_LANES, 1))
        parts = jnp.where(lane < PART_LANES, jnp.where(lane == SPARE_LANE, 1.0, hi), later).astype(BF16)
        extra = AUG - HEAD_DIM
        kbias = _nn(parts, bk_ref[...]).astype(BF16)
        for h in range(HEADS):
            ka_ref[:, h * AUG:h * AUG + HEAD_DIM] = kb[:, h * HEAD_DIM:(h + 1) * HEAD_DIM]
            ka_ref[:, h * AUG + HEAD_DIM:(h + 1) * AUG] = kbias[:, h * extra:(h + 1) * extra]
        qbt = qb.T
        qbias = _nt(bqt_ref[...], parts).astype(BF16)
        vt = vb.T
        kt = kb.T
        for a in range(sub):
            cols = slice(a * TQ, (a + 1) * TQ)
            for h in range(HEADS):
                qat_ref[a, h * AUG:h * AUG + HEAD_DIM, :] = qbt[h * HEAD_DIM:(h + 1) * HEAD_DIM, cols]
                qat_ref[a, h * AUG + HEAD_DIM:(h + 1) * AUG, :] = qbias[h * extra:(h + 1) * extra, cols]
            for ref, mat in ((qt_ref, qt), (kt_ref, kt), (vt_ref, vt)):
                for h in range(HEADS):
                    ref[a, h * VROWS:h * VROWS + HEAD_DIM, :] = mat[h * HEAD_DIM:(h + 1) * HEAD_DIM, cols]
                    ref[a, h * VROWS + HEAD_DIM:(h + 1) * VROWS, :] = jnp.ones((VROWS - HEAD_DIM, TQ), BF16)

        u = _nt(hn, wu_ref[...])
        ubuf[HALO:HALO + TS, :] = u
        t = i * TS + lax.broadcasted_iota(jnp.int32, (TS, 1), 0)
        for g, w in enumerate(POOL_WINDOWS):
            cols = slice(g * POOL_CH, (g + 1) * POOL_CH)
            sm = ubuf[:, cols]
            step = 1
            while step < w:
                sm = sm + pltpu.roll(sm, step, 0)
                step *= 2
            cnt = jnp.minimum(t + 1, w).astype(F32)
            yg = (sm[HALO:, :] / cnt - u[:, cols]).astype(BF16)
            y_ref[:, cols] = yg
            mp_ref[:, cols] = _nn(yg, wp_ref[g])
        ubuf[0:HALO, :] = u[TS - HALO:, :]

        @pl.when(i == nt - 1)
        def _():
            _gather_finish(own_ref, all_ref, send_sems, recv_sems)

    nq = s // TQ
    aug = HEADS * AUG
    outs = (
        jax.ShapeDtypeStruct((s, d), BF16), jax.ShapeDtypeStruct((nq, HEADS * VROWS, TQ), BF16),
        jax.ShapeDtypeStruct((s, aug), BF16), jax.ShapeDtypeStruct((s, D_ATTN), BF16),
        jax.ShapeDtypeStruct((nq, aug, TQ), BF16), jax.ShapeDtypeStruct((nq, HEADS * VROWS, TQ), BF16),
        jax.ShapeDtypeStruct((nq, HEADS * VROWS, TQ), BF16),
        jax.ShapeDtypeStruct((s, LANES), F32),
        jax.ShapeDtypeStruct((s, D_POOL), BF16), jax.ShapeDtypeStruct((s, D_POOL), F32),
        jax.ShapeDtypeStruct((N_DEV,) + own_block.shape, own_block.dtype),
    )
    fixed3 = lambda i: (0, 0, 0)
    tiles3 = lambda rows: pl.BlockSpec((sub, rows, TQ), lambda i: (i, 0, 0))
    return pl.pallas_call(
        body, grid=(nt,), out_shape=outs, name="pre_attn_fwd",
        in_specs=[pl.BlockSpec((TS, d), _row), pl.BlockSpec((1, d), _fixed),
                  pl.BlockSpec((3 * D_ATTN, d), _fixed), pl.BlockSpec(wf.shape, _fixed), pl.BlockSpec(wu.shape, _fixed),
                  pl.BlockSpec((1, LANES), _fixed), pl.BlockSpec(wpool.shape, fixed3),
                  pl.BlockSpec(lay["bias_k"].shape, _fixed), pl.BlockSpec(lay["bias_q_t"].shape, _fixed), ANY],
        out_specs=(pl.BlockSpec((TS, d), _row), tiles3(HEADS * VROWS),
                   pl.BlockSpec((TS, aug), _row), pl.BlockSpec((TS, D_ATTN), _row),
                   tiles3(aug), tiles3(HEADS * VROWS), tiles3(HEADS * VROWS),
                   pl.BlockSpec((TS, LANES), _row),
                   pl.BlockSpec((TS, D_POOL), _row), pl.BlockSpec((TS, D_POOL), _row), ANY),
        scratch_shapes=[pltpu.VMEM((TS + HALO, D_POOL), F32), pltpu.VMEM((1, LANES), F32), pltpu.VMEM((TS, LANES), F32),
                        pltpu.VMEM(own_block.shape, own_block.dtype),
                        pltpu.SemaphoreType.DMA((7,)), pltpu.SemaphoreType.DMA((7,)), pltpu.SemaphoreType.DMA],
        compiler_params=_params(1),
    )(x, g1, wqkv, wf, wu, bpad, wpool, lay["bias_k"], lay["bias_q_t"], own_block)


def _causal_in_tile():
    krow = lax.broadcasted_iota(jnp.int32, (TQ, TQ), 0)
    qcol = lax.broadcasted_iota(jnp.int32, (TQ, TQ), 1)
    return krow <= qcol


def _attn_fwd(ka, qat3, vt3, own_block):
    s = ka.shape[0]
    nq = s // TQ
    pass_on_step = max(nq - 2, 0)

    def body(qa_ref, ka_ref, vt_ref, own_ref, a_ref, lset_ref, all_ref, acc, out_t, st_scr, pt_scr,
             stage, send_sems, recv_sems, local_sem):
        i = pl.program_id(0)

        @pl.when(i == 0)
        def _():
            _gather_start(own_ref, all_ref, stage, send_sems, recv_sems, local_sem)

        @pl.when(i == pass_on_step)
        def _():
            _gather_pass_on(all_ref, send_sems, recv_sems)

        acc[...] = jnp.zeros_like(acc)

        def tile(j, stats, masked):
            tile_max = []
            for h in range(HEADS):
                aug = slice(h * AUG, (h + 1) * AUG)
                st = _nn(ka_ref[pl.ds(j * TQ, TQ), aug], qa_ref[0, aug, :])
                if masked:
                    st = jnp.where(_causal_in_tile(), st, NEG)
                st_scr[h] = st
                tile_max.append(jnp.max(st, axis=0, keepdims=True))
            new, scale = [], []
            for h in range(HEADS):
                m_new = jnp.maximum(stats[h], tile_max[h])
                scale.append(jnp.exp2(stats[h] - m_new))
                pt_scr[h] = jnp.exp2(st_scr[h] - m_new).astype(BF16)
                new.append(m_new)
            for h in range(HEADS):
                rows = slice(h * VROWS, (h + 1) * VROWS)
                acc[rows, :] = scale[h] * acc[rows, :] + _nn(vt_ref[j, rows, :], pt_scr[h])
            return tuple(new)

        init = tuple(jnp.full((1, TQ), NEG, F32) for _ in range(HEADS))
        stats = lax.fori_loop(0, i, functools.partial(tile, masked=False), init)
        stats = tile(i, stats, True)
        for h in range(HEADS):
            denom = acc[h * VROWS + HEAD_DIM:h * VROWS + HEAD_DIM + 1, :]
            out_t[h * HEAD_DIM:(h + 1) * HEAD_DIM, :] = acc[h * VROWS:h * VROWS + HEAD_DIM, :] / denom
            lset_ref[0, h:h + 1, :] = stats[h] + jnp.log2(denom)
        a_ref[...] = out_t[...].T

        @pl.when(i == nq - 1)
        def _():
            _gather_finish(own_ref, all_ref, send_sems, recv_sems)

    r, cdim = own_block.shape
    return pl.pallas_call(
        body, grid=(nq,), name="attn_fwd",
        out_shape=(jax.ShapeDtypeStruct((s, D_ATTN), F32), jax.ShapeDtypeStruct((nq, HEADS, TQ), F32),
                   jax.ShapeDtypeStruct((N_DEV, r, cdim), own_block.dtype)),
        in_specs=[pl.BlockSpec((1, HEADS * AUG, TQ), lambda i: (i, 0, 0)), VMEM_WHOLE, VMEM_WHOLE, ANY],
        out_specs=(pl.BlockSpec((TQ, D_ATTN), _row), pl.BlockSpec((1, HEADS, TQ), lambda i: (i, 0, 0)), ANY),
        scratch_shapes=[pltpu.VMEM((HEADS * VROWS, TQ), F32), pltpu.VMEM((D_ATTN, TQ), F32),
                        pltpu.VMEM((HEADS, TQ, TQ), F32), pltpu.VMEM((HEADS, TQ, TQ), BF16),
                        pltpu.VMEM((r, cdim), own_block.dtype),
                        pltpu.SemaphoreType.DMA((7,)), pltpu.SemaphoreType.DMA((7,)), pltpu.SemaphoreType.DMA],
        compiler_params=_params(1),
    )(qat3, ka, vt3, own_block)


def _post_attn_fwd(a, mpre, x, g_attn, g_pool, pscale, wout, g_post, g_ffn_pre):
    s, d = x.shape

    def body(a_ref, mp_ref, x_ref, ga_ref, gp_ref, ps_ref, wo_ref, gpost_ref, gpre_ref,
             mix_ref, o_ref, h1_ref, hn2_ref):
        for rows in _HALVES:
            av = a_ref[rows, :]
            mix_ref[rows, 0:D_ATTN] = (av * _rstd(av) * ga_ref[...]).astype(BF16)
            mv = mp_ref[rows, :] * ps_ref[...]
            mix_ref[rows, D_ATTN:] = (mv * _rstd(mv) * gp_ref[...]).astype(BF16)
            o = _nn(mix_ref[rows, :], wo_ref[...].reshape(d, d))
            o_ref[rows, :] = o
            h1 = x_ref[rows, :] + o * _rstd(o) * gpost_ref[...]
            h1_ref[rows, :] = h1
            hn2_ref[rows, :] = (h1 * _rstd(h1) * gpre_ref[...]).astype(BF16)

    vec = lambda n: pl.BlockSpec((1, n), _fixed)
    return pl.pallas_call(
        body, grid=(s // TS,), name="post_attn_fwd",
        out_shape=(jax.ShapeDtypeStruct((s, d), BF16), jax.ShapeDtypeStruct((s, d), F32),
                   jax.ShapeDtypeStruct((s, d), F32), jax.ShapeDtypeStruct((s, d), BF16)),
        in_specs=[pl.BlockSpec((TS, D_ATTN), _row), pl.BlockSpec((TS, D_POOL), _row), pl.BlockSpec((TS, d), _row),
                  vec(D_ATTN), vec(D_POOL), vec(D_POOL), _spec_square(0), vec(d), vec(d)],
        out_specs=(pl.BlockSpec((TS, d), _row),) * 4,
        compiler_params=_params(1),
    )(a, mpre, x, g_attn, g_pool, pscale, wout, g_post, g_ffn_pre)


def _ffn_fwd(hn2, wg, wu, wd, h1, g_post):
    s, d = h1.shape
    nc = D_FF // TN_FF
    ts = min(TS_FF, s)

    def body(hn_ref, wg_ref, wu_ref, wd_ref, h1_ref, g_ref, gate_ref, up_ref, act_ref, ff_ref, h2_ref, acc):
        j = pl.program_id(1)

        @pl.when(j == 0)
        def _():
            acc[...] = jnp.zeros_like(acc)

        for r in range(2):
            rows = slice(r * (ts // 2), (r + 1) * (ts // 2))
            hn = hn_ref[rows, :]
            gt = _nt(hn, wg_ref[...].reshape(TN_FF, d))
            up = _nt(hn, wu_ref[...].reshape(TN_FF, d))
            gate_ref[rows, :] = gt.astype(BF16)
            up_ref[rows, :] = up.astype(BF16)
            act_ref[rows, :] = (gt * jax.nn.sigmoid(gt) * up).astype(BF16)
            acc[rows, :] += _nn(act_ref[rows, :], wd_ref[...].reshape(TN_FF, d))

        @pl.when(j == nc - 1)
        def _():
            ff = acc[...]
            ff_ref[...] = ff
            h2_ref[...] = h1_ref[...] + ff * _rstd(ff) * g_ref[...]

    rowblk = pl.BlockSpec((ts, d), lambda i, j: (i, 0))
    chunk = pl.BlockSpec((ts, TN_FF), lambda i, j: (i, j))
    return pl.pallas_call(
        body, grid=(s // ts, nc), name="ffn_fwd",
        out_shape=(jax.ShapeDtypeStruct((s, D_FF), BF16),) * 3 + (jax.ShapeDtypeStruct((s, d), F32),) * 2,
        in_specs=[rowblk, _spec_ff(0), _spec_ff(1), _spec_ff(2), rowblk, pl.BlockSpec((1, d), lambda i, j: (0, 0))],
        out_specs=(chunk, chunk, chunk, rowblk, rowblk),
        scratch_shapes=[pltpu.VMEM((ts, d), F32)],
        compiler_params=_params(2),
    )(hn2, wg, wu, wd, h1, g_post)


def _tail_fwd_bwd(h2, p, tgt, ff, wple, wpg, g_ple, g_ffn_post):
    s, d = h2.shape

    def body(h2_ref, p_ref, t_ref, ff_ref, wple_ref, wpg_ref, gple_ref, gfp_ref,
             dh2_ref, dff_ref, dgl_ref, dpp_ref, h2b_ref, pb_ref, loss_ref, dgple_ref, dgfp_ref):
        i = pl.program_id(0)

        @pl.when(i == 0)
        def _():
            loss_ref[...] = jnp.zeros_like(loss_ref)
            dgple_ref[...] = jnp.zeros_like(dgple_ref)
            dgfp_ref[...] = jnp.zeros_like(dgfp_ref)

        h2 = h2_ref[...]
        h2b = h2.astype(BF16)
        h2b_ref[...] = h2b
        pb = p_ref[...].astype(BF16)
        pb_ref[...] = pb
        pp = _nt(pb, wple_ref[...])
        gple = gple_ref[...]
        e = pp * _rstd(pp) * gple
        wpg = wpg_ref[...].reshape(d, d)
        sg = jax.nn.sigmoid(_nn(h2b, wpg))
        diff = h2 + sg * e - t_ref[...]
        sq = jnp.sum(jnp.sum(diff * diff, axis=1, keepdims=True), axis=0, keepdims=True)
        loss_ref[...] += jnp.broadcast_to(sq * (0.5 / d), loss_ref.shape)
        dh3 = diff * (1.0 / d)
        dgl = (dh3 * e * sg * (1.0 - sg)).astype(BF16)
        dgl_ref[...] = dgl
        dh2 = dh3 + _nt(dgl, wpg)
        dh2_ref[...] = dh2
        dpp, dg = _rms_bwd(pp, gple, dh3 * sg)
        dpp_ref[...] = dpp.astype(BF16)
        dgple_ref[...] += dg
        dff, dg = _rms_bwd(ff_ref[...], gfp_ref[...], dh2)
        dff_ref[...] = dff.astype(BF16)
        dgfp_ref[...] += dg

    rowblk = pl.BlockSpec((TS, d), _row)
    vec = pl.BlockSpec((1, d), _fixed)
    return pl.pallas_call(
        body, grid=(s // TS,), name="tail_fwd_bwd",
        out_shape=(jax.ShapeDtypeStruct((s, d), F32), jax.ShapeDtypeStruct((s, d), BF16),
                   jax.ShapeDtypeStruct((s, d), BF16), jax.ShapeDtypeStruct((s, d), BF16),
                   jax.ShapeDtypeStruct((s, d), BF16), jax.ShapeDtypeStruct((s, D_PLE), BF16),
                   jax.ShapeDtypeStruct((8, LANES), F32), jax.ShapeDtypeStruct((1, d), F32),
                   jax.ShapeDtypeStruct((1, d), F32)),
        in_specs=[rowblk, pl.BlockSpec((TS, D_PLE), _row), rowblk, rowblk,
                  pl.BlockSpec(wple.shape, _fixed), _spec_square(1), vec, vec],
        out_specs=(rowblk, rowblk, rowblk, rowblk, rowblk, pl.BlockSpec((TS, D_PLE), _row),
                   pl.BlockSpec((8, LANES), _fixed), vec, vec),
        compiler_params=_params(1),
    )(h2, p, tgt, ff, wple, wpg, g_ple, g_ffn_post)


def _ffn_bwd(dff, gate, up, wd, wg, wu, h1, dh2, g_pre):
    s, d = h1.shape
    nc = D_FF // TN_FF
    ts = min(TS_FF, s)

    def body(dff_ref, gate_ref, up_ref, wd_ref, wg_ref, wu_ref, h1_ref, dh2_ref, g_ref,
             dgate_ref, dup_ref, dh1_ref, dg_ref, acc):
        i = pl.program_id(0)
        j = pl.program_id(1)

        @pl.when((i == 0) & (j == 0))
        def _():
            dg_ref[...] = jnp.zeros_like(dg_ref)

        @pl.when(j == 0)
        def _():
            acc[...] = jnp.zeros_like(acc)

        for r in range(2):
            rows = slice(r * (ts // 2), (r + 1) * (ts // 2))
            dact = _nt(dff_ref[rows, :], wd_ref[...].reshape(TN_FF, d))
            gt = gate_ref[rows, :].astype(F32)
            sg = jax.nn.sigmoid(gt)
            dup_ref[rows, :] = (dact * gt * sg).astype(BF16)
            dgate_ref[rows, :] = (dact * up_ref[rows, :].astype(F32) * (sg * (1.0 + gt * (1.0 - sg)))).astype(BF16)
            acc[rows, :] += (_nn(dgate_ref[rows, :], wg_ref[...].reshape(TN_FF, d))
                             + _nn(dup_ref[rows, :], wu_ref[...].reshape(TN_FF, d)))

        @pl.when(j == nc - 1)
        def _():
            dv, dg = _rms_bwd(h1_ref[...], g_ref[...], acc[...])
            dh1_ref[...] = dh2_ref[...] + dv
            dg_ref[...] += dg

    rowblk = pl.BlockSpec((ts, d), lambda i, j: (i, 0))
    chunk = pl.BlockSpec((ts, TN_FF), lambda i, j: (i, j))
    vec = pl.BlockSpec((1, d), lambda i, j: (0, 0))
    return pl.pallas_call(
        body, grid=(s // ts, nc), name="ffn_bwd",
        out_shape=(jax.ShapeDtypeStruct((s, D_FF), BF16), jax.ShapeDtypeStruct((s, D_FF), BF16),
                   jax.ShapeDtypeStruct((s, d), F32), jax.ShapeDtypeStruct((1, d), F32)),
        in_specs=[rowblk, chunk, chunk, _spec_ff(2), _spec_ff(0), _spec_ff(1), rowblk, rowblk, vec],
        out_specs=(chunk, chunk, rowblk, vec),
        scratch_shapes=[pltpu.VMEM((ts, d), F32)],
        compiler_params=_params(2),
    )(dff, gate, up, wd, wg, wu, h1, dh2, g_pre)


def _post_attn_bwd(dh1, o, a, mpre, wout, wpool, g_post, g_attn, g_pool, pscale, send):
    s, d = dh1.shape
    sub = TS // TQ
    npc = len(send)

    def body(dh1_ref, o_ref, a_ref, mp_ref, wo_ref, wp_ref, gpost_ref, ga_ref, gp_ref, ps_ref, *refs):
        send_refs, refs = refs[:npc], refs[npc:]
        dob_ref, dat_ref, dlt_ref, dmpb_ref, dy_ref, dgpost_ref, dga_ref, dgp_ref, dps_ref = refs[:9]
        got_refs, (send_sems, recv_sems) = refs[9:9 + npc], refs[9 + npc:]
        i = pl.program_id(0)

        @pl.when(i == 0)
        def _():
            for cp in _pair_copies(send_refs, got_refs, send_sems, recv_sems):
                cp.start()
            dgpost_ref[...] = jnp.zeros_like(dgpost_ref)
            dga_ref[...] = jnp.zeros_like(dga_ref)
            dgp_ref[...] = jnp.zeros_like(dgp_ref)
            dps_ref[...] = jnp.zeros_like(dps_ref)

        do, dg = _rms_bwd(o_ref[...], gpost_ref[...], dh1_ref[...])
        dgpost_ref[...] += dg
        dob = do.astype(BF16)
        dob_ref[...] = dob
        dmix = _nt(dob, wo_ref[...].reshape(d, d))

        av = a_ref[...]
        da, dg = _rms_bwd(av, ga_ref[...], dmix[:, 0:D_ATTN])
        dga_ref[...] += dg
        dat = da.astype(BF16).T
        hsel = (lax.shift_right_logical(lax.broadcasted_iota(jnp.int32, (HEADS, D_ATTN), 1), 6)
                == lax.broadcasted_iota(jnp.int32, (HEADS, D_ATTN), 0)).astype(F32)
        dlt = lax.dot_general(hsel, da * av, (((1,), (1,)), ((), ())), precision=HIGHEST, preferred_element_type=F32)
        for q in range(sub):
            dlt_ref[q] = dlt[:, q * TQ:(q + 1) * TQ]
            dat_ref[q] = dat[:, q * TQ:(q + 1) * TQ]

        ps = ps_ref[...]
        mp = mp_ref[...]
        dm, dg = _rms_bwd(mp * ps, gp_ref[...], dmix[:, D_ATTN:])
        dgp_ref[...] += dg
        dps_ref[...] += jnp.sum(dm * mp, axis=0, keepdims=True)
        dmpb = (dm * ps).astype(BF16)
        dmpb_ref[...] = dmpb
        for g in range(len(POOL_WINDOWS)):
            cols = slice(g * POOL_CH, (g + 1) * POOL_CH)
            dy_ref[:, cols] = _nt(dmpb[:, cols], wp_ref[g])

        @pl.when(i == s // TS - 1)
        def _():
            for cp in _pair_copies(send_refs, got_refs, send_sems, recv_sems):
                cp.wait()

    rowblk = pl.BlockSpec((TS, d), _row)
    half = pl.BlockSpec((TS, D_ATTN), _row)
    vec = lambda n: pl.BlockSpec((1, n), _fixed)
    nk = N_DEV // 2
    res = pl.pallas_call(
        body, grid=(s // TS,), name="post_attn_bwd",
        out_shape=(jax.ShapeDtypeStruct((s, d), BF16), jax.ShapeDtypeStruct((s // TQ, D_ATTN, TQ), BF16),
                   jax.ShapeDtypeStruct((s // TQ, HEADS, TQ), F32), jax.ShapeDtypeStruct((s, D_POOL), BF16),
                   jax.ShapeDtypeStruct((s, D_POOL), F32), jax.ShapeDtypeStruct((1, d), F32),
                   jax.ShapeDtypeStruct((1, D_ATTN), F32), jax.ShapeDtypeStruct((1, D_POOL), F32),
                   jax.ShapeDtypeStruct((1, D_POOL), F32))
        + tuple(jax.ShapeDtypeStruct((nk,) + t.shape[1:], t.dtype) for t in send),
        in_specs=[rowblk, rowblk, half, half, _spec_square(0),
                  pl.BlockSpec(wpool.shape, lambda i: (0, 0, 0)), vec(d), vec(D_ATTN), vec(D_POOL), vec(D_POOL)]
        + [ANY] * npc,
        out_specs=(rowblk, pl.BlockSpec((sub, D_ATTN, TQ), lambda i: (i, 0, 0)),
                   pl.BlockSpec((sub, HEADS, TQ), lambda i: (i, 0, 0)), half, half,
                   vec(d), vec(D_ATTN), vec(D_POOL), vec(D_POOL)) + (ANY,) * npc,
        scratch_shapes=[pltpu.SemaphoreType.DMA((nk, npc)), pltpu.SemaphoreType.DMA((nk, npc))],
        compiler_params=_params(1),
    )(dh1, o, a, mpre, wout, wpool, g_post, g_attn, g_pool, pscale, *send)
    return res[:9], list(res[9:])


def _attn_bwd(ka, v, kt3, qat3, qt3, dot3, lset3, dlt3, chip_blocks, small_block):
    s = ka.shape[0]
    nq = s // TQ

    def body(ka_ref, v_ref, kt_ref, qat_ref, qt_ref, dot_ref, lset_ref, dlt_ref, b_ref, sm_ref,
             dqt_ref, dkt_ref, dvt_ref, got_ref, all_ref, pt_scr, ptb_scr, dsb_scr,
             stage, send_sems, recv_sems, local_sem, stage_s, send_s, recv_s, local_s):
        j = pl.program_id(0)

        @pl.when(j == 0)
        def _():
            _chips_start(b_ref, got_ref, stage, send_sems, recv_sems, local_sem)
            _gather_start(sm_ref, all_ref, stage_s, send_s, recv_s, local_s)
            dqt_ref[...] = jnp.zeros_like(dqt_ref)

        @pl.when(j == max(nq - 2, 0))
        def _():
            _gather_pass_on(all_ref, send_s, recv_s)

        def tile(i, masked):
            def accumulate(ref, idx, val):
                if masked:
                    ref[idx] = val
                else:
                    ref[idx] += val

            for h in range(HEADS):
                aug = slice(h * AUG, (h + 1) * AUG)
                st = _nn(ka_ref[:, aug], qat_ref[i, aug, :]) - lset_ref[i, h:h + 1, :]
                if masked:
                    st = jnp.where(_causal_in_tile(), st, NEG)
                pt = jnp.exp2(st)
                pt_scr[h] = pt
                ptb_scr[h] = pt.astype(BF16)
            heads = [(h, slice(h * HEAD_DIM, (h + 1) * HEAD_DIM)) for h in range(HEADS)]
            for h, hs in heads:
                dst = pt_scr[h] * (_nn(v_ref[:, hs], dot_ref[i, hs, :]) - dlt_ref[i, h:h + 1, :])
                dsb_scr[h] = dst.astype(BF16)
            for h, hs in heads:
                accumulate(dvt_ref, (0, hs, slice(None)), _nt(dot_ref[i, hs, :], ptb_scr[h]))
            for h, hs in heads:
                rows = slice(h * VROWS, (h + 1) * VROWS)
                accumulate(dkt_ref, (0, rows, slice(None)), _nt(qt_ref[i, rows, :], dsb_scr[h]))
            for h, hs in heads:
                rows = slice(h * VROWS, (h + 1) * VROWS)
                dqt_ref[i, rows, :] += _nn(kt_ref[0, rows, :], dsb_scr[h])

        first = j + 1
        pairs = (nq - first) // 2

        def step(p, carry):
            tile(first + 2 * p, False)
            tile(first + 2 * p + 1, False)
            return carry

        tile(j, True)
        lax.fori_loop(0, pairs, step, 0)

        @pl.when(first + 2 * pairs < nq)
        def _():
            tile(nq - 1, False)

        @pl.when(j == nq - 1)
        def _():
            _chips_finish(b_ref, got_ref, send_sems, recv_sems)
            _gather_finish(sm_ref, all_ref, send_s, recv_s)

    blk = pl.BlockSpec((TQ, D_ATTN), _row)
    tile_t = lambda rows: pl.BlockSpec((1, rows, TQ), lambda j: (j, 0, 0))
    per_tile = lambda rows: jax.ShapeDtypeStruct((nq, rows, TQ), F32)
    _, r, cdim = chip_blocks.shape
    dma = pltpu.SemaphoreType.DMA
    return pl.pallas_call(
        body, grid=(nq,), name="attn_bwd",
        out_shape=(per_tile(HEADS * VROWS), per_tile(HEADS * VROWS), per_tile(D_ATTN),
                   jax.ShapeDtypeStruct(chip_blocks.shape, chip_blocks.dtype),
                   jax.ShapeDtypeStruct((N_DEV,) + small_block.shape, small_block.dtype)),
        in_specs=[pl.BlockSpec((TQ, HEADS * AUG), _row), blk, tile_t(HEADS * VROWS),
                  VMEM_WHOLE, VMEM_WHOLE, VMEM_WHOLE, VMEM_WHOLE, VMEM_WHOLE, ANY, ANY],
        out_specs=(pl.BlockSpec((nq, HEADS * VROWS, TQ), lambda j: (0, 0, 0)), tile_t(HEADS * VROWS), tile_t(D_ATTN),
                   ANY, ANY),
        scratch_shapes=[pltpu.VMEM((HEADS, TQ, TQ), F32), pltpu.VMEM((HEADS, TQ, TQ), BF16),
                        pltpu.VMEM((HEADS, TQ, TQ), BF16), pltpu.VMEM((r, cdim), chip_blocks.dtype),
                        dma((3,)), dma((3,)), dma,
                        pltpu.VMEM(small_block.shape, small_block.dtype), dma((7,)), dma((7,)), dma],
        compiler_params=_params(1),
    )(ka, v, kt3, qat3, qt3, dot3, lset3, dlt3, chip_blocks, small_block)


def _pre_attn_bwd(dqt3, dkt3, dvt3, fl, dy, x, dh1, g1, wqkv, wf, wu):
    s, d = x.shape
    nt = s // TS
    n = TS + HALO
    sub = TS // TQ
    qkv, fcols = 3 * D_ATTN, 3 * D_ATTN + LANES

    def body(dqt_ref, dkt_ref, dvt_ref, fl_ref, dy_ref, x_ref, dh1_ref, g_ref, wqkv_ref, wf_ref, wu_ref,
             gx_ref, dz_ref, dg_ref, db_ref, ybuf, ccar, dlog, dsum):
        dqkv_ref = dz_ref.at[:, 0:qkv]
        dfb_ref = dz_ref.at[:, qkv:fcols]
        dub_ref = dz_ref.at[:, fcols:]
        i = pl.program_id(0)
        ti = nt - 1 - i

        @pl.when(i == 0)
        def _():
            ybuf[TS:n, :] = jnp.zeros((HALO, D_POOL), F32)
            ccar[...] = jnp.zeros_like(ccar)
            dg_ref[...] = jnp.zeros_like(dg_ref)
            db_ref[...] = jnp.zeros_like(db_ref)
            dsum[...] = jnp.zeros_like(dsum)

        for a in range(sub):
            for h in range(HEADS):
                r = h * VROWS + HEAD_DIM
                dsum[h:h + 1, a * TQ:(a + 1) * TQ] = dqt_ref[a, r:r + 1, :] - dkt_ref[a, r:r + 1, :]
        dlog[...] = ccar[...] + _running_sum(dsum[...].T, reverse=True)
        ccar[...] = dlog[0:1, :]
        df = dlog[...] * jax.nn.sigmoid(-fl_ref[...])
        db_ref[...] += jnp.sum(df, axis=0, keepdims=True)
        dfb = df.astype(BF16)
        dfb_ref[...] = dfb

        t = ti * TS + lax.broadcasted_iota(jnp.int32, (TS, 1), 0)
        dy = dy_ref[...]
        for g, w in enumerate(POOL_WINDOWS):
            cols = slice(g * POOL_CH, (g + 1) * POOL_CH)
            ybuf[0:TS, cols] = dy[:, cols] / jnp.minimum(t + 1, w).astype(F32)
        for g, w in enumerate(POOL_WINDOWS):
            cols = slice(g * POOL_CH, (g + 1) * POOL_CH)
            sm = ybuf[:, cols]
            step = 1
            while step < w:
                sm = sm + pltpu.roll(sm, n - step, 0)
                step *= 2
            dub_ref[:, cols] = (sm[0:TS, :] - dy[:, cols]).astype(BF16)
        ybuf[TS:n, :] = ybuf[0:HALO, :]

        for a in range(sub):
            rows = slice(a * TQ, (a + 1) * TQ)
            for h in range(HEADS):
                src = slice(h * VROWS, h * VROWS + HEAD_DIM)
                dqkv_ref[rows, h * HEAD_DIM:(h + 1) * HEAD_DIM] = (dqt_ref[a, src, :].T * 0.125).astype(BF16)
                dqkv_ref[rows, D_ATTN + h * HEAD_DIM:D_ATTN + (h + 1) * HEAD_DIM] = dkt_ref[a, src, :].T.astype(BF16)
            dqkv_ref[rows, 2 * D_ATTN:] = dvt_ref[a].T.astype(BF16)
        dhn = _nn(dqkv_ref[...], wqkv_ref[...]) + _nn(dfb, wf_ref[...]) + _nn(dub_ref[...], wu_ref[...])
        dx, dg = _rms_bwd(x_ref[...], g_ref[...], dhn)
        gx_ref[...] = dh1_ref[...] + dx
        dg_ref[...] += dg

    rev = lambda i: (nt - 1 - i, 0)
    blk = lambda w: pl.BlockSpec((TS, w), rev)
    return pl.pallas_call(
        body, grid=(nt,), name="pre_attn_bwd",
        out_shape=(jax.ShapeDtypeStruct((s, d), F32), jax.ShapeDtypeStruct((s, fcols + D_POOL), BF16),
                   jax.ShapeDtypeStruct((1, d), F32), jax.ShapeDtypeStruct((1, LANES), F32)),
        in_specs=[pl.BlockSpec((sub, HEADS * VROWS, TQ), lambda i: (nt - 1 - i, 0, 0)),
                  pl.BlockSpec((sub, HEADS * VROWS, TQ), lambda i: (nt - 1 - i, 0, 0)),
                  pl.BlockSpec((sub, D_ATTN, TQ), lambda i: (nt - 1 - i, 0, 0)),
                  blk(LANES), blk(D_POOL), blk(d), blk(d),
                  pl.BlockSpec((1, d), _fixed), pl.BlockSpec((qkv, d), _fixed), pl.BlockSpec(wf.shape, _fixed),
                  pl.BlockSpec(wu.shape, _fixed)],
        out_specs=(blk(d), blk(fcols + D_POOL), pl.BlockSpec((1, d), _fixed), pl.BlockSpec((1, LANES), _fixed)),
        scratch_shapes=[pltpu.VMEM((n, D_POOL), F32), pltpu.VMEM((1, LANES), F32), pltpu.VMEM((TS, LANES), F32),
                        pltpu.VMEM((LANES, TS), F32)],
        compiler_params=_params(1),
    )(dqt3, dkt3, dvt3, fl, dy, x, dh1, g1, wqkv, wf, wu)


def _wgrad(a, b, out_dtype, name):
    s, m = a.shape
    n = b.shape[1]
    tm = max(t for t in range(LANES, min(m, TM_WGRAD) + 1, LANES) if m % t == 0)
    ts = min(TS_WGRAD, s)
    ns = s // ts

    def body(a_ref, b_ref, o_ref, acc):
        i = pl.program_id(1)

        @pl.when(i == 0)
        def _():
            acc[...] = jnp.zeros_like(acc)

        acc[...] += _tn(a_ref[...], b_ref[...])

        @pl.when(i == ns - 1)
        def _():
            o_ref[...] = acc[...].astype(out_dtype)

    return pl.pallas_call(
        body, grid=(m // tm, ns), name=name, out_shape=jax.ShapeDtypeStruct((m, n), out_dtype),
        in_specs=[pl.BlockSpec((ts, tm), lambda j, i: (i, j)), pl.BlockSpec((ts, n), lambda j, i: (i, 0))],
        out_specs=pl.BlockSpec((tm, n), lambda j, i: (j, 0)),
        scratch_shapes=[pltpu.VMEM((tm, n), F32)],
        compiler_params=_params(2),
    )(a, b)


def _wgrad_pool(y, dm):
    s = y.shape[0]
    ts = min(TS_WGRAD, s)
    groups = len(POOL_WINDOWS)

    def body(a_ref, b_ref, o_ref):
        @pl.when(pl.program_id(0) == 0)
        def _():
            o_ref[...] = jnp.zeros_like(o_ref)

        for g in range(groups):
            cols = slice(g * POOL_CH, (g + 1) * POOL_CH)
            o_ref[g] += _tn(a_ref[:, cols], b_ref[:, cols])

    return pl.pallas_call(
        body, grid=(s // ts,), name="wgrad_pool", out_shape=jax.ShapeDtypeStruct((groups, POOL_CH, POOL_CH), F32),
        in_specs=[pl.BlockSpec((ts, D_POOL), _row), pl.BlockSpec((ts, D_POOL), _row)],
        out_specs=pl.BlockSpec((groups, POOL_CH, POOL_CH), lambda i: (0, 0, 0)),
        compiler_params=_params(1),
    )(y, dm)


def _wgrad_in(dz, hn):
    s, m = dz.shape
    n = hn.shape[1]
    ts = min(TS_WGRAD, s)
    ns = s // ts
    pad_at, pad = 3 * D_ATTN + HEADS, LANES - HEADS
    assert m == D_IN + pad and N_DEV * SHARD_IN == D_IN

    def pieces(d):
        lo, hi = d * SHARD_IN, (d + 1) * SHARD_IN
        spans = [(lo, min(hi, pad_at), 0), (max(lo, pad_at), hi, pad)]
        return [(a + shift, b - a, a - lo) for a, b, shift in spans if b > a]

    def body(a_ref, b_ref, o_ref, acc, stage):
        i = pl.program_id(0)

        @pl.when(i == 0)
        def _():
            acc[...] = jnp.zeros_like(acc)

        acc[...] += _tn(a_ref[...], b_ref[...])

        @pl.when(i == ns - 1)
        def _():
            stage[SHARD_IN:ROWS_IN, :] = jnp.zeros((ROWS_IN - SHARD_IN, n), F32)
            for d in range(N_DEV):
                for src, rows, dst in pieces(d):
                    stage[dst:dst + rows, :] = acc[src:src + rows, :]
                o_ref[d] = stage[...].astype(BF16)

    return pl.pallas_call(
        body, grid=(ns,), name="wgrad_in", out_shape=jax.ShapeDtypeStruct((N_DEV, ROWS_IN, n), BF16),
        in_specs=[pl.BlockSpec((ts, m), _row), pl.BlockSpec((ts, n), _row)],
        out_specs=pl.BlockSpec((N_DEV, ROWS_IN, n), lambda i: (0, 0, 0)),
        scratch_shapes=[pltpu.VMEM((m, n), F32), pltpu.VMEM((ROWS_IN, n), F32)],
        compiler_params=_params(1),
    )(dz, hn)


def _adamw(w, g, m, v):
    m = ADAM_B1 * m + (1.0 - ADAM_B1) * g
    v = ADAM_B2 * v + (1.0 - ADAM_B2) * (g * g)
    m_hat = m / (1.0 - ADAM_B1 ** ADAM_STEP)
    v_hat = v / (1.0 - ADAM_B2 ** ADAM_STEP)
    delta = -ADAM_LR * (m_hat / (jnp.sqrt(v_hat) + ADAM_EPS) + ADAM_WD * w)
    return delta, m, v


def _sum_update(p_ref, w_ref, m_ref, v_ref, g_ref, d_ref, nm_ref, nv_ref):
    g = p_ref[0].astype(F32)
    for k in range(1, p_ref.shape[0]):
        g = g + p_ref[k].astype(F32)
    g_ref[...] = g
    d_ref[...], nm_ref[...], nv_ref[...] = _adamw(w_ref[...], g, m_ref[...], v_ref[...])


def _reduce_update_rest(parts, w, m, v, chip_blocks, small_block):
    nk, r, c = parts.shape
    ns = r // TR_REST

    def body(p_ref, w_ref, m_ref, v_ref, b_ref, sm_ref, g_ref, d_ref, nm_ref, nv_ref, got_ref, all_ref,
             stage_b, stage_s, send_b, recv_b, local_b, send_s, recv_s, local_s):
        i = pl.program_id(0)

        @pl.when(i == 0)
        def _():
            _chips_start(b_ref, got_ref, stage_b, send_b, recv_b, local_b)
            _gather_start(sm_ref, all_ref, stage_s, send_s, recv_s, local_s)

        _sum_update(p_ref, w_ref, m_ref, v_ref, g_ref, d_ref, nm_ref, nv_ref)

        @pl.when(i == ns - 1)
        def _():
            _gather_pass_on(all_ref, send_s, recv_s)
            _chips_finish(b_ref, got_ref, send_b, recv_b)
            _gather_finish(sm_ref, all_ref, send_s, recv_s)

    blk = pl.BlockSpec((TR_REST, c), _row)
    out = jax.ShapeDtypeStruct((r, c), F32)
    dma = pltpu.SemaphoreType.DMA
    return pl.pallas_call(
        body, grid=(ns,), name="reduce_update_rest",
        out_shape=(out,) * 4 + (jax.ShapeDtypeStruct(chip_blocks.shape, chip_blocks.dtype),
                                jax.ShapeDtypeStruct((N_DEV,) + small_block.shape, small_block.dtype)),
        in_specs=[pl.BlockSpec((nk, TR_REST, c), lambda i: (0, i, 0)), blk, blk, blk, ANY, ANY],
        out_specs=(blk,) * 4 + (ANY, ANY),
        scratch_shapes=[pltpu.VMEM(chip_blocks.shape[1:], chip_blocks.dtype), pltpu.VMEM(small_block.shape, small_block.dtype),
                        dma((3,)), dma((3,)), dma, dma((7,)), dma((7,)), dma],
        compiler_params=_params(1),
    )(parts, w, m, v, chip_blocks, small_block)


def _reduce_update_big(parts, w, m, v, tr, name):
    nk, r, c = parts.shape

    def body(p_ref, w_ref, m_ref, v_ref, g_ref, d_ref, nm_ref, nv_ref):
        _sum_update(p_ref, w_ref, m_ref, v_ref, g_ref, d_ref, nm_ref, nv_ref)

    blk = pl.BlockSpec((tr, c), _row)
    out = jax.ShapeDtypeStruct((r, c), F32)
    return pl.pallas_call(
        body, grid=(r // tr,), name=name, out_shape=(out,) * 4,
        in_specs=[pl.BlockSpec((nk, tr, c), lambda i: (0, i, 0)), blk, blk, blk],
        out_specs=(blk,) * 4, compiler_params=_params(1),
    )(parts, w, m, v)


def _reduce_update_small(parts, late, w, m, v):
    nd = parts.shape[0]
    first = parts.shape[1] - late.shape[1]

    def body(p_ref, q_ref, w_ref, m_ref, v_ref, g_ref, d_ref, nm_ref, nv_ref):
        g, t = p_ref[0], q_ref[0]
        for k in range(1, nd):
            g, t = g + p_ref[k], t + q_ref[k]
        g_ref[...] = g
        g_ref[first:, :] = g[first:, :] + t
        d_ref[...], nm_ref[...], nv_ref[...] = _adamw(w_ref[...], g_ref[...], m_ref[...], v_ref[...])

    out = jax.ShapeDtypeStruct(w.shape, F32)
    return pl.pallas_call(body, name="reduce_update_small", out_shape=(out,) * 4,
                          compiler_params=pltpu.CompilerParams(vmem_limit_bytes=VMEM_LIMIT))(parts, late, w, m, v)


MESH = pl.DeviceIdType.MESH


def _copy_through_vmem(src_hbm, dst_hbm, stage, sem):
    load = pltpu.make_async_copy(src_hbm, stage, sem)
    load.start()
    load.wait()
    store = pltpu.make_async_copy(stage, dst_hbm, sem)
    store.start()
    store.wait()


class _GatherPlan:
    def __init__(self, x_ref, out_ref, send_sems, recv_sems):
        x, y, c = lax.axis_index("x"), lax.axis_index("y"), lax.axis_index("c")
        self.me, self.sibling, self.c = (x, y, c), (x, y, 1 - c), c
        self.chips = [(1 - x, y), (x, 1 - y), (1 - x, 1 - y)]
        self.x_ref, self.out_ref, self.send_sems, self.recv_sems = x_ref, out_ref, send_sems, recv_sems

    def slot(self, px, py, pc):
        return self.out_ref.at[4 * px + 2 * py + pc]

    def copy(self, k, block, to, src=None):
        return pltpu.make_async_remote_copy(
            src_ref=self.slot(*block) if src is None else src, dst_ref=self.slot(*block),
            send_sem=self.send_sems.at[k], recv_sem=self.recv_sems.at[k], device_id=to, device_id_type=MESH)

    def first(self):
        return [self.copy(0, self.me, self.sibling, src=self.x_ref)] + [
            self.copy(1 + j, self.me, (*chip, self.c), src=self.x_ref) for j, chip in enumerate(self.chips)]

    def passed(self):
        return [self.copy(4 + j, (*chip, self.c), self.sibling) for j, chip in enumerate(self.chips)]


def _gather_start(x_ref, out_ref, stage, send_sems, recv_sems, local_sem):
    plan = _GatherPlan(x_ref, out_ref, send_sems, recv_sems)
    for cp in plan.first():
        cp.start()
    _copy_through_vmem(x_ref, plan.slot(*plan.me), stage, local_sem)


def _gather_pass_on(out_ref, send_sems, recv_sems):
    plan = _GatherPlan(None, out_ref, send_sems, recv_sems)
    passed = plan.passed()
    for j, chip in enumerate(plan.chips):
        plan.copy(1 + j, (*chip, plan.c), plan.me).wait_recv()
        passed[j].start()


def _gather_finish(x_ref, out_ref, send_sems, recv_sems):
    plan = _GatherPlan(x_ref, out_ref, send_sems, recv_sems)
    plan.copy(0, plan.sibling, plan.me).wait_recv()
    for j, chip in enumerate(plan.chips):
        plan.copy(4 + j, (*chip, 1 - plan.c), plan.me).wait_recv()
    for cp in plan.first() + plan.passed():
        cp.wait_send()


def _gather_w_in(xs):
    r, cdim = xs.shape
    qkv, f_end = 3 * D_ATTN, 3 * D_ATTN + HEADS

    def body(x_ref, wqkv_ref, wf_ref, wu_ref, send_sems, recv_sems, local_sem, blocks, flat):
        plan = _GatherPlan(x_ref, blocks, send_sems, recv_sems)
        for cp in plan.first():
            cp.start()
        own = pltpu.make_async_copy(x_ref, plan.slot(*plan.me), local_sem)
        own.start()
        own.wait()
        _gather_pass_on(blocks, send_sems, recv_sems)
        _gather_finish(x_ref, blocks, send_sems, recv_sems)
        for dev in range(N_DEV):
            flat[dev * SHARD_IN:(dev + 1) * SHARD_IN, :] = blocks[dev, 0:SHARD_IN, :].astype(F32)
        wqkv_ref[...] = flat[0:qkv, :].astype(BF16)
        wf_ref[...] = jnp.concatenate([flat[qkv:f_end, :], jnp.zeros((LANES - HEADS, cdim), F32)], axis=0).astype(BF16)
        wu_ref[...] = flat[f_end:D_IN, :].astype(BF16)

    shape = lambda rows: jax.ShapeDtypeStruct((rows, cdim), xs.dtype)
    dma = pltpu.SemaphoreType.DMA
    return pl.pallas_call(
        body, name="gather_w_in",
        out_shape=(shape(qkv), shape(LANES), shape(D_IN - f_end)),
        in_specs=[ANY], out_specs=(VMEM_WHOLE, VMEM_WHOLE, VMEM_WHOLE),
        scratch_shapes=[dma((7,)), dma((7,)), dma,
                        pltpu.VMEM((N_DEV, r, cdim), xs.dtype), pltpu.VMEM((D_IN, cdim), F32)],
        compiler_params=pltpu.CompilerParams(vmem_limit_bytes=VMEM_LIMIT),
    )(xs)


def _pair_copies(src_refs, dst_refs, send_sems, recv_sems):
    x, y, c = lax.axis_index("x"), lax.axis_index("y"), lax.axis_index("c")
    return [pltpu.make_async_remote_copy(
        src_ref=src.at[2 * k + (1 - c)], dst_ref=dst.at[k], send_sem=send_sems.at[k, p], recv_sem=recv_sems.at[k, p],
        device_id=(x, y, 1 - c), device_id_type=MESH)
        for k in range(N_DEV // 2) for p, (src, dst) in enumerate(zip(src_refs, dst_refs))]


def _rs_pair_sum(core, pieces, offsets, rows, name, landed=()):
    cdim = pieces[0].shape[2]
    nk = N_DEV // 2
    npc = len(pieces)
    nrem = npc - len(landed)
    spans = [(o, t.shape[1]) for t, o in zip(pieces, offsets)]
    ends = [o + n for o, n in spans]
    gaps = [(a, b - a) for a, b in zip(ends, [o for o, _ in spans[1:]] + [rows]) if b > a]

    def body(core_ref, *refs):
        own, src, got, o_ref = refs[:npc], refs[npc:npc + nrem], refs[npc + nrem:2 * npc], refs[2 * npc]
        landing, send_sems, recv_sems = refs[2 * npc + 1:]
        k = pl.program_id(0)
        x, y, c = lax.axis_index("x"), lax.axis_index("y"), lax.axis_index("c")

        def copies(kk):
            return [pltpu.make_async_remote_copy(
                src_ref=src[p].at[2 * kk + (1 - c)], dst_ref=landing.at[kk, pl.ds(o, n)],
                send_sem=send_sems.at[kk, p], recv_sem=recv_sems.at[kk, p], device_id=(x, y, 1 - c),
                device_id_type=MESH) for p, (o, n) in enumerate(spans[:nrem])]

        @pl.when(k == 0)
        def _():
            for kk in range(nk):
                for cp in copies(kk):
                    cp.start()

        for cp, piece, (o, n) in zip(copies(k), own, spans):
            cp.wait_recv()
            o_ref[0, o:o + n, :] = (piece[0].astype(F32) + landing[k, o:o + n, :].astype(F32)).astype(BF16)
        for theirs, piece, (o, n) in zip(got, own[nrem:], spans[nrem:]):
            o_ref[0, o:o + n, :] = (piece[0].astype(F32) + theirs[0].astype(F32)).astype(BF16)
        for o, n in gaps:
            o_ref[0, o:o + n, :] = jnp.zeros((n, cdim), BF16)

        @pl.when(k == nk - 1)
        def _():
            for kk in range(nk):
                for cp in copies(kk):
                    cp.wait_send()

    own_specs = [pl.BlockSpec((1, n, cdim), lambda k, core_ref: (2 * k + core_ref[0], 0, 0)) for _, n in spans]
    got_specs = [pl.BlockSpec((1, n, cdim), lambda k, core_ref: (k, 0, 0)) for _, n in spans[nrem:]]
    land_rows = max(o + n for o, n in spans[:nrem])
    return pl.pallas_call(
        body, name=name, out_shape=jax.ShapeDtypeStruct((nk, rows, cdim), BF16),
        grid_spec=pltpu.PrefetchScalarGridSpec(
            num_scalar_prefetch=1, grid=(nk,),
            in_specs=own_specs + [ANY] * nrem + got_specs,
            out_specs=pl.BlockSpec((1, rows, cdim), lambda k, core_ref: (k, 0, 0)),
            scratch_shapes=[pltpu.VMEM((nk, land_rows, cdim), BF16), pltpu.SemaphoreType.DMA((nk, nrem)),
                            pltpu.SemaphoreType.DMA((nk, nrem))]),
        compiler_params=_params(1),
    )(core, *pieces, *pieces[:nrem], *landed)


def _chips_start(b_ref, out_ref, stage, send_sems, recv_sems, local_sem):
    x, y, c = lax.axis_index("x"), lax.axis_index("y"), lax.axis_index("c")
    mychip = 2 * x + y
    for j, (px, py) in enumerate([(1 - x, y), (x, 1 - y), (1 - x, 1 - y)]):
        pltpu.make_async_remote_copy(
            src_ref=b_ref.at[2 * px + py], dst_ref=out_ref.at[mychip],
            send_sem=send_sems.at[j], recv_sem=recv_sems.at[j], device_id=(px, py, c), device_id_type=MESH).start()
    _copy_through_vmem(b_ref.at[mychip], out_ref.at[mychip], stage, local_sem)


def _chips_finish(b_ref, out_ref, send_sems, recv_sems):
    x, y, c = lax.axis_index("x"), lax.axis_index("y"), lax.axis_index("c")
    for j, (px, py) in enumerate([(1 - x, y), (x, 1 - y), (1 - x, 1 - y)]):
        pltpu.make_async_remote_copy(
            src_ref=b_ref.at[2 * px + py], dst_ref=out_ref.at[2 * px + py],
            send_sem=send_sems.at[j], recv_sem=recv_sems.at[j], device_id=(px, py, c), device_id_type=MESH).wait()


def _pad_rows(a, rows):
    return jnp.pad(a, ((0, rows - a.shape[0]), (0, 0)))


def _pack_in(w_in):
    return _pad_rows(w_in[0].T, ROWS_IN)


def _unpack_in(r):
    return r[0:SHARD_IN].T[None]


def _pack_rest(w_out, w_gate, w_up, w_down, w_ple, w_pg):
    head = _pad_rows(jnp.concatenate([w_out[0], w_pg[0], w_ple[0].T.reshape(ROWS_PLE, D_MODEL)], axis=0), OFF_GATE)
    return jnp.concatenate([head, w_gate[0].T, w_up[0].T, w_down[0]], axis=0)


def _unpack_rest(r):
    return (r[0:OFF_PG][None], r[OFF_GATE:OFF_UP].T[None], r[OFF_UP:OFF_DOWN].T[None], r[OFF_DOWN:ROWS_REST][None],
            r[OFF_PLE:OFF_PLE + ROWS_PLE].reshape(SHARD_SQ, D_PLE).T[None], r[OFF_PG:OFF_PLE][None])


def _pack_small(w_pool, g_mix_pre, g_mix_post, g_ffn_pre, g_ffn_post, g_ple, g_attn, g_pool, pool_scale, b_forget,
                loss=None):
    row = lambda vrow: vrow.reshape(1, -1)
    misc = [row(pool_scale), row(b_forget), row(loss) if loss is not None else jnp.zeros((1, 1), F32),
            jnp.zeros((1, D_MODEL - COL_LOSS - 1), F32)]
    rows = [w_pool.reshape(64, D_MODEL), row(g_mix_pre), row(g_mix_post), row(g_ffn_pre), row(g_ffn_post), row(g_ple),
            jnp.concatenate([row(g_attn), row(g_pool)], axis=1), jnp.concatenate(misc, axis=1),
            jnp.zeros((SMALL_ROWS - ROW_MISC - 1, D_MODEL), F32)]
    return jnp.concatenate(rows, axis=0)


def _pack_small_late(g_mix_pre, b_forget):
    misc = [jnp.zeros((1, COL_B_FORGET), F32), b_forget.reshape(1, -1), jnp.zeros((1, D_MODEL - COL_LOSS), F32)]
    return jnp.concatenate([g_mix_pre.reshape(1, -1), jnp.zeros((ROW_MISC - ROW_G_MIX_PRE - 1, D_MODEL), F32),
                            jnp.concatenate(misc, axis=1), jnp.zeros((SMALL_ROWS - ROW_MISC - 1, D_MODEL), F32)], axis=0)


def _unpack_small(r):
    gains, misc = r[ROW_GROUP_GAINS:ROW_GROUP_GAINS + 1], r[ROW_MISC:ROW_MISC + 1]
    return dict(
        w_pool=r[0:64].reshape(1, 4, POOL_CH, POOL_CH), g_mix_pre=r[ROW_G_MIX_PRE:ROW_G_MIX_PRE + 1],
        g_mix_post=r[ROW_G_MIX_POST:ROW_G_MIX_POST + 1], g_ffn_pre=r[ROW_G_FFN_PRE:ROW_G_FFN_PRE + 1],
        g_ffn_post=r[ROW_G_FFN_POST:ROW_G_FFN_POST + 1], g_ple=r[ROW_G_PLE:ROW_G_PLE + 1],
        g_attn_grp=gains[:, 0:D_ATTN], g_pool_grp=gains[:, D_ATTN:D_ATTN + D_POOL],
        pool_scale=misc[:, 0:D_POOL], b_forget=misc[:, COL_B_FORGET:COL_B_FORGET + HEADS])


def _step(x, p, tgt, small, in_w, in_m, in_v, rest_w, rest_m, rest_v):
    core = lax.axis_index("c").astype(jnp.int32).reshape(1)
    wqkv, wf, wu = _gather_w_in(in_w.astype(BF16))
    wpool = small["w_pool"].astype(BF16)
    bpad = jnp.pad(small["b_forget"], ((0, 0), (0, LANES - HEADS)))

    lay = _attn_layout_constants()
    rest_b = rest_w.astype(BF16)
    hn, qt3, ka, v, qat3, vt3, kt3, fl, y, mpre, gh = _pre_attn_fwd(x, small["g_mix_pre"], wqkv, wf, wu, bpad, wpool, lay,
                                                                 rest_b[0:OFF_GATE])
    a, lset3, gf = _attn_fwd(ka, qat3, vt3, rest_b[OFF_GATE:])
    wple_t = gh[:, OFF_PLE:OFF_PLE + ROWS_PLE].reshape(D_MODEL, D_PLE)
    mix, o, h1, hn2 = _post_attn_fwd(a, mpre, x, small["g_attn_grp"], small["g_pool_grp"], small["pool_scale"], gh,
                                     small["g_mix_post"], small["g_ffn_pre"])
    gate, up, act, ff, h2 = _ffn_fwd(hn2, gf, gf, gf, h1, small["g_ffn_post"])
    dh2, dff, dgl, dpp, h2b, pb, loss8, dg_ple, dg_ffn_post = _tail_fwd_bwd(
        h2, p, tgt, ff, wple_t, gh, small["g_ple"], small["g_ffn_post"])
    dgate, dup, dh1, dg_ffn_pre = _ffn_bwd(dff, gate, up, gf, gf, gf, h1, dh2, small["g_ffn_pre"])
    nd = N_DEV
    send_rest = [
        _wgrad(h2b, dgl, BF16, "wgrad_ple_gate").reshape(nd, SHARD_SQ, D_MODEL),
        _wgrad(dpp, pb, BF16, "wgrad_ple").reshape(nd, ROWS_PLE, D_MODEL),
        _wgrad(dgate, hn2, BF16, "wgrad_gate").reshape(nd, SHARD_FF, D_MODEL),
        _wgrad(dup, hn2, BF16, "wgrad_up").reshape(nd, SHARD_FF, D_MODEL),
        _wgrad(act, dff, BF16, "wgrad_down").reshape(nd, SHARD_FF, D_MODEL)]
    (dob, dat3, dlt3, dmpb, dy, dg_mix_post, dg_attn, dg_pool, dps), landed = _post_attn_bwd(
        dh1, o, a, mpre, gh, wpool, small["g_mix_post"], small["g_attn_grp"], small["g_pool_grp"], small["pool_scale"],
        send_rest)
    send_rest = [_wgrad(mix, dob, BF16, "wgrad_out").reshape(nd, SHARD_SQ, D_MODEL)] + send_rest
    pair_rest = _rs_pair_sum(core, send_rest, [0, OFF_PG, OFF_PLE, OFF_GATE, OFF_UP, OFF_DOWN], ROWS_REST,
                             "rs_pair_sum_rest", landed)

    small_part = _pack_small(_wgrad_pool(y, dmpb), jnp.zeros((1, D_MODEL), F32), dg_mix_post, dg_ffn_pre, dg_ffn_post, dg_ple,
                             dg_attn, dg_pool, dps, jnp.zeros((1, HEADS), F32), loss8[0:1, 0:1])
    dqt3, dkt3, dvt3, chips_rest, small_all = _attn_bwd(ka, v, kt3, qat3, qt3, dat3, lset3, dlt3, pair_rest, small_part)

    gx, dz, dg_mix_pre, db = _pre_attn_bwd(dqt3, dkt3, dvt3, fl, dy, x, dh1, small["g_mix_pre"], wqkv, wf, wu)

    pair_in = _rs_pair_sum(core, [_wgrad_in(dz, hn)], [0], ROWS_IN, "rs_pair_sum_in")

    small_late = _pack_small_late(dg_mix_pre, db[:, 0:HEADS])
    *upd_rest, chips_in, late_all = _reduce_update_rest(chips_rest, rest_w, rest_m, rest_v, pair_in, small_late)
    upd_in = _reduce_update_big(chips_in, in_w, in_m, in_v, ROWS_IN, "reduce_update_in")
    return gx, (small_all, late_all), upd_in, upd_rest


def kernel(x, p, g_mix_pre, w_in, b_forget, g_attn_grp, g_pool_grp, w_pool, pool_scale, w_out, g_mix_post, g_ffn_pre, w_ffn_gate, w_ffn_up, w_ffn_down, g_ffn_post, w_ple_proj, g_ple, w_ple_gate, loss_target, m_g_mix_pre, m_w_in, m_b_forget, m_g_attn_grp, m_g_pool_grp, m_w_pool, m_pool_scale, m_w_out, m_g_mix_post, m_g_ffn_pre, m_w_ffn_gate, m_w_ffn_up, m_w_ffn_down, m_g_ffn_post, m_w_ple_proj, m_g_ple, m_w_ple_gate, v_g_mix_pre, v_w_in, v_b_forget, v_g_attn_grp, v_g_pool_grp, v_w_pool, v_pool_scale, v_w_out, v_g_mix_post, v_g_ffn_pre, v_w_ffn_gate, v_w_ffn_up, v_w_ffn_down, v_g_ffn_post, v_w_ple_proj, v_g_ple, v_w_ple_gate):
    small = dict(w_pool=w_pool[0], g_mix_pre=g_mix_pre, g_mix_post=g_mix_post, g_ffn_pre=g_ffn_pre,
                 g_ffn_post=g_ffn_post, g_ple=g_ple, g_attn_grp=g_attn_grp, g_pool_grp=g_pool_grp,
                 pool_scale=pool_scale, b_forget=b_forget)
    gx, small_all, upd_in, upd_rest = _step(
        x[0], p[0, 0], loss_target[0], small, _pack_in(w_in), _pack_in(m_w_in), _pack_in(v_w_in),
        _pack_rest(w_out, w_ffn_gate, w_ffn_up, w_ffn_down, w_ple_proj, w_ple_gate),
        _pack_rest(m_w_out, m_w_ffn_gate, m_w_ffn_up, m_w_ffn_down, m_w_ple_proj, m_w_ple_gate),
        _pack_rest(v_w_out, v_w_ffn_gate, v_w_ffn_up, v_w_ffn_down, v_w_ple_proj, v_w_ple_gate))

    sm_w = _pack_small(w_pool, g_mix_pre, g_mix_post, g_ffn_pre, g_ffn_post, g_ple, g_attn_grp, g_pool_grp, pool_scale, b_forget)
    sm_m = _pack_small(m_w_pool, m_g_mix_pre, m_g_mix_post, m_g_ffn_pre, m_g_ffn_post, m_g_ple, m_g_attn_grp, m_g_pool_grp, m_pool_scale, m_b_forget)
    sm_v = _pack_small(v_w_pool, v_g_mix_pre, v_g_mix_post, v_g_ffn_pre, v_g_ffn_post, v_g_ple, v_g_attn_grp, v_g_pool_grp, v_pool_scale, v_b_forget)
    upd_small = _reduce_update_small(*small_all, sm_w, sm_m, sm_v)
    loss = upd_small[0][ROW_MISC, COL_LOSS]

    def leaves(k):
        b_out, b_gate, b_up, b_down, b_ple, b_pg = _unpack_rest(upd_rest[k])
        s = _unpack_small(upd_small[k])
        return (s["g_mix_pre"], _unpack_in(upd_in[k]), s["b_forget"], s["g_attn_grp"], s["g_pool_grp"], s["w_pool"],
                s["pool_scale"], b_out, s["g_mix_post"], s["g_ffn_pre"], b_gate, b_up, b_down, s["g_ffn_post"], b_ple,
                s["g_ple"], b_pg)

    return (loss, gx[None], *leaves(0), *leaves(1), *leaves(2), *leaves(3))
```

```python
import functools

import jax
import jax.numpy as jnp
from jax import lax
from jax.experimental import pallas as pl
from jax.experimental.pallas import tpu as pltpu

F32 = jnp.float32
BF16 = jnp.bfloat16
HIGHEST = lax.Precision.HIGHEST

D_MODEL = 1024
HEADS = 8
HEAD_DIM = 64
D_ATTN = HEADS * HEAD_DIM
POOL_WINDOWS = (2, 4, 8, 16)
POOL_CH = 128
D_POOL = POOL_CH * len(POOL_WINDOWS)
D_FF = 2816
D_PLE = 256
D_IN = 3 * D_ATTN + HEADS + D_POOL
RMS_EPS = 1e-6
N_DEV = 8

ADAM_LR = 0.001
ADAM_B1 = 0.9
ADAM_B2 = 0.999
ADAM_EPS = 1e-08
ADAM_WD = 0.01
ADAM_STEP = 10

LANES = 128
HALO = 16
TS = 512
TS_FF = 256
FF_ROW_PARTS = 1
TS_WGRAD = 1024
TM_WGRAD = 2176
TQ = 256
TN_FF = D_FF
NEG = -1e30
VMEM_LIMIT = 56 * 1024 * 1024

SHARD_IN = 257
ROWS_IN = 272
SHARD_FF = 352
SHARD_SQ = D_MODEL // N_DEV
ROWS_PLE = D_PLE * SHARD_SQ // D_MODEL
OFF_PG = SHARD_SQ
OFF_PLE = 2 * SHARD_SQ
OFF_GATE = SHARD_FF
OFF_UP = 2 * SHARD_FF
OFF_DOWN = 3 * SHARD_FF
ROWS_REST = 4 * SHARD_FF
TR_REST = SHARD_FF

SMALL_ROWS = 72
ROW_G_MIX_PRE, ROW_G_MIX_POST, ROW_G_FFN_PRE, ROW_G_FFN_POST, ROW_G_PLE = 64, 65, 66, 67, 68
ROW_GROUP_GAINS, ROW_MISC = 69, 70
COL_B_FORGET = D_POOL
COL_LOSS = D_POOL + HEADS


def _nn(a, b):
    return jnp.dot(a, b, preferred_element_type=F32)


def _nt(a, b):
    return lax.dot_general(a, b, (((1,), (1,)), ((), ())), preferred_element_type=F32)


def _tn(a, b):
    return lax.dot_general(a, b, (((0,), (0,)), ((), ())), preferred_element_type=F32)


def _rstd(v):
    return lax.rsqrt(jnp.mean(v * v, axis=-1, keepdims=True) + RMS_EPS)


def _rms_bwd(v, g, dy):
    r = _rstd(v)
    vh = v * r
    t = dy * g
    dv = r * (t - vh * jnp.mean(t * vh, axis=-1, keepdims=True))
    return dv, jnp.sum(dy * vh, axis=0, keepdims=True)


def _split3(v):
    hi = v.astype(BF16)
    rest = v - hi.astype(F32)
    mid = rest.astype(BF16)
    return hi, mid, (rest - mid.astype(F32)).astype(BF16)


def _mask_matmul(mask, v):
    hi, mid, lo = _split3(v)
    return _nn(mask, lo) + _nn(mask, mid) + _nn(mask, hi)


def _running_sum(v, reverse=False):
    tq = v.shape[0] // 2
    rr = lax.broadcasted_iota(jnp.int32, (tq, tq), 0)
    cc = lax.broadcasted_iota(jnp.int32, (tq, tq), 1)
    mask = ((cc >= rr) if reverse else (cc <= rr)).astype(BF16)
    top, bot = _mask_matmul(mask, v[0:tq]), _mask_matmul(mask, v[tq:])
    if reverse:
        top = top + bot[0:1, :]
    else:
        bot = bot + top[tq - 1:tq, :]
    return jnp.concatenate([top, bot], axis=0)


def _params(n_grid):
    return pltpu.CompilerParams(dimension_semantics=("arbitrary",) * n_grid, vmem_limit_bytes=VMEM_LIMIT)


def _row(i):
    return (i, 0)


def _fixed(*_):
    return (0, 0)


def _spec_square(part):
    return pl.BlockSpec((N_DEV, SHARD_SQ, D_MODEL), lambda *_: (0, part, 0))


def _spec_ff(part):
    return pl.BlockSpec((TN_FF // SHARD_FF, SHARD_FF, D_MODEL), lambda i, j: (j, part, 0), pipeline_mode=pl.Buffered(1))


assert TS == 2 * TQ and TN_FF % SHARD_FF == 0
_HALVES = (slice(0, TQ), slice(TQ, TS))

VMEM_WHOLE = pl.BlockSpec(memory_space=pltpu.VMEM)
SMEM_WHOLE = pl.BlockSpec(memory_space=pltpu.SMEM)
ANY = pl.BlockSpec(memory_space=pl.ANY)


LOG2E = 1.4426950408889634
VROWS = HEAD_DIM + 16
AUG = 128
BIAS_LANE = HEAD_DIM
ONE_LANE = HEAD_DIM + 3
SPARE_LANE = HEADS
PART_LANES = 16
assert SPARE_LANE < PART_LANES and 3 * PART_LANES <= LANES


def _attn_layout_constants():
    import numpy as np
    bias_k = np.zeros((LANES, HEADS * AUG), np.float32)
    bias_q = np.zeros((LANES, HEADS * AUG), np.float32)
    for h in range(HEADS):
        for part in range(3):
            bias_k[part * PART_LANES + h, h * AUG + BIAS_LANE + part] = -1.0
            bias_q[part * PART_LANES + h, h * AUG + ONE_LANE + part] = 1.0
            bias_k[SPARE_LANE, h * AUG + ONE_LANE + part] = 1.0
            bias_q[SPARE_LANE, h * AUG + BIAS_LANE + part] = 1.0
    after = np.concatenate([np.arange(h * AUG + HEAD_DIM, (h + 1) * AUG) for h in range(HEADS)])
    as_bf = lambda a: jnp.asarray(a, BF16)
    return dict(bias_k=as_bf(bias_k[:, after]), bias_q_t=as_bf(bias_q[:, after].T))


def _pre_attn_fwd(x, g1, wqkv, wf, wu, bpad, wpool, lay, own_block):
    s, d = x.shape
    nt = s // TS
    sub = TS // TQ

    def body(x_ref, g_ref, wqkv_ref, wf_ref, wu_ref, b_ref, wp_ref, bk_ref, bqt_ref, own_ref,
             hn_ref, qt_ref, ka_ref, v_ref, qat_ref, vt_ref, kt_ref, fl_ref, y_ref, mp_ref, all_ref,
             ubuf, ccar, cbuf, stage, send_sems, recv_sems, local_sem):
        i = pl.program_id(0)

        @pl.when(i == 0)
        def _():
            _gather_start(own_ref, all_ref, stage, send_sems, recv_sems, local_sem)
            ubuf[0:HALO, :] = jnp.zeros((HALO, D_POOL), F32)
            ccar[...] = jnp.zeros_like(ccar)

        @pl.when(i == max(nt - 2, 0))
        def _():
            _gather_pass_on(all_ref, send_sems, recv_sems)

        xv = x_ref[...]
        hn = (xv * _rstd(xv) * g_ref[...]).astype(BF16)
        hn_ref[...] = hn
        zq = _nt(hn, wqkv_ref[...])
        qt = (zq[:, 0:D_ATTN] * 0.125).astype(BF16).T
        qb = (zq[:, 0:D_ATTN] * (0.125 * LOG2E)).astype(BF16)
        kb = zq[:, D_ATTN:2 * D_ATTN].astype(BF16)
        vb = zq[:, 2 * D_ATTN:3 * D_ATTN].astype(BF16)
        v_ref[...] = vb

        fl = _nt(hn, wf_ref[...]) + b_ref[...]
        fl_ref[...] = fl
        logf = jax.nn.log_sigmoid(fl)
        c = _running_sum(logf) + ccar[...]
        cbuf[...] = c
        ccar[...] = cbuf[TS - 1:TS, :]
        hi, mid, lo = (part.astype(F32) for part in _split3(c * LOG2E))
        lane = lax.broadcasted_iota(jnp.int32, (TS, LANES), 1)
        later = jnp.where(lane < 2 * PART_LANES, pltpu.roll(mid, PART_LANES, 1), pltpu.roll(lo, 2 * PART_LANES, 1))
        parts = jnp.where(lane < PART_LANES, jnp.where(lane == SPARE_LANE, 1.0, hi), later).astype(BF16)
        extra = AUG - HEAD_DIM
        kbias = _nn(parts, bk_ref[...]).astype(BF16)
        for h in range(HEADS):
            ka_ref[:, h * AUG:h * AUG + HEAD_DIM] = kb[:, h * HEAD_DIM:(h + 1) * HEAD_DIM]
            ka_ref[:, h * AUG + HEAD_DIM:(h + 1) * AUG] = kbias[:, h * extra:(h + 1) * extra]
        qbt = qb.T
        qbias = _nt(bqt_ref[...], parts).astype(BF16)
        vt = vb.T
        kt = kb.T
        for a in range(sub):
            cols = slice(a * TQ, (a + 1) * TQ)
            for h in range(HEADS):
                qat_ref[a, h * AUG:h * AUG + HEAD_DIM, :] = qbt[h * HEAD_DIM:(h + 1) * HEAD_DIM, cols]
                qat_ref[a, h * AUG + HEAD_DIM:(h + 1) * AUG, :] = qbias[h * extra:(h + 1) * extra, cols]
            for ref, mat in ((qt_ref, qt), (kt_ref, kt), (vt_ref, vt)):
                for h in range(HEADS):
                    ref[a, h * VROWS:h * VROWS + HEAD_DIM, :] = mat[h * HEAD_DIM:(h + 1) * HEAD_DIM, cols]
                    ref[a, h * VROWS + HEAD_DIM:(h + 1) * VROWS, :] = jnp.ones((VROWS - HEAD_DIM, TQ), BF16)

        u = _nt(hn, wu_ref[...])
        ubuf[HALO:HALO + TS, :] = u
        t = i * TS + lax.broadcasted_iota(jnp.int32, (TS, 1), 0)
        for g, w in enumerate(POOL_WINDOWS):
            cols = slice(g * POOL_CH, (g + 1) * POOL_CH)
            sm = ubuf[:, cols]
            step = 1
            while step < w:
                sm = sm + pltpu.roll(sm, step, 0)
                step *= 2
            cnt = jnp.minimum(t + 1, w).astype(F32)
            yg = (sm[HALO:, :] / cnt - u[:, cols]).astype(BF16)
            y_ref[:, cols] = yg
            mp_ref[:, cols] = _nn(yg, wp_ref[g])
        ubuf[0:HALO, :] = u[TS - HALO:, :]

        @pl.when(i == nt - 1)
        def _():
            _gather_finish(own_ref, all_ref, send_sems, recv_sems)

    nq = s // TQ
    aug = HEADS * AUG
    outs = (
        jax.ShapeDtypeStruct((s, d), BF16), jax.ShapeDtypeStruct((nq, HEADS * VROWS, TQ), BF16),
        jax.ShapeDtypeStruct((s, aug), BF16), jax.ShapeDtypeStruct((s, D_ATTN), BF16),
        jax.ShapeDtypeStruct((nq, aug, TQ), BF16), jax.ShapeDtypeStruct((nq, HEADS * VROWS, TQ), BF16),
        jax.ShapeDtypeStruct((nq, HEADS * VROWS, TQ), BF16),
        jax.ShapeDtypeStruct((s, LANES), F32),
        jax.ShapeDtypeStruct((s, D_POOL), BF16), jax.ShapeDtypeStruct((s, D_POOL), F32),
        jax.ShapeDtypeStruct((N_DEV,) + own_block.shape, own_block.dtype),
    )
    fixed3 = lambda i: (0, 0, 0)
    tiles3 = lambda rows: pl.BlockSpec((sub, rows, TQ), lambda i: (i, 0, 0))
    return pl.pallas_call(
        body, grid=(nt,), out_shape=outs, name="pre_attn_fwd",
        in_specs=[pl.BlockSpec((TS, d), _row), pl.BlockSpec((1, d), _fixed),
                  pl.BlockSpec((3 * D_ATTN, d), _fixed), pl.BlockSpec(wf.shape, _fixed), pl.BlockSpec(wu.shape, _fixed),
                  pl.BlockSpec((1, LANES), _fixed), pl.BlockSpec(wpool.shape, fixed3),
                  pl.BlockSpec(lay["bias_k"].shape, _fixed), pl.BlockSpec(lay["bias_q_t"].shape, _fixed), ANY],
        out_specs=(pl.BlockSpec((TS, d), _row), tiles3(HEADS * VROWS),
                   pl.BlockSpec((TS, aug), _row), pl.BlockSpec((TS, D_ATTN), _row),
                   tiles3(aug), tiles3(HEADS * VROWS), tiles3(HEADS * VROWS),
                   pl.BlockSpec((TS, LANES), _row),
                   pl.BlockSpec((TS, D_POOL), _row), pl.BlockSpec((TS, D_POOL), _row), ANY),
        scratch_shapes=[pltpu.VMEM((TS + HALO, D_POOL), F32), pltpu.VMEM((1, LANES), F32), pltpu.VMEM((TS, LANES), F32),
                        pltpu.VMEM(own_block.shape, own_block.dtype),
                        pltpu.SemaphoreType.DMA((7,)), pltpu.SemaphoreType.DMA((7,)), pltpu.SemaphoreType.DMA],
        compiler_params=_params(1),
    )(x, g1, wqkv, wf, wu, bpad, wpool, lay["bias_k"], lay["bias_q_t"], own_block)


def _causal_in_tile():
    krow = lax.broadcasted_iota(jnp.int32, (TQ, TQ), 0)
    qcol = lax.broadcasted_iota(jnp.int32, (TQ, TQ), 1)
    return krow <= qcol


def _attn_fwd(ka, qat3, vt3, own_block):
    s = ka.shape[0]
    nq = s // TQ
    pass_on_step = max(nq - 2, 0)

    def body(qa_ref, ka_ref, vt_ref, own_ref, a_ref, lset_ref, all_ref, acc, out_t, st_scr, pt_scr,
             stage, send_sems, recv_sems, local_sem):
        i = pl.program_id(0)

        @pl.when(i == 0)
        def _():
            _gather_start(own_ref, all_ref, stage, send_sems, recv_sems, local_sem)

        @pl.when(i == pass_on_step)
        def _():
            _gather_pass_on(all_ref, send_sems, recv_sems)

        acc[...] = jnp.zeros_like(acc)

        def tile(j, stats, masked):
            tile_max = []
            for h in range(HEADS):
                aug = slice(h * AUG, (h + 1) * AUG)
                st = _nn(ka_ref[pl.ds(j * TQ, TQ), aug], qa_ref[0, aug, :])
                if masked:
                    st = jnp.where(_causal_in_tile(), st, NEG)
                st_scr[h] = st
                tile_max.append(jnp.max(st, axis=0, keepdims=True))
            new, scale = [], []
            for h in range(HEADS):
                m_new = jnp.maximum(stats[h], tile_max[h])
                scale.append(jnp.exp2(stats[h] - m_new))
                pt_scr[h] = jnp.exp2(st_scr[h] - m_new).astype(BF16)
                new.append(m_new)
            for h in range(HEADS):
                rows = slice(h * VROWS, (h + 1) * VROWS)
                acc[rows, :] = scale[h] * acc[rows, :] + _nn(vt_ref[j, rows, :], pt_scr[h])
            return tuple(new)

        init = tuple(jnp.full((1, TQ), NEG, F32) for _ in range(HEADS))
        stats = lax.fori_loop(0, i, functools.partial(tile, masked=False), init)
        stats = tile(i, stats, True)
        for h in range(HEADS):
            denom = acc[h * VROWS + HEAD_DIM:h * VROWS + HEAD_DIM + 1, :]
            out_t[h * HEAD_DIM:(h + 1) * HEAD_DIM, :] = acc[h * VROWS:h * VROWS + HEAD_DIM, :] / denom
            lset_ref[0, h:h + 1, :] = stats[h] + jnp.log2(denom)
        a_ref[...] = out_t[...].T

        @pl.when(i == nq - 1)
        def _():
            _gather_finish(own_ref, all_ref, send_sems, recv_sems)

    r, cdim = own_block.shape
    return pl.pallas_call(
        body, grid=(nq,), name="attn_fwd",
        out_shape=(jax.ShapeDtypeStruct((s, D_ATTN), F32), jax.ShapeDtypeStruct((nq, HEADS, TQ), F32),
                   jax.ShapeDtypeStruct((N_DEV, r, cdim), own_block.dtype)),
        in_specs=[pl.BlockSpec((1, HEADS * AUG, TQ), lambda i: (i, 0, 0)), VMEM_WHOLE, VMEM_WHOLE, ANY],
        out_specs=(pl.BlockSpec((TQ, D_ATTN), _row), pl.BlockSpec((1, HEADS, TQ), lambda i: (i, 0, 0)), ANY),
        scratch_shapes=[pltpu.VMEM((HEADS * VROWS, TQ), F32), pltpu.VMEM((D_ATTN, TQ), F32),
                        pltpu.VMEM((HEADS, TQ, TQ), F32), pltpu.VMEM((HEADS, TQ, TQ), BF16),
                        pltpu.VMEM((r, cdim), own_block.dtype),
                        pltpu.SemaphoreType.DMA((7,)), pltpu.SemaphoreType.DMA((7,)), pltpu.SemaphoreType.DMA],
        compiler_params=_params(1),
    )(qat3, ka, vt3, own_block)


def _post_attn_fwd(a, mpre, x, g_attn, g_pool, pscale, wout, g_post, g_ffn_pre):
    s, d = x.shape

    def body(a_ref, mp_ref, x_ref, ga_ref, gp_ref, ps_ref, wo_ref, gpost_ref, gpre_ref,
             mix_ref, o_ref, h1_ref, hn2_ref):
        for rows in _HALVES:
            av = a_ref[rows, :]
            mix_ref[rows, 0:D_ATTN] = (av * _rstd(av) * ga_ref[...]).astype(BF16)
            mv = mp_ref[rows, :] * ps_ref[...]
            mix_ref[rows, D_ATTN:] = (mv * _rstd(mv) * gp_ref[...]).astype(BF16)
            o = _nn(mix_ref[rows, :], wo_ref[...].reshape(d, d))
            o_ref[rows, :] = o
            h1 = x_ref[rows, :] + o * _rstd(o) * gpost_ref[...]
            h1_ref[rows, :] = h1
            hn2_ref[rows, :] = (h1 * _rstd(h1) * gpre_ref[...]).astype(BF16)

    vec = lambda n: pl.BlockSpec((1, n), _fixed)
    return pl.pallas_call(
        body, grid=(s // TS,), name="post_attn_fwd",
        out_shape=(jax.ShapeDtypeStruct((s, d), BF16), jax.ShapeDtypeStruct((s, d), F32),
                   jax.ShapeDtypeStruct((s, d), F32), jax.ShapeDtypeStruct((s, d), BF16)),
        in_specs=[pl.BlockSpec((TS, D_ATTN), _row), pl.BlockSpec((TS, D_POOL), _row), pl.BlockSpec((TS, d), _row),
                  vec(D_ATTN), vec(D_POOL), vec(D_POOL), _spec_square(0), vec(d), vec(d)],
        out_specs=(pl.BlockSpec((TS, d), _row),) * 4,
        compiler_params=_params(1),
    )(a, mpre, x, g_attn, g_pool, pscale, wout, g_post, g_ffn_pre)


def _ffn_fwd(hn2, wg, wu, wd, h1, g_post):
    s, d = h1.shape
    nc = D_FF // TN_FF
    ts = min(TS_FF, s)

    def body(hn_ref, wg_ref, wu_ref, wd_ref, h1_ref, g_ref, gate_ref, up_ref, act_ref, ff_ref, h2_ref, acc):
        j = pl.program_id(1)

        @pl.when(j == 0)
        def _():
            acc[...] = jnp.zeros_like(acc)

        for r in range(FF_ROW_PARTS):
            rows = slice(r * (ts // FF_ROW_PARTS), (r + 1) * (ts // FF_ROW_PARTS))
            hn = hn_ref[rows, :]
            gt = _nt(hn, wg_ref[...].reshape(TN_FF, d))
            up = _nt(hn, wu_ref[...].reshape(TN_FF, d))
            gate_ref[rows, :] = gt.astype(BF16)
            up_ref[rows, :] = up.astype(BF16)
            act_ref[rows, :] = (gt * jax.nn.sigmoid(gt) * up).astype(BF16)
            acc[rows, :] += _nn(act_ref[rows, :], wd_ref[...].reshape(TN_FF, d))

        @pl.when(j == nc - 1)
        def _():
            ff = acc[...]
            ff_ref[...] = ff
            h2_ref[...] = h1_ref[...] + ff * _rstd(ff) * g_ref[...]

    rowblk = pl.BlockSpec((ts, d), lambda i, j: (i, 0))
    chunk = pl.BlockSpec((ts, TN_FF), lambda i, j: (i, j))
    return pl.pallas_call(
        body, grid=(s // ts, nc), name="ffn_fwd",
        out_shape=(jax.ShapeDtypeStruct((s, D_FF), BF16),) * 3 + (jax.ShapeDtypeStruct((s, d), F32),) * 2,
        in_specs=[rowblk, _spec_ff(0), _spec_ff(1), _spec_ff(2), rowblk, pl.BlockSpec((1, d), lambda i, j: (0, 0))],
        out_specs=(chunk, chunk, chunk, rowblk, rowblk),
        scratch_shapes=[pltpu.VMEM((ts, d), F32)],
        compiler_params=_params(2),
    )(hn2, wg, wu, wd, h1, g_post)


def _tail_fwd_bwd(h2, p, tgt, ff, wple, wpg, g_ple, g_ffn_post):
    s, d = h2.shape

    def body(h2_ref, p_ref, t_ref, ff_ref, wple_ref, wpg_ref, gple_ref, gfp_ref,
             dh2_ref, dff_ref, dgl_ref, dpp_ref, h2b_ref, pb_ref, loss_ref, dgple_ref, dgfp_ref):
        i = pl.program_id(0)

        @pl.when(i == 0)
        def _():
            loss_ref[...] = jnp.zeros_like(loss_ref)
            dgple_ref[...] = jnp.zeros_like(dgple_ref)
            dgfp_ref[...] = jnp.zeros_like(dgfp_ref)

        h2 = h2_ref[...]
        h2b = h2.astype(BF16)
        h2b_ref[...] = h2b
        pb = p_ref[...].astype(BF16)
        pb_ref[...] = pb
        pp = _nt(pb, wple_ref[...])
        gple = gple_ref[...]
        e = pp * _rstd(pp) * gple
        wpg = wpg_ref[...].reshape(d, d)
        sg = jax.nn.sigmoid(_nn(h2b, wpg))
        diff = h2 + sg * e - t_ref[...]
        sq = jnp.sum(jnp.sum(diff * diff, axis=1, keepdims=True), axis=0, keepdims=True)
        loss_ref[...] += jnp.broadcast_to(sq * (0.5 / d), loss_ref.shape)
        dh3 = diff * (1.0 / d)
        dgl = (dh3 * e * sg * (1.0 - sg)).astype(BF16)
        dgl_ref[...] = dgl
        dh2 = dh3 + _nt(dgl, wpg)
        dh2_ref[...] = dh2
        dpp, dg = _rms_bwd(pp, gple, dh3 * sg)
        dpp_ref[...] = dpp.astype(BF16)
        dgple_ref[...] += dg
        dff, dg = _rms_bwd(ff_ref[...], gfp_ref[...], dh2)
        dff_ref[...] = dff.astype(BF16)
        dgfp_ref[...] += dg

    rowblk = pl.BlockSpec((TS, d), _row)
    vec = pl.BlockSpec((1, d), _fixed)
    return pl.pallas_call(
        body, grid=(s // TS,), name="tail_fwd_bwd",
        out_shape=(jax.ShapeDtypeStruct((s, d), F32), jax.ShapeDtypeStruct((s, d), BF16),
                   jax.ShapeDtypeStruct((s, d), BF16), jax.ShapeDtypeStruct((s, d), BF16),
                   jax.ShapeDtypeStruct((s, d), BF16), jax.ShapeDtypeStruct((s, D_PLE), BF16),
                   jax.ShapeDtypeStruct((8, LANES), F32), jax.ShapeDtypeStruct((1, d), F32),
                   jax.ShapeDtypeStruct((1, d), F32)),
        in_specs=[rowblk, pl.BlockSpec((TS, D_PLE), _row), rowblk, rowblk,
                  pl.BlockSpec(wple.shape, _fixed), _spec_square(1), vec, vec],
        out_specs=(rowblk, rowblk, rowblk, rowblk, rowblk, pl.BlockSpec((TS, D_PLE), _row),
                   pl.BlockSpec((8, LANES), _fixed), vec, vec),
        compiler_params=_params(1),
    )(h2, p, tgt, ff, wple, wpg, g_ple, g_ffn_post)


def _ffn_bwd(dff, gate, up, wd, wg, wu, h1, dh2, g_pre):
    s, d = h1.shape
    nc = D_FF // TN_FF
    ts = min(TS_FF, s)

    def body(dff_ref, gate_ref, up_ref, wd_ref, wg_ref, wu_ref, h1_ref, dh2_ref, g_ref,
             dgate_ref, dup_ref, dh1_ref, dg_ref, acc):
        i = pl.program_id(0)
        j = pl.program_id(1)

        @pl.when((i == 0) & (j == 0))
        def _():
            dg_ref[...] = jnp.zeros_like(dg_ref)

        @pl.when(j == 0)
        def _():
            acc[...] = jnp.zeros_like(acc)

        for r in range(FF_ROW_PARTS):
            rows = slice(r * (ts // FF_ROW_PARTS), (r + 1) * (ts // FF_ROW_PARTS))
            dact = _nt(dff_ref[rows, :], wd_ref[...].reshape(TN_FF, d))
            gt = gate_ref[rows, :].astype(F32)
            sg = jax.nn.sigmoid(gt)
            dup_ref[rows, :] = (dact * gt * sg).astype(BF16)
            dgate_ref[rows, :] = (dact * up_ref[rows, :].astype(F32) * (sg * (1.0 + gt * (1.0 - sg)))).astype(BF16)
            acc[rows, :] += (_nn(dgate_ref[rows, :], wg_ref[...].reshape(TN_FF, d))
                             + _nn(dup_ref[rows, :], wu_ref[...].reshape(TN_FF, d)))

        @pl.when(j == nc - 1)
        def _():
            dv, dg = _rms_bwd(h1_ref[...], g_ref[...], acc[...])
            dh1_ref[...] = dh2_ref[...] + dv
            dg_ref[...] += dg

    rowblk = pl.BlockSpec((ts, d), lambda i, j: (i, 0))
    chunk = pl.BlockSpec((ts, TN_FF), lambda i, j: (i, j))
    vec = pl.BlockSpec((1, d), lambda i, j: (0, 0))
    return pl.pallas_call(
        body, grid=(s // ts, nc), name="ffn_bwd",
        out_shape=(jax.ShapeDtypeStruct((s, D_FF), BF16), jax.ShapeDtypeStruct((s, D_FF), BF16),
                   jax.ShapeDtypeStruct((s, d), F32), jax.ShapeDtypeStruct((1, d), F32)),
        in_specs=[rowblk, chunk, chunk, _spec_ff(2), _spec_ff(0), _spec_ff(1), rowblk, rowblk, vec],
        out_specs=(chunk, chunk, rowblk, vec),
        scratch_shapes=[pltpu.VMEM((ts, d), F32)],
        compiler_params=_params(2),
    )(dff, gate, up, wd, wg, wu, h1, dh2, g_pre)


def _post_attn_bwd(dh1, o, a, mpre, wout, wpool, g_post, g_attn, g_pool, pscale, send):
    s, d = dh1.shape
    sub = TS // TQ
    npc = len(send)

    def body(dh1_ref, o_ref, a_ref, mp_ref, wo_ref, wp_ref, gpost_ref, ga_ref, gp_ref, ps_ref, *refs):
        send_refs, refs = refs[:npc], refs[npc:]
        dob_ref, dat_ref, dlt_ref, dmpb_ref, dy_ref, dgpost_ref, dga_ref, dgp_ref, dps_ref = refs[:9]
        got_refs, (send_sems, recv_sems) = refs[9:9 + npc], refs[9 + npc:]
        i = pl.program_id(0)

        @pl.when(i == 0)
        def _():
            for cp in _pair_copies(send_refs, got_refs, send_sems, recv_sems):
                cp.start()
            dgpost_ref[...] = jnp.zeros_like(dgpost_ref)
            dga_ref[...] = jnp.zeros_like(dga_ref)
            dgp_ref[...] = jnp.zeros_like(dgp_ref)
            dps_ref[...] = jnp.zeros_like(dps_ref)

        do, dg = _rms_bwd(o_ref[...], gpost_ref[...], dh1_ref[...])
        dgpost_ref[...] += dg
        dob = do.astype(BF16)
        dob_ref[...] = dob
        dmix = _nt(dob, wo_ref[...].reshape(d, d))

        av = a_ref[...]
        da, dg = _rms_bwd(av, ga_ref[...], dmix[:, 0:D_ATTN])
        dga_ref[...] += dg
        dat = da.astype(BF16).T
        hsel = (lax.shift_right_logical(lax.broadcasted_iota(jnp.int32, (HEADS, D_ATTN), 1), 6)
                == lax.broadcasted_iota(jnp.int32, (HEADS, D_ATTN), 0)).astype(F32)
        dlt = lax.dot_general(hsel, da * av, (((1,), (1,)), ((), ())), precision=HIGHEST, preferred_element_type=F32)
        for q in range(sub):
            dlt_ref[q] = dlt[:, q * TQ:(q + 1) * TQ]
            dat_ref[q] = dat[:, q * TQ:(q + 1) * TQ]

        ps = ps_ref[...]
        mp = mp_ref[...]
        dm, dg = _rms_bwd(mp * ps, gp_ref[...], dmix[:, D_ATTN:])
        dgp_ref[...] += dg
        dps_ref[...] += jnp.sum(dm * mp, axis=0, keepdims=True)
        dmpb = (dm * ps).astype(BF16)
        dmpb_ref[...] = dmpb
        for g in range(len(POOL_WINDOWS)):
            cols = slice(g * POOL_CH, (g + 1) * POOL_CH)
            dy_ref[:, cols] = _nt(dmpb[:, cols], wp_ref[g])

        @pl.when(i == s // TS - 1)
        def _():
            for cp in _pair_copies(send_refs, got_refs, send_sems, recv_sems):
                cp.wait()

    rowblk = pl.BlockSpec((TS, d), _row)
    half = pl.BlockSpec((TS, D_ATTN), _row)
    vec = lambda n: pl.BlockSpec((1, n), _fixed)
    nk = N_DEV // 2
    res = pl.pallas_call(
        body, grid=(s // TS,), name="post_attn_bwd",
        out_shape=(jax.ShapeDtypeStruct((s, d), BF16), jax.ShapeDtypeStruct((s // TQ, D_ATTN, TQ), BF16),
                   jax.ShapeDtypeStruct((s // TQ, HEADS, TQ), F32), jax.ShapeDtypeStruct((s, D_POOL), BF16),
                   jax.ShapeDtypeStruct((s, D_POOL), F32), jax.ShapeDtypeStruct((1, d), F32),
                   jax.ShapeDtypeStruct((1, D_ATTN), F32), jax.ShapeDtypeStruct((1, D_POOL), F32),
                   jax.ShapeDtypeStruct((1, D_POOL), F32))
        + tuple(jax.ShapeDtypeStruct((nk,) + t.shape[1:], t.dtype) for t in send),
        in_specs=[rowblk, rowblk, half, half, _spec_square(0),
                  pl.BlockSpec(wpool.shape, lambda i: (0, 0, 0)), vec(d), vec(D_ATTN), vec(D_POOL), vec(D_POOL)]
        + [ANY] * npc,
        out_specs=(rowblk, pl.BlockSpec((sub, D_ATTN, TQ), lambda i: (i, 0, 0)),
                   pl.BlockSpec((sub, HEADS, TQ), lambda i: (i, 0, 0)), half, half,
                   vec(d), vec(D_ATTN), vec(D_POOL), vec(D_POOL)) + (ANY,) * npc,
        scratch_shapes=[pltpu.SemaphoreType.DMA((nk, npc)), pltpu.SemaphoreType.DMA((nk, npc))],
        compiler_params=_params(1),
    )(dh1, o, a, mpre, wout, wpool, g_post, g_attn, g_pool, pscale, *send)
    return res[:9], list(res[9:])


def _attn_bwd(ka, v, kt3, qat3, qt3, dot3, lset3, dlt3, chip_blocks, small_block):
    s = ka.shape[0]
    nq = s // TQ

    def body(ka_ref, v_ref, kt_ref, qat_ref, qt_ref, dot_ref, lset_ref, dlt_ref, b_ref, sm_ref,
             dqt_ref, dkt_ref, dvt_ref, got_ref, all_ref, pt_scr, ptb_scr, dsb_scr,
             stage, send_sems, recv_sems, local_sem, stage_s, send_s, recv_s, local_s):
        j = pl.program_id(0)

        @pl.when(j == 0)
        def _():
            _chips_start(b_ref, got_ref, stage, send_sems, recv_sems, local_sem)
            _gather_start(sm_ref, all_ref, stage_s, send_s, recv_s, local_s)
            dqt_ref[...] = jnp.zeros_like(dqt_ref)

        @pl.when(j == max(nq - 2, 0))
        def _():
            _gather_pass_on(all_ref, send_s, recv_s)

        def tile(i, masked):
            def accumulate(ref, idx, val):
                if masked:
                    ref[idx] = val
                else:
                    ref[idx] += val

            for h in range(HEADS):
                aug = slice(h * AUG, (h + 1) * AUG)
                st = _nn(ka_ref[:, aug], qat_ref[i, aug, :]) - lset_ref[i, h:h + 1, :]
                if masked:
                    st = jnp.where(_causal_in_tile(), st, NEG)
                pt = jnp.exp2(st)
                pt_scr[h] = pt
                ptb_scr[h] = pt.astype(BF16)
            heads = [(h, slice(h * HEAD_DIM, (h + 1) * HEAD_DIM)) for h in range(HEADS)]
            for h, hs in heads:
                dst = pt_scr[h] * (_nn(v_ref[:, hs], dot_ref[i, hs, :]) - dlt_ref[i, h:h + 1, :])
                dsb_scr[h] = dst.astype(BF16)
            for h, hs in heads:
                accumulate(dvt_ref, (0, hs, slice(None)), _nt(dot_ref[i, hs, :], ptb_scr[h]))
            for h, hs in heads:
                rows = slice(h * VROWS, (h + 1) * VROWS)
                accumulate(dkt_ref, (0, rows, slice(None)), _nt(qt_ref[i, rows, :], dsb_scr[h]))
            for h, hs in heads:
                rows = slice(h * VROWS, (h + 1) * VROWS)
                dqt_ref[i, rows, :] += _nn(kt_ref[0, rows, :], dsb_scr[h])

        first = j + 1
        pairs = (nq - first) // 2

        def step(p, carry):
            tile(first + 2 * p, False)
            tile(first + 2 * p + 1, False)
            return carry

        tile(j, True)
        lax.fori_loop(0, pairs, step, 0)

        @pl.when(first + 2 * pairs < nq)
        def _():
            tile(nq - 1, False)

        @pl.when(j == nq - 1)
        def _():
            _chips_finish(b_ref, got_ref, send_sems, recv_sems)
            _gather_finish(sm_ref, all_ref, send_s, recv_s)

    blk = pl.BlockSpec((TQ, D_ATTN), _row)
    tile_t = lambda rows: pl.BlockSpec((1, rows, TQ), lambda j: (j, 0, 0))
    per_tile = lambda rows: jax.ShapeDtypeStruct((nq, rows, TQ), F32)
    _, r, cdim = chip_blocks.shape
    dma = pltpu.SemaphoreType.DMA
    return pl.pallas_call(
        body, grid=(nq,), name="attn_bwd",
        out_shape=(per_tile(HEADS * VROWS), per_tile(HEADS * VROWS), per_tile(D_ATTN),
                   jax.ShapeDtypeStruct(chip_blocks.shape, chip_blocks.dtype),
                   jax.ShapeDtypeStruct((N_DEV,) + small_block.shape, small_block.dtype)),
        in_specs=[pl.BlockSpec((TQ, HEADS * AUG), _row), blk, tile_t(HEADS * VROWS),
                  VMEM_WHOLE, VMEM_WHOLE, VMEM_WHOLE, VMEM_WHOLE, VMEM_WHOLE, ANY, ANY],
        out_specs=(pl.BlockSpec((nq, HEADS * VROWS, TQ), lambda j: (0, 0, 0)), tile_t(HEADS * VROWS), tile_t(D_ATTN),
                   ANY, ANY),
        scratch_shapes=[pltpu.VMEM((HEADS, TQ, TQ), F32), pltpu.VMEM((HEADS, TQ, TQ), BF16),
                        pltpu.VMEM((HEADS, TQ, TQ), BF16), pltpu.VMEM((r, cdim), chip_blocks.dtype),
                        dma((3,)), dma((3,)), dma,
                        pltpu.VMEM(small_block.shape, small_block.dtype), dma((7,)), dma((7,)), dma],
        compiler_params=_params(1),
    )(ka, v, kt3, qat3, qt3, dot3, lset3, dlt3, chip_blocks, small_block)


def _pre_attn_bwd(dqt3, dkt3, dvt3, fl, dy, x, dh1, g1, wqkv, wf, wu):
    s, d = x.shape
    nt = s // TS
    n = TS + HALO
    sub = TS // TQ
    qkv, fcols = 3 * D_ATTN, 3 * D_ATTN + LANES

    def body(dqt_ref, dkt_ref, dvt_ref, fl_ref, dy_ref, x_ref, dh1_ref, g_ref, wqkv_ref, wf_ref, wu_ref,
             gx_ref, dz_ref, dg_ref, db_ref, ybuf, ccar, dlog, dsum):
        dqkv_ref = dz_ref.at[:, 0:qkv]
        dfb_ref = dz_ref.at[:, qkv:fcols]
        dub_ref = dz_ref.at[:, fcols:]
        i = pl.program_id(0)
        ti = nt - 1 - i

        @pl.when(i == 0)
        def _():
            ybuf[TS:n, :] = jnp.zeros((HALO, D_POOL), F32)
            ccar[...] = jnp.zeros_like(ccar)
            dg_ref[...] = jnp.zeros_like(dg_ref)
            db_ref[...] = jnp.zeros_like(db_ref)
            dsum[...] = jnp.zeros_like(dsum)

        for a in range(sub):
            for h in range(HEADS):
                r = h * VROWS + HEAD_DIM
                dsum[h:h + 1, a * TQ:(a + 1) * TQ] = dqt_ref[a, r:r + 1, :] - dkt_ref[a, r:r + 1, :]
        dlog[...] = ccar[...] + _running_sum(dsum[...].T, reverse=True)
        ccar[...] = dlog[0:1, :]
        df = dlog[...] * jax.nn.sigmoid(-fl_ref[...])
        db_ref[...] += jnp.sum(df, axis=0, keepdims=True)
        dfb = df.astype(BF16)
        dfb_ref[...] = dfb

        t = ti * TS + lax.broadcasted_iota(jnp.int32, (TS, 1), 0)
        dy = dy_ref[...]
        for g, w in enumerate(POOL_WINDOWS):
            cols = slice(g * POOL_CH, (g + 1) * POOL_CH)
            ybuf[0:TS, cols] = dy[:, cols] / jnp.minimum(t + 1, w).astype(F32)
        for g, w in enumerate(POOL_WINDOWS):
            cols = slice(g * POOL_CH, (g + 1) * POOL_CH)
            sm = ybuf[:, cols]
            step = 1
            while step < w:
                sm = sm + pltpu.roll(sm, n - step, 0)
                step *= 2
            dub_ref[:, cols] = (sm[0:TS, :] - dy[:, cols]).astype(BF16)
        ybuf[TS:n, :] = ybuf[0:HALO, :]

        for a in range(sub):
            rows = slice(a * TQ, (a + 1) * TQ)
            for h in range(HEADS):
                src = slice(h * VROWS, h * VROWS + HEAD_DIM)
                dqkv_ref[rows, h * HEAD_DIM:(h + 1) * HEAD_DIM] = (dqt_ref[a, src, :].T * 0.125).astype(BF16)
                dqkv_ref[rows, D_ATTN + h * HEAD_DIM:D_ATTN + (h + 1) * HEAD_DIM] = dkt_ref[a, src, :].T.astype(BF16)
            dqkv_ref[rows, 2 * D_ATTN:] = dvt_ref[a].T.astype(BF16)
        dhn = _nn(dqkv_ref[...], wqkv_ref[...]) + _nn(dfb, wf_ref[...]) + _nn(dub_ref[...], wu_ref[...])
        dx, dg = _rms_bwd(x_ref[...], g_ref[...], dhn)
        gx_ref[...] = dh1_ref[...] + dx
        dg_ref[...] += dg

    rev = lambda i: (nt - 1 - i, 0)
    blk = lambda w: pl.BlockSpec((TS, w), rev)
    return pl.pallas_call(
        body, grid=(nt,), name="pre_attn_bwd",
        out_shape=(jax.ShapeDtypeStruct((s, d), F32), jax.ShapeDtypeStruct((s, fcols + D_POOL), BF16),
                   jax.ShapeDtypeStruct((1, d), F32), jax.ShapeDtypeStruct((1, LANES), F32)),
        in_specs=[pl.BlockSpec((sub, HEADS * VROWS, TQ), lambda i: (nt - 1 - i, 0, 0)),
                  pl.BlockSpec((sub, HEADS * VROWS, TQ), lambda i: (nt - 1 - i, 0, 0)),
                  pl.BlockSpec((sub, D_ATTN, TQ), lambda i: (nt - 1 - i, 0, 0)),
                  blk(LANES), blk(D_POOL), blk(d), blk(d),
                  pl.BlockSpec((1, d), _fixed), pl.BlockSpec((qkv, d), _fixed), pl.BlockSpec(wf.shape, _fixed),
                  pl.BlockSpec(wu.shape, _fixed)],
        out_specs=(blk(d), blk(fcols + D_POOL), pl.BlockSpec((1, d), _fixed), pl.BlockSpec((1, LANES), _fixed)),
        scratch_shapes=[pltpu.VMEM((n, D_POOL), F32), pltpu.VMEM((1, LANES), F32), pltpu.VMEM((TS, LANES), F32),
                        pltpu.VMEM((LANES, TS), F32)],
        compiler_params=_params(1),
    )(dqt3, dkt3, dvt3, fl, dy, x, dh1, g1, wqkv, wf, wu)


def _wgrad(a, b, out_dtype, name):
    s, m = a.shape
    n = b.shape[1]
    tm = max(t for t in range(LANES, min(m, TM_WGRAD) + 1, LANES) if m % t == 0)
    ts = min(TS_WGRAD, s)
    ns = s // ts

    def body(a_ref, b_ref, o_ref, acc):
        i = pl.program_id(1)

        @pl.when(i == 0)
        def _():
            acc[...] = jnp.zeros_like(acc)

        acc[...] += _tn(a_ref[...], b_ref[...])

        @pl.when(i == ns - 1)
        def _():
            o_ref[...] = acc[...].astype(out_dtype)

    return pl.pallas_call(
        body, grid=(m // tm, ns), name=name, out_shape=jax.ShapeDtypeStruct((m, n), out_dtype),
        in_specs=[pl.BlockSpec((ts, tm), lambda j, i: (i, j)), pl.BlockSpec((ts, n), lambda j, i: (i, 0))],
        out_specs=pl.BlockSpec((tm, n), lambda j, i: (j, 0)),
        scratch_shapes=[pltpu.VMEM((tm, n), F32)],
        compiler_params=_params(2),
    )(a, b)


def _wgrad_in(dz, hn):
    s, m = dz.shape
    n = hn.shape[1]
    ts = min(TS_WGRAD, s)
    ns = s // ts
    pad_at, pad = 3 * D_ATTN + HEADS, LANES - HEADS
    assert m == D_IN + pad and N_DEV * SHARD_IN == D_IN

    def pieces(d):
        lo, hi = d * SHARD_IN, (d + 1) * SHARD_IN
        spans = [(lo, min(hi, pad_at), 0), (max(lo, pad_at), hi, pad)]
        return [(a + shift, b - a, a - lo) for a, b, shift in spans if b > a]

    def body(a_ref, b_ref, o_ref, acc, stage):
        i = pl.program_id(0)

        @pl.when(i == 0)
        def _():
            acc[...] = jnp.zeros_like(acc)

        acc[...] += _tn(a_ref[...], b_ref[...])

        @pl.when(i == ns - 1)
        def _():
            stage[SHARD_IN:ROWS_IN, :] = jnp.zeros((ROWS_IN - SHARD_IN, n), F32)
            for d in range(N_DEV):
                for src, rows, dst in pieces(d):
                    stage[dst:dst + rows, :] = acc[src:src + rows, :]
                o_ref[d] = stage[...].astype(BF16)

    return pl.pallas_call(
        body, grid=(ns,), name="wgrad_in", out_shape=jax.ShapeDtypeStruct((N_DEV, ROWS_IN, n), BF16),
        in_specs=[pl.BlockSpec((ts, m), _row), pl.BlockSpec((ts, n), _row)],
        out_specs=pl.BlockSpec((N_DEV, ROWS_IN, n), lambda i: (0, 0, 0)),
        scratch_shapes=[pltpu.VMEM((m, n), F32), pltpu.VMEM((ROWS_IN, n), F32)],
        compiler_params=_params(1),
    )(dz, hn)


def _adamw(w, g, m, v):
    m = ADAM_B1 * m + (1.0 - ADAM_B1) * g
    v = ADAM_B2 * v + (1.0 - ADAM_B2) * (g * g)
    m_hat = m / (1.0 - ADAM_B1 ** ADAM_STEP)
    v_hat = v / (1.0 - ADAM_B2 ** ADAM_STEP)
    delta = -ADAM_LR * (m_hat / (jnp.sqrt(v_hat) + ADAM_EPS) + ADAM_WD * w)
    return delta, m, v


def _sum_update(p_ref, w_ref, m_ref, v_ref, g_ref, d_ref, nm_ref, nv_ref):
    g = p_ref[0].astype(F32)
    for k in range(1, p_ref.shape[0]):
        g = g + p_ref[k].astype(F32)
    g_ref[...] = g
    d_ref[...], nm_ref[...], nv_ref[...] = _adamw(w_ref[...], g, m_ref[...], v_ref[...])


def _reduce_update_rest(parts, w, m, v, chip_blocks, small_block):
    nk, r, c = parts.shape
    ns = r // TR_REST

    def body(p_ref, w_ref, m_ref, v_ref, b_ref, sm_ref, g_ref, d_ref, nm_ref, nv_ref, got_ref, all_ref,
             stage_b, stage_s, send_b, recv_b, local_b, send_s, recv_s, local_s):
        i = pl.program_id(0)

        @pl.when(i == 0)
        def _():
            _chips_start(b_ref, got_ref, stage_b, send_b, recv_b, local_b)
            _gather_start(sm_ref, all_ref, stage_s, send_s, recv_s, local_s)

        _sum_update(p_ref, w_ref, m_ref, v_ref, g_ref, d_ref, nm_ref, nv_ref)

        @pl.when(i == ns - 1)
        def _():
            _gather_pass_on(all_ref, send_s, recv_s)
            _chips_finish(b_ref, got_ref, send_b, recv_b)
            _gather_finish(sm_ref, all_ref, send_s, recv_s)

    blk = pl.BlockSpec((TR_REST, c), _row)
    out = jax.ShapeDtypeStruct((r, c), F32)
    dma = pltpu.SemaphoreType.DMA
    return pl.pallas_call(
        body, grid=(ns,), name="reduce_update_rest",
        out_shape=(out,) * 4 + (jax.ShapeDtypeStruct(chip_blocks.shape, chip_blocks.dtype),
                                jax.ShapeDtypeStruct((N_DEV,) + small_block.shape, small_block.dtype)),
        in_specs=[pl.BlockSpec((nk, TR_REST, c), lambda i: (0, i, 0)), blk, blk, blk, ANY, ANY],
        out_specs=(blk,) * 4 + (ANY, ANY),
        scratch_shapes=[pltpu.VMEM(chip_blocks.shape[1:], chip_blocks.dtype), pltpu.VMEM(small_block.shape, small_block.dtype),
                        dma((3,)), dma((3,)), dma, dma((7,)), dma((7,)), dma],
        compiler_params=_params(1),
    )(parts, w, m, v, chip_blocks, small_block)


def _reduce_update_big(parts, w, m, v, tr, name):
    nk, r, c = parts.shape

    def body(p_ref, w_ref, m_ref, v_ref, g_ref, d_ref, nm_ref, nv_ref):
        _sum_update(p_ref, w_ref, m_ref, v_ref, g_ref, d_ref, nm_ref, nv_ref)

    blk = pl.BlockSpec((tr, c), _row)
    out = jax.ShapeDtypeStruct((r, c), F32)
    return pl.pallas_call(
        body, grid=(r // tr,), name=name, out_shape=(out,) * 4,
        in_specs=[pl.BlockSpec((nk, tr, c), lambda i: (0, i, 0)), blk, blk, blk],
        out_specs=(blk,) * 4, compiler_params=_params(1),
    )(parts, w, m, v)


def _reduce_update_small(parts, late, w, m, v):
    nd = parts.shape[0]
    first = parts.shape[1] - late.shape[1]

    def body(p_ref, q_ref, w_ref, m_ref, v_ref, g_ref, d_ref, nm_ref, nv_ref):
        g, t = p_ref[0], q_ref[0]
        for k in range(1, nd):
            g, t = g + p_ref[k], t + q_ref[k]
        g_ref[...] = g
        g_ref[first:, :] = g[first:, :] + t
        d_ref[...], nm_ref[...], nv_ref[...] = _adamw(w_ref[...], g_ref[...], m_ref[...], v_ref[...])

    out = jax.ShapeDtypeStruct(w.shape, F32)
    return pl.pallas_call(body, name="reduce_update_small", out_shape=(out,) * 4,
                          compiler_params=pltpu.CompilerParams(vmem_limit_bytes=VMEM_LIMIT))(parts, late, w, m, v)


MESH = pl.DeviceIdType.MESH


def _copy_through_vmem(src_hbm, dst_hbm, stage, sem):
    load = pltpu.make_async_copy(src_hbm, stage, sem)
    load.start()
    load.wait()
    store = pltpu.make_async_copy(stage, dst_hbm, sem)
    store.start()
    store.wait()


class _GatherPlan:
    def __init__(self, x_ref, out_ref, send_sems, recv_sems):
        x, y, c = lax.axis_index("x"), lax.axis_index("y"), lax.axis_index("c")
        self.me, self.sibling, self.c = (x, y, c), (x, y, 1 - c), c
        self.chips = [(1 - x, y), (x, 1 - y), (1 - x, 1 - y)]
        self.x_ref, self.out_ref, self.send_sems, self.recv_sems = x_ref, out_ref, send_sems, recv_sems

    def slot(self, px, py, pc):
        return self.out_ref.at[4 * px + 2 * py + pc]

    def copy(self, k, block, to, src=None):
        return pltpu.make_async_remote_copy(
            src_ref=self.slot(*block) if src is None else src, dst_ref=self.slot(*block),
            send_sem=self.send_sems.at[k], recv_sem=self.recv_sems.at[k], device_id=to, device_id_type=MESH)

    def first(self):
        return [self.copy(0, self.me, self.sibling, src=self.x_ref)] + [
            self.copy(1 + j, self.me, (*chip, self.c), src=self.x_ref) for j, chip in enumerate(self.chips)]

    def passed(self):
        return [self.copy(4 + j, (*chip, self.c), self.sibling) for j, chip in enumerate(self.chips)]


def _gather_start(x_ref, out_ref, stage, send_sems, recv_sems, local_sem):
    plan = _GatherPlan(x_ref, out_ref, send_sems, recv_sems)
    for cp in plan.first():
        cp.start()
    _copy_through_vmem(x_ref, plan.slot(*plan.me), stage, local_sem)


def _gather_pass_on(out_ref, send_sems, recv_sems):
    plan = _GatherPlan(None, out_ref, send_sems, recv_sems)
    passed = plan.passed()
    for j, chip in enumerate(plan.chips):
        plan.copy(1 + j, (*chip, plan.c), plan.me).wait_recv()
        passed[j].start()


def _gather_finish(x_ref, out_ref, send_sems, recv_sems):
    plan = _GatherPlan(x_ref, out_ref, send_sems, recv_sems)
    plan.copy(0, plan.sibling, plan.me).wait_recv()
    for j, chip in enumerate(plan.chips):
        plan.copy(4 + j, (*chip, 1 - plan.c), plan.me).wait_recv()
    for cp in plan.first() + plan.passed():
        cp.wait_send()


def _gather_w_in(xs):
    r, cdim = xs.shape
    qkv, f_end = 3 * D_ATTN, 3 * D_ATTN + HEADS

    def body(x_ref, wqkv_ref, wf_ref, wu_ref, send_sems, recv_sems, local_sem, blocks, flat):
        plan = _GatherPlan(x_ref, blocks, send_sems, recv_sems)
        for cp in plan.first():
            cp.start()
        own = pltpu.make_async_copy(x_ref, plan.slot(*plan.me), local_sem)
        own.start()
        own.wait()
        _gather_pass_on(blocks, send_sems, recv_sems)
        _gather_finish(x_ref, blocks, send_sems, recv_sems)
        for dev in range(N_DEV):
            flat[dev * SHARD_IN:(dev + 1) * SHARD_IN, :] = blocks[dev, 0:SHARD_IN, :].astype(F32)
        wqkv_ref[...] = flat[0:qkv, :].astype(BF16)
        wf_ref[...] = jnp.concatenate([flat[qkv:f_end, :], jnp.zeros((LANES - HEADS, cdim), F32)], axis=0).astype(BF16)
        wu_ref[...] = flat[f_end:D_IN, :].astype(BF16)

    shape = lambda rows: jax.ShapeDtypeStruct((rows, cdim), xs.dtype)
    dma = pltpu.SemaphoreType.DMA
    return pl.pallas_call(
        body, name="gather_w_in",
        out_shape=(shape(qkv), shape(LANES), shape(D_IN - f_end)),
        in_specs=[ANY], out_specs=(VMEM_WHOLE, VMEM_WHOLE, VMEM_WHOLE),
        scratch_shapes=[dma((7,)), dma((7,)), dma,
                        pltpu.VMEM((N_DEV, r, cdim), xs.dtype), pltpu.VMEM((D_IN, cdim), F32)],
        compiler_params=pltpu.CompilerParams(vmem_limit_bytes=VMEM_LIMIT),
    )(xs)


def _pair_copies(src_refs, dst_refs, send_sems, recv_sems):
    x, y, c = lax.axis_index("x"), lax.axis_index("y"), lax.axis_index("c")
    return [pltpu.make_async_remote_copy(
        src_ref=src.at[2 * k + (1 - c)], dst_ref=dst.at[k], send_sem=send_sems.at[k, p], recv_sem=recv_sems.at[k, p],
        device_id=(x, y, 1 - c), device_id_type=MESH)
        for k in range(N_DEV // 2) for p, (src, dst) in enumerate(zip(src_refs, dst_refs))]


def _rs_pair_sum(core, pieces, offsets, rows, name, landed=()):
    cdim = pieces[0].shape[2]
    nk = N_DEV // 2
    npc = len(pieces)
    nrem = npc - len(landed)
    spans = [(o, t.shape[1]) for t, o in zip(pieces, offsets)]
    ends = [o + n for o, n in spans]
    gaps = [(a, b - a) for a, b in zip(ends, [o for o, _ in spans[1:]] + [rows]) if b > a]

    def body(core_ref, *refs):
        own, src, got, o_ref = refs[:npc], refs[npc:npc + nrem], refs[npc + nrem:2 * npc], refs[2 * npc]
        landing, send_sems, recv_sems = refs[2 * npc + 1:]
        k = pl.program_id(0)
        x, y, c = lax.axis_index("x"), lax.axis_index("y"), lax.axis_index("c")

        def copies(kk):
            return [pltpu.make_async_remote_copy(
                src_ref=src[p].at[2 * kk + (1 - c)], dst_ref=landing.at[kk, pl.ds(o, n)],
                send_sem=send_sems.at[kk, p], recv_sem=recv_sems.at[kk, p], device_id=(x, y, 1 - c),
                device_id_type=MESH) for p, (o, n) in enumerate(spans[:nrem])]

        @pl.when(k == 0)
        def _():
            for kk in range(nk):
                for cp in copies(kk):
                    cp.start()

        for cp, piece, (o, n) in zip(copies(k), own, spans):
            cp.wait_recv()
            o_ref[0, o:o + n, :] = (piece[0].astype(F32) + landing[k, o:o + n, :].astype(F32)).astype(BF16)
        for theirs, piece, (o, n) in zip(got, own[nrem:], spans[nrem:]):
            o_ref[0, o:o + n, :] = (piece[0].astype(F32) + theirs[0].astype(F32)).astype(BF16)
        for o, n in gaps:
            o_ref[0, o:o + n, :] = jnp.zeros((n, cdim), BF16)

        @pl.when(k == nk - 1)
        def _():
            for kk in range(nk):
                for cp in copies(kk):
                    cp.wait_send()

    own_specs = [pl.BlockSpec((1, n, cdim), lambda k, core_ref: (2 * k + core_ref[0], 0, 0)) for _, n in spans]
    got_specs = [pl.BlockSpec((1, n, cdim), lambda k, core_ref: (k, 0, 0)) for _, n in spans[nrem:]]
    land_rows = max(o + n for o, n in spans[:nrem])
    return pl.pallas_call(
        body, name=name, out_shape=jax.ShapeDtypeStruct((nk, rows, cdim), BF16),
        grid_spec=pltpu.PrefetchScalarGridSpec(
            num_scalar_prefetch=1, grid=(nk,),
            in_specs=own_specs + [ANY] * nrem + got_specs,
            out_specs=pl.BlockSpec((1, rows, cdim), lambda k, core_ref: (k, 0, 0)),
            scratch_shapes=[pltpu.VMEM((nk, land_rows, cdim), BF16), pltpu.SemaphoreType.DMA((nk, nrem)),
                            pltpu.SemaphoreType.DMA((nk, nrem))]),
        compiler_params=_params(1),
    )(core, *pieces, *pieces[:nrem], *landed)


def _chips_start(b_ref, out_ref, stage, send_sems, recv_sems, local_sem):
    x, y, c = lax.axis_index("x"), lax.axis_index("y"), lax.axis_index("c")
    mychip = 2 * x + y
    for j, (px, py) in enumerate([(1 - x, y), (x, 1 - y), (1 - x, 1 - y)]):
        pltpu.make_async_remote_copy(
            src_ref=b_ref.at[2 * px + py], dst_ref=out_ref.at[mychip],
            send_sem=send_sems.at[j], recv_sem=recv_sems.at[j], device_id=(px, py, c), device_id_type=MESH).start()
    _copy_through_vmem(b_ref.at[mychip], out_ref.at[mychip], stage, local_sem)


def _chips_finish(b_ref, out_ref, send_sems, recv_sems):
    x, y, c = lax.axis_index("x"), lax.axis_index("y"), lax.axis_index("c")
    for j, (px, py) in enumerate([(1 - x, y), (x, 1 - y), (1 - x, 1 - y)]):
        pltpu.make_async_remote_copy(
            src_ref=b_ref.at[2 * px + py], dst_ref=out_ref.at[2 * px + py],
            send_sem=send_sems.at[j], recv_sem=recv_sems.at[j], device_id=(px, py, c), device_id_type=MESH).wait()


def _pad_rows(a, rows):
    return jnp.pad(a, ((0, rows - a.shape[0]), (0, 0)))


def _pack_in(w_in):
    return _pad_rows(w_in[0].T, ROWS_IN)


def _unpack_in(r):
    return r[0:SHARD_IN].T[None]


def _pack_rest(w_out, w_gate, w_up, w_down, w_ple, w_pg):
    head = _pad_rows(jnp.concatenate([w_out[0], w_pg[0], w_ple[0].T.reshape(ROWS_PLE, D_MODEL)], axis=0), OFF_GATE)
    return jnp.concatenate([head, w_gate[0].T, w_up[0].T, w_down[0]], axis=0)


def _unpack_rest(r):
    return (r[0:OFF_PG][None], r[OFF_GATE:OFF_UP].T[None], r[OFF_UP:OFF_DOWN].T[None], r[OFF_DOWN:ROWS_REST][None],
            r[OFF_PLE:OFF_PLE + ROWS_PLE].reshape(SHARD_SQ, D_PLE).T[None], r[OFF_PG:OFF_PLE][None])


def _pack_small(w_pool, g_mix_pre, g_mix_post, g_ffn_pre, g_ffn_post, g_ple, g_attn, g_pool, pool_scale, b_forget,
                loss=None):
    row = lambda vrow: vrow.reshape(1, -1)
    misc = [row(pool_scale), row(b_forget), row(loss) if loss is not None else jnp.zeros((1, 1), F32),
            jnp.zeros((1, D_MODEL - COL_LOSS - 1), F32)]
    rows = [w_pool.reshape(64, D_MODEL), row(g_mix_pre), row(g_mix_post), row(g_ffn_pre), row(g_ffn_post), row(g_ple),
            jnp.concatenate([row(g_attn), row(g_pool)], axis=1), jnp.concatenate(misc, axis=1),
            jnp.zeros((SMALL_ROWS - ROW_MISC - 1, D_MODEL), F32)]
    return jnp.concatenate(rows, axis=0)


def _pack_small_late(g_mix_pre, b_forget):
    misc = [jnp.zeros((1, COL_B_FORGET), F32), b_forget.reshape(1, -1), jnp.zeros((1, D_MODEL - COL_LOSS), F32)]
    return jnp.concatenate([g_mix_pre.reshape(1, -1), jnp.zeros((ROW_MISC - ROW_G_MIX_PRE - 1, D_MODEL), F32),
                            jnp.concatenate(misc, axis=1), jnp.zeros((SMALL_ROWS - ROW_MISC - 1, D_MODEL), F32)], axis=0)


def _unpack_small(r):
    gains, misc = r[ROW_GROUP_GAINS:ROW_GROUP_GAINS + 1], r[ROW_MISC:ROW_MISC + 1]
    return dict(
        w_pool=r[0:64].reshape(1, 4, POOL_CH, POOL_CH), g_mix_pre=r[ROW_G_MIX_PRE:ROW_G_MIX_PRE + 1],
        g_mix_post=r[ROW_G_MIX_POST:ROW_G_MIX_POST + 1], g_ffn_pre=r[ROW_G_FFN_PRE:ROW_G_FFN_PRE + 1],
        g_ffn_post=r[ROW_G_FFN_POST:ROW_G_FFN_POST + 1], g_ple=r[ROW_G_PLE:ROW_G_PLE + 1],
        g_attn_grp=gains[:, 0:D_ATTN], g_pool_grp=gains[:, D_ATTN:D_ATTN + D_POOL],
        pool_scale=misc[:, 0:D_POOL], b_forget=misc[:, COL_B_FORGET:COL_B_FORGET + HEADS])


def _step(x, p, tgt, small, in_w, in_m, in_v, rest_w, rest_m, rest_v):
    core = lax.axis_index("c").astype(jnp.int32).reshape(1)
    wqkv, wf, wu = _gather_w_in(in_w.astype(BF16))
    wpool = small["w_pool"].astype(BF16)
    bpad = jnp.pad(small["b_forget"], ((0, 0), (0, LANES - HEADS)))

    lay = _attn_layout_constants()
    rest_b = rest_w.astype(BF16)
    hn, qt3, ka, v, qat3, vt3, kt3, fl, y, mpre, gh = _pre_attn_fwd(x, small["g_mix_pre"], wqkv, wf, wu, bpad, wpool, lay,
                                                                 rest_b[0:OFF_GATE])
    a, lset3, gf = _attn_fwd(ka, qat3, vt3, rest_b[OFF_GATE:])
    wple_t = gh[:, OFF_PLE:OFF_PLE + ROWS_PLE].reshape(D_MODEL, D_PLE)
    mix, o, h1, hn2 = _post_attn_fwd(a, mpre, x, small["g_attn_grp"], small["g_pool_grp"], small["pool_scale"], gh,
                                     small["g_mix_post"], small["g_ffn_pre"])
    gate, up, act, ff, h2 = _ffn_fwd(hn2, gf, gf, gf, h1, small["g_ffn_post"])
    dh2, dff, dgl, dpp, h2b, pb, loss8, dg_ple, dg_ffn_post = _tail_fwd_bwd(
        h2, p, tgt, ff, wple_t, gh, small["g_ple"], small["g_ffn_post"])
    dgate, dup, dh1, dg_ffn_pre = _ffn_bwd(dff, gate, up, gf, gf, gf, h1, dh2, small["g_ffn_pre"])
    nd = N_DEV
    send_rest = [
        _wgrad(h2b, dgl, BF16, "wgrad_ple_gate").reshape(nd, SHARD_SQ, D_MODEL),
        _wgrad(dpp, pb, BF16, "wgrad_ple").reshape(nd, ROWS_PLE, D_MODEL),
        _wgrad(dgate, hn2, BF16, "wgrad_gate").reshape(nd, SHARD_FF, D_MODEL),
        _wgrad(dup, hn2, BF16, "wgrad_up").reshape(nd, SHARD_FF, D_MODEL),
        _wgrad(act, dff, BF16, "wgrad_down").reshape(nd, SHARD_FF, D_MODEL)]
    (dob, dat3, dlt3, dmpb, dy, dg_mix_post, dg_attn, dg_pool, dps), landed = _post_attn_bwd(
        dh1, o, a, mpre, gh, wpool, small["g_mix_post"], small["g_attn_grp"], small["g_pool_grp"], small["pool_scale"],
        send_rest)
    send_rest = [_wgrad(mix, dob, BF16, "wgrad_out").reshape(nd, SHARD_SQ, D_MODEL)] + send_rest
    pair_rest = _rs_pair_sum(core, send_rest, [0, OFF_PG, OFF_PLE, OFF_GATE, OFF_UP, OFF_DOWN], ROWS_REST,
                             "rs_pair_sum_rest", landed)

    dwp = _wgrad(y, dmpb, F32, "wgrad_pool")
    dw_pool = jnp.stack([dwp[g * POOL_CH:(g + 1) * POOL_CH, g * POOL_CH:(g + 1) * POOL_CH] for g in range(4)])
    small_part = _pack_small(dw_pool, jnp.zeros((1, D_MODEL), F32), dg_mix_post, dg_ffn_pre, dg_ffn_post, dg_ple,
                             dg_attn, dg_pool, dps, jnp.zeros((1, HEADS), F32), loss8[0:1, 0:1])
    dqt3, dkt3, dvt3, chips_rest, small_all = _attn_bwd(ka, v, kt3, qat3, qt3, dat3, lset3, dlt3, pair_rest, small_part)

    gx, dz, dg_mix_pre, db = _pre_attn_bwd(dqt3, dkt3, dvt3, fl, dy, x, dh1, small["g_mix_pre"], wqkv, wf, wu)

    pair_in = _rs_pair_sum(core, [_wgrad_in(dz, hn)], [0], ROWS_IN, "rs_pair_sum_in")

    small_late = _pack_small_late(dg_mix_pre, db[:, 0:HEADS])
    *upd_rest, chips_in, late_all = _reduce_update_rest(chips_rest, rest_w, rest_m, rest_v, pair_in, small_late)
    upd_in = _reduce_update_big(chips_in, in_w, in_m, in_v, ROWS_IN, "reduce_update_in")
    return gx, (small_all, late_all), upd_in, upd_rest


def kernel(x, p, g_mix_pre, w_in, b_forget, g_attn_grp, g_pool_grp, w_pool, pool_scale, w_out, g_mix_post, g_ffn_pre, w_ffn_gate, w_ffn_up, w_ffn_down, g_ffn_post, w_ple_proj, g_ple, w_ple_gate, loss_target, m_g_mix_pre, m_w_in, m_b_forget, m_g_attn_grp, m_g_pool_grp, m_w_pool, m_pool_scale, m_w_out, m_g_mix_post, m_g_ffn_pre, m_w_ffn_gate, m_w_ffn_up, m_w_ffn_down, m_g_ffn_post, m_w_ple_proj, m_g_ple, m_w_ple_gate, v_g_mix_pre, v_w_in, v_b_forget, v_g_attn_grp, v_g_pool_grp, v_w_pool, v_pool_scale, v_w_out, v_g_mix_post, v_g_ffn_pre, v_w_ffn_gate, v_w_ffn_up, v_w_ffn_down, v_g_ffn_post, v_w_ple_proj, v_g_ple, v_w_ple_gate):
    small = dict(w_pool=w_pool[0], g_mix_pre=g_mix_pre, g_mix_post=g_mix_post, g_ffn_pre=g_ffn_pre,
                 g_ffn_post=g_ffn_post, g_ple=g_ple, g_attn_grp=g_attn_grp, g_pool_grp=g_pool_grp,
                 pool_scale=pool_scale, b_forget=b_forget)
    gx, small_all, upd_in, upd_rest = _step(
        x[0], p[0, 0], loss_target[0], small, _pack_in(w_in), _pack_in(m_w_in), _pack_in(v_w_in),
        _pack_rest(w_out, w_ffn_gate, w_ffn_up, w_ffn_down, w_ple_proj, w_ple_gate),
        _pack_rest(m_w_out, m_w_ffn_gate, m_w_ffn_up, m_w_ffn_down, m_w_ple_proj, m_w_ple_gate),
        _pack_rest(v_w_out, v_w_ffn_gate, v_w_ffn_up, v_w_ffn_down, v_w_ple_proj, v_w_ple_gate))

    sm_w = _pack_small(w_pool, g_mix_pre, g_mix_post, g_ffn_pre, g_ffn_post, g_ple, g_attn_grp, g_pool_grp, pool_scale, b_forget)
    sm_m = _pack_small(m_w_pool, m_g_mix_pre, m_g_mix_post, m_g_ffn_pre, m_g_ffn_post, m_g_ple, m_g_attn_grp, m_g_pool_grp, m_pool_scale, m_b_forget)
    sm_v = _pack_small(v_w_pool, v_g_mix_pre, v_g_mix_post, v_g_ffn_pre, v_g_ffn_post, v_g_ple, v_g_attn_grp, v_g_pool_grp, v_pool_scale, v_b_forget)
    upd_small = _reduce_update_small(*small_all, sm_w, sm_m, sm_v)
    loss = upd_small[0][ROW_MISC, COL_LOSS]

    def leaves(k):
        b_out, b_gate, b_up, b_down, b_ple, b_pg = _unpack_rest(upd_rest[k])
        s = _unpack_small(upd_small[k])
        return (s["g_mix_pre"], _unpack_in(upd_in[k]), s["b_forget"], s["g_attn_grp"], s["g_pool_grp"], s["w_pool"],
                s["pool_scale"], b_out, s["g_mix_post"], s["g_ffn_pre"], b_gate, b_up, b_down, s["g_ffn_post"], b_ple,
                s["g_ple"], b_pg)

    return (loss, gx[None], *leaves(0), *leaves(1), *leaves(2), *leaves(3))
```

```python
import functools

import jax
import jax.numpy as jnp
from jax import lax
from jax.experimental import pallas as pl
from jax.experimental.pallas import tpu as pltpu

F32 = jnp.float32
BF16 = jnp.bfloat16
HIGHEST = lax.Precision.HIGHEST

D_MODEL = 1024
HEADS = 8
HEAD_DIM = 64
D_ATTN = HEADS * HEAD_DIM
POOL_WINDOWS = (2, 4, 8, 16)
POOL_CH = 128
D_POOL = POOL_CH * len(POOL_WINDOWS)
D_FF = 2816
D_PLE = 256
D_IN = 3 * D_ATTN + HEADS + D_POOL
RMS_EPS = 1e-6
N_DEV = 8

ADAM_LR = 0.001
ADAM_B1 = 0.9
ADAM_B2 = 0.999
ADAM_EPS = 1e-08
ADAM_WD = 0.01
ADAM_STEP = 10

LANES = 128
HALO = 16
TS = 512
TS_FF = 256
FF_ROW_PARTS = 2
TS_WGRAD = 1024
TM_WGRAD = 2176
TQ = 256
TN_FF = D_FF
NEG = -1e30
VMEM_LIMIT = 56 * 1024 * 1024

SHARD_IN = 257
ROWS_IN = 272
SHARD_FF = 352
SHARD_SQ = D_MODEL // N_DEV
ROWS_PLE = D_PLE * SHARD_SQ // D_MODEL
OFF_PG = SHARD_SQ
OFF_PLE = 2 * SHARD_SQ
OFF_GATE = SHARD_FF
OFF_UP = 2 * SHARD_FF
OFF_DOWN = 3 * SHARD_FF
ROWS_REST = 4 * SHARD_FF
TR_REST = SHARD_FF

SMALL_ROWS = 72
ROW_G_MIX_PRE, ROW_G_MIX_POST, ROW_G_FFN_PRE, ROW_G_FFN_POST, ROW_G_PLE = 64, 65, 66, 67, 68
ROW_GROUP_GAINS, ROW_MISC = 69, 70
COL_B_FORGET = D_POOL
COL_LOSS = D_POOL + HEADS


def _nn(a, b):
    return jnp.dot(a, b, preferred_element_type=F32)


def _nt(a, b):
    return lax.dot_general(a, b, (((1,), (1,)), ((), ())), preferred_element_type=F32)


def _tn(a, b):
    return lax.dot_general(a, b, (((0,), (0,)), ((), ())), preferred_element_type=F32)


def _rstd(v):
    return lax.rsqrt(jnp.mean(v * v, axis=-1, keepdims=True) + RMS_EPS)


def _rms_bwd(v, g, dy):
    r = _rstd(v)
    vh = v * r
    t = dy * g
    dv = r * (t - vh * jnp.mean(t * vh, axis=-1, keepdims=True))
    return dv, jnp.sum(dy * vh, axis=0, keepdims=True)


def _split3(v):
    hi = v.astype(BF16)
    rest = v - hi.astype(F32)
    mid = rest.astype(BF16)
    return hi, mid, (rest - mid.astype(F32)).astype(BF16)


def _mask_matmul(mask, v):
    hi, mid, lo = _split3(v)
    return _nn(mask, lo) + _nn(mask, mid) + _nn(mask, hi)


def _running_sum(v, reverse=False):
    tq = v.shape[0] // 2
    rr = lax.broadcasted_iota(jnp.int32, (tq, tq), 0)
    cc = lax.broadcasted_iota(jnp.int32, (tq, tq), 1)
    mask = ((cc >= rr) if reverse else (cc <= rr)).astype(BF16)
    top, bot = _mask_matmul(mask, v[0:tq]), _mask_matmul(mask, v[tq:])
    if reverse:
        top = top + bot[0:1, :]
    else:
        bot = bot + top[tq - 1:tq, :]
    return jnp.concatenate([top, bot], axis=0)


def _params(n_grid):
    return pltpu.CompilerParams(dimension_semantics=("arbitrary",) * n_grid, vmem_limit_bytes=VMEM_LIMIT)


def _row(i):
    return (i, 0)


def _fixed(*_):
    return (0, 0)


def _spec_square(part):
    return pl.BlockSpec((N_DEV, SHARD_SQ, D_MODEL), lambda *_: (0, part, 0))


def _spec_ff(part):
    return pl.BlockSpec((TN_FF // SHARD_FF, SHARD_FF, D_MODEL), lambda i, j: (j, part, 0), pipeline_mode=pl.Buffered(1))


assert TS == 2 * TQ and TN_FF % SHARD_FF == 0
_HALVES = (slice(0, TQ), slice(TQ, TS))

VMEM_WHOLE = pl.BlockSpec(memory_space=pltpu.VMEM)
SMEM_WHOLE = pl.BlockSpec(memory_space=pltpu.SMEM)
ANY = pl.BlockSpec(memory_space=pl.ANY)


LOG2E = 1.4426950408889634
VROWS = HEAD_DIM + 16
AUG = 128
BIAS_LANE = HEAD_DIM
ONE_LANE = HEAD_DIM + 3
SPARE_LANE = HEADS
PART_LANES = 16
assert SPARE_LANE < PART_LANES and 3 * PART_LANES <= LANES


def _attn_layout_constants():
    import numpy as np
    bias_k = np.zeros((LANES, HEADS * AUG), np.float32)
    bias_q = np.zeros((LANES, HEADS * AUG), np.float32)
    for h in range(HEADS):
        for part in range(3):
            bias_k[part * PART_LANES + h, h * AUG + BIAS_LANE + part] = -1.0
            bias_q[part * PART_LANES + h, h * AUG + ONE_LANE + part] = 1.0
            bias_k[SPARE_LANE, h * AUG + ONE_LANE + part] = 1.0
            bias_q[SPARE_LANE, h * AUG + BIAS_LANE + part] = 1.0
    after = np.concatenate([np.arange(h * AUG + HEAD_DIM, (h + 1) * AUG) for h in range(HEADS)])
    as_bf = lambda a: jnp.asarray(a, BF16)
    return dict(bias_k=as_bf(bias_k[:, after]), bias_q_t=as_bf(bias_q[:, after].T))


def _pre_attn_fwd(x, g1, wqkv, wf, wu, bpad, wpool, lay, own_block):
    s, d = x.shape
    nt = s // TS
    sub = TS // TQ

    def body(x_ref, g_ref, wqkv_ref, wf_ref, wu_ref, b_ref, wp_ref, bk_ref, bqt_ref, own_ref,
             hn_ref, qt_ref, ka_ref, v_ref, qat_ref, vt_ref, kt_ref, fl_ref, y_ref, mp_ref, all_ref,
             ubuf, ccar, cbuf, stage, send_sems, recv_sems, local_sem):
        i = pl.program_id(0)

        @pl.when(i == 0)
        def _():
            _gather_start(own_ref, all_ref, stage, send_sems, recv_sems, local_sem)
            ubuf[0:HALO, :] = jnp.zeros((HALO, D_POOL), F32)
            ccar[...] = jnp.zeros_like(ccar)

        @pl.when(i == max(nt - 2, 0))
        def _():
            _gather_pass_on(all_ref, send_sems, recv_sems)

        xv = x_ref[...]
        hn = (xv * _rstd(xv) * g_ref[...]).astype(BF16)
        hn_ref[...] = hn
        zq = _nt(hn, wqkv_ref[...])
        qt = (zq[:, 0:D_ATTN] * 0.125).astype(BF16).T
        qb = (zq[:, 0:D_ATTN] * (0.125 * LOG2E)).astype(BF16)
        kb = zq[:, D_ATTN:2 * D_ATTN].astype(BF16)
        vb = zq[:, 2 * D_ATTN:3 * D_ATTN].astype(BF16)
        v_ref[...] = vb

        fl = _nt(hn, wf_ref[...]) + b_ref[...]
        fl_ref[...] = fl
        logf = jax.nn.log_sigmoid(fl)
        c = _running_sum(logf) + ccar[...]
        cbuf[...] = c
        ccar[...] = cbuf[TS - 1:TS, :]
        hi, mid, lo = (part.astype(F32) for part in _split3(c * LOG2E))
        lane = lax.broadcasted_iota(jnp.int32, (TS, LANES), 1)
        later = jnp.where(lane < 2 * PART_LANES, pltpu.roll(mid, PART_LANES, 1), pltpu.roll(lo, 2 * PART_LANES, 1))
        parts = jnp.where(lane < PART_LANES, jnp.where(lane == SPARE_LANE, 1.0, hi), later).astype(BF16)
        extra = AUG - HEAD_DIM
        kbias = _nn(parts, bk_ref[...]).astype(BF16)
        for h in range(HEADS):
            ka_ref[:, h * AUG:h * AUG + HEAD_DIM] = kb[:, h * HEAD_DIM:(h + 1) * HEAD_DIM]
            ka_ref[:, h * AUG + HEAD_DIM:(h + 1) * AUG] = kbias[:, h * extra:(h + 1) * extra]
        qbt = qb.T
        qbias = _nt(bqt_ref[...], parts).astype(BF16)
        vt = vb.T
        kt = kb.T
        for a in range(sub):
            cols = slice(a * TQ, (a + 1) * TQ)
            for h in range(HEADS):
                qat_ref[a, h * AUG:h * AUG + HEAD_DIM, :] = qbt[h * HEAD_DIM:(h + 1) * HEAD_DIM, cols]
                qat_ref[a, h * AUG + HEAD_DIM:(h + 1) * AUG, :] = qbias[h * extra:(h + 1) * extra, cols]
            for ref, mat in ((qt_ref, qt), (kt_ref, kt), (vt_ref, vt)):
                for h in range(HEADS):
                    ref[a, h * VROWS:h * VROWS + HEAD_DIM, :] = mat[h * HEAD_DIM:(h + 1) * HEAD_DIM, cols]
                    ref[a, h * VROWS + HEAD_DIM:(h + 1) * VROWS, :] = jnp.ones((VROWS - HEAD_DIM, TQ), BF16)

        u = _nt(hn, wu_ref[...])
        ubuf[HALO:HALO + TS, :] = u
        t = i * TS + lax.broadcasted_iota(jnp.int32, (TS, 1), 0)
        for g, w in enumerate(POOL_WINDOWS):
            cols = slice(g * POOL_CH, (g + 1) * POOL_CH)
            sm = ubuf[:, cols]
            step = 1
            while step < w:
                sm = sm + pltpu.roll(sm, step, 0)
                step *= 2
            cnt = jnp.minimum(t + 1, w).astype(F32)
            yg = (sm[HALO:, :] / cnt - u[:, cols]).astype(BF16)
            y_ref[:, cols] = yg
            mp_ref[:, cols] = _nn(yg, wp_ref[g])
        ubuf[0:HALO, :] = u[TS - HALO:, :]

        @pl.when(i == nt - 1)
        def _():
            _gather_finish(own_ref, all_ref, send_sems, recv_sems)

    nq = s // TQ
    aug = HEADS * AUG
    outs = (
        jax.ShapeDtypeStruct((s, d), BF16), jax.ShapeDtypeStruct((nq, HEADS * VROWS, TQ), BF16),
        jax.ShapeDtypeStruct((s, aug), BF16), jax.ShapeDtypeStruct((s, D_ATTN), BF16),
        jax.ShapeDtypeStruct((nq, aug, TQ), BF16), jax.ShapeDtypeStruct((nq, HEADS * VROWS, TQ), BF16),
        jax.ShapeDtypeStruct((nq, HEADS * VROWS, TQ), BF16),
        jax.ShapeDtypeStruct((s, LANES), F32),
        jax.ShapeDtypeStruct((s, D_POOL), BF16), jax.ShapeDtypeStruct((s, D_POOL), F32),
        jax.ShapeDtypeStruct((N_DEV,) + own_block.shape, own_block.dtype),
    )
    fixed3 = lambda i: (0, 0, 0)
    tiles3 = lambda rows: pl.BlockSpec((sub, rows, TQ), lambda i: (i, 0, 0))
    return pl.pallas_call(
        body, grid=(nt,), out_shape=outs, name="pre_attn_fwd",
        in_specs=[pl.BlockSpec((TS, d), _row), pl.BlockSpec((1, d), _fixed),
                  pl.BlockSpec((3 * D_ATTN, d), _fixed), pl.BlockSpec(wf.shape, _fixed), pl.BlockSpec(wu.shape, _fixed),
                  pl.BlockSpec((1, LANES), _fixed), pl.BlockSpec(wpool.shape, fixed3),
                  pl.BlockSpec(lay["bias_k"].shape, _fixed), pl.BlockSpec(lay["bias_q_t"].shape, _fixed), ANY],
        out_specs=(pl.BlockSpec((TS, d), _row), tiles3(HEADS * VROWS),
                   pl.BlockSpec((TS, aug), _row), pl.BlockSpec((TS, D_ATTN), _row),
                   tiles3(aug), tiles3(HEADS * VROWS), tiles3(HEADS * VROWS),
                   pl.BlockSpec((TS, LANES), _row),
                   pl.BlockSpec((TS, D_POOL), _row), pl.BlockSpec((TS, D_POOL), _row), ANY),
        scratch_shapes=[pltpu.VMEM((TS + HALO, D_POOL), F32), pltpu.VMEM((1, LANES), F32), pltpu.VMEM((TS, LANES), F32),
                        pltpu.VMEM(own_block.shape, own_block.dtype),
                        pltpu.SemaphoreType.DMA((7,)), pltpu.SemaphoreType.DMA((7,)), pltpu.SemaphoreType.DMA],
        compiler_params=_params(1),
    )(x, g1, wqkv, wf, wu, bpad, wpool, lay["bias_k"], lay["bias_q_t"], own_block)


def _causal_in_tile():
    krow = lax.broadcasted_iota(jnp.int32, (TQ, TQ), 0)
    qcol = lax.broadcasted_iota(jnp.int32, (TQ, TQ), 1)
    return krow <= qcol


def _attn_fwd(ka, qat3, vt3, own_block):
    s = ka.shape[0]
    nq = s // TQ
    pass_on_step = max(nq - 2, 0)

    def body(qa_ref, ka_ref, vt_ref, own_ref, a_ref, lset_ref, all_ref, acc, out_t, st_scr, pt_scr,
             stage, send_sems, recv_sems, local_sem):
        i = pl.program_id(0)

        @pl.when(i == 0)
        def _():
            _gather_start(own_ref, all_ref, stage, send_sems, recv_sems, local_sem)

        @pl.when(i == pass_on_step)
        def _():
            _gather_pass_on(all_ref, send_sems, recv_sems)

        acc[...] = jnp.zeros_like(acc)

        def tile(j, stats, masked):
            tile_max = []
            for h in range(HEADS):
                aug = slice(h * AUG, (h + 1) * AUG)
                st = _nn(ka_ref[pl.ds(j * TQ, TQ), aug], qa_ref[0, aug, :])
                if masked:
                    st = jnp.where(_causal_in_tile(), st, NEG)
                st_scr[h] = st
                tile_max.append(jnp.max(st, axis=0, keepdims=True))
            new, scale = [], []
            for h in range(HEADS):
                m_new = jnp.maximum(stats[h], tile_max[h])
                scale.append(jnp.exp2(stats[h] - m_new))
                pt_scr[h] = jnp.exp2(st_scr[h] - m_new).astype(BF16)
                new.append(m_new)
            for h in range(HEADS):
                rows = slice(h * VROWS, (h + 1) * VROWS)
                acc[rows, :] = scale[h] * acc[rows, :] + _nn(vt_ref[j, rows, :], pt_scr[h])
            return tuple(new)

        init = tuple(jnp.full((1, TQ), NEG, F32) for _ in range(HEADS))
        stats = lax.fori_loop(0, i, functools.partial(tile, masked=False), init)
        stats = tile(i, stats, True)
        for h in range(HEADS):
            denom = acc[h * VROWS + HEAD_DIM:h * VROWS + HEAD_DIM + 1, :]
            out_t[h * HEAD_DIM:(h + 1) * HEAD_DIM, :] = acc[h * VROWS:h * VROWS + HEAD_DIM, :] / denom
            lset_ref[0, h:h + 1, :] = stats[h] + jnp.log2(denom)
        a_ref[...] = out_t[...].T

        @pl.when(i == nq - 1)
        def _():
            _gather_finish(own_ref, all_ref, send_sems, recv_sems)

    r, cdim = own_block.shape
    return pl.pallas_call(
        body, grid=(nq,), name="attn_fwd",
        out_shape=(jax.ShapeDtypeStruct((s, D_ATTN), F32), jax.ShapeDtypeStruct((nq, HEADS, TQ), F32),
                   jax.ShapeDtypeStruct((N_DEV, r, cdim), own_block.dtype)),
        in_specs=[pl.BlockSpec((1, HEADS * AUG, TQ), lambda i: (i, 0, 0)), VMEM_WHOLE, VMEM_WHOLE, ANY],
        out_specs=(pl.BlockSpec((TQ, D_ATTN), _row), pl.BlockSpec((1, HEADS, TQ), lambda i: (i, 0, 0)), ANY),
        scratch_shapes=[pltpu.VMEM((HEADS * VROWS, TQ), F32), pltpu.VMEM((D_ATTN, TQ), F32),
                        pltpu.VMEM((HEADS, TQ, TQ), F32), pltpu.VMEM((HEADS, TQ, TQ), BF16),
                        pltpu.VMEM((r, cdim), own_block.dtype),
                        pltpu.SemaphoreType.DMA((7,)), pltpu.SemaphoreType.DMA((7,)), pltpu.SemaphoreType.DMA],
        compiler_params=_params(1),
    )(qat3, ka, vt3, own_block)


def _post_attn_fwd(a, mpre, x, g_attn, g_pool, pscale, wout, g_post, g_ffn_pre):
    s, d = x.shape

    def body(a_ref, mp_ref, x_ref, ga_ref, gp_ref, ps_ref, wo_ref, gpost_ref, gpre_ref,
             mix_ref, o_ref, h1_ref, hn2_ref):
        for rows in _HALVES:
            av = a_ref[rows, :]
            mix_ref[rows, 0:D_ATTN] = (av * _rstd(av) * ga_ref[...]).astype(BF16)
            mv = mp_ref[rows, :] * ps_ref[...]
            mix_ref[rows, D_ATTN:] = (mv * _rstd(mv) * gp_ref[...]).astype(BF16)
            o = _nn(mix_ref[rows, :], wo_ref[...].reshape(d, d))
            o_ref[rows, :] = o
            h1 = x_ref[rows, :] + o * _rstd(o) * gpost_ref[...]
            h1_ref[rows, :] = h1
            hn2_ref[rows, :] = (h1 * _rstd(h1) * gpre_ref[...]).astype(BF16)

    vec = lambda n: pl.BlockSpec((1, n), _fixed)
    return pl.pallas_call(
        body, grid=(s // TS,), name="post_attn_fwd",
        out_shape=(jax.ShapeDtypeStruct((s, d), BF16), jax.ShapeDtypeStruct((s, d), F32),
                   jax.ShapeDtypeStruct((s, d), F32), jax.ShapeDtypeStruct((s, d), BF16)),
        in_specs=[pl.BlockSpec((TS, D_ATTN), _row), pl.BlockSpec((TS, D_POOL), _row), pl.BlockSpec((TS, d), _row),
                  vec(D_ATTN), vec(D_POOL), vec(D_POOL), _spec_square(0), vec(d), vec(d)],
        out_specs=(pl.BlockSpec((TS, d), _row),) * 4,
        compiler_params=_params(1),
    )(a, mpre, x, g_attn, g_pool, pscale, wout, g_post, g_ffn_pre)


def _ffn_fwd(hn2, wg, wu, wd, h1, g_post):
    s, d = h1.shape
    nc = D_FF // TN_FF
    ts = min(TS_FF, s)

    def body(hn_ref, wg_ref, wu_ref, wd_ref, h1_ref, g_ref, gate_ref, up_ref, act_ref, ff_ref, h2_ref, acc):
        j = pl.program_id(1)

        @pl.when(j == 0)
        def _():
            acc[...] = jnp.zeros_like(acc)

        for r in range(FF_ROW_PARTS):
            rows = slice(r * (ts // FF_ROW_PARTS), (r + 1) * (ts // FF_ROW_PARTS))
            hn = hn_ref[rows, :]
            gt = _nt(hn, wg_ref[...].reshape(TN_FF, d))
            up = _nt(hn, wu_ref[...].reshape(TN_FF, d))
            gate_ref[rows, :] = gt.astype(BF16)
            up_ref[rows, :] = up.astype(BF16)
            act_ref[rows, :] = (gt * jax.nn.sigmoid(gt) * up).astype(BF16)
            acc[rows, :] += _nn(act_ref[rows, :], wd_ref[...].reshape(TN_FF, d))

        @pl.when(j == nc - 1)
        def _():
            ff = acc[...]
            ff_ref[...] = ff
            h2_ref[...] = h1_ref[...] + ff * _rstd(ff) * g_ref[...]

    rowblk = pl.BlockSpec((ts, d), lambda i, j: (i, 0))
    chunk = pl.BlockSpec((ts, TN_FF), lambda i, j: (i, j))
    return pl.pallas_call(
        body, grid=(s // ts, nc), name="ffn_fwd",
        out_shape=(jax.ShapeDtypeStruct((s, D_FF), BF16),) * 3 + (jax.ShapeDtypeStruct((s, d), F32),) * 2,
        in_specs=[rowblk, _spec_ff(0), _spec_ff(1), _spec_ff(2), rowblk, pl.BlockSpec((1, d), lambda i, j: (0, 0))],
        out_specs=(chunk, chunk, chunk, rowblk, rowblk),
        scratch_shapes=[pltpu.VMEM((ts, d), F32)],
        compiler_params=_params(2),
    )(hn2, wg, wu, wd, h1, g_post)


def _tail_fwd_bwd(h2, p, tgt, ff, wple, wpg, g_ple, g_ffn_post):
    s, d = h2.shape

    def body(h2_ref, p_ref, t_ref, ff_ref, wple_ref, wpg_ref, gple_ref, gfp_ref,
             dh2_ref, dff_ref, dgl_ref, dpp_ref, h2b_ref, pb_ref, loss_ref, dgple_ref, dgfp_ref):
        i = pl.program_id(0)

        @pl.when(i == 0)
        def _():
            loss_ref[...] = jnp.zeros_like(loss_ref)
            dgple_ref[...] = jnp.zeros_like(dgple_ref)
            dgfp_ref[...] = jnp.zeros_like(dgfp_ref)

        h2 = h2_ref[...]
        h2b = h2.astype(BF16)
        h2b_ref[...] = h2b
        pb = p_ref[...].astype(BF16)
        pb_ref[...] = pb
        pp = _nt(pb, wple_ref[...])
        gple = gple_ref[...]
        e = pp * _rstd(pp) * gple
        wpg = wpg_ref[...].reshape(d, d)
        sg = jax.nn.sigmoid(_nn(h2b, wpg))
        diff = h2 + sg * e - t_ref[...]
        sq = jnp.sum(jnp.sum(diff * diff, axis=1, keepdims=True), axis=0, keepdims=True)
        loss_ref[...] += jnp.broadcast_to(sq * (0.5 / d), loss_ref.shape)
        dh3 = diff * (1.0 / d)
        dgl = (dh3 * e * sg * (1.0 - sg)).astype(BF16)
        dgl_ref[...] = dgl
        dh2 = dh3 + _nt(dgl, wpg)
        dh2_ref[...] = dh2
        dpp, dg = _rms_bwd(pp, gple, dh3 * sg)
        dpp_ref[...] = dpp.astype(BF16)
        dgple_ref[...] += dg
        dff, dg = _rms_bwd(ff_ref[...], gfp_ref[...], dh2)
        dff_ref[...] = dff.astype(BF16)
        dgfp_ref[...] += dg

    rowblk = pl.BlockSpec((TS, d), _row)
    vec = pl.BlockSpec((1, d), _fixed)
    return pl.pallas_call(
        body, grid=(s // TS,), name="tail_fwd_bwd",
        out_shape=(jax.ShapeDtypeStruct((s, d), F32), jax.ShapeDtypeStruct((s, d), BF16),
                   jax.ShapeDtypeStruct((s, d), BF16), jax.ShapeDtypeStruct((s, d), BF16),
                   jax.ShapeDtypeStruct((s, d), BF16), jax.ShapeDtypeStruct((s, D_PLE), BF16),
                   jax.ShapeDtypeStruct((8, LANES), F32), jax.ShapeDtypeStruct((1, d), F32),
                   jax.ShapeDtypeStruct((1, d), F32)),
        in_specs=[rowblk, pl.BlockSpec((TS, D_PLE), _row), rowblk, rowblk,
                  pl.BlockSpec(wple.shape, _fixed), _spec_square(1), vec, vec],
        out_specs=(rowblk, rowblk, rowblk, rowblk, rowblk, pl.BlockSpec((TS, D_PLE), _row),
                   pl.BlockSpec((8, LANES), _fixed), vec, vec),
        compiler_params=_params(1),
    )(h2, p, tgt, ff, wple, wpg, g_ple, g_ffn_post)


def _ffn_bwd(dff, gate, up, wd, wg, wu, h1, dh2, g_pre):
    s, d = h1.shape
    nc = D_FF // TN_FF
    ts = min(TS_FF, s)

    def body(dff_ref, gate_ref, up_ref, wd_ref, wg_ref, wu_ref, h1_ref, dh2_ref, g_ref,
             dgate_ref, dup_ref, dh1_ref, dg_ref, acc):
        i = pl.program_id(0)
        j = pl.program_id(1)

        @pl.when((i == 0) & (j == 0))
        def _():
            dg_ref[...] = jnp.zeros_like(dg_ref)

        @pl.when(j == 0)
        def _():
            acc[...] = jnp.zeros_like(acc)

        for r in range(FF_ROW_PARTS):
            rows = slice(r * (ts // FF_ROW_PARTS), (r + 1) * (ts // FF_ROW_PARTS))
            dact = _nt(dff_ref[rows, :], wd_ref[...].reshape(TN_FF, d))
            gt = gate_ref[rows, :].astype(F32)
            sg = jax.nn.sigmoid(gt)
            dup_ref[rows, :] = (dact * gt * sg).astype(BF16)
            dgate_ref[rows, :] = (dact * up_ref[rows, :].astype(F32) * (sg * (1.0 + gt * (1.0 - sg)))).astype(BF16)
            acc[rows, :] += (_nn(dgate_ref[rows, :], wg_ref[...].reshape(TN_FF, d))
                             + _nn(dup_ref[rows, :], wu_ref[...].reshape(TN_FF, d)))

        @pl.when(j == nc - 1)
        def _():
            dv, dg = _rms_bwd(h1_ref[...], g_ref[...], acc[...])
            dh1_ref[...] = dh2_ref[...] + dv
            dg_ref[...] += dg

    rowblk = pl.BlockSpec((ts, d), lambda i, j: (i, 0))
    chunk = pl.BlockSpec((ts, TN_FF), lambda i, j: (i, j))
    vec = pl.BlockSpec((1, d), lambda i, j: (0, 0))
    return pl.pallas_call(
        body, grid=(s // ts, nc), name="ffn_bwd",
        out_shape=(jax.ShapeDtypeStruct((s, D_FF), BF16), jax.ShapeDtypeStruct((s, D_FF), BF16),
                   jax.ShapeDtypeStruct((s, d), F32), jax.ShapeDtypeStruct((1, d), F32)),
        in_specs=[rowblk, chunk, chunk, _spec_ff(2), _spec_ff(0), _spec_ff(1), rowblk, rowblk, vec],
        out_specs=(chunk, chunk, rowblk, vec),
        scratch_shapes=[pltpu.VMEM((ts, d), F32)],
        compiler_params=_params(2),
    )(dff, gate, up, wd, wg, wu, h1, dh2, g_pre)


def _post_attn_bwd(dh1, o, a, mpre, wout, wpool, g_post, g_attn, g_pool, pscale, send):
    s, d = dh1.shape
    sub = TS // TQ
    npc = len(send)

    def body(dh1_ref, o_ref, a_ref, mp_ref, wo_ref, wp_ref, gpost_ref, ga_ref, gp_ref, ps_ref, *refs):
        send_refs, refs = refs[:npc], refs[npc:]
        dob_ref, dat_ref, dlt_ref, dmpb_ref, dy_ref, dgpost_ref, dga_ref, dgp_ref, dps_ref = refs[:9]
        got_refs, (send_sems, recv_sems) = refs[9:9 + npc], refs[9 + npc:]
        i = pl.program_id(0)

        @pl.when(i == 0)
        def _():
            for cp in _pair_copies(send_refs, got_refs, send_sems, recv_sems):
                cp.start()
            dgpost_ref[...] = jnp.zeros_like(dgpost_ref)
            dga_ref[...] = jnp.zeros_like(dga_ref)
            dgp_ref[...] = jnp.zeros_like(dgp_ref)
            dps_ref[...] = jnp.zeros_like(dps_ref)

        do, dg = _rms_bwd(o_ref[...], gpost_ref[...], dh1_ref[...])
        dgpost_ref[...] += dg
        dob = do.astype(BF16)
        dob_ref[...] = dob
        dmix = _nt(dob, wo_ref[...].reshape(d, d))

        av = a_ref[...]
        da, dg = _rms_bwd(av, ga_ref[...], dmix[:, 0:D_ATTN])
        dga_ref[...] += dg
        dat = da.astype(BF16).T
        hsel = (lax.shift_right_logical(lax.broadcasted_iota(jnp.int32, (HEADS, D_ATTN), 1), 6)
                == lax.broadcasted_iota(jnp.int32, (HEADS, D_ATTN), 0)).astype(F32)
        dlt = lax.dot_general(hsel, da * av, (((1,), (1,)), ((), ())), precision=HIGHEST, preferred_element_type=F32)
        for q in range(sub):
            dlt_ref[q] = dlt[:, q * TQ:(q + 1) * TQ]
            dat_ref[q] = dat[:, q * TQ:(q + 1) * TQ]

        ps = ps_ref[...]
        mp = mp_ref[...]
        dm, dg = _rms_bwd(mp * ps, gp_ref[...], dmix[:, D_ATTN:])
        dgp_ref[...] += dg
        dps_ref[...] += jnp.sum(dm * mp, axis=0, keepdims=True)
        dmpb = (dm * ps).astype(BF16)
        dmpb_ref[...] = dmpb
        for g in range(len(POOL_WINDOWS)):
            cols = slice(g * POOL_CH, (g + 1) * POOL_CH)
            dy_ref[:, cols] = _nt(dmpb[:, cols], wp_ref[g])

        @pl.when(i == s // TS - 1)
        def _():
            for cp in _pair_copies(send_refs, got_refs, send_sems, recv_sems):
                cp.wait()

    rowblk = pl.BlockSpec((TS, d), _row)
    half = pl.BlockSpec((TS, D_ATTN), _row)
    vec = lambda n: pl.BlockSpec((1, n), _fixed)
    nk = N_DEV // 2
    res = pl.pallas_call(
        body, grid=(s // TS,), name="post_attn_bwd",
        out_shape=(jax.ShapeDtypeStruct((s, d), BF16), jax.ShapeDtypeStruct((s // TQ, D_ATTN, TQ), BF16),
                   jax.ShapeDtypeStruct((s // TQ, HEADS, TQ), F32), jax.ShapeDtypeStruct((s, D_POOL), BF16),
                   jax.ShapeDtypeStruct((s, D_POOL), F32), jax.ShapeDtypeStruct((1, d), F32),
                   jax.ShapeDtypeStruct((1, D_ATTN), F32), jax.ShapeDtypeStruct((1, D_POOL), F32),
                   jax.ShapeDtypeStruct((1, D_POOL), F32))
        + tuple(jax.ShapeDtypeStruct((nk,) + t.shape[1:], t.dtype) for t in send),
        in_specs=[rowblk, rowblk, half, half, _spec_square(0),
                  pl.BlockSpec(wpool.shape, lambda i: (0, 0, 0)), vec(d), vec(D_ATTN), vec(D_POOL), vec(D_POOL)]
        + [ANY] * npc,
        out_specs=(rowblk, pl.BlockSpec((sub, D_ATTN, TQ), lambda i: (i, 0, 0)),
                   pl.BlockSpec((sub, HEADS, TQ), lambda i: (i, 0, 0)), half, half,
                   vec(d), vec(D_ATTN), vec(D_POOL), vec(D_POOL)) + (ANY,) * npc,
        scratch_shapes=[pltpu.SemaphoreType.DMA((nk, npc)), pltpu.SemaphoreType.DMA((nk, npc))],
        compiler_params=_params(1),
    )(dh1, o, a, mpre, wout, wpool, g_post, g_attn, g_pool, pscale, *send)
    return res[:9], list(res[9:])


def _attn_bwd(ka, v, kt3, qat3, qt3, dot3, lset3, dlt3, chip_blocks, small_block):
    s = ka.shape[0]
    nq = s // TQ

    def body(ka_ref, v_ref, kt_ref, qat_ref, qt_ref, dot_ref, lset_ref, dlt_ref, b_ref, sm_ref,
             dqt_ref, dkt_ref, dvt_ref, got_ref, all_ref, pt_scr, ptb_scr, dsb_scr,
             stage, send_sems, recv_sems, local_sem, stage_s, send_s, recv_s, local_s):
        j = pl.program_id(0)

        @pl.when(j == 0)
        def _():
            _chips_start(b_ref, got_ref, stage, send_sems, recv_sems, local_sem)
            _gather_start(sm_ref, all_ref, stage_s, send_s, recv_s, local_s)
            dqt_ref[...] = jnp.zeros_like(dqt_ref)

        @pl.when(j == max(nq - 2, 0))
        def _():
            _gather_pass_on(all_ref, send_s, recv_s)

        def tile(i, masked):
            def accumulate(ref, idx, val):
                if masked:
                    ref[idx] = val
                else:
                    ref[idx] += val

            for h in range(HEADS):
                aug = slice(h * AUG, (h + 1) * AUG)
                st = _nn(ka_ref[:, aug], qat_ref[i, aug, :]) - lset_ref[i, h:h + 1, :]
                if masked:
                    st = jnp.where(_causal_in_tile(), st, NEG)
                pt = jnp.exp2(st)
                pt_scr[h] = pt
                ptb_scr[h] = pt.astype(BF16)
            heads = [(h, slice(h * HEAD_DIM, (h + 1) * HEAD_DIM)) for h in range(HEADS)]
            for h, hs in heads:
                dst = pt_scr[h] * (_nn(v_ref[:, hs], dot_ref[i, hs, :]) - dlt_ref[i, h:h + 1, :])
                dsb_scr[h] = dst.astype(BF16)
            for h, hs in heads:
                accumulate(dvt_ref, (0, hs, slice(None)), _nt(dot_ref[i, hs, :], ptb_scr[h]))
            for h, hs in heads:
                rows = slice(h * VROWS, (h + 1) * VROWS)
                accumulate(dkt_ref, (0, rows, slice(None)), _nt(qt_ref[i, rows, :], dsb_scr[h]))
            for h, hs in heads:
                rows = slice(h * VROWS, (h + 1) * VROWS)
                dqt_ref[i, rows, :] += _nn(kt_ref[0, rows, :], dsb_scr[h])

        first = j + 1
        pairs = (nq - first) // 2

        def step(p, carry):
            tile(first + 2 * p, False)
            tile(first + 2 * p + 1, False)
            return carry

        tile(j, True)
        lax.fori_loop(0, pairs, step, 0)

        @pl.when(first + 2 * pairs < nq)
        def _():
            tile(nq - 1, False)

        @pl.when(j == nq - 1)
        def _():
            _chips_finish(b_ref, got_ref, send_sems, recv_sems)
            _gather_finish(sm_ref, all_ref, send_s, recv_s)

    blk = pl.BlockSpec((TQ, D_ATTN), _row)
    tile_t = lambda rows: pl.BlockSpec((1, rows, TQ), lambda j: (j, 0, 0))
    per_tile = lambda rows: jax.ShapeDtypeStruct((nq, rows, TQ), F32)
    _, r, cdim = chip_blocks.shape
    dma = pltpu.SemaphoreType.DMA
    return pl.pallas_call(
        body, grid=(nq,), name="attn_bwd",
        out_shape=(per_tile(HEADS * VROWS), per_tile(HEADS * VROWS), per_tile(D_ATTN),
                   jax.ShapeDtypeStruct(chip_blocks.shape, chip_blocks.dtype),
                   jax.ShapeDtypeStruct((N_DEV,) + small_block.shape, small_block.dtype)),
        in_specs=[pl.BlockSpec((TQ, HEADS * AUG), _row), blk, tile_t(HEADS * VROWS),
                  VMEM_WHOLE, VMEM_WHOLE, VMEM_WHOLE, VMEM_WHOLE, VMEM_WHOLE, ANY, ANY],
        out_specs=(pl.BlockSpec((nq, HEADS * VROWS, TQ), lambda j: (0, 0, 0)), tile_t(HEADS * VROWS), tile_t(D_ATTN),
                   ANY, ANY),
        scratch_shapes=[pltpu.VMEM((HEADS, TQ, TQ), F32), pltpu.VMEM((HEADS, TQ, TQ), BF16),
                        pltpu.VMEM((HEADS, TQ, TQ), BF16), pltpu.VMEM((r, cdim), chip_blocks.dtype),
                        dma((3,)), dma((3,)), dma,
                        pltpu.VMEM(small_block.shape, small_block.dtype), dma((7,)), dma((7,)), dma],
        compiler_params=_params(1),
    )(ka, v, kt3, qat3, qt3, dot3, lset3, dlt3, chip_blocks, small_block)


def _pre_attn_bwd(dqt3, dkt3, dvt3, fl, dy, x, dh1, g1, wqkv, wf, wu):
    s, d = x.shape
    nt = s // TS
    n = TS + HALO
    sub = TS // TQ
    qkv, fcols = 3 * D_ATTN, 3 * D_ATTN + LANES

    def body(dqt_ref, dkt_ref, dvt_ref, fl_ref, dy_ref, x_ref, dh1_ref, g_ref, wqkv_ref, wf_ref, wu_ref,
             gx_ref, dz_ref, dg_ref, db_ref, ybuf, ccar, dlog, dsum):
        dqkv_ref = dz_ref.at[:, 0:qkv]
        dfb_ref = dz_ref.at[:, qkv:fcols]
        dub_ref = dz_ref.at[:, fcols:]
        i = pl.program_id(0)
        ti = nt - 1 - i

        @pl.when(i == 0)
        def _():
            ybuf[TS:n, :] = jnp.zeros((HALO, D_POOL), F32)
            ccar[...] = jnp.zeros_like(ccar)
            dg_ref[...] = jnp.zeros_like(dg_ref)
            db_ref[...] = jnp.zeros_like(db_ref)
            dsum[...] = jnp.zeros_like(dsum)

        for a in range(sub):
            for h in range(HEADS):
                r = h * VROWS + HEAD_DIM
                dsum[h:h + 1, a * TQ:(a + 1) * TQ] = dqt_ref[a, r:r + 1, :] - dkt_ref[a, r:r + 1, :]
        dlog[...] = ccar[...] + _running_sum(dsum[...].T, reverse=True)
        ccar[...] = dlog[0:1, :]
        df = dlog[...] * jax.nn.sigmoid(-fl_ref[...])
        db_ref[...] += jnp.sum(df, axis=0, keepdims=True)
        dfb = df.astype(BF16)
        dfb_ref[...] = dfb

        t = ti * TS + lax.broadcasted_iota(jnp.int32, (TS, 1), 0)
        dy = dy_ref[...]
        for g, w in enumerate(POOL_WINDOWS):
            cols = slice(g * POOL_CH, (g + 1) * POOL_CH)
            ybuf[0:TS, cols] = dy[:, cols] / jnp.minimum(t + 1, w).astype(F32)
        for g, w in enumerate(POOL_WINDOWS):
            cols = slice(g * POOL_CH, (g + 1) * POOL_CH)
            sm = ybuf[:, cols]
            step = 1
            while step < w:
                sm = sm + pltpu.roll(sm, n - step, 0)
                step *= 2
            dub_ref[:, cols] = (sm[0:TS, :] - dy[:, cols]).astype(BF16)
        ybuf[TS:n, :] = ybuf[0:HALO, :]

        for a in range(sub):
            rows = slice(a * TQ, (a + 1) * TQ)
            for h in range(HEADS):
                src = slice(h * VROWS, h * VROWS + HEAD_DIM)
                dqkv_ref[rows, h * HEAD_DIM:(h + 1) * HEAD_DIM] = (dqt_ref[a, src, :].T * 0.125).astype(BF16)
                dqkv_ref[rows, D_ATTN + h * HEAD_DIM:D_ATTN + (h + 1) * HEAD_DIM] = dkt_ref[a, src, :].T.astype(BF16)
            dqkv_ref[rows, 2 * D_ATTN:] = dvt_ref[a].T.astype(BF16)
        dhn = _nn(dqkv_ref[...], wqkv_ref[...]) + _nn(dfb, wf_ref[...]) + _nn(dub_ref[...], wu_ref[...])
        dx, dg = _rms_bwd(x_ref[...], g_ref[...], dhn)
        gx_ref[...] = dh1_ref[...] + dx
        dg_ref[...] += dg

    rev = lambda i: (nt - 1 - i, 0)
    blk = lambda w: pl.BlockSpec((TS, w), rev)
    return pl.pallas_call(
        body, grid=(nt,), name="pre_attn_bwd",
        out_shape=(jax.ShapeDtypeStruct((s, d), F32), jax.ShapeDtypeStruct((s, fcols + D_POOL), BF16),
                   jax.ShapeDtypeStruct((1, d), F32), jax.ShapeDtypeStruct((1, LANES), F32)),
        in_specs=[pl.BlockSpec((sub, HEADS * VROWS, TQ), lambda i: (nt - 1 - i, 0, 0)),
                  pl.BlockSpec((sub, HEADS * VROWS, TQ), lambda i: (nt - 1 - i, 0, 0)),
                  pl.BlockSpec((sub, D_ATTN, TQ), lambda i: (nt - 1 - i, 0, 0)),
                  blk(LANES), blk(D_POOL), blk(d), blk(d),
                  pl.BlockSpec((1, d), _fixed), pl.BlockSpec((qkv, d), _fixed), pl.BlockSpec(wf.shape, _fixed),
                  pl.BlockSpec(wu.shape, _fixed)],
        out_specs=(blk(d), blk(fcols + D_POOL), pl.BlockSpec((1, d), _fixed), pl.BlockSpec((1, LANES), _fixed)),
        scratch_shapes=[pltpu.VMEM((n, D_POOL), F32), pltpu.VMEM((1, LANES), F32), pltpu.VMEM((TS, LANES), F32),
                        pltpu.VMEM((LANES, TS), F32)],
        compiler_params=_params(1),
    )(dqt3, dkt3, dvt3, fl, dy, x, dh1, g1, wqkv, wf, wu)


def _wgrad(a, b, out_dtype, name):
    s, m = a.shape
    n = b.shape[1]
    tm = max(t for t in range(LANES, min(m, TM_WGRAD) + 1, LANES) if m % t == 0)
    ts = min(TS_WGRAD, s)
    ns = s // ts

    def body(a_ref, b_ref, o_ref, acc):
        i = pl.program_id(1)

        @pl.when(i == 0)
        def _():
            acc[...] = jnp.zeros_like(acc)

        acc[...] += _tn(a_ref[...], b_ref[...])

        @pl.when(i == ns - 1)
        def _():
            o_ref[...] = acc[...].astype(out_dtype)

    return pl.pallas_call(
        body, grid=(m // tm, ns), name=name, out_shape=jax.ShapeDtypeStruct((m, n), out_dtype),
        in_specs=[pl.BlockSpec((ts, tm), lambda j, i: (i, j)), pl.BlockSpec((ts, n), lambda j, i: (i, 0))],
        out_specs=pl.BlockSpec((tm, n), lambda j, i: (j, 0)),
        scratch_shapes=[pltpu.VMEM((tm, n), F32)],
        compiler_params=_params(2),
    )(a, b)


def _wgrad_in(dz, hn):
    s, m = dz.shape
    n = hn.shape[1]
    ts = min(TS_WGRAD, s)
    ns = s // ts
    pad_at, pad = 3 * D_ATTN + HEADS, LANES - HEADS
    assert m == D_IN + pad and N_DEV * SHARD_IN == D_IN

    def pieces(d):
        lo, hi = d * SHARD_IN, (d + 1) * SHARD_IN
        spans = [(lo, min(hi, pad_at), 0), (max(lo, pad_at), hi, pad)]
        return [(a + shift, b - a, a - lo) for a, b, shift in spans if b > a]

    def body(a_ref, b_ref, o_ref, acc, stage):
        i = pl.program_id(0)

        @pl.when(i == 0)
        def _():
            acc[...] = jnp.zeros_like(acc)

        acc[...] += _tn(a_ref[...], b_ref[...])

        @pl.when(i == ns - 1)
        def _():
            stage[SHARD_IN:ROWS_IN, :] = jnp.zeros((ROWS_IN - SHARD_IN, n), F32)
            for d in range(N_DEV):
                for src, rows, dst in pieces(d):
                    stage[dst:dst + rows, :] = acc[src:src + rows, :]
                o_ref[d] = stage[...].astype(BF16)

    return pl.pallas_call(
        body, grid=(ns,), name="wgrad_in", out_shape=jax.ShapeDtypeStruct((N_DEV, ROWS_IN, n), BF16),
        in_specs=[pl.BlockSpec((ts, m), _row), pl.BlockSpec((ts, n), _row)],
        out_specs=pl.BlockSpec((N_DEV, ROWS_IN, n), lambda i: (0, 0, 0)),
        scratch_shapes=[pltpu.VMEM((m, n), F32), pltpu.VMEM((ROWS_IN, n), F32)],
        compiler_params=_params(1),
    )(dz, hn)


def _adamw(w, g, m, v):
    m = ADAM_B1 * m + (1.0 - ADAM_B1) * g
    v = ADAM_B2 * v + (1.0 - ADAM_B2) * (g * g)
    m_hat = m / (1.0 - ADAM_B1 ** ADAM_STEP)
    v_hat = v / (1.0 - ADAM_B2 ** ADAM_STEP)
    delta = -ADAM_LR * (m_hat / (jnp.sqrt(v_hat) + ADAM_EPS) + ADAM_WD * w)
    return delta, m, v


def _sum_update(p_ref, w_ref, m_ref, v_ref, g_ref, d_ref, nm_ref, nv_ref):
    g = p_ref[0].astype(F32)
    for k in range(1, p_ref.shape[0]):
        g = g + p_ref[k].astype(F32)
    g_ref[...] = g
    d_ref[...], nm_ref[...], nv_ref[...] = _adamw(w_ref[...], g, m_ref[...], v_ref[...])


def _reduce_update_rest(parts, w, m, v, chip_blocks, small_block):
    nk, r, c = parts.shape
    ns = r // TR_REST

    def body(p_ref, w_ref, m_ref, v_ref, b_ref, sm_ref, g_ref, d_ref, nm_ref, nv_ref, got_ref, all_ref,
             stage_b, stage_s, send_b, recv_b, local_b, send_s, recv_s, local_s):
        i = pl.program_id(0)

        @pl.when(i == 0)
        def _():
            _chips_start(b_ref, got_ref, stage_b, send_b, recv_b, local_b)
            _gather_start(sm_ref, all_ref, stage_s, send_s, recv_s, local_s)

        _sum_update(p_ref, w_ref, m_ref, v_ref, g_ref, d_ref, nm_ref, nv_ref)

        @pl.when(i == ns - 1)
        def _():
            _gather_pass_on(all_ref, send_s, recv_s)
            _chips_finish(b_ref, got_ref, send_b, recv_b)
            _gather_finish(sm_ref, all_ref, send_s, recv_s)

    blk = pl.BlockSpec((TR_REST, c), _row)
    out = jax.ShapeDtypeStruct((r, c), F32)
    dma = pltpu.SemaphoreType.DMA
    return pl.pallas_call(
        body, grid=(ns,), name="reduce_update_rest",
        out_shape=(out,) * 4 + (jax.ShapeDtypeStruct(chip_blocks.shape, chip_blocks.dtype),
                                jax.ShapeDtypeStruct((N_DEV,) + small_block.shape, small_block.dtype)),
        in_specs=[pl.BlockSpec((nk, TR_REST, c), lambda i: (0, i, 0)), blk, blk, blk, ANY, ANY],
        out_specs=(blk,) * 4 + (ANY, ANY),
        scratch_shapes=[pltpu.VMEM(chip_blocks.shape[1:], chip_blocks.dtype), pltpu.VMEM(small_block.shape, small_block.dtype),
                        dma((3,)), dma((3,)), dma, dma((7,)), dma((7,)), dma],
        compiler_params=_params(1),
    )(parts, w, m, v, chip_blocks, small_block)


def _reduce_update_big(parts, w, m, v, tr, name):
    nk, r, c = parts.shape

    def body(p_ref, w_ref, m_ref, v_ref, g_ref, d_ref, nm_ref, nv_ref):
        _sum_update(p_ref, w_ref, m_ref, v_ref, g_ref, d_ref, nm_ref, nv_ref)

    blk = pl.BlockSpec((tr, c), _row)
    out = jax.ShapeDtypeStruct((r, c), F32)
    return pl.pallas_call(
        body, grid=(r // tr,), name=name, out_shape=(out,) * 4,
        in_specs=[pl.BlockSpec((nk, tr, c), lambda i: (0, i, 0)), blk, blk, blk],
        out_specs=(blk,) * 4, compiler_params=_params(1),
    )(parts, w, m, v)


def _reduce_update_small(parts, late, w, m, v):
    nd = parts.shape[0]
    first = parts.shape[1] - late.shape[1]

    def body(p_ref, q_ref, w_ref, m_ref, v_ref, g_ref, d_ref, nm_ref, nv_ref):
        g, t = p_ref[0], q_ref[0]
        for k in range(1, nd):
            g, t = g + p_ref[k], t + q_ref[k]
        g_ref[...] = g
        g_ref[first:, :] = g[first:, :] + t
        d_ref[...], nm_ref[...], nv_ref[...] = _adamw(w_ref[...], g_ref[...], m_ref[...], v_ref[...])

    out = jax.ShapeDtypeStruct(w.shape, F32)
    return pl.pallas_call(body, name="reduce_update_small", out_shape=(out,) * 4,
                          compiler_params=pltpu.CompilerParams(vmem_limit_bytes=VMEM_LIMIT))(parts, late, w, m, v)


MESH = pl.DeviceIdType.MESH


def _copy_through_vmem(src_hbm, dst_hbm, stage, sem):
    load = pltpu.make_async_copy(src_hbm, stage, sem)
    load.start()
    load.wait()
    store = pltpu.make_async_copy(stage, dst_hbm, sem)
    store.start()
    store.wait()


class _GatherPlan:
    def __init__(self, x_ref, out_ref, send_sems, recv_sems):
        x, y, c = lax.axis_index("x"), lax.axis_index("y"), lax.axis_index("c")
        self.me, self.sibling, self.c = (x, y, c), (x, y, 1 - c), c
        self.chips = [(1 - x, y), (x, 1 - y), (1 - x, 1 - y)]
        self.x_ref, self.out_ref, self.send_sems, self.recv_sems = x_ref, out_ref, send_sems, recv_sems

    def slot(self, px, py, pc):
        return self.out_ref.at[4 * px + 2 * py + pc]

    def copy(self, k, block, to, src=None):
        return pltpu.make_async_remote_copy(
            src_ref=self.slot(*block) if src is None else src, dst_ref=self.slot(*block),
            send_sem=self.send_sems.at[k], recv_sem=self.recv_sems.at[k], device_id=to, device_id_type=MESH)

    def first(self):
        return [self.copy(0, self.me, self.sibling, src=self.x_ref)] + [
            self.copy(1 + j, self.me, (*chip, self.c), src=self.x_ref) for j, chip in enumerate(self.chips)]

    def passed(self):
        return [self.copy(4 + j, (*chip, self.c), self.sibling) for j, chip in enumerate(self.chips)]


def _gather_start(x_ref, out_ref, stage, send_sems, recv_sems, local_sem):
    plan = _GatherPlan(x_ref, out_ref, send_sems, recv_sems)
    for cp in plan.first():
        cp.start()
    _copy_through_vmem(x_ref, plan.slot(*plan.me), stage, local_sem)


def _gather_pass_on(out_ref, send_sems, recv_sems):
    plan = _GatherPlan(None, out_ref, send_sems, recv_sems)
    passed = plan.passed()
    for j, chip in enumerate(plan.chips):
        plan.copy(1 + j, (*chip, plan.c), plan.me).wait_recv()
        passed[j].start()


def _gather_finish(x_ref, out_ref, send_sems, recv_sems):
    plan = _GatherPlan(x_ref, out_ref, send_sems, recv_sems)
    plan.copy(0, plan.sibling, plan.me).wait_recv()
    for j, chip in enumerate(plan.chips):
        plan.copy(4 + j, (*chip, 1 - plan.c), plan.me).wait_recv()
    for cp in plan.first() + plan.passed():
        cp.wait_send()


def _gather_w_in(xs):
    r, cdim = xs.shape
    qkv, f_end = 3 * D_ATTN, 3 * D_ATTN + HEADS

    def body(x_ref, wqkv_ref, wf_ref, wu_ref, send_sems, recv_sems, local_sem, blocks, flat):
        plan = _GatherPlan(x_ref, blocks, send_sems, recv_sems)
        for cp in plan.first():
            cp.start()
        own = pltpu.make_async_copy(x_ref, plan.slot(*plan.me), local_sem)
        own.start()
        own.wait()
        _gather_pass_on(blocks, send_sems, recv_sems)
        _gather_finish(x_ref, blocks, send_sems, recv_sems)
        for dev in range(N_DEV):
            flat[dev * SHARD_IN:(dev + 1) * SHARD_IN, :] = blocks[dev, 0:SHARD_IN, :].astype(F32)
        wqkv_ref[...] = flat[0:qkv, :].astype(BF16)
        wf_ref[...] = jnp.concatenate([flat[qkv:f_end, :], jnp.zeros((LANES - HEADS, cdim), F32)], axis=0).astype(BF16)
        wu_ref[...] = flat[f_end:D_IN, :].astype(BF16)

    shape = lambda rows: jax.ShapeDtypeStruct((rows, cdim), xs.dtype)
    dma = pltpu.SemaphoreType.DMA
    return pl.pallas_call(
        body, name="gather_w_in",
        out_shape=(shape(qkv), shape(LANES), shape(D_IN - f_end)),
        in_specs=[ANY], out_specs=(VMEM_WHOLE, VMEM_WHOLE, VMEM_WHOLE),
        scratch_shapes=[dma((7,)), dma((7,)), dma,
                        pltpu.VMEM((N_DEV, r, cdim), xs.dtype), pltpu.VMEM((D_IN, cdim), F32)],
        compiler_params=pltpu.CompilerParams(vmem_limit_bytes=VMEM_LIMIT),
    )(xs)


def _pair_copies(src_refs, dst_refs, send_sems, recv_sems):
    x, y, c = lax.axis_index("x"), lax.axis_index("y"), lax.axis_index("c")
    return [pltpu.make_async_remote_copy(
        src_ref=src.at[2 * k + (1 - c)], dst_ref=dst.at[k], send_sem=send_sems.at[k, p], recv_sem=recv_sems.at[k, p],
        device_id=(x, y, 1 - c), device_id_type=MESH)
        for k in range(N_DEV // 2) for p, (src, dst) in enumerate(zip(src_refs, dst_refs))]


def _rs_pair_sum(core, pieces, offsets, rows, name, landed=()):
    cdim = pieces[0].shape[2]
    nk = N_DEV // 2
    npc = len(pieces)
    nrem = npc - len(landed)
    spans = [(o, t.shape[1]) for t, o in zip(pieces, offsets)]
    ends = [o + n for o, n in spans]
    gaps = [(a, b - a) for a, b in zip(ends, [o for o, _ in spans[1:]] + [rows]) if b > a]

    def body(core_ref, *refs):
        own, src, got, o_ref = refs[:npc], refs[npc:npc + nrem], refs[npc + nrem:2 * npc], refs[2 * npc]
        landing, send_sems, recv_sems = refs[2 * npc + 1:]
        k = pl.program_id(0)
        x, y, c = lax.axis_index("x"), lax.axis_index("y"), lax.axis_index("c")

        def copies(kk):
            return [pltpu.make_async_remote_copy(
                src_ref=src[p].at[2 * kk + (1 - c)], dst_ref=landing.at[kk, pl.ds(o, n)],
                send_sem=send_sems.at[kk, p], recv_sem=recv_sems.at[kk, p], device_id=(x, y, 1 - c),
                device_id_type=MESH) for p, (o, n) in enumerate(spans[:nrem])]

        @pl.when(k == 0)
        def _():
            for kk in range(nk):
                for cp in copies(kk):
                    cp.start()

        for cp, piece, (o, n) in zip(copies(k), own, spans):
            cp.wait_recv()
            o_ref[0, o:o + n, :] = (piece[0].astype(F32) + landing[k, o:o + n, :].astype(F32)).astype(BF16)
        for theirs, piece, (o, n) in zip(got, own[nrem:], spans[nrem:]):
            o_ref[0, o:o + n, :] = (piece[0].astype(F32) + theirs[0].astype(F32)).astype(BF16)
        for o, n in gaps:
            o_ref[0, o:o + n, :] = jnp.zeros((n, cdim), BF16)

        @pl.when(k == nk - 1)
        def _():
            for kk in range(nk):
                for cp in copies(kk):
                    cp.wait_send()

    own_specs = [pl.BlockSpec((1, n, cdim), lambda k, core_ref: (2 * k + core_ref[0], 0, 0)) for _, n in spans]
    got_specs = [pl.BlockSpec((1, n, cdim), lambda k, core_ref: (k, 0, 0)) for _, n in spans[nrem:]]
    land_rows = max(o + n for o, n in spans[:nrem])
    return pl.pallas_call(
        body, name=name, out_shape=jax.ShapeDtypeStruct((nk, rows, cdim), BF16),
        grid_spec=pltpu.PrefetchScalarGridSpec(
            num_scalar_prefetch=1, grid=(nk,),
            in_specs=own_specs + [ANY] * nrem + got_specs,
            out_specs=pl.BlockSpec((1, rows, cdim), lambda k, core_ref: (k, 0, 0)),
            scratch_shapes=[pltpu.VMEM((nk, land_rows, cdim), BF16), pltpu.SemaphoreType.DMA((nk, nrem)),
                            pltpu.SemaphoreType.DMA((nk, nrem))]),
        compiler_params=_params(1),
    )(core, *pieces, *pieces[:nrem], *landed)


def _chips_start(b_ref, out_ref, stage, send_sems, recv_sems, local_sem):
    x, y, c = lax.axis_index("x"), lax.axis_index("y"), lax.axis_index("c")
    mychip = 2 * x + y
    for j, (px, py) in enumerate([(1 - x, y), (x, 1 - y), (1 - x, 1 - y)]):
        pltpu.make_async_remote_copy(
            src_ref=b_ref.at[2 * px + py], dst_ref=out_ref.at[mychip],
            send_sem=send_sems.at[j], recv_sem=recv_sems.at[j], device_id=(px, py, c), device_id_type=MESH).start()
    _copy_through_vmem(b_ref.at[mychip], out_ref.at[mychip], stage, local_sem)


def _chips_finish(b_ref, out_ref, send_sems, recv_sems):
    x, y, c = lax.axis_index("x"), lax.axis_index("y"), lax.axis_index("c")
    for j, (px, py) in enumerate([(1 - x, y), (x, 1 - y), (1 - x, 1 - y)]):
        pltpu.make_async_remote_copy(
            src_ref=b_ref.at[2 * px + py], dst_ref=out_ref.at[2 * px + py],
            send_sem=send_sems.at[j], recv_sem=recv_sems.at[j], device_id=(px, py, c), device_id_type=MESH).wait()


def _pad_rows(a, rows):
    return jnp.pad(a, ((0, rows - a.shape[0]), (0, 0)))


def _pack_in(w_in):
    return _pad_rows(w_in[0].T, ROWS_IN)


def _unpack_in(r):
    return r[0:SHARD_IN].T[None]


def _pack_rest(w_out, w_gate, w_up, w_down, w_ple, w_pg):
    head = _pad_rows(jnp.concatenate([w_out[0], w_pg[0], w_ple[0].T.reshape(ROWS_PLE, D_MODEL)], axis=0), OFF_GATE)
    return jnp.concatenate([head, w_gate[0].T, w_up[0].T, w_down[0]], axis=0)


def _unpack_rest(r):
    return (r[0:OFF_PG][None], r[OFF_GATE:OFF_UP].T[None], r[OFF_UP:OFF_DOWN].T[None], r[OFF_DOWN:ROWS_REST][None],
            r[OFF_PLE:OFF_PLE + ROWS_PLE].reshape(SHARD_SQ, D_PLE).T[None], r[OFF_PG:OFF_PLE][None])


def _pack_small(w_pool, g_mix_pre, g_mix_post, g_ffn_pre, g_ffn_post, g_ple, g_attn, g_pool, pool_scale, b_forget,
                loss=None):
    row = lambda vrow: vrow.reshape(1, -1)
    misc = [row(pool_scale), row(b_forget), row(loss) if loss is not None else jnp.zeros((1, 1), F32),
            jnp.zeros((1, D_MODEL - COL_LOSS - 1), F32)]
    rows = [w_pool.reshape(64, D_MODEL), row(g_mix_pre), row(g_mix_post), row(g_ffn_pre), row(g_ffn_post), row(g_ple),
            jnp.concatenate([row(g_attn), row(g_pool)], axis=1), jnp.concatenate(misc, axis=1),
            jnp.zeros((SMALL_ROWS - ROW_MISC - 1, D_MODEL), F32)]
    return jnp.concatenate(rows, axis=0)


def _pack_small_late(g_mix_pre, b_forget):
    misc = [jnp.zeros((1, COL_B_FORGET), F32), b_forget.reshape(1, -1), jnp.zeros((1, D_MODEL - COL_LOSS), F32)]
    return jnp.concatenate([g_mix_pre.reshape(1, -1), jnp.zeros((ROW_MISC - ROW_G_MIX_PRE - 1, D_MODEL), F32),
                            jnp.concatenate(misc, axis=1), jnp.zeros((SMALL_ROWS - ROW_MISC - 1, D_MODEL), F32)], axis=0)


def _unpack_small(r):
    gains, misc = r[ROW_GROUP_GAINS:ROW_GROUP_GAINS + 1], r[ROW_MISC:ROW_MISC + 1]
    return dict(
        w_pool=r[0:64].reshape(1, 4, POOL_CH, POOL_CH), g_mix_pre=r[ROW_G_MIX_PRE:ROW_G_MIX_PRE + 1],
        g_mix_post=r[ROW_G_MIX_POST:ROW_G_MIX_POST + 1], g_ffn_pre=r[ROW_G_FFN_PRE:ROW_G_FFN_PRE + 1],
        g_ffn_post=r[ROW_G_FFN_POST:ROW_G_FFN_POST + 1], g_ple=r[ROW_G_PLE:ROW_G_PLE + 1],
        g_attn_grp=gains[:, 0:D_ATTN], g_pool_grp=gains[:, D_ATTN:D_ATTN + D_POOL],
        pool_scale=misc[:, 0:D_POOL], b_forget=misc[:, COL_B_FORGET:COL_B_FORGET + HEADS])


def _step(x, p, tgt, small, in_w, in_m, in_v, rest_w, rest_m, rest_v):
    core = lax.axis_index("c").astype(jnp.int32).reshape(1)
    wqkv, wf, wu = _gather_w_in(in_w.astype(BF16))
    wpool = small["w_pool"].astype(BF16)
    bpad = jnp.pad(small["b_forget"], ((0, 0), (0, LANES - HEADS)))

    lay = _attn_layout_constants()
    rest_b = rest_w.astype(BF16)
    hn, qt3, ka, v, qat3, vt3, kt3, fl, y, mpre, gh = _pre_attn_fwd(x, small["g_mix_pre"], wqkv, wf, wu, bpad, wpool, lay,
                                                                 rest_b[0:OFF_GATE])
    a, lset3, gf = _attn_fwd(ka, qat3, vt3, rest_b[OFF_GATE:])
    wple_t = gh[:, OFF_PLE:OFF_PLE + ROWS_PLE].reshape(D_MODEL, D_PLE)
    mix, o, h1, hn2 = _post_attn_fwd(a, mpre, x, small["g_attn_grp"], small["g_pool_grp"], small["pool_scale"], gh,
                                     small["g_mix_post"], small["g_ffn_pre"])
    gate, up, act, ff, h2 = _ffn_fwd(hn2, gf, gf, gf, h1, small["g_ffn_post"])
    dh2, dff, dgl, dpp, h2b, pb, loss8, dg_ple, dg_ffn_post = _tail_fwd_bwd(
        h2, p, tgt, ff, wple_t, gh, small["g_ple"], small["g_ffn_post"])
    dgate, dup, dh1, dg_ffn_pre = _ffn_bwd(dff, gate, up, gf, gf, gf, h1, dh2, small["g_ffn_pre"])
    nd = N_DEV
    send_rest = [
        _wgrad(h2b, dgl, BF16, "wgrad_ple_gate").reshape(nd, SHARD_SQ, D_MODEL),
        _wgrad(dpp, pb, BF16, "wgrad_ple").reshape(nd, ROWS_PLE, D_MODEL),
        _wgrad(dgate, hn2, BF16, "wgrad_gate").reshape(nd, SHARD_FF, D_MODEL),
        _wgrad(dup, hn2, BF16, "wgrad_up").reshape(nd, SHARD_FF, D_MODEL),
        _wgrad(act, dff, BF16, "wgrad_down").reshape(nd, SHARD_FF, D_MODEL)]
    (dob, dat3, dlt3, dmpb, dy, dg_mix_post, dg_attn, dg_pool, dps), landed = _post_attn_bwd(
        dh1, o, a, mpre, gh, wpool, small["g_mix_post"], small["g_attn_grp"], small["g_pool_grp"], small["pool_scale"],
        send_rest)
    send_rest = [_wgrad(mix, dob, BF16, "wgrad_out").reshape(nd, SHARD_SQ, D_MODEL)] + send_rest
    pair_rest = _rs_pair_sum(core, send_rest, [0, OFF_PG, OFF_PLE, OFF_GATE, OFF_UP, OFF_DOWN], ROWS_REST,
                             "rs_pair_sum_rest", landed)

    dwp = _wgrad(y, dmpb, F32, "wgrad_pool")
    dw_pool = jnp.stack([dwp[g * POOL_CH:(g + 1) * POOL_CH, g * POOL_CH:(g + 1) * POOL_CH] for g in range(4)])
    small_part = _pack_small(dw_pool, jnp.zeros((1, D_MODEL), F32), dg_mix_post, dg_ffn_pre, dg_ffn_post, dg_ple,
                             dg_attn, dg_pool, dps, jnp.zeros((1, HEADS), F32), loss8[0:1, 0:1])
    dqt3, dkt3, dvt3, chips_rest, small_all = _attn_bwd(ka, v, kt3, qat3, qt3, dat3, lset3, dlt3, pair_rest, small_part)

    gx, dz, dg_mix_pre, db = _pre_attn_bwd(dqt3, dkt3, dvt3, fl, dy, x, dh1, small["g_mix_pre"], wqkv, wf, wu)

    pair_in = _rs_pair_sum(core, [_wgrad_in(dz, hn)], [0], ROWS_IN, "rs_pair_sum_in")

    small_late = _pack_small_late(dg_mix_pre, db[:, 0:HEADS])
    *upd_rest, chips_in, late_all = _reduce_update_rest(chips_rest, rest_w, rest_m, rest_v, pair_in, small_late)
    upd_in = _reduce_update_big(chips_in, in_w, in_m, in_v, ROWS_IN, "reduce_update_in")
    return gx, (small_all, late_all), upd_in, upd_rest


def kernel(x, p, g_mix_pre, w_in, b_forget, g_attn_grp, g_pool_grp, w_pool, pool_scale, w_out, g_mix_post, g_ffn_pre, w_ffn_gate, w_ffn_up, w_ffn_down, g_ffn_post, w_ple_proj, g_ple, w_ple_gate, loss_target, m_g_mix_pre, m_w_in, m_b_forget, m_g_attn_grp, m_g_pool_grp, m_w_pool, m_pool_scale, m_w_out, m_g_mix_post, m_g_ffn_pre, m_w_ffn_gate, m_w_ffn_up, m_w_ffn_down, m_g_ffn_post, m_w_ple_proj, m_g_ple, m_w_ple_gate, v_g_mix_pre, v_w_in, v_b_forget, v_g_attn_grp, v_g_pool_grp, v_w_pool, v_pool_scale, v_w_out, v_g_mix_post, v_g_ffn_pre, v_w_ffn_gate, v_w_ffn_up, v_w_ffn_down, v_g_ffn_post, v_w_ple_proj, v_g_ple, v_w_ple_gate):
    small = dict(w_pool=w_pool[0], g_mix_pre=g_mix_pre, g_mix_post=g_mix_post, g_ffn_pre=g_ffn_pre,
                 g_ffn_post=g_ffn_post, g_ple=g_ple, g_attn_grp=g_attn_grp, g_pool_grp=g_pool_grp,
                 pool_scale=pool_scale, b_forget=b_forget)
    gx, small_all, upd_in, upd_rest = _step(
        x[0], p[0, 0], loss_target[0], small, _pack_in(w_in), _pack_in(m_w_in), _pack_in(v_w_in),
        _pack_rest(w_out, w_ffn_gate, w_ffn_up, w_ffn_down, w_ple_proj, w_ple_gate),
        _pack_rest(m_w_out, m_w_ffn_gate, m_w_ffn_up, m_w_ffn_down, m_w_ple_proj, m_w_ple_gate),
        _pack_rest(v_w_out, v_w_ffn_gate, v_w_ffn_up, v_w_ffn_down, v_w_ple_proj, v_w_ple_gate))

    sm_w = _pack_small(w_pool, g_mix_pre, g_mix_post, g_ffn_pre, g_ffn_post, g_ple, g_attn_grp, g_pool_grp, pool_scale, b_forget)
    sm_m = _pack_small(m_w_pool, m_g_mix_pre, m_g_mix_post, m_g_ffn_pre, m_g_ffn_post, m_g_ple, m_g_attn_grp, m_g_pool_grp, m_pool_scale, m_b_forget)
    sm_v = _pack_small(v_w_pool, v_g_mix_pre, v_g_mix_post, v_g_ffn_pre, v_g_ffn_post, v_g_ple, v_g_attn_grp, v_g_pool_grp, v_pool_scale, v_b_forget)
    upd_small = _reduce_update_small(*small_all, sm_w, sm_m, sm_v)
    loss = upd_small[0][ROW_MISC, COL_LOSS]

    def leaves(k):
        b_out, b_gate, b_up, b_down, b_ple, b_pg = _unpack_rest(upd_rest[k])
        s = _unpack_small(upd_small[k])
        return (s["g_mix_pre"], _unpack_in(upd_in[k]), s["b_forget"], s["g_attn_grp"], s["g_pool_grp"], s["w_pool"],
                s["pool_scale"], b_out, s["g_mix_post"], s["g_ffn_pre"], b_gate, b_up, b_down, s["g_ffn_post"], b_ple,
                s["g_ple"], b_pg)

    return (loss, gx[None], *leaves(0), *leaves(1), *leaves(2), *leaves(3))
```

```python
import functools

import jax
import jax.numpy as jnp
from jax import lax
from jax.experimental import pallas as pl
from jax.experimental.pallas import tpu as pltpu

F32 = jnp.float32
BF16 = jnp.bfloat16
HIGHEST = lax.Precision.HIGHEST

D_MODEL = 1024
HEADS = 8
HEAD_DIM = 64
D_ATTN = HEADS * HEAD_DIM
POOL_WINDOWS = (2, 4, 8, 16)
POOL_CH = 128
D_POOL = POOL_CH * len(POOL_WINDOWS)
D_FF = 2816
D_PLE = 256
D_IN = 3 * D_ATTN + HEADS + D_POOL
RMS_EPS = 1e-6
N_DEV = 8

ADAM_LR = 0.001
ADAM_B1 = 0.9
ADAM_B2 = 0.999
ADAM_EPS = 1e-08
ADAM_WD = 0.01
ADAM_STEP = 10

LANES = 128
HALO = 16
TS = 512
TS_FF = 256
FF_ROW_PARTS = 1
TS_WGRAD = 1024
TM_WGRAD = 2176
TQ = 256
TN_FF = D_FF
NEG = -1e30
VMEM_LIMIT = 56 * 1024 * 1024
VMEM_LIMIT_FFN_FWD = 62 * 1024 * 1024

SHARD_IN = 257
ROWS_IN = 272
SHARD_FF = 352
SHARD_SQ = D_MODEL // N_DEV
ROWS_PLE = D_PLE * SHARD_SQ // D_MODEL
OFF_PG = SHARD_SQ
OFF_PLE = 2 * SHARD_SQ
OFF_GATE = SHARD_FF
OFF_UP = 2 * SHARD_FF
OFF_DOWN = 3 * SHARD_FF
ROWS_REST = 4 * SHARD_FF
TR_REST = SHARD_FF

SMALL_ROWS = 72
ROW_G_MIX_PRE, ROW_G_MIX_POST, ROW_G_FFN_PRE, ROW_G_FFN_POST, ROW_G_PLE = 64, 65, 66, 67, 68
ROW_GROUP_GAINS, ROW_MISC = 69, 70
COL_B_FORGET = D_POOL
COL_LOSS = D_POOL + HEADS


def _nn(a, b):
    return jnp.dot(a, b, preferred_element_type=F32)


def _nt(a, b):
    return lax.dot_general(a, b, (((1,), (1,)), ((), ())), preferred_element_type=F32)


def _tn(a, b):
    return lax.dot_general(a, b, (((0,), (0,)), ((), ())), preferred_element_type=F32)


def _rstd(v):
    return lax.rsqrt(jnp.mean(v * v, axis=-1, keepdims=True) + RMS_EPS)


def _rms_bwd(v, g, dy):
    r = _rstd(v)
    vh = v * r
    t = dy * g
    dv = r * (t - vh * jnp.mean(t * vh, axis=-1, keepdims=True))
    return dv, jnp.sum(dy * vh, axis=0, keepdims=True)


def _split3(v):
    hi = v.astype(BF16)
    rest = v - hi.astype(F32)
    mid = rest.astype(BF16)
    return hi, mid, (rest - mid.astype(F32)).astype(BF16)


def _mask_matmul(mask, v):
    hi, mid, lo = _split3(v)
    return _nn(mask, lo) + _nn(mask, mid) + _nn(mask, hi)


def _running_sum(v, reverse=False):
    tq = v.shape[0] // 2
    rr = lax.broadcasted_iota(jnp.int32, (tq, tq), 0)
    cc = lax.broadcasted_iota(jnp.int32, (tq, tq), 1)
    mask = ((cc >= rr) if reverse else (cc <= rr)).astype(BF16)
    top, bot = _mask_matmul(mask, v[0:tq]), _mask_matmul(mask, v[tq:])
    if reverse:
        top = top + bot[0:1, :]
    else:
        bot = bot + top[tq - 1:tq, :]
    return jnp.concatenate([top, bot], axis=0)


def _params(n_grid):
    return pltpu.CompilerParams(dimension_semantics=("arbitrary",) * n_grid, vmem_limit_bytes=VMEM_LIMIT)


def _row(i):
    return (i, 0)


def _fixed(*_):
    return (0, 0)


def _spec_square(part):
    return pl.BlockSpec((N_DEV, SHARD_SQ, D_MODEL), lambda *_: (0, part, 0))


def _spec_ff(part):
    return pl.BlockSpec((TN_FF // SHARD_FF, SHARD_FF, D_MODEL), lambda i, j: (j, part, 0), pipeline_mode=pl.Buffered(1))


assert TS == 2 * TQ and TN_FF % SHARD_FF == 0
_HALVES = (slice(0, TQ), slice(TQ, TS))

VMEM_WHOLE = pl.BlockSpec(memory_space=pltpu.VMEM)
SMEM_WHOLE = pl.BlockSpec(memory_space=pltpu.SMEM)
ANY = pl.BlockSpec(memory_space=pl.ANY)


LOG2E = 1.4426950408889634
VROWS = HEAD_DIM + 16
AUG = 128
BIAS_LANE = HEAD_DIM
ONE_LANE = HEAD_DIM + 3
SPARE_LANE = HEADS
PART_LANES = 16
assert SPARE_LANE < PART_LANES and 3 * PART_LANES <= LANES


def _attn_layout_constants():
    import numpy as np
    bias_k = np.zeros((LANES, HEADS * AUG), np.float32)
    bias_q = np.zeros((LANES, HEADS * AUG), np.float32)
    for h in range(HEADS):
        for part in range(3):
            bias_k[part * PART_LANES + h, h * AUG + BIAS_LANE + part] = -1.0
            bias_q[part * PART_LANES + h, h * AUG + ONE_LANE + part] = 1.0
            bias_k[SPARE_LANE, h * AUG + ONE_LANE + part] = 1.0
            bias_q[SPARE_LANE, h * AUG + BIAS_LANE + part] = 1.0
    after = np.concatenate([np.arange(h * AUG + HEAD_DIM, (h + 1) * AUG) for h in range(HEADS)])
    as_bf = lambda a: jnp.asarray(a, BF16)
    return dict(bias_k=as_bf(bias_k[:, after]), bias_q_t=as_bf(bias_q[:, after].T))


def _pre_attn_fwd(x, g1, wqkv, wf, wu, bpad, wpool, lay, own_block):
    s, d = x.shape
    nt = s // TS
    sub = TS // TQ

    def body(x_ref, g_ref, wqkv_ref, wf_ref, wu_ref, b_ref, wp_ref, bk_ref, bqt_ref, own_ref,
             hn_ref, qt_ref, ka_ref, v_ref, qat_ref, vt_ref, kt_ref, fl_ref, y_ref, mp_ref, all_ref,
             ubuf, ccar, cbuf, stage, send_sems, recv_sems, local_sem):
        i = pl.program_id(0)

        @pl.when(i == 0)
        def _():
            _gather_start(own_ref, all_ref, stage, send_sems, recv_sems, local_sem)
            ubuf[0:HALO, :] = jnp.zeros((HALO, D_POOL), F32)
            ccar[...] = jnp.zeros_like(ccar)

        @pl.when(i == max(nt - 2, 0))
        def _():
            _gather_pass_on(all_ref, send_sems, recv_sems)

        xv = x_ref[...]
        hn = (xv * _rstd(xv) * g_ref[...]).astype(BF16)
        hn_ref[...] = hn
        zq = _nt(hn, wqkv_ref[...])
        qt = (zq[:, 0:D_ATTN] * 0.125).astype(BF16).T
        qb = (zq[:, 0:D_ATTN] * (0.125 * LOG2E)).astype(BF16)
        kb = zq[:, D_ATTN:2 * D_ATTN].astype(BF16)
        vb = zq[:, 2 * D_ATTN:3 * D_ATTN].astype(BF16)
        v_ref[...] = vb

        fl = _nt(hn, wf_ref[...]) + b_ref[...]
        fl_ref[...] = fl
        logf = jax.nn.log_sigmoid(fl)
        c = _running_sum(logf) + ccar[...]
        cbuf[...] = c
        ccar[...] = cbuf[TS - 1:TS, :]
        hi, mid, lo = (part.astype(F32) for part in _split3(c * LOG2E))
        lane = lax.broadcasted_iota(jnp.int32, (TS, LANES), 1)
        later = jnp.where(lane < 2 * PART_LANES, pltpu.roll(mid, PART_LANES, 1), pltpu.roll(lo, 2 * PART_LANES, 1))
        parts = jnp.where(lane < PART_LANES, jnp.where(lane == SPARE_LANE, 1.0, hi), later).astype(BF16)
        extra = AUG - HEAD_DIM
        kbias = _nn(parts, bk_ref[...]).astype(BF16)
        for h in range(HEADS):
            ka_ref[:, h * AUG:h * AUG + HEAD_DIM] = kb[:, h * HEAD_DIM:(h + 1) * HEAD_DIM]
            ka_ref[:, h * AUG + HEAD_DIM:(h + 1) * AUG] = kbias[:, h * extra:(h + 1) * extra]
        qbt = qb.T
        qbias = _nt(bqt_ref[...], parts).astype(BF16)
        vt = vb.T
        kt = kb.T
        for a in range(sub):
            cols = slice(a * TQ, (a + 1) * TQ)
            for h in range(HEADS):
                qat_ref[a, h * AUG:h * AUG + HEAD_DIM, :] = qbt[h * HEAD_DIM:(h + 1) * HEAD_DIM, cols]
                qat_ref[a, h * AUG + HEAD_DIM:(h + 1) * AUG, :] = qbias[h * extra:(h + 1) * extra, cols]
            for ref, mat in ((qt_ref, qt), (kt_ref, kt), (vt_ref, vt)):
                for h in range(HEADS):
                    ref[a, h * VROWS:h * VROWS + HEAD_DIM, :] = mat[h * HEAD_DIM:(h + 1) * HEAD_DIM, cols]
                    ref[a, h * VROWS + HEAD_DIM:(h + 1) * VROWS, :] = jnp.ones((VROWS - HEAD_DIM, TQ), BF16)

        u = _nt(hn, wu_ref[...])
        ubuf[HALO:HALO + TS, :] = u
        t = i * TS + lax.broadcasted_iota(jnp.int32, (TS, 1), 0)
        for g, w in enumerate(POOL_WINDOWS):
            cols = slice(g * POOL_CH, (g + 1) * POOL_CH)
            sm = ubuf[:, cols]
            step = 1
            while step < w:
                sm = sm + pltpu.roll(sm, step, 0)
                step *= 2
            cnt = jnp.minimum(t + 1, w).astype(F32)
            yg = (sm[HALO:, :] / cnt - u[:, cols]).astype(BF16)
            y_ref[:, cols] = yg
            mp_ref[:, cols] = _nn(yg, wp_ref[g])
        ubuf[0:HALO, :] = u[TS - HALO:, :]

        @pl.when(i == nt - 1)
        def _():
            _gather_finish(own_ref, all_ref, send_sems, recv_sems)

    nq = s // TQ
    aug = HEADS * AUG
    outs = (
        jax.ShapeDtypeStruct((s, d), BF16), jax.ShapeDtypeStruct((nq, HEADS * VROWS, TQ), BF16),
        jax.ShapeDtypeStruct((s, aug), BF16), jax.ShapeDtypeStruct((s, D_ATTN), BF16),
        jax.ShapeDtypeStruct((nq, aug, TQ), BF16), jax.ShapeDtypeStruct((nq, HEADS * VROWS, TQ), BF16),
        jax.ShapeDtypeStruct((nq, HEADS * VROWS, TQ), BF16),
        jax.ShapeDtypeStruct((s, LANES), F32),
        jax.ShapeDtypeStruct((s, D_POOL), BF16), jax.ShapeDtypeStruct((s, D_POOL), F32),
        jax.ShapeDtypeStruct((N_DEV,) + own_block.shape, own_block.dtype),
    )
    fixed3 = lambda i: (0, 0, 0)
    tiles3 = lambda rows: pl.BlockSpec((sub, rows, TQ), lambda i: (i, 0, 0))
    return pl.pallas_call(
        body, grid=(nt,), out_shape=outs, name="pre_attn_fwd",
        in_specs=[pl.BlockSpec((TS, d), _row), pl.BlockSpec((1, d), _fixed),
                  pl.BlockSpec((3 * D_ATTN, d), _fixed), pl.BlockSpec(wf.shape, _fixed), pl.BlockSpec(wu.shape, _fixed),
                  pl.BlockSpec((1, LANES), _fixed), pl.BlockSpec(wpool.shape, fixed3),
                  pl.BlockSpec(lay["bias_k"].shape, _fixed), pl.BlockSpec(lay["bias_q_t"].shape, _fixed), ANY],
        out_specs=(pl.BlockSpec((TS, d), _row), tiles3(HEADS * VROWS),
                   pl.BlockSpec((TS, aug), _row), pl.BlockSpec((TS, D_ATTN), _row),
                   tiles3(aug), tiles3(HEADS * VROWS), tiles3(HEADS * VROWS),
                   pl.BlockSpec((TS, LANES), _row),
                   pl.BlockSpec((TS, D_POOL), _row), pl.BlockSpec((TS, D_POOL), _row), ANY),
        scratch_shapes=[pltpu.VMEM((TS + HALO, D_POOL), F32), pltpu.VMEM((1, LANES), F32), pltpu.VMEM((TS, LANES), F32),
                        pltpu.VMEM(own_block.shape, own_block.dtype),
                        pltpu.SemaphoreType.DMA((7,)), pltpu.SemaphoreType.DMA((7,)), pltpu.SemaphoreType.DMA],
        compiler_params=_params(1),
    )(x, g1, wqkv, wf, wu, bpad, wpool, lay["bias_k"], lay["bias_q_t"], own_block)


def _causal_in_tile():
    krow = lax.broadcasted_iota(jnp.int32, (TQ, TQ), 0)
    qcol = lax.broadcasted_iota(jnp.int32, (TQ, TQ), 1)
    return krow <= qcol


def _attn_fwd(ka, qat3, vt3, own_block):
    s = ka.shape[0]
    nq = s // TQ
    pass_on_step = max(nq - 2, 0)

    def body(qa_ref, ka_ref, vt_ref, own_ref, a_ref, lset_ref, all_ref, acc, out_t, st_scr, pt_scr,
             stage, send_sems, recv_sems, local_sem):
        i = pl.program_id(0)

        @pl.when(i == 0)
        def _():
            _gather_start(own_ref, all_ref, stage, send_sems, recv_sems, local_sem)

        @pl.when(i == pass_on_step)
        def _():
            _gather_pass_on(all_ref, send_sems, recv_sems)

        acc[...] = jnp.zeros_like(acc)

        def tile(j, stats, masked):
            tile_max = []
            for h in range(HEADS):
                aug = slice(h * AUG, (h + 1) * AUG)
                st = _nn(ka_ref[pl.ds(j * TQ, TQ), aug], qa_ref[0, aug, :])
                if masked:
                    st = jnp.where(_causal_in_tile(), st, NEG)
                st_scr[h] = st
                tile_max.append(jnp.max(st, axis=0, keepdims=True))
            new, scale = [], []
            for h in range(HEADS):
                m_new = jnp.maximum(stats[h], tile_max[h])
                scale.append(jnp.exp2(stats[h] - m_new))
                pt_scr[h] = jnp.exp2(st_scr[h] - m_new).astype(BF16)
                new.append(m_new)
            for h in range(HEADS):
                rows = slice(h * VROWS, (h + 1) * VROWS)
                acc[rows, :] = scale[h] * acc[rows, :] + _nn(vt_ref[j, rows, :], pt_scr[h])
            return tuple(new)

        init = tuple(jnp.full((1, TQ), NEG, F32) for _ in range(HEADS))
        stats = lax.fori_loop(0, i, functools.partial(tile, masked=False), init)
        stats = tile(i, stats, True)
        for h in range(HEADS):
            denom = acc[h * VROWS + HEAD_DIM:h * VROWS + HEAD_DIM + 1, :]
            out_t[h * HEAD_DIM:(h + 1) * HEAD_DIM, :] = acc[h * VROWS:h * VROWS + HEAD_DIM, :] / denom
            lset_ref[0, h:h + 1, :] = stats[h] + jnp.log2(denom)
        a_ref[...] = out_t[...].T

        @pl.when(i == nq - 1)
        def _():
            _gather_finish(own_ref, all_ref, send_sems, recv_sems)

    r, cdim = own_block.shape
    return pl.pallas_call(
        body, grid=(nq,), name="attn_fwd",
        out_shape=(jax.ShapeDtypeStruct((s, D_ATTN), F32), jax.ShapeDtypeStruct((nq, HEADS, TQ), F32),
                   jax.ShapeDtypeStruct((N_DEV, r, cdim), own_block.dtype)),
        in_specs=[pl.BlockSpec((1, HEADS * AUG, TQ), lambda i: (i, 0, 0)), VMEM_WHOLE, VMEM_WHOLE, ANY],
        out_specs=(pl.BlockSpec((TQ, D_ATTN), _row), pl.BlockSpec((1, HEADS, TQ), lambda i: (i, 0, 0)), ANY),
        scratch_shapes=[pltpu.VMEM((HEADS * VROWS, TQ), F32), pltpu.VMEM((D_ATTN, TQ), F32),
                        pltpu.VMEM((HEADS, TQ, TQ), F32), pltpu.VMEM((HEADS, TQ, TQ), BF16),
                        pltpu.VMEM((r, cdim), own_block.dtype),
                        pltpu.SemaphoreType.DMA((7,)), pltpu.SemaphoreType.DMA((7,)), pltpu.SemaphoreType.DMA],
        compiler_params=_params(1),
    )(qat3, ka, vt3, own_block)


def _post_attn_fwd(a, mpre, x, g_attn, g_pool, pscale, wout, g_post, g_ffn_pre):
    s, d = x.shape

    def body(a_ref, mp_ref, x_ref, ga_ref, gp_ref, ps_ref, wo_ref, gpost_ref, gpre_ref,
             mix_ref, o_ref, h1_ref, hn2_ref):
        for rows in _HALVES:
            av = a_ref[rows, :]
            mix_ref[rows, 0:D_ATTN] = (av * _rstd(av) * ga_ref[...]).astype(BF16)
            mv = mp_ref[rows, :] * ps_ref[...]
            mix_ref[rows, D_ATTN:] = (mv * _rstd(mv) * gp_ref[...]).astype(BF16)
            o = _nn(mix_ref[rows, :], wo_ref[...].reshape(d, d))
            o_ref[rows, :] = o
            h1 = x_ref[rows, :] + o * _rstd(o) * gpost_ref[...]
            h1_ref[rows, :] = h1
            hn2_ref[rows, :] = (h1 * _rstd(h1) * gpre_ref[...]).astype(BF16)

    vec = lambda n: pl.BlockSpec((1, n), _fixed)
    return pl.pallas_call(
        body, grid=(s // TS,), name="post_attn_fwd",
        out_shape=(jax.ShapeDtypeStruct((s, d), BF16), jax.ShapeDtypeStruct((s, d), F32),
                   jax.ShapeDtypeStruct((s, d), F32), jax.ShapeDtypeStruct((s, d), BF16)),
        in_specs=[pl.BlockSpec((TS, D_ATTN), _row), pl.BlockSpec((TS, D_POOL), _row), pl.BlockSpec((TS, d), _row),
                  vec(D_ATTN), vec(D_POOL), vec(D_POOL), _spec_square(0), vec(d), vec(d)],
        out_specs=(pl.BlockSpec((TS, d), _row),) * 4,
        compiler_params=_params(1),
    )(a, mpre, x, g_attn, g_pool, pscale, wout, g_post, g_ffn_pre)


def _ffn_fwd(hn2, wg, wu, wd, h1, g_post):
    s, d = h1.shape
    nc = D_FF // TN_FF
    ts = min(2 * TS_FF, s)

    def body(hn_ref, wg_ref, wu_ref, wd_ref, h1_ref, g_ref, gate_ref, up_ref, act_ref, ff_ref, h2_ref, acc):
        j = pl.program_id(1)

        @pl.when(j == 0)
        def _():
            acc[...] = jnp.zeros_like(acc)

        for r in range(ts // TS_FF):
            rows = slice(r * TS_FF, (r + 1) * TS_FF)
            hn = hn_ref[rows, :]
            gt = _nt(hn, wg_ref[...].reshape(TN_FF, d))
            up = _nt(hn, wu_ref[...].reshape(TN_FF, d))
            gate_ref[rows, :] = gt.astype(BF16)
            up_ref[rows, :] = up.astype(BF16)
            act_ref[rows, :] = (gt * jax.nn.sigmoid(gt) * up).astype(BF16)
            acc[rows, :] += _nn(act_ref[rows, :], wd_ref[...].reshape(TN_FF, d))

        @pl.when(j == nc - 1)
        def _():
            ff = acc[...]
            ff_ref[...] = ff
            h2_ref[...] = h1_ref[...] + ff * _rstd(ff) * g_ref[...]

    rowblk = pl.BlockSpec((ts, d), lambda i, j: (i, 0))
    chunk = pl.BlockSpec((ts, TN_FF), lambda i, j: (i, j))
    return pl.pallas_call(
        body, grid=(s // ts, nc), name="ffn_fwd",
        out_shape=(jax.ShapeDtypeStruct((s, D_FF), BF16),) * 3 + (jax.ShapeDtypeStruct((s, d), F32),) * 2,
        in_specs=[rowblk, _spec_ff(0), _spec_ff(1), _spec_ff(2), rowblk, pl.BlockSpec((1, d), lambda i, j: (0, 0))],
        out_specs=(chunk, chunk, chunk, rowblk, rowblk),
        scratch_shapes=[pltpu.VMEM((ts, d), F32)],
        compiler_params=pltpu.CompilerParams(dimension_semantics=("arbitrary",) * 2, vmem_limit_bytes=VMEM_LIMIT_FFN_FWD),
    )(hn2, wg, wu, wd, h1, g_post)


def _tail_fwd_bwd(h2, p, tgt, ff, wple, wpg, g_ple, g_ffn_post):
    s, d = h2.shape

    def body(h2_ref, p_ref, t_ref, ff_ref, wple_ref, wpg_ref, gple_ref, gfp_ref,
             dh2_ref, dff_ref, dgl_ref, dpp_ref, h2b_ref, pb_ref, loss_ref, dgple_ref, dgfp_ref):
        i = pl.program_id(0)

        @pl.when(i == 0)
        def _():
            loss_ref[...] = jnp.zeros_like(loss_ref)
            dgple_ref[...] = jnp.zeros_like(dgple_ref)
            dgfp_ref[...] = jnp.zeros_like(dgfp_ref)

        h2 = h2_ref[...]
        h2b = h2.astype(BF16)
        h2b_ref[...] = h2b
        pb = p_ref[...].astype(BF16)
        pb_ref[...] = pb
        pp = _nt(pb, wple_ref[...])
        gple = gple_ref[...]
        e = pp * _rstd(pp) * gple
        wpg = wpg_ref[...].reshape(d, d)
        sg = jax.nn.sigmoid(_nn(h2b, wpg))
        diff = h2 + sg * e - t_ref[...]
        sq = jnp.sum(jnp.sum(diff * diff, axis=1, keepdims=True), axis=0, keepdims=True)
        loss_ref[...] += jnp.broadcast_to(sq * (0.5 / d), loss_ref.shape)
        dh3 = diff * (1.0 / d)
        dgl = (dh3 * e * sg * (1.0 - sg)).astype(BF16)
        dgl_ref[...] = dgl
        dh2 = dh3 + _nt(dgl, wpg)
        dh2_ref[...] = dh2
        dpp, dg = _rms_bwd(pp, gple, dh3 * sg)
        dpp_ref[...] = dpp.astype(BF16)
        dgple_ref[...] += dg
        dff, dg = _rms_bwd(ff_ref[...], gfp_ref[...], dh2)
        dff_ref[...] = dff.astype(BF16)
        dgfp_ref[...] += dg

    rowblk = pl.BlockSpec((TS, d), _row)
    vec = pl.BlockSpec((1, d), _fixed)
    return pl.pallas_call(
        body, grid=(s // TS,), name="tail_fwd_bwd",
        out_shape=(jax.ShapeDtypeStruct((s, d), F32), jax.ShapeDtypeStruct((s, d), BF16),
                   jax.ShapeDtypeStruct((s, d), BF16), jax.ShapeDtypeStruct((s, d), BF16),
                   jax.ShapeDtypeStruct((s, d), BF16), jax.ShapeDtypeStruct((s, D_PLE), BF16),
                   jax.ShapeDtypeStruct((8, LANES), F32), jax.ShapeDtypeStruct((1, d), F32),
                   jax.ShapeDtypeStruct((1, d), F32)),
        in_specs=[rowblk, pl.BlockSpec((TS, D_PLE), _row), rowblk, rowblk,
                  pl.BlockSpec(wple.shape, _fixed), _spec_square(1), vec, vec],
        out_specs=(rowblk, rowblk, rowblk, rowblk, rowblk, pl.BlockSpec((TS, D_PLE), _row),
                   pl.BlockSpec((8, LANES), _fixed), vec, vec),
        compiler_params=_params(1),
    )(h2, p, tgt, ff, wple, wpg, g_ple, g_ffn_post)


def _ffn_bwd(dff, gate, up, wd, wg, wu, h1, dh2, g_pre):
    s, d = h1.shape
    nc = D_FF // TN_FF
    ts = min(TS_FF, s)

    def body(dff_ref, gate_ref, up_ref, wd_ref, wg_ref, wu_ref, h1_ref, dh2_ref, g_ref,
             dgate_ref, dup_ref, dh1_ref, dg_ref, acc):
        i = pl.program_id(0)
        j = pl.program_id(1)

        @pl.when((i == 0) & (j == 0))
        def _():
            dg_ref[...] = jnp.zeros_like(dg_ref)

        @pl.when(j == 0)
        def _():
            acc[...] = jnp.zeros_like(acc)

        for r in range(FF_ROW_PARTS):
            rows = slice(r * (ts // FF_ROW_PARTS), (r + 1) * (ts // FF_ROW_PARTS))
            dact = _nt(dff_ref[rows, :], wd_ref[...].reshape(TN_FF, d))
            gt = gate_ref[rows, :].astype(F32)
            sg = jax.nn.sigmoid(gt)
            dup_ref[rows, :] = (dact * gt * sg).astype(BF16)
            dgate_ref[rows, :] = (dact * up_ref[rows, :].astype(F32) * (sg * (1.0 + gt * (1.0 - sg)))).astype(BF16)
            acc[rows, :] += (_nn(dgate_ref[rows, :], wg_ref[...].reshape(TN_FF, d))
                             + _nn(dup_ref[rows, :], wu_ref[...].reshape(TN_FF, d)))

        @pl.when(j == nc - 1)
        def _():
            dv, dg = _rms_bwd(h1_ref[...], g_ref[...], acc[...])
            dh1_ref[...] = dh2_ref[...] + dv
            dg_ref[...] += dg

    rowblk = pl.BlockSpec((ts, d), lambda i, j: (i, 0))
    chunk = pl.BlockSpec((ts, TN_FF), lambda i, j: (i, j))
    vec = pl.BlockSpec((1, d), lambda i, j: (0, 0))
    return pl.pallas_call(
        body, grid=(s // ts, nc), name="ffn_bwd",
        out_shape=(jax.ShapeDtypeStruct((s, D_FF), BF16), jax.ShapeDtypeStruct((s, D_FF), BF16),
                   jax.ShapeDtypeStruct((s, d), F32), jax.ShapeDtypeStruct((1, d), F32)),
        in_specs=[rowblk, chunk, chunk, _spec_ff(2), _spec_ff(0), _spec_ff(1), rowblk, rowblk, vec],
        out_specs=(chunk, chunk, rowblk, vec),
        scratch_shapes=[pltpu.VMEM((ts, d), F32)],
        compiler_params=_params(2),
    )(dff, gate, up, wd, wg, wu, h1, dh2, g_pre)


def _post_attn_bwd(dh1, o, a, mpre, wout, wpool, g_post, g_attn, g_pool, pscale, send):
    s, d = dh1.shape
    sub = TS // TQ
    npc = len(send)

    def body(dh1_ref, o_ref, a_ref, mp_ref, wo_ref, wp_ref, gpost_ref, ga_ref, gp_ref, ps_ref, *refs):
        send_refs, refs = refs[:npc], refs[npc:]
        dob_ref, dat_ref, dlt_ref, dmpb_ref, dy_ref, dgpost_ref, dga_ref, dgp_ref, dps_ref = refs[:9]
        got_refs, (send_sems, recv_sems) = refs[9:9 + npc], refs[9 + npc:]
        i = pl.program_id(0)

        @pl.when(i == 0)
        def _():
            for cp in _pair_copies(send_refs, got_refs, send_sems, recv_sems):
                cp.start()
            dgpost_ref[...] = jnp.zeros_like(dgpost_ref)
            dga_ref[...] = jnp.zeros_like(dga_ref)
            dgp_ref[...] = jnp.zeros_like(dgp_ref)
            dps_ref[...] = jnp.zeros_like(dps_ref)

        do, dg = _rms_bwd(o_ref[...], gpost_ref[...], dh1_ref[...])
        dgpost_ref[...] += dg
        dob = do.astype(BF16)
        dob_ref[...] = dob
        dmix = _nt(dob, wo_ref[...].reshape(d, d))

        av = a_ref[...]
        da, dg = _rms_bwd(av, ga_ref[...], dmix[:, 0:D_ATTN])
        dga_ref[...] += dg
        dat = da.astype(BF16).T
        hsel = (lax.shift_right_logical(lax.broadcasted_iota(jnp.int32, (HEADS, D_ATTN), 1), 6)
                == lax.broadcasted_iota(jnp.int32, (HEADS, D_ATTN), 0)).astype(F32)
        dlt = lax.dot_general(hsel, da * av, (((1,), (1,)), ((), ())), precision=HIGHEST, preferred_element_type=F32)
        for q in range(sub):
            dlt_ref[q] = dlt[:, q * TQ:(q + 1) * TQ]
            dat_ref[q] = dat[:, q * TQ:(q + 1) * TQ]

        ps = ps_ref[...]
        mp = mp_ref[...]
        dm, dg = _rms_bwd(mp * ps, gp_ref[...], dmix[:, D_ATTN:])
        dgp_ref[...] += dg
        dps_ref[...] += jnp.sum(dm * mp, axis=0, keepdims=True)
        dmpb = (dm * ps).astype(BF16)
        dmpb_ref[...] = dmpb
        for g in range(len(POOL_WINDOWS)):
            cols = slice(g * POOL_CH, (g + 1) * POOL_CH)
            dy_ref[:, cols] = _nt(dmpb[:, cols], wp_ref[g])

        @pl.when(i == s // TS - 1)
        def _():
            for cp in _pair_copies(send_refs, got_refs, send_sems, recv_sems):
                cp.wait()

    rowblk = pl.BlockSpec((TS, d), _row)
    half = pl.BlockSpec((TS, D_ATTN), _row)
    vec = lambda n: pl.BlockSpec((1, n), _fixed)
    nk = N_DEV // 2
    res = pl.pallas_call(
        body, grid=(s // TS,), name="post_attn_bwd",
        out_shape=(jax.ShapeDtypeStruct((s, d), BF16), jax.ShapeDtypeStruct((s // TQ, D_ATTN, TQ), BF16),
                   jax.ShapeDtypeStruct((s // TQ, HEADS, TQ), F32), jax.ShapeDtypeStruct((s, D_POOL), BF16),
                   jax.ShapeDtypeStruct((s, D_POOL), F32), jax.ShapeDtypeStruct((1, d), F32),
                   jax.ShapeDtypeStruct((1, D_ATTN), F32), jax.ShapeDtypeStruct((1, D_POOL), F32),
                   jax.ShapeDtypeStruct((1, D_POOL), F32))
        + tuple(jax.ShapeDtypeStruct((nk,) + t.shape[1:], t.dtype) for t in send),
        in_specs=[rowblk, rowblk, half, half, _spec_square(0),
                  pl.BlockSpec(wpool.shape, lambda i: (0, 0, 0)), vec(d), vec(D_ATTN), vec(D_POOL), vec(D_POOL)]
        + [ANY] * npc,
        out_specs=(rowblk, pl.BlockSpec((sub, D_ATTN, TQ), lambda i: (i, 0, 0)),
                   pl.BlockSpec((sub, HEADS, TQ), lambda i: (i, 0, 0)), half, half,
                   vec(d), vec(D_ATTN), vec(D_POOL), vec(D_POOL)) + (ANY,) * npc,
        scratch_shapes=[pltpu.SemaphoreType.DMA((nk, npc)), pltpu.SemaphoreType.DMA((nk, npc))],
        compiler_params=_params(1),
    )(dh1, o, a, mpre, wout, wpool, g_post, g_attn, g_pool, pscale, *send)
    return res[:9], list(res[9:])


def _attn_bwd(ka, v, kt3, qat3, qt3, dot3, lset3, dlt3, chip_blocks, small_block):
    s = ka.shape[0]
    nq = s // TQ

    def body(ka_ref, v_ref, kt_ref, qat_ref, qt_ref, dot_ref, lset_ref, dlt_ref, b_ref, sm_ref,
             dqt_ref, dkt_ref, dvt_ref, got_ref, all_ref, pt_scr, ptb_scr, dsb_scr,
             stage, send_sems, recv_sems, local_sem, stage_s, send_s, recv_s, local_s):
        j = pl.program_id(0)

        @pl.when(j == 0)
        def _():
            _chips_start(b_ref, got_ref, stage, send_sems, recv_sems, local_sem)
            _gather_start(sm_ref, all_ref, stage_s, send_s, recv_s, local_s)
            dqt_ref[...] = jnp.zeros_like(dqt_ref)

        @pl.when(j == max(nq - 2, 0))
        def _():
            _gather_pass_on(all_ref, send_s, recv_s)

        def tile(i, masked):
            def accumulate(ref, idx, val):
                if masked:
                    ref[idx] = val
                else:
                    ref[idx] += val

            for h in range(HEADS):
                aug = slice(h * AUG, (h + 1) * AUG)
                st = _nn(ka_ref[:, aug], qat_ref[i, aug, :]) - lset_ref[i, h:h + 1, :]
                if masked:
                    st = jnp.where(_causal_in_tile(), st, NEG)
                pt = jnp.exp2(st)
                pt_scr[h] = pt
                ptb_scr[h] = pt.astype(BF16)
            heads = [(h, slice(h * HEAD_DIM, (h + 1) * HEAD_DIM)) for h in range(HEADS)]
            for h, hs in heads:
                dst = pt_scr[h] * (_nn(v_ref[:, hs], dot_ref[i, hs, :]) - dlt_ref[i, h:h + 1, :])
                dsb_scr[h] = dst.astype(BF16)
            for h, hs in heads:
                accumulate(dvt_ref, (0, hs, slice(None)), _nt(dot_ref[i, hs, :], ptb_scr[h]))
            for h, hs in heads:
                rows = slice(h * VROWS, (h + 1) * VROWS)
                accumulate(dkt_ref, (0, rows, slice(None)), _nt(qt_ref[i, rows, :], dsb_scr[h]))
            for h, hs in heads:
                rows = slice(h * VROWS, (h + 1) * VROWS)
                dqt_ref[i, rows, :] += _nn(kt_ref[0, rows, :], dsb_scr[h])

        first = j + 1
        pairs = (nq - first) // 2

        def step(p, carry):
            tile(first + 2 * p, False)
            tile(first + 2 * p + 1, False)
            return carry

        tile(j, True)
        lax.fori_loop(0, pairs, step, 0)

        @pl.when(first + 2 * pairs < nq)
        def _():
            tile(nq - 1, False)

        @pl.when(j == nq - 1)
        def _():
            _chips_finish(b_ref, got_ref, send_sems, recv_sems)
            _gather_finish(sm_ref, all_ref, send_s, recv_s)

    blk = pl.BlockSpec((TQ, D_ATTN), _row)
    tile_t = lambda rows: pl.BlockSpec((1, rows, TQ), lambda j: (j, 0, 0))
    per_tile = lambda rows: jax.ShapeDtypeStruct((nq, rows, TQ), F32)
    _, r, cdim = chip_blocks.shape
    dma = pltpu.SemaphoreType.DMA
    return pl.pallas_call(
        body, grid=(nq,), name="attn_bwd",
        out_shape=(per_tile(HEADS * VROWS), per_tile(HEADS * VROWS), per_tile(D_ATTN),
                   jax.ShapeDtypeStruct(chip_blocks.shape, chip_blocks.dtype),
                   jax.ShapeDtypeStruct((N_DEV,) + small_block.shape, small_block.dtype)),
        in_specs=[pl.BlockSpec((TQ, HEADS * AUG), _row), blk, tile_t(HEADS * VROWS),
                  VMEM_WHOLE, VMEM_WHOLE, VMEM_WHOLE, VMEM_WHOLE, VMEM_WHOLE, ANY, ANY],
        out_specs=(pl.BlockSpec((nq, HEADS * VROWS, TQ), lambda j: (0, 0, 0)), tile_t(HEADS * VROWS), tile_t(D_ATTN),
                   ANY, ANY),
        scratch_shapes=[pltpu.VMEM((HEADS, TQ, TQ), F32), pltpu.VMEM((HEADS, TQ, TQ), BF16),
                        pltpu.VMEM((HEADS, TQ, TQ), BF16), pltpu.VMEM((r, cdim), chip_blocks.dtype),
                        dma((3,)), dma((3,)), dma,
                        pltpu.VMEM(small_block.shape, small_block.dtype), dma((7,)), dma((7,)), dma],
        compiler_params=_params(1),
    )(ka, v, kt3, qat3, qt3, dot3, lset3, dlt3, chip_blocks, small_block)


def _pre_attn_bwd(dqt3, dkt3, dvt3, fl, dy, x, dh1, g1, wqkv, wf, wu):
    s, d = x.shape
    nt = s // TS
    n = TS + HALO
    sub = TS // TQ
    qkv, fcols = 3 * D_ATTN, 3 * D_ATTN + LANES

    def body(dqt_ref, dkt_ref, dvt_ref, fl_ref, dy_ref, x_ref, dh1_ref, g_ref, wqkv_ref, wf_ref, wu_ref,
             gx_ref, dz_ref, dg_ref, db_ref, ybuf, ccar, dlog, dsum):
        dqkv_ref = dz_ref.at[:, 0:qkv]
        dfb_ref = dz_ref.at[:, qkv:fcols]
        dub_ref = dz_ref.at[:, fcols:]
        i = pl.program_id(0)
        ti = nt - 1 - i

        @pl.when(i == 0)
        def _():
            ybuf[TS:n, :] = jnp.zeros((HALO, D_POOL), F32)
            ccar[...] = jnp.zeros_like(ccar)
            dg_ref[...] = jnp.zeros_like(dg_ref)
            db_ref[...] = jnp.zeros_like(db_ref)
            dsum[...] = jnp.zeros_like(dsum)

        for a in range(sub):
            for h in range(HEADS):
                r = h * VROWS + HEAD_DIM
                dsum[h:h + 1, a * TQ:(a + 1) * TQ] = dqt_ref[a, r:r + 1, :] - dkt_ref[a, r:r + 1, :]
        dlog[...] = ccar[...] + _running_sum(dsum[...].T, reverse=True)
        ccar[...] = dlog[0:1, :]
        df = dlog[...] * jax.nn.sigmoid(-fl_ref[...])
        db_ref[...] += jnp.sum(df, axis=0, keepdims=True)
        dfb = df.astype(BF16)
        dfb_ref[...] = dfb

        t = ti * TS + lax.broadcasted_iota(jnp.int32, (TS, 1), 0)
        dy = dy_ref[...]
        for g, w in enumerate(POOL_WINDOWS):
            cols = slice(g * POOL_CH, (g + 1) * POOL_CH)
            ybuf[0:TS, cols] = dy[:, cols] / jnp.minimum(t + 1, w).astype(F32)
        for g, w in enumerate(POOL_WINDOWS):
            cols = slice(g * POOL_CH, (g + 1) * POOL_CH)
            sm = ybuf[:, cols]
            step = 1
            while step < w:
                sm = sm + pltpu.roll(sm, n - step, 0)
                step *= 2
            dub_ref[:, cols] = (sm[0:TS, :] - dy[:, cols]).astype(BF16)
        ybuf[TS:n, :] = ybuf[0:HALO, :]

        for a in range(sub):
            rows = slice(a * TQ, (a + 1) * TQ)
            for h in range(HEADS):
                src = slice(h * VROWS, h * VROWS + HEAD_DIM)
                dqkv_ref[rows, h * HEAD_DIM:(h + 1) * HEAD_DIM] = (dqt_ref[a, src, :].T * 0.125).astype(BF16)
                dqkv_ref[rows, D_ATTN + h * HEAD_DIM:D_ATTN + (h + 1) * HEAD_DIM] = dkt_ref[a, src, :].T.astype(BF16)
            dqkv_ref[rows, 2 * D_ATTN:] = dvt_ref[a].T.astype(BF16)
        dhn = _nn(dqkv_ref[...], wqkv_ref[...]) + _nn(dfb, wf_ref[...]) + _nn(dub_ref[...], wu_ref[...])
        dx, dg = _rms_bwd(x_ref[...], g_ref[...], dhn)
        gx_ref[...] = dh1_ref[...] + dx
        dg_ref[...] += dg

    rev = lambda i: (nt - 1 - i, 0)
    blk = lambda w: pl.BlockSpec((TS, w), rev)
    return pl.pallas_call(
        body, grid=(nt,), name="pre_attn_bwd",
        out_shape=(jax.ShapeDtypeStruct((s, d), F32), jax.ShapeDtypeStruct((s, fcols + D_POOL), BF16),
                   jax.ShapeDtypeStruct((1, d), F32), jax.ShapeDtypeStruct((1, LANES), F32)),
        in_specs=[pl.BlockSpec((sub, HEADS * VROWS, TQ), lambda i: (nt - 1 - i, 0, 0)),
                  pl.BlockSpec((sub, HEADS * VROWS, TQ), lambda i: (nt - 1 - i, 0, 0)),
                  pl.BlockSpec((sub, D_ATTN, TQ), lambda i: (nt - 1 - i, 0, 0)),
                  blk(LANES), blk(D_POOL), blk(d), blk(d),
                  pl.BlockSpec((1, d), _fixed), pl.BlockSpec((qkv, d), _fixed), pl.BlockSpec(wf.shape, _fixed),
                  pl.BlockSpec(wu.shape, _fixed)],
        out_specs=(blk(d), blk(fcols + D_POOL), pl.BlockSpec((1, d), _fixed), pl.BlockSpec((1, LANES), _fixed)),
        scratch_shapes=[pltpu.VMEM((n, D_POOL), F32), pltpu.VMEM((1, LANES), F32), pltpu.VMEM((TS, LANES), F32),
                        pltpu.VMEM((LANES, TS), F32)],
        compiler_params=_params(1),
    )(dqt3, dkt3, dvt3, fl, dy, x, dh1, g1, wqkv, wf, wu)


def _wgrad(a, b, out_dtype, name):
    s, m = a.shape
    n = b.shape[1]
    tm = max(t for t in range(LANES, min(m, TM_WGRAD) + 1, LANES) if m % t == 0)
    ts = min(TS_WGRAD, s)
    ns = s // ts

    def body(a_ref, b_ref, o_ref, acc):
        i = pl.program_id(1)

        @pl.when(i == 0)
        def _():
            acc[...] = jnp.zeros_like(acc)

        acc[...] += _tn(a_ref[...], b_ref[...])

        @pl.when(i == ns - 1)
        def _():
            o_ref[...] = acc[...].astype(out_dtype)

    return pl.pallas_call(
        body, grid=(m // tm, ns), name=name, out_shape=jax.ShapeDtypeStruct((m, n), out_dtype),
        in_specs=[pl.BlockSpec((ts, tm), lambda j, i: (i, j)), pl.BlockSpec((ts, n), lambda j, i: (i, 0))],
        out_specs=pl.BlockSpec((tm, n), lambda j, i: (j, 0)),
        scratch_shapes=[pltpu.VMEM((tm, n), F32)],
        compiler_params=_params(2),
    )(a, b)


def _wgrad_in(dz, hn):
    s, m = dz.shape
    n = hn.shape[1]
    ts = min(TS_WGRAD, s)
    ns = s // ts
    pad_at, pad = 3 * D_ATTN + HEADS, LANES - HEADS
    assert m == D_IN + pad and N_DEV * SHARD_IN == D_IN

    def pieces(d):
        lo, hi = d * SHARD_IN, (d + 1) * SHARD_IN
        spans = [(lo, min(hi, pad_at), 0), (max(lo, pad_at), hi, pad)]
        return [(a + shift, b - a, a - lo) for a, b, shift in spans if b > a]

    def body(a_ref, b_ref, o_ref, acc, stage):
        i = pl.program_id(0)

        @pl.when(i == 0)
        def _():
            acc[...] = jnp.zeros_like(acc)

        acc[...] += _tn(a_ref[...], b_ref[...])

        @pl.when(i == ns - 1)
        def _():
            stage[SHARD_IN:ROWS_IN, :] = jnp.zeros((ROWS_IN - SHARD_IN, n), F32)
            for d in range(N_DEV):
                for src, rows, dst in pieces(d):
                    stage[dst:dst + rows, :] = acc[src:src + rows, :]
                o_ref[d] = stage[...].astype(BF16)

    return pl.pallas_call(
        body, grid=(ns,), name="wgrad_in", out_shape=jax.ShapeDtypeStruct((N_DEV, ROWS_IN, n), BF16),
        in_specs=[pl.BlockSpec((ts, m), _row), pl.BlockSpec((ts, n), _row)],
        out_specs=pl.BlockSpec((N_DEV, ROWS_IN, n), lambda i: (0, 0, 0)),
        scratch_shapes=[pltpu.VMEM((m, n), F32), pltpu.VMEM((ROWS_IN, n), F32)],
        compiler_params=_params(1),
    )(dz, hn)


def _adamw(w, g, m, v):
    m = ADAM_B1 * m + (1.0 - ADAM_B1) * g
    v = ADAM_B2 * v + (1.0 - ADAM_B2) * (g * g)
    m_hat = m / (1.0 - ADAM_B1 ** ADAM_STEP)
    v_hat = v / (1.0 - ADAM_B2 ** ADAM_STEP)
    delta = -ADAM_LR * (m_hat / (jnp.sqrt(v_hat) + ADAM_EPS) + ADAM_WD * w)
    return delta, m, v


def _sum_update(p_ref, w_ref, m_ref, v_ref, g_ref, d_ref, nm_ref, nv_ref):
    g = p_ref[0].astype(F32)
    for k in range(1, p_ref.shape[0]):
        g = g + p_ref[k].astype(F32)
    g_ref[...] = g
    d_ref[...], nm_ref[...], nv_ref[...] = _adamw(w_ref[...], g, m_ref[...], v_ref[...])


def _reduce_update_rest(parts, w, m, v, chip_blocks, small_block):
    nk, r, c = parts.shape
    ns = r // TR_REST

    def body(p_ref, w_ref, m_ref, v_ref, b_ref, sm_ref, g_ref, d_ref, nm_ref, nv_ref, got_ref, all_ref,
             stage_b, stage_s, send_b, recv_b, local_b, send_s, recv_s, local_s):
        i = pl.program_id(0)

        @pl.when(i == 0)
        def _():
            _chips_start(b_ref, got_ref, stage_b, send_b, recv_b, local_b)
            _gather_start(sm_ref, all_ref, stage_s, send_s, recv_s, local_s)

        _sum_update(p_ref, w_ref, m_ref, v_ref, g_ref, d_ref, nm_ref, nv_ref)

        @pl.when(i == ns - 1)
        def _():
            _gather_pass_on(all_ref, send_s, recv_s)
            _chips_finish(b_ref, got_ref, send_b, recv_b)
            _gather_finish(sm_ref, all_ref, send_s, recv_s)

    blk = pl.BlockSpec((TR_REST, c), _row)
    out = jax.ShapeDtypeStruct((r, c), F32)
    dma = pltpu.SemaphoreType.DMA
    return pl.pallas_call(
        body, grid=(ns,), name="reduce_update_rest",
        out_shape=(out,) * 4 + (jax.ShapeDtypeStruct(chip_blocks.shape, chip_blocks.dtype),
                                jax.ShapeDtypeStruct((N_DEV,) + small_block.shape, small_block.dtype)),
        in_specs=[pl.BlockSpec((nk, TR_REST, c), lambda i: (0, i, 0)), blk, blk, blk, ANY, ANY],
        out_specs=(blk,) * 4 + (ANY, ANY),
        scratch_shapes=[pltpu.VMEM(chip_blocks.shape[1:], chip_blocks.dtype), pltpu.VMEM(small_block.shape, small_block.dtype),
                        dma((3,)), dma((3,)), dma, dma((7,)), dma((7,)), dma],
        compiler_params=_params(1),
    )(parts, w, m, v, chip_blocks, small_block)


def _reduce_update_big(parts, w, m, v, tr, name):
    nk, r, c = parts.shape

    def body(p_ref, w_ref, m_ref, v_ref, g_ref, d_ref, nm_ref, nv_ref):
        _sum_update(p_ref, w_ref, m_ref, v_ref, g_ref, d_ref, nm_ref, nv_ref)

    blk = pl.BlockSpec((tr, c), _row)
    out = jax.ShapeDtypeStruct((r, c), F32)
    return pl.pallas_call(
        body, grid=(r // tr,), name=name, out_shape=(out,) * 4,
        in_specs=[pl.BlockSpec((nk, tr, c), lambda i: (0, i, 0)), blk, blk, blk],
        out_specs=(blk,) * 4, compiler_params=_params(1),
    )(parts, w, m, v)


def _reduce_update_small(parts, late, w, m, v):
    nd = parts.shape[0]
    first = parts.shape[1] - late.shape[1]

    def body(p_ref, q_ref, w_ref, m_ref, v_ref, g_ref, d_ref, nm_ref, nv_ref):
        g, t = p_ref[0], q_ref[0]
        for k in range(1, nd):
            g, t = g + p_ref[k], t + q_ref[k]
        g_ref[...] = g
        g_ref[first:, :] = g[first:, :] + t
        d_ref[...], nm_ref[...], nv_ref[...] = _adamw(w_ref[...], g_ref[...], m_ref[...], v_ref[...])

    out = jax.ShapeDtypeStruct(w.shape, F32)
    return pl.pallas_call(body, name="reduce_update_small", out_shape=(out,) * 4,
                          compiler_params=pltpu.CompilerParams(vmem_limit_bytes=VMEM_LIMIT))(parts, late, w, m, v)


MESH = pl.DeviceIdType.MESH


def _copy_through_vmem(src_hbm, dst_hbm, stage, sem):
    load = pltpu.make_async_copy(src_hbm, stage, sem)
    load.start()
    load.wait()
    store = pltpu.make_async_copy(stage, dst_hbm, sem)
    store.start()
    store.wait()


class _GatherPlan:
    def __init__(self, x_ref, out_ref, send_sems, recv_sems):
        x, y, c = lax.axis_index("x"), lax.axis_index("y"), lax.axis_index("c")
        self.me, self.sibling, self.c = (x, y, c), (x, y, 1 - c), c
        self.chips = [(1 - x, y), (x, 1 - y), (1 - x, 1 - y)]
        self.x_ref, self.out_ref, self.send_sems, self.recv_sems = x_ref, out_ref, send_sems, recv_sems

    def slot(self, px, py, pc):
        return self.out_ref.at[4 * px + 2 * py + pc]

    def copy(self, k, block, to, src=None):
        return pltpu.make_async_remote_copy(
            src_ref=self.slot(*block) if src is None else src, dst_ref=self.slot(*block),
            send_sem=self.send_sems.at[k], recv_sem=self.recv_sems.at[k], device_id=to, device_id_type=MESH)

    def first(self):
        return [self.copy(0, self.me, self.sibling, src=self.x_ref)] + [
            self.copy(1 + j, self.me, (*chip, self.c), src=self.x_ref) for j, chip in enumerate(self.chips)]

    def passed(self):
        return [self.copy(4 + j, (*chip, self.c), self.sibling) for j, chip in enumerate(self.chips)]


def _gather_start(x_ref, out_ref, stage, send_sems, recv_sems, local_sem):
    plan = _GatherPlan(x_ref, out_ref, send_sems, recv_sems)
    for cp in plan.first():
        cp.start()
    _copy_through_vmem(x_ref, plan.slot(*plan.me), stage, local_sem)


def _gather_pass_on(out_ref, send_sems, recv_sems):
    plan = _GatherPlan(None, out_ref, send_sems, recv_sems)
    passed = plan.passed()
    for j, chip in enumerate(plan.chips):
        plan.copy(1 + j, (*chip, plan.c), plan.me).wait_recv()
        passed[j].start()


def _gather_finish(x_ref, out_ref, send_sems, recv_sems):
    plan = _GatherPlan(x_ref, out_ref, send_sems, recv_sems)
    plan.copy(0, plan.sibling, plan.me).wait_recv()
    for j, chip in enumerate(plan.chips):
        plan.copy(4 + j, (*chip, 1 - plan.c), plan.me).wait_recv()
    for cp in plan.first() + plan.passed():
        cp.wait_send()


def _gather_w_in(xs):
    r, cdim = xs.shape
    qkv, f_end = 3 * D_ATTN, 3 * D_ATTN + HEADS

    def body(x_ref, wqkv_ref, wf_ref, wu_ref, send_sems, recv_sems, local_sem, blocks, flat):
        plan = _GatherPlan(x_ref, blocks, send_sems, recv_sems)
        for cp in plan.first():
            cp.start()
        own = pltpu.make_async_copy(x_ref, plan.slot(*plan.me), local_sem)
        own.start()
        own.wait()
        _gather_pass_on(blocks, send_sems, recv_sems)
        _gather_finish(x_ref, blocks, send_sems, recv_sems)
        for dev in range(N_DEV):
            flat[dev * SHARD_IN:(dev + 1) * SHARD_IN, :] = blocks[dev, 0:SHARD_IN, :].astype(F32)
        wqkv_ref[...] = flat[0:qkv, :].astype(BF16)
        wf_ref[...] = jnp.concatenate([flat[qkv:f_end, :], jnp.zeros((LANES - HEADS, cdim), F32)], axis=0).astype(BF16)
        wu_ref[...] = flat[f_end:D_IN, :].astype(BF16)

    shape = lambda rows: jax.ShapeDtypeStruct((rows, cdim), xs.dtype)
    dma = pltpu.SemaphoreType.DMA
    return pl.pallas_call(
        body, name="gather_w_in",
        out_shape=(shape(qkv), shape(LANES), shape(D_IN - f_end)),
        in_specs=[ANY], out_specs=(VMEM_WHOLE, VMEM_WHOLE, VMEM_WHOLE),
        scratch_shapes=[dma((7,)), dma((7,)), dma,
                        pltpu.VMEM((N_DEV, r, cdim), xs.dtype), pltpu.VMEM((D_IN, cdim), F32)],
        compiler_params=pltpu.CompilerParams(vmem_limit_bytes=VMEM_LIMIT),
    )(xs)


def _pair_copies(src_refs, dst_refs, send_sems, recv_sems):
    x, y, c = lax.axis_index("x"), lax.axis_index("y"), lax.axis_index("c")
    return [pltpu.make_async_remote_copy(
        src_ref=src.at[2 * k + (1 - c)], dst_ref=dst.at[k], send_sem=send_sems.at[k, p], recv_sem=recv_sems.at[k, p],
        device_id=(x, y, 1 - c), device_id_type=MESH)
        for k in range(N_DEV // 2) for p, (src, dst) in enumerate(zip(src_refs, dst_refs))]


def _rs_pair_sum(core, pieces, offsets, rows, name, landed=()):
    cdim = pieces[0].shape[2]
    nk = N_DEV // 2
    npc = len(pieces)
    nrem = npc - len(landed)
    spans = [(o, t.shape[1]) for t, o in zip(pieces, offsets)]
    ends = [o + n for o, n in spans]
    gaps = [(a, b - a) for a, b in zip(ends, [o for o, _ in spans[1:]] + [rows]) if b > a]

    def body(core_ref, *refs):
        own, src, got, o_ref = refs[:npc], refs[npc:npc + nrem], refs[npc + nrem:2 * npc], refs[2 * npc]
        landing, send_sems, recv_sems = refs[2 * npc + 1:]
        k = pl.program_id(0)
        x, y, c = lax.axis_index("x"), lax.axis_index("y"), lax.axis_index("c")

        def copies(kk):
            return [pltpu.make_async_remote_copy(
                src_ref=src[p].at[2 * kk + (1 - c)], dst_ref=landing.at[kk, pl.ds(o, n)],
                send_sem=send_sems.at[kk, p], recv_sem=recv_sems.at[kk, p], device_id=(x, y, 1 - c),
                device_id_type=MESH) for p, (o, n) in enumerate(spans[:nrem])]

        @pl.when(k == 0)
        def _():
            for kk in range(nk):
                for cp in copies(kk):
                    cp.start()

        for cp, piece, (o, n) in zip(copies(k), own, spans):
            cp.wait_recv()
            o_ref[0, o:o + n, :] = (piece[0].astype(F32) + landing[k, o:o + n, :].astype(F32)).astype(BF16)
        for theirs, piece, (o, n) in zip(got, own[nrem:], spans[nrem:]):
            o_ref[0, o:o + n, :] = (piece[0].astype(F32) + theirs[0].astype(F32)).astype(BF16)
        for o, n in gaps:
            o_ref[0, o:o + n, :] = jnp.zeros((n, cdim), BF16)

        @pl.when(k == nk - 1)
        def _():
            for kk in range(nk):
                for cp in copies(kk):
                    cp.wait_send()

    own_specs = [pl.BlockSpec((1, n, cdim), lambda k, core_ref: (2 * k + core_ref[0], 0, 0)) for _, n in spans]
    got_specs = [pl.BlockSpec((1, n, cdim), lambda k, core_ref: (k, 0, 0)) for _, n in spans[nrem:]]
    land_rows = max(o + n for o, n in spans[:nrem])
    return pl.pallas_call(
        body, name=name, out_shape=jax.ShapeDtypeStruct((nk, rows, cdim), BF16),
        grid_spec=pltpu.PrefetchScalarGridSpec(
            num_scalar_prefetch=1, grid=(nk,),
            in_specs=own_specs + [ANY] * nrem + got_specs,
            out_specs=pl.BlockSpec((1, rows, cdim), lambda k, core_ref: (k, 0, 0)),
            scratch_shapes=[pltpu.VMEM((nk, land_rows, cdim), BF16), pltpu.SemaphoreType.DMA((nk, nrem)),
                            pltpu.SemaphoreType.DMA((nk, nrem))]),
        compiler_params=_params(1),
    )(core, *pieces, *pieces[:nrem], *landed)


def _chips_start(b_ref, out_ref, stage, send_sems, recv_sems, local_sem):
    x, y, c = lax.axis_index("x"), lax.axis_index("y"), lax.axis_index("c")
    mychip = 2 * x + y
    for j, (px, py) in enumerate([(1 - x, y), (x, 1 - y), (1 - x, 1 - y)]):
        pltpu.make_async_remote_copy(
            src_ref=b_ref.at[2 * px + py], dst_ref=out_ref.at[mychip],
            send_sem=send_sems.at[j], recv_sem=recv_sems.at[j], device_id=(px, py, c), device_id_type=MESH).start()
    _copy_through_vmem(b_ref.at[mychip], out_ref.at[mychip], stage, local_sem)


def _chips_finish(b_ref, out_ref, send_sems, recv_sems):
    x, y, c = lax.axis_index("x"), lax.axis_index("y"), lax.axis_index("c")
    for j, (px, py) in enumerate([(1 - x, y), (x, 1 - y), (1 - x, 1 - y)]):
        pltpu.make_async_remote_copy(
            src_ref=b_ref.at[2 * px + py], dst_ref=out_ref.at[2 * px + py],
            send_sem=send_sems.at[j], recv_sem=recv_sems.at[j], device_id=(px, py, c), device_id_type=MESH).wait()


def _pad_rows(a, rows):
    return jnp.pad(a, ((0, rows - a.shape[0]), (0, 0)))


def _pack_in(w_in):
    return _pad_rows(w_in[0].T, ROWS_IN)


def _unpack_in(r):
    return r[0:SHARD_IN].T[None]


def _pack_rest(w_out, w_gate, w_up, w_down, w_ple, w_pg):
    head = _pad_rows(jnp.concatenate([w_out[0], w_pg[0], w_ple[0].T.reshape(ROWS_PLE, D_MODEL)], axis=0), OFF_GATE)
    return jnp.concatenate([head, w_gate[0].T, w_up[0].T, w_down[0]], axis=0)


def _unpack_rest(r):
    return (r[0:OFF_PG][None], r[OFF_GATE:OFF_UP].T[None], r[OFF_UP:OFF_DOWN].T[None], r[OFF_DOWN:ROWS_REST][None],
            r[OFF_PLE:OFF_PLE + ROWS_PLE].reshape(SHARD_SQ, D_PLE).T[None], r[OFF_PG:OFF_PLE][None])


def _pack_small(w_pool, g_mix_pre, g_mix_post, g_ffn_pre, g_ffn_post, g_ple, g_attn, g_pool, pool_scale, b_forget,
                loss=None):
    row = lambda vrow: vrow.reshape(1, -1)
    misc = [row(pool_scale), row(b_forget), row(loss) if loss is not None else jnp.zeros((1, 1), F32),
            jnp.zeros((1, D_MODEL - COL_LOSS - 1), F32)]
    rows = [w_pool.reshape(64, D_MODEL), row(g_mix_pre), row(g_mix_post), row(g_ffn_pre), row(g_ffn_post), row(g_ple),
            jnp.concatenate([row(g_attn), row(g_pool)], axis=1), jnp.concatenate(misc, axis=1),
            jnp.zeros((SMALL_ROWS - ROW_MISC - 1, D_MODEL), F32)]
    return jnp.concatenate(rows, axis=0)


def _pack_small_late(g_mix_pre, b_forget):
    misc = [jnp.zeros((1, COL_B_FORGET), F32), b_forget.reshape(1, -1), jnp.zeros((1, D_MODEL - COL_LOSS), F32)]
    return jnp.concatenate([g_mix_pre.reshape(1, -1), jnp.zeros((ROW_MISC - ROW_G_MIX_PRE - 1, D_MODEL), F32),
                            jnp.concatenate(misc, axis=1), jnp.zeros((SMALL_ROWS - ROW_MISC - 1, D_MODEL), F32)], axis=0)


def _unpack_small(r):
    gains, misc = r[ROW_GROUP_GAINS:ROW_GROUP_GAINS + 1], r[ROW_MISC:ROW_MISC + 1]
    return dict(
        w_pool=r[0:64].reshape(1, 4, POOL_CH, POOL_CH), g_mix_pre=r[ROW_G_MIX_PRE:ROW_G_MIX_PRE + 1],
        g_mix_post=r[ROW_G_MIX_POST:ROW_G_MIX_POST + 1], g_ffn_pre=r[ROW_G_FFN_PRE:ROW_G_FFN_PRE + 1],
        g_ffn_post=r[ROW_G_FFN_POST:ROW_G_FFN_POST + 1], g_ple=r[ROW_G_PLE:ROW_G_PLE + 1],
        g_attn_grp=gains[:, 0:D_ATTN], g_pool_grp=gains[:, D_ATTN:D_ATTN + D_POOL],
        pool_scale=misc[:, 0:D_POOL], b_forget=misc[:, COL_B_FORGET:COL_B_FORGET + HEADS])


def _step(x, p, tgt, small, in_w, in_m, in_v, rest_w, rest_m, rest_v):
    core = lax.axis_index("c").astype(jnp.int32).reshape(1)
    wqkv, wf, wu = _gather_w_in(in_w.astype(BF16))
    wpool = small["w_pool"].astype(BF16)
    bpad = jnp.pad(small["b_forget"], ((0, 0), (0, LANES - HEADS)))

    lay = _attn_layout_constants()
    rest_b = rest_w.astype(BF16)
    hn, qt3, ka, v, qat3, vt3, kt3, fl, y, mpre, gh = _pre_attn_fwd(x, small["g_mix_pre"], wqkv, wf, wu, bpad, wpool, lay,
                                                                 rest_b[0:OFF_GATE])
    a, lset3, gf = _attn_fwd(ka, qat3, vt3, rest_b[OFF_GATE:])
    wple_t = gh[:, OFF_PLE:OFF_PLE + ROWS_PLE].reshape(D_MODEL, D_PLE)
    mix, o, h1, hn2 = _post_attn_fwd(a, mpre, x, small["g_attn_grp"], small["g_pool_grp"], small["pool_scale"], gh,
                                     small["g_mix_post"], small["g_ffn_pre"])
    gate, up, act, ff, h2 = _ffn_fwd(hn2, gf, gf, gf, h1, small["g_ffn_post"])
    dh2, dff, dgl, dpp, h2b, pb, loss8, dg_ple, dg_ffn_post = _tail_fwd_bwd(
        h2, p, tgt, ff, wple_t, gh, small["g_ple"], small["g_ffn_post"])
    dgate, dup, dh1, dg_ffn_pre = _ffn_bwd(dff, gate, up, gf, gf, gf, h1, dh2, small["g_ffn_pre"])
    nd = N_DEV
    send_rest = [
        _wgrad(h2b, dgl, BF16, "wgrad_ple_gate").reshape(nd, SHARD_SQ, D_MODEL),
        _wgrad(dpp, pb, BF16, "wgrad_ple").reshape(nd, ROWS_PLE, D_MODEL),
        _wgrad(dgate, hn2, BF16, "wgrad_gate").reshape(nd, SHARD_FF, D_MODEL),
        _wgrad(dup, hn2, BF16, "wgrad_up").reshape(nd, SHARD_FF, D_MODEL),
        _wgrad(act, dff, BF16, "wgrad_down").reshape(nd, SHARD_FF, D_MODEL)]
    (dob, dat3, dlt3, dmpb, dy, dg_mix_post, dg_attn, dg_pool, dps), landed = _post_attn_bwd(
        dh1, o, a, mpre, gh, wpool, small["g_mix_post"], small["g_attn_grp"], small["g_pool_grp"], small["pool_scale"],
        send_rest)
    send_rest = [_wgrad(mix, dob, BF16, "wgrad_out").reshape(nd, SHARD_SQ, D_MODEL)] + send_rest
    pair_rest = _rs_pair_sum(core, send_rest, [0, OFF_PG, OFF_PLE, OFF_GATE, OFF_UP, OFF_DOWN], ROWS_REST,
                             "rs_pair_sum_rest", landed)

    dwp = _wgrad(y, dmpb, F32, "wgrad_pool")
    dw_pool = jnp.stack([dwp[g * POOL_CH:(g + 1) * POOL_CH, g * POOL_CH:(g + 1) * POOL_CH] for g in range(4)])
    small_part = _pack_small(dw_pool, jnp.zeros((1, D_MODEL), F32), dg_mix_post, dg_ffn_pre, dg_ffn_post, dg_ple,
                             dg_attn, dg_pool, dps, jnp.zeros((1, HEADS), F32), loss8[0:1, 0:1])
    dqt3, dkt3, dvt3, chips_rest, small_all = _attn_bwd(ka, v, kt3, qat3, qt3, dat3, lset3, dlt3, pair_rest, small_part)

    gx, dz, dg_mix_pre, db = _pre_attn_bwd(dqt3, dkt3, dvt3, fl, dy, x, dh1, small["g_mix_pre"], wqkv, wf, wu)

    pair_in = _rs_pair_sum(core, [_wgrad_in(dz, hn)], [0], ROWS_IN, "rs_pair_sum_in")

    small_late = _pack_small_late(dg_mix_pre, db[:, 0:HEADS])
    *upd_rest, chips_in, late_all = _reduce_update_rest(chips_rest, rest_w, rest_m, rest_v, pair_in, small_late)
    upd_in = _reduce_update_big(chips_in, in_w, in_m, in_v, ROWS_IN, "reduce_update_in")
    return gx, (small_all, late_all), upd_in, upd_rest


def kernel(x, p, g_mix_pre, w_in, b_forget, g_attn_grp, g_pool_grp, w_pool, pool_scale, w_out, g_mix_post, g_ffn_pre, w_ffn_gate, w_ffn_up, w_ffn_down, g_ffn_post, w_ple_proj, g_ple, w_ple_gate, loss_target, m_g_mix_pre, m_w_in, m_b_forget, m_g_attn_grp, m_g_pool_grp, m_w_pool, m_pool_scale, m_w_out, m_g_mix_post, m_g_ffn_pre, m_w_ffn_gate, m_w_ffn_up, m_w_ffn_down, m_g_ffn_post, m_w_ple_proj, m_g_ple, m_w_ple_gate, v_g_mix_pre, v_w_in, v_b_forget, v_g_attn_grp, v_g_pool_grp, v_w_pool, v_pool_scale, v_w_out, v_g_mix_post, v_g_ffn_pre, v_w_ffn_gate, v_w_ffn_up, v_w_ffn_down, v_g_ffn_post, v_w_ple_proj, v_g_ple, v_w_ple_gate):
    small = dict(w_pool=w_pool[0], g_mix_pre=g_mix_pre, g_mix_post=g_mix_post, g_ffn_pre=g_ffn_pre,
                 g_ffn_post=g_ffn_post, g_ple=g_ple, g_attn_grp=g_attn_grp, g_pool_grp=g_pool_grp,
                 pool_scale=pool_scale, b_forget=b_forget)
    gx, small_all, upd_in, upd_rest = _step(
        x[0], p[0, 0], loss_target[0], small, _pack_in(w_in), _pack_in(m_w_in), _pack_in(v_w_in),
        _pack_rest(w_out, w_ffn_gate, w_ffn_up, w_ffn_down, w_ple_proj, w_ple_gate),
        _pack_rest(m_w_out, m_w_ffn_gate, m_w_ffn_up, m_w_ffn_down, m_w_ple_proj, m_w_ple_gate),
        _pack_rest(v_w_out, v_w_ffn_gate, v_w_ffn_up, v_w_ffn_down, v_w_ple_proj, v_w_ple_gate))

    sm_w = _pack_small(w_pool, g_mix_pre, g_mix_post, g_ffn_pre, g_ffn_post, g_ple, g_attn_grp, g_pool_grp, pool_scale, b_forget)
    sm_m = _pack_small(m_w_pool, m_g_mix_pre, m_g_mix_post, m_g_ffn_pre, m_g_ffn_post, m_g_ple, m_g_attn_grp, m_g_pool_grp, m_pool_scale, m_b_forget)
    sm_v = _pack_small(v_w_pool, v_g_mix_pre, v_g_mix_post, v_g_ffn_pre, v_g_ffn_post, v_g_ple, v_g_attn_grp, v_g_pool_grp, v_pool_scale, v_b_forget)
    upd_small = _reduce_update_small(*small_all, sm_w, sm_m, sm_v)
    loss = upd_small[0][ROW_MISC, COL_LOSS]

    def leaves(k):
        b_out, b_gate, b_up, b_down, b_ple, b_pg = _unpack_rest(upd_rest[k])
        s = _unpack_small(upd_small[k])
        return (s["g_mix_pre"], _unpack_in(upd_in[k]), s["b_forget"], s["g_attn_grp"], s["g_pool_grp"], s["w_pool"],
                s["pool_scale"], b_out, s["g_mix_post"], s["g_ffn_pre"], b_gate, b_up, b_down, s["g_ffn_post"], b_ple,
                s["g_ple"], b_pg)

    return (loss, gx[None], *leaves(0), *leaves(1), *leaves(2), *leaves(3))
```
